```python
import math
import jax, jax.numpy as jnp
from jax import lax
import numpy as np

D_MODEL = 2048
BATCH = 8
SEQ = 2048
DEPTH = 1

POOL_WIDTH = D_MODEL // 2
POOL_WINDOWS = (2, 4, 8, 16)
POOL_GROUPS = len(POOL_WINDOWS)
POOL_GROUP_DIM = POOL_WIDTH // POOL_GROUPS
SB_HEAD_DIM = 128
SB_HEADS = (D_MODEL // 2) // SB_HEAD_DIM
SB_WIDTH = SB_HEADS * SB_HEAD_DIM
N_BRANCHES = 2
IN_WIDTH = POOL_WIDTH + 3 * SB_WIDTH + N_BRANCHES * D_MODEL
D_FF = 4 * D_MODEL
Q_BLOCK = 128
N_MOD = 6
EPS = 1e-6

kernel_name = "hybrid_pool_stickbreak_gated_block"


def rms_norm(x, w):
    xf = x.astype(jnp.float32)
    y = xf * lax.rsqrt(jnp.mean(jnp.square(xf), axis=-1, keepdims=True) + EPS)
    return (y * w.astype(jnp.float32)).astype(x.dtype)


def multiscale_pool(u, w_pool, pool_scale):
    B, S, _ = u.shape
    uf = u.astype(jnp.float32).reshape(B, S, POOL_GROUPS, POOL_GROUP_DIM)
    cs = jnp.cumsum(uf, axis=1)
    pos = jnp.arange(S, dtype=jnp.int32)
    outs = []
    for g, w in enumerate(POOL_WINDOWS):
        csg = cs[:, :, g]
        lag = jnp.pad(csg, ((0, 0), (w, 0), (0, 0)))[:, :S]
        count = jnp.minimum(pos + 1, w).astype(jnp.float32)[None, :, None]
        outs.append((csg - lag) / count - uf[:, :, g])
    pooled = jnp.stack(outs, axis=2)
    mixed = jnp.einsum('bsgc,gce->bsge', pooled, w_pool.astype(jnp.float32))
    y = mixed.reshape(B, S, POOL_WIDTH) * pool_scale.astype(jnp.float32)
    return y.astype(u.dtype)


def stick_breaking_attention(q, k, v):
    B, S, H, Dh = q.shape
    nb = S // Q_BLOCK
    scale = 1.0 / math.sqrt(Dh)
    kh = k.transpose(0, 2, 1, 3)
    vh = v.transpose(0, 2, 1, 3)
    qb = q.transpose(0, 2, 1, 3).reshape(B, H, nb, Q_BLOCK, Dh).transpose(2, 0, 1, 3, 4)
    starts = jnp.arange(nb, dtype=jnp.int32) * Q_BLOCK
    key_pos = jnp.arange(S, dtype=jnp.int32)

    def block(args):
        q_blk, t0 = args
        z = jnp.einsum('bhqd,bhkd->bhqk', q_blk, kh).astype(jnp.float32) * scale
        q_pos = t0 + jnp.arange(Q_BLOCK, dtype=jnp.int32)
        mask = key_pos[None, :] < q_pos[:, None]
        log_beta = jax.nn.log_sigmoid(z)
        log_1m_beta = log_beta - z
        l = jnp.where(mask, log_1m_beta, 0.0)
        suffix = lax.cumsum(l, axis=3, reverse=True) - l
        a = jnp.where(mask, jnp.exp(log_beta + suffix), 0.0)
        return jnp.einsum('bhqk,bhkd->bhqd', a.astype(vh.dtype), vh)

    out = lax.map(block, (qb, starts))
    return out.transpose(1, 0, 3, 2, 4).reshape(B, S, H * Dh)


def _fwd_setup_inputs(seed: int = 0) -> dict:
    key = jax.random.key(seed)
    ks = jax.random.split(key, 20)
    f32 = jnp.float32
    L = DEPTH

    def nrm(k, shape, fan_in, gain=1.0):
        return jax.random.normal(k, shape, f32) * (gain * fan_in ** -0.5)

    return {
        "x": jax.random.normal(ks[0], (BATCH, SEQ, D_MODEL), f32),
        "c": jax.random.normal(ks[1], (BATCH, D_MODEL), f32),
        "w_ada": nrm(ks[2], (L, D_MODEL, N_MOD * D_MODEL), D_MODEL, 0.5),
        "b_ada": 0.02 * jax.random.normal(ks[3], (L, N_MOD * D_MODEL), f32),
        "norm1_w": 1.0 + 0.05 * jax.random.normal(ks[4], (L, D_MODEL), f32),
        "w_in": nrm(ks[5], (L, D_MODEL, IN_WIDTH), D_MODEL),
        "q_norm_w": 1.0 + 0.05 * jax.random.normal(ks[6], (L, SB_HEAD_DIM), f32),
        "k_norm_w": 1.0 + 0.05 * jax.random.normal(ks[7], (L, SB_HEAD_DIM), f32),
        "w_pool": nrm(ks[8], (L, POOL_GROUPS, POOL_GROUP_DIM, POOL_GROUP_DIM), POOL_GROUP_DIM),
        "pool_scale": 1.0 + 0.1 * jax.random.normal(ks[9], (L, POOL_WIDTH), f32),
        "w_a_up": nrm(ks[10], (L, POOL_WIDTH, D_MODEL), POOL_WIDTH),
        "w_b_up": nrm(ks[11], (L, SB_WIDTH, D_MODEL), SB_WIDTH),
        "w_o": nrm(ks[12], (L, D_MODEL, D_MODEL), D_MODEL),
        "norm2_w": 1.0 + 0.05 * jax.random.normal(ks[13], (L, D_MODEL), f32),
        "w_ff1": nrm(ks[14], (L, D_MODEL, D_FF), D_MODEL),
        "w_ff2": nrm(ks[15], (L, D_FF, D_MODEL), D_FF),
    }


def _fwd_reference(x, c, w_ada, b_ada, norm1_w, w_in, q_norm_w, k_norm_w, w_pool, pool_scale,
              w_a_up, w_b_up, w_o, norm2_w, w_ff1, w_ff2):
    B, S, D = x.shape
    split_at = [POOL_WIDTH, POOL_WIDTH + SB_WIDTH, POOL_WIDTH + 2 * SB_WIDTH,
                POOL_WIDTH + 3 * SB_WIDTH, POOL_WIDTH + 3 * SB_WIDTH + D_MODEL]
    for l in range(DEPTH):
        mod = jax.nn.silu(c) @ w_ada[l] + b_ada[l]
        shift1, scale1, gate1, shift2, scale2, gate2 = jnp.split(mod, N_MOD, axis=-1)

        h = rms_norm(x, norm1_w[l]) * (1.0 + scale1[:, None]) + shift1[:, None]
        proj = h @ w_in[l]
        u_pool, q, k, v, g_a, g_b = jnp.split(proj, split_at, axis=-1)

        y_a = multiscale_pool(u_pool, w_pool[l], pool_scale[l]) @ w_a_up[l]

        q = rms_norm(q.reshape(B, S, SB_HEADS, SB_HEAD_DIM), q_norm_w[l])
        k = rms_norm(k.reshape(B, S, SB_HEADS, SB_HEAD_DIM), k_norm_w[l])
        v = v.reshape(B, S, SB_HEADS, SB_HEAD_DIM)
        y_b = stick_breaking_attention(q, k, v) @ w_b_up[l]

        merged = jax.nn.sigmoid(g_a) * y_a + jax.nn.sigmoid(g_b) * y_b
        x = x + gate1[:, None] * (merged @ w_o[l])

        h2 = rms_norm(x, norm2_w[l]) * (1.0 + scale2[:, None]) + shift2[:, None]
        f = jnp.square(jax.nn.relu(h2 @ w_ff1[l])) @ w_ff2[l]
        x = x + gate2[:, None] * f
    return x


import jax as _jax
import jax.numpy as _jnp

TWIN_FORMAT = 'train_step'
FWD_PARAMS = ['x', 'c', 'w_ada', 'b_ada', 'norm1_w', 'w_in', 'q_norm_w', 'k_norm_w', 'w_pool', 'pool_scale', 'w_a_up', 'w_b_up', 'w_o', 'norm2_w', 'w_ff1', 'w_ff2']
TWIN_WEIGHTS = ['w_ada', 'b_ada', 'norm1_w', 'w_in', 'q_norm_w', 'k_norm_w', 'w_pool', 'pool_scale', 'w_a_up', 'w_b_up', 'w_o', 'norm2_w', 'w_ff1', 'w_ff2']
TWIN_DIFF_INPUT = 'x'
TWIN_INPUTS = ['x', 'c', 'w_ada', 'b_ada', 'norm1_w', 'w_in', 'q_norm_w', 'k_norm_w', 'w_pool', 'pool_scale', 'w_a_up', 'w_b_up', 'w_o', 'norm2_w', 'w_ff1', 'w_ff2', 'loss_target', 'm_w_ada', 'm_b_ada', 'm_norm1_w', 'm_w_in', 'm_q_norm_w', 'm_k_norm_w', 'm_w_pool', 'm_pool_scale', 'm_w_a_up', 'm_w_b_up', 'm_w_o', 'm_norm2_w', 'm_w_ff1', 'm_w_ff2', 'v_w_ada', 'v_b_ada', 'v_norm1_w', 'v_w_in', 'v_q_norm_w', 'v_k_norm_w', 'v_w_pool', 'v_pool_scale', 'v_w_a_up', 'v_w_b_up', 'v_w_o', 'v_norm2_w', 'v_w_ff1', 'v_w_ff2']
TWIN_OUTPUTS = ['loss', 'grad_x', 'grad_w_ada', 'grad_b_ada', 'grad_norm1_w', 'grad_w_in', 'grad_q_norm_w', 'grad_k_norm_w', 'grad_w_pool', 'grad_pool_scale', 'grad_w_a_up', 'grad_w_b_up', 'grad_w_o', 'grad_norm2_w', 'grad_w_ff1', 'grad_w_ff2', 'delta_w_ada', 'delta_b_ada', 'delta_norm1_w', 'delta_w_in', 'delta_q_norm_w', 'delta_k_norm_w', 'delta_w_pool', 'delta_pool_scale', 'delta_w_a_up', 'delta_w_b_up', 'delta_w_o', 'delta_norm2_w', 'delta_w_ff1', 'delta_w_ff2', 'new_m_w_ada', 'new_m_b_ada', 'new_m_norm1_w', 'new_m_w_in', 'new_m_q_norm_w', 'new_m_k_norm_w', 'new_m_w_pool', 'new_m_pool_scale', 'new_m_w_a_up', 'new_m_w_b_up', 'new_m_w_o', 'new_m_norm2_w', 'new_m_w_ff1', 'new_m_w_ff2', 'new_v_w_ada', 'new_v_b_ada', 'new_v_norm1_w', 'new_v_w_in', 'new_v_q_norm_w', 'new_v_k_norm_w', 'new_v_w_pool', 'new_v_pool_scale', 'new_v_w_a_up', 'new_v_w_b_up', 'new_v_w_o', 'new_v_norm2_w', 'new_v_w_ff1', 'new_v_w_ff2']
TWIN_LEAF_KINDS = {'loss': 'loss', 'grad_x': 'grad_x', 'grad_w_ada': 'grad_w', 'grad_b_ada': 'grad_w', 'grad_norm1_w': 'grad_w', 'grad_w_in': 'grad_w', 'grad_q_norm_w': 'grad_w', 'grad_k_norm_w': 'grad_w', 'grad_w_pool': 'grad_w', 'grad_pool_scale': 'grad_w', 'grad_w_a_up': 'grad_w', 'grad_w_b_up': 'grad_w', 'grad_w_o': 'grad_w', 'grad_norm2_w': 'grad_w', 'grad_w_ff1': 'grad_w', 'grad_w_ff2': 'grad_w', 'delta_w_ada': 'delta_w', 'delta_b_ada': 'delta_w', 'delta_norm1_w': 'delta_w', 'delta_w_in': 'delta_w', 'delta_q_norm_w': 'delta_w', 'delta_k_norm_w': 'delta_w', 'delta_w_pool': 'delta_w', 'delta_pool_scale': 'delta_w', 'delta_w_a_up': 'delta_w', 'delta_w_b_up': 'delta_w', 'delta_w_o': 'delta_w', 'delta_norm2_w': 'delta_w', 'delta_w_ff1': 'delta_w', 'delta_w_ff2': 'delta_w', 'new_m_w_ada': 'new_m', 'new_m_b_ada': 'new_m', 'new_m_norm1_w': 'new_m', 'new_m_w_in': 'new_m', 'new_m_q_norm_w': 'new_m', 'new_m_k_norm_w': 'new_m', 'new_m_w_pool': 'new_m', 'new_m_pool_scale': 'new_m', 'new_m_w_a_up': 'new_m', 'new_m_w_b_up': 'new_m', 'new_m_w_o': 'new_m', 'new_m_norm2_w': 'new_m', 'new_m_w_ff1': 'new_m', 'new_m_w_ff2': 'new_m', 'new_v_w_ada': 'new_v', 'new_v_b_ada': 'new_v', 'new_v_norm1_w': 'new_v', 'new_v_w_in': 'new_v', 'new_v_q_norm_w': 'new_v', 'new_v_k_norm_w': 'new_v', 'new_v_w_pool': 'new_v', 'new_v_pool_scale': 'new_v', 'new_v_w_a_up': 'new_v', 'new_v_w_b_up': 'new_v', 'new_v_w_o': 'new_v', 'new_v_norm2_w': 'new_v', 'new_v_w_ff1': 'new_v', 'new_v_w_ff2': 'new_v'}


def _forward(args):
    return _fwd_reference(*[args[k] for k in FWD_PARAMS])


def _output_shape():
    out = _jax.eval_shape(lambda: _forward(_fwd_setup_inputs(0)))
    return out.shape, out.dtype

N_MICROBATCH = 1
ADAM_LR = 0.001
ADAM_B1 = 0.9
ADAM_B2 = 0.999
ADAM_EPS = 1e-08
ADAM_WD = 0.01
ADAM_STEP = 10
PER_EXAMPLE_BATCH_AXIS = {'x': 0, 'c': 0, 'loss_target': 0}
SHARED_INPUTS = []
_WEIGHT_DTYPES = {'w_ada': _jnp.float32, 'b_ada': _jnp.float32, 'norm1_w': _jnp.float32, 'w_in': _jnp.float32, 'q_norm_w': _jnp.float32, 'k_norm_w': _jnp.float32, 'w_pool': _jnp.float32, 'pool_scale': _jnp.float32, 'w_a_up': _jnp.float32, 'w_b_up': _jnp.float32, 'w_o': _jnp.float32, 'norm2_w': _jnp.float32, 'w_ff1': _jnp.float32, 'w_ff2': _jnp.float32}
MOMENT_SCALE = {'w_ada': 8.190386e-01, 'b_ada': 1.766148e+00, 'norm1_w': 3.273810e-01, 'w_in': 2.799125e-02, 'q_norm_w': 2.299808e-01, 'k_norm_w': 2.330990e-01, 'w_pool': 3.498743e-02, 'pool_scale': 3.759913e-01, 'w_a_up': 2.159024e-02, 'w_b_up': 5.046390e-02, 'w_o': 5.021914e-02, 'norm2_w': 3.052709e+00, 'w_ff1': 9.278782e-02, 'w_ff2': 3.585806e-01}


def _to_microbatches(a, axis):
    t = _jnp.moveaxis(a, axis, 0)
    t = t.reshape((N_MICROBATCH, t.shape[0] // N_MICROBATCH) + t.shape[1:])
    return _jnp.moveaxis(t, 1, axis + 1)


def setup_inputs(seed: int = 0) -> dict:
    inp = _fwd_setup_inputs(seed)
    key = _jax.random.fold_in(_jax.random.key(seed), 7919)
    shape, _ = _output_shape()
    out = dict(inp)
    out["loss_target"] = _jax.random.normal(_jax.random.fold_in(key, 0), shape, _jnp.float32)
    for i, name in enumerate(TWIN_WEIGHTS):
        w = inp[name].astype(_jnp.float32)
        if MOMENT_SCALE is None:
            s = _jnp.sqrt(_jnp.mean(_jnp.square(w)) + 1e-30)
        else:
            s = MOMENT_SCALE[name]
        km, kv = _jax.random.split(_jax.random.fold_in(key, i + 1))
        out[name] = w
        out["m_" + name] = s * _jax.random.normal(km, w.shape, _jnp.float32)
        out["v_" + name] = (s * s) * _jax.random.uniform(kv, w.shape, _jnp.float32, 0.5, 1.5)
    if N_MICROBATCH > 1:
        for name, axis in PER_EXAMPLE_BATCH_AXIS.items():
            out[name] = _to_microbatches(out[name], axis)
    return {'x': out['x'], 'c': out['c'], 'w_ada': out['w_ada'], 'b_ada': out['b_ada'], 'norm1_w': out['norm1_w'], 'w_in': out['w_in'], 'q_norm_w': out['q_norm_w'], 'k_norm_w': out['k_norm_w'], 'w_pool': out['w_pool'], 'pool_scale': out['pool_scale'], 'w_a_up': out['w_a_up'], 'w_b_up': out['w_b_up'], 'w_o': out['w_o'], 'norm2_w': out['norm2_w'], 'w_ff1': out['w_ff1'], 'w_ff2': out['w_ff2'], 'loss_target': out['loss_target'], 'm_w_ada': out['m_w_ada'], 'm_b_ada': out['m_b_ada'], 'm_norm1_w': out['m_norm1_w'], 'm_w_in': out['m_w_in'], 'm_q_norm_w': out['m_q_norm_w'], 'm_k_norm_w': out['m_k_norm_w'], 'm_w_pool': out['m_w_pool'], 'm_pool_scale': out['m_pool_scale'], 'm_w_a_up': out['m_w_a_up'], 'm_w_b_up': out['m_w_b_up'], 'm_w_o': out['m_w_o'], 'm_norm2_w': out['m_norm2_w'], 'm_w_ff1': out['m_w_ff1'], 'm_w_ff2': out['m_w_ff2'], 'v_w_ada': out['v_w_ada'], 'v_b_ada': out['v_b_ada'], 'v_norm1_w': out['v_norm1_w'], 'v_w_in': out['v_w_in'], 'v_q_norm_w': out['v_q_norm_w'], 'v_k_norm_w': out['v_k_norm_w'], 'v_w_pool': out['v_w_pool'], 'v_pool_scale': out['v_pool_scale'], 'v_w_a_up': out['v_w_a_up'], 'v_w_b_up': out['v_w_b_up'], 'v_w_o': out['v_w_o'], 'v_norm2_w': out['v_norm2_w'], 'v_w_ff1': out['v_w_ff1'], 'v_w_ff2': out['v_w_ff2']}


def _loss(weights, diff, rest, loss_target):
    with _jax.named_scope("forward"):
        args = {**rest, TWIN_DIFF_INPUT: diff, **{k: w.astype(_WEIGHT_DTYPES[k]) for k, w in weights.items()}}
        y = _forward(args)
    with _jax.named_scope("loss_head"):
        err = _jnp.square(y.astype(_jnp.float32) - loss_target)
        return 0.5 * _jnp.sum(_jnp.mean(err, axis=-1)) if err.ndim else 0.5 * err


def _adamw(w, g, m, v):
    m = ADAM_B1 * m + (1.0 - ADAM_B1) * g
    v = ADAM_B2 * v + (1.0 - ADAM_B2) * _jnp.square(g)
    m_hat = m / (1.0 - ADAM_B1 ** ADAM_STEP)
    v_hat = v / (1.0 - ADAM_B2 ** ADAM_STEP)
    delta = -ADAM_LR * (m_hat / (_jnp.sqrt(v_hat) + ADAM_EPS) + ADAM_WD * w)
    return delta, m, v


def reference(x, c, w_ada, b_ada, norm1_w, w_in, q_norm_w, k_norm_w, w_pool, pool_scale, w_a_up, w_b_up, w_o, norm2_w, w_ff1, w_ff2, loss_target, m_w_ada, m_b_ada, m_norm1_w, m_w_in, m_q_norm_w, m_k_norm_w, m_w_pool, m_pool_scale, m_w_a_up, m_w_b_up, m_w_o, m_norm2_w, m_w_ff1, m_w_ff2, v_w_ada, v_b_ada, v_norm1_w, v_w_in, v_q_norm_w, v_k_norm_w, v_w_pool, v_pool_scale, v_w_a_up, v_w_b_up, v_w_o, v_norm2_w, v_w_ff1, v_w_ff2):
    given = dict(x=x, c=c, w_ada=w_ada, b_ada=b_ada, norm1_w=norm1_w, w_in=w_in, q_norm_w=q_norm_w, k_norm_w=k_norm_w, w_pool=w_pool, pool_scale=pool_scale, w_a_up=w_a_up, w_b_up=w_b_up, w_o=w_o, norm2_w=norm2_w, w_ff1=w_ff1, w_ff2=w_ff2, loss_target=loss_target, m_w_ada=m_w_ada, m_b_ada=m_b_ada, m_norm1_w=m_norm1_w, m_w_in=m_w_in, m_q_norm_w=m_q_norm_w, m_k_norm_w=m_k_norm_w, m_w_pool=m_w_pool, m_pool_scale=m_pool_scale, m_w_a_up=m_w_a_up, m_w_b_up=m_w_b_up, m_w_o=m_w_o, m_norm2_w=m_norm2_w, m_w_ff1=m_w_ff1, m_w_ff2=m_w_ff2, v_w_ada=v_w_ada, v_b_ada=v_b_ada, v_norm1_w=v_norm1_w, v_w_in=v_w_in, v_q_norm_w=v_q_norm_w, v_k_norm_w=v_k_norm_w, v_w_pool=v_w_pool, v_pool_scale=v_pool_scale, v_w_a_up=v_w_a_up, v_w_b_up=v_w_b_up, v_w_o=v_w_o, v_norm2_w=v_norm2_w, v_w_ff1=v_w_ff1, v_w_ff2=v_w_ff2)
    weights = {n: given[n] for n in TWIN_WEIGHTS}
    shared = {n: given[n] for n in SHARED_INPUTS}
    per_example = {n: given[n] for n in ['x', 'c']}
    grad_fn = _jax.value_and_grad(_loss, argnums=(0, 1))

    def one_microbatch(ex, loss_target):
        ex = dict(ex)
        diff = ex.pop(TWIN_DIFF_INPUT)
        return grad_fn(weights, diff, {**shared, **ex}, loss_target)

    if N_MICROBATCH == 1:
        loss, (grad_w, grad_x) = one_microbatch(per_example, given["loss_target"])
    else:
        def body(carry, xs):
            loss_sum, grad_sum = carry
            l_k, (gw_k, gx_k) = one_microbatch(xs[0], xs[1])
            with _jax.named_scope("update"):
                return (loss_sum + l_k, _jax.tree.map(_jnp.add, grad_sum, gw_k)), gx_k

        init = (_jnp.zeros((), _jnp.float32), _jax.tree.map(_jnp.zeros_like, weights))
        (loss, grad_w), grad_x = _jax.lax.scan(body, init, (per_example, given["loss_target"]))
    with _jax.named_scope("update"):
        delta_w, new_m, new_v = {}, {}, {}
        for n in TWIN_WEIGHTS:
            delta_w[n], new_m[n], new_v[n] = _adamw(weights[n], grad_w[n], given["m_" + n], given["v_" + n])
    return (loss, grad_x, *[grad_w[n] for n in TWIN_WEIGHTS], *[delta_w[n] for n in TWIN_WEIGHTS],
            *[new_m[n] for n in TWIN_WEIGHTS], *[new_v[n] for n in TWIN_WEIGHTS])
```

```python
import functools
import math

import jax
import jax.numpy as jnp
from jax import lax
from jax.experimental import pallas as pl
from jax.experimental.pallas import tpu as pltpu

F32 = jnp.float32
BF16 = jnp.bfloat16
MESH = pl.DeviceIdType.MESH
ANY = pl.BlockSpec(memory_space=pl.ANY)

EPS = 1e-6
HEAD_DIM = 128
POOL_WINDOWS = (2, 4, 8, 16)
N_GROUPS = len(POOL_WINDOWS)
N_CHIPS = 4
N_DEV = 8
ADAM_LR, ADAM_B1, ADAM_B2, ADAM_EPS, ADAM_WD, ADAM_STEP = 0.001, 0.9, 0.999, 1e-08, 0.01, 10
VMEM_LIMIT_V7X = 56 * 1024 * 1024
ATT_TQ = 256
ATT_TK = 128
POOL_T = 256


def _pcall(body, **kw):
    return pl.pallas_call(body, **kw)


def _params(sem=None):
    return pltpu.CompilerParams(dimension_semantics=sem, vmem_limit_bytes=VMEM_LIMIT_V7X)


def _tile(n, pref):
    if n <= pref:
        return n
    t = pref
    while n % t:
        t //= 2
    return t


def _mm(name, pairs, *, M, N, K, ta=False, tb=False, tm=512, tn=1024, tk=1024,
        a_pro=None, b_pro=None, extras=(), outs, epi):
    tm, tn, tk = _tile(M, tm), _tile(N, tn), _tile(K, tk)
    n_i, n_j, n_k = M // tm, N // tn, K // tk
    n_p, n_e = len(pairs), len(extras)
    arrays, in_specs = [], []
    for a, _ in pairs:
        arrays.append(a)
        in_specs.append(pl.BlockSpec((tk, tm), lambda i, j, k: (k, i)) if ta
                        else pl.BlockSpec((tm, tk), lambda i, j, k: (i, k)))
    for _, b in pairs:
        arrays.append(b)
        in_specs.append(pl.BlockSpec((tn, tk), lambda i, j, k: (j, k)) if tb
                        else pl.BlockSpec((tk, tn), lambda i, j, k: (k, j)))
    for arr, kind, off in extras:
        ob = off // tn
        assert off % tn == 0
        arrays.append(arr)
        if kind == "tile":
            in_specs.append(pl.BlockSpec((tm, tn), lambda i, j, k, ob=ob: (i, j + ob)))
        else:
            in_specs.append(pl.BlockSpec((1, tn), lambda i, j, k, ob=ob: (0, j + ob)))
    out_shape, out_specs = [], []
    for o in outs:
        if o["kind"] == "tile":
            out_shape.append(jax.ShapeDtypeStruct((M, N), o["dtype"]))
            out_specs.append(pl.BlockSpec((tm, tn), lambda i, j, k: (i, j)))
        else:
            out_shape.append(jax.ShapeDtypeStruct((n_i, 1, N), F32))
            out_specs.append(pl.BlockSpec((1, 1, tn), lambda i, j, k: (i, 0, j)))
    dims = (((0 if ta else 1,), (1 if tb else 0,)), ((), ()))

    def body(*refs):
        a_refs, b_refs = refs[:n_p], refs[n_p:2 * n_p]
        e_refs = refs[2 * n_p:2 * n_p + n_e]
        o_refs = refs[2 * n_p + n_e:2 * n_p + n_e + len(outs)]
        acc_refs = refs[2 * n_p + n_e + len(outs):]
        k = pl.program_id(2)

        @pl.when(k == 0)
        def _():
            for acc in acc_refs:
                acc[...] = jnp.zeros_like(acc)

        for p in range(n_p):
            a, b = a_refs[p][...], b_refs[p][...]
            if a_pro is not None:
                a = a_pro(a)
            if b_pro is not None:
                b = b_pro(b)
            acc_refs[p][...] += lax.dot_general(a, b, dims, preferred_element_type=F32)

        @pl.when(k == n_k - 1)
        def _():
            vals = epi([acc[...] for acc in acc_refs], [e[...] for e in e_refs])
            for o, o_ref, val in zip(outs, o_refs, vals):
                if o["kind"] == "tile":
                    o_ref[...] = val.astype(o_ref.dtype)
                else:
                    o_ref[0] = val

    res = _pcall(
        body, name=name, grid=(n_i, n_j, n_k), in_specs=in_specs, out_specs=out_specs, out_shape=out_shape,
        scratch_shapes=[pltpu.VMEM((tm, tn), F32) for _ in pairs],
        compiler_params=_params(("parallel", "parallel", "arbitrary")),
    )(*arrays)
    return res


def _tile_out(dtype):
    return {"kind": "tile", "dtype": dtype}


_COLSUM = {"kind": "colsum"}


def _colsum(v):
    return jnp.sum(v, axis=0, keepdims=True)


def _norm_mod(name, x, norm_w, scale, shift):
    S, D = x.shape
    tr = _tile(S, 256)

    def body(x_ref, nw_ref, sc_ref, sh_ref, h_ref):
        xv = x_ref[...]
        r = lax.rsqrt(jnp.mean(xv * xv, axis=-1, keepdims=True) + EPS)
        h_ref[...] = ((xv * r * nw_ref[...]) * (1.0 + sc_ref[...]) + sh_ref[...]).astype(BF16)

    row = pl.BlockSpec((1, D), lambda i: (0, 0))
    til = pl.BlockSpec((tr, D), lambda i: (i, 0))
    return _pcall(body, name=name, grid=(S // tr,), in_specs=[til, row, row, row], out_specs=til,
                  out_shape=jax.ShapeDtypeStruct((S, D), BF16), compiler_params=_params(("parallel",)))(
                      x, norm_w, scale, shift)


def _norm_mod_bwd(name, dh, x, dres, norm_w, scale, gate_o=None):
    S, D = x.shape
    tr = _tile(S, 256)
    n_r = S // tr
    with_gate = gate_o is not None

    def body(*refs):
        if with_gate:
            dh_ref, x_ref, dres_ref, nw_ref, sc_ref, o_ref, g_ref, dx_ref, p1, p2, p3, do_ref, p4 = refs
        else:
            dh_ref, x_ref, dres_ref, nw_ref, sc_ref, dx_ref, p1, p2, p3 = refs
        xv, dhv, nw = x_ref[...], dh_ref[...], nw_ref[...]
        r = lax.rsqrt(jnp.mean(xv * xv, axis=-1, keepdims=True) + EPS)
        xh = xv * r
        p1[0] = _colsum(dhv)
        p2[0] = _colsum(dhv * (xh * nw))
        dn = dhv * (1.0 + sc_ref[...])
        p3[0] = _colsum(dn * xh)
        dxh = dn * nw
        dx = dres_ref[...] + r * (dxh - xh * jnp.mean(dxh * xh, axis=-1, keepdims=True))
        dx_ref[...] = dx
        if with_gate:
            do_ref[...] = (dx * g_ref[...]).astype(BF16)
            p4[0] = _colsum(dx * o_ref[...].astype(F32))

    row = pl.BlockSpec((1, D), lambda i: (0, 0))
    til = pl.BlockSpec((tr, D), lambda i: (i, 0))
    part = pl.BlockSpec((1, 1, D), lambda i: (i, 0, 0))
    part_shape = jax.ShapeDtypeStruct((n_r, 1, D), F32)
    in_specs = [til, til, til, row, row]
    arrays = [dh, x, dres, norm_w, scale]
    out_specs = [til, part, part, part]
    out_shape = [jax.ShapeDtypeStruct((S, D), F32), part_shape, part_shape, part_shape]
    if with_gate:
        in_specs += [til, row]
        arrays += list(gate_o)
        out_specs += [til, part]
        out_shape += [jax.ShapeDtypeStruct((S, D), BF16), part_shape]
    return _pcall(body, name=name, grid=(n_r,), in_specs=in_specs, out_specs=out_specs, out_shape=out_shape,
                  compiler_params=_params(("parallel",)))(*arrays)


def _pool_w_specs(rows, cg):
    return [pl.BlockSpec((rows, cg), lambda g, j=j: (N_GROUPS * j + g, 0)) for j in range(N_CHIPS)]


def _pool_fwd(proj, wp_full, pool_scale, S, PW):
    cg = PW // N_GROUPS
    rows = cg // N_CHIPS
    T = _tile(S, POOL_T)
    n_t = S // T

    def body(u_ref, w0, w1, w2, w3, ps_ref, pooled_ref, pa_ref):
        g = pl.program_id(0)
        win = jnp.left_shift(2, g)
        w = jnp.concatenate([w0[...], w1[...], w2[...], w3[...]], axis=0)
        t_i = lax.broadcasted_iota(jnp.int32, (T, T), 0)
        j_i = lax.broadcasted_iota(jnp.int32, (T, T), 1)
        b_cur = ((j_i <= t_i) & (j_i > t_i - win)).astype(BF16)
        b_prev = (j_i - T > t_i - win).astype(BF16)
        row = lax.broadcasted_iota(jnp.int32, (T, 1), 0)
        for r in range(n_t):
            cur = u_ref[r * T:(r + 1) * T, :]
            ws = jnp.dot(b_cur, cur, preferred_element_type=F32)
            if r > 0:
                ws += jnp.dot(b_prev, u_ref[(r - 1) * T:r * T, :], preferred_element_type=F32)
            count = jnp.minimum(row + (r * T + 1), win).astype(F32)
            pooled = (ws / count - cur.astype(F32)).astype(BF16)
            pooled_ref[r * T:(r + 1) * T, :] = pooled
            mixed = jnp.dot(pooled, w, preferred_element_type=F32)
            pa_ref[r * T:(r + 1) * T, :] = (mixed * ps_ref[...]).astype(BF16)

    col = pl.BlockSpec((S, cg), lambda g: (0, g))
    return _pcall(
        body, name="pool_fwd", grid=(N_GROUPS,),
        in_specs=[col] + _pool_w_specs(rows, cg) + [pl.BlockSpec((1, cg), lambda g: (0, g))],
        out_specs=[col, col],
        out_shape=[jax.ShapeDtypeStruct((S, PW), BF16), jax.ShapeDtypeStruct((S, PW), BF16)],
        compiler_params=_params(("parallel",)),
    )(proj, wp_full, wp_full, wp_full, wp_full, pool_scale)


def _pool_bwd(dpa, pooled, wp_full, pool_scale, S, PW):
    cg = PW // N_GROUPS
    rows = cg // N_CHIPS
    T = _tile(S, POOL_T)
    n_t = S // T

    def body(dpa_ref, pooled_ref, w0, w1, w2, w3, ps_ref, du_ref, gw_ref, gs_ref, dp_s, dpc_s, dmx_s):
        g = pl.program_id(0)
        win = jnp.left_shift(2, g)
        w = jnp.concatenate([w0[...], w1[...], w2[...], w3[...]], axis=0)
        row = lax.broadcasted_iota(jnp.int32, (T, 1), 0)
        gs = jnp.zeros((1, cg), F32)
        for r in range(n_t):
            sl = slice(r * T, (r + 1) * T)
            mixed = jnp.dot(pooled_ref[sl, :], w, preferred_element_type=F32)
            dpa_t = dpa_ref[sl, :]
            gs += _colsum(dpa_t * mixed)
            dmx = (dpa_t * ps_ref[...]).astype(BF16)
            dmx_s[sl, :] = dmx
            dpo = lax.dot_general(dmx, w, (((1,), (1,)), ((), ())), preferred_element_type=F32)
            dp_s[sl, :] = dpo
            count = jnp.minimum(row + (r * T + 1), win).astype(F32)
            dpc_s[sl, :] = (dpo / count).astype(BF16)
        gs_ref[...] = gs
        gw = lax.dot_general(pooled_ref[...], dmx_s[...], (((0,), (0,)), ((), ())), preferred_element_type=F32)
        for j in range(N_CHIPS):
            gw_ref[j, 0] = gw[j * rows:(j + 1) * rows, :].astype(BF16)
        j_i = lax.broadcasted_iota(jnp.int32, (T, T), 0)
        t_i = lax.broadcasted_iota(jnp.int32, (T, T), 1)
        b_cur = ((t_i >= j_i) & (t_i < j_i + win)).astype(BF16)
        b_next = (t_i + T < j_i + win).astype(BF16)
        for r in range(n_t):
            sl = slice(r * T, (r + 1) * T)
            acc = jnp.dot(b_cur, dpc_s[sl, :], preferred_element_type=F32)
            if r + 1 < n_t:
                acc += jnp.dot(b_next, dpc_s[(r + 1) * T:(r + 2) * T, :], preferred_element_type=F32)
            du_ref[sl, :] = (acc - dp_s[sl, :]).astype(BF16)

    col = pl.BlockSpec((S, cg), lambda g: (0, g))
    return _pcall(
        body, name="pool_bwd", grid=(N_GROUPS,),
        in_specs=[col, col] + _pool_w_specs(rows, cg) + [pl.BlockSpec((1, cg), lambda g: (0, g))],
        out_specs=[col, pl.BlockSpec((N_CHIPS, 1, rows, cg), lambda g: (0, g, 0, 0)),
                   pl.BlockSpec((1, cg), lambda g: (0, g))],
        out_shape=[jax.ShapeDtypeStruct((S, PW), BF16),
                   jax.ShapeDtypeStruct((N_CHIPS, N_GROUPS, rows, cg), BF16),
                   jax.ShapeDtypeStruct((1, PW), F32)],
        scratch_shapes=[pltpu.VMEM((S, cg), F32), pltpu.VMEM((S, cg), BF16), pltpu.VMEM((S, cg), BF16)],
        compiler_params=_params(("parallel",)),
    )(dpa, pooled, wp_full, wp_full, wp_full, wp_full, pool_scale)


_NT = (((1,), (1,)), ((), ()))
_TN = (((0,), (0,)), ((), ()))


def _split_dot(v, tri):
    hi = v.astype(BF16)
    lo = (v - hi.astype(F32)).astype(BF16)
    return jnp.dot(hi, tri, preferred_element_type=F32) + jnp.dot(lo, tri, preferred_element_type=F32)


def _sb_block(q_i, k_j, carry_l, tri_l, delta):
    tq, tk = q_i.shape[0], k_j.shape[0]
    s = lax.dot_general(q_i, k_j, _NT, preferred_element_type=F32) * (1.0 / math.sqrt(HEAD_DIM))
    lp = jnp.log(1.0 + jnp.exp(-jnp.abs(s)))
    l = -jnp.maximum(s, 0.0) - lp
    lb = l + s
    mask = (lax.broadcasted_iota(jnp.int32, (tq, tk), 0) - lax.broadcasted_iota(jnp.int32, (tq, tk), 1) + delta) > 0
    l = jnp.where(mask, l, 0.0)
    within = _split_dot(l, tri_l)
    a = jnp.where(mask, jnp.exp(lb + within + carry_l), 0.0)
    return a, l, lb, mask


def _qk_norm(x_ref, w_ref):
    xv = x_ref[...].astype(F32)
    r = lax.rsqrt(jnp.mean(xv * xv, axis=-1, keepdims=True) + EPS)
    return xv * r, r


def _attn_fwd(proj, q_norm_w, k_norm_w, S, H, q_off):
    tq, tk = _tile(S, ATT_TQ), _tile(S, ATT_TK)
    n_q, kpq = S // tq, tq // tk

    def body(q_ref, k_ref, v_ref, qw_ref, kw_ref, att_ref, attf_ref, qn_s, kn_s):
        qh, _ = _qk_norm(q_ref, qw_ref)
        qn_s[...] = (qh * qw_ref[...]).astype(BF16)
        kh, _ = _qk_norm(k_ref, kw_ref)
        kn_s[...] = (kh * kw_ref[...]).astype(BF16)
        tri_l = (lax.broadcasted_iota(jnp.int32, (tk, tk), 0) > lax.broadcasted_iota(jnp.int32, (tk, tk), 1)).astype(BF16)

        def q_step(i, _):
            qs = pl.multiple_of(i * tq, tq)
            q_i = qn_s[pl.ds(qs, tq), :]
            n_kb = (i + 1) * kpq

            def k_step(jj, carry):
                carry_l, acc = carry
                ks = pl.multiple_of((n_kb - 1 - jj) * tk, tk)
                a, l, _, _ = _sb_block(q_i, kn_s[pl.ds(ks, tk), :], carry_l, tri_l, qs - ks)
                acc = acc + jnp.dot(a.astype(BF16), v_ref[pl.ds(ks, tk), :], preferred_element_type=F32)
                return carry_l + jnp.sum(l, axis=1, keepdims=True), acc

            _, acc = lax.fori_loop(0, n_kb, k_step, (jnp.zeros((tq, 1), F32), jnp.zeros((tq, HEAD_DIM), F32)))
            att_ref[pl.ds(qs, tq), :] = acc.astype(BF16)
            attf_ref[pl.ds(qs, tq), :] = acc
            return 0

        lax.fori_loop(0, n_q, q_step, 0)

    def col(off):
        return pl.BlockSpec((S, HEAD_DIM), lambda h, off=off: (0, off + h))

    wspec = pl.BlockSpec((1, HEAD_DIM), lambda h: (0, 0))
    return _pcall(
        body, name="attn_fwd", grid=(H,),
        in_specs=[col(q_off), col(q_off + H), col(q_off + 2 * H), wspec, wspec],
        out_specs=[col(0), col(0)],
        out_shape=[jax.ShapeDtypeStruct((S, H * HEAD_DIM), BF16), jax.ShapeDtypeStruct((S, H * HEAD_DIM), F32)],
        scratch_shapes=[pltpu.VMEM((S, HEAD_DIM), BF16), pltpu.VMEM((S, HEAD_DIM), BF16)],
        compiler_params=_params(("parallel",)),
    )(proj, proj, proj, q_norm_w, k_norm_w)


def _attn_bwd(proj, datt, attf, q_norm_w, k_norm_w, S, H, q_off):
    tq, tk = _tile(S, ATT_TQ), _tile(S, ATT_TK)
    n_q, kpq = S // tq, tq // tk
    scale = 1.0 / math.sqrt(HEAD_DIM)

    def body(q_ref, k_ref, v_ref, do_ref, o_ref, qw_ref, kw_ref, dq_ref, dk_ref, dv_ref, gq_ref, gk_ref,
             qn_s, kn_s, dk_s, dv_s, gq_s):
        qw, kw = qw_ref[...], kw_ref[...]
        qh, _ = _qk_norm(q_ref, qw_ref)
        qn_s[...] = (qh * qw).astype(BF16)
        kh, _ = _qk_norm(k_ref, kw_ref)
        kn_s[...] = (kh * kw).astype(BF16)
        dk_s[...] = jnp.zeros_like(dk_s)
        dv_s[...] = jnp.zeros_like(dv_s)
        gq_s[...] = jnp.zeros_like(gq_s)
        r_i = lax.broadcasted_iota(jnp.int32, (tk, tk), 0)
        c_i = lax.broadcasted_iota(jnp.int32, (tk, tk), 1)
        tri_l = (r_i > c_i).astype(BF16)
        tri_e = (r_i >= c_i).astype(BF16)

        def q_step(i, _):
            qs = pl.multiple_of(i * tq, tq)
            q_i = qn_s[pl.ds(qs, tq), :]
            do_i = do_ref[pl.ds(qs, tq), :]
            d_i = jnp.sum(do_i.astype(F32) * o_ref[pl.ds(qs, tq), :], axis=1, keepdims=True)
            n_kb = (i + 1) * kpq

            def k_step(jj, carry):
                carry_l, carry_e, dq_acc = carry
                ks = pl.multiple_of((n_kb - 1 - jj) * tk, tk)
                k_j = kn_s[pl.ds(ks, tk), :]
                a, l, lb, mask = _sb_block(q_i, k_j, carry_l, tri_l, qs - ks)
                a_bf = a.astype(BF16)
                da = lax.dot_general(do_i, v_ref[pl.ds(ks, tk), :], _NT, preferred_element_type=F32)
                e = da * a_bf.astype(F32)
                p = d_i - (_split_dot(e, tri_e) + carry_e)
                sig = jnp.exp(lb)
                dz = (jnp.where(mask, e * (1.0 - sig) - p * sig, 0.0) * scale).astype(BF16)
                dq_acc = dq_acc + jnp.dot(dz, k_j, preferred_element_type=F32)
                dk_s[pl.ds(ks, tk), :] += lax.dot_general(dz, q_i, _TN, preferred_element_type=F32)
                dv_s[pl.ds(ks, tk), :] += lax.dot_general(a_bf, do_i, _TN, preferred_element_type=F32)
                return (carry_l + jnp.sum(l, axis=1, keepdims=True),
                        carry_e + jnp.sum(e, axis=1, keepdims=True), dq_acc)

            zero = jnp.zeros((tq, 1), F32)
            _, _, dqn = lax.fori_loop(0, n_kb, k_step, (zero, zero, jnp.zeros((tq, HEAD_DIM), F32)))
            qv = q_ref[pl.ds(qs, tq), :].astype(F32)
            r = lax.rsqrt(jnp.mean(qv * qv, axis=-1, keepdims=True) + EPS)
            xh = qv * r
            gq_s[...] += _colsum(dqn * xh)
            dxh = dqn * qw
            dq_ref[pl.ds(qs, tq), :] = (r * (dxh - xh * jnp.mean(dxh * xh, axis=-1, keepdims=True))).astype(BF16)
            return 0

        lax.fori_loop(0, n_q, q_step, 0)
        gq_ref[0] = gq_s[...]
        kh, rk = _qk_norm(k_ref, kw_ref)
        dkn = dk_s[...]
        gk_ref[0] = _colsum(dkn * kh)
        dxh = dkn * kw
        dk_ref[...] = (rk * (dxh - kh * jnp.mean(dxh * kh, axis=-1, keepdims=True))).astype(BF16)
        dv_ref[...] = dv_s[...].astype(BF16)

    def col(off):
        return pl.BlockSpec((S, HEAD_DIM), lambda h, off=off: (0, off + h))

    wspec = pl.BlockSpec((1, HEAD_DIM), lambda h: (0, 0))
    gspec = pl.BlockSpec((1, 1, HEAD_DIM), lambda h: (h, 0, 0))
    act = jax.ShapeDtypeStruct((S, H * HEAD_DIM), BF16)
    gsh = jax.ShapeDtypeStruct((H, 1, HEAD_DIM), F32)
    return _pcall(
        body, name="attn_bwd", grid=(H,),
        in_specs=[col(q_off), col(q_off + H), col(q_off + 2 * H), col(0), col(0), wspec, wspec],
        out_specs=[col(0), col(0), col(0), gspec, gspec],
        out_shape=[act, act, act, gsh, gsh],
        scratch_shapes=[pltpu.VMEM((S, HEAD_DIM), BF16), pltpu.VMEM((S, HEAD_DIM), BF16),
                        pltpu.VMEM((S, HEAD_DIM), F32), pltpu.VMEM((S, HEAD_DIM), F32),
                        pltpu.VMEM((1, HEAD_DIM), F32)],
        compiler_params=_params(("parallel",)),
    )(proj, proj, proj, datt, attf, q_norm_w, k_norm_w)


def _place():
    x, y, c = lax.axis_index("x"), lax.axis_index("y"), lax.axis_index("c")
    chips = [(1 - x, y), (x, 1 - y), (1 - x, 1 - y)]
    return x, y, c, chips


def _dev_allgather(name, v):
    m_per, n = v.shape

    def body(x_ref, out_ref, send_sems, recv_sems, local_sem):
        x, y, c, chips = _place()
        me, sibling = (x, y, c), (x, y, 1 - c)

        def rows(px, py, pc):
            return out_ref.at[pl.ds((4 * px + 2 * py + pc) * m_per, m_per), :]

        def copy(k, block, to, src=None):
            return pltpu.make_async_remote_copy(
                src_ref=rows(*block) if src is None else src, dst_ref=rows(*block),
                send_sem=send_sems.at[k], recv_sem=recv_sems.at[k], device_id=to, device_id_type=MESH)

        mine = pltpu.make_async_copy(x_ref, rows(*me), local_sem)
        mine.start()
        first = [copy(0, me, sibling, src=x_ref)]
        first += [copy(1 + j, me, (*chip, c), src=x_ref) for j, chip in enumerate(chips)]
        for cp in first:
            cp.start()
        passed = [copy(4 + j, (*chip, c), sibling) for j, chip in enumerate(chips)]
        for j, chip in enumerate(chips):
            copy(1 + j, (*chip, c), me).wait_recv()
            passed[j].start()
        copy(0, sibling, me).wait_recv()
        for j, chip in enumerate(chips):
            copy(4 + j, (*chip, 1 - c), me).wait_recv()
        for cp in first + passed:
            cp.wait_send()
        mine.wait()

    return _pcall(
        body, name=name, out_shape=jax.ShapeDtypeStruct((N_DEV * m_per, n), v.dtype),
        in_specs=[pl.BlockSpec(memory_space=pltpu.VMEM)], out_specs=pl.BlockSpec(memory_space=pltpu.VMEM),
        scratch_shapes=[pltpu.SemaphoreType.DMA((7,)), pltpu.SemaphoreType.DMA((7,)), pltpu.SemaphoreType.DMA],
        compiler_params=pltpu.CompilerParams(vmem_limit_bytes=VMEM_LIMIT_V7X),
    )(v)


class _W:
    def __init__(self, name, kind, R, C):
        self.name, self.kind, self.R, self.C = name, kind, R, C

    @property
    def shard_shape(self):
        return (self.R, self.C // N_CHIPS) if self.kind == "col" else (self.R // N_CHIPS, self.C)

    @property
    def half_rows(self):
        return self.shard_shape[0] // 2

    def shard_half(self, ref, half):
        return ref.at[pl.ds(half * self.half_rows, self.half_rows), :]

    def region(self, full_ref, chip, half):
        hr = self.half_rows
        if self.kind == "col":
            cw = self.C // N_CHIPS
            return full_ref.at[pl.ds(half * hr, hr), pl.ds(chip * cw, cw)]
        return full_ref.at[pl.ds(chip * (2 * hr) + half * hr, hr), :]

    def region_both(self, full_ref, chip):
        hr = self.half_rows
        if self.kind == "col":
            cw = self.C // N_CHIPS
            return full_ref.at[:, pl.ds(chip * cw, cw)]
        return full_ref.at[pl.ds(chip * (2 * hr), 2 * hr), :]


def _weights_allgather(ws, shards):
    n_w = len(ws)

    def body(*refs):
        sh, full = refs[:n_w], refs[n_w:2 * n_w]
        send_sems, recv_sems, local_sems = refs[2 * n_w:]
        x, y, c, chips = _place()
        my_chip = 2 * x + y

        def direct(i, k, chip, recv=False):
            w = ws[i]
            src_chip = (2 * chip[0] + chip[1]) if recv else my_chip
            dst = w.region(full[i], src_chip, c)
            return pltpu.make_async_remote_copy(
                src_ref=dst if recv else w.shard_half(sh[i], c), dst_ref=dst,
                send_sem=send_sems.at[6 * i + k], recv_sem=recv_sems.at[6 * i + k],
                device_id=(*chip, c), device_id_type=MESH)

        def passed(i, k, chip, half):
            reg = ws[i].region(full[i], 2 * chip[0] + chip[1], half)
            return pltpu.make_async_remote_copy(
                src_ref=reg, dst_ref=reg, send_sem=send_sems.at[6 * i + 3 + k], recv_sem=recv_sems.at[6 * i + 3 + k],
                device_id=(x, y, 1 - c), device_id_type=MESH)

        started = []
        for i, w in enumerate(ws):
            for k, chip in enumerate(chips):
                cp = direct(i, k, chip)
                cp.start()
                started.append(cp)
        mine = [pltpu.make_async_copy(sh[i], w.region_both(full[i], my_chip), local_sems.at[i]) for i, w in enumerate(ws)]
        for cp in mine:
            cp.start()
        for i in range(n_w):
            for k, chip in enumerate(chips):
                direct(i, k, chip, recv=True).wait_recv()
                cp = passed(i, k, chip, c)
                cp.start()
                started.append(cp)
        for i in range(n_w):
            for k, chip in enumerate(chips):
                passed(i, k, chip, 1 - c).wait_recv()
        for cp in started:
            cp.wait_send()
        for cp in mine:
            cp.wait()

    return _pcall(
        body, name="weights_allgather",
        out_shape=[jax.ShapeDtypeStruct((w.R, w.C), BF16) for w in ws],
        in_specs=[ANY] * n_w, out_specs=[ANY] * n_w,
        scratch_shapes=[pltpu.SemaphoreType.DMA((6 * n_w,)), pltpu.SemaphoreType.DMA((6 * n_w,)),
                        pltpu.SemaphoreType.DMA((n_w,))],
    )(*shards)


def _half_view(w, g):
    return g if w.kind == "col" else g.reshape(N_CHIPS, w.R // N_CHIPS, w.C)


def _pair_exchange(ws, grads):
    n_w = len(ws)

    def half_all(w, ref, half):
        hr = w.half_rows
        if w.kind == "col":
            return ref.at[pl.ds(half * hr, hr), :]
        return ref.at[:, pl.ds(half * hr, hr), :]

    def body(*refs):
        g, got = refs[:n_w], refs[n_w:2 * n_w]
        send_sems, recv_sems = refs[2 * n_w:]
        x, y, c, _ = _place()
        copies = [pltpu.make_async_remote_copy(
            src_ref=half_all(w, g[i], 1 - c), dst_ref=got[i], send_sem=send_sems.at[i], recv_sem=recv_sems.at[i],
            device_id=(x, y, 1 - c), device_id_type=MESH) for i, w in enumerate(ws)]
        for cp in copies:
            cp.start()
        for cp in copies:
            cp.wait_recv()
        for cp in copies:
            cp.wait_send()

    def got_shape(w):
        hr = w.half_rows
        return (hr, w.C) if w.kind == "col" else (N_CHIPS, hr, w.C)

    return _pcall(
        body, name="grad_pair_exchange",
        out_shape=[jax.ShapeDtypeStruct(got_shape(w), BF16) for w in ws],
        in_specs=[ANY] * n_w, out_specs=[ANY] * n_w,
        scratch_shapes=[pltpu.SemaphoreType.DMA((n_w,)), pltpu.SemaphoreType.DMA((n_w,))],
    )(*[_half_view(w, g) for w, g in zip(ws, grads)])


def _pair_sum(w, g, got, c_arr):
    hr = w.half_rows
    if w.kind == "col":
        tr, tc = _tile(hr, 512), _tile(w.C, 2048)
        n_r = hr // tr
        grid = (n_r, w.C // tc)
        g_spec = pl.BlockSpec((tr, tc), lambda i, j, c: (c[0] * n_r + i, j))
        o_spec = pl.BlockSpec((tr, tc), lambda i, j, c: (i, j))
    else:
        tr = _tile(hr, 512)
        n_r = hr // tr
        grid = (N_CHIPS, n_r)
        g_spec = pl.BlockSpec((1, tr, w.C), lambda s, i, c: (s, c[0] * n_r + i, 0))
        o_spec = pl.BlockSpec((1, tr, w.C), lambda s, i, c: (s, i, 0))

    def body(c_ref, g_ref, got_ref, out_ref):
        out_ref[...] = (g_ref[...].astype(F32) + got_ref[...].astype(F32)).astype(BF16)

    return _pcall(
        body, name="grad_pair_sum_" + w.name, out_shape=jax.ShapeDtypeStruct(got.shape, BF16),
        grid_spec=pltpu.PrefetchScalarGridSpec(num_scalar_prefetch=1, grid=grid, in_specs=[g_spec, o_spec],
                                               out_specs=o_spec),
        compiler_params=_params(("parallel", "parallel")),
    )(c_arr, _half_view(w, g), got)


def _chip_exchange(ws, sums):
    n_w = len(ws)

    def piece(w, ref, chip):
        if w.kind == "col":
            cw = w.C // N_CHIPS
            return ref.at[:, pl.ds(chip * cw, cw)]
        return ref.at[chip]

    def body(*refs):
        p, q = refs[:n_w], refs[n_w:2 * n_w]
        send_sems, recv_sems, local_sems = refs[2 * n_w:]
        x, y, c, chips = _place()
        my_chip = 2 * x + y

        def copy(i, k, chip, recv=False):
            to_chip = 2 * chip[0] + chip[1]
            return pltpu.make_async_remote_copy(
                src_ref=piece(ws[i], p[i], to_chip), dst_ref=q[i].at[to_chip if recv else my_chip],
                send_sem=send_sems.at[3 * i + k], recv_sem=recv_sems.at[3 * i + k],
                device_id=(*chip, c), device_id_type=MESH)

        started = []
        for i in range(n_w):
            for k, chip in enumerate(chips):
                cp = copy(i, k, chip)
                cp.start()
                started.append(cp)
        mine = [pltpu.make_async_copy(piece(w, p[i], my_chip), q[i].at[my_chip], local_sems.at[i])
                for i, w in enumerate(ws)]
        for cp in mine:
            cp.start()
        for i in range(n_w):
            for k, chip in enumerate(chips):
                copy(i, k, chip, recv=True).wait_recv()
        for cp in started:
            cp.wait_send()
        for cp in mine:
            cp.wait()

    def q_shape(w):
        return (N_CHIPS, w.half_rows, w.shard_shape[1])

    return _pcall(
        body, name="grad_chip_exchange",
        out_shape=[jax.ShapeDtypeStruct(q_shape(w), BF16) for w in ws],
        in_specs=[ANY] * n_w, out_specs=[ANY] * n_w,
        scratch_shapes=[pltpu.SemaphoreType.DMA((3 * n_w,)), pltpu.SemaphoreType.DMA((3 * n_w,)),
                        pltpu.SemaphoreType.DMA((n_w,))],
    )(*sums)


def _chip_sum(w, q, c_arr):
    hr, cols = w.half_rows, w.shard_shape[1]
    tr, tc = _tile(hr, 512), _tile(cols, 2048)
    n_r = hr // tr

    def body(c_ref, q0, q1, q2, q3, out_ref):
        out_ref[...] = ((q0[0].astype(F32) + q1[0].astype(F32)) + q2[0].astype(F32)) + q3[0].astype(F32)

    q_specs = [pl.BlockSpec((1, tr, tc), lambda i, j, c, s=s: (s, i, j)) for s in range(N_CHIPS)]
    return _pcall(
        body, name="grad_chip_sum_" + w.name, out_shape=jax.ShapeDtypeStruct(w.shard_shape, F32),
        grid_spec=pltpu.PrefetchScalarGridSpec(
            num_scalar_prefetch=1, grid=(n_r, cols // tc), in_specs=q_specs,
            out_specs=pl.BlockSpec((tr, tc), lambda i, j, c: (c[0] * n_r + i, j))),
        compiler_params=_params(("parallel", "parallel")),
    )(c_arr, q, q, q, q)


def _sibling_fill(ws, grads):
    n_w = len(ws)

    def body(*refs):
        g = refs[n_w:2 * n_w]
        send_sems, recv_sems = refs[2 * n_w:]
        x, y, c, _ = _place()
        copies = [pltpu.make_async_remote_copy(
            src_ref=w.shard_half(g[i], c), dst_ref=w.shard_half(g[i], c), send_sem=send_sems.at[i],
            recv_sem=recv_sems.at[i], device_id=(x, y, 1 - c), device_id_type=MESH) for i, w in enumerate(ws)]
        for cp in copies:
            cp.start()
        for i, w in enumerate(ws):
            pltpu.make_async_remote_copy(
                src_ref=w.shard_half(g[i], 1 - c), dst_ref=w.shard_half(g[i], 1 - c), send_sem=send_sems.at[i],
                recv_sem=recv_sems.at[i], device_id=(x, y, 1 - c), device_id_type=MESH).wait_recv()
        for cp in copies:
            cp.wait_send()

    return _pcall(
        body, name="grad_sibling_fill",
        out_shape=[jax.ShapeDtypeStruct(w.shard_shape, F32) for w in ws],
        in_specs=[ANY] * n_w, out_specs=[ANY] * n_w, input_output_aliases={i: i for i in range(n_w)},
        scratch_shapes=[pltpu.SemaphoreType.DMA((n_w,)), pltpu.SemaphoreType.DMA((n_w,))],
    )(*grads)


def _adamw_math(w, g, m, v):
    m = ADAM_B1 * m + (1.0 - ADAM_B1) * g
    v = ADAM_B2 * v + (1.0 - ADAM_B2) * (g * g)
    m_hat = m / (1.0 - ADAM_B1 ** ADAM_STEP)
    v_hat = v / (1.0 - ADAM_B2 ** ADAM_STEP)
    delta = -ADAM_LR * (m_hat / (jnp.sqrt(v_hat) + ADAM_EPS) + ADAM_WD * w)
    return delta, m, v


def _adamw(name, w, g, m, v):
    R, C = w.shape
    tr, tc = _tile(R, 256), _tile(C, 2048)

    def body(w_ref, g_ref, m_ref, v_ref, d_out, m_out, v_out):
        d_out[...], m_out[...], v_out[...] = _adamw_math(w_ref[...], g_ref[...], m_ref[...], v_ref[...])

    spec = pl.BlockSpec((tr, tc), lambda i, j: (i, j))
    sh = jax.ShapeDtypeStruct((R, C), F32)
    return _pcall(body, name=name, grid=(R // tr, C // tc), in_specs=[spec] * 4, out_specs=[spec] * 3,
                  out_shape=[sh, sh, sh], compiler_params=_params(("parallel", "parallel")))(w, g, m, v)


def _ada_update(sct, dmod_sh, w, m, v):
    R, C = w.shape
    tr, tc = _tile(R, 256), _tile(C, 1024)

    def body(s_ref, d_ref, w_ref, m_ref, v_ref, g_out, d_out, m_out, v_out):
        s, d = s_ref[...], d_ref[...]
        g = s[:, 0:1] * d[0:1, :]
        for b in range(1, N_DEV):
            g += s[:, b:b + 1] * d[b:b + 1, :]
        g_out[...] = g
        d_out[...], m_out[...], v_out[...] = _adamw_math(w_ref[...], g, m_ref[...], v_ref[...])

    spec = pl.BlockSpec((tr, tc), lambda i, j: (i, j))
    sh = jax.ShapeDtypeStruct((R, C), F32)
    return _pcall(
        body, name="ada_update", grid=(R // tr, C // tc),
        in_specs=[pl.BlockSpec((tr, N_DEV), lambda i, j: (i, 0)), pl.BlockSpec((N_DEV, tc), lambda i, j: (0, j)),
                  spec, spec, spec],
        out_specs=[spec] * 4, out_shape=[sh] * 4, compiler_params=_params(("parallel", "parallel")),
    )(sct, dmod_sh, w, m, v)


def _cast_bf16(name, w):
    R, C = w.shape
    tr, tc = _tile(R, 512), _tile(C, 2048)

    def body(w_ref, o_ref):
        o_ref[...] = w_ref[...].astype(BF16)

    spec = pl.BlockSpec((tr, tc), lambda i, j: (i, j))
    return _pcall(body, name=name, grid=(R // tr, C // tc), in_specs=[spec], out_specs=spec,
                  out_shape=jax.ShapeDtypeStruct((R, C), BF16), compiler_params=_params(("parallel", "parallel")))(w)


def _silu_rows(c_row):
    D = c_row.shape[1]

    def body(c_ref, o_ref):
        cv = c_ref[...]
        o_ref[...] = cv * jax.nn.sigmoid(cv)

    return _pcall(body, name="silu_c", out_shape=jax.ShapeDtypeStruct((1, D), F32))(c_row)


def _pack_partials(parts, widths, total):
    n = len(widths)

    def body(*refs):
        loss_p, out_ref, loss_ref = refs[n], refs[n + 1], refs[n + 2]
        off = 0
        for ref, wd in zip(refs[:n], widths):
            out_ref[:, off:off + wd] = jnp.sum(ref[...], axis=0)
            off += wd
        if off < total:
            out_ref[:, off:total] = jnp.zeros((1, total - off), F32)
        loss_ref[...] = jnp.sum(jnp.sum(loss_p[...], axis=0), axis=1, keepdims=True)

    return _pcall(body, name="pack_partials",
                  out_shape=[jax.ShapeDtypeStruct((1, total), F32), jax.ShapeDtypeStruct((1, 1), F32)])(*parts)


def _small_update(gathered, offsets, params):
    n_p = len(params)

    def body(*refs):
        g_ref = refs[0]
        prm = refs[1:1 + 3 * n_p]
        outs = refs[1 + 3 * n_p:]
        for i, (off, wd) in enumerate(offsets):
            blk = g_ref[:, off:off + wd]
            g = blk[0:1, :]
            for b in range(1, N_DEV):
                g = g + blk[b:b + 1, :]
            w, m, v = prm[3 * i][...], prm[3 * i + 1][...], prm[3 * i + 2][...]
            outs[4 * i][...] = g
            outs[4 * i + 1][...], outs[4 * i + 2][...], outs[4 * i + 3][...] = _adamw_math(w, g, m, v)

    flat = [a for t in params for a in t]
    out_shape = [jax.ShapeDtypeStruct(t[0].shape, F32) for t in params for _ in range(4)]
    return _pcall(body, name="small_update", out_shape=out_shape)(gathered, *flat)


def kernel(x, c, w_ada, b_ada, norm1_w, w_in, q_norm_w, k_norm_w, w_pool, pool_scale, w_a_up, w_b_up, w_o, norm2_w, w_ff1, w_ff2, loss_target, m_w_ada, m_b_ada, m_norm1_w, m_w_in, m_q_norm_w, m_k_norm_w, m_w_pool, m_pool_scale, m_w_a_up, m_w_b_up, m_w_o, m_norm2_w, m_w_ff1, m_w_ff2, v_w_ada, v_b_ada, v_norm1_w, v_w_in, v_q_norm_w, v_k_norm_w, v_w_pool, v_pool_scale, v_w_a_up, v_w_b_up, v_w_o, v_norm2_w, v_w_ff1, v_w_ff2):
    _, S, D = x.shape
    PW = D // 2
    H = PW // HEAD_DIM
    cg = PW // N_GROUPS
    IN = w_in.shape[2] * N_CHIPS
    FF = w_ff1.shape[2] * N_CHIPS
    A_COLS = w_ada.shape[2]
    xi, yi, ci = lax.axis_index("x"), lax.axis_index("y"), lax.axis_index("c")
    chip = 2 * xi + yi
    dev = 2 * chip + ci
    c_arr = jnp.reshape(ci, (1,)).astype(jnp.int32)
    x2, tgt = x[0], loss_target[0]

    ws = [_W("w_in", "col", D, IN), _W("w_pool", "row", PW, cg), _W("w_a_up", "col", PW, D),
          _W("w_b_up", "col", PW, D), _W("w_o", "row", D, D), _W("w_ff1", "col", D, FF), _W("w_ff2", "row", FF, D)]
    w32 = [w_in[0], w_pool[0].reshape(cg, cg), w_a_up[0], w_b_up[0], w_o[0], w_ff1[0], w_ff2[0]]
    m32 = [m_w_in[0], m_w_pool[0].reshape(cg, cg), m_w_a_up[0], m_w_b_up[0], m_w_o[0], m_w_ff1[0], m_w_ff2[0]]
    v32 = [v_w_in[0], v_w_pool[0].reshape(cg, cg), v_w_a_up[0], v_w_b_up[0], v_w_o[0], v_w_ff1[0], v_w_ff2[0]]

    shards16 = [_cast_bf16("cast_" + w.name, a) for w, a in zip(ws, w32)]
    win_f, wpool_f, wa_f, wb_f, wo_f, wff1_f, wff2_f = _weights_allgather(ws, shards16)

    sc_row = _silu_rows(c)
    sc_all = _dev_allgather("gather_silu_c", sc_row.reshape(8, D // 8)).reshape(N_DEV, D)
    sc16 = jnp.concatenate([sc_all, jnp.zeros_like(sc_all)], axis=0)
    b_cols = lax.dynamic_slice(b_ada, (0, chip * A_COLS), (1, A_COLS))
    (mod_cols,) = _mm("mod_cols", [(sc16, w_ada[0])], M=2 * N_DEV, N=A_COLS, K=D, tm=16, tn=1024, tk=1024,
                      a_pro=lambda a: a.astype(BF16), b_pro=lambda b: b.astype(BF16),
                      extras=[(b_cols, "row", 0)], outs=[_tile_out(F32)], epi=lambda accs, ex: [accs[0] + ex[0]])
    mod_all = _dev_allgather("gather_mod", mod_cols[:N_DEV]).reshape(N_CHIPS, 2, N_DEV, A_COLS)
    mod_row = lax.dynamic_index_in_dim(mod_all[:, 0], dev, axis=1, keepdims=False).reshape(1, N_CHIPS * A_COLS)
    shift1, scale1, gate1, shift2, scale2, gate2 = [mod_row[:, i * D:(i + 1) * D] for i in range(6)]

    h = _norm_mod("norm1_mod", x2, norm1_w, scale1, shift1)
    (proj,) = _mm("in_proj", [(h, win_f)], M=S, N=IN, K=D, outs=[_tile_out(BF16)], epi=lambda accs, ex: [accs[0]])
    pooled, pa = _pool_fwd(proj, wpool_f, pool_scale, S, PW)
    att, attf = _attn_fwd(proj, q_norm_w, k_norm_w, S, H, PW // HEAD_DIM)

    def merge_epi(accs, ex):
        sa, sb = jax.nn.sigmoid(ex[0].astype(F32)), jax.nn.sigmoid(ex[1].astype(F32))
        return [sa * accs[0] + sb * accs[1], accs[0], accs[1]]

    merged, ya, yb = _mm("branch_up_merge", [(pa, wa_f), (att, wb_f)], M=S, N=D, K=PW,
                         extras=[(proj, "tile", 4 * PW), (proj, "tile", 4 * PW + D)],
                         outs=[_tile_out(BF16)] * 3, epi=merge_epi)
    x1, o = _mm("out_proj", [(merged, wo_f)], M=S, N=D, K=D, extras=[(x2, "tile", 0), (gate1, "row", 0)],
                outs=[_tile_out(F32), _tile_out(BF16)], epi=lambda accs, ex: [ex[0] + ex[1] * accs[0], accs[0]])
    h2 = _norm_mod("norm2_mod", x1, norm2_w, scale2, shift2)
    (rl,) = _mm("ff1", [(h2, wff1_f)], M=S, N=FF, K=D, outs=[_tile_out(BF16)],
                epi=lambda accs, ex: [jnp.maximum(accs[0], 0.0)])

    def square(a):
        af = a.astype(F32)
        return (af * af).astype(BF16)

    def loss_epi(accs, ex):
        x1_t, tgt_t, g2 = ex
        f = accs[0]
        diff = (x1_t + g2 * f) - tgt_t
        dy = diff * (1.0 / D)
        return [dy, dy * g2, _colsum(dy * f), _colsum(diff * diff)]

    dy, df, dgate2_p, loss_p = _mm("ff2_loss", [(rl, wff2_f)], M=S, N=D, K=FF, a_pro=square,
                                   extras=[(x1, "tile", 0), (tgt, "tile", 0), (gate2, "row", 0)],
                                   outs=[_tile_out(F32), _tile_out(BF16), _COLSUM, _COLSUM], epi=loss_epi)

    (g_ff2,) = _mm("grad_w_ff2", [(rl, df)], M=FF, N=D, K=S, ta=True, tm=1024, tn=1024, tk=512, a_pro=square,
                   outs=[_tile_out(BF16)], epi=lambda accs, ex: [accs[0]])
    (dz1,) = _mm("d_ff_hidden", [(df, wff2_f)], M=S, N=FF, K=D, tb=True, extras=[(rl, "tile", 0)],
                 outs=[_tile_out(BF16)], epi=lambda accs, ex: [accs[0] * (2.0 * ex[0].astype(F32))])
    (g_ff1,) = _mm("grad_w_ff1", [(h2, dz1)], M=D, N=FF, K=S, ta=True, tm=1024, tn=1024, tk=512,
                   outs=[_tile_out(BF16)], epi=lambda accs, ex: [accs[0]])
    (dh2,) = _mm("d_h2", [(dz1, wff1_f)], M=S, N=D, K=FF, tb=True, outs=[_tile_out(F32)], epi=lambda accs, ex: [accs[0]])
    dx1, dshift2_p, dscale2_p, gn2_p, do, dgate1_p = _norm_mod_bwd("norm2_bwd", dh2, x1, dy, norm2_w, scale2,
                                                                   gate_o=(o, gate1))

    (g_wo,) = _mm("grad_w_o", [(merged, do)], M=D, N=D, K=S, ta=True, tm=1024, tn=1024, tk=512,
                  outs=[_tile_out(BF16)], epi=lambda accs, ex: [accs[0]])

    def gate_epi(accs, ex):
        dm = accs[0]
        sa, sb = jax.nn.sigmoid(ex[0].astype(F32)), jax.nn.sigmoid(ex[1].astype(F32))
        ya_t, yb_t = ex[2].astype(F32), ex[3].astype(F32)
        return [dm * sa, dm * sb, dm * ya_t * (sa * (1.0 - sa)), dm * yb_t * (sb * (1.0 - sb))]

    dya, dyb, dga, dgb = _mm("d_merged", [(do, wo_f)], M=S, N=D, K=D, tb=True,
                             extras=[(proj, "tile", 4 * PW), (proj, "tile", 4 * PW + D), (ya, "tile", 0), (yb, "tile", 0)],
                             outs=[_tile_out(BF16)] * 4, epi=gate_epi)
    (g_wa,) = _mm("grad_w_a_up", [(pa, dya)], M=PW, N=D, K=S, ta=True, tm=1024, tn=1024, tk=512,
                  outs=[_tile_out(BF16)], epi=lambda accs, ex: [accs[0]])
    (g_wb,) = _mm("grad_w_b_up", [(att, dyb)], M=PW, N=D, K=S, ta=True, tm=1024, tn=1024, tk=512,
                  outs=[_tile_out(BF16)], epi=lambda accs, ex: [accs[0]])
    (dpa,) = _mm("d_pool_out", [(dya, wa_f)], M=S, N=PW, K=D, tb=True, outs=[_tile_out(F32)], epi=lambda accs, ex: [accs[0]])
    (datt,) = _mm("d_att", [(dyb, wb_f)], M=S, N=PW, K=D, tb=True, outs=[_tile_out(BF16)], epi=lambda accs, ex: [accs[0]])
    du, g_wpool4, gscale_p = _pool_bwd(dpa, pooled, wpool_f, pool_scale, S, PW)
    dq, dk, dv, gq_p, gk_p = _attn_bwd(proj, datt, attf, q_norm_w, k_norm_w, S, H, PW // HEAD_DIM)
    dproj = jnp.concatenate([du, dq, dk, dv, dga, dgb], axis=1)
    (g_win,) = _mm("grad_w_in", [(h, dproj)], M=D, N=IN, K=S, ta=True, tm=1024, tn=1024, tk=512,
                   outs=[_tile_out(BF16)], epi=lambda accs, ex: [accs[0]])
    (dh,) = _mm("d_h", [(dproj, win_f)], M=S, N=D, K=IN, tb=True, outs=[_tile_out(F32)], epi=lambda accs, ex: [accs[0]])
    grad_x, dshift1_p, dscale1_p, gn1_p = _norm_mod_bwd("norm1_bwd", dh, x2, dx1, norm1_w, scale1)

    partial = [g_win, g_wpool4.reshape(PW, cg), g_wa, g_wb, g_wo, g_ff1, g_ff2]
    got = _pair_exchange(ws, partial)
    sums = [_pair_sum(w, g, r, c_arr) for w, g, r in zip(ws, partial, got)]
    from_chips = _chip_exchange(ws, sums)
    halves = [_chip_sum(w, q, c_arr) for w, q in zip(ws, from_chips)]
    grads = _sibling_fill(ws, halves)
    upd = [_adamw("adamw_" + w.name, a, g, m, v) for w, a, g, m, v in zip(ws, w32, grads, m32, v32)]

    parts = [dshift1_p, dscale1_p, dgate1_p, dshift2_p, dscale2_p, dgate2_p, gn1_p, gn2_p,
             gscale_p.reshape(1, 1, PW), gq_p, gk_p]
    widths = [D] * 8 + [PW, HEAD_DIM, HEAD_DIM]
    used = sum(widths)
    P = -(-used // 1024) * 1024
    packed, loss_part = _pack_partials(parts + [loss_p], widths, P)
    gathered = _dev_allgather("gather_vector_grads", packed.reshape(8, P // 8)).reshape(N_DEV, P)
    small = [(b_ada, m_b_ada, v_b_ada), (norm1_w, m_norm1_w, v_norm1_w), (norm2_w, m_norm2_w, v_norm2_w),
             (pool_scale, m_pool_scale, v_pool_scale), (q_norm_w, m_q_norm_w, v_q_norm_w),
             (k_norm_w, m_k_norm_w, v_k_norm_w)]
    offsets = [(0, 6 * D), (6 * D, D), (7 * D, D), (8 * D, PW), (8 * D + PW, HEAD_DIM), (8 * D + PW + HEAD_DIM, HEAD_DIM)]
    su = _small_update(gathered, offsets, small)
    (g_b, d_b, nm_b, nv_b, g_n1, d_n1, nm_n1, nv_n1, g_n2, d_n2, nm_n2, nv_n2, g_ps, d_ps, nm_ps, nv_ps,
     g_qn, d_qn, nm_qn, nv_qn, g_kn, d_kn, nm_kn, nv_kn) = su
    dmod_sh = lax.dynamic_slice(gathered, (0, chip * A_COLS), (N_DEV, A_COLS))
    g_ada, d_ada, nm_ada, nv_ada = _ada_update(sc_all.T, dmod_sh, w_ada[0], m_w_ada[0], v_w_ada[0])

    loss = 0.5 / D * lax.psum(loss_part[0, 0], ("x", "y", "c"))

    def up(a):
        return a[None]

    def pool4(a):
        return a.reshape(1, N_GROUPS, cg // N_CHIPS, cg)

    (d_win, nm_win, nv_win), (d_wp, nm_wp, nv_wp), (d_wa, nm_wa, nv_wa), (d_wb, nm_wb, nv_wb), \
        (d_wo, nm_wo, nv_wo), (d_f1, nm_f1, nv_f1), (d_f2, nm_f2, nv_f2) = upd
    gr_win, gr_wp, gr_wa, gr_wb, gr_wo, gr_f1, gr_f2 = grads
    return (
        loss, grad_x[None],
        up(g_ada), g_b, g_n1, up(gr_win), g_qn, g_kn, pool4(gr_wp), g_ps, up(gr_wa), up(gr_wb), up(gr_wo), g_n2,
        up(gr_f1), up(gr_f2),
        up(d_ada), d_b, d_n1, up(d_win), d_qn, d_kn, pool4(d_wp), d_ps, up(d_wa), up(d_wb), up(d_wo), d_n2,
        up(d_f1), up(d_f2),
        up(nm_ada), nm_b, nm_n1, up(nm_win), nm_qn, nm_kn, pool4(nm_wp), nm_ps, up(nm_wa), up(nm_wb), up(nm_wo), nm_n2,
        up(nm_f1), up(nm_f2),
        up(nv_ada), nv_b, nv_n1, up(nv_win), nv_qn, nv_kn, pool4(nv_wp), nv_ps, up(nv_wa), up(nv_wb), up(nv_wo), nv_n2,
        up(nv_f1), up(nv_f2),
    )
```

```python
import functools
import math

import jax
import jax.numpy as jnp
from jax import lax
from jax.experimental import pallas as pl
from jax.experimental.pallas import tpu as pltpu

F32 = jnp.float32
BF16 = jnp.bfloat16
MESH = pl.DeviceIdType.MESH
ANY = pl.BlockSpec(memory_space=pl.ANY)

EPS = 1e-6
HEAD_DIM = 128
POOL_WINDOWS = (2, 4, 8, 16)
N_GROUPS = len(POOL_WINDOWS)
N_CHIPS = 4
N_DEV = 8
ADAM_LR, ADAM_B1, ADAM_B2, ADAM_EPS, ADAM_WD, ADAM_STEP = 0.001, 0.9, 0.999, 1e-08, 0.01, 10
VMEM_LIMIT_V7X = 56 * 1024 * 1024
ATT_T = 256
POOL_T = 256


def _pcall(body, **kw):
    return pl.pallas_call(body, **kw)


def _params(sem=None):
    return pltpu.CompilerParams(dimension_semantics=sem, vmem_limit_bytes=VMEM_LIMIT_V7X)


def _tile(n, pref):
    if n <= pref:
        return n
    t = pref
    while n % t:
        t //= 2
    return t


def _mm(name, pairs, *, M, N, K, ta=False, tb=False, tm=512, tn=1024, tk=1024,
        a_pro=None, b_pro=None, extras=(), outs, epi):
    tm, tn, tk = _tile(M, tm), _tile(N, tn), _tile(K, tk)
    n_i, n_j, n_k = M // tm, N // tn, K // tk
    n_p, n_e = len(pairs), len(extras)
    arrays, in_specs = [], []
    for a, _ in pairs:
        arrays.append(a)
        in_specs.append(pl.BlockSpec((tk, tm), lambda i, j, k: (k, i)) if ta
                        else pl.BlockSpec((tm, tk), lambda i, j, k: (i, k)))
    for _, b in pairs:
        arrays.append(b)
        in_specs.append(pl.BlockSpec((tn, tk), lambda i, j, k: (j, k)) if tb
                        else pl.BlockSpec((tk, tn), lambda i, j, k: (k, j)))
    for arr, kind, off in extras:
        ob = off // tn
        assert off % tn == 0
        arrays.append(arr)
        if kind == "tile":
            in_specs.append(pl.BlockSpec((tm, tn), lambda i, j, k, ob=ob: (i, j + ob)))
        else:
            in_specs.append(pl.BlockSpec((1, tn), lambda i, j, k, ob=ob: (0, j + ob)))
    out_shape, out_specs = [], []
    for o in outs:
        if o["kind"] == "tile":
            out_shape.append(jax.ShapeDtypeStruct((M, N), o["dtype"]))
            out_specs.append(pl.BlockSpec((tm, tn), lambda i, j, k: (i, j)))
        else:
            out_shape.append(jax.ShapeDtypeStruct((n_i, 1, N), F32))
            out_specs.append(pl.BlockSpec((1, 1, tn), lambda i, j, k: (i, 0, j)))
    dims = (((0 if ta else 1,), (1 if tb else 0,)), ((), ()))

    def body(*refs):
        a_refs, b_refs = refs[:n_p], refs[n_p:2 * n_p]
        e_refs = refs[2 * n_p:2 * n_p + n_e]
        o_refs = refs[2 * n_p + n_e:2 * n_p + n_e + len(outs)]
        acc_refs = refs[2 * n_p + n_e + len(outs):]
        k = pl.program_id(2)

        @pl.when(k == 0)
        def _():
            for acc in acc_refs:
                acc[...] = jnp.zeros_like(acc)

        for p in range(n_p):
            a, b = a_refs[p][...], b_refs[p][...]
            if a_pro is not None:
                a = a_pro(a)
            if b_pro is not None:
                b = b_pro(b)
            acc_refs[p][...] += lax.dot_general(a, b, dims, preferred_element_type=F32)

        @pl.when(k == n_k - 1)
        def _():
            vals = epi([acc[...] for acc in acc_refs], [e[...] for e in e_refs])
            for o, o_ref, val in zip(outs, o_refs, vals):
                if o["kind"] == "tile":
                    o_ref[...] = val.astype(o_ref.dtype)
                else:
                    o_ref[0] = val

    res = _pcall(
        body, name=name, grid=(n_i, n_j, n_k), in_specs=in_specs, out_specs=out_specs, out_shape=out_shape,
        scratch_shapes=[pltpu.VMEM((tm, tn), F32) for _ in pairs],
        compiler_params=_params(("parallel", "parallel", "arbitrary")),
    )(*arrays)
    return res


def _tile_out(dtype):
    return {"kind": "tile", "dtype": dtype}


_COLSUM = {"kind": "colsum"}


def _colsum(v):
    return jnp.sum(v, axis=0, keepdims=True)


def _norm_mod(name, x, norm_w, scale, shift):
    S, D = x.shape
    tr = _tile(S, 256)

    def body(x_ref, nw_ref, sc_ref, sh_ref, h_ref):
        xv = x_ref[...]
        r = lax.rsqrt(jnp.mean(xv * xv, axis=-1, keepdims=True) + EPS)
        h_ref[...] = ((xv * r * nw_ref[...]) * (1.0 + sc_ref[...]) + sh_ref[...]).astype(BF16)

    row = pl.BlockSpec((1, D), lambda i: (0, 0))
    til = pl.BlockSpec((tr, D), lambda i: (i, 0))
    return _pcall(body, name=name, grid=(S // tr,), in_specs=[til, row, row, row], out_specs=til,
                  out_shape=jax.ShapeDtypeStruct((S, D), BF16), compiler_params=_params(("parallel",)))(
                      x, norm_w, scale, shift)


def _norm_mod_bwd(name, dh, x, dres, norm_w, scale, gate_o=None):
    S, D = x.shape
    tr = _tile(S, 256)
    n_r = S // tr
    with_gate = gate_o is not None

    def body(*refs):
        if with_gate:
            dh_ref, x_ref, dres_ref, nw_ref, sc_ref, o_ref, g_ref, dx_ref, p1, p2, p3, do_ref, p4 = refs
        else:
            dh_ref, x_ref, dres_ref, nw_ref, sc_ref, dx_ref, p1, p2, p3 = refs
        xv, dhv, nw = x_ref[...], dh_ref[...], nw_ref[...]
        r = lax.rsqrt(jnp.mean(xv * xv, axis=-1, keepdims=True) + EPS)
        xh = xv * r
        p1[0] = _colsum(dhv)
        p2[0] = _colsum(dhv * (xh * nw))
        dn = dhv * (1.0 + sc_ref[...])
        p3[0] = _colsum(dn * xh)
        dxh = dn * nw
        dx = dres_ref[...] + r * (dxh - xh * jnp.mean(dxh * xh, axis=-1, keepdims=True))
        dx_ref[...] = dx
        if with_gate:
            do_ref[...] = (dx * g_ref[...]).astype(BF16)
            p4[0] = _colsum(dx * o_ref[...].astype(F32))

    row = pl.BlockSpec((1, D), lambda i: (0, 0))
    til = pl.BlockSpec((tr, D), lambda i: (i, 0))
    part = pl.BlockSpec((1, 1, D), lambda i: (i, 0, 0))
    part_shape = jax.ShapeDtypeStruct((n_r, 1, D), F32)
    in_specs = [til, til, til, row, row]
    arrays = [dh, x, dres, norm_w, scale]
    out_specs = [til, part, part, part]
    out_shape = [jax.ShapeDtypeStruct((S, D), F32), part_shape, part_shape, part_shape]
    if with_gate:
        in_specs += [til, row]
        arrays += list(gate_o)
        out_specs += [til, part]
        out_shape += [jax.ShapeDtypeStruct((S, D), BF16), part_shape]
    return _pcall(body, name=name, grid=(n_r,), in_specs=in_specs, out_specs=out_specs, out_shape=out_shape,
                  compiler_params=_params(("parallel",)))(*arrays)


def _pool_w_specs(rows, cg):
    return [pl.BlockSpec((rows, cg), lambda g, j=j: (N_GROUPS * j + g, 0)) for j in range(N_CHIPS)]


def _pool_fwd(proj, wp_full, pool_scale, S, PW):
    cg = PW // N_GROUPS
    rows = cg // N_CHIPS
    T = _tile(S, POOL_T)
    n_t = S // T

    def body(u_ref, w0, w1, w2, w3, ps_ref, pooled_ref, pa_ref):
        g = pl.program_id(0)
        win = jnp.left_shift(2, g)
        w = jnp.concatenate([w0[...], w1[...], w2[...], w3[...]], axis=0)
        t_i = lax.broadcasted_iota(jnp.int32, (T, T), 0)
        j_i = lax.broadcasted_iota(jnp.int32, (T, T), 1)
        b_cur = ((j_i <= t_i) & (j_i > t_i - win)).astype(BF16)
        b_prev = (j_i - T > t_i - win).astype(BF16)
        row = lax.broadcasted_iota(jnp.int32, (T, 1), 0)
        for r in range(n_t):
            cur = u_ref[r * T:(r + 1) * T, :]
            ws = jnp.dot(b_cur, cur, preferred_element_type=F32)
            if r > 0:
                ws += jnp.dot(b_prev, u_ref[(r - 1) * T:r * T, :], preferred_element_type=F32)
            count = jnp.minimum(row + (r * T + 1), win).astype(F32)
            pooled = (ws / count - cur.astype(F32)).astype(BF16)
            pooled_ref[r * T:(r + 1) * T, :] = pooled
            mixed = jnp.dot(pooled, w, preferred_element_type=F32)
            pa_ref[r * T:(r + 1) * T, :] = (mixed * ps_ref[...]).astype(BF16)

    col = pl.BlockSpec((S, cg), lambda g: (0, g))
    return _pcall(
        body, name="pool_fwd", grid=(N_GROUPS,),
        in_specs=[col] + _pool_w_specs(rows, cg) + [pl.BlockSpec((1, cg), lambda g: (0, g))],
        out_specs=[col, col],
        out_shape=[jax.ShapeDtypeStruct((S, PW), BF16), jax.ShapeDtypeStruct((S, PW), BF16)],
        compiler_params=_params(("parallel",)),
    )(proj, wp_full, wp_full, wp_full, wp_full, pool_scale)


def _pool_bwd(dpa, pooled, wp_full, pool_scale, S, PW):
    cg = PW // N_GROUPS
    rows = cg // N_CHIPS
    T = _tile(S, POOL_T)
    n_t = S // T

    def body(dpa_ref, pooled_ref, w0, w1, w2, w3, ps_ref, du_ref, gw_ref, gs_ref, dp_s, dpc_s, dmx_s):
        g = pl.program_id(0)
        win = jnp.left_shift(2, g)
        w = jnp.concatenate([w0[...], w1[...], w2[...], w3[...]], axis=0)
        row = lax.broadcasted_iota(jnp.int32, (T, 1), 0)
        gs = jnp.zeros((1, cg), F32)
        for r in range(n_t):
            sl = slice(r * T, (r + 1) * T)
            mixed = jnp.dot(pooled_ref[sl, :], w, preferred_element_type=F32)
            dpa_t = dpa_ref[sl, :]
            gs += _colsum(dpa_t * mixed)
            dmx = (dpa_t * ps_ref[...]).astype(BF16)
            dmx_s[sl, :] = dmx
            dpo = lax.dot_general(dmx, w, (((1,), (1,)), ((), ())), preferred_element_type=F32)
            dp_s[sl, :] = dpo
            count = jnp.minimum(row + (r * T + 1), win).astype(F32)
            dpc_s[sl, :] = (dpo / count).astype(BF16)
        gs_ref[...] = gs
        gw = lax.dot_general(pooled_ref[...], dmx_s[...], (((0,), (0,)), ((), ())), preferred_element_type=F32)
        for j in range(N_CHIPS):
            gw_ref[j, 0] = gw[j * rows:(j + 1) * rows, :].astype(BF16)
        j_i = lax.broadcasted_iota(jnp.int32, (T, T), 0)
        t_i = lax.broadcasted_iota(jnp.int32, (T, T), 1)
        b_cur = ((t_i >= j_i) & (t_i < j_i + win)).astype(BF16)
        b_next = (t_i + T < j_i + win).astype(BF16)
        for r in range(n_t):
            sl = slice(r * T, (r + 1) * T)
            acc = jnp.dot(b_cur, dpc_s[sl, :], preferred_element_type=F32)
            if r + 1 < n_t:
                acc += jnp.dot(b_next, dpc_s[(r + 1) * T:(r + 2) * T, :], preferred_element_type=F32)
            du_ref[sl, :] = (acc - dp_s[sl, :]).astype(BF16)

    col = pl.BlockSpec((S, cg), lambda g: (0, g))
    return _pcall(
        body, name="pool_bwd", grid=(N_GROUPS,),
        in_specs=[col, col] + _pool_w_specs(rows, cg) + [pl.BlockSpec((1, cg), lambda g: (0, g))],
        out_specs=[col, pl.BlockSpec((N_CHIPS, 1, rows, cg), lambda g: (0, g, 0, 0)),
                   pl.BlockSpec((1, cg), lambda g: (0, g))],
        out_shape=[jax.ShapeDtypeStruct((S, PW), BF16),
                   jax.ShapeDtypeStruct((N_CHIPS, N_GROUPS, rows, cg), BF16),
                   jax.ShapeDtypeStruct((1, PW), F32)],
        scratch_shapes=[pltpu.VMEM((S, cg), F32), pltpu.VMEM((S, cg), BF16), pltpu.VMEM((S, cg), BF16)],
        compiler_params=_params(("parallel",)),
    )(dpa, pooled, wp_full, wp_full, wp_full, wp_full, pool_scale)


_NT = (((1,), (1,)), ((), ()))
_TN = (((0,), (0,)), ((), ()))


def _split_dot(v, tri):
    hi = v.astype(BF16)
    lo = (v - hi.astype(F32)).astype(BF16)
    return jnp.dot(hi, tri, preferred_element_type=F32) + jnp.dot(lo, tri, preferred_element_type=F32)


def _sb_scores(q_i, k_j, tri_l, masked):
    tq, tk = q_i.shape[0], k_j.shape[0]
    s = lax.dot_general(q_i, k_j, _NT, preferred_element_type=F32) * (1.0 / math.sqrt(HEAD_DIM))
    lp = jnp.log(1.0 + jnp.exp(-jnp.abs(s)))
    l = -jnp.maximum(s, 0.0) - lp
    lb = l + s
    mask = None
    if masked:
        mask = lax.broadcasted_iota(jnp.int32, (tq, tk), 0) > lax.broadcasted_iota(jnp.int32, (tq, tk), 1)
        l = jnp.where(mask, l, 0.0)
    return l, lb, lb + _split_dot(l, tri_l), mask


def _sb_weights(t, carry_l, mask):
    a = jnp.exp(t + carry_l)
    return a if mask is None else jnp.where(mask, a, 0.0)


def _rowsum(v):
    return jnp.sum(v, axis=1, keepdims=True)


def _qk_norm(x_ref, w_ref):
    xv = x_ref[...].astype(F32)
    r = lax.rsqrt(jnp.mean(xv * xv, axis=-1, keepdims=True) + EPS)
    return xv * r, r


def _attn_fwd(proj, q_norm_w, k_norm_w, S, H, q_off):
    t = _tile(S, ATT_T)
    n_q = S // t

    def body(q_ref, k_ref, v_ref, qw_ref, kw_ref, att_ref, attf_ref, qn_s, kn_s):
        qh, _ = _qk_norm(q_ref, qw_ref)
        qn_s[...] = (qh * qw_ref[...]).astype(BF16)
        kh, _ = _qk_norm(k_ref, kw_ref)
        kn_s[...] = (kh * kw_ref[...]).astype(BF16)
        tri_l = (lax.broadcasted_iota(jnp.int32, (t, t), 0) > lax.broadcasted_iota(jnp.int32, (t, t), 1)).astype(BF16)

        def rows(j):
            return pl.ds(pl.multiple_of(j * t, t), t)

        def q_step(i, _):
            q_i = qn_s[rows(i), :]

            def av(a, j):
                return jnp.dot(a.astype(BF16), v_ref[rows(j), :], preferred_element_type=F32)

            l, _, tt, mask = _sb_scores(q_i, kn_s[rows(i), :], tri_l, True)
            acc = av(_sb_weights(tt, 0.0, mask), i)
            carry = _rowsum(l)

            def single(_, c):
                carry, acc = c
                l, _, tt, _ = _sb_scores(q_i, kn_s[rows(i - 1), :], tri_l, False)
                return carry + _rowsum(l), acc + av(_sb_weights(tt, carry, None), i - 1)

            carry, acc = lax.fori_loop(0, i % 2, single, (carry, acc))
            top = i - 1 - i % 2

            def pair(p, c):
                carry, acc = c
                j0 = top - 2 * p
                l0, _, t0, _ = _sb_scores(q_i, kn_s[rows(j0), :], tri_l, False)
                l1, _, t1, _ = _sb_scores(q_i, kn_s[rows(j0 - 1), :], tri_l, False)
                mid = carry + _rowsum(l0)
                acc = acc + av(_sb_weights(t0, carry, None), j0) + av(_sb_weights(t1, mid, None), j0 - 1)
                return mid + _rowsum(l1), acc

            _, acc = lax.fori_loop(0, i // 2, pair, (carry, acc))
            att_ref[rows(i), :] = acc.astype(BF16)
            attf_ref[rows(i), :] = acc
            return 0

        lax.fori_loop(0, n_q, q_step, 0)

    def col(off):
        return pl.BlockSpec((S, HEAD_DIM), lambda h, off=off: (0, off + h))

    wspec = pl.BlockSpec((1, HEAD_DIM), lambda h: (0, 0))
    return _pcall(
        body, name="attn_fwd", grid=(H,),
        in_specs=[col(q_off), col(q_off + H), col(q_off + 2 * H), wspec, wspec],
        out_specs=[col(0), col(0)],
        out_shape=[jax.ShapeDtypeStruct((S, H * HEAD_DIM), BF16), jax.ShapeDtypeStruct((S, H * HEAD_DIM), F32)],
        scratch_shapes=[pltpu.VMEM((S, HEAD_DIM), BF16), pltpu.VMEM((S, HEAD_DIM), BF16)],
        compiler_params=_params(("parallel",)),
    )(proj, proj, proj, q_norm_w, k_norm_w)


def _attn_bwd(proj, datt, attf, q_norm_w, k_norm_w, S, H, q_off):
    t = _tile(S, ATT_T)
    n_q = S // t
    scale = 1.0 / math.sqrt(HEAD_DIM)

    def body(q_ref, k_ref, v_ref, do_ref, o_ref, qw_ref, kw_ref, dq_ref, dk_ref, dv_ref, gq_ref, gk_ref,
             qn_s, kn_s, dk_s, dv_s, gq_s):
        qw, kw = qw_ref[...], kw_ref[...]
        qh, _ = _qk_norm(q_ref, qw_ref)
        qn_s[...] = (qh * qw).astype(BF16)
        kh, _ = _qk_norm(k_ref, kw_ref)
        kn_s[...] = (kh * kw).astype(BF16)
        dk_s[...] = jnp.zeros_like(dk_s)
        dv_s[...] = jnp.zeros_like(dv_s)
        gq_s[...] = jnp.zeros_like(gq_s)
        r_i = lax.broadcasted_iota(jnp.int32, (t, t), 0)
        c_i = lax.broadcasted_iota(jnp.int32, (t, t), 1)
        tri_l = (r_i > c_i).astype(BF16)
        tri_e = (r_i >= c_i).astype(BF16)

        def rows(j):
            return pl.ds(pl.multiple_of(j * t, t), t)

        def q_step(i, _):
            q_i = qn_s[rows(i), :]
            do_i = do_ref[rows(i), :]
            d_i = _rowsum(do_i.astype(F32) * o_ref[rows(i), :])

            def scores(j, masked):
                k_j = kn_s[rows(j), :]
                l, lb, tt, mask = _sb_scores(q_i, k_j, tri_l, masked)
                da = lax.dot_general(do_i, v_ref[rows(j), :], _NT, preferred_element_type=F32)
                return k_j, l, lb, tt, mask, da

            def grads(j, sc, carry_l, carry_e, dq_acc):
                k_j, l, lb, tt, mask, da = sc
                a_bf = _sb_weights(tt, carry_l, mask).astype(BF16)
                e = da * a_bf.astype(F32)
                p = d_i - (_split_dot(e, tri_e) + carry_e)
                sig = jnp.exp(lb)
                dz = e * (1.0 - sig) - p * sig
                if mask is not None:
                    dz = jnp.where(mask, dz, 0.0)
                dz = (dz * scale).astype(BF16)
                dk_s[rows(j), :] += lax.dot_general(dz, q_i, _TN, preferred_element_type=F32)
                dv_s[rows(j), :] += lax.dot_general(a_bf, do_i, _TN, preferred_element_type=F32)
                return (carry_l + _rowsum(l), carry_e + _rowsum(e),
                        dq_acc + jnp.dot(dz, k_j, preferred_element_type=F32))

            c = grads(i, scores(i, True), 0.0, 0.0, jnp.zeros((t, HEAD_DIM), F32))
            c = lax.fori_loop(0, i % 2, lambda _, c: grads(i - 1, scores(i - 1, False), *c), c)
            top = i - 1 - i % 2

            def pair(p, c):
                j0 = top - 2 * p
                s0, s1 = scores(j0, False), scores(j0 - 1, False)
                return grads(j0 - 1, s1, *grads(j0, s0, *c))

            _, _, dqn = lax.fori_loop(0, i // 2, pair, c)
            qv = q_ref[rows(i), :].astype(F32)
            r = lax.rsqrt(jnp.mean(qv * qv, axis=-1, keepdims=True) + EPS)
            xh = qv * r
            gq_s[...] += _colsum(dqn * xh)
            dxh = dqn * qw
            dq_ref[rows(i), :] = (r * (dxh - xh * jnp.mean(dxh * xh, axis=-1, keepdims=True))).astype(BF16)
            return 0

        lax.fori_loop(0, n_q, q_step, 0)
        gq_ref[0] = gq_s[...]
        kh, rk = _qk_norm(k_ref, kw_ref)
        dkn = dk_s[...]
        gk_ref[0] = _colsum(dkn * kh)
        dxh = dkn * kw
        dk_ref[...] = (rk * (dxh - kh * jnp.mean(dxh * kh, axis=-1, keepdims=True))).astype(BF16)
        dv_ref[...] = dv_s[...].astype(BF16)

    def col(off):
        return pl.BlockSpec((S, HEAD_DIM), lambda h, off=off: (0, off + h))

    wspec = pl.BlockSpec((1, HEAD_DIM), lambda h: (0, 0))
    gspec = pl.BlockSpec((1, 1, HEAD_DIM), lambda h: (h, 0, 0))
    act = jax.ShapeDtypeStruct((S, H * HEAD_DIM), BF16)
    gsh = jax.ShapeDtypeStruct((H, 1, HEAD_DIM), F32)
    return _pcall(
        body, name="attn_bwd", grid=(H,),
        in_specs=[col(q_off), col(q_off + H), col(q_off + 2 * H), col(0), col(0), wspec, wspec],
        out_specs=[col(0), col(0), col(0), gspec, gspec],
        out_shape=[act, act, act, gsh, gsh],
        scratch_shapes=[pltpu.VMEM((S, HEAD_DIM), BF16), pltpu.VMEM((S, HEAD_DIM), BF16),
                        pltpu.VMEM((S, HEAD_DIM), F32), pltpu.VMEM((S, HEAD_DIM), F32),
                        pltpu.VMEM((1, HEAD_DIM), F32)],
        compiler_params=_params(("parallel",)),
    )(proj, proj, proj, datt, attf, q_norm_w, k_norm_w)


def _place():
    x, y, c = lax.axis_index("x"), lax.axis_index("y"), lax.axis_index("c")
    chips = [(1 - x, y), (x, 1 - y), (1 - x, 1 - y)]
    return x, y, c, chips


def _dev_allgather(name, v):
    m_per, n = v.shape

    def body(x_ref, out_ref, send_sems, recv_sems, local_sem):
        x, y, c, chips = _place()
        me, sibling = (x, y, c), (x, y, 1 - c)

        def rows(px, py, pc):
            return out_ref.at[pl.ds((4 * px + 2 * py + pc) * m_per, m_per), :]

        def copy(k, block, to, src=None):
            return pltpu.make_async_remote_copy(
                src_ref=rows(*block) if src is None else src, dst_ref=rows(*block),
                send_sem=send_sems.at[k], recv_sem=recv_sems.at[k], device_id=to, device_id_type=MESH)

        mine = pltpu.make_async_copy(x_ref, rows(*me), local_sem)
        mine.start()
        first = [copy(0, me, sibling, src=x_ref)]
        first += [copy(1 + j, me, (*chip, c), src=x_ref) for j, chip in enumerate(chips)]
        for cp in first:
            cp.start()
        passed = [copy(4 + j, (*chip, c), sibling) for j, chip in enumerate(chips)]
        for j, chip in enumerate(chips):
            copy(1 + j, (*chip, c), me).wait_recv()
            passed[j].start()
        copy(0, sibling, me).wait_recv()
        for j, chip in enumerate(chips):
            copy(4 + j, (*chip, 1 - c), me).wait_recv()
        for cp in first + passed:
            cp.wait_send()
        mine.wait()

    return _pcall(
        body, name=name, out_shape=jax.ShapeDtypeStruct((N_DEV * m_per, n), v.dtype),
        in_specs=[pl.BlockSpec(memory_space=pltpu.VMEM)], out_specs=pl.BlockSpec(memory_space=pltpu.VMEM),
        scratch_shapes=[pltpu.SemaphoreType.DMA((7,)), pltpu.SemaphoreType.DMA((7,)), pltpu.SemaphoreType.DMA],
        compiler_params=pltpu.CompilerParams(vmem_limit_bytes=VMEM_LIMIT_V7X),
    )(v)


class _W:
    def __init__(self, name, kind, R, C):
        self.name, self.kind, self.R, self.C = name, kind, R, C

    @property
    def shard_shape(self):
        return (self.R, self.C // N_CHIPS) if self.kind == "col" else (self.R // N_CHIPS, self.C)

    @property
    def half_rows(self):
        return self.shard_shape[0] // 2

    def shard_half(self, ref, half):
        return ref.at[pl.ds(half * self.half_rows, self.half_rows), :]

    def region(self, full_ref, chip, half):
        hr = self.half_rows
        if self.kind == "col":
            cw = self.C // N_CHIPS
            return full_ref.at[pl.ds(half * hr, hr), pl.ds(chip * cw, cw)]
        return full_ref.at[pl.ds(chip * (2 * hr) + half * hr, hr), :]

    def region_both(self, full_ref, chip):
        hr = self.half_rows
        if self.kind == "col":
            cw = self.C // N_CHIPS
            return full_ref.at[:, pl.ds(chip * cw, cw)]
        return full_ref.at[pl.ds(chip * (2 * hr), 2 * hr), :]


def _weights_allgather(ws, shards):
    n_w = len(ws)

    def body(*refs):
        sh, full = refs[:n_w], refs[n_w:2 * n_w]
        send_sems, recv_sems, local_sems = refs[2 * n_w:]
        x, y, c, chips = _place()
        my_chip = 2 * x + y

        def direct(i, k, chip, recv=False):
            w = ws[i]
            src_chip = (2 * chip[0] + chip[1]) if recv else my_chip
            dst = w.region(full[i], src_chip, c)
            return pltpu.make_async_remote_copy(
                src_ref=dst if recv else w.shard_half(sh[i], c), dst_ref=dst,
                send_sem=send_sems.at[6 * i + k], recv_sem=recv_sems.at[6 * i + k],
                device_id=(*chip, c), device_id_type=MESH)

        def passed(i, k, chip, half):
            reg = ws[i].region(full[i], 2 * chip[0] + chip[1], half)
            return pltpu.make_async_remote_copy(
                src_ref=reg, dst_ref=reg, send_sem=send_sems.at[6 * i + 3 + k], recv_sem=recv_sems.at[6 * i + 3 + k],
                device_id=(x, y, 1 - c), device_id_type=MESH)

        started = []
        for i, w in enumerate(ws):
            for k, chip in enumerate(chips):
                cp = direct(i, k, chip)
                cp.start()
                started.append(cp)
        mine = [pltpu.make_async_copy(sh[i], w.region_both(full[i], my_chip), local_sems.at[i]) for i, w in enumerate(ws)]
        for cp in mine:
            cp.start()
        for i in range(n_w):
            for k, chip in enumerate(chips):
                direct(i, k, chip, recv=True).wait_recv()
                cp = passed(i, k, chip, c)
                cp.start()
                started.append(cp)
        for i in range(n_w):
            for k, chip in enumerate(chips):
                passed(i, k, chip, 1 - c).wait_recv()
        for cp in started:
            cp.wait_send()
        for cp in mine:
            cp.wait()

    return _pcall(
        body, name="weights_allgather",
        out_shape=[jax.ShapeDtypeStruct((w.R, w.C), BF16) for w in ws],
        in_specs=[ANY] * n_w, out_specs=[ANY] * n_w,
        scratch_shapes=[pltpu.SemaphoreType.DMA((6 * n_w,)), pltpu.SemaphoreType.DMA((6 * n_w,)),
                        pltpu.SemaphoreType.DMA((n_w,))],
    )(*shards)


def _half_view(w, g):
    return g if w.kind == "col" else g.reshape(N_CHIPS, w.R // N_CHIPS, w.C)


def _pair_exchange(ws, grads):
    n_w = len(ws)

    def half_all(w, ref, half):
        hr = w.half_rows
        if w.kind == "col":
            return ref.at[pl.ds(half * hr, hr), :]
        return ref.at[:, pl.ds(half * hr, hr), :]

    def body(*refs):
        g, got = refs[:n_w], refs[n_w:2 * n_w]
        send_sems, recv_sems = refs[2 * n_w:]
        x, y, c, _ = _place()
        copies = [pltpu.make_async_remote_copy(
            src_ref=half_all(w, g[i], 1 - c), dst_ref=got[i], send_sem=send_sems.at[i], recv_sem=recv_sems.at[i],
            device_id=(x, y, 1 - c), device_id_type=MESH) for i, w in enumerate(ws)]
        for cp in copies:
            cp.start()
        for cp in copies:
            cp.wait_recv()
        for cp in copies:
            cp.wait_send()

    def got_shape(w):
        hr = w.half_rows
        return (hr, w.C) if w.kind == "col" else (N_CHIPS, hr, w.C)

    return _pcall(
        body, name="grad_pair_exchange",
        out_shape=[jax.ShapeDtypeStruct(got_shape(w), BF16) for w in ws],
        in_specs=[ANY] * n_w, out_specs=[ANY] * n_w,
        scratch_shapes=[pltpu.SemaphoreType.DMA((n_w,)), pltpu.SemaphoreType.DMA((n_w,))],
    )(*[_half_view(w, g) for w, g in zip(ws, grads)])


def _pair_sum(w, g, got, c_arr):
    hr = w.half_rows
    if w.kind == "col":
        tr, tc = _tile(hr, 512), _tile(w.C, 2048)
        n_r = hr // tr
        grid = (n_r, w.C // tc)
        g_spec = pl.BlockSpec((tr, tc), lambda i, j, c: (c[0] * n_r + i, j))
        o_spec = pl.BlockSpec((tr, tc), lambda i, j, c: (i, j))
    else:
        tr = _tile(hr, 512)
        n_r = hr // tr
        grid = (N_CHIPS, n_r)
        g_spec = pl.BlockSpec((1, tr, w.C), lambda s, i, c: (s, c[0] * n_r + i, 0))
        o_spec = pl.BlockSpec((1, tr, w.C), lambda s, i, c: (s, i, 0))

    def body(c_ref, g_ref, got_ref, out_ref):
        out_ref[...] = (g_ref[...].astype(F32) + got_ref[...].astype(F32)).astype(BF16)

    return _pcall(
        body, name="grad_pair_sum_" + w.name, out_shape=jax.ShapeDtypeStruct(got.shape, BF16),
        grid_spec=pltpu.PrefetchScalarGridSpec(num_scalar_prefetch=1, grid=grid, in_specs=[g_spec, o_spec],
                                               out_specs=o_spec),
        compiler_params=_params(("parallel", "parallel")),
    )(c_arr, _half_view(w, g), got)


def _chip_exchange(ws, sums):
    n_w = len(ws)

    def piece(w, ref, chip):
        if w.kind == "col":
            cw = w.C // N_CHIPS
            return ref.at[:, pl.ds(chip * cw, cw)]
        return ref.at[chip]

    def body(*refs):
        p, q = refs[:n_w], refs[n_w:2 * n_w]
        send_sems, recv_sems, local_sems = refs[2 * n_w:]
        x, y, c, chips = _place()
        my_chip = 2 * x + y

        def copy(i, k, chip, recv=False):
            to_chip = 2 * chip[0] + chip[1]
            return pltpu.make_async_remote_copy(
                src_ref=piece(ws[i], p[i], to_chip), dst_ref=q[i].at[to_chip if recv else my_chip],
                send_sem=send_sems.at[3 * i + k], recv_sem=recv_sems.at[3 * i + k],
                device_id=(*chip, c), device_id_type=MESH)

        started = []
        for i in range(n_w):
            for k, chip in enumerate(chips):
                cp = copy(i, k, chip)
                cp.start()
                started.append(cp)
        mine = [pltpu.make_async_copy(piece(w, p[i], my_chip), q[i].at[my_chip], local_sems.at[i])
                for i, w in enumerate(ws)]
        for cp in mine:
            cp.start()
        for i in range(n_w):
            for k, chip in enumerate(chips):
                copy(i, k, chip, recv=True).wait_recv()
        for cp in started:
            cp.wait_send()
        for cp in mine:
            cp.wait()

    def q_shape(w):
        return (N_CHIPS, w.half_rows, w.shard_shape[1])

    return _pcall(
        body, name="grad_chip_exchange",
        out_shape=[jax.ShapeDtypeStruct(q_shape(w), BF16) for w in ws],
        in_specs=[ANY] * n_w, out_specs=[ANY] * n_w,
        scratch_shapes=[pltpu.SemaphoreType.DMA((3 * n_w,)), pltpu.SemaphoreType.DMA((3 * n_w,)),
                        pltpu.SemaphoreType.DMA((n_w,))],
    )(*sums)


def _chip_sum(w, q, c_arr):
    hr, cols = w.half_rows, w.shard_shape[1]
    tr, tc = _tile(hr, 512), _tile(cols, 2048)
    n_r = hr // tr

    def body(c_ref, q0, q1, q2, q3, out_ref):
        out_ref[...] = ((q0[0].astype(F32) + q1[0].astype(F32)) + q2[0].astype(F32)) + q3[0].astype(F32)

    q_specs = [pl.BlockSpec((1, tr, tc), lambda i, j, c, s=s: (s, i, j)) for s in range(N_CHIPS)]
    return _pcall(
        body, name="grad_chip_sum_" + w.name, out_shape=jax.ShapeDtypeStruct(w.shard_shape, F32),
        grid_spec=pltpu.PrefetchScalarGridSpec(
            num_scalar_prefetch=1, grid=(n_r, cols // tc), in_specs=q_specs,
            out_specs=pl.BlockSpec((tr, tc), lambda i, j, c: (c[0] * n_r + i, j))),
        compiler_params=_params(("parallel", "parallel")),
    )(c_arr, q, q, q, q)


def _sibling_fill(ws, grads):
    n_w = len(ws)

    def body(*refs):
        g = refs[n_w:2 * n_w]
        send_sems, recv_sems = refs[2 * n_w:]
        x, y, c, _ = _place()
        copies = [pltpu.make_async_remote_copy(
            src_ref=w.shard_half(g[i], c), dst_ref=w.shard_half(g[i], c), send_sem=send_sems.at[i],
            recv_sem=recv_sems.at[i], device_id=(x, y, 1 - c), device_id_type=MESH) for i, w in enumerate(ws)]
        for cp in copies:
            cp.start()
        for i, w in enumerate(ws):
            pltpu.make_async_remote_copy(
                src_ref=w.shard_half(g[i], 1 - c), dst_ref=w.shard_half(g[i], 1 - c), send_sem=send_sems.at[i],
                recv_sem=recv_sems.at[i], device_id=(x, y, 1 - c), device_id_type=MESH).wait_recv()
        for cp in copies:
            cp.wait_send()

    return _pcall(
        body, name="grad_sibling_fill",
        out_shape=[jax.ShapeDtypeStruct(w.shard_shape, F32) for w in ws],
        in_specs=[ANY] * n_w, out_specs=[ANY] * n_w, input_output_aliases={i: i for i in range(n_w)},
        scratch_shapes=[pltpu.SemaphoreType.DMA((n_w,)), pltpu.SemaphoreType.DMA((n_w,))],
    )(*grads)


def _adamw_math(w, g, m, v):
    m = ADAM_B1 * m + (1.0 - ADAM_B1) * g
    v = ADAM_B2 * v + (1.0 - ADAM_B2) * (g * g)
    m_hat = m / (1.0 - ADAM_B1 ** ADAM_STEP)
    v_hat = v / (1.0 - ADAM_B2 ** ADAM_STEP)
    delta = -ADAM_LR * (m_hat / (jnp.sqrt(v_hat) + ADAM_EPS) + ADAM_WD * w)
    return delta, m, v


def _adamw(name, w, g, m, v):
    R, C = w.shape
    tr, tc = _tile(R, 256), _tile(C, 2048)

    def body(w_ref, g_ref, m_ref, v_ref, d_out, m_out, v_out):
        d_out[...], m_out[...], v_out[...] = _adamw_math(w_ref[...], g_ref[...], m_ref[...], v_ref[...])

    spec = pl.BlockSpec((tr, tc), lambda i, j: (i, j))
    sh = jax.ShapeDtypeStruct((R, C), F32)
    return _pcall(body, name=name, grid=(R // tr, C // tc), in_specs=[spec] * 4, out_specs=[spec] * 3,
                  out_shape=[sh, sh, sh], compiler_params=_params(("parallel", "parallel")))(w, g, m, v)


def _ada_update(sct, dmod_sh, w, m, v):
    R, C = w.shape
    tr, tc = _tile(R, 256), _tile(C, 1024)

    def body(s_ref, d_ref, w_ref, m_ref, v_ref, g_out, d_out, m_out, v_out):
        s, d = s_ref[...], d_ref[...]
        g = s[:, 0:1] * d[0:1, :]
        for b in range(1, N_DEV):
            g += s[:, b:b + 1] * d[b:b + 1, :]
        g_out[...] = g
        d_out[...], m_out[...], v_out[...] = _adamw_math(w_ref[...], g, m_ref[...], v_ref[...])

    spec = pl.BlockSpec((tr, tc), lambda i, j: (i, j))
    sh = jax.ShapeDtypeStruct((R, C), F32)
    return _pcall(
        body, name="ada_update", grid=(R // tr, C // tc),
        in_specs=[pl.BlockSpec((tr, N_DEV), lambda i, j: (i, 0)), pl.BlockSpec((N_DEV, tc), lambda i, j: (0, j)),
                  spec, spec, spec],
        out_specs=[spec] * 4, out_shape=[sh] * 4, compiler_params=_params(("parallel", "parallel")),
    )(sct, dmod_sh, w, m, v)


def _cast_bf16(name, w):
    R, C = w.shape
    tr, tc = _tile(R, 512), _tile(C, 2048)

    def body(w_ref, o_ref):
        o_ref[...] = w_ref[...].astype(BF16)

    spec = pl.BlockSpec((tr, tc), lambda i, j: (i, j))
    return _pcall(body, name=name, grid=(R // tr, C // tc), in_specs=[spec], out_specs=spec,
                  out_shape=jax.ShapeDtypeStruct((R, C), BF16), compiler_params=_params(("parallel", "parallel")))(w)


def _silu_rows(c_row):
    D = c_row.shape[1]

    def body(c_ref, o_ref):
        cv = c_ref[...]
        o_ref[...] = cv * jax.nn.sigmoid(cv)

    return _pcall(body, name="silu_c", out_shape=jax.ShapeDtypeStruct((1, D), F32))(c_row)


def _pack_partials(parts, widths, total):
    n = len(widths)

    def body(*refs):
        loss_p, out_ref, loss_ref = refs[n], refs[n + 1], refs[n + 2]
        off = 0
        for ref, wd in zip(refs[:n], widths):
            out_ref[:, off:off + wd] = jnp.sum(ref[...], axis=0)
            off += wd
        if off < total:
            out_ref[:, off:total] = jnp.zeros((1, total - off), F32)
        loss_ref[...] = jnp.sum(jnp.sum(loss_p[...], axis=0), axis=1, keepdims=True)

    return _pcall(body, name="pack_partials",
                  out_shape=[jax.ShapeDtypeStruct((1, total), F32), jax.ShapeDtypeStruct((1, 1), F32)])(*parts)


def _small_update(gathered, offsets, params):
    n_p = len(params)

    def body(*refs):
        g_ref = refs[0]
        prm = refs[1:1 + 3 * n_p]
        outs = refs[1 + 3 * n_p:]
        for i, (off, wd) in enumerate(offsets):
            blk = g_ref[:, off:off + wd]
            g = blk[0:1, :]
            for b in range(1, N_DEV):
                g = g + blk[b:b + 1, :]
            w, m, v = prm[3 * i][...], prm[3 * i + 1][...], prm[3 * i + 2][...]
            outs[4 * i][...] = g
            outs[4 * i + 1][...], outs[4 * i + 2][...], outs[4 * i + 3][...] = _adamw_math(w, g, m, v)

    flat = [a for t in params for a in t]
    out_shape = [jax.ShapeDtypeStruct(t[0].shape, F32) for t in params for _ in range(4)]
    return _pcall(body, name="small_update", out_shape=out_shape)(gathered, *flat)


def kernel(x, c, w_ada, b_ada, norm1_w, w_in, q_norm_w, k_norm_w, w_pool, pool_scale, w_a_up, w_b_up, w_o, norm2_w, w_ff1, w_ff2, loss_target, m_w_ada, m_b_ada, m_norm1_w, m_w_in, m_q_norm_w, m_k_norm_w, m_w_pool, m_pool_scale, m_w_a_up, m_w_b_up, m_w_o, m_norm2_w, m_w_ff1, m_w_ff2, v_w_ada, v_b_ada, v_norm1_w, v_w_in, v_q_norm_w, v_k_norm_w, v_w_pool, v_pool_scale, v_w_a_up, v_w_b_up, v_w_o, v_norm2_w, v_w_ff1, v_w_ff2):
    _, S, D = x.shape
    PW = D // 2
    H = PW // HEAD_DIM
    cg = PW // N_GROUPS
    IN = w_in.shape[2] * N_CHIPS
    FF = w_ff1.shape[2] * N_CHIPS
    A_COLS = w_ada.shape[2]
    xi, yi, ci = lax.axis_index("x"), lax.axis_index("y"), lax.axis_index("c")
    chip = 2 * xi + yi
    dev = 2 * chip + ci
    c_arr = jnp.reshape(ci, (1,)).astype(jnp.int32)
    x2, tgt = x[0], loss_target[0]

    ws = [_W("w_in", "col", D, IN), _W("w_pool", "row", PW, cg), _W("w_a_up", "col", PW, D),
          _W("w_b_up", "col", PW, D), _W("w_o", "row", D, D), _W("w_ff1", "col", D, FF), _W("w_ff2", "row", FF, D)]
    w32 = [w_in[0], w_pool[0].reshape(cg, cg), w_a_up[0], w_b_up[0], w_o[0], w_ff1[0], w_ff2[0]]
    m32 = [m_w_in[0], m_w_pool[0].reshape(cg, cg), m_w_a_up[0], m_w_b_up[0], m_w_o[0], m_w_ff1[0], m_w_ff2[0]]
    v32 = [v_w_in[0], v_w_pool[0].reshape(cg, cg), v_w_a_up[0], v_w_b_up[0], v_w_o[0], v_w_ff1[0], v_w_ff2[0]]

    shards16 = [_cast_bf16("cast_" + w.name, a) for w, a in zip(ws, w32)]
    win_f, wpool_f, wa_f, wb_f, wo_f, wff1_f, wff2_f = _weights_allgather(ws, shards16)

    sc_row = _silu_rows(c)
    sc_all = _dev_allgather("gather_silu_c", sc_row.reshape(8, D // 8)).reshape(N_DEV, D)
    sc16 = jnp.concatenate([sc_all, jnp.zeros_like(sc_all)], axis=0)
    b_cols = lax.dynamic_slice(b_ada, (0, chip * A_COLS), (1, A_COLS))
    (mod_cols,) = _mm("mod_cols", [(sc16, w_ada[0])], M=2 * N_DEV, N=A_COLS, K=D, tm=16, tn=1024, tk=1024,
                      a_pro=lambda a: a.astype(BF16), b_pro=lambda b: b.astype(BF16),
                      extras=[(b_cols, "row", 0)], outs=[_tile_out(F32)], epi=lambda accs, ex: [accs[0] + ex[0]])
    mod_all = _dev_allgather("gather_mod", mod_cols[:N_DEV]).reshape(N_CHIPS, 2, N_DEV, A_COLS)
    mod_row = lax.dynamic_index_in_dim(mod_all[:, 0], dev, axis=1, keepdims=False).reshape(1, N_CHIPS * A_COLS)
    shift1, scale1, gate1, shift2, scale2, gate2 = [mod_row[:, i * D:(i + 1) * D] for i in range(6)]

    h = _norm_mod("norm1_mod", x2, norm1_w, scale1, shift1)
    (proj,) = _mm("in_proj", [(h, win_f)], M=S, N=IN, K=D, outs=[_tile_out(BF16)], epi=lambda accs, ex: [accs[0]])
    pooled, pa = _pool_fwd(proj, wpool_f, pool_scale, S, PW)
    att, attf = _attn_fwd(proj, q_norm_w, k_norm_w, S, H, PW // HEAD_DIM)

    def merge_epi(accs, ex):
        sa, sb = jax.nn.sigmoid(ex[0].astype(F32)), jax.nn.sigmoid(ex[1].astype(F32))
        return [sa * accs[0] + sb * accs[1], accs[0], accs[1]]

    merged, ya, yb = _mm("branch_up_merge", [(pa, wa_f), (att, wb_f)], M=S, N=D, K=PW,
                         extras=[(proj, "tile", 4 * PW), (proj, "tile", 4 * PW + D)],
                         outs=[_tile_out(BF16)] * 3, epi=merge_epi)
    x1, o = _mm("out_proj", [(merged, wo_f)], M=S, N=D, K=D, extras=[(x2, "tile", 0), (gate1, "row", 0)],
                outs=[_tile_out(F32), _tile_out(BF16)], epi=lambda accs, ex: [ex[0] + ex[1] * accs[0], accs[0]])
    h2 = _norm_mod("norm2_mod", x1, norm2_w, scale2, shift2)
    (rl,) = _mm("ff1", [(h2, wff1_f)], M=S, N=FF, K=D, outs=[_tile_out(BF16)],
                epi=lambda accs, ex: [jnp.maximum(accs[0], 0.0)])

    def square(a):
        af = a.astype(F32)
        return (af * af).astype(BF16)

    def loss_epi(accs, ex):
        x1_t, tgt_t, g2 = ex
        f = accs[0]
        diff = (x1_t + g2 * f) - tgt_t
        dy = diff * (1.0 / D)
        return [dy, dy * g2, _colsum(dy * f), _colsum(diff * diff)]

    dy, df, dgate2_p, loss_p = _mm("ff2_loss", [(rl, wff2_f)], M=S, N=D, K=FF, a_pro=square,
                                   extras=[(x1, "tile", 0), (tgt, "tile", 0), (gate2, "row", 0)],
                                   outs=[_tile_out(F32), _tile_out(BF16), _COLSUM, _COLSUM], epi=loss_epi)

    (g_ff2,) = _mm("grad_w_ff2", [(rl, df)], M=FF, N=D, K=S, ta=True, tm=1024, tn=1024, tk=512, a_pro=square,
                   outs=[_tile_out(BF16)], epi=lambda accs, ex: [accs[0]])
    (dz1,) = _mm("d_ff_hidden", [(df, wff2_f)], M=S, N=FF, K=D, tb=True, extras=[(rl, "tile", 0)],
                 outs=[_tile_out(BF16)], epi=lambda accs, ex: [accs[0] * (2.0 * ex[0].astype(F32))])
    (g_ff1,) = _mm("grad_w_ff1", [(h2, dz1)], M=D, N=FF, K=S, ta=True, tm=1024, tn=1024, tk=512,
                   outs=[_tile_out(BF16)], epi=lambda accs, ex: [accs[0]])
    (dh2,) = _mm("d_h2", [(dz1, wff1_f)], M=S, N=D, K=FF, tb=True, outs=[_tile_out(F32)], epi=lambda accs, ex: [accs[0]])
    dx1, dshift2_p, dscale2_p, gn2_p, do, dgate1_p = _norm_mod_bwd("norm2_bwd", dh2, x1, dy, norm2_w, scale2,
                                                                   gate_o=(o, gate1))

    (g_wo,) = _mm("grad_w_o", [(merged, do)], M=D, N=D, K=S, ta=True, tm=1024, tn=1024, tk=512,
                  outs=[_tile_out(BF16)], epi=lambda accs, ex: [accs[0]])

    def gate_epi(accs, ex):
        dm = accs[0]
        sa, sb = jax.nn.sigmoid(ex[0].astype(F32)), jax.nn.sigmoid(ex[1].astype(F32))
        ya_t, yb_t = ex[2].astype(F32), ex[3].astype(F32)
        return [dm * sa, dm * sb, dm * ya_t * (sa * (1.0 - sa)), dm * yb_t * (sb * (1.0 - sb))]

    dya, dyb, dga, dgb = _mm("d_merged", [(do, wo_f)], M=S, N=D, K=D, tb=True,
                             extras=[(proj, "tile", 4 * PW), (proj, "tile", 4 * PW + D), (ya, "tile", 0), (yb, "tile", 0)],
                             outs=[_tile_out(BF16)] * 4, epi=gate_epi)
    (g_wa,) = _mm("grad_w_a_up", [(pa, dya)], M=PW, N=D, K=S, ta=True, tm=1024, tn=1024, tk=512,
                  outs=[_tile_out(BF16)], epi=lambda accs, ex: [accs[0]])
    (g_wb,) = _mm("grad_w_b_up", [(att, dyb)], M=PW, N=D, K=S, ta=True, tm=1024, tn=1024, tk=512,
                  outs=[_tile_out(BF16)], epi=lambda accs, ex: [accs[0]])
    (dpa,) = _mm("d_pool_out", [(dya, wa_f)], M=S, N=PW, K=D, tb=True, outs=[_tile_out(F32)], epi=lambda accs, ex: [accs[0]])
    (datt,) = _mm("d_att", [(dyb, wb_f)], M=S, N=PW, K=D, tb=True, outs=[_tile_out(BF16)], epi=lambda accs, ex: [accs[0]])
    du, g_wpool4, gscale_p = _pool_bwd(dpa, pooled, wpool_f, pool_scale, S, PW)
    dq, dk, dv, gq_p, gk_p = _attn_bwd(proj, datt, attf, q_norm_w, k_norm_w, S, H, PW // HEAD_DIM)
    dproj = jnp.concatenate([du, dq, dk, dv, dga, dgb], axis=1)
    (g_win,) = _mm("grad_w_in", [(h, dproj)], M=D, N=IN, K=S, ta=True, tm=1024, tn=1024, tk=512,
                   outs=[_tile_out(BF16)], epi=lambda accs, ex: [accs[0]])
    (dh,) = _mm("d_h", [(dproj, win_f)], M=S, N=D, K=IN, tb=True, outs=[_tile_out(F32)], epi=lambda accs, ex: [accs[0]])
    grad_x, dshift1_p, dscale1_p, gn1_p = _norm_mod_bwd("norm1_bwd", dh, x2, dx1, norm1_w, scale1)

    partial = [g_win, g_wpool4.reshape(PW, cg), g_wa, g_wb, g_wo, g_ff1, g_ff2]
    got = _pair_exchange(ws, partial)
    sums = [_pair_sum(w, g, r, c_arr) for w, g, r in zip(ws, partial, got)]
    from_chips = _chip_exchange(ws, sums)
    halves = [_chip_sum(w, q, c_arr) for w, q in zip(ws, from_chips)]
    grads = _sibling_fill(ws, halves)
    upd = [_adamw("adamw_" + w.name, a, g, m, v) for w, a, g, m, v in zip(ws, w32, grads, m32, v32)]

    parts = [dshift1_p, dscale1_p, dgate1_p, dshift2_p, dscale2_p, dgate2_p, gn1_p, gn2_p,
             gscale_p.reshape(1, 1, PW), gq_p, gk_p]
    widths = [D] * 8 + [PW, HEAD_DIM, HEAD_DIM]
    used = sum(widths)
    P = -(-used // 1024) * 1024
    packed, loss_part = _pack_partials(parts + [loss_p], widths, P)
    gathered = _dev_allgather("gather_vector_grads", packed.reshape(8, P // 8)).reshape(N_DEV, P)
    small = [(b_ada, m_b_ada, v_b_ada), (norm1_w, m_norm1_w, v_norm1_w), (norm2_w, m_norm2_w, v_norm2_w),
             (pool_scale, m_pool_scale, v_pool_scale), (q_norm_w, m_q_norm_w, v_q_norm_w),
             (k_norm_w, m_k_norm_w, v_k_norm_w)]
    offsets = [(0, 6 * D), (6 * D, D), (7 * D, D), (8 * D, PW), (8 * D + PW, HEAD_DIM), (8 * D + PW + HEAD_DIM, HEAD_DIM)]
    su = _small_update(gathered, offsets, small)
    (g_b, d_b, nm_b, nv_b, g_n1, d_n1, nm_n1, nv_n1, g_n2, d_n2, nm_n2, nv_n2, g_ps, d_ps, nm_ps, nv_ps,
     g_qn, d_qn, nm_qn, nv_qn, g_kn, d_kn, nm_kn, nv_kn) = su
    dmod_sh = lax.dynamic_slice(gathered, (0, chip * A_COLS), (N_DEV, A_COLS))
    g_ada, d_ada, nm_ada, nv_ada = _ada_update(sc_all.T, dmod_sh, w_ada[0], m_w_ada[0], v_w_ada[0])

    loss = 0.5 / D * lax.psum(loss_part[0, 0], ("x", "y", "c"))

    def up(a):
        return a[None]

    def pool4(a):
        return a.reshape(1, N_GROUPS, cg // N_CHIPS, cg)

    (d_win, nm_win, nv_win), (d_wp, nm_wp, nv_wp), (d_wa, nm_wa, nv_wa), (d_wb, nm_wb, nv_wb), \
        (d_wo, nm_wo, nv_wo), (d_f1, nm_f1, nv_f1), (d_f2, nm_f2, nv_f2) = upd
    gr_win, gr_wp, gr_wa, gr_wb, gr_wo, gr_f1, gr_f2 = grads
    return (
        loss, grad_x[None],
        up(g_ada), g_b, g_n1, up(gr_win), g_qn, g_kn, pool4(gr_wp), g_ps, up(gr_wa), up(gr_wb), up(gr_wo), g_n2,
        up(gr_f1), up(gr_f2),
        up(d_ada), d_b, d_n1, up(d_win), d_qn, d_kn, pool4(d_wp), d_ps, up(d_wa), up(d_wb), up(d_wo), d_n2,
        up(d_f1), up(d_f2),
        up(nm_ada), nm_b, nm_n1, up(nm_win), nm_qn, nm_kn, pool4(nm_wp), nm_ps, up(nm_wa), up(nm_wb), up(nm_wo), nm_n2,
        up(nm_f1), up(nm_f2),
        up(nv_ada), nv_b, nv_n1, up(nv_win), nv_qn, nv_kn, pool4(nv_wp), nv_ps, up(nv_wa), up(nv_wb), up(nv_wo), nv_n2,
        up(nv_f1), up(nv_f2),
    )
```

```python
import functools
import math

import jax
import jax.numpy as jnp
from jax import lax
from jax.experimental import pallas as pl
from jax.experimental.pallas import tpu as pltpu

F32 = jnp.float32
BF16 = jnp.bfloat16
MESH = pl.DeviceIdType.MESH
ANY = pl.BlockSpec(memory_space=pl.ANY)

EPS = 1e-6
HEAD_DIM = 128
POOL_WINDOWS = (2, 4, 8, 16)
N_GROUPS = len(POOL_WINDOWS)
N_CHIPS = 4
N_DEV = 8
ADAM_LR, ADAM_B1, ADAM_B2, ADAM_EPS, ADAM_WD, ADAM_STEP = 0.001, 0.9, 0.999, 1e-08, 0.01, 10
VMEM_LIMIT_V7X = 56 * 1024 * 1024
ATT_T = 256
POOL_T = 256


def _pcall(body, **kw):
    return pl.pallas_call(body, **kw)


def _params(sem=None):
    return pltpu.CompilerParams(dimension_semantics=sem, vmem_limit_bytes=VMEM_LIMIT_V7X)


def _tile(n, pref):
    if n <= pref:
        return n
    t = pref
    while n % t:
        t //= 2
    return t


class _Rider:
    def __init__(self, arrays, out_shape, sems, start, finish, aliases=None):
        self.arrays, self.out_shape, self.sems = list(arrays), list(out_shape), list(sems)
        self.start, self.finish, self.aliases = start, finish, aliases or {}


def _ride(name, body, riders, arrays, *, grid, in_specs, out_specs, out_shape, scratch_shapes, sem):
    n_in, n_out, n_scr = len(arrays), len(out_shape), len(scratch_shapes)
    r_arrays = [a for r in riders for a in r.arrays]
    r_outs = [o for r in riders for o in r.out_shape]
    r_sems = [s for r in riders for s in r.sems]
    aliases, off_i, off_o = {}, n_in, n_out
    for r in riders:
        for a, o in r.aliases.items():
            aliases[off_i + a] = off_o + o
        off_i += len(r.arrays)
        off_o += len(r.out_shape)

    def full(*refs):
        p = 0
        groups = []
        for n in (n_in, len(r_arrays), n_out, len(r_outs), n_scr, len(r_sems)):
            groups.append(refs[p:p + n])
            p += n
        ins, rin, outs, rout, scr, rsem = groups

        def each(what):
            a = o = s = 0
            for r in riders:
                getattr(r, what)(rin[a:a + len(r.arrays)], rout[o:o + len(r.out_shape)], rsem[s:s + len(r.sems)])
                a, o, s = a + len(r.arrays), o + len(r.out_shape), s + len(r.sems)

        if riders:
            ids = [pl.program_id(d) for d in range(len(grid))]
            first = functools.reduce(jnp.logical_and, [i == 0 for i in ids])
            last = functools.reduce(jnp.logical_and, [i == g - 1 for i, g in zip(ids, grid)])
            pl.when(first)(lambda: each("start"))
        body(*ins, *outs, *scr)
        if riders:
            pl.when(last)(lambda: each("finish"))

    res = _pcall(
        full, name=name, grid=grid, in_specs=list(in_specs) + [ANY] * len(r_arrays),
        out_specs=list(out_specs) + [ANY] * len(r_outs), out_shape=list(out_shape) + r_outs,
        scratch_shapes=list(scratch_shapes) + r_sems, input_output_aliases=aliases,
        compiler_params=_params(("arbitrary",) * len(grid) if riders else sem),
    )(*arrays, *r_arrays)
    if not riders:
        return res
    main, rest, per = res[:n_out], res[n_out:], []
    for r in riders:
        per.append(rest[:len(r.out_shape)])
        rest = rest[len(r.out_shape):]
    return main, per


def _run_rider(name, rider):
    def body(*refs):
        n_a, n_o = len(rider.arrays), len(rider.out_shape)
        ins, outs, sems = refs[:n_a], refs[n_a:n_a + n_o], refs[n_a + n_o:]
        rider.start(ins, outs, sems)
        rider.finish(ins, outs, sems)

    return _pcall(body, name=name, out_shape=rider.out_shape, in_specs=[ANY] * len(rider.arrays),
                  out_specs=[ANY] * len(rider.out_shape), scratch_shapes=rider.sems,
                  input_output_aliases=rider.aliases)(*rider.arrays)


def _mm(name, pairs, *, M, N, K, ta=False, tb=False, tm=512, tn=1024, tk=1024,
        a_pro=None, b_pro=None, extras=(), outs, epi, riders=()):
    tm, tn, tk = _tile(M, tm), _tile(N, tn), _tile(K, tk)
    n_i, n_j, n_k = M // tm, N // tn, K // tk
    n_p, n_e = len(pairs), len(extras)
    arrays, in_specs = [], []
    for a, _ in pairs:
        arrays.append(a)
        in_specs.append(pl.BlockSpec((tk, tm), lambda i, j, k: (k, i)) if ta
                        else pl.BlockSpec((tm, tk), lambda i, j, k: (i, k)))
    for _, b in pairs:
        arrays.append(b)
        in_specs.append(pl.BlockSpec((tn, tk), lambda i, j, k: (j, k)) if tb
                        else pl.BlockSpec((tk, tn), lambda i, j, k: (k, j)))
    for arr, kind, off in extras:
        ob = off // tn
        assert off % tn == 0
        arrays.append(arr)
        if kind == "tile":
            in_specs.append(pl.BlockSpec((tm, tn), lambda i, j, k, ob=ob: (i, j + ob)))
        else:
            in_specs.append(pl.BlockSpec((1, tn), lambda i, j, k, ob=ob: (0, j + ob)))
    out_shape, out_specs = [], []
    for o in outs:
        if o["kind"] == "tile":
            out_shape.append(jax.ShapeDtypeStruct((M, N), o["dtype"]))
            out_specs.append(pl.BlockSpec((tm, tn), lambda i, j, k: (i, j)))
        else:
            out_shape.append(jax.ShapeDtypeStruct((n_i, 1, N), F32))
            out_specs.append(pl.BlockSpec((1, 1, tn), lambda i, j, k: (i, 0, j)))
    dims = (((0 if ta else 1,), (1 if tb else 0,)), ((), ()))

    def body(*refs):
        a_refs, b_refs = refs[:n_p], refs[n_p:2 * n_p]
        e_refs = refs[2 * n_p:2 * n_p + n_e]
        o_refs = refs[2 * n_p + n_e:2 * n_p + n_e + len(outs)]
        acc_refs = refs[2 * n_p + n_e + len(outs):]
        k = pl.program_id(2)

        @pl.when(k == 0)
        def _():
            for acc in acc_refs:
                acc[...] = jnp.zeros_like(acc)

        for p in range(n_p):
            a, b = a_refs[p][...], b_refs[p][...]
            if a_pro is not None:
                a = a_pro(a)
            if b_pro is not None:
                b = b_pro(b)
            acc_refs[p][...] += lax.dot_general(a, b, dims, preferred_element_type=F32)

        @pl.when(k == n_k - 1)
        def _():
            vals = epi([acc[...] for acc in acc_refs], [e[...] for e in e_refs])
            for o, o_ref, val in zip(outs, o_refs, vals):
                if o["kind"] == "tile":
                    o_ref[...] = val.astype(o_ref.dtype)
                else:
                    o_ref[0] = val

    return _ride(name, body, riders, arrays, grid=(n_i, n_j, n_k), in_specs=in_specs, out_specs=out_specs,
                 out_shape=out_shape, scratch_shapes=[pltpu.VMEM((tm, tn), F32) for _ in pairs],
                 sem=("parallel", "parallel", "arbitrary"))


def _tile_out(dtype):
    return {"kind": "tile", "dtype": dtype}


_COLSUM = {"kind": "colsum"}


def _colsum(v):
    return jnp.sum(v, axis=0, keepdims=True)


def _norm_mod(name, x, norm_w, scale, shift):
    S, D = x.shape
    tr = _tile(S, 256)

    def body(x_ref, nw_ref, sc_ref, sh_ref, h_ref):
        xv = x_ref[...]
        r = lax.rsqrt(jnp.mean(xv * xv, axis=-1, keepdims=True) + EPS)
        h_ref[...] = ((xv * r * nw_ref[...]) * (1.0 + sc_ref[...]) + sh_ref[...]).astype(BF16)

    row = pl.BlockSpec((1, D), lambda i: (0, 0))
    til = pl.BlockSpec((tr, D), lambda i: (i, 0))
    return _pcall(body, name=name, grid=(S // tr,), in_specs=[til, row, row, row], out_specs=til,
                  out_shape=jax.ShapeDtypeStruct((S, D), BF16), compiler_params=_params(("parallel",)))(
                      x, norm_w, scale, shift)


def _norm_mod_bwd(name, dh, x, dres, norm_w, scale, gate_o=None):
    S, D = x.shape
    tr = _tile(S, 256)
    n_r = S // tr
    with_gate = gate_o is not None

    def body(*refs):
        if with_gate:
            dh_ref, x_ref, dres_ref, nw_ref, sc_ref, o_ref, g_ref, dx_ref, p1, p2, p3, do_ref, p4 = refs
        else:
            dh_ref, x_ref, dres_ref, nw_ref, sc_ref, dx_ref, p1, p2, p3 = refs
        xv, dhv, nw = x_ref[...], dh_ref[...], nw_ref[...]
        r = lax.rsqrt(jnp.mean(xv * xv, axis=-1, keepdims=True) + EPS)
        xh = xv * r
        p1[0] = _colsum(dhv)
        p2[0] = _colsum(dhv * (xh * nw))
        dn = dhv * (1.0 + sc_ref[...])
        p3[0] = _colsum(dn * xh)
        dxh = dn * nw
        dx = dres_ref[...] + r * (dxh - xh * jnp.mean(dxh * xh, axis=-1, keepdims=True))
        dx_ref[...] = dx
        if with_gate:
            do_ref[...] = (dx * g_ref[...]).astype(BF16)
            p4[0] = _colsum(dx * o_ref[...].astype(F32))

    row = pl.BlockSpec((1, D), lambda i: (0, 0))
    til = pl.BlockSpec((tr, D), lambda i: (i, 0))
    part = pl.BlockSpec((1, 1, D), lambda i: (i, 0, 0))
    part_shape = jax.ShapeDtypeStruct((n_r, 1, D), F32)
    in_specs = [til, til, til, row, row]
    arrays = [dh, x, dres, norm_w, scale]
    out_specs = [til, part, part, part]
    out_shape = [jax.ShapeDtypeStruct((S, D), F32), part_shape, part_shape, part_shape]
    if with_gate:
        in_specs += [til, row]
        arrays += list(gate_o)
        out_specs += [til, part]
        out_shape += [jax.ShapeDtypeStruct((S, D), BF16), part_shape]
    return _pcall(body, name=name, grid=(n_r,), in_specs=in_specs, out_specs=out_specs, out_shape=out_shape,
                  compiler_params=_params(("parallel",)))(*arrays)


def _pool_w_specs(rows, cg):
    return [pl.BlockSpec((rows, cg), lambda g, j=j: (N_GROUPS * j + g, 0)) for j in range(N_CHIPS)]


def _pool_fwd(proj, wp_full, pool_scale, S, PW):
    cg = PW // N_GROUPS
    rows = cg // N_CHIPS
    T = _tile(S, POOL_T)
    n_t = S // T

    def body(u_ref, w0, w1, w2, w3, ps_ref, pooled_ref, pa_ref):
        g = pl.program_id(0)
        win = jnp.left_shift(2, g)
        w = jnp.concatenate([w0[...], w1[...], w2[...], w3[...]], axis=0)
        t_i = lax.broadcasted_iota(jnp.int32, (T, T), 0)
        j_i = lax.broadcasted_iota(jnp.int32, (T, T), 1)
        b_cur = ((j_i <= t_i) & (j_i > t_i - win)).astype(BF16)
        b_prev = (j_i - T > t_i - win).astype(BF16)
        row = lax.broadcasted_iota(jnp.int32, (T, 1), 0)
        for r in range(n_t):
            cur = u_ref[r * T:(r + 1) * T, :]
            ws = jnp.dot(b_cur, cur, preferred_element_type=F32)
            if r > 0:
                ws += jnp.dot(b_prev, u_ref[(r - 1) * T:r * T, :], preferred_element_type=F32)
            count = jnp.minimum(row + (r * T + 1), win).astype(F32)
            pooled = (ws / count - cur.astype(F32)).astype(BF16)
            pooled_ref[r * T:(r + 1) * T, :] = pooled
            mixed = jnp.dot(pooled, w, preferred_element_type=F32)
            pa_ref[r * T:(r + 1) * T, :] = (mixed * ps_ref[...]).astype(BF16)

    col = pl.BlockSpec((S, cg), lambda g: (0, g))
    return _pcall(
        body, name="pool_fwd", grid=(N_GROUPS,),
        in_specs=[col] + _pool_w_specs(rows, cg) + [pl.BlockSpec((1, cg), lambda g: (0, g))],
        out_specs=[col, col],
        out_shape=[jax.ShapeDtypeStruct((S, PW), BF16), jax.ShapeDtypeStruct((S, PW), BF16)],
        compiler_params=_params(("parallel",)),
    )(proj, wp_full, wp_full, wp_full, wp_full, pool_scale)


def _pool_bwd(dpa, pooled, wp_full, pool_scale, S, PW):
    cg = PW // N_GROUPS
    rows = cg // N_CHIPS
    T = _tile(S, POOL_T)
    n_t = S // T

    def body(dpa_ref, pooled_ref, w0, w1, w2, w3, ps_ref, du_ref, gw_ref, gs_ref, dp_s, dpc_s, dmx_s):
        g = pl.program_id(0)
        win = jnp.left_shift(2, g)
        w = jnp.concatenate([w0[...], w1[...], w2[...], w3[...]], axis=0)
        row = lax.broadcasted_iota(jnp.int32, (T, 1), 0)
        gs = jnp.zeros((1, cg), F32)
        for r in range(n_t):
            sl = slice(r * T, (r + 1) * T)
            mixed = jnp.dot(pooled_ref[sl, :], w, preferred_element_type=F32)
            dpa_t = dpa_ref[sl, :]
            gs += _colsum(dpa_t * mixed)
            dmx = (dpa_t * ps_ref[...]).astype(BF16)
            dmx_s[sl, :] = dmx
            dpo = lax.dot_general(dmx, w, (((1,), (1,)), ((), ())), preferred_element_type=F32)
            dp_s[sl, :] = dpo
            count = jnp.minimum(row + (r * T + 1), win).astype(F32)
            dpc_s[sl, :] = (dpo / count).astype(BF16)
        gs_ref[...] = gs
        gw = lax.dot_general(pooled_ref[...], dmx_s[...], (((0,), (0,)), ((), ())), preferred_element_type=F32)
        for j in range(N_CHIPS):
            gw_ref[j, 0] = gw[j * rows:(j + 1) * rows, :].astype(BF16)
        j_i = lax.broadcasted_iota(jnp.int32, (T, T), 0)
        t_i = lax.broadcasted_iota(jnp.int32, (T, T), 1)
        b_cur = ((t_i >= j_i) & (t_i < j_i + win)).astype(BF16)
        b_next = (t_i + T < j_i + win).astype(BF16)
        for r in range(n_t):
            sl = slice(r * T, (r + 1) * T)
            acc = jnp.dot(b_cur, dpc_s[sl, :], preferred_element_type=F32)
            if r + 1 < n_t:
                acc += jnp.dot(b_next, dpc_s[(r + 1) * T:(r + 2) * T, :], preferred_element_type=F32)
            du_ref[sl, :] = (acc - dp_s[sl, :]).astype(BF16)

    col = pl.BlockSpec((S, cg), lambda g: (0, g))
    return _pcall(
        body, name="pool_bwd", grid=(N_GROUPS,),
        in_specs=[col, col] + _pool_w_specs(rows, cg) + [pl.BlockSpec((1, cg), lambda g: (0, g))],
        out_specs=[col, pl.BlockSpec((N_CHIPS, 1, rows, cg), lambda g: (0, g, 0, 0)),
                   pl.BlockSpec((1, cg), lambda g: (0, g))],
        out_shape=[jax.ShapeDtypeStruct((S, PW), BF16),
                   jax.ShapeDtypeStruct((N_CHIPS, N_GROUPS, rows, cg), BF16),
                   jax.ShapeDtypeStruct((1, PW), F32)],
        scratch_shapes=[pltpu.VMEM((S, cg), F32), pltpu.VMEM((S, cg), BF16), pltpu.VMEM((S, cg), BF16)],
        compiler_params=_params(("parallel",)),
    )(dpa, pooled, wp_full, wp_full, wp_full, wp_full, pool_scale)


_NT = (((1,), (1,)), ((), ()))
_TN = (((0,), (0,)), ((), ()))


def _split_dot(v, tri):
    hi = v.astype(BF16)
    lo = (v - hi.astype(F32)).astype(BF16)
    return jnp.dot(hi, tri, preferred_element_type=F32) + jnp.dot(lo, tri, preferred_element_type=F32)


def _sb_scores(q_i, k_j, tri_l, masked):
    tq, tk = q_i.shape[0], k_j.shape[0]
    s = lax.dot_general(q_i, k_j, _NT, preferred_element_type=F32) * (1.0 / math.sqrt(HEAD_DIM))
    lp = jnp.log(1.0 + jnp.exp(-jnp.abs(s)))
    l = -jnp.maximum(s, 0.0) - lp
    lb = l + s
    mask = None
    if masked:
        mask = lax.broadcasted_iota(jnp.int32, (tq, tk), 0) > lax.broadcasted_iota(jnp.int32, (tq, tk), 1)
        l = jnp.where(mask, l, 0.0)
    return l, lb, lb + _split_dot(l, tri_l), mask


def _sb_weights(t, carry_l, mask):
    a = jnp.exp(t + carry_l)
    return a if mask is None else jnp.where(mask, a, 0.0)


def _rowsum(v):
    return jnp.sum(v, axis=1, keepdims=True)


def _qk_norm(x_ref, w_ref):
    xv = x_ref[...].astype(F32)
    r = lax.rsqrt(jnp.mean(xv * xv, axis=-1, keepdims=True) + EPS)
    return xv * r, r


def _attn_fwd(proj, q_norm_w, k_norm_w, S, H, q_off, riders=()):
    t = _tile(S, ATT_T)
    n_q = S // t

    def body(q_ref, k_ref, v_ref, qw_ref, kw_ref, att_ref, attf_ref, qn_s, kn_s):
        qh, _ = _qk_norm(q_ref, qw_ref)
        qn_s[...] = (qh * qw_ref[...]).astype(BF16)
        kh, _ = _qk_norm(k_ref, kw_ref)
        kn_s[...] = (kh * kw_ref[...]).astype(BF16)
        tri_l = (lax.broadcasted_iota(jnp.int32, (t, t), 0) > lax.broadcasted_iota(jnp.int32, (t, t), 1)).astype(BF16)

        def rows(j):
            return pl.ds(pl.multiple_of(j * t, t), t)

        def q_step(i, _):
            q_i = qn_s[rows(i), :]

            def av(a, j):
                return jnp.dot(a.astype(BF16), v_ref[rows(j), :], preferred_element_type=F32)

            l, _, tt, mask = _sb_scores(q_i, kn_s[rows(i), :], tri_l, True)
            acc = av(_sb_weights(tt, 0.0, mask), i)
            carry = _rowsum(l)

            def single(_, c):
                carry, acc = c
                l, _, tt, _ = _sb_scores(q_i, kn_s[rows(i - 1), :], tri_l, False)
                return carry + _rowsum(l), acc + av(_sb_weights(tt, carry, None), i - 1)

            carry, acc = lax.fori_loop(0, i % 2, single, (carry, acc))
            top = i - 1 - i % 2

            def pair(p, c):
                carry, acc = c
                j0 = top - 2 * p
                l0, _, t0, _ = _sb_scores(q_i, kn_s[rows(j0), :], tri_l, False)
                l1, _, t1, _ = _sb_scores(q_i, kn_s[rows(j0 - 1), :], tri_l, False)
                mid = carry + _rowsum(l0)
                acc = acc + av(_sb_weights(t0, carry, None), j0) + av(_sb_weights(t1, mid, None), j0 - 1)
                return mid + _rowsum(l1), acc

            _, acc = lax.fori_loop(0, i // 2, pair, (carry, acc))
            att_ref[rows(i), :] = acc.astype(BF16)
            attf_ref[rows(i), :] = acc
            return 0

        lax.fori_loop(0, n_q, q_step, 0)

    def col(off):
        return pl.BlockSpec((S, HEAD_DIM), lambda h, off=off: (0, off + h))

    wspec = pl.BlockSpec((1, HEAD_DIM), lambda h: (0, 0))
    return _ride(
        "attn_fwd", body, riders, [proj, proj, proj, q_norm_w, k_norm_w], grid=(H,),
        in_specs=[col(q_off), col(q_off + H), col(q_off + 2 * H), wspec, wspec],
        out_specs=[col(0), col(0)],
        out_shape=[jax.ShapeDtypeStruct((S, H * HEAD_DIM), BF16), jax.ShapeDtypeStruct((S, H * HEAD_DIM), F32)],
        scratch_shapes=[pltpu.VMEM((S, HEAD_DIM), BF16), pltpu.VMEM((S, HEAD_DIM), BF16)],
        sem=("parallel",))


def _attn_bwd(proj, datt, attf, q_norm_w, k_norm_w, S, H, q_off, riders=()):
    t = _tile(S, ATT_T)
    n_q = S // t
    scale = 1.0 / math.sqrt(HEAD_DIM)

    def body(q_ref, k_ref, v_ref, do_ref, o_ref, qw_ref, kw_ref, dq_ref, dk_ref, dv_ref, gq_ref, gk_ref,
             qn_s, kn_s, dk_s, dv_s, gq_s):
        qw, kw = qw_ref[...], kw_ref[...]
        qh, _ = _qk_norm(q_ref, qw_ref)
        qn_s[...] = (qh * qw).astype(BF16)
        kh, _ = _qk_norm(k_ref, kw_ref)
        kn_s[...] = (kh * kw).astype(BF16)
        dk_s[...] = jnp.zeros_like(dk_s)
        dv_s[...] = jnp.zeros_like(dv_s)
        gq_s[...] = jnp.zeros_like(gq_s)
        r_i = lax.broadcasted_iota(jnp.int32, (t, t), 0)
        c_i = lax.broadcasted_iota(jnp.int32, (t, t), 1)
        tri_l = (r_i > c_i).astype(BF16)
        tri_e = (r_i >= c_i).astype(BF16)

        def rows(j):
            return pl.ds(pl.multiple_of(j * t, t), t)

        def q_step(i, _):
            q_i = qn_s[rows(i), :]
            do_i = do_ref[rows(i), :]
            d_i = _rowsum(do_i.astype(F32) * o_ref[rows(i), :])

            def scores(j, masked):
                k_j = kn_s[rows(j), :]
                l, lb, tt, mask = _sb_scores(q_i, k_j, tri_l, masked)
                da = lax.dot_general(do_i, v_ref[rows(j), :], _NT, preferred_element_type=F32)
                return k_j, l, lb, tt, mask, da

            def grads(j, sc, carry_l, carry_e, dq_acc):
                k_j, l, lb, tt, mask, da = sc
                a_bf = _sb_weights(tt, carry_l, mask).astype(BF16)
                e = da * a_bf.astype(F32)
                p = d_i - (_split_dot(e, tri_e) + carry_e)
                sig = jnp.exp(lb)
                dz = e * (1.0 - sig) - p * sig
                if mask is not None:
                    dz = jnp.where(mask, dz, 0.0)
                dz = (dz * scale).astype(BF16)
                dk_s[rows(j), :] += lax.dot_general(dz, q_i, _TN, preferred_element_type=F32)
                dv_s[rows(j), :] += lax.dot_general(a_bf, do_i, _TN, preferred_element_type=F32)
                return (carry_l + _rowsum(l), carry_e + _rowsum(e),
                        dq_acc + jnp.dot(dz, k_j, preferred_element_type=F32))

            c = grads(i, scores(i, True), 0.0, 0.0, jnp.zeros((t, HEAD_DIM), F32))
            c = lax.fori_loop(0, i % 2, lambda _, c: grads(i - 1, scores(i - 1, False), *c), c)
            top = i - 1 - i % 2

            def pair(p, c):
                j0 = top - 2 * p
                s0, s1 = scores(j0, False), scores(j0 - 1, False)
                return grads(j0 - 1, s1, *grads(j0, s0, *c))

            _, _, dqn = lax.fori_loop(0, i // 2, pair, c)
            qv = q_ref[rows(i), :].astype(F32)
            r = lax.rsqrt(jnp.mean(qv * qv, axis=-1, keepdims=True) + EPS)
            xh = qv * r
            gq_s[...] += _colsum(dqn * xh)
            dxh = dqn * qw
            dq_ref[rows(i), :] = (r * (dxh - xh * jnp.mean(dxh * xh, axis=-1, keepdims=True))).astype(BF16)
            return 0

        lax.fori_loop(0, n_q, q_step, 0)
        gq_ref[0] = gq_s[...]
        kh, rk = _qk_norm(k_ref, kw_ref)
        dkn = dk_s[...]
        gk_ref[0] = _colsum(dkn * kh)
        dxh = dkn * kw
        dk_ref[...] = (rk * (dxh - kh * jnp.mean(dxh * kh, axis=-1, keepdims=True))).astype(BF16)
        dv_ref[...] = dv_s[...].astype(BF16)

    def col(off):
        return pl.BlockSpec((S, HEAD_DIM), lambda h, off=off: (0, off + h))

    wspec = pl.BlockSpec((1, HEAD_DIM), lambda h: (0, 0))
    gspec = pl.BlockSpec((1, 1, HEAD_DIM), lambda h: (h, 0, 0))
    act = jax.ShapeDtypeStruct((S, H * HEAD_DIM), BF16)
    gsh = jax.ShapeDtypeStruct((H, 1, HEAD_DIM), F32)
    return _ride(
        "attn_bwd", body, riders, [proj, proj, proj, datt, attf, q_norm_w, k_norm_w], grid=(H,),
        in_specs=[col(q_off), col(q_off + H), col(q_off + 2 * H), col(0), col(0), wspec, wspec],
        out_specs=[col(0), col(0), col(0), gspec, gspec],
        out_shape=[act, act, act, gsh, gsh],
        scratch_shapes=[pltpu.VMEM((S, HEAD_DIM), BF16), pltpu.VMEM((S, HEAD_DIM), BF16),
                        pltpu.VMEM((S, HEAD_DIM), F32), pltpu.VMEM((S, HEAD_DIM), F32),
                        pltpu.VMEM((1, HEAD_DIM), F32)],
        sem=("parallel",))


def _place():
    x, y, c = lax.axis_index("x"), lax.axis_index("y"), lax.axis_index("c")
    chips = [(1 - x, y), (x, 1 - y), (1 - x, 1 - y)]
    return x, y, c, chips


def _dev_allgather(name, v):
    m_per, n = v.shape

    def body(x_ref, out_ref, send_sems, recv_sems, local_sem):
        x, y, c, chips = _place()
        me, sibling = (x, y, c), (x, y, 1 - c)

        def rows(px, py, pc):
            return out_ref.at[pl.ds((4 * px + 2 * py + pc) * m_per, m_per), :]

        def copy(k, block, to, src=None):
            return pltpu.make_async_remote_copy(
                src_ref=rows(*block) if src is None else src, dst_ref=rows(*block),
                send_sem=send_sems.at[k], recv_sem=recv_sems.at[k], device_id=to, device_id_type=MESH)

        mine = pltpu.make_async_copy(x_ref, rows(*me), local_sem)
        mine.start()
        first = [copy(0, me, sibling, src=x_ref)]
        first += [copy(1 + j, me, (*chip, c), src=x_ref) for j, chip in enumerate(chips)]
        for cp in first:
            cp.start()
        passed = [copy(4 + j, (*chip, c), sibling) for j, chip in enumerate(chips)]
        for j, chip in enumerate(chips):
            copy(1 + j, (*chip, c), me).wait_recv()
            passed[j].start()
        copy(0, sibling, me).wait_recv()
        for j, chip in enumerate(chips):
            copy(4 + j, (*chip, 1 - c), me).wait_recv()
        for cp in first + passed:
            cp.wait_send()
        mine.wait()

    return _pcall(
        body, name=name, out_shape=jax.ShapeDtypeStruct((N_DEV * m_per, n), v.dtype),
        in_specs=[pl.BlockSpec(memory_space=pltpu.VMEM)], out_specs=pl.BlockSpec(memory_space=pltpu.VMEM),
        scratch_shapes=[pltpu.SemaphoreType.DMA((7,)), pltpu.SemaphoreType.DMA((7,)), pltpu.SemaphoreType.DMA],
        compiler_params=pltpu.CompilerParams(vmem_limit_bytes=VMEM_LIMIT_V7X),
    )(v)


class _W:
    def __init__(self, name, kind, R, C):
        self.name, self.kind, self.R, self.C = name, kind, R, C

    @property
    def shard_shape(self):
        return (self.R, self.C // N_CHIPS) if self.kind == "col" else (self.R // N_CHIPS, self.C)

    @property
    def half_rows(self):
        return self.shard_shape[0] // 2

    def shard_half(self, ref, half):
        return ref.at[pl.ds(half * self.half_rows, self.half_rows), :]

    def region(self, full_ref, chip, half):
        hr = self.half_rows
        if self.kind == "col":
            cw = self.C // N_CHIPS
            return full_ref.at[pl.ds(half * hr, hr), pl.ds(chip * cw, cw)]
        return full_ref.at[pl.ds(chip * (2 * hr) + half * hr, hr), :]

    def region_both(self, full_ref, chip):
        hr = self.half_rows
        if self.kind == "col":
            cw = self.C // N_CHIPS
            return full_ref.at[:, pl.ds(chip * cw, cw)]
        return full_ref.at[pl.ds(chip * (2 * hr), 2 * hr), :]


def _ag_rider(ws, shards):
    n_w = len(ws)

    def parts(sh, full, sems):
        send_sems, recv_sems, local_sems = sems
        x, y, c, chips = _place()
        my_chip = 2 * x + y

        def direct(i, k, recv=False):
            w, chip = ws[i], chips[k]
            src_chip = (2 * chip[0] + chip[1]) if recv else my_chip
            dst = w.region(full[i], src_chip, c)
            return pltpu.make_async_remote_copy(
                src_ref=dst if recv else w.shard_half(sh[i], c), dst_ref=dst,
                send_sem=send_sems.at[6 * i + k], recv_sem=recv_sems.at[6 * i + k],
                device_id=(*chip, c), device_id_type=MESH)

        def passed(i, k, half):
            chip = chips[k]
            reg = ws[i].region(full[i], 2 * chip[0] + chip[1], half)
            return pltpu.make_async_remote_copy(
                src_ref=reg, dst_ref=reg, send_sem=send_sems.at[6 * i + 3 + k], recv_sem=recv_sems.at[6 * i + 3 + k],
                device_id=(x, y, 1 - c), device_id_type=MESH)

        def mine(i):
            return pltpu.make_async_copy(sh[i], ws[i].region_both(full[i], my_chip), local_sems.at[i])

        return c, direct, passed, mine

    both = [(i, k) for i in range(n_w) for k in range(N_CHIPS - 1)]

    def start(sh, full, sems):
        _, direct, _, mine = parts(sh, full, sems)
        for i, k in both:
            direct(i, k).start()
        for i in range(n_w):
            mine(i).start()

    def finish(sh, full, sems):
        c, direct, passed, mine = parts(sh, full, sems)
        for i, k in both:
            direct(i, k, recv=True).wait_recv()
            passed(i, k, c).start()
        for i, k in both:
            passed(i, k, 1 - c).wait_recv()
        for i, k in both:
            direct(i, k).wait_send()
            passed(i, k, c).wait_send()
        for i in range(n_w):
            mine(i).wait()

    return _Rider(shards, [jax.ShapeDtypeStruct((w.R, w.C), BF16) for w in ws],
                  [pltpu.SemaphoreType.DMA((6 * n_w,)), pltpu.SemaphoreType.DMA((6 * n_w,)),
                   pltpu.SemaphoreType.DMA((n_w,))], start, finish)


def _half_view(w, g):
    return g if w.kind == "col" else g.reshape(N_CHIPS, w.R // N_CHIPS, w.C)


def _px_rider(ws, grads):
    n_w = len(ws)

    def copies(g, got, sems):
        send_sems, recv_sems = sems
        x, y, c, _ = _place()

        def half_all(w, ref, half):
            hr = w.half_rows
            if w.kind == "col":
                return ref.at[pl.ds(half * hr, hr), :]
            return ref.at[:, pl.ds(half * hr, hr), :]

        return [pltpu.make_async_remote_copy(
            src_ref=half_all(w, g[i], 1 - c), dst_ref=got[i], send_sem=send_sems.at[i], recv_sem=recv_sems.at[i],
            device_id=(x, y, 1 - c), device_id_type=MESH) for i, w in enumerate(ws)]

    def start(g, got, sems):
        for cp in copies(g, got, sems):
            cp.start()

    def finish(g, got, sems):
        for cp in copies(g, got, sems):
            cp.wait_recv()
            cp.wait_send()

    def got_shape(w):
        hr = w.half_rows
        return (hr, w.C) if w.kind == "col" else (N_CHIPS, hr, w.C)

    return _Rider([_half_view(w, g) for w, g in zip(ws, grads)],
                  [jax.ShapeDtypeStruct(got_shape(w), BF16) for w in ws],
                  [pltpu.SemaphoreType.DMA((n_w,)), pltpu.SemaphoreType.DMA((n_w,))], start, finish)


def _pair_sum(w, g, got, c_arr):
    hr = w.half_rows
    if w.kind == "col":
        tr, tc = _tile(hr, 512), _tile(w.C, 2048)
        n_r = hr // tr
        grid = (n_r, w.C // tc)
        g_spec = pl.BlockSpec((tr, tc), lambda i, j, c: (c[0] * n_r + i, j))
        o_spec = pl.BlockSpec((tr, tc), lambda i, j, c: (i, j))
    else:
        tr = _tile(hr, 512)
        n_r = hr // tr
        grid = (N_CHIPS, n_r)
        g_spec = pl.BlockSpec((1, tr, w.C), lambda s, i, c: (s, c[0] * n_r + i, 0))
        o_spec = pl.BlockSpec((1, tr, w.C), lambda s, i, c: (s, i, 0))

    def body(c_ref, g_ref, got_ref, out_ref):
        out_ref[...] = (g_ref[...].astype(F32) + got_ref[...].astype(F32)).astype(BF16)

    return _pcall(
        body, name="grad_pair_sum_" + w.name, out_shape=jax.ShapeDtypeStruct(got.shape, BF16),
        grid_spec=pltpu.PrefetchScalarGridSpec(num_scalar_prefetch=1, grid=grid, in_specs=[g_spec, o_spec],
                                               out_specs=o_spec),
        compiler_params=_params(("parallel", "parallel")),
    )(c_arr, _half_view(w, g), got)


def _cx_rider(ws, sums):
    n_w = len(ws)

    def parts(p, q, sems):
        send_sems, recv_sems, local_sems = sems
        x, y, c, chips = _place()
        my_chip = 2 * x + y

        def piece(w, ref, chip):
            if w.kind == "col":
                cw = w.C // N_CHIPS
                return ref.at[:, pl.ds(chip * cw, cw)]
            return ref.at[chip]

        def copy(i, k, recv=False):
            chip = chips[k]
            to_chip = 2 * chip[0] + chip[1]
            return pltpu.make_async_remote_copy(
                src_ref=piece(ws[i], p[i], to_chip), dst_ref=q[i].at[to_chip if recv else my_chip],
                send_sem=send_sems.at[3 * i + k], recv_sem=recv_sems.at[3 * i + k],
                device_id=(*chip, c), device_id_type=MESH)

        def mine(i):
            return pltpu.make_async_copy(piece(ws[i], p[i], my_chip), q[i].at[my_chip], local_sems.at[i])

        return copy, mine

    both = [(i, k) for i in range(n_w) for k in range(N_CHIPS - 1)]

    def start(p, q, sems):
        copy, mine = parts(p, q, sems)
        for i, k in both:
            copy(i, k).start()
        for i in range(n_w):
            mine(i).start()

    def finish(p, q, sems):
        copy, mine = parts(p, q, sems)
        for i, k in both:
            copy(i, k, recv=True).wait_recv()
        for i, k in both:
            copy(i, k).wait_send()
        for i in range(n_w):
            mine(i).wait()

    return _Rider(sums, [jax.ShapeDtypeStruct((N_CHIPS, w.half_rows, w.shard_shape[1]), BF16) for w in ws],
                  [pltpu.SemaphoreType.DMA((3 * n_w,)), pltpu.SemaphoreType.DMA((3 * n_w,)),
                   pltpu.SemaphoreType.DMA((n_w,))], start, finish)


def _chip_sum(w, q, c_arr):
    hr, cols = w.half_rows, w.shard_shape[1]
    tr, tc = _tile(hr, 512), _tile(cols, 2048)
    n_r = hr // tr

    def body(c_ref, q0, q1, q2, q3, out_ref):
        out_ref[...] = ((q0[0].astype(F32) + q1[0].astype(F32)) + q2[0].astype(F32)) + q3[0].astype(F32)

    q_specs = [pl.BlockSpec((1, tr, tc), lambda i, j, c, s=s: (s, i, j)) for s in range(N_CHIPS)]
    return _pcall(
        body, name="grad_chip_sum_" + w.name, out_shape=jax.ShapeDtypeStruct(w.shard_shape, F32),
        grid_spec=pltpu.PrefetchScalarGridSpec(
            num_scalar_prefetch=1, grid=(n_r, cols // tc), in_specs=q_specs,
            out_specs=pl.BlockSpec((tr, tc), lambda i, j, c: (c[0] * n_r + i, j))),
        compiler_params=_params(("parallel", "parallel")),
    )(c_arr, q, q, q, q)


def _sf_rider(ws, grads):
    n_w = len(ws)

    def copy(g, sems, i, half):
        send_sems, recv_sems = sems
        x, y, c, _ = _place()
        h = c if half == "mine" else 1 - c
        reg = ws[i].shard_half(g[i], h)
        return pltpu.make_async_remote_copy(src_ref=reg, dst_ref=reg, send_sem=send_sems.at[i], recv_sem=recv_sems.at[i],
                                            device_id=(x, y, 1 - c), device_id_type=MESH)

    def start(_, g, sems):
        for i in range(n_w):
            copy(g, sems, i, "mine").start()

    def finish(_, g, sems):
        for i in range(n_w):
            copy(g, sems, i, "other").wait_recv()
            copy(g, sems, i, "mine").wait_send()

    return _Rider(grads, [jax.ShapeDtypeStruct(w.shard_shape, F32) for w in ws],
                  [pltpu.SemaphoreType.DMA((n_w,)), pltpu.SemaphoreType.DMA((n_w,))], start, finish,
                  aliases={i: i for i in range(n_w)})


def _adamw_math(w, g, m, v):
    m = ADAM_B1 * m + (1.0 - ADAM_B1) * g
    v = ADAM_B2 * v + (1.0 - ADAM_B2) * (g * g)
    m_hat = m / (1.0 - ADAM_B1 ** ADAM_STEP)
    v_hat = v / (1.0 - ADAM_B2 ** ADAM_STEP)
    delta = -ADAM_LR * (m_hat / (jnp.sqrt(v_hat) + ADAM_EPS) + ADAM_WD * w)
    return delta, m, v


def _adamw(name, w, g, m, v):
    R, C = w.shape
    tr, tc = _tile(R, 256), _tile(C, 2048)

    def body(w_ref, g_ref, m_ref, v_ref, d_out, m_out, v_out):
        d_out[...], m_out[...], v_out[...] = _adamw_math(w_ref[...], g_ref[...], m_ref[...], v_ref[...])

    spec = pl.BlockSpec((tr, tc), lambda i, j: (i, j))
    sh = jax.ShapeDtypeStruct((R, C), F32)
    return _pcall(body, name=name, grid=(R // tr, C // tc), in_specs=[spec] * 4, out_specs=[spec] * 3,
                  out_shape=[sh, sh, sh], compiler_params=_params(("parallel", "parallel")))(w, g, m, v)


def _ada_update(sct, dmod_sh, w, m, v, riders=()):
    R, C = w.shape
    tr, tc = _tile(R, 256), _tile(C, 1024)

    def body(s_ref, d_ref, w_ref, m_ref, v_ref, g_out, d_out, m_out, v_out):
        s, d = s_ref[...], d_ref[...]
        g = s[:, 0:1] * d[0:1, :]
        for b in range(1, N_DEV):
            g += s[:, b:b + 1] * d[b:b + 1, :]
        g_out[...] = g
        d_out[...], m_out[...], v_out[...] = _adamw_math(w_ref[...], g, m_ref[...], v_ref[...])

    spec = pl.BlockSpec((tr, tc), lambda i, j: (i, j))
    sh = jax.ShapeDtypeStruct((R, C), F32)
    return _ride(
        "ada_update", body, riders, [sct, dmod_sh, w, m, v], grid=(R // tr, C // tc),
        in_specs=[pl.BlockSpec((tr, N_DEV), lambda i, j: (i, 0)), pl.BlockSpec((N_DEV, tc), lambda i, j: (0, j)),
                  spec, spec, spec],
        out_specs=[spec] * 4, out_shape=[sh] * 4, scratch_shapes=[], sem=("parallel", "parallel"))


def _cast_bf16(name, w):
    R, C = w.shape
    tr, tc = _tile(R, 512), _tile(C, 2048)

    def body(w_ref, o_ref):
        o_ref[...] = w_ref[...].astype(BF16)

    spec = pl.BlockSpec((tr, tc), lambda i, j: (i, j))
    return _pcall(body, name=name, grid=(R // tr, C // tc), in_specs=[spec], out_specs=spec,
                  out_shape=jax.ShapeDtypeStruct((R, C), BF16), compiler_params=_params(("parallel", "parallel")))(w)


def _silu_rows(c_row):
    D = c_row.shape[1]

    def body(c_ref, o_ref):
        cv = c_ref[...]
        o_ref[...] = cv * jax.nn.sigmoid(cv)

    return _pcall(body, name="silu_c", out_shape=jax.ShapeDtypeStruct((1, D), F32))(c_row)


def _pack_partials(parts, widths, total):
    n = len(widths)

    def body(*refs):
        loss_p, out_ref, loss_ref = refs[n], refs[n + 1], refs[n + 2]
        off = 0
        for ref, wd in zip(refs[:n], widths):
            out_ref[:, off:off + wd] = jnp.sum(ref[...], axis=0)
            off += wd
        if off < total:
            out_ref[:, off:total] = jnp.zeros((1, total - off), F32)
        loss_ref[...] = jnp.sum(jnp.sum(loss_p[...], axis=0), axis=1, keepdims=True)

    return _pcall(body, name="pack_partials",
                  out_shape=[jax.ShapeDtypeStruct((1, total), F32), jax.ShapeDtypeStruct((1, 1), F32)])(*parts)


def _small_update(gathered, offsets, params):
    n_p = len(params)

    def body(*refs):
        g_ref = refs[0]
        prm = refs[1:1 + 3 * n_p]
        outs = refs[1 + 3 * n_p:]
        for i, (off, wd) in enumerate(offsets):
            blk = g_ref[:, off:off + wd]
            g = blk[0:1, :]
            for b in range(1, N_DEV):
                g = g + blk[b:b + 1, :]
            w, m, v = prm[3 * i][...], prm[3 * i + 1][...], prm[3 * i + 2][...]
            outs[4 * i][...] = g
            outs[4 * i + 1][...], outs[4 * i + 2][...], outs[4 * i + 3][...] = _adamw_math(w, g, m, v)

    flat = [a for t in params for a in t]
    out_shape = [jax.ShapeDtypeStruct(t[0].shape, F32) for t in params for _ in range(4)]
    return _pcall(body, name="small_update", out_shape=out_shape)(gathered, *flat)


def kernel(x, c, w_ada, b_ada, norm1_w, w_in, q_norm_w, k_norm_w, w_pool, pool_scale, w_a_up, w_b_up, w_o, norm2_w, w_ff1, w_ff2, loss_target, m_w_ada, m_b_ada, m_norm1_w, m_w_in, m_q_norm_w, m_k_norm_w, m_w_pool, m_pool_scale, m_w_a_up, m_w_b_up, m_w_o, m_norm2_w, m_w_ff1, m_w_ff2, v_w_ada, v_b_ada, v_norm1_w, v_w_in, v_q_norm_w, v_k_norm_w, v_w_pool, v_pool_scale, v_w_a_up, v_w_b_up, v_w_o, v_norm2_w, v_w_ff1, v_w_ff2):
    _, S, D = x.shape
    PW = D // 2
    H = PW // HEAD_DIM
    cg = PW // N_GROUPS
    IN = w_in.shape[2] * N_CHIPS
    FF = w_ff1.shape[2] * N_CHIPS
    A_COLS = w_ada.shape[2]
    xi, yi, ci = lax.axis_index("x"), lax.axis_index("y"), lax.axis_index("c")
    chip = 2 * xi + yi
    dev = 2 * chip + ci
    c_arr = jnp.reshape(ci, (1,)).astype(jnp.int32)
    x2, tgt = x[0], loss_target[0]

    ws = [_W("w_in", "col", D, IN), _W("w_pool", "row", PW, cg), _W("w_a_up", "col", PW, D),
          _W("w_b_up", "col", PW, D), _W("w_o", "row", D, D), _W("w_ff1", "col", D, FF), _W("w_ff2", "row", FF, D)]
    w32 = [w_in[0], w_pool[0].reshape(cg, cg), w_a_up[0], w_b_up[0], w_o[0], w_ff1[0], w_ff2[0]]
    m32 = [m_w_in[0], m_w_pool[0].reshape(cg, cg), m_w_a_up[0], m_w_b_up[0], m_w_o[0], m_w_ff1[0], m_w_ff2[0]]
    v32 = [v_w_in[0], v_w_pool[0].reshape(cg, cg), v_w_a_up[0], v_w_b_up[0], v_w_o[0], v_w_ff1[0], v_w_ff2[0]]

    W_IN, W_POOL, W_A, W_B, W_O, W_FF1, W_FF2 = ws
    s_in, s_pool, s_a, s_b, s_o, s_ff1, s_ff2 = [_cast_bf16("cast_" + w.name, a) for w, a in zip(ws, w32)]
    (win_f,) = _run_rider("gather_w_in", _ag_rider([W_IN], [s_in]))

    sc_row = _silu_rows(c)
    sc_all = _dev_allgather("gather_silu_c", sc_row.reshape(8, D // 8)).reshape(N_DEV, D)
    sc16 = jnp.concatenate([sc_all, jnp.zeros_like(sc_all)], axis=0)
    b_cols = lax.dynamic_slice(b_ada, (0, chip * A_COLS), (1, A_COLS))
    (mod_cols,) = _mm("mod_cols", [(sc16, w_ada[0])], M=2 * N_DEV, N=A_COLS, K=D, tm=16, tn=1024, tk=1024,
                      a_pro=lambda a: a.astype(BF16), b_pro=lambda b: b.astype(BF16),
                      extras=[(b_cols, "row", 0)], outs=[_tile_out(F32)], epi=lambda accs, ex: [accs[0] + ex[0]])
    mod_all = _dev_allgather("gather_mod", mod_cols[:N_DEV]).reshape(N_CHIPS, 2, N_DEV, A_COLS)
    mod_row = lax.dynamic_index_in_dim(mod_all[:, 0], dev, axis=1, keepdims=False).reshape(1, N_CHIPS * A_COLS)
    shift1, scale1, gate1, shift2, scale2, gate2 = [mod_row[:, i * D:(i + 1) * D] for i in range(6)]

    h = _norm_mod("norm1_mod", x2, norm1_w, scale1, shift1)
    (proj,), ((wpool_f, wa_f, wb_f, wo_f),) = _mm(
        "in_proj", [(h, win_f)], M=S, N=IN, K=D, outs=[_tile_out(BF16)], epi=lambda accs, ex: [accs[0]],
        riders=[_ag_rider([W_POOL, W_A, W_B, W_O], [s_pool, s_a, s_b, s_o])])
    pooled, pa = _pool_fwd(proj, wpool_f, pool_scale, S, PW)
    (att, attf), ((wff1_f,),) = _attn_fwd(proj, q_norm_w, k_norm_w, S, H, PW // HEAD_DIM,
                                          riders=[_ag_rider([W_FF1], [s_ff1])])

    def merge_epi(accs, ex):
        sa, sb = jax.nn.sigmoid(ex[0].astype(F32)), jax.nn.sigmoid(ex[1].astype(F32))
        return [sa * accs[0] + sb * accs[1], accs[0], accs[1]]

    merged, ya, yb = _mm("branch_up_merge", [(pa, wa_f), (att, wb_f)], M=S, N=D, K=PW,
                         extras=[(proj, "tile", 4 * PW), (proj, "tile", 4 * PW + D)],
                         outs=[_tile_out(BF16)] * 3, epi=merge_epi)
    x1, o = _mm("out_proj", [(merged, wo_f)], M=S, N=D, K=D, extras=[(x2, "tile", 0), (gate1, "row", 0)],
                outs=[_tile_out(F32), _tile_out(BF16)], epi=lambda accs, ex: [ex[0] + ex[1] * accs[0], accs[0]])
    h2 = _norm_mod("norm2_mod", x1, norm2_w, scale2, shift2)
    (rl,), ((wff2_f,),) = _mm("ff1", [(h2, wff1_f)], M=S, N=FF, K=D, outs=[_tile_out(BF16)],
                              epi=lambda accs, ex: [jnp.maximum(accs[0], 0.0)],
                              riders=[_ag_rider([W_FF2], [s_ff2])])

    def square(a):
        af = a.astype(F32)
        return (af * af).astype(BF16)

    def loss_epi(accs, ex):
        x1_t, tgt_t, g2 = ex
        f = accs[0]
        diff = (x1_t + g2 * f) - tgt_t
        dy = diff * (1.0 / D)
        return [dy, dy * g2, _colsum(dy * f), _colsum(diff * diff)]

    dy, df, dgate2_p, loss_p = _mm("ff2_loss", [(rl, wff2_f)], M=S, N=D, K=FF, a_pro=square,
                                   extras=[(x1, "tile", 0), (tgt, "tile", 0), (gate2, "row", 0)],
                                   outs=[_tile_out(F32), _tile_out(BF16), _COLSUM, _COLSUM], epi=loss_epi)

    def pair_sums(group, partials, got):
        return [_pair_sum(w, g, r, c_arr) for w, g, r in zip(group, partials, got)]

    first = lambda accs, ex: [accs[0]]
    gmm = dict(ta=True, tm=1024, tn=1024, tk=512, outs=[_tile_out(BF16)], epi=first)
    (g_ff2,) = _mm("grad_w_ff2", [(rl, df)], M=FF, N=D, K=S, a_pro=square, **gmm)
    (dz1,), (got_ff2,) = _mm("d_ff_hidden", [(df, wff2_f)], M=S, N=FF, K=D, tb=True, extras=[(rl, "tile", 0)],
                             outs=[_tile_out(BF16)], epi=lambda accs, ex: [accs[0] * (2.0 * ex[0].astype(F32))],
                             riders=[_px_rider([W_FF2], [g_ff2])])
    sum_ff2 = pair_sums([W_FF2], [g_ff2], got_ff2)
    (g_ff1,), (q_ff2,) = _mm("grad_w_ff1", [(h2, dz1)], M=D, N=FF, K=S, riders=[_cx_rider([W_FF2], sum_ff2)], **gmm)
    (dh2,), (got_ff1,) = _mm("d_h2", [(dz1, wff1_f)], M=S, N=D, K=FF, tb=True, outs=[_tile_out(F32)], epi=first,
                             riders=[_px_rider([W_FF1], [g_ff1])])
    sum_ff1 = pair_sums([W_FF1], [g_ff1], got_ff1)
    dx1, dshift2_p, dscale2_p, gn2_p, do, dgate1_p = _norm_mod_bwd("norm2_bwd", dh2, x1, dy, norm2_w, scale2,
                                                                   gate_o=(o, gate1))
    (g_wo,) = _mm("grad_w_o", [(merged, do)], M=D, N=D, K=S, **gmm)

    def gate_epi(accs, ex):
        dm = accs[0]
        sa, sb = jax.nn.sigmoid(ex[0].astype(F32)), jax.nn.sigmoid(ex[1].astype(F32))
        ya_t, yb_t = ex[2].astype(F32), ex[3].astype(F32)
        return [dm * sa, dm * sb, dm * ya_t * (sa * (1.0 - sa)), dm * yb_t * (sb * (1.0 - sb))]

    dya, dyb, dga, dgb = _mm("d_merged", [(do, wo_f)], M=S, N=D, K=D, tb=True,
                             extras=[(proj, "tile", 4 * PW), (proj, "tile", 4 * PW + D), (ya, "tile", 0), (yb, "tile", 0)],
                             outs=[_tile_out(BF16)] * 4, epi=gate_epi)
    (g_wa,) = _mm("grad_w_a_up", [(pa, dya)], M=PW, N=D, K=S, **gmm)
    (g_wb,) = _mm("grad_w_b_up", [(att, dyb)], M=PW, N=D, K=S, **gmm)
    (dpa,) = _mm("d_pool_out", [(dya, wa_f)], M=S, N=PW, K=D, tb=True, outs=[_tile_out(F32)], epi=first)
    mid = [W_A, W_B, W_O]
    (datt,), (got_mid,) = _mm("d_att", [(dyb, wb_f)], M=S, N=PW, K=D, tb=True, outs=[_tile_out(BF16)], epi=first,
                              riders=[_px_rider(mid, [g_wa, g_wb, g_wo])])
    sum_mid = pair_sums(mid, [g_wa, g_wb, g_wo], got_mid)
    du, g_wpool4, gscale_p = _pool_bwd(dpa, pooled, wpool_f, pool_scale, S, PW)
    (dq, dk, dv, gq_p, gk_p), ((q_ff1, q_wa, q_wb, q_wo),) = _attn_bwd(
        proj, datt, attf, q_norm_w, k_norm_w, S, H, PW // HEAD_DIM, riders=[_cx_rider([W_FF1] + mid, sum_ff1 + sum_mid)])
    dproj = jnp.concatenate([du, dq, dk, dv, dga, dgb], axis=1)
    (g_win,) = _mm("grad_w_in", [(h, dproj)], M=D, N=IN, K=S, **gmm)
    last = [W_IN, W_POOL]
    g_last = [g_win, g_wpool4.reshape(PW, cg)]
    (dh,), (got_last,) = _mm("d_h", [(dproj, win_f)], M=S, N=D, K=IN, tb=True, outs=[_tile_out(F32)], epi=first,
                             riders=[_px_rider(last, g_last)])
    sum_last = pair_sums(last, g_last, got_last)
    grad_x, dshift1_p, dscale1_p, gn1_p = _norm_mod_bwd("norm1_bwd", dh, x2, dx1, norm1_w, scale1)

    parts = [dshift1_p, dscale1_p, dgate1_p, dshift2_p, dscale2_p, dgate2_p, gn1_p, gn2_p,
             gscale_p.reshape(1, 1, PW), gq_p, gk_p]
    widths = [D] * 8 + [PW, HEAD_DIM, HEAD_DIM]
    used = sum(widths)
    P = -(-used // 1024) * 1024
    packed, loss_part = _pack_partials(parts + [loss_p], widths, P)
    gathered = _dev_allgather("gather_vector_grads", packed.reshape(8, P // 8)).reshape(N_DEV, P)
    small = [(b_ada, m_b_ada, v_b_ada), (norm1_w, m_norm1_w, v_norm1_w), (norm2_w, m_norm2_w, v_norm2_w),
             (pool_scale, m_pool_scale, v_pool_scale), (q_norm_w, m_q_norm_w, v_q_norm_w),
             (k_norm_w, m_k_norm_w, v_k_norm_w)]
    offsets = [(0, 6 * D), (6 * D, D), (7 * D, D), (8 * D, PW), (8 * D + PW, HEAD_DIM), (8 * D + PW + HEAD_DIM, HEAD_DIM)]
    su = _small_update(gathered, offsets, small)
    (g_b, d_b, nm_b, nv_b, g_n1, d_n1, nm_n1, nv_n1, g_n2, d_n2, nm_n2, nv_n2, g_ps, d_ps, nm_ps, nv_ps,
     g_qn, d_qn, nm_qn, nv_qn, g_kn, d_kn, nm_kn, nv_kn) = su
    dmod_sh = lax.dynamic_slice(gathered, (0, chip * A_COLS), (N_DEV, A_COLS))
    (g_ada, d_ada, nm_ada, nv_ada), ((q_win, q_wpool),) = _ada_update(
        sc_all.T, dmod_sh, w_ada[0], m_w_ada[0], v_w_ada[0], riders=[_cx_rider(last, sum_last)])

    from_chips = [q_win, q_wpool, q_wa, q_wb, q_wo, q_ff1, q_ff2[0]]
    halves = [_chip_sum(w, q, c_arr) for w, q in zip(ws, from_chips)]
    grads = _run_rider("grad_sibling_fill", _sf_rider(ws, halves))
    upd = [_adamw("adamw_" + w.name, a, g, m, v) for w, a, g, m, v in zip(ws, w32, grads, m32, v32)]

    loss = 0.5 / D * lax.psum(loss_part[0, 0], ("x", "y", "c"))

    def up(a):
        return a[None]

    def pool4(a):
        return a.reshape(1, N_GROUPS, cg // N_CHIPS, cg)

    (d_win, nm_win, nv_win), (d_wp, nm_wp, nv_wp), (d_wa, nm_wa, nv_wa), (d_wb, nm_wb, nv_wb), \
        (d_wo, nm_wo, nv_wo), (d_f1, nm_f1, nv_f1), (d_f2, nm_f2, nv_f2) = upd
    gr_win, gr_wp, gr_wa, gr_wb, gr_wo, gr_f1, gr_f2 = grads
    return (
        loss, grad_x[None],
        up(g_ada), g_b, g_n1, up(gr_win), g_qn, g_kn, pool4(gr_wp), g_ps, up(gr_wa), up(gr_wb), up(gr_wo), g_n2,
        up(gr_f1), up(gr_f2),
        up(d_ada), d_b, d_n1, up(d_win), d_qn, d_kn, pool4(d_wp), d_ps, up(d_wa), up(d_wb), up(d_wo), d_n2,
        up(d_f1), up(d_f2),
        up(nm_ada), nm_b, nm_n1, up(nm_win), nm_qn, nm_kn, pool4(nm_wp), nm_ps, up(nm_wa), up(nm_wb), up(nm_wo), nm_n2,
        up(nm_f1), up(nm_f2),
        up(nv_ada), nv_b, nv_n1, up(nv_win), nv_qn, nv_kn, pool4(nv_wp), nv_ps, up(nv_wa), up(nv_wb), up(nv_wo), nv_n2,
        up(nv_f1), up(nv_f2),
    )
```

```python
import functools
import math

import jax
import jax.numpy as jnp
from jax import lax
from jax.experimental import pallas as pl
from jax.experimental.pallas import tpu as pltpu

F32 = jnp.float32
BF16 = jnp.bfloat16
MESH = pl.DeviceIdType.MESH
ANY = pl.BlockSpec(memory_space=pl.ANY)

EPS = 1e-6
HEAD_DIM = 128
POOL_WINDOWS = (2, 4, 8, 16)
N_GROUPS = len(POOL_WINDOWS)
N_CHIPS = 4
N_DEV = 8
ADAM_LR, ADAM_B1, ADAM_B2, ADAM_EPS, ADAM_WD, ADAM_STEP = 0.001, 0.9, 0.999, 1e-08, 0.01, 10
VMEM_LIMIT_V7X = 56 * 1024 * 1024
ATT_T = 256
POOL_T = 256


def _pcall(body, **kw):
    return pl.pallas_call(body, **kw)


def _params(sem=None):
    return pltpu.CompilerParams(dimension_semantics=sem, vmem_limit_bytes=VMEM_LIMIT_V7X)


def _tile(n, pref):
    if n <= pref:
        return n
    t = pref
    while n % t:
        t //= 2
    return t


class _Rider:
    def __init__(self, arrays, out_shape, sems, start, finish, aliases=None):
        self.arrays, self.out_shape, self.sems = list(arrays), list(out_shape), list(sems)
        self.start, self.finish, self.aliases = start, finish, aliases or {}


def _ride(name, body, riders, arrays, *, grid, in_specs, out_specs, out_shape, scratch_shapes, sem):
    n_in, n_out, n_scr = len(arrays), len(out_shape), len(scratch_shapes)
    r_arrays = [a for r in riders for a in r.arrays]
    r_outs = [o for r in riders for o in r.out_shape]
    r_sems = [s for r in riders for s in r.sems]
    aliases, off_i, off_o = {}, n_in, n_out
    for r in riders:
        for a, o in r.aliases.items():
            aliases[off_i + a] = off_o + o
        off_i += len(r.arrays)
        off_o += len(r.out_shape)

    def full(*refs):
        p = 0
        groups = []
        for n in (n_in, len(r_arrays), n_out, len(r_outs), n_scr, len(r_sems)):
            groups.append(refs[p:p + n])
            p += n
        ins, rin, outs, rout, scr, rsem = groups

        def each(what):
            a = o = s = 0
            for r in riders:
                getattr(r, what)(rin[a:a + len(r.arrays)], rout[o:o + len(r.out_shape)], rsem[s:s + len(r.sems)])
                a, o, s = a + len(r.arrays), o + len(r.out_shape), s + len(r.sems)

        if riders:
            ids = [pl.program_id(d) for d in range(len(grid))]
            first = functools.reduce(jnp.logical_and, [i == 0 for i in ids])
            last = functools.reduce(jnp.logical_and, [i == g - 1 for i, g in zip(ids, grid)])
            pl.when(first)(lambda: each("start"))
        body(*ins, *outs, *scr)
        if riders:
            pl.when(last)(lambda: each("finish"))

    res = _pcall(
        full, name=name, grid=grid, in_specs=list(in_specs) + [ANY] * len(r_arrays),
        out_specs=list(out_specs) + [ANY] * len(r_outs), out_shape=list(out_shape) + r_outs,
        scratch_shapes=list(scratch_shapes) + r_sems, input_output_aliases=aliases,
        compiler_params=_params(("arbitrary",) * len(grid) if riders else sem),
    )(*arrays, *r_arrays)
    if not riders:
        return res
    main, rest, per = res[:n_out], res[n_out:], []
    for r in riders:
        per.append(rest[:len(r.out_shape)])
        rest = rest[len(r.out_shape):]
    return main, per


def _run_rider(name, rider):
    def body(*refs):
        n_a, n_o = len(rider.arrays), len(rider.out_shape)
        ins, outs, sems = refs[:n_a], refs[n_a:n_a + n_o], refs[n_a + n_o:]
        rider.start(ins, outs, sems)
        rider.finish(ins, outs, sems)

    return _pcall(body, name=name, out_shape=rider.out_shape, in_specs=[ANY] * len(rider.arrays),
                  out_specs=[ANY] * len(rider.out_shape), scratch_shapes=rider.sems,
                  input_output_aliases=rider.aliases)(*rider.arrays)


def _mm(name, pairs, *, M, N, K, ta=False, tb=False, tm=512, tn=1024, tk=1024,
        a_pro=None, b_pro=None, extras=(), outs, epi, riders=(), b_noff=0):
    tm, tn, tk = _tile(M, tm), _tile(N, tn), _tile(K, tk)
    n_i, n_j, n_k = M // tm, N // tn, K // tk
    n_p, n_e = len(pairs), len(extras)
    arrays, in_specs = [], []
    for a, _ in pairs:
        arrays.append(a)
        in_specs.append(pl.BlockSpec((tk, tm), lambda i, j, k: (k, i)) if ta
                        else pl.BlockSpec((tm, tk), lambda i, j, k: (i, k)))
    for _, b in pairs:
        arrays.append(b)
        in_specs.append(pl.BlockSpec((tn, tk), lambda i, j, k: (j + b_noff // tn, k)) if tb
                        else pl.BlockSpec((tk, tn), lambda i, j, k: (k, j + b_noff // tn)))
    for arr, kind, off in extras:
        ob = off // tn
        assert off % tn == 0
        arrays.append(arr)
        if kind == "tile":
            in_specs.append(pl.BlockSpec((tm, tn), lambda i, j, k, ob=ob: (i, j + ob)))
        else:
            in_specs.append(pl.BlockSpec((1, tn), lambda i, j, k, ob=ob: (0, j + ob)))
    out_shape, out_specs = [], []
    for o in outs:
        if o["kind"] == "tile":
            out_shape.append(jax.ShapeDtypeStruct((M, N), o["dtype"]))
            out_specs.append(pl.BlockSpec((tm, tn), lambda i, j, k: (i, j)))
        else:
            out_shape.append(jax.ShapeDtypeStruct((n_i, 1, N), F32))
            out_specs.append(pl.BlockSpec((1, 1, tn), lambda i, j, k: (i, 0, j)))
    dims = (((0 if ta else 1,), (1 if tb else 0,)), ((), ()))

    def body(*refs):
        a_refs, b_refs = refs[:n_p], refs[n_p:2 * n_p]
        e_refs = refs[2 * n_p:2 * n_p + n_e]
        o_refs = refs[2 * n_p + n_e:2 * n_p + n_e + len(outs)]
        acc_refs = refs[2 * n_p + n_e + len(outs):]

        def product(p):
            a, b = a_refs[p][...], b_refs[p][...]
            if a_pro is not None:
                a = a_pro(a)
            if b_pro is not None:
                b = b_pro(b)
            return lax.dot_general(a, b, dims, preferred_element_type=F32)

        def write(accs):
            vals = epi(accs, [e[...] for e in e_refs])
            for o, o_ref, val in zip(outs, o_refs, vals):
                if o["kind"] == "tile":
                    o_ref[...] = val.astype(o_ref.dtype)
                else:
                    o_ref[0] = val

        if n_k == 1:
            write([product(p) for p in range(n_p)])
            return
        k = pl.program_id(2)

        @pl.when(k == 0)
        def _():
            for acc in acc_refs:
                acc[...] = jnp.zeros_like(acc)

        for p in range(n_p):
            acc_refs[p][...] += product(p)

        pl.when(k == n_k - 1)(lambda: write([acc[...] for acc in acc_refs]))

    return _ride(name, body, riders, arrays, grid=(n_i, n_j, n_k), in_specs=in_specs, out_specs=out_specs,
                 out_shape=out_shape, scratch_shapes=[pltpu.VMEM((tm, tn), F32) for _ in pairs] if n_k > 1 else [],
                 sem=("parallel", "parallel", "arbitrary"))


def _tile_out(dtype):
    return {"kind": "tile", "dtype": dtype}


_COLSUM = {"kind": "colsum"}


def _colsum(v):
    return jnp.sum(v, axis=0, keepdims=True)


def _norm_mod(name, x, norm_w, scale, shift):
    S, D = x.shape
    tr = _tile(S, 256)

    def body(x_ref, nw_ref, sc_ref, sh_ref, h_ref):
        xv = x_ref[...]
        r = lax.rsqrt(jnp.mean(xv * xv, axis=-1, keepdims=True) + EPS)
        h_ref[...] = ((xv * r * nw_ref[...]) * (1.0 + sc_ref[...]) + sh_ref[...]).astype(BF16)

    row = pl.BlockSpec((1, D), lambda i: (0, 0))
    til = pl.BlockSpec((tr, D), lambda i: (i, 0))
    return _pcall(body, name=name, grid=(S // tr,), in_specs=[til, row, row, row], out_specs=til,
                  out_shape=jax.ShapeDtypeStruct((S, D), BF16), compiler_params=_params(("parallel",)))(
                      x, norm_w, scale, shift)


def _norm_mod_bwd(name, dh, x, dres, norm_w, scale, gate_o=None):
    S, D = x.shape
    tr = _tile(S, 256)
    n_r = S // tr
    with_gate = gate_o is not None
    dh = list(dh) if isinstance(dh, (list, tuple)) else [dh]
    n_dh = len(dh)

    def body(*refs):
        dh_refs, refs = refs[:n_dh], refs[n_dh:]
        if with_gate:
            x_ref, dres_ref, nw_ref, sc_ref, o_ref, g_ref, dx_ref, p1, p2, p3, do_ref, p4 = refs
        else:
            x_ref, dres_ref, nw_ref, sc_ref, dx_ref, p1, p2, p3 = refs
        dhv = dh_refs[0][...] if n_dh == 1 else jnp.concatenate([r[...] for r in dh_refs], axis=1)
        xv, nw = x_ref[...], nw_ref[...]
        r = lax.rsqrt(jnp.mean(xv * xv, axis=-1, keepdims=True) + EPS)
        xh = xv * r
        p1[0] = _colsum(dhv)
        p2[0] = _colsum(dhv * (xh * nw))
        dn = dhv * (1.0 + sc_ref[...])
        p3[0] = _colsum(dn * xh)
        dxh = dn * nw
        dx = dres_ref[...] + r * (dxh - xh * jnp.mean(dxh * xh, axis=-1, keepdims=True))
        dx_ref[...] = dx
        if with_gate:
            do_ref[...] = (dx * g_ref[...]).astype(BF16)
            p4[0] = _colsum(dx * o_ref[...].astype(F32))

    row = pl.BlockSpec((1, D), lambda i: (0, 0))
    til = pl.BlockSpec((tr, D), lambda i: (i, 0))
    part = pl.BlockSpec((1, 1, D), lambda i: (i, 0, 0))
    part_shape = jax.ShapeDtypeStruct((n_r, 1, D), F32)
    in_specs = [pl.BlockSpec((tr, D // n_dh), lambda i: (i, 0))] * n_dh + [til, til, row, row]
    arrays = dh + [x, dres, norm_w, scale]
    out_specs = [til, part, part, part]
    out_shape = [jax.ShapeDtypeStruct((S, D), F32), part_shape, part_shape, part_shape]
    if with_gate:
        in_specs += [til, row]
        arrays += list(gate_o)
        out_specs += [til, part]
        out_shape += [jax.ShapeDtypeStruct((S, D), BF16), part_shape]
    return _pcall(body, name=name, grid=(n_r,), in_specs=in_specs, out_specs=out_specs, out_shape=out_shape,
                  compiler_params=_params(("parallel",)))(*arrays)


def _pool_w_specs(rows, cg):
    return [pl.BlockSpec((rows, cg), lambda g, j=j: (N_GROUPS * j + g, 0)) for j in range(N_CHIPS)]


def _pool_fwd(proj, wp_full, pool_scale, S, PW):
    cg = PW // N_GROUPS
    rows = cg // N_CHIPS
    T = _tile(S, POOL_T)
    n_t = S // T

    def body(u_ref, w0, w1, w2, w3, ps_ref, pooled_ref, pa_ref):
        g = pl.program_id(0)
        win = jnp.left_shift(2, g)
        w = jnp.concatenate([w0[...], w1[...], w2[...], w3[...]], axis=0)
        t_i = lax.broadcasted_iota(jnp.int32, (T, T), 0)
        j_i = lax.broadcasted_iota(jnp.int32, (T, T), 1)
        b_cur = ((j_i <= t_i) & (j_i > t_i - win)).astype(BF16)
        b_prev = (j_i - T > t_i - win).astype(BF16)
        row = lax.broadcasted_iota(jnp.int32, (T, 1), 0)
        for r in range(n_t):
            cur = u_ref[r * T:(r + 1) * T, :]
            ws = jnp.dot(b_cur, cur, preferred_element_type=F32)
            if r > 0:
                ws += jnp.dot(b_prev, u_ref[(r - 1) * T:r * T, :], preferred_element_type=F32)
            count = jnp.minimum(row + (r * T + 1), win).astype(F32)
            pooled = (ws / count - cur.astype(F32)).astype(BF16)
            pooled_ref[r * T:(r + 1) * T, :] = pooled
            mixed = jnp.dot(pooled, w, preferred_element_type=F32)
            pa_ref[r * T:(r + 1) * T, :] = (mixed * ps_ref[...]).astype(BF16)

    col = pl.BlockSpec((S, cg), lambda g: (0, g))
    return _pcall(
        body, name="pool_fwd", grid=(N_GROUPS,),
        in_specs=[col] + _pool_w_specs(rows, cg) + [pl.BlockSpec((1, cg), lambda g: (0, g))],
        out_specs=[col, col],
        out_shape=[jax.ShapeDtypeStruct((S, PW), BF16), jax.ShapeDtypeStruct((S, PW), BF16)],
        compiler_params=_params(("parallel",)),
    )(proj, wp_full, wp_full, wp_full, wp_full, pool_scale)


def _pool_bwd(dpa, pooled, wp_full, pool_scale, S, PW):
    cg = PW // N_GROUPS
    rows = cg // N_CHIPS
    T = _tile(S, POOL_T)
    n_t = S // T

    def body(dpa_ref, pooled_ref, w0, w1, w2, w3, ps_ref, du_ref, gw_ref, gs_ref, dp_s, dpc_s, dmx_s):
        g = pl.program_id(0)
        win = jnp.left_shift(2, g)
        w = jnp.concatenate([w0[...], w1[...], w2[...], w3[...]], axis=0)
        row = lax.broadcasted_iota(jnp.int32, (T, 1), 0)
        gs = jnp.zeros((1, cg), F32)
        for r in range(n_t):
            sl = slice(r * T, (r + 1) * T)
            mixed = jnp.dot(pooled_ref[sl, :], w, preferred_element_type=F32)
            dpa_t = dpa_ref[sl, :]
            gs += _colsum(dpa_t * mixed)
            dmx = (dpa_t * ps_ref[...]).astype(BF16)
            dmx_s[sl, :] = dmx
            dpo = lax.dot_general(dmx, w, (((1,), (1,)), ((), ())), preferred_element_type=F32)
            dp_s[sl, :] = dpo
            count = jnp.minimum(row + (r * T + 1), win).astype(F32)
            dpc_s[sl, :] = (dpo / count).astype(BF16)
        gs_ref[...] = gs
        gw = lax.dot_general(pooled_ref[...], dmx_s[...], (((0,), (0,)), ((), ())), preferred_element_type=F32)
        for j in range(N_CHIPS):
            gw_ref[j, 0] = gw[j * rows:(j + 1) * rows, :].astype(BF16)
        j_i = lax.broadcasted_iota(jnp.int32, (T, T), 0)
        t_i = lax.broadcasted_iota(jnp.int32, (T, T), 1)
        b_cur = ((t_i >= j_i) & (t_i < j_i + win)).astype(BF16)
        b_next = (t_i + T < j_i + win).astype(BF16)
        for r in range(n_t):
            sl = slice(r * T, (r + 1) * T)
            acc = jnp.dot(b_cur, dpc_s[sl, :], preferred_element_type=F32)
            if r + 1 < n_t:
                acc += jnp.dot(b_next, dpc_s[(r + 1) * T:(r + 2) * T, :], preferred_element_type=F32)
            du_ref[sl, :] = (acc - dp_s[sl, :]).astype(BF16)

    col = pl.BlockSpec((S, cg), lambda g: (0, g))
    return _pcall(
        body, name="pool_bwd", grid=(N_GROUPS,),
        in_specs=[col, col] + _pool_w_specs(rows, cg) + [pl.BlockSpec((1, cg), lambda g: (0, g))],
        out_specs=[col, pl.BlockSpec((N_CHIPS, 1, rows, cg), lambda g: (0, g, 0, 0)),
                   pl.BlockSpec((1, cg), lambda g: (0, g))],
        out_shape=[jax.ShapeDtypeStruct((S, PW), BF16),
                   jax.ShapeDtypeStruct((N_CHIPS, N_GROUPS, rows, cg), BF16),
                   jax.ShapeDtypeStruct((1, PW), F32)],
        scratch_shapes=[pltpu.VMEM((S, cg), F32), pltpu.VMEM((S, cg), BF16), pltpu.VMEM((S, cg), BF16)],
        compiler_params=_params(("parallel",)),
    )(dpa, pooled, wp_full, wp_full, wp_full, wp_full, pool_scale)


_NT = (((1,), (1,)), ((), ()))
_TN = (((0,), (0,)), ((), ()))


def _split_dot(v, tri):
    hi = v.astype(BF16)
    lo = (v - hi.astype(F32)).astype(BF16)
    return jnp.dot(hi, tri, preferred_element_type=F32) + jnp.dot(lo, tri, preferred_element_type=F32)


def _sb_scores(q_i, k_j, tri_l, masked):
    tq, tk = q_i.shape[0], k_j.shape[0]
    s = lax.dot_general(q_i, k_j, _NT, preferred_element_type=F32) * (1.0 / math.sqrt(HEAD_DIM))
    lp = jnp.log(1.0 + jnp.exp(-jnp.abs(s)))
    l = -jnp.maximum(s, 0.0) - lp
    lb = l + s
    mask = None
    if masked:
        mask = lax.broadcasted_iota(jnp.int32, (tq, tk), 0) > lax.broadcasted_iota(jnp.int32, (tq, tk), 1)
        l = jnp.where(mask, l, 0.0)
    return l, lb, lb + _split_dot(l, tri_l), mask


def _sb_weights(t, carry_l, mask):
    a = jnp.exp(t + carry_l)
    return a if mask is None else jnp.where(mask, a, 0.0)


def _rowsum(v):
    return jnp.sum(v, axis=1, keepdims=True)


def _qk_norm(x_ref, w_ref):
    xv = x_ref[...].astype(F32)
    r = lax.rsqrt(jnp.mean(xv * xv, axis=-1, keepdims=True) + EPS)
    return xv * r, r


def _attn_fwd(proj, q_norm_w, k_norm_w, S, H, q_off, riders=()):
    t = _tile(S, ATT_T)
    n_q = S // t

    def body(q_ref, k_ref, v_ref, qw_ref, kw_ref, att_ref, attf_ref, qn_s, kn_s):
        qh, _ = _qk_norm(q_ref, qw_ref)
        qn_s[...] = (qh * qw_ref[...]).astype(BF16)
        kh, _ = _qk_norm(k_ref, kw_ref)
        kn_s[...] = (kh * kw_ref[...]).astype(BF16)
        tri_l = (lax.broadcasted_iota(jnp.int32, (t, t), 0) > lax.broadcasted_iota(jnp.int32, (t, t), 1)).astype(BF16)

        def rows(j):
            return pl.ds(pl.multiple_of(j * t, t), t)

        def q_step(i, _):
            q_i = qn_s[rows(i), :]

            def av(a, j):
                return jnp.dot(a.astype(BF16), v_ref[rows(j), :], preferred_element_type=F32)

            l, _, tt, mask = _sb_scores(q_i, kn_s[rows(i), :], tri_l, True)
            acc = av(_sb_weights(tt, 0.0, mask), i)
            carry = _rowsum(l)

            def single(_, c):
                carry, acc = c
                l, _, tt, _ = _sb_scores(q_i, kn_s[rows(i - 1), :], tri_l, False)
                return carry + _rowsum(l), acc + av(_sb_weights(tt, carry, None), i - 1)

            carry, acc = lax.fori_loop(0, i % 2, single, (carry, acc))
            top = i - 1 - i % 2

            def pair(p, c):
                carry, acc = c
                j0 = top - 2 * p
                l0, _, t0, _ = _sb_scores(q_i, kn_s[rows(j0), :], tri_l, False)
                l1, _, t1, _ = _sb_scores(q_i, kn_s[rows(j0 - 1), :], tri_l, False)
                mid = carry + _rowsum(l0)
                acc = acc + av(_sb_weights(t0, carry, None), j0) + av(_sb_weights(t1, mid, None), j0 - 1)
                return mid + _rowsum(l1), acc

            _, acc = lax.fori_loop(0, i // 2, pair, (carry, acc))
            att_ref[rows(i), :] = acc.astype(BF16)
            attf_ref[rows(i), :] = acc
            return 0

        lax.fori_loop(0, n_q, q_step, 0)

    def col(off):
        return pl.BlockSpec((S, HEAD_DIM), lambda h, off=off: (0, off + h))

    wspec = pl.BlockSpec((1, HEAD_DIM), lambda h: (0, 0))
    return _ride(
        "attn_fwd", body, riders, [proj, proj, proj, q_norm_w, k_norm_w], grid=(H,),
        in_specs=[col(q_off), col(q_off + H), col(q_off + 2 * H), wspec, wspec],
        out_specs=[col(0), col(0)],
        out_shape=[jax.ShapeDtypeStruct((S, H * HEAD_DIM), BF16), jax.ShapeDtypeStruct((S, H * HEAD_DIM), F32)],
        scratch_shapes=[pltpu.VMEM((S, HEAD_DIM), BF16), pltpu.VMEM((S, HEAD_DIM), BF16)],
        sem=("parallel",))


def _attn_bwd(proj, datt, attf, q_norm_w, k_norm_w, S, H, q_off, riders=()):
    t = _tile(S, ATT_T)
    n_q = S // t
    scale = 1.0 / math.sqrt(HEAD_DIM)

    def body(q_ref, k_ref, v_ref, do_ref, o_ref, qw_ref, kw_ref, dq_ref, dk_ref, dv_ref, gq_ref, gk_ref,
             qn_s, kn_s, dk_s, dv_s, gq_s):
        qw, kw = qw_ref[...], kw_ref[...]
        qh, _ = _qk_norm(q_ref, qw_ref)
        qn_s[...] = (qh * qw).astype(BF16)
        kh, _ = _qk_norm(k_ref, kw_ref)
        kn_s[...] = (kh * kw).astype(BF16)
        dk_s[...] = jnp.zeros_like(dk_s)
        dv_s[...] = jnp.zeros_like(dv_s)
        gq_s[...] = jnp.zeros_like(gq_s)
        r_i = lax.broadcasted_iota(jnp.int32, (t, t), 0)
        c_i = lax.broadcasted_iota(jnp.int32, (t, t), 1)
        tri_l = (r_i > c_i).astype(BF16)
        tri_e = (r_i >= c_i).astype(BF16)

        def rows(j):
            return pl.ds(pl.multiple_of(j * t, t), t)

        def q_step(i, _):
            q_i = qn_s[rows(i), :]
            do_i = do_ref[rows(i), :]
            d_i = _rowsum(do_i.astype(F32) * o_ref[rows(i), :])

            def scores(j, masked):
                k_j = kn_s[rows(j), :]
                l, lb, tt, mask = _sb_scores(q_i, k_j, tri_l, masked)
                da = lax.dot_general(do_i, v_ref[rows(j), :], _NT, preferred_element_type=F32)
                return k_j, l, lb, tt, mask, da

            def grads(j, sc, carry_l, carry_e, dq_acc):
                k_j, l, lb, tt, mask, da = sc
                a_bf = _sb_weights(tt, carry_l, mask).astype(BF16)
                e = da * a_bf.astype(F32)
                p = d_i - (_split_dot(e, tri_e) + carry_e)
                sig = jnp.exp(lb)
                dz = e * (1.0 - sig) - p * sig
                if mask is not None:
                    dz = jnp.where(mask, dz, 0.0)
                dz = (dz * scale).astype(BF16)
                dk_s[rows(j), :] += lax.dot_general(dz, q_i, _TN, preferred_element_type=F32)
                dv_s[rows(j), :] += lax.dot_general(a_bf, do_i, _TN, preferred_element_type=F32)
                return (carry_l + _rowsum(l), carry_e + _rowsum(e),
                        dq_acc + jnp.dot(dz, k_j, preferred_element_type=F32))

            c = grads(i, scores(i, True), 0.0, 0.0, jnp.zeros((t, HEAD_DIM), F32))
            c = lax.fori_loop(0, i % 2, lambda _, c: grads(i - 1, scores(i - 1, False), *c), c)
            top = i - 1 - i % 2

            def pair(p, c):
                j0 = top - 2 * p
                s0, s1 = scores(j0, False), scores(j0 - 1, False)
                return grads(j0 - 1, s1, *grads(j0, s0, *c))

            _, _, dqn = lax.fori_loop(0, i // 2, pair, c)
            qv = q_ref[rows(i), :].astype(F32)
            r = lax.rsqrt(jnp.mean(qv * qv, axis=-1, keepdims=True) + EPS)
            xh = qv * r
            gq_s[...] += _colsum(dqn * xh)
            dxh = dqn * qw
            dq_ref[rows(i), :] = (r * (dxh - xh * jnp.mean(dxh * xh, axis=-1, keepdims=True))).astype(BF16)
            return 0

        lax.fori_loop(0, n_q, q_step, 0)
        gq_ref[0] = gq_s[...]
        kh, rk = _qk_norm(k_ref, kw_ref)
        dkn = dk_s[...]
        gk_ref[0] = _colsum(dkn * kh)
        dxh = dkn * kw
        dk_ref[...] = (rk * (dxh - kh * jnp.mean(dxh * kh, axis=-1, keepdims=True))).astype(BF16)
        dv_ref[...] = dv_s[...].astype(BF16)

    def col(off):
        return pl.BlockSpec((S, HEAD_DIM), lambda h, off=off: (0, off + h))

    wspec = pl.BlockSpec((1, HEAD_DIM), lambda h: (0, 0))
    gspec = pl.BlockSpec((1, 1, HEAD_DIM), lambda h: (h, 0, 0))
    act = jax.ShapeDtypeStruct((S, H * HEAD_DIM), BF16)
    gsh = jax.ShapeDtypeStruct((H, 1, HEAD_DIM), F32)
    return _ride(
        "attn_bwd", body, riders, [proj, proj, proj, datt, attf, q_norm_w, k_norm_w], grid=(H,),
        in_specs=[col(q_off), col(q_off + H), col(q_off + 2 * H), col(0), col(0), wspec, wspec],
        out_specs=[col(0), col(0), col(0), gspec, gspec],
        out_shape=[act, act, act, gsh, gsh],
        scratch_shapes=[pltpu.VMEM((S, HEAD_DIM), BF16), pltpu.VMEM((S, HEAD_DIM), BF16),
                        pltpu.VMEM((S, HEAD_DIM), F32), pltpu.VMEM((S, HEAD_DIM), F32),
                        pltpu.VMEM((1, HEAD_DIM), F32)],
        sem=("parallel",))


def _place():
    x, y, c = lax.axis_index("x"), lax.axis_index("y"), lax.axis_index("c")
    chips = [(1 - x, y), (x, 1 - y), (1 - x, 1 - y)]
    return x, y, c, chips


def _dev_allgather(name, v):
    m_per, n = v.shape

    def body(x_ref, out_ref, send_sems, recv_sems, local_sem):
        x, y, c, chips = _place()
        me, sibling = (x, y, c), (x, y, 1 - c)

        def rows(px, py, pc):
            return out_ref.at[pl.ds((4 * px + 2 * py + pc) * m_per, m_per), :]

        def copy(k, block, to, src=None):
            return pltpu.make_async_remote_copy(
                src_ref=rows(*block) if src is None else src, dst_ref=rows(*block),
                send_sem=send_sems.at[k], recv_sem=recv_sems.at[k], device_id=to, device_id_type=MESH)

        mine = pltpu.make_async_copy(x_ref, rows(*me), local_sem)
        mine.start()
        first = [copy(0, me, sibling, src=x_ref)]
        first += [copy(1 + j, me, (*chip, c), src=x_ref) for j, chip in enumerate(chips)]
        for cp in first:
            cp.start()
        passed = [copy(4 + j, (*chip, c), sibling) for j, chip in enumerate(chips)]
        for j, chip in enumerate(chips):
            copy(1 + j, (*chip, c), me).wait_recv()
            passed[j].start()
        copy(0, sibling, me).wait_recv()
        for j, chip in enumerate(chips):
            copy(4 + j, (*chip, 1 - c), me).wait_recv()
        for cp in first + passed:
            cp.wait_send()
        mine.wait()

    return _pcall(
        body, name=name, out_shape=jax.ShapeDtypeStruct((N_DEV * m_per, n), v.dtype),
        in_specs=[pl.BlockSpec(memory_space=pltpu.VMEM)], out_specs=pl.BlockSpec(memory_space=pltpu.VMEM),
        scratch_shapes=[pltpu.SemaphoreType.DMA((7,)), pltpu.SemaphoreType.DMA((7,)), pltpu.SemaphoreType.DMA],
        compiler_params=pltpu.CompilerParams(vmem_limit_bytes=VMEM_LIMIT_V7X),
    )(v)


class _W:
    def __init__(self, name, kind, R, C):
        self.name, self.kind, self.R, self.C = name, kind, R, C

    @property
    def shard_shape(self):
        return (self.R, self.C // N_CHIPS) if self.kind == "col" else (self.R // N_CHIPS, self.C)

    @property
    def half_rows(self):
        return self.shard_shape[0] // 2

    def shard_half(self, ref, half):
        return ref.at[pl.ds(half * self.half_rows, self.half_rows), :]

    def region(self, full_ref, chip, half):
        hr = self.half_rows
        if self.kind == "col":
            cw = self.C // N_CHIPS
            return full_ref.at[pl.ds(half * hr, hr), pl.ds(chip * cw, cw)]
        return full_ref.at[pl.ds(chip * (2 * hr) + half * hr, hr), :]

    def region_both(self, full_ref, chip):
        hr = self.half_rows
        if self.kind == "col":
            cw = self.C // N_CHIPS
            return full_ref.at[:, pl.ds(chip * cw, cw)]
        return full_ref.at[pl.ds(chip * (2 * hr), 2 * hr), :]


def _ag_rider(ws, shards):
    n_w = len(ws)

    def parts(sh, full, sems):
        send_sems, recv_sems, local_sems = sems
        x, y, c, chips = _place()
        my_chip = 2 * x + y

        def direct(i, k, recv=False):
            w, chip = ws[i], chips[k]
            src_chip = (2 * chip[0] + chip[1]) if recv else my_chip
            dst = w.region(full[i], src_chip, c)
            return pltpu.make_async_remote_copy(
                src_ref=dst if recv else w.shard_half(sh[i], c), dst_ref=dst,
                send_sem=send_sems.at[6 * i + k], recv_sem=recv_sems.at[6 * i + k],
                device_id=(*chip, c), device_id_type=MESH)

        def passed(i, k, half):
            chip = chips[k]
            reg = ws[i].region(full[i], 2 * chip[0] + chip[1], half)
            return pltpu.make_async_remote_copy(
                src_ref=reg, dst_ref=reg, send_sem=send_sems.at[6 * i + 3 + k], recv_sem=recv_sems.at[6 * i + 3 + k],
                device_id=(x, y, 1 - c), device_id_type=MESH)

        def mine(i):
            return pltpu.make_async_copy(sh[i], ws[i].region_both(full[i], my_chip), local_sems.at[i])

        return c, direct, passed, mine

    both = [(i, k) for i in range(n_w) for k in range(N_CHIPS - 1)]

    def start(sh, full, sems):
        _, direct, _, mine = parts(sh, full, sems)
        for i, k in both:
            direct(i, k).start()
        for i in range(n_w):
            mine(i).start()

    def finish(sh, full, sems):
        c, direct, passed, mine = parts(sh, full, sems)
        for i, k in both:
            direct(i, k, recv=True).wait_recv()
            passed(i, k, c).start()
        for i, k in both:
            passed(i, k, 1 - c).wait_recv()
        for i, k in both:
            direct(i, k).wait_send()
            passed(i, k, c).wait_send()
        for i in range(n_w):
            mine(i).wait()

    return _Rider(shards, [jax.ShapeDtypeStruct((w.R, w.C), BF16) for w in ws],
                  [pltpu.SemaphoreType.DMA((6 * n_w,)), pltpu.SemaphoreType.DMA((6 * n_w,)),
                   pltpu.SemaphoreType.DMA((n_w,))], start, finish)


def _half_view(w, g):
    return g if w.kind == "col" else g.reshape(N_CHIPS, w.R // N_CHIPS, w.C)


def _px_rider(ws, grads):
    n_w = len(ws)

    def copies(g, got, sems):
        send_sems, recv_sems = sems
        x, y, c, _ = _place()

        def half_all(w, ref, half):
            hr = w.half_rows
            if w.kind == "col":
                return ref.at[pl.ds(half * hr, hr), :]
            return ref.at[:, pl.ds(half * hr, hr), :]

        return [pltpu.make_async_remote_copy(
            src_ref=half_all(w, g[i], 1 - c), dst_ref=got[i], send_sem=send_sems.at[i], recv_sem=recv_sems.at[i],
            device_id=(x, y, 1 - c), device_id_type=MESH) for i, w in enumerate(ws)]

    def start(g, got, sems):
        for cp in copies(g, got, sems):
            cp.start()

    def finish(g, got, sems):
        for cp in copies(g, got, sems):
            cp.wait_recv()
            cp.wait_send()

    def got_shape(w):
        hr = w.half_rows
        return (hr, w.C) if w.kind == "col" else (N_CHIPS, hr, w.C)

    return _Rider([_half_view(w, g) for w, g in zip(ws, grads)],
                  [jax.ShapeDtypeStruct(got_shape(w), BF16) for w in ws],
                  [pltpu.SemaphoreType.DMA((n_w,)), pltpu.SemaphoreType.DMA((n_w,))], start, finish)


def _pair_sum(w, g, got, c_arr):
    hr = w.half_rows
    if w.kind == "col":
        tr, tc = _tile(hr, 512), _tile(w.C, 2048)
        n_r = hr // tr
        grid = (n_r, w.C // tc)
        g_spec = pl.BlockSpec((tr, tc), lambda i, j, c: (c[0] * n_r + i, j))
        o_spec = pl.BlockSpec((tr, tc), lambda i, j, c: (i, j))
    else:
        tr = _tile(hr, 512)
        n_r = hr // tr
        grid = (N_CHIPS, n_r)
        g_spec = pl.BlockSpec((1, tr, w.C), lambda s, i, c: (s, c[0] * n_r + i, 0))
        o_spec = pl.BlockSpec((1, tr, w.C), lambda s, i, c: (s, i, 0))

    def body(c_ref, g_ref, got_ref, out_ref):
        out_ref[...] = (g_ref[...].astype(F32) + got_ref[...].astype(F32)).astype(BF16)

    return _pcall(
        body, name="grad_pair_sum_" + w.name, out_shape=jax.ShapeDtypeStruct(got.shape, BF16),
        grid_spec=pltpu.PrefetchScalarGridSpec(num_scalar_prefetch=1, grid=grid, in_specs=[g_spec, o_spec],
                                               out_specs=o_spec),
        compiler_params=_params(("parallel", "parallel")),
    )(c_arr, _half_view(w, g), got)


def _cx_rider(ws, sums, part=(0, 1), q_in=None):
    n_w = len(ws)

    def parts(p, q, sems):
        send_sems, recv_sems, local_sems = sems
        x, y, c, chips = _place()
        my_chip = 2 * x + y

        def rows(w, ref):
            nr = w.half_rows // part[1]
            return ref.at[pl.ds(part[0] * nr, nr), :]

        def piece(w, ref, chip):
            if w.kind == "col":
                cw = w.C // N_CHIPS
                return rows(w, ref.at[:, pl.ds(chip * cw, cw)])
            return rows(w, ref.at[chip])

        def copy(i, k, recv=False):
            chip = chips[k]
            to_chip = 2 * chip[0] + chip[1]
            return pltpu.make_async_remote_copy(
                src_ref=piece(ws[i], p[i], to_chip), dst_ref=rows(ws[i], q[i].at[to_chip if recv else my_chip]),
                send_sem=send_sems.at[3 * i + k], recv_sem=recv_sems.at[3 * i + k],
                device_id=(*chip, c), device_id_type=MESH)

        def mine(i):
            return pltpu.make_async_copy(piece(ws[i], p[i], my_chip), rows(ws[i], q[i].at[my_chip]), local_sems.at[i])

        return copy, mine

    both = [(i, k) for i in range(n_w) for k in range(N_CHIPS - 1)]

    def start(p, q, sems):
        copy, mine = parts(p, q, sems)
        for i, k in both:
            copy(i, k).start()
        for i in range(n_w):
            mine(i).start()

    def finish(p, q, sems):
        copy, mine = parts(p, q, sems)
        for i, k in both:
            copy(i, k, recv=True).wait_recv()
        for i, k in both:
            copy(i, k).wait_send()
        for i in range(n_w):
            mine(i).wait()

    return _Rider(list(sums) + list(q_in or []),
                  [jax.ShapeDtypeStruct((N_CHIPS, w.half_rows, w.shard_shape[1]), BF16) for w in ws],
                  [pltpu.SemaphoreType.DMA((3 * n_w,)), pltpu.SemaphoreType.DMA((3 * n_w,)),
                   pltpu.SemaphoreType.DMA((n_w,))], start, finish,
                  aliases={n_w + i: i for i in range(n_w)} if q_in else None)


def _chip_sum(w, q, c_arr):
    hr, cols = w.half_rows, w.shard_shape[1]
    tr, tc = _tile(hr, 512), _tile(cols, 2048)
    n_r = hr // tr

    def body(c_ref, q0, q1, q2, q3, out_ref):
        out_ref[...] = ((q0[0].astype(F32) + q1[0].astype(F32)) + q2[0].astype(F32)) + q3[0].astype(F32)

    q_specs = [pl.BlockSpec((1, tr, tc), lambda i, j, c, s=s: (s, i, j)) for s in range(N_CHIPS)]
    return _pcall(
        body, name="grad_chip_sum_" + w.name, out_shape=jax.ShapeDtypeStruct(w.shard_shape, F32),
        grid_spec=pltpu.PrefetchScalarGridSpec(
            num_scalar_prefetch=1, grid=(n_r, cols // tc), in_specs=q_specs,
            out_specs=pl.BlockSpec((tr, tc), lambda i, j, c: (c[0] * n_r + i, j))),
        compiler_params=_params(("parallel", "parallel")),
    )(c_arr, q, q, q, q)


def _sf_rider(ws, grads):
    n_w = len(ws)

    def copy(g, sems, i, half):
        send_sems, recv_sems = sems
        x, y, c, _ = _place()
        h = c if half == "mine" else 1 - c
        reg = ws[i].shard_half(g[i], h)
        return pltpu.make_async_remote_copy(src_ref=reg, dst_ref=reg, send_sem=send_sems.at[i], recv_sem=recv_sems.at[i],
                                            device_id=(x, y, 1 - c), device_id_type=MESH)

    def start(_, g, sems):
        for i in range(n_w):
            copy(g, sems, i, "mine").start()

    def finish(_, g, sems):
        for i in range(n_w):
            copy(g, sems, i, "other").wait_recv()
            copy(g, sems, i, "mine").wait_send()

    return _Rider(grads, [jax.ShapeDtypeStruct(w.shard_shape, F32) for w in ws],
                  [pltpu.SemaphoreType.DMA((n_w,)), pltpu.SemaphoreType.DMA((n_w,))], start, finish,
                  aliases={i: i for i in range(n_w)})


def _adamw_math(w, g, m, v):
    m = ADAM_B1 * m + (1.0 - ADAM_B1) * g
    v = ADAM_B2 * v + (1.0 - ADAM_B2) * (g * g)
    m_hat = m / (1.0 - ADAM_B1 ** ADAM_STEP)
    v_hat = v / (1.0 - ADAM_B2 ** ADAM_STEP)
    delta = -ADAM_LR * (m_hat / (jnp.sqrt(v_hat) + ADAM_EPS) + ADAM_WD * w)
    return delta, m, v


def _adamw(name, w, g, m, v):
    R, C = w.shape
    tr, tc = _tile(R, 256), _tile(C, 2048)

    def body(w_ref, g_ref, m_ref, v_ref, d_out, m_out, v_out):
        d_out[...], m_out[...], v_out[...] = _adamw_math(w_ref[...], g_ref[...], m_ref[...], v_ref[...])

    spec = pl.BlockSpec((tr, tc), lambda i, j: (i, j))
    sh = jax.ShapeDtypeStruct((R, C), F32)
    return _pcall(body, name=name, grid=(R // tr, C // tc), in_specs=[spec] * 4, out_specs=[spec] * 3,
                  out_shape=[sh, sh, sh], compiler_params=_params(("parallel", "parallel")))(w, g, m, v)


def _ada_update(sct, dmod_sh, w, m, v, riders=()):
    R, C = w.shape
    tr, tc = _tile(R, 256), _tile(C, 1024)

    def body(s_ref, d_ref, w_ref, m_ref, v_ref, g_out, d_out, m_out, v_out):
        s, d = s_ref[...], d_ref[...]
        g = s[:, 0:1] * d[0:1, :]
        for b in range(1, N_DEV):
            g += s[:, b:b + 1] * d[b:b + 1, :]
        g_out[...] = g
        d_out[...], m_out[...], v_out[...] = _adamw_math(w_ref[...], g, m_ref[...], v_ref[...])

    spec = pl.BlockSpec((tr, tc), lambda i, j: (i, j))
    sh = jax.ShapeDtypeStruct((R, C), F32)
    return _ride(
        "ada_update", body, riders, [sct, dmod_sh, w, m, v], grid=(R // tr, C // tc),
        in_specs=[pl.BlockSpec((tr, N_DEV), lambda i, j: (i, 0)), pl.BlockSpec((N_DEV, tc), lambda i, j: (0, j)),
                  spec, spec, spec],
        out_specs=[spec] * 4, out_shape=[sh] * 4, scratch_shapes=[], sem=("parallel", "parallel"))


def _cast_bf16(name, w):
    R, C = w.shape
    tr, tc = _tile(R, 512), _tile(C, 2048)

    def body(w_ref, o_ref):
        o_ref[...] = w_ref[...].astype(BF16)

    spec = pl.BlockSpec((tr, tc), lambda i, j: (i, j))
    return _pcall(body, name=name, grid=(R // tr, C // tc), in_specs=[spec], out_specs=spec,
                  out_shape=jax.ShapeDtypeStruct((R, C), BF16), compiler_params=_params(("parallel", "parallel")))(w)


def _silu_rows(c_row):
    D = c_row.shape[1]

    def body(c_ref, o_ref):
        cv = c_ref[...]
        o_ref[...] = cv * jax.nn.sigmoid(cv)

    return _pcall(body, name="silu_c", out_shape=jax.ShapeDtypeStruct((1, D), F32))(c_row)


def _pack_partials(parts, widths, total):
    n = len(widths)

    def body(*refs):
        loss_p, out_ref, loss_ref = refs[n], refs[n + 1], refs[n + 2]
        off = 0
        for ref, wd in zip(refs[:n], widths):
            out_ref[:, off:off + wd] = jnp.sum(ref[...], axis=0)
            off += wd
        if off < total:
            out_ref[:, off:total] = jnp.zeros((1, total - off), F32)
        loss_ref[...] = jnp.sum(jnp.sum(loss_p[...], axis=0), axis=1, keepdims=True)

    return _pcall(body, name="pack_partials",
                  out_shape=[jax.ShapeDtypeStruct((1, total), F32), jax.ShapeDtypeStruct((1, 1), F32)])(*parts)


def _small_update(gathered, offsets, params):
    n_p = len(params)

    def body(*refs):
        g_ref = refs[0]
        prm = refs[1:1 + 3 * n_p]
        outs = refs[1 + 3 * n_p:]
        for i, (off, wd) in enumerate(offsets):
            blk = g_ref[:, off:off + wd]
            g = blk[0:1, :]
            for b in range(1, N_DEV):
                g = g + blk[b:b + 1, :]
            w, m, v = prm[3 * i][...], prm[3 * i + 1][...], prm[3 * i + 2][...]
            outs[4 * i][...] = g
            outs[4 * i + 1][...], outs[4 * i + 2][...], outs[4 * i + 3][...] = _adamw_math(w, g, m, v)

    flat = [a for t in params for a in t]
    out_shape = [jax.ShapeDtypeStruct(t[0].shape, F32) for t in params for _ in range(4)]
    return _pcall(body, name="small_update", out_shape=out_shape)(gathered, *flat)


def kernel(x, c, w_ada, b_ada, norm1_w, w_in, q_norm_w, k_norm_w, w_pool, pool_scale, w_a_up, w_b_up, w_o, norm2_w, w_ff1, w_ff2, loss_target, m_w_ada, m_b_ada, m_norm1_w, m_w_in, m_q_norm_w, m_k_norm_w, m_w_pool, m_pool_scale, m_w_a_up, m_w_b_up, m_w_o, m_norm2_w, m_w_ff1, m_w_ff2, v_w_ada, v_b_ada, v_norm1_w, v_w_in, v_q_norm_w, v_k_norm_w, v_w_pool, v_pool_scale, v_w_a_up, v_w_b_up, v_w_o, v_norm2_w, v_w_ff1, v_w_ff2):
    _, S, D = x.shape
    PW = D // 2
    H = PW // HEAD_DIM
    cg = PW // N_GROUPS
    IN = w_in.shape[2] * N_CHIPS
    FF = w_ff1.shape[2] * N_CHIPS
    A_COLS = w_ada.shape[2]
    xi, yi, ci = lax.axis_index("x"), lax.axis_index("y"), lax.axis_index("c")
    chip = 2 * xi + yi
    dev = 2 * chip + ci
    c_arr = jnp.reshape(ci, (1,)).astype(jnp.int32)
    x2, tgt = x[0], loss_target[0]

    ws = [_W("w_in", "col", D, IN), _W("w_pool", "row", PW, cg), _W("w_a_up", "col", PW, D),
          _W("w_b_up", "col", PW, D), _W("w_o", "row", D, D), _W("w_ff1", "col", D, FF), _W("w_ff2", "row", FF, D)]
    w32 = [w_in[0], w_pool[0].reshape(cg, cg), w_a_up[0], w_b_up[0], w_o[0], w_ff1[0], w_ff2[0]]
    m32 = [m_w_in[0], m_w_pool[0].reshape(cg, cg), m_w_a_up[0], m_w_b_up[0], m_w_o[0], m_w_ff1[0], m_w_ff2[0]]
    v32 = [v_w_in[0], v_w_pool[0].reshape(cg, cg), v_w_a_up[0], v_w_b_up[0], v_w_o[0], v_w_ff1[0], v_w_ff2[0]]

    W_IN, W_POOL, W_A, W_B, W_O, W_FF1, W_FF2 = ws
    s_in, s_pool, s_a, s_b, s_o, s_ff1, s_ff2 = [_cast_bf16("cast_" + w.name, a) for w, a in zip(ws, w32)]
    (win_f,) = _run_rider("gather_w_in", _ag_rider([W_IN], [s_in]))

    sc_row = _silu_rows(c)
    sc_all = _dev_allgather("gather_silu_c", sc_row.reshape(8, D // 8)).reshape(N_DEV, D)
    sc16 = jnp.concatenate([sc_all, jnp.zeros_like(sc_all)], axis=0)
    b_cols = lax.dynamic_slice(b_ada, (0, chip * A_COLS), (1, A_COLS))
    (mod_cols,) = _mm("mod_cols", [(sc16, w_ada[0])], M=2 * N_DEV, N=A_COLS, K=D, tm=16, tn=1024, tk=1024,
                      a_pro=lambda a: a.astype(BF16), b_pro=lambda b: b.astype(BF16),
                      extras=[(b_cols, "row", 0)], outs=[_tile_out(F32)], epi=lambda accs, ex: [accs[0] + ex[0]])
    mod_all = _dev_allgather("gather_mod", mod_cols[:N_DEV]).reshape(N_CHIPS, 2, N_DEV, A_COLS)
    mod_row = lax.dynamic_index_in_dim(mod_all[:, 0], dev, axis=1, keepdims=False).reshape(1, N_CHIPS * A_COLS)
    shift1, scale1, gate1, shift2, scale2, gate2 = [mod_row[:, i * D:(i + 1) * D] for i in range(6)]

    WIDE = dict(tm=2048, tn=512, tk=2048)
    DEEP = dict(tm=1024, tn=1024, tk=1024)
    h = _norm_mod("norm1_mod", x2, norm1_w, scale1, shift1)
    (proj,), ((wpool_f, wa_f, wb_f, wo_f),) = _mm(
        "in_proj", [(h, win_f)], M=S, N=IN, K=D, outs=[_tile_out(BF16)], epi=lambda accs, ex: [accs[0]], **WIDE,
        riders=[_ag_rider([W_POOL, W_A, W_B, W_O], [s_pool, s_a, s_b, s_o])])
    pooled, pa = _pool_fwd(proj, wpool_f, pool_scale, S, PW)
    (att, attf), ((wff1_f,),) = _attn_fwd(proj, q_norm_w, k_norm_w, S, H, PW // HEAD_DIM,
                                          riders=[_ag_rider([W_FF1], [s_ff1])])

    def merge_epi(accs, ex):
        sa, sb = jax.nn.sigmoid(ex[0].astype(F32)), jax.nn.sigmoid(ex[1].astype(F32))
        return [sa * accs[0] + sb * accs[1], accs[0], accs[1]]

    merged, ya, yb = _mm("branch_up_merge", [(pa, wa_f), (att, wb_f)], M=S, N=D, K=PW,
                         extras=[(proj, "tile", 4 * PW), (proj, "tile", 4 * PW + D)],
                         outs=[_tile_out(BF16)] * 3, epi=merge_epi)
    x1, o = _mm("out_proj", [(merged, wo_f)], M=S, N=D, K=D, extras=[(x2, "tile", 0), (gate1, "row", 0)], **WIDE,
                outs=[_tile_out(F32), _tile_out(BF16)], epi=lambda accs, ex: [ex[0] + ex[1] * accs[0], accs[0]])
    h2 = _norm_mod("norm2_mod", x1, norm2_w, scale2, shift2)
    (rl,), ((wff2_f,),) = _mm("ff1", [(h2, wff1_f)], M=S, N=FF, K=D, outs=[_tile_out(BF16)], **WIDE,
                              epi=lambda accs, ex: [jnp.maximum(accs[0], 0.0)],
                              riders=[_ag_rider([W_FF2], [s_ff2])])

    def square(a):
        af = a.astype(F32)
        return (af * af).astype(BF16)

    def loss_epi(accs, ex):
        x1_t, tgt_t, g2 = ex
        f = accs[0]
        diff = (x1_t + g2 * f) - tgt_t
        dy = diff * (1.0 / D)
        return [dy, dy * g2, _colsum(dy * f), _colsum(diff * diff)]

    dy, df, dgate2_p, loss_p = _mm("ff2_loss", [(rl, wff2_f)], M=S, N=D, K=FF, a_pro=square, tm=1024, tn=1024, tk=512,
                                   extras=[(x1, "tile", 0), (tgt, "tile", 0), (gate2, "row", 0)],
                                   outs=[_tile_out(F32), _tile_out(BF16), _COLSUM, _COLSUM], epi=loss_epi)

    def pair_sums(group, partials, got):
        return [_pair_sum(w, g, r, c_arr) for w, g, r in zip(group, partials, got)]

    first = lambda accs, ex: [accs[0]]
    gmm = dict(ta=True, outs=[_tile_out(BF16)], epi=first, **WIDE)
    (g_ff2,) = _mm("grad_w_ff2", [(rl, df)], M=FF, N=D, K=S, a_pro=square, ta=True, tm=512, tn=2048, tk=2048,
                   outs=[_tile_out(BF16)], epi=first)
    (dz1,), (got_ff2,) = _mm("d_ff_hidden", [(df, wff2_f)], M=S, N=FF, K=D, tb=True, extras=[(rl, "tile", 0)], **WIDE,
                             outs=[_tile_out(BF16)], epi=lambda accs, ex: [accs[0] * (2.0 * ex[0].astype(F32))],
                             riders=[_px_rider([W_FF2], [g_ff2])])
    sum_ff2 = pair_sums([W_FF2], [g_ff2], got_ff2)
    (g_ff1,), (q_ff2,) = _mm("grad_w_ff1", [(h2, dz1)], M=D, N=FF, K=S,
                             riders=[_cx_rider([W_FF2], sum_ff2, part=(0, 2))], **gmm)
    (dh2,), (got_ff1, q_ff2) = _mm("d_h2", [(dz1, wff1_f)], M=S, N=D, K=FF, tb=True, outs=[_tile_out(F32)], epi=first,
                                   riders=[_px_rider([W_FF1], [g_ff1]),
                                           _cx_rider([W_FF2], sum_ff2, part=(1, 2), q_in=q_ff2)], **DEEP)
    sum_ff1 = pair_sums([W_FF1], [g_ff1], got_ff1)
    dx1, dshift2_p, dscale2_p, gn2_p, do, dgate1_p = _norm_mod_bwd("norm2_bwd", dh2, x1, dy, norm2_w, scale2,
                                                                   gate_o=(o, gate1))
    (g_wo,) = _mm("grad_w_o", [(merged, do)], M=D, N=D, K=S, **gmm)

    def gate_epi(accs, ex):
        dm = accs[0]
        sa, sb = jax.nn.sigmoid(ex[0].astype(F32)), jax.nn.sigmoid(ex[1].astype(F32))
        ya_t, yb_t = ex[2].astype(F32), ex[3].astype(F32)
        return [dm * sa, dm * sb, dm * ya_t * (sa * (1.0 - sa)), dm * yb_t * (sb * (1.0 - sb))]

    dya, dyb, dga, dgb = _mm("d_merged", [(do, wo_f)], M=S, N=D, K=D, tb=True, tm=1024, tn=512, tk=2048,
                             extras=[(proj, "tile", 4 * PW), (proj, "tile", 4 * PW + D), (ya, "tile", 0), (yb, "tile", 0)],
                             outs=[_tile_out(BF16)] * 4, epi=gate_epi)
    (g_wa,) = _mm("grad_w_a_up", [(pa, dya)], M=PW, N=D, K=S, **gmm)
    (g_wb,) = _mm("grad_w_b_up", [(att, dyb)], M=PW, N=D, K=S, **gmm)
    (dpa,) = _mm("d_pool_out", [(dya, wa_f)], M=S, N=PW, K=D, tb=True, outs=[_tile_out(F32)], epi=first, **WIDE)
    mid = [W_A, W_B, W_O]
    (datt,), (got_mid,) = _mm("d_att", [(dyb, wb_f)], M=S, N=PW, K=D, tb=True, outs=[_tile_out(BF16)], epi=first, **WIDE,
                              riders=[_px_rider(mid, [g_wa, g_wb, g_wo])])
    sum_mid = pair_sums(mid, [g_wa, g_wb, g_wo], got_mid)
    du, g_wpool4, gscale_p = _pool_bwd(dpa, pooled, wpool_f, pool_scale, S, PW)
    (dq, dk, dv, gq_p, gk_p), ((q_ff1, q_wa, q_wb, q_wo),) = _attn_bwd(
        proj, datt, attf, q_norm_w, k_norm_w, S, H, PW // HEAD_DIM, riders=[_cx_rider([W_FF1] + mid, sum_ff1 + sum_mid)])
    dproj = jnp.concatenate([du, dq, dk, dv, dga, dgb], axis=1)
    early = mid + [W_FF1, W_FF2]
    halves_early = [_chip_sum(w, q, c_arr) for w, q in zip(early, [q_wa, q_wb, q_wo, q_ff1, q_ff2[0]])]
    (g_win,), (grads_early,) = _mm("grad_w_in", [(h, dproj)], M=D, N=IN, K=S, riders=[_sf_rider(early, halves_early)],
                                   **gmm)
    last = [W_IN, W_POOL]
    g_last = [g_win, g_wpool4.reshape(PW, cg)]
    dh_kw = dict(M=S, N=D // 2, K=IN, tb=True, outs=[_tile_out(F32)], epi=first, **DEEP)
    (dh_left,), (got_last,) = _mm("d_h_left", [(dproj, win_f)], riders=[_px_rider(last, g_last)], **dh_kw)
    sum_last = pair_sums(last, g_last, got_last)
    (dh_right,), ((q_win, q_wpool),) = _mm("d_h_right", [(dproj, win_f)], b_noff=D // 2,
                                           riders=[_cx_rider(last, sum_last)], **dh_kw)
    grad_x, dshift1_p, dscale1_p, gn1_p = _norm_mod_bwd("norm1_bwd", (dh_left, dh_right), x2, dx1, norm1_w, scale1)

    parts = [dshift1_p, dscale1_p, dgate1_p, dshift2_p, dscale2_p, dgate2_p, gn1_p, gn2_p,
             gscale_p.reshape(1, 1, PW), gq_p, gk_p]
    widths = [D] * 8 + [PW, HEAD_DIM, HEAD_DIM]
    used = sum(widths)
    P = -(-used // 1024) * 1024
    packed, loss_part = _pack_partials(parts + [loss_p], widths, P)
    gathered = _dev_allgather("gather_vector_grads", packed.reshape(8, P // 8)).reshape(N_DEV, P)
    small = [(b_ada, m_b_ada, v_b_ada), (norm1_w, m_norm1_w, v_norm1_w), (norm2_w, m_norm2_w, v_norm2_w),
             (pool_scale, m_pool_scale, v_pool_scale), (q_norm_w, m_q_norm_w, v_q_norm_w),
             (k_norm_w, m_k_norm_w, v_k_norm_w)]
    offsets = [(0, 6 * D), (6 * D, D), (7 * D, D), (8 * D, PW), (8 * D + PW, HEAD_DIM), (8 * D + PW + HEAD_DIM, HEAD_DIM)]
    su = _small_update(gathered, offsets, small)
    (g_b, d_b, nm_b, nv_b, g_n1, d_n1, nm_n1, nv_n1, g_n2, d_n2, nm_n2, nv_n2, g_ps, d_ps, nm_ps, nv_ps,
     g_qn, d_qn, nm_qn, nv_qn, g_kn, d_kn, nm_kn, nv_kn) = su
    dmod_sh = lax.dynamic_slice(gathered, (0, chip * A_COLS), (N_DEV, A_COLS))
    g_ada, d_ada, nm_ada, nv_ada = _ada_update(sc_all.T, dmod_sh, w_ada[0], m_w_ada[0], v_w_ada[0])

    halves_last = [_chip_sum(w, q, c_arr) for w, q in zip(last, [q_win, q_wpool])]
    grads = list(_run_rider("grad_sibling_fill", _sf_rider(last, halves_last))) + list(grads_early)
    upd = [_adamw("adamw_" + w.name, a, g, m, v) for w, a, g, m, v in zip(ws, w32, grads, m32, v32)]

    loss = 0.5 / D * lax.psum(loss_part[0, 0], ("x", "y", "c"))

    def up(a):
        return a[None]

    def pool4(a):
        return a.reshape(1, N_GROUPS, cg // N_CHIPS, cg)

    (d_win, nm_win, nv_win), (d_wp, nm_wp, nv_wp), (d_wa, nm_wa, nv_wa), (d_wb, nm_wb, nv_wb), \
        (d_wo, nm_wo, nv_wo), (d_f1, nm_f1, nv_f1), (d_f2, nm_f2, nv_f2) = upd
    gr_win, gr_wp, gr_wa, gr_wb, gr_wo, gr_f1, gr_f2 = grads
    return (
        loss, grad_x[None],
        up(g_ada), g_b, g_n1, up(gr_win), g_qn, g_kn, pool4(gr_wp), g_ps, up(gr_wa), up(gr_wb), up(gr_wo), g_n2,
        up(gr_f1), up(gr_f2),
        up(d_ada), d_b, d_n1, up(d_win), d_qn, d_kn, pool4(d_wp), d_ps, up(d_wa), up(d_wb), up(d_wo), d_n2,
        up(d_f1), up(d_f2),
        up(nm_ada), nm_b, nm_n1, up(nm_win), nm_qn, nm_kn, pool4(nm_wp), nm_ps, up(nm_wa), up(nm_wb), up(nm_wo), nm_n2,
        up(nm_f1), up(nm_f2),
        up(nv_ada), nv_b, nv_n1, up(nv_win), nv_qn, nv_kn, pool4(nv_wp), nv_ps, up(nv_wa), up(nv_wb), up(nv_wo), nv_n2,
        up(nv_f1), up(nv_f2),
    )
```

```python
import functools
import math

import jax
import jax.numpy as jnp
from jax import lax
from jax.experimental import pallas as pl
from jax.experimental.pallas import tpu as pltpu

F32 = jnp.float32
BF16 = jnp.bfloat16
MESH = pl.DeviceIdType.MESH
ANY = pl.BlockSpec(memory_space=pl.ANY)

EPS = 1e-6
HEAD_DIM = 128
POOL_WINDOWS = (2, 4, 8, 16)
N_GROUPS = len(POOL_WINDOWS)
N_CHIPS = 4
N_DEV = 8
ADAM_LR, ADAM_B1, ADAM_B2, ADAM_EPS, ADAM_WD, ADAM_STEP = 0.001, 0.9, 0.999, 1e-08, 0.01, 10
VMEM_LIMIT_V7X = 56 * 1024 * 1024
ATT_T = 256
POOL_T = 256


def _pcall(body, **kw):
    return pl.pallas_call(body, **kw)


def _params(sem=None):
    return pltpu.CompilerParams(dimension_semantics=sem, vmem_limit_bytes=VMEM_LIMIT_V7X)


def _tile(n, pref):
    if n <= pref:
        return n
    t = pref
    while n % t:
        t //= 2
    return t


class _Rider:
    def __init__(self, arrays, out_shape, sems, start, finish, aliases=None, steps=()):
        self.arrays, self.out_shape, self.sems = list(arrays), list(out_shape), list(sems)
        self.start, self.finish, self.aliases, self.steps = start, finish, aliases or {}, list(steps)


def _ride(name, body, riders, arrays, *, grid, in_specs, out_specs, out_shape, scratch_shapes, sem):
    n_in, n_out, n_scr = len(arrays), len(out_shape), len(scratch_shapes)
    r_arrays = [a for r in riders for a in r.arrays]
    r_outs = [o for r in riders for o in r.out_shape]
    r_sems = [s for r in riders for s in r.sems]
    n_hooks = max([len(r.steps) for r in riders], default=0)
    total = math.prod(grid)
    aliases, off_i, off_o = {}, n_in, n_out
    for r in riders:
        for a, o in r.aliases.items():
            aliases[off_i + a] = off_o + o
        off_i += len(r.arrays)
        off_o += len(r.out_shape)

    def full(*refs):
        p = 0
        groups = []
        for n in (n_in, len(r_arrays), n_out, len(r_outs), n_scr, len(r_sems)):
            groups.append(refs[p:p + n])
            p += n
        ins, rin, outs, rout, scr, rsem = groups

        def each(what):
            a = o = s = 0
            for r in riders:
                fn = what(r)
                if fn is not None:
                    fn(rin[a:a + len(r.arrays)], rout[o:o + len(r.out_shape)], rsem[s:s + len(r.sems)])
                a, o, s = a + len(r.arrays), o + len(r.out_shape), s + len(r.sems)

        if riders:
            lin = 0
            for d, g in enumerate(grid):
                lin = lin * g + pl.program_id(d)
            pl.when(lin == 0)(lambda: each(lambda r: r.start))
            for t in range(n_hooks):
                pl.when(lin == ((t + 1) * total) // (n_hooks + 1))(
                    lambda t=t: each(lambda r: r.steps[t] if t < len(r.steps) else None))
        body(*ins, *outs, *scr)
        if riders:
            pl.when(lin == total - 1)(lambda: each(lambda r: r.finish))

    res = _pcall(
        full, name=name, grid=grid, in_specs=list(in_specs) + [ANY] * len(r_arrays),
        out_specs=list(out_specs) + [ANY] * len(r_outs), out_shape=list(out_shape) + r_outs,
        scratch_shapes=list(scratch_shapes) + r_sems, input_output_aliases=aliases,
        compiler_params=_params(("arbitrary",) * len(grid) if riders else sem),
    )(*arrays, *r_arrays)
    if not riders:
        return res
    main, rest, per = res[:n_out], res[n_out:], []
    for r in riders:
        per.append(rest[:len(r.out_shape)])
        rest = rest[len(r.out_shape):]
    return main, per


def _run_rider(name, rider):
    def body(*refs):
        n_a, n_o = len(rider.arrays), len(rider.out_shape)
        ins, outs, sems = refs[:n_a], refs[n_a:n_a + n_o], refs[n_a + n_o:]
        for fn in [rider.start] + rider.steps + [rider.finish]:
            fn(ins, outs, sems)

    return _pcall(body, name=name, out_shape=rider.out_shape, in_specs=[ANY] * len(rider.arrays),
                  out_specs=[ANY] * len(rider.out_shape), scratch_shapes=rider.sems,
                  input_output_aliases=rider.aliases)(*rider.arrays)


def _mm(name, pairs, *, M, N, K, ta=False, tb=False, tm=512, tn=1024, tk=1024,
        a_pro=None, b_pro=None, extras=(), outs, epi, riders=(), b_noff=0):
    tm, tn, tk = _tile(M, tm), _tile(N, tn), _tile(K, tk)
    n_i, n_j, n_k = M // tm, N // tn, K // tk
    n_p, n_e = len(pairs), len(extras)
    arrays, in_specs = [], []
    for a, _ in pairs:
        arrays.append(a)
        in_specs.append(pl.BlockSpec((tk, tm), lambda i, j, k: (k, i)) if ta
                        else pl.BlockSpec((tm, tk), lambda i, j, k: (i, k)))
    for _, b in pairs:
        arrays.append(b)
        in_specs.append(pl.BlockSpec((tn, tk), lambda i, j, k: (j + b_noff // tn, k)) if tb
                        else pl.BlockSpec((tk, tn), lambda i, j, k: (k, j + b_noff // tn)))
    for arr, kind, off in extras:
        ob = off // tn
        assert off % tn == 0
        arrays.append(arr)
        if kind == "tile":
            in_specs.append(pl.BlockSpec((tm, tn), lambda i, j, k, ob=ob: (i, j + ob)))
        else:
            in_specs.append(pl.BlockSpec((1, tn), lambda i, j, k, ob=ob: (0, j + ob)))
    out_shape, out_specs = [], []
    for o in outs:
        if o["kind"] == "tile":
            out_shape.append(jax.ShapeDtypeStruct((M, N), o["dtype"]))
            out_specs.append(pl.BlockSpec((tm, tn), lambda i, j, k: (i, j)))
        else:
            out_shape.append(jax.ShapeDtypeStruct((n_i, 1, N), F32))
            out_specs.append(pl.BlockSpec((1, 1, tn), lambda i, j, k: (i, 0, j)))
    dims = (((0 if ta else 1,), (1 if tb else 0,)), ((), ()))

    def body(*refs):
        a_refs, b_refs = refs[:n_p], refs[n_p:2 * n_p]
        e_refs = refs[2 * n_p:2 * n_p + n_e]
        o_refs = refs[2 * n_p + n_e:2 * n_p + n_e + len(outs)]
        acc_refs = refs[2 * n_p + n_e + len(outs):]

        def product(p):
            a, b = a_refs[p][...], b_refs[p][...]
            if a_pro is not None:
                a = a_pro(a)
            if b_pro is not None:
                b = b_pro(b)
            return lax.dot_general(a, b, dims, preferred_element_type=F32)

        def write(accs):
            vals = epi(accs, [e[...] for e in e_refs])
            for o, o_ref, val in zip(outs, o_refs, vals):
                if o["kind"] == "tile":
                    o_ref[...] = val.astype(o_ref.dtype)
                else:
                    o_ref[0] = val

        if n_k == 1:
            write([product(p) for p in range(n_p)])
            return
        k = pl.program_id(2)

        @pl.when(k == 0)
        def _():
            for acc in acc_refs:
                acc[...] = jnp.zeros_like(acc)

        for p in range(n_p):
            acc_refs[p][...] += product(p)

        pl.when(k == n_k - 1)(lambda: write([acc[...] for acc in acc_refs]))

    return _ride(name, body, riders, arrays, grid=(n_i, n_j, n_k), in_specs=in_specs, out_specs=out_specs,
                 out_shape=out_shape, scratch_shapes=[pltpu.VMEM((tm, tn), F32) for _ in pairs] if n_k > 1 else [],
                 sem=("parallel", "parallel", "arbitrary"))


def _tile_out(dtype):
    return {"kind": "tile", "dtype": dtype}


_COLSUM = {"kind": "colsum"}


def _colsum(v):
    return jnp.sum(v, axis=0, keepdims=True)


def _norm_mod(name, x, norm_w, scale, shift):
    S, D = x.shape
    tr = _tile(S, 256)

    def body(x_ref, nw_ref, sc_ref, sh_ref, h_ref):
        xv = x_ref[...]
        r = lax.rsqrt(jnp.mean(xv * xv, axis=-1, keepdims=True) + EPS)
        h_ref[...] = ((xv * r * nw_ref[...]) * (1.0 + sc_ref[...]) + sh_ref[...]).astype(BF16)

    row = pl.BlockSpec((1, D), lambda i: (0, 0))
    til = pl.BlockSpec((tr, D), lambda i: (i, 0))
    return _pcall(body, name=name, grid=(S // tr,), in_specs=[til, row, row, row], out_specs=til,
                  out_shape=jax.ShapeDtypeStruct((S, D), BF16), compiler_params=_params(("parallel",)))(
                      x, norm_w, scale, shift)


def _norm_mod_bwd(name, dh, x, dres, norm_w, scale, gate_o=None):
    S, D = x.shape
    tr = _tile(S, 256)
    n_r = S // tr
    with_gate = gate_o is not None
    dh = list(dh) if isinstance(dh, (list, tuple)) else [dh]
    n_dh = len(dh)

    def body(*refs):
        dh_refs, refs = refs[:n_dh], refs[n_dh:]
        if with_gate:
            x_ref, dres_ref, nw_ref, sc_ref, o_ref, g_ref, dx_ref, p1, p2, p3, do_ref, p4 = refs
        else:
            x_ref, dres_ref, nw_ref, sc_ref, dx_ref, p1, p2, p3 = refs
        dhv = dh_refs[0][...] if n_dh == 1 else jnp.concatenate([r[...] for r in dh_refs], axis=1)
        xv, nw = x_ref[...], nw_ref[...]
        r = lax.rsqrt(jnp.mean(xv * xv, axis=-1, keepdims=True) + EPS)
        xh = xv * r
        p1[0] = _colsum(dhv)
        p2[0] = _colsum(dhv * (xh * nw))
        dn = dhv * (1.0 + sc_ref[...])
        p3[0] = _colsum(dn * xh)
        dxh = dn * nw
        dx = dres_ref[...] + r * (dxh - xh * jnp.mean(dxh * xh, axis=-1, keepdims=True))
        dx_ref[...] = dx
        if with_gate:
            do_ref[...] = (dx * g_ref[...]).astype(BF16)
            p4[0] = _colsum(dx * o_ref[...].astype(F32))

    row = pl.BlockSpec((1, D), lambda i: (0, 0))
    til = pl.BlockSpec((tr, D), lambda i: (i, 0))
    part = pl.BlockSpec((1, 1, D), lambda i: (i, 0, 0))
    part_shape = jax.ShapeDtypeStruct((n_r, 1, D), F32)
    in_specs = [pl.BlockSpec((tr, D // n_dh), lambda i: (i, 0))] * n_dh + [til, til, row, row]
    arrays = dh + [x, dres, norm_w, scale]
    out_specs = [til, part, part, part]
    out_shape = [jax.ShapeDtypeStruct((S, D), F32), part_shape, part_shape, part_shape]
    if with_gate:
        in_specs += [til, row]
        arrays += list(gate_o)
        out_specs += [til, part]
        out_shape += [jax.ShapeDtypeStruct((S, D), BF16), part_shape]
    return _pcall(body, name=name, grid=(n_r,), in_specs=in_specs, out_specs=out_specs, out_shape=out_shape,
                  compiler_params=_params(("parallel",)))(*arrays)


def _pool_w_specs(rows, cg):
    return [pl.BlockSpec((rows, cg), lambda g, j=j: (N_GROUPS * j + g, 0)) for j in range(N_CHIPS)]


def _pool_fwd(proj, wp_full, pool_scale, S, PW):
    cg = PW // N_GROUPS
    rows = cg // N_CHIPS
    T = _tile(S, POOL_T)
    n_t = S // T

    def body(u_ref, w0, w1, w2, w3, ps_ref, pooled_ref, pa_ref):
        g = pl.program_id(0)
        win = jnp.left_shift(2, g)
        w = jnp.concatenate([w0[...], w1[...], w2[...], w3[...]], axis=0)
        t_i = lax.broadcasted_iota(jnp.int32, (T, T), 0)
        j_i = lax.broadcasted_iota(jnp.int32, (T, T), 1)
        b_cur = ((j_i <= t_i) & (j_i > t_i - win)).astype(BF16)
        b_prev = (j_i - T > t_i - win).astype(BF16)
        row = lax.broadcasted_iota(jnp.int32, (T, 1), 0)
        for r in range(n_t):
            cur = u_ref[r * T:(r + 1) * T, :]
            ws = jnp.dot(b_cur, cur, preferred_element_type=F32)
            if r > 0:
                ws += jnp.dot(b_prev, u_ref[(r - 1) * T:r * T, :], preferred_element_type=F32)
            count = jnp.minimum(row + (r * T + 1), win).astype(F32)
            pooled = (ws / count - cur.astype(F32)).astype(BF16)
            pooled_ref[r * T:(r + 1) * T, :] = pooled
            mixed = jnp.dot(pooled, w, preferred_element_type=F32)
            pa_ref[r * T:(r + 1) * T, :] = (mixed * ps_ref[...]).astype(BF16)

    col = pl.BlockSpec((S, cg), lambda g: (0, g))
    return _pcall(
        body, name="pool_fwd", grid=(N_GROUPS,),
        in_specs=[col] + _pool_w_specs(rows, cg) + [pl.BlockSpec((1, cg), lambda g: (0, g))],
        out_specs=[col, col],
        out_shape=[jax.ShapeDtypeStruct((S, PW), BF16), jax.ShapeDtypeStruct((S, PW), BF16)],
        compiler_params=_params(("parallel",)),
    )(proj, wp_full, wp_full, wp_full, wp_full, pool_scale)


def _pool_bwd(dpa, pooled, wp_full, pool_scale, S, PW):
    cg = PW // N_GROUPS
    rows = cg // N_CHIPS
    T = _tile(S, POOL_T)
    n_t = S // T

    def body(dpa_ref, pooled_ref, w0, w1, w2, w3, ps_ref, du_ref, gw_ref, gs_ref, dp_s, dpc_s, dmx_s):
        g = pl.program_id(0)
        win = jnp.left_shift(2, g)
        w = jnp.concatenate([w0[...], w1[...], w2[...], w3[...]], axis=0)
        row = lax.broadcasted_iota(jnp.int32, (T, 1), 0)
        gs = jnp.zeros((1, cg), F32)
        for r in range(n_t):
            sl = slice(r * T, (r + 1) * T)
            mixed = jnp.dot(pooled_ref[sl, :], w, preferred_element_type=F32)
            dpa_t = dpa_ref[sl, :]
            gs += _colsum(dpa_t * mixed)
            dmx = (dpa_t * ps_ref[...]).astype(BF16)
            dmx_s[sl, :] = dmx
            dpo = lax.dot_general(dmx, w, (((1,), (1,)), ((), ())), preferred_element_type=F32)
            dp_s[sl, :] = dpo
            count = jnp.minimum(row + (r * T + 1), win).astype(F32)
            dpc_s[sl, :] = (dpo / count).astype(BF16)
        gs_ref[...] = gs
        gw = lax.dot_general(pooled_ref[...], dmx_s[...], (((0,), (0,)), ((), ())), preferred_element_type=F32)
        for j in range(N_CHIPS):
            gw_ref[j, 0] = gw[j * rows:(j + 1) * rows, :].astype(BF16)
        j_i = lax.broadcasted_iota(jnp.int32, (T, T), 0)
        t_i = lax.broadcasted_iota(jnp.int32, (T, T), 1)
        b_cur = ((t_i >= j_i) & (t_i < j_i + win)).astype(BF16)
        b_next = (t_i + T < j_i + win).astype(BF16)
        for r in range(n_t):
            sl = slice(r * T, (r + 1) * T)
            acc = jnp.dot(b_cur, dpc_s[sl, :], preferred_element_type=F32)
            if r + 1 < n_t:
                acc += jnp.dot(b_next, dpc_s[(r + 1) * T:(r + 2) * T, :], preferred_element_type=F32)
            du_ref[sl, :] = (acc - dp_s[sl, :]).astype(BF16)

    col = pl.BlockSpec((S, cg), lambda g: (0, g))
    return _pcall(
        body, name="pool_bwd", grid=(N_GROUPS,),
        in_specs=[col, col] + _pool_w_specs(rows, cg) + [pl.BlockSpec((1, cg), lambda g: (0, g))],
        out_specs=[col, pl.BlockSpec((N_CHIPS, 1, rows, cg), lambda g: (0, g, 0, 0)),
                   pl.BlockSpec((1, cg), lambda g: (0, g))],
        out_shape=[jax.ShapeDtypeStruct((S, PW), BF16),
                   jax.ShapeDtypeStruct((N_CHIPS, N_GROUPS, rows, cg), BF16),
                   jax.ShapeDtypeStruct((1, PW), F32)],
        scratch_shapes=[pltpu.VMEM((S, cg), F32), pltpu.VMEM((S, cg), BF16), pltpu.VMEM((S, cg), BF16)],
        compiler_params=_params(("parallel",)),
    )(dpa, pooled, wp_full, wp_full, wp_full, wp_full, pool_scale)


_NT = (((1,), (1,)), ((), ()))
_TN = (((0,), (0,)), ((), ()))


def _split_dot(v, tri):
    hi = v.astype(BF16)
    lo = (v - hi.astype(F32)).astype(BF16)
    return jnp.dot(hi, tri, preferred_element_type=F32) + jnp.dot(lo, tri, preferred_element_type=F32)


def _sb_scores(q_i, k_j, tri_l, masked):
    tq, tk = q_i.shape[0], k_j.shape[0]
    s = lax.dot_general(q_i, k_j, _NT, preferred_element_type=F32) * (1.0 / math.sqrt(HEAD_DIM))
    lp = jnp.log(1.0 + jnp.exp(-jnp.abs(s)))
    l = -jnp.maximum(s, 0.0) - lp
    lb = l + s
    mask = None
    if masked:
        mask = lax.broadcasted_iota(jnp.int32, (tq, tk), 0) > lax.broadcasted_iota(jnp.int32, (tq, tk), 1)
        l = jnp.where(mask, l, 0.0)
    return l, lb, lb + _split_dot(l, tri_l), mask


def _sb_weights(t, carry_l, mask):
    a = jnp.exp(t + carry_l)
    return a if mask is None else jnp.where(mask, a, 0.0)


def _rowsum(v):
    return jnp.sum(v, axis=1, keepdims=True)


def _qk_norm(x_ref, w_ref):
    xv = x_ref[...].astype(F32)
    r = lax.rsqrt(jnp.mean(xv * xv, axis=-1, keepdims=True) + EPS)
    return xv * r, r


def _attn_fwd(proj, q_norm_w, k_norm_w, S, H, q_off, riders=()):
    t = _tile(S, ATT_T)
    n_q = S // t

    def body(q_ref, k_ref, v_ref, qw_ref, kw_ref, att_ref, attf_ref, qn_s, kn_s):
        qh, _ = _qk_norm(q_ref, qw_ref)
        qn_s[...] = (qh * qw_ref[...]).astype(BF16)
        kh, _ = _qk_norm(k_ref, kw_ref)
        kn_s[...] = (kh * kw_ref[...]).astype(BF16)
        tri_l = (lax.broadcasted_iota(jnp.int32, (t, t), 0) > lax.broadcasted_iota(jnp.int32, (t, t), 1)).astype(BF16)

        def rows(j):
            return pl.ds(pl.multiple_of(j * t, t), t)

        def q_step(i, _):
            q_i = qn_s[rows(i), :]

            def av(a, j):
                return jnp.dot(a.astype(BF16), v_ref[rows(j), :], preferred_element_type=F32)

            l, _, tt, mask = _sb_scores(q_i, kn_s[rows(i), :], tri_l, True)
            acc = av(_sb_weights(tt, 0.0, mask), i)
            carry = _rowsum(l)

            def single(_, c):
                carry, acc = c
                l, _, tt, _ = _sb_scores(q_i, kn_s[rows(i - 1), :], tri_l, False)
                return carry + _rowsum(l), acc + av(_sb_weights(tt, carry, None), i - 1)

            carry, acc = lax.fori_loop(0, i % 2, single, (carry, acc))
            top = i - 1 - i % 2

            def pair(p, c):
                carry, acc = c
                j0 = top - 2 * p
                l0, _, t0, _ = _sb_scores(q_i, kn_s[rows(j0), :], tri_l, False)
                l1, _, t1, _ = _sb_scores(q_i, kn_s[rows(j0 - 1), :], tri_l, False)
                mid = carry + _rowsum(l0)
                acc = acc + av(_sb_weights(t0, carry, None), j0) + av(_sb_weights(t1, mid, None), j0 - 1)
                return mid + _rowsum(l1), acc

            _, acc = lax.fori_loop(0, i // 2, pair, (carry, acc))
            att_ref[rows(i), :] = acc.astype(BF16)
            attf_ref[rows(i), :] = acc
            return 0

        lax.fori_loop(0, n_q, q_step, 0)

    def col(off):
        return pl.BlockSpec((S, HEAD_DIM), lambda h, off=off: (0, off + h))

    wspec = pl.BlockSpec((1, HEAD_DIM), lambda h: (0, 0))
    return _ride(
        "attn_fwd", body, riders, [proj, proj, proj, q_norm_w, k_norm_w], grid=(H,),
        in_specs=[col(q_off), col(q_off + H), col(q_off + 2 * H), wspec, wspec],
        out_specs=[col(0), col(0)],
        out_shape=[jax.ShapeDtypeStruct((S, H * HEAD_DIM), BF16), jax.ShapeDtypeStruct((S, H * HEAD_DIM), F32)],
        scratch_shapes=[pltpu.VMEM((S, HEAD_DIM), BF16), pltpu.VMEM((S, HEAD_DIM), BF16)],
        sem=("parallel",))


def _attn_bwd(proj, datt, attf, q_norm_w, k_norm_w, S, H, q_off, riders=()):
    t = _tile(S, ATT_T)
    n_q = S // t
    scale = 1.0 / math.sqrt(HEAD_DIM)

    def body(q_ref, k_ref, v_ref, do_ref, o_ref, qw_ref, kw_ref, dq_ref, dk_ref, dv_ref, gq_ref, gk_ref,
             qn_s, kn_s, dk_s, dv_s, gq_s):
        qw, kw = qw_ref[...], kw_ref[...]
        qh, _ = _qk_norm(q_ref, qw_ref)
        qn_s[...] = (qh * qw).astype(BF16)
        kh, _ = _qk_norm(k_ref, kw_ref)
        kn_s[...] = (kh * kw).astype(BF16)
        dk_s[...] = jnp.zeros_like(dk_s)
        dv_s[...] = jnp.zeros_like(dv_s)
        gq_s[...] = jnp.zeros_like(gq_s)
        r_i = lax.broadcasted_iota(jnp.int32, (t, t), 0)
        c_i = lax.broadcasted_iota(jnp.int32, (t, t), 1)
        tri_l = (r_i > c_i).astype(BF16)
        tri_e = (r_i >= c_i).astype(BF16)

        def rows(j):
            return pl.ds(pl.multiple_of(j * t, t), t)

        def q_step(i, _):
            q_i = qn_s[rows(i), :]
            do_i = do_ref[rows(i), :]
            d_i = _rowsum(do_i.astype(F32) * o_ref[rows(i), :])

            def scores(j, masked):
                k_j = kn_s[rows(j), :]
                l, lb, tt, mask = _sb_scores(q_i, k_j, tri_l, masked)
                da = lax.dot_general(do_i, v_ref[rows(j), :], _NT, preferred_element_type=F32)
                return k_j, l, lb, tt, mask, da

            def grads(j, sc, carry_l, carry_e, dq_acc):
                k_j, l, lb, tt, mask, da = sc
                a_bf = _sb_weights(tt, carry_l, mask).astype(BF16)
                e = da * a_bf.astype(F32)
                p = d_i - (_split_dot(e, tri_e) + carry_e)
                sig = jnp.exp(lb)
                dz = e * (1.0 - sig) - p * sig
                if mask is not None:
                    dz = jnp.where(mask, dz, 0.0)
                dz = (dz * scale).astype(BF16)
                dk_s[rows(j), :] += lax.dot_general(dz, q_i, _TN, preferred_element_type=F32)
                dv_s[rows(j), :] += lax.dot_general(a_bf, do_i, _TN, preferred_element_type=F32)
                return (carry_l + _rowsum(l), carry_e + _rowsum(e),
                        dq_acc + jnp.dot(dz, k_j, preferred_element_type=F32))

            c = grads(i, scores(i, True), 0.0, 0.0, jnp.zeros((t, HEAD_DIM), F32))
            c = lax.fori_loop(0, i % 2, lambda _, c: grads(i - 1, scores(i - 1, False), *c), c)
            top = i - 1 - i % 2

            def pair(p, c):
                j0 = top - 2 * p
                s0, s1 = scores(j0, False), scores(j0 - 1, False)
                return grads(j0 - 1, s1, *grads(j0, s0, *c))

            _, _, dqn = lax.fori_loop(0, i // 2, pair, c)
            qv = q_ref[rows(i), :].astype(F32)
            r = lax.rsqrt(jnp.mean(qv * qv, axis=-1, keepdims=True) + EPS)
            xh = qv * r
            gq_s[...] += _colsum(dqn * xh)
            dxh = dqn * qw
            dq_ref[rows(i), :] = (r * (dxh - xh * jnp.mean(dxh * xh, axis=-1, keepdims=True))).astype(BF16)
            return 0

        lax.fori_loop(0, n_q, q_step, 0)
        gq_ref[0] = gq_s[...]
        kh, rk = _qk_norm(k_ref, kw_ref)
        dkn = dk_s[...]
        gk_ref[0] = _colsum(dkn * kh)
        dxh = dkn * kw
        dk_ref[...] = (rk * (dxh - kh * jnp.mean(dxh * kh, axis=-1, keepdims=True))).astype(BF16)
        dv_ref[...] = dv_s[...].astype(BF16)

    def col(off):
        return pl.BlockSpec((S, HEAD_DIM), lambda h, off=off: (0, off + h))

    wspec = pl.BlockSpec((1, HEAD_DIM), lambda h: (0, 0))
    gspec = pl.BlockSpec((1, 1, HEAD_DIM), lambda h: (h, 0, 0))
    act = jax.ShapeDtypeStruct((S, H * HEAD_DIM), BF16)
    gsh = jax.ShapeDtypeStruct((H, 1, HEAD_DIM), F32)
    return _ride(
        "attn_bwd", body, riders, [proj, proj, proj, datt, attf, q_norm_w, k_norm_w], grid=(H,),
        in_specs=[col(q_off), col(q_off + H), col(q_off + 2 * H), col(0), col(0), wspec, wspec],
        out_specs=[col(0), col(0), col(0), gspec, gspec],
        out_shape=[act, act, act, gsh, gsh],
        scratch_shapes=[pltpu.VMEM((S, HEAD_DIM), BF16), pltpu.VMEM((S, HEAD_DIM), BF16),
                        pltpu.VMEM((S, HEAD_DIM), F32), pltpu.VMEM((S, HEAD_DIM), F32),
                        pltpu.VMEM((1, HEAD_DIM), F32)],
        sem=("parallel",))


def _place():
    x, y, c = lax.axis_index("x"), lax.axis_index("y"), lax.axis_index("c")
    chips = [(1 - x, y), (x, 1 - y), (1 - x, 1 - y)]
    return x, y, c, chips


def _dev_allgather(name, v):
    m_per, n = v.shape

    def body(x_ref, out_ref, send_sems, recv_sems, local_sem):
        x, y, c, chips = _place()
        me, sibling = (x, y, c), (x, y, 1 - c)

        def rows(px, py, pc):
            return out_ref.at[pl.ds((4 * px + 2 * py + pc) * m_per, m_per), :]

        def copy(k, block, to, src=None):
            return pltpu.make_async_remote_copy(
                src_ref=rows(*block) if src is None else src, dst_ref=rows(*block),
                send_sem=send_sems.at[k], recv_sem=recv_sems.at[k], device_id=to, device_id_type=MESH)

        mine = pltpu.make_async_copy(x_ref, rows(*me), local_sem)
        mine.start()
        first = [copy(0, me, sibling, src=x_ref)]
        first += [copy(1 + j, me, (*chip, c), src=x_ref) for j, chip in enumerate(chips)]
        for cp in first:
            cp.start()
        passed = [copy(4 + j, (*chip, c), sibling) for j, chip in enumerate(chips)]
        for j, chip in enumerate(chips):
            copy(1 + j, (*chip, c), me).wait_recv()
            passed[j].start()
        copy(0, sibling, me).wait_recv()
        for j, chip in enumerate(chips):
            copy(4 + j, (*chip, 1 - c), me).wait_recv()
        for cp in first + passed:
            cp.wait_send()
        mine.wait()

    return _pcall(
        body, name=name, out_shape=jax.ShapeDtypeStruct((N_DEV * m_per, n), v.dtype),
        in_specs=[pl.BlockSpec(memory_space=pltpu.VMEM)], out_specs=pl.BlockSpec(memory_space=pltpu.VMEM),
        scratch_shapes=[pltpu.SemaphoreType.DMA((7,)), pltpu.SemaphoreType.DMA((7,)), pltpu.SemaphoreType.DMA],
        compiler_params=pltpu.CompilerParams(vmem_limit_bytes=VMEM_LIMIT_V7X),
    )(v)


class _W:
    def __init__(self, name, kind, R, C):
        self.name, self.kind, self.R, self.C = name, kind, R, C

    @property
    def shard_shape(self):
        return (self.R, self.C // N_CHIPS) if self.kind == "col" else (self.R // N_CHIPS, self.C)

    @property
    def half_rows(self):
        return self.shard_shape[0] // 2

    def shard_half(self, ref, half):
        return ref.at[pl.ds(half * self.half_rows, self.half_rows), :]

    def region(self, full_ref, chip, half):
        hr = self.half_rows
        if self.kind == "col":
            cw = self.C // N_CHIPS
            return full_ref.at[pl.ds(half * hr, hr), pl.ds(chip * cw, cw)]
        return full_ref.at[pl.ds(chip * (2 * hr) + half * hr, hr), :]

    def region_both(self, full_ref, chip):
        hr = self.half_rows
        if self.kind == "col":
            cw = self.C // N_CHIPS
            return full_ref.at[:, pl.ds(chip * cw, cw)]
        return full_ref.at[pl.ds(chip * (2 * hr), 2 * hr), :]


def _ag_rider(ws, shards, n_ch=4):
    n_w = len(ws)
    per = 6

    def parts(sh, full, sems):
        send_sems, recv_sems, local_sems = sems
        x, y, c, _ = _place()
        xn, yn, dg = (1 - x, y), (x, 1 - y), (1 - x, 1 - y)
        via = (x + (1 - c) * (1 - 2 * x), y + c * (1 - 2 * y))
        to = (x + c * (1 - 2 * x), y + (1 - c) * (1 - 2 * y))

        def chunk(w, ref, t):
            nr = w.half_rows // n_ch
            return ref.at[pl.ds(t * nr, nr), :]

        def reg(i, chip, half, t):
            return chunk(ws[i], ws[i].region(full[i], 2 * chip[0] + chip[1], half), t)

        def copy(src, dst, i, t, k, dev):
            s = (i * n_ch + t) * per + k
            return pltpu.make_async_remote_copy(src_ref=src, dst_ref=dst, send_sem=send_sems.at[s],
                                                recv_sem=recv_sems.at[s], device_id=dev, device_id_type=MESH)

        def direct(i, t, k):
            return copy(chunk(ws[i], ws[i].shard_half(sh[i], c), t), reg(i, (x, y), c, t), i, t, k, (*(xn, yn)[k], c))

        def direct_in(i, t, k):
            r = reg(i, (xn, yn)[k], c, t)
            return copy(r, r, i, t, k, (*(xn, yn)[k], c))

        def relay(i, t):
            r = reg(i, via, c, t)
            return copy(r, r, i, t, 2, (*to, c))

        def relay_in(i, t):
            r = reg(i, dg, c, t)
            return copy(r, r, i, t, 2, (*to, c))

        def hand(i, t, k, half):
            r = reg(i, (xn, yn, dg)[k], half, t)
            return copy(r, r, i, t, 3 + k, (x, y, 1 - c))

        def mine(i):
            return pltpu.make_async_copy(sh[i], ws[i].region_both(full[i], 2 * x + y), local_sems.at[i])

        return c, direct, direct_in, relay, relay_in, hand, mine

    def start(sh, full, sems):
        _, direct, _, _, _, _, mine = parts(sh, full, sems)
        for t in range(n_ch):
            for i in range(n_w):
                direct(i, t, 0).start()
                direct(i, t, 1).start()
        for i in range(n_w):
            mine(i).start()

    def arrived(t):
        def step(sh, full, sems):
            c, _, direct_in, relay, relay_in, hand, _ = parts(sh, full, sems)
            for i in range(n_w):
                direct_in(i, t, 0).wait_recv()
                direct_in(i, t, 1).wait_recv()
                relay(i, t).start()
                hand(i, t, 0, c).start()
                hand(i, t, 1, c).start()
                if t > 0:
                    relay_in(i, t - 1).wait_recv()
                    hand(i, t - 1, 2, c).start()
        return step

    def finish(sh, full, sems):
        c, direct, _, relay, relay_in, hand, mine = parts(sh, full, sems)
        for i in range(n_w):
            relay_in(i, n_ch - 1).wait_recv()
            hand(i, n_ch - 1, 2, c).start()
        for i in range(n_w):
            for t in range(n_ch):
                for k in range(3):
                    hand(i, t, k, 1 - c).wait_recv()
        for i in range(n_w):
            for t in range(n_ch):
                direct(i, t, 0).wait_send()
                direct(i, t, 1).wait_send()
                relay(i, t).wait_send()
                for k in range(3):
                    hand(i, t, k, c).wait_send()
            mine(i).wait()

    n_sem = per * n_ch * n_w
    return _Rider(shards, [jax.ShapeDtypeStruct((w.R, w.C), BF16) for w in ws],
                  [pltpu.SemaphoreType.DMA((n_sem,)), pltpu.SemaphoreType.DMA((n_sem,)),
                   pltpu.SemaphoreType.DMA((n_w,))], start, finish, steps=[arrived(t) for t in range(n_ch)])


def _half_view(w, g):
    return g if w.kind == "col" else g.reshape(N_CHIPS, w.R // N_CHIPS, w.C)


def _px_rider(ws, grads):
    n_w = len(ws)

    def copies(g, got, sems):
        send_sems, recv_sems = sems
        x, y, c, _ = _place()

        def half_all(w, ref, half):
            hr = w.half_rows
            if w.kind == "col":
                return ref.at[pl.ds(half * hr, hr), :]
            return ref.at[:, pl.ds(half * hr, hr), :]

        return [pltpu.make_async_remote_copy(
            src_ref=half_all(w, g[i], 1 - c), dst_ref=got[i], send_sem=send_sems.at[i], recv_sem=recv_sems.at[i],
            device_id=(x, y, 1 - c), device_id_type=MESH) for i, w in enumerate(ws)]

    def start(g, got, sems):
        for cp in copies(g, got, sems):
            cp.start()

    def finish(g, got, sems):
        for cp in copies(g, got, sems):
            cp.wait_recv()
            cp.wait_send()

    def got_shape(w):
        hr = w.half_rows
        return (hr, w.C) if w.kind == "col" else (N_CHIPS, hr, w.C)

    return _Rider([_half_view(w, g) for w, g in zip(ws, grads)],
                  [jax.ShapeDtypeStruct(got_shape(w), BF16) for w in ws],
                  [pltpu.SemaphoreType.DMA((n_w,)), pltpu.SemaphoreType.DMA((n_w,))], start, finish)


def _pair_sum(w, g, got, c_arr):
    hr = w.half_rows
    if w.kind == "col":
        tr, tc = _tile(hr, 512), _tile(w.C, 2048)
        n_r = hr // tr
        grid = (n_r, w.C // tc)
        g_spec = pl.BlockSpec((tr, tc), lambda i, j, c: (c[0] * n_r + i, j))
        o_spec = pl.BlockSpec((tr, tc), lambda i, j, c: (i, j))
    else:
        tr = _tile(hr, 512)
        n_r = hr // tr
        grid = (N_CHIPS, n_r)
        g_spec = pl.BlockSpec((1, tr, w.C), lambda s, i, c: (s, c[0] * n_r + i, 0))
        o_spec = pl.BlockSpec((1, tr, w.C), lambda s, i, c: (s, i, 0))

    def body(c_ref, g_ref, got_ref, out_ref):
        out_ref[...] = (g_ref[...].astype(F32) + got_ref[...].astype(F32)).astype(BF16)

    return _pcall(
        body, name="grad_pair_sum_" + w.name, out_shape=jax.ShapeDtypeStruct(got.shape, BF16),
        grid_spec=pltpu.PrefetchScalarGridSpec(num_scalar_prefetch=1, grid=grid, in_specs=[g_spec, o_spec],
                                               out_specs=o_spec),
        compiler_params=_params(("parallel", "parallel")),
    )(c_arr, _half_view(w, g), got)


def _cx_rider(ws, sums, part=(0, 1), q_in=None):
    n_w = len(ws)

    def parts(p, q, sems):
        send_sems, recv_sems, local_sems = sems
        x, y, c, chips = _place()
        my_chip = 2 * x + y

        def rows(w, ref):
            nr = w.half_rows // part[1]
            return ref.at[pl.ds(part[0] * nr, nr), :]

        def piece(w, ref, chip):
            if w.kind == "col":
                cw = w.C // N_CHIPS
                return rows(w, ref.at[:, pl.ds(chip * cw, cw)])
            return rows(w, ref.at[chip])

        def copy(i, k, recv=False):
            chip = chips[k]
            to_chip = 2 * chip[0] + chip[1]
            return pltpu.make_async_remote_copy(
                src_ref=piece(ws[i], p[i], to_chip), dst_ref=rows(ws[i], q[i].at[to_chip if recv else my_chip]),
                send_sem=send_sems.at[3 * i + k], recv_sem=recv_sems.at[3 * i + k],
                device_id=(*chip, c), device_id_type=MESH)

        def mine(i):
            return pltpu.make_async_copy(piece(ws[i], p[i], my_chip), rows(ws[i], q[i].at[my_chip]), local_sems.at[i])

        return copy, mine

    both = [(i, k) for i in range(n_w) for k in range(N_CHIPS - 1)]

    def start(p, q, sems):
        copy, mine = parts(p, q, sems)
        for i, k in both:
            copy(i, k).start()
        for i in range(n_w):
            mine(i).start()

    def finish(p, q, sems):
        copy, mine = parts(p, q, sems)
        for i, k in both:
            copy(i, k, recv=True).wait_recv()
        for i, k in both:
            copy(i, k).wait_send()
        for i in range(n_w):
            mine(i).wait()

    return _Rider(list(sums) + list(q_in or []),
                  [jax.ShapeDtypeStruct((N_CHIPS, w.half_rows, w.shard_shape[1]), BF16) for w in ws],
                  [pltpu.SemaphoreType.DMA((3 * n_w,)), pltpu.SemaphoreType.DMA((3 * n_w,)),
                   pltpu.SemaphoreType.DMA((n_w,))], start, finish,
                  aliases={n_w + i: i for i in range(n_w)} if q_in else None)


def _chip_sum(w, q, c_arr):
    hr, cols = w.half_rows, w.shard_shape[1]
    tr, tc = _tile(hr, 512), _tile(cols, 2048)
    n_r = hr // tr

    def body(c_ref, q0, q1, q2, q3, out_ref):
        out_ref[...] = ((q0[0].astype(F32) + q1[0].astype(F32)) + q2[0].astype(F32)) + q3[0].astype(F32)

    q_specs = [pl.BlockSpec((1, tr, tc), lambda i, j, c, s=s: (s, i, j)) for s in range(N_CHIPS)]
    return _pcall(
        body, name="grad_chip_sum_" + w.name, out_shape=jax.ShapeDtypeStruct(w.shard_shape, F32),
        grid_spec=pltpu.PrefetchScalarGridSpec(
            num_scalar_prefetch=1, grid=(n_r, cols // tc), in_specs=q_specs,
            out_specs=pl.BlockSpec((tr, tc), lambda i, j, c: (c[0] * n_r + i, j))),
        compiler_params=_params(("parallel", "parallel")),
    )(c_arr, q, q, q, q)


def _sf_rider(ws, grads):
    n_w = len(ws)

    def copy(g, sems, i, half):
        send_sems, recv_sems = sems
        x, y, c, _ = _place()
        h = c if half == "mine" else 1 - c
        reg = ws[i].shard_half(g[i], h)
        return pltpu.make_async_remote_copy(src_ref=reg, dst_ref=reg, send_sem=send_sems.at[i], recv_sem=recv_sems.at[i],
                                            device_id=(x, y, 1 - c), device_id_type=MESH)

    def start(_, g, sems):
        for i in range(n_w):
            copy(g, sems, i, "mine").start()

    def finish(_, g, sems):
        for i in range(n_w):
            copy(g, sems, i, "other").wait_recv()
            copy(g, sems, i, "mine").wait_send()

    return _Rider(grads, [jax.ShapeDtypeStruct(w.shard_shape, F32) for w in ws],
                  [pltpu.SemaphoreType.DMA((n_w,)), pltpu.SemaphoreType.DMA((n_w,))], start, finish,
                  aliases={i: i for i in range(n_w)})


def _adamw_math(w, g, m, v):
    m = ADAM_B1 * m + (1.0 - ADAM_B1) * g
    v = ADAM_B2 * v + (1.0 - ADAM_B2) * (g * g)
    m_hat = m / (1.0 - ADAM_B1 ** ADAM_STEP)
    v_hat = v / (1.0 - ADAM_B2 ** ADAM_STEP)
    delta = -ADAM_LR * (m_hat / (jnp.sqrt(v_hat) + ADAM_EPS) + ADAM_WD * w)
    return delta, m, v


def _adamw(name, w, g, m, v):
    R, C = w.shape
    tr, tc = _tile(R, 256), _tile(C, 2048)

    def body(w_ref, g_ref, m_ref, v_ref, d_out, m_out, v_out):
        d_out[...], m_out[...], v_out[...] = _adamw_math(w_ref[...], g_ref[...], m_ref[...], v_ref[...])

    spec = pl.BlockSpec((tr, tc), lambda i, j: (i, j))
    sh = jax.ShapeDtypeStruct((R, C), F32)
    return _pcall(body, name=name, grid=(R // tr, C // tc), in_specs=[spec] * 4, out_specs=[spec] * 3,
                  out_shape=[sh, sh, sh], compiler_params=_params(("parallel", "parallel")))(w, g, m, v)


def _ada_update(sct, dmod_sh, w, m, v, riders=()):
    R, C = w.shape
    tr, tc = _tile(R, 256), _tile(C, 1024)

    def body(s_ref, d_ref, w_ref, m_ref, v_ref, g_out, d_out, m_out, v_out):
        s, d = s_ref[...], d_ref[...]
        g = s[:, 0:1] * d[0:1, :]
        for b in range(1, N_DEV):
            g += s[:, b:b + 1] * d[b:b + 1, :]
        g_out[...] = g
        d_out[...], m_out[...], v_out[...] = _adamw_math(w_ref[...], g, m_ref[...], v_ref[...])

    spec = pl.BlockSpec((tr, tc), lambda i, j: (i, j))
    sh = jax.ShapeDtypeStruct((R, C), F32)
    return _ride(
        "ada_update", body, riders, [sct, dmod_sh, w, m, v], grid=(R // tr, C // tc),
        in_specs=[pl.BlockSpec((tr, N_DEV), lambda i, j: (i, 0)), pl.BlockSpec((N_DEV, tc), lambda i, j: (0, j)),
                  spec, spec, spec],
        out_specs=[spec] * 4, out_shape=[sh] * 4, scratch_shapes=[], sem=("parallel", "parallel"))


def _cast_bf16(name, w):
    R, C = w.shape
    tr, tc = _tile(R, 512), _tile(C, 2048)

    def body(w_ref, o_ref):
        o_ref[...] = w_ref[...].astype(BF16)

    spec = pl.BlockSpec((tr, tc), lambda i, j: (i, j))
    return _pcall(body, name=name, grid=(R // tr, C // tc), in_specs=[spec], out_specs=spec,
                  out_shape=jax.ShapeDtypeStruct((R, C), BF16), compiler_params=_params(("parallel", "parallel")))(w)


def _silu_rows(c_row):
    D = c_row.shape[1]

    def body(c_ref, o_ref):
        cv = c_ref[...]
        o_ref[...] = cv * jax.nn.sigmoid(cv)

    return _pcall(body, name="silu_c", out_shape=jax.ShapeDtypeStruct((1, D), F32))(c_row)


def _pack_partials(parts, widths, total):
    n = len(widths)

    def body(*refs):
        loss_p, out_ref, loss_ref = refs[n], refs[n + 1], refs[n + 2]
        off = 0
        for ref, wd in zip(refs[:n], widths):
            out_ref[:, off:off + wd] = jnp.sum(ref[...], axis=0)
            off += wd
        if off < total:
            out_ref[:, off:total] = jnp.zeros((1, total - off), F32)
        loss_ref[...] = jnp.sum(jnp.sum(loss_p[...], axis=0), axis=1, keepdims=True)

    return _pcall(body, name="pack_partials",
                  out_shape=[jax.ShapeDtypeStruct((1, total), F32), jax.ShapeDtypeStruct((1, 1), F32)])(*parts)


def _small_update(gathered, offsets, params):
    n_p = len(params)

    def body(*refs):
        g_ref = refs[0]
        prm = refs[1:1 + 3 * n_p]
        outs = refs[1 + 3 * n_p:]
        for i, (off, wd) in enumerate(offsets):
            blk = g_ref[:, off:off + wd]
            g = blk[0:1, :]
            for b in range(1, N_DEV):
                g = g + blk[b:b + 1, :]
            w, m, v = prm[3 * i][...], prm[3 * i + 1][...], prm[3 * i + 2][...]
            outs[4 * i][...] = g
            outs[4 * i + 1][...], outs[4 * i + 2][...], outs[4 * i + 3][...] = _adamw_math(w, g, m, v)

    flat = [a for t in params for a in t]
    out_shape = [jax.ShapeDtypeStruct(t[0].shape, F32) for t in params for _ in range(4)]
    return _pcall(body, name="small_update", out_shape=out_shape)(gathered, *flat)


def kernel(x, c, w_ada, b_ada, norm1_w, w_in, q_norm_w, k_norm_w, w_pool, pool_scale, w_a_up, w_b_up, w_o, norm2_w, w_ff1, w_ff2, loss_target, m_w_ada, m_b_ada, m_norm1_w, m_w_in, m_q_norm_w, m_k_norm_w, m_w_pool, m_pool_scale, m_w_a_up, m_w_b_up, m_w_o, m_norm2_w, m_w_ff1, m_w_ff2, v_w_ada, v_b_ada, v_norm1_w, v_w_in, v_q_norm_w, v_k_norm_w, v_w_pool, v_pool_scale, v_w_a_up, v_w_b_up, v_w_o, v_norm2_w, v_w_ff1, v_w_ff2):
    _, S, D = x.shape
    PW = D // 2
    H = PW // HEAD_DIM
    cg = PW // N_GROUPS
    IN = w_in.shape[2] * N_CHIPS
    FF = w_ff1.shape[2] * N_CHIPS
    A_COLS = w_ada.shape[2]
    xi, yi, ci = lax.axis_index("x"), lax.axis_index("y"), lax.axis_index("c")
    chip = 2 * xi + yi
    dev = 2 * chip + ci
    c_arr = jnp.reshape(ci, (1,)).astype(jnp.int32)
    x2, tgt = x[0], loss_target[0]

    ws = [_W("w_in", "col", D, IN), _W("w_pool", "row", PW, cg), _W("w_a_up", "col", PW, D),
          _W("w_b_up", "col", PW, D), _W("w_o", "row", D, D), _W("w_ff1", "col", D, FF), _W("w_ff2", "row", FF, D)]
    w32 = [w_in[0], w_pool[0].reshape(cg, cg), w_a_up[0], w_b_up[0], w_o[0], w_ff1[0], w_ff2[0]]
    m32 = [m_w_in[0], m_w_pool[0].reshape(cg, cg), m_w_a_up[0], m_w_b_up[0], m_w_o[0], m_w_ff1[0], m_w_ff2[0]]
    v32 = [v_w_in[0], v_w_pool[0].reshape(cg, cg), v_w_a_up[0], v_w_b_up[0], v_w_o[0], v_w_ff1[0], v_w_ff2[0]]

    W_IN, W_POOL, W_A, W_B, W_O, W_FF1, W_FF2 = ws
    s_in, s_pool, s_a, s_b, s_o, s_ff1, s_ff2 = [_cast_bf16("cast_" + w.name, a) for w, a in zip(ws, w32)]
    (win_f,) = _run_rider("gather_w_in", _ag_rider([W_IN], [s_in]))

    sc_row = _silu_rows(c)
    sc_all = _dev_allgather("gather_silu_c", sc_row.reshape(8, D // 8)).reshape(N_DEV, D)
    sc16 = jnp.concatenate([sc_all, jnp.zeros_like(sc_all)], axis=0)
    b_cols = lax.dynamic_slice(b_ada, (0, chip * A_COLS), (1, A_COLS))
    (mod_cols,) = _mm("mod_cols", [(sc16, w_ada[0])], M=2 * N_DEV, N=A_COLS, K=D, tm=16, tn=1024, tk=1024,
                      a_pro=lambda a: a.astype(BF16), b_pro=lambda b: b.astype(BF16),
                      extras=[(b_cols, "row", 0)], outs=[_tile_out(F32)], epi=lambda accs, ex: [accs[0] + ex[0]])
    mod_all = _dev_allgather("gather_mod", mod_cols[:N_DEV]).reshape(N_CHIPS, 2, N_DEV, A_COLS)
    mod_row = lax.dynamic_index_in_dim(mod_all[:, 0], dev, axis=1, keepdims=False).reshape(1, N_CHIPS * A_COLS)
    shift1, scale1, gate1, shift2, scale2, gate2 = [mod_row[:, i * D:(i + 1) * D] for i in range(6)]

    WIDE = dict(tm=2048, tn=512, tk=2048)
    DEEP = dict(tm=1024, tn=1024, tk=1024)
    h = _norm_mod("norm1_mod", x2, norm1_w, scale1, shift1)
    (proj,), ((wpool_f, wa_f, wb_f, wo_f),) = _mm(
        "in_proj", [(h, win_f)], M=S, N=IN, K=D, outs=[_tile_out(BF16)], epi=lambda accs, ex: [accs[0]], **WIDE,
        riders=[_ag_rider([W_POOL, W_A, W_B, W_O], [s_pool, s_a, s_b, s_o], n_ch=2)])
    pooled, pa = _pool_fwd(proj, wpool_f, pool_scale, S, PW)
    (att, attf), ((wff1_f,),) = _attn_fwd(proj, q_norm_w, k_norm_w, S, H, PW // HEAD_DIM,
                                          riders=[_ag_rider([W_FF1], [s_ff1])])

    def merge_epi(accs, ex):
        sa, sb = jax.nn.sigmoid(ex[0].astype(F32)), jax.nn.sigmoid(ex[1].astype(F32))
        return [sa * accs[0] + sb * accs[1], accs[0], accs[1]]

    merged, ya, yb = _mm("branch_up_merge", [(pa, wa_f), (att, wb_f)], M=S, N=D, K=PW,
                         extras=[(proj, "tile", 4 * PW), (proj, "tile", 4 * PW + D)],
                         outs=[_tile_out(BF16)] * 3, epi=merge_epi)
    x1, o = _mm("out_proj", [(merged, wo_f)], M=S, N=D, K=D, extras=[(x2, "tile", 0), (gate1, "row", 0)], **WIDE,
                outs=[_tile_out(F32), _tile_out(BF16)], epi=lambda accs, ex: [ex[0] + ex[1] * accs[0], accs[0]])
    h2 = _norm_mod("norm2_mod", x1, norm2_w, scale2, shift2)
    (rl,), ((wff2_f,),) = _mm("ff1", [(h2, wff1_f)], M=S, N=FF, K=D, outs=[_tile_out(BF16)], **WIDE,
                              epi=lambda accs, ex: [jnp.maximum(accs[0], 0.0)],
                              riders=[_ag_rider([W_FF2], [s_ff2])])

    def square(a):
        af = a.astype(F32)
        return (af * af).astype(BF16)

    def loss_epi(accs, ex):
        x1_t, tgt_t, g2 = ex
        f = accs[0]
        diff = (x1_t + g2 * f) - tgt_t
        dy = diff * (1.0 / D)
        return [dy, dy * g2, _colsum(dy * f), _colsum(diff * diff)]

    dy, df, dgate2_p, loss_p = _mm("ff2_loss", [(rl, wff2_f)], M=S, N=D, K=FF, a_pro=square, tm=1024, tn=1024, tk=512,
                                   extras=[(x1, "tile", 0), (tgt, "tile", 0), (gate2, "row", 0)],
                                   outs=[_tile_out(F32), _tile_out(BF16), _COLSUM, _COLSUM], epi=loss_epi)

    def pair_sums(group, partials, got):
        return [_pair_sum(w, g, r, c_arr) for w, g, r in zip(group, partials, got)]

    first = lambda accs, ex: [accs[0]]
    gmm = dict(ta=True, outs=[_tile_out(BF16)], epi=first, **WIDE)
    (g_ff2,) = _mm("grad_w_ff2", [(rl, df)], M=FF, N=D, K=S, a_pro=square, ta=True, tm=512, tn=2048, tk=2048,
                   outs=[_tile_out(BF16)], epi=first)
    (dz1,), (got_ff2,) = _mm("d_ff_hidden", [(df, wff2_f)], M=S, N=FF, K=D, tb=True, extras=[(rl, "tile", 0)], **WIDE,
                             outs=[_tile_out(BF16)], epi=lambda accs, ex: [accs[0] * (2.0 * ex[0].astype(F32))],
                             riders=[_px_rider([W_FF2], [g_ff2])])
    sum_ff2 = pair_sums([W_FF2], [g_ff2], got_ff2)
    (g_ff1,), (q_ff2,) = _mm("grad_w_ff1", [(h2, dz1)], M=D, N=FF, K=S,
                             riders=[_cx_rider([W_FF2], sum_ff2, part=(0, 2))], **gmm)
    (dh2,), (got_ff1, q_ff2) = _mm("d_h2", [(dz1, wff1_f)], M=S, N=D, K=FF, tb=True, outs=[_tile_out(F32)], epi=first,
                                   riders=[_px_rider([W_FF1], [g_ff1]),
                                           _cx_rider([W_FF2], sum_ff2, part=(1, 2), q_in=q_ff2)], **DEEP)
    sum_ff1 = pair_sums([W_FF1], [g_ff1], got_ff1)
    dx1, dshift2_p, dscale2_p, gn2_p, do, dgate1_p = _norm_mod_bwd("norm2_bwd", dh2, x1, dy, norm2_w, scale2,
                                                                   gate_o=(o, gate1))
    (g_wo,) = _mm("grad_w_o", [(merged, do)], M=D, N=D, K=S, **gmm)

    def gate_epi(accs, ex):
        dm = accs[0]
        sa, sb = jax.nn.sigmoid(ex[0].astype(F32)), jax.nn.sigmoid(ex[1].astype(F32))
        ya_t, yb_t = ex[2].astype(F32), ex[3].astype(F32)
        return [dm * sa, dm * sb, dm * ya_t * (sa * (1.0 - sa)), dm * yb_t * (sb * (1.0 - sb))]

    dya, dyb, dga, dgb = _mm("d_merged", [(do, wo_f)], M=S, N=D, K=D, tb=True, tm=1024, tn=512, tk=2048,
                             extras=[(proj, "tile", 4 * PW), (proj, "tile", 4 * PW + D), (ya, "tile", 0), (yb, "tile", 0)],
                             outs=[_tile_out(BF16)] * 4, epi=gate_epi)
    (g_wa,) = _mm("grad_w_a_up", [(pa, dya)], M=PW, N=D, K=S, **gmm)
    (g_wb,) = _mm("grad_w_b_up", [(att, dyb)], M=PW, N=D, K=S, **gmm)
    (dpa,) = _mm("d_pool_out", [(dya, wa_f)], M=S, N=PW, K=D, tb=True, outs=[_tile_out(F32)], epi=first, **WIDE)
    mid = [W_A, W_B, W_O]
    (datt,), (got_mid,) = _mm("d_att", [(dyb, wb_f)], M=S, N=PW, K=D, tb=True, outs=[_tile_out(BF16)], epi=first, **WIDE,
                              riders=[_px_rider(mid, [g_wa, g_wb, g_wo])])
    sum_mid = pair_sums(mid, [g_wa, g_wb, g_wo], got_mid)
    du, g_wpool4, gscale_p = _pool_bwd(dpa, pooled, wpool_f, pool_scale, S, PW)
    (dq, dk, dv, gq_p, gk_p), ((q_ff1, q_wa, q_wb, q_wo),) = _attn_bwd(
        proj, datt, attf, q_norm_w, k_norm_w, S, H, PW // HEAD_DIM, riders=[_cx_rider([W_FF1] + mid, sum_ff1 + sum_mid)])
    dproj = jnp.concatenate([du, dq, dk, dv, dga, dgb], axis=1)
    early = mid + [W_FF1, W_FF2]
    halves_early = [_chip_sum(w, q, c_arr) for w, q in zip(early, [q_wa, q_wb, q_wo, q_ff1, q_ff2[0]])]
    (g_win,), (grads_early,) = _mm("grad_w_in", [(h, dproj)], M=D, N=IN, K=S, riders=[_sf_rider(early, halves_early)],
                                   **gmm)
    last = [W_IN, W_POOL]
    g_last = [g_win, g_wpool4.reshape(PW, cg)]
    dh_kw = dict(M=S, N=D // 2, K=IN, tb=True, outs=[_tile_out(F32)], epi=first, **DEEP)
    (dh_left,), (got_last,) = _mm("d_h_left", [(dproj, win_f)], riders=[_px_rider(last, g_last)], **dh_kw)
    sum_last = pair_sums(last, g_last, got_last)
    (dh_right,), ((q_win, q_wpool),) = _mm("d_h_right", [(dproj, win_f)], b_noff=D // 2,
                                           riders=[_cx_rider(last, sum_last)], **dh_kw)
    grad_x, dshift1_p, dscale1_p, gn1_p = _norm_mod_bwd("norm1_bwd", (dh_left, dh_right), x2, dx1, norm1_w, scale1)

    parts = [dshift1_p, dscale1_p, dgate1_p, dshift2_p, dscale2_p, dgate2_p, gn1_p, gn2_p,
             gscale_p.reshape(1, 1, PW), gq_p, gk_p]
    widths = [D] * 8 + [PW, HEAD_DIM, HEAD_DIM]
    used = sum(widths)
    P = -(-used // 1024) * 1024
    packed, loss_part = _pack_partials(parts + [loss_p], widths, P)
    gathered = _dev_allgather("gather_vector_grads", packed.reshape(8, P // 8)).reshape(N_DEV, P)
    small = [(b_ada, m_b_ada, v_b_ada), (norm1_w, m_norm1_w, v_norm1_w), (norm2_w, m_norm2_w, v_norm2_w),
             (pool_scale, m_pool_scale, v_pool_scale), (q_norm_w, m_q_norm_w, v_q_norm_w),
             (k_norm_w, m_k_norm_w, v_k_norm_w)]
    offsets = [(0, 6 * D), (6 * D, D), (7 * D, D), (8 * D, PW), (8 * D + PW, HEAD_DIM), (8 * D + PW + HEAD_DIM, HEAD_DIM)]
    su = _small_update(gathered, offsets, small)
    (g_b, d_b, nm_b, nv_b, g_n1, d_n1, nm_n1, nv_n1, g_n2, d_n2, nm_n2, nv_n2, g_ps, d_ps, nm_ps, nv_ps,
     g_qn, d_qn, nm_qn, nv_qn, g_kn, d_kn, nm_kn, nv_kn) = su
    dmod_sh = lax.dynamic_slice(gathered, (0, chip * A_COLS), (N_DEV, A_COLS))
    g_ada, d_ada, nm_ada, nv_ada = _ada_update(sc_all.T, dmod_sh, w_ada[0], m_w_ada[0], v_w_ada[0])

    halves_last = [_chip_sum(w, q, c_arr) for w, q in zip(last, [q_win, q_wpool])]
    grads = list(_run_rider("grad_sibling_fill", _sf_rider(last, halves_last))) + list(grads_early)
    upd = [_adamw("adamw_" + w.name, a, g, m, v) for w, a, g, m, v in zip(ws, w32, grads, m32, v32)]

    loss = 0.5 / D * lax.psum(loss_part[0, 0], ("x", "y", "c"))

    def up(a):
        return a[None]

    def pool4(a):
        return a.reshape(1, N_GROUPS, cg // N_CHIPS, cg)

    (d_win, nm_win, nv_win), (d_wp, nm_wp, nv_wp), (d_wa, nm_wa, nv_wa), (d_wb, nm_wb, nv_wb), \
        (d_wo, nm_wo, nv_wo), (d_f1, nm_f1, nv_f1), (d_f2, nm_f2, nv_f2) = upd
    gr_win, gr_wp, gr_wa, gr_wb, gr_wo, gr_f1, gr_f2 = grads
    return (
        loss, grad_x[None],
        up(g_ada), g_b, g_n1, up(gr_win), g_qn, g_kn, pool4(gr_wp), g_ps, up(gr_wa), up(gr_wb), up(gr_wo), g_n2,
        up(gr_f1), up(gr_f2),
        up(d_ada), d_b, d_n1, up(d_win), d_qn, d_kn, pool4(d_wp), d_ps, up(d_wa), up(d_wb), up(d_wo), d_n2,
        up(d_f1), up(d_f2),
        up(nm_ada), nm_b, nm_n1, up(nm_win), nm_qn, nm_kn, pool4(nm_wp), nm_ps, up(nm_wa), up(nm_wb), up(nm_wo), nm_n2,
        up(nm_f1), up(nm_f2),
        up(nv_ada), nv_b, nv_n1, up(nv_win), nv_qn, nv_kn, pool4(nv_wp), nv_ps, up(nv_wa), up(nv_wb), up(nv_wo), nv_n2,
        up(nv_f1), up(nv_f2),
    )
```

```python
import functools
import math

import jax
import jax.numpy as jnp
from jax import lax
from jax.experimental import pallas as pl
from jax.experimental.pallas import tpu as pltpu

F32 = jnp.float32
BF16 = jnp.bfloat16
MESH = pl.DeviceIdType.MESH
ANY = pl.BlockSpec(memory_space=pl.ANY)

EPS = 1e-6
HEAD_DIM = 128
POOL_WINDOWS = (2, 4, 8, 16)
N_GROUPS = len(POOL_WINDOWS)
N_CHIPS = 4
N_DEV = 8
ADAM_LR, ADAM_B1, ADAM_B2, ADAM_EPS, ADAM_WD, ADAM_STEP = 0.001, 0.9, 0.999, 1e-08, 0.01, 10
VMEM_LIMIT_V7X = 56 * 1024 * 1024
ATT_T = 256
POOL_T = 256


def _pcall(body, **kw):
    return pl.pallas_call(body, **kw)


def _params(sem=None):
    return pltpu.CompilerParams(dimension_semantics=sem, vmem_limit_bytes=VMEM_LIMIT_V7X)


def _tile(n, pref):
    if n <= pref:
        return n
    t = pref
    while n % t:
        t //= 2
    return t


class _Rider:
    def __init__(self, arrays, out_shape, sems, start, finish, aliases=None, steps=()):
        self.arrays, self.out_shape, self.sems = list(arrays), list(out_shape), list(sems)
        self.start, self.finish, self.aliases, self.steps = start, finish, aliases or {}, list(steps)


def _ride(name, body, riders, arrays, *, grid, in_specs, out_specs, out_shape, scratch_shapes, sem):
    n_in, n_out, n_scr = len(arrays), len(out_shape), len(scratch_shapes)
    r_arrays = [a for r in riders for a in r.arrays]
    r_outs = [o for r in riders for o in r.out_shape]
    r_sems = [s for r in riders for s in r.sems]
    n_hooks = max([len(r.steps) for r in riders], default=0)
    total = math.prod(grid)
    aliases, off_i, off_o = {}, n_in, n_out
    for r in riders:
        for a, o in r.aliases.items():
            aliases[off_i + a] = off_o + o
        off_i += len(r.arrays)
        off_o += len(r.out_shape)

    def full(*refs):
        p = 0
        groups = []
        for n in (n_in, len(r_arrays), n_out, len(r_outs), n_scr, len(r_sems)):
            groups.append(refs[p:p + n])
            p += n
        ins, rin, outs, rout, scr, rsem = groups

        def each(what):
            a = o = s = 0
            for r in riders:
                fn = what(r)
                if fn is not None:
                    fn(rin[a:a + len(r.arrays)], rout[o:o + len(r.out_shape)], rsem[s:s + len(r.sems)])
                a, o, s = a + len(r.arrays), o + len(r.out_shape), s + len(r.sems)

        if riders:
            lin = 0
            for d, g in enumerate(grid):
                lin = lin * g + pl.program_id(d)
            pl.when(lin == 0)(lambda: each(lambda r: r.start))
            for t in range(n_hooks):
                pl.when(lin == ((t + 1) * total) // (n_hooks + 1))(
                    lambda t=t: each(lambda r: r.steps[t] if t < len(r.steps) else None))
        body(*ins, *outs, *scr)
        if riders:
            pl.when(lin == total - 1)(lambda: each(lambda r: r.finish))

    res = _pcall(
        full, name=name, grid=grid, in_specs=list(in_specs) + [ANY] * len(r_arrays),
        out_specs=list(out_specs) + [ANY] * len(r_outs), out_shape=list(out_shape) + r_outs,
        scratch_shapes=list(scratch_shapes) + r_sems, input_output_aliases=aliases,
        compiler_params=_params(("arbitrary",) * len(grid) if riders else sem),
    )(*arrays, *r_arrays)
    if not riders:
        return res
    main, rest, per = res[:n_out], res[n_out:], []
    for r in riders:
        per.append(rest[:len(r.out_shape)])
        rest = rest[len(r.out_shape):]
    return main, per


def _run_rider(name, rider):
    def body(*refs):
        n_a, n_o = len(rider.arrays), len(rider.out_shape)
        ins, outs, sems = refs[:n_a], refs[n_a:n_a + n_o], refs[n_a + n_o:]
        for fn in [rider.start] + rider.steps + [rider.finish]:
            fn(ins, outs, sems)

    return _pcall(body, name=name, out_shape=rider.out_shape, in_specs=[ANY] * len(rider.arrays),
                  out_specs=[ANY] * len(rider.out_shape), scratch_shapes=rider.sems,
                  input_output_aliases=rider.aliases)(*rider.arrays)


def _mm(name, pairs, *, M, N, K, ta=False, tb=False, tm=512, tn=1024, tk=1024,
        a_pro=None, b_pro=None, extras=(), outs, epi, riders=(), b_noff=0):
    tm, tn, tk = _tile(M, tm), _tile(N, tn), _tile(K, tk)
    n_i, n_j, n_k = M // tm, N // tn, K // tk
    n_p, n_e = len(pairs), len(extras)
    arrays, in_specs = [], []
    for a, _ in pairs:
        arrays.append(a)
        in_specs.append(pl.BlockSpec((tk, tm), lambda i, j, k: (k, i)) if ta
                        else pl.BlockSpec((tm, tk), lambda i, j, k: (i, k)))
    for _, b in pairs:
        arrays.append(b)
        in_specs.append(pl.BlockSpec((tn, tk), lambda i, j, k: (j + b_noff // tn, k)) if tb
                        else pl.BlockSpec((tk, tn), lambda i, j, k: (k, j + b_noff // tn)))
    for arr, kind, off in extras:
        ob = off // tn
        assert off % tn == 0
        arrays.append(arr)
        if kind == "tile":
            in_specs.append(pl.BlockSpec((tm, tn), lambda i, j, k, ob=ob: (i, j + ob)))
        else:
            in_specs.append(pl.BlockSpec((1, tn), lambda i, j, k, ob=ob: (0, j + ob)))
    out_shape, out_specs = [], []
    for o in outs:
        if o["kind"] == "tile":
            out_shape.append(jax.ShapeDtypeStruct((M, N), o["dtype"]))
            out_specs.append(pl.BlockSpec((tm, tn), lambda i, j, k: (i, j)))
        else:
            out_shape.append(jax.ShapeDtypeStruct((n_i, 1, N), F32))
            out_specs.append(pl.BlockSpec((1, 1, tn), lambda i, j, k: (i, 0, j)))
    dims = (((0 if ta else 1,), (1 if tb else 0,)), ((), ()))

    def body(*refs):
        a_refs, b_refs = refs[:n_p], refs[n_p:2 * n_p]
        e_refs = refs[2 * n_p:2 * n_p + n_e]
        o_refs = refs[2 * n_p + n_e:2 * n_p + n_e + len(outs)]
        acc_refs = refs[2 * n_p + n_e + len(outs):]

        def product(p):
            a, b = a_refs[p][...], b_refs[p][...]
            if a_pro is not None:
                a = a_pro(a)
            if b_pro is not None:
                b = b_pro(b)
            return lax.dot_general(a, b, dims, preferred_element_type=F32)

        def write(accs):
            vals = epi(accs, [e[...] for e in e_refs])
            for o, o_ref, val in zip(outs, o_refs, vals):
                if o["kind"] == "tile":
                    o_ref[...] = val.astype(o_ref.dtype)
                else:
                    o_ref[0] = val

        if n_k == 1:
            write([product(p) for p in range(n_p)])
            return
        k = pl.program_id(2)

        @pl.when(k == 0)
        def _():
            for acc in acc_refs:
                acc[...] = jnp.zeros_like(acc)

        for p in range(n_p):
            acc_refs[p][...] += product(p)

        pl.when(k == n_k - 1)(lambda: write([acc[...] for acc in acc_refs]))

    return _ride(name, body, riders, arrays, grid=(n_i, n_j, n_k), in_specs=in_specs, out_specs=out_specs,
                 out_shape=out_shape, scratch_shapes=[pltpu.VMEM((tm, tn), F32) for _ in pairs] if n_k > 1 else [],
                 sem=("parallel", "parallel", "arbitrary"))


def _tile_out(dtype):
    return {"kind": "tile", "dtype": dtype}


_COLSUM = {"kind": "colsum"}


def _colsum(v):
    return jnp.sum(v, axis=0, keepdims=True)


def _norm_mod(name, x, norm_w, scale, shift):
    S, D = x.shape
    tr = _tile(S, 256)

    def body(x_ref, nw_ref, sc_ref, sh_ref, h_ref):
        xv = x_ref[...]
        r = lax.rsqrt(jnp.mean(xv * xv, axis=-1, keepdims=True) + EPS)
        h_ref[...] = ((xv * r * nw_ref[...]) * (1.0 + sc_ref[...]) + sh_ref[...]).astype(BF16)

    row = pl.BlockSpec((1, D), lambda i: (0, 0))
    til = pl.BlockSpec((tr, D), lambda i: (i, 0))
    return _pcall(body, name=name, grid=(S // tr,), in_specs=[til, row, row, row], out_specs=til,
                  out_shape=jax.ShapeDtypeStruct((S, D), BF16), compiler_params=_params(("parallel",)))(
                      x, norm_w, scale, shift)


def _norm_mod_bwd(name, dh, x, dres, norm_w, scale, gate_o=None):
    S, D = x.shape
    tr = _tile(S, 256)
    n_r = S // tr
    with_gate = gate_o is not None
    dh = list(dh) if isinstance(dh, (list, tuple)) else [dh]
    n_dh = len(dh)

    def body(*refs):
        dh_refs, refs = refs[:n_dh], refs[n_dh:]
        if with_gate:
            x_ref, dres_ref, nw_ref, sc_ref, o_ref, g_ref, dx_ref, p1, p2, p3, do_ref, p4 = refs
        else:
            x_ref, dres_ref, nw_ref, sc_ref, dx_ref, p1, p2, p3 = refs
        dhv = dh_refs[0][...] if n_dh == 1 else jnp.concatenate([r[...] for r in dh_refs], axis=1)
        xv, nw = x_ref[...], nw_ref[...]
        r = lax.rsqrt(jnp.mean(xv * xv, axis=-1, keepdims=True) + EPS)
        xh = xv * r
        p1[0] = _colsum(dhv)
        p2[0] = _colsum(dhv * (xh * nw))
        dn = dhv * (1.0 + sc_ref[...])
        p3[0] = _colsum(dn * xh)
        dxh = dn * nw
        dx = dres_ref[...] + r * (dxh - xh * jnp.mean(dxh * xh, axis=-1, keepdims=True))
        dx_ref[...] = dx
        if with_gate:
            do_ref[...] = (dx * g_ref[...]).astype(BF16)
            p4[0] = _colsum(dx * o_ref[...].astype(F32))

    row = pl.BlockSpec((1, D), lambda i: (0, 0))
    til = pl.BlockSpec((tr, D), lambda i: (i, 0))
    part = pl.BlockSpec((1, 1, D), lambda i: (i, 0, 0))
    part_shape = jax.ShapeDtypeStruct((n_r, 1, D), F32)
    in_specs = [pl.BlockSpec((tr, D // n_dh), lambda i: (i, 0))] * n_dh + [til, til, row, row]
    arrays = dh + [x, dres, norm_w, scale]
    out_specs = [til, part, part, part]
    out_shape = [jax.ShapeDtypeStruct((S, D), F32), part_shape, part_shape, part_shape]
    if with_gate:
        in_specs += [til, row]
        arrays += list(gate_o)
        out_specs += [til, part]
        out_shape += [jax.ShapeDtypeStruct((S, D), BF16), part_shape]
    return _pcall(body, name=name, grid=(n_r,), in_specs=in_specs, out_specs=out_specs, out_shape=out_shape,
                  compiler_params=_params(("parallel",)))(*arrays)


def _pool_w_specs(rows, cg):
    return [pl.BlockSpec((rows, cg), lambda g, j=j: (N_GROUPS * j + g, 0)) for j in range(N_CHIPS)]


def _pool_fwd(proj, wp_full, pool_scale, S, PW):
    cg = PW // N_GROUPS
    rows = cg // N_CHIPS
    T = _tile(S, POOL_T)
    n_t = S // T

    def body(u_ref, w0, w1, w2, w3, ps_ref, pooled_ref, pa_ref):
        g = pl.program_id(0)
        win = jnp.left_shift(2, g)
        w = jnp.concatenate([w0[...], w1[...], w2[...], w3[...]], axis=0)
        t_i = lax.broadcasted_iota(jnp.int32, (T, T), 0)
        j_i = lax.broadcasted_iota(jnp.int32, (T, T), 1)
        b_cur = ((j_i <= t_i) & (j_i > t_i - win)).astype(BF16)
        b_prev = (j_i - T > t_i - win).astype(BF16)
        row = lax.broadcasted_iota(jnp.int32, (T, 1), 0)
        for r in range(n_t):
            cur = u_ref[r * T:(r + 1) * T, :]
            ws = jnp.dot(b_cur, cur, preferred_element_type=F32)
            if r > 0:
                ws += jnp.dot(b_prev, u_ref[(r - 1) * T:r * T, :], preferred_element_type=F32)
            count = jnp.minimum(row + (r * T + 1), win).astype(F32)
            pooled = (ws / count - cur.astype(F32)).astype(BF16)
            pooled_ref[r * T:(r + 1) * T, :] = pooled
            mixed = jnp.dot(pooled, w, preferred_element_type=F32)
            pa_ref[r * T:(r + 1) * T, :] = (mixed * ps_ref[...]).astype(BF16)

    col = pl.BlockSpec((S, cg), lambda g: (0, g))
    return _pcall(
        body, name="pool_fwd", grid=(N_GROUPS,),
        in_specs=[col] + _pool_w_specs(rows, cg) + [pl.BlockSpec((1, cg), lambda g: (0, g))],
        out_specs=[col, col],
        out_shape=[jax.ShapeDtypeStruct((S, PW), BF16), jax.ShapeDtypeStruct((S, PW), BF16)],
        compiler_params=_params(("parallel",)),
    )(proj, wp_full, wp_full, wp_full, wp_full, pool_scale)


def _pool_bwd(dpa, pooled, wp_full, pool_scale, S, PW):
    cg = PW // N_GROUPS
    rows = cg // N_CHIPS
    T = _tile(S, POOL_T)
    n_t = S // T

    def body(dpa_ref, pooled_ref, w0, w1, w2, w3, ps_ref, du_ref, gw_ref, gs_ref, dp_s, dpc_s, dmx_s):
        g = pl.program_id(0)
        win = jnp.left_shift(2, g)
        w = jnp.concatenate([w0[...], w1[...], w2[...], w3[...]], axis=0)
        row = lax.broadcasted_iota(jnp.int32, (T, 1), 0)
        gs = jnp.zeros((1, cg), F32)
        for r in range(n_t):
            sl = slice(r * T, (r + 1) * T)
            mixed = jnp.dot(pooled_ref[sl, :], w, preferred_element_type=F32)
            dpa_t = dpa_ref[sl, :]
            gs += _colsum(dpa_t * mixed)
            dmx = (dpa_t * ps_ref[...]).astype(BF16)
            dmx_s[sl, :] = dmx
            dpo = lax.dot_general(dmx, w, (((1,), (1,)), ((), ())), preferred_element_type=F32)
            dp_s[sl, :] = dpo
            count = jnp.minimum(row + (r * T + 1), win).astype(F32)
            dpc_s[sl, :] = (dpo / count).astype(BF16)
        gs_ref[...] = gs
        gw = lax.dot_general(pooled_ref[...], dmx_s[...], (((0,), (0,)), ((), ())), preferred_element_type=F32)
        for j in range(N_CHIPS):
            gw_ref[j, 0] = gw[j * rows:(j + 1) * rows, :].astype(BF16)
        j_i = lax.broadcasted_iota(jnp.int32, (T, T), 0)
        t_i = lax.broadcasted_iota(jnp.int32, (T, T), 1)
        b_cur = ((t_i >= j_i) & (t_i < j_i + win)).astype(BF16)
        b_next = (t_i + T < j_i + win).astype(BF16)
        for r in range(n_t):
            sl = slice(r * T, (r + 1) * T)
            acc = jnp.dot(b_cur, dpc_s[sl, :], preferred_element_type=F32)
            if r + 1 < n_t:
                acc += jnp.dot(b_next, dpc_s[(r + 1) * T:(r + 2) * T, :], preferred_element_type=F32)
            du_ref[sl, :] = (acc - dp_s[sl, :]).astype(BF16)

    col = pl.BlockSpec((S, cg), lambda g: (0, g))
    return _pcall(
        body, name="pool_bwd", grid=(N_GROUPS,),
        in_specs=[col, col] + _pool_w_specs(rows, cg) + [pl.BlockSpec((1, cg), lambda g: (0, g))],
        out_specs=[col, pl.BlockSpec((N_CHIPS, 1, rows, cg), lambda g: (0, g, 0, 0)),
                   pl.BlockSpec((1, cg), lambda g: (0, g))],
        out_shape=[jax.ShapeDtypeStruct((S, PW), BF16),
                   jax.ShapeDtypeStruct((N_CHIPS, N_GROUPS, rows, cg), BF16),
                   jax.ShapeDtypeStruct((1, PW), F32)],
        scratch_shapes=[pltpu.VMEM((S, cg), F32), pltpu.VMEM((S, cg), BF16), pltpu.VMEM((S, cg), BF16)],
        compiler_params=_params(("parallel",)),
    )(dpa, pooled, wp_full, wp_full, wp_full, wp_full, pool_scale)


_NT = (((1,), (1,)), ((), ()))
_TN = (((0,), (0,)), ((), ()))


def _split_dot(v, tri):
    hi = v.astype(BF16)
    lo = (v - hi.astype(F32)).astype(BF16)
    return jnp.dot(hi, tri, preferred_element_type=F32) + jnp.dot(lo, tri, preferred_element_type=F32)


def _sb_scores(q_i, k_j, tri_l, masked):
    tq, tk = q_i.shape[0], k_j.shape[0]
    s = lax.dot_general(q_i, k_j, _NT, preferred_element_type=F32) * (1.0 / math.sqrt(HEAD_DIM))
    lp = jnp.log(1.0 + jnp.exp(-jnp.abs(s)))
    l = -jnp.maximum(s, 0.0) - lp
    lb = l + s
    mask = None
    if masked:
        mask = lax.broadcasted_iota(jnp.int32, (tq, tk), 0) > lax.broadcasted_iota(jnp.int32, (tq, tk), 1)
        l = jnp.where(mask, l, 0.0)
    return l, lb, lb + _split_dot(l, tri_l), mask


def _sb_weights(t, carry_l, mask):
    a = jnp.exp(t + carry_l)
    return a if mask is None else jnp.where(mask, a, 0.0)


def _rowsum(v):
    return jnp.sum(v, axis=1, keepdims=True)


def _qk_norm(x_ref, w_ref):
    xv = x_ref[...].astype(F32)
    r = lax.rsqrt(jnp.mean(xv * xv, axis=-1, keepdims=True) + EPS)
    return xv * r, r


def _attn_fwd(proj, q_norm_w, k_norm_w, S, H, q_off, riders=()):
    t = _tile(S, ATT_T)
    n_q = S // t

    def body(q_ref, k_ref, v_ref, qw_ref, kw_ref, att_ref, attf_ref, qn_s, kn_s):
        qh, _ = _qk_norm(q_ref, qw_ref)
        qn_s[...] = (qh * qw_ref[...]).astype(BF16)
        kh, _ = _qk_norm(k_ref, kw_ref)
        kn_s[...] = (kh * kw_ref[...]).astype(BF16)
        tri_l = (lax.broadcasted_iota(jnp.int32, (t, t), 0) > lax.broadcasted_iota(jnp.int32, (t, t), 1)).astype(BF16)

        def rows(j):
            return pl.ds(pl.multiple_of(j * t, t), t)

        def q_step(i, _):
            q_i = qn_s[rows(i), :]

            def av(a, j):
                return jnp.dot(a.astype(BF16), v_ref[rows(j), :], preferred_element_type=F32)

            l, _, tt, mask = _sb_scores(q_i, kn_s[rows(i), :], tri_l, True)
            acc = av(_sb_weights(tt, 0.0, mask), i)
            carry = _rowsum(l)

            def single(_, c):
                carry, acc = c
                l, _, tt, _ = _sb_scores(q_i, kn_s[rows(i - 1), :], tri_l, False)
                return carry + _rowsum(l), acc + av(_sb_weights(tt, carry, None), i - 1)

            carry, acc = lax.fori_loop(0, i % 2, single, (carry, acc))
            top = i - 1 - i % 2

            def pair(p, c):
                carry, acc = c
                j0 = top - 2 * p
                l0, _, t0, _ = _sb_scores(q_i, kn_s[rows(j0), :], tri_l, False)
                l1, _, t1, _ = _sb_scores(q_i, kn_s[rows(j0 - 1), :], tri_l, False)
                mid = carry + _rowsum(l0)
                acc = acc + av(_sb_weights(t0, carry, None), j0) + av(_sb_weights(t1, mid, None), j0 - 1)
                return mid + _rowsum(l1), acc

            _, acc = lax.fori_loop(0, i // 2, pair, (carry, acc))
            att_ref[rows(i), :] = acc.astype(BF16)
            attf_ref[rows(i), :] = acc
            return 0

        lax.fori_loop(0, n_q, q_step, 0)

    def col(off):
        return pl.BlockSpec((S, HEAD_DIM), lambda h, off=off: (0, off + h))

    wspec = pl.BlockSpec((1, HEAD_DIM), lambda h: (0, 0))
    return _ride(
        "attn_fwd", body, riders, [proj, proj, proj, q_norm_w, k_norm_w], grid=(H,),
        in_specs=[col(q_off), col(q_off + H), col(q_off + 2 * H), wspec, wspec],
        out_specs=[col(0), col(0)],
        out_shape=[jax.ShapeDtypeStruct((S, H * HEAD_DIM), BF16), jax.ShapeDtypeStruct((S, H * HEAD_DIM), F32)],
        scratch_shapes=[pltpu.VMEM((S, HEAD_DIM), BF16), pltpu.VMEM((S, HEAD_DIM), BF16)],
        sem=("parallel",))


def _attn_bwd(proj, datt, attf, q_norm_w, k_norm_w, S, H, q_off, riders=()):
    t = _tile(S, ATT_T)
    n_q = S // t
    scale = 1.0 / math.sqrt(HEAD_DIM)

    def body(q_ref, k_ref, v_ref, do_ref, o_ref, qw_ref, kw_ref, dq_ref, dk_ref, dv_ref, gq_ref, gk_ref,
             qn_s, kn_s, dk_s, dv_s, gq_s):
        qw, kw = qw_ref[...], kw_ref[...]
        qh, _ = _qk_norm(q_ref, qw_ref)
        qn_s[...] = (qh * qw).astype(BF16)
        kh, _ = _qk_norm(k_ref, kw_ref)
        kn_s[...] = (kh * kw).astype(BF16)
        dk_s[...] = jnp.zeros_like(dk_s)
        dv_s[...] = jnp.zeros_like(dv_s)
        gq_s[...] = jnp.zeros_like(gq_s)
        r_i = lax.broadcasted_iota(jnp.int32, (t, t), 0)
        c_i = lax.broadcasted_iota(jnp.int32, (t, t), 1)
        tri_l = (r_i > c_i).astype(BF16)
        tri_e = (r_i >= c_i).astype(BF16)

        def rows(j):
            return pl.ds(pl.multiple_of(j * t, t), t)

        def q_step(i, _):
            q_i = qn_s[rows(i), :]
            do_i = do_ref[rows(i), :]
            d_i = _rowsum(do_i.astype(F32) * o_ref[rows(i), :])

            def scores(j, masked):
                k_j = kn_s[rows(j), :]
                l, lb, tt, mask = _sb_scores(q_i, k_j, tri_l, masked)
                da = lax.dot_general(do_i, v_ref[rows(j), :], _NT, preferred_element_type=F32)
                return k_j, l, lb, tt, mask, da

            def grads(j, sc, carry_l, carry_e, dq_acc):
                k_j, l, lb, tt, mask, da = sc
                a_bf = _sb_weights(tt, carry_l, mask).astype(BF16)
                e = da * a_bf.astype(F32)
                p = d_i - (_split_dot(e, tri_e) + carry_e)
                sig = jnp.exp(lb)
                dz = e * (1.0 - sig) - p * sig
                if mask is not None:
                    dz = jnp.where(mask, dz, 0.0)
                dz = (dz * scale).astype(BF16)
                dk_s[rows(j), :] += lax.dot_general(dz, q_i, _TN, preferred_element_type=F32)
                dv_s[rows(j), :] += lax.dot_general(a_bf, do_i, _TN, preferred_element_type=F32)
                return (carry_l + _rowsum(l), carry_e + _rowsum(e),
                        dq_acc + jnp.dot(dz, k_j, preferred_element_type=F32))

            c = grads(i, scores(i, True), 0.0, 0.0, jnp.zeros((t, HEAD_DIM), F32))
            c = lax.fori_loop(0, i % 2, lambda _, c: grads(i - 1, scores(i - 1, False), *c), c)
            top = i - 1 - i % 2

            def pair(p, c):
                j0 = top - 2 * p
                s0, s1 = scores(j0, False), scores(j0 - 1, False)
                return grads(j0 - 1, s1, *grads(j0, s0, *c))

            _, _, dqn = lax.fori_loop(0, i // 2, pair, c)
            qv = q_ref[rows(i), :].astype(F32)
            r = lax.rsqrt(jnp.mean(qv * qv, axis=-1, keepdims=True) + EPS)
            xh = qv * r
            gq_s[...] += _colsum(dqn * xh)
            dxh = dqn * qw
            dq_ref[rows(i), :] = (r * (dxh - xh * jnp.mean(dxh * xh, axis=-1, keepdims=True))).astype(BF16)
            return 0

        lax.fori_loop(0, n_q, q_step, 0)
        gq_ref[0] = gq_s[...]
        kh, rk = _qk_norm(k_ref, kw_ref)
        dkn = dk_s[...]
        gk_ref[0] = _colsum(dkn * kh)
        dxh = dkn * kw
        dk_ref[...] = (rk * (dxh - kh * jnp.mean(dxh * kh, axis=-1, keepdims=True))).astype(BF16)
        dv_ref[...] = dv_s[...].astype(BF16)

    def col(off):
        return pl.BlockSpec((S, HEAD_DIM), lambda h, off=off: (0, off + h))

    wspec = pl.BlockSpec((1, HEAD_DIM), lambda h: (0, 0))
    gspec = pl.BlockSpec((1, 1, HEAD_DIM), lambda h: (h, 0, 0))
    act = jax.ShapeDtypeStruct((S, H * HEAD_DIM), BF16)
    gsh = jax.ShapeDtypeStruct((H, 1, HEAD_DIM), F32)
    return _ride(
        "attn_bwd", body, riders, [proj, proj, proj, datt, attf, q_norm_w, k_norm_w], grid=(H,),
        in_specs=[col(q_off), col(q_off + H), col(q_off + 2 * H), col(0), col(0), wspec, wspec],
        out_specs=[col(0), col(0), col(0), gspec, gspec],
        out_shape=[act, act, act, gsh, gsh],
        scratch_shapes=[pltpu.VMEM((S, HEAD_DIM), BF16), pltpu.VMEM((S, HEAD_DIM), BF16),
                        pltpu.VMEM((S, HEAD_DIM), F32), pltpu.VMEM((S, HEAD_DIM), F32),
                        pltpu.VMEM((1, HEAD_DIM), F32)],
        sem=("parallel",))


def _place():
    x, y, c = lax.axis_index("x"), lax.axis_index("y"), lax.axis_index("c")
    chips = [(1 - x, y), (x, 1 - y), (1 - x, 1 - y)]
    return x, y, c, chips


def _dev_allgather(name, v):
    m_per, n = v.shape

    def body(x_ref, out_ref, send_sems, recv_sems, local_sem):
        x, y, c, chips = _place()
        me, sibling = (x, y, c), (x, y, 1 - c)

        def rows(px, py, pc):
            return out_ref.at[pl.ds((4 * px + 2 * py + pc) * m_per, m_per), :]

        def copy(k, block, to, src=None):
            return pltpu.make_async_remote_copy(
                src_ref=rows(*block) if src is None else src, dst_ref=rows(*block),
                send_sem=send_sems.at[k], recv_sem=recv_sems.at[k], device_id=to, device_id_type=MESH)

        mine = pltpu.make_async_copy(x_ref, rows(*me), local_sem)
        mine.start()
        first = [copy(0, me, sibling, src=x_ref)]
        first += [copy(1 + j, me, (*chip, c), src=x_ref) for j, chip in enumerate(chips)]
        for cp in first:
            cp.start()
        passed = [copy(4 + j, (*chip, c), sibling) for j, chip in enumerate(chips)]
        for j, chip in enumerate(chips):
            copy(1 + j, (*chip, c), me).wait_recv()
            passed[j].start()
        copy(0, sibling, me).wait_recv()
        for j, chip in enumerate(chips):
            copy(4 + j, (*chip, 1 - c), me).wait_recv()
        for cp in first + passed:
            cp.wait_send()
        mine.wait()

    return _pcall(
        body, name=name, out_shape=jax.ShapeDtypeStruct((N_DEV * m_per, n), v.dtype),
        in_specs=[pl.BlockSpec(memory_space=pltpu.VMEM)], out_specs=pl.BlockSpec(memory_space=pltpu.VMEM),
        scratch_shapes=[pltpu.SemaphoreType.DMA((7,)), pltpu.SemaphoreType.DMA((7,)), pltpu.SemaphoreType.DMA],
        compiler_params=pltpu.CompilerParams(vmem_limit_bytes=VMEM_LIMIT_V7X),
    )(v)


class _W:
    def __init__(self, name, kind, R, C):
        self.name, self.kind, self.R, self.C = name, kind, R, C

    @property
    def shard_shape(self):
        return (self.R, self.C // N_CHIPS) if self.kind == "col" else (self.R // N_CHIPS, self.C)

    @property
    def half_rows(self):
        return self.shard_shape[0] // 2

    def shard_half(self, ref, half):
        return ref.at[pl.ds(half * self.half_rows, self.half_rows), :]

    def region(self, full_ref, chip, half):
        hr = self.half_rows
        if self.kind == "col":
            cw = self.C // N_CHIPS
            return full_ref.at[pl.ds(half * hr, hr), pl.ds(chip * cw, cw)]
        return full_ref.at[pl.ds(chip * (2 * hr) + half * hr, hr), :]

    def region_both(self, full_ref, chip):
        hr = self.half_rows
        if self.kind == "col":
            cw = self.C // N_CHIPS
            return full_ref.at[:, pl.ds(chip * cw, cw)]
        return full_ref.at[pl.ds(chip * (2 * hr), 2 * hr), :]


def _ag_rider(ws, fulls, n_ch=4, chunks=None):
    n_w = len(ws)
    lo, hi = chunks or (0, n_ch)
    per = 6

    def parts(full, sems):
        send_sems, recv_sems = sems
        x, y, c, _ = _place()
        xn, yn, dg = (1 - x, y), (x, 1 - y), (1 - x, 1 - y)
        via = (x + (1 - c) * (1 - 2 * x), y + c * (1 - 2 * y))
        to = (x + c * (1 - 2 * x), y + (1 - c) * (1 - 2 * y))

        def reg(i, chip, half, t):
            nr = ws[i].half_rows // n_ch
            return ws[i].region(full[i], 2 * chip[0] + chip[1], half).at[pl.ds(t * nr, nr), :]

        def copy(r, i, t, k, dev):
            s = (i * (hi - lo) + t - lo) * per + k
            return pltpu.make_async_remote_copy(src_ref=r, dst_ref=r, send_sem=send_sems.at[s],
                                                recv_sem=recv_sems.at[s], device_id=dev, device_id_type=MESH)

        def direct(i, t, k):
            return copy(reg(i, (x, y), c, t), i, t, k, (*(via, to)[k], c))

        def direct_in(i, t, k):
            return copy(reg(i, (via, to)[k], c, t), i, t, k, (*(via, to)[k], c))

        def relay(i, t):
            return copy(reg(i, via, c, t), i, t, 2, (*to, c))

        def relay_in(i, t):
            return copy(reg(i, dg, c, t), i, t, 2, (*to, c))

        def hand(i, t, k, half):
            return copy(reg(i, (xn, yn, dg)[k], half, t), i, t, 3 + k, (x, y, 1 - c))

        return c, direct, direct_in, relay, relay_in, hand

    def start(_, full, sems):
        _, direct, _, _, _, _ = parts(full, sems)
        for t in range(lo, hi):
            for i in range(n_w):
                direct(i, t, 0).start()
                direct(i, t, 1).start()

    def arrived(t):
        def step(_, full, sems):
            c, _, direct_in, relay, relay_in, hand = parts(full, sems)
            for i in range(n_w):
                direct_in(i, t, 0).wait_recv()
                direct_in(i, t, 1).wait_recv()
                relay(i, t).start()
                hand(i, t, 0, c).start()
                hand(i, t, 1, c).start()
                if t > lo:
                    relay_in(i, t - 1).wait_recv()
                    hand(i, t - 1, 2, c).start()
        return step

    def finish(_, full, sems):
        c, direct, _, relay, relay_in, hand = parts(full, sems)
        for i in range(n_w):
            relay_in(i, hi - 1).wait_recv()
            hand(i, hi - 1, 2, c).start()
        for i in range(n_w):
            for t in range(lo, hi):
                for k in range(3):
                    hand(i, t, k, 1 - c).wait_recv()
        for i in range(n_w):
            for t in range(lo, hi):
                direct(i, t, 0).wait_send()
                direct(i, t, 1).wait_send()
                relay(i, t).wait_send()
                for k in range(3):
                    hand(i, t, k, c).wait_send()

    n_sem = per * (hi - lo) * n_w
    return _Rider(fulls, [jax.ShapeDtypeStruct((w.R, w.C), BF16) for w in ws],
                  [pltpu.SemaphoreType.DMA((n_sem,)), pltpu.SemaphoreType.DMA((n_sem,))], start, finish,
                  steps=[arrived(t) for t in range(lo, hi)], aliases={i: i for i in range(n_w)})


def _cast_into_full(w, a32, chip_arr):
    sr, sc = w.shard_shape
    tr, tc = _tile(sr, 512), _tile(sc, 2048)
    n_r, n_c = sr // tr, sc // tc
    if w.kind == "col":
        out_spec = pl.BlockSpec((tr, tc), lambda i, j, chip: (i, chip[0] * n_c + j))
    else:
        out_spec = pl.BlockSpec((tr, tc), lambda i, j, chip: (chip[0] * n_r + i, j))

    def body(chip_ref, a_ref, o_ref):
        o_ref[...] = a_ref[...].astype(BF16)

    return _pcall(
        body, name="cast_" + w.name, out_shape=jax.ShapeDtypeStruct((w.R, w.C), BF16),
        grid_spec=pltpu.PrefetchScalarGridSpec(
            num_scalar_prefetch=1, grid=(n_r, n_c),
            in_specs=[pl.BlockSpec((tr, tc), lambda i, j, chip: (i, j))], out_specs=out_spec),
        compiler_params=_params(("parallel", "parallel")),
    )(chip_arr, a32)


def _half_view(w, g):
    return g if w.kind == "col" else g.reshape(N_CHIPS, w.R // N_CHIPS, w.C)


def _px_rider(ws, grads):
    n_w = len(ws)

    def copies(g, got, sems):
        send_sems, recv_sems = sems
        x, y, c, _ = _place()

        def half_all(w, ref, half):
            hr = w.half_rows
            if w.kind == "col":
                return ref.at[pl.ds(half * hr, hr), :]
            return ref.at[:, pl.ds(half * hr, hr), :]

        return [pltpu.make_async_remote_copy(
            src_ref=half_all(w, g[i], 1 - c), dst_ref=got[i], send_sem=send_sems.at[i], recv_sem=recv_sems.at[i],
            device_id=(x, y, 1 - c), device_id_type=MESH) for i, w in enumerate(ws)]

    def start(g, got, sems):
        for cp in copies(g, got, sems):
            cp.start()

    def finish(g, got, sems):
        for cp in copies(g, got, sems):
            cp.wait_recv()
            cp.wait_send()

    def got_shape(w):
        hr = w.half_rows
        return (hr, w.C) if w.kind == "col" else (N_CHIPS, hr, w.C)

    return _Rider([_half_view(w, g) for w, g in zip(ws, grads)],
                  [jax.ShapeDtypeStruct(got_shape(w), BF16) for w in ws],
                  [pltpu.SemaphoreType.DMA((n_w,)), pltpu.SemaphoreType.DMA((n_w,))], start, finish)


def _pair_sum(w, g, got, c_arr):
    hr = w.half_rows
    if w.kind == "col":
        tr, tc = _tile(hr, 512), _tile(w.C, 2048)
        n_r = hr // tr
        grid = (n_r, w.C // tc)
        g_spec = pl.BlockSpec((tr, tc), lambda i, j, c: (c[0] * n_r + i, j))
        o_spec = pl.BlockSpec((tr, tc), lambda i, j, c: (i, j))
    else:
        tr = _tile(hr, 512)
        n_r = hr // tr
        grid = (N_CHIPS, n_r)
        g_spec = pl.BlockSpec((1, tr, w.C), lambda s, i, c: (s, c[0] * n_r + i, 0))
        o_spec = pl.BlockSpec((1, tr, w.C), lambda s, i, c: (s, i, 0))

    def body(c_ref, g_ref, got_ref, out_ref):
        out_ref[...] = (g_ref[...].astype(F32) + got_ref[...].astype(F32)).astype(BF16)

    return _pcall(
        body, name="grad_pair_sum_" + w.name, out_shape=jax.ShapeDtypeStruct(got.shape, BF16),
        grid_spec=pltpu.PrefetchScalarGridSpec(num_scalar_prefetch=1, grid=grid, in_specs=[g_spec, o_spec],
                                               out_specs=o_spec),
        compiler_params=_params(("parallel", "parallel")),
    )(c_arr, _half_view(w, g), got)


def _cx_rider(ws, sums, part=(0, 1), q_in=None):
    n_w = len(ws)

    def parts(p, q, sems):
        send_sems, recv_sems = sems
        x, y, c, chips = _place()
        my_chip = 2 * x + y

        def rows(w, ref):
            nr = w.half_rows // part[1]
            return ref.at[pl.ds(part[0] * nr, nr), :]

        def piece(w, ref, chip):
            if w.kind == "col":
                cw = w.C // N_CHIPS
                return rows(w, ref.at[:, pl.ds(chip * cw, cw)])
            return rows(w, ref.at[chip])

        def copy(i, k, recv=False):
            chip = chips[k]
            to_chip = 2 * chip[0] + chip[1]
            return pltpu.make_async_remote_copy(
                src_ref=piece(ws[i], p[i], to_chip), dst_ref=rows(ws[i], q[i].at[to_chip if recv else my_chip]),
                send_sem=send_sems.at[3 * i + k], recv_sem=recv_sems.at[3 * i + k],
                device_id=(*chip, c), device_id_type=MESH)

        return copy

    both = [(i, k) for i in range(n_w) for k in range(N_CHIPS - 1)]

    def start(p, q, sems):
        copy = parts(p, q, sems)
        for i, k in both:
            copy(i, k).start()

    def finish(p, q, sems):
        copy = parts(p, q, sems)
        for i, k in both:
            copy(i, k, recv=True).wait_recv()
        for i, k in both:
            copy(i, k).wait_send()

    return _Rider(list(sums) + list(q_in or []),
                  [jax.ShapeDtypeStruct((N_CHIPS, w.half_rows, w.shard_shape[1]), BF16) for w in ws],
                  [pltpu.SemaphoreType.DMA((3 * n_w,)), pltpu.SemaphoreType.DMA((3 * n_w,))], start, finish,
                  aliases={n_w + i: i for i in range(n_w)} if q_in else None)


def _chip_sum(w, p, q, cc_arr):
    hr, cols = w.half_rows, w.shard_shape[1]
    tr, tc = _tile(hr, 512), _tile(cols, 2048)
    n_r, n_c = hr // tr, cols // tc

    def body(cc_ref, own, q1, q2, q3, out_ref):
        own_v = own[...] if w.kind == "col" else own[0]
        out_ref[...] = ((own_v.astype(F32) + q1[0].astype(F32)) + q2[0].astype(F32)) + q3[0].astype(F32)

    if w.kind == "col":
        own_spec = pl.BlockSpec((tr, tc), lambda i, j, cc: (i, cc[1] * n_c + j))
    else:
        own_spec = pl.BlockSpec((1, tr, tc), lambda i, j, cc: (cc[1], i, j))
    q_specs = [pl.BlockSpec((1, tr, tc), lambda i, j, cc, s=s: ((cc[1] + s) % N_CHIPS, i, j)) for s in (1, 2, 3)]
    return _pcall(
        body, name="grad_chip_sum_" + w.name, out_shape=jax.ShapeDtypeStruct(w.shard_shape, F32),
        grid_spec=pltpu.PrefetchScalarGridSpec(
            num_scalar_prefetch=1, grid=(n_r, n_c), in_specs=[own_spec] + q_specs,
            out_specs=pl.BlockSpec((tr, tc), lambda i, j, cc: (cc[0] * n_r + i, j))),
        compiler_params=_params(("parallel", "parallel")),
    )(cc_arr, p, q, q, q)


def _sf_rider(ws, grads):
    n_w = len(ws)

    def copy(g, sems, i, half):
        send_sems, recv_sems = sems
        x, y, c, _ = _place()
        h = c if half == "mine" else 1 - c
        reg = ws[i].shard_half(g[i], h)
        return pltpu.make_async_remote_copy(src_ref=reg, dst_ref=reg, send_sem=send_sems.at[i], recv_sem=recv_sems.at[i],
                                            device_id=(x, y, 1 - c), device_id_type=MESH)

    def start(_, g, sems):
        for i in range(n_w):
            copy(g, sems, i, "mine").start()

    def finish(_, g, sems):
        for i in range(n_w):
            copy(g, sems, i, "other").wait_recv()
            copy(g, sems, i, "mine").wait_send()

    return _Rider(grads, [jax.ShapeDtypeStruct(w.shard_shape, F32) for w in ws],
                  [pltpu.SemaphoreType.DMA((n_w,)), pltpu.SemaphoreType.DMA((n_w,))], start, finish,
                  aliases={i: i for i in range(n_w)})


def _adamw_math(w, g, m, v):
    m = ADAM_B1 * m + (1.0 - ADAM_B1) * g
    v = ADAM_B2 * v + (1.0 - ADAM_B2) * (g * g)
    m_hat = m / (1.0 - ADAM_B1 ** ADAM_STEP)
    v_hat = v / (1.0 - ADAM_B2 ** ADAM_STEP)
    delta = -ADAM_LR * (m_hat / (jnp.sqrt(v_hat) + ADAM_EPS) + ADAM_WD * w)
    return delta, m, v


def _adamw(name, w, g, m, v):
    R, C = w.shape
    tr, tc = _tile(R, 256), _tile(C, 2048)

    def body(w_ref, g_ref, m_ref, v_ref, d_out, m_out, v_out):
        d_out[...], m_out[...], v_out[...] = _adamw_math(w_ref[...], g_ref[...], m_ref[...], v_ref[...])

    spec = pl.BlockSpec((tr, tc), lambda i, j: (i, j))
    sh = jax.ShapeDtypeStruct((R, C), F32)
    return _pcall(body, name=name, grid=(R // tr, C // tc), in_specs=[spec] * 4, out_specs=[spec] * 3,
                  out_shape=[sh, sh, sh], compiler_params=_params(("parallel", "parallel")))(w, g, m, v)


def _ada_update(sct, dmod_sh, w, m, v, riders=()):
    R, C = w.shape
    tr, tc = _tile(R, 256), _tile(C, 1024)

    def body(s_ref, d_ref, w_ref, m_ref, v_ref, g_out, d_out, m_out, v_out):
        s, d = s_ref[...], d_ref[...]
        g = s[:, 0:1] * d[0:1, :]
        for b in range(1, N_DEV):
            g += s[:, b:b + 1] * d[b:b + 1, :]
        g_out[...] = g
        d_out[...], m_out[...], v_out[...] = _adamw_math(w_ref[...], g, m_ref[...], v_ref[...])

    spec = pl.BlockSpec((tr, tc), lambda i, j: (i, j))
    sh = jax.ShapeDtypeStruct((R, C), F32)
    return _ride(
        "ada_update", body, riders, [sct, dmod_sh, w, m, v], grid=(R // tr, C // tc),
        in_specs=[pl.BlockSpec((tr, N_DEV), lambda i, j: (i, 0)), pl.BlockSpec((N_DEV, tc), lambda i, j: (0, j)),
                  spec, spec, spec],
        out_specs=[spec] * 4, out_shape=[sh] * 4, scratch_shapes=[], sem=("parallel", "parallel"))


def _silu_rows(c_row):
    D = c_row.shape[1]

    def body(c_ref, o_ref):
        cv = c_ref[...]
        o_ref[...] = cv * jax.nn.sigmoid(cv)

    return _pcall(body, name="silu_c", out_shape=jax.ShapeDtypeStruct((1, D), F32))(c_row)


def _pack_partials(parts, widths, total):
    n = len(widths)

    def body(*refs):
        loss_p, out_ref, loss_ref = refs[n], refs[n + 1], refs[n + 2]
        off = 0
        for ref, wd in zip(refs[:n], widths):
            out_ref[:, off:off + wd] = jnp.sum(ref[...], axis=0)
            off += wd
        if off < total:
            out_ref[:, off:total] = jnp.zeros((1, total - off), F32)
        loss_ref[...] = jnp.sum(jnp.sum(loss_p[...], axis=0), axis=1, keepdims=True)

    return _pcall(body, name="pack_partials",
                  out_shape=[jax.ShapeDtypeStruct((1, total), F32), jax.ShapeDtypeStruct((1, 1), F32)])(*parts)


def _small_update(gathered, offsets, params):
    n_p = len(params)

    def body(*refs):
        g_ref = refs[0]
        prm = refs[1:1 + 3 * n_p]
        outs = refs[1 + 3 * n_p:]
        for i, (off, wd) in enumerate(offsets):
            blk = g_ref[:, off:off + wd]
            g = blk[0:1, :]
            for b in range(1, N_DEV):
                g = g + blk[b:b + 1, :]
            w, m, v = prm[3 * i][...], prm[3 * i + 1][...], prm[3 * i + 2][...]
            outs[4 * i][...] = g
            outs[4 * i + 1][...], outs[4 * i + 2][...], outs[4 * i + 3][...] = _adamw_math(w, g, m, v)

    flat = [a for t in params for a in t]
    out_shape = [jax.ShapeDtypeStruct(t[0].shape, F32) for t in params for _ in range(4)]
    return _pcall(body, name="small_update", out_shape=out_shape)(gathered, *flat)


def kernel(x, c, w_ada, b_ada, norm1_w, w_in, q_norm_w, k_norm_w, w_pool, pool_scale, w_a_up, w_b_up, w_o, norm2_w, w_ff1, w_ff2, loss_target, m_w_ada, m_b_ada, m_norm1_w, m_w_in, m_q_norm_w, m_k_norm_w, m_w_pool, m_pool_scale, m_w_a_up, m_w_b_up, m_w_o, m_norm2_w, m_w_ff1, m_w_ff2, v_w_ada, v_b_ada, v_norm1_w, v_w_in, v_q_norm_w, v_k_norm_w, v_w_pool, v_pool_scale, v_w_a_up, v_w_b_up, v_w_o, v_norm2_w, v_w_ff1, v_w_ff2):
    _, S, D = x.shape
    PW = D // 2
    H = PW // HEAD_DIM
    cg = PW // N_GROUPS
    IN = w_in.shape[2] * N_CHIPS
    FF = w_ff1.shape[2] * N_CHIPS
    A_COLS = w_ada.shape[2]
    xi, yi, ci = lax.axis_index("x"), lax.axis_index("y"), lax.axis_index("c")
    chip = 2 * xi + yi
    dev = 2 * chip + ci
    c_arr = jnp.reshape(ci, (1,)).astype(jnp.int32)
    x2, tgt = x[0], loss_target[0]

    ws = [_W("w_in", "col", D, IN), _W("w_pool", "row", PW, cg), _W("w_a_up", "col", PW, D),
          _W("w_b_up", "col", PW, D), _W("w_o", "row", D, D), _W("w_ff1", "col", D, FF), _W("w_ff2", "row", FF, D)]
    w32 = [w_in[0], w_pool[0].reshape(cg, cg), w_a_up[0], w_b_up[0], w_o[0], w_ff1[0], w_ff2[0]]
    m32 = [m_w_in[0], m_w_pool[0].reshape(cg, cg), m_w_a_up[0], m_w_b_up[0], m_w_o[0], m_w_ff1[0], m_w_ff2[0]]
    v32 = [v_w_in[0], v_w_pool[0].reshape(cg, cg), v_w_a_up[0], v_w_b_up[0], v_w_o[0], v_w_ff1[0], v_w_ff2[0]]

    W_IN, W_POOL, W_A, W_B, W_O, W_FF1, W_FF2 = ws
    chip_arr = jnp.reshape(chip, (1,)).astype(jnp.int32)
    cc_arr = jnp.stack([ci, chip]).astype(jnp.int32)
    s_in, s_pool, s_a, s_b, s_o, s_ff1, s_ff2 = [_cast_into_full(w, a, chip_arr) for w, a in zip(ws, w32)]
    (win_f,) = _run_rider("gather_w_in", _ag_rider([W_IN], [s_in]))

    sc_row = _silu_rows(c)
    sc_all = _dev_allgather("gather_silu_c", sc_row.reshape(8, D // 8)).reshape(N_DEV, D)
    sc16 = jnp.concatenate([sc_all, jnp.zeros_like(sc_all)], axis=0)
    b_cols = lax.dynamic_slice(b_ada, (0, chip * A_COLS), (1, A_COLS))
    (mod_cols,) = _mm("mod_cols", [(sc16, w_ada[0])], M=2 * N_DEV, N=A_COLS, K=D, tm=16, tn=1024, tk=1024,
                      a_pro=lambda a: a.astype(BF16), b_pro=lambda b: b.astype(BF16),
                      extras=[(b_cols, "row", 0)], outs=[_tile_out(F32)], epi=lambda accs, ex: [accs[0] + ex[0]])
    mod_all = _dev_allgather("gather_mod", mod_cols[:N_DEV]).reshape(N_CHIPS, 2, N_DEV, A_COLS)
    mod_row = lax.dynamic_index_in_dim(mod_all[:, 0], dev, axis=1, keepdims=False).reshape(1, N_CHIPS * A_COLS)
    shift1, scale1, gate1, shift2, scale2, gate2 = [mod_row[:, i * D:(i + 1) * D] for i in range(6)]

    WIDE = dict(tm=2048, tn=512, tk=2048)
    DEEP = dict(tm=1024, tn=1024, tk=1024)
    h = _norm_mod("norm1_mod", x2, norm1_w, scale1, shift1)
    (proj,), ((wpool_f, wa_f, wb_f, wo_f),) = _mm(
        "in_proj", [(h, win_f)], M=S, N=IN, K=D, outs=[_tile_out(BF16)], epi=lambda accs, ex: [accs[0]], **WIDE,
        riders=[_ag_rider([W_POOL, W_A, W_B, W_O], [s_pool, s_a, s_b, s_o], n_ch=2)])
    pooled, pa = _pool_fwd(proj, wpool_f, pool_scale, S, PW)
    (att, attf), ((wff1_f,),) = _attn_fwd(proj, q_norm_w, k_norm_w, S, H, PW // HEAD_DIM,
                                          riders=[_ag_rider([W_FF1], [s_ff1])])

    def merge_epi(accs, ex):
        sa, sb = jax.nn.sigmoid(ex[0].astype(F32)), jax.nn.sigmoid(ex[1].astype(F32))
        return [sa * accs[0] + sb * accs[1], accs[0], accs[1]]

    (merged, ya, yb), (ff2_a,) = _mm("branch_up_merge", [(pa, wa_f), (att, wb_f)], M=S, N=D, K=PW,
                                     extras=[(proj, "tile", 4 * PW), (proj, "tile", 4 * PW + D)],
                                     outs=[_tile_out(BF16)] * 3, epi=merge_epi,
                                     riders=[_ag_rider([W_FF2], [s_ff2], chunks=(0, 1))])
    (x1, o), (ff2_b,) = _mm("out_proj", [(merged, wo_f)], M=S, N=D, K=D, extras=[(x2, "tile", 0), (gate1, "row", 0)],
                            outs=[_tile_out(F32), _tile_out(BF16)], epi=lambda accs, ex: [ex[0] + ex[1] * accs[0], accs[0]],
                            riders=[_ag_rider([W_FF2], ff2_a, chunks=(1, 2))], **WIDE)
    h2 = _norm_mod("norm2_mod", x1, norm2_w, scale2, shift2)
    (rl,), ((wff2_f,),) = _mm("ff1", [(h2, wff1_f)], M=S, N=FF, K=D, outs=[_tile_out(BF16)], **WIDE,
                              epi=lambda accs, ex: [jnp.maximum(accs[0], 0.0)],
                              riders=[_ag_rider([W_FF2], ff2_b, chunks=(2, 4))])

    def square(a):
        af = a.astype(F32)
        return (af * af).astype(BF16)

    def loss_epi(accs, ex):
        x1_t, tgt_t, g2 = ex
        f = accs[0]
        diff = (x1_t + g2 * f) - tgt_t
        dy = diff * (1.0 / D)
        return [dy, dy * g2, _colsum(dy * f), _colsum(diff * diff)]

    dy, df, dgate2_p, loss_p = _mm("ff2_loss", [(rl, wff2_f)], M=S, N=D, K=FF, a_pro=square, tm=1024, tn=1024, tk=512,
                                   extras=[(x1, "tile", 0), (tgt, "tile", 0), (gate2, "row", 0)],
                                   outs=[_tile_out(F32), _tile_out(BF16), _COLSUM, _COLSUM], epi=loss_epi)

    def pair_sums(group, partials, got):
        return [_pair_sum(w, g, r, c_arr) for w, g, r in zip(group, partials, got)]

    def chip_sums(group, sums, from_chips):
        return [_chip_sum(w, p, q, cc_arr) for w, p, q in zip(group, sums, from_chips)]

    first = lambda accs, ex: [accs[0]]
    gmm = dict(ta=True, outs=[_tile_out(BF16)], epi=first, **WIDE)
    (g_ff2,) = _mm("grad_w_ff2", [(rl, df)], M=FF, N=D, K=S, a_pro=square, ta=True, tm=512, tn=2048, tk=2048,
                   outs=[_tile_out(BF16)], epi=first)
    (dz1,), (got_ff2,) = _mm("d_ff_hidden", [(df, wff2_f)], M=S, N=FF, K=D, tb=True, extras=[(rl, "tile", 0)], **WIDE,
                             outs=[_tile_out(BF16)], epi=lambda accs, ex: [accs[0] * (2.0 * ex[0].astype(F32))],
                             riders=[_px_rider([W_FF2], [g_ff2])])
    sum_ff2 = pair_sums([W_FF2], [g_ff2], got_ff2)
    (g_ff1,), (q_ff2,) = _mm("grad_w_ff1", [(h2, dz1)], M=D, N=FF, K=S,
                             riders=[_cx_rider([W_FF2], sum_ff2, part=(0, 2))], **gmm)
    (dh2,), (got_ff1, q_ff2) = _mm("d_h2", [(dz1, wff1_f)], M=S, N=D, K=FF, tb=True, outs=[_tile_out(F32)], epi=first,
                                   riders=[_px_rider([W_FF1], [g_ff1]),
                                           _cx_rider([W_FF2], sum_ff2, part=(1, 2), q_in=q_ff2)], **DEEP)
    sum_ff1 = pair_sums([W_FF1], [g_ff1], got_ff1)
    dx1, dshift2_p, dscale2_p, gn2_p, do, dgate1_p = _norm_mod_bwd("norm2_bwd", dh2, x1, dy, norm2_w, scale2,
                                                                   gate_o=(o, gate1))
    (g_wo,) = _mm("grad_w_o", [(merged, do)], M=D, N=D, K=S, **gmm)

    def gate_epi(accs, ex):
        dm = accs[0]
        sa, sb = jax.nn.sigmoid(ex[0].astype(F32)), jax.nn.sigmoid(ex[1].astype(F32))
        ya_t, yb_t = ex[2].astype(F32), ex[3].astype(F32)
        return [dm * sa, dm * sb, dm * ya_t * (sa * (1.0 - sa)), dm * yb_t * (sb * (1.0 - sb))]

    dya, dyb, dga, dgb = _mm("d_merged", [(do, wo_f)], M=S, N=D, K=D, tb=True, tm=1024, tn=512, tk=2048,
                             extras=[(proj, "tile", 4 * PW), (proj, "tile", 4 * PW + D), (ya, "tile", 0), (yb, "tile", 0)],
                             outs=[_tile_out(BF16)] * 4, epi=gate_epi)
    (g_wa,) = _mm("grad_w_a_up", [(pa, dya)], M=PW, N=D, K=S, **gmm)
    (g_wb,) = _mm("grad_w_b_up", [(att, dyb)], M=PW, N=D, K=S, **gmm)
    (dpa,) = _mm("d_pool_out", [(dya, wa_f)], M=S, N=PW, K=D, tb=True, outs=[_tile_out(F32)], epi=first, **WIDE)
    mid = [W_A, W_B, W_O]
    (datt,), (got_mid,) = _mm("d_att", [(dyb, wb_f)], M=S, N=PW, K=D, tb=True, outs=[_tile_out(BF16)], epi=first, **WIDE,
                              riders=[_px_rider(mid, [g_wa, g_wb, g_wo])])
    sum_mid = pair_sums(mid, [g_wa, g_wb, g_wo], got_mid)
    du, g_wpool4, gscale_p = _pool_bwd(dpa, pooled, wpool_f, pool_scale, S, PW)
    (dq, dk, dv, gq_p, gk_p), ((q_ff1, q_wa, q_wb, q_wo),) = _attn_bwd(
        proj, datt, attf, q_norm_w, k_norm_w, S, H, PW // HEAD_DIM, riders=[_cx_rider([W_FF1] + mid, sum_ff1 + sum_mid)])
    dproj = jnp.concatenate([du, dq, dk, dv, dga, dgb], axis=1)
    early = mid + [W_FF1, W_FF2]
    halves_early = chip_sums(early, sum_mid + sum_ff1 + sum_ff2, [q_wa, q_wb, q_wo, q_ff1, q_ff2[0]])
    (g_win,), (grads_early,) = _mm("grad_w_in", [(h, dproj)], M=D, N=IN, K=S, riders=[_sf_rider(early, halves_early)],
                                   **gmm)
    last = [W_IN, W_POOL]
    g_last = [g_win, g_wpool4.reshape(PW, cg)]
    dh_kw = dict(M=S, N=D // 2, K=IN, tb=True, outs=[_tile_out(F32)], epi=first, **DEEP)
    (dh_left,), (got_last,) = _mm("d_h_left", [(dproj, win_f)], riders=[_px_rider(last, g_last)], **dh_kw)
    sum_last = pair_sums(last, g_last, got_last)
    (dh_right,), ((q_win, q_wpool),) = _mm("d_h_right", [(dproj, win_f)], b_noff=D // 2,
                                           riders=[_cx_rider(last, sum_last)], **dh_kw)
    grad_x, dshift1_p, dscale1_p, gn1_p = _norm_mod_bwd("norm1_bwd", (dh_left, dh_right), x2, dx1, norm1_w, scale1)

    parts = [dshift1_p, dscale1_p, dgate1_p, dshift2_p, dscale2_p, dgate2_p, gn1_p, gn2_p,
             gscale_p.reshape(1, 1, PW), gq_p, gk_p]
    widths = [D] * 8 + [PW, HEAD_DIM, HEAD_DIM]
    used = sum(widths)
    P = -(-used // 1024) * 1024
    packed, loss_part = _pack_partials(parts + [loss_p], widths, P)
    gathered = _dev_allgather("gather_vector_grads", packed.reshape(8, P // 8)).reshape(N_DEV, P)
    small = [(b_ada, m_b_ada, v_b_ada), (norm1_w, m_norm1_w, v_norm1_w), (norm2_w, m_norm2_w, v_norm2_w),
             (pool_scale, m_pool_scale, v_pool_scale), (q_norm_w, m_q_norm_w, v_q_norm_w),
             (k_norm_w, m_k_norm_w, v_k_norm_w)]
    offsets = [(0, 6 * D), (6 * D, D), (7 * D, D), (8 * D, PW), (8 * D + PW, HEAD_DIM), (8 * D + PW + HEAD_DIM, HEAD_DIM)]
    su = _small_update(gathered, offsets, small)
    (g_b, d_b, nm_b, nv_b, g_n1, d_n1, nm_n1, nv_n1, g_n2, d_n2, nm_n2, nv_n2, g_ps, d_ps, nm_ps, nv_ps,
     g_qn, d_qn, nm_qn, nv_qn, g_kn, d_kn, nm_kn, nv_kn) = su
    dmod_sh = lax.dynamic_slice(gathered, (0, chip * A_COLS), (N_DEV, A_COLS))
    g_ada, d_ada, nm_ada, nv_ada = _ada_update(sc_all.T, dmod_sh, w_ada[0], m_w_ada[0], v_w_ada[0])

    halves_last = chip_sums(last, sum_last, [q_win, q_wpool])
    grads = list(_run_rider("grad_sibling_fill", _sf_rider(last, halves_last))) + list(grads_early)
    upd = [_adamw("adamw_" + w.name, a, g, m, v) for w, a, g, m, v in zip(ws, w32, grads, m32, v32)]

    loss = 0.5 / D * lax.psum(loss_part[0, 0], ("x", "y", "c"))

    def up(a):
        return a[None]

    def pool4(a):
        return a.reshape(1, N_GROUPS, cg // N_CHIPS, cg)

    (d_win, nm_win, nv_win), (d_wp, nm_wp, nv_wp), (d_wa, nm_wa, nv_wa), (d_wb, nm_wb, nv_wb), \
        (d_wo, nm_wo, nv_wo), (d_f1, nm_f1, nv_f1), (d_f2, nm_f2, nv_f2) = upd
    gr_win, gr_wp, gr_wa, gr_wb, gr_wo, gr_f1, gr_f2 = grads
    return (
        loss, grad_x[None],
        up(g_ada), g_b, g_n1, up(gr_win), g_qn, g_kn, pool4(gr_wp), g_ps, up(gr_wa), up(gr_wb), up(gr_wo), g_n2,
        up(gr_f1), up(gr_f2),
        up(d_ada), d_b, d_n1, up(d_win), d_qn, d_kn, pool4(d_wp), d_ps, up(d_wa), up(d_wb), up(d_wo), d_n2,
        up(d_f1), up(d_f2),
        up(nm_ada), nm_b, nm_n1, up(nm_win), nm_qn, nm_kn, pool4(nm_wp), nm_ps, up(nm_wa), up(nm_wb), up(nm_wo), nm_n2,
        up(nm_f1), up(nm_f2),
        up(nv_ada), nv_b, nv_n1, up(nv_win), nv_qn, nv_kn, pool4(nv_wp), nv_ps, up(nv_wa), up(nv_wb), up(nv_wo), nv_n2,
        up(nv_f1), up(nv_f2),
    )
```

```python
import functools
import math

import jax
import jax.numpy as jnp
from jax import lax
from jax.experimental import pallas as pl
from jax.experimental.pallas import tpu as pltpu

F32 = jnp.float32
BF16 = jnp.bfloat16
MESH = pl.DeviceIdType.MESH
ANY = pl.BlockSpec(memory_space=pl.ANY)

EPS = 1e-6
HEAD_DIM = 128
POOL_WINDOWS = (2, 4, 8, 16)
N_GROUPS = len(POOL_WINDOWS)
N_CHIPS = 4
N_DEV = 8
ADAM_LR, ADAM_B1, ADAM_B2, ADAM_EPS, ADAM_WD, ADAM_STEP = 0.001, 0.9, 0.999, 1e-08, 0.01, 10
VMEM_LIMIT_V7X = 56 * 1024 * 1024
ATT_T = 256
POOL_T = 256


def _pcall(body, **kw):
    return pl.pallas_call(body, **kw)


def _params(sem=None):
    return pltpu.CompilerParams(dimension_semantics=sem, vmem_limit_bytes=VMEM_LIMIT_V7X)


def _tile(n, pref):
    if n <= pref:
        return n
    t = pref
    while n % t:
        t //= 2
    return t


class _Rider:
    def __init__(self, arrays, out_shape, sems, start, finish, aliases=None, steps=()):
        self.arrays, self.out_shape, self.sems = list(arrays), list(out_shape), list(sems)
        self.start, self.finish, self.aliases, self.steps = start, finish, aliases or {}, list(steps)


def _ride(name, body, riders, arrays, *, grid, in_specs, out_specs, out_shape, scratch_shapes, sem):
    n_in, n_out, n_scr = len(arrays), len(out_shape), len(scratch_shapes)
    r_arrays = [a for r in riders for a in r.arrays]
    r_outs = [o for r in riders for o in r.out_shape]
    r_sems = [s for r in riders for s in r.sems]
    n_hooks = max([len(r.steps) for r in riders], default=0)
    total = math.prod(grid)
    aliases, off_i, off_o = {}, n_in, n_out
    for r in riders:
        for a, o in r.aliases.items():
            aliases[off_i + a] = off_o + o
        off_i += len(r.arrays)
        off_o += len(r.out_shape)

    def full(*refs):
        p = 0
        groups = []
        for n in (n_in, len(r_arrays), n_out, len(r_outs), n_scr, len(r_sems)):
            groups.append(refs[p:p + n])
            p += n
        ins, rin, outs, rout, scr, rsem = groups

        def each(what):
            a = o = s = 0
            for r in riders:
                fn = what(r)
                if fn is not None:
                    fn(rin[a:a + len(r.arrays)], rout[o:o + len(r.out_shape)], rsem[s:s + len(r.sems)])
                a, o, s = a + len(r.arrays), o + len(r.out_shape), s + len(r.sems)

        if riders:
            lin = 0
            for d, g in enumerate(grid):
                lin = lin * g + pl.program_id(d)
            pl.when(lin == 0)(lambda: each(lambda r: r.start))
            for t in range(n_hooks):
                pl.when(lin == min(total - 1, ((t + 1) * total) // n_hooks))(
                    lambda t=t: each(lambda r: r.steps[t] if t < len(r.steps) else None))
        body(*ins, *outs, *scr)
        if riders:
            pl.when(lin == total - 1)(lambda: each(lambda r: r.finish))

    res = _pcall(
        full, name=name, grid=grid, in_specs=list(in_specs) + [ANY] * len(r_arrays),
        out_specs=list(out_specs) + [ANY] * len(r_outs), out_shape=list(out_shape) + r_outs,
        scratch_shapes=list(scratch_shapes) + r_sems, input_output_aliases=aliases,
        compiler_params=_params(("arbitrary",) * len(grid) if riders else sem),
    )(*arrays, *r_arrays)
    if not riders:
        return res
    main, rest, per = res[:n_out], res[n_out:], []
    for r in riders:
        per.append(rest[:len(r.out_shape)])
        rest = rest[len(r.out_shape):]
    return main, per


def _run_rider(name, rider):
    def body(*refs):
        n_a, n_o = len(rider.arrays), len(rider.out_shape)
        ins, outs, sems = refs[:n_a], refs[n_a:n_a + n_o], refs[n_a + n_o:]
        for fn in [rider.start] + rider.steps + [rider.finish]:
            fn(ins, outs, sems)

    return _pcall(body, name=name, out_shape=rider.out_shape, in_specs=[ANY] * len(rider.arrays),
                  out_specs=[ANY] * len(rider.out_shape), scratch_shapes=rider.sems,
                  input_output_aliases=rider.aliases)(*rider.arrays)


def _mm(name, pairs, *, M, N, K, ta=False, tb=False, tm=512, tn=1024, tk=1024,
        a_pro=None, b_pro=None, extras=(), outs, epi, riders=(), b_noff=0):
    tm, tn, tk = _tile(M, tm), _tile(N, tn), _tile(K, tk)
    n_i, n_j, n_k = M // tm, N // tn, K // tk
    n_p, n_e = len(pairs), len(extras)
    arrays, in_specs = [], []
    for a, _ in pairs:
        arrays.append(a)
        in_specs.append(pl.BlockSpec((tk, tm), lambda i, j, k: (k, i)) if ta
                        else pl.BlockSpec((tm, tk), lambda i, j, k: (i, k)))
    for _, b in pairs:
        arrays.append(b)
        in_specs.append(pl.BlockSpec((tn, tk), lambda i, j, k: (j + b_noff // tn, k)) if tb
                        else pl.BlockSpec((tk, tn), lambda i, j, k: (k, j + b_noff // tn)))
    for arr, kind, off in extras:
        ob = off // tn
        assert off % tn == 0
        arrays.append(arr)
        if kind == "tile":
            in_specs.append(pl.BlockSpec((tm, tn), lambda i, j, k, ob=ob: (i, j + ob)))
        else:
            in_specs.append(pl.BlockSpec((1, tn), lambda i, j, k, ob=ob: (0, j + ob)))
    out_shape, out_specs = [], []
    for o in outs:
        if o["kind"] == "tile":
            out_shape.append(jax.ShapeDtypeStruct((M, N), o["dtype"]))
            out_specs.append(pl.BlockSpec((tm, tn), lambda i, j, k: (i, j)))
        else:
            out_shape.append(jax.ShapeDtypeStruct((n_i, 1, N), F32))
            out_specs.append(pl.BlockSpec((1, 1, tn), lambda i, j, k: (i, 0, j)))
    dims = (((0 if ta else 1,), (1 if tb else 0,)), ((), ()))

    def body(*refs):
        a_refs, b_refs = refs[:n_p], refs[n_p:2 * n_p]
        e_refs = refs[2 * n_p:2 * n_p + n_e]
        o_refs = refs[2 * n_p + n_e:2 * n_p + n_e + len(outs)]
        acc_refs = refs[2 * n_p + n_e + len(outs):]

        def product(p):
            a, b = a_refs[p][...], b_refs[p][...]
            if a_pro is not None:
                a = a_pro(a)
            if b_pro is not None:
                b = b_pro(b)
            return lax.dot_general(a, b, dims, preferred_element_type=F32)

        def write(accs):
            vals = epi(accs, [e[...] for e in e_refs])
            for o, o_ref, val in zip(outs, o_refs, vals):
                if o["kind"] == "tile":
                    o_ref[...] = val.astype(o_ref.dtype)
                else:
                    o_ref[0] = val

        if n_k == 1:
            write([product(p) for p in range(n_p)])
            return
        k = pl.program_id(2)

        @pl.when(k == 0)
        def _():
            for acc in acc_refs:
                acc[...] = jnp.zeros_like(acc)

        for p in range(n_p):
            acc_refs[p][...] += product(p)

        pl.when(k == n_k - 1)(lambda: write([acc[...] for acc in acc_refs]))

    return _ride(name, body, riders, arrays, grid=(n_i, n_j, n_k), in_specs=in_specs, out_specs=out_specs,
                 out_shape=out_shape, scratch_shapes=[pltpu.VMEM((tm, tn), F32) for _ in pairs] if n_k > 1 else [],
                 sem=("parallel", "parallel", "arbitrary"))


def _tile_out(dtype):
    return {"kind": "tile", "dtype": dtype}


_COLSUM = {"kind": "colsum"}


def _colsum(v):
    return jnp.sum(v, axis=0, keepdims=True)


def _norm_mod(name, x, norm_w, scale, shift):
    S, D = x.shape
    tr = _tile(S, 256)

    def body(x_ref, nw_ref, sc_ref, sh_ref, h_ref):
        xv = x_ref[...]
        r = lax.rsqrt(jnp.mean(xv * xv, axis=-1, keepdims=True) + EPS)
        h_ref[...] = ((xv * r * nw_ref[...]) * (1.0 + sc_ref[...]) + sh_ref[...]).astype(BF16)

    row = pl.BlockSpec((1, D), lambda i: (0, 0))
    til = pl.BlockSpec((tr, D), lambda i: (i, 0))
    return _pcall(body, name=name, grid=(S // tr,), in_specs=[til, row, row, row], out_specs=til,
                  out_shape=jax.ShapeDtypeStruct((S, D), BF16), compiler_params=_params(("parallel",)))(
                      x, norm_w, scale, shift)


def _norm_mod_bwd(name, dh, x, dres, norm_w, scale, gate_o=None):
    S, D = x.shape
    tr = _tile(S, 256)
    n_r = S // tr
    with_gate = gate_o is not None
    dh = list(dh) if isinstance(dh, (list, tuple)) else [dh]
    n_dh = len(dh)

    def body(*refs):
        dh_refs, refs = refs[:n_dh], refs[n_dh:]
        if with_gate:
            x_ref, dres_ref, nw_ref, sc_ref, o_ref, g_ref, dx_ref, p1, p2, p3, do_ref, p4 = refs
        else:
            x_ref, dres_ref, nw_ref, sc_ref, dx_ref, p1, p2, p3 = refs
        dhv = dh_refs[0][...] if n_dh == 1 else jnp.concatenate([r[...] for r in dh_refs], axis=1)
        xv, nw = x_ref[...], nw_ref[...]
        r = lax.rsqrt(jnp.mean(xv * xv, axis=-1, keepdims=True) + EPS)
        xh = xv * r
        p1[0] = _colsum(dhv)
        p2[0] = _colsum(dhv * (xh * nw))
        dn = dhv * (1.0 + sc_ref[...])
        p3[0] = _colsum(dn * xh)
        dxh = dn * nw
        dx = dres_ref[...] + r * (dxh - xh * jnp.mean(dxh * xh, axis=-1, keepdims=True))
        dx_ref[...] = dx
        if with_gate:
            do_ref[...] = (dx * g_ref[...]).astype(BF16)
            p4[0] = _colsum(dx * o_ref[...].astype(F32))

    row = pl.BlockSpec((1, D), lambda i: (0, 0))
    til = pl.BlockSpec((tr, D), lambda i: (i, 0))
    part = pl.BlockSpec((1, 1, D), lambda i: (i, 0, 0))
    part_shape = jax.ShapeDtypeStruct((n_r, 1, D), F32)
    in_specs = [pl.BlockSpec((tr, D // n_dh), lambda i: (i, 0))] * n_dh + [til, til, row, row]
    arrays = dh + [x, dres, norm_w, scale]
    out_specs = [til, part, part, part]
    out_shape = [jax.ShapeDtypeStruct((S, D), F32), part_shape, part_shape, part_shape]
    if with_gate:
        in_specs += [til, row]
        arrays += list(gate_o)
        out_specs += [til, part]
        out_shape += [jax.ShapeDtypeStruct((S, D), BF16), part_shape]
    return _pcall(body, name=name, grid=(n_r,), in_specs=in_specs, out_specs=out_specs, out_shape=out_shape,
                  compiler_params=_params(("parallel",)))(*arrays)


def _pool_w_specs(rows, cg):
    return [pl.BlockSpec((rows, cg), lambda g, j=j: (N_GROUPS * j + g, 0)) for j in range(N_CHIPS)]


def _pool_fwd(proj, wp_full, pool_scale, S, PW):
    cg = PW // N_GROUPS
    rows = cg // N_CHIPS
    T = _tile(S, POOL_T)
    n_t = S // T

    def body(u_ref, w0, w1, w2, w3, ps_ref, pooled_ref, pa_ref):
        g = pl.program_id(0)
        win = jnp.left_shift(2, g)
        w = jnp.concatenate([w0[...], w1[...], w2[...], w3[...]], axis=0)
        t_i = lax.broadcasted_iota(jnp.int32, (T, T), 0)
        j_i = lax.broadcasted_iota(jnp.int32, (T, T), 1)
        b_cur = ((j_i <= t_i) & (j_i > t_i - win)).astype(BF16)
        b_prev = (j_i - T > t_i - win).astype(BF16)
        row = lax.broadcasted_iota(jnp.int32, (T, 1), 0)
        for r in range(n_t):
            cur = u_ref[r * T:(r + 1) * T, :]
            ws = jnp.dot(b_cur, cur, preferred_element_type=F32)
            if r > 0:
                ws += jnp.dot(b_prev, u_ref[(r - 1) * T:r * T, :], preferred_element_type=F32)
            count = jnp.minimum(row + (r * T + 1), win).astype(F32)
            pooled = (ws / count - cur.astype(F32)).astype(BF16)
            pooled_ref[r * T:(r + 1) * T, :] = pooled
            mixed = jnp.dot(pooled, w, preferred_element_type=F32)
            pa_ref[r * T:(r + 1) * T, :] = (mixed * ps_ref[...]).astype(BF16)

    col = pl.BlockSpec((S, cg), lambda g: (0, g))
    return _pcall(
        body, name="pool_fwd", grid=(N_GROUPS,),
        in_specs=[col] + _pool_w_specs(rows, cg) + [pl.BlockSpec((1, cg), lambda g: (0, g))],
        out_specs=[col, col],
        out_shape=[jax.ShapeDtypeStruct((S, PW), BF16), jax.ShapeDtypeStruct((S, PW), BF16)],
        compiler_params=_params(("parallel",)),
    )(proj, wp_full, wp_full, wp_full, wp_full, pool_scale)


def _pool_bwd(dpa, pooled, wp_full, pool_scale, S, PW):
    cg = PW // N_GROUPS
    rows = cg // N_CHIPS
    T = _tile(S, POOL_T)
    n_t = S // T

    def body(dpa_ref, pooled_ref, w0, w1, w2, w3, ps_ref, du_ref, gw_ref, gs_ref, dp_s, dpc_s, dmx_s):
        g = pl.program_id(0)
        win = jnp.left_shift(2, g)
        w = jnp.concatenate([w0[...], w1[...], w2[...], w3[...]], axis=0)
        row = lax.broadcasted_iota(jnp.int32, (T, 1), 0)
        gs = jnp.zeros((1, cg), F32)
        for r in range(n_t):
            sl = slice(r * T, (r + 1) * T)
            mixed = jnp.dot(pooled_ref[sl, :], w, preferred_element_type=F32)
            dpa_t = dpa_ref[sl, :]
            gs += _colsum(dpa_t * mixed)
            dmx = (dpa_t * ps_ref[...]).astype(BF16)
            dmx_s[sl, :] = dmx
            dpo = lax.dot_general(dmx, w, (((1,), (1,)), ((), ())), preferred_element_type=F32)
            dp_s[sl, :] = dpo
            count = jnp.minimum(row + (r * T + 1), win).astype(F32)
            dpc_s[sl, :] = (dpo / count).astype(BF16)
        gs_ref[...] = gs
        gw = lax.dot_general(pooled_ref[...], dmx_s[...], (((0,), (0,)), ((), ())), preferred_element_type=F32)
        for j in range(N_CHIPS):
            gw_ref[j, 0] = gw[j * rows:(j + 1) * rows, :].astype(BF16)
        j_i = lax.broadcasted_iota(jnp.int32, (T, T), 0)
        t_i = lax.broadcasted_iota(jnp.int32, (T, T), 1)
        b_cur = ((t_i >= j_i) & (t_i < j_i + win)).astype(BF16)
        b_next = (t_i + T < j_i + win).astype(BF16)
        for r in range(n_t):
            sl = slice(r * T, (r + 1) * T)
            acc = jnp.dot(b_cur, dpc_s[sl, :], preferred_element_type=F32)
            if r + 1 < n_t:
                acc += jnp.dot(b_next, dpc_s[(r + 1) * T:(r + 2) * T, :], preferred_element_type=F32)
            du_ref[sl, :] = (acc - dp_s[sl, :]).astype(BF16)

    col = pl.BlockSpec((S, cg), lambda g: (0, g))
    return _pcall(
        body, name="pool_bwd", grid=(N_GROUPS,),
        in_specs=[col, col] + _pool_w_specs(rows, cg) + [pl.BlockSpec((1, cg), lambda g: (0, g))],
        out_specs=[col, pl.BlockSpec((N_CHIPS, 1, rows, cg), lambda g: (0, g, 0, 0)),
                   pl.BlockSpec((1, cg), lambda g: (0, g))],
        out_shape=[jax.ShapeDtypeStruct((S, PW), BF16),
                   jax.ShapeDtypeStruct((N_CHIPS, N_GROUPS, rows, cg), BF16),
                   jax.ShapeDtypeStruct((1, PW), F32)],
        scratch_shapes=[pltpu.VMEM((S, cg), F32), pltpu.VMEM((S, cg), BF16), pltpu.VMEM((S, cg), BF16)],
        compiler_params=_params(("parallel",)),
    )(dpa, pooled, wp_full, wp_full, wp_full, wp_full, pool_scale)


_NT = (((1,), (1,)), ((), ()))
_TN = (((0,), (0,)), ((), ()))


def _split_dot(v, tri):
    hi = v.astype(BF16)
    lo = (v - hi.astype(F32)).astype(BF16)
    return jnp.dot(hi, tri, preferred_element_type=F32) + jnp.dot(lo, tri, preferred_element_type=F32)


def _sb_scores(q_i, k_j, tri_l, masked):
    tq, tk = q_i.shape[0], k_j.shape[0]
    s = lax.dot_general(q_i, k_j, _NT, preferred_element_type=F32) * (1.0 / math.sqrt(HEAD_DIM))
    lp = jnp.log(1.0 + jnp.exp(-jnp.abs(s)))
    l = -jnp.maximum(s, 0.0) - lp
    lb = l + s
    mask = None
    if masked:
        mask = lax.broadcasted_iota(jnp.int32, (tq, tk), 0) > lax.broadcasted_iota(jnp.int32, (tq, tk), 1)
        l = jnp.where(mask, l, 0.0)
    return l, lb, lb + _split_dot(l, tri_l), mask


def _sb_weights(t, carry_l, mask):
    a = jnp.exp(t + carry_l)
    return a if mask is None else jnp.where(mask, a, 0.0)


def _rowsum(v):
    return jnp.sum(v, axis=1, keepdims=True)


def _qk_norm(x_ref, w_ref):
    xv = x_ref[...].astype(F32)
    r = lax.rsqrt(jnp.mean(xv * xv, axis=-1, keepdims=True) + EPS)
    return xv * r, r


def _attn_fwd(proj, q_norm_w, k_norm_w, S, H, q_off, riders=()):
    t = _tile(S, ATT_T)
    n_q = S // t

    def body(q_ref, k_ref, v_ref, qw_ref, kw_ref, att_ref, attf_ref, qn_s, kn_s):
        qh, _ = _qk_norm(q_ref, qw_ref)
        qn_s[...] = (qh * qw_ref[...]).astype(BF16)
        kh, _ = _qk_norm(k_ref, kw_ref)
        kn_s[...] = (kh * kw_ref[...]).astype(BF16)
        tri_l = (lax.broadcasted_iota(jnp.int32, (t, t), 0) > lax.broadcasted_iota(jnp.int32, (t, t), 1)).astype(BF16)

        def rows(j):
            return pl.ds(pl.multiple_of(j * t, t), t)

        def q_step(i, _):
            q_i = qn_s[rows(i), :]

            def av(a, j):
                return jnp.dot(a.astype(BF16), v_ref[rows(j), :], preferred_element_type=F32)

            l, _, tt, mask = _sb_scores(q_i, kn_s[rows(i), :], tri_l, True)
            acc = av(_sb_weights(tt, 0.0, mask), i)
            carry = _rowsum(l)

            def single(_, c):
                carry, acc = c
                l, _, tt, _ = _sb_scores(q_i, kn_s[rows(i - 1), :], tri_l, False)
                return carry + _rowsum(l), acc + av(_sb_weights(tt, carry, None), i - 1)

            carry, acc = lax.fori_loop(0, i % 2, single, (carry, acc))
            top = i - 1 - i % 2

            def pair(p, c):
                carry, acc = c
                j0 = top - 2 * p
                l0, _, t0, _ = _sb_scores(q_i, kn_s[rows(j0), :], tri_l, False)
                l1, _, t1, _ = _sb_scores(q_i, kn_s[rows(j0 - 1), :], tri_l, False)
                mid = carry + _rowsum(l0)
                acc = acc + av(_sb_weights(t0, carry, None), j0) + av(_sb_weights(t1, mid, None), j0 - 1)
                return mid + _rowsum(l1), acc

            _, acc = lax.fori_loop(0, i // 2, pair, (carry, acc))
            att_ref[rows(i), :] = acc.astype(BF16)
            attf_ref[rows(i), :] = acc
            return 0

        lax.fori_loop(0, n_q, q_step, 0)

    def col(off):
        return pl.BlockSpec((S, HEAD_DIM), lambda h, off=off: (0, off + h))

    wspec = pl.BlockSpec((1, HEAD_DIM), lambda h: (0, 0))
    return _ride(
        "attn_fwd", body, riders, [proj, proj, proj, q_norm_w, k_norm_w], grid=(H,),
        in_specs=[col(q_off), col(q_off + H), col(q_off + 2 * H), wspec, wspec],
        out_specs=[col(0), col(0)],
        out_shape=[jax.ShapeDtypeStruct((S, H * HEAD_DIM), BF16), jax.ShapeDtypeStruct((S, H * HEAD_DIM), F32)],
        scratch_shapes=[pltpu.VMEM((S, HEAD_DIM), BF16), pltpu.VMEM((S, HEAD_DIM), BF16)],
        sem=("parallel",))


def _attn_bwd(proj, datt, attf, q_norm_w, k_norm_w, S, H, q_off, riders=()):
    t = _tile(S, ATT_T)
    n_q = S // t
    scale = 1.0 / math.sqrt(HEAD_DIM)

    def body(q_ref, k_ref, v_ref, do_ref, o_ref, qw_ref, kw_ref, dq_ref, dk_ref, dv_ref, gq_ref, gk_ref,
             qn_s, kn_s, dk_s, dv_s, gq_s):
        qw, kw = qw_ref[...], kw_ref[...]
        qh, _ = _qk_norm(q_ref, qw_ref)
        qn_s[...] = (qh * qw).astype(BF16)
        kh, _ = _qk_norm(k_ref, kw_ref)
        kn_s[...] = (kh * kw).astype(BF16)
        dk_s[...] = jnp.zeros_like(dk_s)
        dv_s[...] = jnp.zeros_like(dv_s)
        gq_s[...] = jnp.zeros_like(gq_s)
        r_i = lax.broadcasted_iota(jnp.int32, (t, t), 0)
        c_i = lax.broadcasted_iota(jnp.int32, (t, t), 1)
        tri_l = (r_i > c_i).astype(BF16)
        tri_e = (r_i >= c_i).astype(BF16)

        def rows(j):
            return pl.ds(pl.multiple_of(j * t, t), t)

        def q_step(i, _):
            q_i = qn_s[rows(i), :]
            do_i = do_ref[rows(i), :]
            d_i = _rowsum(do_i.astype(F32) * o_ref[rows(i), :])

            def scores(j, masked):
                k_j = kn_s[rows(j), :]
                l, lb, tt, mask = _sb_scores(q_i, k_j, tri_l, masked)
                da = lax.dot_general(do_i, v_ref[rows(j), :], _NT, preferred_element_type=F32)
                return k_j, l, lb, tt, mask, da

            def grads(j, sc, carry_l, carry_e, dq_acc):
                k_j, l, lb, tt, mask, da = sc
                a_bf = _sb_weights(tt, carry_l, mask).astype(BF16)
                e = da * a_bf.astype(F32)
                p = d_i - (_split_dot(e, tri_e) + carry_e)
                sig = jnp.exp(lb)
                dz = e * (1.0 - sig) - p * sig
                if mask is not None:
                    dz = jnp.where(mask, dz, 0.0)
                dz = (dz * scale).astype(BF16)
                dk_s[rows(j), :] += lax.dot_general(dz, q_i, _TN, preferred_element_type=F32)
                dv_s[rows(j), :] += lax.dot_general(a_bf, do_i, _TN, preferred_element_type=F32)
                return (carry_l + _rowsum(l), carry_e + _rowsum(e),
                        dq_acc + jnp.dot(dz, k_j, preferred_element_type=F32))

            c = grads(i, scores(i, True), 0.0, 0.0, jnp.zeros((t, HEAD_DIM), F32))
            c = lax.fori_loop(0, i % 2, lambda _, c: grads(i - 1, scores(i - 1, False), *c), c)
            top = i - 1 - i % 2

            def pair(p, c):
                j0 = top - 2 * p
                s0, s1 = scores(j0, False), scores(j0 - 1, False)
                return grads(j0 - 1, s1, *grads(j0, s0, *c))

            _, _, dqn = lax.fori_loop(0, i // 2, pair, c)
            qv = q_ref[rows(i), :].astype(F32)
            r = lax.rsqrt(jnp.mean(qv * qv, axis=-1, keepdims=True) + EPS)
            xh = qv * r
            gq_s[...] += _colsum(dqn * xh)
            dxh = dqn * qw
            dq_ref[rows(i), :] = (r * (dxh - xh * jnp.mean(dxh * xh, axis=-1, keepdims=True))).astype(BF16)
            return 0

        lax.fori_loop(0, n_q, q_step, 0)
        gq_ref[0] = gq_s[...]
        kh, rk = _qk_norm(k_ref, kw_ref)
        dkn = dk_s[...]
        gk_ref[0] = _colsum(dkn * kh)
        dxh = dkn * kw
        dk_ref[...] = (rk * (dxh - kh * jnp.mean(dxh * kh, axis=-1, keepdims=True))).astype(BF16)
        dv_ref[...] = dv_s[...].astype(BF16)

    def col(off):
        return pl.BlockSpec((S, HEAD_DIM), lambda h, off=off: (0, off + h))

    wspec = pl.BlockSpec((1, HEAD_DIM), lambda h: (0, 0))
    gspec = pl.BlockSpec((1, 1, HEAD_DIM), lambda h: (h, 0, 0))
    act = jax.ShapeDtypeStruct((S, H * HEAD_DIM), BF16)
    gsh = jax.ShapeDtypeStruct((H, 1, HEAD_DIM), F32)
    return _ride(
        "attn_bwd", body, riders, [proj, proj, proj, datt, attf, q_norm_w, k_norm_w], grid=(H,),
        in_specs=[col(q_off), col(q_off + H), col(q_off + 2 * H), col(0), col(0), wspec, wspec],
        out_specs=[col(0), col(0), col(0), gspec, gspec],
        out_shape=[act, act, act, gsh, gsh],
        scratch_shapes=[pltpu.VMEM((S, HEAD_DIM), BF16), pltpu.VMEM((S, HEAD_DIM), BF16),
                        pltpu.VMEM((S, HEAD_DIM), F32), pltpu.VMEM((S, HEAD_DIM), F32),
                        pltpu.VMEM((1, HEAD_DIM), F32)],
        sem=("parallel",))


def _place():
    x, y, c = lax.axis_index("x"), lax.axis_index("y"), lax.axis_index("c")
    chips = [(1 - x, y), (x, 1 - y), (1 - x, 1 - y)]
    return x, y, c, chips


def _dev_allgather(name, v):
    m_per, n = v.shape

    def body(x_ref, out_ref, send_sems, recv_sems, local_sem):
        x, y, c, chips = _place()
        me, sibling = (x, y, c), (x, y, 1 - c)

        def rows(px, py, pc):
            return out_ref.at[pl.ds((4 * px + 2 * py + pc) * m_per, m_per), :]

        def copy(k, block, to, src=None):
            return pltpu.make_async_remote_copy(
                src_ref=rows(*block) if src is None else src, dst_ref=rows(*block),
                send_sem=send_sems.at[k], recv_sem=recv_sems.at[k], device_id=to, device_id_type=MESH)

        mine = pltpu.make_async_copy(x_ref, rows(*me), local_sem)
        mine.start()
        first = [copy(0, me, sibling, src=x_ref)]
        first += [copy(1 + j, me, (*chip, c), src=x_ref) for j, chip in enumerate(chips)]
        for cp in first:
            cp.start()
        passed = [copy(4 + j, (*chip, c), sibling) for j, chip in enumerate(chips)]
        for j, chip in enumerate(chips):
            copy(1 + j, (*chip, c), me).wait_recv()
            passed[j].start()
        copy(0, sibling, me).wait_recv()
        for j, chip in enumerate(chips):
            copy(4 + j, (*chip, 1 - c), me).wait_recv()
        for cp in first + passed:
            cp.wait_send()
        mine.wait()

    return _pcall(
        body, name=name, out_shape=jax.ShapeDtypeStruct((N_DEV * m_per, n), v.dtype),
        in_specs=[pl.BlockSpec(memory_space=pltpu.VMEM)], out_specs=pl.BlockSpec(memory_space=pltpu.VMEM),
        scratch_shapes=[pltpu.SemaphoreType.DMA((7,)), pltpu.SemaphoreType.DMA((7,)), pltpu.SemaphoreType.DMA],
        compiler_params=pltpu.CompilerParams(vmem_limit_bytes=VMEM_LIMIT_V7X),
    )(v)


class _W:
    def __init__(self, name, kind, R, C):
        self.name, self.kind, self.R, self.C = name, kind, R, C

    @property
    def shard_shape(self):
        return (self.R, self.C // N_CHIPS) if self.kind == "col" else (self.R // N_CHIPS, self.C)

    @property
    def half_rows(self):
        return self.shard_shape[0] // 2

    def shard_half(self, ref, half):
        return ref.at[pl.ds(half * self.half_rows, self.half_rows), :]

    def region(self, full_ref, chip, half):
        hr = self.half_rows
        if self.kind == "col":
            cw = self.C // N_CHIPS
            return full_ref.at[pl.ds(half * hr, hr), pl.ds(chip * cw, cw)]
        return full_ref.at[pl.ds(chip * (2 * hr) + half * hr, hr), :]

    def region_both(self, full_ref, chip):
        hr = self.half_rows
        if self.kind == "col":
            cw = self.C // N_CHIPS
            return full_ref.at[:, pl.ds(chip * cw, cw)]
        return full_ref.at[pl.ds(chip * (2 * hr), 2 * hr), :]


def _ag_rider(ws, fulls, n_ch=4, chunks=None):
    n_w = len(ws)
    lo, hi = chunks or (0, n_ch)
    per = 6

    def parts(full, sems):
        send_sems, recv_sems = sems
        x, y, c, _ = _place()
        xn, yn, dg = (1 - x, y), (x, 1 - y), (1 - x, 1 - y)
        via = (x + (1 - c) * (1 - 2 * x), y + c * (1 - 2 * y))
        to = (x + c * (1 - 2 * x), y + (1 - c) * (1 - 2 * y))

        def reg(i, chip, half, t):
            nr = ws[i].half_rows // n_ch
            return ws[i].region(full[i], 2 * chip[0] + chip[1], half).at[pl.ds(t * nr, nr), :]

        def copy(r, i, t, k, dev):
            s = (i * (hi - lo) + t - lo) * per + k
            return pltpu.make_async_remote_copy(src_ref=r, dst_ref=r, send_sem=send_sems.at[s],
                                                recv_sem=recv_sems.at[s], device_id=dev, device_id_type=MESH)

        def direct(i, t, k):
            return copy(reg(i, (x, y), c, t), i, t, k, (*(via, to)[k], c))

        def direct_in(i, t, k):
            return copy(reg(i, (via, to)[k], c, t), i, t, k, (*(via, to)[k], c))

        def relay(i, t):
            return copy(reg(i, via, c, t), i, t, 2, (*to, c))

        def relay_in(i, t):
            return copy(reg(i, dg, c, t), i, t, 2, (*to, c))

        def hand(i, t, k, half):
            return copy(reg(i, (xn, yn, dg)[k], half, t), i, t, 3 + k, (x, y, 1 - c))

        return c, direct, direct_in, relay, relay_in, hand

    def start(_, full, sems):
        _, direct, _, _, _, _ = parts(full, sems)
        for t in range(lo, hi):
            for i in range(n_w):
                direct(i, t, 0).start()
                direct(i, t, 1).start()

    def arrived(t):
        def step(_, full, sems):
            c, _, direct_in, relay, relay_in, hand = parts(full, sems)
            for i in range(n_w):
                direct_in(i, t, 0).wait_recv()
                direct_in(i, t, 1).wait_recv()
                relay(i, t).start()
                hand(i, t, 0, c).start()
                hand(i, t, 1, c).start()
                if t > lo:
                    relay_in(i, t - 1).wait_recv()
                    hand(i, t - 1, 2, c).start()
        return step

    def finish(_, full, sems):
        c, direct, _, relay, relay_in, hand = parts(full, sems)
        for i in range(n_w):
            relay_in(i, hi - 1).wait_recv()
            hand(i, hi - 1, 2, c).start()
        for i in range(n_w):
            for t in range(lo, hi):
                for k in range(3):
                    hand(i, t, k, 1 - c).wait_recv()
        for i in range(n_w):
            for t in range(lo, hi):
                direct(i, t, 0).wait_send()
                direct(i, t, 1).wait_send()
                relay(i, t).wait_send()
                for k in range(3):
                    hand(i, t, k, c).wait_send()

    n_sem = per * (hi - lo) * n_w
    return _Rider(fulls, [jax.ShapeDtypeStruct((w.R, w.C), BF16) for w in ws],
                  [pltpu.SemaphoreType.DMA((n_sem,)), pltpu.SemaphoreType.DMA((n_sem,))], start, finish,
                  steps=[arrived(t) for t in range(lo, hi)], aliases={i: i for i in range(n_w)})


def _cast_into_full(w, a32, chip_arr):
    sr, sc = w.shard_shape
    tr, tc = _tile(sr, 512), _tile(sc, 2048)
    n_r, n_c = sr // tr, sc // tc
    if w.kind == "col":
        out_spec = pl.BlockSpec((tr, tc), lambda i, j, chip: (i, chip[0] * n_c + j))
    else:
        out_spec = pl.BlockSpec((tr, tc), lambda i, j, chip: (chip[0] * n_r + i, j))

    def body(chip_ref, a_ref, o_ref):
        o_ref[...] = a_ref[...].astype(BF16)

    return _pcall(
        body, name="cast_" + w.name, out_shape=jax.ShapeDtypeStruct((w.R, w.C), BF16),
        grid_spec=pltpu.PrefetchScalarGridSpec(
            num_scalar_prefetch=1, grid=(n_r, n_c),
            in_specs=[pl.BlockSpec((tr, tc), lambda i, j, chip: (i, j))], out_specs=out_spec),
        compiler_params=_params(("parallel", "parallel")),
    )(chip_arr, a32)


def _half_view(w, g):
    return g if w.kind == "col" else g.reshape(N_CHIPS, w.R // N_CHIPS, w.C)


def _px_rider(ws, grads):
    n_w = len(ws)

    def copies(g, got, sems):
        send_sems, recv_sems = sems
        x, y, c, _ = _place()

        def half_all(w, ref, half):
            hr = w.half_rows
            if w.kind == "col":
                return ref.at[pl.ds(half * hr, hr), :]
            return ref.at[:, pl.ds(half * hr, hr), :]

        return [pltpu.make_async_remote_copy(
            src_ref=half_all(w, g[i], 1 - c), dst_ref=got[i], send_sem=send_sems.at[i], recv_sem=recv_sems.at[i],
            device_id=(x, y, 1 - c), device_id_type=MESH) for i, w in enumerate(ws)]

    def start(g, got, sems):
        for cp in copies(g, got, sems):
            cp.start()

    def finish(g, got, sems):
        for cp in copies(g, got, sems):
            cp.wait_recv()
            cp.wait_send()

    def got_shape(w):
        hr = w.half_rows
        return (hr, w.C) if w.kind == "col" else (N_CHIPS, hr, w.C)

    return _Rider([_half_view(w, g) for w, g in zip(ws, grads)],
                  [jax.ShapeDtypeStruct(got_shape(w), BF16) for w in ws],
                  [pltpu.SemaphoreType.DMA((n_w,)), pltpu.SemaphoreType.DMA((n_w,))], start, finish)


def _pair_sum(w, g, got, c_arr):
    hr = w.half_rows
    if w.kind == "col":
        tr, tc = _tile(hr, 512), _tile(w.C, 2048)
        n_r = hr // tr
        grid = (n_r, w.C // tc)
        g_spec = pl.BlockSpec((tr, tc), lambda i, j, c: (c[0] * n_r + i, j))
        o_spec = pl.BlockSpec((tr, tc), lambda i, j, c: (i, j))
    else:
        tr = _tile(hr, 512)
        n_r = hr // tr
        grid = (N_CHIPS, n_r)
        g_spec = pl.BlockSpec((1, tr, w.C), lambda s, i, c: (s, c[0] * n_r + i, 0))
        o_spec = pl.BlockSpec((1, tr, w.C), lambda s, i, c: (s, i, 0))

    def body(c_ref, g_ref, got_ref, out_ref):
        out_ref[...] = (g_ref[...].astype(F32) + got_ref[...].astype(F32)).astype(BF16)

    return _pcall(
        body, name="grad_pair_sum_" + w.name, out_shape=jax.ShapeDtypeStruct(got.shape, BF16),
        grid_spec=pltpu.PrefetchScalarGridSpec(num_scalar_prefetch=1, grid=grid, in_specs=[g_spec, o_spec],
                                               out_specs=o_spec),
        compiler_params=_params(("parallel", "parallel")),
    )(c_arr, _half_view(w, g), got)


def _cx_rider(ws, sums, part=(0, 1), q_in=None):
    n_w = len(ws)

    def parts(p, q, sems):
        send_sems, recv_sems = sems
        x, y, c, chips = _place()
        my_chip = 2 * x + y

        def rows(w, ref):
            nr = w.half_rows // part[1]
            return ref.at[pl.ds(part[0] * nr, nr), :]

        def piece(w, ref, chip):
            if w.kind == "col":
                cw = w.C // N_CHIPS
                return rows(w, ref.at[:, pl.ds(chip * cw, cw)])
            return rows(w, ref.at[chip])

        def copy(i, k, recv=False):
            chip = chips[k]
            to_chip = 2 * chip[0] + chip[1]
            return pltpu.make_async_remote_copy(
                src_ref=piece(ws[i], p[i], to_chip), dst_ref=rows(ws[i], q[i].at[to_chip if recv else my_chip]),
                send_sem=send_sems.at[3 * i + k], recv_sem=recv_sems.at[3 * i + k],
                device_id=(*chip, c), device_id_type=MESH)

        return copy

    both = [(i, k) for i in range(n_w) for k in range(N_CHIPS - 1)]

    def start(p, q, sems):
        copy = parts(p, q, sems)
        for i, k in both:
            copy(i, k).start()

    def finish(p, q, sems):
        copy = parts(p, q, sems)
        for i, k in both:
            copy(i, k, recv=True).wait_recv()
        for i, k in both:
            copy(i, k).wait_send()

    return _Rider(list(sums) + list(q_in or []),
                  [jax.ShapeDtypeStruct((N_CHIPS, w.half_rows, w.shard_shape[1]), BF16) for w in ws],
                  [pltpu.SemaphoreType.DMA((3 * n_w,)), pltpu.SemaphoreType.DMA((3 * n_w,))], start, finish,
                  aliases={n_w + i: i for i in range(n_w)} if q_in else None)


def _chip_sum(w, p, q, cc_arr):
    hr, cols = w.half_rows, w.shard_shape[1]
    tr, tc = _tile(hr, 512), _tile(cols, 2048)
    n_r, n_c = hr // tr, cols // tc

    def body(cc_ref, own, q1, q2, q3, out_ref):
        own_v = own[...] if w.kind == "col" else own[0]
        out_ref[...] = ((own_v.astype(F32) + q1[0].astype(F32)) + q2[0].astype(F32)) + q3[0].astype(F32)

    if w.kind == "col":
        own_spec = pl.BlockSpec((tr, tc), lambda i, j, cc: (i, cc[1] * n_c + j))
    else:
        own_spec = pl.BlockSpec((1, tr, tc), lambda i, j, cc: (cc[1], i, j))
    q_specs = [pl.BlockSpec((1, tr, tc), lambda i, j, cc, s=s: ((cc[1] + s) % N_CHIPS, i, j)) for s in (1, 2, 3)]
    return _pcall(
        body, name="grad_chip_sum_" + w.name, out_shape=jax.ShapeDtypeStruct(w.shard_shape, F32),
        grid_spec=pltpu.PrefetchScalarGridSpec(
            num_scalar_prefetch=1, grid=(n_r, n_c), in_specs=[own_spec] + q_specs,
            out_specs=pl.BlockSpec((tr, tc), lambda i, j, cc: (cc[0] * n_r + i, j))),
        compiler_params=_params(("parallel", "parallel")),
    )(cc_arr, p, q, q, q)


def _sf_rider(ws, grads):
    n_w = len(ws)

    def copy(g, sems, i, half):
        send_sems, recv_sems = sems
        x, y, c, _ = _place()
        h = c if half == "mine" else 1 - c
        reg = ws[i].shard_half(g[i], h)
        return pltpu.make_async_remote_copy(src_ref=reg, dst_ref=reg, send_sem=send_sems.at[i], recv_sem=recv_sems.at[i],
                                            device_id=(x, y, 1 - c), device_id_type=MESH)

    def start(_, g, sems):
        for i in range(n_w):
            copy(g, sems, i, "mine").start()

    def finish(_, g, sems):
        for i in range(n_w):
            copy(g, sems, i, "other").wait_recv()
            copy(g, sems, i, "mine").wait_send()

    return _Rider(grads, [jax.ShapeDtypeStruct(w.shard_shape, F32) for w in ws],
                  [pltpu.SemaphoreType.DMA((n_w,)), pltpu.SemaphoreType.DMA((n_w,))], start, finish,
                  aliases={i: i for i in range(n_w)})


def _adamw_math(w, g, m, v):
    m = ADAM_B1 * m + (1.0 - ADAM_B1) * g
    v = ADAM_B2 * v + (1.0 - ADAM_B2) * (g * g)
    m_hat = m / (1.0 - ADAM_B1 ** ADAM_STEP)
    v_hat = v / (1.0 - ADAM_B2 ** ADAM_STEP)
    delta = -ADAM_LR * (m_hat / (jnp.sqrt(v_hat) + ADAM_EPS) + ADAM_WD * w)
    return delta, m, v


def _adamw(name, w, g, m, v):
    R, C = w.shape
    tr, tc = _tile(R, 256), _tile(C, 2048)

    def body(w_ref, g_ref, m_ref, v_ref, g_out, d_out, m_out, v_out):
        g = g_ref[...]
        g_out[...] = g
        d_out[...], m_out[...], v_out[...] = _adamw_math(w_ref[...], g, m_ref[...], v_ref[...])

    spec = pl.BlockSpec((tr, tc), lambda i, j: (i, j))
    sh = jax.ShapeDtypeStruct((R, C), F32)
    return _pcall(body, name=name, grid=(R // tr, C // tc), in_specs=[spec] * 4, out_specs=[spec] * 4,
                  out_shape=[sh] * 4, compiler_params=_params(("parallel", "parallel")))(w, g, m, v)


def _ada_update(sct, dmod_sh, w, m, v, riders=()):
    R, C = w.shape
    tr, tc = _tile(R, 256), _tile(C, 1024)

    def body(s_ref, d_ref, w_ref, m_ref, v_ref, g_out, d_out, m_out, v_out):
        s, d = s_ref[...], d_ref[...]
        g = s[:, 0:1] * d[0:1, :]
        for b in range(1, N_DEV):
            g += s[:, b:b + 1] * d[b:b + 1, :]
        g_out[...] = g
        d_out[...], m_out[...], v_out[...] = _adamw_math(w_ref[...], g, m_ref[...], v_ref[...])

    spec = pl.BlockSpec((tr, tc), lambda i, j: (i, j))
    sh = jax.ShapeDtypeStruct((R, C), F32)
    return _ride(
        "ada_update", body, riders, [sct, dmod_sh, w, m, v], grid=(R // tr, C // tc),
        in_specs=[pl.BlockSpec((tr, N_DEV), lambda i, j: (i, 0)), pl.BlockSpec((N_DEV, tc), lambda i, j: (0, j)),
                  spec, spec, spec],
        out_specs=[spec] * 4, out_shape=[sh] * 4, scratch_shapes=[], sem=("parallel", "parallel"))


def _silu_rows(c_row):
    D = c_row.shape[1]

    def body(c_ref, o_ref):
        cv = c_ref[...]
        o_ref[...] = cv * jax.nn.sigmoid(cv)

    return _pcall(body, name="silu_c", out_shape=jax.ShapeDtypeStruct((1, D), F32))(c_row)


def _pack_partials(parts, widths, total):
    n = len(widths)

    def body(*refs):
        loss_p, out_ref, loss_ref = refs[n], refs[n + 1], refs[n + 2]
        off = 0
        for ref, wd in zip(refs[:n], widths):
            out_ref[:, off:off + wd] = jnp.sum(ref[...], axis=0)
            off += wd
        if off < total:
            out_ref[:, off:total] = jnp.zeros((1, total - off), F32)
        loss_ref[...] = jnp.sum(jnp.sum(loss_p[...], axis=0), axis=1, keepdims=True)

    return _pcall(body, name="pack_partials",
                  out_shape=[jax.ShapeDtypeStruct((1, total), F32), jax.ShapeDtypeStruct((1, 1), F32)])(*parts)


def _small_update(gathered, offsets, params):
    n_p = len(params)

    def body(*refs):
        g_ref = refs[0]
        prm = refs[1:1 + 3 * n_p]
        outs = refs[1 + 3 * n_p:]
        for i, (off, wd) in enumerate(offsets):
            blk = g_ref[:, off:off + wd]
            g = blk[0:1, :]
            for b in range(1, N_DEV):
                g = g + blk[b:b + 1, :]
            w, m, v = prm[3 * i][...], prm[3 * i + 1][...], prm[3 * i + 2][...]
            outs[4 * i][...] = g
            outs[4 * i + 1][...], outs[4 * i + 2][...], outs[4 * i + 3][...] = _adamw_math(w, g, m, v)

    flat = [a for t in params for a in t]
    out_shape = [jax.ShapeDtypeStruct(t[0].shape, F32) for t in params for _ in range(4)]
    return _pcall(body, name="small_update", out_shape=out_shape)(gathered, *flat)


def kernel(x, c, w_ada, b_ada, norm1_w, w_in, q_norm_w, k_norm_w, w_pool, pool_scale, w_a_up, w_b_up, w_o, norm2_w, w_ff1, w_ff2, loss_target, m_w_ada, m_b_ada, m_norm1_w, m_w_in, m_q_norm_w, m_k_norm_w, m_w_pool, m_pool_scale, m_w_a_up, m_w_b_up, m_w_o, m_norm2_w, m_w_ff1, m_w_ff2, v_w_ada, v_b_ada, v_norm1_w, v_w_in, v_q_norm_w, v_k_norm_w, v_w_pool, v_pool_scale, v_w_a_up, v_w_b_up, v_w_o, v_norm2_w, v_w_ff1, v_w_ff2):
    _, S, D = x.shape
    PW = D // 2
    H = PW // HEAD_DIM
    cg = PW // N_GROUPS
    IN = w_in.shape[2] * N_CHIPS
    FF = w_ff1.shape[2] * N_CHIPS
    A_COLS = w_ada.shape[2]
    xi, yi, ci = lax.axis_index("x"), lax.axis_index("y"), lax.axis_index("c")
    chip = 2 * xi + yi
    dev = 2 * chip + ci
    c_arr = jnp.reshape(ci, (1,)).astype(jnp.int32)
    x2, tgt = x[0], loss_target[0]

    ws = [_W("w_in", "col", D, IN), _W("w_pool", "row", PW, cg), _W("w_a_up", "col", PW, D),
          _W("w_b_up", "col", PW, D), _W("w_o", "row", D, D), _W("w_ff1", "col", D, FF), _W("w_ff2", "row", FF, D)]
    w32 = [w_in[0], w_pool[0].reshape(cg, cg), w_a_up[0], w_b_up[0], w_o[0], w_ff1[0], w_ff2[0]]
    m32 = [m_w_in[0], m_w_pool[0].reshape(cg, cg), m_w_a_up[0], m_w_b_up[0], m_w_o[0], m_w_ff1[0], m_w_ff2[0]]
    v32 = [v_w_in[0], v_w_pool[0].reshape(cg, cg), v_w_a_up[0], v_w_b_up[0], v_w_o[0], v_w_ff1[0], v_w_ff2[0]]

    W_IN, W_POOL, W_A, W_B, W_O, W_FF1, W_FF2 = ws
    chip_arr = jnp.reshape(chip, (1,)).astype(jnp.int32)
    cc_arr = jnp.stack([ci, chip]).astype(jnp.int32)
    s_in, s_pool, s_a, s_b, s_o, s_ff1, s_ff2 = [_cast_into_full(w, a, chip_arr) for w, a in zip(ws, w32)]
    (win_f,) = _run_rider("gather_w_in", _ag_rider([W_IN], [s_in]))

    sc_row = _silu_rows(c)
    sc_all = _dev_allgather("gather_silu_c", sc_row.reshape(8, D // 8)).reshape(N_DEV, D)
    sc16 = jnp.concatenate([sc_all, jnp.zeros_like(sc_all)], axis=0)
    b_cols = lax.dynamic_slice(b_ada, (0, chip * A_COLS), (1, A_COLS))
    (mod_cols,) = _mm("mod_cols", [(sc16, w_ada[0])], M=2 * N_DEV, N=A_COLS, K=D, tm=16, tn=1024, tk=1024,
                      a_pro=lambda a: a.astype(BF16), b_pro=lambda b: b.astype(BF16),
                      extras=[(b_cols, "row", 0)], outs=[_tile_out(F32)], epi=lambda accs, ex: [accs[0] + ex[0]])
    mod_all = _dev_allgather("gather_mod", mod_cols[:N_DEV]).reshape(N_CHIPS, 2, N_DEV, A_COLS)
    mod_row = lax.dynamic_index_in_dim(mod_all[:, 0], dev, axis=1, keepdims=False).reshape(1, N_CHIPS * A_COLS)
    shift1, scale1, gate1, shift2, scale2, gate2 = [mod_row[:, i * D:(i + 1) * D] for i in range(6)]

    WIDE = dict(tm=2048, tn=512, tk=2048)
    DEEP = dict(tm=1024, tn=1024, tk=1024)
    h = _norm_mod("norm1_mod", x2, norm1_w, scale1, shift1)
    (proj,), ((wpool_f, wa_f, wb_f, wo_f),) = _mm(
        "in_proj", [(h, win_f)], M=S, N=IN, K=D, outs=[_tile_out(BF16)], epi=lambda accs, ex: [accs[0]], **WIDE,
        riders=[_ag_rider([W_POOL, W_A, W_B, W_O], [s_pool, s_a, s_b, s_o], n_ch=2)])
    pooled, pa = _pool_fwd(proj, wpool_f, pool_scale, S, PW)
    (att, attf), ((wff1_f,),) = _attn_fwd(proj, q_norm_w, k_norm_w, S, H, PW // HEAD_DIM,
                                          riders=[_ag_rider([W_FF1], [s_ff1])])

    def merge_epi(accs, ex):
        sa, sb = jax.nn.sigmoid(ex[0].astype(F32)), jax.nn.sigmoid(ex[1].astype(F32))
        return [sa * accs[0] + sb * accs[1], accs[0], accs[1]]

    (merged, ya, yb), (ff2_a,) = _mm("branch_up_merge", [(pa, wa_f), (att, wb_f)], M=S, N=D, K=PW,
                                     extras=[(proj, "tile", 4 * PW), (proj, "tile", 4 * PW + D)],
                                     outs=[_tile_out(BF16)] * 3, epi=merge_epi,
                                     riders=[_ag_rider([W_FF2], [s_ff2], chunks=(0, 1))])
    (x1, o), (ff2_b,) = _mm("out_proj", [(merged, wo_f)], M=S, N=D, K=D, extras=[(x2, "tile", 0), (gate1, "row", 0)],
                            outs=[_tile_out(F32), _tile_out(BF16)], epi=lambda accs, ex: [ex[0] + ex[1] * accs[0], accs[0]],
                            riders=[_ag_rider([W_FF2], ff2_a, chunks=(1, 2))], **WIDE)
    h2 = _norm_mod("norm2_mod", x1, norm2_w, scale2, shift2)
    (rl,), ((wff2_f,),) = _mm("ff1", [(h2, wff1_f)], M=S, N=FF, K=D, outs=[_tile_out(BF16)], **WIDE,
                              epi=lambda accs, ex: [jnp.maximum(accs[0], 0.0)],
                              riders=[_ag_rider([W_FF2], ff2_b, chunks=(2, 4))])

    def square(a):
        af = a.astype(F32)
        return (af * af).astype(BF16)

    def loss_epi(accs, ex):
        x1_t, tgt_t, g2 = ex
        f = accs[0]
        diff = (x1_t + g2 * f) - tgt_t
        dy = diff * (1.0 / D)
        return [dy, dy * g2, _colsum(dy * f), _colsum(diff * diff)]

    dy, df, dgate2_p, loss_p = _mm("ff2_loss", [(rl, wff2_f)], M=S, N=D, K=FF, a_pro=square, tm=1024, tn=1024, tk=512,
                                   extras=[(x1, "tile", 0), (tgt, "tile", 0), (gate2, "row", 0)],
                                   outs=[_tile_out(F32), _tile_out(BF16), _COLSUM, _COLSUM], epi=loss_epi)

    def pair_sums(group, partials, got):
        return [_pair_sum(w, g, r, c_arr) for w, g, r in zip(group, partials, got)]

    def chip_sums(group, sums, from_chips):
        return [_chip_sum(w, p, q, cc_arr) for w, p, q in zip(group, sums, from_chips)]

    first = lambda accs, ex: [accs[0]]
    gmm = dict(ta=True, outs=[_tile_out(BF16)], epi=first, **WIDE)
    (g_ff2,) = _mm("grad_w_ff2", [(rl, df)], M=FF, N=D, K=S, a_pro=square, ta=True, tm=512, tn=2048, tk=2048,
                   outs=[_tile_out(BF16)], epi=first)
    (dz1,), (got_ff2,) = _mm("d_ff_hidden", [(df, wff2_f)], M=S, N=FF, K=D, tb=True, extras=[(rl, "tile", 0)], **WIDE,
                             outs=[_tile_out(BF16)], epi=lambda accs, ex: [accs[0] * (2.0 * ex[0].astype(F32))],
                             riders=[_px_rider([W_FF2], [g_ff2])])
    sum_ff2 = pair_sums([W_FF2], [g_ff2], got_ff2)
    (g_ff1,), (q_ff2,) = _mm("grad_w_ff1", [(h2, dz1)], M=D, N=FF, K=S,
                             riders=[_cx_rider([W_FF2], sum_ff2, part=(0, 2))], **gmm)
    (dh2,), (got_ff1, q_ff2) = _mm("d_h2", [(dz1, wff1_f)], M=S, N=D, K=FF, tb=True, outs=[_tile_out(F32)], epi=first,
                                   riders=[_px_rider([W_FF1], [g_ff1]),
                                           _cx_rider([W_FF2], sum_ff2, part=(1, 2), q_in=q_ff2)], **DEEP)
    sum_ff1 = pair_sums([W_FF1], [g_ff1], got_ff1)
    dx1, dshift2_p, dscale2_p, gn2_p, do, dgate1_p = _norm_mod_bwd("norm2_bwd", dh2, x1, dy, norm2_w, scale2,
                                                                   gate_o=(o, gate1))
    (g_wo,) = _mm("grad_w_o", [(merged, do)], M=D, N=D, K=S, **gmm)

    def gate_epi(accs, ex):
        dm = accs[0]
        sa, sb = jax.nn.sigmoid(ex[0].astype(F32)), jax.nn.sigmoid(ex[1].astype(F32))
        ya_t, yb_t = ex[2].astype(F32), ex[3].astype(F32)
        return [dm * sa, dm * sb, dm * ya_t * (sa * (1.0 - sa)), dm * yb_t * (sb * (1.0 - sb))]

    dya, dyb, dga, dgb = _mm("d_merged", [(do, wo_f)], M=S, N=D, K=D, tb=True, tm=1024, tn=512, tk=2048,
                             extras=[(proj, "tile", 4 * PW), (proj, "tile", 4 * PW + D), (ya, "tile", 0), (yb, "tile", 0)],
                             outs=[_tile_out(BF16)] * 4, epi=gate_epi)
    (g_wa,) = _mm("grad_w_a_up", [(pa, dya)], M=PW, N=D, K=S, **gmm)
    (g_wb,) = _mm("grad_w_b_up", [(att, dyb)], M=PW, N=D, K=S, **gmm)
    (dpa,) = _mm("d_pool_out", [(dya, wa_f)], M=S, N=PW, K=D, tb=True, outs=[_tile_out(F32)], epi=first, **WIDE)
    mid = [W_A, W_B, W_O]
    (datt,), (got_mid,) = _mm("d_att", [(dyb, wb_f)], M=S, N=PW, K=D, tb=True, outs=[_tile_out(BF16)], epi=first, **WIDE,
                              riders=[_px_rider(mid, [g_wa, g_wb, g_wo])])
    sum_mid = pair_sums(mid, [g_wa, g_wb, g_wo], got_mid)
    du, g_wpool4, gscale_p = _pool_bwd(dpa, pooled, wpool_f, pool_scale, S, PW)
    (dq, dk, dv, gq_p, gk_p), ((q_ff1,),) = _attn_bwd(
        proj, datt, attf, q_norm_w, k_norm_w, S, H, PW // HEAD_DIM, riders=[_cx_rider([W_FF1], sum_ff1)])
    dproj = jnp.concatenate([du, dq, dk, dv, dga, dgb], axis=1)
    early = [W_FF1, W_FF2]
    halves_early = chip_sums(early, sum_ff1 + sum_ff2, [q_ff1, q_ff2[0]])
    (g_win,), (grads_early, (q_wa, q_wb, q_wo)) = _mm(
        "grad_w_in", [(h, dproj)], M=D, N=IN, K=S, riders=[_sf_rider(early, halves_early), _cx_rider(mid, sum_mid)], **gmm)
    last = [W_IN, W_POOL]
    g_last = [g_win, g_wpool4.reshape(PW, cg)]
    dh_kw = dict(M=S, N=D // 2, K=IN, tb=True, outs=[_tile_out(F32)], epi=first, **DEEP)
    (dh_left,), (got_last,) = _mm("d_h_left", [(dproj, win_f)], riders=[_px_rider(last, g_last)], **dh_kw)
    sum_last = pair_sums(last, g_last, got_last)
    (dh_right,), ((q_win, q_wpool),) = _mm("d_h_right", [(dproj, win_f)], b_noff=D // 2,
                                           riders=[_cx_rider(last, sum_last)], **dh_kw)
    grad_x, dshift1_p, dscale1_p, gn1_p = _norm_mod_bwd("norm1_bwd", (dh_left, dh_right), x2, dx1, norm1_w, scale1)

    parts = [dshift1_p, dscale1_p, dgate1_p, dshift2_p, dscale2_p, dgate2_p, gn1_p, gn2_p,
             gscale_p.reshape(1, 1, PW), gq_p, gk_p]
    widths = [D] * 8 + [PW, HEAD_DIM, HEAD_DIM]
    used = sum(widths)
    P = -(-used // 1024) * 1024
    packed, loss_part = _pack_partials(parts + [loss_p], widths, P)
    gathered = _dev_allgather("gather_vector_grads", packed.reshape(8, P // 8)).reshape(N_DEV, P)
    small = [(b_ada, m_b_ada, v_b_ada), (norm1_w, m_norm1_w, v_norm1_w), (norm2_w, m_norm2_w, v_norm2_w),
             (pool_scale, m_pool_scale, v_pool_scale), (q_norm_w, m_q_norm_w, v_q_norm_w),
             (k_norm_w, m_k_norm_w, v_k_norm_w)]
    offsets = [(0, 6 * D), (6 * D, D), (7 * D, D), (8 * D, PW), (8 * D + PW, HEAD_DIM), (8 * D + PW + HEAD_DIM, HEAD_DIM)]
    su = _small_update(gathered, offsets, small)
    (g_b, d_b, nm_b, nv_b, g_n1, d_n1, nm_n1, nv_n1, g_n2, d_n2, nm_n2, nv_n2, g_ps, d_ps, nm_ps, nv_ps,
     g_qn, d_qn, nm_qn, nv_qn, g_kn, d_kn, nm_kn, nv_kn) = su
    dmod_sh = lax.dynamic_slice(gathered, (0, chip * A_COLS), (N_DEV, A_COLS))
    g_ada, d_ada, nm_ada, nv_ada = _ada_update(sc_all.T, dmod_sh, w_ada[0], m_w_ada[0], v_w_ada[0])

    halves_late = chip_sums(last + mid, sum_last + sum_mid, [q_win, q_wpool, q_wa, q_wb, q_wo])
    filled = list(_run_rider("grad_sibling_fill", _sf_rider(last + mid, halves_late))) + list(grads_early)
    upd = [_adamw("adamw_" + w.name, a, g, m, v) for w, a, g, m, v in zip(ws, w32, filled, m32, v32)]

    loss = 0.5 / D * lax.psum(loss_part[0, 0], ("x", "y", "c"))

    def up(a):
        return a[None]

    def pool4(a):
        return a.reshape(1, N_GROUPS, cg // N_CHIPS, cg)

    (gr_win, d_win, nm_win, nv_win), (gr_wp, d_wp, nm_wp, nv_wp), (gr_wa, d_wa, nm_wa, nv_wa), \
        (gr_wb, d_wb, nm_wb, nv_wb), (gr_wo, d_wo, nm_wo, nv_wo), (gr_f1, d_f1, nm_f1, nv_f1), \
        (gr_f2, d_f2, nm_f2, nv_f2) = upd
    return (
        loss, grad_x[None],
        up(g_ada), g_b, g_n1, up(gr_win), g_qn, g_kn, pool4(gr_wp), g_ps, up(gr_wa), up(gr_wb), up(gr_wo), g_n2,
        up(gr_f1), up(gr_f2),
        up(d_ada), d_b, d_n1, up(d_win), d_qn, d_kn, pool4(d_wp), d_ps, up(d_wa), up(d_wb), up(d_wo), d_n2,
        up(d_f1), up(d_f2),
        up(nm_ada), nm_b, nm_n1, up(nm_win), nm_qn, nm_kn, pool4(nm_wp), nm_ps, up(nm_wa), up(nm_wb), up(nm_wo), nm_n2,
        up(nm_f1), up(nm_f2),
        up(nv_ada), nv_b, nv_n1, up(nv_win), nv_qn, nv_kn, pool4(nv_wp), nv_ps, up(nv_wa), up(nv_wb), up(nv_wo), nv_n2,
        up(nv_f1), up(nv_f2),
    )
```

```python
import functools
import math

import jax
import jax.numpy as jnp
from jax import lax
from jax.experimental import pallas as pl
from jax.experimental.pallas import tpu as pltpu

F32 = jnp.float32
BF16 = jnp.bfloat16
MESH = pl.DeviceIdType.MESH
ANY = pl.BlockSpec(memory_space=pl.ANY)

EPS = 1e-6
HEAD_DIM = 128
POOL_WINDOWS = (2, 4, 8, 16)
N_GROUPS = len(POOL_WINDOWS)
N_CHIPS = 4
N_DEV = 8
ADAM_LR, ADAM_B1, ADAM_B2, ADAM_EPS, ADAM_WD, ADAM_STEP = 0.001, 0.9, 0.999, 1e-08, 0.01, 10
VMEM_LIMIT_V7X = 56 * 1024 * 1024
ATT_T = 256
POOL_T = 256


def _pcall(body, **kw):
    return pl.pallas_call(body, **kw)


def _params(sem=None):
    return pltpu.CompilerParams(dimension_semantics=sem, vmem_limit_bytes=VMEM_LIMIT_V7X)


def _tile(n, pref):
    if n <= pref:
        return n
    t = pref
    while n % t:
        t //= 2
    return t


class _Rider:
    def __init__(self, arrays, out_shape, sems, start, finish, aliases=None, steps=()):
        self.arrays, self.out_shape, self.sems = list(arrays), list(out_shape), list(sems)
        self.start, self.finish, self.aliases, self.steps = start, finish, aliases or {}, list(steps)


def _ride(name, body, riders, arrays, *, grid, in_specs, out_specs, out_shape, scratch_shapes, sem):
    n_in, n_out, n_scr = len(arrays), len(out_shape), len(scratch_shapes)
    r_arrays = [a for r in riders for a in r.arrays]
    r_outs = [o for r in riders for o in r.out_shape]
    r_sems = [s for r in riders for s in r.sems]
    n_hooks = max([len(r.steps) for r in riders], default=0)
    total = math.prod(grid)
    aliases, off_i, off_o = {}, n_in, n_out
    for r in riders:
        for a, o in r.aliases.items():
            aliases[off_i + a] = off_o + o
        off_i += len(r.arrays)
        off_o += len(r.out_shape)

    def full(*refs):
        p = 0
        groups = []
        for n in (n_in, len(r_arrays), n_out, len(r_outs), n_scr, len(r_sems)):
            groups.append(refs[p:p + n])
            p += n
        ins, rin, outs, rout, scr, rsem = groups

        def each(what):
            a = o = s = 0
            for r in riders:
                fn = what(r)
                if fn is not None:
                    fn(rin[a:a + len(r.arrays)], rout[o:o + len(r.out_shape)], rsem[s:s + len(r.sems)])
                a, o, s = a + len(r.arrays), o + len(r.out_shape), s + len(r.sems)

        if riders:
            lin = 0
            for d, g in enumerate(grid):
                lin = lin * g + pl.program_id(d)
            pl.when(lin == 0)(lambda: each(lambda r: r.start))
            for t in range(n_hooks):
                pl.when(lin == min(total - 1, ((t + 1) * total) // n_hooks))(
                    lambda t=t: each(lambda r: r.steps[t] if t < len(r.steps) else None))
        body(*ins, *outs, *scr)
        if riders:
            pl.when(lin == total - 1)(lambda: each(lambda r: r.finish))

    res = _pcall(
        full, name=name, grid=grid, in_specs=list(in_specs) + [ANY] * len(r_arrays),
        out_specs=list(out_specs) + [ANY] * len(r_outs), out_shape=list(out_shape) + r_outs,
        scratch_shapes=list(scratch_shapes) + r_sems, input_output_aliases=aliases,
        compiler_params=_params(("arbitrary",) * len(grid) if riders else sem),
    )(*arrays, *r_arrays)
    if not riders:
        return res
    main, rest, per = res[:n_out], res[n_out:], []
    for r in riders:
        per.append(rest[:len(r.out_shape)])
        rest = rest[len(r.out_shape):]
    return main, per


def _run_rider(name, rider):
    def body(*refs):
        n_a, n_o = len(rider.arrays), len(rider.out_shape)
        ins, outs, sems = refs[:n_a], refs[n_a:n_a + n_o], refs[n_a + n_o:]
        for fn in [rider.start] + rider.steps + [rider.finish]:
            fn(ins, outs, sems)

    return _pcall(body, name=name, out_shape=rider.out_shape, in_specs=[ANY] * len(rider.arrays),
                  out_specs=[ANY] * len(rider.out_shape), scratch_shapes=rider.sems,
                  input_output_aliases=rider.aliases)(*rider.arrays)


def _mm(name, pairs, *, M, N, K, ta=False, tb=False, tm=512, tn=1024, tk=1024,
        a_pro=None, b_pro=None, extras=(), outs, epi, riders=(), b_noff=0):
    tm, tn, tk = _tile(M, tm), _tile(N, tn), _tile(K, tk)
    n_i, n_j, n_k = M // tm, N // tn, K // tk
    n_p, n_e = len(pairs), len(extras)
    arrays, in_specs = [], []
    for a, _ in pairs:
        arrays.append(a)
        in_specs.append(pl.BlockSpec((tk, tm), lambda i, j, k: (k, i)) if ta
                        else pl.BlockSpec((tm, tk), lambda i, j, k: (i, k)))
    for _, b in pairs:
        arrays.append(b)
        in_specs.append(pl.BlockSpec((tn, tk), lambda i, j, k: (j + b_noff // tn, k)) if tb
                        else pl.BlockSpec((tk, tn), lambda i, j, k: (k, j + b_noff // tn)))
    for arr, kind, off in extras:
        ob = off // tn
        assert off % tn == 0
        arrays.append(arr)
        if kind == "tile":
            in_specs.append(pl.BlockSpec((tm, tn), lambda i, j, k, ob=ob: (i, j + ob)))
        else:
            in_specs.append(pl.BlockSpec((1, tn), lambda i, j, k, ob=ob: (0, j + ob)))
    out_shape, out_specs = [], []
    for o in outs:
        if o["kind"] == "tile":
            out_shape.append(jax.ShapeDtypeStruct((M, N), o["dtype"]))
            out_specs.append(pl.BlockSpec((tm, tn), lambda i, j, k: (i, j)))
        else:
            out_shape.append(jax.ShapeDtypeStruct((n_i, 1, N), F32))
            out_specs.append(pl.BlockSpec((1, 1, tn), lambda i, j, k: (i, 0, j)))
    dims = (((0 if ta else 1,), (1 if tb else 0,)), ((), ()))

    def body(*refs):
        a_refs, b_refs = refs[:n_p], refs[n_p:2 * n_p]
        e_refs = refs[2 * n_p:2 * n_p + n_e]
        o_refs = refs[2 * n_p + n_e:2 * n_p + n_e + len(outs)]
        acc_refs = refs[2 * n_p + n_e + len(outs):]

        def product(p):
            a, b = a_refs[p][...], b_refs[p][...]
            if a_pro is not None:
                a = a_pro(a)
            if b_pro is not None:
                b = b_pro(b)
            return lax.dot_general(a, b, dims, preferred_element_type=F32)

        def write(accs):
            vals = epi(accs, [e[...] for e in e_refs])
            for o, o_ref, val in zip(outs, o_refs, vals):
                if o["kind"] == "tile":
                    o_ref[...] = val.astype(o_ref.dtype)
                else:
                    o_ref[0] = val

        if n_k == 1:
            write([product(p) for p in range(n_p)])
            return
        k = pl.program_id(2)

        @pl.when(k == 0)
        def _():
            for acc in acc_refs:
                acc[...] = jnp.zeros_like(acc)

        for p in range(n_p):
            acc_refs[p][...] += product(p)

        pl.when(k == n_k - 1)(lambda: write([acc[...] for acc in acc_refs]))

    return _ride(name, body, riders, arrays, grid=(n_i, n_j, n_k), in_specs=in_specs, out_specs=out_specs,
                 out_shape=out_shape, scratch_shapes=[pltpu.VMEM((tm, tn), F32) for _ in pairs] if n_k > 1 else [],
                 sem=("parallel", "parallel", "arbitrary"))


def _tile_out(dtype):
    return {"kind": "tile", "dtype": dtype}


_COLSUM = {"kind": "colsum"}


def _colsum(v):
    return jnp.sum(v, axis=0, keepdims=True)


def _norm_mod(name, x, norm_w, scale, shift):
    S, D = x.shape
    tr = _tile(S, 256)

    def body(x_ref, nw_ref, sc_ref, sh_ref, h_ref):
        xv = x_ref[...]
        r = lax.rsqrt(jnp.mean(xv * xv, axis=-1, keepdims=True) + EPS)
        h_ref[...] = ((xv * r * nw_ref[...]) * (1.0 + sc_ref[...]) + sh_ref[...]).astype(BF16)

    row = pl.BlockSpec((1, D), lambda i: (0, 0))
    til = pl.BlockSpec((tr, D), lambda i: (i, 0))
    return _pcall(body, name=name, grid=(S // tr,), in_specs=[til, row, row, row], out_specs=til,
                  out_shape=jax.ShapeDtypeStruct((S, D), BF16), compiler_params=_params(("parallel",)))(
                      x, norm_w, scale, shift)


def _norm_mod_bwd(name, dh, x, dres, norm_w, scale, gate_o=None):
    S, D = x.shape
    tr = _tile(S, 256)
    n_r = S // tr
    with_gate = gate_o is not None
    dh = list(dh) if isinstance(dh, (list, tuple)) else [dh]
    n_dh = len(dh)

    def body(*refs):
        dh_refs, refs = refs[:n_dh], refs[n_dh:]
        if with_gate:
            x_ref, dres_ref, nw_ref, sc_ref, o_ref, g_ref, dx_ref, p1, p2, p3, do_ref, p4 = refs
        else:
            x_ref, dres_ref, nw_ref, sc_ref, dx_ref, p1, p2, p3 = refs
        dhv = dh_refs[0][...] if n_dh == 1 else jnp.concatenate([r[...] for r in dh_refs], axis=1)
        xv, nw = x_ref[...], nw_ref[...]
        r = lax.rsqrt(jnp.mean(xv * xv, axis=-1, keepdims=True) + EPS)
        xh = xv * r
        p1[0] = _colsum(dhv)
        p2[0] = _colsum(dhv * (xh * nw))
        dn = dhv * (1.0 + sc_ref[...])
        p3[0] = _colsum(dn * xh)
        dxh = dn * nw
        dx = dres_ref[...] + r * (dxh - xh * jnp.mean(dxh * xh, axis=-1, keepdims=True))
        dx_ref[...] = dx
        if with_gate:
            do_ref[...] = (dx * g_ref[...]).astype(BF16)
            p4[0] = _colsum(dx * o_ref[...].astype(F32))

    row = pl.BlockSpec((1, D), lambda i: (0, 0))
    til = pl.BlockSpec((tr, D), lambda i: (i, 0))
    part = pl.BlockSpec((1, 1, D), lambda i: (i, 0, 0))
    part_shape = jax.ShapeDtypeStruct((n_r, 1, D), F32)
    in_specs = [pl.BlockSpec((tr, D // n_dh), lambda i: (i, 0))] * n_dh + [til, til, row, row]
    arrays = dh + [x, dres, norm_w, scale]
    out_specs = [til, part, part, part]
    out_shape = [jax.ShapeDtypeStruct((S, D), F32), part_shape, part_shape, part_shape]
    if with_gate:
        in_specs += [til, row]
        arrays += list(gate_o)
        out_specs += [til, part]
        out_shape += [jax.ShapeDtypeStruct((S, D), BF16), part_shape]
    return _pcall(body, name=name, grid=(n_r,), in_specs=in_specs, out_specs=out_specs, out_shape=out_shape,
                  compiler_params=_params(("parallel",)))(*arrays)


def _pool_w_specs(rows, cg):
    return [pl.BlockSpec((rows, cg), lambda g, j=j: (N_GROUPS * j + g, 0)) for j in range(N_CHIPS)]


def _pool_fwd(proj, wp_full, pool_scale, S, PW):
    cg = PW // N_GROUPS
    rows = cg // N_CHIPS
    T = _tile(S, POOL_T)
    n_t = S // T

    def body(u_ref, w0, w1, w2, w3, ps_ref, pooled_ref, pa_ref):
        g = pl.program_id(0)
        win = jnp.left_shift(2, g)
        w = jnp.concatenate([w0[...], w1[...], w2[...], w3[...]], axis=0)
        t_i = lax.broadcasted_iota(jnp.int32, (T, T), 0)
        j_i = lax.broadcasted_iota(jnp.int32, (T, T), 1)
        b_cur = ((j_i <= t_i) & (j_i > t_i - win)).astype(BF16)
        b_prev = (j_i - T > t_i - win).astype(BF16)
        row = lax.broadcasted_iota(jnp.int32, (T, 1), 0)
        for r in range(n_t):
            cur = u_ref[r * T:(r + 1) * T, :]
            ws = jnp.dot(b_cur, cur, preferred_element_type=F32)
            if r > 0:
                ws += jnp.dot(b_prev, u_ref[(r - 1) * T:r * T, :], preferred_element_type=F32)
            count = jnp.minimum(row + (r * T + 1), win).astype(F32)
            pooled = (ws / count - cur.astype(F32)).astype(BF16)
            pooled_ref[r * T:(r + 1) * T, :] = pooled
            mixed = jnp.dot(pooled, w, preferred_element_type=F32)
            pa_ref[r * T:(r + 1) * T, :] = (mixed * ps_ref[...]).astype(BF16)

    col = pl.BlockSpec((S, cg), lambda g: (0, g))
    return _pcall(
        body, name="pool_fwd", grid=(N_GROUPS,),
        in_specs=[col] + _pool_w_specs(rows, cg) + [pl.BlockSpec((1, cg), lambda g: (0, g))],
        out_specs=[col, col],
        out_shape=[jax.ShapeDtypeStruct((S, PW), BF16), jax.ShapeDtypeStruct((S, PW), BF16)],
        compiler_params=_params(("parallel",)),
    )(proj, wp_full, wp_full, wp_full, wp_full, pool_scale)


def _pool_bwd(dpa, pooled, wp_full, pool_scale, S, PW):
    cg = PW // N_GROUPS
    rows = cg // N_CHIPS
    T = _tile(S, POOL_T)
    n_t = S // T

    def body(dpa_ref, pooled_ref, w0, w1, w2, w3, ps_ref, du_ref, gw_ref, gs_ref, dp_s, dpc_s, dmx_s):
        g = pl.program_id(0)
        win = jnp.left_shift(2, g)
        w = jnp.concatenate([w0[...], w1[...], w2[...], w3[...]], axis=0)
        row = lax.broadcasted_iota(jnp.int32, (T, 1), 0)
        gs = jnp.zeros((1, cg), F32)
        for r in range(n_t):
            sl = slice(r * T, (r + 1) * T)
            mixed = jnp.dot(pooled_ref[sl, :], w, preferred_element_type=F32)
            dpa_t = dpa_ref[sl, :]
            gs += _colsum(dpa_t * mixed)
            dmx = (dpa_t * ps_ref[...]).astype(BF16)
            dmx_s[sl, :] = dmx
            dpo = lax.dot_general(dmx, w, (((1,), (1,)), ((), ())), preferred_element_type=F32)
            dp_s[sl, :] = dpo
            count = jnp.minimum(row + (r * T + 1), win).astype(F32)
            dpc_s[sl, :] = (dpo / count).astype(BF16)
        gs_ref[...] = gs
        gw = lax.dot_general(pooled_ref[...], dmx_s[...], (((0,), (0,)), ((), ())), preferred_element_type=F32)
        for j in range(N_CHIPS):
            gw_ref[j, 0] = gw[j * rows:(j + 1) * rows, :].astype(BF16)
        j_i = lax.broadcasted_iota(jnp.int32, (T, T), 0)
        t_i = lax.broadcasted_iota(jnp.int32, (T, T), 1)
        b_cur = ((t_i >= j_i) & (t_i < j_i + win)).astype(BF16)
        b_next = (t_i + T < j_i + win).astype(BF16)
        for r in range(n_t):
            sl = slice(r * T, (r + 1) * T)
            acc = jnp.dot(b_cur, dpc_s[sl, :], preferred_element_type=F32)
            if r + 1 < n_t:
                acc += jnp.dot(b_next, dpc_s[(r + 1) * T:(r + 2) * T, :], preferred_element_type=F32)
            du_ref[sl, :] = (acc - dp_s[sl, :]).astype(BF16)

    col = pl.BlockSpec((S, cg), lambda g: (0, g))
    return _pcall(
        body, name="pool_bwd", grid=(N_GROUPS,),
        in_specs=[col, col] + _pool_w_specs(rows, cg) + [pl.BlockSpec((1, cg), lambda g: (0, g))],
        out_specs=[col, pl.BlockSpec((N_CHIPS, 1, rows, cg), lambda g: (0, g, 0, 0)),
                   pl.BlockSpec((1, cg), lambda g: (0, g))],
        out_shape=[jax.ShapeDtypeStruct((S, PW), BF16),
                   jax.ShapeDtypeStruct((N_CHIPS, N_GROUPS, rows, cg), BF16),
                   jax.ShapeDtypeStruct((1, PW), F32)],
        scratch_shapes=[pltpu.VMEM((S, cg), F32), pltpu.VMEM((S, cg), BF16), pltpu.VMEM((S, cg), BF16)],
        compiler_params=_params(("parallel",)),
    )(dpa, pooled, wp_full, wp_full, wp_full, wp_full, pool_scale)


_NT = (((1,), (1,)), ((), ()))
_TN = (((0,), (0,)), ((), ()))


def _split_dot(v, tri):
    hi = v.astype(BF16)
    lo = (v - hi.astype(F32)).astype(BF16)
    return jnp.dot(hi, tri, preferred_element_type=F32) + jnp.dot(lo, tri, preferred_element_type=F32)


def _sb_scores(q_i, k_j, tri_l, masked):
    tq, tk = q_i.shape[0], k_j.shape[0]
    s = lax.dot_general(q_i, k_j, _NT, preferred_element_type=F32) * (1.0 / math.sqrt(HEAD_DIM))
    lp = jnp.log(1.0 + jnp.exp(-jnp.abs(s)))
    l = -jnp.maximum(s, 0.0) - lp
    lb = l + s
    mask = None
    if masked:
        mask = lax.broadcasted_iota(jnp.int32, (tq, tk), 0) > lax.broadcasted_iota(jnp.int32, (tq, tk), 1)
        l = jnp.where(mask, l, 0.0)
    return l, lb, lb + _split_dot(l, tri_l), mask


def _sb_weights(t, carry_l, mask):
    a = jnp.exp(t + carry_l)
    return a if mask is None else jnp.where(mask, a, 0.0)


def _rowsum(v):
    return jnp.sum(v, axis=1, keepdims=True)


def _qk_norm(x_ref, w_ref):
    xv = x_ref[...].astype(F32)
    r = lax.rsqrt(jnp.mean(xv * xv, axis=-1, keepdims=True) + EPS)
    return xv * r, r


def _attn_fwd(proj, q_norm_w, k_norm_w, S, H, q_off, riders=()):
    t = _tile(S, ATT_T)
    n_q = S // t

    def body(q_ref, k_ref, v_ref, qw_ref, kw_ref, att_ref, attf_ref, qn_s, kn_s):
        qh, _ = _qk_norm(q_ref, qw_ref)
        qn_s[...] = (qh * qw_ref[...]).astype(BF16)
        kh, _ = _qk_norm(k_ref, kw_ref)
        kn_s[...] = (kh * kw_ref[...]).astype(BF16)
        tri_l = (lax.broadcasted_iota(jnp.int32, (t, t), 0) > lax.broadcasted_iota(jnp.int32, (t, t), 1)).astype(BF16)

        def rows(j):
            return pl.ds(pl.multiple_of(j * t, t), t)

        def q_step(i, _):
            q_i = qn_s[rows(i), :]

            def av(a, j):
                return jnp.dot(a.astype(BF16), v_ref[rows(j), :], preferred_element_type=F32)

            l, _, tt, mask = _sb_scores(q_i, kn_s[rows(i), :], tri_l, True)
            acc = av(_sb_weights(tt, 0.0, mask), i)
            carry = _rowsum(l)

            def single(_, c):
                carry, acc = c
                l, _, tt, _ = _sb_scores(q_i, kn_s[rows(i - 1), :], tri_l, False)
                return carry + _rowsum(l), acc + av(_sb_weights(tt, carry, None), i - 1)

            carry, acc = lax.fori_loop(0, i % 2, single, (carry, acc))
            top = i - 1 - i % 2

            def pair(p, c):
                carry, acc = c
                j0 = top - 2 * p
                l0, _, t0, _ = _sb_scores(q_i, kn_s[rows(j0), :], tri_l, False)
                l1, _, t1, _ = _sb_scores(q_i, kn_s[rows(j0 - 1), :], tri_l, False)
                mid = carry + _rowsum(l0)
                acc = acc + av(_sb_weights(t0, carry, None), j0) + av(_sb_weights(t1, mid, None), j0 - 1)
                return mid + _rowsum(l1), acc

            _, acc = lax.fori_loop(0, i // 2, pair, (carry, acc))
            att_ref[rows(i), :] = acc.astype(BF16)
            attf_ref[rows(i), :] = acc
            return 0

        lax.fori_loop(0, n_q, q_step, 0)

    def col(off):
        return pl.BlockSpec((S, HEAD_DIM), lambda h, off=off: (0, off + h))

    wspec = pl.BlockSpec((1, HEAD_DIM), lambda h: (0, 0))
    return _ride(
        "attn_fwd", body, riders, [proj, proj, proj, q_norm_w, k_norm_w], grid=(H,),
        in_specs=[col(q_off), col(q_off + H), col(q_off + 2 * H), wspec, wspec],
        out_specs=[col(0), col(0)],
        out_shape=[jax.ShapeDtypeStruct((S, H * HEAD_DIM), BF16), jax.ShapeDtypeStruct((S, H * HEAD_DIM), F32)],
        scratch_shapes=[pltpu.VMEM((S, HEAD_DIM), BF16), pltpu.VMEM((S, HEAD_DIM), BF16)],
        sem=("parallel",))


def _attn_bwd(proj, datt, attf, q_norm_w, k_norm_w, S, H, q_off, riders=()):
    t = _tile(S, ATT_T)
    n_q = S // t
    scale = 1.0 / math.sqrt(HEAD_DIM)

    def body(q_ref, k_ref, v_ref, do_ref, o_ref, qw_ref, kw_ref, dq_ref, dk_ref, dv_ref, gq_ref, gk_ref,
             qn_s, kn_s, dk_s, dv_s, gq_s):
        qw, kw = qw_ref[...], kw_ref[...]
        qh, _ = _qk_norm(q_ref, qw_ref)
        qn_s[...] = (qh * qw).astype(BF16)
        kh, _ = _qk_norm(k_ref, kw_ref)
        kn_s[...] = (kh * kw).astype(BF16)
        dk_s[...] = jnp.zeros_like(dk_s)
        dv_s[...] = jnp.zeros_like(dv_s)
        gq_s[...] = jnp.zeros_like(gq_s)
        r_i = lax.broadcasted_iota(jnp.int32, (t, t), 0)
        c_i = lax.broadcasted_iota(jnp.int32, (t, t), 1)
        tri_l = (r_i > c_i).astype(BF16)
        tri_e = (r_i >= c_i).astype(BF16)

        def rows(j):
            return pl.ds(pl.multiple_of(j * t, t), t)

        def q_step(i, _):
            q_i = qn_s[rows(i), :]
            do_i = do_ref[rows(i), :]
            d_i = _rowsum(do_i.astype(F32) * o_ref[rows(i), :])

            def scores(j, masked):
                k_j = kn_s[rows(j), :]
                l, lb, tt, mask = _sb_scores(q_i, k_j, tri_l, masked)
                da = lax.dot_general(do_i, v_ref[rows(j), :], _NT, preferred_element_type=F32)
                return k_j, l, lb, tt, mask, da

            def grads(j, sc, carry_l, carry_e, dq_acc):
                k_j, l, lb, tt, mask, da = sc
                a_bf = _sb_weights(tt, carry_l, mask).astype(BF16)
                e = da * a_bf.astype(F32)
                p = d_i - (_split_dot(e, tri_e) + carry_e)
                sig = jnp.exp(lb)
                dz = e * (1.0 - sig) - p * sig
                if mask is not None:
                    dz = jnp.where(mask, dz, 0.0)
                dz = (dz * scale).astype(BF16)
                dk_s[rows(j), :] += lax.dot_general(dz, q_i, _TN, preferred_element_type=F32)
                dv_s[rows(j), :] += lax.dot_general(a_bf, do_i, _TN, preferred_element_type=F32)
                return (carry_l + _rowsum(l), carry_e + _rowsum(e),
                        dq_acc + jnp.dot(dz, k_j, preferred_element_type=F32))

            c = grads(i, scores(i, True), 0.0, 0.0, jnp.zeros((t, HEAD_DIM), F32))
            c = lax.fori_loop(0, i % 2, lambda _, c: grads(i - 1, scores(i - 1, False), *c), c)
            top = i - 1 - i % 2

            def pair(p, c):
                j0 = top - 2 * p
                s0, s1 = scores(j0, False), scores(j0 - 1, False)
                return grads(j0 - 1, s1, *grads(j0, s0, *c))

            _, _, dqn = lax.fori_loop(0, i // 2, pair, c)
            qv = q_ref[rows(i), :].astype(F32)
            r = lax.rsqrt(jnp.mean(qv * qv, axis=-1, keepdims=True) + EPS)
            xh = qv * r
            gq_s[...] += _colsum(dqn * xh)
            dxh = dqn * qw
            dq_ref[rows(i), :] = (r * (dxh - xh * jnp.mean(dxh * xh, axis=-1, keepdims=True))).astype(BF16)
            return 0

        lax.fori_loop(0, n_q, q_step, 0)
        gq_ref[0] = gq_s[...]
        kh, rk = _qk_norm(k_ref, kw_ref)
        dkn = dk_s[...]
        gk_ref[0] = _colsum(dkn * kh)
        dxh = dkn * kw
        dk_ref[...] = (rk * (dxh - kh * jnp.mean(dxh * kh, axis=-1, keepdims=True))).astype(BF16)
        dv_ref[...] = dv_s[...].astype(BF16)

    def col(off):
        return pl.BlockSpec((S, HEAD_DIM), lambda h, off=off: (0, off + h))

    wspec = pl.BlockSpec((1, HEAD_DIM), lambda h: (0, 0))
    gspec = pl.BlockSpec((1, 1, HEAD_DIM), lambda h: (h, 0, 0))
    act = jax.ShapeDtypeStruct((S, H * HEAD_DIM), BF16)
    gsh = jax.ShapeDtypeStruct((H, 1, HEAD_DIM), F32)
    return _ride(
        "attn_bwd", body, riders, [proj, proj, proj, datt, attf, q_norm_w, k_norm_w], grid=(H,),
        in_specs=[col(q_off), col(q_off + H), col(q_off + 2 * H), col(0), col(0), wspec, wspec],
        out_specs=[col(0), col(0), col(0), gspec, gspec],
        out_shape=[act, act, act, gsh, gsh],
        scratch_shapes=[pltpu.VMEM((S, HEAD_DIM), BF16), pltpu.VMEM((S, HEAD_DIM), BF16),
                        pltpu.VMEM((S, HEAD_DIM), F32), pltpu.VMEM((S, HEAD_DIM), F32),
                        pltpu.VMEM((1, HEAD_DIM), F32)],
        sem=("parallel",))


def _place():
    x, y, c = lax.axis_index("x"), lax.axis_index("y"), lax.axis_index("c")
    chips = [(1 - x, y), (x, 1 - y), (1 - x, 1 - y)]
    return x, y, c, chips


def _dev_allgather(name, v):
    m_per, n = v.shape

    def body(x_ref, out_ref, send_sems, recv_sems, local_sem):
        x, y, c, chips = _place()
        me, sibling = (x, y, c), (x, y, 1 - c)

        def rows(px, py, pc):
            return out_ref.at[pl.ds((4 * px + 2 * py + pc) * m_per, m_per), :]

        def copy(k, block, to, src=None):
            return pltpu.make_async_remote_copy(
                src_ref=rows(*block) if src is None else src, dst_ref=rows(*block),
                send_sem=send_sems.at[k], recv_sem=recv_sems.at[k], device_id=to, device_id_type=MESH)

        mine = pltpu.make_async_copy(x_ref, rows(*me), local_sem)
        mine.start()
        first = [copy(0, me, sibling, src=x_ref)]
        first += [copy(1 + j, me, (*chip, c), src=x_ref) for j, chip in enumerate(chips)]
        for cp in first:
            cp.start()
        passed = [copy(4 + j, (*chip, c), sibling) for j, chip in enumerate(chips)]
        for j, chip in enumerate(chips):
            copy(1 + j, (*chip, c), me).wait_recv()
            passed[j].start()
        copy(0, sibling, me).wait_recv()
        for j, chip in enumerate(chips):
            copy(4 + j, (*chip, 1 - c), me).wait_recv()
        for cp in first + passed:
            cp.wait_send()
        mine.wait()

    return _pcall(
        body, name=name, out_shape=jax.ShapeDtypeStruct((N_DEV * m_per, n), v.dtype),
        in_specs=[pl.BlockSpec(memory_space=pltpu.VMEM)], out_specs=pl.BlockSpec(memory_space=pltpu.VMEM),
        scratch_shapes=[pltpu.SemaphoreType.DMA((7,)), pltpu.SemaphoreType.DMA((7,)), pltpu.SemaphoreType.DMA],
        compiler_params=pltpu.CompilerParams(vmem_limit_bytes=VMEM_LIMIT_V7X),
    )(v)


class _W:
    def __init__(self, name, kind, R, C):
        self.name, self.kind, self.R, self.C = name, kind, R, C

    @property
    def shard_shape(self):
        return (self.R, self.C // N_CHIPS) if self.kind == "col" else (self.R // N_CHIPS, self.C)

    @property
    def half_rows(self):
        return self.shard_shape[0] // 2

    def shard_half(self, ref, half):
        return ref.at[pl.ds(half * self.half_rows, self.half_rows), :]

    def region(self, full_ref, chip, half):
        hr = self.half_rows
        if self.kind == "col":
            cw = self.C // N_CHIPS
            return full_ref.at[pl.ds(half * hr, hr), pl.ds(chip * cw, cw)]
        return full_ref.at[pl.ds(chip * (2 * hr) + half * hr, hr), :]

    def region_both(self, full_ref, chip):
        hr = self.half_rows
        if self.kind == "col":
            cw = self.C // N_CHIPS
            return full_ref.at[:, pl.ds(chip * cw, cw)]
        return full_ref.at[pl.ds(chip * (2 * hr), 2 * hr), :]


def _ag_rider(ws, fulls, n_ch=4, chunks=None):
    n_w = len(ws)
    lo, hi = chunks or (0, n_ch)
    per = 6

    def parts(full, sems):
        send_sems, recv_sems = sems
        x, y, c, _ = _place()
        xn, yn, dg = (1 - x, y), (x, 1 - y), (1 - x, 1 - y)
        via = (x + (1 - c) * (1 - 2 * x), y + c * (1 - 2 * y))
        to = (x + c * (1 - 2 * x), y + (1 - c) * (1 - 2 * y))

        def reg(i, chip, half, t):
            nr = ws[i].half_rows // n_ch
            return ws[i].region(full[i], 2 * chip[0] + chip[1], half).at[pl.ds(t * nr, nr), :]

        def copy(r, i, t, k, dev):
            s = (i * (hi - lo) + t - lo) * per + k
            return pltpu.make_async_remote_copy(src_ref=r, dst_ref=r, send_sem=send_sems.at[s],
                                                recv_sem=recv_sems.at[s], device_id=dev, device_id_type=MESH)

        def direct(i, t, k):
            return copy(reg(i, (x, y), c, t), i, t, k, (*(via, to)[k], c))

        def direct_in(i, t, k):
            return copy(reg(i, (via, to)[k], c, t), i, t, k, (*(via, to)[k], c))

        def relay(i, t):
            return copy(reg(i, via, c, t), i, t, 2, (*to, c))

        def relay_in(i, t):
            return copy(reg(i, dg, c, t), i, t, 2, (*to, c))

        def hand(i, t, k, half):
            return copy(reg(i, (xn, yn, dg)[k], half, t), i, t, 3 + k, (x, y, 1 - c))

        return c, direct, direct_in, relay, relay_in, hand

    def start(_, full, sems):
        _, direct, _, _, _, _ = parts(full, sems)
        for t in range(lo, hi):
            for i in range(n_w):
                direct(i, t, 0).start()
                direct(i, t, 1).start()

    def arrived(t):
        def step(_, full, sems):
            c, _, direct_in, relay, relay_in, hand = parts(full, sems)
            for i in range(n_w):
                direct_in(i, t, 0).wait_recv()
                direct_in(i, t, 1).wait_recv()
                relay(i, t).start()
                hand(i, t, 0, c).start()
                hand(i, t, 1, c).start()
                if t > lo:
                    relay_in(i, t - 1).wait_recv()
                    hand(i, t - 1, 2, c).start()
        return step

    def finish(_, full, sems):
        c, direct, _, relay, relay_in, hand = parts(full, sems)
        for i in range(n_w):
            relay_in(i, hi - 1).wait_recv()
            hand(i, hi - 1, 2, c).start()
        for i in range(n_w):
            for t in range(lo, hi):
                for k in range(3):
                    hand(i, t, k, 1 - c).wait_recv()
        for i in range(n_w):
            for t in range(lo, hi):
                direct(i, t, 0).wait_send()
                direct(i, t, 1).wait_send()
                relay(i, t).wait_send()
                for k in range(3):
                    hand(i, t, k, c).wait_send()

    n_sem = per * (hi - lo) * n_w
    return _Rider(fulls, [jax.ShapeDtypeStruct((w.R, w.C), BF16) for w in ws],
                  [pltpu.SemaphoreType.DMA((n_sem,)), pltpu.SemaphoreType.DMA((n_sem,))], start, finish,
                  steps=[arrived(t) for t in range(lo, hi)], aliases={i: i for i in range(n_w)})


def _cast_into_full(w, a32, chip_arr):
    sr, sc = w.shard_shape
    tr, tc = _tile(sr, 512), _tile(sc, 2048)
    n_r, n_c = sr // tr, sc // tc
    if w.kind == "col":
        out_spec = pl.BlockSpec((tr, tc), lambda i, j, chip: (i, chip[0] * n_c + j))
    else:
        out_spec = pl.BlockSpec((tr, tc), lambda i, j, chip: (chip[0] * n_r + i, j))

    def body(chip_ref, a_ref, o_ref):
        o_ref[...] = a_ref[...].astype(BF16)

    return _pcall(
        body, name="cast_" + w.name, out_shape=jax.ShapeDtypeStruct((w.R, w.C), BF16),
        grid_spec=pltpu.PrefetchScalarGridSpec(
            num_scalar_prefetch=1, grid=(n_r, n_c),
            in_specs=[pl.BlockSpec((tr, tc), lambda i, j, chip: (i, j))], out_specs=out_spec),
        compiler_params=_params(("parallel", "parallel")),
    )(chip_arr, a32)


def _half_view(w, g):
    return g if w.kind == "col" else g.reshape(N_CHIPS, w.R // N_CHIPS, w.C)


def _px_rider(ws, grads):
    n_w = len(ws)

    def copies(g, got, sems):
        send_sems, recv_sems = sems
        x, y, c, _ = _place()

        def half_all(w, ref, half):
            hr = w.half_rows
            if w.kind == "col":
                return ref.at[pl.ds(half * hr, hr), :]
            return ref.at[:, pl.ds(half * hr, hr), :]

        return [pltpu.make_async_remote_copy(
            src_ref=half_all(w, g[i], 1 - c), dst_ref=got[i], send_sem=send_sems.at[i], recv_sem=recv_sems.at[i],
            device_id=(x, y, 1 - c), device_id_type=MESH) for i, w in enumerate(ws)]

    def start(g, got, sems):
        for cp in copies(g, got, sems):
            cp.start()

    def finish(g, got, sems):
        for cp in copies(g, got, sems):
            cp.wait_recv()
            cp.wait_send()

    def got_shape(w):
        hr = w.half_rows
        return (hr, w.C) if w.kind == "col" else (N_CHIPS, hr, w.C)

    return _Rider([_half_view(w, g) for w, g in zip(ws, grads)],
                  [jax.ShapeDtypeStruct(got_shape(w), BF16) for w in ws],
                  [pltpu.SemaphoreType.DMA((n_w,)), pltpu.SemaphoreType.DMA((n_w,))], start, finish)


def _pair_sum(w, g, got, c_arr):
    hr = w.half_rows
    if w.kind == "col":
        tr, tc = _tile(hr, 512), _tile(w.C, 2048)
        n_r = hr // tr
        grid = (n_r, w.C // tc)
        g_spec = pl.BlockSpec((tr, tc), lambda i, j, c: (c[0] * n_r + i, j))
        o_spec = pl.BlockSpec((tr, tc), lambda i, j, c: (i, j))
    else:
        tr = _tile(hr, 512)
        n_r = hr // tr
        grid = (N_CHIPS, n_r)
        g_spec = pl.BlockSpec((1, tr, w.C), lambda s, i, c: (s, c[0] * n_r + i, 0))
        o_spec = pl.BlockSpec((1, tr, w.C), lambda s, i, c: (s, i, 0))

    def body(c_ref, g_ref, got_ref, out_ref):
        out_ref[...] = (g_ref[...].astype(F32) + got_ref[...].astype(F32)).astype(BF16)

    return _pcall(
        body, name="grad_pair_sum_" + w.name, out_shape=jax.ShapeDtypeStruct(got.shape, BF16),
        grid_spec=pltpu.PrefetchScalarGridSpec(num_scalar_prefetch=1, grid=grid, in_specs=[g_spec, o_spec],
                                               out_specs=o_spec),
        compiler_params=_params(("parallel", "parallel")),
    )(c_arr, _half_view(w, g), got)


def _cx_rider(ws, sums, part=(0, 1), q_in=None):
    n_w = len(ws)

    def parts(p, q, sems):
        send_sems, recv_sems = sems
        x, y, c, chips = _place()
        my_chip = 2 * x + y

        def rows(w, ref):
            nr = w.half_rows // part[1]
            return ref.at[pl.ds(part[0] * nr, nr), :]

        def piece(w, ref, chip):
            if w.kind == "col":
                cw = w.C // N_CHIPS
                return rows(w, ref.at[:, pl.ds(chip * cw, cw)])
            return rows(w, ref.at[chip])

        def copy(i, k, recv=False):
            chip = chips[k]
            to_chip = 2 * chip[0] + chip[1]
            return pltpu.make_async_remote_copy(
                src_ref=piece(ws[i], p[i], to_chip), dst_ref=rows(ws[i], q[i].at[to_chip if recv else my_chip]),
                send_sem=send_sems.at[3 * i + k], recv_sem=recv_sems.at[3 * i + k],
                device_id=(*chip, c), device_id_type=MESH)

        return copy

    both = [(i, k) for i in range(n_w) for k in range(N_CHIPS - 1)]

    def start(p, q, sems):
        copy = parts(p, q, sems)
        for i, k in both:
            copy(i, k).start()

    def finish(p, q, sems):
        copy = parts(p, q, sems)
        for i, k in both:
            copy(i, k, recv=True).wait_recv()
        for i, k in both:
            copy(i, k).wait_send()

    return _Rider(list(sums) + list(q_in or []),
                  [jax.ShapeDtypeStruct((N_CHIPS, w.half_rows, w.shard_shape[1]), BF16) for w in ws],
                  [pltpu.SemaphoreType.DMA((3 * n_w,)), pltpu.SemaphoreType.DMA((3 * n_w,))], start, finish,
                  aliases={n_w + i: i for i in range(n_w)} if q_in else None)


def _chip_sum(w, p, q, cc_arr):
    hr, cols = w.half_rows, w.shard_shape[1]
    tr, tc = _tile(hr, 512), _tile(cols, 2048)
    n_r, n_c = hr // tr, cols // tc

    def body(cc_ref, own, q1, q2, q3, out_ref):
        own_v = own[...] if w.kind == "col" else own[0]
        out_ref[...] = ((own_v.astype(F32) + q1[0].astype(F32)) + q2[0].astype(F32)) + q3[0].astype(F32)

    if w.kind == "col":
        own_spec = pl.BlockSpec((tr, tc), lambda i, j, cc: (i, cc[1] * n_c + j))
    else:
        own_spec = pl.BlockSpec((1, tr, tc), lambda i, j, cc: (cc[1], i, j))
    q_specs = [pl.BlockSpec((1, tr, tc), lambda i, j, cc, s=s: ((cc[1] + s) % N_CHIPS, i, j)) for s in (1, 2, 3)]
    return _pcall(
        body, name="grad_chip_sum_" + w.name, out_shape=jax.ShapeDtypeStruct(w.shard_shape, F32),
        grid_spec=pltpu.PrefetchScalarGridSpec(
            num_scalar_prefetch=1, grid=(n_r, n_c), in_specs=[own_spec] + q_specs,
            out_specs=pl.BlockSpec((tr, tc), lambda i, j, cc: (cc[0] * n_r + i, j))),
        compiler_params=_params(("parallel", "parallel")),
    )(cc_arr, p, q, q, q)


_SEM = pl.BlockSpec(memory_space=pltpu.SEMAPHORE)
_HBM = pl.BlockSpec(memory_space=pltpu.HBM)


def _cx_split_copies(ws, p, land, send_sems, recv_sems):
    x, y, c, chips = _place()
    my_chip = 2 * x + y
    pairs = []
    for i, w in enumerate(ws):
        for k, chip in enumerate(chips):
            to_chip = 2 * chip[0] + chip[1]
            src = p[i].at[:, pl.ds(to_chip * (w.C // N_CHIPS), w.C // N_CHIPS)] if w.kind == "col" else p[i].at[to_chip]
            kw = dict(send_sem=send_sems.at[3 * i + k], recv_sem=recv_sems.at[3 * i + k], device_id=(*chip, c),
                      device_id_type=MESH)
            pairs.append((pltpu.make_async_remote_copy(src_ref=src, dst_ref=land[i].at[my_chip], **kw),
                          pltpu.make_async_remote_copy(src_ref=src, dst_ref=land[i].at[to_chip], **kw)))
    return pairs


def _cx_start(ws, sums):
    n_w = len(ws)
    lands = [lax.empty((N_CHIPS, w.half_rows, w.shard_shape[1]), BF16) for w in ws]

    def body(*refs):
        p, land = refs[:n_w], refs[n_w:2 * n_w]
        for out, _ in _cx_split_copies(ws, p, land, refs[2 * n_w], refs[2 * n_w + 1]):
            out.start()
        refs[-1][...] = jnp.zeros_like(refs[-1])

    arrays = [pltpu.with_memory_space_constraint(a, pltpu.HBM) for a in list(sums) + lands]
    res = _pcall(
        body, name="grad_last_exchange_start",
        out_shape=(pltpu.SemaphoreType.DMA((3 * n_w,)), pltpu.SemaphoreType.DMA((3 * n_w,)),
                   *[pltpu.HBM(a.shape, a.dtype) for a in arrays], jax.ShapeDtypeStruct((8, 128), F32)),
        in_specs=[_HBM] * (2 * n_w),
        out_specs=(_SEM, _SEM, *[_HBM] * (2 * n_w), pl.BlockSpec(memory_space=pltpu.VMEM)),
        input_output_aliases={i: 2 + i for i in range(2 * n_w)},
        compiler_params=pltpu.CompilerParams(has_side_effects=pltpu.SideEffectType.DATAFLOW_SIDE_EFFECTING),
    )(*arrays)
    return res[0], res[1], list(res[2:2 + n_w]), list(res[2 + n_w:2 + 2 * n_w]), res[-1]


def _cx_wait(ws, send_sems, recv_sems, sums, lands, after):
    n_w = len(ws)

    def body(*refs):
        p, land = refs[:n_w], refs[n_w:2 * n_w]
        for _, cp in _cx_split_copies(ws, p, land, refs[2 * n_w], refs[2 * n_w + 1]):
            cp.wait_send()
            cp.wait_recv()

    res = _pcall(
        body, name="grad_last_exchange_wait",
        out_shape=[pltpu.HBM(a.shape, a.dtype) for a in list(sums) + list(lands)],
        in_specs=[_HBM] * (2 * n_w) + [_SEM, _SEM] + [ANY] * len(after), out_specs=[_HBM] * (2 * n_w),
        input_output_aliases={i: i for i in range(2 * n_w)},
        compiler_params=pltpu.CompilerParams(has_side_effects=pltpu.SideEffectType.DATAFLOW_SIDE_EFFECTING),
    )(*sums, *lands, send_sems, recv_sems, *after)
    return list(res[:n_w]), list(res[n_w:])


def _sf_rider(ws, grads):
    n_w = len(ws)

    def copy(g, sems, i, half):
        send_sems, recv_sems = sems
        x, y, c, _ = _place()
        h = c if half == "mine" else 1 - c
        reg = ws[i].shard_half(g[i], h)
        return pltpu.make_async_remote_copy(src_ref=reg, dst_ref=reg, send_sem=send_sems.at[i], recv_sem=recv_sems.at[i],
                                            device_id=(x, y, 1 - c), device_id_type=MESH)

    def start(_, g, sems):
        for i in range(n_w):
            copy(g, sems, i, "mine").start()

    def finish(_, g, sems):
        for i in range(n_w):
            copy(g, sems, i, "other").wait_recv()
            copy(g, sems, i, "mine").wait_send()

    return _Rider(grads, [jax.ShapeDtypeStruct(w.shard_shape, F32) for w in ws],
                  [pltpu.SemaphoreType.DMA((n_w,)), pltpu.SemaphoreType.DMA((n_w,))], start, finish,
                  aliases={i: i for i in range(n_w)})


def _adamw_math(w, g, m, v):
    m = ADAM_B1 * m + (1.0 - ADAM_B1) * g
    v = ADAM_B2 * v + (1.0 - ADAM_B2) * (g * g)
    m_hat = m / (1.0 - ADAM_B1 ** ADAM_STEP)
    v_hat = v / (1.0 - ADAM_B2 ** ADAM_STEP)
    delta = -ADAM_LR * (m_hat / (jnp.sqrt(v_hat) + ADAM_EPS) + ADAM_WD * w)
    return delta, m, v


def _adamw(name, w, g, m, v, after=None):
    R, C = w.shape
    tr, tc = _tile(R, 256), _tile(C, 2048)
    behind = [] if after is None else [after]

    def body(w_ref, g_ref, m_ref, v_ref, *rest):
        g_out, d_out, m_out, v_out = rest[len(behind):]
        g = g_ref[...]
        g_out[...] = g
        d_out[...], m_out[...], v_out[...] = _adamw_math(w_ref[...], g, m_ref[...], v_ref[...])

    spec = pl.BlockSpec((tr, tc), lambda i, j: (i, j))
    sh = jax.ShapeDtypeStruct((R, C), F32)
    return _pcall(body, name=name, grid=(R // tr, C // tc), in_specs=[spec] * 4 + [ANY] * len(behind),
                  out_specs=[spec] * 4, out_shape=[sh] * 4, compiler_params=_params(("parallel", "parallel")))(
                      w, g, m, v, *behind)


def _ada_update(sct, dmod_sh, w, m, v, riders=()):
    R, C = w.shape
    tr, tc = _tile(R, 256), _tile(C, 1024)

    def body(s_ref, d_ref, w_ref, m_ref, v_ref, g_out, d_out, m_out, v_out):
        s, d = s_ref[...], d_ref[...]
        g = s[:, 0:1] * d[0:1, :]
        for b in range(1, N_DEV):
            g += s[:, b:b + 1] * d[b:b + 1, :]
        g_out[...] = g
        d_out[...], m_out[...], v_out[...] = _adamw_math(w_ref[...], g, m_ref[...], v_ref[...])

    spec = pl.BlockSpec((tr, tc), lambda i, j: (i, j))
    sh = jax.ShapeDtypeStruct((R, C), F32)
    return _ride(
        "ada_update", body, riders, [sct, dmod_sh, w, m, v], grid=(R // tr, C // tc),
        in_specs=[pl.BlockSpec((tr, N_DEV), lambda i, j: (i, 0)), pl.BlockSpec((N_DEV, tc), lambda i, j: (0, j)),
                  spec, spec, spec],
        out_specs=[spec] * 4, out_shape=[sh] * 4, scratch_shapes=[], sem=("parallel", "parallel"))


def _silu_rows(c_row):
    D = c_row.shape[1]

    def body(c_ref, o_ref):
        cv = c_ref[...]
        o_ref[...] = cv * jax.nn.sigmoid(cv)

    return _pcall(body, name="silu_c", out_shape=jax.ShapeDtypeStruct((1, D), F32))(c_row)


def _pack_partials(parts, widths, total):
    n = len(widths)

    def body(*refs):
        loss_p, out_ref, loss_ref = refs[n], refs[n + 1], refs[n + 2]
        off = 0
        for ref, wd in zip(refs[:n], widths):
            out_ref[:, off:off + wd] = jnp.sum(ref[...], axis=0)
            off += wd
        if off < total:
            out_ref[:, off:total] = jnp.zeros((1, total - off), F32)
        loss_ref[...] = jnp.sum(jnp.sum(loss_p[...], axis=0), axis=1, keepdims=True)

    return _pcall(body, name="pack_partials",
                  out_shape=[jax.ShapeDtypeStruct((1, total), F32), jax.ShapeDtypeStruct((1, 1), F32)])(*parts)


def _small_update(gathered, offsets, params):
    n_p = len(params)

    def body(*refs):
        g_ref = refs[0]
        prm = refs[1:1 + 3 * n_p]
        outs = refs[1 + 3 * n_p:]
        for i, (off, wd) in enumerate(offsets):
            blk = g_ref[:, off:off + wd]
            g = blk[0:1, :]
            for b in range(1, N_DEV):
                g = g + blk[b:b + 1, :]
            w, m, v = prm[3 * i][...], prm[3 * i + 1][...], prm[3 * i + 2][...]
            outs[4 * i][...] = g
            outs[4 * i + 1][...], outs[4 * i + 2][...], outs[4 * i + 3][...] = _adamw_math(w, g, m, v)

    flat = [a for t in params for a in t]
    out_shape = [jax.ShapeDtypeStruct(t[0].shape, F32) for t in params for _ in range(4)]
    return _pcall(body, name="small_update", out_shape=out_shape)(gathered, *flat)


def kernel(x, c, w_ada, b_ada, norm1_w, w_in, q_norm_w, k_norm_w, w_pool, pool_scale, w_a_up, w_b_up, w_o, norm2_w, w_ff1, w_ff2, loss_target, m_w_ada, m_b_ada, m_norm1_w, m_w_in, m_q_norm_w, m_k_norm_w, m_w_pool, m_pool_scale, m_w_a_up, m_w_b_up, m_w_o, m_norm2_w, m_w_ff1, m_w_ff2, v_w_ada, v_b_ada, v_norm1_w, v_w_in, v_q_norm_w, v_k_norm_w, v_w_pool, v_pool_scale, v_w_a_up, v_w_b_up, v_w_o, v_norm2_w, v_w_ff1, v_w_ff2):
    _, S, D = x.shape
    PW = D // 2
    H = PW // HEAD_DIM
    cg = PW // N_GROUPS
    IN = w_in.shape[2] * N_CHIPS
    FF = w_ff1.shape[2] * N_CHIPS
    A_COLS = w_ada.shape[2]
    xi, yi, ci = lax.axis_index("x"), lax.axis_index("y"), lax.axis_index("c")
    chip = 2 * xi + yi
    dev = 2 * chip + ci
    c_arr = jnp.reshape(ci, (1,)).astype(jnp.int32)
    x2, tgt = x[0], loss_target[0]

    ws = [_W("w_in", "col", D, IN), _W("w_pool", "row", PW, cg), _W("w_a_up", "col", PW, D),
          _W("w_b_up", "col", PW, D), _W("w_o", "row", D, D), _W("w_ff1", "col", D, FF), _W("w_ff2", "row", FF, D)]
    w32 = [w_in[0], w_pool[0].reshape(cg, cg), w_a_up[0], w_b_up[0], w_o[0], w_ff1[0], w_ff2[0]]
    m32 = [m_w_in[0], m_w_pool[0].reshape(cg, cg), m_w_a_up[0], m_w_b_up[0], m_w_o[0], m_w_ff1[0], m_w_ff2[0]]
    v32 = [v_w_in[0], v_w_pool[0].reshape(cg, cg), v_w_a_up[0], v_w_b_up[0], v_w_o[0], v_w_ff1[0], v_w_ff2[0]]

    W_IN, W_POOL, W_A, W_B, W_O, W_FF1, W_FF2 = ws
    chip_arr = jnp.reshape(chip, (1,)).astype(jnp.int32)
    cc_arr = jnp.stack([ci, chip]).astype(jnp.int32)
    s_in, s_pool, s_a, s_b, s_o, s_ff1, s_ff2 = [_cast_into_full(w, a, chip_arr) for w, a in zip(ws, w32)]
    (win_f,) = _run_rider("gather_w_in", _ag_rider([W_IN], [s_in]))

    sc_row = _silu_rows(c)
    sc_all = _dev_allgather("gather_silu_c", sc_row.reshape(8, D // 8)).reshape(N_DEV, D)
    sc16 = jnp.concatenate([sc_all, jnp.zeros_like(sc_all)], axis=0)
    b_cols = lax.dynamic_slice(b_ada, (0, chip * A_COLS), (1, A_COLS))
    (mod_cols,) = _mm("mod_cols", [(sc16, w_ada[0])], M=2 * N_DEV, N=A_COLS, K=D, tm=16, tn=1024, tk=1024,
                      a_pro=lambda a: a.astype(BF16), b_pro=lambda b: b.astype(BF16),
                      extras=[(b_cols, "row", 0)], outs=[_tile_out(F32)], epi=lambda accs, ex: [accs[0] + ex[0]])
    mod_all = _dev_allgather("gather_mod", mod_cols[:N_DEV]).reshape(N_CHIPS, 2, N_DEV, A_COLS)
    mod_row = lax.dynamic_index_in_dim(mod_all[:, 0], dev, axis=1, keepdims=False).reshape(1, N_CHIPS * A_COLS)
    shift1, scale1, gate1, shift2, scale2, gate2 = [mod_row[:, i * D:(i + 1) * D] for i in range(6)]

    WIDE = dict(tm=2048, tn=512, tk=2048)
    DEEP = dict(tm=1024, tn=1024, tk=1024)
    h = _norm_mod("norm1_mod", x2, norm1_w, scale1, shift1)
    (proj,), ((wpool_f, wa_f, wb_f, wo_f),) = _mm(
        "in_proj", [(h, win_f)], M=S, N=IN, K=D, outs=[_tile_out(BF16)], epi=lambda accs, ex: [accs[0]], **WIDE,
        riders=[_ag_rider([W_POOL, W_A, W_B, W_O], [s_pool, s_a, s_b, s_o], n_ch=2)])
    pooled, pa = _pool_fwd(proj, wpool_f, pool_scale, S, PW)
    (att, attf), ((wff1_f,),) = _attn_fwd(proj, q_norm_w, k_norm_w, S, H, PW // HEAD_DIM,
                                          riders=[_ag_rider([W_FF1], [s_ff1])])

    def merge_epi(accs, ex):
        sa, sb = jax.nn.sigmoid(ex[0].astype(F32)), jax.nn.sigmoid(ex[1].astype(F32))
        return [sa * accs[0] + sb * accs[1], accs[0], accs[1]]

    (merged, ya, yb), (ff2_a,) = _mm("branch_up_merge", [(pa, wa_f), (att, wb_f)], M=S, N=D, K=PW,
                                     extras=[(proj, "tile", 4 * PW), (proj, "tile", 4 * PW + D)],
                                     outs=[_tile_out(BF16)] * 3, epi=merge_epi,
                                     riders=[_ag_rider([W_FF2], [s_ff2], chunks=(0, 1))])
    (x1, o), (ff2_b,) = _mm("out_proj", [(merged, wo_f)], M=S, N=D, K=D, extras=[(x2, "tile", 0), (gate1, "row", 0)],
                            outs=[_tile_out(F32), _tile_out(BF16)], epi=lambda accs, ex: [ex[0] + ex[1] * accs[0], accs[0]],
                            riders=[_ag_rider([W_FF2], ff2_a, chunks=(1, 2))], **WIDE)
    h2 = _norm_mod("norm2_mod", x1, norm2_w, scale2, shift2)
    (rl,), ((wff2_f,),) = _mm("ff1", [(h2, wff1_f)], M=S, N=FF, K=D, outs=[_tile_out(BF16)], **WIDE,
                              epi=lambda accs, ex: [jnp.maximum(accs[0], 0.0)],
                              riders=[_ag_rider([W_FF2], ff2_b, chunks=(2, 4))])

    def square(a):
        af = a.astype(F32)
        return (af * af).astype(BF16)

    def loss_epi(accs, ex):
        x1_t, tgt_t, g2 = ex
        f = accs[0]
        diff = (x1_t + g2 * f) - tgt_t
        dy = diff * (1.0 / D)
        return [dy, dy * g2, _colsum(dy * f), _colsum(diff * diff)]

    dy, df, dgate2_p, loss_p = _mm("ff2_loss", [(rl, wff2_f)], M=S, N=D, K=FF, a_pro=square, tm=1024, tn=1024, tk=512,
                                   extras=[(x1, "tile", 0), (tgt, "tile", 0), (gate2, "row", 0)],
                                   outs=[_tile_out(F32), _tile_out(BF16), _COLSUM, _COLSUM], epi=loss_epi)

    def pair_sums(group, partials, got):
        return [_pair_sum(w, g, r, c_arr) for w, g, r in zip(group, partials, got)]

    def chip_sums(group, sums, from_chips):
        return [_chip_sum(w, p, q, cc_arr) for w, p, q in zip(group, sums, from_chips)]

    first = lambda accs, ex: [accs[0]]
    gmm = dict(ta=True, outs=[_tile_out(BF16)], epi=first, **WIDE)
    (g_ff2,) = _mm("grad_w_ff2", [(rl, df)], M=FF, N=D, K=S, a_pro=square, ta=True, tm=512, tn=2048, tk=2048,
                   outs=[_tile_out(BF16)], epi=first)
    (dz1,), (got_ff2,) = _mm("d_ff_hidden", [(df, wff2_f)], M=S, N=FF, K=D, tb=True, extras=[(rl, "tile", 0)], **WIDE,
                             outs=[_tile_out(BF16)], epi=lambda accs, ex: [accs[0] * (2.0 * ex[0].astype(F32))],
                             riders=[_px_rider([W_FF2], [g_ff2])])
    sum_ff2 = pair_sums([W_FF2], [g_ff2], got_ff2)
    (g_ff1,), (q_ff2,) = _mm("grad_w_ff1", [(h2, dz1)], M=D, N=FF, K=S,
                             riders=[_cx_rider([W_FF2], sum_ff2, part=(0, 2))], **gmm)
    (dh2,), (got_ff1, q_ff2) = _mm("d_h2", [(dz1, wff1_f)], M=S, N=D, K=FF, tb=True, outs=[_tile_out(F32)], epi=first,
                                   riders=[_px_rider([W_FF1], [g_ff1]),
                                           _cx_rider([W_FF2], sum_ff2, part=(1, 2), q_in=q_ff2)], **DEEP)
    sum_ff1 = pair_sums([W_FF1], [g_ff1], got_ff1)
    dx1, dshift2_p, dscale2_p, gn2_p, do, dgate1_p = _norm_mod_bwd("norm2_bwd", dh2, x1, dy, norm2_w, scale2,
                                                                   gate_o=(o, gate1))
    (g_wo,) = _mm("grad_w_o", [(merged, do)], M=D, N=D, K=S, **gmm)

    def gate_epi(accs, ex):
        dm = accs[0]
        sa, sb = jax.nn.sigmoid(ex[0].astype(F32)), jax.nn.sigmoid(ex[1].astype(F32))
        ya_t, yb_t = ex[2].astype(F32), ex[3].astype(F32)
        return [dm * sa, dm * sb, dm * ya_t * (sa * (1.0 - sa)), dm * yb_t * (sb * (1.0 - sb))]

    dya, dyb, dga, dgb = _mm("d_merged", [(do, wo_f)], M=S, N=D, K=D, tb=True, tm=1024, tn=512, tk=2048,
                             extras=[(proj, "tile", 4 * PW), (proj, "tile", 4 * PW + D), (ya, "tile", 0), (yb, "tile", 0)],
                             outs=[_tile_out(BF16)] * 4, epi=gate_epi)
    (g_wa,) = _mm("grad_w_a_up", [(pa, dya)], M=PW, N=D, K=S, **gmm)
    (g_wb,) = _mm("grad_w_b_up", [(att, dyb)], M=PW, N=D, K=S, **gmm)
    (dpa,) = _mm("d_pool_out", [(dya, wa_f)], M=S, N=PW, K=D, tb=True, outs=[_tile_out(F32)], epi=first, **WIDE)
    mid = [W_A, W_B, W_O]
    (datt,), (got_mid,) = _mm("d_att", [(dyb, wb_f)], M=S, N=PW, K=D, tb=True, outs=[_tile_out(BF16)], epi=first, **WIDE,
                              riders=[_px_rider(mid, [g_wa, g_wb, g_wo])])
    sum_mid = pair_sums(mid, [g_wa, g_wb, g_wo], got_mid)
    du, g_wpool4, gscale_p = _pool_bwd(dpa, pooled, wpool_f, pool_scale, S, PW)
    (dq, dk, dv, gq_p, gk_p), ((q_ff1,),) = _attn_bwd(
        proj, datt, attf, q_norm_w, k_norm_w, S, H, PW // HEAD_DIM, riders=[_cx_rider([W_FF1], sum_ff1)])
    dproj = jnp.concatenate([du, dq, dk, dv, dga, dgb], axis=1)
    early = [W_FF1, W_FF2]
    halves_early = chip_sums(early, sum_ff1 + sum_ff2, [q_ff1, q_ff2[0]])
    (g_win,), (grads_early, (q_wa, q_wb, q_wo)) = _mm(
        "grad_w_in", [(h, dproj)], M=D, N=IN, K=S, riders=[_sf_rider(early, halves_early), _cx_rider(mid, sum_mid)], **gmm)
    last = [W_IN, W_POOL]
    g_last = [g_win, g_wpool4.reshape(PW, cg)]
    (dh,), (got_last,) = _mm("d_h", [(dproj, win_f)], M=S, N=D, K=IN, tb=True, outs=[_tile_out(F32)], epi=first,
                             riders=[_px_rider(last, g_last)], **DEEP)
    sum_last = pair_sums(last, g_last, got_last)
    cx_send, cx_recv, sum_last, land_last, token = _cx_start(last, sum_last)
    grad_x, dshift1_p, dscale1_p, gn1_p = _norm_mod_bwd("norm1_bwd", dh, x2, dx1, norm1_w, scale1 + token[0:1, 0:1])

    parts = [dshift1_p, dscale1_p, dgate1_p, dshift2_p, dscale2_p, dgate2_p, gn1_p, gn2_p,
             gscale_p.reshape(1, 1, PW), gq_p, gk_p]
    widths = [D] * 8 + [PW, HEAD_DIM, HEAD_DIM]
    used = sum(widths)
    P = -(-used // 1024) * 1024
    packed, loss_part = _pack_partials(parts + [loss_p], widths, P)
    gathered = _dev_allgather("gather_vector_grads", packed.reshape(8, P // 8)).reshape(N_DEV, P)
    small = [(b_ada, m_b_ada, v_b_ada), (norm1_w, m_norm1_w, v_norm1_w), (norm2_w, m_norm2_w, v_norm2_w),
             (pool_scale, m_pool_scale, v_pool_scale), (q_norm_w, m_q_norm_w, v_q_norm_w),
             (k_norm_w, m_k_norm_w, v_k_norm_w)]
    offsets = [(0, 6 * D), (6 * D, D), (7 * D, D), (8 * D, PW), (8 * D + PW, HEAD_DIM), (8 * D + PW + HEAD_DIM, HEAD_DIM)]
    su = _small_update(gathered, offsets, small)
    (g_b, d_b, nm_b, nv_b, g_n1, d_n1, nm_n1, nv_n1, g_n2, d_n2, nm_n2, nv_n2, g_ps, d_ps, nm_ps, nv_ps,
     g_qn, d_qn, nm_qn, nv_qn, g_kn, d_kn, nm_kn, nv_kn) = su
    dmod_sh = lax.dynamic_slice(gathered, (0, chip * A_COLS), (N_DEV, A_COLS))
    g_ada, d_ada, nm_ada, nv_ada = _ada_update(sc_all.T, dmod_sh, w_ada[0], m_w_ada[0], v_w_ada[0])

    upd_early = [_adamw("adamw_" + w.name, a, g, m, v, after=token)
                 for w, a, g, m, v in zip(ws[5:], w32[5:], grads_early, m32[5:], v32[5:])]

    sum_last, q_last = _cx_wait(last, cx_send, cx_recv, sum_last, land_last,
                                after=[nv_ada] + [u[3] for u in upd_early])
    halves_late = chip_sums(last + mid, sum_last + sum_mid, q_last + [q_wa, q_wb, q_wo])
    filled = _run_rider("grad_sibling_fill", _sf_rider(last + mid, halves_late))
    upd = [_adamw("adamw_" + w.name, a, g, m, v) for w, a, g, m, v in zip(ws[:5], w32[:5], filled, m32[:5], v32[:5])]
    upd += upd_early

    loss = 0.5 / D * lax.psum(loss_part[0, 0], ("x", "y", "c"))

    def up(a):
        return a[None]

    def pool4(a):
        return a.reshape(1, N_GROUPS, cg // N_CHIPS, cg)

    (gr_win, d_win, nm_win, nv_win), (gr_wp, d_wp, nm_wp, nv_wp), (gr_wa, d_wa, nm_wa, nv_wa), \
        (gr_wb, d_wb, nm_wb, nv_wb), (gr_wo, d_wo, nm_wo, nv_wo), (gr_f1, d_f1, nm_f1, nv_f1), \
        (gr_f2, d_f2, nm_f2, nv_f2) = upd
    return (
        loss, grad_x[None],
        up(g_ada), g_b, g_n1, up(gr_win), g_qn, g_kn, pool4(gr_wp), g_ps, up(gr_wa), up(gr_wb), up(gr_wo), g_n2,
        up(gr_f1), up(gr_f2),
        up(d_ada), d_b, d_n1, up(d_win), d_qn, d_kn, pool4(d_wp), d_ps, up(d_wa), up(d_wb), up(d_wo), d_n2,
        up(d_f1), up(d_f2),
        up(nm_ada), nm_b, nm_n1, up(nm_win), nm_qn, nm_kn, pool4(nm_wp), nm_ps, up(nm_wa), up(nm_wb), up(nm_wo), nm_n2,
        up(nm_f1), up(nm_f2),
        up(nv_ada), nv_b, nv_n1, up(nv_win), nv_qn, nv_kn, pool4(nv_wp), nv_ps, up(nv_wa), up(nv_wb), up(nv_wo), nv_n2,
        up(nv_f1), up(nv_f2),
    )
```

```python
import functools
import math

import jax
import jax.numpy as jnp
from jax import lax
from jax.experimental import pallas as pl
from jax.experimental.pallas import tpu as pltpu

F32 = jnp.float32
BF16 = jnp.bfloat16
MESH = pl.DeviceIdType.MESH
ANY = pl.BlockSpec(memory_space=pl.ANY)

EPS = 1e-6
HEAD_DIM = 128
POOL_WINDOWS = (2, 4, 8, 16)
N_GROUPS = len(POOL_WINDOWS)
N_CHIPS = 4
N_DEV = 8
ADAM_LR, ADAM_B1, ADAM_B2, ADAM_EPS, ADAM_WD, ADAM_STEP = 0.001, 0.9, 0.999, 1e-08, 0.01, 10
VMEM_LIMIT_V7X = 56 * 1024 * 1024
ATT_T = 256
POOL_T = 256


def _pcall(body, **kw):
    return pl.pallas_call(body, **kw)


def _params(sem=None):
    return pltpu.CompilerParams(dimension_semantics=sem, vmem_limit_bytes=VMEM_LIMIT_V7X)


def _tile(n, pref):
    if n <= pref:
        return n
    t = pref
    while n % t:
        t //= 2
    return t


class _Rider:
    def __init__(self, arrays, out_shape, sems, start, finish, aliases=None, steps=()):
        self.arrays, self.out_shape, self.sems = list(arrays), list(out_shape), list(sems)
        self.start, self.finish, self.aliases, self.steps = start, finish, aliases or {}, list(steps)


def _ride(name, body, riders, arrays, *, grid, in_specs, out_specs, out_shape, scratch_shapes, sem):
    n_in, n_out, n_scr = len(arrays), len(out_shape), len(scratch_shapes)
    r_arrays = [a for r in riders for a in r.arrays]
    r_outs = [o for r in riders for o in r.out_shape]
    r_sems = [s for r in riders for s in r.sems]
    n_hooks = max([len(r.steps) for r in riders], default=0)
    total = math.prod(grid)
    aliases, off_i, off_o = {}, n_in, n_out
    for r in riders:
        for a, o in r.aliases.items():
            aliases[off_i + a] = off_o + o
        off_i += len(r.arrays)
        off_o += len(r.out_shape)

    def full(*refs):
        p = 0
        groups = []
        for n in (n_in, len(r_arrays), n_out, len(r_outs), n_scr, len(r_sems)):
            groups.append(refs[p:p + n])
            p += n
        ins, rin, outs, rout, scr, rsem = groups

        def each(what):
            a = o = s = 0
            for r in riders:
                fn = what(r)
                if fn is not None:
                    fn(rin[a:a + len(r.arrays)], rout[o:o + len(r.out_shape)], rsem[s:s + len(r.sems)])
                a, o, s = a + len(r.arrays), o + len(r.out_shape), s + len(r.sems)

        if riders:
            lin = 0
            for d, g in enumerate(grid):
                lin = lin * g + pl.program_id(d)
            pl.when(lin == 0)(lambda: each(lambda r: r.start))
            for t in range(n_hooks):
                pl.when(lin == min(total - 1, ((t + 1) * total) // n_hooks))(
                    lambda t=t: each(lambda r: r.steps[t] if t < len(r.steps) else None))
        body(*ins, *outs, *scr)
        if riders:
            pl.when(lin == total - 1)(lambda: each(lambda r: r.finish))

    res = _pcall(
        full, name=name, grid=grid, in_specs=list(in_specs) + [ANY] * len(r_arrays),
        out_specs=list(out_specs) + [ANY] * len(r_outs), out_shape=list(out_shape) + r_outs,
        scratch_shapes=list(scratch_shapes) + r_sems, input_output_aliases=aliases,
        compiler_params=_params(("arbitrary",) * len(grid) if riders else sem),
    )(*arrays, *r_arrays)
    if not riders:
        return res
    main, rest, per = res[:n_out], res[n_out:], []
    for r in riders:
        per.append(rest[:len(r.out_shape)])
        rest = rest[len(r.out_shape):]
    return main, per


def _run_rider(name, rider):
    def body(*refs):
        n_a, n_o = len(rider.arrays), len(rider.out_shape)
        ins, outs, sems = refs[:n_a], refs[n_a:n_a + n_o], refs[n_a + n_o:]
        for fn in [rider.start] + rider.steps + [rider.finish]:
            fn(ins, outs, sems)

    return _pcall(body, name=name, out_shape=rider.out_shape, in_specs=[ANY] * len(rider.arrays),
                  out_specs=[ANY] * len(rider.out_shape), scratch_shapes=rider.sems,
                  input_output_aliases=rider.aliases)(*rider.arrays)


def _mm(name, pairs, *, M, N, K, ta=False, tb=False, tm=512, tn=1024, tk=1024,
        a_pro=None, b_pro=None, extras=(), outs, epi, riders=(), b_noff=0):
    tm, tn, tk = _tile(M, tm), _tile(N, tn), _tile(K, tk)
    n_i, n_j, n_k = M // tm, N // tn, K // tk
    n_p, n_e = len(pairs), len(extras)
    arrays, in_specs = [], []
    for a, _ in pairs:
        arrays.append(a)
        in_specs.append(pl.BlockSpec((tk, tm), lambda i, j, k: (k, i)) if ta
                        else pl.BlockSpec((tm, tk), lambda i, j, k: (i, k)))
    for _, b in pairs:
        arrays.append(b)
        in_specs.append(pl.BlockSpec((tn, tk), lambda i, j, k: (j + b_noff // tn, k)) if tb
                        else pl.BlockSpec((tk, tn), lambda i, j, k: (k, j + b_noff // tn)))
    for arr, kind, off in extras:
        ob = off // tn
        assert off % tn == 0
        arrays.append(arr)
        if kind == "tile":
            in_specs.append(pl.BlockSpec((tm, tn), lambda i, j, k, ob=ob: (i, j + ob)))
        else:
            in_specs.append(pl.BlockSpec((1, tn), lambda i, j, k, ob=ob: (0, j + ob)))
    out_shape, out_specs = [], []
    for o in outs:
        if o["kind"] == "tile":
            out_shape.append(jax.ShapeDtypeStruct((M, N), o["dtype"]))
            out_specs.append(pl.BlockSpec((tm, tn), lambda i, j, k: (i, j)))
        else:
            out_shape.append(jax.ShapeDtypeStruct((n_i, 1, N), F32))
            out_specs.append(pl.BlockSpec((1, 1, tn), lambda i, j, k: (i, 0, j)))
    dims = (((0 if ta else 1,), (1 if tb else 0,)), ((), ()))

    def body(*refs):
        a_refs, b_refs = refs[:n_p], refs[n_p:2 * n_p]
        e_refs = refs[2 * n_p:2 * n_p + n_e]
        o_refs = refs[2 * n_p + n_e:2 * n_p + n_e + len(outs)]
        acc_refs = refs[2 * n_p + n_e + len(outs):]

        def product(p):
            a, b = a_refs[p][...], b_refs[p][...]
            if a_pro is not None:
                a = a_pro(a)
            if b_pro is not None:
                b = b_pro(b)
            return lax.dot_general(a, b, dims, preferred_element_type=F32)

        def write(accs):
            vals = epi(accs, [e[...] for e in e_refs])
            for o, o_ref, val in zip(outs, o_refs, vals):
                if o["kind"] == "tile":
                    o_ref[...] = val.astype(o_ref.dtype)
                else:
                    o_ref[0] = val

        if n_k == 1:
            write([product(p) for p in range(n_p)])
            return
        k = pl.program_id(2)

        @pl.when(k == 0)
        def _():
            for acc in acc_refs:
                acc[...] = jnp.zeros_like(acc)

        for p in range(n_p):
            acc_refs[p][...] += product(p)

        pl.when(k == n_k - 1)(lambda: write([acc[...] for acc in acc_refs]))

    return _ride(name, body, riders, arrays, grid=(n_i, n_j, n_k), in_specs=in_specs, out_specs=out_specs,
                 out_shape=out_shape, scratch_shapes=[pltpu.VMEM((tm, tn), F32) for _ in pairs] if n_k > 1 else [],
                 sem=("parallel", "parallel", "arbitrary"))


def _tile_out(dtype):
    return {"kind": "tile", "dtype": dtype}


_COLSUM = {"kind": "colsum"}


def _colsum(v):
    return jnp.sum(v, axis=0, keepdims=True)


def _norm_mod(name, x, norm_w, scale, shift):
    S, D = x.shape
    tr = _tile(S, 256)

    def body(x_ref, nw_ref, sc_ref, sh_ref, h_ref):
        xv = x_ref[...]
        r = lax.rsqrt(jnp.mean(xv * xv, axis=-1, keepdims=True) + EPS)
        h_ref[...] = ((xv * r * nw_ref[...]) * (1.0 + sc_ref[...]) + sh_ref[...]).astype(BF16)

    row = pl.BlockSpec((1, D), lambda i: (0, 0))
    til = pl.BlockSpec((tr, D), lambda i: (i, 0))
    return _pcall(body, name=name, grid=(S // tr,), in_specs=[til, row, row, row], out_specs=til,
                  out_shape=jax.ShapeDtypeStruct((S, D), BF16), compiler_params=_params(("parallel",)))(
                      x, norm_w, scale, shift)


def _norm_mod_bwd(name, dh, x, dres, norm_w, scale, gate_o=None):
    S, D = x.shape
    tr = _tile(S, 256)
    n_r = S // tr
    with_gate = gate_o is not None
    dh = list(dh) if isinstance(dh, (list, tuple)) else [dh]
    n_dh = len(dh)

    def body(*refs):
        dh_refs, refs = refs[:n_dh], refs[n_dh:]
        if with_gate:
            x_ref, dres_ref, nw_ref, sc_ref, o_ref, g_ref, dx_ref, p1, p2, p3, do_ref, p4 = refs
        else:
            x_ref, dres_ref, nw_ref, sc_ref, dx_ref, p1, p2, p3 = refs
        dhv = dh_refs[0][...] if n_dh == 1 else jnp.concatenate([r[...] for r in dh_refs], axis=1)
        xv, nw = x_ref[...], nw_ref[...]
        r = lax.rsqrt(jnp.mean(xv * xv, axis=-1, keepdims=True) + EPS)
        xh = xv * r
        p1[0] = _colsum(dhv)
        p2[0] = _colsum(dhv * (xh * nw))
        dn = dhv * (1.0 + sc_ref[...])
        p3[0] = _colsum(dn * xh)
        dxh = dn * nw
        dx = dres_ref[...] + r * (dxh - xh * jnp.mean(dxh * xh, axis=-1, keepdims=True))
        dx_ref[...] = dx
        if with_gate:
            do_ref[...] = (dx * g_ref[...]).astype(BF16)
            p4[0] = _colsum(dx * o_ref[...].astype(F32))

    row = pl.BlockSpec((1, D), lambda i: (0, 0))
    til = pl.BlockSpec((tr, D), lambda i: (i, 0))
    part = pl.BlockSpec((1, 1, D), lambda i: (i, 0, 0))
    part_shape = jax.ShapeDtypeStruct((n_r, 1, D), F32)
    in_specs = [pl.BlockSpec((tr, D // n_dh), lambda i: (i, 0))] * n_dh + [til, til, row, row]
    arrays = dh + [x, dres, norm_w, scale]
    out_specs = [til, part, part, part]
    out_shape = [jax.ShapeDtypeStruct((S, D), F32), part_shape, part_shape, part_shape]
    if with_gate:
        in_specs += [til, row]
        arrays += list(gate_o)
        out_specs += [til, part]
        out_shape += [jax.ShapeDtypeStruct((S, D), BF16), part_shape]
    return _pcall(body, name=name, grid=(n_r,), in_specs=in_specs, out_specs=out_specs, out_shape=out_shape,
                  compiler_params=_params(("parallel",)))(*arrays)


def _pool_w_specs(rows, cg):
    return [pl.BlockSpec((rows, cg), lambda g, j=j: (N_GROUPS * j + g, 0)) for j in range(N_CHIPS)]


def _pool_fwd(proj, wp_full, pool_scale, S, PW):
    cg = PW // N_GROUPS
    rows = cg // N_CHIPS
    T = _tile(S, POOL_T)
    n_t = S // T

    def body(u_ref, w0, w1, w2, w3, ps_ref, pooled_ref, pa_ref):
        g = pl.program_id(0)
        win = jnp.left_shift(2, g)
        w = jnp.concatenate([w0[...], w1[...], w2[...], w3[...]], axis=0)
        t_i = lax.broadcasted_iota(jnp.int32, (T, T), 0)
        j_i = lax.broadcasted_iota(jnp.int32, (T, T), 1)
        b_cur = ((j_i <= t_i) & (j_i > t_i - win)).astype(BF16)
        b_prev = (j_i - T > t_i - win).astype(BF16)
        row = lax.broadcasted_iota(jnp.int32, (T, 1), 0)
        for r in range(n_t):
            cur = u_ref[r * T:(r + 1) * T, :]
            ws = jnp.dot(b_cur, cur, preferred_element_type=F32)
            if r > 0:
                ws += jnp.dot(b_prev, u_ref[(r - 1) * T:r * T, :], preferred_element_type=F32)
            count = jnp.minimum(row + (r * T + 1), win).astype(F32)
            pooled = (ws / count - cur.astype(F32)).astype(BF16)
            pooled_ref[r * T:(r + 1) * T, :] = pooled
            mixed = jnp.dot(pooled, w, preferred_element_type=F32)
            pa_ref[r * T:(r + 1) * T, :] = (mixed * ps_ref[...]).astype(BF16)

    col = pl.BlockSpec((S, cg), lambda g: (0, g))
    return _pcall(
        body, name="pool_fwd", grid=(N_GROUPS,),
        in_specs=[col] + _pool_w_specs(rows, cg) + [pl.BlockSpec((1, cg), lambda g: (0, g))],
        out_specs=[col, col],
        out_shape=[jax.ShapeDtypeStruct((S, PW), BF16), jax.ShapeDtypeStruct((S, PW), BF16)],
        compiler_params=_params(("parallel",)),
    )(proj, wp_full, wp_full, wp_full, wp_full, pool_scale)


def _pool_bwd(dpa, pooled, wp_full, pool_scale, S, PW):
    cg = PW // N_GROUPS
    rows = cg // N_CHIPS
    T = _tile(S, POOL_T)
    n_t = S // T

    def body(dpa_ref, pooled_ref, w0, w1, w2, w3, ps_ref, du_ref, gw_ref, gs_ref, dp_s, dpc_s, dmx_s):
        g = pl.program_id(0)
        win = jnp.left_shift(2, g)
        w = jnp.concatenate([w0[...], w1[...], w2[...], w3[...]], axis=0)
        row = lax.broadcasted_iota(jnp.int32, (T, 1), 0)
        gs = jnp.zeros((1, cg), F32)
        for r in range(n_t):
            sl = slice(r * T, (r + 1) * T)
            mixed = jnp.dot(pooled_ref[sl, :], w, preferred_element_type=F32)
            dpa_t = dpa_ref[sl, :]
            gs += _colsum(dpa_t * mixed)
            dmx = (dpa_t * ps_ref[...]).astype(BF16)
            dmx_s[sl, :] = dmx
            dpo = lax.dot_general(dmx, w, (((1,), (1,)), ((), ())), preferred_element_type=F32)
            dp_s[sl, :] = dpo
            count = jnp.minimum(row + (r * T + 1), win).astype(F32)
            dpc_s[sl, :] = (dpo / count).astype(BF16)
        gs_ref[...] = gs
        gw = lax.dot_general(pooled_ref[...], dmx_s[...], (((0,), (0,)), ((), ())), preferred_element_type=F32)
        for j in range(N_CHIPS):
            gw_ref[j, 0] = gw[j * rows:(j + 1) * rows, :].astype(BF16)
        j_i = lax.broadcasted_iota(jnp.int32, (T, T), 0)
        t_i = lax.broadcasted_iota(jnp.int32, (T, T), 1)
        b_cur = ((t_i >= j_i) & (t_i < j_i + win)).astype(BF16)
        b_next = (t_i + T < j_i + win).astype(BF16)
        for r in range(n_t):
            sl = slice(r * T, (r + 1) * T)
            acc = jnp.dot(b_cur, dpc_s[sl, :], preferred_element_type=F32)
            if r + 1 < n_t:
                acc += jnp.dot(b_next, dpc_s[(r + 1) * T:(r + 2) * T, :], preferred_element_type=F32)
            du_ref[sl, :] = (acc - dp_s[sl, :]).astype(BF16)

    col = pl.BlockSpec((S, cg), lambda g: (0, g))
    return _pcall(
        body, name="pool_bwd", grid=(N_GROUPS,),
        in_specs=[col, col] + _pool_w_specs(rows, cg) + [pl.BlockSpec((1, cg), lambda g: (0, g))],
        out_specs=[col, pl.BlockSpec((N_CHIPS, 1, rows, cg), lambda g: (0, g, 0, 0)),
                   pl.BlockSpec((1, cg), lambda g: (0, g))],
        out_shape=[jax.ShapeDtypeStruct((S, PW), BF16),
                   jax.ShapeDtypeStruct((N_CHIPS, N_GROUPS, rows, cg), BF16),
                   jax.ShapeDtypeStruct((1, PW), F32)],
        scratch_shapes=[pltpu.VMEM((S, cg), F32), pltpu.VMEM((S, cg), BF16), pltpu.VMEM((S, cg), BF16)],
        compiler_params=_params(("parallel",)),
    )(dpa, pooled, wp_full, wp_full, wp_full, wp_full, pool_scale)


_NT = (((1,), (1,)), ((), ()))
_TN = (((0,), (0,)), ((), ()))


def _split_dot(v, tri):
    hi = v.astype(BF16)
    lo = (v - hi.astype(F32)).astype(BF16)
    return jnp.dot(hi, tri, preferred_element_type=F32) + jnp.dot(lo, tri, preferred_element_type=F32)


LOG2E = 1.4426950408889634
QK_SCALE = 1.0 / math.sqrt(HEAD_DIM)


def _sb_scores(q2_i, k_j, tri_l, masked):
    tq, tk = q2_i.shape[0], k_j.shape[0]
    s = lax.dot_general(q2_i, k_j, _NT, preferred_element_type=F32)
    lp = jnp.log(1.0 + jnp.exp2(-jnp.abs(s))) * LOG2E
    lb = jnp.minimum(s, 0.0) - lp
    l = lb - s
    mask = None
    if masked:
        mask = lax.broadcasted_iota(jnp.int32, (tq, tk), 0) > lax.broadcasted_iota(jnp.int32, (tq, tk), 1)
        l = jnp.where(mask, l, 0.0)
    return l, lb, lb + _split_dot(l, tri_l), mask


def _sb_weights(t, carry_l, mask):
    a = jnp.exp2(t + carry_l)
    return a if mask is None else jnp.where(mask, a, 0.0)


def _rowsum(v):
    return jnp.sum(v, axis=1, keepdims=True)


def _qk_norm(x_ref, w_ref):
    xv = x_ref[...].astype(F32)
    r = lax.rsqrt(jnp.mean(xv * xv, axis=-1, keepdims=True) + EPS)
    return xv * r, r


def _attn_fwd(proj, q_norm_w, k_norm_w, S, H, q_off, riders=()):
    t = _tile(S, ATT_T)
    n_q = S // t

    def body(q_ref, k_ref, v_ref, qw_ref, kw_ref, att_ref, attf_ref, qn_s, kn_s):
        qh, _ = _qk_norm(q_ref, qw_ref)
        qn_s[...] = (qh * qw_ref[...] * (QK_SCALE * LOG2E)).astype(BF16)
        kh, _ = _qk_norm(k_ref, kw_ref)
        kn_s[...] = (kh * kw_ref[...]).astype(BF16)
        tri_l = (lax.broadcasted_iota(jnp.int32, (t, t), 0) > lax.broadcasted_iota(jnp.int32, (t, t), 1)).astype(BF16)

        def rows(j):
            return pl.ds(pl.multiple_of(j * t, t), t)

        def q_step(i, _):
            q_i = qn_s[rows(i), :]

            def av(a, j):
                return jnp.dot(a.astype(BF16), v_ref[rows(j), :], preferred_element_type=F32)

            l, _, tt, mask = _sb_scores(q_i, kn_s[rows(i), :], tri_l, True)
            acc = av(_sb_weights(tt, 0.0, mask), i)
            carry = _rowsum(l)

            def single(_, c):
                carry, acc = c
                l, _, tt, _ = _sb_scores(q_i, kn_s[rows(i - 1), :], tri_l, False)
                return carry + _rowsum(l), acc + av(_sb_weights(tt, carry, None), i - 1)

            carry, acc = lax.fori_loop(0, i % 2, single, (carry, acc))
            top = i - 1 - i % 2

            def pair(p, c):
                carry, acc = c
                j0 = top - 2 * p
                l0, _, t0, _ = _sb_scores(q_i, kn_s[rows(j0), :], tri_l, False)
                l1, _, t1, _ = _sb_scores(q_i, kn_s[rows(j0 - 1), :], tri_l, False)
                mid = carry + _rowsum(l0)
                acc = acc + av(_sb_weights(t0, carry, None), j0) + av(_sb_weights(t1, mid, None), j0 - 1)
                return mid + _rowsum(l1), acc

            _, acc = lax.fori_loop(0, i // 2, pair, (carry, acc))
            att_ref[rows(i), :] = acc.astype(BF16)
            attf_ref[rows(i), :] = acc
            return 0

        lax.fori_loop(0, n_q, q_step, 0)

    def col(off):
        return pl.BlockSpec((S, HEAD_DIM), lambda h, off=off: (0, off + h))

    wspec = pl.BlockSpec((1, HEAD_DIM), lambda h: (0, 0))
    return _ride(
        "attn_fwd", body, riders, [proj, proj, proj, q_norm_w, k_norm_w], grid=(H,),
        in_specs=[col(q_off), col(q_off + H), col(q_off + 2 * H), wspec, wspec],
        out_specs=[col(0), col(0)],
        out_shape=[jax.ShapeDtypeStruct((S, H * HEAD_DIM), BF16), jax.ShapeDtypeStruct((S, H * HEAD_DIM), F32)],
        scratch_shapes=[pltpu.VMEM((S, HEAD_DIM), BF16), pltpu.VMEM((S, HEAD_DIM), BF16)],
        sem=("parallel",))


def _attn_bwd(proj, datt, attf, q_norm_w, k_norm_w, S, H, q_off, riders=()):
    t = _tile(S, ATT_T)
    n_q = S // t

    def body(q_ref, k_ref, v_ref, do_ref, o_ref, qw_ref, kw_ref, dq_ref, dk_ref, dv_ref, gq_ref, gk_ref,
             qn_s, kn_s, qz_s, kz_s, dk_s, dv_s, gq_s):
        qw, kw = qw_ref[...], kw_ref[...]
        qh, _ = _qk_norm(q_ref, qw_ref)
        qn_s[...] = (qh * qw * (QK_SCALE * LOG2E)).astype(BF16)
        qz_s[...] = (qh * qw * QK_SCALE).astype(BF16)
        kh, _ = _qk_norm(k_ref, kw_ref)
        kn_s[...] = (kh * kw).astype(BF16)
        kz_s[...] = (kh * kw * QK_SCALE).astype(BF16)
        dk_s[...] = jnp.zeros_like(dk_s)
        dv_s[...] = jnp.zeros_like(dv_s)
        gq_s[...] = jnp.zeros_like(gq_s)
        r_i = lax.broadcasted_iota(jnp.int32, (t, t), 0)
        c_i = lax.broadcasted_iota(jnp.int32, (t, t), 1)
        tri_l = (r_i > c_i).astype(BF16)
        tri_e = (r_i >= c_i).astype(BF16)

        def rows(j):
            return pl.ds(pl.multiple_of(j * t, t), t)

        def q_step(i, _):
            q_i = qn_s[rows(i), :]
            do_i = do_ref[rows(i), :]
            d_i = _rowsum(do_i.astype(F32) * o_ref[rows(i), :])

            def scores(j, masked):
                l, lb, tt, mask = _sb_scores(q_i, kn_s[rows(j), :], tri_l, masked)
                da = lax.dot_general(do_i, v_ref[rows(j), :], _NT, preferred_element_type=F32)
                return l, lb, tt, mask, da

            def grads(j, sc, carry_l, carry_e, dq_acc):
                l, lb, tt, mask, da = sc
                a_bf = _sb_weights(tt, carry_l, mask).astype(BF16)
                e = da * a_bf.astype(F32)
                p = (d_i - carry_e) - _split_dot(e, tri_e)
                dz = e - jnp.exp2(lb) * (e + p)
                if mask is not None:
                    dz = jnp.where(mask, dz, 0.0)
                dz = dz.astype(BF16)
                dk_s[rows(j), :] += lax.dot_general(dz, qz_s[rows(i), :], _TN, preferred_element_type=F32)
                dv_s[rows(j), :] += lax.dot_general(a_bf, do_i, _TN, preferred_element_type=F32)
                return (carry_l + _rowsum(l), carry_e + _rowsum(e),
                        dq_acc + jnp.dot(dz, kz_s[rows(j), :], preferred_element_type=F32))

            c = grads(i, scores(i, True), 0.0, 0.0, jnp.zeros((t, HEAD_DIM), F32))
            c = lax.fori_loop(0, i % 2, lambda _, c: grads(i - 1, scores(i - 1, False), *c), c)
            top = i - 1 - i % 2

            def pair(p, c):
                j0 = top - 2 * p
                s0, s1 = scores(j0, False), scores(j0 - 1, False)
                return grads(j0 - 1, s1, *grads(j0, s0, *c))

            _, _, dqn = lax.fori_loop(0, i // 2, pair, c)
            qv = q_ref[rows(i), :].astype(F32)
            r = lax.rsqrt(jnp.mean(qv * qv, axis=-1, keepdims=True) + EPS)
            xh = qv * r
            gq_s[...] += _colsum(dqn * xh)
            dxh = dqn * qw
            dq_ref[rows(i), :] = (r * (dxh - xh * jnp.mean(dxh * xh, axis=-1, keepdims=True))).astype(BF16)
            return 0

        lax.fori_loop(0, n_q, q_step, 0)
        gq_ref[0] = gq_s[...]
        kh, rk = _qk_norm(k_ref, kw_ref)
        dkn = dk_s[...]
        gk_ref[0] = _colsum(dkn * kh)
        dxh = dkn * kw
        dk_ref[...] = (rk * (dxh - kh * jnp.mean(dxh * kh, axis=-1, keepdims=True))).astype(BF16)
        dv_ref[...] = dv_s[...].astype(BF16)

    def col(off):
        return pl.BlockSpec((S, HEAD_DIM), lambda h, off=off: (0, off + h))

    wspec = pl.BlockSpec((1, HEAD_DIM), lambda h: (0, 0))
    gspec = pl.BlockSpec((1, 1, HEAD_DIM), lambda h: (h, 0, 0))
    act = jax.ShapeDtypeStruct((S, H * HEAD_DIM), BF16)
    gsh = jax.ShapeDtypeStruct((H, 1, HEAD_DIM), F32)
    return _ride(
        "attn_bwd", body, riders, [proj, proj, proj, datt, attf, q_norm_w, k_norm_w], grid=(H,),
        in_specs=[col(q_off), col(q_off + H), col(q_off + 2 * H), col(0), col(0), wspec, wspec],
        out_specs=[col(0), col(0), col(0), gspec, gspec],
        out_shape=[act, act, act, gsh, gsh],
        scratch_shapes=[pltpu.VMEM((S, HEAD_DIM), BF16)] * 4 + [pltpu.VMEM((S, HEAD_DIM), F32)] * 2
        + [pltpu.VMEM((1, HEAD_DIM), F32)],
        sem=("parallel",))


def _place():
    x, y, c = lax.axis_index("x"), lax.axis_index("y"), lax.axis_index("c")
    chips = [(1 - x, y), (x, 1 - y), (1 - x, 1 - y)]
    return x, y, c, chips


def _dev_allgather(name, v):
    m_per, n = v.shape

    def body(x_ref, out_ref, send_sems, recv_sems, local_sem):
        x, y, c, chips = _place()
        me, sibling = (x, y, c), (x, y, 1 - c)

        def rows(px, py, pc):
            return out_ref.at[pl.ds((4 * px + 2 * py + pc) * m_per, m_per), :]

        def copy(k, block, to, src=None):
            return pltpu.make_async_remote_copy(
                src_ref=rows(*block) if src is None else src, dst_ref=rows(*block),
                send_sem=send_sems.at[k], recv_sem=recv_sems.at[k], device_id=to, device_id_type=MESH)

        mine = pltpu.make_async_copy(x_ref, rows(*me), local_sem)
        mine.start()
        first = [copy(0, me, sibling, src=x_ref)]
        first += [copy(1 + j, me, (*chip, c), src=x_ref) for j, chip in enumerate(chips)]
        for cp in first:
            cp.start()
        passed = [copy(4 + j, (*chip, c), sibling) for j, chip in enumerate(chips)]
        for j, chip in enumerate(chips):
            copy(1 + j, (*chip, c), me).wait_recv()
            passed[j].start()
        copy(0, sibling, me).wait_recv()
        for j, chip in enumerate(chips):
            copy(4 + j, (*chip, 1 - c), me).wait_recv()
        for cp in first + passed:
            cp.wait_send()
        mine.wait()

    return _pcall(
        body, name=name, out_shape=jax.ShapeDtypeStruct((N_DEV * m_per, n), v.dtype),
        in_specs=[pl.BlockSpec(memory_space=pltpu.VMEM)], out_specs=pl.BlockSpec(memory_space=pltpu.VMEM),
        scratch_shapes=[pltpu.SemaphoreType.DMA((7,)), pltpu.SemaphoreType.DMA((7,)), pltpu.SemaphoreType.DMA],
        compiler_params=pltpu.CompilerParams(vmem_limit_bytes=VMEM_LIMIT_V7X),
    )(v)


class _W:
    def __init__(self, name, kind, R, C):
        self.name, self.kind, self.R, self.C = name, kind, R, C

    @property
    def shard_shape(self):
        return (self.R, self.C // N_CHIPS) if self.kind == "col" else (self.R // N_CHIPS, self.C)

    @property
    def half_rows(self):
        return self.shard_shape[0] // 2

    def shard_half(self, ref, half):
        return ref.at[pl.ds(half * self.half_rows, self.half_rows), :]

    def region(self, full_ref, chip, half):
        hr = self.half_rows
        if self.kind == "col":
            cw = self.C // N_CHIPS
            return full_ref.at[pl.ds(half * hr, hr), pl.ds(chip * cw, cw)]
        return full_ref.at[pl.ds(chip * (2 * hr) + half * hr, hr), :]

    def region_both(self, full_ref, chip):
        hr = self.half_rows
        if self.kind == "col":
            cw = self.C // N_CHIPS
            return full_ref.at[:, pl.ds(chip * cw, cw)]
        return full_ref.at[pl.ds(chip * (2 * hr), 2 * hr), :]


def _ag_rider(ws, fulls, n_ch=4, chunks=None):
    n_w = len(ws)
    lo, hi = chunks or (0, n_ch)
    per = 6

    def parts(full, sems):
        send_sems, recv_sems = sems
        x, y, c, _ = _place()
        xn, yn, dg = (1 - x, y), (x, 1 - y), (1 - x, 1 - y)
        via = (x + (1 - c) * (1 - 2 * x), y + c * (1 - 2 * y))
        to = (x + c * (1 - 2 * x), y + (1 - c) * (1 - 2 * y))

        def reg(i, chip, half, t):
            nr = ws[i].half_rows // n_ch
            return ws[i].region(full[i], 2 * chip[0] + chip[1], half).at[pl.ds(t * nr, nr), :]

        def copy(r, i, t, k, dev):
            s = (i * (hi - lo) + t - lo) * per + k
            return pltpu.make_async_remote_copy(src_ref=r, dst_ref=r, send_sem=send_sems.at[s],
                                                recv_sem=recv_sems.at[s], device_id=dev, device_id_type=MESH)

        def direct(i, t, k):
            return copy(reg(i, (x, y), c, t), i, t, k, (*(via, to)[k], c))

        def direct_in(i, t, k):
            return copy(reg(i, (via, to)[k], c, t), i, t, k, (*(via, to)[k], c))

        def relay(i, t):
            return copy(reg(i, via, c, t), i, t, 2, (*to, c))

        def relay_in(i, t):
            return copy(reg(i, dg, c, t), i, t, 2, (*to, c))

        def hand(i, t, k, half):
            return copy(reg(i, (xn, yn, dg)[k], half, t), i, t, 3 + k, (x, y, 1 - c))

        return c, direct, direct_in, relay, relay_in, hand

    def start(_, full, sems):
        _, direct, _, _, _, _ = parts(full, sems)
        for t in range(lo, hi):
            for i in range(n_w):
                direct(i, t, 0).start()
                direct(i, t, 1).start()

    def arrived(t):
        def step(_, full, sems):
            c, _, direct_in, relay, relay_in, hand = parts(full, sems)
            for i in range(n_w):
                direct_in(i, t, 0).wait_recv()
                direct_in(i, t, 1).wait_recv()
                relay(i, t).start()
                hand(i, t, 0, c).start()
                hand(i, t, 1, c).start()
                if t > lo:
                    relay_in(i, t - 1).wait_recv()
                    hand(i, t - 1, 2, c).start()
        return step

    def finish(_, full, sems):
        c, direct, _, relay, relay_in, hand = parts(full, sems)
        for i in range(n_w):
            relay_in(i, hi - 1).wait_recv()
            hand(i, hi - 1, 2, c).start()
        for i in range(n_w):
            for t in range(lo, hi):
                for k in range(3):
                    hand(i, t, k, 1 - c).wait_recv()
        for i in range(n_w):
            for t in range(lo, hi):
                direct(i, t, 0).wait_send()
                direct(i, t, 1).wait_send()
                relay(i, t).wait_send()
                for k in range(3):
                    hand(i, t, k, c).wait_send()

    n_sem = per * (hi - lo) * n_w
    return _Rider(fulls, [jax.ShapeDtypeStruct((w.R, w.C), BF16) for w in ws],
                  [pltpu.SemaphoreType.DMA((n_sem,)), pltpu.SemaphoreType.DMA((n_sem,))], start, finish,
                  steps=[arrived(t) for t in range(lo, hi)], aliases={i: i for i in range(n_w)})


def _cast_into_full(w, a32, chip_arr):
    sr, sc = w.shard_shape
    tr, tc = _tile(sr, 512), _tile(sc, 2048)
    n_r, n_c = sr // tr, sc // tc
    if w.kind == "col":
        out_spec = pl.BlockSpec((tr, tc), lambda i, j, chip: (i, chip[0] * n_c + j))
    else:
        out_spec = pl.BlockSpec((tr, tc), lambda i, j, chip: (chip[0] * n_r + i, j))

    def body(chip_ref, a_ref, o_ref):
        o_ref[...] = a_ref[...].astype(BF16)

    return _pcall(
        body, name="cast_" + w.name, out_shape=jax.ShapeDtypeStruct((w.R, w.C), BF16),
        grid_spec=pltpu.PrefetchScalarGridSpec(
            num_scalar_prefetch=1, grid=(n_r, n_c),
            in_specs=[pl.BlockSpec((tr, tc), lambda i, j, chip: (i, j))], out_specs=out_spec),
        compiler_params=_params(("parallel", "parallel")),
    )(chip_arr, a32)


def _half_view(w, g):
    return g if w.kind == "col" else g.reshape(N_CHIPS, w.R // N_CHIPS, w.C)


def _px_rider(ws, grads):
    n_w = len(ws)

    def copies(g, got, sems):
        send_sems, recv_sems = sems
        x, y, c, _ = _place()

        def half_all(w, ref, half):
            hr = w.half_rows
            if w.kind == "col":
                return ref.at[pl.ds(half * hr, hr), :]
            return ref.at[:, pl.ds(half * hr, hr), :]

        return [pltpu.make_async_remote_copy(
            src_ref=half_all(w, g[i], 1 - c), dst_ref=got[i], send_sem=send_sems.at[i], recv_sem=recv_sems.at[i],
            device_id=(x, y, 1 - c), device_id_type=MESH) for i, w in enumerate(ws)]

    def start(g, got, sems):
        for cp in copies(g, got, sems):
            cp.start()

    def finish(g, got, sems):
        for cp in copies(g, got, sems):
            cp.wait_recv()
            cp.wait_send()

    def got_shape(w):
        hr = w.half_rows
        return (hr, w.C) if w.kind == "col" else (N_CHIPS, hr, w.C)

    return _Rider([_half_view(w, g) for w, g in zip(ws, grads)],
                  [jax.ShapeDtypeStruct(got_shape(w), BF16) for w in ws],
                  [pltpu.SemaphoreType.DMA((n_w,)), pltpu.SemaphoreType.DMA((n_w,))], start, finish)


def _pair_sum(w, g, got, c_arr):
    hr = w.half_rows
    if w.kind == "col":
        tr, tc = _tile(hr, 512), _tile(w.C, 2048)
        n_r = hr // tr
        grid = (n_r, w.C // tc)
        g_spec = pl.BlockSpec((tr, tc), lambda i, j, c: (c[0] * n_r + i, j))
        o_spec = pl.BlockSpec((tr, tc), lambda i, j, c: (i, j))
    else:
        tr = _tile(hr, 512)
        n_r = hr // tr
        grid = (N_CHIPS, n_r)
        g_spec = pl.BlockSpec((1, tr, w.C), lambda s, i, c: (s, c[0] * n_r + i, 0))
        o_spec = pl.BlockSpec((1, tr, w.C), lambda s, i, c: (s, i, 0))

    def body(c_ref, g_ref, got_ref, out_ref):
        out_ref[...] = (g_ref[...].astype(F32) + got_ref[...].astype(F32)).astype(BF16)

    return _pcall(
        body, name="grad_pair_sum_" + w.name, out_shape=jax.ShapeDtypeStruct(got.shape, BF16),
        grid_spec=pltpu.PrefetchScalarGridSpec(num_scalar_prefetch=1, grid=grid, in_specs=[g_spec, o_spec],
                                               out_specs=o_spec),
        compiler_params=_params(("parallel", "parallel")),
    )(c_arr, _half_view(w, g), got)


def _cx_rider(ws, sums, part=(0, 1), q_in=None):
    n_w = len(ws)

    def parts(p, q, sems):
        send_sems, recv_sems = sems
        x, y, c, chips = _place()
        my_chip = 2 * x + y

        def rows(w, ref):
            nr = w.half_rows // part[1]
            return ref.at[pl.ds(part[0] * nr, nr), :]

        def piece(w, ref, chip):
            if w.kind == "col":
                cw = w.C // N_CHIPS
                return rows(w, ref.at[:, pl.ds(chip * cw, cw)])
            return rows(w, ref.at[chip])

        def copy(i, k, recv=False):
            chip = chips[k]
            to_chip = 2 * chip[0] + chip[1]
            return pltpu.make_async_remote_copy(
                src_ref=piece(ws[i], p[i], to_chip), dst_ref=rows(ws[i], q[i].at[to_chip if recv else my_chip]),
                send_sem=send_sems.at[3 * i + k], recv_sem=recv_sems.at[3 * i + k],
                device_id=(*chip, c), device_id_type=MESH)

        return copy

    both = [(i, k) for i in range(n_w) for k in range(N_CHIPS - 1)]

    def start(p, q, sems):
        copy = parts(p, q, sems)
        for i, k in both:
            copy(i, k).start()

    def finish(p, q, sems):
        copy = parts(p, q, sems)
        for i, k in both:
            copy(i, k, recv=True).wait_recv()
        for i, k in both:
            copy(i, k).wait_send()

    return _Rider(list(sums) + list(q_in or []),
                  [jax.ShapeDtypeStruct((N_CHIPS, w.half_rows, w.shard_shape[1]), BF16) for w in ws],
                  [pltpu.SemaphoreType.DMA((3 * n_w,)), pltpu.SemaphoreType.DMA((3 * n_w,))], start, finish,
                  aliases={n_w + i: i for i in range(n_w)} if q_in else None)


def _chip_sum(w, p, q, cc_arr):
    hr, cols = w.half_rows, w.shard_shape[1]
    tr, tc = _tile(hr, 512), _tile(cols, 2048)
    n_r, n_c = hr // tr, cols // tc

    def body(cc_ref, own, q1, q2, q3, out_ref):
        own_v = own[...] if w.kind == "col" else own[0]
        out_ref[...] = ((own_v.astype(F32) + q1[0].astype(F32)) + q2[0].astype(F32)) + q3[0].astype(F32)

    if w.kind == "col":
        own_spec = pl.BlockSpec((tr, tc), lambda i, j, cc: (i, cc[1] * n_c + j))
    else:
        own_spec = pl.BlockSpec((1, tr, tc), lambda i, j, cc: (cc[1], i, j))
    q_specs = [pl.BlockSpec((1, tr, tc), lambda i, j, cc, s=s: ((cc[1] + s) % N_CHIPS, i, j)) for s in (1, 2, 3)]
    return _pcall(
        body, name="grad_chip_sum_" + w.name, out_shape=jax.ShapeDtypeStruct(w.shard_shape, F32),
        grid_spec=pltpu.PrefetchScalarGridSpec(
            num_scalar_prefetch=1, grid=(n_r, n_c), in_specs=[own_spec] + q_specs,
            out_specs=pl.BlockSpec((tr, tc), lambda i, j, cc: (cc[0] * n_r + i, j))),
        compiler_params=_params(("parallel", "parallel")),
    )(cc_arr, p, q, q, q)


_SEM = pl.BlockSpec(memory_space=pltpu.SEMAPHORE)
_HBM = pl.BlockSpec(memory_space=pltpu.HBM)


def _cx_split_copies(ws, p, land, send_sems, recv_sems):
    x, y, c, chips = _place()
    my_chip = 2 * x + y
    pairs = []
    for i, w in enumerate(ws):
        for k, chip in enumerate(chips):
            to_chip = 2 * chip[0] + chip[1]
            src = p[i].at[:, pl.ds(to_chip * (w.C // N_CHIPS), w.C // N_CHIPS)] if w.kind == "col" else p[i].at[to_chip]
            kw = dict(send_sem=send_sems.at[3 * i + k], recv_sem=recv_sems.at[3 * i + k], device_id=(*chip, c),
                      device_id_type=MESH)
            pairs.append((pltpu.make_async_remote_copy(src_ref=src, dst_ref=land[i].at[my_chip], **kw),
                          pltpu.make_async_remote_copy(src_ref=src, dst_ref=land[i].at[to_chip], **kw)))
    return pairs


def _cx_start(ws, sums):
    n_w = len(ws)
    lands = [lax.empty((N_CHIPS, w.half_rows, w.shard_shape[1]), BF16) for w in ws]

    def body(*refs):
        p, land = refs[:n_w], refs[n_w:2 * n_w]
        for out, _ in _cx_split_copies(ws, p, land, refs[2 * n_w], refs[2 * n_w + 1]):
            out.start()
        refs[-1][...] = jnp.zeros_like(refs[-1])

    arrays = [pltpu.with_memory_space_constraint(a, pltpu.HBM) for a in list(sums) + lands]
    res = _pcall(
        body, name="grad_last_exchange_start",
        out_shape=(pltpu.SemaphoreType.DMA((3 * n_w,)), pltpu.SemaphoreType.DMA((3 * n_w,)),
                   *[pltpu.HBM(a.shape, a.dtype) for a in arrays], jax.ShapeDtypeStruct((8, 128), F32)),
        in_specs=[_HBM] * (2 * n_w),
        out_specs=(_SEM, _SEM, *[_HBM] * (2 * n_w), pl.BlockSpec(memory_space=pltpu.VMEM)),
        input_output_aliases={i: 2 + i for i in range(2 * n_w)},
        compiler_params=pltpu.CompilerParams(has_side_effects=pltpu.SideEffectType.DATAFLOW_SIDE_EFFECTING),
    )(*arrays)
    return res[0], res[1], list(res[2:2 + n_w]), list(res[2 + n_w:2 + 2 * n_w]), res[-1]


def _cx_wait(ws, send_sems, recv_sems, sums, lands, after):
    n_w = len(ws)

    def body(*refs):
        p, land = refs[:n_w], refs[n_w:2 * n_w]
        for _, cp in _cx_split_copies(ws, p, land, refs[2 * n_w], refs[2 * n_w + 1]):
            cp.wait_send()
            cp.wait_recv()

    res = _pcall(
        body, name="grad_last_exchange_wait",
        out_shape=[pltpu.HBM(a.shape, a.dtype) for a in list(sums) + list(lands)],
        in_specs=[_HBM] * (2 * n_w) + [_SEM, _SEM] + [ANY] * len(after), out_specs=[_HBM] * (2 * n_w),
        input_output_aliases={i: i for i in range(2 * n_w)},
        compiler_params=pltpu.CompilerParams(has_side_effects=pltpu.SideEffectType.DATAFLOW_SIDE_EFFECTING),
    )(*sums, *lands, send_sems, recv_sems, *after)
    return list(res[:n_w]), list(res[n_w:])


def _sf_rider(ws, grads):
    n_w = len(ws)

    def copy(g, sems, i, half):
        send_sems, recv_sems = sems
        x, y, c, _ = _place()
        h = c if half == "mine" else 1 - c
        reg = ws[i].shard_half(g[i], h)
        return pltpu.make_async_remote_copy(src_ref=reg, dst_ref=reg, send_sem=send_sems.at[i], recv_sem=recv_sems.at[i],
                                            device_id=(x, y, 1 - c), device_id_type=MESH)

    def start(_, g, sems):
        for i in range(n_w):
            copy(g, sems, i, "mine").start()

    def finish(_, g, sems):
        for i in range(n_w):
            copy(g, sems, i, "other").wait_recv()
            copy(g, sems, i, "mine").wait_send()

    return _Rider(grads, [jax.ShapeDtypeStruct(w.shard_shape, F32) for w in ws],
                  [pltpu.SemaphoreType.DMA((n_w,)), pltpu.SemaphoreType.DMA((n_w,))], start, finish,
                  aliases={i: i for i in range(n_w)})


def _adamw_math(w, g, m, v):
    m = ADAM_B1 * m + (1.0 - ADAM_B1) * g
    v = ADAM_B2 * v + (1.0 - ADAM_B2) * (g * g)
    m_hat = m / (1.0 - ADAM_B1 ** ADAM_STEP)
    v_hat = v / (1.0 - ADAM_B2 ** ADAM_STEP)
    delta = -ADAM_LR * (m_hat / (jnp.sqrt(v_hat) + ADAM_EPS) + ADAM_WD * w)
    return delta, m, v


def _adamw(name, w, g, m, v, after=None):
    R, C = w.shape
    tr, tc = _tile(R, 256), _tile(C, 2048)
    behind = [] if after is None else [after]

    def body(w_ref, g_ref, m_ref, v_ref, *rest):
        g_out, d_out, m_out, v_out = rest[len(behind):]
        g = g_ref[...]
        g_out[...] = g
        d_out[...], m_out[...], v_out[...] = _adamw_math(w_ref[...], g, m_ref[...], v_ref[...])

    spec = pl.BlockSpec((tr, tc), lambda i, j: (i, j))
    sh = jax.ShapeDtypeStruct((R, C), F32)
    return _pcall(body, name=name, grid=(R // tr, C // tc), in_specs=[spec] * 4 + [ANY] * len(behind),
                  out_specs=[spec] * 4, out_shape=[sh] * 4, compiler_params=_params(("parallel", "parallel")))(
                      w, g, m, v, *behind)


def _ada_update(sct, dmod_sh, w, m, v, riders=()):
    R, C = w.shape
    tr, tc = _tile(R, 256), _tile(C, 1024)

    def body(s_ref, d_ref, w_ref, m_ref, v_ref, g_out, d_out, m_out, v_out):
        s, d = s_ref[...], d_ref[...]
        g = s[:, 0:1] * d[0:1, :]
        for b in range(1, N_DEV):
            g += s[:, b:b + 1] * d[b:b + 1, :]
        g_out[...] = g
        d_out[...], m_out[...], v_out[...] = _adamw_math(w_ref[...], g, m_ref[...], v_ref[...])

    spec = pl.BlockSpec((tr, tc), lambda i, j: (i, j))
    sh = jax.ShapeDtypeStruct((R, C), F32)
    return _ride(
        "ada_update", body, riders, [sct, dmod_sh, w, m, v], grid=(R // tr, C // tc),
        in_specs=[pl.BlockSpec((tr, N_DEV), lambda i, j: (i, 0)), pl.BlockSpec((N_DEV, tc), lambda i, j: (0, j)),
                  spec, spec, spec],
        out_specs=[spec] * 4, out_shape=[sh] * 4, scratch_shapes=[], sem=("parallel", "parallel"))


def _silu_rows(c_row):
    D = c_row.shape[1]

    def body(c_ref, o_ref):
        cv = c_ref[...]
        o_ref[...] = cv * jax.nn.sigmoid(cv)

    return _pcall(body, name="silu_c", out_shape=jax.ShapeDtypeStruct((1, D), F32))(c_row)


def _pack_partials(parts, widths, total):
    n = len(widths)

    def body(*refs):
        loss_p, out_ref = refs[n], refs[n + 1]
        off = 0
        for ref, wd in zip(refs[:n], widths):
            out_ref[:, off:off + wd] = jnp.sum(ref[...], axis=0)
            off += wd
        loss = jnp.sum(jnp.sum(loss_p[...], axis=0), axis=1, keepdims=True)
        out_ref[:, off:off + 128] = jnp.broadcast_to(loss, (1, 128))
        if off + 128 < total:
            out_ref[:, off + 128:total] = jnp.zeros((1, total - off - 128), F32)

    return _pcall(body, name="pack_partials", out_shape=jax.ShapeDtypeStruct((1, total), F32))(*parts)


def _small_update(gathered, offsets, params, loss_off):
    n_p = len(params)

    def over_devices(g_ref, off, wd):
        blk = g_ref[:, off:off + wd]
        g = blk[0:1, :]
        for b in range(1, N_DEV):
            g = g + blk[b:b + 1, :]
        return g

    def body(*refs):
        g_ref = refs[0]
        prm = refs[1:1 + 3 * n_p]
        outs = refs[1 + 3 * n_p:]
        outs[4 * n_p][...] = over_devices(g_ref, loss_off, 128)
        for i, (off, wd) in enumerate(offsets):
            g = over_devices(g_ref, off, wd)
            w, m, v = prm[3 * i][...], prm[3 * i + 1][...], prm[3 * i + 2][...]
            outs[4 * i][...] = g
            outs[4 * i + 1][...], outs[4 * i + 2][...], outs[4 * i + 3][...] = _adamw_math(w, g, m, v)

    flat = [a for t in params for a in t]
    out_shape = [jax.ShapeDtypeStruct(t[0].shape, F32) for t in params for _ in range(4)]
    out_shape.append(jax.ShapeDtypeStruct((1, 128), F32))
    return _pcall(body, name="small_update", out_shape=out_shape)(gathered, *flat)


def kernel(x, c, w_ada, b_ada, norm1_w, w_in, q_norm_w, k_norm_w, w_pool, pool_scale, w_a_up, w_b_up, w_o, norm2_w, w_ff1, w_ff2, loss_target, m_w_ada, m_b_ada, m_norm1_w, m_w_in, m_q_norm_w, m_k_norm_w, m_w_pool, m_pool_scale, m_w_a_up, m_w_b_up, m_w_o, m_norm2_w, m_w_ff1, m_w_ff2, v_w_ada, v_b_ada, v_norm1_w, v_w_in, v_q_norm_w, v_k_norm_w, v_w_pool, v_pool_scale, v_w_a_up, v_w_b_up, v_w_o, v_norm2_w, v_w_ff1, v_w_ff2):
    _, S, D = x.shape
    PW = D // 2
    H = PW // HEAD_DIM
    cg = PW // N_GROUPS
    IN = w_in.shape[2] * N_CHIPS
    FF = w_ff1.shape[2] * N_CHIPS
    A_COLS = w_ada.shape[2]
    xi, yi, ci = lax.axis_index("x"), lax.axis_index("y"), lax.axis_index("c")
    chip = 2 * xi + yi
    dev = 2 * chip + ci
    c_arr = jnp.reshape(ci, (1,)).astype(jnp.int32)
    x2, tgt = x[0], loss_target[0]

    ws = [_W("w_in", "col", D, IN), _W("w_pool", "row", PW, cg), _W("w_a_up", "col", PW, D),
          _W("w_b_up", "col", PW, D), _W("w_o", "row", D, D), _W("w_ff1", "col", D, FF), _W("w_ff2", "row", FF, D)]
    w32 = [w_in[0], w_pool[0].reshape(cg, cg), w_a_up[0], w_b_up[0], w_o[0], w_ff1[0], w_ff2[0]]
    m32 = [m_w_in[0], m_w_pool[0].reshape(cg, cg), m_w_a_up[0], m_w_b_up[0], m_w_o[0], m_w_ff1[0], m_w_ff2[0]]
    v32 = [v_w_in[0], v_w_pool[0].reshape(cg, cg), v_w_a_up[0], v_w_b_up[0], v_w_o[0], v_w_ff1[0], v_w_ff2[0]]

    W_IN, W_POOL, W_A, W_B, W_O, W_FF1, W_FF2 = ws
    chip_arr = jnp.reshape(chip, (1,)).astype(jnp.int32)
    cc_arr = jnp.stack([ci, chip]).astype(jnp.int32)
    s_in, s_pool, s_a, s_b, s_o, s_ff1, s_ff2 = [_cast_into_full(w, a, chip_arr) for w, a in zip(ws, w32)]
    (win_f,) = _run_rider("gather_w_in", _ag_rider([W_IN], [s_in]))

    sc_row = _silu_rows(c)
    sc_all = _dev_allgather("gather_silu_c", sc_row.reshape(8, D // 8)).reshape(N_DEV, D)
    sc16 = jnp.concatenate([sc_all, jnp.zeros_like(sc_all)], axis=0)
    b_cols = lax.dynamic_slice(b_ada, (0, chip * A_COLS), (1, A_COLS))
    (mod_cols,) = _mm("mod_cols", [(sc16, w_ada[0])], M=2 * N_DEV, N=A_COLS, K=D, tm=16, tn=1024, tk=1024,
                      a_pro=lambda a: a.astype(BF16), b_pro=lambda b: b.astype(BF16),
                      extras=[(b_cols, "row", 0)], outs=[_tile_out(F32)], epi=lambda accs, ex: [accs[0] + ex[0]])
    mod_all = _dev_allgather("gather_mod", mod_cols[:N_DEV]).reshape(N_CHIPS, 2, N_DEV, A_COLS)
    mod_row = lax.dynamic_index_in_dim(mod_all[:, 0], dev, axis=1, keepdims=False).reshape(1, N_CHIPS * A_COLS)
    shift1, scale1, gate1, shift2, scale2, gate2 = [mod_row[:, i * D:(i + 1) * D] for i in range(6)]

    WIDE = dict(tm=2048, tn=512, tk=2048)
    DEEP = dict(tm=1024, tn=1024, tk=1024)
    h = _norm_mod("norm1_mod", x2, norm1_w, scale1, shift1)
    (proj,), ((wpool_f, wa_f, wb_f, wo_f),) = _mm(
        "in_proj", [(h, win_f)], M=S, N=IN, K=D, outs=[_tile_out(BF16)], epi=lambda accs, ex: [accs[0]], **WIDE,
        riders=[_ag_rider([W_POOL, W_A, W_B, W_O], [s_pool, s_a, s_b, s_o], n_ch=2)])
    pooled, pa = _pool_fwd(proj, wpool_f, pool_scale, S, PW)
    (att, attf), ((wff1_f,),) = _attn_fwd(proj, q_norm_w, k_norm_w, S, H, PW // HEAD_DIM,
                                          riders=[_ag_rider([W_FF1], [s_ff1])])

    def merge_epi(accs, ex):
        sa, sb = jax.nn.sigmoid(ex[0].astype(F32)), jax.nn.sigmoid(ex[1].astype(F32))
        return [sa * accs[0] + sb * accs[1], accs[0], accs[1]]

    (merged, ya, yb), (ff2_a,) = _mm("branch_up_merge", [(pa, wa_f), (att, wb_f)], M=S, N=D, K=PW,
                                     extras=[(proj, "tile", 4 * PW), (proj, "tile", 4 * PW + D)],
                                     outs=[_tile_out(BF16)] * 3, epi=merge_epi,
                                     riders=[_ag_rider([W_FF2], [s_ff2], chunks=(0, 1))])
    (x1, o), (ff2_b,) = _mm("out_proj", [(merged, wo_f)], M=S, N=D, K=D, extras=[(x2, "tile", 0), (gate1, "row", 0)],
                            outs=[_tile_out(F32), _tile_out(BF16)], epi=lambda accs, ex: [ex[0] + ex[1] * accs[0], accs[0]],
                            riders=[_ag_rider([W_FF2], ff2_a, chunks=(1, 2))], **WIDE)
    h2 = _norm_mod("norm2_mod", x1, norm2_w, scale2, shift2)
    (rl,), ((wff2_f,),) = _mm("ff1", [(h2, wff1_f)], M=S, N=FF, K=D, outs=[_tile_out(BF16)], **WIDE,
                              epi=lambda accs, ex: [jnp.maximum(accs[0], 0.0)],
                              riders=[_ag_rider([W_FF2], ff2_b, chunks=(2, 4))])

    def square(a):
        af = a.astype(F32)
        return (af * af).astype(BF16)

    def loss_epi(accs, ex):
        x1_t, tgt_t, g2 = ex
        f = accs[0]
        diff = (x1_t + g2 * f) - tgt_t
        dy = diff * (1.0 / D)
        return [dy, dy * g2, _colsum(dy * f), _colsum(diff * diff)]

    dy, df, dgate2_p, loss_p = _mm("ff2_loss", [(rl, wff2_f)], M=S, N=D, K=FF, a_pro=square, tm=1024, tn=1024, tk=512,
                                   extras=[(x1, "tile", 0), (tgt, "tile", 0), (gate2, "row", 0)],
                                   outs=[_tile_out(F32), _tile_out(BF16), _COLSUM, _COLSUM], epi=loss_epi)

    def pair_sums(group, partials, got):
        return [_pair_sum(w, g, r, c_arr) for w, g, r in zip(group, partials, got)]

    def chip_sums(group, sums, from_chips):
        return [_chip_sum(w, p, q, cc_arr) for w, p, q in zip(group, sums, from_chips)]

    first = lambda accs, ex: [accs[0]]
    gmm = dict(ta=True, outs=[_tile_out(BF16)], epi=first, **WIDE)
    (g_ff2,) = _mm("grad_w_ff2", [(rl, df)], M=FF, N=D, K=S, a_pro=square, ta=True, tm=512, tn=2048, tk=2048,
                   outs=[_tile_out(BF16)], epi=first)
    (dz1,), (got_ff2,) = _mm("d_ff_hidden", [(df, wff2_f)], M=S, N=FF, K=D, tb=True, extras=[(rl, "tile", 0)], **WIDE,
                             outs=[_tile_out(BF16)], epi=lambda accs, ex: [accs[0] * (2.0 * ex[0].astype(F32))],
                             riders=[_px_rider([W_FF2], [g_ff2])])
    sum_ff2 = pair_sums([W_FF2], [g_ff2], got_ff2)
    (g_ff1,), (q_ff2,) = _mm("grad_w_ff1", [(h2, dz1)], M=D, N=FF, K=S,
                             riders=[_cx_rider([W_FF2], sum_ff2, part=(0, 2))], **gmm)
    (dh2,), (got_ff1, q_ff2) = _mm("d_h2", [(dz1, wff1_f)], M=S, N=D, K=FF, tb=True, outs=[_tile_out(F32)], epi=first,
                                   riders=[_px_rider([W_FF1], [g_ff1]),
                                           _cx_rider([W_FF2], sum_ff2, part=(1, 2), q_in=q_ff2)], **DEEP)
    sum_ff1 = pair_sums([W_FF1], [g_ff1], got_ff1)
    dx1, dshift2_p, dscale2_p, gn2_p, do, dgate1_p = _norm_mod_bwd("norm2_bwd", dh2, x1, dy, norm2_w, scale2,
                                                                   gate_o=(o, gate1))
    (g_wo,) = _mm("grad_w_o", [(merged, do)], M=D, N=D, K=S, **gmm)

    def gate_epi(accs, ex):
        dm = accs[0]
        sa, sb = jax.nn.sigmoid(ex[0].astype(F32)), jax.nn.sigmoid(ex[1].astype(F32))
        ya_t, yb_t = ex[2].astype(F32), ex[3].astype(F32)
        return [dm * sa, dm * sb, dm * ya_t * (sa * (1.0 - sa)), dm * yb_t * (sb * (1.0 - sb))]

    dya, dyb, dga, dgb = _mm("d_merged", [(do, wo_f)], M=S, N=D, K=D, tb=True, tm=1024, tn=512, tk=2048,
                             extras=[(proj, "tile", 4 * PW), (proj, "tile", 4 * PW + D), (ya, "tile", 0), (yb, "tile", 0)],
                             outs=[_tile_out(BF16)] * 4, epi=gate_epi)
    (g_wa,) = _mm("grad_w_a_up", [(pa, dya)], M=PW, N=D, K=S, **gmm)
    (g_wb,) = _mm("grad_w_b_up", [(att, dyb)], M=PW, N=D, K=S, **gmm)
    (dpa,) = _mm("d_pool_out", [(dya, wa_f)], M=S, N=PW, K=D, tb=True, outs=[_tile_out(F32)], epi=first, **WIDE)
    mid = [W_A, W_B, W_O]
    (datt,), (got_mid,) = _mm("d_att", [(dyb, wb_f)], M=S, N=PW, K=D, tb=True, outs=[_tile_out(BF16)], epi=first, **WIDE,
                              riders=[_px_rider(mid, [g_wa, g_wb, g_wo])])
    sum_mid = pair_sums(mid, [g_wa, g_wb, g_wo], got_mid)
    du, g_wpool4, gscale_p = _pool_bwd(dpa, pooled, wpool_f, pool_scale, S, PW)
    (dq, dk, dv, gq_p, gk_p), ((q_ff1,),) = _attn_bwd(
        proj, datt, attf, q_norm_w, k_norm_w, S, H, PW // HEAD_DIM, riders=[_cx_rider([W_FF1], sum_ff1)])
    dproj = jnp.concatenate([du, dq, dk, dv, dga, dgb], axis=1)
    early = [W_FF1, W_FF2]
    halves_early = chip_sums(early, sum_ff1 + sum_ff2, [q_ff1, q_ff2[0]])
    (g_win,), (grads_early, (q_wa, q_wb, q_wo)) = _mm(
        "grad_w_in", [(h, dproj)], M=D, N=IN, K=S, riders=[_sf_rider(early, halves_early), _cx_rider(mid, sum_mid)], **gmm)
    last = [W_IN, W_POOL]
    g_last = [g_win, g_wpool4.reshape(PW, cg)]
    (dh,), (got_last,) = _mm("d_h", [(dproj, win_f)], M=S, N=D, K=IN, tb=True, outs=[_tile_out(F32)], epi=first,
                             riders=[_px_rider(last, g_last)], **DEEP)
    sum_last = pair_sums(last, g_last, got_last)
    cx_send, cx_recv, sum_last, land_last, token = _cx_start(last, sum_last)
    grad_x, dshift1_p, dscale1_p, gn1_p = _norm_mod_bwd("norm1_bwd", dh, x2, dx1, norm1_w, scale1 + token[0:1, 0:1])

    parts = [dshift1_p, dscale1_p, dgate1_p, dshift2_p, dscale2_p, dgate2_p, gn1_p, gn2_p,
             gscale_p.reshape(1, 1, PW), gq_p, gk_p]
    widths = [D] * 8 + [PW, HEAD_DIM, HEAD_DIM]
    used = sum(widths)
    P = -(-(used + 128) // 1024) * 1024
    packed = _pack_partials(parts + [loss_p], widths, P)
    gathered = _dev_allgather("gather_vector_grads", packed.reshape(8, P // 8)).reshape(N_DEV, P)
    small = [(b_ada, m_b_ada, v_b_ada), (norm1_w, m_norm1_w, v_norm1_w), (norm2_w, m_norm2_w, v_norm2_w),
             (pool_scale, m_pool_scale, v_pool_scale), (q_norm_w, m_q_norm_w, v_q_norm_w),
             (k_norm_w, m_k_norm_w, v_k_norm_w)]
    offsets = [(0, 6 * D), (6 * D, D), (7 * D, D), (8 * D, PW), (8 * D + PW, HEAD_DIM), (8 * D + PW + HEAD_DIM, HEAD_DIM)]
    su = _small_update(gathered, offsets, small, used)
    (g_b, d_b, nm_b, nv_b, g_n1, d_n1, nm_n1, nv_n1, g_n2, d_n2, nm_n2, nv_n2, g_ps, d_ps, nm_ps, nv_ps,
     g_qn, d_qn, nm_qn, nv_qn, g_kn, d_kn, nm_kn, nv_kn, loss_sum) = su
    dmod_sh = lax.dynamic_slice(gathered, (0, chip * A_COLS), (N_DEV, A_COLS))
    g_ada, d_ada, nm_ada, nv_ada = _ada_update(sc_all.T, dmod_sh, w_ada[0], m_w_ada[0], v_w_ada[0])

    upd_early = [_adamw("adamw_" + w.name, a, g, m, v, after=token)
                 for w, a, g, m, v in zip(ws[5:], w32[5:], grads_early, m32[5:], v32[5:])]

    sum_last, q_last = _cx_wait(last, cx_send, cx_recv, sum_last, land_last,
                                after=[nv_ada] + [u[3] for u in upd_early])
    halves_late = chip_sums(last + mid, sum_last + sum_mid, q_last + [q_wa, q_wb, q_wo])
    filled = _run_rider("grad_sibling_fill", _sf_rider(last + mid, halves_late))
    upd = [_adamw("adamw_" + w.name, a, g, m, v) for w, a, g, m, v in zip(ws[:5], w32[:5], filled, m32[:5], v32[:5])]
    upd += upd_early

    loss = (0.5 / D) * loss_sum[0, 0]

    def up(a):
        return a[None]

    def pool4(a):
        return a.reshape(1, N_GROUPS, cg // N_CHIPS, cg)

    (gr_win, d_win, nm_win, nv_win), (gr_wp, d_wp, nm_wp, nv_wp), (gr_wa, d_wa, nm_wa, nv_wa), \
        (gr_wb, d_wb, nm_wb, nv_wb), (gr_wo, d_wo, nm_wo, nv_wo), (gr_f1, d_f1, nm_f1, nv_f1), \
        (gr_f2, d_f2, nm_f2, nv_f2) = upd
    return (
        loss, grad_x[None],
        up(g_ada), g_b, g_n1, up(gr_win), g_qn, g_kn, pool4(gr_wp), g_ps, up(gr_wa), up(gr_wb), up(gr_wo), g_n2,
        up(gr_f1), up(gr_f2),
        up(d_ada), d_b, d_n1, up(d_win), d_qn, d_kn, pool4(d_wp), d_ps, up(d_wa), up(d_wb), up(d_wo), d_n2,
        up(d_f1), up(d_f2),
        up(nm_ada), nm_b, nm_n1, up(nm_win), nm_qn, nm_kn, pool4(nm_wp), nm_ps, up(nm_wa), up(nm_wb), up(nm_wo), nm_n2,
        up(nm_f1), up(nm_f2),
        up(nv_ada), nv_b, nv_n1, up(nv_win), nv_qn, nv_kn, pool4(nv_wp), nv_ps, up(nv_wa), up(nv_wb), up(nv_wo), nv_n2,
        up(nv_f1), up(nv_f2),
    )
```

```python
import functools
import math

import jax
import jax.numpy as jnp
from jax import lax
from jax.experimental import pallas as pl
from jax.experimental.pallas import tpu as pltpu

F32 = jnp.float32
BF16 = jnp.bfloat16
MESH = pl.DeviceIdType.MESH
ANY = pl.BlockSpec(memory_space=pl.ANY)

EPS = 1e-6
HEAD_DIM = 128
POOL_WINDOWS = (2, 4, 8, 16)
N_GROUPS = len(POOL_WINDOWS)
N_CHIPS = 4
N_DEV = 8
ADAM_LR, ADAM_B1, ADAM_B2, ADAM_EPS, ADAM_WD, ADAM_STEP = 0.001, 0.9, 0.999, 1e-08, 0.01, 10
VMEM_LIMIT_V7X = 56 * 1024 * 1024
ATT_T = 256
POOL_T = 256


def _pcall(body, **kw):
    return pl.pallas_call(body, **kw)


def _params(sem=None):
    return pltpu.CompilerParams(dimension_semantics=sem, vmem_limit_bytes=VMEM_LIMIT_V7X)


def _tile(n, pref):
    if n <= pref:
        return n
    t = pref
    while n % t:
        t //= 2
    return t


class _Rider:
    def __init__(self, arrays, out_shape, sems, start, finish, aliases=None, steps=()):
        self.arrays, self.out_shape, self.sems = list(arrays), list(out_shape), list(sems)
        self.start, self.finish, self.aliases, self.steps = start, finish, aliases or {}, list(steps)


def _ride(name, body, riders, arrays, *, grid, in_specs, out_specs, out_shape, scratch_shapes, sem):
    n_in, n_out, n_scr = len(arrays), len(out_shape), len(scratch_shapes)
    r_arrays = [a for r in riders for a in r.arrays]
    r_outs = [o for r in riders for o in r.out_shape]
    r_sems = [s for r in riders for s in r.sems]
    n_hooks = max([len(r.steps) for r in riders], default=0)
    total = math.prod(grid)
    aliases, off_i, off_o = {}, n_in, n_out
    for r in riders:
        for a, o in r.aliases.items():
            aliases[off_i + a] = off_o + o
        off_i += len(r.arrays)
        off_o += len(r.out_shape)

    def full(*refs):
        p = 0
        groups = []
        for n in (n_in, len(r_arrays), n_out, len(r_outs), n_scr, len(r_sems)):
            groups.append(refs[p:p + n])
            p += n
        ins, rin, outs, rout, scr, rsem = groups

        def each(what):
            a = o = s = 0
            for r in riders:
                fn = what(r)
                if fn is not None:
                    fn(rin[a:a + len(r.arrays)], rout[o:o + len(r.out_shape)], rsem[s:s + len(r.sems)])
                a, o, s = a + len(r.arrays), o + len(r.out_shape), s + len(r.sems)

        if riders:
            lin = 0
            for d, g in enumerate(grid):
                lin = lin * g + pl.program_id(d)
            pl.when(lin == 0)(lambda: each(lambda r: r.start))
            for t in range(n_hooks):
                pl.when(lin == min(total - 1, ((t + 1) * total) // n_hooks))(
                    lambda t=t: each(lambda r: r.steps[t] if t < len(r.steps) else None))
        body(*ins, *outs, *scr)
        if riders:
            pl.when(lin == total - 1)(lambda: each(lambda r: r.finish))

    res = _pcall(
        full, name=name, grid=grid, in_specs=list(in_specs) + [ANY] * len(r_arrays),
        out_specs=list(out_specs) + [ANY] * len(r_outs), out_shape=list(out_shape) + r_outs,
        scratch_shapes=list(scratch_shapes) + r_sems, input_output_aliases=aliases,
        compiler_params=_params(("arbitrary",) * len(grid) if riders else sem),
    )(*arrays, *r_arrays)
    if not riders:
        return res
    main, rest, per = res[:n_out], res[n_out:], []
    for r in riders:
        per.append(rest[:len(r.out_shape)])
        rest = rest[len(r.out_shape):]
    return main, per


def _run_rider(name, rider):
    def body(*refs):
        n_a, n_o = len(rider.arrays), len(rider.out_shape)
        ins, outs, sems = refs[:n_a], refs[n_a:n_a + n_o], refs[n_a + n_o:]
        for fn in [rider.start] + rider.steps + [rider.finish]:
            fn(ins, outs, sems)

    return _pcall(body, name=name, out_shape=rider.out_shape, in_specs=[ANY] * len(rider.arrays),
                  out_specs=[ANY] * len(rider.out_shape), scratch_shapes=rider.sems,
                  input_output_aliases=rider.aliases)(*rider.arrays)


def _mm(name, pairs, *, M, N, K, ta=False, tb=False, tm=512, tn=1024, tk=1024,
        a_pro=None, b_pro=None, extras=(), outs, epi, riders=(), b_noff=0):
    tm, tn, tk = _tile(M, tm), _tile(N, tn), _tile(K, tk)
    n_i, n_j, n_k = M // tm, N // tn, K // tk
    n_p, n_e = len(pairs), len(extras)
    arrays, in_specs = [], []
    for a, _ in pairs:
        arrays.append(a)
        in_specs.append(pl.BlockSpec((tk, tm), lambda i, j, k: (k, i)) if ta
                        else pl.BlockSpec((tm, tk), lambda i, j, k: (i, k)))
    for _, b in pairs:
        arrays.append(b)
        in_specs.append(pl.BlockSpec((tn, tk), lambda i, j, k: (j + b_noff // tn, k)) if tb
                        else pl.BlockSpec((tk, tn), lambda i, j, k: (k, j + b_noff // tn)))
    for arr, kind, off in extras:
        ob = off // tn
        assert off % tn == 0
        arrays.append(arr)
        if kind == "tile":
            in_specs.append(pl.BlockSpec((tm, tn), lambda i, j, k, ob=ob: (i, j + ob)))
        else:
            in_specs.append(pl.BlockSpec((1, tn), lambda i, j, k, ob=ob: (0, j + ob)))
    out_shape, out_specs = [], []
    for o in outs:
        if o["kind"] == "tile":
            out_shape.append(jax.ShapeDtypeStruct((M, N), o["dtype"]))
            out_specs.append(pl.BlockSpec((tm, tn), lambda i, j, k: (i, j)))
        else:
            out_shape.append(jax.ShapeDtypeStruct((n_i, 1, N), F32))
            out_specs.append(pl.BlockSpec((1, 1, tn), lambda i, j, k: (i, 0, j)))
    dims = (((0 if ta else 1,), (1 if tb else 0,)), ((), ()))

    def body(*refs):
        a_refs, b_refs = refs[:n_p], refs[n_p:2 * n_p]
        e_refs = refs[2 * n_p:2 * n_p + n_e]
        o_refs = refs[2 * n_p + n_e:2 * n_p + n_e + len(outs)]
        acc_refs = refs[2 * n_p + n_e + len(outs):]

        def product(p):
            a, b = a_refs[p][...], b_refs[p][...]
            if a_pro is not None:
                a = a_pro(a)
            if b_pro is not None:
                b = b_pro(b)
            return lax.dot_general(a, b, dims, preferred_element_type=F32)

        def write(accs):
            vals = epi(accs, [e[...] for e in e_refs])
            for o, o_ref, val in zip(outs, o_refs, vals):
                if o["kind"] == "tile":
                    o_ref[...] = val.astype(o_ref.dtype)
                else:
                    o_ref[0] = val

        if n_k == 1:
            write([product(p) for p in range(n_p)])
            return
        k = pl.program_id(2)

        @pl.when(k == 0)
        def _():
            for acc in acc_refs:
                acc[...] = jnp.zeros_like(acc)

        for p in range(n_p):
            acc_refs[p][...] += product(p)

        pl.when(k == n_k - 1)(lambda: write([acc[...] for acc in acc_refs]))

    return _ride(name, body, riders, arrays, grid=(n_i, n_j, n_k), in_specs=in_specs, out_specs=out_specs,
                 out_shape=out_shape, scratch_shapes=[pltpu.VMEM((tm, tn), F32) for _ in pairs] if n_k > 1 else [],
                 sem=("parallel", "parallel", "arbitrary"))


def _tile_out(dtype):
    return {"kind": "tile", "dtype": dtype}


_COLSUM = {"kind": "colsum"}


def _colsum(v):
    return jnp.sum(v, axis=0, keepdims=True)


def _norm_mod(name, x, norm_w, scale, shift):
    S, D = x.shape
    tr = _tile(S, 256)

    def body(x_ref, nw_ref, sc_ref, sh_ref, h_ref):
        xv = x_ref[...]
        r = lax.rsqrt(jnp.mean(xv * xv, axis=-1, keepdims=True) + EPS)
        h_ref[...] = ((xv * r * nw_ref[...]) * (1.0 + sc_ref[...]) + sh_ref[...]).astype(BF16)

    row = pl.BlockSpec((1, D), lambda i: (0, 0))
    til = pl.BlockSpec((tr, D), lambda i: (i, 0))
    return _pcall(body, name=name, grid=(S // tr,), in_specs=[til, row, row, row], out_specs=til,
                  out_shape=jax.ShapeDtypeStruct((S, D), BF16), compiler_params=_params(("parallel",)))(
                      x, norm_w, scale, shift)


def _norm_mod_bwd(name, dh, x, dres, norm_w, scale, gate_o=None):
    S, D = x.shape
    tr = _tile(S, 256)
    n_r = S // tr
    with_gate = gate_o is not None
    dh = list(dh) if isinstance(dh, (list, tuple)) else [dh]
    n_dh = len(dh)

    def body(*refs):
        dh_refs, refs = refs[:n_dh], refs[n_dh:]
        if with_gate:
            x_ref, dres_ref, nw_ref, sc_ref, o_ref, g_ref, dx_ref, p1, p2, p3, do_ref, p4 = refs
        else:
            x_ref, dres_ref, nw_ref, sc_ref, dx_ref, p1, p2, p3 = refs
        dhv = dh_refs[0][...] if n_dh == 1 else jnp.concatenate([r[...] for r in dh_refs], axis=1)
        xv, nw = x_ref[...], nw_ref[...]
        r = lax.rsqrt(jnp.mean(xv * xv, axis=-1, keepdims=True) + EPS)
        xh = xv * r
        p1[0] = _colsum(dhv)
        p2[0] = _colsum(dhv * (xh * nw))
        dn = dhv * (1.0 + sc_ref[...])
        p3[0] = _colsum(dn * xh)
        dxh = dn * nw
        dx = dres_ref[...] + r * (dxh - xh * jnp.mean(dxh * xh, axis=-1, keepdims=True))
        dx_ref[...] = dx
        if with_gate:
            do_ref[...] = (dx * g_ref[...]).astype(BF16)
            p4[0] = _colsum(dx * o_ref[...].astype(F32))

    row = pl.BlockSpec((1, D), lambda i: (0, 0))
    til = pl.BlockSpec((tr, D), lambda i: (i, 0))
    part = pl.BlockSpec((1, 1, D), lambda i: (i, 0, 0))
    part_shape = jax.ShapeDtypeStruct((n_r, 1, D), F32)
    in_specs = [pl.BlockSpec((tr, D // n_dh), lambda i: (i, 0))] * n_dh + [til, til, row, row]
    arrays = dh + [x, dres, norm_w, scale]
    out_specs = [til, part, part, part]
    out_shape = [jax.ShapeDtypeStruct((S, D), F32), part_shape, part_shape, part_shape]
    if with_gate:
        in_specs += [til, row]
        arrays += list(gate_o)
        out_specs += [til, part]
        out_shape += [jax.ShapeDtypeStruct((S, D), BF16), part_shape]
    return _pcall(body, name=name, grid=(n_r,), in_specs=in_specs, out_specs=out_specs, out_shape=out_shape,
                  compiler_params=_params(("parallel",)))(*arrays)


def _pool_w_specs(rows, cg):
    return [pl.BlockSpec((rows, cg), lambda g, j=j: (N_GROUPS * j + g, 0)) for j in range(N_CHIPS)]


def _pool_fwd(proj, wp_full, pool_scale, S, PW):
    cg = PW // N_GROUPS
    rows = cg // N_CHIPS
    T = _tile(S, POOL_T)
    n_t = S // T

    def body(u_ref, w0, w1, w2, w3, ps_ref, pooled_ref, pa_ref):
        g = pl.program_id(0)
        win = jnp.left_shift(2, g)
        w = jnp.concatenate([w0[...], w1[...], w2[...], w3[...]], axis=0)
        t_i = lax.broadcasted_iota(jnp.int32, (T, T), 0)
        j_i = lax.broadcasted_iota(jnp.int32, (T, T), 1)
        b_cur = ((j_i <= t_i) & (j_i > t_i - win)).astype(BF16)
        b_prev = (j_i - T > t_i - win).astype(BF16)
        row = lax.broadcasted_iota(jnp.int32, (T, 1), 0)
        for r in range(n_t):
            cur = u_ref[r * T:(r + 1) * T, :]
            ws = jnp.dot(b_cur, cur, preferred_element_type=F32)
            if r > 0:
                ws += jnp.dot(b_prev, u_ref[(r - 1) * T:r * T, :], preferred_element_type=F32)
            count = jnp.minimum(row + (r * T + 1), win).astype(F32)
            pooled = (ws / count - cur.astype(F32)).astype(BF16)
            pooled_ref[r * T:(r + 1) * T, :] = pooled
            mixed = jnp.dot(pooled, w, preferred_element_type=F32)
            pa_ref[r * T:(r + 1) * T, :] = (mixed * ps_ref[...]).astype(BF16)

    col = pl.BlockSpec((S, cg), lambda g: (0, g))
    return _pcall(
        body, name="pool_fwd", grid=(N_GROUPS,),
        in_specs=[col] + _pool_w_specs(rows, cg) + [pl.BlockSpec((1, cg), lambda g: (0, g))],
        out_specs=[col, col],
        out_shape=[jax.ShapeDtypeStruct((S, PW), BF16), jax.ShapeDtypeStruct((S, PW), BF16)],
        compiler_params=_params(("parallel",)),
    )(proj, wp_full, wp_full, wp_full, wp_full, pool_scale)


def _pool_bwd(dpa, pooled, wp_full, pool_scale, S, PW):
    cg = PW // N_GROUPS
    rows = cg // N_CHIPS
    T = _tile(S, POOL_T)
    n_t = S // T

    def body(dpa_ref, pooled_ref, w0, w1, w2, w3, ps_ref, du_ref, gw_ref, gs_ref, dp_s, dpc_s, dmx_s):
        g = pl.program_id(0)
        win = jnp.left_shift(2, g)
        w = jnp.concatenate([w0[...], w1[...], w2[...], w3[...]], axis=0)
        row = lax.broadcasted_iota(jnp.int32, (T, 1), 0)
        gs = jnp.zeros((1, cg), F32)
        for r in range(n_t):
            sl = slice(r * T, (r + 1) * T)
            mixed = jnp.dot(pooled_ref[sl, :], w, preferred_element_type=F32)
            dpa_t = dpa_ref[sl, :]
            gs += _colsum(dpa_t * mixed)
            dmx = (dpa_t * ps_ref[...]).astype(BF16)
            dmx_s[sl, :] = dmx
            dpo = lax.dot_general(dmx, w, (((1,), (1,)), ((), ())), preferred_element_type=F32)
            dp_s[sl, :] = dpo
            count = jnp.minimum(row + (r * T + 1), win).astype(F32)
            dpc_s[sl, :] = (dpo / count).astype(BF16)
        gs_ref[...] = gs
        gw = lax.dot_general(pooled_ref[...], dmx_s[...], (((0,), (0,)), ((), ())), preferred_element_type=F32)
        for j in range(N_CHIPS):
            gw_ref[j, 0] = gw[j * rows:(j + 1) * rows, :].astype(BF16)
        j_i = lax.broadcasted_iota(jnp.int32, (T, T), 0)
        t_i = lax.broadcasted_iota(jnp.int32, (T, T), 1)
        b_cur = ((t_i >= j_i) & (t_i < j_i + win)).astype(BF16)
        b_next = (t_i + T < j_i + win).astype(BF16)
        for r in range(n_t):
            sl = slice(r * T, (r + 1) * T)
            acc = jnp.dot(b_cur, dpc_s[sl, :], preferred_element_type=F32)
            if r + 1 < n_t:
                acc += jnp.dot(b_next, dpc_s[(r + 1) * T:(r + 2) * T, :], preferred_element_type=F32)
            du_ref[sl, :] = (acc - dp_s[sl, :]).astype(BF16)

    col = pl.BlockSpec((S, cg), lambda g: (0, g))
    return _pcall(
        body, name="pool_bwd", grid=(N_GROUPS,),
        in_specs=[col, col] + _pool_w_specs(rows, cg) + [pl.BlockSpec((1, cg), lambda g: (0, g))],
        out_specs=[col, pl.BlockSpec((N_CHIPS, 1, rows, cg), lambda g: (0, g, 0, 0)),
                   pl.BlockSpec((1, cg), lambda g: (0, g))],
        out_shape=[jax.ShapeDtypeStruct((S, PW), BF16),
                   jax.ShapeDtypeStruct((N_CHIPS, N_GROUPS, rows, cg), BF16),
                   jax.ShapeDtypeStruct((1, PW), F32)],
        scratch_shapes=[pltpu.VMEM((S, cg), F32), pltpu.VMEM((S, cg), BF16), pltpu.VMEM((S, cg), BF16)],
        compiler_params=_params(("parallel",)),
    )(dpa, pooled, wp_full, wp_full, wp_full, wp_full, pool_scale)


_NT = (((1,), (1,)), ((), ()))
_TN = (((0,), (0,)), ((), ()))


def _split_dot(v, tri):
    hi = v.astype(BF16)
    lo = (v - hi.astype(F32)).astype(BF16)
    return jnp.dot(hi, tri, preferred_element_type=F32) + jnp.dot(lo, tri, preferred_element_type=F32)


LOG2E = 1.4426950408889634
QK_SCALE = 1.0 / math.sqrt(HEAD_DIM)


def _sb_scores(q2_i, k_j, tri_l, masked):
    tq, tk = q2_i.shape[0], k_j.shape[0]
    s = lax.dot_general(q2_i, k_j, _NT, preferred_element_type=F32)
    lp = jnp.log(1.0 + jnp.exp2(-jnp.abs(s))) * LOG2E
    lb = jnp.minimum(s, 0.0) - lp
    l = lb - s
    mask = None
    if masked:
        mask = lax.broadcasted_iota(jnp.int32, (tq, tk), 0) > lax.broadcasted_iota(jnp.int32, (tq, tk), 1)
        l = jnp.where(mask, l, 0.0)
    return l, lb, lb + _split_dot(l, tri_l), mask


def _sb_weights(t, carry_l, mask):
    a = jnp.exp2(t + carry_l)
    return a if mask is None else jnp.where(mask, a, 0.0)


def _rowsum(v):
    return jnp.sum(v, axis=1, keepdims=True)


def _qk_norm(x_ref, w_ref):
    xv = x_ref[...].astype(F32)
    r = lax.rsqrt(jnp.mean(xv * xv, axis=-1, keepdims=True) + EPS)
    return xv * r, r


def _attn_fwd(proj, q_norm_w, k_norm_w, S, H, q_off, riders=()):
    t = _tile(S, ATT_T)
    n_q = S // t

    def body(q_ref, k_ref, v_ref, qw_ref, kw_ref, att_ref, attf_ref, qn_s, kn_s):
        qh, _ = _qk_norm(q_ref, qw_ref)
        qn_s[...] = (qh * qw_ref[...] * (QK_SCALE * LOG2E)).astype(BF16)
        kh, _ = _qk_norm(k_ref, kw_ref)
        kn_s[...] = (kh * kw_ref[...]).astype(BF16)
        tri_l = (lax.broadcasted_iota(jnp.int32, (t, t), 0) > lax.broadcasted_iota(jnp.int32, (t, t), 1)).astype(BF16)

        def rows(j):
            return pl.ds(pl.multiple_of(j * t, t), t)

        def q_step(i, _):
            q_i = qn_s[rows(i), :]

            def av(a, j):
                return jnp.dot(a.astype(BF16), v_ref[rows(j), :], preferred_element_type=F32)

            l, _, tt, mask = _sb_scores(q_i, kn_s[rows(i), :], tri_l, True)
            acc = av(_sb_weights(tt, 0.0, mask), i)
            carry = _rowsum(l)

            def single(_, c):
                carry, acc = c
                l, _, tt, _ = _sb_scores(q_i, kn_s[rows(i - 1), :], tri_l, False)
                return carry + _rowsum(l), acc + av(_sb_weights(tt, carry, None), i - 1)

            carry, acc = lax.fori_loop(0, i % 2, single, (carry, acc))
            top = i - 1 - i % 2

            def pair(p, c):
                carry, acc = c
                j0 = top - 2 * p
                l0, _, t0, _ = _sb_scores(q_i, kn_s[rows(j0), :], tri_l, False)
                l1, _, t1, _ = _sb_scores(q_i, kn_s[rows(j0 - 1), :], tri_l, False)
                mid = carry + _rowsum(l0)
                acc = acc + av(_sb_weights(t0, carry, None), j0) + av(_sb_weights(t1, mid, None), j0 - 1)
                return mid + _rowsum(l1), acc

            _, acc = lax.fori_loop(0, i // 2, pair, (carry, acc))
            att_ref[rows(i), :] = acc.astype(BF16)
            attf_ref[rows(i), :] = acc
            return 0

        lax.fori_loop(0, n_q, q_step, 0)

    def col(off):
        return pl.BlockSpec((S, HEAD_DIM), lambda h, off=off: (0, off + h))

    wspec = pl.BlockSpec((1, HEAD_DIM), lambda h: (0, 0))
    return _ride(
        "attn_fwd", body, riders, [proj, proj, proj, q_norm_w, k_norm_w], grid=(H,),
        in_specs=[col(q_off), col(q_off + H), col(q_off + 2 * H), wspec, wspec],
        out_specs=[col(0), col(0)],
        out_shape=[jax.ShapeDtypeStruct((S, H * HEAD_DIM), BF16), jax.ShapeDtypeStruct((S, H * HEAD_DIM), F32)],
        scratch_shapes=[pltpu.VMEM((S, HEAD_DIM), BF16), pltpu.VMEM((S, HEAD_DIM), BF16)],
        sem=("parallel",))


def _attn_bwd(proj, datt, attf, q_norm_w, k_norm_w, S, H, q_off, riders=()):
    t = _tile(S, ATT_T)
    n_q = S // t

    def body(q_ref, k_ref, v_ref, do_ref, o_ref, qw_ref, kw_ref, dq_ref, dk_ref, dv_ref, gq_ref, gk_ref,
             qn_s, kn_s, qz_s, kz_s, dk_s, dv_s, gq_s):
        qw, kw = qw_ref[...], kw_ref[...]
        qh, _ = _qk_norm(q_ref, qw_ref)
        qn_s[...] = (qh * qw * (QK_SCALE * LOG2E)).astype(BF16)
        qz_s[...] = (qh * qw * QK_SCALE).astype(BF16)
        kh, _ = _qk_norm(k_ref, kw_ref)
        kn_s[...] = (kh * kw).astype(BF16)
        kz_s[...] = (kh * kw * QK_SCALE).astype(BF16)
        dk_s[...] = jnp.zeros_like(dk_s)
        dv_s[...] = jnp.zeros_like(dv_s)
        gq_s[...] = jnp.zeros_like(gq_s)
        r_i = lax.broadcasted_iota(jnp.int32, (t, t), 0)
        c_i = lax.broadcasted_iota(jnp.int32, (t, t), 1)
        tri_l = (r_i > c_i).astype(BF16)
        tri_e = (r_i >= c_i).astype(BF16)

        def rows(j):
            return pl.ds(pl.multiple_of(j * t, t), t)

        def q_step(i, _):
            q_i = qn_s[rows(i), :]
            do_i = do_ref[rows(i), :]
            d_i = _rowsum(do_i.astype(F32) * o_ref[rows(i), :])

            def scores(j, masked):
                l, lb, tt, mask = _sb_scores(q_i, kn_s[rows(j), :], tri_l, masked)
                da = lax.dot_general(do_i, v_ref[rows(j), :], _NT, preferred_element_type=F32)
                return l, lb, tt, mask, da

            def grads(j, sc, carry_l, carry_e, dq_acc):
                l, lb, tt, mask, da = sc
                a_bf = _sb_weights(tt, carry_l, mask).astype(BF16)
                e = da * a_bf.astype(F32)
                p = (d_i - carry_e) - _split_dot(e, tri_e)
                dz = e - jnp.exp2(lb) * (e + p)
                if mask is not None:
                    dz = jnp.where(mask, dz, 0.0)
                dz = dz.astype(BF16)
                dk_s[rows(j), :] += lax.dot_general(dz, qz_s[rows(i), :], _TN, preferred_element_type=F32)
                dv_s[rows(j), :] += lax.dot_general(a_bf, do_i, _TN, preferred_element_type=F32)
                return (carry_l + _rowsum(l), carry_e + _rowsum(e),
                        dq_acc + jnp.dot(dz, kz_s[rows(j), :], preferred_element_type=F32))

            c = grads(i, scores(i, True), 0.0, 0.0, jnp.zeros((t, HEAD_DIM), F32))
            c = lax.fori_loop(0, i % 2, lambda _, c: grads(i - 1, scores(i - 1, False), *c), c)
            top = i - 1 - i % 2

            def pair(p, c):
                j0 = top - 2 * p
                s0, s1 = scores(j0, False), scores(j0 - 1, False)
                return grads(j0 - 1, s1, *grads(j0, s0, *c))

            _, _, dqn = lax.fori_loop(0, i // 2, pair, c)
            qv = q_ref[rows(i), :].astype(F32)
            r = lax.rsqrt(jnp.mean(qv * qv, axis=-1, keepdims=True) + EPS)
            xh = qv * r
            gq_s[...] += _colsum(dqn * xh)
            dxh = dqn * qw
            dq_ref[rows(i), :] = (r * (dxh - xh * jnp.mean(dxh * xh, axis=-1, keepdims=True))).astype(BF16)
            return 0

        lax.fori_loop(0, n_q, q_step, 0)
        gq_ref[0] = gq_s[...]
        kh, rk = _qk_norm(k_ref, kw_ref)
        dkn = dk_s[...]
        gk_ref[0] = _colsum(dkn * kh)
        dxh = dkn * kw
        dk_ref[...] = (rk * (dxh - kh * jnp.mean(dxh * kh, axis=-1, keepdims=True))).astype(BF16)
        dv_ref[...] = dv_s[...].astype(BF16)

    def col(off):
        return pl.BlockSpec((S, HEAD_DIM), lambda h, off=off: (0, off + h))

    wspec = pl.BlockSpec((1, HEAD_DIM), lambda h: (0, 0))
    gspec = pl.BlockSpec((1, 1, HEAD_DIM), lambda h: (h, 0, 0))
    act = jax.ShapeDtypeStruct((S, H * HEAD_DIM), BF16)
    gsh = jax.ShapeDtypeStruct((H, 1, HEAD_DIM), F32)
    return _ride(
        "attn_bwd", body, riders, [proj, proj, proj, datt, attf, q_norm_w, k_norm_w], grid=(H,),
        in_specs=[col(q_off), col(q_off + H), col(q_off + 2 * H), col(0), col(0), wspec, wspec],
        out_specs=[col(0), col(0), col(0), gspec, gspec],
        out_shape=[act, act, act, gsh, gsh],
        scratch_shapes=[pltpu.VMEM((S, HEAD_DIM), BF16)] * 4 + [pltpu.VMEM((S, HEAD_DIM), F32)] * 2
        + [pltpu.VMEM((1, HEAD_DIM), F32)],
        sem=("parallel",))


def _place():
    x, y, c = lax.axis_index("x"), lax.axis_index("y"), lax.axis_index("c")
    chips = [(1 - x, y), (x, 1 - y), (1 - x, 1 - y)]
    return x, y, c, chips


def _dev_allgather(name, v):
    m_per, n = v.shape

    def body(x_ref, out_ref, send_sems, recv_sems, local_sem):
        x, y, c, chips = _place()
        me, sibling = (x, y, c), (x, y, 1 - c)

        def rows(px, py, pc):
            return out_ref.at[pl.ds((4 * px + 2 * py + pc) * m_per, m_per), :]

        def copy(k, block, to, src=None):
            return pltpu.make_async_remote_copy(
                src_ref=rows(*block) if src is None else src, dst_ref=rows(*block),
                send_sem=send_sems.at[k], recv_sem=recv_sems.at[k], device_id=to, device_id_type=MESH)

        mine = pltpu.make_async_copy(x_ref, rows(*me), local_sem)
        mine.start()
        first = [copy(0, me, sibling, src=x_ref)]
        first += [copy(1 + j, me, (*chip, c), src=x_ref) for j, chip in enumerate(chips)]
        for cp in first:
            cp.start()
        passed = [copy(4 + j, (*chip, c), sibling) for j, chip in enumerate(chips)]
        for j, chip in enumerate(chips):
            copy(1 + j, (*chip, c), me).wait_recv()
            passed[j].start()
        copy(0, sibling, me).wait_recv()
        for j, chip in enumerate(chips):
            copy(4 + j, (*chip, 1 - c), me).wait_recv()
        for cp in first + passed:
            cp.wait_send()
        mine.wait()

    return _pcall(
        body, name=name, out_shape=jax.ShapeDtypeStruct((N_DEV * m_per, n), v.dtype),
        in_specs=[pl.BlockSpec(memory_space=pltpu.VMEM)], out_specs=pl.BlockSpec(memory_space=pltpu.VMEM),
        scratch_shapes=[pltpu.SemaphoreType.DMA((7,)), pltpu.SemaphoreType.DMA((7,)), pltpu.SemaphoreType.DMA],
        compiler_params=pltpu.CompilerParams(vmem_limit_bytes=VMEM_LIMIT_V7X),
    )(v)


class _W:
    def __init__(self, name, kind, R, C):
        self.name, self.kind, self.R, self.C = name, kind, R, C

    @property
    def shard_shape(self):
        return (self.R, self.C // N_CHIPS) if self.kind == "col" else (self.R // N_CHIPS, self.C)

    @property
    def half_rows(self):
        return self.shard_shape[0] // 2

    def shard_half(self, ref, half):
        return ref.at[pl.ds(half * self.half_rows, self.half_rows), :]

    def region(self, full_ref, chip, half):
        hr = self.half_rows
        if self.kind == "col":
            cw = self.C // N_CHIPS
            return full_ref.at[pl.ds(half * hr, hr), pl.ds(chip * cw, cw)]
        return full_ref.at[pl.ds(chip * (2 * hr) + half * hr, hr), :]

    def region_both(self, full_ref, chip):
        hr = self.half_rows
        if self.kind == "col":
            cw = self.C // N_CHIPS
            return full_ref.at[:, pl.ds(chip * cw, cw)]
        return full_ref.at[pl.ds(chip * (2 * hr), 2 * hr), :]


def _ag_rider(ws, fulls, n_ch=4, chunks=None):
    n_w = len(ws)
    lo, hi = chunks or (0, n_ch)
    per = 6

    def parts(full, sems):
        send_sems, recv_sems = sems
        x, y, c, _ = _place()
        xn, yn, dg = (1 - x, y), (x, 1 - y), (1 - x, 1 - y)
        via = (x + (1 - c) * (1 - 2 * x), y + c * (1 - 2 * y))
        to = (x + c * (1 - 2 * x), y + (1 - c) * (1 - 2 * y))

        def reg(i, chip, half, t):
            nr = ws[i].half_rows // n_ch
            return ws[i].region(full[i], 2 * chip[0] + chip[1], half).at[pl.ds(t * nr, nr), :]

        def copy(r, i, t, k, dev):
            s = (i * (hi - lo) + t - lo) * per + k
            return pltpu.make_async_remote_copy(src_ref=r, dst_ref=r, send_sem=send_sems.at[s],
                                                recv_sem=recv_sems.at[s], device_id=dev, device_id_type=MESH)

        def direct(i, t, k):
            return copy(reg(i, (x, y), c, t), i, t, k, (*(via, to)[k], c))

        def direct_in(i, t, k):
            return copy(reg(i, (via, to)[k], c, t), i, t, k, (*(via, to)[k], c))

        def relay(i, t):
            return copy(reg(i, via, c, t), i, t, 2, (*to, c))

        def relay_in(i, t):
            return copy(reg(i, dg, c, t), i, t, 2, (*to, c))

        def hand(i, t, k, half):
            return copy(reg(i, (xn, yn, dg)[k], half, t), i, t, 3 + k, (x, y, 1 - c))

        return c, direct, direct_in, relay, relay_in, hand

    def start(_, full, sems):
        _, direct, _, _, _, _ = parts(full, sems)
        for t in range(lo, hi):
            for i in range(n_w):
                direct(i, t, 0).start()
                direct(i, t, 1).start()

    def arrived(t):
        def step(_, full, sems):
            c, _, direct_in, relay, relay_in, hand = parts(full, sems)
            for i in range(n_w):
                direct_in(i, t, 0).wait_recv()
                direct_in(i, t, 1).wait_recv()
                relay(i, t).start()
                hand(i, t, 0, c).start()
                hand(i, t, 1, c).start()
                if t > lo:
                    relay_in(i, t - 1).wait_recv()
                    hand(i, t - 1, 2, c).start()
        return step

    def finish(_, full, sems):
        c, direct, _, relay, relay_in, hand = parts(full, sems)
        for i in range(n_w):
            relay_in(i, hi - 1).wait_recv()
            hand(i, hi - 1, 2, c).start()
        for i in range(n_w):
            for t in range(lo, hi):
                for k in range(3):
                    hand(i, t, k, 1 - c).wait_recv()
        for i in range(n_w):
            for t in range(lo, hi):
                direct(i, t, 0).wait_send()
                direct(i, t, 1).wait_send()
                relay(i, t).wait_send()
                for k in range(3):
                    hand(i, t, k, c).wait_send()

    n_sem = per * (hi - lo) * n_w
    return _Rider(fulls, [jax.ShapeDtypeStruct((w.R, w.C), BF16) for w in ws],
                  [pltpu.SemaphoreType.DMA((n_sem,)), pltpu.SemaphoreType.DMA((n_sem,))], start, finish,
                  steps=[arrived(t) for t in range(lo, hi)], aliases={i: i for i in range(n_w)})


def _cast_into_full(w, a32, chip_arr):
    sr, sc = w.shard_shape
    tr, tc = _tile(sr, 512), _tile(sc, 2048)
    n_r, n_c = sr // tr, sc // tc
    if w.kind == "col":
        out_spec = pl.BlockSpec((tr, tc), lambda i, j, chip: (i, chip[0] * n_c + j))
    else:
        out_spec = pl.BlockSpec((tr, tc), lambda i, j, chip: (chip[0] * n_r + i, j))

    def body(chip_ref, a_ref, o_ref):
        o_ref[...] = a_ref[...].astype(BF16)

    return _pcall(
        body, name="cast_" + w.name, out_shape=jax.ShapeDtypeStruct((w.R, w.C), BF16),
        grid_spec=pltpu.PrefetchScalarGridSpec(
            num_scalar_prefetch=1, grid=(n_r, n_c),
            in_specs=[pl.BlockSpec((tr, tc), lambda i, j, chip: (i, j))], out_specs=out_spec),
        compiler_params=_params(("parallel", "parallel")),
    )(chip_arr, a32)


def _half_view(w, g):
    return g if w.kind == "col" else g.reshape(N_CHIPS, w.R // N_CHIPS, w.C)


def _px_rider(ws, grads):
    n_w = len(ws)

    def copies(g, got, sems):
        send_sems, recv_sems = sems
        x, y, c, _ = _place()

        def half_all(w, ref, half):
            hr = w.half_rows
            if w.kind == "col":
                return ref.at[pl.ds(half * hr, hr), :]
            return ref.at[:, pl.ds(half * hr, hr), :]

        return [pltpu.make_async_remote_copy(
            src_ref=half_all(w, g[i], 1 - c), dst_ref=got[i], send_sem=send_sems.at[i], recv_sem=recv_sems.at[i],
            device_id=(x, y, 1 - c), device_id_type=MESH) for i, w in enumerate(ws)]

    def start(g, got, sems):
        for cp in copies(g, got, sems):
            cp.start()

    def finish(g, got, sems):
        for cp in copies(g, got, sems):
            cp.wait_recv()
            cp.wait_send()

    def got_shape(w):
        hr = w.half_rows
        return (hr, w.C) if w.kind == "col" else (N_CHIPS, hr, w.C)

    return _Rider([_half_view(w, g) for w, g in zip(ws, grads)],
                  [jax.ShapeDtypeStruct(got_shape(w), BF16) for w in ws],
                  [pltpu.SemaphoreType.DMA((n_w,)), pltpu.SemaphoreType.DMA((n_w,))], start, finish)


def _pair_sum(w, g, got, c_arr):
    hr = w.half_rows
    if w.kind == "col":
        tr, tc = _tile(hr, 512), _tile(w.C, 2048)
        n_r = hr // tr
        grid = (n_r, w.C // tc)
        g_spec = pl.BlockSpec((tr, tc), lambda i, j, c: (c[0] * n_r + i, j))
        o_spec = pl.BlockSpec((tr, tc), lambda i, j, c: (i, j))
    else:
        tr = _tile(hr, 512)
        n_r = hr // tr
        grid = (N_CHIPS, n_r)
        g_spec = pl.BlockSpec((1, tr, w.C), lambda s, i, c: (s, c[0] * n_r + i, 0))
        o_spec = pl.BlockSpec((1, tr, w.C), lambda s, i, c: (s, i, 0))

    def body(c_ref, g_ref, got_ref, out_ref):
        out_ref[...] = (g_ref[...].astype(F32) + got_ref[...].astype(F32)).astype(BF16)

    return _pcall(
        body, name="grad_pair_sum_" + w.name, out_shape=jax.ShapeDtypeStruct(got.shape, BF16),
        grid_spec=pltpu.PrefetchScalarGridSpec(num_scalar_prefetch=1, grid=grid, in_specs=[g_spec, o_spec],
                                               out_specs=o_spec),
        compiler_params=_params(("parallel", "parallel")),
    )(c_arr, _half_view(w, g), got)


def _cx_rider(ws, sums, part=(0, 1), q_in=None):
    n_w = len(ws)

    def parts(p, q, sems):
        send_sems, recv_sems = sems
        x, y, c, chips = _place()
        my_chip = 2 * x + y

        def rows(w, ref):
            nr = w.half_rows // part[1]
            return ref.at[pl.ds(part[0] * nr, nr), :]

        def piece(w, ref, chip):
            if w.kind == "col":
                cw = w.C // N_CHIPS
                return rows(w, ref.at[:, pl.ds(chip * cw, cw)])
            return rows(w, ref.at[chip])

        def copy(i, k, recv=False):
            chip = chips[k]
            to_chip = 2 * chip[0] + chip[1]
            return pltpu.make_async_remote_copy(
                src_ref=piece(ws[i], p[i], to_chip), dst_ref=rows(ws[i], q[i].at[to_chip if recv else my_chip]),
                send_sem=send_sems.at[3 * i + k], recv_sem=recv_sems.at[3 * i + k],
                device_id=(*chip, c), device_id_type=MESH)

        return copy

    both = [(i, k) for i in range(n_w) for k in range(N_CHIPS - 1)]

    def start(p, q, sems):
        copy = parts(p, q, sems)
        for i, k in both:
            copy(i, k).start()

    def finish(p, q, sems):
        copy = parts(p, q, sems)
        for i, k in both:
            copy(i, k, recv=True).wait_recv()
        for i, k in both:
            copy(i, k).wait_send()

    return _Rider(list(sums) + list(q_in or []),
                  [jax.ShapeDtypeStruct((N_CHIPS, w.half_rows, w.shard_shape[1]), BF16) for w in ws],
                  [pltpu.SemaphoreType.DMA((3 * n_w,)), pltpu.SemaphoreType.DMA((3 * n_w,))], start, finish,
                  aliases={n_w + i: i for i in range(n_w)} if q_in else None)


def _chip_sum(w, p, q, cc_arr):
    hr, cols = w.half_rows, w.shard_shape[1]
    tr, tc = _tile(hr, 512), _tile(cols, 2048)
    n_r, n_c = hr // tr, cols // tc

    def body(cc_ref, own, q1, q2, q3, out_ref):
        own_v = own[...] if w.kind == "col" else own[0]
        out_ref[...] = ((own_v.astype(F32) + q1[0].astype(F32)) + q2[0].astype(F32)) + q3[0].astype(F32)

    if w.kind == "col":
        own_spec = pl.BlockSpec((tr, tc), lambda i, j, cc: (i, cc[1] * n_c + j))
    else:
        own_spec = pl.BlockSpec((1, tr, tc), lambda i, j, cc: (cc[1], i, j))
    q_specs = [pl.BlockSpec((1, tr, tc), lambda i, j, cc, s=s: ((cc[1] + s) % N_CHIPS, i, j)) for s in (1, 2, 3)]
    return _pcall(
        body, name="grad_chip_sum_" + w.name, out_shape=jax.ShapeDtypeStruct(w.shard_shape, F32),
        grid_spec=pltpu.PrefetchScalarGridSpec(
            num_scalar_prefetch=1, grid=(n_r, n_c), in_specs=[own_spec] + q_specs,
            out_specs=pl.BlockSpec((tr, tc), lambda i, j, cc: (cc[0] * n_r + i, j))),
        compiler_params=_params(("parallel", "parallel")),
    )(cc_arr, p, q, q, q)


_SEM = pl.BlockSpec(memory_space=pltpu.SEMAPHORE)
_HBM = pl.BlockSpec(memory_space=pltpu.HBM)


def _cx_split_copies(ws, p, land, send_sems, recv_sems):
    x, y, c, chips = _place()
    my_chip = 2 * x + y
    pairs = []
    for i, w in enumerate(ws):
        for k, chip in enumerate(chips):
            to_chip = 2 * chip[0] + chip[1]
            src = p[i].at[:, pl.ds(to_chip * (w.C // N_CHIPS), w.C // N_CHIPS)] if w.kind == "col" else p[i].at[to_chip]
            kw = dict(send_sem=send_sems.at[3 * i + k], recv_sem=recv_sems.at[3 * i + k], device_id=(*chip, c),
                      device_id_type=MESH)
            pairs.append((pltpu.make_async_remote_copy(src_ref=src, dst_ref=land[i].at[my_chip], **kw),
                          pltpu.make_async_remote_copy(src_ref=src, dst_ref=land[i].at[to_chip], **kw)))
    return pairs


def _cx_start(ws, sums):
    n_w = len(ws)
    lands = [lax.empty((N_CHIPS, w.half_rows, w.shard_shape[1]), BF16) for w in ws]

    def body(*refs):
        p, land = refs[:n_w], refs[n_w:2 * n_w]
        for out, _ in _cx_split_copies(ws, p, land, refs[2 * n_w], refs[2 * n_w + 1]):
            out.start()
        refs[-1][...] = jnp.zeros_like(refs[-1])

    arrays = [pltpu.with_memory_space_constraint(a, pltpu.HBM) for a in list(sums) + lands]
    res = _pcall(
        body, name="grad_last_exchange_start",
        out_shape=(pltpu.SemaphoreType.DMA((3 * n_w,)), pltpu.SemaphoreType.DMA((3 * n_w,)),
                   *[pltpu.HBM(a.shape, a.dtype) for a in arrays], jax.ShapeDtypeStruct((8, 128), F32)),
        in_specs=[_HBM] * (2 * n_w),
        out_specs=(_SEM, _SEM, *[_HBM] * (2 * n_w), pl.BlockSpec(memory_space=pltpu.VMEM)),
        input_output_aliases={i: 2 + i for i in range(2 * n_w)},
        compiler_params=pltpu.CompilerParams(has_side_effects=pltpu.SideEffectType.DATAFLOW_SIDE_EFFECTING),
    )(*arrays)
    return res[0], res[1], list(res[2:2 + n_w]), list(res[2 + n_w:2 + 2 * n_w]), res[-1]


def _cx_wait(ws, send_sems, recv_sems, sums, lands, after):
    n_w = len(ws)

    def body(*refs):
        p, land = refs[:n_w], refs[n_w:2 * n_w]
        for _, cp in _cx_split_copies(ws, p, land, refs[2 * n_w], refs[2 * n_w + 1]):
            cp.wait_send()
            cp.wait_recv()

    res = _pcall(
        body, name="grad_last_exchange_wait",
        out_shape=[pltpu.HBM(a.shape, a.dtype) for a in list(sums) + list(lands)],
        in_specs=[_HBM] * (2 * n_w) + [_SEM, _SEM] + [ANY] * len(after), out_specs=[_HBM] * (2 * n_w),
        input_output_aliases={i: i for i in range(2 * n_w)},
        compiler_params=pltpu.CompilerParams(has_side_effects=pltpu.SideEffectType.DATAFLOW_SIDE_EFFECTING),
    )(*sums, *lands, send_sems, recv_sems, *after)
    return list(res[:n_w]), list(res[n_w:])


def _sf_rider(ws, grads):
    n_w = len(ws)

    def copy(g, sems, i, half):
        send_sems, recv_sems = sems
        x, y, c, _ = _place()
        h = c if half == "mine" else 1 - c
        reg = ws[i].shard_half(g[i], h)
        return pltpu.make_async_remote_copy(src_ref=reg, dst_ref=reg, send_sem=send_sems.at[i], recv_sem=recv_sems.at[i],
                                            device_id=(x, y, 1 - c), device_id_type=MESH)

    def start(_, g, sems):
        for i in range(n_w):
            copy(g, sems, i, "mine").start()

    def finish(_, g, sems):
        for i in range(n_w):
            copy(g, sems, i, "other").wait_recv()
            copy(g, sems, i, "mine").wait_send()

    return _Rider(grads, [jax.ShapeDtypeStruct(w.shard_shape, F32) for w in ws],
                  [pltpu.SemaphoreType.DMA((n_w,)), pltpu.SemaphoreType.DMA((n_w,))], start, finish,
                  aliases={i: i for i in range(n_w)})


def _adamw_math(w, g, m, v):
    m = ADAM_B1 * m + (1.0 - ADAM_B1) * g
    v = ADAM_B2 * v + (1.0 - ADAM_B2) * (g * g)
    m_hat = m / (1.0 - ADAM_B1 ** ADAM_STEP)
    v_hat = v / (1.0 - ADAM_B2 ** ADAM_STEP)
    delta = -ADAM_LR * (m_hat / (jnp.sqrt(v_hat) + ADAM_EPS) + ADAM_WD * w)
    return delta, m, v


def _adamw(name, w, g, m, v, after=None):
    R, C = w.shape
    tr, tc = _tile(R, 256), _tile(C, 2048)
    behind = [] if after is None else [after]

    def body(w_ref, g_ref, m_ref, v_ref, *rest):
        g_out, d_out, m_out, v_out = rest[len(behind):]
        g = g_ref[...]
        g_out[...] = g
        d_out[...], m_out[...], v_out[...] = _adamw_math(w_ref[...], g, m_ref[...], v_ref[...])

    spec = pl.BlockSpec((tr, tc), lambda i, j: (i, j))
    sh = jax.ShapeDtypeStruct((R, C), F32)
    return _pcall(body, name=name, grid=(R // tr, C // tc), in_specs=[spec] * 4 + [ANY] * len(behind),
                  out_specs=[spec] * 4, out_shape=[sh] * 4, compiler_params=_params(("parallel", "parallel")))(
                      w, g, m, v, *behind)


def _ada_update(sct, dmod_sh, w, m, v, riders=()):
    R, C = w.shape
    tr, tc = _tile(R, 256), _tile(C, 1024)

    def body(s_ref, d_ref, w_ref, m_ref, v_ref, g_out, d_out, m_out, v_out):
        s, d = s_ref[...], d_ref[...]
        g = s[:, 0:1] * d[0:1, :]
        for b in range(1, N_DEV):
            g += s[:, b:b + 1] * d[b:b + 1, :]
        g_out[...] = g
        d_out[...], m_out[...], v_out[...] = _adamw_math(w_ref[...], g, m_ref[...], v_ref[...])

    spec = pl.BlockSpec((tr, tc), lambda i, j: (i, j))
    sh = jax.ShapeDtypeStruct((R, C), F32)
    return _ride(
        "ada_update", body, riders, [sct, dmod_sh, w, m, v], grid=(R // tr, C // tc),
        in_specs=[pl.BlockSpec((tr, N_DEV), lambda i, j: (i, 0)), pl.BlockSpec((N_DEV, tc), lambda i, j: (0, j)),
                  spec, spec, spec],
        out_specs=[spec] * 4, out_shape=[sh] * 4, scratch_shapes=[], sem=("parallel", "parallel"))


def _silu_rows(c_row):
    D = c_row.shape[1]

    def body(c_ref, o_ref):
        cv = c_ref[...]
        o_ref[...] = cv * jax.nn.sigmoid(cv)

    return _pcall(body, name="silu_c", out_shape=jax.ShapeDtypeStruct((1, D), F32))(c_row)


def _pack_partials(parts, widths, total):
    n = len(widths)

    def body(*refs):
        loss_p, out_ref = refs[n], refs[n + 1]
        off = 0
        for ref, wd in zip(refs[:n], widths):
            out_ref[:, off:off + wd] = jnp.sum(ref[...], axis=0)
            off += wd
        loss = jnp.sum(jnp.sum(loss_p[...], axis=0), axis=1, keepdims=True)
        out_ref[:, off:off + 128] = jnp.broadcast_to(loss, (1, 128))
        if off + 128 < total:
            out_ref[:, off + 128:total] = jnp.zeros((1, total - off - 128), F32)

    return _pcall(body, name="pack_partials", out_shape=jax.ShapeDtypeStruct((1, total), F32))(*parts)


def _small_update(gathered, offsets, params, loss_off):
    n_p = len(params)

    def over_devices(g_ref, off, wd):
        blk = g_ref[:, off:off + wd]
        g = blk[0:1, :]
        for b in range(1, N_DEV):
            g = g + blk[b:b + 1, :]
        return g

    def body(*refs):
        g_ref = refs[0]
        prm = refs[1:1 + 3 * n_p]
        outs = refs[1 + 3 * n_p:]
        outs[4 * n_p][...] = over_devices(g_ref, loss_off, 128)
        for i, (off, wd) in enumerate(offsets):
            g = over_devices(g_ref, off, wd)
            w, m, v = prm[3 * i][...], prm[3 * i + 1][...], prm[3 * i + 2][...]
            outs[4 * i][...] = g
            outs[4 * i + 1][...], outs[4 * i + 2][...], outs[4 * i + 3][...] = _adamw_math(w, g, m, v)

    flat = [a for t in params for a in t]
    out_shape = [jax.ShapeDtypeStruct(t[0].shape, F32) for t in params for _ in range(4)]
    out_shape.append(jax.ShapeDtypeStruct((1, 128), F32))
    return _pcall(body, name="small_update", out_shape=out_shape)(gathered, *flat)


def kernel(x, c, w_ada, b_ada, norm1_w, w_in, q_norm_w, k_norm_w, w_pool, pool_scale, w_a_up, w_b_up, w_o, norm2_w, w_ff1, w_ff2, loss_target, m_w_ada, m_b_ada, m_norm1_w, m_w_in, m_q_norm_w, m_k_norm_w, m_w_pool, m_pool_scale, m_w_a_up, m_w_b_up, m_w_o, m_norm2_w, m_w_ff1, m_w_ff2, v_w_ada, v_b_ada, v_norm1_w, v_w_in, v_q_norm_w, v_k_norm_w, v_w_pool, v_pool_scale, v_w_a_up, v_w_b_up, v_w_o, v_norm2_w, v_w_ff1, v_w_ff2):
    _, S, D = x.shape
    PW = D // 2
    H = PW // HEAD_DIM
    cg = PW // N_GROUPS
    IN = w_in.shape[2] * N_CHIPS
    FF = w_ff1.shape[2] * N_CHIPS
    A_COLS = w_ada.shape[2]
    xi, yi, ci = lax.axis_index("x"), lax.axis_index("y"), lax.axis_index("c")
    chip = 2 * xi + yi
    dev = 2 * chip + ci
    c_arr = jnp.reshape(ci, (1,)).astype(jnp.int32)
    x2, tgt = x[0], loss_target[0]

    ws = [_W("w_in", "col", D, IN), _W("w_pool", "row", PW, cg), _W("w_a_up", "col", PW, D),
          _W("w_b_up", "col", PW, D), _W("w_o", "row", D, D), _W("w_ff1", "col", D, FF), _W("w_ff2", "row", FF, D)]
    w32 = [w_in[0], w_pool[0].reshape(cg, cg), w_a_up[0], w_b_up[0], w_o[0], w_ff1[0], w_ff2[0]]
    m32 = [m_w_in[0], m_w_pool[0].reshape(cg, cg), m_w_a_up[0], m_w_b_up[0], m_w_o[0], m_w_ff1[0], m_w_ff2[0]]
    v32 = [v_w_in[0], v_w_pool[0].reshape(cg, cg), v_w_a_up[0], v_w_b_up[0], v_w_o[0], v_w_ff1[0], v_w_ff2[0]]

    W_IN, W_POOL, W_A, W_B, W_O, W_FF1, W_FF2 = ws
    chip_arr = jnp.reshape(chip, (1,)).astype(jnp.int32)
    cc_arr = jnp.stack([ci, chip]).astype(jnp.int32)
    s_in, s_pool, s_a, s_b, s_o, s_ff1, s_ff2 = [_cast_into_full(w, a, chip_arr) for w, a in zip(ws, w32)]
    (win_f,) = _run_rider("gather_w_in", _ag_rider([W_IN], [s_in]))

    sc_row = _silu_rows(c)
    sc_all = _dev_allgather("gather_silu_c", sc_row.reshape(8, D // 8)).reshape(N_DEV, D)
    sc16 = jnp.concatenate([sc_all, jnp.zeros_like(sc_all)], axis=0)
    b_cols = lax.dynamic_slice(b_ada, (0, chip * A_COLS), (1, A_COLS))
    (mod_cols,) = _mm("mod_cols", [(sc16, w_ada[0])], M=2 * N_DEV, N=A_COLS, K=D, tm=16, tn=1024, tk=1024,
                      a_pro=lambda a: a.astype(BF16), b_pro=lambda b: b.astype(BF16),
                      extras=[(b_cols, "row", 0)], outs=[_tile_out(F32)], epi=lambda accs, ex: [accs[0] + ex[0]])
    mod_all = _dev_allgather("gather_mod", mod_cols[:N_DEV]).reshape(N_CHIPS, 2, N_DEV, A_COLS)
    mod_row = lax.dynamic_index_in_dim(mod_all[:, 0], dev, axis=1, keepdims=False).reshape(1, N_CHIPS * A_COLS)
    shift1, scale1, gate1, shift2, scale2, gate2 = [mod_row[:, i * D:(i + 1) * D] for i in range(6)]

    WIDE = dict(tm=2048, tn=512, tk=2048)
    DEEP = dict(tm=1024, tn=1024, tk=1024)
    h = _norm_mod("norm1_mod", x2, norm1_w, scale1, shift1)
    (proj,), ((wpool_f, wa_f, wb_f, wo_f),) = _mm(
        "in_proj", [(h, win_f)], M=S, N=IN, K=D, outs=[_tile_out(BF16)], epi=lambda accs, ex: [accs[0]], **WIDE,
        riders=[_ag_rider([W_POOL, W_A, W_B, W_O], [s_pool, s_a, s_b, s_o], n_ch=2)])
    pooled, pa = _pool_fwd(proj, wpool_f, pool_scale, S, PW)
    (att, attf), ((wff1_f,),) = _attn_fwd(proj, q_norm_w, k_norm_w, S, H, PW // HEAD_DIM,
                                          riders=[_ag_rider([W_FF1], [s_ff1], n_ch=8)])

    def merge_epi(accs, ex):
        sa, sb = jax.nn.sigmoid(ex[0].astype(F32)), jax.nn.sigmoid(ex[1].astype(F32))
        return [sa * accs[0] + sb * accs[1], accs[0], accs[1]]

    (merged, ya, yb), (ff2_a,) = _mm("branch_up_merge", [(pa, wa_f), (att, wb_f)], M=S, N=D, K=PW,
                                     extras=[(proj, "tile", 4 * PW), (proj, "tile", 4 * PW + D)],
                                     outs=[_tile_out(BF16)] * 3, epi=merge_epi,
                                     riders=[_ag_rider([W_FF2], [s_ff2], chunks=(0, 1))])
    (x1, o), (ff2_b,) = _mm("out_proj", [(merged, wo_f)], M=S, N=D, K=D, extras=[(x2, "tile", 0), (gate1, "row", 0)],
                            outs=[_tile_out(F32), _tile_out(BF16)], epi=lambda accs, ex: [ex[0] + ex[1] * accs[0], accs[0]],
                            riders=[_ag_rider([W_FF2], ff2_a, chunks=(1, 2))], **WIDE)
    h2 = _norm_mod("norm2_mod", x1, norm2_w, scale2, shift2)
    (rl,), ((wff2_f,),) = _mm("ff1", [(h2, wff1_f)], M=S, N=FF, K=D, outs=[_tile_out(BF16)], **WIDE,
                              epi=lambda accs, ex: [jnp.maximum(accs[0], 0.0)],
                              riders=[_ag_rider([W_FF2], ff2_b, chunks=(2, 4))])

    def square(a):
        af = a.astype(F32)
        return (af * af).astype(BF16)

    def loss_epi(accs, ex):
        x1_t, tgt_t, g2 = ex
        f = accs[0]
        diff = (x1_t + g2 * f) - tgt_t
        dy = diff * (1.0 / D)
        return [dy, dy * g2, _colsum(dy * f), _colsum(diff * diff)]

    dy, df, dgate2_p, loss_p = _mm("ff2_loss", [(rl, wff2_f)], M=S, N=D, K=FF, a_pro=square, tm=1024, tn=1024, tk=512,
                                   extras=[(x1, "tile", 0), (tgt, "tile", 0), (gate2, "row", 0)],
                                   outs=[_tile_out(F32), _tile_out(BF16), _COLSUM, _COLSUM], epi=loss_epi)

    def pair_sums(group, partials, got):
        return [_pair_sum(w, g, r, c_arr) for w, g, r in zip(group, partials, got)]

    def chip_sums(group, sums, from_chips):
        return [_chip_sum(w, p, q, cc_arr) for w, p, q in zip(group, sums, from_chips)]

    first = lambda accs, ex: [accs[0]]
    gmm = dict(ta=True, outs=[_tile_out(BF16)], epi=first, **WIDE)
    (g_ff2,) = _mm("grad_w_ff2", [(rl, df)], M=FF, N=D, K=S, a_pro=square, ta=True, tm=512, tn=2048, tk=2048,
                   outs=[_tile_out(BF16)], epi=first)
    (dz1,), (got_ff2,) = _mm("d_ff_hidden", [(df, wff2_f)], M=S, N=FF, K=D, tb=True, extras=[(rl, "tile", 0)], **WIDE,
                             outs=[_tile_out(BF16)], epi=lambda accs, ex: [accs[0] * (2.0 * ex[0].astype(F32))],
                             riders=[_px_rider([W_FF2], [g_ff2])])
    sum_ff2 = pair_sums([W_FF2], [g_ff2], got_ff2)
    (g_ff1,), (q_ff2,) = _mm("grad_w_ff1", [(h2, dz1)], M=D, N=FF, K=S,
                             riders=[_cx_rider([W_FF2], sum_ff2, part=(0, 2))], **gmm)
    (dh2,), (got_ff1, q_ff2) = _mm("d_h2", [(dz1, wff1_f)], M=S, N=D, K=FF, tb=True, outs=[_tile_out(F32)], epi=first,
                                   riders=[_px_rider([W_FF1], [g_ff1]),
                                           _cx_rider([W_FF2], sum_ff2, part=(1, 2), q_in=q_ff2)], **DEEP)
    sum_ff1 = pair_sums([W_FF1], [g_ff1], got_ff1)
    dx1, dshift2_p, dscale2_p, gn2_p, do, dgate1_p = _norm_mod_bwd("norm2_bwd", dh2, x1, dy, norm2_w, scale2,
                                                                   gate_o=(o, gate1))
    (g_wo,) = _mm("grad_w_o", [(merged, do)], M=D, N=D, K=S, **gmm)

    def gate_epi(accs, ex):
        dm = accs[0]
        sa, sb = jax.nn.sigmoid(ex[0].astype(F32)), jax.nn.sigmoid(ex[1].astype(F32))
        ya_t, yb_t = ex[2].astype(F32), ex[3].astype(F32)
        return [dm * sa, dm * sb, dm * ya_t * (sa * (1.0 - sa)), dm * yb_t * (sb * (1.0 - sb))]

    dya, dyb, dga, dgb = _mm("d_merged", [(do, wo_f)], M=S, N=D, K=D, tb=True, tm=1024, tn=512, tk=2048,
                             extras=[(proj, "tile", 4 * PW), (proj, "tile", 4 * PW + D), (ya, "tile", 0), (yb, "tile", 0)],
                             outs=[_tile_out(BF16)] * 4, epi=gate_epi)
    (g_wa,) = _mm("grad_w_a_up", [(pa, dya)], M=PW, N=D, K=S, **gmm)
    (g_wb,) = _mm("grad_w_b_up", [(att, dyb)], M=PW, N=D, K=S, **gmm)
    (dpa,) = _mm("d_pool_out", [(dya, wa_f)], M=S, N=PW, K=D, tb=True, outs=[_tile_out(F32)], epi=first, **WIDE)
    mid = [W_A, W_B, W_O]
    (datt,), (got_mid,) = _mm("d_att", [(dyb, wb_f)], M=S, N=PW, K=D, tb=True, outs=[_tile_out(BF16)], epi=first, **WIDE,
                              riders=[_px_rider(mid, [g_wa, g_wb, g_wo])])
    sum_mid = pair_sums(mid, [g_wa, g_wb, g_wo], got_mid)
    du, g_wpool4, gscale_p = _pool_bwd(dpa, pooled, wpool_f, pool_scale, S, PW)
    (dq, dk, dv, gq_p, gk_p), ((q_ff1,),) = _attn_bwd(
        proj, datt, attf, q_norm_w, k_norm_w, S, H, PW // HEAD_DIM, riders=[_cx_rider([W_FF1], sum_ff1)])
    dproj = jnp.concatenate([du, dq, dk, dv, dga, dgb], axis=1)
    early = [W_FF1, W_FF2]
    halves_early = chip_sums(early, sum_ff1 + sum_ff2, [q_ff1, q_ff2[0]])
    (g_win,), (grads_early, (q_wa, q_wb, q_wo)) = _mm(
        "grad_w_in", [(h, dproj)], M=D, N=IN, K=S, riders=[_sf_rider(early, halves_early), _cx_rider(mid, sum_mid)], **gmm)
    last = [W_IN, W_POOL]
    g_last = [g_win, g_wpool4.reshape(PW, cg)]
    (dh,), (got_last,) = _mm("d_h", [(dproj, win_f)], M=S, N=D, K=IN, tb=True, outs=[_tile_out(F32)], epi=first,
                             riders=[_px_rider(last, g_last)], **DEEP)
    sum_last = pair_sums(last, g_last, got_last)
    grad_x, dshift1_p, dscale1_p, gn1_p = _norm_mod_bwd("norm1_bwd", dh, x2, dx1, norm1_w, scale1)

    parts = [dshift1_p, dscale1_p, dgate1_p, dshift2_p, dscale2_p, dgate2_p, gn1_p, gn2_p,
             gscale_p.reshape(1, 1, PW), gq_p, gk_p]
    widths = [D] * 8 + [PW, HEAD_DIM, HEAD_DIM]
    used = sum(widths)
    P = -(-(used + 128) // 1024) * 1024
    packed = _pack_partials(parts + [loss_p], widths, P)
    gathered = _dev_allgather("gather_vector_grads", packed.reshape(8, P // 8)).reshape(N_DEV, P)
    sum_last, gathered = lax.optimization_barrier((sum_last, gathered))
    cx_send, cx_recv, sum_last, land_last, token = _cx_start(last, sum_last)
    small = [(b_ada, m_b_ada, v_b_ada), (norm1_w, m_norm1_w, v_norm1_w), (norm2_w, m_norm2_w, v_norm2_w),
             (pool_scale, m_pool_scale, v_pool_scale), (q_norm_w, m_q_norm_w, v_q_norm_w),
             (k_norm_w, m_k_norm_w, v_k_norm_w)]
    offsets = [(0, 6 * D), (6 * D, D), (7 * D, D), (8 * D, PW), (8 * D + PW, HEAD_DIM), (8 * D + PW + HEAD_DIM, HEAD_DIM)]
    su = _small_update(gathered, offsets, small, used)
    (g_b, d_b, nm_b, nv_b, g_n1, d_n1, nm_n1, nv_n1, g_n2, d_n2, nm_n2, nv_n2, g_ps, d_ps, nm_ps, nv_ps,
     g_qn, d_qn, nm_qn, nv_qn, g_kn, d_kn, nm_kn, nv_kn, loss_sum) = su
    dmod_sh = lax.dynamic_slice(gathered, (0, chip * A_COLS), (N_DEV, A_COLS))
    dmod_sh, token = lax.optimization_barrier((dmod_sh, token))
    g_ada, d_ada, nm_ada, nv_ada = _ada_update(sc_all.T, dmod_sh, w_ada[0], m_w_ada[0], v_w_ada[0])

    upd_early = [_adamw("adamw_" + w.name, a, g, m, v, after=token)
                 for w, a, g, m, v in zip(ws[5:], w32[5:], grads_early, m32[5:], v32[5:])]

    sum_last, q_last = _cx_wait(last, cx_send, cx_recv, sum_last, land_last,
                                after=[nv_ada] + [u[3] for u in upd_early])
    halves_late = chip_sums(last + mid, sum_last + sum_mid, q_last + [q_wa, q_wb, q_wo])
    filled = _run_rider("grad_sibling_fill", _sf_rider(last + mid, halves_late))
    upd = [_adamw("adamw_" + w.name, a, g, m, v) for w, a, g, m, v in zip(ws[:5], w32[:5], filled, m32[:5], v32[:5])]
    upd += upd_early

    loss = (0.5 / D) * loss_sum[0, 0]

    def up(a):
        return a[None]

    def pool4(a):
        return a.reshape(1, N_GROUPS, cg // N_CHIPS, cg)

    (gr_win, d_win, nm_win, nv_win), (gr_wp, d_wp, nm_wp, nv_wp), (gr_wa, d_wa, nm_wa, nv_wa), \
        (gr_wb, d_wb, nm_wb, nv_wb), (gr_wo, d_wo, nm_wo, nv_wo), (gr_f1, d_f1, nm_f1, nv_f1), \
        (gr_f2, d_f2, nm_f2, nv_f2) = upd
    return (
        loss, grad_x[None],
        up(g_ada), g_b, g_n1, up(gr_win), g_qn, g_kn, pool4(gr_wp), g_ps, up(gr_wa), up(gr_wb), up(gr_wo), g_n2,
        up(gr_f1), up(gr_f2),
        up(d_ada), d_b, d_n1, up(d_win), d_qn, d_kn, pool4(d_wp), d_ps, up(d_wa), up(d_wb), up(d_wo), d_n2,
        up(d_f1), up(d_f2),
        up(nm_ada), nm_b, nm_n1, up(nm_win), nm_qn, nm_kn, pool4(nm_wp), nm_ps, up(nm_wa), up(nm_wb), up(nm_wo), nm_n2,
        up(nm_f1), up(nm_f2),
        up(nv_ada), nv_b, nv_n1, up(nv_win), nv_qn, nv_kn, pool4(nv_wp), nv_ps, up(nv_wa), up(nv_wb), up(nv_wo), nv_n2,
        up(nv_f1), up(nv_f2),
    )
```

```python
import functools
import math

import jax
import jax.numpy as jnp
from jax import lax
from jax.experimental import pallas as pl
from jax.experimental.pallas import tpu as pltpu

F32 = jnp.float32
BF16 = jnp.bfloat16
MESH = pl.DeviceIdType.MESH
ANY = pl.BlockSpec(memory_space=pl.ANY)

EPS = 1e-6
HEAD_DIM = 128
POOL_WINDOWS = (2, 4, 8, 16)
N_GROUPS = len(POOL_WINDOWS)
N_CHIPS = 4
N_DEV = 8
ADAM_LR, ADAM_B1, ADAM_B2, ADAM_EPS, ADAM_WD, ADAM_STEP = 0.001, 0.9, 0.999, 1e-08, 0.01, 10
VMEM_LIMIT_V7X = 56 * 1024 * 1024
ATT_T = 256
POOL_T = 256


def _pcall(body, **kw):
    return pl.pallas_call(body, **kw)


def _params(sem=None):
    return pltpu.CompilerParams(dimension_semantics=sem, vmem_limit_bytes=VMEM_LIMIT_V7X)


def _tile(n, pref):
    if n <= pref:
        return n
    t = pref
    while n % t:
        t //= 2
    return t


class _Rider:
    def __init__(self, arrays, out_shape, sems, start, finish, aliases=None, steps=()):
        self.arrays, self.out_shape, self.sems = list(arrays), list(out_shape), list(sems)
        self.start, self.finish, self.aliases, self.steps = start, finish, aliases or {}, list(steps)


def _ride(name, body, riders, arrays, *, grid, in_specs, out_specs, out_shape, scratch_shapes, sem):
    n_in, n_out, n_scr = len(arrays), len(out_shape), len(scratch_shapes)
    r_arrays = [a for r in riders for a in r.arrays]
    r_outs = [o for r in riders for o in r.out_shape]
    r_sems = [s for r in riders for s in r.sems]
    n_hooks = max([len(r.steps) for r in riders], default=0)
    total = math.prod(grid)
    aliases, off_i, off_o = {}, n_in, n_out
    for r in riders:
        for a, o in r.aliases.items():
            aliases[off_i + a] = off_o + o
        off_i += len(r.arrays)
        off_o += len(r.out_shape)

    def full(*refs):
        p = 0
        groups = []
        for n in (n_in, len(r_arrays), n_out, len(r_outs), n_scr, len(r_sems)):
            groups.append(refs[p:p + n])
            p += n
        ins, rin, outs, rout, scr, rsem = groups

        def each(what):
            a = o = s = 0
            for r in riders:
                fn = what(r)
                if fn is not None:
                    fn(rin[a:a + len(r.arrays)], rout[o:o + len(r.out_shape)], rsem[s:s + len(r.sems)])
                a, o, s = a + len(r.arrays), o + len(r.out_shape), s + len(r.sems)

        if riders:
            lin = 0
            for d, g in enumerate(grid):
                lin = lin * g + pl.program_id(d)
            pl.when(lin == 0)(lambda: each(lambda r: r.start))
            for t in range(n_hooks):
                pl.when(lin == min(total - 1, ((t + 1) * total) // n_hooks))(
                    lambda t=t: each(lambda r: r.steps[t] if t < len(r.steps) else None))
        body(*ins, *outs, *scr)
        if riders:
            pl.when(lin == total - 1)(lambda: each(lambda r: r.finish))

    res = _pcall(
        full, name=name, grid=grid, in_specs=list(in_specs) + [ANY] * len(r_arrays),
        out_specs=list(out_specs) + [ANY] * len(r_outs), out_shape=list(out_shape) + r_outs,
        scratch_shapes=list(scratch_shapes) + r_sems, input_output_aliases=aliases,
        compiler_params=_params(("arbitrary",) * len(grid) if riders else sem),
    )(*arrays, *r_arrays)
    if not riders:
        return res
    main, rest, per = res[:n_out], res[n_out:], []
    for r in riders:
        per.append(rest[:len(r.out_shape)])
        rest = rest[len(r.out_shape):]
    return main, per


def _run_rider(name, rider):
    def body(*refs):
        n_a, n_o = len(rider.arrays), len(rider.out_shape)
        ins, outs, sems = refs[:n_a], refs[n_a:n_a + n_o], refs[n_a + n_o:]
        for fn in [rider.start] + rider.steps + [rider.finish]:
            fn(ins, outs, sems)

    return _pcall(body, name=name, out_shape=rider.out_shape, in_specs=[ANY] * len(rider.arrays),
                  out_specs=[ANY] * len(rider.out_shape), scratch_shapes=rider.sems,
                  input_output_aliases=rider.aliases)(*rider.arrays)


def _mm(name, pairs, *, M, N, K, ta=False, tb=False, tm=512, tn=1024, tk=1024,
        a_pro=None, b_pro=None, extras=(), outs, epi, riders=(), b_noff=0):
    tm, tn, tk = _tile(M, tm), _tile(N, tn), _tile(K, tk)
    n_i, n_j, n_k = M // tm, N // tn, K // tk
    n_p, n_e = len(pairs), len(extras)
    arrays, in_specs = [], []
    for a, _ in pairs:
        arrays.append(a)
        in_specs.append(pl.BlockSpec((tk, tm), lambda i, j, k: (k, i)) if ta
                        else pl.BlockSpec((tm, tk), lambda i, j, k: (i, k)))
    for _, b in pairs:
        arrays.append(b)
        in_specs.append(pl.BlockSpec((tn, tk), lambda i, j, k: (j + b_noff // tn, k)) if tb
                        else pl.BlockSpec((tk, tn), lambda i, j, k: (k, j + b_noff // tn)))
    for arr, kind, off in extras:
        ob = off // tn
        assert off % tn == 0
        arrays.append(arr)
        if kind == "tile":
            in_specs.append(pl.BlockSpec((tm, tn), lambda i, j, k, ob=ob: (i, j + ob)))
        else:
            in_specs.append(pl.BlockSpec((1, tn), lambda i, j, k, ob=ob: (0, j + ob)))
    out_shape, out_specs = [], []
    for o in outs:
        if o["kind"] == "tile":
            out_shape.append(jax.ShapeDtypeStruct((M, N), o["dtype"]))
            out_specs.append(pl.BlockSpec((tm, tn), lambda i, j, k: (i, j)))
        else:
            out_shape.append(jax.ShapeDtypeStruct((n_i, 1, N), F32))
            out_specs.append(pl.BlockSpec((1, 1, tn), lambda i, j, k: (i, 0, j)))
    dims = (((0 if ta else 1,), (1 if tb else 0,)), ((), ()))

    def body(*refs):
        a_refs, b_refs = refs[:n_p], refs[n_p:2 * n_p]
        e_refs = refs[2 * n_p:2 * n_p + n_e]
        o_refs = refs[2 * n_p + n_e:2 * n_p + n_e + len(outs)]
        acc_refs = refs[2 * n_p + n_e + len(outs):]

        def product(p):
            a, b = a_refs[p][...], b_refs[p][...]
            if a_pro is not None:
                a = a_pro(a)
            if b_pro is not None:
                b = b_pro(b)
            return lax.dot_general(a, b, dims, preferred_element_type=F32)

        def write(accs):
            vals = epi(accs, [e[...] for e in e_refs])
            for o, o_ref, val in zip(outs, o_refs, vals):
                if o["kind"] == "tile":
                    o_ref[...] = val.astype(o_ref.dtype)
                else:
                    o_ref[0] = val

        if n_k == 1:
            write([product(p) for p in range(n_p)])
            return
        k = pl.program_id(2)

        @pl.when(k == 0)
        def _():
            for acc in acc_refs:
                acc[...] = jnp.zeros_like(acc)

        for p in range(n_p):
            acc_refs[p][...] += product(p)

        pl.when(k == n_k - 1)(lambda: write([acc[...] for acc in acc_refs]))

    return _ride(name, body, riders, arrays, grid=(n_i, n_j, n_k), in_specs=in_specs, out_specs=out_specs,
                 out_shape=out_shape, scratch_shapes=[pltpu.VMEM((tm, tn), F32) for _ in pairs] if n_k > 1 else [],
                 sem=("parallel", "parallel", "arbitrary"))


def _tile_out(dtype):
    return {"kind": "tile", "dtype": dtype}


_COLSUM = {"kind": "colsum"}


def _colsum(v):
    return jnp.sum(v, axis=0, keepdims=True)


def _norm_mod(name, x, norm_w, scale, shift):
    S, D = x.shape
    tr = _tile(S, 256)

    def body(x_ref, nw_ref, sc_ref, sh_ref, h_ref):
        xv = x_ref[...]
        r = lax.rsqrt(jnp.mean(xv * xv, axis=-1, keepdims=True) + EPS)
        h_ref[...] = ((xv * r * nw_ref[...]) * (1.0 + sc_ref[...]) + sh_ref[...]).astype(BF16)

    row = pl.BlockSpec((1, D), lambda i: (0, 0))
    til = pl.BlockSpec((tr, D), lambda i: (i, 0))
    return _pcall(body, name=name, grid=(S // tr,), in_specs=[til, row, row, row], out_specs=til,
                  out_shape=jax.ShapeDtypeStruct((S, D), BF16), compiler_params=_params(("parallel",)))(
                      x, norm_w, scale, shift)


def _norm_mod_bwd(name, dh, x, dres, norm_w, scale, gate_o=None):
    S, D = x.shape
    tr = _tile(S, 256)
    n_r = S // tr
    with_gate = gate_o is not None
    dh = list(dh) if isinstance(dh, (list, tuple)) else [dh]
    n_dh = len(dh)

    def body(*refs):
        dh_refs, refs = refs[:n_dh], refs[n_dh:]
        if with_gate:
            x_ref, dres_ref, nw_ref, sc_ref, o_ref, g_ref, dx_ref, p1, p2, p3, do_ref, p4 = refs
        else:
            x_ref, dres_ref, nw_ref, sc_ref, dx_ref, p1, p2, p3 = refs
        dhv = dh_refs[0][...] if n_dh == 1 else jnp.concatenate([r[...] for r in dh_refs], axis=1)
        xv, nw = x_ref[...], nw_ref[...]
        r = lax.rsqrt(jnp.mean(xv * xv, axis=-1, keepdims=True) + EPS)
        xh = xv * r
        p1[0] = _colsum(dhv)
        p2[0] = _colsum(dhv * (xh * nw))
        dn = dhv * (1.0 + sc_ref[...])
        p3[0] = _colsum(dn * xh)
        dxh = dn * nw
        dx = dres_ref[...] + r * (dxh - xh * jnp.mean(dxh * xh, axis=-1, keepdims=True))
        dx_ref[...] = dx
        if with_gate:
            do_ref[...] = (dx * g_ref[...]).astype(BF16)
            p4[0] = _colsum(dx * o_ref[...].astype(F32))

    row = pl.BlockSpec((1, D), lambda i: (0, 0))
    til = pl.BlockSpec((tr, D), lambda i: (i, 0))
    part = pl.BlockSpec((1, 1, D), lambda i: (i, 0, 0))
    part_shape = jax.ShapeDtypeStruct((n_r, 1, D), F32)
    in_specs = [pl.BlockSpec((tr, D // n_dh), lambda i: (i, 0))] * n_dh + [til, til, row, row]
    arrays = dh + [x, dres, norm_w, scale]
    out_specs = [til, part, part, part]
    out_shape = [jax.ShapeDtypeStruct((S, D), F32), part_shape, part_shape, part_shape]
    if with_gate:
        in_specs += [til, row]
        arrays += list(gate_o)
        out_specs += [til, part]
        out_shape += [jax.ShapeDtypeStruct((S, D), BF16), part_shape]
    return _pcall(body, name=name, grid=(n_r,), in_specs=in_specs, out_specs=out_specs, out_shape=out_shape,
                  compiler_params=_params(("parallel",)))(*arrays)


def _pool_w_specs(rows, cg):
    return [pl.BlockSpec((rows, cg), lambda g, j=j: (N_GROUPS * j + g, 0)) for j in range(N_CHIPS)]


def _pool_fwd(proj, wp_full, pool_scale, S, PW):
    cg = PW // N_GROUPS
    rows = cg // N_CHIPS
    T = _tile(S, POOL_T)
    n_t = S // T

    def body(u_ref, w0, w1, w2, w3, ps_ref, pooled_ref, pa_ref):
        g = pl.program_id(0)
        win = jnp.left_shift(2, g)
        w = jnp.concatenate([w0[...], w1[...], w2[...], w3[...]], axis=0)
        t_i = lax.broadcasted_iota(jnp.int32, (T, T), 0)
        j_i = lax.broadcasted_iota(jnp.int32, (T, T), 1)
        b_cur = ((j_i <= t_i) & (j_i > t_i - win)).astype(BF16)
        b_prev = (j_i - T > t_i - win).astype(BF16)
        row = lax.broadcasted_iota(jnp.int32, (T, 1), 0)
        for r in range(n_t):
            cur = u_ref[r * T:(r + 1) * T, :]
            ws = jnp.dot(b_cur, cur, preferred_element_type=F32)
            if r > 0:
                ws += jnp.dot(b_prev, u_ref[(r - 1) * T:r * T, :], preferred_element_type=F32)
            count = jnp.minimum(row + (r * T + 1), win).astype(F32)
            pooled = (ws / count - cur.astype(F32)).astype(BF16)
            pooled_ref[r * T:(r + 1) * T, :] = pooled
            mixed = jnp.dot(pooled, w, preferred_element_type=F32)
            pa_ref[r * T:(r + 1) * T, :] = (mixed * ps_ref[...]).astype(BF16)

    col = pl.BlockSpec((S, cg), lambda g: (0, g))
    return _pcall(
        body, name="pool_fwd", grid=(N_GROUPS,),
        in_specs=[col] + _pool_w_specs(rows, cg) + [pl.BlockSpec((1, cg), lambda g: (0, g))],
        out_specs=[col, col],
        out_shape=[jax.ShapeDtypeStruct((S, PW), BF16), jax.ShapeDtypeStruct((S, PW), BF16)],
        compiler_params=_params(("parallel",)),
    )(proj, wp_full, wp_full, wp_full, wp_full, pool_scale)


def _pool_bwd(dpa, pooled, wp_full, pool_scale, S, PW):
    cg = PW // N_GROUPS
    rows = cg // N_CHIPS
    T = _tile(S, POOL_T)
    n_t = S // T

    def body(dpa_ref, pooled_ref, w0, w1, w2, w3, ps_ref, du_ref, gw_ref, gs_ref, dp_s, dpc_s, dmx_s):
        g = pl.program_id(0)
        win = jnp.left_shift(2, g)
        w = jnp.concatenate([w0[...], w1[...], w2[...], w3[...]], axis=0)
        row = lax.broadcasted_iota(jnp.int32, (T, 1), 0)
        gs = jnp.zeros((1, cg), F32)
        for r in range(n_t):
            sl = slice(r * T, (r + 1) * T)
            mixed = jnp.dot(pooled_ref[sl, :], w, preferred_element_type=F32)
            dpa_t = dpa_ref[sl, :]
            gs += _colsum(dpa_t * mixed)
            dmx = (dpa_t * ps_ref[...]).astype(BF16)
            dmx_s[sl, :] = dmx
            dpo = lax.dot_general(dmx, w, (((1,), (1,)), ((), ())), preferred_element_type=F32)
            dp_s[sl, :] = dpo
            count = jnp.minimum(row + (r * T + 1), win).astype(F32)
            dpc_s[sl, :] = (dpo / count).astype(BF16)
        gs_ref[...] = gs
        gw = lax.dot_general(pooled_ref[...], dmx_s[...], (((0,), (0,)), ((), ())), preferred_element_type=F32)
        for j in range(N_CHIPS):
            gw_ref[j, 0] = gw[j * rows:(j + 1) * rows, :].astype(BF16)
        j_i = lax.broadcasted_iota(jnp.int32, (T, T), 0)
        t_i = lax.broadcasted_iota(jnp.int32, (T, T), 1)
        b_cur = ((t_i >= j_i) & (t_i < j_i + win)).astype(BF16)
        b_next = (t_i + T < j_i + win).astype(BF16)
        for r in range(n_t):
            sl = slice(r * T, (r + 1) * T)
            acc = jnp.dot(b_cur, dpc_s[sl, :], preferred_element_type=F32)
            if r + 1 < n_t:
                acc += jnp.dot(b_next, dpc_s[(r + 1) * T:(r + 2) * T, :], preferred_element_type=F32)
            du_ref[sl, :] = (acc - dp_s[sl, :]).astype(BF16)

    col = pl.BlockSpec((S, cg), lambda g: (0, g))
    return _pcall(
        body, name="pool_bwd", grid=(N_GROUPS,),
        in_specs=[col, col] + _pool_w_specs(rows, cg) + [pl.BlockSpec((1, cg), lambda g: (0, g))],
        out_specs=[col, pl.BlockSpec((N_CHIPS, 1, rows, cg), lambda g: (0, g, 0, 0)),
                   pl.BlockSpec((1, cg), lambda g: (0, g))],
        out_shape=[jax.ShapeDtypeStruct((S, PW), BF16),
                   jax.ShapeDtypeStruct((N_CHIPS, N_GROUPS, rows, cg), BF16),
                   jax.ShapeDtypeStruct((1, PW), F32)],
        scratch_shapes=[pltpu.VMEM((S, cg), F32), pltpu.VMEM((S, cg), BF16), pltpu.VMEM((S, cg), BF16)],
        compiler_params=_params(("parallel",)),
    )(dpa, pooled, wp_full, wp_full, wp_full, wp_full, pool_scale)


_NT = (((1,), (1,)), ((), ()))
_TN = (((0,), (0,)), ((), ()))


def _split_dot(v, tri):
    hi = v.astype(BF16)
    lo = (v - hi.astype(F32)).astype(BF16)
    return jnp.dot(hi, tri, preferred_element_type=F32) + jnp.dot(lo, tri, preferred_element_type=F32)


LOG2E = 1.4426950408889634
QK_SCALE = 1.0 / math.sqrt(HEAD_DIM)


def _sb_scores(q2_i, k_j, tri_l, masked):
    tq, tk = q2_i.shape[0], k_j.shape[0]
    s = lax.dot_general(q2_i, k_j, _NT, preferred_element_type=F32)
    lp = jnp.log(1.0 + jnp.exp2(-jnp.abs(s))) * LOG2E
    lb = jnp.minimum(s, 0.0) - lp
    l = lb - s
    mask = None
    if masked:
        mask = lax.broadcasted_iota(jnp.int32, (tq, tk), 0) > lax.broadcasted_iota(jnp.int32, (tq, tk), 1)
        l = jnp.where(mask, l, 0.0)
    return l, lb, lb + _split_dot(l, tri_l), mask


def _sb_weights(t, carry_l, mask):
    a = jnp.exp2(t + carry_l)
    return a if mask is None else jnp.where(mask, a, 0.0)


def _rowsum(v):
    return jnp.sum(v, axis=1, keepdims=True)


def _qk_norm(x_ref, w_ref):
    xv = x_ref[...].astype(F32)
    r = lax.rsqrt(jnp.mean(xv * xv, axis=-1, keepdims=True) + EPS)
    return xv * r, r


def _attn_fwd(proj, q_norm_w, k_norm_w, S, H, q_off, riders=()):
    t = _tile(S, ATT_T)
    n_q = S // t

    def body(q_ref, k_ref, v_ref, qw_ref, kw_ref, att_ref, attf_ref, qn_s, kn_s):
        qh, _ = _qk_norm(q_ref, qw_ref)
        qn_s[...] = (qh * qw_ref[...] * (QK_SCALE * LOG2E)).astype(BF16)
        kh, _ = _qk_norm(k_ref, kw_ref)
        kn_s[...] = (kh * kw_ref[...]).astype(BF16)
        tri_l = (lax.broadcasted_iota(jnp.int32, (t, t), 0) > lax.broadcasted_iota(jnp.int32, (t, t), 1)).astype(BF16)

        def rows(j):
            return pl.ds(pl.multiple_of(j * t, t), t)

        def q_step(i, _):
            q_i = qn_s[rows(i), :]

            def av(a, j):
                return jnp.dot(a.astype(BF16), v_ref[rows(j), :], preferred_element_type=F32)

            l, _, tt, mask = _sb_scores(q_i, kn_s[rows(i), :], tri_l, True)
            acc = av(_sb_weights(tt, 0.0, mask), i)
            carry = _rowsum(l)

            def single(_, c):
                carry, acc = c
                l, _, tt, _ = _sb_scores(q_i, kn_s[rows(i - 1), :], tri_l, False)
                return carry + _rowsum(l), acc + av(_sb_weights(tt, carry, None), i - 1)

            carry, acc = lax.fori_loop(0, i % 2, single, (carry, acc))
            top = i - 1 - i % 2

            def pair(p, c):
                carry, acc = c
                j0 = top - 2 * p
                l0, _, t0, _ = _sb_scores(q_i, kn_s[rows(j0), :], tri_l, False)
                l1, _, t1, _ = _sb_scores(q_i, kn_s[rows(j0 - 1), :], tri_l, False)
                mid = carry + _rowsum(l0)
                acc = acc + av(_sb_weights(t0, carry, None), j0) + av(_sb_weights(t1, mid, None), j0 - 1)
                return mid + _rowsum(l1), acc

            _, acc = lax.fori_loop(0, i // 2, pair, (carry, acc))
            att_ref[rows(i), :] = acc.astype(BF16)
            attf_ref[rows(i), :] = acc
            return 0

        lax.fori_loop(0, n_q, q_step, 0)

    def col(off):
        return pl.BlockSpec((S, HEAD_DIM), lambda h, off=off: (0, off + h))

    wspec = pl.BlockSpec((1, HEAD_DIM), lambda h: (0, 0))
    return _ride(
        "attn_fwd", body, riders, [proj, proj, proj, q_norm_w, k_norm_w], grid=(H,),
        in_specs=[col(q_off), col(q_off + H), col(q_off + 2 * H), wspec, wspec],
        out_specs=[col(0), col(0)],
        out_shape=[jax.ShapeDtypeStruct((S, H * HEAD_DIM), BF16), jax.ShapeDtypeStruct((S, H * HEAD_DIM), F32)],
        scratch_shapes=[pltpu.VMEM((S, HEAD_DIM), BF16), pltpu.VMEM((S, HEAD_DIM), BF16)],
        sem=("parallel",))


def _attn_bwd(proj, datt, attf, q_norm_w, k_norm_w, S, H, q_off, riders=()):
    t = _tile(S, ATT_T)
    n_q = S // t

    def body(q_ref, k_ref, v_ref, do_ref, o_ref, qw_ref, kw_ref, dq_ref, dk_ref, dv_ref, gq_ref, gk_ref,
             qn_s, kn_s, qz_s, kz_s, dk_s, dv_s, gq_s):
        qw, kw = qw_ref[...], kw_ref[...]
        qh, _ = _qk_norm(q_ref, qw_ref)
        qn_s[...] = (qh * qw * (QK_SCALE * LOG2E)).astype(BF16)
        qz_s[...] = (qh * qw * QK_SCALE).astype(BF16)
        kh, _ = _qk_norm(k_ref, kw_ref)
        kn_s[...] = (kh * kw).astype(BF16)
        kz_s[...] = (kh * kw * QK_SCALE).astype(BF16)
        dk_s[...] = jnp.zeros_like(dk_s)
        dv_s[...] = jnp.zeros_like(dv_s)
        gq_s[...] = jnp.zeros_like(gq_s)
        r_i = lax.broadcasted_iota(jnp.int32, (t, t), 0)
        c_i = lax.broadcasted_iota(jnp.int32, (t, t), 1)
        tri_l = (r_i > c_i).astype(BF16)
        tri_e = (r_i >= c_i).astype(BF16)

        def rows(j):
            return pl.ds(pl.multiple_of(j * t, t), t)

        def q_step(i, _):
            q_i = qn_s[rows(i), :]
            do_i = do_ref[rows(i), :]
            d_i = _rowsum(do_i.astype(F32) * o_ref[rows(i), :])

            def scores(j, masked):
                l, lb, tt, mask = _sb_scores(q_i, kn_s[rows(j), :], tri_l, masked)
                da = lax.dot_general(do_i, v_ref[rows(j), :], _NT, preferred_element_type=F32)
                return l, lb, tt, mask, da

            def grads(j, sc, carry_l, carry_e, dq_acc):
                l, lb, tt, mask, da = sc
                a_bf = _sb_weights(tt, carry_l, mask).astype(BF16)
                e = da * a_bf.astype(F32)
                p = (d_i - carry_e) - _split_dot(e, tri_e)
                dz = e - jnp.exp2(lb) * (e + p)
                if mask is not None:
                    dz = jnp.where(mask, dz, 0.0)
                dz = dz.astype(BF16)
                dk_s[rows(j), :] += lax.dot_general(dz, qz_s[rows(i), :], _TN, preferred_element_type=F32)
                dv_s[rows(j), :] += lax.dot_general(a_bf, do_i, _TN, preferred_element_type=F32)
                return (carry_l + _rowsum(l), carry_e + _rowsum(e),
                        dq_acc + jnp.dot(dz, kz_s[rows(j), :], preferred_element_type=F32))

            c = grads(i, scores(i, True), 0.0, 0.0, jnp.zeros((t, HEAD_DIM), F32))
            c = lax.fori_loop(0, i % 2, lambda _, c: grads(i - 1, scores(i - 1, False), *c), c)
            top = i - 1 - i % 2

            def pair(p, c):
                j0 = top - 2 * p
                s0, s1 = scores(j0, False), scores(j0 - 1, False)
                return grads(j0 - 1, s1, *grads(j0, s0, *c))

            _, _, dqn = lax.fori_loop(0, i // 2, pair, c)
            qv = q_ref[rows(i), :].astype(F32)
            r = lax.rsqrt(jnp.mean(qv * qv, axis=-1, keepdims=True) + EPS)
            xh = qv * r
            gq_s[...] += _colsum(dqn * xh)
            dxh = dqn * qw
            dq_ref[rows(i), :] = (r * (dxh - xh * jnp.mean(dxh * xh, axis=-1, keepdims=True))).astype(BF16)
            return 0

        lax.fori_loop(0, n_q, q_step, 0)
        gq_ref[0] = gq_s[...]
        kh, rk = _qk_norm(k_ref, kw_ref)
        dkn = dk_s[...]
        gk_ref[0] = _colsum(dkn * kh)
        dxh = dkn * kw
        dk_ref[...] = (rk * (dxh - kh * jnp.mean(dxh * kh, axis=-1, keepdims=True))).astype(BF16)
        dv_ref[...] = dv_s[...].astype(BF16)

    def col(off):
        return pl.BlockSpec((S, HEAD_DIM), lambda h, off=off: (0, off + h))

    wspec = pl.BlockSpec((1, HEAD_DIM), lambda h: (0, 0))
    gspec = pl.BlockSpec((1, 1, HEAD_DIM), lambda h: (h, 0, 0))
    act = jax.ShapeDtypeStruct((S, H * HEAD_DIM), BF16)
    gsh = jax.ShapeDtypeStruct((H, 1, HEAD_DIM), F32)
    return _ride(
        "attn_bwd", body, riders, [proj, proj, proj, datt, attf, q_norm_w, k_norm_w], grid=(H,),
        in_specs=[col(q_off), col(q_off + H), col(q_off + 2 * H), col(0), col(0), wspec, wspec],
        out_specs=[col(0), col(0), col(0), gspec, gspec],
        out_shape=[act, act, act, gsh, gsh],
        scratch_shapes=[pltpu.VMEM((S, HEAD_DIM), BF16)] * 4 + [pltpu.VMEM((S, HEAD_DIM), F32)] * 2
        + [pltpu.VMEM((1, HEAD_DIM), F32)],
        sem=("parallel",))


def _place():
    x, y, c = lax.axis_index("x"), lax.axis_index("y"), lax.axis_index("c")
    chips = [(1 - x, y), (x, 1 - y), (1 - x, 1 - y)]
    return x, y, c, chips


def _dev_allgather(name, v):
    m_per, n = v.shape

    def body(x_ref, out_ref, send_sems, recv_sems, local_sem):
        x, y, c, chips = _place()
        me, sibling = (x, y, c), (x, y, 1 - c)

        def rows(px, py, pc):
            return out_ref.at[pl.ds((4 * px + 2 * py + pc) * m_per, m_per), :]

        def copy(k, block, to, src=None):
            return pltpu.make_async_remote_copy(
                src_ref=rows(*block) if src is None else src, dst_ref=rows(*block),
                send_sem=send_sems.at[k], recv_sem=recv_sems.at[k], device_id=to, device_id_type=MESH)

        mine = pltpu.make_async_copy(x_ref, rows(*me), local_sem)
        mine.start()
        first = [copy(0, me, sibling, src=x_ref)]
        first += [copy(1 + j, me, (*chip, c), src=x_ref) for j, chip in enumerate(chips)]
        for cp in first:
            cp.start()
        passed = [copy(4 + j, (*chip, c), sibling) for j, chip in enumerate(chips)]
        for j, chip in enumerate(chips):
            copy(1 + j, (*chip, c), me).wait_recv()
            passed[j].start()
        copy(0, sibling, me).wait_recv()
        for j, chip in enumerate(chips):
            copy(4 + j, (*chip, 1 - c), me).wait_recv()
        for cp in first + passed:
            cp.wait_send()
        mine.wait()

    return _pcall(
        body, name=name, out_shape=jax.ShapeDtypeStruct((N_DEV * m_per, n), v.dtype),
        in_specs=[pl.BlockSpec(memory_space=pltpu.VMEM)], out_specs=pl.BlockSpec(memory_space=pltpu.VMEM),
        scratch_shapes=[pltpu.SemaphoreType.DMA((7,)), pltpu.SemaphoreType.DMA((7,)), pltpu.SemaphoreType.DMA],
        compiler_params=pltpu.CompilerParams(vmem_limit_bytes=VMEM_LIMIT_V7X),
    )(v)


class _W:
    def __init__(self, name, kind, R, C):
        self.name, self.kind, self.R, self.C = name, kind, R, C

    @property
    def shard_shape(self):
        return (self.R, self.C // N_CHIPS) if self.kind == "col" else (self.R // N_CHIPS, self.C)

    @property
    def half_rows(self):
        return self.shard_shape[0] // 2

    def shard_half(self, ref, half):
        return ref.at[pl.ds(half * self.half_rows, self.half_rows), :]

    def region(self, full_ref, chip, half):
        hr = self.half_rows
        if self.kind == "col":
            cw = self.C // N_CHIPS
            return full_ref.at[pl.ds(half * hr, hr), pl.ds(chip * cw, cw)]
        return full_ref.at[pl.ds(chip * (2 * hr) + half * hr, hr), :]

    def region_both(self, full_ref, chip):
        hr = self.half_rows
        if self.kind == "col":
            cw = self.C // N_CHIPS
            return full_ref.at[:, pl.ds(chip * cw, cw)]
        return full_ref.at[pl.ds(chip * (2 * hr), 2 * hr), :]


def _ag_rider(ws, fulls, n_ch=4, chunks=None):
    n_w = len(ws)
    lo, hi = chunks or (0, n_ch)
    per = 6

    def parts(full, sems):
        send_sems, recv_sems = sems
        x, y, c, _ = _place()
        xn, yn, dg = (1 - x, y), (x, 1 - y), (1 - x, 1 - y)
        via = (x + (1 - c) * (1 - 2 * x), y + c * (1 - 2 * y))
        to = (x + c * (1 - 2 * x), y + (1 - c) * (1 - 2 * y))

        def reg(i, chip, half, t):
            nr = ws[i].half_rows // n_ch
            return ws[i].region(full[i], 2 * chip[0] + chip[1], half).at[pl.ds(t * nr, nr), :]

        def copy(r, i, t, k, dev):
            s = (i * (hi - lo) + t - lo) * per + k
            return pltpu.make_async_remote_copy(src_ref=r, dst_ref=r, send_sem=send_sems.at[s],
                                                recv_sem=recv_sems.at[s], device_id=dev, device_id_type=MESH)

        def direct(i, t, k):
            return copy(reg(i, (x, y), c, t), i, t, k, (*(via, to)[k], c))

        def direct_in(i, t, k):
            return copy(reg(i, (via, to)[k], c, t), i, t, k, (*(via, to)[k], c))

        def relay(i, t):
            return copy(reg(i, via, c, t), i, t, 2, (*to, c))

        def relay_in(i, t):
            return copy(reg(i, dg, c, t), i, t, 2, (*to, c))

        def hand(i, t, k, half):
            return copy(reg(i, (xn, yn, dg)[k], half, t), i, t, 3 + k, (x, y, 1 - c))

        return c, direct, direct_in, relay, relay_in, hand

    def start(_, full, sems):
        _, direct, _, _, _, _ = parts(full, sems)
        for t in range(lo, hi):
            for i in range(n_w):
                direct(i, t, 0).start()
                direct(i, t, 1).start()

    def arrived(t):
        def step(_, full, sems):
            c, _, direct_in, relay, relay_in, hand = parts(full, sems)
            for i in range(n_w):
                direct_in(i, t, 0).wait_recv()
                direct_in(i, t, 1).wait_recv()
                relay(i, t).start()
                hand(i, t, 0, c).start()
                hand(i, t, 1, c).start()
        return step

    def finish(_, full, sems):
        c, direct, _, relay, relay_in, hand = parts(full, sems)
        for t in range(lo, hi):
            for i in range(n_w):
                relay_in(i, t).wait_recv()
                hand(i, t, 2, c).start()
        for i in range(n_w):
            for t in range(lo, hi):
                for k in range(3):
                    hand(i, t, k, 1 - c).wait_recv()
        for i in range(n_w):
            for t in range(lo, hi):
                direct(i, t, 0).wait_send()
                direct(i, t, 1).wait_send()
                relay(i, t).wait_send()
                for k in range(3):
                    hand(i, t, k, c).wait_send()

    n_sem = per * (hi - lo) * n_w
    return _Rider(fulls, [jax.ShapeDtypeStruct((w.R, w.C), BF16) for w in ws],
                  [pltpu.SemaphoreType.DMA((n_sem,)), pltpu.SemaphoreType.DMA((n_sem,))], start, finish,
                  steps=[arrived(t) for t in range(lo, hi)], aliases={i: i for i in range(n_w)})


def _cast_into_full(w, a32, chip_arr):
    sr, sc = w.shard_shape
    tr, tc = _tile(sr, 512), _tile(sc, 2048)
    n_r, n_c = sr // tr, sc // tc
    if w.kind == "col":
        out_spec = pl.BlockSpec((tr, tc), lambda i, j, chip: (i, chip[0] * n_c + j))
    else:
        out_spec = pl.BlockSpec((tr, tc), lambda i, j, chip: (chip[0] * n_r + i, j))

    def body(chip_ref, a_ref, o_ref):
        o_ref[...] = a_ref[...].astype(BF16)

    return _pcall(
        body, name="cast_" + w.name, out_shape=jax.ShapeDtypeStruct((w.R, w.C), BF16),
        grid_spec=pltpu.PrefetchScalarGridSpec(
            num_scalar_prefetch=1, grid=(n_r, n_c),
            in_specs=[pl.BlockSpec((tr, tc), lambda i, j, chip: (i, j))], out_specs=out_spec),
        compiler_params=_params(("parallel", "parallel")),
    )(chip_arr, a32)


def _half_view(w, g):
    return g if w.kind == "col" else g.reshape(N_CHIPS, w.R // N_CHIPS, w.C)


def _px_rider(ws, grads):
    n_w = len(ws)

    def copies(g, got, sems):
        send_sems, recv_sems = sems
        x, y, c, _ = _place()

        def half_all(w, ref, half):
            hr = w.half_rows
            if w.kind == "col":
                return ref.at[pl.ds(half * hr, hr), :]
            return ref.at[:, pl.ds(half * hr, hr), :]

        return [pltpu.make_async_remote_copy(
            src_ref=half_all(w, g[i], 1 - c), dst_ref=got[i], send_sem=send_sems.at[i], recv_sem=recv_sems.at[i],
            device_id=(x, y, 1 - c), device_id_type=MESH) for i, w in enumerate(ws)]

    def start(g, got, sems):
        for cp in copies(g, got, sems):
            cp.start()

    def finish(g, got, sems):
        for cp in copies(g, got, sems):
            cp.wait_recv()
            cp.wait_send()

    def got_shape(w):
        hr = w.half_rows
        return (hr, w.C) if w.kind == "col" else (N_CHIPS, hr, w.C)

    return _Rider([_half_view(w, g) for w, g in zip(ws, grads)],
                  [jax.ShapeDtypeStruct(got_shape(w), BF16) for w in ws],
                  [pltpu.SemaphoreType.DMA((n_w,)), pltpu.SemaphoreType.DMA((n_w,))], start, finish)


def _pair_sum(w, g, got, c_arr):
    hr = w.half_rows
    if w.kind == "col":
        tr, tc = _tile(hr, 512), _tile(w.C, 2048)
        n_r = hr // tr
        grid = (n_r, w.C // tc)
        g_spec = pl.BlockSpec((tr, tc), lambda i, j, c: (c[0] * n_r + i, j))
        o_spec = pl.BlockSpec((tr, tc), lambda i, j, c: (i, j))
    else:
        tr = _tile(hr, 512)
        n_r = hr // tr
        grid = (N_CHIPS, n_r)
        g_spec = pl.BlockSpec((1, tr, w.C), lambda s, i, c: (s, c[0] * n_r + i, 0))
        o_spec = pl.BlockSpec((1, tr, w.C), lambda s, i, c: (s, i, 0))

    def body(c_ref, g_ref, got_ref, out_ref):
        out_ref[...] = (g_ref[...].astype(F32) + got_ref[...].astype(F32)).astype(BF16)

    return _pcall(
        body, name="grad_pair_sum_" + w.name, out_shape=jax.ShapeDtypeStruct(got.shape, BF16),
        grid_spec=pltpu.PrefetchScalarGridSpec(num_scalar_prefetch=1, grid=grid, in_specs=[g_spec, o_spec],
                                               out_specs=o_spec),
        compiler_params=_params(("parallel", "parallel")),
    )(c_arr, _half_view(w, g), got)


def _cx_rider(ws, sums, part=(0, 1), q_in=None):
    n_w = len(ws)

    def parts(p, q, sems):
        send_sems, recv_sems = sems
        x, y, c, chips = _place()
        my_chip = 2 * x + y

        def rows(w, ref):
            nr = w.half_rows // part[1]
            return ref.at[pl.ds(part[0] * nr, nr), :]

        def piece(w, ref, chip):
            if w.kind == "col":
                cw = w.C // N_CHIPS
                return rows(w, ref.at[:, pl.ds(chip * cw, cw)])
            return rows(w, ref.at[chip])

        def copy(i, k, recv=False):
            chip = chips[k]
            to_chip = 2 * chip[0] + chip[1]
            return pltpu.make_async_remote_copy(
                src_ref=piece(ws[i], p[i], to_chip), dst_ref=rows(ws[i], q[i].at[to_chip if recv else my_chip]),
                send_sem=send_sems.at[3 * i + k], recv_sem=recv_sems.at[3 * i + k],
                device_id=(*chip, c), device_id_type=MESH)

        return copy

    both = [(i, k) for i in range(n_w) for k in range(N_CHIPS - 1)]

    def start(p, q, sems):
        copy = parts(p, q, sems)
        for i, k in both:
            copy(i, k).start()

    def finish(p, q, sems):
        copy = parts(p, q, sems)
        for i, k in both:
            copy(i, k, recv=True).wait_recv()
        for i, k in both:
            copy(i, k).wait_send()

    return _Rider(list(sums) + list(q_in or []),
                  [jax.ShapeDtypeStruct((N_CHIPS, w.half_rows, w.shard_shape[1]), BF16) for w in ws],
                  [pltpu.SemaphoreType.DMA((3 * n_w,)), pltpu.SemaphoreType.DMA((3 * n_w,))], start, finish,
                  aliases={n_w + i: i for i in range(n_w)} if q_in else None)


def _chip_sum(w, p, q, cc_arr):
    hr, cols = w.half_rows, w.shard_shape[1]
    tr, tc = _tile(hr, 512), _tile(cols, 2048)
    n_r, n_c = hr // tr, cols // tc

    def body(cc_ref, own, q1, q2, q3, out_ref):
        own_v = own[...] if w.kind == "col" else own[0]
        out_ref[...] = ((own_v.astype(F32) + q1[0].astype(F32)) + q2[0].astype(F32)) + q3[0].astype(F32)

    if w.kind == "col":
        own_spec = pl.BlockSpec((tr, tc), lambda i, j, cc: (i, cc[1] * n_c + j))
    else:
        own_spec = pl.BlockSpec((1, tr, tc), lambda i, j, cc: (cc[1], i, j))
    q_specs = [pl.BlockSpec((1, tr, tc), lambda i, j, cc, s=s: ((cc[1] + s) % N_CHIPS, i, j)) for s in (1, 2, 3)]
    return _pcall(
        body, name="grad_chip_sum_" + w.name, out_shape=jax.ShapeDtypeStruct(w.shard_shape, F32),
        grid_spec=pltpu.PrefetchScalarGridSpec(
            num_scalar_prefetch=1, grid=(n_r, n_c), in_specs=[own_spec] + q_specs,
            out_specs=pl.BlockSpec((tr, tc), lambda i, j, cc: (cc[0] * n_r + i, j))),
        compiler_params=_params(("parallel", "parallel")),
    )(cc_arr, p, q, q, q)


_SEM = pl.BlockSpec(memory_space=pltpu.SEMAPHORE)
_HBM = pl.BlockSpec(memory_space=pltpu.HBM)


def _cx_split_copies(ws, p, land, send_sems, recv_sems):
    x, y, c, chips = _place()
    my_chip = 2 * x + y
    pairs = []
    for i, w in enumerate(ws):
        for k, chip in enumerate(chips):
            to_chip = 2 * chip[0] + chip[1]
            src = p[i].at[:, pl.ds(to_chip * (w.C // N_CHIPS), w.C // N_CHIPS)] if w.kind == "col" else p[i].at[to_chip]
            kw = dict(send_sem=send_sems.at[3 * i + k], recv_sem=recv_sems.at[3 * i + k], device_id=(*chip, c),
                      device_id_type=MESH)
            pairs.append((pltpu.make_async_remote_copy(src_ref=src, dst_ref=land[i].at[my_chip], **kw),
                          pltpu.make_async_remote_copy(src_ref=src, dst_ref=land[i].at[to_chip], **kw)))
    return pairs


def _cx_start(ws, sums):
    n_w = len(ws)
    lands = [lax.empty((N_CHIPS, w.half_rows, w.shard_shape[1]), BF16) for w in ws]

    def body(*refs):
        p, land = refs[:n_w], refs[n_w:2 * n_w]
        for out, _ in _cx_split_copies(ws, p, land, refs[2 * n_w], refs[2 * n_w + 1]):
            out.start()
        refs[-1][...] = jnp.zeros_like(refs[-1])

    arrays = [pltpu.with_memory_space_constraint(a, pltpu.HBM) for a in list(sums) + lands]
    res = _pcall(
        body, name="grad_last_exchange_start",
        out_shape=(pltpu.SemaphoreType.DMA((3 * n_w,)), pltpu.SemaphoreType.DMA((3 * n_w,)),
                   *[pltpu.HBM(a.shape, a.dtype) for a in arrays], jax.ShapeDtypeStruct((8, 128), F32)),
        in_specs=[_HBM] * (2 * n_w),
        out_specs=(_SEM, _SEM, *[_HBM] * (2 * n_w), pl.BlockSpec(memory_space=pltpu.VMEM)),
        input_output_aliases={i: 2 + i for i in range(2 * n_w)},
        compiler_params=pltpu.CompilerParams(has_side_effects=pltpu.SideEffectType.DATAFLOW_SIDE_EFFECTING),
    )(*arrays)
    return res[0], res[1], list(res[2:2 + n_w]), list(res[2 + n_w:2 + 2 * n_w]), res[-1]


def _cx_wait(ws, send_sems, recv_sems, sums, lands, after):
    n_w = len(ws)

    def body(*refs):
        p, land = refs[:n_w], refs[n_w:2 * n_w]
        for _, cp in _cx_split_copies(ws, p, land, refs[2 * n_w], refs[2 * n_w + 1]):
            cp.wait_send()
            cp.wait_recv()

    res = _pcall(
        body, name="grad_last_exchange_wait",
        out_shape=[pltpu.HBM(a.shape, a.dtype) for a in list(sums) + list(lands)],
        in_specs=[_HBM] * (2 * n_w) + [_SEM, _SEM] + [ANY] * len(after), out_specs=[_HBM] * (2 * n_w),
        input_output_aliases={i: i for i in range(2 * n_w)},
        compiler_params=pltpu.CompilerParams(has_side_effects=pltpu.SideEffectType.DATAFLOW_SIDE_EFFECTING),
    )(*sums, *lands, send_sems, recv_sems, *after)
    return list(res[:n_w]), list(res[n_w:])


def _sf_rider(ws, grads):
    n_w = len(ws)

    def copy(g, sems, i, half):
        send_sems, recv_sems = sems
        x, y, c, _ = _place()
        h = c if half == "mine" else 1 - c
        reg = ws[i].shard_half(g[i], h)
        return pltpu.make_async_remote_copy(src_ref=reg, dst_ref=reg, send_sem=send_sems.at[i], recv_sem=recv_sems.at[i],
                                            device_id=(x, y, 1 - c), device_id_type=MESH)

    def start(_, g, sems):
        for i in range(n_w):
            copy(g, sems, i, "mine").start()

    def finish(_, g, sems):
        for i in range(n_w):
            copy(g, sems, i, "other").wait_recv()
            copy(g, sems, i, "mine").wait_send()

    return _Rider(grads, [jax.ShapeDtypeStruct(w.shard_shape, F32) for w in ws],
                  [pltpu.SemaphoreType.DMA((n_w,)), pltpu.SemaphoreType.DMA((n_w,))], start, finish,
                  aliases={i: i for i in range(n_w)})


def _adamw_math(w, g, m, v):
    m = ADAM_B1 * m + (1.0 - ADAM_B1) * g
    v = ADAM_B2 * v + (1.0 - ADAM_B2) * (g * g)
    m_hat = m / (1.0 - ADAM_B1 ** ADAM_STEP)
    v_hat = v / (1.0 - ADAM_B2 ** ADAM_STEP)
    delta = -ADAM_LR * (m_hat / (jnp.sqrt(v_hat) + ADAM_EPS) + ADAM_WD * w)
    return delta, m, v


def _adamw(name, w, g, m, v, after=None):
    R, C = w.shape
    tr, tc = _tile(R, 256), _tile(C, 2048)
    behind = [] if after is None else [after]

    def body(w_ref, g_ref, m_ref, v_ref, *rest):
        g_out, d_out, m_out, v_out = rest[len(behind):]
        g = g_ref[...]
        g_out[...] = g
        d_out[...], m_out[...], v_out[...] = _adamw_math(w_ref[...], g, m_ref[...], v_ref[...])

    spec = pl.BlockSpec((tr, tc), lambda i, j: (i, j))
    sh = jax.ShapeDtypeStruct((R, C), F32)
    return _pcall(body, name=name, grid=(R // tr, C // tc), in_specs=[spec] * 4 + [ANY] * len(behind),
                  out_specs=[spec] * 4, out_shape=[sh] * 4, compiler_params=_params(("parallel", "parallel")))(
                      w, g, m, v, *behind)


def _ada_update(sct, dmod_sh, w, m, v, riders=()):
    R, C = w.shape
    tr, tc = _tile(R, 256), _tile(C, 1024)

    def body(s_ref, d_ref, w_ref, m_ref, v_ref, g_out, d_out, m_out, v_out):
        s, d = s_ref[...], d_ref[...]
        g = s[:, 0:1] * d[0:1, :]
        for b in range(1, N_DEV):
            g += s[:, b:b + 1] * d[b:b + 1, :]
        g_out[...] = g
        d_out[...], m_out[...], v_out[...] = _adamw_math(w_ref[...], g, m_ref[...], v_ref[...])

    spec = pl.BlockSpec((tr, tc), lambda i, j: (i, j))
    sh = jax.ShapeDtypeStruct((R, C), F32)
    return _ride(
        "ada_update", body, riders, [sct, dmod_sh, w, m, v], grid=(R // tr, C // tc),
        in_specs=[pl.BlockSpec((tr, N_DEV), lambda i, j: (i, 0)), pl.BlockSpec((N_DEV, tc), lambda i, j: (0, j)),
                  spec, spec, spec],
        out_specs=[spec] * 4, out_shape=[sh] * 4, scratch_shapes=[], sem=("parallel", "parallel"))


def _silu_rows(c_row):
    D = c_row.shape[1]

    def body(c_ref, o_ref):
        cv = c_ref[...]
        o_ref[...] = cv * jax.nn.sigmoid(cv)

    return _pcall(body, name="silu_c", out_shape=jax.ShapeDtypeStruct((1, D), F32))(c_row)


def _pack_partials(parts, widths, total):
    n = len(widths)

    def body(*refs):
        loss_p, out_ref = refs[n], refs[n + 1]
        off = 0
        for ref, wd in zip(refs[:n], widths):
            out_ref[:, off:off + wd] = jnp.sum(ref[...], axis=0)
            off += wd
        loss = jnp.sum(jnp.sum(loss_p[...], axis=0), axis=1, keepdims=True)
        out_ref[:, off:off + 128] = jnp.broadcast_to(loss, (1, 128))
        if off + 128 < total:
            out_ref[:, off + 128:total] = jnp.zeros((1, total - off - 128), F32)

    return _pcall(body, name="pack_partials", out_shape=jax.ShapeDtypeStruct((1, total), F32))(*parts)


def _small_update(gathered, offsets, params, loss_off):
    n_p = len(params)

    def over_devices(g_ref, off, wd):
        blk = g_ref[:, off:off + wd]
        g = blk[0:1, :]
        for b in range(1, N_DEV):
            g = g + blk[b:b + 1, :]
        return g

    def body(*refs):
        g_ref = refs[0]
        prm = refs[1:1 + 3 * n_p]
        outs = refs[1 + 3 * n_p:]
        outs[4 * n_p][...] = over_devices(g_ref, loss_off, 128)
        for i, (off, wd) in enumerate(offsets):
            g = over_devices(g_ref, off, wd)
            w, m, v = prm[3 * i][...], prm[3 * i + 1][...], prm[3 * i + 2][...]
            outs[4 * i][...] = g
            outs[4 * i + 1][...], outs[4 * i + 2][...], outs[4 * i + 3][...] = _adamw_math(w, g, m, v)

    flat = [a for t in params for a in t]
    out_shape = [jax.ShapeDtypeStruct(t[0].shape, F32) for t in params for _ in range(4)]
    out_shape.append(jax.ShapeDtypeStruct((1, 128), F32))
    return _pcall(body, name="small_update", out_shape=out_shape)(gathered, *flat)


def kernel(x, c, w_ada, b_ada, norm1_w, w_in, q_norm_w, k_norm_w, w_pool, pool_scale, w_a_up, w_b_up, w_o, norm2_w, w_ff1, w_ff2, loss_target, m_w_ada, m_b_ada, m_norm1_w, m_w_in, m_q_norm_w, m_k_norm_w, m_w_pool, m_pool_scale, m_w_a_up, m_w_b_up, m_w_o, m_norm2_w, m_w_ff1, m_w_ff2, v_w_ada, v_b_ada, v_norm1_w, v_w_in, v_q_norm_w, v_k_norm_w, v_w_pool, v_pool_scale, v_w_a_up, v_w_b_up, v_w_o, v_norm2_w, v_w_ff1, v_w_ff2):
    _, S, D = x.shape
    PW = D // 2
    H = PW // HEAD_DIM
    cg = PW // N_GROUPS
    IN = w_in.shape[2] * N_CHIPS
    FF = w_ff1.shape[2] * N_CHIPS
    A_COLS = w_ada.shape[2]
    xi, yi, ci = lax.axis_index("x"), lax.axis_index("y"), lax.axis_index("c")
    chip = 2 * xi + yi
    dev = 2 * chip + ci
    c_arr = jnp.reshape(ci, (1,)).astype(jnp.int32)
    x2, tgt = x[0], loss_target[0]

    ws = [_W("w_in", "col", D, IN), _W("w_pool", "row", PW, cg), _W("w_a_up", "col", PW, D),
          _W("w_b_up", "col", PW, D), _W("w_o", "row", D, D), _W("w_ff1", "col", D, FF), _W("w_ff2", "row", FF, D)]
    w32 = [w_in[0], w_pool[0].reshape(cg, cg), w_a_up[0], w_b_up[0], w_o[0], w_ff1[0], w_ff2[0]]
    m32 = [m_w_in[0], m_w_pool[0].reshape(cg, cg), m_w_a_up[0], m_w_b_up[0], m_w_o[0], m_w_ff1[0], m_w_ff2[0]]
    v32 = [v_w_in[0], v_w_pool[0].reshape(cg, cg), v_w_a_up[0], v_w_b_up[0], v_w_o[0], v_w_ff1[0], v_w_ff2[0]]

    W_IN, W_POOL, W_A, W_B, W_O, W_FF1, W_FF2 = ws
    chip_arr = jnp.reshape(chip, (1,)).astype(jnp.int32)
    cc_arr = jnp.stack([ci, chip]).astype(jnp.int32)
    s_in, s_pool, s_a, s_b, s_o, s_ff1, s_ff2 = [_cast_into_full(w, a, chip_arr) for w, a in zip(ws, w32)]
    (win_f,) = _run_rider("gather_w_in", _ag_rider([W_IN], [s_in]))

    sc_row = _silu_rows(c)
    sc_all = _dev_allgather("gather_silu_c", sc_row.reshape(8, D // 8)).reshape(N_DEV, D)
    sc16 = jnp.concatenate([sc_all, jnp.zeros_like(sc_all)], axis=0)
    b_cols = lax.dynamic_slice(b_ada, (0, chip * A_COLS), (1, A_COLS))
    (mod_cols,) = _mm("mod_cols", [(sc16, w_ada[0])], M=2 * N_DEV, N=A_COLS, K=D, tm=16, tn=1024, tk=1024,
                      a_pro=lambda a: a.astype(BF16), b_pro=lambda b: b.astype(BF16),
                      extras=[(b_cols, "row", 0)], outs=[_tile_out(F32)], epi=lambda accs, ex: [accs[0] + ex[0]])
    mod_all = _dev_allgather("gather_mod", mod_cols[:N_DEV]).reshape(N_CHIPS, 2, N_DEV, A_COLS)
    mod_row = lax.dynamic_index_in_dim(mod_all[:, 0], dev, axis=1, keepdims=False).reshape(1, N_CHIPS * A_COLS)
    shift1, scale1, gate1, shift2, scale2, gate2 = [mod_row[:, i * D:(i + 1) * D] for i in range(6)]

    WIDE = dict(tm=2048, tn=512, tk=2048)
    DEEP = dict(tm=1024, tn=1024, tk=1024)
    h = _norm_mod("norm1_mod", x2, norm1_w, scale1, shift1)
    (proj,), ((wpool_f, wa_f, wb_f, wo_f),) = _mm(
        "in_proj", [(h, win_f)], M=S, N=IN, K=D, outs=[_tile_out(BF16)], epi=lambda accs, ex: [accs[0]], **WIDE,
        riders=[_ag_rider([W_POOL, W_A, W_B, W_O], [s_pool, s_a, s_b, s_o], n_ch=2)])
    pooled, pa = _pool_fwd(proj, wpool_f, pool_scale, S, PW)
    (att, attf), ((wff1_f,),) = _attn_fwd(proj, q_norm_w, k_norm_w, S, H, PW // HEAD_DIM,
                                          riders=[_ag_rider([W_FF1], [s_ff1])])

    def merge_epi(accs, ex):
        sa, sb = jax.nn.sigmoid(ex[0].astype(F32)), jax.nn.sigmoid(ex[1].astype(F32))
        return [sa * accs[0] + sb * accs[1], accs[0], accs[1]]

    (merged, ya, yb), (ff2_a,) = _mm("branch_up_merge", [(pa, wa_f), (att, wb_f)], M=S, N=D, K=PW,
                                     extras=[(proj, "tile", 4 * PW), (proj, "tile", 4 * PW + D)],
                                     outs=[_tile_out(BF16)] * 3, epi=merge_epi,
                                     riders=[_ag_rider([W_FF2], [s_ff2], chunks=(0, 1))])
    (x1, o), (ff2_b,) = _mm("out_proj", [(merged, wo_f)], M=S, N=D, K=D, extras=[(x2, "tile", 0), (gate1, "row", 0)],
                            outs=[_tile_out(F32), _tile_out(BF16)], epi=lambda accs, ex: [ex[0] + ex[1] * accs[0], accs[0]],
                            riders=[_ag_rider([W_FF2], ff2_a, chunks=(1, 2))], **WIDE)
    h2 = _norm_mod("norm2_mod", x1, norm2_w, scale2, shift2)
    (rl,), ((wff2_f,),) = _mm("ff1", [(h2, wff1_f)], M=S, N=FF, K=D, outs=[_tile_out(BF16)], **WIDE,
                              epi=lambda accs, ex: [jnp.maximum(accs[0], 0.0)],
                              riders=[_ag_rider([W_FF2], ff2_b, chunks=(2, 4))])

    def square(a):
        af = a.astype(F32)
        return (af * af).astype(BF16)

    def loss_epi(accs, ex):
        x1_t, tgt_t, g2 = ex
        f = accs[0]
        diff = (x1_t + g2 * f) - tgt_t
        dy = diff * (1.0 / D)
        return [dy, dy * g2, _colsum(dy * f), _colsum(diff * diff)]

    dy, df, dgate2_p, loss_p = _mm("ff2_loss", [(rl, wff2_f)], M=S, N=D, K=FF, a_pro=square, tm=1024, tn=1024, tk=512,
                                   extras=[(x1, "tile", 0), (tgt, "tile", 0), (gate2, "row", 0)],
                                   outs=[_tile_out(F32), _tile_out(BF16), _COLSUM, _COLSUM], epi=loss_epi)

    def pair_sums(group, partials, got):
        return [_pair_sum(w, g, r, c_arr) for w, g, r in zip(group, partials, got)]

    def chip_sums(group, sums, from_chips):
        return [_chip_sum(w, p, q, cc_arr) for w, p, q in zip(group, sums, from_chips)]

    first = lambda accs, ex: [accs[0]]
    gmm = dict(ta=True, outs=[_tile_out(BF16)], epi=first, **WIDE)
    (g_ff2,) = _mm("grad_w_ff2", [(rl, df)], M=FF, N=D, K=S, a_pro=square, ta=True, tm=512, tn=2048, tk=2048,
                   outs=[_tile_out(BF16)], epi=first)
    (dz1,), (got_ff2,) = _mm("d_ff_hidden", [(df, wff2_f)], M=S, N=FF, K=D, tb=True, extras=[(rl, "tile", 0)], **WIDE,
                             outs=[_tile_out(BF16)], epi=lambda accs, ex: [accs[0] * (2.0 * ex[0].astype(F32))],
                             riders=[_px_rider([W_FF2], [g_ff2])])
    sum_ff2 = pair_sums([W_FF2], [g_ff2], got_ff2)
    (g_ff1,), (q_ff2,) = _mm("grad_w_ff1", [(h2, dz1)], M=D, N=FF, K=S,
                             riders=[_cx_rider([W_FF2], sum_ff2, part=(0, 2))], **gmm)
    (dh2,), (got_ff1, q_ff2) = _mm("d_h2", [(dz1, wff1_f)], M=S, N=D, K=FF, tb=True, outs=[_tile_out(F32)], epi=first,
                                   riders=[_px_rider([W_FF1], [g_ff1]),
                                           _cx_rider([W_FF2], sum_ff2, part=(1, 2), q_in=q_ff2)], **DEEP)
    sum_ff1 = pair_sums([W_FF1], [g_ff1], got_ff1)
    dx1, dshift2_p, dscale2_p, gn2_p, do, dgate1_p = _norm_mod_bwd("norm2_bwd", dh2, x1, dy, norm2_w, scale2,
                                                                   gate_o=(o, gate1))
    (g_wo,) = _mm("grad_w_o", [(merged, do)], M=D, N=D, K=S, **gmm)

    def gate_epi(accs, ex):
        dm = accs[0]
        sa, sb = jax.nn.sigmoid(ex[0].astype(F32)), jax.nn.sigmoid(ex[1].astype(F32))
        ya_t, yb_t = ex[2].astype(F32), ex[3].astype(F32)
        return [dm * sa, dm * sb, dm * ya_t * (sa * (1.0 - sa)), dm * yb_t * (sb * (1.0 - sb))]

    dya, dyb, dga, dgb = _mm("d_merged", [(do, wo_f)], M=S, N=D, K=D, tb=True, tm=1024, tn=512, tk=2048,
                             extras=[(proj, "tile", 4 * PW), (proj, "tile", 4 * PW + D), (ya, "tile", 0), (yb, "tile", 0)],
                             outs=[_tile_out(BF16)] * 4, epi=gate_epi)
    (g_wa,) = _mm("grad_w_a_up", [(pa, dya)], M=PW, N=D, K=S, **gmm)
    (g_wb,) = _mm("grad_w_b_up", [(att, dyb)], M=PW, N=D, K=S, **gmm)
    (dpa,) = _mm("d_pool_out", [(dya, wa_f)], M=S, N=PW, K=D, tb=True, outs=[_tile_out(F32)], epi=first, **WIDE)
    mid = [W_A, W_B, W_O]
    (datt,), (got_mid,) = _mm("d_att", [(dyb, wb_f)], M=S, N=PW, K=D, tb=True, outs=[_tile_out(BF16)], epi=first, **WIDE,
                              riders=[_px_rider(mid, [g_wa, g_wb, g_wo])])
    sum_mid = pair_sums(mid, [g_wa, g_wb, g_wo], got_mid)
    du, g_wpool4, gscale_p = _pool_bwd(dpa, pooled, wpool_f, pool_scale, S, PW)
    (dq, dk, dv, gq_p, gk_p), ((q_ff1,),) = _attn_bwd(
        proj, datt, attf, q_norm_w, k_norm_w, S, H, PW // HEAD_DIM, riders=[_cx_rider([W_FF1], sum_ff1)])
    dproj = jnp.concatenate([du, dq, dk, dv, dga, dgb], axis=1)
    early = [W_FF1, W_FF2]
    halves_early = chip_sums(early, sum_ff1 + sum_ff2, [q_ff1, q_ff2[0]])
    (g_win,), (grads_early, (q_wa, q_wb, q_wo)) = _mm(
        "grad_w_in", [(h, dproj)], M=D, N=IN, K=S, riders=[_sf_rider(early, halves_early), _cx_rider(mid, sum_mid)], **gmm)
    last = [W_IN, W_POOL]
    g_last = [g_win, g_wpool4.reshape(PW, cg)]
    halves_mid = chip_sums(mid, sum_mid, [q_wa, q_wb, q_wo])
    (dh,), (got_last, grads_mid) = _mm("d_h", [(dproj, win_f)], M=S, N=D, K=IN, tb=True, outs=[_tile_out(F32)], epi=first,
                                       riders=[_px_rider(last, g_last), _sf_rider(mid, halves_mid)], **DEEP)
    sum_last = pair_sums(last, g_last, got_last)
    grad_x, dshift1_p, dscale1_p, gn1_p = _norm_mod_bwd("norm1_bwd", dh, x2, dx1, norm1_w, scale1)

    parts = [dshift1_p, dscale1_p, dgate1_p, dshift2_p, dscale2_p, dgate2_p, gn1_p, gn2_p,
             gscale_p.reshape(1, 1, PW), gq_p, gk_p]
    widths = [D] * 8 + [PW, HEAD_DIM, HEAD_DIM]
    used = sum(widths)
    P = -(-(used + 128) // 1024) * 1024
    packed = _pack_partials(parts + [loss_p], widths, P)
    gathered = _dev_allgather("gather_vector_grads", packed.reshape(8, P // 8)).reshape(N_DEV, P)
    sum_last, gathered = lax.optimization_barrier((sum_last, gathered))
    cx_send, cx_recv, sum_last, land_last, token = _cx_start(last, sum_last)
    small = [(b_ada, m_b_ada, v_b_ada), (norm1_w, m_norm1_w, v_norm1_w), (norm2_w, m_norm2_w, v_norm2_w),
             (pool_scale, m_pool_scale, v_pool_scale), (q_norm_w, m_q_norm_w, v_q_norm_w),
             (k_norm_w, m_k_norm_w, v_k_norm_w)]
    offsets = [(0, 6 * D), (6 * D, D), (7 * D, D), (8 * D, PW), (8 * D + PW, HEAD_DIM), (8 * D + PW + HEAD_DIM, HEAD_DIM)]
    su = _small_update(gathered, offsets, small, used)
    (g_b, d_b, nm_b, nv_b, g_n1, d_n1, nm_n1, nv_n1, g_n2, d_n2, nm_n2, nv_n2, g_ps, d_ps, nm_ps, nv_ps,
     g_qn, d_qn, nm_qn, nv_qn, g_kn, d_kn, nm_kn, nv_kn, loss_sum) = su
    dmod_sh = lax.dynamic_slice(gathered, (0, chip * A_COLS), (N_DEV, A_COLS))
    dmod_sh, token = lax.optimization_barrier((dmod_sh, token))
    g_ada, d_ada, nm_ada, nv_ada = _ada_update(sc_all.T, dmod_sh, w_ada[0], m_w_ada[0], v_w_ada[0])

    upd_done = [_adamw("adamw_" + w.name, a, g, m, v, after=token)
                for w, a, g, m, v in zip(ws[2:], w32[2:], list(grads_mid) + list(grads_early), m32[2:], v32[2:])]

    sum_last, q_last = _cx_wait(last, cx_send, cx_recv, sum_last, land_last,
                                after=[nv_ada] + [u[3] for u in upd_done])
    halves_last = chip_sums(last, sum_last, q_last)
    filled = _run_rider("grad_sibling_fill", _sf_rider(last, halves_last))
    upd = [_adamw("adamw_" + w.name, a, g, m, v) for w, a, g, m, v in zip(ws[:2], w32[:2], filled, m32[:2], v32[:2])]
    upd += upd_done

    loss = (0.5 / D) * loss_sum[0, 0]

    def up(a):
        return a[None]

    def pool4(a):
        return a.reshape(1, N_GROUPS, cg // N_CHIPS, cg)

    (gr_win, d_win, nm_win, nv_win), (gr_wp, d_wp, nm_wp, nv_wp), (gr_wa, d_wa, nm_wa, nv_wa), \
        (gr_wb, d_wb, nm_wb, nv_wb), (gr_wo, d_wo, nm_wo, nv_wo), (gr_f1, d_f1, nm_f1, nv_f1), \
        (gr_f2, d_f2, nm_f2, nv_f2) = upd
    return (
        loss, grad_x[None],
        up(g_ada), g_b, g_n1, up(gr_win), g_qn, g_kn, pool4(gr_wp), g_ps, up(gr_wa), up(gr_wb), up(gr_wo), g_n2,
        up(gr_f1), up(gr_f2),
        up(d_ada), d_b, d_n1, up(d_win), d_qn, d_kn, pool4(d_wp), d_ps, up(d_wa), up(d_wb), up(d_wo), d_n2,
        up(d_f1), up(d_f2),
        up(nm_ada), nm_b, nm_n1, up(nm_win), nm_qn, nm_kn, pool4(nm_wp), nm_ps, up(nm_wa), up(nm_wb), up(nm_wo), nm_n2,
        up(nm_f1), up(nm_f2),
        up(nv_ada), nv_b, nv_n1, up(nv_win), nv_qn, nv_kn, pool4(nv_wp), nv_ps, up(nv_wa), up(nv_wb), up(nv_wo), nv_n2,
        up(nv_f1), up(nv_f2),
    )
```

```python
import functools
import math

import jax
import jax.numpy as jnp
from jax import lax
from jax.experimental import pallas as pl
from jax.experimental.pallas import tpu as pltpu

F32 = jnp.float32
BF16 = jnp.bfloat16
MESH = pl.DeviceIdType.MESH
ANY = pl.BlockSpec(memory_space=pl.ANY)

EPS = 1e-6
HEAD_DIM = 128
POOL_WINDOWS = (2, 4, 8, 16)
N_GROUPS = len(POOL_WINDOWS)
N_CHIPS = 4
N_DEV = 8
ADAM_LR, ADAM_B1, ADAM_B2, ADAM_EPS, ADAM_WD, ADAM_STEP = 0.001, 0.9, 0.999, 1e-08, 0.01, 10
VMEM_LIMIT_V7X = 56 * 1024 * 1024
ATT_T = 256
POOL_T = 256


def _pcall(body, **kw):
    return pl.pallas_call(body, **kw)


def _params(sem=None):
    return pltpu.CompilerParams(dimension_semantics=sem, vmem_limit_bytes=VMEM_LIMIT_V7X)


def _tile(n, pref):
    if n <= pref:
        return n
    t = pref
    while n % t:
        t //= 2
    return t


class _Rider:
    def __init__(self, arrays, out_shape, sems, start, finish, aliases=None, steps=()):
        self.arrays, self.out_shape, self.sems = list(arrays), list(out_shape), list(sems)
        self.start, self.finish, self.aliases, self.steps = start, finish, aliases or {}, list(steps)


def _ride(name, body, riders, arrays, *, grid, in_specs, out_specs, out_shape, scratch_shapes, sem):
    n_in, n_out, n_scr = len(arrays), len(out_shape), len(scratch_shapes)
    r_arrays = [a for r in riders for a in r.arrays]
    r_outs = [o for r in riders for o in r.out_shape]
    r_sems = [s for r in riders for s in r.sems]
    n_hooks = max([len(r.steps) for r in riders], default=0)
    total = math.prod(grid)
    aliases, off_i, off_o = {}, n_in, n_out
    for r in riders:
        for a, o in r.aliases.items():
            aliases[off_i + a] = off_o + o
        off_i += len(r.arrays)
        off_o += len(r.out_shape)

    def full(*refs):
        p = 0
        groups = []
        for n in (n_in, len(r_arrays), n_out, len(r_outs), n_scr, len(r_sems)):
            groups.append(refs[p:p + n])
            p += n
        ins, rin, outs, rout, scr, rsem = groups

        def each(what):
            a = o = s = 0
            for r in riders:
                fn = what(r)
                if fn is not None:
                    fn(rin[a:a + len(r.arrays)], rout[o:o + len(r.out_shape)], rsem[s:s + len(r.sems)])
                a, o, s = a + len(r.arrays), o + len(r.out_shape), s + len(r.sems)

        if riders:
            lin = 0
            for d, g in enumerate(grid):
                lin = lin * g + pl.program_id(d)
            pl.when(lin == 0)(lambda: each(lambda r: r.start))
            for t in range(n_hooks):
                pl.when(lin == min(total - 1, ((t + 1) * total) // n_hooks))(
                    lambda t=t: each(lambda r: r.steps[t] if t < len(r.steps) else None))
        body(*ins, *outs, *scr)
        if riders:
            pl.when(lin == total - 1)(lambda: each(lambda r: r.finish))

    res = _pcall(
        full, name=name, grid=grid, in_specs=list(in_specs) + [ANY] * len(r_arrays),
        out_specs=list(out_specs) + [ANY] * len(r_outs), out_shape=list(out_shape) + r_outs,
        scratch_shapes=list(scratch_shapes) + r_sems, input_output_aliases=aliases,
        compiler_params=_params(("arbitrary",) * len(grid) if riders else sem),
    )(*arrays, *r_arrays)
    if not riders:
        return res
    main, rest, per = res[:n_out], res[n_out:], []
    for r in riders:
        per.append(rest[:len(r.out_shape)])
        rest = rest[len(r.out_shape):]
    return main, per


def _run_rider(name, rider):
    def body(*refs):
        n_a, n_o = len(rider.arrays), len(rider.out_shape)
        ins, outs, sems = refs[:n_a], refs[n_a:n_a + n_o], refs[n_a + n_o:]
        for fn in [rider.start] + rider.steps + [rider.finish]:
            fn(ins, outs, sems)

    return _pcall(body, name=name, out_shape=rider.out_shape, in_specs=[ANY] * len(rider.arrays),
                  out_specs=[ANY] * len(rider.out_shape), scratch_shapes=rider.sems,
                  input_output_aliases=rider.aliases)(*rider.arrays)


def _mm(name, pairs, *, M, N, K, ta=False, tb=False, tm=512, tn=1024, tk=1024,
        a_pro=None, b_pro=None, extras=(), outs, epi, riders=(), b_noff=0):
    tm, tn, tk = _tile(M, tm), _tile(N, tn), _tile(K, tk)
    n_i, n_j, n_k = M // tm, N // tn, K // tk
    n_p, n_e = len(pairs), len(extras)
    arrays, in_specs = [], []
    for a, _ in pairs:
        arrays.append(a)
        in_specs.append(pl.BlockSpec((tk, tm), lambda i, j, k: (k, i)) if ta
                        else pl.BlockSpec((tm, tk), lambda i, j, k: (i, k)))
    for _, b in pairs:
        arrays.append(b)
        in_specs.append(pl.BlockSpec((tn, tk), lambda i, j, k: (j + b_noff // tn, k)) if tb
                        else pl.BlockSpec((tk, tn), lambda i, j, k: (k, j + b_noff // tn)))
    for arr, kind, off in extras:
        ob = off // tn
        assert off % tn == 0
        arrays.append(arr)
        if kind == "tile":
            in_specs.append(pl.BlockSpec((tm, tn), lambda i, j, k, ob=ob: (i, j + ob)))
        else:
            in_specs.append(pl.BlockSpec((1, tn), lambda i, j, k, ob=ob: (0, j + ob)))
    out_shape, out_specs = [], []
    for o in outs:
        if o["kind"] == "tile":
            out_shape.append(jax.ShapeDtypeStruct((M, N), o["dtype"]))
            out_specs.append(pl.BlockSpec((tm, tn), lambda i, j, k: (i, j)))
        else:
            out_shape.append(jax.ShapeDtypeStruct((n_i, 1, N), F32))
            out_specs.append(pl.BlockSpec((1, 1, tn), lambda i, j, k: (i, 0, j)))
    dims = (((0 if ta else 1,), (1 if tb else 0,)), ((), ()))

    def body(*refs):
        a_refs, b_refs = refs[:n_p], refs[n_p:2 * n_p]
        e_refs = refs[2 * n_p:2 * n_p + n_e]
        o_refs = refs[2 * n_p + n_e:2 * n_p + n_e + len(outs)]
        acc_refs = refs[2 * n_p + n_e + len(outs):]

        def product(p):
            a, b = a_refs[p][...], b_refs[p][...]
            if a_pro is not None:
                a = a_pro(a)
            if b_pro is not None:
                b = b_pro(b)
            return lax.dot_general(a, b, dims, preferred_element_type=F32)

        def write(accs):
            vals = epi(accs, [e[...] for e in e_refs])
            for o, o_ref, val in zip(outs, o_refs, vals):
                if o["kind"] == "tile":
                    o_ref[...] = val.astype(o_ref.dtype)
                else:
                    o_ref[0] = val

        if n_k == 1:
            write([product(p) for p in range(n_p)])
            return
        k = pl.program_id(2)

        @pl.when(k == 0)
        def _():
            for acc in acc_refs:
                acc[...] = jnp.zeros_like(acc)

        for p in range(n_p):
            acc_refs[p][...] += product(p)

        pl.when(k == n_k - 1)(lambda: write([acc[...] for acc in acc_refs]))

    return _ride(name, body, riders, arrays, grid=(n_i, n_j, n_k), in_specs=in_specs, out_specs=out_specs,
                 out_shape=out_shape, scratch_shapes=[pltpu.VMEM((tm, tn), F32) for _ in pairs] if n_k > 1 else [],
                 sem=("parallel", "parallel", "arbitrary"))


def _tile_out(dtype):
    return {"kind": "tile", "dtype": dtype}


_COLSUM = {"kind": "colsum"}


def _colsum(v):
    return jnp.sum(v, axis=0, keepdims=True)


def _norm_mod(name, x, norm_w, scale, shift):
    S, D = x.shape
    tr = _tile(S, 256)

    def body(x_ref, nw_ref, sc_ref, sh_ref, h_ref):
        xv = x_ref[...]
        r = lax.rsqrt(jnp.mean(xv * xv, axis=-1, keepdims=True) + EPS)
        h_ref[...] = ((xv * r * nw_ref[...]) * (1.0 + sc_ref[...]) + sh_ref[...]).astype(BF16)

    row = pl.BlockSpec((1, D), lambda i: (0, 0))
    til = pl.BlockSpec((tr, D), lambda i: (i, 0))
    return _pcall(body, name=name, grid=(S // tr,), in_specs=[til, row, row, row], out_specs=til,
                  out_shape=jax.ShapeDtypeStruct((S, D), BF16), compiler_params=_params(("parallel",)))(
                      x, norm_w, scale, shift)


def _norm_mod_bwd(name, dh, x, dres, norm_w, scale, gate_o=None):
    S, D = x.shape
    tr = _tile(S, 256)
    n_r = S // tr
    with_gate = gate_o is not None
    dh = list(dh) if isinstance(dh, (list, tuple)) else [dh]
    n_dh = len(dh)

    def body(*refs):
        dh_refs, refs = refs[:n_dh], refs[n_dh:]
        if with_gate:
            x_ref, dres_ref, nw_ref, sc_ref, o_ref, g_ref, dx_ref, p1, p2, p3, do_ref, p4 = refs
        else:
            x_ref, dres_ref, nw_ref, sc_ref, dx_ref, p1, p2, p3 = refs
        dhv = dh_refs[0][...] if n_dh == 1 else jnp.concatenate([r[...] for r in dh_refs], axis=1)
        xv, nw = x_ref[...], nw_ref[...]
        r = lax.rsqrt(jnp.mean(xv * xv, axis=-1, keepdims=True) + EPS)
        xh = xv * r
        p1[0] = _colsum(dhv)
        p2[0] = _colsum(dhv * (xh * nw))
        dn = dhv * (1.0 + sc_ref[...])
        p3[0] = _colsum(dn * xh)
        dxh = dn * nw
        dx = dres_ref[...] + r * (dxh - xh * jnp.mean(dxh * xh, axis=-1, keepdims=True))
        dx_ref[...] = dx
        if with_gate:
            do_ref[...] = (dx * g_ref[...]).astype(BF16)
            p4[0] = _colsum(dx * o_ref[...].astype(F32))

    row = pl.BlockSpec((1, D), lambda i: (0, 0))
    til = pl.BlockSpec((tr, D), lambda i: (i, 0))
    part = pl.BlockSpec((1, 1, D), lambda i: (i, 0, 0))
    part_shape = jax.ShapeDtypeStruct((n_r, 1, D), F32)
    in_specs = [pl.BlockSpec((tr, D // n_dh), lambda i: (i, 0))] * n_dh + [til, til, row, row]
    arrays = dh + [x, dres, norm_w, scale]
    out_specs = [til, part, part, part]
    out_shape = [jax.ShapeDtypeStruct((S, D), F32), part_shape, part_shape, part_shape]
    if with_gate:
        in_specs += [til, row]
        arrays += list(gate_o)
        out_specs += [til, part]
        out_shape += [jax.ShapeDtypeStruct((S, D), BF16), part_shape]
    return _pcall(body, name=name, grid=(n_r,), in_specs=in_specs, out_specs=out_specs, out_shape=out_shape,
                  compiler_params=_params(("parallel",)))(*arrays)


def _pool_w_specs(rows, cg):
    return [pl.BlockSpec((rows, cg), lambda g, j=j: (N_GROUPS * j + g, 0)) for j in range(N_CHIPS)]


def _pool_fwd(proj, wp_full, pool_scale, S, PW):
    cg = PW // N_GROUPS
    rows = cg // N_CHIPS
    T = _tile(S, POOL_T)
    n_t = S // T

    def body(u_ref, w0, w1, w2, w3, ps_ref, pooled_ref, pa_ref):
        g = pl.program_id(0)
        win = jnp.left_shift(2, g)
        w = jnp.concatenate([w0[...], w1[...], w2[...], w3[...]], axis=0)
        t_i = lax.broadcasted_iota(jnp.int32, (T, T), 0)
        j_i = lax.broadcasted_iota(jnp.int32, (T, T), 1)
        b_cur = ((j_i <= t_i) & (j_i > t_i - win)).astype(BF16)
        b_prev = (j_i - T > t_i - win).astype(BF16)
        row = lax.broadcasted_iota(jnp.int32, (T, 1), 0)
        for r in range(n_t):
            cur = u_ref[r * T:(r + 1) * T, :]
            ws = jnp.dot(b_cur, cur, preferred_element_type=F32)
            if r > 0:
                ws += jnp.dot(b_prev, u_ref[(r - 1) * T:r * T, :], preferred_element_type=F32)
            count = jnp.minimum(row + (r * T + 1), win).astype(F32)
            pooled = (ws / count - cur.astype(F32)).astype(BF16)
            pooled_ref[r * T:(r + 1) * T, :] = pooled
            mixed = jnp.dot(pooled, w, preferred_element_type=F32)
            pa_ref[r * T:(r + 1) * T, :] = (mixed * ps_ref[...]).astype(BF16)

    col = pl.BlockSpec((S, cg), lambda g: (0, g))
    return _pcall(
        body, name="pool_fwd", grid=(N_GROUPS,),
        in_specs=[col] + _pool_w_specs(rows, cg) + [pl.BlockSpec((1, cg), lambda g: (0, g))],
        out_specs=[col, col],
        out_shape=[jax.ShapeDtypeStruct((S, PW), BF16), jax.ShapeDtypeStruct((S, PW), BF16)],
        compiler_params=_params(("parallel",)),
    )(proj, wp_full, wp_full, wp_full, wp_full, pool_scale)


def _pool_bwd(dpa, pooled, wp_full, pool_scale, S, PW):
    cg = PW // N_GROUPS
    rows = cg // N_CHIPS
    T = _tile(S, POOL_T)
    n_t = S // T

    def body(dpa_ref, pooled_ref, w0, w1, w2, w3, ps_ref, du_ref, gw_ref, gs_ref, dp_s, dpc_s, dmx_s):
        g = pl.program_id(0)
        win = jnp.left_shift(2, g)
        w = jnp.concatenate([w0[...], w1[...], w2[...], w3[...]], axis=0)
        row = lax.broadcasted_iota(jnp.int32, (T, 1), 0)
        gs = jnp.zeros((1, cg), F32)
        for r in range(n_t):
            sl = slice(r * T, (r + 1) * T)
            mixed = jnp.dot(pooled_ref[sl, :], w, preferred_element_type=F32)
            dpa_t = dpa_ref[sl, :]
            gs += _colsum(dpa_t * mixed)
            dmx = (dpa_t * ps_ref[...]).astype(BF16)
            dmx_s[sl, :] = dmx
            dpo = lax.dot_general(dmx, w, (((1,), (1,)), ((), ())), preferred_element_type=F32)
            dp_s[sl, :] = dpo
            count = jnp.minimum(row + (r * T + 1), win).astype(F32)
            dpc_s[sl, :] = (dpo / count).astype(BF16)
        gs_ref[...] = gs
        gw = lax.dot_general(pooled_ref[...], dmx_s[...], (((0,), (0,)), ((), ())), preferred_element_type=F32)
        for j in range(N_CHIPS):
            gw_ref[j, 0] = gw[j * rows:(j + 1) * rows, :].astype(BF16)
        j_i = lax.broadcasted_iota(jnp.int32, (T, T), 0)
        t_i = lax.broadcasted_iota(jnp.int32, (T, T), 1)
        b_cur = ((t_i >= j_i) & (t_i < j_i + win)).astype(BF16)
        b_next = (t_i + T < j_i + win).astype(BF16)
        for r in range(n_t):
            sl = slice(r * T, (r + 1) * T)
            acc = jnp.dot(b_cur, dpc_s[sl, :], preferred_element_type=F32)
            if r + 1 < n_t:
                acc += jnp.dot(b_next, dpc_s[(r + 1) * T:(r + 2) * T, :], preferred_element_type=F32)
            du_ref[sl, :] = (acc - dp_s[sl, :]).astype(BF16)

    col = pl.BlockSpec((S, cg), lambda g: (0, g))
    return _pcall(
        body, name="pool_bwd", grid=(N_GROUPS,),
        in_specs=[col, col] + _pool_w_specs(rows, cg) + [pl.BlockSpec((1, cg), lambda g: (0, g))],
        out_specs=[col, pl.BlockSpec((N_CHIPS, 1, rows, cg), lambda g: (0, g, 0, 0)),
                   pl.BlockSpec((1, cg), lambda g: (0, g))],
        out_shape=[jax.ShapeDtypeStruct((S, PW), BF16),
                   jax.ShapeDtypeStruct((N_CHIPS, N_GROUPS, rows, cg), BF16),
                   jax.ShapeDtypeStruct((1, PW), F32)],
        scratch_shapes=[pltpu.VMEM((S, cg), F32), pltpu.VMEM((S, cg), BF16), pltpu.VMEM((S, cg), BF16)],
        compiler_params=_params(("parallel",)),
    )(dpa, pooled, wp_full, wp_full, wp_full, wp_full, pool_scale)


_NT = (((1,), (1,)), ((), ()))
_TN = (((0,), (0,)), ((), ()))


def _split_dot(v, tri):
    hi = v.astype(BF16)
    lo = (v - hi.astype(F32)).astype(BF16)
    return jnp.dot(hi, tri, preferred_element_type=F32) + jnp.dot(lo, tri, preferred_element_type=F32)


LOG2E = 1.4426950408889634
QK_SCALE = 1.0 / math.sqrt(HEAD_DIM)


def _sb_scores(q2_i, k_j, tri_l, masked):
    tq, tk = q2_i.shape[0], k_j.shape[0]
    s = lax.dot_general(q2_i, k_j, _NT, preferred_element_type=F32)
    lp = jnp.log(1.0 + jnp.exp2(-jnp.abs(s))) * LOG2E
    lb = jnp.minimum(s, 0.0) - lp
    l = lb - s
    mask = None
    if masked:
        mask = lax.broadcasted_iota(jnp.int32, (tq, tk), 0) > lax.broadcasted_iota(jnp.int32, (tq, tk), 1)
        l = jnp.where(mask, l, 0.0)
    return l, lb, lb + _split_dot(l, tri_l), mask


def _sb_weights(t, carry_l, mask):
    a = jnp.exp2(t + carry_l)
    return a if mask is None else jnp.where(mask, a, 0.0)


def _rowsum(v):
    return jnp.sum(v, axis=1, keepdims=True)


def _qk_norm(x_ref, w_ref):
    xv = x_ref[...].astype(F32)
    r = lax.rsqrt(jnp.mean(xv * xv, axis=-1, keepdims=True) + EPS)
    return xv * r, r


def _attn_fwd(proj, q_norm_w, k_norm_w, S, H, q_off, riders=()):
    t = _tile(S, ATT_T)
    n_q = S // t

    def body(q_ref, k_ref, v_ref, qw_ref, kw_ref, att_ref, attf_ref, qn_s, kn_s):
        qh, _ = _qk_norm(q_ref, qw_ref)
        qn_s[...] = (qh * qw_ref[...] * (QK_SCALE * LOG2E)).astype(BF16)
        kh, _ = _qk_norm(k_ref, kw_ref)
        kn_s[...] = (kh * kw_ref[...]).astype(BF16)
        tri_l = (lax.broadcasted_iota(jnp.int32, (t, t), 0) > lax.broadcasted_iota(jnp.int32, (t, t), 1)).astype(BF16)

        def rows(j):
            return pl.ds(pl.multiple_of(j * t, t), t)

        def q_step(i, _):
            q_i = qn_s[rows(i), :]

            def av(a, j):
                return jnp.dot(a.astype(BF16), v_ref[rows(j), :], preferred_element_type=F32)

            l, _, tt, mask = _sb_scores(q_i, kn_s[rows(i), :], tri_l, True)
            acc = av(_sb_weights(tt, 0.0, mask), i)
            carry = _rowsum(l)

            def single(_, c):
                carry, acc = c
                l, _, tt, _ = _sb_scores(q_i, kn_s[rows(i - 1), :], tri_l, False)
                return carry + _rowsum(l), acc + av(_sb_weights(tt, carry, None), i - 1)

            carry, acc = lax.fori_loop(0, i % 2, single, (carry, acc))
            top = i - 1 - i % 2

            def pair(p, c):
                carry, acc = c
                j0 = top - 2 * p
                l0, _, t0, _ = _sb_scores(q_i, kn_s[rows(j0), :], tri_l, False)
                l1, _, t1, _ = _sb_scores(q_i, kn_s[rows(j0 - 1), :], tri_l, False)
                mid = carry + _rowsum(l0)
                acc = acc + av(_sb_weights(t0, carry, None), j0) + av(_sb_weights(t1, mid, None), j0 - 1)
                return mid + _rowsum(l1), acc

            _, acc = lax.fori_loop(0, i // 2, pair, (carry, acc))
            att_ref[rows(i), :] = acc.astype(BF16)
            attf_ref[rows(i), :] = acc
            return 0

        lax.fori_loop(0, n_q, q_step, 0)

    def col(off):
        return pl.BlockSpec((S, HEAD_DIM), lambda h, off=off: (0, off + h))

    wspec = pl.BlockSpec((1, HEAD_DIM), lambda h: (0, 0))
    return _ride(
        "attn_fwd", body, riders, [proj, proj, proj, q_norm_w, k_norm_w], grid=(H,),
        in_specs=[col(q_off), col(q_off + H), col(q_off + 2 * H), wspec, wspec],
        out_specs=[col(0), col(0)],
        out_shape=[jax.ShapeDtypeStruct((S, H * HEAD_DIM), BF16), jax.ShapeDtypeStruct((S, H * HEAD_DIM), F32)],
        scratch_shapes=[pltpu.VMEM((S, HEAD_DIM), BF16), pltpu.VMEM((S, HEAD_DIM), BF16)],
        sem=("parallel",))


def _attn_bwd(proj, datt, attf, q_norm_w, k_norm_w, S, H, q_off, riders=()):
    t = _tile(S, ATT_T)
    n_q = S // t

    def body(q_ref, k_ref, v_ref, do_ref, o_ref, qw_ref, kw_ref, dq_ref, dk_ref, dv_ref, gq_ref, gk_ref,
             qn_s, kn_s, qz_s, kz_s, dk_s, dv_s, gq_s):
        qw, kw = qw_ref[...], kw_ref[...]
        qh, _ = _qk_norm(q_ref, qw_ref)
        qn_s[...] = (qh * qw * (QK_SCALE * LOG2E)).astype(BF16)
        qz_s[...] = (qh * qw * QK_SCALE).astype(BF16)
        kh, _ = _qk_norm(k_ref, kw_ref)
        kn_s[...] = (kh * kw).astype(BF16)
        kz_s[...] = (kh * kw * QK_SCALE).astype(BF16)
        dk_s[...] = jnp.zeros_like(dk_s)
        dv_s[...] = jnp.zeros_like(dv_s)
        gq_s[...] = jnp.zeros_like(gq_s)
        r_i = lax.broadcasted_iota(jnp.int32, (t, t), 0)
        c_i = lax.broadcasted_iota(jnp.int32, (t, t), 1)
        tri_l = (r_i > c_i).astype(BF16)
        tri_e = (r_i >= c_i).astype(BF16)

        def rows(j):
            return pl.ds(pl.multiple_of(j * t, t), t)

        def q_step(i, _):
            q_i = qn_s[rows(i), :]
            do_i = do_ref[rows(i), :]
            d_i = _rowsum(do_i.astype(F32) * o_ref[rows(i), :])

            def scores(j, masked):
                l, lb, tt, mask = _sb_scores(q_i, kn_s[rows(j), :], tri_l, masked)
                da = lax.dot_general(do_i, v_ref[rows(j), :], _NT, preferred_element_type=F32)
                return l, lb, tt, mask, da

            def grads(j, sc, carry_l, carry_e, dq_acc):
                l, lb, tt, mask, da = sc
                a_bf = _sb_weights(tt, carry_l, mask).astype(BF16)
                e = da * a_bf.astype(F32)
                p = (d_i - carry_e) - _split_dot(e, tri_e)
                dz = e - jnp.exp2(lb) * (e + p)
                if mask is not None:
                    dz = jnp.where(mask, dz, 0.0)
                dz = dz.astype(BF16)
                dk_s[rows(j), :] += lax.dot_general(dz, qz_s[rows(i), :], _TN, preferred_element_type=F32)
                dv_s[rows(j), :] += lax.dot_general(a_bf, do_i, _TN, preferred_element_type=F32)
                return (carry_l + _rowsum(l), carry_e + _rowsum(e),
                        dq_acc + jnp.dot(dz, kz_s[rows(j), :], preferred_element_type=F32))

            zero = jnp.zeros((t, 1), F32)
            first = (zero, zero, jnp.zeros((t, HEAD_DIM), F32))

            def diagonal_alone():
                return grads(i, scores(i, True), *first)

            def diagonal_and_left():
                s0, s1 = scores(i, True), scores(i - 1, False)
                return grads(i - 1, s1, *grads(i, s0, *first))

            c = lax.cond(i > 0, diagonal_and_left, diagonal_alone)
            rest = jnp.maximum(i - 1, 0)
            c = lax.fori_loop(0, rest % 2, lambda _, c: grads(i - 2, scores(i - 2, False), *c), c)
            top = i - 2 - rest % 2

            def pair(p, c):
                j0 = top - 2 * p
                s0, s1 = scores(j0, False), scores(j0 - 1, False)
                return grads(j0 - 1, s1, *grads(j0, s0, *c))

            _, _, dqn = lax.fori_loop(0, rest // 2, pair, c)
            qv = q_ref[rows(i), :].astype(F32)
            r = lax.rsqrt(jnp.mean(qv * qv, axis=-1, keepdims=True) + EPS)
            xh = qv * r
            gq_s[...] += _colsum(dqn * xh)
            dxh = dqn * qw
            dq_ref[rows(i), :] = (r * (dxh - xh * jnp.mean(dxh * xh, axis=-1, keepdims=True))).astype(BF16)
            return 0

        lax.fori_loop(0, n_q, q_step, 0)
        gq_ref[0] = gq_s[...]
        kh, rk = _qk_norm(k_ref, kw_ref)
        dkn = dk_s[...]
        gk_ref[0] = _colsum(dkn * kh)
        dxh = dkn * kw
        dk_ref[...] = (rk * (dxh - kh * jnp.mean(dxh * kh, axis=-1, keepdims=True))).astype(BF16)
        dv_ref[...] = dv_s[...].astype(BF16)

    def col(off):
        return pl.BlockSpec((S, HEAD_DIM), lambda h, off=off: (0, off + h))

    wspec = pl.BlockSpec((1, HEAD_DIM), lambda h: (0, 0))
    gspec = pl.BlockSpec((1, 1, HEAD_DIM), lambda h: (h, 0, 0))
    act = jax.ShapeDtypeStruct((S, H * HEAD_DIM), BF16)
    gsh = jax.ShapeDtypeStruct((H, 1, HEAD_DIM), F32)
    return _ride(
        "attn_bwd", body, riders, [proj, proj, proj, datt, attf, q_norm_w, k_norm_w], grid=(H,),
        in_specs=[col(q_off), col(q_off + H), col(q_off + 2 * H), col(0), col(0), wspec, wspec],
        out_specs=[col(0), col(0), col(0), gspec, gspec],
        out_shape=[act, act, act, gsh, gsh],
        scratch_shapes=[pltpu.VMEM((S, HEAD_DIM), BF16)] * 4 + [pltpu.VMEM((S, HEAD_DIM), F32)] * 2
        + [pltpu.VMEM((1, HEAD_DIM), F32)],
        sem=("parallel",))


def _place():
    x, y, c = lax.axis_index("x"), lax.axis_index("y"), lax.axis_index("c")
    chips = [(1 - x, y), (x, 1 - y), (1 - x, 1 - y)]
    return x, y, c, chips


def _dev_allgather(name, v):
    m_per, n = v.shape

    def body(x_ref, out_ref, send_sems, recv_sems, local_sem):
        x, y, c, chips = _place()
        me, sibling = (x, y, c), (x, y, 1 - c)

        def rows(px, py, pc):
            return out_ref.at[pl.ds((4 * px + 2 * py + pc) * m_per, m_per), :]

        def copy(k, block, to, src=None):
            return pltpu.make_async_remote_copy(
                src_ref=rows(*block) if src is None else src, dst_ref=rows(*block),
                send_sem=send_sems.at[k], recv_sem=recv_sems.at[k], device_id=to, device_id_type=MESH)

        mine = pltpu.make_async_copy(x_ref, rows(*me), local_sem)
        mine.start()
        first = [copy(0, me, sibling, src=x_ref)]
        first += [copy(1 + j, me, (*chip, c), src=x_ref) for j, chip in enumerate(chips)]
        for cp in first:
            cp.start()
        passed = [copy(4 + j, (*chip, c), sibling) for j, chip in enumerate(chips)]
        for j, chip in enumerate(chips):
            copy(1 + j, (*chip, c), me).wait_recv()
            passed[j].start()
        copy(0, sibling, me).wait_recv()
        for j, chip in enumerate(chips):
            copy(4 + j, (*chip, 1 - c), me).wait_recv()
        for cp in first + passed:
            cp.wait_send()
        mine.wait()

    return _pcall(
        body, name=name, out_shape=jax.ShapeDtypeStruct((N_DEV * m_per, n), v.dtype),
        in_specs=[pl.BlockSpec(memory_space=pltpu.VMEM)], out_specs=pl.BlockSpec(memory_space=pltpu.VMEM),
        scratch_shapes=[pltpu.SemaphoreType.DMA((7,)), pltpu.SemaphoreType.DMA((7,)), pltpu.SemaphoreType.DMA],
        compiler_params=pltpu.CompilerParams(vmem_limit_bytes=VMEM_LIMIT_V7X),
    )(v)


class _W:
    def __init__(self, name, kind, R, C):
        self.name, self.kind, self.R, self.C = name, kind, R, C

    @property
    def shard_shape(self):
        return (self.R, self.C // N_CHIPS) if self.kind == "col" else (self.R // N_CHIPS, self.C)

    @property
    def half_rows(self):
        return self.shard_shape[0] // 2

    def shard_half(self, ref, half):
        return ref.at[pl.ds(half * self.half_rows, self.half_rows), :]

    def region(self, full_ref, chip, half):
        hr = self.half_rows
        if self.kind == "col":
            cw = self.C // N_CHIPS
            return full_ref.at[pl.ds(half * hr, hr), pl.ds(chip * cw, cw)]
        return full_ref.at[pl.ds(chip * (2 * hr) + half * hr, hr), :]

    def region_both(self, full_ref, chip):
        hr = self.half_rows
        if self.kind == "col":
            cw = self.C // N_CHIPS
            return full_ref.at[:, pl.ds(chip * cw, cw)]
        return full_ref.at[pl.ds(chip * (2 * hr), 2 * hr), :]


def _ag_rider(ws, fulls, n_ch=4, chunks=None):
    n_w = len(ws)
    lo, hi = chunks or (0, n_ch)
    per = 6

    def parts(full, sems):
        send_sems, recv_sems = sems
        x, y, c, _ = _place()
        xn, yn, dg = (1 - x, y), (x, 1 - y), (1 - x, 1 - y)
        via = (x + (1 - c) * (1 - 2 * x), y + c * (1 - 2 * y))
        to = (x + c * (1 - 2 * x), y + (1 - c) * (1 - 2 * y))

        def reg(i, chip, half, t):
            nr = ws[i].half_rows // n_ch
            return ws[i].region(full[i], 2 * chip[0] + chip[1], half).at[pl.ds(t * nr, nr), :]

        def copy(r, i, t, k, dev):
            s = (i * (hi - lo) + t - lo) * per + k
            return pltpu.make_async_remote_copy(src_ref=r, dst_ref=r, send_sem=send_sems.at[s],
                                                recv_sem=recv_sems.at[s], device_id=dev, device_id_type=MESH)

        def direct(i, t, k):
            return copy(reg(i, (x, y), c, t), i, t, k, (*(via, to)[k], c))

        def direct_in(i, t, k):
            return copy(reg(i, (via, to)[k], c, t), i, t, k, (*(via, to)[k], c))

        def relay(i, t):
            return copy(reg(i, via, c, t), i, t, 2, (*to, c))

        def relay_in(i, t):
            return copy(reg(i, dg, c, t), i, t, 2, (*to, c))

        def hand(i, t, k, half):
            return copy(reg(i, (xn, yn, dg)[k], half, t), i, t, 3 + k, (x, y, 1 - c))

        return c, direct, direct_in, relay, relay_in, hand

    def start(_, full, sems):
        _, direct, _, _, _, _ = parts(full, sems)
        for t in range(lo, hi):
            for i in range(n_w):
                direct(i, t, 0).start()
                direct(i, t, 1).start()

    def arrived(t):
        def step(_, full, sems):
            c, _, direct_in, relay, relay_in, hand = parts(full, sems)
            for i in range(n_w):
                direct_in(i, t, 0).wait_recv()
                direct_in(i, t, 1).wait_recv()
                relay(i, t).start()
                hand(i, t, 0, c).start()
                hand(i, t, 1, c).start()
        return step

    def finish(_, full, sems):
        c, direct, _, relay, relay_in, hand = parts(full, sems)
        for t in range(lo, hi):
            for i in range(n_w):
                relay_in(i, t).wait_recv()
                hand(i, t, 2, c).start()
        for i in range(n_w):
            for t in range(lo, hi):
                for k in range(3):
                    hand(i, t, k, 1 - c).wait_recv()
        for i in range(n_w):
            for t in range(lo, hi):
                direct(i, t, 0).wait_send()
                direct(i, t, 1).wait_send()
                relay(i, t).wait_send()
                for k in range(3):
                    hand(i, t, k, c).wait_send()

    n_sem = per * (hi - lo) * n_w
    return _Rider(fulls, [jax.ShapeDtypeStruct((w.R, w.C), BF16) for w in ws],
                  [pltpu.SemaphoreType.DMA((n_sem,)), pltpu.SemaphoreType.DMA((n_sem,))], start, finish,
                  steps=[arrived(t) for t in range(lo, hi)], aliases={i: i for i in range(n_w)})


def _cast_into_full(w, a32, chip_arr):
    sr, sc = w.shard_shape
    tr, tc = _tile(sr, 512), _tile(sc, 2048)
    n_r, n_c = sr // tr, sc // tc
    if w.kind == "col":
        out_spec = pl.BlockSpec((tr, tc), lambda i, j, chip: (i, chip[0] * n_c + j))
    else:
        out_spec = pl.BlockSpec((tr, tc), lambda i, j, chip: (chip[0] * n_r + i, j))

    def body(chip_ref, a_ref, o_ref):
        o_ref[...] = a_ref[...].astype(BF16)

    return _pcall(
        body, name="cast_" + w.name, out_shape=jax.ShapeDtypeStruct((w.R, w.C), BF16),
        grid_spec=pltpu.PrefetchScalarGridSpec(
            num_scalar_prefetch=1, grid=(n_r, n_c),
            in_specs=[pl.BlockSpec((tr, tc), lambda i, j, chip: (i, j))], out_specs=out_spec),
        compiler_params=_params(("parallel", "parallel")),
    )(chip_arr, a32)


def _half_view(w, g):
    return g if w.kind == "col" else g.reshape(N_CHIPS, w.R // N_CHIPS, w.C)


def _px_rider(ws, grads):
    n_w = len(ws)

    def copies(g, got, sems):
        send_sems, recv_sems = sems
        x, y, c, _ = _place()

        def half_all(w, ref, half):
            hr = w.half_rows
            if w.kind == "col":
                return ref.at[pl.ds(half * hr, hr), :]
            return ref.at[:, pl.ds(half * hr, hr), :]

        return [pltpu.make_async_remote_copy(
            src_ref=half_all(w, g[i], 1 - c), dst_ref=got[i], send_sem=send_sems.at[i], recv_sem=recv_sems.at[i],
            device_id=(x, y, 1 - c), device_id_type=MESH) for i, w in enumerate(ws)]

    def start(g, got, sems):
        for cp in copies(g, got, sems):
            cp.start()

    def finish(g, got, sems):
        for cp in copies(g, got, sems):
            cp.wait_recv()
            cp.wait_send()

    def got_shape(w):
        hr = w.half_rows
        return (hr, w.C) if w.kind == "col" else (N_CHIPS, hr, w.C)

    return _Rider([_half_view(w, g) for w, g in zip(ws, grads)],
                  [jax.ShapeDtypeStruct(got_shape(w), BF16) for w in ws],
                  [pltpu.SemaphoreType.DMA((n_w,)), pltpu.SemaphoreType.DMA((n_w,))], start, finish)


def _pair_sum(w, g, got, c_arr):
    hr = w.half_rows
    if w.kind == "col":
        tr, tc = _tile(hr, 512), _tile(w.C, 2048)
        n_r = hr // tr
        grid = (n_r, w.C // tc)
        g_spec = pl.BlockSpec((tr, tc), lambda i, j, c: (c[0] * n_r + i, j))
        o_spec = pl.BlockSpec((tr, tc), lambda i, j, c: (i, j))
    else:
        tr = _tile(hr, 512)
        n_r = hr // tr
        grid = (N_CHIPS, n_r)
        g_spec = pl.BlockSpec((1, tr, w.C), lambda s, i, c: (s, c[0] * n_r + i, 0))
        o_spec = pl.BlockSpec((1, tr, w.C), lambda s, i, c: (s, i, 0))

    def body(c_ref, g_ref, got_ref, out_ref):
        out_ref[...] = (g_ref[...].astype(F32) + got_ref[...].astype(F32)).astype(BF16)

    return _pcall(
        body, name="grad_pair_sum_" + w.name, out_shape=jax.ShapeDtypeStruct(got.shape, BF16),
        grid_spec=pltpu.PrefetchScalarGridSpec(num_scalar_prefetch=1, grid=grid, in_specs=[g_spec, o_spec],
                                               out_specs=o_spec),
        compiler_params=_params(("parallel", "parallel")),
    )(c_arr, _half_view(w, g), got)


def _cx_rider(ws, sums, part=(0, 1), q_in=None):
    n_w = len(ws)

    def parts(p, q, sems):
        send_sems, recv_sems = sems
        x, y, c, chips = _place()
        my_chip = 2 * x + y

        def rows(w, ref):
            nr = w.half_rows // part[1]
            return ref.at[pl.ds(part[0] * nr, nr), :]

        def piece(w, ref, chip):
            if w.kind == "col":
                cw = w.C // N_CHIPS
                return rows(w, ref.at[:, pl.ds(chip * cw, cw)])
            return rows(w, ref.at[chip])

        def copy(i, k, recv=False):
            chip = chips[k]
            to_chip = 2 * chip[0] + chip[1]
            return pltpu.make_async_remote_copy(
                src_ref=piece(ws[i], p[i], to_chip), dst_ref=rows(ws[i], q[i].at[to_chip if recv else my_chip]),
                send_sem=send_sems.at[3 * i + k], recv_sem=recv_sems.at[3 * i + k],
                device_id=(*chip, c), device_id_type=MESH)

        return copy

    both = [(i, k) for i in range(n_w) for k in range(N_CHIPS - 1)]

    def start(p, q, sems):
        copy = parts(p, q, sems)
        for i, k in both:
            copy(i, k).start()

    def finish(p, q, sems):
        copy = parts(p, q, sems)
        for i, k in both:
            copy(i, k, recv=True).wait_recv()
        for i, k in both:
            copy(i, k).wait_send()

    return _Rider(list(sums) + list(q_in or []),
                  [jax.ShapeDtypeStruct((N_CHIPS, w.half_rows, w.shard_shape[1]), BF16) for w in ws],
                  [pltpu.SemaphoreType.DMA((3 * n_w,)), pltpu.SemaphoreType.DMA((3 * n_w,))], start, finish,
                  aliases={n_w + i: i for i in range(n_w)} if q_in else None)


def _chip_sum(w, p, q, cc_arr):
    hr, cols = w.half_rows, w.shard_shape[1]
    tr, tc = _tile(hr, 512), _tile(cols, 2048)
    n_r, n_c = hr // tr, cols // tc

    def body(cc_ref, own, q1, q2, q3, out_ref):
        own_v = own[...] if w.kind == "col" else own[0]
        out_ref[...] = ((own_v.astype(F32) + q1[0].astype(F32)) + q2[0].astype(F32)) + q3[0].astype(F32)

    if w.kind == "col":
        own_spec = pl.BlockSpec((tr, tc), lambda i, j, cc: (i, cc[1] * n_c + j))
    else:
        own_spec = pl.BlockSpec((1, tr, tc), lambda i, j, cc: (cc[1], i, j))
    q_specs = [pl.BlockSpec((1, tr, tc), lambda i, j, cc, s=s: ((cc[1] + s) % N_CHIPS, i, j)) for s in (1, 2, 3)]
    return _pcall(
        body, name="grad_chip_sum_" + w.name, out_shape=jax.ShapeDtypeStruct(w.shard_shape, F32),
        grid_spec=pltpu.PrefetchScalarGridSpec(
            num_scalar_prefetch=1, grid=(n_r, n_c), in_specs=[own_spec] + q_specs,
            out_specs=pl.BlockSpec((tr, tc), lambda i, j, cc: (cc[0] * n_r + i, j))),
        compiler_params=_params(("parallel", "parallel")),
    )(cc_arr, p, q, q, q)


_SEM = pl.BlockSpec(memory_space=pltpu.SEMAPHORE)
_HBM = pl.BlockSpec(memory_space=pltpu.HBM)


def _cx_split_copies(ws, p, land, send_sems, recv_sems):
    x, y, c, chips = _place()
    my_chip = 2 * x + y
    pairs = []
    for i, w in enumerate(ws):
        for k, chip in enumerate(chips):
            to_chip = 2 * chip[0] + chip[1]
            src = p[i].at[:, pl.ds(to_chip * (w.C // N_CHIPS), w.C // N_CHIPS)] if w.kind == "col" else p[i].at[to_chip]
            kw = dict(send_sem=send_sems.at[3 * i + k], recv_sem=recv_sems.at[3 * i + k], device_id=(*chip, c),
                      device_id_type=MESH)
            pairs.append((pltpu.make_async_remote_copy(src_ref=src, dst_ref=land[i].at[my_chip], **kw),
                          pltpu.make_async_remote_copy(src_ref=src, dst_ref=land[i].at[to_chip], **kw)))
    return pairs


def _cx_start(ws, sums):
    n_w = len(ws)
    lands = [lax.empty((N_CHIPS, w.half_rows, w.shard_shape[1]), BF16) for w in ws]

    def body(*refs):
        p, land = refs[:n_w], refs[n_w:2 * n_w]
        for out, _ in _cx_split_copies(ws, p, land, refs[2 * n_w], refs[2 * n_w + 1]):
            out.start()
        refs[-1][...] = jnp.zeros_like(refs[-1])

    arrays = [pltpu.with_memory_space_constraint(a, pltpu.HBM) for a in list(sums) + lands]
    res = _pcall(
        body, name="grad_last_exchange_start",
        out_shape=(pltpu.SemaphoreType.DMA((3 * n_w,)), pltpu.SemaphoreType.DMA((3 * n_w,)),
                   *[pltpu.HBM(a.shape, a.dtype) for a in arrays], jax.ShapeDtypeStruct((8, 128), F32)),
        in_specs=[_HBM] * (2 * n_w),
        out_specs=(_SEM, _SEM, *[_HBM] * (2 * n_w), pl.BlockSpec(memory_space=pltpu.VMEM)),
        input_output_aliases={i: 2 + i for i in range(2 * n_w)},
        compiler_params=pltpu.CompilerParams(has_side_effects=pltpu.SideEffectType.DATAFLOW_SIDE_EFFECTING),
    )(*arrays)
    return res[0], res[1], list(res[2:2 + n_w]), list(res[2 + n_w:2 + 2 * n_w]), res[-1]


def _cx_wait(ws, send_sems, recv_sems, sums, lands, after):
    n_w = len(ws)

    def body(*refs):
        p, land = refs[:n_w], refs[n_w:2 * n_w]
        for _, cp in _cx_split_copies(ws, p, land, refs[2 * n_w], refs[2 * n_w + 1]):
            cp.wait_send()
            cp.wait_recv()

    res = _pcall(
        body, name="grad_last_exchange_wait",
        out_shape=[pltpu.HBM(a.shape, a.dtype) for a in list(sums) + list(lands)],
        in_specs=[_HBM] * (2 * n_w) + [_SEM, _SEM] + [ANY] * len(after), out_specs=[_HBM] * (2 * n_w),
        input_output_aliases={i: i for i in range(2 * n_w)},
        compiler_params=pltpu.CompilerParams(has_side_effects=pltpu.SideEffectType.DATAFLOW_SIDE_EFFECTING),
    )(*sums, *lands, send_sems, recv_sems, *after)
    return list(res[:n_w]), list(res[n_w:])


def _sf_rider(ws, grads):
    n_w = len(ws)

    def copy(g, sems, i, half):
        send_sems, recv_sems = sems
        x, y, c, _ = _place()
        h = c if half == "mine" else 1 - c
        reg = ws[i].shard_half(g[i], h)
        return pltpu.make_async_remote_copy(src_ref=reg, dst_ref=reg, send_sem=send_sems.at[i], recv_sem=recv_sems.at[i],
                                            device_id=(x, y, 1 - c), device_id_type=MESH)

    def start(_, g, sems):
        for i in range(n_w):
            copy(g, sems, i, "mine").start()

    def finish(_, g, sems):
        for i in range(n_w):
            copy(g, sems, i, "other").wait_recv()
            copy(g, sems, i, "mine").wait_send()

    return _Rider(grads, [jax.ShapeDtypeStruct(w.shard_shape, F32) for w in ws],
                  [pltpu.SemaphoreType.DMA((n_w,)), pltpu.SemaphoreType.DMA((n_w,))], start, finish,
                  aliases={i: i for i in range(n_w)})


def _adamw_math(w, g, m, v):
    m = ADAM_B1 * m + (1.0 - ADAM_B1) * g
    v = ADAM_B2 * v + (1.0 - ADAM_B2) * (g * g)
    m_hat = m / (1.0 - ADAM_B1 ** ADAM_STEP)
    v_hat = v / (1.0 - ADAM_B2 ** ADAM_STEP)
    delta = -ADAM_LR * (m_hat / (jnp.sqrt(v_hat) + ADAM_EPS) + ADAM_WD * w)
    return delta, m, v


def _adamw(name, w, g, m, v, after=None):
    R, C = w.shape
    tr, tc = _tile(R, 256), _tile(C, 2048)
    behind = [] if after is None else [after]

    def body(w_ref, g_ref, m_ref, v_ref, *rest):
        g_out, d_out, m_out, v_out = rest[len(behind):]
        g = g_ref[...]
        g_out[...] = g
        d_out[...], m_out[...], v_out[...] = _adamw_math(w_ref[...], g, m_ref[...], v_ref[...])

    spec = pl.BlockSpec((tr, tc), lambda i, j: (i, j))
    sh = jax.ShapeDtypeStruct((R, C), F32)
    return _pcall(body, name=name, grid=(R // tr, C // tc), in_specs=[spec] * 4 + [ANY] * len(behind),
                  out_specs=[spec] * 4, out_shape=[sh] * 4, compiler_params=_params(("parallel", "parallel")))(
                      w, g, m, v, *behind)


def _ada_update(sct, dmod_sh, w, m, v, riders=()):
    R, C = w.shape
    tr, tc = _tile(R, 256), _tile(C, 1024)

    def body(s_ref, d_ref, w_ref, m_ref, v_ref, g_out, d_out, m_out, v_out):
        s, d = s_ref[...], d_ref[...]
        g = s[:, 0:1] * d[0:1, :]
        for b in range(1, N_DEV):
            g += s[:, b:b + 1] * d[b:b + 1, :]
        g_out[...] = g
        d_out[...], m_out[...], v_out[...] = _adamw_math(w_ref[...], g, m_ref[...], v_ref[...])

    spec = pl.BlockSpec((tr, tc), lambda i, j: (i, j))
    sh = jax.ShapeDtypeStruct((R, C), F32)
    return _ride(
        "ada_update", body, riders, [sct, dmod_sh, w, m, v], grid=(R // tr, C // tc),
        in_specs=[pl.BlockSpec((tr, N_DEV), lambda i, j: (i, 0)), pl.BlockSpec((N_DEV, tc), lambda i, j: (0, j)),
                  spec, spec, spec],
        out_specs=[spec] * 4, out_shape=[sh] * 4, scratch_shapes=[], sem=("parallel", "parallel"))


def _silu_rows(c_row):
    D = c_row.shape[1]

    def body(c_ref, o_ref):
        cv = c_ref[...]
        o_ref[...] = cv * jax.nn.sigmoid(cv)

    return _pcall(body, name="silu_c", out_shape=jax.ShapeDtypeStruct((1, D), F32))(c_row)


def _pack_partials(parts, widths, total):
    n = len(widths)

    def body(*refs):
        loss_p, out_ref = refs[n], refs[n + 1]
        off = 0
        for ref, wd in zip(refs[:n], widths):
            out_ref[:, off:off + wd] = jnp.sum(ref[...], axis=0)
            off += wd
        loss = jnp.sum(jnp.sum(loss_p[...], axis=0), axis=1, keepdims=True)
        out_ref[:, off:off + 128] = jnp.broadcast_to(loss, (1, 128))
        if off + 128 < total:
            out_ref[:, off + 128:total] = jnp.zeros((1, total - off - 128), F32)

    return _pcall(body, name="pack_partials", out_shape=jax.ShapeDtypeStruct((1, total), F32))(*parts)


def _small_update(gathered, offsets, params, loss_off):
    n_p = len(params)

    def over_devices(g_ref, off, wd):
        blk = g_ref[:, off:off + wd]
        g = blk[0:1, :]
        for b in range(1, N_DEV):
            g = g + blk[b:b + 1, :]
        return g

    def body(*refs):
        g_ref = refs[0]
        prm = refs[1:1 + 3 * n_p]
        outs = refs[1 + 3 * n_p:]
        outs[4 * n_p][...] = over_devices(g_ref, loss_off, 128)
        for i, (off, wd) in enumerate(offsets):
            g = over_devices(g_ref, off, wd)
            w, m, v = prm[3 * i][...], prm[3 * i + 1][...], prm[3 * i + 2][...]
            outs[4 * i][...] = g
            outs[4 * i + 1][...], outs[4 * i + 2][...], outs[4 * i + 3][...] = _adamw_math(w, g, m, v)

    flat = [a for t in params for a in t]
    out_shape = [jax.ShapeDtypeStruct(t[0].shape, F32) for t in params for _ in range(4)]
    out_shape.append(jax.ShapeDtypeStruct((1, 128), F32))
    return _pcall(body, name="small_update", out_shape=out_shape)(gathered, *flat)


def kernel(x, c, w_ada, b_ada, norm1_w, w_in, q_norm_w, k_norm_w, w_pool, pool_scale, w_a_up, w_b_up, w_o, norm2_w, w_ff1, w_ff2, loss_target, m_w_ada, m_b_ada, m_norm1_w, m_w_in, m_q_norm_w, m_k_norm_w, m_w_pool, m_pool_scale, m_w_a_up, m_w_b_up, m_w_o, m_norm2_w, m_w_ff1, m_w_ff2, v_w_ada, v_b_ada, v_norm1_w, v_w_in, v_q_norm_w, v_k_norm_w, v_w_pool, v_pool_scale, v_w_a_up, v_w_b_up, v_w_o, v_norm2_w, v_w_ff1, v_w_ff2):
    _, S, D = x.shape
    PW = D // 2
    H = PW // HEAD_DIM
    cg = PW // N_GROUPS
    IN = w_in.shape[2] * N_CHIPS
    FF = w_ff1.shape[2] * N_CHIPS
    A_COLS = w_ada.shape[2]
    xi, yi, ci = lax.axis_index("x"), lax.axis_index("y"), lax.axis_index("c")
    chip = 2 * xi + yi
    dev = 2 * chip + ci
    c_arr = jnp.reshape(ci, (1,)).astype(jnp.int32)
    x2, tgt = x[0], loss_target[0]

    ws = [_W("w_in", "col", D, IN), _W("w_pool", "row", PW, cg), _W("w_a_up", "col", PW, D),
          _W("w_b_up", "col", PW, D), _W("w_o", "row", D, D), _W("w_ff1", "col", D, FF), _W("w_ff2", "row", FF, D)]
    w32 = [w_in[0], w_pool[0].reshape(cg, cg), w_a_up[0], w_b_up[0], w_o[0], w_ff1[0], w_ff2[0]]
    m32 = [m_w_in[0], m_w_pool[0].reshape(cg, cg), m_w_a_up[0], m_w_b_up[0], m_w_o[0], m_w_ff1[0], m_w_ff2[0]]
    v32 = [v_w_in[0], v_w_pool[0].reshape(cg, cg), v_w_a_up[0], v_w_b_up[0], v_w_o[0], v_w_ff1[0], v_w_ff2[0]]

    W_IN, W_POOL, W_A, W_B, W_O, W_FF1, W_FF2 = ws
    chip_arr = jnp.reshape(chip, (1,)).astype(jnp.int32)
    cc_arr = jnp.stack([ci, chip]).astype(jnp.int32)
    s_in, s_pool, s_a, s_b, s_o, s_ff1, s_ff2 = [_cast_into_full(w, a, chip_arr) for w, a in zip(ws, w32)]
    (win_f,) = _run_rider("gather_w_in", _ag_rider([W_IN], [s_in]))

    sc_row = _silu_rows(c)
    sc_all = _dev_allgather("gather_silu_c", sc_row.reshape(8, D // 8)).reshape(N_DEV, D)
    sc16 = jnp.concatenate([sc_all, jnp.zeros_like(sc_all)], axis=0)
    b_cols = lax.dynamic_slice(b_ada, (0, chip * A_COLS), (1, A_COLS))
    (mod_cols,) = _mm("mod_cols", [(sc16, w_ada[0])], M=2 * N_DEV, N=A_COLS, K=D, tm=16, tn=1024, tk=1024,
                      a_pro=lambda a: a.astype(BF16), b_pro=lambda b: b.astype(BF16),
                      extras=[(b_cols, "row", 0)], outs=[_tile_out(F32)], epi=lambda accs, ex: [accs[0] + ex[0]])
    mod_all = _dev_allgather("gather_mod", mod_cols[:N_DEV]).reshape(N_CHIPS, 2, N_DEV, A_COLS)
    mod_row = lax.dynamic_index_in_dim(mod_all[:, 0], dev, axis=1, keepdims=False).reshape(1, N_CHIPS * A_COLS)
    shift1, scale1, gate1, shift2, scale2, gate2 = [mod_row[:, i * D:(i + 1) * D] for i in range(6)]

    WIDE = dict(tm=2048, tn=512, tk=2048)
    DEEP = dict(tm=1024, tn=1024, tk=1024)
    h = _norm_mod("norm1_mod", x2, norm1_w, scale1, shift1)
    (proj,), ((wpool_f, wa_f, wb_f, wo_f),) = _mm(
        "in_proj", [(h, win_f)], M=S, N=IN, K=D, outs=[_tile_out(BF16)], epi=lambda accs, ex: [accs[0]], **WIDE,
        riders=[_ag_rider([W_POOL, W_A, W_B, W_O], [s_pool, s_a, s_b, s_o], n_ch=2)])
    pooled, pa = _pool_fwd(proj, wpool_f, pool_scale, S, PW)
    (att, attf), ((wff1_f,),) = _attn_fwd(proj, q_norm_w, k_norm_w, S, H, PW // HEAD_DIM,
                                          riders=[_ag_rider([W_FF1], [s_ff1])])

    def merge_epi(accs, ex):
        sa, sb = jax.nn.sigmoid(ex[0].astype(F32)), jax.nn.sigmoid(ex[1].astype(F32))
        return [sa * accs[0] + sb * accs[1], accs[0], accs[1]]

    (merged, ya, yb), (ff2_a,) = _mm("branch_up_merge", [(pa, wa_f), (att, wb_f)], M=S, N=D, K=PW,
                                     extras=[(proj, "tile", 4 * PW), (proj, "tile", 4 * PW + D)],
                                     outs=[_tile_out(BF16)] * 3, epi=merge_epi,
                                     riders=[_ag_rider([W_FF2], [s_ff2], chunks=(0, 1))])
    (x1, o), (ff2_b,) = _mm("out_proj", [(merged, wo_f)], M=S, N=D, K=D, extras=[(x2, "tile", 0), (gate1, "row", 0)],
                            outs=[_tile_out(F32), _tile_out(BF16)], epi=lambda accs, ex: [ex[0] + ex[1] * accs[0], accs[0]],
                            riders=[_ag_rider([W_FF2], ff2_a, chunks=(1, 2))], **WIDE)
    h2 = _norm_mod("norm2_mod", x1, norm2_w, scale2, shift2)
    (rl,), ((wff2_f,),) = _mm("ff1", [(h2, wff1_f)], M=S, N=FF, K=D, outs=[_tile_out(BF16)], **WIDE,
                              epi=lambda accs, ex: [jnp.maximum(accs[0], 0.0)],
                              riders=[_ag_rider([W_FF2], ff2_b, chunks=(2, 4))])

    def square(a):
        af = a.astype(F32)
        return (af * af).astype(BF16)

    def loss_epi(accs, ex):
        x1_t, tgt_t, g2 = ex
        f = accs[0]
        diff = (x1_t + g2 * f) - tgt_t
        dy = diff * (1.0 / D)
        return [dy, dy * g2, _colsum(dy * f), _colsum(diff * diff)]

    dy, df, dgate2_p, loss_p = _mm("ff2_loss", [(rl, wff2_f)], M=S, N=D, K=FF, a_pro=square, tm=1024, tn=1024, tk=512,
                                   extras=[(x1, "tile", 0), (tgt, "tile", 0), (gate2, "row", 0)],
                                   outs=[_tile_out(F32), _tile_out(BF16), _COLSUM, _COLSUM], epi=loss_epi)

    def pair_sums(group, partials, got):
        return [_pair_sum(w, g, r, c_arr) for w, g, r in zip(group, partials, got)]

    def chip_sums(group, sums, from_chips):
        return [_chip_sum(w, p, q, cc_arr) for w, p, q in zip(group, sums, from_chips)]

    first = lambda accs, ex: [accs[0]]
    gmm = dict(ta=True, outs=[_tile_out(BF16)], epi=first, **WIDE)
    (g_ff2,) = _mm("grad_w_ff2", [(rl, df)], M=FF, N=D, K=S, a_pro=square, ta=True, tm=512, tn=2048, tk=2048,
                   outs=[_tile_out(BF16)], epi=first)
    (dz1,), (got_ff2,) = _mm("d_ff_hidden", [(df, wff2_f)], M=S, N=FF, K=D, tb=True, extras=[(rl, "tile", 0)], **WIDE,
                             outs=[_tile_out(BF16)], epi=lambda accs, ex: [accs[0] * (2.0 * ex[0].astype(F32))],
                             riders=[_px_rider([W_FF2], [g_ff2])])
    sum_ff2 = pair_sums([W_FF2], [g_ff2], got_ff2)
    (g_ff1,), (q_ff2,) = _mm("grad_w_ff1", [(h2, dz1)], M=D, N=FF, K=S,
                             riders=[_cx_rider([W_FF2], sum_ff2, part=(0, 2))], **gmm)
    (dh2,), (got_ff1, q_ff2) = _mm("d_h2", [(dz1, wff1_f)], M=S, N=D, K=FF, tb=True, outs=[_tile_out(F32)], epi=first,
                                   riders=[_px_rider([W_FF1], [g_ff1]),
                                           _cx_rider([W_FF2], sum_ff2, part=(1, 2), q_in=q_ff2)], **DEEP)
    sum_ff1 = pair_sums([W_FF1], [g_ff1], got_ff1)
    dx1, dshift2_p, dscale2_p, gn2_p, do, dgate1_p = _norm_mod_bwd("norm2_bwd", dh2, x1, dy, norm2_w, scale2,
                                                                   gate_o=(o, gate1))
    (g_wo,) = _mm("grad_w_o", [(merged, do)], M=D, N=D, K=S, **gmm)

    def gate_epi(accs, ex):
        dm = accs[0]
        sa, sb = jax.nn.sigmoid(ex[0].astype(F32)), jax.nn.sigmoid(ex[1].astype(F32))
        ya_t, yb_t = ex[2].astype(F32), ex[3].astype(F32)
        return [dm * sa, dm * sb, dm * ya_t * (sa * (1.0 - sa)), dm * yb_t * (sb * (1.0 - sb))]

    dya, dyb, dga, dgb = _mm("d_merged", [(do, wo_f)], M=S, N=D, K=D, tb=True, tm=1024, tn=512, tk=2048,
                             extras=[(proj, "tile", 4 * PW), (proj, "tile", 4 * PW + D), (ya, "tile", 0), (yb, "tile", 0)],
                             outs=[_tile_out(BF16)] * 4, epi=gate_epi)
    (g_wa,) = _mm("grad_w_a_up", [(pa, dya)], M=PW, N=D, K=S, **gmm)
    (g_wb,) = _mm("grad_w_b_up", [(att, dyb)], M=PW, N=D, K=S, **gmm)
    (dpa,) = _mm("d_pool_out", [(dya, wa_f)], M=S, N=PW, K=D, tb=True, outs=[_tile_out(F32)], epi=first, **WIDE)
    mid = [W_A, W_B, W_O]
    (datt,), (got_mid,) = _mm("d_att", [(dyb, wb_f)], M=S, N=PW, K=D, tb=True, outs=[_tile_out(BF16)], epi=first, **WIDE,
                              riders=[_px_rider(mid, [g_wa, g_wb, g_wo])])
    sum_mid = pair_sums(mid, [g_wa, g_wb, g_wo], got_mid)
    du, g_wpool4, gscale_p = _pool_bwd(dpa, pooled, wpool_f, pool_scale, S, PW)
    (dq, dk, dv, gq_p, gk_p), ((q_ff1,),) = _attn_bwd(
        proj, datt, attf, q_norm_w, k_norm_w, S, H, PW // HEAD_DIM, riders=[_cx_rider([W_FF1], sum_ff1)])
    dproj = jnp.concatenate([du, dq, dk, dv, dga, dgb], axis=1)
    early = [W_FF1, W_FF2]
    halves_early = chip_sums(early, sum_ff1 + sum_ff2, [q_ff1, q_ff2[0]])
    (g_win,), (grads_early, (q_wa, q_wb, q_wo)) = _mm(
        "grad_w_in", [(h, dproj)], M=D, N=IN, K=S, riders=[_sf_rider(early, halves_early), _cx_rider(mid, sum_mid)], **gmm)
    last = [W_IN, W_POOL]
    g_last = [g_win, g_wpool4.reshape(PW, cg)]
    halves_mid = chip_sums(mid, sum_mid, [q_wa, q_wb, q_wo])
    (dh,), (got_last, grads_mid) = _mm("d_h", [(dproj, win_f)], M=S, N=D, K=IN, tb=True, outs=[_tile_out(F32)], epi=first,
                                       riders=[_px_rider(last, g_last), _sf_rider(mid, halves_mid)], **DEEP)
    sum_last = pair_sums(last, g_last, got_last)
    grad_x, dshift1_p, dscale1_p, gn1_p = _norm_mod_bwd("norm1_bwd", dh, x2, dx1, norm1_w, scale1)

    parts = [dshift1_p, dscale1_p, dgate1_p, dshift2_p, dscale2_p, dgate2_p, gn1_p, gn2_p,
             gscale_p.reshape(1, 1, PW), gq_p, gk_p]
    widths = [D] * 8 + [PW, HEAD_DIM, HEAD_DIM]
    used = sum(widths)
    P = -(-(used + 128) // 1024) * 1024
    packed = _pack_partials(parts + [loss_p], widths, P)
    gathered = _dev_allgather("gather_vector_grads", packed.reshape(8, P // 8)).reshape(N_DEV, P)
    sum_last, gathered = lax.optimization_barrier((sum_last, gathered))
    cx_send, cx_recv, sum_last, land_last, token = _cx_start(last, sum_last)
    small = [(b_ada, m_b_ada, v_b_ada), (norm1_w, m_norm1_w, v_norm1_w), (norm2_w, m_norm2_w, v_norm2_w),
             (pool_scale, m_pool_scale, v_pool_scale), (q_norm_w, m_q_norm_w, v_q_norm_w),
             (k_norm_w, m_k_norm_w, v_k_norm_w)]
    offsets = [(0, 6 * D), (6 * D, D), (7 * D, D), (8 * D, PW), (8 * D + PW, HEAD_DIM), (8 * D + PW + HEAD_DIM, HEAD_DIM)]
    su = _small_update(gathered, offsets, small, used)
    (g_b, d_b, nm_b, nv_b, g_n1, d_n1, nm_n1, nv_n1, g_n2, d_n2, nm_n2, nv_n2, g_ps, d_ps, nm_ps, nv_ps,
     g_qn, d_qn, nm_qn, nv_qn, g_kn, d_kn, nm_kn, nv_kn, loss_sum) = su
    dmod_sh = lax.dynamic_slice(gathered, (0, chip * A_COLS), (N_DEV, A_COLS))
    dmod_sh, token = lax.optimization_barrier((dmod_sh, token))
    g_ada, d_ada, nm_ada, nv_ada = _ada_update(sc_all.T, dmod_sh, w_ada[0], m_w_ada[0], v_w_ada[0])

    upd_done = [_adamw("adamw_" + w.name, a, g, m, v, after=token)
                for w, a, g, m, v in zip(ws[2:], w32[2:], list(grads_mid) + list(grads_early), m32[2:], v32[2:])]

    sum_last, q_last = _cx_wait(last, cx_send, cx_recv, sum_last, land_last,
                                after=[nv_ada] + [u[3] for u in upd_done])
    halves_last = chip_sums(last, sum_last, q_last)
    filled = _run_rider("grad_sibling_fill", _sf_rider(last, halves_last))
    upd = [_adamw("adamw_" + w.name, a, g, m, v) for w, a, g, m, v in zip(ws[:2], w32[:2], filled, m32[:2], v32[:2])]
    upd += upd_done

    loss = (0.5 / D) * loss_sum[0, 0]

    def up(a):
        return a[None]

    def pool4(a):
        return a.reshape(1, N_GROUPS, cg // N_CHIPS, cg)

    (gr_win, d_win, nm_win, nv_win), (gr_wp, d_wp, nm_wp, nv_wp), (gr_wa, d_wa, nm_wa, nv_wa), \
        (gr_wb, d_wb, nm_wb, nv_wb), (gr_wo, d_wo, nm_wo, nv_wo), (gr_f1, d_f1, nm_f1, nv_f1), \
        (gr_f2, d_f2, nm_f2, nv_f2) = upd
    return (
        loss, grad_x[None],
        up(g_ada), g_b, g_n1, up(gr_win), g_qn, g_kn, pool4(gr_wp), g_ps, up(gr_wa), up(gr_wb), up(gr_wo), g_n2,
        up(gr_f1), up(gr_f2),
        up(d_ada), d_b, d_n1, up(d_win), d_qn, d_kn, pool4(d_wp), d_ps, up(d_wa), up(d_wb), up(d_wo), d_n2,
        up(d_f1), up(d_f2),
        up(nm_ada), nm_b, nm_n1, up(nm_win), nm_qn, nm_kn, pool4(nm_wp), nm_ps, up(nm_wa), up(nm_wb), up(nm_wo), nm_n2,
        up(nm_f1), up(nm_f2),
        up(nv_ada), nv_b, nv_n1, up(nv_win), nv_qn, nv_kn, pool4(nv_wp), nv_ps, up(nv_wa), up(nv_wb), up(nv_wo), nv_n2,
        up(nv_f1), up(nv_f2),
    )
```

```python
import functools
import math

import jax
import jax.numpy as jnp
from jax import lax
from jax.experimental import pallas as pl
from jax.experimental.pallas import tpu as pltpu

F32 = jnp.float32
BF16 = jnp.bfloat16
MESH = pl.DeviceIdType.MESH
ANY = pl.BlockSpec(memory_space=pl.ANY)

EPS = 1e-6
HEAD_DIM = 128
POOL_WINDOWS = (2, 4, 8, 16)
N_GROUPS = len(POOL_WINDOWS)
N_CHIPS = 4
N_DEV = 8
ADAM_LR, ADAM_B1, ADAM_B2, ADAM_EPS, ADAM_WD, ADAM_STEP = 0.001, 0.9, 0.999, 1e-08, 0.01, 10
VMEM_LIMIT_V7X = 56 * 1024 * 1024
ATT_T = 256
ATT_GROUP = 4
POOL_T = 256


def _pcall(body, **kw):
    return pl.pallas_call(body, **kw)


def _params(sem=None):
    return pltpu.CompilerParams(dimension_semantics=sem, vmem_limit_bytes=VMEM_LIMIT_V7X)


def _tile(n, pref):
    if n <= pref:
        return n
    t = pref
    while n % t:
        t //= 2
    return t


class _Rider:
    def __init__(self, arrays, out_shape, sems, start, finish, aliases=None, steps=()):
        self.arrays, self.out_shape, self.sems = list(arrays), list(out_shape), list(sems)
        self.start, self.finish, self.aliases, self.steps = start, finish, aliases or {}, list(steps)


def _ride(name, body, riders, arrays, *, grid, in_specs, out_specs, out_shape, scratch_shapes, sem):
    n_in, n_out, n_scr = len(arrays), len(out_shape), len(scratch_shapes)
    r_arrays = [a for r in riders for a in r.arrays]
    r_outs = [o for r in riders for o in r.out_shape]
    r_sems = [s for r in riders for s in r.sems]
    n_hooks = max([len(r.steps) for r in riders], default=0)
    total = math.prod(grid)
    aliases, off_i, off_o = {}, n_in, n_out
    for r in riders:
        for a, o in r.aliases.items():
            aliases[off_i + a] = off_o + o
        off_i += len(r.arrays)
        off_o += len(r.out_shape)

    def full(*refs):
        p = 0
        groups = []
        for n in (n_in, len(r_arrays), n_out, len(r_outs), n_scr, len(r_sems)):
            groups.append(refs[p:p + n])
            p += n
        ins, rin, outs, rout, scr, rsem = groups

        def each(what):
            a = o = s = 0
            for r in riders:
                fn = what(r)
                if fn is not None:
                    fn(rin[a:a + len(r.arrays)], rout[o:o + len(r.out_shape)], rsem[s:s + len(r.sems)])
                a, o, s = a + len(r.arrays), o + len(r.out_shape), s + len(r.sems)

        if riders:
            lin = 0
            for d, g in enumerate(grid):
                lin = lin * g + pl.program_id(d)
            pl.when(lin == 0)(lambda: each(lambda r: r.start))
            for t in range(n_hooks):
                pl.when(lin == min(total - 1, ((t + 1) * total) // n_hooks))(
                    lambda t=t: each(lambda r: r.steps[t] if t < len(r.steps) else None))
        body(*ins, *outs, *scr)
        if riders:
            pl.when(lin == total - 1)(lambda: each(lambda r: r.finish))

    res = _pcall(
        full, name=name, grid=grid, in_specs=list(in_specs) + [ANY] * len(r_arrays),
        out_specs=list(out_specs) + [ANY] * len(r_outs), out_shape=list(out_shape) + r_outs,
        scratch_shapes=list(scratch_shapes) + r_sems, input_output_aliases=aliases,
        compiler_params=_params(("arbitrary",) * len(grid) if riders else sem),
    )(*arrays, *r_arrays)
    if not riders:
        return res
    main, rest, per = res[:n_out], res[n_out:], []
    for r in riders:
        per.append(rest[:len(r.out_shape)])
        rest = rest[len(r.out_shape):]
    return main, per


def _run_rider(name, rider):
    def body(*refs):
        n_a, n_o = len(rider.arrays), len(rider.out_shape)
        ins, outs, sems = refs[:n_a], refs[n_a:n_a + n_o], refs[n_a + n_o:]
        for fn in [rider.start] + rider.steps + [rider.finish]:
            fn(ins, outs, sems)

    return _pcall(body, name=name, out_shape=rider.out_shape, in_specs=[ANY] * len(rider.arrays),
                  out_specs=[ANY] * len(rider.out_shape), scratch_shapes=rider.sems,
                  input_output_aliases=rider.aliases)(*rider.arrays)


def _mm(name, pairs, *, M, N, K, ta=False, tb=False, tm=512, tn=1024, tk=1024,
        a_pro=None, b_pro=None, extras=(), outs, epi, riders=(), b_noff=0):
    tm, tn, tk = _tile(M, tm), _tile(N, tn), _tile(K, tk)
    n_i, n_j, n_k = M // tm, N // tn, K // tk
    n_p, n_e = len(pairs), len(extras)
    arrays, in_specs = [], []
    for a, _ in pairs:
        arrays.append(a)
        in_specs.append(pl.BlockSpec((tk, tm), lambda i, j, k: (k, i)) if ta
                        else pl.BlockSpec((tm, tk), lambda i, j, k: (i, k)))
    for _, b in pairs:
        arrays.append(b)
        in_specs.append(pl.BlockSpec((tn, tk), lambda i, j, k: (j + b_noff // tn, k)) if tb
                        else pl.BlockSpec((tk, tn), lambda i, j, k: (k, j + b_noff // tn)))
    for arr, kind, off in extras:
        ob = off // tn
        assert off % tn == 0
        arrays.append(arr)
        if kind == "tile":
            in_specs.append(pl.BlockSpec((tm, tn), lambda i, j, k, ob=ob: (i, j + ob)))
        else:
            in_specs.append(pl.BlockSpec((1, tn), lambda i, j, k, ob=ob: (0, j + ob)))
    out_shape, out_specs = [], []
    for o in outs:
        if o["kind"] == "tile":
            out_shape.append(jax.ShapeDtypeStruct((M, N), o["dtype"]))
            out_specs.append(pl.BlockSpec((tm, tn), lambda i, j, k: (i, j)))
        else:
            out_shape.append(jax.ShapeDtypeStruct((n_i, 1, N), F32))
            out_specs.append(pl.BlockSpec((1, 1, tn), lambda i, j, k: (i, 0, j)))
    dims = (((0 if ta else 1,), (1 if tb else 0,)), ((), ()))

    def body(*refs):
        a_refs, b_refs = refs[:n_p], refs[n_p:2 * n_p]
        e_refs = refs[2 * n_p:2 * n_p + n_e]
        o_refs = refs[2 * n_p + n_e:2 * n_p + n_e + len(outs)]
        acc_refs = refs[2 * n_p + n_e + len(outs):]

        def product(p):
            a, b = a_refs[p][...], b_refs[p][...]
            if a_pro is not None:
                a = a_pro(a)
            if b_pro is not None:
                b = b_pro(b)
            return lax.dot_general(a, b, dims, preferred_element_type=F32)

        def write(accs):
            vals = epi(accs, [e[...] for e in e_refs])
            for o, o_ref, val in zip(outs, o_refs, vals):
                if o["kind"] == "tile":
                    o_ref[...] = val.astype(o_ref.dtype)
                else:
                    o_ref[0] = val

        if n_k == 1:
            write([product(p) for p in range(n_p)])
            return
        k = pl.program_id(2)

        @pl.when(k == 0)
        def _():
            for acc in acc_refs:
                acc[...] = jnp.zeros_like(acc)

        for p in range(n_p):
            acc_refs[p][...] += product(p)

        pl.when(k == n_k - 1)(lambda: write([acc[...] for acc in acc_refs]))

    return _ride(name, body, riders, arrays, grid=(n_i, n_j, n_k), in_specs=in_specs, out_specs=out_specs,
                 out_shape=out_shape, scratch_shapes=[pltpu.VMEM((tm, tn), F32) for _ in pairs] if n_k > 1 else [],
                 sem=("parallel", "parallel", "arbitrary"))


def _tile_out(dtype):
    return {"kind": "tile", "dtype": dtype}


_COLSUM = {"kind": "colsum"}


def _colsum(v):
    return jnp.sum(v, axis=0, keepdims=True)


def _norm_mod(name, x, norm_w, scale, shift):
    S, D = x.shape
    tr = _tile(S, 256)

    def body(x_ref, nw_ref, sc_ref, sh_ref, h_ref):
        xv = x_ref[...]
        r = lax.rsqrt(jnp.mean(xv * xv, axis=-1, keepdims=True) + EPS)
        h_ref[...] = ((xv * r * nw_ref[...]) * (1.0 + sc_ref[...]) + sh_ref[...]).astype(BF16)

    row = pl.BlockSpec((1, D), lambda i: (0, 0))
    til = pl.BlockSpec((tr, D), lambda i: (i, 0))
    return _pcall(body, name=name, grid=(S // tr,), in_specs=[til, row, row, row], out_specs=til,
                  out_shape=jax.ShapeDtypeStruct((S, D), BF16), compiler_params=_params(("parallel",)))(
                      x, norm_w, scale, shift)


def _norm_mod_bwd(name, dh, x, dres, norm_w, scale, gate_o=None):
    S, D = x.shape
    tr = _tile(S, 256)
    n_r = S // tr
    with_gate = gate_o is not None
    dh = list(dh) if isinstance(dh, (list, tuple)) else [dh]
    n_dh = len(dh)

    def body(*refs):
        dh_refs, refs = refs[:n_dh], refs[n_dh:]
        if with_gate:
            x_ref, dres_ref, nw_ref, sc_ref, o_ref, g_ref, dx_ref, p1, p2, p3, do_ref, p4 = refs
        else:
            x_ref, dres_ref, nw_ref, sc_ref, dx_ref, p1, p2, p3 = refs
        dhv = dh_refs[0][...] if n_dh == 1 else jnp.concatenate([r[...] for r in dh_refs], axis=1)
        xv, nw = x_ref[...], nw_ref[...]
        r = lax.rsqrt(jnp.mean(xv * xv, axis=-1, keepdims=True) + EPS)
        xh = xv * r
        p1[0] = _colsum(dhv)
        p2[0] = _colsum(dhv * (xh * nw))
        dn = dhv * (1.0 + sc_ref[...])
        p3[0] = _colsum(dn * xh)
        dxh = dn * nw
        dx = dres_ref[...] + r * (dxh - xh * jnp.mean(dxh * xh, axis=-1, keepdims=True))
        dx_ref[...] = dx
        if with_gate:
            do_ref[...] = (dx * g_ref[...]).astype(BF16)
            p4[0] = _colsum(dx * o_ref[...].astype(F32))

    row = pl.BlockSpec((1, D), lambda i: (0, 0))
    til = pl.BlockSpec((tr, D), lambda i: (i, 0))
    part = pl.BlockSpec((1, 1, D), lambda i: (i, 0, 0))
    part_shape = jax.ShapeDtypeStruct((n_r, 1, D), F32)
    in_specs = [pl.BlockSpec((tr, D // n_dh), lambda i: (i, 0))] * n_dh + [til, til, row, row]
    arrays = dh + [x, dres, norm_w, scale]
    out_specs = [til, part, part, part]
    out_shape = [jax.ShapeDtypeStruct((S, D), F32), part_shape, part_shape, part_shape]
    if with_gate:
        in_specs += [til, row]
        arrays += list(gate_o)
        out_specs += [til, part]
        out_shape += [jax.ShapeDtypeStruct((S, D), BF16), part_shape]
    return _pcall(body, name=name, grid=(n_r,), in_specs=in_specs, out_specs=out_specs, out_shape=out_shape,
                  compiler_params=_params(("parallel",)))(*arrays)


def _pool_w_specs(rows, cg):
    return [pl.BlockSpec((rows, cg), lambda g, j=j: (N_GROUPS * j + g, 0)) for j in range(N_CHIPS)]


def _pool_fwd(proj, wp_full, pool_scale, S, PW):
    cg = PW // N_GROUPS
    rows = cg // N_CHIPS
    T = _tile(S, POOL_T)
    n_t = S // T

    def body(u_ref, w0, w1, w2, w3, ps_ref, pooled_ref, pa_ref):
        g = pl.program_id(0)
        win = jnp.left_shift(2, g)
        w = jnp.concatenate([w0[...], w1[...], w2[...], w3[...]], axis=0)
        t_i = lax.broadcasted_iota(jnp.int32, (T, T), 0)
        j_i = lax.broadcasted_iota(jnp.int32, (T, T), 1)
        b_cur = ((j_i <= t_i) & (j_i > t_i - win)).astype(BF16)
        b_prev = (j_i - T > t_i - win).astype(BF16)
        row = lax.broadcasted_iota(jnp.int32, (T, 1), 0)
        for r in range(n_t):
            cur = u_ref[r * T:(r + 1) * T, :]
            ws = jnp.dot(b_cur, cur, preferred_element_type=F32)
            if r > 0:
                ws += jnp.dot(b_prev, u_ref[(r - 1) * T:r * T, :], preferred_element_type=F32)
            count = jnp.minimum(row + (r * T + 1), win).astype(F32)
            pooled = (ws / count - cur.astype(F32)).astype(BF16)
            pooled_ref[r * T:(r + 1) * T, :] = pooled
            mixed = jnp.dot(pooled, w, preferred_element_type=F32)
            pa_ref[r * T:(r + 1) * T, :] = (mixed * ps_ref[...]).astype(BF16)

    col = pl.BlockSpec((S, cg), lambda g: (0, g))
    return _pcall(
        body, name="pool_fwd", grid=(N_GROUPS,),
        in_specs=[col] + _pool_w_specs(rows, cg) + [pl.BlockSpec((1, cg), lambda g: (0, g))],
        out_specs=[col, col],
        out_shape=[jax.ShapeDtypeStruct((S, PW), BF16), jax.ShapeDtypeStruct((S, PW), BF16)],
        compiler_params=_params(("parallel",)),
    )(proj, wp_full, wp_full, wp_full, wp_full, pool_scale)


def _pool_bwd(dpa, pooled, wp_full, pool_scale, S, PW):
    cg = PW // N_GROUPS
    rows = cg // N_CHIPS
    T = _tile(S, POOL_T)
    n_t = S // T

    def body(dpa_ref, pooled_ref, w0, w1, w2, w3, ps_ref, du_ref, gw_ref, gs_ref, dp_s, dpc_s, dmx_s):
        g = pl.program_id(0)
        win = jnp.left_shift(2, g)
        w = jnp.concatenate([w0[...], w1[...], w2[...], w3[...]], axis=0)
        row = lax.broadcasted_iota(jnp.int32, (T, 1), 0)
        gs = jnp.zeros((1, cg), F32)
        for r in range(n_t):
            sl = slice(r * T, (r + 1) * T)
            mixed = jnp.dot(pooled_ref[sl, :], w, preferred_element_type=F32)
            dpa_t = dpa_ref[sl, :]
            gs += _colsum(dpa_t * mixed)
            dmx = (dpa_t * ps_ref[...]).astype(BF16)
            dmx_s[sl, :] = dmx
            dpo = lax.dot_general(dmx, w, (((1,), (1,)), ((), ())), preferred_element_type=F32)
            dp_s[sl, :] = dpo
            count = jnp.minimum(row + (r * T + 1), win).astype(F32)
            dpc_s[sl, :] = (dpo / count).astype(BF16)
        gs_ref[...] = gs
        gw = lax.dot_general(pooled_ref[...], dmx_s[...], (((0,), (0,)), ((), ())), preferred_element_type=F32)
        for j in range(N_CHIPS):
            gw_ref[j, 0] = gw[j * rows:(j + 1) * rows, :].astype(BF16)
        j_i = lax.broadcasted_iota(jnp.int32, (T, T), 0)
        t_i = lax.broadcasted_iota(jnp.int32, (T, T), 1)
        b_cur = ((t_i >= j_i) & (t_i < j_i + win)).astype(BF16)
        b_next = (t_i + T < j_i + win).astype(BF16)
        for r in range(n_t):
            sl = slice(r * T, (r + 1) * T)
            acc = jnp.dot(b_cur, dpc_s[sl, :], preferred_element_type=F32)
            if r + 1 < n_t:
                acc += jnp.dot(b_next, dpc_s[(r + 1) * T:(r + 2) * T, :], preferred_element_type=F32)
            du_ref[sl, :] = (acc - dp_s[sl, :]).astype(BF16)

    col = pl.BlockSpec((S, cg), lambda g: (0, g))
    return _pcall(
        body, name="pool_bwd", grid=(N_GROUPS,),
        in_specs=[col, col] + _pool_w_specs(rows, cg) + [pl.BlockSpec((1, cg), lambda g: (0, g))],
        out_specs=[col, pl.BlockSpec((N_CHIPS, 1, rows, cg), lambda g: (0, g, 0, 0)),
                   pl.BlockSpec((1, cg), lambda g: (0, g))],
        out_shape=[jax.ShapeDtypeStruct((S, PW), BF16),
                   jax.ShapeDtypeStruct((N_CHIPS, N_GROUPS, rows, cg), BF16),
                   jax.ShapeDtypeStruct((1, PW), F32)],
        scratch_shapes=[pltpu.VMEM((S, cg), F32), pltpu.VMEM((S, cg), BF16), pltpu.VMEM((S, cg), BF16)],
        compiler_params=_params(("parallel",)),
    )(dpa, pooled, wp_full, wp_full, wp_full, wp_full, pool_scale)


_NT = (((1,), (1,)), ((), ()))
_TN = (((0,), (0,)), ((), ()))


def _split_dot(v, tri):
    hi = v.astype(BF16)
    lo = (v - hi.astype(F32)).astype(BF16)
    return jnp.dot(hi, tri, preferred_element_type=F32) + jnp.dot(lo, tri, preferred_element_type=F32)


LOG2E = 1.4426950408889634
QK_SCALE = 1.0 / math.sqrt(HEAD_DIM)


def _sb_scores(q2_i, k_j, tri_l, masked):
    tq, tk = q2_i.shape[0], k_j.shape[0]
    s = lax.dot_general(q2_i, k_j, _NT, preferred_element_type=F32)
    lp = jnp.log(1.0 + jnp.exp2(-jnp.abs(s))) * LOG2E
    lb = jnp.minimum(s, 0.0) - lp
    l = lb - s
    mask = None
    if masked:
        mask = lax.broadcasted_iota(jnp.int32, (tq, tk), 0) > lax.broadcasted_iota(jnp.int32, (tq, tk), 1)
        l = jnp.where(mask, l, 0.0)
    return l, lb, lb + _split_dot(l, tri_l), mask


def _sb_weights(t, carry_l, mask):
    a = jnp.exp2(t + carry_l)
    return a if mask is None else jnp.where(mask, a, 0.0)


def _rowsum(v):
    return jnp.sum(v, axis=1, keepdims=True)


def _qk_norm(x_ref, w_ref):
    xv = x_ref[...].astype(F32)
    r = lax.rsqrt(jnp.mean(xv * xv, axis=-1, keepdims=True) + EPS)
    return xv * r, r


def _attn_fwd(proj, q_norm_w, k_norm_w, S, H, q_off, riders=()):
    t = _tile(S, ATT_T)
    n_q = S // t

    def body(q_ref, k_ref, v_ref, qw_ref, kw_ref, att_ref, attf_ref, qn_s, kn_s):
        qh, _ = _qk_norm(q_ref, qw_ref)
        qn_s[...] = (qh * qw_ref[...] * (QK_SCALE * LOG2E)).astype(BF16)
        kh, _ = _qk_norm(k_ref, kw_ref)
        kn_s[...] = (kh * kw_ref[...]).astype(BF16)
        tri_l = (lax.broadcasted_iota(jnp.int32, (t, t), 0) > lax.broadcasted_iota(jnp.int32, (t, t), 1)).astype(BF16)

        def rows(j):
            return pl.ds(pl.multiple_of(j * t, t), t)

        def q_step(i, _):
            q_i = qn_s[rows(i), :]

            def av(a, j):
                return jnp.dot(a.astype(BF16), v_ref[rows(j), :], preferred_element_type=F32)

            l, _, tt, mask = _sb_scores(q_i, kn_s[rows(i), :], tri_l, True)
            acc = av(_sb_weights(tt, 0.0, mask), i)
            carry = _rowsum(l)

            def single(_, c):
                carry, acc = c
                l, _, tt, _ = _sb_scores(q_i, kn_s[rows(i - 1), :], tri_l, False)
                return carry + _rowsum(l), acc + av(_sb_weights(tt, carry, None), i - 1)

            carry, acc = lax.fori_loop(0, i % 2, single, (carry, acc))
            top = i - 1 - i % 2

            def pair(p, c):
                carry, acc = c
                j0 = top - 2 * p
                l0, _, t0, _ = _sb_scores(q_i, kn_s[rows(j0), :], tri_l, False)
                l1, _, t1, _ = _sb_scores(q_i, kn_s[rows(j0 - 1), :], tri_l, False)
                mid = carry + _rowsum(l0)
                acc = acc + av(_sb_weights(t0, carry, None), j0) + av(_sb_weights(t1, mid, None), j0 - 1)
                return mid + _rowsum(l1), acc

            _, acc = lax.fori_loop(0, i // 2, pair, (carry, acc))
            att_ref[rows(i), :] = acc.astype(BF16)
            attf_ref[rows(i), :] = acc
            return 0

        lax.fori_loop(0, n_q, q_step, 0)

    def col(off):
        return pl.BlockSpec((S, HEAD_DIM), lambda h, off=off: (0, off + h))

    wspec = pl.BlockSpec((1, HEAD_DIM), lambda h: (0, 0))
    return _ride(
        "attn_fwd", body, riders, [proj, proj, proj, q_norm_w, k_norm_w], grid=(H,),
        in_specs=[col(q_off), col(q_off + H), col(q_off + 2 * H), wspec, wspec],
        out_specs=[col(0), col(0)],
        out_shape=[jax.ShapeDtypeStruct((S, H * HEAD_DIM), BF16), jax.ShapeDtypeStruct((S, H * HEAD_DIM), F32)],
        scratch_shapes=[pltpu.VMEM((S, HEAD_DIM), BF16), pltpu.VMEM((S, HEAD_DIM), BF16)],
        sem=("parallel",))


def _attn_bwd(proj, datt, attf, q_norm_w, k_norm_w, S, H, q_off, riders=()):
    t = _tile(S, ATT_T)
    n_q = S // t

    def body(q_ref, k_ref, v_ref, do_ref, o_ref, qw_ref, kw_ref, dq_ref, dk_ref, dv_ref, gq_ref, gk_ref,
             qn_s, kn_s, qz_s, kz_s, dk_s, dv_s, gq_s):
        qw, kw = qw_ref[...], kw_ref[...]
        qh, _ = _qk_norm(q_ref, qw_ref)
        qn_s[...] = (qh * qw * (QK_SCALE * LOG2E)).astype(BF16)
        qz_s[...] = (qh * qw * QK_SCALE).astype(BF16)
        kh, _ = _qk_norm(k_ref, kw_ref)
        kn_s[...] = (kh * kw).astype(BF16)
        kz_s[...] = (kh * kw * QK_SCALE).astype(BF16)
        dk_s[...] = jnp.zeros_like(dk_s)
        dv_s[...] = jnp.zeros_like(dv_s)
        gq_s[...] = jnp.zeros_like(gq_s)
        r_i = lax.broadcasted_iota(jnp.int32, (t, t), 0)
        c_i = lax.broadcasted_iota(jnp.int32, (t, t), 1)
        tri_l = (r_i > c_i).astype(BF16)
        tri_e = (r_i >= c_i).astype(BF16)

        def rows(j):
            return pl.ds(pl.multiple_of(j * t, t), t)

        def q_step(i, _):
            q_i = qn_s[rows(i), :]
            do_i = do_ref[rows(i), :]
            d_i = _rowsum(do_i.astype(F32) * o_ref[rows(i), :])

            def scores(j, masked):
                l, lb, tt, mask = _sb_scores(q_i, kn_s[rows(j), :], tri_l, masked)
                da = lax.dot_general(do_i, v_ref[rows(j), :], _NT, preferred_element_type=F32)
                return l, lb, tt, mask, da

            def grads(j, sc, carry_l, carry_e, dq_acc):
                l, lb, tt, mask, da = sc
                a_bf = _sb_weights(tt, carry_l, mask).astype(BF16)
                e = da * a_bf.astype(F32)
                p = (d_i - carry_e) - _split_dot(e, tri_e)
                dz = e - jnp.exp2(lb) * (e + p)
                if mask is not None:
                    dz = jnp.where(mask, dz, 0.0)
                dz = dz.astype(BF16)
                dk_s[rows(j), :] += lax.dot_general(dz, qz_s[rows(i), :], _TN, preferred_element_type=F32)
                dv_s[rows(j), :] += lax.dot_general(a_bf, do_i, _TN, preferred_element_type=F32)
                return (carry_l + _rowsum(l), carry_e + _rowsum(e),
                        dq_acc + jnp.dot(dz, kz_s[rows(j), :], preferred_element_type=F32))

            zero = jnp.zeros((t, 1), F32)
            first = (zero, zero, jnp.zeros((t, HEAD_DIM), F32))

            def group(js, diagonal_first, c):
                scs = [scores(j, diagonal_first and n == 0) for n, j in enumerate(js)]
                for j, sc in zip(js, scs):
                    c = grads(j, sc, *c)
                return c

            n_first = i % ATT_GROUP
            c = lax.switch(n_first, [functools.partial(group, [i - u for u in range(n + 1)], True, first)
                                     for n in range(ATT_GROUP)])
            top = i - 1 - n_first

            def whole(p, c):
                j0 = top - ATT_GROUP * p
                return group([j0 - u for u in range(ATT_GROUP)], False, c)

            _, _, dqn = lax.fori_loop(0, (i - n_first) // ATT_GROUP, whole, c)
            qv = q_ref[rows(i), :].astype(F32)
            r = lax.rsqrt(jnp.mean(qv * qv, axis=-1, keepdims=True) + EPS)
            xh = qv * r
            gq_s[...] += _colsum(dqn * xh)
            dxh = dqn * qw
            dq_ref[rows(i), :] = (r * (dxh - xh * jnp.mean(dxh * xh, axis=-1, keepdims=True))).astype(BF16)
            return 0

        lax.fori_loop(0, n_q, q_step, 0)
        gq_ref[0] = gq_s[...]
        kh, rk = _qk_norm(k_ref, kw_ref)
        dkn = dk_s[...]
        gk_ref[0] = _colsum(dkn * kh)
        dxh = dkn * kw
        dk_ref[...] = (rk * (dxh - kh * jnp.mean(dxh * kh, axis=-1, keepdims=True))).astype(BF16)
        dv_ref[...] = dv_s[...].astype(BF16)

    def col(off):
        return pl.BlockSpec((S, HEAD_DIM), lambda h, off=off: (0, off + h))

    wspec = pl.BlockSpec((1, HEAD_DIM), lambda h: (0, 0))
    gspec = pl.BlockSpec((1, 1, HEAD_DIM), lambda h: (h, 0, 0))
    act = jax.ShapeDtypeStruct((S, H * HEAD_DIM), BF16)
    gsh = jax.ShapeDtypeStruct((H, 1, HEAD_DIM), F32)
    return _ride(
        "attn_bwd", body, riders, [proj, proj, proj, datt, attf, q_norm_w, k_norm_w], grid=(H,),
        in_specs=[col(q_off), col(q_off + H), col(q_off + 2 * H), col(0), col(0), wspec, wspec],
        out_specs=[col(0), col(0), col(0), gspec, gspec],
        out_shape=[act, act, act, gsh, gsh],
        scratch_shapes=[pltpu.VMEM((S, HEAD_DIM), BF16)] * 4 + [pltpu.VMEM((S, HEAD_DIM), F32)] * 2
        + [pltpu.VMEM((1, HEAD_DIM), F32)],
        sem=("parallel",))


def _place():
    x, y, c = lax.axis_index("x"), lax.axis_index("y"), lax.axis_index("c")
    chips = [(1 - x, y), (x, 1 - y), (1 - x, 1 - y)]
    return x, y, c, chips


def _dev_allgather(name, v):
    m_per, n = v.shape

    def body(x_ref, out_ref, send_sems, recv_sems, local_sem):
        x, y, c, chips = _place()
        me, sibling = (x, y, c), (x, y, 1 - c)

        def rows(px, py, pc):
            return out_ref.at[pl.ds((4 * px + 2 * py + pc) * m_per, m_per), :]

        def copy(k, block, to, src=None):
            return pltpu.make_async_remote_copy(
                src_ref=rows(*block) if src is None else src, dst_ref=rows(*block),
                send_sem=send_sems.at[k], recv_sem=recv_sems.at[k], device_id=to, device_id_type=MESH)

        mine = pltpu.make_async_copy(x_ref, rows(*me), local_sem)
        mine.start()
        first = [copy(0, me, sibling, src=x_ref)]
        first += [copy(1 + j, me, (*chip, c), src=x_ref) for j, chip in enumerate(chips)]
        for cp in first:
            cp.start()
        passed = [copy(4 + j, (*chip, c), sibling) for j, chip in enumerate(chips)]
        for j, chip in enumerate(chips):
            copy(1 + j, (*chip, c), me).wait_recv()
            passed[j].start()
        copy(0, sibling, me).wait_recv()
        for j, chip in enumerate(chips):
            copy(4 + j, (*chip, 1 - c), me).wait_recv()
        for cp in first + passed:
            cp.wait_send()
        mine.wait()

    return _pcall(
        body, name=name, out_shape=jax.ShapeDtypeStruct((N_DEV * m_per, n), v.dtype),
        in_specs=[pl.BlockSpec(memory_space=pltpu.VMEM)], out_specs=pl.BlockSpec(memory_space=pltpu.VMEM),
        scratch_shapes=[pltpu.SemaphoreType.DMA((7,)), pltpu.SemaphoreType.DMA((7,)), pltpu.SemaphoreType.DMA],
        compiler_params=pltpu.CompilerParams(vmem_limit_bytes=VMEM_LIMIT_V7X),
    )(v)


class _W:
    def __init__(self, name, kind, R, C):
        self.name, self.kind, self.R, self.C = name, kind, R, C

    @property
    def shard_shape(self):
        return (self.R, self.C // N_CHIPS) if self.kind == "col" else (self.R // N_CHIPS, self.C)

    @property
    def half_rows(self):
        return self.shard_shape[0] // 2

    def shard_half(self, ref, half):
        return ref.at[pl.ds(half * self.half_rows, self.half_rows), :]

    def region(self, full_ref, chip, half):
        hr = self.half_rows
        if self.kind == "col":
            cw = self.C // N_CHIPS
            return full_ref.at[pl.ds(half * hr, hr), pl.ds(chip * cw, cw)]
        return full_ref.at[pl.ds(chip * (2 * hr) + half * hr, hr), :]

    def region_both(self, full_ref, chip):
        hr = self.half_rows
        if self.kind == "col":
            cw = self.C // N_CHIPS
            return full_ref.at[:, pl.ds(chip * cw, cw)]
        return full_ref.at[pl.ds(chip * (2 * hr), 2 * hr), :]


def _ag_rider(ws, fulls, n_ch=4, chunks=None):
    n_w = len(ws)
    lo, hi = chunks or (0, n_ch)
    per = 6

    def parts(full, sems):
        send_sems, recv_sems = sems
        x, y, c, _ = _place()
        xn, yn, dg = (1 - x, y), (x, 1 - y), (1 - x, 1 - y)
        via = (x + (1 - c) * (1 - 2 * x), y + c * (1 - 2 * y))
        to = (x + c * (1 - 2 * x), y + (1 - c) * (1 - 2 * y))

        def reg(i, chip, half, t):
            nr = ws[i].half_rows // n_ch
            return ws[i].region(full[i], 2 * chip[0] + chip[1], half).at[pl.ds(t * nr, nr), :]

        def copy(r, i, t, k, dev):
            s = (i * (hi - lo) + t - lo) * per + k
            return pltpu.make_async_remote_copy(src_ref=r, dst_ref=r, send_sem=send_sems.at[s],
                                                recv_sem=recv_sems.at[s], device_id=dev, device_id_type=MESH)

        def direct(i, t, k):
            return copy(reg(i, (x, y), c, t), i, t, k, (*(via, to)[k], c))

        def direct_in(i, t, k):
            return copy(reg(i, (via, to)[k], c, t), i, t, k, (*(via, to)[k], c))

        def relay(i, t):
            return copy(reg(i, via, c, t), i, t, 2, (*to, c))

        def relay_in(i, t):
            return copy(reg(i, dg, c, t), i, t, 2, (*to, c))

        def hand(i, t, k, half):
            return copy(reg(i, (xn, yn, dg)[k], half, t), i, t, 3 + k, (x, y, 1 - c))

        return c, direct, direct_in, relay, relay_in, hand

    def start(_, full, sems):
        _, direct, _, _, _, _ = parts(full, sems)
        for t in range(lo, hi):
            for i in range(n_w):
                direct(i, t, 0).start()
                direct(i, t, 1).start()

    def arrived(t):
        def step(_, full, sems):
            c, _, direct_in, relay, relay_in, hand = parts(full, sems)
            for i in range(n_w):
                direct_in(i, t, 0).wait_recv()
                direct_in(i, t, 1).wait_recv()
                relay(i, t).start()
                hand(i, t, 0, c).start()
                hand(i, t, 1, c).start()
        return step

    def finish(_, full, sems):
        c, direct, _, relay, relay_in, hand = parts(full, sems)
        for t in range(lo, hi):
            for i in range(n_w):
                relay_in(i, t).wait_recv()
                hand(i, t, 2, c).start()
        for i in range(n_w):
            for t in range(lo, hi):
                for k in range(3):
                    hand(i, t, k, 1 - c).wait_recv()
        for i in range(n_w):
            for t in range(lo, hi):
                direct(i, t, 0).wait_send()
                direct(i, t, 1).wait_send()
                relay(i, t).wait_send()
                for k in range(3):
                    hand(i, t, k, c).wait_send()

    n_sem = per * (hi - lo) * n_w
    return _Rider(fulls, [jax.ShapeDtypeStruct((w.R, w.C), BF16) for w in ws],
                  [pltpu.SemaphoreType.DMA((n_sem,)), pltpu.SemaphoreType.DMA((n_sem,))], start, finish,
                  steps=[arrived(t) for t in range(lo, hi)], aliases={i: i for i in range(n_w)})


def _cast_into_full(w, a32, chip_arr):
    sr, sc = w.shard_shape
    tr, tc = _tile(sr, 512), _tile(sc, 2048)
    n_r, n_c = sr // tr, sc // tc
    if w.kind == "col":
        out_spec = pl.BlockSpec((tr, tc), lambda i, j, chip: (i, chip[0] * n_c + j))
    else:
        out_spec = pl.BlockSpec((tr, tc), lambda i, j, chip: (chip[0] * n_r + i, j))

    def body(chip_ref, a_ref, o_ref):
        o_ref[...] = a_ref[...].astype(BF16)

    return _pcall(
        body, name="cast_" + w.name, out_shape=jax.ShapeDtypeStruct((w.R, w.C), BF16),
        grid_spec=pltpu.PrefetchScalarGridSpec(
            num_scalar_prefetch=1, grid=(n_r, n_c),
            in_specs=[pl.BlockSpec((tr, tc), lambda i, j, chip: (i, j))], out_specs=out_spec),
        compiler_params=_params(("parallel", "parallel")),
    )(chip_arr, a32)


def _half_view(w, g):
    return g if w.kind == "col" else g.reshape(N_CHIPS, w.R // N_CHIPS, w.C)


def _px_rider(ws, grads):
    n_w = len(ws)

    def copies(g, got, sems):
        send_sems, recv_sems = sems
        x, y, c, _ = _place()

        def half_all(w, ref, half):
            hr = w.half_rows
            if w.kind == "col":
                return ref.at[pl.ds(half * hr, hr), :]
            return ref.at[:, pl.ds(half * hr, hr), :]

        return [pltpu.make_async_remote_copy(
            src_ref=half_all(w, g[i], 1 - c), dst_ref=got[i], send_sem=send_sems.at[i], recv_sem=recv_sems.at[i],
            device_id=(x, y, 1 - c), device_id_type=MESH) for i, w in enumerate(ws)]

    def start(g, got, sems):
        for cp in copies(g, got, sems):
            cp.start()

    def finish(g, got, sems):
        for cp in copies(g, got, sems):
            cp.wait_recv()
            cp.wait_send()

    def got_shape(w):
        hr = w.half_rows
        return (hr, w.C) if w.kind == "col" else (N_CHIPS, hr, w.C)

    return _Rider([_half_view(w, g) for w, g in zip(ws, grads)],
                  [jax.ShapeDtypeStruct(got_shape(w), BF16) for w in ws],
                  [pltpu.SemaphoreType.DMA((n_w,)), pltpu.SemaphoreType.DMA((n_w,))], start, finish)


def _pair_sum(w, g, got, c_arr):
    hr = w.half_rows
    if w.kind == "col":
        tr, tc = _tile(hr, 512), _tile(w.C, 2048)
        n_r = hr // tr
        grid = (n_r, w.C // tc)
        g_spec = pl.BlockSpec((tr, tc), lambda i, j, c: (c[0] * n_r + i, j))
        o_spec = pl.BlockSpec((tr, tc), lambda i, j, c: (i, j))
    else:
        tr = _tile(hr, 512)
        n_r = hr // tr
        grid = (N_CHIPS, n_r)
        g_spec = pl.BlockSpec((1, tr, w.C), lambda s, i, c: (s, c[0] * n_r + i, 0))
        o_spec = pl.BlockSpec((1, tr, w.C), lambda s, i, c: (s, i, 0))

    def body(c_ref, g_ref, got_ref, out_ref):
        out_ref[...] = (g_ref[...].astype(F32) + got_ref[...].astype(F32)).astype(BF16)

    return _pcall(
        body, name="grad_pair_sum_" + w.name, out_shape=jax.ShapeDtypeStruct(got.shape, BF16),
        grid_spec=pltpu.PrefetchScalarGridSpec(num_scalar_prefetch=1, grid=grid, in_specs=[g_spec, o_spec],
                                               out_specs=o_spec),
        compiler_params=_params(("parallel", "parallel")),
    )(c_arr, _half_view(w, g), got)


def _cx_rider(ws, sums, part=(0, 1), q_in=None):
    n_w = len(ws)

    def parts(p, q, sems):
        send_sems, recv_sems = sems
        x, y, c, chips = _place()
        my_chip = 2 * x + y

        def rows(w, ref):
            nr = w.half_rows // part[1]
            return ref.at[pl.ds(part[0] * nr, nr), :]

        def piece(w, ref, chip):
            if w.kind == "col":
                cw = w.C // N_CHIPS
                return rows(w, ref.at[:, pl.ds(chip * cw, cw)])
            return rows(w, ref.at[chip])

        def copy(i, k, recv=False):
            chip = chips[k]
            to_chip = 2 * chip[0] + chip[1]
            return pltpu.make_async_remote_copy(
                src_ref=piece(ws[i], p[i], to_chip), dst_ref=rows(ws[i], q[i].at[to_chip if recv else my_chip]),
                send_sem=send_sems.at[3 * i + k], recv_sem=recv_sems.at[3 * i + k],
                device_id=(*chip, c), device_id_type=MESH)

        return copy

    both = [(i, k) for i in range(n_w) for k in range(N_CHIPS - 1)]

    def start(p, q, sems):
        copy = parts(p, q, sems)
        for i, k in both:
            copy(i, k).start()

    def finish(p, q, sems):
        copy = parts(p, q, sems)
        for i, k in both:
            copy(i, k, recv=True).wait_recv()
        for i, k in both:
            copy(i, k).wait_send()

    return _Rider(list(sums) + list(q_in or []),
                  [jax.ShapeDtypeStruct((N_CHIPS, w.half_rows, w.shard_shape[1]), BF16) for w in ws],
                  [pltpu.SemaphoreType.DMA((3 * n_w,)), pltpu.SemaphoreType.DMA((3 * n_w,))], start, finish,
                  aliases={n_w + i: i for i in range(n_w)} if q_in else None)


def _chip_sum(w, p, q, cc_arr):
    hr, cols = w.half_rows, w.shard_shape[1]
    tr, tc = _tile(hr, 512), _tile(cols, 2048)
    n_r, n_c = hr // tr, cols // tc

    def body(cc_ref, own, q1, q2, q3, out_ref):
        own_v = own[...] if w.kind == "col" else own[0]
        out_ref[...] = ((own_v.astype(F32) + q1[0].astype(F32)) + q2[0].astype(F32)) + q3[0].astype(F32)

    if w.kind == "col":
        own_spec = pl.BlockSpec((tr, tc), lambda i, j, cc: (i, cc[1] * n_c + j))
    else:
        own_spec = pl.BlockSpec((1, tr, tc), lambda i, j, cc: (cc[1], i, j))
    q_specs = [pl.BlockSpec((1, tr, tc), lambda i, j, cc, s=s: ((cc[1] + s) % N_CHIPS, i, j)) for s in (1, 2, 3)]
    return _pcall(
        body, name="grad_chip_sum_" + w.name, out_shape=jax.ShapeDtypeStruct(w.shard_shape, F32),
        grid_spec=pltpu.PrefetchScalarGridSpec(
            num_scalar_prefetch=1, grid=(n_r, n_c), in_specs=[own_spec] + q_specs,
            out_specs=pl.BlockSpec((tr, tc), lambda i, j, cc: (cc[0] * n_r + i, j))),
        compiler_params=_params(("parallel", "parallel")),
    )(cc_arr, p, q, q, q)


_SEM = pl.BlockSpec(memory_space=pltpu.SEMAPHORE)
_HBM = pl.BlockSpec(memory_space=pltpu.HBM)


def _cx_split_copies(ws, p, land, send_sems, recv_sems):
    x, y, c, chips = _place()
    my_chip = 2 * x + y
    pairs = []
    for i, w in enumerate(ws):
        for k, chip in enumerate(chips):
            to_chip = 2 * chip[0] + chip[1]
            src = p[i].at[:, pl.ds(to_chip * (w.C // N_CHIPS), w.C // N_CHIPS)] if w.kind == "col" else p[i].at[to_chip]
            kw = dict(send_sem=send_sems.at[3 * i + k], recv_sem=recv_sems.at[3 * i + k], device_id=(*chip, c),
                      device_id_type=MESH)
            pairs.append((pltpu.make_async_remote_copy(src_ref=src, dst_ref=land[i].at[my_chip], **kw),
                          pltpu.make_async_remote_copy(src_ref=src, dst_ref=land[i].at[to_chip], **kw)))
    return pairs


def _cx_start(ws, sums):
    n_w = len(ws)
    lands = [lax.empty((N_CHIPS, w.half_rows, w.shard_shape[1]), BF16) for w in ws]

    def body(*refs):
        p, land = refs[:n_w], refs[n_w:2 * n_w]
        for out, _ in _cx_split_copies(ws, p, land, refs[2 * n_w], refs[2 * n_w + 1]):
            out.start()
        refs[-1][...] = jnp.zeros_like(refs[-1])

    arrays = [pltpu.with_memory_space_constraint(a, pltpu.HBM) for a in list(sums) + lands]
    res = _pcall(
        body, name="grad_last_exchange_start",
        out_shape=(pltpu.SemaphoreType.DMA((3 * n_w,)), pltpu.SemaphoreType.DMA((3 * n_w,)),
                   *[pltpu.HBM(a.shape, a.dtype) for a in arrays], jax.ShapeDtypeStruct((8, 128), F32)),
        in_specs=[_HBM] * (2 * n_w),
        out_specs=(_SEM, _SEM, *[_HBM] * (2 * n_w), pl.BlockSpec(memory_space=pltpu.VMEM)),
        input_output_aliases={i: 2 + i for i in range(2 * n_w)},
        compiler_params=pltpu.CompilerParams(has_side_effects=pltpu.SideEffectType.DATAFLOW_SIDE_EFFECTING),
    )(*arrays)
    return res[0], res[1], list(res[2:2 + n_w]), list(res[2 + n_w:2 + 2 * n_w]), res[-1]


def _cx_wait(ws, send_sems, recv_sems, sums, lands, after):
    n_w = len(ws)

    def body(*refs):
        p, land = refs[:n_w], refs[n_w:2 * n_w]
        for _, cp in _cx_split_copies(ws, p, land, refs[2 * n_w], refs[2 * n_w + 1]):
            cp.wait_send()
            cp.wait_recv()

    res = _pcall(
        body, name="grad_last_exchange_wait",
        out_shape=[pltpu.HBM(a.shape, a.dtype) for a in list(sums) + list(lands)],
        in_specs=[_HBM] * (2 * n_w) + [_SEM, _SEM] + [ANY] * len(after), out_specs=[_HBM] * (2 * n_w),
        input_output_aliases={i: i for i in range(2 * n_w)},
        compiler_params=pltpu.CompilerParams(has_side_effects=pltpu.SideEffectType.DATAFLOW_SIDE_EFFECTING),
    )(*sums, *lands, send_sems, recv_sems, *after)
    return list(res[:n_w]), list(res[n_w:])


def _sf_rider(ws, grads):
    n_w = len(ws)

    def copy(g, sems, i, half):
        send_sems, recv_sems = sems
        x, y, c, _ = _place()
        h = c if half == "mine" else 1 - c
        reg = ws[i].shard_half(g[i], h)
        return pltpu.make_async_remote_copy(src_ref=reg, dst_ref=reg, send_sem=send_sems.at[i], recv_sem=recv_sems.at[i],
                                            device_id=(x, y, 1 - c), device_id_type=MESH)

    def start(_, g, sems):
        for i in range(n_w):
            copy(g, sems, i, "mine").start()

    def finish(_, g, sems):
        for i in range(n_w):
            copy(g, sems, i, "other").wait_recv()
            copy(g, sems, i, "mine").wait_send()

    return _Rider(grads, [jax.ShapeDtypeStruct(w.shard_shape, F32) for w in ws],
                  [pltpu.SemaphoreType.DMA((n_w,)), pltpu.SemaphoreType.DMA((n_w,))], start, finish,
                  aliases={i: i for i in range(n_w)})


def _adamw_math(w, g, m, v):
    m = ADAM_B1 * m + (1.0 - ADAM_B1) * g
    v = ADAM_B2 * v + (1.0 - ADAM_B2) * (g * g)
    m_hat = m / (1.0 - ADAM_B1 ** ADAM_STEP)
    v_hat = v / (1.0 - ADAM_B2 ** ADAM_STEP)
    delta = -ADAM_LR * (m_hat / (jnp.sqrt(v_hat) + ADAM_EPS) + ADAM_WD * w)
    return delta, m, v


def _adamw(name, w, g, m, v, after=None):
    R, C = w.shape
    tr, tc = _tile(R, 256), _tile(C, 2048)
    behind = [] if after is None else [after]

    def body(w_ref, g_ref, m_ref, v_ref, *rest):
        g_out, d_out, m_out, v_out = rest[len(behind):]
        g = g_ref[...]
        g_out[...] = g
        d_out[...], m_out[...], v_out[...] = _adamw_math(w_ref[...], g, m_ref[...], v_ref[...])

    spec = pl.BlockSpec((tr, tc), lambda i, j: (i, j))
    sh = jax.ShapeDtypeStruct((R, C), F32)
    return _pcall(body, name=name, grid=(R // tr, C // tc), in_specs=[spec] * 4 + [ANY] * len(behind),
                  out_specs=[spec] * 4, out_shape=[sh] * 4, compiler_params=_params(("parallel", "parallel")))(
                      w, g, m, v, *behind)


def _ada_update(sct, dmod_sh, w, m, v, riders=()):
    R, C = w.shape
    tr, tc = _tile(R, 256), _tile(C, 1024)

    def body(s_ref, d_ref, w_ref, m_ref, v_ref, g_out, d_out, m_out, v_out):
        s, d = s_ref[...], d_ref[...]
        g = s[:, 0:1] * d[0:1, :]
        for b in range(1, N_DEV):
            g += s[:, b:b + 1] * d[b:b + 1, :]
        g_out[...] = g
        d_out[...], m_out[...], v_out[...] = _adamw_math(w_ref[...], g, m_ref[...], v_ref[...])

    spec = pl.BlockSpec((tr, tc), lambda i, j: (i, j))
    sh = jax.ShapeDtypeStruct((R, C), F32)
    return _ride(
        "ada_update", body, riders, [sct, dmod_sh, w, m, v], grid=(R // tr, C // tc),
        in_specs=[pl.BlockSpec((tr, N_DEV), lambda i, j: (i, 0)), pl.BlockSpec((N_DEV, tc), lambda i, j: (0, j)),
                  spec, spec, spec],
        out_specs=[spec] * 4, out_shape=[sh] * 4, scratch_shapes=[], sem=("parallel", "parallel"))


def _silu_rows(c_row):
    D = c_row.shape[1]

    def body(c_ref, o_ref):
        cv = c_ref[...]
        o_ref[...] = cv * jax.nn.sigmoid(cv)

    return _pcall(body, name="silu_c", out_shape=jax.ShapeDtypeStruct((1, D), F32))(c_row)


def _pack_partials(parts, widths, total):
    n = len(widths)

    def body(*refs):
        loss_p, out_ref = refs[n], refs[n + 1]
        off = 0
        for ref, wd in zip(refs[:n], widths):
            out_ref[:, off:off + wd] = jnp.sum(ref[...], axis=0)
            off += wd
        loss = jnp.sum(jnp.sum(loss_p[...], axis=0), axis=1, keepdims=True)
        out_ref[:, off:off + 128] = jnp.broadcast_to(loss, (1, 128))
        if off + 128 < total:
            out_ref[:, off + 128:total] = jnp.zeros((1, total - off - 128), F32)

    return _pcall(body, name="pack_partials", out_shape=jax.ShapeDtypeStruct((1, total), F32))(*parts)


def _small_update(gathered, offsets, params, loss_off):
    n_p = len(params)

    def over_devices(g_ref, off, wd):
        blk = g_ref[:, off:off + wd]
        g = blk[0:1, :]
        for b in range(1, N_DEV):
            g = g + blk[b:b + 1, :]
        return g

    def body(*refs):
        g_ref = refs[0]
        prm = refs[1:1 + 3 * n_p]
        outs = refs[1 + 3 * n_p:]
        outs[4 * n_p][...] = over_devices(g_ref, loss_off, 128)
        for i, (off, wd) in enumerate(offsets):
            g = over_devices(g_ref, off, wd)
            w, m, v = prm[3 * i][...], prm[3 * i + 1][...], prm[3 * i + 2][...]
            outs[4 * i][...] = g
            outs[4 * i + 1][...], outs[4 * i + 2][...], outs[4 * i + 3][...] = _adamw_math(w, g, m, v)

    flat = [a for t in params for a in t]
    out_shape = [jax.ShapeDtypeStruct(t[0].shape, F32) for t in params for _ in range(4)]
    out_shape.append(jax.ShapeDtypeStruct((1, 128), F32))
    return _pcall(body, name="small_update", out_shape=out_shape)(gathered, *flat)


def kernel(x, c, w_ada, b_ada, norm1_w, w_in, q_norm_w, k_norm_w, w_pool, pool_scale, w_a_up, w_b_up, w_o, norm2_w, w_ff1, w_ff2, loss_target, m_w_ada, m_b_ada, m_norm1_w, m_w_in, m_q_norm_w, m_k_norm_w, m_w_pool, m_pool_scale, m_w_a_up, m_w_b_up, m_w_o, m_norm2_w, m_w_ff1, m_w_ff2, v_w_ada, v_b_ada, v_norm1_w, v_w_in, v_q_norm_w, v_k_norm_w, v_w_pool, v_pool_scale, v_w_a_up, v_w_b_up, v_w_o, v_norm2_w, v_w_ff1, v_w_ff2):
    _, S, D = x.shape
    PW = D // 2
    H = PW // HEAD_DIM
    cg = PW // N_GROUPS
    IN = w_in.shape[2] * N_CHIPS
    FF = w_ff1.shape[2] * N_CHIPS
    A_COLS = w_ada.shape[2]
    xi, yi, ci = lax.axis_index("x"), lax.axis_index("y"), lax.axis_index("c")
    chip = 2 * xi + yi
    dev = 2 * chip + ci
    c_arr = jnp.reshape(ci, (1,)).astype(jnp.int32)
    x2, tgt = x[0], loss_target[0]

    ws = [_W("w_in", "col", D, IN), _W("w_pool", "row", PW, cg), _W("w_a_up", "col", PW, D),
          _W("w_b_up", "col", PW, D), _W("w_o", "row", D, D), _W("w_ff1", "col", D, FF), _W("w_ff2", "row", FF, D)]
    w32 = [w_in[0], w_pool[0].reshape(cg, cg), w_a_up[0], w_b_up[0], w_o[0], w_ff1[0], w_ff2[0]]
    m32 = [m_w_in[0], m_w_pool[0].reshape(cg, cg), m_w_a_up[0], m_w_b_up[0], m_w_o[0], m_w_ff1[0], m_w_ff2[0]]
    v32 = [v_w_in[0], v_w_pool[0].reshape(cg, cg), v_w_a_up[0], v_w_b_up[0], v_w_o[0], v_w_ff1[0], v_w_ff2[0]]

    W_IN, W_POOL, W_A, W_B, W_O, W_FF1, W_FF2 = ws
    chip_arr = jnp.reshape(chip, (1,)).astype(jnp.int32)
    cc_arr = jnp.stack([ci, chip]).astype(jnp.int32)
    s_in, s_pool, s_a, s_b, s_o, s_ff1, s_ff2 = [_cast_into_full(w, a, chip_arr) for w, a in zip(ws, w32)]
    (win_f,) = _run_rider("gather_w_in", _ag_rider([W_IN], [s_in]))

    sc_row = _silu_rows(c)
    sc_all = _dev_allgather("gather_silu_c", sc_row.reshape(8, D // 8)).reshape(N_DEV, D)
    sc16 = jnp.concatenate([sc_all, jnp.zeros_like(sc_all)], axis=0)
    b_cols = lax.dynamic_slice(b_ada, (0, chip * A_COLS), (1, A_COLS))
    (mod_cols,) = _mm("mod_cols", [(sc16, w_ada[0])], M=2 * N_DEV, N=A_COLS, K=D, tm=16, tn=1024, tk=1024,
                      a_pro=lambda a: a.astype(BF16), b_pro=lambda b: b.astype(BF16),
                      extras=[(b_cols, "row", 0)], outs=[_tile_out(F32)], epi=lambda accs, ex: [accs[0] + ex[0]])
    mod_all = _dev_allgather("gather_mod", mod_cols[:N_DEV]).reshape(N_CHIPS, 2, N_DEV, A_COLS)
    mod_row = lax.dynamic_index_in_dim(mod_all[:, 0], dev, axis=1, keepdims=False).reshape(1, N_CHIPS * A_COLS)
    shift1, scale1, gate1, shift2, scale2, gate2 = [mod_row[:, i * D:(i + 1) * D] for i in range(6)]

    WIDE = dict(tm=2048, tn=512, tk=2048)
    DEEP = dict(tm=1024, tn=1024, tk=1024)
    h = _norm_mod("norm1_mod", x2, norm1_w, scale1, shift1)
    (proj,), ((wpool_f, wa_f, wb_f, wo_f),) = _mm(
        "in_proj", [(h, win_f)], M=S, N=IN, K=D, outs=[_tile_out(BF16)], epi=lambda accs, ex: [accs[0]], **WIDE,
        riders=[_ag_rider([W_POOL, W_A, W_B, W_O], [s_pool, s_a, s_b, s_o], n_ch=2)])
    pooled, pa = _pool_fwd(proj, wpool_f, pool_scale, S, PW)
    (att, attf), ((wff1_f,),) = _attn_fwd(proj, q_norm_w, k_norm_w, S, H, PW // HEAD_DIM,
                                          riders=[_ag_rider([W_FF1], [s_ff1])])

    def merge_epi(accs, ex):
        sa, sb = jax.nn.sigmoid(ex[0].astype(F32)), jax.nn.sigmoid(ex[1].astype(F32))
        return [sa * accs[0] + sb * accs[1], accs[0], accs[1]]

    (merged, ya, yb), (ff2_a,) = _mm("branch_up_merge", [(pa, wa_f), (att, wb_f)], M=S, N=D, K=PW,
                                     extras=[(proj, "tile", 4 * PW), (proj, "tile", 4 * PW + D)],
                                     outs=[_tile_out(BF16)] * 3, epi=merge_epi,
                                     riders=[_ag_rider([W_FF2], [s_ff2], chunks=(0, 1))])
    (x1, o), (ff2_b,) = _mm("out_proj", [(merged, wo_f)], M=S, N=D, K=D, extras=[(x2, "tile", 0), (gate1, "row", 0)],
                            outs=[_tile_out(F32), _tile_out(BF16)], epi=lambda accs, ex: [ex[0] + ex[1] * accs[0], accs[0]],
                            riders=[_ag_rider([W_FF2], ff2_a, chunks=(1, 2))], **WIDE)
    h2 = _norm_mod("norm2_mod", x1, norm2_w, scale2, shift2)
    (rl,), ((wff2_f,),) = _mm("ff1", [(h2, wff1_f)], M=S, N=FF, K=D, outs=[_tile_out(BF16)], **WIDE,
                              epi=lambda accs, ex: [jnp.maximum(accs[0], 0.0)],
                              riders=[_ag_rider([W_FF2], ff2_b, chunks=(2, 4))])

    def square(a):
        af = a.astype(F32)
        return (af * af).astype(BF16)

    def loss_epi(accs, ex):
        x1_t, tgt_t, g2 = ex
        f = accs[0]
        diff = (x1_t + g2 * f) - tgt_t
        dy = diff * (1.0 / D)
        return [dy, dy * g2, _colsum(dy * f), _colsum(diff * diff)]

    dy, df, dgate2_p, loss_p = _mm("ff2_loss", [(rl, wff2_f)], M=S, N=D, K=FF, a_pro=square, tm=1024, tn=1024, tk=512,
                                   extras=[(x1, "tile", 0), (tgt, "tile", 0), (gate2, "row", 0)],
                                   outs=[_tile_out(F32), _tile_out(BF16), _COLSUM, _COLSUM], epi=loss_epi)

    def pair_sums(group, partials, got):
        return [_pair_sum(w, g, r, c_arr) for w, g, r in zip(group, partials, got)]

    def chip_sums(group, sums, from_chips):
        return [_chip_sum(w, p, q, cc_arr) for w, p, q in zip(group, sums, from_chips)]

    first = lambda accs, ex: [accs[0]]
    gmm = dict(ta=True, outs=[_tile_out(BF16)], epi=first, **WIDE)
    (g_ff2,) = _mm("grad_w_ff2", [(rl, df)], M=FF, N=D, K=S, a_pro=square, ta=True, tm=512, tn=2048, tk=2048,
                   outs=[_tile_out(BF16)], epi=first)
    (dz1,), (got_ff2,) = _mm("d_ff_hidden", [(df, wff2_f)], M=S, N=FF, K=D, tb=True, extras=[(rl, "tile", 0)], **WIDE,
                             outs=[_tile_out(BF16)], epi=lambda accs, ex: [accs[0] * (2.0 * ex[0].astype(F32))],
                             riders=[_px_rider([W_FF2], [g_ff2])])
    sum_ff2 = pair_sums([W_FF2], [g_ff2], got_ff2)
    (g_ff1,), (q_ff2,) = _mm("grad_w_ff1", [(h2, dz1)], M=D, N=FF, K=S,
                             riders=[_cx_rider([W_FF2], sum_ff2, part=(0, 2))], **gmm)
    (dh2,), (got_ff1, q_ff2) = _mm("d_h2", [(dz1, wff1_f)], M=S, N=D, K=FF, tb=True, outs=[_tile_out(F32)], epi=first,
                                   riders=[_px_rider([W_FF1], [g_ff1]),
                                           _cx_rider([W_FF2], sum_ff2, part=(1, 2), q_in=q_ff2)], **DEEP)
    sum_ff1 = pair_sums([W_FF1], [g_ff1], got_ff1)
    dx1, dshift2_p, dscale2_p, gn2_p, do, dgate1_p = _norm_mod_bwd("norm2_bwd", dh2, x1, dy, norm2_w, scale2,
                                                                   gate_o=(o, gate1))
    (g_wo,) = _mm("grad_w_o", [(merged, do)], M=D, N=D, K=S, **gmm)

    def gate_epi(accs, ex):
        dm = accs[0]
        sa, sb = jax.nn.sigmoid(ex[0].astype(F32)), jax.nn.sigmoid(ex[1].astype(F32))
        ya_t, yb_t = ex[2].astype(F32), ex[3].astype(F32)
        return [dm * sa, dm * sb, dm * ya_t * (sa * (1.0 - sa)), dm * yb_t * (sb * (1.0 - sb))]

    dya, dyb, dga, dgb = _mm("d_merged", [(do, wo_f)], M=S, N=D, K=D, tb=True, tm=1024, tn=512, tk=2048,
                             extras=[(proj, "tile", 4 * PW), (proj, "tile", 4 * PW + D), (ya, "tile", 0), (yb, "tile", 0)],
                             outs=[_tile_out(BF16)] * 4, epi=gate_epi)
    (g_wa,) = _mm("grad_w_a_up", [(pa, dya)], M=PW, N=D, K=S, **gmm)
    (g_wb,) = _mm("grad_w_b_up", [(att, dyb)], M=PW, N=D, K=S, **gmm)
    (dpa,) = _mm("d_pool_out", [(dya, wa_f)], M=S, N=PW, K=D, tb=True, outs=[_tile_out(F32)], epi=first, **WIDE)
    mid = [W_A, W_B, W_O]
    (datt,), (got_mid,) = _mm("d_att", [(dyb, wb_f)], M=S, N=PW, K=D, tb=True, outs=[_tile_out(BF16)], epi=first, **WIDE,
                              riders=[_px_rider(mid, [g_wa, g_wb, g_wo])])
    sum_mid = pair_sums(mid, [g_wa, g_wb, g_wo], got_mid)
    du, g_wpool4, gscale_p = _pool_bwd(dpa, pooled, wpool_f, pool_scale, S, PW)
    (dq, dk, dv, gq_p, gk_p), ((q_ff1,),) = _attn_bwd(
        proj, datt, attf, q_norm_w, k_norm_w, S, H, PW // HEAD_DIM, riders=[_cx_rider([W_FF1], sum_ff1)])
    dproj = jnp.concatenate([du, dq, dk, dv, dga, dgb], axis=1)
    early = [W_FF1, W_FF2]
    halves_early = chip_sums(early, sum_ff1 + sum_ff2, [q_ff1, q_ff2[0]])
    (g_win,), (grads_early, (q_wa, q_wb, q_wo)) = _mm(
        "grad_w_in", [(h, dproj)], M=D, N=IN, K=S, riders=[_sf_rider(early, halves_early), _cx_rider(mid, sum_mid)], **gmm)
    last = [W_IN, W_POOL]
    g_last = [g_win, g_wpool4.reshape(PW, cg)]
    halves_mid = chip_sums(mid, sum_mid, [q_wa, q_wb, q_wo])
    (dh,), (got_last, grads_mid) = _mm("d_h", [(dproj, win_f)], M=S, N=D, K=IN, tb=True, outs=[_tile_out(F32)], epi=first,
                                       riders=[_px_rider(last, g_last), _sf_rider(mid, halves_mid)], **DEEP)
    sum_last = pair_sums(last, g_last, got_last)
    grad_x, dshift1_p, dscale1_p, gn1_p = _norm_mod_bwd("norm1_bwd", dh, x2, dx1, norm1_w, scale1)

    parts = [dshift1_p, dscale1_p, dgate1_p, dshift2_p, dscale2_p, dgate2_p, gn1_p, gn2_p,
             gscale_p.reshape(1, 1, PW), gq_p, gk_p]
    widths = [D] * 8 + [PW, HEAD_DIM, HEAD_DIM]
    used = sum(widths)
    P = -(-(used + 128) // 1024) * 1024
    packed = _pack_partials(parts + [loss_p], widths, P)
    gathered = _dev_allgather("gather_vector_grads", packed.reshape(8, P // 8)).reshape(N_DEV, P)
    sum_last, gathered = lax.optimization_barrier((sum_last, gathered))
    cx_send, cx_recv, sum_last, land_last, token = _cx_start(last, sum_last)
    small = [(b_ada, m_b_ada, v_b_ada), (norm1_w, m_norm1_w, v_norm1_w), (norm2_w, m_norm2_w, v_norm2_w),
             (pool_scale, m_pool_scale, v_pool_scale), (q_norm_w, m_q_norm_w, v_q_norm_w),
             (k_norm_w, m_k_norm_w, v_k_norm_w)]
    offsets = [(0, 6 * D), (6 * D, D), (7 * D, D), (8 * D, PW), (8 * D + PW, HEAD_DIM), (8 * D + PW + HEAD_DIM, HEAD_DIM)]
    su = _small_update(gathered, offsets, small, used)
    (g_b, d_b, nm_b, nv_b, g_n1, d_n1, nm_n1, nv_n1, g_n2, d_n2, nm_n2, nv_n2, g_ps, d_ps, nm_ps, nv_ps,
     g_qn, d_qn, nm_qn, nv_qn, g_kn, d_kn, nm_kn, nv_kn, loss_sum) = su
    dmod_sh = lax.dynamic_slice(gathered, (0, chip * A_COLS), (N_DEV, A_COLS))
    dmod_sh, token = lax.optimization_barrier((dmod_sh, token))
    g_ada, d_ada, nm_ada, nv_ada = _ada_update(sc_all.T, dmod_sh, w_ada[0], m_w_ada[0], v_w_ada[0])

    upd_done = [_adamw("adamw_" + w.name, a, g, m, v, after=token)
                for w, a, g, m, v in zip(ws[2:], w32[2:], list(grads_mid) + list(grads_early), m32[2:], v32[2:])]

    sum_last, q_last = _cx_wait(last, cx_send, cx_recv, sum_last, land_last,
                                after=[nv_ada] + [u[3] for u in upd_done])
    halves_last = chip_sums(last, sum_last, q_last)
    filled = _run_rider("grad_sibling_fill", _sf_rider(last, halves_last))
    upd = [_adamw("adamw_" + w.name, a, g, m, v) for w, a, g, m, v in zip(ws[:2], w32[:2], filled, m32[:2], v32[:2])]
    upd += upd_done

    loss = (0.5 / D) * loss_sum[0, 0]

    def up(a):
        return a[None]

    def pool4(a):
        return a.reshape(1, N_GROUPS, cg // N_CHIPS, cg)

    (gr_win, d_win, nm_win, nv_win), (gr_wp, d_wp, nm_wp, nv_wp), (gr_wa, d_wa, nm_wa, nv_wa), \
        (gr_wb, d_wb, nm_wb, nv_wb), (gr_wo, d_wo, nm_wo, nv_wo), (gr_f1, d_f1, nm_f1, nv_f1), \
        (gr_f2, d_f2, nm_f2, nv_f2) = upd
    return (
        loss, grad_x[None],
        up(g_ada), g_b, g_n1, up(gr_win), g_qn, g_kn, pool4(gr_wp), g_ps, up(gr_wa), up(gr_wb), up(gr_wo), g_n2,
        up(gr_f1), up(gr_f2),
        up(d_ada), d_b, d_n1, up(d_win), d_qn, d_kn, pool4(d_wp), d_ps, up(d_wa), up(d_wb), up(d_wo), d_n2,
        up(d_f1), up(d_f2),
        up(nm_ada), nm_b, nm_n1, up(nm_win), nm_qn, nm_kn, pool4(nm_wp), nm_ps, up(nm_wa), up(nm_wb), up(nm_wo), nm_n2,
        up(nm_f1), up(nm_f2),
        up(nv_ada), nv_b, nv_n1, up(nv_win), nv_qn, nv_kn, pool4(nv_wp), nv_ps, up(nv_wa), up(nv_wb), up(nv_wo), nv_n2,
        up(nv_f1), up(nv_f2),
    )
```

```python
import functools
import math

import jax
import jax.numpy as jnp
from jax import lax
from jax.experimental import pallas as pl
from jax.experimental.pallas import tpu as pltpu

F32 = jnp.float32
BF16 = jnp.bfloat16
MESH = pl.DeviceIdType.MESH
ANY = pl.BlockSpec(memory_space=pl.ANY)

EPS = 1e-6
HEAD_DIM = 128
POOL_WINDOWS = (2, 4, 8, 16)
N_GROUPS = len(POOL_WINDOWS)
N_CHIPS = 4
N_DEV = 8
ADAM_LR, ADAM_B1, ADAM_B2, ADAM_EPS, ADAM_WD, ADAM_STEP = 0.001, 0.9, 0.999, 1e-08, 0.01, 10
VMEM_LIMIT_V7X = 56 * 1024 * 1024
ATT_T = 256
ATT_GROUP = 4
POOL_T = 256


def _pcall(body, **kw):
    return pl.pallas_call(body, **kw)


def _params(sem=None):
    return pltpu.CompilerParams(dimension_semantics=sem, vmem_limit_bytes=VMEM_LIMIT_V7X)


def _tile(n, pref):
    if n <= pref:
        return n
    t = pref
    while n % t:
        t //= 2
    return t


class _Rider:
    def __init__(self, arrays, out_shape, sems, start, finish, aliases=None, steps=()):
        self.arrays, self.out_shape, self.sems = list(arrays), list(out_shape), list(sems)
        self.start, self.finish, self.aliases, self.steps = start, finish, aliases or {}, list(steps)


def _ride(name, body, riders, arrays, *, grid, in_specs, out_specs, out_shape, scratch_shapes, sem):
    n_in, n_out, n_scr = len(arrays), len(out_shape), len(scratch_shapes)
    r_arrays = [a for r in riders for a in r.arrays]
    r_outs = [o for r in riders for o in r.out_shape]
    r_sems = [s for r in riders for s in r.sems]
    n_hooks = max([len(r.steps) for r in riders], default=0)
    total = math.prod(grid)
    aliases, off_i, off_o = {}, n_in, n_out
    for r in riders:
        for a, o in r.aliases.items():
            aliases[off_i + a] = off_o + o
        off_i += len(r.arrays)
        off_o += len(r.out_shape)

    def full(*refs):
        p = 0
        groups = []
        for n in (n_in, len(r_arrays), n_out, len(r_outs), n_scr, len(r_sems)):
            groups.append(refs[p:p + n])
            p += n
        ins, rin, outs, rout, scr, rsem = groups

        def each(what):
            a = o = s = 0
            for r in riders:
                fn = what(r)
                if fn is not None:
                    fn(rin[a:a + len(r.arrays)], rout[o:o + len(r.out_shape)], rsem[s:s + len(r.sems)])
                a, o, s = a + len(r.arrays), o + len(r.out_shape), s + len(r.sems)

        if riders:
            lin = 0
            for d, g in enumerate(grid):
                lin = lin * g + pl.program_id(d)
            pl.when(lin == 0)(lambda: each(lambda r: r.start))
            for t in range(n_hooks):
                pl.when(lin == min(total - 1, ((t + 1) * total) // n_hooks))(
                    lambda t=t: each(lambda r: r.steps[t] if t < len(r.steps) else None))
        body(*ins, *outs, *scr)
        if riders:
            pl.when(lin == total - 1)(lambda: each(lambda r: r.finish))

    res = _pcall(
        full, name=name, grid=grid, in_specs=list(in_specs) + [ANY] * len(r_arrays),
        out_specs=list(out_specs) + [ANY] * len(r_outs), out_shape=list(out_shape) + r_outs,
        scratch_shapes=list(scratch_shapes) + r_sems, input_output_aliases=aliases,
        compiler_params=_params(("arbitrary",) * len(grid) if riders else sem),
    )(*arrays, *r_arrays)
    if not riders:
        return res
    main, rest, per = res[:n_out], res[n_out:], []
    for r in riders:
        per.append(rest[:len(r.out_shape)])
        rest = rest[len(r.out_shape):]
    return main, per


def _run_rider(name, rider):
    def body(*refs):
        n_a, n_o = len(rider.arrays), len(rider.out_shape)
        ins, outs, sems = refs[:n_a], refs[n_a:n_a + n_o], refs[n_a + n_o:]
        for fn in [rider.start] + rider.steps + [rider.finish]:
            fn(ins, outs, sems)

    return _pcall(body, name=name, out_shape=rider.out_shape, in_specs=[ANY] * len(rider.arrays),
                  out_specs=[ANY] * len(rider.out_shape), scratch_shapes=rider.sems,
                  input_output_aliases=rider.aliases)(*rider.arrays)


def _mm(name, pairs, *, M, N, K, ta=False, tb=False, tm=512, tn=1024, tk=1024,
        a_pro=None, b_pro=None, extras=(), outs, epi, riders=(), b_noff=0):
    tm, tn, tk = _tile(M, tm), _tile(N, tn), _tile(K, tk)
    n_i, n_j, n_k = M // tm, N // tn, K // tk
    n_p, n_e = len(pairs), len(extras)
    arrays, in_specs = [], []
    for a, _ in pairs:
        arrays.append(a)
        in_specs.append(pl.BlockSpec((tk, tm), lambda i, j, k: (k, i)) if ta
                        else pl.BlockSpec((tm, tk), lambda i, j, k: (i, k)))
    for _, b in pairs:
        arrays.append(b)
        in_specs.append(pl.BlockSpec((tn, tk), lambda i, j, k: (j + b_noff // tn, k)) if tb
                        else pl.BlockSpec((tk, tn), lambda i, j, k: (k, j + b_noff // tn)))
    for arr, kind, off in extras:
        ob = off // tn
        assert off % tn == 0
        arrays.append(arr)
        if kind == "tile":
            in_specs.append(pl.BlockSpec((tm, tn), lambda i, j, k, ob=ob: (i, j + ob)))
        else:
            in_specs.append(pl.BlockSpec((1, tn), lambda i, j, k, ob=ob: (0, j + ob)))
    out_shape, out_specs = [], []
    for o in outs:
        if o["kind"] == "tile":
            out_shape.append(jax.ShapeDtypeStruct((M, N), o["dtype"]))
            out_specs.append(pl.BlockSpec((tm, tn), lambda i, j, k: (i, j)))
        else:
            out_shape.append(jax.ShapeDtypeStruct((n_i, 1, N), F32))
            out_specs.append(pl.BlockSpec((1, 1, tn), lambda i, j, k: (i, 0, j)))
    dims = (((0 if ta else 1,), (1 if tb else 0,)), ((), ()))

    def body(*refs):
        a_refs, b_refs = refs[:n_p], refs[n_p:2 * n_p]
        e_refs = refs[2 * n_p:2 * n_p + n_e]
        o_refs = refs[2 * n_p + n_e:2 * n_p + n_e + len(outs)]
        acc_refs = refs[2 * n_p + n_e + len(outs):]

        def product(p):
            a, b = a_refs[p][...], b_refs[p][...]
            if a_pro is not None:
                a = a_pro(a)
            if b_pro is not None:
                b = b_pro(b)
            return lax.dot_general(a, b, dims, preferred_element_type=F32)

        def write(accs):
            vals = epi(accs, [e[...] for e in e_refs])
            for o, o_ref, val in zip(outs, o_refs, vals):
                if o["kind"] == "tile":
                    o_ref[...] = val.astype(o_ref.dtype)
                else:
                    o_ref[0] = val

        if n_k == 1:
            write([product(p) for p in range(n_p)])
            return
        k = pl.program_id(2)

        @pl.when(k == 0)
        def _():
            for acc in acc_refs:
                acc[...] = jnp.zeros_like(acc)

        for p in range(n_p):
            acc_refs[p][...] += product(p)

        pl.when(k == n_k - 1)(lambda: write([acc[...] for acc in acc_refs]))

    return _ride(name, body, riders, arrays, grid=(n_i, n_j, n_k), in_specs=in_specs, out_specs=out_specs,
                 out_shape=out_shape, scratch_shapes=[pltpu.VMEM((tm, tn), F32) for _ in pairs] if n_k > 1 else [],
                 sem=("parallel", "parallel", "arbitrary"))


def _tile_out(dtype):
    return {"kind": "tile", "dtype": dtype}


_COLSUM = {"kind": "colsum"}


def _colsum(v):
    return jnp.sum(v, axis=0, keepdims=True)


def _norm_mod(name, x, norm_w, scale, shift):
    S, D = x.shape
    tr = _tile(S, 256)

    def body(x_ref, nw_ref, sc_ref, sh_ref, h_ref):
        xv = x_ref[...]
        r = lax.rsqrt(jnp.mean(xv * xv, axis=-1, keepdims=True) + EPS)
        h_ref[...] = ((xv * r * nw_ref[...]) * (1.0 + sc_ref[...]) + sh_ref[...]).astype(BF16)

    row = pl.BlockSpec((1, D), lambda i: (0, 0))
    til = pl.BlockSpec((tr, D), lambda i: (i, 0))
    return _pcall(body, name=name, grid=(S // tr,), in_specs=[til, row, row, row], out_specs=til,
                  out_shape=jax.ShapeDtypeStruct((S, D), BF16), compiler_params=_params(("parallel",)))(
                      x, norm_w, scale, shift)


def _norm_mod_bwd(name, dh, x, dres, norm_w, scale, gate_o=None):
    S, D = x.shape
    tr = _tile(S, 256)
    n_r = S // tr
    with_gate = gate_o is not None
    dh = list(dh) if isinstance(dh, (list, tuple)) else [dh]
    n_dh = len(dh)

    def body(*refs):
        dh_refs, refs = refs[:n_dh], refs[n_dh:]
        if with_gate:
            x_ref, dres_ref, nw_ref, sc_ref, o_ref, g_ref, dx_ref, p1, p2, p3, do_ref, p4 = refs
        else:
            x_ref, dres_ref, nw_ref, sc_ref, dx_ref, p1, p2, p3 = refs
        dhv = dh_refs[0][...] if n_dh == 1 else jnp.concatenate([r[...] for r in dh_refs], axis=1)
        xv, nw = x_ref[...], nw_ref[...]
        r = lax.rsqrt(jnp.mean(xv * xv, axis=-1, keepdims=True) + EPS)
        xh = xv * r
        p1[0] = _colsum(dhv)
        p2[0] = _colsum(dhv * (xh * nw))
        dn = dhv * (1.0 + sc_ref[...])
        p3[0] = _colsum(dn * xh)
        dxh = dn * nw
        dx = dres_ref[...] + r * (dxh - xh * jnp.mean(dxh * xh, axis=-1, keepdims=True))
        dx_ref[...] = dx
        if with_gate:
            do_ref[...] = (dx * g_ref[...]).astype(BF16)
            p4[0] = _colsum(dx * o_ref[...].astype(F32))

    row = pl.BlockSpec((1, D), lambda i: (0, 0))
    til = pl.BlockSpec((tr, D), lambda i: (i, 0))
    part = pl.BlockSpec((1, 1, D), lambda i: (i, 0, 0))
    part_shape = jax.ShapeDtypeStruct((n_r, 1, D), F32)
    in_specs = [pl.BlockSpec((tr, D // n_dh), lambda i: (i, 0))] * n_dh + [til, til, row, row]
    arrays = dh + [x, dres, norm_w, scale]
    out_specs = [til, part, part, part]
    out_shape = [jax.ShapeDtypeStruct((S, D), F32), part_shape, part_shape, part_shape]
    if with_gate:
        in_specs += [til, row]
        arrays += list(gate_o)
        out_specs += [til, part]
        out_shape += [jax.ShapeDtypeStruct((S, D), BF16), part_shape]
    return _pcall(body, name=name, grid=(n_r,), in_specs=in_specs, out_specs=out_specs, out_shape=out_shape,
                  compiler_params=_params(("parallel",)))(*arrays)


def _pool_w_specs(rows, cg):
    return [pl.BlockSpec((rows, cg), lambda g, j=j: (N_GROUPS * j + g, 0)) for j in range(N_CHIPS)]


def _pool_fwd(proj, wp_full, pool_scale, S, PW):
    cg = PW // N_GROUPS
    rows = cg // N_CHIPS
    T = _tile(S, POOL_T)
    n_t = S // T

    def body(u_ref, w0, w1, w2, w3, ps_ref, pooled_ref, pa_ref):
        g = pl.program_id(0)
        win = jnp.left_shift(2, g)
        w = jnp.concatenate([w0[...], w1[...], w2[...], w3[...]], axis=0)
        t_i = lax.broadcasted_iota(jnp.int32, (T, T), 0)
        j_i = lax.broadcasted_iota(jnp.int32, (T, T), 1)
        b_cur = ((j_i <= t_i) & (j_i > t_i - win)).astype(BF16)
        b_prev = (j_i - T > t_i - win).astype(BF16)
        row = lax.broadcasted_iota(jnp.int32, (T, 1), 0)
        for r in range(n_t):
            cur = u_ref[r * T:(r + 1) * T, :]
            ws = jnp.dot(b_cur, cur, preferred_element_type=F32)
            if r > 0:
                ws += jnp.dot(b_prev, u_ref[(r - 1) * T:r * T, :], preferred_element_type=F32)
            count = jnp.minimum(row + (r * T + 1), win).astype(F32)
            pooled = (ws / count - cur.astype(F32)).astype(BF16)
            pooled_ref[r * T:(r + 1) * T, :] = pooled
            mixed = jnp.dot(pooled, w, preferred_element_type=F32)
            pa_ref[r * T:(r + 1) * T, :] = (mixed * ps_ref[...]).astype(BF16)

    col = pl.BlockSpec((S, cg), lambda g: (0, g))
    return _pcall(
        body, name="pool_fwd", grid=(N_GROUPS,),
        in_specs=[col] + _pool_w_specs(rows, cg) + [pl.BlockSpec((1, cg), lambda g: (0, g))],
        out_specs=[col, col],
        out_shape=[jax.ShapeDtypeStruct((S, PW), BF16), jax.ShapeDtypeStruct((S, PW), BF16)],
        compiler_params=_params(("parallel",)),
    )(proj, wp_full, wp_full, wp_full, wp_full, pool_scale)


def _pool_bwd(dpa, pooled, wp_full, pool_scale, S, PW):
    cg = PW // N_GROUPS
    rows = cg // N_CHIPS
    T = _tile(S, POOL_T)
    n_t = S // T

    def body(dpa_ref, pooled_ref, w0, w1, w2, w3, ps_ref, du_ref, gw_ref, gs_ref, dp_s, dpc_s, dmx_s):
        g = pl.program_id(0)
        win = jnp.left_shift(2, g)
        w = jnp.concatenate([w0[...], w1[...], w2[...], w3[...]], axis=0)
        row = lax.broadcasted_iota(jnp.int32, (T, 1), 0)
        gs = jnp.zeros((1, cg), F32)
        for r in range(n_t):
            sl = slice(r * T, (r + 1) * T)
            mixed = jnp.dot(pooled_ref[sl, :], w, preferred_element_type=F32)
            dpa_t = dpa_ref[sl, :]
            gs += _colsum(dpa_t * mixed)
            dmx = (dpa_t * ps_ref[...]).astype(BF16)
            dmx_s[sl, :] = dmx
            dpo = lax.dot_general(dmx, w, (((1,), (1,)), ((), ())), preferred_element_type=F32)
            dp_s[sl, :] = dpo
            count = jnp.minimum(row + (r * T + 1), win).astype(F32)
            dpc_s[sl, :] = (dpo / count).astype(BF16)
        gs_ref[...] = gs
        gw = lax.dot_general(pooled_ref[...], dmx_s[...], (((0,), (0,)), ((), ())), preferred_element_type=F32)
        for j in range(N_CHIPS):
            gw_ref[j, 0] = gw[j * rows:(j + 1) * rows, :].astype(BF16)
        j_i = lax.broadcasted_iota(jnp.int32, (T, T), 0)
        t_i = lax.broadcasted_iota(jnp.int32, (T, T), 1)
        b_cur = ((t_i >= j_i) & (t_i < j_i + win)).astype(BF16)
        b_next = (t_i + T < j_i + win).astype(BF16)
        for r in range(n_t):
            sl = slice(r * T, (r + 1) * T)
            acc = jnp.dot(b_cur, dpc_s[sl, :], preferred_element_type=F32)
            if r + 1 < n_t:
                acc += jnp.dot(b_next, dpc_s[(r + 1) * T:(r + 2) * T, :], preferred_element_type=F32)
            du_ref[sl, :] = (acc - dp_s[sl, :]).astype(BF16)

    col = pl.BlockSpec((S, cg), lambda g: (0, g))
    return _pcall(
        body, name="pool_bwd", grid=(N_GROUPS,),
        in_specs=[col, col] + _pool_w_specs(rows, cg) + [pl.BlockSpec((1, cg), lambda g: (0, g))],
        out_specs=[col, pl.BlockSpec((N_CHIPS, 1, rows, cg), lambda g: (0, g, 0, 0)),
                   pl.BlockSpec((1, cg), lambda g: (0, g))],
        out_shape=[jax.ShapeDtypeStruct((S, PW), BF16),
                   jax.ShapeDtypeStruct((N_CHIPS, N_GROUPS, rows, cg), BF16),
                   jax.ShapeDtypeStruct((1, PW), F32)],
        scratch_shapes=[pltpu.VMEM((S, cg), F32), pltpu.VMEM((S, cg), BF16), pltpu.VMEM((S, cg), BF16)],
        compiler_params=_params(("parallel",)),
    )(dpa, pooled, wp_full, wp_full, wp_full, wp_full, pool_scale)


_NT = (((1,), (1,)), ((), ()))
_TN = (((0,), (0,)), ((), ()))


def _split_dot(v, tri):
    hi = v.astype(BF16)
    lo = (v - hi.astype(F32)).astype(BF16)
    return jnp.dot(hi, tri, preferred_element_type=F32) + jnp.dot(lo, tri, preferred_element_type=F32)


LOG2E = 1.4426950408889634
QK_SCALE = 1.0 / math.sqrt(HEAD_DIM)


def _sb_scores(q2_i, k_j, tri_l, masked):
    tq, tk = q2_i.shape[0], k_j.shape[0]
    s = lax.dot_general(q2_i, k_j, _NT, preferred_element_type=F32)
    lp = jnp.log(1.0 + jnp.exp2(-jnp.abs(s))) * LOG2E
    lb = jnp.minimum(s, 0.0) - lp
    l = lb - s
    mask = None
    if masked:
        mask = lax.broadcasted_iota(jnp.int32, (tq, tk), 0) > lax.broadcasted_iota(jnp.int32, (tq, tk), 1)
        l = jnp.where(mask, l, 0.0)
    return l, lb, lb + _split_dot(l, tri_l), mask


def _sb_weights(t, carry_l, mask):
    a = jnp.exp2(t + carry_l)
    return a if mask is None else jnp.where(mask, a, 0.0)


def _rowsum(v):
    return jnp.sum(v, axis=1, keepdims=True)


def _qk_norm(x_ref, w_ref):
    xv = x_ref[...].astype(F32)
    r = lax.rsqrt(jnp.mean(xv * xv, axis=-1, keepdims=True) + EPS)
    return xv * r, r


def _attn_fwd(proj, q_norm_w, k_norm_w, S, H, q_off, riders=()):
    t = _tile(S, ATT_T)
    n_q = S // t

    def body(q_ref, k_ref, v_ref, qw_ref, kw_ref, att_ref, attf_ref, qn_s, kn_s):
        qh, _ = _qk_norm(q_ref, qw_ref)
        qn_s[...] = (qh * qw_ref[...] * (QK_SCALE * LOG2E)).astype(BF16)
        kh, _ = _qk_norm(k_ref, kw_ref)
        kn_s[...] = (kh * kw_ref[...]).astype(BF16)
        tri_l = (lax.broadcasted_iota(jnp.int32, (t, t), 0) > lax.broadcasted_iota(jnp.int32, (t, t), 1)).astype(BF16)

        def rows(j):
            return pl.ds(pl.multiple_of(j * t, t), t)

        def q_step(i, _):
            q_i = qn_s[rows(i), :]

            def av(a, j):
                return jnp.dot(a.astype(BF16), v_ref[rows(j), :], preferred_element_type=F32)

            l, _, tt, mask = _sb_scores(q_i, kn_s[rows(i), :], tri_l, True)
            acc = av(_sb_weights(tt, 0.0, mask), i)
            carry = _rowsum(l)

            def single(_, c):
                carry, acc = c
                l, _, tt, _ = _sb_scores(q_i, kn_s[rows(i - 1), :], tri_l, False)
                return carry + _rowsum(l), acc + av(_sb_weights(tt, carry, None), i - 1)

            carry, acc = lax.fori_loop(0, i % 2, single, (carry, acc))
            top = i - 1 - i % 2

            def pair(p, c):
                carry, acc = c
                j0 = top - 2 * p
                l0, _, t0, _ = _sb_scores(q_i, kn_s[rows(j0), :], tri_l, False)
                l1, _, t1, _ = _sb_scores(q_i, kn_s[rows(j0 - 1), :], tri_l, False)
                mid = carry + _rowsum(l0)
                acc = acc + av(_sb_weights(t0, carry, None), j0) + av(_sb_weights(t1, mid, None), j0 - 1)
                return mid + _rowsum(l1), acc

            _, acc = lax.fori_loop(0, i // 2, pair, (carry, acc))
            att_ref[rows(i), :] = acc.astype(BF16)
            attf_ref[rows(i), :] = acc
            return 0

        lax.fori_loop(0, n_q, q_step, 0)

    def col(off):
        return pl.BlockSpec((S, HEAD_DIM), lambda h, off=off: (0, off + h))

    wspec = pl.BlockSpec((1, HEAD_DIM), lambda h: (0, 0))
    return _ride(
        "attn_fwd", body, riders, [proj, proj, proj, q_norm_w, k_norm_w], grid=(H,),
        in_specs=[col(q_off), col(q_off + H), col(q_off + 2 * H), wspec, wspec],
        out_specs=[col(0), col(0)],
        out_shape=[jax.ShapeDtypeStruct((S, H * HEAD_DIM), BF16), jax.ShapeDtypeStruct((S, H * HEAD_DIM), F32)],
        scratch_shapes=[pltpu.VMEM((S, HEAD_DIM), BF16), pltpu.VMEM((S, HEAD_DIM), BF16)],
        sem=("parallel",))


def _attn_bwd(proj, datt, attf, q_norm_w, k_norm_w, S, H, q_off, riders=()):
    t = _tile(S, ATT_T)
    n_q = S // t

    def body(q_ref, k_ref, v_ref, do_ref, o_ref, qw_ref, kw_ref, dq_ref, dk_ref, dv_ref, gq_ref, gk_ref,
             qn_s, kn_s, qz_s, kz_s, dk_s, dv_s, gq_s):
        qw, kw = qw_ref[...], kw_ref[...]
        qh, _ = _qk_norm(q_ref, qw_ref)
        qn_s[...] = (qh * qw * (QK_SCALE * LOG2E)).astype(BF16)
        qz_s[...] = (qh * qw * QK_SCALE).astype(BF16)
        kh, _ = _qk_norm(k_ref, kw_ref)
        kn_s[...] = (kh * kw).astype(BF16)
        kz_s[...] = (kh * kw * QK_SCALE).astype(BF16)
        dk_s[...] = jnp.zeros_like(dk_s)
        dv_s[...] = jnp.zeros_like(dv_s)
        gq_s[...] = jnp.zeros_like(gq_s)
        r_i = lax.broadcasted_iota(jnp.int32, (t, t), 0)
        c_i = lax.broadcasted_iota(jnp.int32, (t, t), 1)
        tri_l = (r_i > c_i).astype(BF16)
        tri_e = (r_i >= c_i).astype(BF16)

        def rows(j):
            return pl.ds(pl.multiple_of(j * t, t), t)

        def q_step(i, _):
            q_i = qn_s[rows(i), :]
            do_i = do_ref[rows(i), :]
            d_i = _rowsum(do_i.astype(F32) * o_ref[rows(i), :])

            def scores(j, masked):
                l, lb, tt, mask = _sb_scores(q_i, kn_s[rows(j), :], tri_l, masked)
                da = lax.dot_general(do_i, v_ref[rows(j), :], _NT, preferred_element_type=F32)
                return l, lb, tt, mask, da

            def grads(j, sc, carry_l, carry_e, dq_acc):
                l, lb, tt, mask, da = sc
                a_bf = _sb_weights(tt, carry_l, mask).astype(BF16)
                e = da * a_bf.astype(F32)
                p = (d_i - carry_e) - _split_dot(e, tri_e)
                dz = e - jnp.exp2(lb) * (e + p)
                if mask is not None:
                    dz = jnp.where(mask, dz, 0.0)
                dz = dz.astype(BF16)
                dk_s[rows(j), :] += lax.dot_general(dz, qz_s[rows(i), :], _TN, preferred_element_type=F32)
                dv_s[rows(j), :] += lax.dot_general(a_bf, do_i, _TN, preferred_element_type=F32)
                return (carry_l + _rowsum(l), carry_e + _rowsum(e),
                        dq_acc + jnp.dot(dz, kz_s[rows(j), :], preferred_element_type=F32))

            zero = jnp.zeros((t, 1), F32)
            first = (zero, zero, jnp.zeros((t, HEAD_DIM), F32))

            def group(js, diagonal_first, c):
                scs = [scores(j, diagonal_first and n == 0) for n, j in enumerate(js)]
                for j, sc in zip(js, scs):
                    c = grads(j, sc, *c)
                return c

            n_first = i % ATT_GROUP
            c = lax.switch(n_first, [functools.partial(group, [i - u for u in range(n + 1)], True, first)
                                     for n in range(ATT_GROUP)])
            top = i - 1 - n_first

            def whole(p, c):
                j0 = top - ATT_GROUP * p
                return group([j0 - u for u in range(ATT_GROUP)], False, c)

            _, _, dqn = lax.fori_loop(0, (i - n_first) // ATT_GROUP, whole, c)
            qv = q_ref[rows(i), :].astype(F32)
            r = lax.rsqrt(jnp.mean(qv * qv, axis=-1, keepdims=True) + EPS)
            xh = qv * r
            gq_s[...] += _colsum(dqn * xh)
            dxh = dqn * qw
            dq_ref[rows(i), :] = (r * (dxh - xh * jnp.mean(dxh * xh, axis=-1, keepdims=True))).astype(BF16)
            return 0

        lax.fori_loop(0, n_q, q_step, 0)
        gq_ref[0] = gq_s[...]
        kh, rk = _qk_norm(k_ref, kw_ref)
        dkn = dk_s[...]
        gk_ref[0] = _colsum(dkn * kh)
        dxh = dkn * kw
        dk_ref[...] = (rk * (dxh - kh * jnp.mean(dxh * kh, axis=-1, keepdims=True))).astype(BF16)
        dv_ref[...] = dv_s[...].astype(BF16)

    def col(off):
        return pl.BlockSpec((S, HEAD_DIM), lambda h, off=off: (0, off + h))

    wspec = pl.BlockSpec((1, HEAD_DIM), lambda h: (0, 0))
    gspec = pl.BlockSpec((1, 1, HEAD_DIM), lambda h: (h, 0, 0))
    act = jax.ShapeDtypeStruct((S, H * HEAD_DIM), BF16)
    gsh = jax.ShapeDtypeStruct((H, 1, HEAD_DIM), F32)
    return _ride(
        "attn_bwd", body, riders, [proj, proj, proj, datt, attf, q_norm_w, k_norm_w], grid=(H,),
        in_specs=[col(q_off), col(q_off + H), col(q_off + 2 * H), col(0), col(0), wspec, wspec],
        out_specs=[col(0), col(0), col(0), gspec, gspec],
        out_shape=[act, act, act, gsh, gsh],
        scratch_shapes=[pltpu.VMEM((S, HEAD_DIM), BF16)] * 4 + [pltpu.VMEM((S, HEAD_DIM), F32)] * 2
        + [pltpu.VMEM((1, HEAD_DIM), F32)],
        sem=("parallel",))


def _place():
    x, y, c = lax.axis_index("x"), lax.axis_index("y"), lax.axis_index("c")
    chips = [(1 - x, y), (x, 1 - y), (1 - x, 1 - y)]
    return x, y, c, chips


def _dev_allgather(name, v):
    m_per, n = v.shape

    def body(x_ref, out_ref, send_sems, recv_sems, local_sem):
        x, y, c, chips = _place()
        me, sibling = (x, y, c), (x, y, 1 - c)

        def rows(px, py, pc):
            return out_ref.at[pl.ds((4 * px + 2 * py + pc) * m_per, m_per), :]

        def copy(k, block, to, src=None):
            return pltpu.make_async_remote_copy(
                src_ref=rows(*block) if src is None else src, dst_ref=rows(*block),
                send_sem=send_sems.at[k], recv_sem=recv_sems.at[k], device_id=to, device_id_type=MESH)

        mine = pltpu.make_async_copy(x_ref, rows(*me), local_sem)
        mine.start()
        first = [copy(0, me, sibling, src=x_ref)]
        first += [copy(1 + j, me, (*chip, c), src=x_ref) for j, chip in enumerate(chips)]
        for cp in first:
            cp.start()
        passed = [copy(4 + j, (*chip, c), sibling) for j, chip in enumerate(chips)]
        for j, chip in enumerate(chips):
            copy(1 + j, (*chip, c), me).wait_recv()
            passed[j].start()
        copy(0, sibling, me).wait_recv()
        for j, chip in enumerate(chips):
            copy(4 + j, (*chip, 1 - c), me).wait_recv()
        for cp in first + passed:
            cp.wait_send()
        mine.wait()

    return _pcall(
        body, name=name, out_shape=jax.ShapeDtypeStruct((N_DEV * m_per, n), v.dtype),
        in_specs=[pl.BlockSpec(memory_space=pltpu.VMEM)], out_specs=pl.BlockSpec(memory_space=pltpu.VMEM),
        scratch_shapes=[pltpu.SemaphoreType.DMA((7,)), pltpu.SemaphoreType.DMA((7,)), pltpu.SemaphoreType.DMA],
        compiler_params=pltpu.CompilerParams(vmem_limit_bytes=VMEM_LIMIT_V7X),
    )(v)


class _W:
    def __init__(self, name, kind, R, C):
        self.name, self.kind, self.R, self.C = name, kind, R, C

    @property
    def shard_shape(self):
        return (self.R, self.C // N_CHIPS) if self.kind == "col" else (self.R // N_CHIPS, self.C)

    @property
    def half_rows(self):
        return self.shard_shape[0] // 2

    def shard_half(self, ref, half):
        return ref.at[pl.ds(half * self.half_rows, self.half_rows), :]

    def region(self, full_ref, chip, half):
        hr = self.half_rows
        if self.kind == "col":
            cw = self.C // N_CHIPS
            return full_ref.at[pl.ds(half * hr, hr), pl.ds(chip * cw, cw)]
        return full_ref.at[pl.ds(chip * (2 * hr) + half * hr, hr), :]

    def region_both(self, full_ref, chip):
        hr = self.half_rows
        if self.kind == "col":
            cw = self.C // N_CHIPS
            return full_ref.at[:, pl.ds(chip * cw, cw)]
        return full_ref.at[pl.ds(chip * (2 * hr), 2 * hr), :]


def _ag_rider(ws, fulls, n_ch=4, chunks=None):
    n_w = len(ws)
    lo, hi = chunks or (0, n_ch)
    per = 6

    def parts(full, sems):
        send_sems, recv_sems = sems
        x, y, c, _ = _place()
        xn, yn, dg = (1 - x, y), (x, 1 - y), (1 - x, 1 - y)
        via = (x + (1 - c) * (1 - 2 * x), y + c * (1 - 2 * y))
        to = (x + c * (1 - 2 * x), y + (1 - c) * (1 - 2 * y))

        def reg(i, chip, half, t):
            nr = ws[i].half_rows // n_ch
            return ws[i].region(full[i], 2 * chip[0] + chip[1], half).at[pl.ds(t * nr, nr), :]

        def copy(r, i, t, k, dev):
            s = (i * (hi - lo) + t - lo) * per + k
            return pltpu.make_async_remote_copy(src_ref=r, dst_ref=r, send_sem=send_sems.at[s],
                                                recv_sem=recv_sems.at[s], device_id=dev, device_id_type=MESH)

        def direct(i, t, k):
            return copy(reg(i, (x, y), c, t), i, t, k, (*(via, to)[k], c))

        def direct_in(i, t, k):
            return copy(reg(i, (via, to)[k], c, t), i, t, k, (*(via, to)[k], c))

        def relay(i, t):
            return copy(reg(i, via, c, t), i, t, 2, (*to, c))

        def relay_in(i, t):
            return copy(reg(i, dg, c, t), i, t, 2, (*to, c))

        def hand(i, t, k, half):
            return copy(reg(i, (xn, yn, dg)[k], half, t), i, t, 3 + k, (x, y, 1 - c))

        return c, direct, direct_in, relay, relay_in, hand

    def start(_, full, sems):
        _, direct, _, _, _, _ = parts(full, sems)
        for t in range(lo, hi):
            for i in range(n_w):
                direct(i, t, 0).start()
                direct(i, t, 1).start()

    def arrived(t):
        def step(_, full, sems):
            c, _, direct_in, relay, relay_in, hand = parts(full, sems)
            for i in range(n_w):
                direct_in(i, t, 0).wait_recv()
                direct_in(i, t, 1).wait_recv()
                relay(i, t).start()
                hand(i, t, 0, c).start()
                hand(i, t, 1, c).start()
        return step

    def finish(_, full, sems):
        c, direct, _, relay, relay_in, hand = parts(full, sems)
        for t in range(lo, hi):
            for i in range(n_w):
                relay_in(i, t).wait_recv()
                hand(i, t, 2, c).start()
        for i in range(n_w):
            for t in range(lo, hi):
                for k in range(3):
                    hand(i, t, k, 1 - c).wait_recv()
        for i in range(n_w):
            for t in range(lo, hi):
                direct(i, t, 0).wait_send()
                direct(i, t, 1).wait_send()
                relay(i, t).wait_send()
                for k in range(3):
                    hand(i, t, k, c).wait_send()

    n_sem = per * (hi - lo) * n_w
    return _Rider(fulls, [jax.ShapeDtypeStruct((w.R, w.C), BF16) for w in ws],
                  [pltpu.SemaphoreType.DMA((n_sem,)), pltpu.SemaphoreType.DMA((n_sem,))], start, finish,
                  steps=[arrived(t) for t in range(lo, hi)], aliases={i: i for i in range(n_w)})


def _cast_into_full(w, a32, chip_arr):
    sr, sc = w.shard_shape
    tr, tc = _tile(sr, 512), _tile(sc, 2048)
    n_r, n_c = sr // tr, sc // tc
    if w.kind == "col":
        out_spec = pl.BlockSpec((tr, tc), lambda i, j, chip: (i, chip[0] * n_c + j))
    else:
        out_spec = pl.BlockSpec((tr, tc), lambda i, j, chip: (chip[0] * n_r + i, j))

    def body(chip_ref, a_ref, o_ref):
        o_ref[...] = a_ref[...].astype(BF16)

    return _pcall(
        body, name="cast_" + w.name, out_shape=jax.ShapeDtypeStruct((w.R, w.C), BF16),
        grid_spec=pltpu.PrefetchScalarGridSpec(
            num_scalar_prefetch=1, grid=(n_r, n_c),
            in_specs=[pl.BlockSpec((tr, tc), lambda i, j, chip: (i, j))], out_specs=out_spec),
        compiler_params=_params(("parallel", "parallel")),
    )(chip_arr, a32)


def _half_view(w, g):
    return g if w.kind == "col" else g.reshape(N_CHIPS, w.R // N_CHIPS, w.C)


def _px_rider(ws, grads):
    n_w = len(ws)

    def copies(g, got, sems):
        send_sems, recv_sems = sems
        x, y, c, _ = _place()

        def half_all(w, ref, half):
            hr = w.half_rows
            if w.kind == "col":
                return ref.at[pl.ds(half * hr, hr), :]
            return ref.at[:, pl.ds(half * hr, hr), :]

        return [pltpu.make_async_remote_copy(
            src_ref=half_all(w, g[i], 1 - c), dst_ref=got[i], send_sem=send_sems.at[i], recv_sem=recv_sems.at[i],
            device_id=(x, y, 1 - c), device_id_type=MESH) for i, w in enumerate(ws)]

    def start(g, got, sems):
        for cp in copies(g, got, sems):
            cp.start()

    def finish(g, got, sems):
        for cp in copies(g, got, sems):
            cp.wait_recv()
            cp.wait_send()

    def got_shape(w):
        hr = w.half_rows
        return (hr, w.C) if w.kind == "col" else (N_CHIPS, hr, w.C)

    return _Rider([_half_view(w, g) for w, g in zip(ws, grads)],
                  [jax.ShapeDtypeStruct(got_shape(w), BF16) for w in ws],
                  [pltpu.SemaphoreType.DMA((n_w,)), pltpu.SemaphoreType.DMA((n_w,))], start, finish)


def _pair_sum(w, g, got, c_arr):
    hr = w.half_rows
    if w.kind == "col":
        tr, tc = _tile(hr, 512), _tile(w.C, 2048)
        n_r = hr // tr
        grid = (n_r, w.C // tc)
        g_spec = pl.BlockSpec((tr, tc), lambda i, j, c: (c[0] * n_r + i, j))
        o_spec = pl.BlockSpec((tr, tc), lambda i, j, c: (i, j))
    else:
        tr = _tile(hr, 512)
        n_r = hr // tr
        grid = (N_CHIPS, n_r)
        g_spec = pl.BlockSpec((1, tr, w.C), lambda s, i, c: (s, c[0] * n_r + i, 0))
        o_spec = pl.BlockSpec((1, tr, w.C), lambda s, i, c: (s, i, 0))

    def body(c_ref, g_ref, got_ref, out_ref):
        out_ref[...] = (g_ref[...].astype(F32) + got_ref[...].astype(F32)).astype(BF16)

    return _pcall(
        body, name="grad_pair_sum_" + w.name, out_shape=jax.ShapeDtypeStruct(got.shape, BF16),
        grid_spec=pltpu.PrefetchScalarGridSpec(num_scalar_prefetch=1, grid=grid, in_specs=[g_spec, o_spec],
                                               out_specs=o_spec),
        compiler_params=_params(("parallel", "parallel")),
    )(c_arr, _half_view(w, g), got)


def _cx_rider(ws, sums, part=(0, 1), q_in=None):
    n_w = len(ws)

    def parts(p, q, sems):
        send_sems, recv_sems = sems
        x, y, c, chips = _place()
        my_chip = 2 * x + y

        def rows(w, ref):
            nr = w.half_rows // part[1]
            return ref.at[pl.ds(part[0] * nr, nr), :]

        def piece(w, ref, chip):
            if w.kind == "col":
                cw = w.C // N_CHIPS
                return rows(w, ref.at[:, pl.ds(chip * cw, cw)])
            return rows(w, ref.at[chip])

        def copy(i, k, recv=False):
            chip = chips[k]
            to_chip = 2 * chip[0] + chip[1]
            return pltpu.make_async_remote_copy(
                src_ref=piece(ws[i], p[i], to_chip), dst_ref=rows(ws[i], q[i].at[to_chip if recv else my_chip]),
                send_sem=send_sems.at[3 * i + k], recv_sem=recv_sems.at[3 * i + k],
                device_id=(*chip, c), device_id_type=MESH)

        return copy

    both = [(i, k) for i in range(n_w) for k in range(N_CHIPS - 1)]

    def start(p, q, sems):
        copy = parts(p, q, sems)
        for i, k in both:
            copy(i, k).start()

    def finish(p, q, sems):
        copy = parts(p, q, sems)
        for i, k in both:
            copy(i, k, recv=True).wait_recv()
        for i, k in both:
            copy(i, k).wait_send()

    return _Rider(list(sums) + list(q_in or []),
                  [jax.ShapeDtypeStruct((N_CHIPS, w.half_rows, w.shard_shape[1]), BF16) for w in ws],
                  [pltpu.SemaphoreType.DMA((3 * n_w,)), pltpu.SemaphoreType.DMA((3 * n_w,))], start, finish,
                  aliases={n_w + i: i for i in range(n_w)} if q_in else None)


def _chip_sum(w, p, q, cc_arr):
    hr, cols = w.half_rows, w.shard_shape[1]
    tr, tc = _tile(hr, 512), _tile(cols, 2048)
    n_r, n_c = hr // tr, cols // tc

    def body(cc_ref, own, q1, q2, q3, out_ref):
        own_v = own[...] if w.kind == "col" else own[0]
        out_ref[...] = ((own_v.astype(F32) + q1[0].astype(F32)) + q2[0].astype(F32)) + q3[0].astype(F32)

    if w.kind == "col":
        own_spec = pl.BlockSpec((tr, tc), lambda i, j, cc: (i, cc[1] * n_c + j))
    else:
        own_spec = pl.BlockSpec((1, tr, tc), lambda i, j, cc: (cc[1], i, j))
    q_specs = [pl.BlockSpec((1, tr, tc), lambda i, j, cc, s=s: ((cc[1] + s) % N_CHIPS, i, j)) for s in (1, 2, 3)]
    return _pcall(
        body, name="grad_chip_sum_" + w.name, out_shape=jax.ShapeDtypeStruct(w.shard_shape, F32),
        grid_spec=pltpu.PrefetchScalarGridSpec(
            num_scalar_prefetch=1, grid=(n_r, n_c), in_specs=[own_spec] + q_specs,
            out_specs=pl.BlockSpec((tr, tc), lambda i, j, cc: (cc[0] * n_r + i, j))),
        compiler_params=_params(("parallel", "parallel")),
    )(cc_arr, p, q, q, q)


_SEM = pl.BlockSpec(memory_space=pltpu.SEMAPHORE)
_HBM = pl.BlockSpec(memory_space=pltpu.HBM)


def _split_copies(kind, ws, p, land, send_sems, recv_sems):
    x, y, c, chips = _place()
    my_chip = 2 * x + y
    pairs = []
    for i, w in enumerate(ws):
        if kind == "pair":
            hr = w.half_rows
            src = p[i].at[pl.ds((1 - c) * hr, hr), :] if w.kind == "col" else p[i].at[:, pl.ds((1 - c) * hr, hr), :]
            cp = pltpu.make_async_remote_copy(src_ref=src, dst_ref=land[i], send_sem=send_sems.at[i],
                                              recv_sem=recv_sems.at[i], device_id=(x, y, 1 - c), device_id_type=MESH)
            pairs.append((cp, cp))
            continue
        for k, chip in enumerate(chips):
            to_chip = 2 * chip[0] + chip[1]
            src = p[i].at[:, pl.ds(to_chip * (w.C // N_CHIPS), w.C // N_CHIPS)] if w.kind == "col" else p[i].at[to_chip]
            kw = dict(send_sem=send_sems.at[3 * i + k], recv_sem=recv_sems.at[3 * i + k], device_id=(*chip, c),
                      device_id_type=MESH)
            pairs.append((pltpu.make_async_remote_copy(src_ref=src, dst_ref=land[i].at[my_chip], **kw),
                          pltpu.make_async_remote_copy(src_ref=src, dst_ref=land[i].at[to_chip], **kw)))
    return pairs


def _split_start(name, kind, ws, arrays):
    n_w = len(ws)
    if kind == "pair":
        arrays = [_half_view(w, g) for w, g in zip(ws, arrays)]
        lands = [lax.empty((w.half_rows, w.C) if w.kind == "col" else (N_CHIPS, w.half_rows, w.C), BF16) for w in ws]
    else:
        lands = [lax.empty((N_CHIPS, w.half_rows, w.shard_shape[1]), BF16) for w in ws]
    n_sem = n_w if kind == "pair" else 3 * n_w

    def body(*refs):
        p, land = refs[:n_w], refs[n_w:2 * n_w]
        for out, _ in _split_copies(kind, ws, p, land, refs[2 * n_w], refs[2 * n_w + 1]):
            out.start()
        refs[-1][...] = jnp.zeros_like(refs[-1])

    arrays = [pltpu.with_memory_space_constraint(a, pltpu.HBM) for a in list(arrays) + lands]
    res = _pcall(
        body, name=name,
        out_shape=(pltpu.SemaphoreType.DMA((n_sem,)), pltpu.SemaphoreType.DMA((n_sem,)),
                   *[pltpu.HBM(a.shape, a.dtype) for a in arrays], jax.ShapeDtypeStruct((8, 128), F32)),
        in_specs=[_HBM] * (2 * n_w),
        out_specs=(_SEM, _SEM, *[_HBM] * (2 * n_w), pl.BlockSpec(memory_space=pltpu.VMEM)),
        input_output_aliases={i: 2 + i for i in range(2 * n_w)},
        compiler_params=pltpu.CompilerParams(has_side_effects=pltpu.SideEffectType.DATAFLOW_SIDE_EFFECTING),
    )(*arrays)
    return (kind, ws, res[0], res[1], list(res[2:2 + n_w]), list(res[2 + n_w:2 + 2 * n_w])), res[-1]


def _split_wait(name, flight, after):
    kind, ws, send_sems, recv_sems, arrays, lands = flight
    n_w = len(ws)

    def body(*refs):
        p, land = refs[:n_w], refs[n_w:2 * n_w]
        for _, cp in _split_copies(kind, ws, p, land, refs[2 * n_w], refs[2 * n_w + 1]):
            cp.wait_send()
            cp.wait_recv()

    res = _pcall(
        body, name=name,
        out_shape=[pltpu.HBM(a.shape, a.dtype) for a in list(arrays) + list(lands)],
        in_specs=[_HBM] * (2 * n_w) + [_SEM, _SEM] + [ANY] * len(after), out_specs=[_HBM] * (2 * n_w),
        input_output_aliases={i: i for i in range(2 * n_w)},
        compiler_params=pltpu.CompilerParams(has_side_effects=pltpu.SideEffectType.DATAFLOW_SIDE_EFFECTING),
    )(*arrays, *lands, send_sems, recv_sems, *after)
    return list(res[:n_w]), list(res[n_w:])


def _sf_rider(ws, grads):
    n_w = len(ws)

    def copy(g, sems, i, half):
        send_sems, recv_sems = sems
        x, y, c, _ = _place()
        h = c if half == "mine" else 1 - c
        reg = ws[i].shard_half(g[i], h)
        return pltpu.make_async_remote_copy(src_ref=reg, dst_ref=reg, send_sem=send_sems.at[i], recv_sem=recv_sems.at[i],
                                            device_id=(x, y, 1 - c), device_id_type=MESH)

    def start(_, g, sems):
        for i in range(n_w):
            copy(g, sems, i, "mine").start()

    def finish(_, g, sems):
        for i in range(n_w):
            copy(g, sems, i, "other").wait_recv()
            copy(g, sems, i, "mine").wait_send()

    return _Rider(grads, [jax.ShapeDtypeStruct(w.shard_shape, F32) for w in ws],
                  [pltpu.SemaphoreType.DMA((n_w,)), pltpu.SemaphoreType.DMA((n_w,))], start, finish,
                  aliases={i: i for i in range(n_w)})


def _adamw_math(w, g, m, v):
    m = ADAM_B1 * m + (1.0 - ADAM_B1) * g
    v = ADAM_B2 * v + (1.0 - ADAM_B2) * (g * g)
    m_hat = m / (1.0 - ADAM_B1 ** ADAM_STEP)
    v_hat = v / (1.0 - ADAM_B2 ** ADAM_STEP)
    delta = -ADAM_LR * (m_hat / (jnp.sqrt(v_hat) + ADAM_EPS) + ADAM_WD * w)
    return delta, m, v


def _adamw(name, w, g, m, v, after=None):
    R, C = w.shape
    tr, tc = _tile(R, 256), _tile(C, 2048)
    behind = [] if after is None else [after]

    def body(w_ref, g_ref, m_ref, v_ref, *rest):
        g_out, d_out, m_out, v_out = rest[len(behind):]
        g = g_ref[...]
        g_out[...] = g
        d_out[...], m_out[...], v_out[...] = _adamw_math(w_ref[...], g, m_ref[...], v_ref[...])

    spec = pl.BlockSpec((tr, tc), lambda i, j: (i, j))
    sh = jax.ShapeDtypeStruct((R, C), F32)
    return _pcall(body, name=name, grid=(R // tr, C // tc), in_specs=[spec] * 4 + [ANY] * len(behind),
                  out_specs=[spec] * 4, out_shape=[sh] * 4, compiler_params=_params(("parallel", "parallel")))(
                      w, g, m, v, *behind)


def _ada_update(sct, dmod_sh, w, m, v, riders=()):
    R, C = w.shape
    tr, tc = _tile(R, 256), _tile(C, 1024)

    def body(s_ref, d_ref, w_ref, m_ref, v_ref, g_out, d_out, m_out, v_out):
        s, d = s_ref[...], d_ref[...]
        g = s[:, 0:1] * d[0:1, :]
        for b in range(1, N_DEV):
            g += s[:, b:b + 1] * d[b:b + 1, :]
        g_out[...] = g
        d_out[...], m_out[...], v_out[...] = _adamw_math(w_ref[...], g, m_ref[...], v_ref[...])

    spec = pl.BlockSpec((tr, tc), lambda i, j: (i, j))
    sh = jax.ShapeDtypeStruct((R, C), F32)
    return _ride(
        "ada_update", body, riders, [sct, dmod_sh, w, m, v], grid=(R // tr, C // tc),
        in_specs=[pl.BlockSpec((tr, N_DEV), lambda i, j: (i, 0)), pl.BlockSpec((N_DEV, tc), lambda i, j: (0, j)),
                  spec, spec, spec],
        out_specs=[spec] * 4, out_shape=[sh] * 4, scratch_shapes=[], sem=("parallel", "parallel"))


def _silu_rows(c_row):
    D = c_row.shape[1]

    def body(c_ref, o_ref):
        cv = c_ref[...]
        o_ref[...] = cv * jax.nn.sigmoid(cv)

    return _pcall(body, name="silu_c", out_shape=jax.ShapeDtypeStruct((1, D), F32))(c_row)


def _pack_partials(parts, widths, total):
    n = len(widths)

    def body(*refs):
        loss_p, out_ref = refs[n], refs[n + 1]
        off = 0
        for ref, wd in zip(refs[:n], widths):
            out_ref[:, off:off + wd] = jnp.sum(ref[...], axis=0)
            off += wd
        loss = jnp.sum(jnp.sum(loss_p[...], axis=0), axis=1, keepdims=True)
        out_ref[:, off:off + 128] = jnp.broadcast_to(loss, (1, 128))
        if off + 128 < total:
            out_ref[:, off + 128:total] = jnp.zeros((1, total - off - 128), F32)

    return _pcall(body, name="pack_partials", out_shape=jax.ShapeDtypeStruct((1, total), F32))(*parts)


def _small_update(gathered, offsets, params, loss_off):
    n_p = len(params)

    def over_devices(g_ref, off, wd):
        blk = g_ref[:, off:off + wd]
        g = blk[0:1, :]
        for b in range(1, N_DEV):
            g = g + blk[b:b + 1, :]
        return g

    def body(*refs):
        g_ref = refs[0]
        prm = refs[1:1 + 3 * n_p]
        outs = refs[1 + 3 * n_p:]
        outs[4 * n_p][...] = over_devices(g_ref, loss_off, 128)
        for i, (off, wd) in enumerate(offsets):
            g = over_devices(g_ref, off, wd)
            w, m, v = prm[3 * i][...], prm[3 * i + 1][...], prm[3 * i + 2][...]
            outs[4 * i][...] = g
            outs[4 * i + 1][...], outs[4 * i + 2][...], outs[4 * i + 3][...] = _adamw_math(w, g, m, v)

    flat = [a for t in params for a in t]
    out_shape = [jax.ShapeDtypeStruct(t[0].shape, F32) for t in params for _ in range(4)]
    out_shape.append(jax.ShapeDtypeStruct((1, 128), F32))
    return _pcall(body, name="small_update", out_shape=out_shape)(gathered, *flat)


def kernel(x, c, w_ada, b_ada, norm1_w, w_in, q_norm_w, k_norm_w, w_pool, pool_scale, w_a_up, w_b_up, w_o, norm2_w, w_ff1, w_ff2, loss_target, m_w_ada, m_b_ada, m_norm1_w, m_w_in, m_q_norm_w, m_k_norm_w, m_w_pool, m_pool_scale, m_w_a_up, m_w_b_up, m_w_o, m_norm2_w, m_w_ff1, m_w_ff2, v_w_ada, v_b_ada, v_norm1_w, v_w_in, v_q_norm_w, v_k_norm_w, v_w_pool, v_pool_scale, v_w_a_up, v_w_b_up, v_w_o, v_norm2_w, v_w_ff1, v_w_ff2):
    _, S, D = x.shape
    PW = D // 2
    H = PW // HEAD_DIM
    cg = PW // N_GROUPS
    IN = w_in.shape[2] * N_CHIPS
    FF = w_ff1.shape[2] * N_CHIPS
    A_COLS = w_ada.shape[2]
    xi, yi, ci = lax.axis_index("x"), lax.axis_index("y"), lax.axis_index("c")
    chip = 2 * xi + yi
    dev = 2 * chip + ci
    c_arr = jnp.reshape(ci, (1,)).astype(jnp.int32)
    x2, tgt = x[0], loss_target[0]

    ws = [_W("w_in", "col", D, IN), _W("w_pool", "row", PW, cg), _W("w_a_up", "col", PW, D),
          _W("w_b_up", "col", PW, D), _W("w_o", "row", D, D), _W("w_ff1", "col", D, FF), _W("w_ff2", "row", FF, D)]
    w32 = [w_in[0], w_pool[0].reshape(cg, cg), w_a_up[0], w_b_up[0], w_o[0], w_ff1[0], w_ff2[0]]
    m32 = [m_w_in[0], m_w_pool[0].reshape(cg, cg), m_w_a_up[0], m_w_b_up[0], m_w_o[0], m_w_ff1[0], m_w_ff2[0]]
    v32 = [v_w_in[0], v_w_pool[0].reshape(cg, cg), v_w_a_up[0], v_w_b_up[0], v_w_o[0], v_w_ff1[0], v_w_ff2[0]]

    W_IN, W_POOL, W_A, W_B, W_O, W_FF1, W_FF2 = ws
    chip_arr = jnp.reshape(chip, (1,)).astype(jnp.int32)
    cc_arr = jnp.stack([ci, chip]).astype(jnp.int32)
    s_in, s_pool, s_a, s_b, s_o, s_ff1, s_ff2 = [_cast_into_full(w, a, chip_arr) for w, a in zip(ws, w32)]
    (win_f,) = _run_rider("gather_w_in", _ag_rider([W_IN], [s_in]))

    sc_row = _silu_rows(c)
    sc_all = _dev_allgather("gather_silu_c", sc_row.reshape(8, D // 8)).reshape(N_DEV, D)
    sc16 = jnp.concatenate([sc_all, jnp.zeros_like(sc_all)], axis=0)
    b_cols = lax.dynamic_slice(b_ada, (0, chip * A_COLS), (1, A_COLS))
    (mod_cols,) = _mm("mod_cols", [(sc16, w_ada[0])], M=2 * N_DEV, N=A_COLS, K=D, tm=16, tn=1024, tk=1024,
                      a_pro=lambda a: a.astype(BF16), b_pro=lambda b: b.astype(BF16),
                      extras=[(b_cols, "row", 0)], outs=[_tile_out(F32)], epi=lambda accs, ex: [accs[0] + ex[0]])
    mod_all = _dev_allgather("gather_mod", mod_cols[:N_DEV]).reshape(N_CHIPS, 2, N_DEV, A_COLS)
    mod_row = lax.dynamic_index_in_dim(mod_all[:, 0], dev, axis=1, keepdims=False).reshape(1, N_CHIPS * A_COLS)
    shift1, scale1, gate1, shift2, scale2, gate2 = [mod_row[:, i * D:(i + 1) * D] for i in range(6)]

    WIDE = dict(tm=2048, tn=512, tk=2048)
    DEEP = dict(tm=1024, tn=1024, tk=1024)
    h = _norm_mod("norm1_mod", x2, norm1_w, scale1, shift1)
    (proj,), ((wpool_f, wa_f, wb_f, wo_f),) = _mm(
        "in_proj", [(h, win_f)], M=S, N=IN, K=D, outs=[_tile_out(BF16)], epi=lambda accs, ex: [accs[0]], **WIDE,
        riders=[_ag_rider([W_POOL, W_A, W_B, W_O], [s_pool, s_a, s_b, s_o], n_ch=2)])
    pooled, pa = _pool_fwd(proj, wpool_f, pool_scale, S, PW)
    (att, attf), ((wff1_f,),) = _attn_fwd(proj, q_norm_w, k_norm_w, S, H, PW // HEAD_DIM,
                                          riders=[_ag_rider([W_FF1], [s_ff1])])

    def merge_epi(accs, ex):
        sa, sb = jax.nn.sigmoid(ex[0].astype(F32)), jax.nn.sigmoid(ex[1].astype(F32))
        return [sa * accs[0] + sb * accs[1], accs[0], accs[1]]

    (merged, ya, yb), (ff2_a,) = _mm("branch_up_merge", [(pa, wa_f), (att, wb_f)], M=S, N=D, K=PW,
                                     extras=[(proj, "tile", 4 * PW), (proj, "tile", 4 * PW + D)],
                                     outs=[_tile_out(BF16)] * 3, epi=merge_epi,
                                     riders=[_ag_rider([W_FF2], [s_ff2], chunks=(0, 1))])
    (x1, o), (ff2_b,) = _mm("out_proj", [(merged, wo_f)], M=S, N=D, K=D, extras=[(x2, "tile", 0), (gate1, "row", 0)],
                            outs=[_tile_out(F32), _tile_out(BF16)], epi=lambda accs, ex: [ex[0] + ex[1] * accs[0], accs[0]],
                            riders=[_ag_rider([W_FF2], ff2_a, chunks=(1, 2))], **WIDE)
    h2 = _norm_mod("norm2_mod", x1, norm2_w, scale2, shift2)
    (rl,), ((wff2_f,),) = _mm("ff1", [(h2, wff1_f)], M=S, N=FF, K=D, outs=[_tile_out(BF16)], **WIDE,
                              epi=lambda accs, ex: [jnp.maximum(accs[0], 0.0)],
                              riders=[_ag_rider([W_FF2], ff2_b, chunks=(2, 4))])

    def square(a):
        af = a.astype(F32)
        return (af * af).astype(BF16)

    def loss_epi(accs, ex):
        x1_t, tgt_t, g2 = ex
        f = accs[0]
        diff = (x1_t + g2 * f) - tgt_t
        dy = diff * (1.0 / D)
        return [dy, dy * g2, _colsum(dy * f), _colsum(diff * diff)]

    dy, df, dgate2_p, loss_p = _mm("ff2_loss", [(rl, wff2_f)], M=S, N=D, K=FF, a_pro=square, tm=1024, tn=1024, tk=512,
                                   extras=[(x1, "tile", 0), (tgt, "tile", 0), (gate2, "row", 0)],
                                   outs=[_tile_out(F32), _tile_out(BF16), _COLSUM, _COLSUM], epi=loss_epi)

    def behind(token, a):
        return lax.optimization_barrier((a, token))[0]

    def pair_sums(group, partials, got):
        return [_pair_sum(w, g, r, c_arr) for w, g, r in zip(group, partials, got)]

    def chip_sums(group, sums, from_chips):
        return [_chip_sum(w, p, q, cc_arr) for w, p, q in zip(group, sums, from_chips)]

    first = lambda accs, ex: [accs[0]]
    gmm = dict(ta=True, outs=[_tile_out(BF16)], epi=first, **WIDE)
    (g_ff2,) = _mm("grad_w_ff2", [(rl, df)], M=FF, N=D, K=S, a_pro=square, ta=True, tm=512, tn=2048, tk=2048,
                   outs=[_tile_out(BF16)], epi=first)
    flight, token = _split_start("pair_w_ff2_start", "pair", [W_FF2], [g_ff2])
    (dz1,) = _mm("d_ff_hidden", [(behind(token, df), wff2_f)], M=S, N=FF, K=D, tb=True, extras=[(rl, "tile", 0)],
                 outs=[_tile_out(BF16)], epi=lambda accs, ex: [accs[0] * (2.0 * ex[0].astype(F32))], **WIDE)
    sum_ff2 = pair_sums([W_FF2], *_split_wait("pair_w_ff2_wait", flight, after=[dz1]))
    chip_ff2, token = _split_start("chip_w_ff2_start", "chip", [W_FF2], sum_ff2)
    (g_ff1,) = _mm("grad_w_ff1", [(behind(token, h2), dz1)], M=D, N=FF, K=S, **gmm)
    flight, token = _split_start("pair_w_ff1_start", "pair", [W_FF1], [g_ff1])
    (dh2,) = _mm("d_h2", [(behind(token, dz1), wff1_f)], M=S, N=D, K=FF, tb=True, outs=[_tile_out(F32)], epi=first, **DEEP)
    sum_ff1 = pair_sums([W_FF1], *_split_wait("pair_w_ff1_wait", flight, after=[dh2]))
    chip_ff1, token = _split_start("chip_w_ff1_start", "chip", [W_FF1], sum_ff1)
    dx1, dshift2_p, dscale2_p, gn2_p, do, dgate1_p = _norm_mod_bwd("norm2_bwd", behind(token, dh2), x1, dy, norm2_w, scale2,
                                                                   gate_o=(o, gate1))
    (g_wo,) = _mm("grad_w_o", [(merged, do)], M=D, N=D, K=S, **gmm)

    def gate_epi(accs, ex):
        dm = accs[0]
        sa, sb = jax.nn.sigmoid(ex[0].astype(F32)), jax.nn.sigmoid(ex[1].astype(F32))
        ya_t, yb_t = ex[2].astype(F32), ex[3].astype(F32)
        return [dm * sa, dm * sb, dm * ya_t * (sa * (1.0 - sa)), dm * yb_t * (sb * (1.0 - sb))]

    dya, dyb, dga, dgb = _mm("d_merged", [(do, wo_f)], M=S, N=D, K=D, tb=True, tm=1024, tn=512, tk=2048,
                             extras=[(proj, "tile", 4 * PW), (proj, "tile", 4 * PW + D), (ya, "tile", 0), (yb, "tile", 0)],
                             outs=[_tile_out(BF16)] * 4, epi=gate_epi)
    (g_wa,) = _mm("grad_w_a_up", [(pa, dya)], M=PW, N=D, K=S, **gmm)
    (g_wb,) = _mm("grad_w_b_up", [(att, dyb)], M=PW, N=D, K=S, **gmm)
    (dpa,) = _mm("d_pool_out", [(dya, wa_f)], M=S, N=PW, K=D, tb=True, outs=[_tile_out(F32)], epi=first, **WIDE)
    mid = [W_A, W_B, W_O]
    flight, token = _split_start("pair_mid_start", "pair", mid, [g_wa, g_wb, g_wo])
    (datt,) = _mm("d_att", [(behind(token, dyb), wb_f)], M=S, N=PW, K=D, tb=True, outs=[_tile_out(BF16)], epi=first, **WIDE)
    sum_mid = pair_sums(mid, *_split_wait("pair_mid_wait", flight, after=[datt]))
    chip_mid, token = _split_start("chip_mid_start", "chip", mid, sum_mid)
    du, g_wpool4, gscale_p = _pool_bwd(behind(token, dpa), pooled, wpool_f, pool_scale, S, PW)
    dq, dk, dv, gq_p, gk_p = _attn_bwd(proj, datt, attf, q_norm_w, k_norm_w, S, H, PW // HEAD_DIM)
    dproj = jnp.concatenate([du, dq, dk, dv, dga, dgb], axis=1)
    early = [W_FF1, W_FF2]
    sum_ff1, q_ff1 = _split_wait("chip_w_ff1_wait", chip_ff1, after=[dq])
    sum_ff2, q_ff2 = _split_wait("chip_w_ff2_wait", chip_ff2, after=[dq])
    halves_early = chip_sums(early, sum_ff1 + sum_ff2, q_ff1 + q_ff2)
    (g_win,), (grads_early,) = _mm("grad_w_in", [(h, dproj)], M=D, N=IN, K=S, riders=[_sf_rider(early, halves_early)],
                                   **gmm)
    last = [W_IN, W_POOL]
    g_last = [g_win, g_wpool4.reshape(PW, cg)]
    sum_mid, q_mid = _split_wait("chip_mid_wait", chip_mid, after=[g_win])
    halves_mid = chip_sums(mid, sum_mid, q_mid)
    (dh,), (got_last, grads_mid) = _mm("d_h", [(dproj, win_f)], M=S, N=D, K=IN, tb=True, outs=[_tile_out(F32)], epi=first,
                                       riders=[_px_rider(last, g_last), _sf_rider(mid, halves_mid)], **DEEP)
    sum_last = pair_sums(last, g_last, got_last)
    grad_x, dshift1_p, dscale1_p, gn1_p = _norm_mod_bwd("norm1_bwd", dh, x2, dx1, norm1_w, scale1)

    parts = [dshift1_p, dscale1_p, dgate1_p, dshift2_p, dscale2_p, dgate2_p, gn1_p, gn2_p,
             gscale_p.reshape(1, 1, PW), gq_p, gk_p]
    widths = [D] * 8 + [PW, HEAD_DIM, HEAD_DIM]
    used = sum(widths)
    P = -(-(used + 128) // 1024) * 1024
    packed = _pack_partials(parts + [loss_p], widths, P)
    gathered = _dev_allgather("gather_vector_grads", packed.reshape(8, P // 8)).reshape(N_DEV, P)
    sum_last, gathered = lax.optimization_barrier((sum_last, gathered))
    chip_last, token = _split_start("chip_last_start", "chip", last, sum_last)
    small = [(b_ada, m_b_ada, v_b_ada), (norm1_w, m_norm1_w, v_norm1_w), (norm2_w, m_norm2_w, v_norm2_w),
             (pool_scale, m_pool_scale, v_pool_scale), (q_norm_w, m_q_norm_w, v_q_norm_w),
             (k_norm_w, m_k_norm_w, v_k_norm_w)]
    offsets = [(0, 6 * D), (6 * D, D), (7 * D, D), (8 * D, PW), (8 * D + PW, HEAD_DIM), (8 * D + PW + HEAD_DIM, HEAD_DIM)]
    su = _small_update(gathered, offsets, small, used)
    (g_b, d_b, nm_b, nv_b, g_n1, d_n1, nm_n1, nv_n1, g_n2, d_n2, nm_n2, nv_n2, g_ps, d_ps, nm_ps, nv_ps,
     g_qn, d_qn, nm_qn, nv_qn, g_kn, d_kn, nm_kn, nv_kn, loss_sum) = su
    dmod_sh = lax.dynamic_slice(gathered, (0, chip * A_COLS), (N_DEV, A_COLS))
    dmod_sh, token = lax.optimization_barrier((dmod_sh, token))
    g_ada, d_ada, nm_ada, nv_ada = _ada_update(sc_all.T, dmod_sh, w_ada[0], m_w_ada[0], v_w_ada[0])

    upd_done = [_adamw("adamw_" + w.name, a, g, m, v, after=token)
                for w, a, g, m, v in zip(ws[2:], w32[2:], list(grads_mid) + list(grads_early), m32[2:], v32[2:])]

    sum_last, q_last = _split_wait("chip_last_wait", chip_last, after=[nv_ada] + [u[3] for u in upd_done])
    halves_last = chip_sums(last, sum_last, q_last)
    filled = _run_rider("grad_sibling_fill", _sf_rider(last, halves_last))
    upd = [_adamw("adamw_" + w.name, a, g, m, v) for w, a, g, m, v in zip(ws[:2], w32[:2], filled, m32[:2], v32[:2])]
    upd += upd_done

    loss = (0.5 / D) * loss_sum[0, 0]

    def up(a):
        return a[None]

    def pool4(a):
        return a.reshape(1, N_GROUPS, cg // N_CHIPS, cg)

    (gr_win, d_win, nm_win, nv_win), (gr_wp, d_wp, nm_wp, nv_wp), (gr_wa, d_wa, nm_wa, nv_wa), \
        (gr_wb, d_wb, nm_wb, nv_wb), (gr_wo, d_wo, nm_wo, nv_wo), (gr_f1, d_f1, nm_f1, nv_f1), \
        (gr_f2, d_f2, nm_f2, nv_f2) = upd
    return (
        loss, grad_x[None],
        up(g_ada), g_b, g_n1, up(gr_win), g_qn, g_kn, pool4(gr_wp), g_ps, up(gr_wa), up(gr_wb), up(gr_wo), g_n2,
        up(gr_f1), up(gr_f2),
        up(d_ada), d_b, d_n1, up(d_win), d_qn, d_kn, pool4(d_wp), d_ps, up(d_wa), up(d_wb), up(d_wo), d_n2,
        up(d_f1), up(d_f2),
        up(nm_ada), nm_b, nm_n1, up(nm_win), nm_qn, nm_kn, pool4(nm_wp), nm_ps, up(nm_wa), up(nm_wb), up(nm_wo), nm_n2,
        up(nm_f1), up(nm_f2),
        up(nv_ada), nv_b, nv_n1, up(nv_win), nv_qn, nv_kn, pool4(nv_wp), nv_ps, up(nv_wa), up(nv_wb), up(nv_wo), nv_n2,
        up(nv_f1), up(nv_f2),
    )
```

```python
import functools
import math

import jax
import jax.numpy as jnp
from jax import lax
from jax.experimental import pallas as pl
from jax.experimental.pallas import tpu as pltpu

F32 = jnp.float32
BF16 = jnp.bfloat16
MESH = pl.DeviceIdType.MESH
ANY = pl.BlockSpec(memory_space=pl.ANY)

EPS = 1e-6
HEAD_DIM = 128
POOL_WINDOWS = (2, 4, 8, 16)
N_GROUPS = len(POOL_WINDOWS)
N_CHIPS = 4
N_DEV = 8
ADAM_LR, ADAM_B1, ADAM_B2, ADAM_EPS, ADAM_WD, ADAM_STEP = 0.001, 0.9, 0.999, 1e-08, 0.01, 10
VMEM_LIMIT_V7X = 56 * 1024 * 1024
ATT_T = 256
ATT_GROUP = 4
POOL_T = 256


def _pcall(body, **kw):
    return pl.pallas_call(body, **kw)


def _params(sem=None):
    return pltpu.CompilerParams(dimension_semantics=sem, vmem_limit_bytes=VMEM_LIMIT_V7X)


def _tile(n, pref):
    if n <= pref:
        return n
    t = pref
    while n % t:
        t //= 2
    return t


class _Rider:
    def __init__(self, arrays, out_shape, sems, start, finish, aliases=None, steps=()):
        self.arrays, self.out_shape, self.sems = list(arrays), list(out_shape), list(sems)
        self.start, self.finish, self.aliases, self.steps = start, finish, aliases or {}, list(steps)


def _ride(name, body, riders, arrays, *, grid, in_specs, out_specs, out_shape, scratch_shapes, sem):
    n_in, n_out, n_scr = len(arrays), len(out_shape), len(scratch_shapes)
    r_arrays = [a for r in riders for a in r.arrays]
    r_outs = [o for r in riders for o in r.out_shape]
    r_sems = [s for r in riders for s in r.sems]
    n_hooks = max([len(r.steps) for r in riders], default=0)
    total = math.prod(grid)
    aliases, off_i, off_o = {}, n_in, n_out
    for r in riders:
        for a, o in r.aliases.items():
            aliases[off_i + a] = off_o + o
        off_i += len(r.arrays)
        off_o += len(r.out_shape)

    def full(*refs):
        p = 0
        groups = []
        for n in (n_in, len(r_arrays), n_out, len(r_outs), n_scr, len(r_sems)):
            groups.append(refs[p:p + n])
            p += n
        ins, rin, outs, rout, scr, rsem = groups

        def each(what):
            a = o = s = 0
            for r in riders:
                fn = what(r)
                if fn is not None:
                    fn(rin[a:a + len(r.arrays)], rout[o:o + len(r.out_shape)], rsem[s:s + len(r.sems)])
                a, o, s = a + len(r.arrays), o + len(r.out_shape), s + len(r.sems)

        if riders:
            lin = 0
            for d, g in enumerate(grid):
                lin = lin * g + pl.program_id(d)
            pl.when(lin == 0)(lambda: each(lambda r: r.start))
            for t in range(n_hooks):
                pl.when(lin == min(total - 1, ((t + 1) * total) // n_hooks))(
                    lambda t=t: each(lambda r: r.steps[t] if t < len(r.steps) else None))
        body(*ins, *outs, *scr)
        if riders:
            pl.when(lin == total - 1)(lambda: each(lambda r: r.finish))

    res = _pcall(
        full, name=name, grid=grid, in_specs=list(in_specs) + [ANY] * len(r_arrays),
        out_specs=list(out_specs) + [ANY] * len(r_outs), out_shape=list(out_shape) + r_outs,
        scratch_shapes=list(scratch_shapes) + r_sems, input_output_aliases=aliases,
        compiler_params=_params(("arbitrary",) * len(grid) if riders else sem),
    )(*arrays, *r_arrays)
    if not riders:
        return res
    main, rest, per = res[:n_out], res[n_out:], []
    for r in riders:
        per.append(rest[:len(r.out_shape)])
        rest = rest[len(r.out_shape):]
    return main, per


def _run_rider(name, rider):
    def body(*refs):
        n_a, n_o = len(rider.arrays), len(rider.out_shape)
        ins, outs, sems = refs[:n_a], refs[n_a:n_a + n_o], refs[n_a + n_o:]
        for fn in [rider.start] + rider.steps + [rider.finish]:
            fn(ins, outs, sems)

    return _pcall(body, name=name, out_shape=rider.out_shape, in_specs=[ANY] * len(rider.arrays),
                  out_specs=[ANY] * len(rider.out_shape), scratch_shapes=rider.sems,
                  input_output_aliases=rider.aliases)(*rider.arrays)


def _mm(name, pairs, *, M, N, K, ta=False, tb=False, tm=512, tn=1024, tk=1024,
        a_pro=None, b_pro=None, extras=(), outs, epi, riders=(), b_noff=0):
    tm, tn, tk = _tile(M, tm), _tile(N, tn), _tile(K, tk)
    n_i, n_j, n_k = M // tm, N // tn, K // tk
    n_p, n_e = len(pairs), len(extras)
    arrays, in_specs = [], []
    for a, _ in pairs:
        arrays.append(a)
        in_specs.append(pl.BlockSpec((tk, tm), lambda i, j, k: (k, i)) if ta
                        else pl.BlockSpec((tm, tk), lambda i, j, k: (i, k)))
    for _, b in pairs:
        arrays.append(b)
        in_specs.append(pl.BlockSpec((tn, tk), lambda i, j, k: (j + b_noff // tn, k)) if tb
                        else pl.BlockSpec((tk, tn), lambda i, j, k: (k, j + b_noff // tn)))
    for arr, kind, off in extras:
        ob = off // tn
        assert off % tn == 0
        arrays.append(arr)
        if kind == "tile":
            in_specs.append(pl.BlockSpec((tm, tn), lambda i, j, k, ob=ob: (i, j + ob)))
        else:
            in_specs.append(pl.BlockSpec((1, tn), lambda i, j, k, ob=ob: (0, j + ob)))
    out_shape, out_specs = [], []
    for o in outs:
        if o["kind"] == "tile":
            out_shape.append(jax.ShapeDtypeStruct((M, N), o["dtype"]))
            out_specs.append(pl.BlockSpec((tm, tn), lambda i, j, k: (i, j)))
        else:
            out_shape.append(jax.ShapeDtypeStruct((n_i, 1, N), F32))
            out_specs.append(pl.BlockSpec((1, 1, tn), lambda i, j, k: (i, 0, j)))
    dims = (((0 if ta else 1,), (1 if tb else 0,)), ((), ()))

    def body(*refs):
        a_refs, b_refs = refs[:n_p], refs[n_p:2 * n_p]
        e_refs = refs[2 * n_p:2 * n_p + n_e]
        o_refs = refs[2 * n_p + n_e:2 * n_p + n_e + len(outs)]
        acc_refs = refs[2 * n_p + n_e + len(outs):]

        def product(p):
            a, b = a_refs[p][...], b_refs[p][...]
            if a_pro is not None:
                a = a_pro(a)
            if b_pro is not None:
                b = b_pro(b)
            return lax.dot_general(a, b, dims, preferred_element_type=F32)

        def write(accs):
            vals = epi(accs, [e[...] for e in e_refs])
            for o, o_ref, val in zip(outs, o_refs, vals):
                if o["kind"] == "tile":
                    o_ref[...] = val.astype(o_ref.dtype)
                else:
                    o_ref[0] = val

        if n_k == 1:
            write([product(p) for p in range(n_p)])
            return
        k = pl.program_id(2)

        @pl.when(k == 0)
        def _():
            for acc in acc_refs:
                acc[...] = jnp.zeros_like(acc)

        for p in range(n_p):
            acc_refs[p][...] += product(p)

        pl.when(k == n_k - 1)(lambda: write([acc[...] for acc in acc_refs]))

    return _ride(name, body, riders, arrays, grid=(n_i, n_j, n_k), in_specs=in_specs, out_specs=out_specs,
                 out_shape=out_shape, scratch_shapes=[pltpu.VMEM((tm, tn), F32) for _ in pairs] if n_k > 1 else [],
                 sem=("parallel", "parallel", "arbitrary"))


def _tile_out(dtype):
    return {"kind": "tile", "dtype": dtype}


_COLSUM = {"kind": "colsum"}


def _colsum(v):
    return jnp.sum(v, axis=0, keepdims=True)


def _norm_mod(name, x, norm_w, scale, shift):
    S, D = x.shape
    tr = _tile(S, 256)

    def body(x_ref, nw_ref, sc_ref, sh_ref, h_ref):
        xv = x_ref[...]
        r = lax.rsqrt(jnp.mean(xv * xv, axis=-1, keepdims=True) + EPS)
        h_ref[...] = ((xv * r * nw_ref[...]) * (1.0 + sc_ref[...]) + sh_ref[...]).astype(BF16)

    row = pl.BlockSpec((1, D), lambda i: (0, 0))
    til = pl.BlockSpec((tr, D), lambda i: (i, 0))
    return _pcall(body, name=name, grid=(S // tr,), in_specs=[til, row, row, row], out_specs=til,
                  out_shape=jax.ShapeDtypeStruct((S, D), BF16), compiler_params=_params(("parallel",)))(
                      x, norm_w, scale, shift)


def _norm_mod_bwd(name, dh, x, dres, norm_w, scale, gate_o=None):
    S, D = x.shape
    tr = _tile(S, 256)
    n_r = S // tr
    with_gate = gate_o is not None
    dh = list(dh) if isinstance(dh, (list, tuple)) else [dh]
    n_dh = len(dh)

    def body(*refs):
        dh_refs, refs = refs[:n_dh], refs[n_dh:]
        if with_gate:
            x_ref, dres_ref, nw_ref, sc_ref, o_ref, g_ref, dx_ref, p1, p2, p3, do_ref, p4 = refs
        else:
            x_ref, dres_ref, nw_ref, sc_ref, dx_ref, p1, p2, p3 = refs
        dhv = dh_refs[0][...] if n_dh == 1 else jnp.concatenate([r[...] for r in dh_refs], axis=1)
        xv, nw = x_ref[...], nw_ref[...]
        r = lax.rsqrt(jnp.mean(xv * xv, axis=-1, keepdims=True) + EPS)
        xh = xv * r
        p1[0] = _colsum(dhv)
        p2[0] = _colsum(dhv * (xh * nw))
        dn = dhv * (1.0 + sc_ref[...])
        p3[0] = _colsum(dn * xh)
        dxh = dn * nw
        dx = dres_ref[...] + r * (dxh - xh * jnp.mean(dxh * xh, axis=-1, keepdims=True))
        dx_ref[...] = dx
        if with_gate:
            do_ref[...] = (dx * g_ref[...]).astype(BF16)
            p4[0] = _colsum(dx * o_ref[...].astype(F32))

    row = pl.BlockSpec((1, D), lambda i: (0, 0))
    til = pl.BlockSpec((tr, D), lambda i: (i, 0))
    part = pl.BlockSpec((1, 1, D), lambda i: (i, 0, 0))
    part_shape = jax.ShapeDtypeStruct((n_r, 1, D), F32)
    in_specs = [pl.BlockSpec((tr, D // n_dh), lambda i: (i, 0))] * n_dh + [til, til, row, row]
    arrays = dh + [x, dres, norm_w, scale]
    out_specs = [til, part, part, part]
    out_shape = [jax.ShapeDtypeStruct((S, D), F32), part_shape, part_shape, part_shape]
    if with_gate:
        in_specs += [til, row]
        arrays += list(gate_o)
        out_specs += [til, part]
        out_shape += [jax.ShapeDtypeStruct((S, D), BF16), part_shape]
    return _pcall(body, name=name, grid=(n_r,), in_specs=in_specs, out_specs=out_specs, out_shape=out_shape,
                  compiler_params=_params(("parallel",)))(*arrays)


def _pool_w_specs(rows, cg):
    return [pl.BlockSpec((rows, cg), lambda g, j=j: (N_GROUPS * j + g, 0)) for j in range(N_CHIPS)]


def _pool_fwd(proj, wp_full, pool_scale, S, PW):
    cg = PW // N_GROUPS
    rows = cg // N_CHIPS
    T = _tile(S, POOL_T)
    n_t = S // T

    def body(u_ref, w0, w1, w2, w3, ps_ref, pooled_ref, pa_ref):
        g = pl.program_id(0)
        win = jnp.left_shift(2, g)
        w = jnp.concatenate([w0[...], w1[...], w2[...], w3[...]], axis=0)
        t_i = lax.broadcasted_iota(jnp.int32, (T, T), 0)
        j_i = lax.broadcasted_iota(jnp.int32, (T, T), 1)
        b_cur = ((j_i <= t_i) & (j_i > t_i - win)).astype(BF16)
        b_prev = (j_i - T > t_i - win).astype(BF16)
        row = lax.broadcasted_iota(jnp.int32, (T, 1), 0)
        for r in range(n_t):
            cur = u_ref[r * T:(r + 1) * T, :]
            ws = jnp.dot(b_cur, cur, preferred_element_type=F32)
            if r > 0:
                ws += jnp.dot(b_prev, u_ref[(r - 1) * T:r * T, :], preferred_element_type=F32)
            count = jnp.minimum(row + (r * T + 1), win).astype(F32)
            pooled = (ws / count - cur.astype(F32)).astype(BF16)
            pooled_ref[r * T:(r + 1) * T, :] = pooled
            mixed = jnp.dot(pooled, w, preferred_element_type=F32)
            pa_ref[r * T:(r + 1) * T, :] = (mixed * ps_ref[...]).astype(BF16)

    col = pl.BlockSpec((S, cg), lambda g: (0, g))
    return _pcall(
        body, name="pool_fwd", grid=(N_GROUPS,),
        in_specs=[col] + _pool_w_specs(rows, cg) + [pl.BlockSpec((1, cg), lambda g: (0, g))],
        out_specs=[col, col],
        out_shape=[jax.ShapeDtypeStruct((S, PW), BF16), jax.ShapeDtypeStruct((S, PW), BF16)],
        compiler_params=_params(("parallel",)),
    )(proj, wp_full, wp_full, wp_full, wp_full, pool_scale)


def _pool_bwd(dpa, pooled, wp_full, pool_scale, S, PW):
    cg = PW // N_GROUPS
    rows = cg // N_CHIPS
    T = _tile(S, POOL_T)
    n_t = S // T

    def body(dpa_ref, pooled_ref, w0, w1, w2, w3, ps_ref, du_ref, gw_ref, gs_ref, dp_s, dpc_s, dmx_s):
        g = pl.program_id(0)
        win = jnp.left_shift(2, g)
        w = jnp.concatenate([w0[...], w1[...], w2[...], w3[...]], axis=0)
        row = lax.broadcasted_iota(jnp.int32, (T, 1), 0)
        gs = jnp.zeros((1, cg), F32)
        for r in range(n_t):
            sl = slice(r * T, (r + 1) * T)
            mixed = jnp.dot(pooled_ref[sl, :], w, preferred_element_type=F32)
            dpa_t = dpa_ref[sl, :]
            gs += _colsum(dpa_t * mixed)
            dmx = (dpa_t * ps_ref[...]).astype(BF16)
            dmx_s[sl, :] = dmx
            dpo = lax.dot_general(dmx, w, (((1,), (1,)), ((), ())), preferred_element_type=F32)
            dp_s[sl, :] = dpo
            count = jnp.minimum(row + (r * T + 1), win).astype(F32)
            dpc_s[sl, :] = (dpo / count).astype(BF16)
        gs_ref[...] = gs
        gw = lax.dot_general(pooled_ref[...], dmx_s[...], (((0,), (0,)), ((), ())), preferred_element_type=F32)
        for j in range(N_CHIPS):
            gw_ref[j, 0] = gw[j * rows:(j + 1) * rows, :].astype(BF16)
        j_i = lax.broadcasted_iota(jnp.int32, (T, T), 0)
        t_i = lax.broadcasted_iota(jnp.int32, (T, T), 1)
        b_cur = ((t_i >= j_i) & (t_i < j_i + win)).astype(BF16)
        b_next = (t_i + T < j_i + win).astype(BF16)
        for r in range(n_t):
            sl = slice(r * T, (r + 1) * T)
            acc = jnp.dot(b_cur, dpc_s[sl, :], preferred_element_type=F32)
            if r + 1 < n_t:
                acc += jnp.dot(b_next, dpc_s[(r + 1) * T:(r + 2) * T, :], preferred_element_type=F32)
            du_ref[sl, :] = (acc - dp_s[sl, :]).astype(BF16)

    col = pl.BlockSpec((S, cg), lambda g: (0, g))
    return _pcall(
        body, name="pool_bwd", grid=(N_GROUPS,),
        in_specs=[col, col] + _pool_w_specs(rows, cg) + [pl.BlockSpec((1, cg), lambda g: (0, g))],
        out_specs=[col, pl.BlockSpec((N_CHIPS, 1, rows, cg), lambda g: (0, g, 0, 0)),
                   pl.BlockSpec((1, cg), lambda g: (0, g))],
        out_shape=[jax.ShapeDtypeStruct((S, PW), BF16),
                   jax.ShapeDtypeStruct((N_CHIPS, N_GROUPS, rows, cg), BF16),
                   jax.ShapeDtypeStruct((1, PW), F32)],
        scratch_shapes=[pltpu.VMEM((S, cg), F32), pltpu.VMEM((S, cg), BF16), pltpu.VMEM((S, cg), BF16)],
        compiler_params=_params(("parallel",)),
    )(dpa, pooled, wp_full, wp_full, wp_full, wp_full, pool_scale)


_NT = (((1,), (1,)), ((), ()))
_TN = (((0,), (0,)), ((), ()))


def _split_dot(v, tri):
    hi = v.astype(BF16)
    lo = (v - hi.astype(F32)).astype(BF16)
    return jnp.dot(hi, tri, preferred_element_type=F32) + jnp.dot(lo, tri, preferred_element_type=F32)


LOG2E = 1.4426950408889634
QK_SCALE = 1.0 / math.sqrt(HEAD_DIM)


def _sb_scores(q2_i, k_j, tri_l, masked):
    tq, tk = q2_i.shape[0], k_j.shape[0]
    s = lax.dot_general(q2_i, k_j, _NT, preferred_element_type=F32)
    lp = jnp.log(1.0 + jnp.exp2(-jnp.abs(s))) * LOG2E
    lb = jnp.minimum(s, 0.0) - lp
    l = lb - s
    mask = None
    if masked:
        mask = lax.broadcasted_iota(jnp.int32, (tq, tk), 0) > lax.broadcasted_iota(jnp.int32, (tq, tk), 1)
        l = jnp.where(mask, l, 0.0)
    return l, lb, lb + _split_dot(l, tri_l), mask


def _sb_weights(t, carry_l, mask):
    a = jnp.exp2(t + carry_l)
    return a if mask is None else jnp.where(mask, a, 0.0)


def _rowsum(v):
    return jnp.sum(v, axis=1, keepdims=True)


def _qk_norm(x_ref, w_ref):
    xv = x_ref[...].astype(F32)
    r = lax.rsqrt(jnp.mean(xv * xv, axis=-1, keepdims=True) + EPS)
    return xv * r, r


def _attn_fwd(proj, q_norm_w, k_norm_w, S, H, q_off, riders=()):
    t = _tile(S, ATT_T)
    n_q = S // t

    def body(q_ref, k_ref, v_ref, qw_ref, kw_ref, att_ref, attf_ref, qn_s, kn_s):
        qh, _ = _qk_norm(q_ref, qw_ref)
        qn_s[...] = (qh * qw_ref[...] * (QK_SCALE * LOG2E)).astype(BF16)
        kh, _ = _qk_norm(k_ref, kw_ref)
        kn_s[...] = (kh * kw_ref[...]).astype(BF16)
        tri_l = (lax.broadcasted_iota(jnp.int32, (t, t), 0) > lax.broadcasted_iota(jnp.int32, (t, t), 1)).astype(BF16)

        def rows(j):
            return pl.ds(pl.multiple_of(j * t, t), t)

        def q_step(i, _):
            q_i = qn_s[rows(i), :]

            def av(a, j):
                return jnp.dot(a.astype(BF16), v_ref[rows(j), :], preferred_element_type=F32)

            l, _, tt, mask = _sb_scores(q_i, kn_s[rows(i), :], tri_l, True)
            acc = av(_sb_weights(tt, 0.0, mask), i)
            carry = _rowsum(l)

            def single(_, c):
                carry, acc = c
                l, _, tt, _ = _sb_scores(q_i, kn_s[rows(i - 1), :], tri_l, False)
                return carry + _rowsum(l), acc + av(_sb_weights(tt, carry, None), i - 1)

            carry, acc = lax.fori_loop(0, i % 2, single, (carry, acc))
            top = i - 1 - i % 2

            def pair(p, c):
                carry, acc = c
                j0 = top - 2 * p
                l0, _, t0, _ = _sb_scores(q_i, kn_s[rows(j0), :], tri_l, False)
                l1, _, t1, _ = _sb_scores(q_i, kn_s[rows(j0 - 1), :], tri_l, False)
                mid = carry + _rowsum(l0)
                acc = acc + av(_sb_weights(t0, carry, None), j0) + av(_sb_weights(t1, mid, None), j0 - 1)
                return mid + _rowsum(l1), acc

            _, acc = lax.fori_loop(0, i // 2, pair, (carry, acc))
            att_ref[rows(i), :] = acc.astype(BF16)
            attf_ref[rows(i), :] = acc
            return 0

        lax.fori_loop(0, n_q, q_step, 0)

    def col(off):
        return pl.BlockSpec((S, HEAD_DIM), lambda h, off=off: (0, off + h))

    wspec = pl.BlockSpec((1, HEAD_DIM), lambda h: (0, 0))
    return _ride(
        "attn_fwd", body, riders, [proj, proj, proj, q_norm_w, k_norm_w], grid=(H,),
        in_specs=[col(q_off), col(q_off + H), col(q_off + 2 * H), wspec, wspec],
        out_specs=[col(0), col(0)],
        out_shape=[jax.ShapeDtypeStruct((S, H * HEAD_DIM), BF16), jax.ShapeDtypeStruct((S, H * HEAD_DIM), F32)],
        scratch_shapes=[pltpu.VMEM((S, HEAD_DIM), BF16), pltpu.VMEM((S, HEAD_DIM), BF16)],
        sem=("parallel",))


def _attn_bwd(proj, datt, attf, q_norm_w, k_norm_w, S, H, q_off, riders=()):
    t = _tile(S, ATT_T)
    n_q = S // t

    def body(q_ref, k_ref, v_ref, do_ref, o_ref, qw_ref, kw_ref, dq_ref, dk_ref, dv_ref, gq_ref, gk_ref,
             qn_s, kn_s, qz_s, kz_s, dk_s, dv_s, gq_s):
        qw, kw = qw_ref[...], kw_ref[...]
        qh, _ = _qk_norm(q_ref, qw_ref)
        qn_s[...] = (qh * qw * (QK_SCALE * LOG2E)).astype(BF16)
        qz_s[...] = (qh * qw * QK_SCALE).astype(BF16)
        kh, _ = _qk_norm(k_ref, kw_ref)
        kn_s[...] = (kh * kw).astype(BF16)
        kz_s[...] = (kh * kw * QK_SCALE).astype(BF16)
        dk_s[...] = jnp.zeros_like(dk_s)
        dv_s[...] = jnp.zeros_like(dv_s)
        gq_s[...] = jnp.zeros_like(gq_s)
        r_i = lax.broadcasted_iota(jnp.int32, (t, t), 0)
        c_i = lax.broadcasted_iota(jnp.int32, (t, t), 1)
        tri_l = (r_i > c_i).astype(BF16)
        tri_e = (r_i >= c_i).astype(BF16)

        def rows(j):
            return pl.ds(pl.multiple_of(j * t, t), t)

        def q_step(i, _):
            q_i = qn_s[rows(i), :]
            do_i = do_ref[rows(i), :]
            d_i = _rowsum(do_i.astype(F32) * o_ref[rows(i), :])

            def scores(j, masked):
                l, lb, tt, mask = _sb_scores(q_i, kn_s[rows(j), :], tri_l, masked)
                da = lax.dot_general(do_i, v_ref[rows(j), :], _NT, preferred_element_type=F32)
                return l, lb, tt, mask, da

            def grads(j, sc, carry_l, carry_e, dq_acc):
                l, lb, tt, mask, da = sc
                a_bf = _sb_weights(tt, carry_l, mask).astype(BF16)
                e = da * a_bf.astype(F32)
                p = (d_i - carry_e) - _split_dot(e, tri_e)
                dz = e - jnp.exp2(lb) * (e + p)
                if mask is not None:
                    dz = jnp.where(mask, dz, 0.0)
                dz = dz.astype(BF16)
                dk_s[rows(j), :] += lax.dot_general(dz, qz_s[rows(i), :], _TN, preferred_element_type=F32)
                dv_s[rows(j), :] += lax.dot_general(a_bf, do_i, _TN, preferred_element_type=F32)
                return (carry_l + _rowsum(l), carry_e + _rowsum(e),
                        dq_acc + jnp.dot(dz, kz_s[rows(j), :], preferred_element_type=F32))

            zero = jnp.zeros((t, 1), F32)
            first = (zero, zero, jnp.zeros((t, HEAD_DIM), F32))

            def group(js, diagonal_first, c):
                scs = [scores(j, diagonal_first and n == 0) for n, j in enumerate(js)]
                for j, sc in zip(js, scs):
                    c = grads(j, sc, *c)
                return c

            n_first = i % ATT_GROUP
            c = lax.switch(n_first, [functools.partial(group, [i - u for u in range(n + 1)], True, first)
                                     for n in range(ATT_GROUP)])
            top = i - 1 - n_first

            def whole(p, c):
                j0 = top - ATT_GROUP * p
                return group([j0 - u for u in range(ATT_GROUP)], False, c)

            _, _, dqn = lax.fori_loop(0, (i - n_first) // ATT_GROUP, whole, c)
            qv = q_ref[rows(i), :].astype(F32)
            r = lax.rsqrt(jnp.mean(qv * qv, axis=-1, keepdims=True) + EPS)
            xh = qv * r
            gq_s[...] += _colsum(dqn * xh)
            dxh = dqn * qw
            dq_ref[rows(i), :] = (r * (dxh - xh * jnp.mean(dxh * xh, axis=-1, keepdims=True))).astype(BF16)
            return 0

        lax.fori_loop(0, n_q, q_step, 0)
        gq_ref[0] = gq_s[...]
        kh, rk = _qk_norm(k_ref, kw_ref)
        dkn = dk_s[...]
        gk_ref[0] = _colsum(dkn * kh)
        dxh = dkn * kw
        dk_ref[...] = (rk * (dxh - kh * jnp.mean(dxh * kh, axis=-1, keepdims=True))).astype(BF16)
        dv_ref[...] = dv_s[...].astype(BF16)

    def col(off):
        return pl.BlockSpec((S, HEAD_DIM), lambda h, off=off: (0, off + h))

    wspec = pl.BlockSpec((1, HEAD_DIM), lambda h: (0, 0))
    gspec = pl.BlockSpec((1, 1, HEAD_DIM), lambda h: (h, 0, 0))
    act = jax.ShapeDtypeStruct((S, H * HEAD_DIM), BF16)
    gsh = jax.ShapeDtypeStruct((H, 1, HEAD_DIM), F32)
    return _ride(
        "attn_bwd", body, riders, [proj, proj, proj, datt, attf, q_norm_w, k_norm_w], grid=(H,),
        in_specs=[col(q_off), col(q_off + H), col(q_off + 2 * H), col(0), col(0), wspec, wspec],
        out_specs=[col(0), col(0), col(0), gspec, gspec],
        out_shape=[act, act, act, gsh, gsh],
        scratch_shapes=[pltpu.VMEM((S, HEAD_DIM), BF16)] * 4 + [pltpu.VMEM((S, HEAD_DIM), F32)] * 2
        + [pltpu.VMEM((1, HEAD_DIM), F32)],
        sem=("parallel",))


def _place():
    x, y, c = lax.axis_index("x"), lax.axis_index("y"), lax.axis_index("c")
    chips = [(1 - x, y), (x, 1 - y), (1 - x, 1 - y)]
    return x, y, c, chips


def _dev_allgather(name, v):
    m_per, n = v.shape

    def body(x_ref, out_ref, send_sems, recv_sems, local_sem):
        x, y, c, chips = _place()
        me, sibling = (x, y, c), (x, y, 1 - c)

        def rows(px, py, pc):
            return out_ref.at[pl.ds((4 * px + 2 * py + pc) * m_per, m_per), :]

        def copy(k, block, to, src=None):
            return pltpu.make_async_remote_copy(
                src_ref=rows(*block) if src is None else src, dst_ref=rows(*block),
                send_sem=send_sems.at[k], recv_sem=recv_sems.at[k], device_id=to, device_id_type=MESH)

        mine = pltpu.make_async_copy(x_ref, rows(*me), local_sem)
        mine.start()
        first = [copy(0, me, sibling, src=x_ref)]
        first += [copy(1 + j, me, (*chip, c), src=x_ref) for j, chip in enumerate(chips)]
        for cp in first:
            cp.start()
        passed = [copy(4 + j, (*chip, c), sibling) for j, chip in enumerate(chips)]
        for j, chip in enumerate(chips):
            copy(1 + j, (*chip, c), me).wait_recv()
            passed[j].start()
        copy(0, sibling, me).wait_recv()
        for j, chip in enumerate(chips):
            copy(4 + j, (*chip, 1 - c), me).wait_recv()
        for cp in first + passed:
            cp.wait_send()
        mine.wait()

    return _pcall(
        body, name=name, out_shape=jax.ShapeDtypeStruct((N_DEV * m_per, n), v.dtype),
        in_specs=[pl.BlockSpec(memory_space=pltpu.VMEM)], out_specs=pl.BlockSpec(memory_space=pltpu.VMEM),
        scratch_shapes=[pltpu.SemaphoreType.DMA((7,)), pltpu.SemaphoreType.DMA((7,)), pltpu.SemaphoreType.DMA],
        compiler_params=pltpu.CompilerParams(vmem_limit_bytes=VMEM_LIMIT_V7X),
    )(v)


class _W:
    def __init__(self, name, kind, R, C):
        self.name, self.kind, self.R, self.C = name, kind, R, C

    @property
    def shard_shape(self):
        return (self.R, self.C // N_CHIPS) if self.kind == "col" else (self.R // N_CHIPS, self.C)

    @property
    def half_rows(self):
        return self.shard_shape[0] // 2

    def shard_half(self, ref, half):
        return ref.at[pl.ds(half * self.half_rows, self.half_rows), :]

    def region(self, full_ref, chip, half):
        hr = self.half_rows
        if self.kind == "col":
            cw = self.C // N_CHIPS
            return full_ref.at[pl.ds(half * hr, hr), pl.ds(chip * cw, cw)]
        return full_ref.at[pl.ds(chip * (2 * hr) + half * hr, hr), :]

    def region_both(self, full_ref, chip):
        hr = self.half_rows
        if self.kind == "col":
            cw = self.C // N_CHIPS
            return full_ref.at[:, pl.ds(chip * cw, cw)]
        return full_ref.at[pl.ds(chip * (2 * hr), 2 * hr), :]


def _ag_rider(ws, fulls, n_ch=4, chunks=None):
    n_w = len(ws)
    lo, hi = chunks or (0, n_ch)
    per = 6

    def parts(full, sems):
        send_sems, recv_sems = sems
        x, y, c, _ = _place()
        xn, yn, dg = (1 - x, y), (x, 1 - y), (1 - x, 1 - y)
        via = (x + (1 - c) * (1 - 2 * x), y + c * (1 - 2 * y))
        to = (x + c * (1 - 2 * x), y + (1 - c) * (1 - 2 * y))

        def reg(i, chip, half, t):
            nr = ws[i].half_rows // n_ch
            return ws[i].region(full[i], 2 * chip[0] + chip[1], half).at[pl.ds(t * nr, nr), :]

        def copy(r, i, t, k, dev):
            s = (i * (hi - lo) + t - lo) * per + k
            return pltpu.make_async_remote_copy(src_ref=r, dst_ref=r, send_sem=send_sems.at[s],
                                                recv_sem=recv_sems.at[s], device_id=dev, device_id_type=MESH)

        def direct(i, t, k):
            return copy(reg(i, (x, y), c, t), i, t, k, (*(via, to)[k], c))

        def direct_in(i, t, k):
            return copy(reg(i, (via, to)[k], c, t), i, t, k, (*(via, to)[k], c))

        def relay(i, t):
            return copy(reg(i, via, c, t), i, t, 2, (*to, c))

        def relay_in(i, t):
            return copy(reg(i, dg, c, t), i, t, 2, (*to, c))

        def hand(i, t, k, half):
            return copy(reg(i, (xn, yn, dg)[k], half, t), i, t, 3 + k, (x, y, 1 - c))

        return c, direct, direct_in, relay, relay_in, hand

    def start(_, full, sems):
        _, direct, _, _, _, _ = parts(full, sems)
        for t in range(lo, hi):
            for i in range(n_w):
                direct(i, t, 0).start()
                direct(i, t, 1).start()

    def arrived(t):
        def step(_, full, sems):
            c, _, direct_in, relay, relay_in, hand = parts(full, sems)
            for i in range(n_w):
                direct_in(i, t, 0).wait_recv()
                direct_in(i, t, 1).wait_recv()
                relay(i, t).start()
                hand(i, t, 0, c).start()
                hand(i, t, 1, c).start()
        return step

    def finish(_, full, sems):
        c, direct, _, relay, relay_in, hand = parts(full, sems)
        for t in range(lo, hi):
            for i in range(n_w):
                relay_in(i, t).wait_recv()
                hand(i, t, 2, c).start()
        for i in range(n_w):
            for t in range(lo, hi):
                for k in range(3):
                    hand(i, t, k, 1 - c).wait_recv()
        for i in range(n_w):
            for t in range(lo, hi):
                direct(i, t, 0).wait_send()
                direct(i, t, 1).wait_send()
                relay(i, t).wait_send()
                for k in range(3):
                    hand(i, t, k, c).wait_send()

    n_sem = per * (hi - lo) * n_w
    return _Rider(fulls, [jax.ShapeDtypeStruct((w.R, w.C), BF16) for w in ws],
                  [pltpu.SemaphoreType.DMA((n_sem,)), pltpu.SemaphoreType.DMA((n_sem,))], start, finish,
                  steps=[arrived(t) for t in range(lo, hi)], aliases={i: i for i in range(n_w)})


def _cast_into_full(w, a32, chip_arr):
    sr, sc = w.shard_shape
    tr, tc = _tile(sr, 512), _tile(sc, 2048)
    n_r, n_c = sr // tr, sc // tc
    if w.kind == "col":
        out_spec = pl.BlockSpec((tr, tc), lambda i, j, chip: (i, chip[0] * n_c + j))
    else:
        out_spec = pl.BlockSpec((tr, tc), lambda i, j, chip: (chip[0] * n_r + i, j))

    def body(chip_ref, a_ref, o_ref):
        o_ref[...] = a_ref[...].astype(BF16)

    return _pcall(
        body, name="cast_" + w.name, out_shape=jax.ShapeDtypeStruct((w.R, w.C), BF16),
        grid_spec=pltpu.PrefetchScalarGridSpec(
            num_scalar_prefetch=1, grid=(n_r, n_c),
            in_specs=[pl.BlockSpec((tr, tc), lambda i, j, chip: (i, j))], out_specs=out_spec),
        compiler_params=_params(("parallel", "parallel")),
    )(chip_arr, a32)


def _half_view(w, g):
    return g if w.kind == "col" else g.reshape(N_CHIPS, w.R // N_CHIPS, w.C)


def _px_rider(ws, grads):
    n_w = len(ws)

    def copies(g, got, sems):
        send_sems, recv_sems = sems
        x, y, c, _ = _place()

        def half_all(w, ref, half):
            hr = w.half_rows
            if w.kind == "col":
                return ref.at[pl.ds(half * hr, hr), :]
            return ref.at[:, pl.ds(half * hr, hr), :]

        return [pltpu.make_async_remote_copy(
            src_ref=half_all(w, g[i], 1 - c), dst_ref=got[i], send_sem=send_sems.at[i], recv_sem=recv_sems.at[i],
            device_id=(x, y, 1 - c), device_id_type=MESH) for i, w in enumerate(ws)]

    def start(g, got, sems):
        for cp in copies(g, got, sems):
            cp.start()

    def finish(g, got, sems):
        for cp in copies(g, got, sems):
            cp.wait_recv()
            cp.wait_send()

    def got_shape(w):
        hr = w.half_rows
        return (hr, w.C) if w.kind == "col" else (N_CHIPS, hr, w.C)

    return _Rider([_half_view(w, g) for w, g in zip(ws, grads)],
                  [jax.ShapeDtypeStruct(got_shape(w), BF16) for w in ws],
                  [pltpu.SemaphoreType.DMA((n_w,)), pltpu.SemaphoreType.DMA((n_w,))], start, finish)


def _pair_sum(w, g, got, c_arr):
    hr = w.half_rows
    if w.kind == "col":
        tr, tc = _tile(hr, 512), _tile(w.C, 2048)
        n_r = hr // tr
        grid = (n_r, w.C // tc)
        g_spec = pl.BlockSpec((tr, tc), lambda i, j, c: (c[0] * n_r + i, j))
        o_spec = pl.BlockSpec((tr, tc), lambda i, j, c: (i, j))
    else:
        tr = _tile(hr, 512)
        n_r = hr // tr
        grid = (N_CHIPS, n_r)
        g_spec = pl.BlockSpec((1, tr, w.C), lambda s, i, c: (s, c[0] * n_r + i, 0))
        o_spec = pl.BlockSpec((1, tr, w.C), lambda s, i, c: (s, i, 0))

    def body(c_ref, g_ref, got_ref, out_ref):
        out_ref[...] = (g_ref[...].astype(F32) + got_ref[...].astype(F32)).astype(BF16)

    return _pcall(
        body, name="grad_pair_sum_" + w.name, out_shape=jax.ShapeDtypeStruct(got.shape, BF16),
        grid_spec=pltpu.PrefetchScalarGridSpec(num_scalar_prefetch=1, grid=grid, in_specs=[g_spec, o_spec],
                                               out_specs=o_spec),
        compiler_params=_params(("parallel", "parallel")),
    )(c_arr, _half_view(w, g), got)


def _cx_rider(ws, sums, part=(0, 1), q_in=None):
    n_w = len(ws)

    def parts(p, q, sems):
        send_sems, recv_sems = sems
        x, y, c, chips = _place()
        my_chip = 2 * x + y

        def rows(w, ref):
            nr = w.half_rows // part[1]
            return ref.at[pl.ds(part[0] * nr, nr), :]

        def piece(w, ref, chip):
            if w.kind == "col":
                cw = w.C // N_CHIPS
                return rows(w, ref.at[:, pl.ds(chip * cw, cw)])
            return rows(w, ref.at[chip])

        def copy(i, k, recv=False):
            chip = chips[k]
            to_chip = 2 * chip[0] + chip[1]
            return pltpu.make_async_remote_copy(
                src_ref=piece(ws[i], p[i], to_chip), dst_ref=rows(ws[i], q[i].at[to_chip if recv else my_chip]),
                send_sem=send_sems.at[3 * i + k], recv_sem=recv_sems.at[3 * i + k],
                device_id=(*chip, c), device_id_type=MESH)

        return copy

    both = [(i, k) for i in range(n_w) for k in range(N_CHIPS - 1)]

    def start(p, q, sems):
        copy = parts(p, q, sems)
        for i, k in both:
            copy(i, k).start()

    def finish(p, q, sems):
        copy = parts(p, q, sems)
        for i, k in both:
            copy(i, k, recv=True).wait_recv()
        for i, k in both:
            copy(i, k).wait_send()

    return _Rider(list(sums) + list(q_in or []),
                  [jax.ShapeDtypeStruct((N_CHIPS, w.half_rows, w.shard_shape[1]), BF16) for w in ws],
                  [pltpu.SemaphoreType.DMA((3 * n_w,)), pltpu.SemaphoreType.DMA((3 * n_w,))], start, finish,
                  aliases={n_w + i: i for i in range(n_w)} if q_in else None)


def _chip_sum(w, p, q, cc_arr):
    hr, cols = w.half_rows, w.shard_shape[1]
    tr, tc = _tile(hr, 512), _tile(cols, 2048)
    n_r, n_c = hr // tr, cols // tc

    def body(cc_ref, own, q1, q2, q3, out_ref):
        own_v = own[...] if w.kind == "col" else own[0]
        out_ref[...] = ((own_v.astype(F32) + q1[0].astype(F32)) + q2[0].astype(F32)) + q3[0].astype(F32)

    if w.kind == "col":
        own_spec = pl.BlockSpec((tr, tc), lambda i, j, cc: (i, cc[1] * n_c + j))
    else:
        own_spec = pl.BlockSpec((1, tr, tc), lambda i, j, cc: (cc[1], i, j))
    q_specs = [pl.BlockSpec((1, tr, tc), lambda i, j, cc, s=s: ((cc[1] + s) % N_CHIPS, i, j)) for s in (1, 2, 3)]
    return _pcall(
        body, name="grad_chip_sum_" + w.name, out_shape=jax.ShapeDtypeStruct(w.shard_shape, F32),
        grid_spec=pltpu.PrefetchScalarGridSpec(
            num_scalar_prefetch=1, grid=(n_r, n_c), in_specs=[own_spec] + q_specs,
            out_specs=pl.BlockSpec((tr, tc), lambda i, j, cc: (cc[0] * n_r + i, j))),
        compiler_params=_params(("parallel", "parallel")),
    )(cc_arr, p, q, q, q)


_SEM = pl.BlockSpec(memory_space=pltpu.SEMAPHORE)
_HBM = pl.BlockSpec(memory_space=pltpu.HBM)


def _split_copies(kind, ws, p, land, send_sems, recv_sems):
    x, y, c, chips = _place()
    my_chip = 2 * x + y
    pairs = []
    for i, w in enumerate(ws):
        if kind == "pair":
            hr = w.half_rows
            src = p[i].at[pl.ds((1 - c) * hr, hr), :] if w.kind == "col" else p[i].at[:, pl.ds((1 - c) * hr, hr), :]
            cp = pltpu.make_async_remote_copy(src_ref=src, dst_ref=land[i], send_sem=send_sems.at[i],
                                              recv_sem=recv_sems.at[i], device_id=(x, y, 1 - c), device_id_type=MESH)
            pairs.append((cp, cp))
            continue
        for k, chip in enumerate(chips):
            to_chip = 2 * chip[0] + chip[1]
            src = p[i].at[:, pl.ds(to_chip * (w.C // N_CHIPS), w.C // N_CHIPS)] if w.kind == "col" else p[i].at[to_chip]
            kw = dict(send_sem=send_sems.at[3 * i + k], recv_sem=recv_sems.at[3 * i + k], device_id=(*chip, c),
                      device_id_type=MESH)
            pairs.append((pltpu.make_async_remote_copy(src_ref=src, dst_ref=land[i].at[my_chip], **kw),
                          pltpu.make_async_remote_copy(src_ref=src, dst_ref=land[i].at[to_chip], **kw)))
    return pairs


def _split_start(name, kind, ws, arrays):
    n_w = len(ws)
    if kind == "pair":
        arrays = [_half_view(w, g) for w, g in zip(ws, arrays)]
        lands = [lax.empty((w.half_rows, w.C) if w.kind == "col" else (N_CHIPS, w.half_rows, w.C), BF16) for w in ws]
    else:
        lands = [lax.empty((N_CHIPS, w.half_rows, w.shard_shape[1]), BF16) for w in ws]
    n_sem = n_w if kind == "pair" else 3 * n_w

    def body(*refs):
        p, land = refs[:n_w], refs[n_w:2 * n_w]
        for out, _ in _split_copies(kind, ws, p, land, refs[2 * n_w], refs[2 * n_w + 1]):
            out.start()
        refs[-1][...] = jnp.zeros_like(refs[-1])

    arrays = [pltpu.with_memory_space_constraint(a, pltpu.HBM) for a in list(arrays) + lands]
    res = _pcall(
        body, name=name,
        out_shape=(pltpu.SemaphoreType.DMA((n_sem,)), pltpu.SemaphoreType.DMA((n_sem,)),
                   *[pltpu.HBM(a.shape, a.dtype) for a in arrays], jax.ShapeDtypeStruct((8, 128), F32)),
        in_specs=[_HBM] * (2 * n_w),
        out_specs=(_SEM, _SEM, *[_HBM] * (2 * n_w), pl.BlockSpec(memory_space=pltpu.VMEM)),
        input_output_aliases={i: 2 + i for i in range(2 * n_w)},
        compiler_params=pltpu.CompilerParams(has_side_effects=pltpu.SideEffectType.DATAFLOW_SIDE_EFFECTING),
    )(*arrays)
    return (kind, ws, res[0], res[1], list(res[2:2 + n_w]), list(res[2 + n_w:2 + 2 * n_w])), res[-1]


def _split_wait(name, flight, after):
    kind, ws, send_sems, recv_sems, arrays, lands = flight
    n_w = len(ws)

    def body(*refs):
        p, land = refs[:n_w], refs[n_w:2 * n_w]
        for _, cp in _split_copies(kind, ws, p, land, refs[2 * n_w], refs[2 * n_w + 1]):
            cp.wait_send()
            cp.wait_recv()

    res = _pcall(
        body, name=name,
        out_shape=[pltpu.HBM(a.shape, a.dtype) for a in list(arrays) + list(lands)],
        in_specs=[_HBM] * (2 * n_w) + [_SEM, _SEM] + [ANY] * len(after), out_specs=[_HBM] * (2 * n_w),
        input_output_aliases={i: i for i in range(2 * n_w)},
        compiler_params=pltpu.CompilerParams(has_side_effects=pltpu.SideEffectType.DATAFLOW_SIDE_EFFECTING),
    )(*arrays, *lands, send_sems, recv_sems, *after)
    return list(res[:n_w]), list(res[n_w:])


def _sf_rider(ws, grads):
    n_w = len(ws)

    def copy(g, sems, i, half):
        send_sems, recv_sems = sems
        x, y, c, _ = _place()
        h = c if half == "mine" else 1 - c
        reg = ws[i].shard_half(g[i], h)
        return pltpu.make_async_remote_copy(src_ref=reg, dst_ref=reg, send_sem=send_sems.at[i], recv_sem=recv_sems.at[i],
                                            device_id=(x, y, 1 - c), device_id_type=MESH)

    def start(_, g, sems):
        for i in range(n_w):
            copy(g, sems, i, "mine").start()

    def finish(_, g, sems):
        for i in range(n_w):
            copy(g, sems, i, "other").wait_recv()
            copy(g, sems, i, "mine").wait_send()

    return _Rider(grads, [jax.ShapeDtypeStruct(w.shard_shape, F32) for w in ws],
                  [pltpu.SemaphoreType.DMA((n_w,)), pltpu.SemaphoreType.DMA((n_w,))], start, finish,
                  aliases={i: i for i in range(n_w)})


def _adamw_math(w, g, m, v):
    m = ADAM_B1 * m + (1.0 - ADAM_B1) * g
    v = ADAM_B2 * v + (1.0 - ADAM_B2) * (g * g)
    m_hat = m / (1.0 - ADAM_B1 ** ADAM_STEP)
    v_hat = v / (1.0 - ADAM_B2 ** ADAM_STEP)
    delta = -ADAM_LR * (m_hat / (jnp.sqrt(v_hat) + ADAM_EPS) + ADAM_WD * w)
    return delta, m, v


def _adamw(name, w, g, m, v, after=None):
    R, C = w.shape
    tr, tc = _tile(R, 256), _tile(C, 2048)
    behind = [] if after is None else [after]

    def body(w_ref, g_ref, m_ref, v_ref, *rest):
        g_out, d_out, m_out, v_out = rest[len(behind):]
        g = g_ref[...]
        g_out[...] = g
        d_out[...], m_out[...], v_out[...] = _adamw_math(w_ref[...], g, m_ref[...], v_ref[...])

    spec = pl.BlockSpec((tr, tc), lambda i, j: (i, j))
    sh = jax.ShapeDtypeStruct((R, C), F32)
    return _pcall(body, name=name, grid=(R // tr, C // tc), in_specs=[spec] * 4 + [ANY] * len(behind),
                  out_specs=[spec] * 4, out_shape=[sh] * 4, compiler_params=_params(("parallel", "parallel")))(
                      w, g, m, v, *behind)


def _ada_update(sct, dmod_sh, w, m, v, riders=()):
    R, C = w.shape
    tr, tc = _tile(R, 256), _tile(C, 1024)

    def body(s_ref, d_ref, w_ref, m_ref, v_ref, g_out, d_out, m_out, v_out):
        s, d = s_ref[...], d_ref[...]
        g = s[:, 0:1] * d[0:1, :]
        for b in range(1, N_DEV):
            g += s[:, b:b + 1] * d[b:b + 1, :]
        g_out[...] = g
        d_out[...], m_out[...], v_out[...] = _adamw_math(w_ref[...], g, m_ref[...], v_ref[...])

    spec = pl.BlockSpec((tr, tc), lambda i, j: (i, j))
    sh = jax.ShapeDtypeStruct((R, C), F32)
    return _ride(
        "ada_update", body, riders, [sct, dmod_sh, w, m, v], grid=(R // tr, C // tc),
        in_specs=[pl.BlockSpec((tr, N_DEV), lambda i, j: (i, 0)), pl.BlockSpec((N_DEV, tc), lambda i, j: (0, j)),
                  spec, spec, spec],
        out_specs=[spec] * 4, out_shape=[sh] * 4, scratch_shapes=[], sem=("parallel", "parallel"))


def _silu_rows(c_row):
    D = c_row.shape[1]

    def body(c_ref, o_ref):
        cv = c_ref[...]
        o_ref[...] = cv * jax.nn.sigmoid(cv)

    return _pcall(body, name="silu_c", out_shape=jax.ShapeDtypeStruct((1, D), F32))(c_row)


def _pack_partials(parts, widths, total):
    n = len(widths)

    def body(*refs):
        loss_p, out_ref = refs[n], refs[n + 1]
        off = 0
        for ref, wd in zip(refs[:n], widths):
            out_ref[:, off:off + wd] = jnp.sum(ref[...], axis=0)
            off += wd
        loss = jnp.sum(jnp.sum(loss_p[...], axis=0), axis=1, keepdims=True)
        out_ref[:, off:off + 128] = jnp.broadcast_to(loss, (1, 128))
        if off + 128 < total:
            out_ref[:, off + 128:total] = jnp.zeros((1, total - off - 128), F32)

    return _pcall(body, name="pack_partials", out_shape=jax.ShapeDtypeStruct((1, total), F32))(*parts)


def _small_update(gathered, offsets, params, loss_off):
    n_p = len(params)

    def over_devices(g_ref, off, wd):
        blk = g_ref[:, off:off + wd]
        g = blk[0:1, :]
        for b in range(1, N_DEV):
            g = g + blk[b:b + 1, :]
        return g

    def body(*refs):
        g_ref = refs[0]
        prm = refs[1:1 + 3 * n_p]
        outs = refs[1 + 3 * n_p:]
        outs[4 * n_p][...] = over_devices(g_ref, loss_off, 128)
        for i, (off, wd) in enumerate(offsets):
            g = over_devices(g_ref, off, wd)
            w, m, v = prm[3 * i][...], prm[3 * i + 1][...], prm[3 * i + 2][...]
            outs[4 * i][...] = g
            outs[4 * i + 1][...], outs[4 * i + 2][...], outs[4 * i + 3][...] = _adamw_math(w, g, m, v)

    flat = [a for t in params for a in t]
    out_shape = [jax.ShapeDtypeStruct(t[0].shape, F32) for t in params for _ in range(4)]
    out_shape.append(jax.ShapeDtypeStruct((1, 128), F32))
    return _pcall(body, name="small_update", out_shape=out_shape)(gathered, *flat)


def kernel(x, c, w_ada, b_ada, norm1_w, w_in, q_norm_w, k_norm_w, w_pool, pool_scale, w_a_up, w_b_up, w_o, norm2_w, w_ff1, w_ff2, loss_target, m_w_ada, m_b_ada, m_norm1_w, m_w_in, m_q_norm_w, m_k_norm_w, m_w_pool, m_pool_scale, m_w_a_up, m_w_b_up, m_w_o, m_norm2_w, m_w_ff1, m_w_ff2, v_w_ada, v_b_ada, v_norm1_w, v_w_in, v_q_norm_w, v_k_norm_w, v_w_pool, v_pool_scale, v_w_a_up, v_w_b_up, v_w_o, v_norm2_w, v_w_ff1, v_w_ff2):
    _, S, D = x.shape
    PW = D // 2
    H = PW // HEAD_DIM
    cg = PW // N_GROUPS
    IN = w_in.shape[2] * N_CHIPS
    FF = w_ff1.shape[2] * N_CHIPS
    A_COLS = w_ada.shape[2]
    xi, yi, ci = lax.axis_index("x"), lax.axis_index("y"), lax.axis_index("c")
    chip = 2 * xi + yi
    dev = 2 * chip + ci
    c_arr = jnp.reshape(ci, (1,)).astype(jnp.int32)
    x2, tgt = x[0], loss_target[0]

    ws = [_W("w_in", "col", D, IN), _W("w_pool", "row", PW, cg), _W("w_a_up", "col", PW, D),
          _W("w_b_up", "col", PW, D), _W("w_o", "row", D, D), _W("w_ff1", "col", D, FF), _W("w_ff2", "row", FF, D)]
    w32 = [w_in[0], w_pool[0].reshape(cg, cg), w_a_up[0], w_b_up[0], w_o[0], w_ff1[0], w_ff2[0]]
    m32 = [m_w_in[0], m_w_pool[0].reshape(cg, cg), m_w_a_up[0], m_w_b_up[0], m_w_o[0], m_w_ff1[0], m_w_ff2[0]]
    v32 = [v_w_in[0], v_w_pool[0].reshape(cg, cg), v_w_a_up[0], v_w_b_up[0], v_w_o[0], v_w_ff1[0], v_w_ff2[0]]

    W_IN, W_POOL, W_A, W_B, W_O, W_FF1, W_FF2 = ws
    chip_arr = jnp.reshape(chip, (1,)).astype(jnp.int32)
    cc_arr = jnp.stack([ci, chip]).astype(jnp.int32)
    s_in, s_pool, s_a, s_b, s_o, s_ff1, s_ff2 = [_cast_into_full(w, a, chip_arr) for w, a in zip(ws, w32)]
    (win_f,) = _run_rider("gather_w_in", _ag_rider([W_IN], [s_in]))

    sc_row = _silu_rows(c)
    sc_all = _dev_allgather("gather_silu_c", sc_row.reshape(8, D // 8)).reshape(N_DEV, D)
    sc16 = jnp.concatenate([sc_all, jnp.zeros_like(sc_all)], axis=0)
    b_cols = lax.dynamic_slice(b_ada, (0, chip * A_COLS), (1, A_COLS))
    (mod_cols,) = _mm("mod_cols", [(sc16, w_ada[0])], M=2 * N_DEV, N=A_COLS, K=D, tm=16, tn=1024, tk=1024,
                      a_pro=lambda a: a.astype(BF16), b_pro=lambda b: b.astype(BF16),
                      extras=[(b_cols, "row", 0)], outs=[_tile_out(F32)], epi=lambda accs, ex: [accs[0] + ex[0]])
    mod_all = _dev_allgather("gather_mod", mod_cols[:N_DEV]).reshape(N_CHIPS, 2, N_DEV, A_COLS)
    mod_row = lax.dynamic_index_in_dim(mod_all[:, 0], dev, axis=1, keepdims=False).reshape(1, N_CHIPS * A_COLS)
    shift1, scale1, gate1, shift2, scale2, gate2 = [mod_row[:, i * D:(i + 1) * D] for i in range(6)]

    WIDE = dict(tm=2048, tn=512, tk=2048)
    DEEP = dict(tm=1024, tn=1024, tk=1024)
    h = _norm_mod("norm1_mod", x2, norm1_w, scale1, shift1)
    (proj,), ((wpool_f, wa_f, wb_f, wo_f),) = _mm(
        "in_proj", [(h, win_f)], M=S, N=IN, K=D, outs=[_tile_out(BF16)], epi=lambda accs, ex: [accs[0]], **WIDE,
        riders=[_ag_rider([W_POOL, W_A, W_B, W_O], [s_pool, s_a, s_b, s_o], n_ch=2)])
    pooled, pa = _pool_fwd(proj, wpool_f, pool_scale, S, PW)
    (att, attf), ((wff1_f,),) = _attn_fwd(proj, q_norm_w, k_norm_w, S, H, PW // HEAD_DIM,
                                          riders=[_ag_rider([W_FF1], [s_ff1])])

    def merge_epi(accs, ex):
        sa, sb = jax.nn.sigmoid(ex[0].astype(F32)), jax.nn.sigmoid(ex[1].astype(F32))
        return [sa * accs[0] + sb * accs[1], accs[0], accs[1]]

    (merged, ya, yb), (ff2_a,) = _mm("branch_up_merge", [(pa, wa_f), (att, wb_f)], M=S, N=D, K=PW,
                                     extras=[(proj, "tile", 4 * PW), (proj, "tile", 4 * PW + D)],
                                     outs=[_tile_out(BF16)] * 3, epi=merge_epi,
                                     riders=[_ag_rider([W_FF2], [s_ff2], chunks=(0, 1))])
    (x1, o), (ff2_b,) = _mm("out_proj", [(merged, wo_f)], M=S, N=D, K=D, extras=[(x2, "tile", 0), (gate1, "row", 0)],
                            outs=[_tile_out(F32), _tile_out(BF16)], epi=lambda accs, ex: [ex[0] + ex[1] * accs[0], accs[0]],
                            riders=[_ag_rider([W_FF2], ff2_a, chunks=(1, 2))], **WIDE)
    h2 = _norm_mod("norm2_mod", x1, norm2_w, scale2, shift2)
    (rl,), ((wff2_f,),) = _mm("ff1", [(h2, wff1_f)], M=S, N=FF, K=D, outs=[_tile_out(BF16)], **WIDE,
                              epi=lambda accs, ex: [jnp.maximum(accs[0], 0.0)],
                              riders=[_ag_rider([W_FF2], ff2_b, chunks=(2, 4))])

    def square(a):
        af = a.astype(F32)
        return (af * af).astype(BF16)

    def loss_epi(accs, ex):
        x1_t, tgt_t, g2 = ex
        f = accs[0]
        diff = (x1_t + g2 * f) - tgt_t
        dy = diff * (1.0 / D)
        return [dy, dy * g2, _colsum(dy * f), _colsum(diff * diff)]

    dy, df, dgate2_p, loss_p = _mm("ff2_loss", [(rl, wff2_f)], M=S, N=D, K=FF, a_pro=square, tm=1024, tn=1024, tk=512,
                                   extras=[(x1, "tile", 0), (tgt, "tile", 0), (gate2, "row", 0)],
                                   outs=[_tile_out(F32), _tile_out(BF16), _COLSUM, _COLSUM], epi=loss_epi)

    tied = []

    def behind(token, a):
        a, token = lax.optimization_barrier((a, token))
        tied.append(token)
        return a

    def pair_sums(group, partials, got):
        return [_pair_sum(w, g, r, c_arr) for w, g, r in zip(group, partials, got)]

    def chip_sums(group, sums, from_chips):
        return [_chip_sum(w, p, q, cc_arr) for w, p, q in zip(group, sums, from_chips)]

    first = lambda accs, ex: [accs[0]]
    gmm = dict(ta=True, outs=[_tile_out(BF16)], epi=first, **WIDE)
    (g_ff2,) = _mm("grad_w_ff2", [(rl, df)], M=FF, N=D, K=S, a_pro=square, ta=True, tm=512, tn=2048, tk=2048,
                   outs=[_tile_out(BF16)], epi=first)
    flight, token = _split_start("pair_w_ff2_start", "pair", [W_FF2], [g_ff2])
    (dz1,) = _mm("d_ff_hidden", [(behind(token, df), wff2_f)], M=S, N=FF, K=D, tb=True, extras=[(rl, "tile", 0)],
                 outs=[_tile_out(BF16)], epi=lambda accs, ex: [accs[0] * (2.0 * ex[0].astype(F32))], **WIDE)
    sum_ff2 = pair_sums([W_FF2], *_split_wait("pair_w_ff2_wait", flight, after=[dz1] + tied))
    chip_ff2, token = _split_start("chip_w_ff2_start", "chip", [W_FF2], sum_ff2)
    (g_ff1,) = _mm("grad_w_ff1", [(behind(token, h2), dz1)], M=D, N=FF, K=S, **gmm)
    flight, token = _split_start("pair_w_ff1_start", "pair", [W_FF1], [g_ff1])
    (dh2,) = _mm("d_h2", [(behind(token, dz1), wff1_f)], M=S, N=D, K=FF, tb=True, outs=[_tile_out(F32)], epi=first, **DEEP)
    sum_ff1 = pair_sums([W_FF1], *_split_wait("pair_w_ff1_wait", flight, after=[dh2] + tied))
    chip_ff1, token = _split_start("chip_w_ff1_start", "chip", [W_FF1], sum_ff1)
    dx1, dshift2_p, dscale2_p, gn2_p, do, dgate1_p = _norm_mod_bwd("norm2_bwd", behind(token, dh2), x1, dy, norm2_w, scale2,
                                                                   gate_o=(o, gate1))
    (g_wo,) = _mm("grad_w_o", [(merged, do)], M=D, N=D, K=S, **gmm)

    def gate_epi(accs, ex):
        dm = accs[0]
        sa, sb = jax.nn.sigmoid(ex[0].astype(F32)), jax.nn.sigmoid(ex[1].astype(F32))
        ya_t, yb_t = ex[2].astype(F32), ex[3].astype(F32)
        return [dm * sa, dm * sb, dm * ya_t * (sa * (1.0 - sa)), dm * yb_t * (sb * (1.0 - sb))]

    dya, dyb, dga, dgb = _mm("d_merged", [(do, wo_f)], M=S, N=D, K=D, tb=True, tm=1024, tn=512, tk=2048,
                             extras=[(proj, "tile", 4 * PW), (proj, "tile", 4 * PW + D), (ya, "tile", 0), (yb, "tile", 0)],
                             outs=[_tile_out(BF16)] * 4, epi=gate_epi)
    (g_wa,) = _mm("grad_w_a_up", [(pa, dya)], M=PW, N=D, K=S, **gmm)
    (g_wb,) = _mm("grad_w_b_up", [(att, dyb)], M=PW, N=D, K=S, **gmm)
    (dpa,) = _mm("d_pool_out", [(dya, wa_f)], M=S, N=PW, K=D, tb=True, outs=[_tile_out(F32)], epi=first, **WIDE)
    mid = [W_A, W_B, W_O]
    flight, token = _split_start("pair_mid_start", "pair", mid, [g_wa, g_wb, g_wo])
    (datt,) = _mm("d_att", [(behind(token, dyb), wb_f)], M=S, N=PW, K=D, tb=True, outs=[_tile_out(BF16)], epi=first, **WIDE)
    sum_mid = pair_sums(mid, *_split_wait("pair_mid_wait", flight, after=[datt] + tied))
    chip_mid, token = _split_start("chip_mid_start", "chip", mid, sum_mid)
    du, g_wpool4, gscale_p = _pool_bwd(dpa, pooled, wpool_f, pool_scale, S, PW)
    dq, dk, dv, gq_p, gk_p = _attn_bwd(proj, behind(token, datt), attf, q_norm_w, k_norm_w, S, H, PW // HEAD_DIM)
    dproj = jnp.concatenate([du, dq, dk, dv, dga, dgb], axis=1)
    early = [W_FF1, W_FF2]
    sum_ff1, q_ff1 = _split_wait("chip_w_ff1_wait", chip_ff1, after=[dq] + tied)
    sum_ff2, q_ff2 = _split_wait("chip_w_ff2_wait", chip_ff2, after=[dq] + tied)
    halves_early = chip_sums(early, sum_ff1 + sum_ff2, q_ff1 + q_ff2)
    (g_win,), (grads_early,) = _mm("grad_w_in", [(h, dproj)], M=D, N=IN, K=S, riders=[_sf_rider(early, halves_early)],
                                   **gmm)
    last = [W_IN, W_POOL]
    g_last = [g_win, g_wpool4.reshape(PW, cg)]
    sum_mid, q_mid = _split_wait("chip_mid_wait", chip_mid, after=[g_win] + tied)
    halves_mid = chip_sums(mid, sum_mid, q_mid)
    (dh,), (got_last, grads_mid) = _mm("d_h", [(dproj, win_f)], M=S, N=D, K=IN, tb=True, outs=[_tile_out(F32)], epi=first,
                                       riders=[_px_rider(last, g_last), _sf_rider(mid, halves_mid)], **DEEP)
    sum_last = pair_sums(last, g_last, got_last)
    grad_x, dshift1_p, dscale1_p, gn1_p = _norm_mod_bwd("norm1_bwd", dh, x2, dx1, norm1_w, scale1)

    parts = [dshift1_p, dscale1_p, dgate1_p, dshift2_p, dscale2_p, dgate2_p, gn1_p, gn2_p,
             gscale_p.reshape(1, 1, PW), gq_p, gk_p]
    widths = [D] * 8 + [PW, HEAD_DIM, HEAD_DIM]
    used = sum(widths)
    P = -(-(used + 128) // 1024) * 1024
    packed = _pack_partials(parts + [loss_p], widths, P)
    gathered = _dev_allgather("gather_vector_grads", packed.reshape(8, P // 8)).reshape(N_DEV, P)
    sum_last, gathered = lax.optimization_barrier((sum_last, gathered))
    chip_last, token = _split_start("chip_last_start", "chip", last, sum_last)
    small = [(b_ada, m_b_ada, v_b_ada), (norm1_w, m_norm1_w, v_norm1_w), (norm2_w, m_norm2_w, v_norm2_w),
             (pool_scale, m_pool_scale, v_pool_scale), (q_norm_w, m_q_norm_w, v_q_norm_w),
             (k_norm_w, m_k_norm_w, v_k_norm_w)]
    offsets = [(0, 6 * D), (6 * D, D), (7 * D, D), (8 * D, PW), (8 * D + PW, HEAD_DIM), (8 * D + PW + HEAD_DIM, HEAD_DIM)]
    su = _small_update(gathered, offsets, small, used)
    (g_b, d_b, nm_b, nv_b, g_n1, d_n1, nm_n1, nv_n1, g_n2, d_n2, nm_n2, nv_n2, g_ps, d_ps, nm_ps, nv_ps,
     g_qn, d_qn, nm_qn, nv_qn, g_kn, d_kn, nm_kn, nv_kn, loss_sum) = su
    dmod_sh = lax.dynamic_slice(gathered, (0, chip * A_COLS), (N_DEV, A_COLS))
    dmod_sh, token = lax.optimization_barrier((dmod_sh, token))
    g_ada, d_ada, nm_ada, nv_ada = _ada_update(sc_all.T, dmod_sh, w_ada[0], m_w_ada[0], v_w_ada[0])

    upd_done = [_adamw("adamw_" + w.name, a, g, m, v, after=token)
                for w, a, g, m, v in zip(ws[2:], w32[2:], list(grads_mid) + list(grads_early), m32[2:], v32[2:])]

    sum_last, q_last = _split_wait("chip_last_wait", chip_last, after=[nv_ada] + [u[3] for u in upd_done])
    halves_last = chip_sums(last, sum_last, q_last)
    filled = _run_rider("grad_sibling_fill", _sf_rider(last, halves_last))
    upd = [_adamw("adamw_" + w.name, a, g, m, v) for w, a, g, m, v in zip(ws[:2], w32[:2], filled, m32[:2], v32[:2])]
    upd += upd_done

    loss = (0.5 / D) * loss_sum[0, 0]

    def up(a):
        return a[None]

    def pool4(a):
        return a.reshape(1, N_GROUPS, cg // N_CHIPS, cg)

    (gr_win, d_win, nm_win, nv_win), (gr_wp, d_wp, nm_wp, nv_wp), (gr_wa, d_wa, nm_wa, nv_wa), \
        (gr_wb, d_wb, nm_wb, nv_wb), (gr_wo, d_wo, nm_wo, nv_wo), (gr_f1, d_f1, nm_f1, nv_f1), \
        (gr_f2, d_f2, nm_f2, nv_f2) = upd
    return (
        loss, grad_x[None],
        up(g_ada), g_b, g_n1, up(gr_win), g_qn, g_kn, pool4(gr_wp), g_ps, up(gr_wa), up(gr_wb), up(gr_wo), g_n2,
        up(gr_f1), up(gr_f2),
        up(d_ada), d_b, d_n1, up(d_win), d_qn, d_kn, pool4(d_wp), d_ps, up(d_wa), up(d_wb), up(d_wo), d_n2,
        up(d_f1), up(d_f2),
        up(nm_ada), nm_b, nm_n1, up(nm_win), nm_qn, nm_kn, pool4(nm_wp), nm_ps, up(nm_wa), up(nm_wb), up(nm_wo), nm_n2,
        up(nm_f1), up(nm_f2),
        up(nv_ada), nv_b, nv_n1, up(nv_win), nv_qn, nv_kn, pool4(nv_wp), nv_ps, up(nv_wa), up(nv_wb), up(nv_wo), nv_n2,
        up(nv_f1), up(nv_f2),
    )
```

```python
import functools
import math

import jax
import jax.numpy as jnp
from jax import lax
from jax.experimental import pallas as pl
from jax.experimental.pallas import tpu as pltpu

F32 = jnp.float32
BF16 = jnp.bfloat16
MESH = pl.DeviceIdType.MESH
ANY = pl.BlockSpec(memory_space=pl.ANY)

EPS = 1e-6
HEAD_DIM = 128
LANES, SUBLANES = 128, 8
POOL_WINDOWS = (2, 4, 8, 16)
N_GROUPS = len(POOL_WINDOWS)
assert POOL_WINDOWS == tuple(2 << g for g in range(N_GROUPS))
N_CHIPS = 4
N_DEV = 8
ADAM_LR, ADAM_B1, ADAM_B2, ADAM_EPS, ADAM_WD, ADAM_STEP = 0.001, 0.9, 0.999, 1e-08, 0.01, 10
VMEM_LIMIT_V7X = 56 * 1024 * 1024
ATT_T = 256
ATT_GROUP = 4
POOL_T = 256


def _pcall(body, **kw):
    return pl.pallas_call(body, **kw)


def _params(sem=None):
    return pltpu.CompilerParams(dimension_semantics=sem, vmem_limit_bytes=VMEM_LIMIT_V7X)


def _tile(n, pref):
    if n <= pref:
        return n
    t = pref
    while n % t:
        t //= 2
    return t


class _Rider:
    def __init__(self, arrays, out_shape, sems, start, finish, aliases=None, steps=()):
        self.arrays, self.out_shape, self.sems = list(arrays), list(out_shape), list(sems)
        self.start, self.finish, self.aliases, self.steps = start, finish, aliases or {}, list(steps)


def _ride(name, body, riders, arrays, *, grid, in_specs, out_specs, out_shape, scratch_shapes, sem):
    n_in, n_out, n_scr = len(arrays), len(out_shape), len(scratch_shapes)
    r_arrays = [a for r in riders for a in r.arrays]
    r_outs = [o for r in riders for o in r.out_shape]
    r_sems = [s for r in riders for s in r.sems]
    n_hooks = max([len(r.steps) for r in riders], default=0)
    total = math.prod(grid)
    aliases, off_i, off_o = {}, n_in, n_out
    for r in riders:
        for a, o in r.aliases.items():
            aliases[off_i + a] = off_o + o
        off_i += len(r.arrays)
        off_o += len(r.out_shape)

    def full(*refs):
        p = 0
        groups = []
        for n in (n_in, len(r_arrays), n_out, len(r_outs), n_scr, len(r_sems)):
            groups.append(refs[p:p + n])
            p += n
        ins, rin, outs, rout, scr, rsem = groups

        def each(what):
            a = o = s = 0
            for r in riders:
                fn = what(r)
                if fn is not None:
                    fn(rin[a:a + len(r.arrays)], rout[o:o + len(r.out_shape)], rsem[s:s + len(r.sems)])
                a, o, s = a + len(r.arrays), o + len(r.out_shape), s + len(r.sems)

        if riders:
            lin = 0
            for d, g in enumerate(grid):
                lin = lin * g + pl.program_id(d)
            pl.when(lin == 0)(lambda: each(lambda r: r.start))
            for t in range(n_hooks):
                pl.when(lin == min(total - 1, ((t + 1) * total) // n_hooks))(
                    lambda t=t: each(lambda r: r.steps[t] if t < len(r.steps) else None))
        body(*ins, *outs, *scr)
        if riders:
            pl.when(lin == total - 1)(lambda: each(lambda r: r.finish))

    res = _pcall(
        full, name=name, grid=grid, in_specs=list(in_specs) + [ANY] * len(r_arrays),
        out_specs=list(out_specs) + [ANY] * len(r_outs), out_shape=list(out_shape) + r_outs,
        scratch_shapes=list(scratch_shapes) + r_sems, input_output_aliases=aliases,
        compiler_params=_params(("arbitrary",) * len(grid) if riders else sem),
    )(*arrays, *r_arrays)
    if not riders:
        return res
    main, rest, per = res[:n_out], res[n_out:], []
    for r in riders:
        per.append(rest[:len(r.out_shape)])
        rest = rest[len(r.out_shape):]
    return main, per


def _run_rider(name, rider):
    def body(*refs):
        n_a, n_o = len(rider.arrays), len(rider.out_shape)
        ins, outs, sems = refs[:n_a], refs[n_a:n_a + n_o], refs[n_a + n_o:]
        for fn in [rider.start] + rider.steps + [rider.finish]:
            fn(ins, outs, sems)

    return _pcall(body, name=name, out_shape=rider.out_shape, in_specs=[ANY] * len(rider.arrays),
                  out_specs=[ANY] * len(rider.out_shape), scratch_shapes=rider.sems,
                  input_output_aliases=rider.aliases)(*rider.arrays)


def _mm(name, pairs, *, M, N, K, ta=False, tb=False, tm=512, tn=1024, tk=1024,
        a_pro=None, b_pro=None, extras=(), outs, epi, riders=()):
    tm, tn, tk = _tile(M, tm), _tile(N, tn), _tile(K, tk)
    n_i, n_j, n_k = M // tm, N // tn, K // tk
    n_p, n_e = len(pairs), len(extras)
    arrays, in_specs = [], []
    for a, _ in pairs:
        arrays.append(a)
        in_specs.append(pl.BlockSpec((tk, tm), lambda i, j, k: (k, i)) if ta
                        else pl.BlockSpec((tm, tk), lambda i, j, k: (i, k)))
    for _, b in pairs:
        arrays.append(b)
        in_specs.append(pl.BlockSpec((tn, tk), lambda i, j, k: (j, k)) if tb
                        else pl.BlockSpec((tk, tn), lambda i, j, k: (k, j)))
    for arr, kind, off in extras:
        ob = off // tn
        assert off % tn == 0
        arrays.append(arr)
        if kind == "tile":
            in_specs.append(pl.BlockSpec((tm, tn), lambda i, j, k, ob=ob: (i, j + ob)))
        else:
            in_specs.append(pl.BlockSpec((1, tn), lambda i, j, k, ob=ob: (0, j + ob)))
    out_shape, out_specs = [], []
    for o in outs:
        if o["kind"] == "tile":
            out_shape.append(jax.ShapeDtypeStruct((M, N), o["dtype"]))
            out_specs.append(pl.BlockSpec((tm, tn), lambda i, j, k: (i, j)))
        else:
            out_shape.append(jax.ShapeDtypeStruct((n_i, 1, N), F32))
            out_specs.append(pl.BlockSpec((1, 1, tn), lambda i, j, k: (i, 0, j)))
    dims = (((0 if ta else 1,), (1 if tb else 0,)), ((), ()))

    def body(*refs):
        a_refs, b_refs = refs[:n_p], refs[n_p:2 * n_p]
        e_refs = refs[2 * n_p:2 * n_p + n_e]
        o_refs = refs[2 * n_p + n_e:2 * n_p + n_e + len(outs)]
        acc_refs = refs[2 * n_p + n_e + len(outs):]

        def product(p):
            a, b = a_refs[p][...], b_refs[p][...]
            if a_pro is not None:
                a = a_pro(a)
            if b_pro is not None:
                b = b_pro(b)
            return lax.dot_general(a, b, dims, preferred_element_type=F32)

        def write(accs):
            vals = epi(accs, [e[...] for e in e_refs])
            for o, o_ref, val in zip(outs, o_refs, vals):
                if o["kind"] == "tile":
                    o_ref[...] = val.astype(o_ref.dtype)
                else:
                    o_ref[0] = val

        if n_k == 1:
            write([product(p) for p in range(n_p)])
            return
        k = pl.program_id(2)

        @pl.when(k == 0)
        def _():
            for acc in acc_refs:
                acc[...] = jnp.zeros_like(acc)

        for p in range(n_p):
            acc_refs[p][...] += product(p)

        pl.when(k == n_k - 1)(lambda: write([acc[...] for acc in acc_refs]))

    return _ride(name, body, riders, arrays, grid=(n_i, n_j, n_k), in_specs=in_specs, out_specs=out_specs,
                 out_shape=out_shape, scratch_shapes=[pltpu.VMEM((tm, tn), F32) for _ in pairs] if n_k > 1 else [],
                 sem=("parallel", "parallel", "arbitrary"))


def _tile_out(dtype):
    return {"kind": "tile", "dtype": dtype}


_COLSUM = {"kind": "colsum"}


def _colsum(v):
    return jnp.sum(v, axis=0, keepdims=True)


def _norm_mod(name, x, norm_w, scale, shift):
    S, D = x.shape
    tr = _tile(S, 256)

    def body(x_ref, nw_ref, sc_ref, sh_ref, h_ref):
        xv = x_ref[...]
        r = lax.rsqrt(jnp.mean(xv * xv, axis=-1, keepdims=True) + EPS)
        h_ref[...] = ((xv * r * nw_ref[...]) * (1.0 + sc_ref[...]) + sh_ref[...]).astype(BF16)

    row = pl.BlockSpec((1, D), lambda i: (0, 0))
    til = pl.BlockSpec((tr, D), lambda i: (i, 0))
    return _pcall(body, name=name, grid=(S // tr,), in_specs=[til, row, row, row], out_specs=til,
                  out_shape=jax.ShapeDtypeStruct((S, D), BF16), compiler_params=_params(("parallel",)))(
                      x, norm_w, scale, shift)


def _norm_mod_bwd(name, dh, x, dres, norm_w, scale, gate_o=None):
    S, D = x.shape
    tr = _tile(S, 256)
    n_r = S // tr
    with_gate = gate_o is not None

    def body(*refs):
        if with_gate:
            dh_ref, x_ref, dres_ref, nw_ref, sc_ref, o_ref, g_ref, dx_ref, p1, p2, p3, do_ref, p4 = refs
        else:
            dh_ref, x_ref, dres_ref, nw_ref, sc_ref, dx_ref, p1, p2, p3 = refs
        dhv, xv, nw = dh_ref[...], x_ref[...], nw_ref[...]
        r = lax.rsqrt(jnp.mean(xv * xv, axis=-1, keepdims=True) + EPS)
        xh = xv * r
        p1[0] = _colsum(dhv)
        p2[0] = _colsum(dhv * (xh * nw))
        dn = dhv * (1.0 + sc_ref[...])
        p3[0] = _colsum(dn * xh)
        dxh = dn * nw
        dx = dres_ref[...] + r * (dxh - xh * jnp.mean(dxh * xh, axis=-1, keepdims=True))
        dx_ref[...] = dx
        if with_gate:
            do_ref[...] = (dx * g_ref[...]).astype(BF16)
            p4[0] = _colsum(dx * o_ref[...].astype(F32))

    row = pl.BlockSpec((1, D), lambda i: (0, 0))
    til = pl.BlockSpec((tr, D), lambda i: (i, 0))
    part = pl.BlockSpec((1, 1, D), lambda i: (i, 0, 0))
    part_shape = jax.ShapeDtypeStruct((n_r, 1, D), F32)
    in_specs = [til, til, til, row, row]
    arrays = [dh, x, dres, norm_w, scale]
    out_specs = [til, part, part, part]
    out_shape = [jax.ShapeDtypeStruct((S, D), F32), part_shape, part_shape, part_shape]
    if with_gate:
        in_specs += [til, row]
        arrays += list(gate_o)
        out_specs += [til, part]
        out_shape += [jax.ShapeDtypeStruct((S, D), BF16), part_shape]
    return _pcall(body, name=name, grid=(n_r,), in_specs=in_specs, out_specs=out_specs, out_shape=out_shape,
                  compiler_params=_params(("parallel",)))(*arrays)


def _pool_w_specs(rows, cg):
    return [pl.BlockSpec((rows, cg), lambda g, j=j: (N_GROUPS * j + g, 0)) for j in range(N_CHIPS)]


def _pool_fwd(proj, wp_full, pool_scale, S, PW):
    cg = PW // N_GROUPS
    rows = cg // N_CHIPS
    T = _tile(S, POOL_T)
    n_t = S // T

    def body(u_ref, w0, w1, w2, w3, ps_ref, pooled_ref, pa_ref):
        g = pl.program_id(0)
        win = jnp.left_shift(2, g)
        w = jnp.concatenate([w0[...], w1[...], w2[...], w3[...]], axis=0)
        t_i = lax.broadcasted_iota(jnp.int32, (T, T), 0)
        j_i = lax.broadcasted_iota(jnp.int32, (T, T), 1)
        b_cur = ((j_i <= t_i) & (j_i > t_i - win)).astype(BF16)
        b_prev = (j_i - T > t_i - win).astype(BF16)
        row = lax.broadcasted_iota(jnp.int32, (T, 1), 0)
        for r in range(n_t):
            cur = u_ref[r * T:(r + 1) * T, :]
            ws = jnp.dot(b_cur, cur, preferred_element_type=F32)
            if r > 0:
                ws += jnp.dot(b_prev, u_ref[(r - 1) * T:r * T, :], preferred_element_type=F32)
            count = jnp.minimum(row + (r * T + 1), win).astype(F32)
            pooled = (ws / count - cur.astype(F32)).astype(BF16)
            pooled_ref[r * T:(r + 1) * T, :] = pooled
            mixed = jnp.dot(pooled, w, preferred_element_type=F32)
            pa_ref[r * T:(r + 1) * T, :] = (mixed * ps_ref[...]).astype(BF16)

    col = pl.BlockSpec((S, cg), lambda g: (0, g))
    return _pcall(
        body, name="pool_fwd", grid=(N_GROUPS,),
        in_specs=[col] + _pool_w_specs(rows, cg) + [pl.BlockSpec((1, cg), lambda g: (0, g))],
        out_specs=[col, col],
        out_shape=[jax.ShapeDtypeStruct((S, PW), BF16), jax.ShapeDtypeStruct((S, PW), BF16)],
        compiler_params=_params(("parallel",)),
    )(proj, wp_full, wp_full, wp_full, wp_full, pool_scale)


def _pool_bwd(dpa, pooled, wp_full, pool_scale, S, PW):
    cg = PW // N_GROUPS
    rows = cg // N_CHIPS
    T = _tile(S, POOL_T)
    n_t = S // T

    def body(dpa_ref, pooled_ref, w0, w1, w2, w3, ps_ref, du_ref, gw_ref, gs_ref, dp_s, dpc_s, dmx_s):
        g = pl.program_id(0)
        win = jnp.left_shift(2, g)
        w = jnp.concatenate([w0[...], w1[...], w2[...], w3[...]], axis=0)
        row = lax.broadcasted_iota(jnp.int32, (T, 1), 0)
        gs = jnp.zeros((1, cg), F32)
        for r in range(n_t):
            sl = slice(r * T, (r + 1) * T)
            mixed = jnp.dot(pooled_ref[sl, :], w, preferred_element_type=F32)
            dpa_t = dpa_ref[sl, :]
            gs += _colsum(dpa_t * mixed)
            dmx = (dpa_t * ps_ref[...]).astype(BF16)
            dmx_s[sl, :] = dmx
            dpo = lax.dot_general(dmx, w, (((1,), (1,)), ((), ())), preferred_element_type=F32)
            dp_s[sl, :] = dpo
            count = jnp.minimum(row + (r * T + 1), win).astype(F32)
            dpc_s[sl, :] = (dpo / count).astype(BF16)
        gs_ref[...] = gs
        gw = lax.dot_general(pooled_ref[...], dmx_s[...], (((0,), (0,)), ((), ())), preferred_element_type=F32)
        for j in range(N_CHIPS):
            gw_ref[j, 0] = gw[j * rows:(j + 1) * rows, :].astype(BF16)
        j_i = lax.broadcasted_iota(jnp.int32, (T, T), 0)
        t_i = lax.broadcasted_iota(jnp.int32, (T, T), 1)
        b_cur = ((t_i >= j_i) & (t_i < j_i + win)).astype(BF16)
        b_next = (t_i + T < j_i + win).astype(BF16)
        for r in range(n_t):
            sl = slice(r * T, (r + 1) * T)
            acc = jnp.dot(b_cur, dpc_s[sl, :], preferred_element_type=F32)
            if r + 1 < n_t:
                acc += jnp.dot(b_next, dpc_s[(r + 1) * T:(r + 2) * T, :], preferred_element_type=F32)
            du_ref[sl, :] = (acc - dp_s[sl, :]).astype(BF16)

    col = pl.BlockSpec((S, cg), lambda g: (0, g))
    return _pcall(
        body, name="pool_bwd", grid=(N_GROUPS,),
        in_specs=[col, col] + _pool_w_specs(rows, cg) + [pl.BlockSpec((1, cg), lambda g: (0, g))],
        out_specs=[col, pl.BlockSpec((N_CHIPS, 1, rows, cg), lambda g: (0, g, 0, 0)),
                   pl.BlockSpec((1, cg), lambda g: (0, g))],
        out_shape=[jax.ShapeDtypeStruct((S, PW), BF16),
                   jax.ShapeDtypeStruct((N_CHIPS, N_GROUPS, rows, cg), BF16),
                   jax.ShapeDtypeStruct((1, PW), F32)],
        scratch_shapes=[pltpu.VMEM((S, cg), F32), pltpu.VMEM((S, cg), BF16), pltpu.VMEM((S, cg), BF16)],
        compiler_params=_params(("parallel",)),
    )(dpa, pooled, wp_full, wp_full, wp_full, wp_full, pool_scale)


_NT = (((1,), (1,)), ((), ()))
_TN = (((0,), (0,)), ((), ()))


def _split_dot(v, tri):
    hi = v.astype(BF16)
    lo = (v - hi.astype(F32)).astype(BF16)
    return jnp.dot(hi, tri, preferred_element_type=F32) + jnp.dot(lo, tri, preferred_element_type=F32)


LOG2E = 1.4426950408889634
QK_SCALE = 1.0 / math.sqrt(HEAD_DIM)


def _sb_scores(q2_i, k_j, tri_l, masked):
    tq, tk = q2_i.shape[0], k_j.shape[0]
    s = lax.dot_general(q2_i, k_j, _NT, preferred_element_type=F32)
    lp = jnp.log(1.0 + jnp.exp2(-jnp.abs(s))) * LOG2E
    lb = jnp.minimum(s, 0.0) - lp
    l = lb - s
    mask = None
    if masked:
        mask = lax.broadcasted_iota(jnp.int32, (tq, tk), 0) > lax.broadcasted_iota(jnp.int32, (tq, tk), 1)
        l = jnp.where(mask, l, 0.0)
    return l, lb, lb + _split_dot(l, tri_l), mask


def _sb_weights(t, carry_l, mask):
    a = jnp.exp2(t + carry_l)
    return a if mask is None else jnp.where(mask, a, 0.0)


def _rowsum(v):
    return jnp.sum(v, axis=1, keepdims=True)


def _qk_norm(x_ref, w_ref):
    xv = x_ref[...].astype(F32)
    r = lax.rsqrt(jnp.mean(xv * xv, axis=-1, keepdims=True) + EPS)
    return xv * r, r


def _attn_fwd(proj, q_norm_w, k_norm_w, S, H, q_off, riders=()):
    t = _tile(S, ATT_T)
    n_q = S // t

    def body(q_ref, k_ref, v_ref, qw_ref, kw_ref, att_ref, attf_ref, qn_s, kn_s):
        qh, _ = _qk_norm(q_ref, qw_ref)
        qn_s[...] = (qh * qw_ref[...] * (QK_SCALE * LOG2E)).astype(BF16)
        kh, _ = _qk_norm(k_ref, kw_ref)
        kn_s[...] = (kh * kw_ref[...]).astype(BF16)
        tri_l = (lax.broadcasted_iota(jnp.int32, (t, t), 0) > lax.broadcasted_iota(jnp.int32, (t, t), 1)).astype(BF16)

        def rows(j):
            return pl.ds(pl.multiple_of(j * t, t), t)

        def q_step(i, _):
            q_i = qn_s[rows(i), :]

            def av(a, j):
                return jnp.dot(a.astype(BF16), v_ref[rows(j), :], preferred_element_type=F32)

            l, _, tt, mask = _sb_scores(q_i, kn_s[rows(i), :], tri_l, True)
            acc = av(_sb_weights(tt, 0.0, mask), i)
            carry = _rowsum(l)

            def single(_, c):
                carry, acc = c
                l, _, tt, _ = _sb_scores(q_i, kn_s[rows(i - 1), :], tri_l, False)
                return carry + _rowsum(l), acc + av(_sb_weights(tt, carry, None), i - 1)

            carry, acc = lax.fori_loop(0, i % 2, single, (carry, acc))
            top = i - 1 - i % 2

            def pair(p, c):
                carry, acc = c
                j0 = top - 2 * p
                l0, _, t0, _ = _sb_scores(q_i, kn_s[rows(j0), :], tri_l, False)
                l1, _, t1, _ = _sb_scores(q_i, kn_s[rows(j0 - 1), :], tri_l, False)
                mid = carry + _rowsum(l0)
                acc = acc + av(_sb_weights(t0, carry, None), j0) + av(_sb_weights(t1, mid, None), j0 - 1)
                return mid + _rowsum(l1), acc

            _, acc = lax.fori_loop(0, i // 2, pair, (carry, acc))
            att_ref[rows(i), :] = acc.astype(BF16)
            attf_ref[rows(i), :] = acc
            return 0

        lax.fori_loop(0, n_q, q_step, 0)

    def col(off):
        return pl.BlockSpec((S, HEAD_DIM), lambda h, off=off: (0, off + h))

    wspec = pl.BlockSpec((1, HEAD_DIM), lambda h: (0, 0))
    return _ride(
        "attn_fwd", body, riders, [proj, proj, proj, q_norm_w, k_norm_w], grid=(H,),
        in_specs=[col(q_off), col(q_off + H), col(q_off + 2 * H), wspec, wspec],
        out_specs=[col(0), col(0)],
        out_shape=[jax.ShapeDtypeStruct((S, H * HEAD_DIM), BF16), jax.ShapeDtypeStruct((S, H * HEAD_DIM), F32)],
        scratch_shapes=[pltpu.VMEM((S, HEAD_DIM), BF16), pltpu.VMEM((S, HEAD_DIM), BF16)],
        sem=("parallel",))


def _attn_bwd(proj, datt, attf, q_norm_w, k_norm_w, S, H, q_off, riders=()):
    t = _tile(S, ATT_T)
    n_q = S // t

    def body(q_ref, k_ref, v_ref, do_ref, o_ref, qw_ref, kw_ref, dq_ref, dk_ref, dv_ref, gq_ref, gk_ref,
             qn_s, kn_s, qz_s, kz_s, dk_s, dv_s, gq_s):
        qw, kw = qw_ref[...], kw_ref[...]
        qh, _ = _qk_norm(q_ref, qw_ref)
        qn_s[...] = (qh * qw * (QK_SCALE * LOG2E)).astype(BF16)
        qz_s[...] = (qh * qw * QK_SCALE).astype(BF16)
        kh, _ = _qk_norm(k_ref, kw_ref)
        kn_s[...] = (kh * kw).astype(BF16)
        kz_s[...] = (kh * kw * QK_SCALE).astype(BF16)
        dk_s[...] = jnp.zeros_like(dk_s)
        dv_s[...] = jnp.zeros_like(dv_s)
        gq_s[...] = jnp.zeros_like(gq_s)
        r_i = lax.broadcasted_iota(jnp.int32, (t, t), 0)
        c_i = lax.broadcasted_iota(jnp.int32, (t, t), 1)
        tri_l = (r_i > c_i).astype(BF16)
        tri_e = (r_i >= c_i).astype(BF16)

        def rows(j):
            return pl.ds(pl.multiple_of(j * t, t), t)

        def q_step(i, _):
            q_i = qn_s[rows(i), :]
            do_i = do_ref[rows(i), :]
            d_i = _rowsum(do_i.astype(F32) * o_ref[rows(i), :])

            def scores(j, masked):
                l, lb, tt, mask = _sb_scores(q_i, kn_s[rows(j), :], tri_l, masked)
                da = lax.dot_general(do_i, v_ref[rows(j), :], _NT, preferred_element_type=F32)
                return l, lb, tt, mask, da

            def grads(j, sc, carry_l, carry_e, dq_acc):
                l, lb, tt, mask, da = sc
                a_bf = _sb_weights(tt, carry_l, mask).astype(BF16)
                e = da * a_bf.astype(F32)
                p = (d_i - carry_e) - _split_dot(e, tri_e)
                dz = e - jnp.exp2(lb) * (e + p)
                if mask is not None:
                    dz = jnp.where(mask, dz, 0.0)
                dz = dz.astype(BF16)
                dk_s[rows(j), :] += lax.dot_general(dz, qz_s[rows(i), :], _TN, preferred_element_type=F32)
                dv_s[rows(j), :] += lax.dot_general(a_bf, do_i, _TN, preferred_element_type=F32)
                return (carry_l + _rowsum(l), carry_e + _rowsum(e),
                        dq_acc + jnp.dot(dz, kz_s[rows(j), :], preferred_element_type=F32))

            zero = jnp.zeros((t, 1), F32)
            first = (zero, zero, jnp.zeros((t, HEAD_DIM), F32))

            def group(js, diagonal_first, c):
                scs = [scores(j, diagonal_first and n == 0) for n, j in enumerate(js)]
                for j, sc in zip(js, scs):
                    c = grads(j, sc, *c)
                return c

            n_first = i % ATT_GROUP
            c = lax.switch(n_first, [functools.partial(group, [i - u for u in range(n + 1)], True, first)
                                     for n in range(ATT_GROUP)])
            top = i - 1 - n_first

            def whole(p, c):
                j0 = top - ATT_GROUP * p
                return group([j0 - u for u in range(ATT_GROUP)], False, c)

            _, _, dqn = lax.fori_loop(0, (i - n_first) // ATT_GROUP, whole, c)
            qv = q_ref[rows(i), :].astype(F32)
            r = lax.rsqrt(jnp.mean(qv * qv, axis=-1, keepdims=True) + EPS)
            xh = qv * r
            gq_s[...] += _colsum(dqn * xh)
            dxh = dqn * qw
            dq_ref[rows(i), :] = (r * (dxh - xh * jnp.mean(dxh * xh, axis=-1, keepdims=True))).astype(BF16)
            return 0

        lax.fori_loop(0, n_q, q_step, 0)
        gq_ref[0] = gq_s[...]
        kh, rk = _qk_norm(k_ref, kw_ref)
        dkn = dk_s[...]
        gk_ref[0] = _colsum(dkn * kh)
        dxh = dkn * kw
        dk_ref[...] = (rk * (dxh - kh * jnp.mean(dxh * kh, axis=-1, keepdims=True))).astype(BF16)
        dv_ref[...] = dv_s[...].astype(BF16)

    def col(off):
        return pl.BlockSpec((S, HEAD_DIM), lambda h, off=off: (0, off + h))

    wspec = pl.BlockSpec((1, HEAD_DIM), lambda h: (0, 0))
    gspec = pl.BlockSpec((1, 1, HEAD_DIM), lambda h: (h, 0, 0))
    act = jax.ShapeDtypeStruct((S, H * HEAD_DIM), BF16)
    gsh = jax.ShapeDtypeStruct((H, 1, HEAD_DIM), F32)
    return _ride(
        "attn_bwd", body, riders, [proj, proj, proj, datt, attf, q_norm_w, k_norm_w], grid=(H,),
        in_specs=[col(q_off), col(q_off + H), col(q_off + 2 * H), col(0), col(0), wspec, wspec],
        out_specs=[col(0), col(0), col(0), gspec, gspec],
        out_shape=[act, act, act, gsh, gsh],
        scratch_shapes=[pltpu.VMEM((S, HEAD_DIM), BF16)] * 4 + [pltpu.VMEM((S, HEAD_DIM), F32)] * 2
        + [pltpu.VMEM((1, HEAD_DIM), F32)],
        sem=("parallel",))


def _place():
    x, y, c = lax.axis_index("x"), lax.axis_index("y"), lax.axis_index("c")
    chips = [(1 - x, y), (x, 1 - y), (1 - x, 1 - y)]
    return x, y, c, chips


def _dev_allgather(name, v):
    m_per, n = v.shape

    def body(x_ref, out_ref, send_sems, recv_sems, local_sem):
        x, y, c, chips = _place()
        me, sibling = (x, y, c), (x, y, 1 - c)

        def rows(px, py, pc):
            return out_ref.at[pl.ds((4 * px + 2 * py + pc) * m_per, m_per), :]

        def copy(k, block, to, src=None):
            return pltpu.make_async_remote_copy(
                src_ref=rows(*block) if src is None else src, dst_ref=rows(*block),
                send_sem=send_sems.at[k], recv_sem=recv_sems.at[k], device_id=to, device_id_type=MESH)

        mine = pltpu.make_async_copy(x_ref, rows(*me), local_sem)
        mine.start()
        first = [copy(0, me, sibling, src=x_ref)]
        first += [copy(1 + j, me, (*chip, c), src=x_ref) for j, chip in enumerate(chips)]
        for cp in first:
            cp.start()
        passed = [copy(4 + j, (*chip, c), sibling) for j, chip in enumerate(chips)]
        for j, chip in enumerate(chips):
            copy(1 + j, (*chip, c), me).wait_recv()
            passed[j].start()
        copy(0, sibling, me).wait_recv()
        for j, chip in enumerate(chips):
            copy(4 + j, (*chip, 1 - c), me).wait_recv()
        for cp in first + passed:
            cp.wait_send()
        mine.wait()

    return _pcall(
        body, name=name, out_shape=jax.ShapeDtypeStruct((N_DEV * m_per, n), v.dtype),
        in_specs=[pl.BlockSpec(memory_space=pltpu.VMEM)], out_specs=pl.BlockSpec(memory_space=pltpu.VMEM),
        scratch_shapes=[pltpu.SemaphoreType.DMA((7,)), pltpu.SemaphoreType.DMA((7,)), pltpu.SemaphoreType.DMA],
        compiler_params=pltpu.CompilerParams(vmem_limit_bytes=VMEM_LIMIT_V7X),
    )(v)


class _W:
    def __init__(self, name, kind, R, C):
        self.name, self.kind, self.R, self.C = name, kind, R, C

    @property
    def shard_shape(self):
        return (self.R, self.C // N_CHIPS) if self.kind == "col" else (self.R // N_CHIPS, self.C)

    @property
    def half_rows(self):
        return self.shard_shape[0] // 2

    def shard_half(self, ref, half):
        return ref.at[pl.ds(half * self.half_rows, self.half_rows), :]

    def region(self, full_ref, chip, half):
        hr = self.half_rows
        if self.kind == "col":
            cw = self.C // N_CHIPS
            return full_ref.at[pl.ds(half * hr, hr), pl.ds(chip * cw, cw)]
        return full_ref.at[pl.ds(chip * (2 * hr) + half * hr, hr), :]


def _ag_rider(ws, fulls, n_ch=4, chunks=None):
    n_w = len(ws)
    lo, hi = chunks or (0, n_ch)
    per = 6

    def parts(full, sems):
        send_sems, recv_sems = sems
        x, y, c, _ = _place()
        xn, yn, dg = (1 - x, y), (x, 1 - y), (1 - x, 1 - y)
        via = (x + (1 - c) * (1 - 2 * x), y + c * (1 - 2 * y))
        to = (x + c * (1 - 2 * x), y + (1 - c) * (1 - 2 * y))

        def reg(i, chip, half, t):
            nr = ws[i].half_rows // n_ch
            return ws[i].region(full[i], 2 * chip[0] + chip[1], half).at[pl.ds(t * nr, nr), :]

        def copy(r, i, t, k, dev):
            s = (i * (hi - lo) + t - lo) * per + k
            return pltpu.make_async_remote_copy(src_ref=r, dst_ref=r, send_sem=send_sems.at[s],
                                                recv_sem=recv_sems.at[s], device_id=dev, device_id_type=MESH)

        def direct(i, t, k):
            return copy(reg(i, (x, y), c, t), i, t, k, (*(via, to)[k], c))

        def direct_in(i, t, k):
            return copy(reg(i, (via, to)[k], c, t), i, t, k, (*(via, to)[k], c))

        def relay(i, t):
            return copy(reg(i, via, c, t), i, t, 2, (*to, c))

        def relay_in(i, t):
            return copy(reg(i, dg, c, t), i, t, 2, (*to, c))

        def hand(i, t, k, half):
            return copy(reg(i, (xn, yn, dg)[k], half, t), i, t, 3 + k, (x, y, 1 - c))

        return c, direct, direct_in, relay, relay_in, hand

    def start(_, full, sems):
        _, direct, _, _, _, _ = parts(full, sems)
        for t in range(lo, hi):
            for i in range(n_w):
                direct(i, t, 0).start()
                direct(i, t, 1).start()

    def arrived(t):
        def step(_, full, sems):
            c, _, direct_in, relay, relay_in, hand = parts(full, sems)
            for i in range(n_w):
                direct_in(i, t, 0).wait_recv()
                direct_in(i, t, 1).wait_recv()
                relay(i, t).start()
                hand(i, t, 0, c).start()
                hand(i, t, 1, c).start()
        return step

    def finish(_, full, sems):
        c, direct, _, relay, relay_in, hand = parts(full, sems)
        for t in range(lo, hi):
            for i in range(n_w):
                relay_in(i, t).wait_recv()
                hand(i, t, 2, c).start()
        for i in range(n_w):
            for t in range(lo, hi):
                for k in range(3):
                    hand(i, t, k, 1 - c).wait_recv()
        for i in range(n_w):
            for t in range(lo, hi):
                direct(i, t, 0).wait_send()
                direct(i, t, 1).wait_send()
                relay(i, t).wait_send()
                for k in range(3):
                    hand(i, t, k, c).wait_send()

    n_sem = per * (hi - lo) * n_w
    return _Rider(fulls, [jax.ShapeDtypeStruct((w.R, w.C), BF16) for w in ws],
                  [pltpu.SemaphoreType.DMA((n_sem,)), pltpu.SemaphoreType.DMA((n_sem,))], start, finish,
                  steps=[arrived(t) for t in range(lo, hi)], aliases={i: i for i in range(n_w)})


def _cast_into_full(w, a32, chip_arr):
    sr, sc = w.shard_shape
    tr, tc = _tile(sr, 512), _tile(sc, 2048)
    n_r, n_c = sr // tr, sc // tc
    if w.kind == "col":
        out_spec = pl.BlockSpec((tr, tc), lambda i, j, chip: (i, chip[0] * n_c + j))
    else:
        out_spec = pl.BlockSpec((tr, tc), lambda i, j, chip: (chip[0] * n_r + i, j))

    def body(chip_ref, a_ref, o_ref):
        o_ref[...] = a_ref[...].astype(BF16)

    return _pcall(
        body, name="cast_" + w.name, out_shape=jax.ShapeDtypeStruct((w.R, w.C), BF16),
        grid_spec=pltpu.PrefetchScalarGridSpec(
            num_scalar_prefetch=1, grid=(n_r, n_c),
            in_specs=[pl.BlockSpec((tr, tc), lambda i, j, chip: (i, j))], out_specs=out_spec),
        compiler_params=_params(("parallel", "parallel")),
    )(chip_arr, a32)


def _half_view(w, g):
    return g if w.kind == "col" else g.reshape(N_CHIPS, w.R // N_CHIPS, w.C)


def _px_rider(ws, grads):
    n_w = len(ws)

    def copies(g, got, sems):
        send_sems, recv_sems = sems
        x, y, c, _ = _place()

        def half_all(w, ref, half):
            hr = w.half_rows
            if w.kind == "col":
                return ref.at[pl.ds(half * hr, hr), :]
            return ref.at[:, pl.ds(half * hr, hr), :]

        return [pltpu.make_async_remote_copy(
            src_ref=half_all(w, g[i], 1 - c), dst_ref=got[i], send_sem=send_sems.at[i], recv_sem=recv_sems.at[i],
            device_id=(x, y, 1 - c), device_id_type=MESH) for i, w in enumerate(ws)]

    def start(g, got, sems):
        for cp in copies(g, got, sems):
            cp.start()

    def finish(g, got, sems):
        for cp in copies(g, got, sems):
            cp.wait_recv()
            cp.wait_send()

    def got_shape(w):
        hr = w.half_rows
        return (hr, w.C) if w.kind == "col" else (N_CHIPS, hr, w.C)

    return _Rider([_half_view(w, g) for w, g in zip(ws, grads)],
                  [jax.ShapeDtypeStruct(got_shape(w), BF16) for w in ws],
                  [pltpu.SemaphoreType.DMA((n_w,)), pltpu.SemaphoreType.DMA((n_w,))], start, finish)


def _pair_sum(w, g, got, c_arr):
    hr = w.half_rows
    if w.kind == "col":
        tr, tc = _tile(hr, 512), _tile(w.C, 2048)
        n_r = hr // tr
        grid = (n_r, w.C // tc)
        g_spec = pl.BlockSpec((tr, tc), lambda i, j, c: (c[0] * n_r + i, j))
        o_spec = pl.BlockSpec((tr, tc), lambda i, j, c: (i, j))
    else:
        tr = _tile(hr, 512)
        n_r = hr // tr
        grid = (N_CHIPS, n_r)
        g_spec = pl.BlockSpec((1, tr, w.C), lambda s, i, c: (s, c[0] * n_r + i, 0))
        o_spec = pl.BlockSpec((1, tr, w.C), lambda s, i, c: (s, i, 0))

    def body(c_ref, g_ref, got_ref, out_ref):
        out_ref[...] = (g_ref[...].astype(F32) + got_ref[...].astype(F32)).astype(BF16)

    return _pcall(
        body, name="grad_pair_sum_" + w.name, out_shape=jax.ShapeDtypeStruct(got.shape, BF16),
        grid_spec=pltpu.PrefetchScalarGridSpec(num_scalar_prefetch=1, grid=grid, in_specs=[g_spec, o_spec],
                                               out_specs=o_spec),
        compiler_params=_params(("parallel", "parallel")),
    )(c_arr, _half_view(w, g), got)


def _chip_sum(w, p, q, cc_arr):
    hr, cols = w.half_rows, w.shard_shape[1]
    tr, tc = _tile(hr, 512), _tile(cols, 2048)
    n_r, n_c = hr // tr, cols // tc

    def body(cc_ref, own, q1, q2, q3, out_ref):
        own_v = own[...] if w.kind == "col" else own[0]
        out_ref[...] = ((own_v.astype(F32) + q1[0].astype(F32)) + q2[0].astype(F32)) + q3[0].astype(F32)

    if w.kind == "col":
        own_spec = pl.BlockSpec((tr, tc), lambda i, j, cc: (i, cc[1] * n_c + j))
    else:
        own_spec = pl.BlockSpec((1, tr, tc), lambda i, j, cc: (cc[1], i, j))
    q_specs = [pl.BlockSpec((1, tr, tc), lambda i, j, cc, s=s: ((cc[1] + s) % N_CHIPS, i, j)) for s in (1, 2, 3)]
    return _pcall(
        body, name="grad_chip_sum_" + w.name, out_shape=jax.ShapeDtypeStruct(w.shard_shape, F32),
        grid_spec=pltpu.PrefetchScalarGridSpec(
            num_scalar_prefetch=1, grid=(n_r, n_c), in_specs=[own_spec] + q_specs,
            out_specs=pl.BlockSpec((tr, tc), lambda i, j, cc: (cc[0] * n_r + i, j))),
        compiler_params=_params(("parallel", "parallel")),
    )(cc_arr, p, q, q, q)


_SEM = pl.BlockSpec(memory_space=pltpu.SEMAPHORE)
_HBM = pl.BlockSpec(memory_space=pltpu.HBM)


def _split_copies(kind, ws, p, land, send_sems, recv_sems):
    x, y, c, chips = _place()
    my_chip = 2 * x + y
    pairs = []
    for i, w in enumerate(ws):
        if kind == "pair":
            hr = w.half_rows
            src = p[i].at[pl.ds((1 - c) * hr, hr), :] if w.kind == "col" else p[i].at[:, pl.ds((1 - c) * hr, hr), :]
            cp = pltpu.make_async_remote_copy(src_ref=src, dst_ref=land[i], send_sem=send_sems.at[i],
                                              recv_sem=recv_sems.at[i], device_id=(x, y, 1 - c), device_id_type=MESH)
            pairs.append((cp, cp))
            continue
        for k, chip in enumerate(chips):
            to_chip = 2 * chip[0] + chip[1]
            src = p[i].at[:, pl.ds(to_chip * (w.C // N_CHIPS), w.C // N_CHIPS)] if w.kind == "col" else p[i].at[to_chip]
            kw = dict(send_sem=send_sems.at[3 * i + k], recv_sem=recv_sems.at[3 * i + k], device_id=(*chip, c),
                      device_id_type=MESH)
            pairs.append((pltpu.make_async_remote_copy(src_ref=src, dst_ref=land[i].at[my_chip], **kw),
                          pltpu.make_async_remote_copy(src_ref=src, dst_ref=land[i].at[to_chip], **kw)))
    return pairs


def _split_start(name, kind, ws, arrays):
    n_w = len(ws)
    if kind == "pair":
        arrays = [_half_view(w, g) for w, g in zip(ws, arrays)]
        lands = [lax.empty((w.half_rows, w.C) if w.kind == "col" else (N_CHIPS, w.half_rows, w.C), BF16) for w in ws]
    else:
        lands = [lax.empty((N_CHIPS, w.half_rows, w.shard_shape[1]), BF16) for w in ws]
    n_sem = n_w if kind == "pair" else 3 * n_w

    def body(*refs):
        p, land = refs[:n_w], refs[n_w:2 * n_w]
        for out, _ in _split_copies(kind, ws, p, land, refs[2 * n_w], refs[2 * n_w + 1]):
            out.start()
        refs[-1][...] = jnp.zeros_like(refs[-1])

    arrays = [pltpu.with_memory_space_constraint(a, pltpu.HBM) for a in list(arrays) + lands]
    res = _pcall(
        body, name=name,
        out_shape=(pltpu.SemaphoreType.DMA((n_sem,)), pltpu.SemaphoreType.DMA((n_sem,)),
                   *[pltpu.HBM(a.shape, a.dtype) for a in arrays], jax.ShapeDtypeStruct((SUBLANES, LANES), F32)),
        in_specs=[_HBM] * (2 * n_w),
        out_specs=(_SEM, _SEM, *[_HBM] * (2 * n_w), pl.BlockSpec(memory_space=pltpu.VMEM)),
        input_output_aliases={i: 2 + i for i in range(2 * n_w)},
        compiler_params=pltpu.CompilerParams(has_side_effects=pltpu.SideEffectType.DATAFLOW_SIDE_EFFECTING),
    )(*arrays)
    return (kind, ws, res[0], res[1], list(res[2:2 + n_w]), list(res[2 + n_w:2 + 2 * n_w])), res[-1]


def _split_wait(name, flight, after):
    kind, ws, send_sems, recv_sems, arrays, lands = flight
    n_w = len(ws)

    def body(*refs):
        p, land = refs[:n_w], refs[n_w:2 * n_w]
        for _, cp in _split_copies(kind, ws, p, land, refs[2 * n_w], refs[2 * n_w + 1]):
            cp.wait_send()
            cp.wait_recv()

    res = _pcall(
        body, name=name,
        out_shape=[pltpu.HBM(a.shape, a.dtype) for a in list(arrays) + list(lands)],
        in_specs=[_HBM] * (2 * n_w) + [_SEM, _SEM] + [ANY] * len(after), out_specs=[_HBM] * (2 * n_w),
        input_output_aliases={i: i for i in range(2 * n_w)},
        compiler_params=pltpu.CompilerParams(has_side_effects=pltpu.SideEffectType.DATAFLOW_SIDE_EFFECTING),
    )(*arrays, *lands, send_sems, recv_sems, *after)
    return list(res[:n_w]), list(res[n_w:])


def _sf_rider(ws, grads):
    n_w = len(ws)

    def copy(g, sems, i, half):
        send_sems, recv_sems = sems
        x, y, c, _ = _place()
        h = c if half == "mine" else 1 - c
        reg = ws[i].shard_half(g[i], h)
        return pltpu.make_async_remote_copy(src_ref=reg, dst_ref=reg, send_sem=send_sems.at[i], recv_sem=recv_sems.at[i],
                                            device_id=(x, y, 1 - c), device_id_type=MESH)

    def start(_, g, sems):
        for i in range(n_w):
            copy(g, sems, i, "mine").start()

    def finish(_, g, sems):
        for i in range(n_w):
            copy(g, sems, i, "other").wait_recv()
            copy(g, sems, i, "mine").wait_send()

    return _Rider(grads, [jax.ShapeDtypeStruct(w.shard_shape, F32) for w in ws],
                  [pltpu.SemaphoreType.DMA((n_w,)), pltpu.SemaphoreType.DMA((n_w,))], start, finish,
                  aliases={i: i for i in range(n_w)})


def _adamw_math(w, g, m, v):
    m = ADAM_B1 * m + (1.0 - ADAM_B1) * g
    v = ADAM_B2 * v + (1.0 - ADAM_B2) * (g * g)
    m_hat = m / (1.0 - ADAM_B1 ** ADAM_STEP)
    v_hat = v / (1.0 - ADAM_B2 ** ADAM_STEP)
    delta = -ADAM_LR * (m_hat / (jnp.sqrt(v_hat) + ADAM_EPS) + ADAM_WD * w)
    return delta, m, v


def _adamw(name, w, g, m, v, after=None):
    R, C = w.shape
    tr, tc = _tile(R, 256), _tile(C, 2048)
    behind = [] if after is None else [after]

    def body(w_ref, g_ref, m_ref, v_ref, *rest):
        g_out, d_out, m_out, v_out = rest[len(behind):]
        g = g_ref[...]
        g_out[...] = g
        d_out[...], m_out[...], v_out[...] = _adamw_math(w_ref[...], g, m_ref[...], v_ref[...])

    spec = pl.BlockSpec((tr, tc), lambda i, j: (i, j))
    sh = jax.ShapeDtypeStruct((R, C), F32)
    return _pcall(body, name=name, grid=(R // tr, C // tc), in_specs=[spec] * 4 + [ANY] * len(behind),
                  out_specs=[spec] * 4, out_shape=[sh] * 4, compiler_params=_params(("parallel", "parallel")))(
                      w, g, m, v, *behind)


def _ada_update(sct, dmod_sh, w, m, v, riders=()):
    R, C = w.shape
    tr, tc = _tile(R, 256), _tile(C, 1024)

    def body(s_ref, d_ref, w_ref, m_ref, v_ref, g_out, d_out, m_out, v_out):
        s, d = s_ref[...], d_ref[...]
        g = s[:, 0:1] * d[0:1, :]
        for b in range(1, N_DEV):
            g += s[:, b:b + 1] * d[b:b + 1, :]
        g_out[...] = g
        d_out[...], m_out[...], v_out[...] = _adamw_math(w_ref[...], g, m_ref[...], v_ref[...])

    spec = pl.BlockSpec((tr, tc), lambda i, j: (i, j))
    sh = jax.ShapeDtypeStruct((R, C), F32)
    return _ride(
        "ada_update", body, riders, [sct, dmod_sh, w, m, v], grid=(R // tr, C // tc),
        in_specs=[pl.BlockSpec((tr, N_DEV), lambda i, j: (i, 0)), pl.BlockSpec((N_DEV, tc), lambda i, j: (0, j)),
                  spec, spec, spec],
        out_specs=[spec] * 4, out_shape=[sh] * 4, scratch_shapes=[], sem=("parallel", "parallel"))


def _silu_rows(c_row):
    D = c_row.shape[1]

    def body(c_ref, o_ref):
        cv = c_ref[...]
        o_ref[...] = cv * jax.nn.sigmoid(cv)

    return _pcall(body, name="silu_c", out_shape=jax.ShapeDtypeStruct((1, D), F32))(c_row)


def _pack_partials(parts, widths, total):
    n = len(widths)

    def body(*refs):
        loss_p, out_ref = refs[n], refs[n + 1]
        off = 0
        for ref, wd in zip(refs[:n], widths):
            out_ref[:, off:off + wd] = jnp.sum(ref[...], axis=0)
            off += wd
        loss = jnp.sum(jnp.sum(loss_p[...], axis=0), axis=1, keepdims=True)
        out_ref[:, off:off + LANES] = jnp.broadcast_to(loss, (1, LANES))
        if off + LANES < total:
            out_ref[:, off + LANES:total] = jnp.zeros((1, total - off - LANES), F32)

    return _pcall(body, name="pack_partials", out_shape=jax.ShapeDtypeStruct((1, total), F32))(*parts)


def _small_update(gathered, offsets, params, loss_off):
    n_p = len(params)

    def over_devices(g_ref, off, wd):
        blk = g_ref[:, off:off + wd]
        g = blk[0:1, :]
        for b in range(1, N_DEV):
            g = g + blk[b:b + 1, :]
        return g

    def body(*refs):
        g_ref = refs[0]
        prm = refs[1:1 + 3 * n_p]
        outs = refs[1 + 3 * n_p:]
        outs[4 * n_p][...] = over_devices(g_ref, loss_off, LANES)
        for i, (off, wd) in enumerate(offsets):
            g = over_devices(g_ref, off, wd)
            w, m, v = prm[3 * i][...], prm[3 * i + 1][...], prm[3 * i + 2][...]
            outs[4 * i][...] = g
            outs[4 * i + 1][...], outs[4 * i + 2][...], outs[4 * i + 3][...] = _adamw_math(w, g, m, v)

    flat = [a for t in params for a in t]
    out_shape = [jax.ShapeDtypeStruct(t[0].shape, F32) for t in params for _ in range(4)]
    out_shape.append(jax.ShapeDtypeStruct((1, LANES), F32))
    return _pcall(body, name="small_update", out_shape=out_shape)(gathered, *flat)


def kernel(x, c, w_ada, b_ada, norm1_w, w_in, q_norm_w, k_norm_w, w_pool, pool_scale, w_a_up, w_b_up, w_o, norm2_w, w_ff1, w_ff2, loss_target, m_w_ada, m_b_ada, m_norm1_w, m_w_in, m_q_norm_w, m_k_norm_w, m_w_pool, m_pool_scale, m_w_a_up, m_w_b_up, m_w_o, m_norm2_w, m_w_ff1, m_w_ff2, v_w_ada, v_b_ada, v_norm1_w, v_w_in, v_q_norm_w, v_k_norm_w, v_w_pool, v_pool_scale, v_w_a_up, v_w_b_up, v_w_o, v_norm2_w, v_w_ff1, v_w_ff2):
    _, S, D = x.shape
    PW = D // 2
    H = PW // HEAD_DIM
    cg = PW // N_GROUPS
    IN = w_in.shape[2] * N_CHIPS
    FF = w_ff1.shape[2] * N_CHIPS
    A_COLS = w_ada.shape[2]
    xi, yi, ci = lax.axis_index("x"), lax.axis_index("y"), lax.axis_index("c")
    chip = 2 * xi + yi
    dev = 2 * chip + ci
    c_arr = jnp.reshape(ci, (1,)).astype(jnp.int32)
    x2, tgt = x[0], loss_target[0]

    ws = [_W("w_in", "col", D, IN), _W("w_pool", "row", PW, cg), _W("w_a_up", "col", PW, D),
          _W("w_b_up", "col", PW, D), _W("w_o", "row", D, D), _W("w_ff1", "col", D, FF), _W("w_ff2", "row", FF, D)]
    w32 = [w_in[0], w_pool[0].reshape(cg, cg), w_a_up[0], w_b_up[0], w_o[0], w_ff1[0], w_ff2[0]]
    m32 = [m_w_in[0], m_w_pool[0].reshape(cg, cg), m_w_a_up[0], m_w_b_up[0], m_w_o[0], m_w_ff1[0], m_w_ff2[0]]
    v32 = [v_w_in[0], v_w_pool[0].reshape(cg, cg), v_w_a_up[0], v_w_b_up[0], v_w_o[0], v_w_ff1[0], v_w_ff2[0]]

    W_IN, W_POOL, W_A, W_B, W_O, W_FF1, W_FF2 = ws
    chip_arr = jnp.reshape(chip, (1,)).astype(jnp.int32)
    cc_arr = jnp.stack([ci, chip]).astype(jnp.int32)
    s_in, s_pool, s_a, s_b, s_o, s_ff1, s_ff2 = [_cast_into_full(w, a, chip_arr) for w, a in zip(ws, w32)]
    (win_f,) = _run_rider("gather_w_in", _ag_rider([W_IN], [s_in]))

    sc_row = _silu_rows(c)
    sc_all = _dev_allgather("gather_silu_c", sc_row.reshape(SUBLANES, D // SUBLANES)).reshape(N_DEV, D)
    sc16 = jnp.concatenate([sc_all, jnp.zeros_like(sc_all)], axis=0)
    b_cols = lax.dynamic_slice(b_ada, (0, chip * A_COLS), (1, A_COLS))
    (mod_cols,) = _mm("mod_cols", [(sc16, w_ada[0])], M=2 * N_DEV, N=A_COLS, K=D, tm=16, tn=1024, tk=1024,
                      a_pro=lambda a: a.astype(BF16), b_pro=lambda b: b.astype(BF16),
                      extras=[(b_cols, "row", 0)], outs=[_tile_out(F32)], epi=lambda accs, ex: [accs[0] + ex[0]])
    mod_all = _dev_allgather("gather_mod", mod_cols[:N_DEV]).reshape(N_CHIPS, 2, N_DEV, A_COLS)
    mod_row = lax.dynamic_index_in_dim(mod_all[:, 0], dev, axis=1, keepdims=False).reshape(1, N_CHIPS * A_COLS)
    shift1, scale1, gate1, shift2, scale2, gate2 = [mod_row[:, i * D:(i + 1) * D] for i in range(6)]

    WIDE = dict(tm=2048, tn=512, tk=2048)
    DEEP = dict(tm=1024, tn=1024, tk=1024)
    h = _norm_mod("norm1_mod", x2, norm1_w, scale1, shift1)
    (proj,), ((wpool_f, wa_f, wb_f, wo_f),) = _mm(
        "in_proj", [(h, win_f)], M=S, N=IN, K=D, outs=[_tile_out(BF16)], epi=lambda accs, ex: [accs[0]], **WIDE,
        riders=[_ag_rider([W_POOL, W_A, W_B, W_O], [s_pool, s_a, s_b, s_o], n_ch=2)])
    pooled, pa = _pool_fwd(proj, wpool_f, pool_scale, S, PW)
    (att, attf), ((wff1_f,),) = _attn_fwd(proj, q_norm_w, k_norm_w, S, H, PW // HEAD_DIM,
                                          riders=[_ag_rider([W_FF1], [s_ff1])])

    def merge_epi(accs, ex):
        sa, sb = jax.nn.sigmoid(ex[0].astype(F32)), jax.nn.sigmoid(ex[1].astype(F32))
        return [sa * accs[0] + sb * accs[1], accs[0], accs[1]]

    (merged, ya, yb), (ff2_a,) = _mm("branch_up_merge", [(pa, wa_f), (att, wb_f)], M=S, N=D, K=PW,
                                     extras=[(proj, "tile", 4 * PW), (proj, "tile", 4 * PW + D)],
                                     outs=[_tile_out(BF16)] * 3, epi=merge_epi,
                                     riders=[_ag_rider([W_FF2], [s_ff2], chunks=(0, 1))])
    (x1, o), (ff2_b,) = _mm("out_proj", [(merged, wo_f)], M=S, N=D, K=D, extras=[(x2, "tile", 0), (gate1, "row", 0)],
                            outs=[_tile_out(F32), _tile_out(BF16)], epi=lambda accs, ex: [ex[0] + ex[1] * accs[0], accs[0]],
                            riders=[_ag_rider([W_FF2], ff2_a, chunks=(1, 2))], **WIDE)
    h2 = _norm_mod("norm2_mod", x1, norm2_w, scale2, shift2)
    (rl,), ((wff2_f,),) = _mm("ff1", [(h2, wff1_f)], M=S, N=FF, K=D, outs=[_tile_out(BF16)], **WIDE,
                              epi=lambda accs, ex: [jnp.maximum(accs[0], 0.0)],
                              riders=[_ag_rider([W_FF2], ff2_b, chunks=(2, 4))])

    def square(a):
        af = a.astype(F32)
        return (af * af).astype(BF16)

    def loss_epi(accs, ex):
        x1_t, tgt_t, g2 = ex
        f = accs[0]
        diff = (x1_t + g2 * f) - tgt_t
        dy = diff * (1.0 / D)
        return [dy, dy * g2, _colsum(dy * f), _colsum(diff * diff)]

    dy, df, dgate2_p, loss_p = _mm("ff2_loss", [(rl, wff2_f)], M=S, N=D, K=FF, a_pro=square, **DEEP,
                                   extras=[(x1, "tile", 0), (tgt, "tile", 0), (gate2, "row", 0)],
                                   outs=[_tile_out(F32), _tile_out(BF16), _COLSUM, _COLSUM], epi=loss_epi)

    tied = []

    def behind(token, a):
        a, token = lax.optimization_barrier((a, token))
        tied.append(token)
        return a

    def pair_sums(group, partials, got):
        return [_pair_sum(w, g, r, c_arr) for w, g, r in zip(group, partials, got)]

    def chip_sums(group, sums, from_chips):
        return [_chip_sum(w, p, q, cc_arr) for w, p, q in zip(group, sums, from_chips)]

    first = lambda accs, ex: [accs[0]]
    gmm = dict(ta=True, outs=[_tile_out(BF16)], epi=first, **WIDE)
    (g_ff2,) = _mm("grad_w_ff2", [(rl, df)], M=FF, N=D, K=S, a_pro=square, ta=True, tm=512, tn=2048, tk=2048,
                   outs=[_tile_out(BF16)], epi=first)
    flight, token = _split_start("pair_w_ff2_start", "pair", [W_FF2], [g_ff2])
    (dz1,) = _mm("d_ff_hidden", [(behind(token, df), wff2_f)], M=S, N=FF, K=D, tb=True, extras=[(rl, "tile", 0)],
                 outs=[_tile_out(BF16)], epi=lambda accs, ex: [accs[0] * (2.0 * ex[0].astype(F32))], **WIDE)
    sum_ff2 = pair_sums([W_FF2], *_split_wait("pair_w_ff2_wait", flight, after=[dz1] + tied))
    chip_ff2, token = _split_start("chip_w_ff2_start", "chip", [W_FF2], sum_ff2)
    (g_ff1,) = _mm("grad_w_ff1", [(behind(token, h2), dz1)], M=D, N=FF, K=S, **gmm)
    flight, token = _split_start("pair_w_ff1_start", "pair", [W_FF1], [g_ff1])
    (dh2,) = _mm("d_h2", [(behind(token, dz1), wff1_f)], M=S, N=D, K=FF, tb=True, outs=[_tile_out(F32)], epi=first, **DEEP)
    sum_ff1 = pair_sums([W_FF1], *_split_wait("pair_w_ff1_wait", flight, after=[dh2] + tied))
    chip_ff1, token = _split_start("chip_w_ff1_start", "chip", [W_FF1], sum_ff1)
    dx1, dshift2_p, dscale2_p, gn2_p, do, dgate1_p = _norm_mod_bwd("norm2_bwd", behind(token, dh2), x1, dy, norm2_w, scale2,
                                                                   gate_o=(o, gate1))
    (g_wo,) = _mm("grad_w_o", [(merged, do)], M=D, N=D, K=S, **gmm)

    def gate_epi(accs, ex):
        dm = accs[0]
        sa, sb = jax.nn.sigmoid(ex[0].astype(F32)), jax.nn.sigmoid(ex[1].astype(F32))
        ya_t, yb_t = ex[2].astype(F32), ex[3].astype(F32)
        return [dm * sa, dm * sb, dm * ya_t * (sa * (1.0 - sa)), dm * yb_t * (sb * (1.0 - sb))]

    dya, dyb, dga, dgb = _mm("d_merged", [(do, wo_f)], M=S, N=D, K=D, tb=True, tm=1024, tn=512, tk=2048,
                             extras=[(proj, "tile", 4 * PW), (proj, "tile", 4 * PW + D), (ya, "tile", 0), (yb, "tile", 0)],
                             outs=[_tile_out(BF16)] * 4, epi=gate_epi)
    (g_wa,) = _mm("grad_w_a_up", [(pa, dya)], M=PW, N=D, K=S, **gmm)
    (g_wb,) = _mm("grad_w_b_up", [(att, dyb)], M=PW, N=D, K=S, **gmm)
    (dpa,) = _mm("d_pool_out", [(dya, wa_f)], M=S, N=PW, K=D, tb=True, outs=[_tile_out(F32)], epi=first, **WIDE)
    mid = [W_A, W_B, W_O]
    flight, token = _split_start("pair_mid_start", "pair", mid, [g_wa, g_wb, g_wo])
    (datt,) = _mm("d_att", [(behind(token, dyb), wb_f)], M=S, N=PW, K=D, tb=True, outs=[_tile_out(BF16)], epi=first, **WIDE)
    sum_mid = pair_sums(mid, *_split_wait("pair_mid_wait", flight, after=[datt] + tied))
    chip_mid, token = _split_start("chip_mid_start", "chip", mid, sum_mid)
    du, g_wpool4, gscale_p = _pool_bwd(dpa, pooled, wpool_f, pool_scale, S, PW)
    dq, dk, dv, gq_p, gk_p = _attn_bwd(proj, behind(token, datt), attf, q_norm_w, k_norm_w, S, H, PW // HEAD_DIM)
    dproj = jnp.concatenate([du, dq, dk, dv, dga, dgb], axis=1)
    early = [W_FF1, W_FF2]
    sum_ff1, q_ff1 = _split_wait("chip_w_ff1_wait", chip_ff1, after=[dq] + tied)
    sum_ff2, q_ff2 = _split_wait("chip_w_ff2_wait", chip_ff2, after=[dq] + tied)
    halves_early = chip_sums(early, sum_ff1 + sum_ff2, q_ff1 + q_ff2)
    (g_win,), (grads_early,) = _mm("grad_w_in", [(h, dproj)], M=D, N=IN, K=S, riders=[_sf_rider(early, halves_early)],
                                   **gmm)
    last = [W_IN, W_POOL]
    g_last = [g_win, g_wpool4.reshape(PW, cg)]
    sum_mid, q_mid = _split_wait("chip_mid_wait", chip_mid, after=[g_win] + tied)
    halves_mid = chip_sums(mid, sum_mid, q_mid)
    (dh,), (got_last, grads_mid) = _mm("d_h", [(dproj, win_f)], M=S, N=D, K=IN, tb=True, outs=[_tile_out(F32)], epi=first,
                                       riders=[_px_rider(last, g_last), _sf_rider(mid, halves_mid)], **DEEP)
    sum_last = pair_sums(last, g_last, got_last)
    grad_x, dshift1_p, dscale1_p, gn1_p = _norm_mod_bwd("norm1_bwd", dh, x2, dx1, norm1_w, scale1)

    parts = [dshift1_p, dscale1_p, dgate1_p, dshift2_p, dscale2_p, dgate2_p, gn1_p, gn2_p,
             gscale_p.reshape(1, 1, PW), gq_p, gk_p]
    widths = [D] * 8 + [PW, HEAD_DIM, HEAD_DIM]
    used = sum(widths)
    P = -(-(used + LANES) // (SUBLANES * LANES)) * (SUBLANES * LANES)
    packed = _pack_partials(parts + [loss_p], widths, P)
    gathered = _dev_allgather("gather_vector_grads", packed.reshape(SUBLANES, P // SUBLANES)).reshape(N_DEV, P)
    sum_last, gathered = lax.optimization_barrier((sum_last, gathered))
    chip_last, token = _split_start("chip_last_start", "chip", last, sum_last)
    small = [(b_ada, m_b_ada, v_b_ada), (norm1_w, m_norm1_w, v_norm1_w), (norm2_w, m_norm2_w, v_norm2_w),
             (pool_scale, m_pool_scale, v_pool_scale), (q_norm_w, m_q_norm_w, v_q_norm_w),
             (k_norm_w, m_k_norm_w, v_k_norm_w)]
    offsets = [(0, 6 * D), (6 * D, D), (7 * D, D), (8 * D, PW), (8 * D + PW, HEAD_DIM), (8 * D + PW + HEAD_DIM, HEAD_DIM)]
    su = _small_update(gathered, offsets, small, used)
    (g_b, d_b, nm_b, nv_b, g_n1, d_n1, nm_n1, nv_n1, g_n2, d_n2, nm_n2, nv_n2, g_ps, d_ps, nm_ps, nv_ps,
     g_qn, d_qn, nm_qn, nv_qn, g_kn, d_kn, nm_kn, nv_kn, loss_sum) = su
    dmod_sh = lax.dynamic_slice(gathered, (0, chip * A_COLS), (N_DEV, A_COLS))
    dmod_sh, token = lax.optimization_barrier((dmod_sh, token))
    g_ada, d_ada, nm_ada, nv_ada = _ada_update(sc_all.T, dmod_sh, w_ada[0], m_w_ada[0], v_w_ada[0])

    upd_done = [_adamw("adamw_" + w.name, a, g, m, v, after=token)
                for w, a, g, m, v in zip(ws[2:], w32[2:], list(grads_mid) + list(grads_early), m32[2:], v32[2:])]

    sum_last, q_last = _split_wait("chip_last_wait", chip_last, after=[nv_ada] + [u[3] for u in upd_done])
    halves_last = chip_sums(last, sum_last, q_last)
    filled = _run_rider("grad_sibling_fill", _sf_rider(last, halves_last))
    upd = [_adamw("adamw_" + w.name, a, g, m, v) for w, a, g, m, v in zip(ws[:2], w32[:2], filled, m32[:2], v32[:2])]
    upd += upd_done

    loss = (0.5 / D) * loss_sum[0, 0]

    def up(a):
        return a[None]

    def pool4(a):
        return a.reshape(1, N_GROUPS, cg // N_CHIPS, cg)

    (gr_win, d_win, nm_win, nv_win), (gr_wp, d_wp, nm_wp, nv_wp), (gr_wa, d_wa, nm_wa, nv_wa), \
        (gr_wb, d_wb, nm_wb, nv_wb), (gr_wo, d_wo, nm_wo, nv_wo), (gr_f1, d_f1, nm_f1, nv_f1), \
        (gr_f2, d_f2, nm_f2, nv_f2) = upd
    return (
        loss, grad_x[None],
        up(g_ada), g_b, g_n1, up(gr_win), g_qn, g_kn, pool4(gr_wp), g_ps, up(gr_wa), up(gr_wb), up(gr_wo), g_n2,
        up(gr_f1), up(gr_f2),
        up(d_ada), d_b, d_n1, up(d_win), d_qn, d_kn, pool4(d_wp), d_ps, up(d_wa), up(d_wb), up(d_wo), d_n2,
        up(d_f1), up(d_f2),
        up(nm_ada), nm_b, nm_n1, up(nm_win), nm_qn, nm_kn, pool4(nm_wp), nm_ps, up(nm_wa), up(nm_wb), up(nm_wo), nm_n2,
        up(nm_f1), up(nm_f2),
        up(nv_ada), nv_b, nv_n1, up(nv_win), nv_qn, nv_kn, pool4(nv_wp), nv_ps, up(nv_wa), up(nv_wb), up(nv_wo), nv_n2,
        up(nv_f1), up(nv_f2),
    )
```

```python
import functools
import math

import jax
import jax.numpy as jnp
from jax import lax
from jax.experimental import pallas as pl
from jax.experimental.pallas import tpu as pltpu

F32 = jnp.float32
BF16 = jnp.bfloat16
MESH = pl.DeviceIdType.MESH
ANY = pl.BlockSpec(memory_space=pl.ANY)

EPS = 1e-6
HEAD_DIM = 128
LANES, SUBLANES = 128, 8
POOL_WINDOWS = (2, 4, 8, 16)
N_GROUPS = len(POOL_WINDOWS)
assert POOL_WINDOWS == tuple(2 << g for g in range(N_GROUPS))
N_CHIPS = 4
N_DEV = 8
ADAM_LR, ADAM_B1, ADAM_B2, ADAM_EPS, ADAM_WD, ADAM_STEP = 0.001, 0.9, 0.999, 1e-08, 0.01, 10
VMEM_LIMIT_V7X = 56 * 1024 * 1024
ATT_T = 256
ATT_GROUP = 4
POOL_T = 256


def _pcall(body, **kw):
    return pl.pallas_call(body, **kw)


def _params(sem=None):
    return pltpu.CompilerParams(dimension_semantics=sem, vmem_limit_bytes=VMEM_LIMIT_V7X)


def _tile(n, pref):
    if n <= pref:
        return n
    t = pref
    while n % t:
        t //= 2
    return t


class _Rider:
    def __init__(self, arrays, out_shape, sems, start, finish, aliases=None, steps=()):
        self.arrays, self.out_shape, self.sems = list(arrays), list(out_shape), list(sems)
        self.start, self.finish, self.aliases, self.steps = start, finish, aliases or {}, list(steps)


def _ride(name, body, riders, arrays, *, grid, in_specs, out_specs, out_shape, scratch_shapes, sem):
    n_in, n_out, n_scr = len(arrays), len(out_shape), len(scratch_shapes)
    r_arrays = [a for r in riders for a in r.arrays]
    r_outs = [o for r in riders for o in r.out_shape]
    r_sems = [s for r in riders for s in r.sems]
    n_hooks = max([len(r.steps) for r in riders], default=0)
    total = math.prod(grid)
    aliases, off_i, off_o = {}, n_in, n_out
    for r in riders:
        for a, o in r.aliases.items():
            aliases[off_i + a] = off_o + o
        off_i += len(r.arrays)
        off_o += len(r.out_shape)

    def full(*refs):
        p = 0
        groups = []
        for n in (n_in, len(r_arrays), n_out, len(r_outs), n_scr, len(r_sems)):
            groups.append(refs[p:p + n])
            p += n
        ins, rin, outs, rout, scr, rsem = groups

        def each(what):
            a = o = s = 0
            for r in riders:
                fn = what(r)
                if fn is not None:
                    fn(rin[a:a + len(r.arrays)], rout[o:o + len(r.out_shape)], rsem[s:s + len(r.sems)])
                a, o, s = a + len(r.arrays), o + len(r.out_shape), s + len(r.sems)

        if riders:
            lin = 0
            for d, g in enumerate(grid):
                lin = lin * g + pl.program_id(d)
            pl.when(lin == 0)(lambda: each(lambda r: r.start))
            for t in range(n_hooks):
                pl.when(lin == min(total - 1, ((t + 1) * total) // n_hooks))(
                    lambda t=t: each(lambda r: r.steps[t] if t < len(r.steps) else None))
        body(*ins, *outs, *scr)
        if riders:
            pl.when(lin == total - 1)(lambda: each(lambda r: r.finish))

    res = _pcall(
        full, name=name, grid=grid, in_specs=list(in_specs) + [ANY] * len(r_arrays),
        out_specs=list(out_specs) + [ANY] * len(r_outs), out_shape=list(out_shape) + r_outs,
        scratch_shapes=list(scratch_shapes) + r_sems, input_output_aliases=aliases,
        compiler_params=_params(("arbitrary",) * len(grid) if riders else sem),
    )(*arrays, *r_arrays)
    if not riders:
        return res
    main, rest, per = res[:n_out], res[n_out:], []
    for r in riders:
        per.append(rest[:len(r.out_shape)])
        rest = rest[len(r.out_shape):]
    return main, per


def _run_rider(name, rider):
    def body(*refs):
        n_a, n_o = len(rider.arrays), len(rider.out_shape)
        ins, outs, sems = refs[:n_a], refs[n_a:n_a + n_o], refs[n_a + n_o:]
        for fn in [rider.start] + rider.steps + [rider.finish]:
            fn(ins, outs, sems)

    return _pcall(body, name=name, out_shape=rider.out_shape, in_specs=[ANY] * len(rider.arrays),
                  out_specs=[ANY] * len(rider.out_shape), scratch_shapes=rider.sems,
                  input_output_aliases=rider.aliases)(*rider.arrays)


def _mm(name, pairs, *, M, N, K, ta=False, tb=False, tm=512, tn=1024, tk=1024,
        a_pro=None, b_pro=None, extras=(), outs, epi, riders=()):
    tm, tn, tk = _tile(M, tm), _tile(N, tn), _tile(K, tk)
    n_i, n_j, n_k = M // tm, N // tn, K // tk
    n_p, n_e = len(pairs), len(extras)
    arrays, in_specs = [], []
    for a, _ in pairs:
        arrays.append(a)
        in_specs.append(pl.BlockSpec((tk, tm), lambda i, j, k: (k, i)) if ta
                        else pl.BlockSpec((tm, tk), lambda i, j, k: (i, k)))
    for _, b in pairs:
        arrays.append(b)
        in_specs.append(pl.BlockSpec((tn, tk), lambda i, j, k: (j, k)) if tb
                        else pl.BlockSpec((tk, tn), lambda i, j, k: (k, j)))
    for arr, kind, off in extras:
        ob = off // tn
        assert off % tn == 0
        arrays.append(arr)
        if kind == "tile":
            in_specs.append(pl.BlockSpec((tm, tn), lambda i, j, k, ob=ob: (i, j + ob)))
        else:
            in_specs.append(pl.BlockSpec((1, tn), lambda i, j, k, ob=ob: (0, j + ob)))
    out_shape, out_specs = [], []
    for o in outs:
        if o["kind"] == "tile":
            out_shape.append(jax.ShapeDtypeStruct((M, N), o["dtype"]))
            out_specs.append(pl.BlockSpec((tm, tn), lambda i, j, k: (i, j)))
        else:
            out_shape.append(jax.ShapeDtypeStruct((n_i, 1, N), F32))
            out_specs.append(pl.BlockSpec((1, 1, tn), lambda i, j, k: (i, 0, j)))
    dims = (((0 if ta else 1,), (1 if tb else 0,)), ((), ()))

    def body(*refs):
        a_refs, b_refs = refs[:n_p], refs[n_p:2 * n_p]
        e_refs = refs[2 * n_p:2 * n_p + n_e]
        o_refs = refs[2 * n_p + n_e:2 * n_p + n_e + len(outs)]
        acc_refs = refs[2 * n_p + n_e + len(outs):]

        def product(p):
            a, b = a_refs[p][...], b_refs[p][...]
            if a_pro is not None:
                a = a_pro(a)
            if b_pro is not None:
                b = b_pro(b)
            return lax.dot_general(a, b, dims, preferred_element_type=F32)

        def write(accs):
            vals = epi(accs, [e[...] for e in e_refs])
            for o, o_ref, val in zip(outs, o_refs, vals):
                if o["kind"] == "tile":
                    o_ref[...] = val.astype(o_ref.dtype)
                else:
                    o_ref[0] = val

        if n_k == 1:
            write([product(p) for p in range(n_p)])
            return
        k = pl.program_id(2)

        @pl.when(k == 0)
        def _():
            for acc in acc_refs:
                acc[...] = jnp.zeros_like(acc)

        for p in range(n_p):
            acc_refs[p][...] += product(p)

        pl.when(k == n_k - 1)(lambda: write([acc[...] for acc in acc_refs]))

    return _ride(name, body, riders, arrays, grid=(n_i, n_j, n_k), in_specs=in_specs, out_specs=out_specs,
                 out_shape=out_shape, scratch_shapes=[pltpu.VMEM((tm, tn), F32) for _ in pairs] if n_k > 1 else [],
                 sem=("parallel", "parallel", "arbitrary"))


def _tile_out(dtype):
    return {"kind": "tile", "dtype": dtype}


_COLSUM = {"kind": "colsum"}


def _colsum(v):
    return jnp.sum(v, axis=0, keepdims=True)


def _norm_mod(name, x, norm_w, scale, shift):
    S, D = x.shape
    tr = _tile(S, 256)

    def body(x_ref, nw_ref, sc_ref, sh_ref, h_ref):
        xv = x_ref[...]
        r = lax.rsqrt(jnp.mean(xv * xv, axis=-1, keepdims=True) + EPS)
        h_ref[...] = ((xv * r * nw_ref[...]) * (1.0 + sc_ref[...]) + sh_ref[...]).astype(BF16)

    row = pl.BlockSpec((1, D), lambda i: (0, 0))
    til = pl.BlockSpec((tr, D), lambda i: (i, 0))
    return _pcall(body, name=name, grid=(S // tr,), in_specs=[til, row, row, row], out_specs=til,
                  out_shape=jax.ShapeDtypeStruct((S, D), BF16), compiler_params=_params(("parallel",)))(
                      x, norm_w, scale, shift)


def _norm_mod_bwd(name, dh, x, dres, norm_w, scale, gate_o=None):
    S, D = x.shape
    tr = _tile(S, 256)
    n_r = S // tr
    with_gate = gate_o is not None

    def body(*refs):
        if with_gate:
            dh_ref, x_ref, dres_ref, nw_ref, sc_ref, o_ref, g_ref, dx_ref, p1, p2, p3, do_ref, p4 = refs
        else:
            dh_ref, x_ref, dres_ref, nw_ref, sc_ref, dx_ref, p1, p2, p3 = refs
        dhv, xv, nw = dh_ref[...], x_ref[...], nw_ref[...]
        r = lax.rsqrt(jnp.mean(xv * xv, axis=-1, keepdims=True) + EPS)
        xh = xv * r
        p1[0] = _colsum(dhv)
        p2[0] = _colsum(dhv * (xh * nw))
        dn = dhv * (1.0 + sc_ref[...])
        p3[0] = _colsum(dn * xh)
        dxh = dn * nw
        dx = dres_ref[...] + r * (dxh - xh * jnp.mean(dxh * xh, axis=-1, keepdims=True))
        dx_ref[...] = dx
        if with_gate:
            do_ref[...] = (dx * g_ref[...]).astype(BF16)
            p4[0] = _colsum(dx * o_ref[...].astype(F32))

    row = pl.BlockSpec((1, D), lambda i: (0, 0))
    til = pl.BlockSpec((tr, D), lambda i: (i, 0))
    part = pl.BlockSpec((1, 1, D), lambda i: (i, 0, 0))
    part_shape = jax.ShapeDtypeStruct((n_r, 1, D), F32)
    in_specs = [til, til, til, row, row]
    arrays = [dh, x, dres, norm_w, scale]
    out_specs = [til, part, part, part]
    out_shape = [jax.ShapeDtypeStruct((S, D), F32), part_shape, part_shape, part_shape]
    if with_gate:
        in_specs += [til, row]
        arrays += list(gate_o)
        out_specs += [til, part]
        out_shape += [jax.ShapeDtypeStruct((S, D), BF16), part_shape]
    return _pcall(body, name=name, grid=(n_r,), in_specs=in_specs, out_specs=out_specs, out_shape=out_shape,
                  compiler_params=_params(("parallel",)))(*arrays)


def _pool_w_specs(rows, cg):
    return [pl.BlockSpec((rows, cg), lambda g, j=j: (N_GROUPS * j + g, 0)) for j in range(N_CHIPS)]


def _pool_fwd(proj, wp_full, pool_scale, S, PW):
    cg = PW // N_GROUPS
    rows = cg // N_CHIPS
    T = _tile(S, POOL_T)
    n_t = S // T

    def body(u_ref, w0, w1, w2, w3, ps_ref, pooled_ref, pa_ref):
        g = pl.program_id(0)
        win = jnp.left_shift(2, g)
        w = jnp.concatenate([w0[...], w1[...], w2[...], w3[...]], axis=0)
        t_i = lax.broadcasted_iota(jnp.int32, (T, T), 0)
        j_i = lax.broadcasted_iota(jnp.int32, (T, T), 1)
        b_cur = ((j_i <= t_i) & (j_i > t_i - win)).astype(BF16)
        b_prev = (j_i - T > t_i - win).astype(BF16)
        row = lax.broadcasted_iota(jnp.int32, (T, 1), 0)
        for r in range(n_t):
            cur = u_ref[r * T:(r + 1) * T, :]
            ws = jnp.dot(b_cur, cur, preferred_element_type=F32)
            if r > 0:
                ws += jnp.dot(b_prev, u_ref[(r - 1) * T:r * T, :], preferred_element_type=F32)
            count = jnp.minimum(row + (r * T + 1), win).astype(F32)
            pooled = (ws / count - cur.astype(F32)).astype(BF16)
            pooled_ref[r * T:(r + 1) * T, :] = pooled
            mixed = jnp.dot(pooled, w, preferred_element_type=F32)
            pa_ref[r * T:(r + 1) * T, :] = (mixed * ps_ref[...]).astype(BF16)

    col = pl.BlockSpec((S, cg), lambda g: (0, g))
    return _pcall(
        body, name="pool_fwd", grid=(N_GROUPS,),
        in_specs=[col] + _pool_w_specs(rows, cg) + [pl.BlockSpec((1, cg), lambda g: (0, g))],
        out_specs=[col, col],
        out_shape=[jax.ShapeDtypeStruct((S, PW), BF16), jax.ShapeDtypeStruct((S, PW), BF16)],
        compiler_params=_params(("parallel",)),
    )(proj, wp_full, wp_full, wp_full, wp_full, pool_scale)


def _pool_bwd(dpa, pooled, wp_full, pool_scale, S, PW):
    cg = PW // N_GROUPS
    rows = cg // N_CHIPS
    T = _tile(S, POOL_T)
    n_t = S // T

    def body(dpa_ref, pooled_ref, w0, w1, w2, w3, ps_ref, du_ref, gw_ref, gs_ref, dp_s, dpc_s, dmx_s):
        g = pl.program_id(0)
        win = jnp.left_shift(2, g)
        w = jnp.concatenate([w0[...], w1[...], w2[...], w3[...]], axis=0)
        row = lax.broadcasted_iota(jnp.int32, (T, 1), 0)
        gs = jnp.zeros((1, cg), F32)
        for r in range(n_t):
            sl = slice(r * T, (r + 1) * T)
            mixed = jnp.dot(pooled_ref[sl, :], w, preferred_element_type=F32)
            dpa_t = dpa_ref[sl, :]
            gs += _colsum(dpa_t * mixed)
            dmx = (dpa_t * ps_ref[...]).astype(BF16)
            dmx_s[sl, :] = dmx
            dpo = lax.dot_general(dmx, w, (((1,), (1,)), ((), ())), preferred_element_type=F32)
            dp_s[sl, :] = dpo
            count = jnp.minimum(row + (r * T + 1), win).astype(F32)
            dpc_s[sl, :] = (dpo / count).astype(BF16)
        gs_ref[...] = gs
        gw = lax.dot_general(pooled_ref[...], dmx_s[...], (((0,), (0,)), ((), ())), preferred_element_type=F32)
        for j in range(N_CHIPS):
            gw_ref[j, 0] = gw[j * rows:(j + 1) * rows, :].astype(BF16)
        j_i = lax.broadcasted_iota(jnp.int32, (T, T), 0)
        t_i = lax.broadcasted_iota(jnp.int32, (T, T), 1)
        b_cur = ((t_i >= j_i) & (t_i < j_i + win)).astype(BF16)
        b_next = (t_i + T < j_i + win).astype(BF16)
        for r in range(n_t):
            sl = slice(r * T, (r + 1) * T)
            acc = jnp.dot(b_cur, dpc_s[sl, :], preferred_element_type=F32)
            if r + 1 < n_t:
                acc += jnp.dot(b_next, dpc_s[(r + 1) * T:(r + 2) * T, :], preferred_element_type=F32)
            du_ref[sl, :] = (acc - dp_s[sl, :]).astype(BF16)

    col = pl.BlockSpec((S, cg), lambda g: (0, g))
    return _pcall(
        body, name="pool_bwd", grid=(N_GROUPS,),
        in_specs=[col, col] + _pool_w_specs(rows, cg) + [pl.BlockSpec((1, cg), lambda g: (0, g))],
        out_specs=[col, pl.BlockSpec((N_CHIPS, 1, rows, cg), lambda g: (0, g, 0, 0)),
                   pl.BlockSpec((1, cg), lambda g: (0, g))],
        out_shape=[jax.ShapeDtypeStruct((S, PW), BF16),
                   jax.ShapeDtypeStruct((N_CHIPS, N_GROUPS, rows, cg), BF16),
                   jax.ShapeDtypeStruct((1, PW), F32)],
        scratch_shapes=[pltpu.VMEM((S, cg), F32), pltpu.VMEM((S, cg), BF16), pltpu.VMEM((S, cg), BF16)],
        compiler_params=_params(("parallel",)),
    )(dpa, pooled, wp_full, wp_full, wp_full, wp_full, pool_scale)


_NT = (((1,), (1,)), ((), ()))
_TN = (((0,), (0,)), ((), ()))


def _split_dot(v, tri):
    hi = v.astype(BF16)
    lo = (v - hi.astype(F32)).astype(BF16)
    return jnp.dot(hi, tri, preferred_element_type=F32) + jnp.dot(lo, tri, preferred_element_type=F32)


LOG2E = 1.4426950408889634
QK_SCALE = 1.0 / math.sqrt(HEAD_DIM)


def _sb_scores(q2_i, k_j, tri_l, masked):
    tq, tk = q2_i.shape[0], k_j.shape[0]
    s = lax.dot_general(q2_i, k_j, _NT, preferred_element_type=F32)
    lp = jnp.log(1.0 + jnp.exp2(-jnp.abs(s))) * LOG2E
    lb = jnp.minimum(s, 0.0) - lp
    l = lb - s
    mask = None
    if masked:
        mask = lax.broadcasted_iota(jnp.int32, (tq, tk), 0) > lax.broadcasted_iota(jnp.int32, (tq, tk), 1)
        l = jnp.where(mask, l, 0.0)
    return l, lb, lb + _split_dot(l, tri_l), mask


def _sb_weights(t, carry_l, mask):
    a = jnp.exp2(t + carry_l)
    return a if mask is None else jnp.where(mask, a, 0.0)


def _rowsum(v):
    return jnp.sum(v, axis=1, keepdims=True)


def _qk_norm(x_ref, w_ref):
    xv = x_ref[...].astype(F32)
    r = lax.rsqrt(jnp.mean(xv * xv, axis=-1, keepdims=True) + EPS)
    return xv * r, r


def _attn_fwd(proj, q_norm_w, k_norm_w, S, H, q_off, riders=()):
    t = _tile(S, ATT_T)
    n_q = S // t

    def body(q_ref, k_ref, v_ref, qw_ref, kw_ref, att_ref, attf_ref, qn_s, kn_s):
        qh, _ = _qk_norm(q_ref, qw_ref)
        qn_s[...] = (qh * qw_ref[...] * (QK_SCALE * LOG2E)).astype(BF16)
        kh, _ = _qk_norm(k_ref, kw_ref)
        kn_s[...] = (kh * kw_ref[...]).astype(BF16)
        tri_l = (lax.broadcasted_iota(jnp.int32, (t, t), 0) > lax.broadcasted_iota(jnp.int32, (t, t), 1)).astype(BF16)

        def rows(j):
            return pl.ds(pl.multiple_of(j * t, t), t)

        def q_step(i, _):
            q_i = qn_s[rows(i), :]

            def av(a, j):
                return jnp.dot(a.astype(BF16), v_ref[rows(j), :], preferred_element_type=F32)

            l, _, tt, mask = _sb_scores(q_i, kn_s[rows(i), :], tri_l, True)
            acc = av(_sb_weights(tt, 0.0, mask), i)
            carry = _rowsum(l)

            def single(_, c):
                carry, acc = c
                l, _, tt, _ = _sb_scores(q_i, kn_s[rows(i - 1), :], tri_l, False)
                return carry + _rowsum(l), acc + av(_sb_weights(tt, carry, None), i - 1)

            carry, acc = lax.fori_loop(0, i % 2, single, (carry, acc))
            top = i - 1 - i % 2

            def pair(p, c):
                carry, acc = c
                j0 = top - 2 * p
                l0, _, t0, _ = _sb_scores(q_i, kn_s[rows(j0), :], tri_l, False)
                l1, _, t1, _ = _sb_scores(q_i, kn_s[rows(j0 - 1), :], tri_l, False)
                mid = carry + _rowsum(l0)
                acc = acc + av(_sb_weights(t0, carry, None), j0) + av(_sb_weights(t1, mid, None), j0 - 1)
                return mid + _rowsum(l1), acc

            _, acc = lax.fori_loop(0, i // 2, pair, (carry, acc))
            att_ref[rows(i), :] = acc.astype(BF16)
            attf_ref[rows(i), :] = acc
            return 0

        lax.fori_loop(0, n_q, q_step, 0)

    def col(off):
        return pl.BlockSpec((S, HEAD_DIM), lambda h, off=off: (0, off + h))

    wspec = pl.BlockSpec((1, HEAD_DIM), lambda h: (0, 0))
    return _ride(
        "attn_fwd", body, riders, [proj, proj, proj, q_norm_w, k_norm_w], grid=(H,),
        in_specs=[col(q_off), col(q_off + H), col(q_off + 2 * H), wspec, wspec],
        out_specs=[col(0), col(0)],
        out_shape=[jax.ShapeDtypeStruct((S, H * HEAD_DIM), BF16), jax.ShapeDtypeStruct((S, H * HEAD_DIM), F32)],
        scratch_shapes=[pltpu.VMEM((S, HEAD_DIM), BF16), pltpu.VMEM((S, HEAD_DIM), BF16)],
        sem=("parallel",))


def _attn_bwd(proj, datt, attf, q_norm_w, k_norm_w, S, H, q_off, riders=()):
    t = _tile(S, ATT_T)
    n_q = S // t

    def body(q_ref, k_ref, v_ref, do_ref, o_ref, qw_ref, kw_ref, dq_ref, dk_ref, dv_ref, gq_ref, gk_ref,
             qn_s, kn_s, qz_s, kz_s, dk_s, dv_s, gq_s):
        qw, kw = qw_ref[...], kw_ref[...]
        qh, _ = _qk_norm(q_ref, qw_ref)
        qn_s[...] = (qh * qw * (QK_SCALE * LOG2E)).astype(BF16)
        qz_s[...] = (qh * qw * QK_SCALE).astype(BF16)
        kh, _ = _qk_norm(k_ref, kw_ref)
        kn_s[...] = (kh * kw).astype(BF16)
        kz_s[...] = (kh * kw * QK_SCALE).astype(BF16)
        dk_s[...] = jnp.zeros_like(dk_s)
        dv_s[...] = jnp.zeros_like(dv_s)
        gq_s[...] = jnp.zeros_like(gq_s)
        r_i = lax.broadcasted_iota(jnp.int32, (t, t), 0)
        c_i = lax.broadcasted_iota(jnp.int32, (t, t), 1)
        tri_l = (r_i > c_i).astype(BF16)
        tri_e = (r_i >= c_i).astype(BF16)

        def rows(j):
            return pl.ds(pl.multiple_of(j * t, t), t)

        def q_step(i, _):
            q_i = qn_s[rows(i), :]
            do_i = do_ref[rows(i), :]
            d_i = _rowsum(do_i.astype(F32) * o_ref[rows(i), :])

            def scores(j, masked):
                l, lb, tt, mask = _sb_scores(q_i, kn_s[rows(j), :], tri_l, masked)
                da = lax.dot_general(do_i, v_ref[rows(j), :], _NT, preferred_element_type=F32)
                return l, lb, tt, mask, da

            def grads(j, sc, carry_l, carry_e, dq_acc):
                l, lb, tt, mask, da = sc
                a_bf = _sb_weights(tt, carry_l, mask).astype(BF16)
                e = da * a_bf.astype(F32)
                p = (d_i - carry_e) - _split_dot(e, tri_e)
                dz = e - jnp.exp2(lb) * (e + p)
                if mask is not None:
                    dz = jnp.where(mask, dz, 0.0)
                dz = dz.astype(BF16)
                dk_s[rows(j), :] += lax.dot_general(dz, qz_s[rows(i), :], _TN, preferred_element_type=F32)
                dv_s[rows(j), :] += lax.dot_general(a_bf, do_i, _TN, preferred_element_type=F32)
                return (carry_l + _rowsum(l), carry_e + _rowsum(e),
                        dq_acc + jnp.dot(dz, kz_s[rows(j), :], preferred_element_type=F32))

            zero = jnp.zeros((t, 1), F32)
            first = (zero, zero, jnp.zeros((t, HEAD_DIM), F32))

            def group(js, diagonal_first, c):
                scs = [scores(j, diagonal_first and n == 0) for n, j in enumerate(js)]
                for j, sc in zip(js, scs):
                    c = grads(j, sc, *c)
                return c

            n_first = i % ATT_GROUP
            c = lax.switch(n_first, [functools.partial(group, [i - u for u in range(n + 1)], True, first)
                                     for n in range(ATT_GROUP)])
            top = i - 1 - n_first

            def whole(p, c):
                j0 = top - ATT_GROUP * p
                return group([j0 - u for u in range(ATT_GROUP)], False, c)

            _, _, dqn = lax.fori_loop(0, (i - n_first) // ATT_GROUP, whole, c)
            qv = q_ref[rows(i), :].astype(F32)
            r = lax.rsqrt(jnp.mean(qv * qv, axis=-1, keepdims=True) + EPS)
            xh = qv * r
            gq_s[...] += _colsum(dqn * xh)
            dxh = dqn * qw
            dq_ref[rows(i), :] = (r * (dxh - xh * jnp.mean(dxh * xh, axis=-1, keepdims=True))).astype(BF16)
            return 0

        lax.fori_loop(0, n_q, q_step, 0)
        gq_ref[0] = gq_s[...]
        kh, rk = _qk_norm(k_ref, kw_ref)
        dkn = dk_s[...]
        gk_ref[0] = _colsum(dkn * kh)
        dxh = dkn * kw
        dk_ref[...] = (rk * (dxh - kh * jnp.mean(dxh * kh, axis=-1, keepdims=True))).astype(BF16)
        dv_ref[...] = dv_s[...].astype(BF16)

    def col(off):
        return pl.BlockSpec((S, HEAD_DIM), lambda h, off=off: (0, off + h))

    wspec = pl.BlockSpec((1, HEAD_DIM), lambda h: (0, 0))
    gspec = pl.BlockSpec((1, 1, HEAD_DIM), lambda h: (h, 0, 0))
    act = jax.ShapeDtypeStruct((S, H * HEAD_DIM), BF16)
    gsh = jax.ShapeDtypeStruct((H, 1, HEAD_DIM), F32)
    return _ride(
        "attn_bwd", body, riders, [proj, proj, proj, datt, attf, q_norm_w, k_norm_w], grid=(H,),
        in_specs=[col(q_off), col(q_off + H), col(q_off + 2 * H), col(0), col(0), wspec, wspec],
        out_specs=[col(0), col(0), col(0), gspec, gspec],
        out_shape=[act, act, act, gsh, gsh],
        scratch_shapes=[pltpu.VMEM((S, HEAD_DIM), BF16)] * 4 + [pltpu.VMEM((S, HEAD_DIM), F32)] * 2
        + [pltpu.VMEM((1, HEAD_DIM), F32)],
        sem=("parallel",))


def _place():
    x, y, c = lax.axis_index("x"), lax.axis_index("y"), lax.axis_index("c")
    chips = [(1 - x, y), (x, 1 - y), (1 - x, 1 - y)]
    return x, y, c, chips


def _dev_allgather(name, v):
    m_per, n = v.shape

    def body(x_ref, out_ref, send_sems, recv_sems, local_sem):
        x, y, c, chips = _place()
        me, sibling = (x, y, c), (x, y, 1 - c)

        def rows(px, py, pc):
            return out_ref.at[pl.ds((4 * px + 2 * py + pc) * m_per, m_per), :]

        def copy(k, block, to, src=None):
            return pltpu.make_async_remote_copy(
                src_ref=rows(*block) if src is None else src, dst_ref=rows(*block),
                send_sem=send_sems.at[k], recv_sem=recv_sems.at[k], device_id=to, device_id_type=MESH)

        mine = pltpu.make_async_copy(x_ref, rows(*me), local_sem)
        mine.start()
        first = [copy(0, me, sibling, src=x_ref)]
        first += [copy(1 + j, me, (*chip, c), src=x_ref) for j, chip in enumerate(chips)]
        for cp in first:
            cp.start()
        passed = [copy(4 + j, (*chip, c), sibling) for j, chip in enumerate(chips)]
        for j, chip in enumerate(chips):
            copy(1 + j, (*chip, c), me).wait_recv()
            passed[j].start()
        copy(0, sibling, me).wait_recv()
        for j, chip in enumerate(chips):
            copy(4 + j, (*chip, 1 - c), me).wait_recv()
        for cp in first + passed:
            cp.wait_send()
        mine.wait()

    return _pcall(
        body, name=name, out_shape=jax.ShapeDtypeStruct((N_DEV * m_per, n), v.dtype),
        in_specs=[pl.BlockSpec(memory_space=pltpu.VMEM)], out_specs=pl.BlockSpec(memory_space=pltpu.VMEM),
        scratch_shapes=[pltpu.SemaphoreType.DMA((7,)), pltpu.SemaphoreType.DMA((7,)), pltpu.SemaphoreType.DMA],
        compiler_params=pltpu.CompilerParams(vmem_limit_bytes=VMEM_LIMIT_V7X),
    )(v)


class _W:
    def __init__(self, name, kind, R, C):
        self.name, self.kind, self.R, self.C = name, kind, R, C

    @property
    def shard_shape(self):
        return (self.R, self.C // N_CHIPS) if self.kind == "col" else (self.R // N_CHIPS, self.C)

    @property
    def half_rows(self):
        return self.shard_shape[0] // 2

    def shard_half(self, ref, half):
        return ref.at[pl.ds(half * self.half_rows, self.half_rows), :]

    def region(self, full_ref, chip, half):
        hr = self.half_rows
        if self.kind == "col":
            cw = self.C // N_CHIPS
            return full_ref.at[pl.ds(half * hr, hr), pl.ds(chip * cw, cw)]
        return full_ref.at[pl.ds(chip * (2 * hr) + half * hr, hr), :]


def _ag_rider(ws, fulls, n_ch=4, chunks=None):
    n_w = len(ws)
    lo, hi = chunks or (0, n_ch)
    per = 6

    def parts(full, sems):
        send_sems, recv_sems = sems
        x, y, c, _ = _place()
        xn, yn, dg = (1 - x, y), (x, 1 - y), (1 - x, 1 - y)
        via = (x + (1 - c) * (1 - 2 * x), y + c * (1 - 2 * y))
        to = (x + c * (1 - 2 * x), y + (1 - c) * (1 - 2 * y))

        def reg(i, chip, half, t):
            nr = ws[i].half_rows // n_ch
            return ws[i].region(full[i], 2 * chip[0] + chip[1], half).at[pl.ds(t * nr, nr), :]

        def copy(r, i, t, k, dev):
            s = (i * (hi - lo) + t - lo) * per + k
            return pltpu.make_async_remote_copy(src_ref=r, dst_ref=r, send_sem=send_sems.at[s],
                                                recv_sem=recv_sems.at[s], device_id=dev, device_id_type=MESH)

        def direct(i, t, k):
            return copy(reg(i, (x, y), c, t), i, t, k, (*(via, to)[k], c))

        def direct_in(i, t, k):
            return copy(reg(i, (via, to)[k], c, t), i, t, k, (*(via, to)[k], c))

        def relay(i, t):
            return copy(reg(i, via, c, t), i, t, 2, (*to, c))

        def relay_in(i, t):
            return copy(reg(i, dg, c, t), i, t, 2, (*to, c))

        def hand(i, t, k, half):
            return copy(reg(i, (xn, yn, dg)[k], half, t), i, t, 3 + k, (x, y, 1 - c))

        return c, direct, direct_in, relay, relay_in, hand

    def start(_, full, sems):
        _, direct, _, _, _, _ = parts(full, sems)
        for t in range(lo, hi):
            for i in range(n_w):
                direct(i, t, 0).start()
                direct(i, t, 1).start()

    def arrived(t):
        def step(_, full, sems):
            c, _, direct_in, relay, relay_in, hand = parts(full, sems)
            for i in range(n_w):
                direct_in(i, t, 0).wait_recv()
                direct_in(i, t, 1).wait_recv()
                relay(i, t).start()
                hand(i, t, 0, c).start()
                hand(i, t, 1, c).start()
        return step

    def finish(_, full, sems):
        c, direct, _, relay, relay_in, hand = parts(full, sems)
        for t in range(lo, hi):
            for i in range(n_w):
                relay_in(i, t).wait_recv()
                hand(i, t, 2, c).start()
        for i in range(n_w):
            for t in range(lo, hi):
                for k in range(3):
                    hand(i, t, k, 1 - c).wait_recv()
        for i in range(n_w):
            for t in range(lo, hi):
                direct(i, t, 0).wait_send()
                direct(i, t, 1).wait_send()
                relay(i, t).wait_send()
                for k in range(3):
                    hand(i, t, k, c).wait_send()

    n_sem = per * (hi - lo) * n_w
    return _Rider(fulls, [jax.ShapeDtypeStruct((w.R, w.C), BF16) for w in ws],
                  [pltpu.SemaphoreType.DMA((n_sem,)), pltpu.SemaphoreType.DMA((n_sem,))], start, finish,
                  steps=[arrived(t) for t in range(lo, hi)], aliases={i: i for i in range(n_w)})


def _cast_into_full(w, a32, chip_arr):
    sr, sc = w.shard_shape
    tr, tc = _tile(sr, 512), _tile(sc, 2048)
    n_r, n_c = sr // tr, sc // tc
    if w.kind == "col":
        out_spec = pl.BlockSpec((tr, tc), lambda i, j, chip: (i, chip[0] * n_c + j))
    else:
        out_spec = pl.BlockSpec((tr, tc), lambda i, j, chip: (chip[0] * n_r + i, j))

    def body(chip_ref, a_ref, o_ref):
        o_ref[...] = a_ref[...].astype(BF16)

    return _pcall(
        body, name="cast_" + w.name, out_shape=jax.ShapeDtypeStruct((w.R, w.C), BF16),
        grid_spec=pltpu.PrefetchScalarGridSpec(
            num_scalar_prefetch=1, grid=(n_r, n_c),
            in_specs=[pl.BlockSpec((tr, tc), lambda i, j, chip: (i, j))], out_specs=out_spec),
        compiler_params=_params(("parallel", "parallel")),
    )(chip_arr, a32)


def _half_view(w, g):
    return g if w.kind == "col" else g.reshape(N_CHIPS, w.R // N_CHIPS, w.C)


def _px_rider(ws, grads):
    n_w = len(ws)

    def copies(g, got, sems):
        send_sems, recv_sems = sems
        x, y, c, _ = _place()

        def half_all(w, ref, half):
            hr = w.half_rows
            if w.kind == "col":
                return ref.at[pl.ds(half * hr, hr), :]
            return ref.at[:, pl.ds(half * hr, hr), :]

        return [pltpu.make_async_remote_copy(
            src_ref=half_all(w, g[i], 1 - c), dst_ref=got[i], send_sem=send_sems.at[i], recv_sem=recv_sems.at[i],
            device_id=(x, y, 1 - c), device_id_type=MESH) for i, w in enumerate(ws)]

    def start(g, got, sems):
        for cp in copies(g, got, sems):
            cp.start()

    def finish(g, got, sems):
        for cp in copies(g, got, sems):
            cp.wait_recv()
            cp.wait_send()

    def got_shape(w):
        hr = w.half_rows
        return (hr, w.C) if w.kind == "col" else (N_CHIPS, hr, w.C)

    return _Rider([_half_view(w, g) for w, g in zip(ws, grads)],
                  [jax.ShapeDtypeStruct(got_shape(w), BF16) for w in ws],
                  [pltpu.SemaphoreType.DMA((n_w,)), pltpu.SemaphoreType.DMA((n_w,))], start, finish)


def _pair_sum(w, g, got, c_arr):
    hr = w.half_rows
    if w.kind == "col":
        tr, tc = _tile(hr, 512), _tile(w.C, 2048)
        n_r = hr // tr
        grid = (n_r, w.C // tc)
        g_spec = pl.BlockSpec((tr, tc), lambda i, j, c: (c[0] * n_r + i, j))
        o_spec = pl.BlockSpec((tr, tc), lambda i, j, c: (i, j))
    else:
        tr = _tile(hr, 512)
        n_r = hr // tr
        grid = (N_CHIPS, n_r)
        g_spec = pl.BlockSpec((1, tr, w.C), lambda s, i, c: (s, c[0] * n_r + i, 0))
        o_spec = pl.BlockSpec((1, tr, w.C), lambda s, i, c: (s, i, 0))

    def body(c_ref, g_ref, got_ref, out_ref):
        out_ref[...] = (g_ref[...].astype(F32) + got_ref[...].astype(F32)).astype(BF16)

    return _pcall(
        body, name="grad_pair_sum_" + w.name, out_shape=jax.ShapeDtypeStruct(got.shape, BF16),
        grid_spec=pltpu.PrefetchScalarGridSpec(num_scalar_prefetch=1, grid=grid, in_specs=[g_spec, o_spec],
                                               out_specs=o_spec),
        compiler_params=_params(("parallel", "parallel")),
    )(c_arr, _half_view(w, g), got)


def _chip_sum(w, p, q, cc_arr):
    hr, cols = w.half_rows, w.shard_shape[1]
    tr, tc = _tile(hr, 512), _tile(cols, 2048)
    n_r, n_c = hr // tr, cols // tc

    def body(cc_ref, own, q1, q2, q3, out_ref):
        own_v = own[...] if w.kind == "col" else own[0]
        out_ref[...] = ((own_v.astype(F32) + q1[0].astype(F32)) + q2[0].astype(F32)) + q3[0].astype(F32)

    if w.kind == "col":
        own_spec = pl.BlockSpec((tr, tc), lambda i, j, cc: (i, cc[1] * n_c + j))
    else:
        own_spec = pl.BlockSpec((1, tr, tc), lambda i, j, cc: (cc[1], i, j))
    q_specs = [pl.BlockSpec((1, tr, tc), lambda i, j, cc, s=s: ((cc[1] + s) % N_CHIPS, i, j)) for s in (1, 2, 3)]
    return _pcall(
        body, name="grad_chip_sum_" + w.name, out_shape=jax.ShapeDtypeStruct(w.shard_shape, F32),
        grid_spec=pltpu.PrefetchScalarGridSpec(
            num_scalar_prefetch=1, grid=(n_r, n_c), in_specs=[own_spec] + q_specs,
            out_specs=pl.BlockSpec((tr, tc), lambda i, j, cc: (cc[0] * n_r + i, j))),
        compiler_params=_params(("parallel", "parallel")),
    )(cc_arr, p, q, q, q)


_SEM = pl.BlockSpec(memory_space=pltpu.SEMAPHORE)
_HBM = pl.BlockSpec(memory_space=pltpu.HBM)


def _split_copies(kind, ws, p, land, send_sems, recv_sems):
    x, y, c, chips = _place()
    my_chip = 2 * x + y
    pairs = []
    for i, w in enumerate(ws):
        if kind == "pair":
            hr = w.half_rows
            src = p[i].at[pl.ds((1 - c) * hr, hr), :] if w.kind == "col" else p[i].at[:, pl.ds((1 - c) * hr, hr), :]
            cp = pltpu.make_async_remote_copy(src_ref=src, dst_ref=land[i], send_sem=send_sems.at[i],
                                              recv_sem=recv_sems.at[i], device_id=(x, y, 1 - c), device_id_type=MESH)
            pairs.append((cp, cp))
            continue
        for k, chip in enumerate(chips):
            to_chip = 2 * chip[0] + chip[1]
            src = p[i].at[:, pl.ds(to_chip * (w.C // N_CHIPS), w.C // N_CHIPS)] if w.kind == "col" else p[i].at[to_chip]
            kw = dict(send_sem=send_sems.at[3 * i + k], recv_sem=recv_sems.at[3 * i + k], device_id=(*chip, c),
                      device_id_type=MESH)
            pairs.append((pltpu.make_async_remote_copy(src_ref=src, dst_ref=land[i].at[my_chip], **kw),
                          pltpu.make_async_remote_copy(src_ref=src, dst_ref=land[i].at[to_chip], **kw)))
    return pairs


def _split_start(name, kind, ws, arrays):
    n_w = len(ws)
    if kind == "pair":
        arrays = [_half_view(w, g) for w, g in zip(ws, arrays)]
        lands = [lax.empty((w.half_rows, w.C) if w.kind == "col" else (N_CHIPS, w.half_rows, w.C), BF16) for w in ws]
    else:
        lands = [lax.empty((N_CHIPS, w.half_rows, w.shard_shape[1]), BF16) for w in ws]
    n_sem = n_w if kind == "pair" else 3 * n_w

    def body(*refs):
        p, land = refs[:n_w], refs[n_w:2 * n_w]
        for out, _ in _split_copies(kind, ws, p, land, refs[2 * n_w], refs[2 * n_w + 1]):
            out.start()
        refs[-1][...] = jnp.zeros_like(refs[-1])

    arrays = [pltpu.with_memory_space_constraint(a, pltpu.HBM) for a in list(arrays) + lands]
    res = _pcall(
        body, name=name,
        out_shape=(pltpu.SemaphoreType.DMA((n_sem,)), pltpu.SemaphoreType.DMA((n_sem,)),
                   *[pltpu.HBM(a.shape, a.dtype) for a in arrays], jax.ShapeDtypeStruct((SUBLANES, LANES), F32)),
        in_specs=[_HBM] * (2 * n_w),
        out_specs=(_SEM, _SEM, *[_HBM] * (2 * n_w), pl.BlockSpec(memory_space=pltpu.VMEM)),
        input_output_aliases={i: 2 + i for i in range(2 * n_w)},
        compiler_params=pltpu.CompilerParams(has_side_effects=pltpu.SideEffectType.DATAFLOW_SIDE_EFFECTING),
    )(*arrays)
    return (kind, ws, res[0], res[1], list(res[2:2 + n_w]), list(res[2 + n_w:2 + 2 * n_w])), res[-1]


def _split_wait(name, flight, after):
    kind, ws, send_sems, recv_sems, arrays, lands = flight
    n_w = len(ws)

    def body(*refs):
        p, land = refs[:n_w], refs[n_w:2 * n_w]
        for _, cp in _split_copies(kind, ws, p, land, refs[2 * n_w], refs[2 * n_w + 1]):
            cp.wait_send()
            cp.wait_recv()

    res = _pcall(
        body, name=name,
        out_shape=[pltpu.HBM(a.shape, a.dtype) for a in list(arrays) + list(lands)],
        in_specs=[_HBM] * (2 * n_w) + [_SEM, _SEM] + [ANY] * len(after), out_specs=[_HBM] * (2 * n_w),
        input_output_aliases={i: i for i in range(2 * n_w)},
        compiler_params=pltpu.CompilerParams(has_side_effects=pltpu.SideEffectType.DATAFLOW_SIDE_EFFECTING),
    )(*arrays, *lands, send_sems, recv_sems, *after)
    return list(res[:n_w]), list(res[n_w:])


def _sf_rider(ws, grads):
    n_w = len(ws)

    def copy(g, sems, i, half):
        send_sems, recv_sems = sems
        x, y, c, _ = _place()
        h = c if half == "mine" else 1 - c
        reg = ws[i].shard_half(g[i], h)
        return pltpu.make_async_remote_copy(src_ref=reg, dst_ref=reg, send_sem=send_sems.at[i], recv_sem=recv_sems.at[i],
                                            device_id=(x, y, 1 - c), device_id_type=MESH)

    def start(_, g, sems):
        for i in range(n_w):
            copy(g, sems, i, "mine").start()

    def finish(_, g, sems):
        for i in range(n_w):
            copy(g, sems, i, "other").wait_recv()
            copy(g, sems, i, "mine").wait_send()

    return _Rider(grads, [jax.ShapeDtypeStruct(w.shard_shape, F32) for w in ws],
                  [pltpu.SemaphoreType.DMA((n_w,)), pltpu.SemaphoreType.DMA((n_w,))], start, finish,
                  aliases={i: i for i in range(n_w)})


def _adamw_math(w, g, m, v):
    m = ADAM_B1 * m + (1.0 - ADAM_B1) * g
    v = ADAM_B2 * v + (1.0 - ADAM_B2) * (g * g)
    m_hat = m / (1.0 - ADAM_B1 ** ADAM_STEP)
    v_hat = v / (1.0 - ADAM_B2 ** ADAM_STEP)
    delta = -ADAM_LR * (m_hat / (jnp.sqrt(v_hat) + ADAM_EPS) + ADAM_WD * w)
    return delta, m, v


def _adamw(name, w, g, m, v, after=None):
    R, C = w.shape
    tr, tc = _tile(R, 256), _tile(C, 2048)
    behind = [] if after is None else [after]

    def body(w_ref, g_ref, m_ref, v_ref, *rest):
        g_out, d_out, m_out, v_out = rest[len(behind):]
        g = g_ref[...]
        g_out[...] = g
        d_out[...], m_out[...], v_out[...] = _adamw_math(w_ref[...], g, m_ref[...], v_ref[...])

    spec = pl.BlockSpec((tr, tc), lambda i, j: (i, j))
    sh = jax.ShapeDtypeStruct((R, C), F32)
    return _pcall(body, name=name, grid=(R // tr, C // tc), in_specs=[spec] * 4 + [ANY] * len(behind),
                  out_specs=[spec] * 4, out_shape=[sh] * 4, compiler_params=_params(("parallel", "parallel")))(
                      w, g, m, v, *behind)


def _ada_update(sct, dmod_sh, w, m, v, riders=()):
    R, C = w.shape
    tr, tc = _tile(R, 256), _tile(C, 1024)

    def body(s_ref, d_ref, w_ref, m_ref, v_ref, g_out, d_out, m_out, v_out):
        s, d = s_ref[...], d_ref[...]
        g = s[:, 0:1] * d[0:1, :]
        for b in range(1, N_DEV):
            g += s[:, b:b + 1] * d[b:b + 1, :]
        g_out[...] = g
        d_out[...], m_out[...], v_out[...] = _adamw_math(w_ref[...], g, m_ref[...], v_ref[...])

    spec = pl.BlockSpec((tr, tc), lambda i, j: (i, j))
    sh = jax.ShapeDtypeStruct((R, C), F32)
    return _ride(
        "ada_update", body, riders, [sct, dmod_sh, w, m, v], grid=(R // tr, C // tc),
        in_specs=[pl.BlockSpec((tr, N_DEV), lambda i, j: (i, 0)), pl.BlockSpec((N_DEV, tc), lambda i, j: (0, j)),
                  spec, spec, spec],
        out_specs=[spec] * 4, out_shape=[sh] * 4, scratch_shapes=[], sem=("parallel", "parallel"))


def _silu_rows(c_row):
    D = c_row.shape[1]

    def body(c_ref, o_ref):
        cv = c_ref[...]
        o_ref[...] = cv * jax.nn.sigmoid(cv)

    return _pcall(body, name="silu_c", out_shape=jax.ShapeDtypeStruct((1, D), F32))(c_row)


def _pack_partials(parts, widths, total):
    n = len(widths)

    def body(*refs):
        loss_p, out_ref = refs[n], refs[n + 1]
        off = 0
        for ref, wd in zip(refs[:n], widths):
            out_ref[:, off:off + wd] = jnp.sum(ref[...], axis=0)
            off += wd
        loss = jnp.sum(jnp.sum(loss_p[...], axis=0), axis=1, keepdims=True)
        out_ref[:, off:off + LANES] = jnp.broadcast_to(loss, (1, LANES))
        if off + LANES < total:
            out_ref[:, off + LANES:total] = jnp.zeros((1, total - off - LANES), F32)

    return _pcall(body, name="pack_partials", out_shape=jax.ShapeDtypeStruct((1, total), F32))(*parts)


def _small_update(gathered, offsets, params, loss_off):
    n_p = len(params)

    def over_devices(g_ref, off, wd):
        blk = g_ref[:, off:off + wd]
        g = blk[0:1, :]
        for b in range(1, N_DEV):
            g = g + blk[b:b + 1, :]
        return g

    def body(*refs):
        g_ref = refs[0]
        prm = refs[1:1 + 3 * n_p]
        outs = refs[1 + 3 * n_p:]
        outs[4 * n_p][...] = over_devices(g_ref, loss_off, LANES)
        for i, (off, wd) in enumerate(offsets):
            g = over_devices(g_ref, off, wd)
            w, m, v = prm[3 * i][...], prm[3 * i + 1][...], prm[3 * i + 2][...]
            outs[4 * i][...] = g
            outs[4 * i + 1][...], outs[4 * i + 2][...], outs[4 * i + 3][...] = _adamw_math(w, g, m, v)

    flat = [a for t in params for a in t]
    out_shape = [jax.ShapeDtypeStruct(t[0].shape, F32) for t in params for _ in range(4)]
    out_shape.append(jax.ShapeDtypeStruct((1, LANES), F32))
    return _pcall(body, name="small_update", out_shape=out_shape)(gathered, *flat)


def kernel(x, c, w_ada, b_ada, norm1_w, w_in, q_norm_w, k_norm_w, w_pool, pool_scale, w_a_up, w_b_up, w_o, norm2_w, w_ff1, w_ff2, loss_target, m_w_ada, m_b_ada, m_norm1_w, m_w_in, m_q_norm_w, m_k_norm_w, m_w_pool, m_pool_scale, m_w_a_up, m_w_b_up, m_w_o, m_norm2_w, m_w_ff1, m_w_ff2, v_w_ada, v_b_ada, v_norm1_w, v_w_in, v_q_norm_w, v_k_norm_w, v_w_pool, v_pool_scale, v_w_a_up, v_w_b_up, v_w_o, v_norm2_w, v_w_ff1, v_w_ff2):
    _, S, D = x.shape
    PW = D // 2
    H = PW // HEAD_DIM
    cg = PW // N_GROUPS
    IN = w_in.shape[2] * N_CHIPS
    FF = w_ff1.shape[2] * N_CHIPS
    A_COLS = w_ada.shape[2]
    xi, yi, ci = lax.axis_index("x"), lax.axis_index("y"), lax.axis_index("c")
    chip = 2 * xi + yi
    dev = 2 * chip + ci
    c_arr = jnp.reshape(ci, (1,)).astype(jnp.int32)
    x2, tgt = x[0], loss_target[0]

    ws = [_W("w_in", "col", D, IN), _W("w_pool", "row", PW, cg), _W("w_a_up", "col", PW, D),
          _W("w_b_up", "col", PW, D), _W("w_o", "row", D, D), _W("w_ff1", "col", D, FF), _W("w_ff2", "row", FF, D)]
    w32 = [w_in[0], w_pool[0].reshape(cg, cg), w_a_up[0], w_b_up[0], w_o[0], w_ff1[0], w_ff2[0]]
    m32 = [m_w_in[0], m_w_pool[0].reshape(cg, cg), m_w_a_up[0], m_w_b_up[0], m_w_o[0], m_w_ff1[0], m_w_ff2[0]]
    v32 = [v_w_in[0], v_w_pool[0].reshape(cg, cg), v_w_a_up[0], v_w_b_up[0], v_w_o[0], v_w_ff1[0], v_w_ff2[0]]

    W_IN, W_POOL, W_A, W_B, W_O, W_FF1, W_FF2 = ws
    chip_arr = jnp.reshape(chip, (1,)).astype(jnp.int32)
    cc_arr = jnp.stack([ci, chip]).astype(jnp.int32)
    s_in, s_pool, s_a, s_b, s_o, s_ff1, s_ff2 = [_cast_into_full(w, a, chip_arr) for w, a in zip(ws, w32)]
    (win_f,) = _run_rider("gather_w_in", _ag_rider([W_IN], [s_in]))

    sc_row = _silu_rows(c)
    sc_all = _dev_allgather("gather_silu_c", sc_row.reshape(SUBLANES, D // SUBLANES)).reshape(N_DEV, D)
    sc16 = jnp.concatenate([sc_all, jnp.zeros_like(sc_all)], axis=0)
    b_cols = lax.dynamic_slice(b_ada, (0, chip * A_COLS), (1, A_COLS))
    (mod_cols,) = _mm("mod_cols", [(sc16, w_ada[0])], M=2 * N_DEV, N=A_COLS, K=D, tm=16, tn=1024, tk=1024,
                      a_pro=lambda a: a.astype(BF16), b_pro=lambda b: b.astype(BF16),
                      extras=[(b_cols, "row", 0)], outs=[_tile_out(F32)], epi=lambda accs, ex: [accs[0] + ex[0]])
    mod_all = _dev_allgather("gather_mod", mod_cols[:N_DEV]).reshape(N_CHIPS, 2, N_DEV, A_COLS)
    mod_row = lax.dynamic_index_in_dim(mod_all[:, 0], dev, axis=1, keepdims=False).reshape(1, N_CHIPS * A_COLS)
    shift1, scale1, gate1, shift2, scale2, gate2 = [mod_row[:, i * D:(i + 1) * D] for i in range(6)]

    WIDE = dict(tm=2048, tn=512, tk=2048)
    DEEP = dict(tm=1024, tn=1024, tk=2048)
    h = _norm_mod("norm1_mod", x2, norm1_w, scale1, shift1)
    (proj,), ((wpool_f, wa_f, wb_f, wo_f),) = _mm(
        "in_proj", [(h, win_f)], M=S, N=IN, K=D, outs=[_tile_out(BF16)], epi=lambda accs, ex: [accs[0]], **WIDE,
        riders=[_ag_rider([W_POOL, W_A, W_B, W_O], [s_pool, s_a, s_b, s_o], n_ch=2)])
    pooled, pa = _pool_fwd(proj, wpool_f, pool_scale, S, PW)
    (att, attf), ((wff1_f,),) = _attn_fwd(proj, q_norm_w, k_norm_w, S, H, PW // HEAD_DIM,
                                          riders=[_ag_rider([W_FF1], [s_ff1])])

    def merge_epi(accs, ex):
        sa, sb = jax.nn.sigmoid(ex[0].astype(F32)), jax.nn.sigmoid(ex[1].astype(F32))
        return [sa * accs[0] + sb * accs[1], accs[0], accs[1]]

    (merged, ya, yb), (ff2_a,) = _mm("branch_up_merge", [(pa, wa_f), (att, wb_f)], M=S, N=D, K=PW,
                                     extras=[(proj, "tile", 4 * PW), (proj, "tile", 4 * PW + D)],
                                     outs=[_tile_out(BF16)] * 3, epi=merge_epi,
                                     riders=[_ag_rider([W_FF2], [s_ff2], chunks=(0, 1))])
    (x1, o), (ff2_b,) = _mm("out_proj", [(merged, wo_f)], M=S, N=D, K=D, extras=[(x2, "tile", 0), (gate1, "row", 0)],
                            outs=[_tile_out(F32), _tile_out(BF16)], epi=lambda accs, ex: [ex[0] + ex[1] * accs[0], accs[0]],
                            riders=[_ag_rider([W_FF2], ff2_a, chunks=(1, 2))], **WIDE)
    h2 = _norm_mod("norm2_mod", x1, norm2_w, scale2, shift2)
    (rl,), ((wff2_f,),) = _mm("ff1", [(h2, wff1_f)], M=S, N=FF, K=D, outs=[_tile_out(BF16)], **WIDE,
                              epi=lambda accs, ex: [jnp.maximum(accs[0], 0.0)],
                              riders=[_ag_rider([W_FF2], ff2_b, chunks=(2, 4))])

    def square(a):
        af = a.astype(F32)
        return (af * af).astype(BF16)

    def loss_epi(accs, ex):
        x1_t, tgt_t, g2 = ex
        f = accs[0]
        diff = (x1_t + g2 * f) - tgt_t
        dy = diff * (1.0 / D)
        return [dy, dy * g2, _colsum(dy * f), _colsum(diff * diff)]

    dy, df, dgate2_p, loss_p = _mm("ff2_loss", [(rl, wff2_f)], M=S, N=D, K=FF, a_pro=square, **DEEP,
                                   extras=[(x1, "tile", 0), (tgt, "tile", 0), (gate2, "row", 0)],
                                   outs=[_tile_out(F32), _tile_out(BF16), _COLSUM, _COLSUM], epi=loss_epi)

    tied = []

    def behind(token, a):
        a, token = lax.optimization_barrier((a, token))
        tied.append(token)
        return a

    def pair_sums(group, partials, got):
        return [_pair_sum(w, g, r, c_arr) for w, g, r in zip(group, partials, got)]

    def chip_sums(group, sums, from_chips):
        return [_chip_sum(w, p, q, cc_arr) for w, p, q in zip(group, sums, from_chips)]

    first = lambda accs, ex: [accs[0]]
    gmm = dict(ta=True, outs=[_tile_out(BF16)], epi=first, **WIDE)
    (g_ff2,) = _mm("grad_w_ff2", [(rl, df)], M=FF, N=D, K=S, a_pro=square, ta=True, tm=512, tn=2048, tk=2048,
                   outs=[_tile_out(BF16)], epi=first)
    flight, token = _split_start("pair_w_ff2_start", "pair", [W_FF2], [g_ff2])
    (dz1,) = _mm("d_ff_hidden", [(behind(token, df), wff2_f)], M=S, N=FF, K=D, tb=True, extras=[(rl, "tile", 0)],
                 outs=[_tile_out(BF16)], epi=lambda accs, ex: [accs[0] * (2.0 * ex[0].astype(F32))], **WIDE)
    sum_ff2 = pair_sums([W_FF2], *_split_wait("pair_w_ff2_wait", flight, after=[dz1] + tied))
    chip_ff2, token = _split_start("chip_w_ff2_start", "chip", [W_FF2], sum_ff2)
    (g_ff1,) = _mm("grad_w_ff1", [(behind(token, h2), dz1)], M=D, N=FF, K=S, **gmm)
    flight, token = _split_start("pair_w_ff1_start", "pair", [W_FF1], [g_ff1])
    (dh2,) = _mm("d_h2", [(behind(token, dz1), wff1_f)], M=S, N=D, K=FF, tb=True, outs=[_tile_out(F32)], epi=first, **DEEP)
    sum_ff1 = pair_sums([W_FF1], *_split_wait("pair_w_ff1_wait", flight, after=[dh2] + tied))
    chip_ff1, token = _split_start("chip_w_ff1_start", "chip", [W_FF1], sum_ff1)
    dx1, dshift2_p, dscale2_p, gn2_p, do, dgate1_p = _norm_mod_bwd("norm2_bwd", behind(token, dh2), x1, dy, norm2_w, scale2,
                                                                   gate_o=(o, gate1))
    (g_wo,) = _mm("grad_w_o", [(merged, do)], M=D, N=D, K=S, **gmm)

    def gate_epi(accs, ex):
        dm = accs[0]
        sa, sb = jax.nn.sigmoid(ex[0].astype(F32)), jax.nn.sigmoid(ex[1].astype(F32))
        ya_t, yb_t = ex[2].astype(F32), ex[3].astype(F32)
        return [dm * sa, dm * sb, dm * ya_t * (sa * (1.0 - sa)), dm * yb_t * (sb * (1.0 - sb))]

    dya, dyb, dga, dgb = _mm("d_merged", [(do, wo_f)], M=S, N=D, K=D, tb=True, tm=1024, tn=512, tk=2048,
                             extras=[(proj, "tile", 4 * PW), (proj, "tile", 4 * PW + D), (ya, "tile", 0), (yb, "tile", 0)],
                             outs=[_tile_out(BF16)] * 4, epi=gate_epi)
    (g_wa,) = _mm("grad_w_a_up", [(pa, dya)], M=PW, N=D, K=S, **gmm)
    (g_wb,) = _mm("grad_w_b_up", [(att, dyb)], M=PW, N=D, K=S, **gmm)
    (dpa,) = _mm("d_pool_out", [(dya, wa_f)], M=S, N=PW, K=D, tb=True, outs=[_tile_out(F32)], epi=first, **WIDE)
    mid = [W_A, W_B, W_O]
    flight, token = _split_start("pair_mid_start", "pair", mid, [g_wa, g_wb, g_wo])
    (datt,) = _mm("d_att", [(behind(token, dyb), wb_f)], M=S, N=PW, K=D, tb=True, outs=[_tile_out(BF16)], epi=first, **WIDE)
    sum_mid = pair_sums(mid, *_split_wait("pair_mid_wait", flight, after=[datt] + tied))
    chip_mid, token = _split_start("chip_mid_start", "chip", mid, sum_mid)
    du, g_wpool4, gscale_p = _pool_bwd(dpa, pooled, wpool_f, pool_scale, S, PW)
    dq, dk, dv, gq_p, gk_p = _attn_bwd(proj, behind(token, datt), attf, q_norm_w, k_norm_w, S, H, PW // HEAD_DIM)
    dproj = jnp.concatenate([du, dq, dk, dv, dga, dgb], axis=1)
    early = [W_FF1, W_FF2]
    sum_ff1, q_ff1 = _split_wait("chip_w_ff1_wait", chip_ff1, after=[dq] + tied)
    sum_ff2, q_ff2 = _split_wait("chip_w_ff2_wait", chip_ff2, after=[dq] + tied)
    halves_early = chip_sums(early, sum_ff1 + sum_ff2, q_ff1 + q_ff2)
    (g_win,), (grads_early,) = _mm("grad_w_in", [(h, dproj)], M=D, N=IN, K=S, riders=[_sf_rider(early, halves_early)],
                                   **gmm)
    last = [W_IN, W_POOL]
    g_last = [g_win, g_wpool4.reshape(PW, cg)]
    sum_mid, q_mid = _split_wait("chip_mid_wait", chip_mid, after=[g_win] + tied)
    halves_mid = chip_sums(mid, sum_mid, q_mid)
    (dh,), (got_last, grads_mid) = _mm("d_h", [(dproj, win_f)], M=S, N=D, K=IN, tb=True, outs=[_tile_out(F32)], epi=first,
                                       riders=[_px_rider(last, g_last), _sf_rider(mid, halves_mid)], **DEEP)
    sum_last = pair_sums(last, g_last, got_last)
    grad_x, dshift1_p, dscale1_p, gn1_p = _norm_mod_bwd("norm1_bwd", dh, x2, dx1, norm1_w, scale1)

    parts = [dshift1_p, dscale1_p, dgate1_p, dshift2_p, dscale2_p, dgate2_p, gn1_p, gn2_p,
             gscale_p.reshape(1, 1, PW), gq_p, gk_p]
    widths = [D] * 8 + [PW, HEAD_DIM, HEAD_DIM]
    used = sum(widths)
    P = -(-(used + LANES) // (SUBLANES * LANES)) * (SUBLANES * LANES)
    packed = _pack_partials(parts + [loss_p], widths, P)
    gathered = _dev_allgather("gather_vector_grads", packed.reshape(SUBLANES, P // SUBLANES)).reshape(N_DEV, P)
    sum_last, gathered = lax.optimization_barrier((sum_last, gathered))
    chip_last, token = _split_start("chip_last_start", "chip", last, sum_last)
    small = [(b_ada, m_b_ada, v_b_ada), (norm1_w, m_norm1_w, v_norm1_w), (norm2_w, m_norm2_w, v_norm2_w),
             (pool_scale, m_pool_scale, v_pool_scale), (q_norm_w, m_q_norm_w, v_q_norm_w),
             (k_norm_w, m_k_norm_w, v_k_norm_w)]
    offsets = [(0, 6 * D), (6 * D, D), (7 * D, D), (8 * D, PW), (8 * D + PW, HEAD_DIM), (8 * D + PW + HEAD_DIM, HEAD_DIM)]
    su = _small_update(gathered, offsets, small, used)
    (g_b, d_b, nm_b, nv_b, g_n1, d_n1, nm_n1, nv_n1, g_n2, d_n2, nm_n2, nv_n2, g_ps, d_ps, nm_ps, nv_ps,
     g_qn, d_qn, nm_qn, nv_qn, g_kn, d_kn, nm_kn, nv_kn, loss_sum) = su
    dmod_sh = lax.dynamic_slice(gathered, (0, chip * A_COLS), (N_DEV, A_COLS))
    dmod_sh, token = lax.optimization_barrier((dmod_sh, token))
    g_ada, d_ada, nm_ada, nv_ada = _ada_update(sc_all.T, dmod_sh, w_ada[0], m_w_ada[0], v_w_ada[0])

    upd_done = [_adamw("adamw_" + w.name, a, g, m, v, after=token)
                for w, a, g, m, v in zip(ws[2:], w32[2:], list(grads_mid) + list(grads_early), m32[2:], v32[2:])]

    sum_last, q_last = _split_wait("chip_last_wait", chip_last, after=[nv_ada] + [u[3] for u in upd_done])
    halves_last = chip_sums(last, sum_last, q_last)
    filled = _run_rider("grad_sibling_fill", _sf_rider(last, halves_last))
    upd = [_adamw("adamw_" + w.name, a, g, m, v) for w, a, g, m, v in zip(ws[:2], w32[:2], filled, m32[:2], v32[:2])]
    upd += upd_done

    loss = (0.5 / D) * loss_sum[0, 0]

    def up(a):
        return a[None]

    def pool4(a):
        return a.reshape(1, N_GROUPS, cg // N_CHIPS, cg)

    (gr_win, d_win, nm_win, nv_win), (gr_wp, d_wp, nm_wp, nv_wp), (gr_wa, d_wa, nm_wa, nv_wa), \
        (gr_wb, d_wb, nm_wb, nv_wb), (gr_wo, d_wo, nm_wo, nv_wo), (gr_f1, d_f1, nm_f1, nv_f1), \
        (gr_f2, d_f2, nm_f2, nv_f2) = upd
    return (
        loss, grad_x[None],
        up(g_ada), g_b, g_n1, up(gr_win), g_qn, g_kn, pool4(gr_wp), g_ps, up(gr_wa), up(gr_wb), up(gr_wo), g_n2,
        up(gr_f1), up(gr_f2),
        up(d_ada), d_b, d_n1, up(d_win), d_qn, d_kn, pool4(d_wp), d_ps, up(d_wa), up(d_wb), up(d_wo), d_n2,
        up(d_f1), up(d_f2),
        up(nm_ada), nm_b, nm_n1, up(nm_win), nm_qn, nm_kn, pool4(nm_wp), nm_ps, up(nm_wa), up(nm_wb), up(nm_wo), nm_n2,
        up(nm_f1), up(nm_f2),
        up(nv_ada), nv_b, nv_n1, up(nv_win), nv_qn, nv_kn, pool4(nv_wp), nv_ps, up(nv_wa), up(nv_wb), up(nv_wo), nv_n2,
        up(nv_f1), up(nv_f2),
    )
```

```python
import functools
import math

import jax
import jax.numpy as jnp
from jax import lax
from jax.experimental import pallas as pl
from jax.experimental.pallas import tpu as pltpu

F32 = jnp.float32
BF16 = jnp.bfloat16
MESH = pl.DeviceIdType.MESH
ANY = pl.BlockSpec(memory_space=pl.ANY)

EPS = 1e-6
HEAD_DIM = 128
LANES, SUBLANES = 128, 8
POOL_WINDOWS = (2, 4, 8, 16)
N_GROUPS = len(POOL_WINDOWS)
assert POOL_WINDOWS == tuple(2 << g for g in range(N_GROUPS))
N_CHIPS = 4
N_DEV = 8
ADAM_LR, ADAM_B1, ADAM_B2, ADAM_EPS, ADAM_WD, ADAM_STEP = 0.001, 0.9, 0.999, 1e-08, 0.01, 10
VMEM_LIMIT_V7X = 56 * 1024 * 1024
ATT_T = 256
ATT_GROUP = 4
POOL_T = 256


def _pcall(body, **kw):
    return pl.pallas_call(body, **kw)


def _params(sem=None):
    return pltpu.CompilerParams(dimension_semantics=sem, vmem_limit_bytes=VMEM_LIMIT_V7X)


def _tile(n, pref):
    if n <= pref:
        return n
    t = pref
    while n % t:
        t //= 2
    return t


class _Rider:
    def __init__(self, arrays, out_shape, sems, start, finish, aliases=None, steps=()):
        self.arrays, self.out_shape, self.sems = list(arrays), list(out_shape), list(sems)
        self.start, self.finish, self.aliases, self.steps = start, finish, aliases or {}, list(steps)


def _ride(name, body, riders, arrays, *, grid, in_specs, out_specs, out_shape, scratch_shapes, sem, scalars=None):
    n_in, n_out, n_scr = len(arrays), len(out_shape), len(scratch_shapes)
    r_arrays = [a for r in riders for a in r.arrays]
    r_outs = [o for r in riders for o in r.out_shape]
    r_sems = [s for r in riders for s in r.sems]
    n_hooks = max([len(r.steps) for r in riders], default=0)
    total = math.prod(grid)
    aliases, off_i, off_o = {}, n_in + (scalars is not None), n_out
    for r in riders:
        for a, o in r.aliases.items():
            aliases[off_i + a] = off_o + o
        off_i += len(r.arrays)
        off_o += len(r.out_shape)

    def full(*refs):
        p = 0
        groups = []
        for n in (n_in, len(r_arrays), n_out, len(r_outs), n_scr, len(r_sems)):
            groups.append(refs[p:p + n])
            p += n
        ins, rin, outs, rout, scr, rsem = groups

        def each(what):
            a = o = s = 0
            for r in riders:
                fn = what(r)
                if fn is not None:
                    fn(rin[a:a + len(r.arrays)], rout[o:o + len(r.out_shape)], rsem[s:s + len(r.sems)])
                a, o, s = a + len(r.arrays), o + len(r.out_shape), s + len(r.sems)

        if riders:
            lin = 0
            for d, g in enumerate(grid):
                lin = lin * g + pl.program_id(d)
            pl.when(lin == 0)(lambda: each(lambda r: r.start))
            for t in range(n_hooks):
                pl.when(lin == min(total - 1, ((t + 1) * total) // n_hooks))(
                    lambda t=t: each(lambda r: r.steps[t] if t < len(r.steps) else None))
        body(*ins, *outs, *scr)
        if riders:
            pl.when(lin == total - 1)(lambda: each(lambda r: r.finish))

    specs = dict(grid=grid, in_specs=list(in_specs) + [ANY] * len(r_arrays),
                 out_specs=list(out_specs) + [ANY] * len(r_outs), scratch_shapes=list(scratch_shapes) + r_sems)
    common = dict(name=name, out_shape=list(out_shape) + r_outs, input_output_aliases=aliases,
                  compiler_params=_params(("arbitrary",) * len(grid) if riders else sem))
    if scalars is None:
        res = _pcall(full, **specs, **common)(*arrays, *r_arrays)
    else:
        res = _pcall(lambda _, *refs: full(*refs), **common,
                     grid_spec=pltpu.PrefetchScalarGridSpec(num_scalar_prefetch=1, **specs))(scalars, *arrays, *r_arrays)
    if not riders:
        return res
    main, rest, per = res[:n_out], res[n_out:], []
    for r in riders:
        per.append(rest[:len(r.out_shape)])
        rest = rest[len(r.out_shape):]
    return main, per


def _run_rider(name, rider):
    def body(*refs):
        n_a, n_o = len(rider.arrays), len(rider.out_shape)
        ins, outs, sems = refs[:n_a], refs[n_a:n_a + n_o], refs[n_a + n_o:]
        for fn in [rider.start] + rider.steps + [rider.finish]:
            fn(ins, outs, sems)

    return _pcall(body, name=name, out_shape=rider.out_shape, in_specs=[ANY] * len(rider.arrays),
                  out_specs=[ANY] * len(rider.out_shape), scratch_shapes=rider.sems,
                  input_output_aliases=rider.aliases)(*rider.arrays)


def _mm(name, pairs, *, M, N, K, ta=False, tb=False, tm=512, tn=1024, tk=1024,
        a_pro=None, b_pro=None, extras=(), outs, epi, riders=()):
    tm, tn, tk = _tile(M, tm), _tile(N, tn), _tile(K, tk)
    n_i, n_j, n_k = M // tm, N // tn, K // tk
    n_p, n_e = len(pairs), len(extras)
    arrays, in_specs = [], []
    for a, _ in pairs:
        arrays.append(a)
        in_specs.append(pl.BlockSpec((tk, tm), lambda i, j, k: (k, i)) if ta
                        else pl.BlockSpec((tm, tk), lambda i, j, k: (i, k)))
    for _, b in pairs:
        arrays.append(b)
        in_specs.append(pl.BlockSpec((tn, tk), lambda i, j, k: (j, k)) if tb
                        else pl.BlockSpec((tk, tn), lambda i, j, k: (k, j)))
    for arr, kind, off in extras:
        ob = off // tn
        assert off % tn == 0
        arrays.append(arr)
        if kind == "tile":
            in_specs.append(pl.BlockSpec((tm, tn), lambda i, j, k, ob=ob: (i, j + ob)))
        else:
            in_specs.append(pl.BlockSpec((1, tn), lambda i, j, k, ob=ob: (0, j + ob)))
    out_shape, out_specs = [], []
    for o in outs:
        if o["kind"] == "tile":
            out_shape.append(jax.ShapeDtypeStruct((M, N), o["dtype"]))
            out_specs.append(pl.BlockSpec((tm, tn), lambda i, j, k: (i, j)))
        else:
            out_shape.append(jax.ShapeDtypeStruct((n_i, 1, N), F32))
            out_specs.append(pl.BlockSpec((1, 1, tn), lambda i, j, k: (i, 0, j)))
    dims = (((0 if ta else 1,), (1 if tb else 0,)), ((), ()))

    def body(*refs):
        a_refs, b_refs = refs[:n_p], refs[n_p:2 * n_p]
        e_refs = refs[2 * n_p:2 * n_p + n_e]
        o_refs = refs[2 * n_p + n_e:2 * n_p + n_e + len(outs)]
        acc_refs = refs[2 * n_p + n_e + len(outs):]

        def product(p):
            a, b = a_refs[p][...], b_refs[p][...]
            if a_pro is not None:
                a = a_pro(a)
            if b_pro is not None:
                b = b_pro(b)
            return lax.dot_general(a, b, dims, preferred_element_type=F32)

        def write(accs):
            vals = epi(accs, [e[...] for e in e_refs])
            for o, o_ref, val in zip(outs, o_refs, vals):
                if o["kind"] == "tile":
                    o_ref[...] = val.astype(o_ref.dtype)
                else:
                    o_ref[0] = val

        if n_k == 1:
            write([product(p) for p in range(n_p)])
            return
        k = pl.program_id(2)

        @pl.when(k == 0)
        def _():
            for acc in acc_refs:
                acc[...] = jnp.zeros_like(acc)

        for p in range(n_p):
            acc_refs[p][...] += product(p)

        pl.when(k == n_k - 1)(lambda: write([acc[...] for acc in acc_refs]))

    return _ride(name, body, riders, arrays, grid=(n_i, n_j, n_k), in_specs=in_specs, out_specs=out_specs,
                 out_shape=out_shape, scratch_shapes=[pltpu.VMEM((tm, tn), F32) for _ in pairs] if n_k > 1 else [],
                 sem=("parallel", "parallel", "arbitrary"))


def _tile_out(dtype):
    return {"kind": "tile", "dtype": dtype}


_COLSUM = {"kind": "colsum"}


def _colsum(v):
    return jnp.sum(v, axis=0, keepdims=True)


def _norm_mod(name, x, norm_w, scale, shift):
    S, D = x.shape
    tr = _tile(S, 256)

    def body(x_ref, nw_ref, sc_ref, sh_ref, h_ref):
        xv = x_ref[...]
        r = lax.rsqrt(jnp.mean(xv * xv, axis=-1, keepdims=True) + EPS)
        h_ref[...] = ((xv * r * nw_ref[...]) * (1.0 + sc_ref[...]) + sh_ref[...]).astype(BF16)

    row = pl.BlockSpec((1, D), lambda i: (0, 0))
    til = pl.BlockSpec((tr, D), lambda i: (i, 0))
    return _pcall(body, name=name, grid=(S // tr,), in_specs=[til, row, row, row], out_specs=til,
                  out_shape=jax.ShapeDtypeStruct((S, D), BF16), compiler_params=_params(("parallel",)))(
                      x, norm_w, scale, shift)


def _norm_mod_bwd(name, dh, x, dres, norm_w, scale, gate_o=None):
    S, D = x.shape
    tr = _tile(S, 256)
    n_r = S // tr
    with_gate = gate_o is not None

    def body(*refs):
        if with_gate:
            dh_ref, x_ref, dres_ref, nw_ref, sc_ref, o_ref, g_ref, dx_ref, p1, p2, p3, do_ref, p4 = refs
        else:
            dh_ref, x_ref, dres_ref, nw_ref, sc_ref, dx_ref, p1, p2, p3 = refs
        dhv, xv, nw = dh_ref[...], x_ref[...], nw_ref[...]
        r = lax.rsqrt(jnp.mean(xv * xv, axis=-1, keepdims=True) + EPS)
        xh = xv * r
        p1[0] = _colsum(dhv)
        p2[0] = _colsum(dhv * (xh * nw))
        dn = dhv * (1.0 + sc_ref[...])
        p3[0] = _colsum(dn * xh)
        dxh = dn * nw
        dx = dres_ref[...] + r * (dxh - xh * jnp.mean(dxh * xh, axis=-1, keepdims=True))
        dx_ref[...] = dx
        if with_gate:
            do_ref[...] = (dx * g_ref[...]).astype(BF16)
            p4[0] = _colsum(dx * o_ref[...].astype(F32))

    row = pl.BlockSpec((1, D), lambda i: (0, 0))
    til = pl.BlockSpec((tr, D), lambda i: (i, 0))
    part = pl.BlockSpec((1, 1, D), lambda i: (i, 0, 0))
    part_shape = jax.ShapeDtypeStruct((n_r, 1, D), F32)
    in_specs = [til, til, til, row, row]
    arrays = [dh, x, dres, norm_w, scale]
    out_specs = [til, part, part, part]
    out_shape = [jax.ShapeDtypeStruct((S, D), F32), part_shape, part_shape, part_shape]
    if with_gate:
        in_specs += [til, row]
        arrays += list(gate_o)
        out_specs += [til, part]
        out_shape += [jax.ShapeDtypeStruct((S, D), BF16), part_shape]
    return _pcall(body, name=name, grid=(n_r,), in_specs=in_specs, out_specs=out_specs, out_shape=out_shape,
                  compiler_params=_params(("parallel",)))(*arrays)


def _pool_w_specs(rows, cg):
    return [pl.BlockSpec((rows, cg), lambda g, j=j: (N_GROUPS * j + g, 0)) for j in range(N_CHIPS)]


def _pool_fwd(proj, wp_full, pool_scale, S, PW):
    cg = PW // N_GROUPS
    rows = cg // N_CHIPS
    T = _tile(S, POOL_T)
    n_t = S // T

    def body(u_ref, w0, w1, w2, w3, ps_ref, pooled_ref, pa_ref):
        g = pl.program_id(0)
        win = jnp.left_shift(2, g)
        w = jnp.concatenate([w0[...], w1[...], w2[...], w3[...]], axis=0)
        t_i = lax.broadcasted_iota(jnp.int32, (T, T), 0)
        j_i = lax.broadcasted_iota(jnp.int32, (T, T), 1)
        b_cur = ((j_i <= t_i) & (j_i > t_i - win)).astype(BF16)
        b_prev = (j_i - T > t_i - win).astype(BF16)
        row = lax.broadcasted_iota(jnp.int32, (T, 1), 0)
        for r in range(n_t):
            cur = u_ref[r * T:(r + 1) * T, :]
            ws = jnp.dot(b_cur, cur, preferred_element_type=F32)
            if r > 0:
                ws += jnp.dot(b_prev, u_ref[(r - 1) * T:r * T, :], preferred_element_type=F32)
            count = jnp.minimum(row + (r * T + 1), win).astype(F32)
            pooled = (ws / count - cur.astype(F32)).astype(BF16)
            pooled_ref[r * T:(r + 1) * T, :] = pooled
            mixed = jnp.dot(pooled, w, preferred_element_type=F32)
            pa_ref[r * T:(r + 1) * T, :] = (mixed * ps_ref[...]).astype(BF16)

    col = pl.BlockSpec((S, cg), lambda g: (0, g))
    return _pcall(
        body, name="pool_fwd", grid=(N_GROUPS,),
        in_specs=[col] + _pool_w_specs(rows, cg) + [pl.BlockSpec((1, cg), lambda g: (0, g))],
        out_specs=[col, col],
        out_shape=[jax.ShapeDtypeStruct((S, PW), BF16), jax.ShapeDtypeStruct((S, PW), BF16)],
        compiler_params=_params(("parallel",)),
    )(proj, wp_full, wp_full, wp_full, wp_full, pool_scale)


def _pool_bwd(dpa, pooled, wp_full, pool_scale, S, PW):
    cg = PW // N_GROUPS
    rows = cg // N_CHIPS
    T = _tile(S, POOL_T)
    n_t = S // T

    def body(dpa_ref, pooled_ref, w0, w1, w2, w3, ps_ref, du_ref, gw_ref, gs_ref, dp_s, dpc_s, dmx_s):
        g = pl.program_id(0)
        win = jnp.left_shift(2, g)
        w = jnp.concatenate([w0[...], w1[...], w2[...], w3[...]], axis=0)
        row = lax.broadcasted_iota(jnp.int32, (T, 1), 0)
        gs = jnp.zeros((1, cg), F32)
        for r in range(n_t):
            sl = slice(r * T, (r + 1) * T)
            mixed = jnp.dot(pooled_ref[sl, :], w, preferred_element_type=F32)
            dpa_t = dpa_ref[sl, :]
            gs += _colsum(dpa_t * mixed)
            dmx = (dpa_t * ps_ref[...]).astype(BF16)
            dmx_s[sl, :] = dmx
            dpo = lax.dot_general(dmx, w, (((1,), (1,)), ((), ())), preferred_element_type=F32)
            dp_s[sl, :] = dpo
            count = jnp.minimum(row + (r * T + 1), win).astype(F32)
            dpc_s[sl, :] = (dpo / count).astype(BF16)
        gs_ref[...] = gs
        gw = lax.dot_general(pooled_ref[...], dmx_s[...], (((0,), (0,)), ((), ())), preferred_element_type=F32)
        for j in range(N_CHIPS):
            gw_ref[j, 0] = gw[j * rows:(j + 1) * rows, :].astype(BF16)
        j_i = lax.broadcasted_iota(jnp.int32, (T, T), 0)
        t_i = lax.broadcasted_iota(jnp.int32, (T, T), 1)
        b_cur = ((t_i >= j_i) & (t_i < j_i + win)).astype(BF16)
        b_next = (t_i + T < j_i + win).astype(BF16)
        for r in range(n_t):
            sl = slice(r * T, (r + 1) * T)
            acc = jnp.dot(b_cur, dpc_s[sl, :], preferred_element_type=F32)
            if r + 1 < n_t:
                acc += jnp.dot(b_next, dpc_s[(r + 1) * T:(r + 2) * T, :], preferred_element_type=F32)
            du_ref[sl, :] = (acc - dp_s[sl, :]).astype(BF16)

    col = pl.BlockSpec((S, cg), lambda g: (0, g))
    return _pcall(
        body, name="pool_bwd", grid=(N_GROUPS,),
        in_specs=[col, col] + _pool_w_specs(rows, cg) + [pl.BlockSpec((1, cg), lambda g: (0, g))],
        out_specs=[col, pl.BlockSpec((N_CHIPS, 1, rows, cg), lambda g: (0, g, 0, 0)),
                   pl.BlockSpec((1, cg), lambda g: (0, g))],
        out_shape=[jax.ShapeDtypeStruct((S, PW), BF16),
                   jax.ShapeDtypeStruct((N_CHIPS, N_GROUPS, rows, cg), BF16),
                   jax.ShapeDtypeStruct((1, PW), F32)],
        scratch_shapes=[pltpu.VMEM((S, cg), F32), pltpu.VMEM((S, cg), BF16), pltpu.VMEM((S, cg), BF16)],
        compiler_params=_params(("parallel",)),
    )(dpa, pooled, wp_full, wp_full, wp_full, wp_full, pool_scale)


_NT = (((1,), (1,)), ((), ()))
_TN = (((0,), (0,)), ((), ()))


def _split_dot(v, tri):
    hi = v.astype(BF16)
    lo = (v - hi.astype(F32)).astype(BF16)
    return jnp.dot(hi, tri, preferred_element_type=F32) + jnp.dot(lo, tri, preferred_element_type=F32)


LOG2E = 1.4426950408889634
QK_SCALE = 1.0 / math.sqrt(HEAD_DIM)


def _sb_scores(q2_i, k_j, tri_l, masked):
    tq, tk = q2_i.shape[0], k_j.shape[0]
    s = lax.dot_general(q2_i, k_j, _NT, preferred_element_type=F32)
    lp = jnp.log(1.0 + jnp.exp2(-jnp.abs(s))) * LOG2E
    lb = jnp.minimum(s, 0.0) - lp
    l = lb - s
    mask = None
    if masked:
        mask = lax.broadcasted_iota(jnp.int32, (tq, tk), 0) > lax.broadcasted_iota(jnp.int32, (tq, tk), 1)
        l = jnp.where(mask, l, 0.0)
    return l, lb, lb + _split_dot(l, tri_l), mask


def _sb_weights(t, carry_l, mask):
    a = jnp.exp2(t + carry_l)
    return a if mask is None else jnp.where(mask, a, 0.0)


def _rowsum(v):
    return jnp.sum(v, axis=1, keepdims=True)


def _qk_norm(x_ref, w_ref):
    xv = x_ref[...].astype(F32)
    r = lax.rsqrt(jnp.mean(xv * xv, axis=-1, keepdims=True) + EPS)
    return xv * r, r


def _attn_fwd(proj, q_norm_w, k_norm_w, S, H, q_off, riders=()):
    t = _tile(S, ATT_T)
    n_q = S // t

    def body(q_ref, k_ref, v_ref, qw_ref, kw_ref, att_ref, attf_ref, qn_s, kn_s):
        qh, _ = _qk_norm(q_ref, qw_ref)
        qn_s[...] = (qh * qw_ref[...] * (QK_SCALE * LOG2E)).astype(BF16)
        kh, _ = _qk_norm(k_ref, kw_ref)
        kn_s[...] = (kh * kw_ref[...]).astype(BF16)
        tri_l = (lax.broadcasted_iota(jnp.int32, (t, t), 0) > lax.broadcasted_iota(jnp.int32, (t, t), 1)).astype(BF16)

        def rows(j):
            return pl.ds(pl.multiple_of(j * t, t), t)

        def q_step(i, _):
            q_i = qn_s[rows(i), :]

            def av(a, j):
                return jnp.dot(a.astype(BF16), v_ref[rows(j), :], preferred_element_type=F32)

            l, _, tt, mask = _sb_scores(q_i, kn_s[rows(i), :], tri_l, True)
            acc = av(_sb_weights(tt, 0.0, mask), i)
            carry = _rowsum(l)

            def single(_, c):
                carry, acc = c
                l, _, tt, _ = _sb_scores(q_i, kn_s[rows(i - 1), :], tri_l, False)
                return carry + _rowsum(l), acc + av(_sb_weights(tt, carry, None), i - 1)

            carry, acc = lax.fori_loop(0, i % 2, single, (carry, acc))
            top = i - 1 - i % 2

            def pair(p, c):
                carry, acc = c
                j0 = top - 2 * p
                l0, _, t0, _ = _sb_scores(q_i, kn_s[rows(j0), :], tri_l, False)
                l1, _, t1, _ = _sb_scores(q_i, kn_s[rows(j0 - 1), :], tri_l, False)
                mid = carry + _rowsum(l0)
                acc = acc + av(_sb_weights(t0, carry, None), j0) + av(_sb_weights(t1, mid, None), j0 - 1)
                return mid + _rowsum(l1), acc

            _, acc = lax.fori_loop(0, i // 2, pair, (carry, acc))
            att_ref[rows(i), :] = acc.astype(BF16)
            attf_ref[rows(i), :] = acc
            return 0

        lax.fori_loop(0, n_q, q_step, 0)

    def col(off):
        return pl.BlockSpec((S, HEAD_DIM), lambda h, off=off: (0, off + h))

    wspec = pl.BlockSpec((1, HEAD_DIM), lambda h: (0, 0))
    return _ride(
        "attn_fwd", body, riders, [proj, proj, proj, q_norm_w, k_norm_w], grid=(H,),
        in_specs=[col(q_off), col(q_off + H), col(q_off + 2 * H), wspec, wspec],
        out_specs=[col(0), col(0)],
        out_shape=[jax.ShapeDtypeStruct((S, H * HEAD_DIM), BF16), jax.ShapeDtypeStruct((S, H * HEAD_DIM), F32)],
        scratch_shapes=[pltpu.VMEM((S, HEAD_DIM), BF16), pltpu.VMEM((S, HEAD_DIM), BF16)],
        sem=("parallel",))


def _attn_bwd(proj, datt, attf, q_norm_w, k_norm_w, S, H, q_off, riders=()):
    t = _tile(S, ATT_T)
    n_q = S // t

    def body(q_ref, k_ref, v_ref, do_ref, o_ref, qw_ref, kw_ref, dq_ref, dk_ref, dv_ref, gq_ref, gk_ref,
             qn_s, kn_s, qz_s, kz_s, dk_s, dv_s, gq_s):
        qw, kw = qw_ref[...], kw_ref[...]
        qh, _ = _qk_norm(q_ref, qw_ref)
        qn_s[...] = (qh * qw * (QK_SCALE * LOG2E)).astype(BF16)
        qz_s[...] = (qh * qw * QK_SCALE).astype(BF16)
        kh, _ = _qk_norm(k_ref, kw_ref)
        kn_s[...] = (kh * kw).astype(BF16)
        kz_s[...] = (kh * kw * QK_SCALE).astype(BF16)
        dk_s[...] = jnp.zeros_like(dk_s)
        dv_s[...] = jnp.zeros_like(dv_s)
        gq_s[...] = jnp.zeros_like(gq_s)
        r_i = lax.broadcasted_iota(jnp.int32, (t, t), 0)
        c_i = lax.broadcasted_iota(jnp.int32, (t, t), 1)
        tri_l = (r_i > c_i).astype(BF16)
        tri_e = (r_i >= c_i).astype(BF16)

        def rows(j):
            return pl.ds(pl.multiple_of(j * t, t), t)

        def q_step(i, _):
            q_i = qn_s[rows(i), :]
            do_i = do_ref[rows(i), :]
            d_i = _rowsum(do_i.astype(F32) * o_ref[rows(i), :])

            def scores(j, masked):
                l, lb, tt, mask = _sb_scores(q_i, kn_s[rows(j), :], tri_l, masked)
                da = lax.dot_general(do_i, v_ref[rows(j), :], _NT, preferred_element_type=F32)
                return l, lb, tt, mask, da

            def grads(j, sc, carry_l, carry_e, dq_acc):
                l, lb, tt, mask, da = sc
                a_bf = _sb_weights(tt, carry_l, mask).astype(BF16)
                e = da * a_bf.astype(F32)
                p = (d_i - carry_e) - _split_dot(e, tri_e)
                dz = e - jnp.exp2(lb) * (e + p)
                if mask is not None:
                    dz = jnp.where(mask, dz, 0.0)
                dz = dz.astype(BF16)
                dk_s[rows(j), :] += lax.dot_general(dz, qz_s[rows(i), :], _TN, preferred_element_type=F32)
                dv_s[rows(j), :] += lax.dot_general(a_bf, do_i, _TN, preferred_element_type=F32)
                return (carry_l + _rowsum(l), carry_e + _rowsum(e),
                        dq_acc + jnp.dot(dz, kz_s[rows(j), :], preferred_element_type=F32))

            zero = jnp.zeros((t, 1), F32)
            first = (zero, zero, jnp.zeros((t, HEAD_DIM), F32))

            def group(js, diagonal_first, c):
                scs = [scores(j, diagonal_first and n == 0) for n, j in enumerate(js)]
                for j, sc in zip(js, scs):
                    c = grads(j, sc, *c)
                return c

            n_first = i % ATT_GROUP
            c = lax.switch(n_first, [functools.partial(group, [i - u for u in range(n + 1)], True, first)
                                     for n in range(ATT_GROUP)])
            top = i - 1 - n_first

            def whole(p, c):
                j0 = top - ATT_GROUP * p
                return group([j0 - u for u in range(ATT_GROUP)], False, c)

            _, _, dqn = lax.fori_loop(0, (i - n_first) // ATT_GROUP, whole, c)
            qv = q_ref[rows(i), :].astype(F32)
            r = lax.rsqrt(jnp.mean(qv * qv, axis=-1, keepdims=True) + EPS)
            xh = qv * r
            gq_s[...] += _colsum(dqn * xh)
            dxh = dqn * qw
            dq_ref[rows(i), :] = (r * (dxh - xh * jnp.mean(dxh * xh, axis=-1, keepdims=True))).astype(BF16)
            return 0

        lax.fori_loop(0, n_q, q_step, 0)
        gq_ref[0] = gq_s[...]
        kh, rk = _qk_norm(k_ref, kw_ref)
        dkn = dk_s[...]
        gk_ref[0] = _colsum(dkn * kh)
        dxh = dkn * kw
        dk_ref[...] = (rk * (dxh - kh * jnp.mean(dxh * kh, axis=-1, keepdims=True))).astype(BF16)
        dv_ref[...] = dv_s[...].astype(BF16)

    def col(off):
        return pl.BlockSpec((S, HEAD_DIM), lambda h, off=off: (0, off + h))

    wspec = pl.BlockSpec((1, HEAD_DIM), lambda h: (0, 0))
    gspec = pl.BlockSpec((1, 1, HEAD_DIM), lambda h: (h, 0, 0))
    act = jax.ShapeDtypeStruct((S, H * HEAD_DIM), BF16)
    gsh = jax.ShapeDtypeStruct((H, 1, HEAD_DIM), F32)
    return _ride(
        "attn_bwd", body, riders, [proj, proj, proj, datt, attf, q_norm_w, k_norm_w], grid=(H,),
        in_specs=[col(q_off), col(q_off + H), col(q_off + 2 * H), col(0), col(0), wspec, wspec],
        out_specs=[col(0), col(0), col(0), gspec, gspec],
        out_shape=[act, act, act, gsh, gsh],
        scratch_shapes=[pltpu.VMEM((S, HEAD_DIM), BF16)] * 4 + [pltpu.VMEM((S, HEAD_DIM), F32)] * 2
        + [pltpu.VMEM((1, HEAD_DIM), F32)],
        sem=("parallel",))


def _place():
    x, y, c = lax.axis_index("x"), lax.axis_index("y"), lax.axis_index("c")
    chips = [(1 - x, y), (x, 1 - y), (1 - x, 1 - y)]
    return x, y, c, chips


def _dev_allgather(name, v):
    m_per, n = v.shape

    def body(x_ref, out_ref, send_sems, recv_sems, local_sem):
        x, y, c, chips = _place()
        me, sibling = (x, y, c), (x, y, 1 - c)

        def rows(px, py, pc):
            return out_ref.at[pl.ds((4 * px + 2 * py + pc) * m_per, m_per), :]

        def copy(k, block, to, src=None):
            return pltpu.make_async_remote_copy(
                src_ref=rows(*block) if src is None else src, dst_ref=rows(*block),
                send_sem=send_sems.at[k], recv_sem=recv_sems.at[k], device_id=to, device_id_type=MESH)

        mine = pltpu.make_async_copy(x_ref, rows(*me), local_sem)
        mine.start()
        first = [copy(0, me, sibling, src=x_ref)]
        first += [copy(1 + j, me, (*chip, c), src=x_ref) for j, chip in enumerate(chips)]
        for cp in first:
            cp.start()
        passed = [copy(4 + j, (*chip, c), sibling) for j, chip in enumerate(chips)]
        for j, chip in enumerate(chips):
            copy(1 + j, (*chip, c), me).wait_recv()
            passed[j].start()
        copy(0, sibling, me).wait_recv()
        for j, chip in enumerate(chips):
            copy(4 + j, (*chip, 1 - c), me).wait_recv()
        for cp in first + passed:
            cp.wait_send()
        mine.wait()

    return _pcall(
        body, name=name, out_shape=jax.ShapeDtypeStruct((N_DEV * m_per, n), v.dtype),
        in_specs=[pl.BlockSpec(memory_space=pltpu.VMEM)], out_specs=pl.BlockSpec(memory_space=pltpu.VMEM),
        scratch_shapes=[pltpu.SemaphoreType.DMA((7,)), pltpu.SemaphoreType.DMA((7,)), pltpu.SemaphoreType.DMA],
        compiler_params=pltpu.CompilerParams(vmem_limit_bytes=VMEM_LIMIT_V7X),
    )(v)


class _W:
    def __init__(self, name, kind, R, C):
        self.name, self.kind, self.R, self.C = name, kind, R, C

    @property
    def shard_shape(self):
        return (self.R, self.C // N_CHIPS) if self.kind == "col" else (self.R // N_CHIPS, self.C)

    @property
    def half_rows(self):
        return self.shard_shape[0] // 2

    def shard_half(self, ref, half):
        return ref.at[pl.ds(half * self.half_rows, self.half_rows), :]

    def region(self, full_ref, chip, half):
        hr = self.half_rows
        if self.kind == "col":
            cw = self.C // N_CHIPS
            return full_ref.at[pl.ds(half * hr, hr), pl.ds(chip * cw, cw)]
        return full_ref.at[pl.ds(chip * (2 * hr) + half * hr, hr), :]


def _ag_rider(ws, fulls, n_ch=4, chunks=None):
    n_w = len(ws)
    lo, hi = chunks or (0, n_ch)
    per = 6

    def parts(full, sems):
        send_sems, recv_sems = sems
        x, y, c, _ = _place()
        xn, yn, dg = (1 - x, y), (x, 1 - y), (1 - x, 1 - y)
        via = (x + (1 - c) * (1 - 2 * x), y + c * (1 - 2 * y))
        to = (x + c * (1 - 2 * x), y + (1 - c) * (1 - 2 * y))

        def reg(i, chip, half, t):
            nr = ws[i].half_rows // n_ch
            return ws[i].region(full[i], 2 * chip[0] + chip[1], half).at[pl.ds(t * nr, nr), :]

        def copy(r, i, t, k, dev):
            s = (i * (hi - lo) + t - lo) * per + k
            return pltpu.make_async_remote_copy(src_ref=r, dst_ref=r, send_sem=send_sems.at[s],
                                                recv_sem=recv_sems.at[s], device_id=dev, device_id_type=MESH)

        def direct(i, t, k):
            return copy(reg(i, (x, y), c, t), i, t, k, (*(via, to)[k], c))

        def direct_in(i, t, k):
            return copy(reg(i, (via, to)[k], c, t), i, t, k, (*(via, to)[k], c))

        def relay(i, t):
            return copy(reg(i, via, c, t), i, t, 2, (*to, c))

        def relay_in(i, t):
            return copy(reg(i, dg, c, t), i, t, 2, (*to, c))

        def hand(i, t, k, half):
            return copy(reg(i, (xn, yn, dg)[k], half, t), i, t, 3 + k, (x, y, 1 - c))

        return c, direct, direct_in, relay, relay_in, hand

    def start(_, full, sems):
        _, direct, _, _, _, _ = parts(full, sems)
        for t in range(lo, hi):
            for i in range(n_w):
                direct(i, t, 0).start()
                direct(i, t, 1).start()

    def arrived(t):
        def step(_, full, sems):
            c, _, direct_in, relay, relay_in, hand = parts(full, sems)
            for i in range(n_w):
                direct_in(i, t, 0).wait_recv()
                direct_in(i, t, 1).wait_recv()
                relay(i, t).start()
                hand(i, t, 0, c).start()
                hand(i, t, 1, c).start()
        return step

    def finish(_, full, sems):
        c, direct, _, relay, relay_in, hand = parts(full, sems)
        for t in range(lo, hi):
            for i in range(n_w):
                relay_in(i, t).wait_recv()
                hand(i, t, 2, c).start()
        for i in range(n_w):
            for t in range(lo, hi):
                for k in range(3):
                    hand(i, t, k, 1 - c).wait_recv()
        for i in range(n_w):
            for t in range(lo, hi):
                direct(i, t, 0).wait_send()
                direct(i, t, 1).wait_send()
                relay(i, t).wait_send()
                for k in range(3):
                    hand(i, t, k, c).wait_send()

    n_sem = per * (hi - lo) * n_w
    return _Rider(fulls, [jax.ShapeDtypeStruct((w.R, w.C), BF16) for w in ws],
                  [pltpu.SemaphoreType.DMA((n_sem,)), pltpu.SemaphoreType.DMA((n_sem,))], start, finish,
                  steps=[arrived(t) for t in range(lo, hi)], aliases={i: i for i in range(n_w)})


def _cast_into_full(ws, shards, chip_arr, riders=()):
    sr, sc = ws[0].shard_shape
    assert all(w.shard_shape == (sr, sc) for w in ws)
    tr, tc = _tile(sr, 512), _tile(sc, 2048)
    n_r, n_c = sr // tr, sc // tc

    def place(w):
        if w.kind == "col":
            return pl.BlockSpec((tr, tc), lambda i, j, chip: (i, chip[0] * n_c + j))
        return pl.BlockSpec((tr, tc), lambda i, j, chip: (chip[0] * n_r + i, j))

    def body(*refs):
        for a_ref, o_ref in zip(refs[:len(ws)], refs[len(ws):]):
            o_ref[...] = a_ref[...].astype(BF16)

    return _ride("cast_" + "_".join(w.name for w in ws), body, riders, list(shards), grid=(n_r, n_c),
                 in_specs=[pl.BlockSpec((tr, tc), lambda i, j, chip: (i, j))] * len(ws),
                 out_specs=[place(w) for w in ws], out_shape=[jax.ShapeDtypeStruct((w.R, w.C), BF16) for w in ws],
                 scratch_shapes=[], sem=("parallel", "parallel"), scalars=chip_arr)


def _half_view(w, g):
    return g if w.kind == "col" else g.reshape(N_CHIPS, w.R // N_CHIPS, w.C)


def _px_rider(ws, grads):
    n_w = len(ws)

    def copies(g, got, sems):
        send_sems, recv_sems = sems
        x, y, c, _ = _place()

        def half_all(w, ref, half):
            hr = w.half_rows
            if w.kind == "col":
                return ref.at[pl.ds(half * hr, hr), :]
            return ref.at[:, pl.ds(half * hr, hr), :]

        return [pltpu.make_async_remote_copy(
            src_ref=half_all(w, g[i], 1 - c), dst_ref=got[i], send_sem=send_sems.at[i], recv_sem=recv_sems.at[i],
            device_id=(x, y, 1 - c), device_id_type=MESH) for i, w in enumerate(ws)]

    def start(g, got, sems):
        for cp in copies(g, got, sems):
            cp.start()

    def finish(g, got, sems):
        for cp in copies(g, got, sems):
            cp.wait_recv()
            cp.wait_send()

    def got_shape(w):
        hr = w.half_rows
        return (hr, w.C) if w.kind == "col" else (N_CHIPS, hr, w.C)

    return _Rider([_half_view(w, g) for w, g in zip(ws, grads)],
                  [jax.ShapeDtypeStruct(got_shape(w), BF16) for w in ws],
                  [pltpu.SemaphoreType.DMA((n_w,)), pltpu.SemaphoreType.DMA((n_w,))], start, finish)


def _pair_sum(w, g, got, c_arr):
    hr = w.half_rows
    if w.kind == "col":
        tr, tc = _tile(hr, 512), _tile(w.C, 2048)
        n_r = hr // tr
        grid = (n_r, w.C // tc)
        g_spec = pl.BlockSpec((tr, tc), lambda i, j, c: (c[0] * n_r + i, j))
        o_spec = pl.BlockSpec((tr, tc), lambda i, j, c: (i, j))
    else:
        tr = _tile(hr, 512)
        n_r = hr // tr
        grid = (N_CHIPS, n_r)
        g_spec = pl.BlockSpec((1, tr, w.C), lambda s, i, c: (s, c[0] * n_r + i, 0))
        o_spec = pl.BlockSpec((1, tr, w.C), lambda s, i, c: (s, i, 0))

    def body(c_ref, g_ref, got_ref, out_ref):
        out_ref[...] = (g_ref[...].astype(F32) + got_ref[...].astype(F32)).astype(BF16)

    return _pcall(
        body, name="grad_pair_sum_" + w.name, out_shape=jax.ShapeDtypeStruct(got.shape, BF16),
        grid_spec=pltpu.PrefetchScalarGridSpec(num_scalar_prefetch=1, grid=grid, in_specs=[g_spec, o_spec],
                                               out_specs=o_spec),
        compiler_params=_params(("parallel", "parallel")),
    )(c_arr, _half_view(w, g), got)


def _chip_sum(w, p, q, cc_arr):
    hr, cols = w.half_rows, w.shard_shape[1]
    tr, tc = _tile(hr, 512), _tile(cols, 2048)
    n_r, n_c = hr // tr, cols // tc

    def body(cc_ref, own, q1, q2, q3, out_ref):
        own_v = own[...] if w.kind == "col" else own[0]
        out_ref[...] = ((own_v.astype(F32) + q1[0].astype(F32)) + q2[0].astype(F32)) + q3[0].astype(F32)

    if w.kind == "col":
        own_spec = pl.BlockSpec((tr, tc), lambda i, j, cc: (i, cc[1] * n_c + j))
    else:
        own_spec = pl.BlockSpec((1, tr, tc), lambda i, j, cc: (cc[1], i, j))
    q_specs = [pl.BlockSpec((1, tr, tc), lambda i, j, cc, s=s: ((cc[1] + s) % N_CHIPS, i, j)) for s in (1, 2, 3)]
    return _pcall(
        body, name="grad_chip_sum_" + w.name, out_shape=jax.ShapeDtypeStruct(w.shard_shape, F32),
        grid_spec=pltpu.PrefetchScalarGridSpec(
            num_scalar_prefetch=1, grid=(n_r, n_c), in_specs=[own_spec] + q_specs,
            out_specs=pl.BlockSpec((tr, tc), lambda i, j, cc: (cc[0] * n_r + i, j))),
        compiler_params=_params(("parallel", "parallel")),
    )(cc_arr, p, q, q, q)


_SEM = pl.BlockSpec(memory_space=pltpu.SEMAPHORE)
_HBM = pl.BlockSpec(memory_space=pltpu.HBM)


def _split_copies(kind, ws, p, land, send_sems, recv_sems):
    x, y, c, chips = _place()
    my_chip = 2 * x + y
    pairs = []
    for i, w in enumerate(ws):
        if kind == "pair":
            hr = w.half_rows
            src = p[i].at[pl.ds((1 - c) * hr, hr), :] if w.kind == "col" else p[i].at[:, pl.ds((1 - c) * hr, hr), :]
            cp = pltpu.make_async_remote_copy(src_ref=src, dst_ref=land[i], send_sem=send_sems.at[i],
                                              recv_sem=recv_sems.at[i], device_id=(x, y, 1 - c), device_id_type=MESH)
            pairs.append((cp, cp))
            continue
        for k, chip in enumerate(chips):
            to_chip = 2 * chip[0] + chip[1]
            src = p[i].at[:, pl.ds(to_chip * (w.C // N_CHIPS), w.C // N_CHIPS)] if w.kind == "col" else p[i].at[to_chip]
            kw = dict(send_sem=send_sems.at[3 * i + k], recv_sem=recv_sems.at[3 * i + k], device_id=(*chip, c),
                      device_id_type=MESH)
            pairs.append((pltpu.make_async_remote_copy(src_ref=src, dst_ref=land[i].at[my_chip], **kw),
                          pltpu.make_async_remote_copy(src_ref=src, dst_ref=land[i].at[to_chip], **kw)))
    return pairs


def _split_start(name, kind, ws, arrays):
    n_w = len(ws)
    if kind == "pair":
        arrays = [_half_view(w, g) for w, g in zip(ws, arrays)]
        lands = [lax.empty((w.half_rows, w.C) if w.kind == "col" else (N_CHIPS, w.half_rows, w.C), BF16) for w in ws]
    else:
        lands = [lax.empty((N_CHIPS, w.half_rows, w.shard_shape[1]), BF16) for w in ws]
    n_sem = n_w if kind == "pair" else 3 * n_w

    def body(*refs):
        p, land = refs[:n_w], refs[n_w:2 * n_w]
        for out, _ in _split_copies(kind, ws, p, land, refs[2 * n_w], refs[2 * n_w + 1]):
            out.start()
        refs[-1][...] = jnp.zeros_like(refs[-1])

    arrays = [pltpu.with_memory_space_constraint(a, pltpu.HBM) for a in list(arrays) + lands]
    res = _pcall(
        body, name=name,
        out_shape=(pltpu.SemaphoreType.DMA((n_sem,)), pltpu.SemaphoreType.DMA((n_sem,)),
                   *[pltpu.HBM(a.shape, a.dtype) for a in arrays], jax.ShapeDtypeStruct((SUBLANES, LANES), F32)),
        in_specs=[_HBM] * (2 * n_w),
        out_specs=(_SEM, _SEM, *[_HBM] * (2 * n_w), pl.BlockSpec(memory_space=pltpu.VMEM)),
        input_output_aliases={i: 2 + i for i in range(2 * n_w)},
        compiler_params=pltpu.CompilerParams(has_side_effects=pltpu.SideEffectType.DATAFLOW_SIDE_EFFECTING),
    )(*arrays)
    return (kind, ws, res[0], res[1], list(res[2:2 + n_w]), list(res[2 + n_w:2 + 2 * n_w])), res[-1]


def _split_wait(name, flight, after):
    kind, ws, send_sems, recv_sems, arrays, lands = flight
    n_w = len(ws)

    def body(*refs):
        p, land = refs[:n_w], refs[n_w:2 * n_w]
        for _, cp in _split_copies(kind, ws, p, land, refs[2 * n_w], refs[2 * n_w + 1]):
            cp.wait_send()
            cp.wait_recv()

    res = _pcall(
        body, name=name,
        out_shape=[pltpu.HBM(a.shape, a.dtype) for a in list(arrays) + list(lands)],
        in_specs=[_HBM] * (2 * n_w) + [_SEM, _SEM] + [ANY] * len(after), out_specs=[_HBM] * (2 * n_w),
        input_output_aliases={i: i for i in range(2 * n_w)},
        compiler_params=pltpu.CompilerParams(has_side_effects=pltpu.SideEffectType.DATAFLOW_SIDE_EFFECTING),
    )(*arrays, *lands, send_sems, recv_sems, *after)
    return list(res[:n_w]), list(res[n_w:])


def _sf_rider(ws, grads):
    n_w = len(ws)

    def copy(g, sems, i, half):
        send_sems, recv_sems = sems
        x, y, c, _ = _place()
        h = c if half == "mine" else 1 - c
        reg = ws[i].shard_half(g[i], h)
        return pltpu.make_async_remote_copy(src_ref=reg, dst_ref=reg, send_sem=send_sems.at[i], recv_sem=recv_sems.at[i],
                                            device_id=(x, y, 1 - c), device_id_type=MESH)

    def start(_, g, sems):
        for i in range(n_w):
            copy(g, sems, i, "mine").start()

    def finish(_, g, sems):
        for i in range(n_w):
            copy(g, sems, i, "other").wait_recv()
            copy(g, sems, i, "mine").wait_send()

    return _Rider(grads, [jax.ShapeDtypeStruct(w.shard_shape, F32) for w in ws],
                  [pltpu.SemaphoreType.DMA((n_w,)), pltpu.SemaphoreType.DMA((n_w,))], start, finish,
                  aliases={i: i for i in range(n_w)})


def _adamw_math(w, g, m, v):
    m = ADAM_B1 * m + (1.0 - ADAM_B1) * g
    v = ADAM_B2 * v + (1.0 - ADAM_B2) * (g * g)
    m_hat = m / (1.0 - ADAM_B1 ** ADAM_STEP)
    v_hat = v / (1.0 - ADAM_B2 ** ADAM_STEP)
    delta = -ADAM_LR * (m_hat / (jnp.sqrt(v_hat) + ADAM_EPS) + ADAM_WD * w)
    return delta, m, v


def _adamw(name, w, g, m, v, after=None):
    R, C = w.shape
    tr, tc = _tile(R, 256), _tile(C, 2048)
    behind = [] if after is None else [after]

    def body(w_ref, g_ref, m_ref, v_ref, *rest):
        g_out, d_out, m_out, v_out = rest[len(behind):]
        g = g_ref[...]
        g_out[...] = g
        d_out[...], m_out[...], v_out[...] = _adamw_math(w_ref[...], g, m_ref[...], v_ref[...])

    spec = pl.BlockSpec((tr, tc), lambda i, j: (i, j))
    sh = jax.ShapeDtypeStruct((R, C), F32)
    return _pcall(body, name=name, grid=(R // tr, C // tc), in_specs=[spec] * 4 + [ANY] * len(behind),
                  out_specs=[spec] * 4, out_shape=[sh] * 4, compiler_params=_params(("parallel", "parallel")))(
                      w, g, m, v, *behind)


def _ada_update(sct, dmod_sh, w, m, v, riders=()):
    R, C = w.shape
    tr, tc = _tile(R, 256), _tile(C, 1024)

    def body(s_ref, d_ref, w_ref, m_ref, v_ref, g_out, d_out, m_out, v_out):
        s, d = s_ref[...], d_ref[...]
        g = s[:, 0:1] * d[0:1, :]
        for b in range(1, N_DEV):
            g += s[:, b:b + 1] * d[b:b + 1, :]
        g_out[...] = g
        d_out[...], m_out[...], v_out[...] = _adamw_math(w_ref[...], g, m_ref[...], v_ref[...])

    spec = pl.BlockSpec((tr, tc), lambda i, j: (i, j))
    sh = jax.ShapeDtypeStruct((R, C), F32)
    return _ride(
        "ada_update", body, riders, [sct, dmod_sh, w, m, v], grid=(R // tr, C // tc),
        in_specs=[pl.BlockSpec((tr, N_DEV), lambda i, j: (i, 0)), pl.BlockSpec((N_DEV, tc), lambda i, j: (0, j)),
                  spec, spec, spec],
        out_specs=[spec] * 4, out_shape=[sh] * 4, scratch_shapes=[], sem=("parallel", "parallel"))


def _silu_rows(c_row):
    D = c_row.shape[1]

    def body(c_ref, o_ref):
        cv = c_ref[...]
        o_ref[...] = cv * jax.nn.sigmoid(cv)

    return _pcall(body, name="silu_c", out_shape=jax.ShapeDtypeStruct((1, D), F32))(c_row)


def _pack_partials(parts, widths, total):
    n = len(widths)

    def body(*refs):
        loss_p, out_ref = refs[n], refs[n + 1]
        off = 0
        for ref, wd in zip(refs[:n], widths):
            out_ref[:, off:off + wd] = jnp.sum(ref[...], axis=0)
            off += wd
        loss = jnp.sum(jnp.sum(loss_p[...], axis=0), axis=1, keepdims=True)
        out_ref[:, off:off + LANES] = jnp.broadcast_to(loss, (1, LANES))
        if off + LANES < total:
            out_ref[:, off + LANES:total] = jnp.zeros((1, total - off - LANES), F32)

    return _pcall(body, name="pack_partials", out_shape=jax.ShapeDtypeStruct((1, total), F32))(*parts)


def _small_update(gathered, offsets, params, loss_off):
    n_p = len(params)

    def over_devices(g_ref, off, wd):
        blk = g_ref[:, off:off + wd]
        g = blk[0:1, :]
        for b in range(1, N_DEV):
            g = g + blk[b:b + 1, :]
        return g

    def body(*refs):
        g_ref = refs[0]
        prm = refs[1:1 + 3 * n_p]
        outs = refs[1 + 3 * n_p:]
        outs[4 * n_p][...] = over_devices(g_ref, loss_off, LANES)
        for i, (off, wd) in enumerate(offsets):
            g = over_devices(g_ref, off, wd)
            w, m, v = prm[3 * i][...], prm[3 * i + 1][...], prm[3 * i + 2][...]
            outs[4 * i][...] = g
            outs[4 * i + 1][...], outs[4 * i + 2][...], outs[4 * i + 3][...] = _adamw_math(w, g, m, v)

    flat = [a for t in params for a in t]
    out_shape = [jax.ShapeDtypeStruct(t[0].shape, F32) for t in params for _ in range(4)]
    out_shape.append(jax.ShapeDtypeStruct((1, LANES), F32))
    return _pcall(body, name="small_update", out_shape=out_shape)(gathered, *flat)


def kernel(x, c, w_ada, b_ada, norm1_w, w_in, q_norm_w, k_norm_w, w_pool, pool_scale, w_a_up, w_b_up, w_o, norm2_w, w_ff1, w_ff2, loss_target, m_w_ada, m_b_ada, m_norm1_w, m_w_in, m_q_norm_w, m_k_norm_w, m_w_pool, m_pool_scale, m_w_a_up, m_w_b_up, m_w_o, m_norm2_w, m_w_ff1, m_w_ff2, v_w_ada, v_b_ada, v_norm1_w, v_w_in, v_q_norm_w, v_k_norm_w, v_w_pool, v_pool_scale, v_w_a_up, v_w_b_up, v_w_o, v_norm2_w, v_w_ff1, v_w_ff2):
    _, S, D = x.shape
    PW = D // 2
    H = PW // HEAD_DIM
    cg = PW // N_GROUPS
    IN = w_in.shape[2] * N_CHIPS
    FF = w_ff1.shape[2] * N_CHIPS
    A_COLS = w_ada.shape[2]
    xi, yi, ci = lax.axis_index("x"), lax.axis_index("y"), lax.axis_index("c")
    chip = 2 * xi + yi
    dev = 2 * chip + ci
    c_arr = jnp.reshape(ci, (1,)).astype(jnp.int32)
    x2, tgt = x[0], loss_target[0]

    ws = [_W("w_in", "col", D, IN), _W("w_pool", "row", PW, cg), _W("w_a_up", "col", PW, D),
          _W("w_b_up", "col", PW, D), _W("w_o", "row", D, D), _W("w_ff1", "col", D, FF), _W("w_ff2", "row", FF, D)]
    w32 = [w_in[0], w_pool[0].reshape(cg, cg), w_a_up[0], w_b_up[0], w_o[0], w_ff1[0], w_ff2[0]]
    m32 = [m_w_in[0], m_w_pool[0].reshape(cg, cg), m_w_a_up[0], m_w_b_up[0], m_w_o[0], m_w_ff1[0], m_w_ff2[0]]
    v32 = [v_w_in[0], v_w_pool[0].reshape(cg, cg), v_w_a_up[0], v_w_b_up[0], v_w_o[0], v_w_ff1[0], v_w_ff2[0]]

    W_IN, W_POOL, W_A, W_B, W_O, W_FF1, W_FF2 = ws
    chip_arr = jnp.reshape(chip, (1,)).astype(jnp.int32)
    cc_arr = jnp.stack([ci, chip]).astype(jnp.int32)
    s_in, s_pool, s_a, s_b, s_o = [_cast_into_full([w], [a], chip_arr)[0] for w, a in zip(ws[:5], w32[:5])]
    (s_ff1, s_ff2), ((win_f,),) = _cast_into_full([W_FF1, W_FF2], w32[5:], chip_arr, riders=[_ag_rider([W_IN], [s_in])])

    sc_row = _silu_rows(c)
    sc_all = _dev_allgather("gather_silu_c", sc_row.reshape(SUBLANES, D // SUBLANES)).reshape(N_DEV, D)
    sc16 = jnp.concatenate([sc_all, jnp.zeros_like(sc_all)], axis=0)
    b_cols = lax.dynamic_slice(b_ada, (0, chip * A_COLS), (1, A_COLS))
    (mod_cols,) = _mm("mod_cols", [(sc16, w_ada[0])], M=2 * N_DEV, N=A_COLS, K=D, tm=16, tn=1024, tk=1024,
                      a_pro=lambda a: a.astype(BF16), b_pro=lambda b: b.astype(BF16),
                      extras=[(b_cols, "row", 0)], outs=[_tile_out(F32)], epi=lambda accs, ex: [accs[0] + ex[0]])
    mod_all = _dev_allgather("gather_mod", mod_cols[:N_DEV]).reshape(N_CHIPS, 2, N_DEV, A_COLS)
    mod_row = lax.dynamic_index_in_dim(mod_all[:, 0], dev, axis=1, keepdims=False).reshape(1, N_CHIPS * A_COLS)
    shift1, scale1, gate1, shift2, scale2, gate2 = [mod_row[:, i * D:(i + 1) * D] for i in range(6)]

    WIDE = dict(tm=2048, tn=512, tk=2048)
    DEEP = dict(tm=1024, tn=1024, tk=2048)
    h = _norm_mod("norm1_mod", x2, norm1_w, scale1, shift1)
    (proj,), ((wpool_f, wa_f, wb_f, wo_f),) = _mm(
        "in_proj", [(h, win_f)], M=S, N=IN, K=D, outs=[_tile_out(BF16)], epi=lambda accs, ex: [accs[0]], **WIDE,
        riders=[_ag_rider([W_POOL, W_A, W_B, W_O], [s_pool, s_a, s_b, s_o], n_ch=2)])
    pooled, pa = _pool_fwd(proj, wpool_f, pool_scale, S, PW)
    (att, attf), ((wff1_f,),) = _attn_fwd(proj, q_norm_w, k_norm_w, S, H, PW // HEAD_DIM,
                                          riders=[_ag_rider([W_FF1], [s_ff1])])

    def merge_epi(accs, ex):
        sa, sb = jax.nn.sigmoid(ex[0].astype(F32)), jax.nn.sigmoid(ex[1].astype(F32))
        return [sa * accs[0] + sb * accs[1], accs[0], accs[1]]

    (merged, ya, yb), (ff2_a,) = _mm("branch_up_merge", [(pa, wa_f), (att, wb_f)], M=S, N=D, K=PW,
                                     extras=[(proj, "tile", 4 * PW), (proj, "tile", 4 * PW + D)],
                                     outs=[_tile_out(BF16)] * 3, epi=merge_epi,
                                     riders=[_ag_rider([W_FF2], [s_ff2], chunks=(0, 1))])
    (x1, o), (ff2_b,) = _mm("out_proj", [(merged, wo_f)], M=S, N=D, K=D, extras=[(x2, "tile", 0), (gate1, "row", 0)],
                            outs=[_tile_out(F32), _tile_out(BF16)], epi=lambda accs, ex: [ex[0] + ex[1] * accs[0], accs[0]],
                            riders=[_ag_rider([W_FF2], ff2_a, chunks=(1, 2))], **WIDE)
    h2 = _norm_mod("norm2_mod", x1, norm2_w, scale2, shift2)
    (rl,), ((wff2_f,),) = _mm("ff1", [(h2, wff1_f)], M=S, N=FF, K=D, outs=[_tile_out(BF16)], **WIDE,
                              epi=lambda accs, ex: [jnp.maximum(accs[0], 0.0)],
                              riders=[_ag_rider([W_FF2], ff2_b, chunks=(2, 4))])

    def square(a):
        af = a.astype(F32)
        return (af * af).astype(BF16)

    def loss_epi(accs, ex):
        x1_t, tgt_t, g2 = ex
        f = accs[0]
        diff = (x1_t + g2 * f) - tgt_t
        dy = diff * (1.0 / D)
        return [dy, dy * g2, _colsum(dy * f), _colsum(diff * diff)]

    dy, df, dgate2_p, loss_p = _mm("ff2_loss", [(rl, wff2_f)], M=S, N=D, K=FF, a_pro=square, **DEEP,
                                   extras=[(x1, "tile", 0), (tgt, "tile", 0), (gate2, "row", 0)],
                                   outs=[_tile_out(F32), _tile_out(BF16), _COLSUM, _COLSUM], epi=loss_epi)

    tied = []

    def behind(token, a):
        a, token = lax.optimization_barrier((a, token))
        tied.append(token)
        return a

    def pair_sums(group, partials, got):
        return [_pair_sum(w, g, r, c_arr) for w, g, r in zip(group, partials, got)]

    def chip_sums(group, sums, from_chips):
        return [_chip_sum(w, p, q, cc_arr) for w, p, q in zip(group, sums, from_chips)]

    first = lambda accs, ex: [accs[0]]
    gmm = dict(ta=True, outs=[_tile_out(BF16)], epi=first, **WIDE)
    (g_ff2,) = _mm("grad_w_ff2", [(rl, df)], M=FF, N=D, K=S, a_pro=square, ta=True, tm=512, tn=2048, tk=2048,
                   outs=[_tile_out(BF16)], epi=first)
    flight, token = _split_start("pair_w_ff2_start", "pair", [W_FF2], [g_ff2])
    (dz1,) = _mm("d_ff_hidden", [(behind(token, df), wff2_f)], M=S, N=FF, K=D, tb=True, extras=[(rl, "tile", 0)],
                 outs=[_tile_out(BF16)], epi=lambda accs, ex: [accs[0] * (2.0 * ex[0].astype(F32))], **WIDE)
    sum_ff2 = pair_sums([W_FF2], *_split_wait("pair_w_ff2_wait", flight, after=[dz1] + tied))
    chip_ff2, token = _split_start("chip_w_ff2_start", "chip", [W_FF2], sum_ff2)
    (g_ff1,) = _mm("grad_w_ff1", [(behind(token, h2), dz1)], M=D, N=FF, K=S, **gmm)
    flight, token = _split_start("pair_w_ff1_start", "pair", [W_FF1], [g_ff1])
    (dh2,) = _mm("d_h2", [(behind(token, dz1), wff1_f)], M=S, N=D, K=FF, tb=True, outs=[_tile_out(F32)], epi=first, **DEEP)
    sum_ff1 = pair_sums([W_FF1], *_split_wait("pair_w_ff1_wait", flight, after=[dh2] + tied))
    chip_ff1, token = _split_start("chip_w_ff1_start", "chip", [W_FF1], sum_ff1)
    dx1, dshift2_p, dscale2_p, gn2_p, do, dgate1_p = _norm_mod_bwd("norm2_bwd", behind(token, dh2), x1, dy, norm2_w, scale2,
                                                                   gate_o=(o, gate1))
    (g_wo,) = _mm("grad_w_o", [(merged, do)], M=D, N=D, K=S, **gmm)

    def gate_epi(accs, ex):
        dm = accs[0]
        sa, sb = jax.nn.sigmoid(ex[0].astype(F32)), jax.nn.sigmoid(ex[1].astype(F32))
        ya_t, yb_t = ex[2].astype(F32), ex[3].astype(F32)
        return [dm * sa, dm * sb, dm * ya_t * (sa * (1.0 - sa)), dm * yb_t * (sb * (1.0 - sb))]

    dya, dyb, dga, dgb = _mm("d_merged", [(do, wo_f)], M=S, N=D, K=D, tb=True, tm=1024, tn=512, tk=2048,
                             extras=[(proj, "tile", 4 * PW), (proj, "tile", 4 * PW + D), (ya, "tile", 0), (yb, "tile", 0)],
                             outs=[_tile_out(BF16)] * 4, epi=gate_epi)
    both = lambda accs, ex: [accs[0], accs[1]]
    g_wa, g_wb = _mm("grad_w_up", [(pa, dya), (att, dyb)], M=PW, N=D, K=S, ta=True, outs=[_tile_out(BF16)] * 2, epi=both,
                     **WIDE)
    mid = [W_A, W_B, W_O]
    flight, token = _split_start("pair_mid_start", "pair", mid, [g_wa, g_wb, g_wo])
    dpa, datt = _mm("d_branches", [(dya, wa_f), (behind(token, dyb), wb_f)], M=S, N=PW, K=D, tb=True,
                    outs=[_tile_out(F32), _tile_out(BF16)], epi=both, tm=1024, tn=512, tk=2048)
    sum_mid = pair_sums(mid, *_split_wait("pair_mid_wait", flight, after=[datt] + tied))
    chip_mid, token = _split_start("chip_mid_start", "chip", mid, sum_mid)
    du, g_wpool4, gscale_p = _pool_bwd(dpa, pooled, wpool_f, pool_scale, S, PW)
    dq, dk, dv, gq_p, gk_p = _attn_bwd(proj, behind(token, datt), attf, q_norm_w, k_norm_w, S, H, PW // HEAD_DIM)
    dproj = jnp.concatenate([du, dq, dk, dv, dga, dgb], axis=1)
    early = [W_FF1, W_FF2]
    sum_ff1, q_ff1 = _split_wait("chip_w_ff1_wait", chip_ff1, after=[dq] + tied)
    sum_ff2, q_ff2 = _split_wait("chip_w_ff2_wait", chip_ff2, after=[dq] + tied)
    halves_early = chip_sums(early, sum_ff1 + sum_ff2, q_ff1 + q_ff2)
    (g_win,), (grads_early,) = _mm("grad_w_in", [(h, dproj)], M=D, N=IN, K=S, riders=[_sf_rider(early, halves_early)],
                                   **gmm)
    last = [W_IN, W_POOL]
    g_last = [g_win, g_wpool4.reshape(PW, cg)]
    sum_mid, q_mid = _split_wait("chip_mid_wait", chip_mid, after=[g_win] + tied)
    halves_mid = chip_sums(mid, sum_mid, q_mid)
    (dh,), (got_last, grads_mid) = _mm("d_h", [(dproj, win_f)], M=S, N=D, K=IN, tb=True, outs=[_tile_out(F32)], epi=first,
                                       riders=[_px_rider(last, g_last), _sf_rider(mid, halves_mid)], **DEEP)
    sum_last = pair_sums(last, g_last, got_last)
    grad_x, dshift1_p, dscale1_p, gn1_p = _norm_mod_bwd("norm1_bwd", dh, x2, dx1, norm1_w, scale1)

    parts = [dshift1_p, dscale1_p, dgate1_p, dshift2_p, dscale2_p, dgate2_p, gn1_p, gn2_p,
             gscale_p.reshape(1, 1, PW), gq_p, gk_p]
    widths = [D] * 8 + [PW, HEAD_DIM, HEAD_DIM]
    used = sum(widths)
    P = -(-(used + LANES) // (SUBLANES * LANES)) * (SUBLANES * LANES)
    packed = _pack_partials(parts + [loss_p], widths, P)
    gathered = _dev_allgather("gather_vector_grads", packed.reshape(SUBLANES, P // SUBLANES)).reshape(N_DEV, P)
    sum_last, gathered = lax.optimization_barrier((sum_last, gathered))
    chip_last, token = _split_start("chip_last_start", "chip", last, sum_last)
    small = [(b_ada, m_b_ada, v_b_ada), (norm1_w, m_norm1_w, v_norm1_w), (norm2_w, m_norm2_w, v_norm2_w),
             (pool_scale, m_pool_scale, v_pool_scale), (q_norm_w, m_q_norm_w, v_q_norm_w),
             (k_norm_w, m_k_norm_w, v_k_norm_w)]
    offsets = [(0, 6 * D), (6 * D, D), (7 * D, D), (8 * D, PW), (8 * D + PW, HEAD_DIM), (8 * D + PW + HEAD_DIM, HEAD_DIM)]
    su = _small_update(gathered, offsets, small, used)
    (g_b, d_b, nm_b, nv_b, g_n1, d_n1, nm_n1, nv_n1, g_n2, d_n2, nm_n2, nv_n2, g_ps, d_ps, nm_ps, nv_ps,
     g_qn, d_qn, nm_qn, nv_qn, g_kn, d_kn, nm_kn, nv_kn, loss_sum) = su
    dmod_sh = lax.dynamic_slice(gathered, (0, chip * A_COLS), (N_DEV, A_COLS))
    dmod_sh, token = lax.optimization_barrier((dmod_sh, token))
    g_ada, d_ada, nm_ada, nv_ada = _ada_update(sc_all.T, dmod_sh, w_ada[0], m_w_ada[0], v_w_ada[0])

    upd_done = [_adamw("adamw_" + w.name, a, g, m, v, after=token)
                for w, a, g, m, v in zip(ws[2:], w32[2:], list(grads_mid) + list(grads_early), m32[2:], v32[2:])]

    sum_last, q_last = _split_wait("chip_last_wait", chip_last, after=[nv_ada] + [u[3] for u in upd_done])
    halves_last = chip_sums(last, sum_last, q_last)
    filled = _run_rider("grad_sibling_fill", _sf_rider(last, halves_last))
    upd = [_adamw("adamw_" + w.name, a, g, m, v) for w, a, g, m, v in zip(ws[:2], w32[:2], filled, m32[:2], v32[:2])]
    upd += upd_done

    loss = (0.5 / D) * loss_sum[0, 0]

    def up(a):
        return a[None]

    def pool4(a):
        return a.reshape(1, N_GROUPS, cg // N_CHIPS, cg)

    (gr_win, d_win, nm_win, nv_win), (gr_wp, d_wp, nm_wp, nv_wp), (gr_wa, d_wa, nm_wa, nv_wa), \
        (gr_wb, d_wb, nm_wb, nv_wb), (gr_wo, d_wo, nm_wo, nv_wo), (gr_f1, d_f1, nm_f1, nv_f1), \
        (gr_f2, d_f2, nm_f2, nv_f2) = upd
    return (
        loss, grad_x[None],
        up(g_ada), g_b, g_n1, up(gr_win), g_qn, g_kn, pool4(gr_wp), g_ps, up(gr_wa), up(gr_wb), up(gr_wo), g_n2,
        up(gr_f1), up(gr_f2),
        up(d_ada), d_b, d_n1, up(d_win), d_qn, d_kn, pool4(d_wp), d_ps, up(d_wa), up(d_wb), up(d_wo), d_n2,
        up(d_f1), up(d_f2),
        up(nm_ada), nm_b, nm_n1, up(nm_win), nm_qn, nm_kn, pool4(nm_wp), nm_ps, up(nm_wa), up(nm_wb), up(nm_wo), nm_n2,
        up(nm_f1), up(nm_f2),
        up(nv_ada), nv_b, nv_n1, up(nv_win), nv_qn, nv_kn, pool4(nv_wp), nv_ps, up(nv_wa), up(nv_wb), up(nv_wo), nv_n2,
        up(nv_f1), up(nv_f2),
    )
```

```python
import functools
import math

import jax
import jax.numpy as jnp
from jax import lax
from jax.experimental import pallas as pl
from jax.experimental.pallas import tpu as pltpu

F32 = jnp.float32
BF16 = jnp.bfloat16
MESH = pl.DeviceIdType.MESH
ANY = pl.BlockSpec(memory_space=pl.ANY)

EPS = 1e-6
HEAD_DIM = 128
LANES, SUBLANES = 128, 8
POOL_WINDOWS = (2, 4, 8, 16)
N_GROUPS = len(POOL_WINDOWS)
assert POOL_WINDOWS == tuple(2 << g for g in range(N_GROUPS))
N_CHIPS = 4
N_DEV = 8
ADAM_LR, ADAM_B1, ADAM_B2, ADAM_EPS, ADAM_WD, ADAM_STEP = 0.001, 0.9, 0.999, 1e-08, 0.01, 10
VMEM_LIMIT_V7X = 56 * 1024 * 1024
ATT_T = 256
ATT_GROUP = 8
POOL_T = 256


def _pcall(body, **kw):
    return pl.pallas_call(body, **kw)


def _params(sem=None):
    return pltpu.CompilerParams(dimension_semantics=sem, vmem_limit_bytes=VMEM_LIMIT_V7X)


def _tile(n, pref):
    if n <= pref:
        return n
    t = pref
    while n % t:
        t //= 2
    return t


class _Rider:
    def __init__(self, arrays, out_shape, sems, start, finish, aliases=None, steps=()):
        self.arrays, self.out_shape, self.sems = list(arrays), list(out_shape), list(sems)
        self.start, self.finish, self.aliases, self.steps = start, finish, aliases or {}, list(steps)


def _ride(name, body, riders, arrays, *, grid, in_specs, out_specs, out_shape, scratch_shapes, sem, scalars=None):
    n_in, n_out, n_scr = len(arrays), len(out_shape), len(scratch_shapes)
    r_arrays = [a for r in riders for a in r.arrays]
    r_outs = [o for r in riders for o in r.out_shape]
    r_sems = [s for r in riders for s in r.sems]
    n_hooks = max([len(r.steps) for r in riders], default=0)
    total = math.prod(grid)
    aliases, off_i, off_o = {}, n_in + (scalars is not None), n_out
    for r in riders:
        for a, o in r.aliases.items():
            aliases[off_i + a] = off_o + o
        off_i += len(r.arrays)
        off_o += len(r.out_shape)

    def full(*refs):
        p = 0
        groups = []
        for n in (n_in, len(r_arrays), n_out, len(r_outs), n_scr, len(r_sems)):
            groups.append(refs[p:p + n])
            p += n
        ins, rin, outs, rout, scr, rsem = groups

        def each(what):
            a = o = s = 0
            for r in riders:
                fn = what(r)
                if fn is not None:
                    fn(rin[a:a + len(r.arrays)], rout[o:o + len(r.out_shape)], rsem[s:s + len(r.sems)])
                a, o, s = a + len(r.arrays), o + len(r.out_shape), s + len(r.sems)

        if riders:
            lin = 0
            for d, g in enumerate(grid):
                lin = lin * g + pl.program_id(d)
            pl.when(lin == 0)(lambda: each(lambda r: r.start))
            for t in range(n_hooks):
                pl.when(lin == min(total - 1, ((t + 1) * total) // n_hooks))(
                    lambda t=t: each(lambda r: r.steps[t] if t < len(r.steps) else None))
        body(*ins, *outs, *scr)
        if riders:
            pl.when(lin == total - 1)(lambda: each(lambda r: r.finish))

    specs = dict(grid=grid, in_specs=list(in_specs) + [ANY] * len(r_arrays),
                 out_specs=list(out_specs) + [ANY] * len(r_outs), scratch_shapes=list(scratch_shapes) + r_sems)
    common = dict(name=name, out_shape=list(out_shape) + r_outs, input_output_aliases=aliases,
                  compiler_params=_params(("arbitrary",) * len(grid) if riders else sem))
    if scalars is None:
        res = _pcall(full, **specs, **common)(*arrays, *r_arrays)
    else:
        res = _pcall(lambda _, *refs: full(*refs), **common,
                     grid_spec=pltpu.PrefetchScalarGridSpec(num_scalar_prefetch=1, **specs))(scalars, *arrays, *r_arrays)
    if not riders:
        return res
    main, rest, per = res[:n_out], res[n_out:], []
    for r in riders:
        per.append(rest[:len(r.out_shape)])
        rest = rest[len(r.out_shape):]
    return main, per


def _run_rider(name, rider):
    def body(*refs):
        n_a, n_o = len(rider.arrays), len(rider.out_shape)
        ins, outs, sems = refs[:n_a], refs[n_a:n_a + n_o], refs[n_a + n_o:]
        for fn in [rider.start] + rider.steps + [rider.finish]:
            fn(ins, outs, sems)

    return _pcall(body, name=name, out_shape=rider.out_shape, in_specs=[ANY] * len(rider.arrays),
                  out_specs=[ANY] * len(rider.out_shape), scratch_shapes=rider.sems,
                  input_output_aliases=rider.aliases)(*rider.arrays)


def _mm(name, pairs, *, M, N, K, ta=False, tb=False, tm=512, tn=1024, tk=1024,
        a_pro=None, b_pro=None, extras=(), outs, epi, riders=()):
    tm, tn, tk = _tile(M, tm), _tile(N, tn), _tile(K, tk)
    n_i, n_j, n_k = M // tm, N // tn, K // tk
    n_p, n_e = len(pairs), len(extras)
    arrays, in_specs = [], []
    for a, _ in pairs:
        arrays.append(a)
        in_specs.append(pl.BlockSpec((tk, tm), lambda i, j, k: (k, i)) if ta
                        else pl.BlockSpec((tm, tk), lambda i, j, k: (i, k)))
    for _, b in pairs:
        arrays.append(b)
        in_specs.append(pl.BlockSpec((tn, tk), lambda i, j, k: (j, k)) if tb
                        else pl.BlockSpec((tk, tn), lambda i, j, k: (k, j)))
    for arr, kind, off in extras:
        ob = off // tn
        assert off % tn == 0
        arrays.append(arr)
        if kind == "tile":
            in_specs.append(pl.BlockSpec((tm, tn), lambda i, j, k, ob=ob: (i, j + ob)))
        else:
            in_specs.append(pl.BlockSpec((1, tn), lambda i, j, k, ob=ob: (0, j + ob)))
    out_shape, out_specs = [], []
    for o in outs:
        if o["kind"] == "tile":
            out_shape.append(jax.ShapeDtypeStruct((M, N), o["dtype"]))
            out_specs.append(pl.BlockSpec((tm, tn), lambda i, j, k: (i, j)))
        else:
            out_shape.append(jax.ShapeDtypeStruct((n_i, 1, N), F32))
            out_specs.append(pl.BlockSpec((1, 1, tn), lambda i, j, k: (i, 0, j)))
    dims = (((0 if ta else 1,), (1 if tb else 0,)), ((), ()))

    def body(*refs):
        a_refs, b_refs = refs[:n_p], refs[n_p:2 * n_p]
        e_refs = refs[2 * n_p:2 * n_p + n_e]
        o_refs = refs[2 * n_p + n_e:2 * n_p + n_e + len(outs)]
        acc_refs = refs[2 * n_p + n_e + len(outs):]

        def product(p):
            a, b = a_refs[p][...], b_refs[p][...]
            if a_pro is not None:
                a = a_pro(a)
            if b_pro is not None:
                b = b_pro(b)
            return lax.dot_general(a, b, dims, preferred_element_type=F32)

        def write(accs):
            vals = epi(accs, [e[...] for e in e_refs])
            for o, o_ref, val in zip(outs, o_refs, vals):
                if o["kind"] == "tile":
                    o_ref[...] = val.astype(o_ref.dtype)
                else:
                    o_ref[0] = val

        if n_k == 1:
            write([product(p) for p in range(n_p)])
            return
        k = pl.program_id(2)

        @pl.when(k == 0)
        def _():
            for acc in acc_refs:
                acc[...] = jnp.zeros_like(acc)

        for p in range(n_p):
            acc_refs[p][...] += product(p)

        pl.when(k == n_k - 1)(lambda: write([acc[...] for acc in acc_refs]))

    return _ride(name, body, riders, arrays, grid=(n_i, n_j, n_k), in_specs=in_specs, out_specs=out_specs,
                 out_shape=out_shape, scratch_shapes=[pltpu.VMEM((tm, tn), F32) for _ in pairs] if n_k > 1 else [],
                 sem=("parallel", "parallel", "arbitrary"))


def _tile_out(dtype):
    return {"kind": "tile", "dtype": dtype}


_COLSUM = {"kind": "colsum"}


def _colsum(v):
    return jnp.sum(v, axis=0, keepdims=True)


def _norm_mod(name, x, norm_w, scale, shift):
    S, D = x.shape
    tr = _tile(S, 256)

    def body(x_ref, nw_ref, sc_ref, sh_ref, h_ref):
        xv = x_ref[...]
        r = lax.rsqrt(jnp.mean(xv * xv, axis=-1, keepdims=True) + EPS)
        h_ref[...] = ((xv * r * nw_ref[...]) * (1.0 + sc_ref[...]) + sh_ref[...]).astype(BF16)

    row = pl.BlockSpec((1, D), lambda i: (0, 0))
    til = pl.BlockSpec((tr, D), lambda i: (i, 0))
    return _pcall(body, name=name, grid=(S // tr,), in_specs=[til, row, row, row], out_specs=til,
                  out_shape=jax.ShapeDtypeStruct((S, D), BF16), compiler_params=_params(("parallel",)))(
                      x, norm_w, scale, shift)


def _norm_mod_bwd(name, dh, x, dres, norm_w, scale, gate_o=None):
    S, D = x.shape
    tr = _tile(S, 256)
    n_r = S // tr
    with_gate = gate_o is not None

    def body(*refs):
        if with_gate:
            dh_ref, x_ref, dres_ref, nw_ref, sc_ref, o_ref, g_ref, dx_ref, p1, p2, p3, do_ref, p4 = refs
        else:
            dh_ref, x_ref, dres_ref, nw_ref, sc_ref, dx_ref, p1, p2, p3 = refs
        dhv, xv, nw = dh_ref[...], x_ref[...], nw_ref[...]
        r = lax.rsqrt(jnp.mean(xv * xv, axis=-1, keepdims=True) + EPS)
        xh = xv * r
        p1[0] = _colsum(dhv)
        p2[0] = _colsum(dhv * (xh * nw))
        dn = dhv * (1.0 + sc_ref[...])
        p3[0] = _colsum(dn * xh)
        dxh = dn * nw
        dx = dres_ref[...] + r * (dxh - xh * jnp.mean(dxh * xh, axis=-1, keepdims=True))
        dx_ref[...] = dx
        if with_gate:
            do_ref[...] = (dx * g_ref[...]).astype(BF16)
            p4[0] = _colsum(dx * o_ref[...].astype(F32))

    row = pl.BlockSpec((1, D), lambda i: (0, 0))
    til = pl.BlockSpec((tr, D), lambda i: (i, 0))
    part = pl.BlockSpec((1, 1, D), lambda i: (i, 0, 0))
    part_shape = jax.ShapeDtypeStruct((n_r, 1, D), F32)
    in_specs = [til, til, til, row, row]
    arrays = [dh, x, dres, norm_w, scale]
    out_specs = [til, part, part, part]
    out_shape = [jax.ShapeDtypeStruct((S, D), F32), part_shape, part_shape, part_shape]
    if with_gate:
        in_specs += [til, row]
        arrays += list(gate_o)
        out_specs += [til, part]
        out_shape += [jax.ShapeDtypeStruct((S, D), BF16), part_shape]
    return _pcall(body, name=name, grid=(n_r,), in_specs=in_specs, out_specs=out_specs, out_shape=out_shape,
                  compiler_params=_params(("parallel",)))(*arrays)


def _pool_w_specs(rows, cg):
    return [pl.BlockSpec((rows, cg), lambda g, j=j: (N_GROUPS * j + g, 0)) for j in range(N_CHIPS)]


def _pool_fwd(proj, wp_full, pool_scale, S, PW):
    cg = PW // N_GROUPS
    rows = cg // N_CHIPS
    T = _tile(S, POOL_T)
    n_t = S // T

    def body(u_ref, w0, w1, w2, w3, ps_ref, pooled_ref, pa_ref):
        g = pl.program_id(0)
        win = jnp.left_shift(2, g)
        w = jnp.concatenate([w0[...], w1[...], w2[...], w3[...]], axis=0)
        t_i = lax.broadcasted_iota(jnp.int32, (T, T), 0)
        j_i = lax.broadcasted_iota(jnp.int32, (T, T), 1)
        b_cur = ((j_i <= t_i) & (j_i > t_i - win)).astype(BF16)
        b_prev = (j_i - T > t_i - win).astype(BF16)
        row = lax.broadcasted_iota(jnp.int32, (T, 1), 0)
        for r in range(n_t):
            cur = u_ref[r * T:(r + 1) * T, :]
            ws = jnp.dot(b_cur, cur, preferred_element_type=F32)
            if r > 0:
                ws += jnp.dot(b_prev, u_ref[(r - 1) * T:r * T, :], preferred_element_type=F32)
            count = jnp.minimum(row + (r * T + 1), win).astype(F32)
            pooled = (ws / count - cur.astype(F32)).astype(BF16)
            pooled_ref[r * T:(r + 1) * T, :] = pooled
            mixed = jnp.dot(pooled, w, preferred_element_type=F32)
            pa_ref[r * T:(r + 1) * T, :] = (mixed * ps_ref[...]).astype(BF16)

    col = pl.BlockSpec((S, cg), lambda g: (0, g))
    return _pcall(
        body, name="pool_fwd", grid=(N_GROUPS,),
        in_specs=[col] + _pool_w_specs(rows, cg) + [pl.BlockSpec((1, cg), lambda g: (0, g))],
        out_specs=[col, col],
        out_shape=[jax.ShapeDtypeStruct((S, PW), BF16), jax.ShapeDtypeStruct((S, PW), BF16)],
        compiler_params=_params(("parallel",)),
    )(proj, wp_full, wp_full, wp_full, wp_full, pool_scale)


def _pool_bwd(dpa, pooled, wp_full, pool_scale, S, PW):
    cg = PW // N_GROUPS
    rows = cg // N_CHIPS
    T = _tile(S, POOL_T)
    n_t = S // T

    def body(dpa_ref, pooled_ref, w0, w1, w2, w3, ps_ref, du_ref, gw_ref, gs_ref, dp_s, dpc_s, dmx_s):
        g = pl.program_id(0)
        win = jnp.left_shift(2, g)
        w = jnp.concatenate([w0[...], w1[...], w2[...], w3[...]], axis=0)
        row = lax.broadcasted_iota(jnp.int32, (T, 1), 0)
        gs = jnp.zeros((1, cg), F32)
        for r in range(n_t):
            sl = slice(r * T, (r + 1) * T)
            mixed = jnp.dot(pooled_ref[sl, :], w, preferred_element_type=F32)
            dpa_t = dpa_ref[sl, :]
            gs += _colsum(dpa_t * mixed)
            dmx = (dpa_t * ps_ref[...]).astype(BF16)
            dmx_s[sl, :] = dmx
            dpo = lax.dot_general(dmx, w, (((1,), (1,)), ((), ())), preferred_element_type=F32)
            dp_s[sl, :] = dpo
            count = jnp.minimum(row + (r * T + 1), win).astype(F32)
            dpc_s[sl, :] = (dpo / count).astype(BF16)
        gs_ref[...] = gs
        gw = lax.dot_general(pooled_ref[...], dmx_s[...], (((0,), (0,)), ((), ())), preferred_element_type=F32)
        for j in range(N_CHIPS):
            gw_ref[j, 0] = gw[j * rows:(j + 1) * rows, :].astype(BF16)
        j_i = lax.broadcasted_iota(jnp.int32, (T, T), 0)
        t_i = lax.broadcasted_iota(jnp.int32, (T, T), 1)
        b_cur = ((t_i >= j_i) & (t_i < j_i + win)).astype(BF16)
        b_next = (t_i + T < j_i + win).astype(BF16)
        for r in range(n_t):
            sl = slice(r * T, (r + 1) * T)
            acc = jnp.dot(b_cur, dpc_s[sl, :], preferred_element_type=F32)
            if r + 1 < n_t:
                acc += jnp.dot(b_next, dpc_s[(r + 1) * T:(r + 2) * T, :], preferred_element_type=F32)
            du_ref[sl, :] = (acc - dp_s[sl, :]).astype(BF16)

    col = pl.BlockSpec((S, cg), lambda g: (0, g))
    return _pcall(
        body, name="pool_bwd", grid=(N_GROUPS,),
        in_specs=[col, col] + _pool_w_specs(rows, cg) + [pl.BlockSpec((1, cg), lambda g: (0, g))],
        out_specs=[col, pl.BlockSpec((N_CHIPS, 1, rows, cg), lambda g: (0, g, 0, 0)),
                   pl.BlockSpec((1, cg), lambda g: (0, g))],
        out_shape=[jax.ShapeDtypeStruct((S, PW), BF16),
                   jax.ShapeDtypeStruct((N_CHIPS, N_GROUPS, rows, cg), BF16),
                   jax.ShapeDtypeStruct((1, PW), F32)],
        scratch_shapes=[pltpu.VMEM((S, cg), F32), pltpu.VMEM((S, cg), BF16), pltpu.VMEM((S, cg), BF16)],
        compiler_params=_params(("parallel",)),
    )(dpa, pooled, wp_full, wp_full, wp_full, wp_full, pool_scale)


_NT = (((1,), (1,)), ((), ()))
_TN = (((0,), (0,)), ((), ()))


def _split_dot(v, tri):
    hi = v.astype(BF16)
    lo = (v - hi.astype(F32)).astype(BF16)
    return jnp.dot(hi, tri, preferred_element_type=F32) + jnp.dot(lo, tri, preferred_element_type=F32)


LOG2E = 1.4426950408889634
QK_SCALE = 1.0 / math.sqrt(HEAD_DIM)


def _sb_scores(q2_i, k_j, tri_l, masked):
    tq, tk = q2_i.shape[0], k_j.shape[0]
    s = lax.dot_general(q2_i, k_j, _NT, preferred_element_type=F32)
    lp = jnp.log(1.0 + jnp.exp2(-jnp.abs(s))) * LOG2E
    lb = jnp.minimum(s, 0.0) - lp
    l = lb - s
    mask = None
    if masked:
        mask = lax.broadcasted_iota(jnp.int32, (tq, tk), 0) > lax.broadcasted_iota(jnp.int32, (tq, tk), 1)
        l = jnp.where(mask, l, 0.0)
    return l, lb, lb + _split_dot(l, tri_l), mask


def _sb_weights(t, carry_l, mask):
    a = jnp.exp2(t + carry_l)
    return a if mask is None else jnp.where(mask, a, 0.0)


def _rowsum(v):
    return jnp.sum(v, axis=1, keepdims=True)


def _qk_norm(x_ref, w_ref):
    xv = x_ref[...].astype(F32)
    r = lax.rsqrt(jnp.mean(xv * xv, axis=-1, keepdims=True) + EPS)
    return xv * r, r


def _attn_fwd(proj, q_norm_w, k_norm_w, S, H, q_off, riders=()):
    t = _tile(S, ATT_T)
    n_q = S // t

    def body(q_ref, k_ref, v_ref, qw_ref, kw_ref, att_ref, attf_ref, qn_s, kn_s):
        qh, _ = _qk_norm(q_ref, qw_ref)
        qn_s[...] = (qh * qw_ref[...] * (QK_SCALE * LOG2E)).astype(BF16)
        kh, _ = _qk_norm(k_ref, kw_ref)
        kn_s[...] = (kh * kw_ref[...]).astype(BF16)
        tri_l = (lax.broadcasted_iota(jnp.int32, (t, t), 0) > lax.broadcasted_iota(jnp.int32, (t, t), 1)).astype(BF16)

        def rows(j):
            return pl.ds(pl.multiple_of(j * t, t), t)

        def q_step(i, _):
            q_i = qn_s[rows(i), :]

            def av(a, j):
                return jnp.dot(a.astype(BF16), v_ref[rows(j), :], preferred_element_type=F32)

            l, _, tt, mask = _sb_scores(q_i, kn_s[rows(i), :], tri_l, True)
            acc = av(_sb_weights(tt, 0.0, mask), i)
            carry = _rowsum(l)

            def single(_, c):
                carry, acc = c
                l, _, tt, _ = _sb_scores(q_i, kn_s[rows(i - 1), :], tri_l, False)
                return carry + _rowsum(l), acc + av(_sb_weights(tt, carry, None), i - 1)

            carry, acc = lax.fori_loop(0, i % 2, single, (carry, acc))
            top = i - 1 - i % 2

            def pair(p, c):
                carry, acc = c
                j0 = top - 2 * p
                l0, _, t0, _ = _sb_scores(q_i, kn_s[rows(j0), :], tri_l, False)
                l1, _, t1, _ = _sb_scores(q_i, kn_s[rows(j0 - 1), :], tri_l, False)
                mid = carry + _rowsum(l0)
                acc = acc + av(_sb_weights(t0, carry, None), j0) + av(_sb_weights(t1, mid, None), j0 - 1)
                return mid + _rowsum(l1), acc

            _, acc = lax.fori_loop(0, i // 2, pair, (carry, acc))
            att_ref[rows(i), :] = acc.astype(BF16)
            attf_ref[rows(i), :] = acc
            return 0

        lax.fori_loop(0, n_q, q_step, 0)

    def col(off):
        return pl.BlockSpec((S, HEAD_DIM), lambda h, off=off: (0, off + h))

    wspec = pl.BlockSpec((1, HEAD_DIM), lambda h: (0, 0))
    return _ride(
        "attn_fwd", body, riders, [proj, proj, proj, q_norm_w, k_norm_w], grid=(H,),
        in_specs=[col(q_off), col(q_off + H), col(q_off + 2 * H), wspec, wspec],
        out_specs=[col(0), col(0)],
        out_shape=[jax.ShapeDtypeStruct((S, H * HEAD_DIM), BF16), jax.ShapeDtypeStruct((S, H * HEAD_DIM), F32)],
        scratch_shapes=[pltpu.VMEM((S, HEAD_DIM), BF16), pltpu.VMEM((S, HEAD_DIM), BF16)],
        sem=("parallel",))


def _attn_bwd(proj, datt, attf, q_norm_w, k_norm_w, S, H, q_off, riders=()):
    t = _tile(S, ATT_T)
    n_q = S // t

    def body(q_ref, k_ref, v_ref, do_ref, o_ref, qw_ref, kw_ref, dq_ref, dk_ref, dv_ref, gq_ref, gk_ref,
             qn_s, kn_s, qz_s, kz_s, dk_s, dv_s, gq_s):
        qw, kw = qw_ref[...], kw_ref[...]
        qh, _ = _qk_norm(q_ref, qw_ref)
        qn_s[...] = (qh * qw * (QK_SCALE * LOG2E)).astype(BF16)
        qz_s[...] = (qh * qw * QK_SCALE).astype(BF16)
        kh, _ = _qk_norm(k_ref, kw_ref)
        kn_s[...] = (kh * kw).astype(BF16)
        kz_s[...] = (kh * kw * QK_SCALE).astype(BF16)
        dk_s[...] = jnp.zeros_like(dk_s)
        dv_s[...] = jnp.zeros_like(dv_s)
        gq_s[...] = jnp.zeros_like(gq_s)
        r_i = lax.broadcasted_iota(jnp.int32, (t, t), 0)
        c_i = lax.broadcasted_iota(jnp.int32, (t, t), 1)
        tri_l = (r_i > c_i).astype(BF16)
        tri_e = (r_i >= c_i).astype(BF16)

        def rows(j):
            return pl.ds(pl.multiple_of(j * t, t), t)

        def q_step(i, _):
            q_i = qn_s[rows(i), :]
            do_i = do_ref[rows(i), :]
            d_i = _rowsum(do_i.astype(F32) * o_ref[rows(i), :])

            def scores(j, masked):
                l, lb, tt, mask = _sb_scores(q_i, kn_s[rows(j), :], tri_l, masked)
                da = lax.dot_general(do_i, v_ref[rows(j), :], _NT, preferred_element_type=F32)
                return l, lb, tt, mask, da

            def grads(j, sc, carry_l, carry_e, dq_acc):
                l, lb, tt, mask, da = sc
                a_bf = _sb_weights(tt, carry_l, mask).astype(BF16)
                e = da * a_bf.astype(F32)
                p = (d_i - carry_e) - _split_dot(e, tri_e)
                dz = e - jnp.exp2(lb) * (e + p)
                if mask is not None:
                    dz = jnp.where(mask, dz, 0.0)
                dz = dz.astype(BF16)
                dk_s[rows(j), :] += lax.dot_general(dz, qz_s[rows(i), :], _TN, preferred_element_type=F32)
                dv_s[rows(j), :] += lax.dot_general(a_bf, do_i, _TN, preferred_element_type=F32)
                return (carry_l + _rowsum(l), carry_e + _rowsum(e),
                        dq_acc + jnp.dot(dz, kz_s[rows(j), :], preferred_element_type=F32))

            zero = jnp.zeros((t, 1), F32)
            first = (zero, zero, jnp.zeros((t, HEAD_DIM), F32))

            def group(js, diagonal_first, c):
                scs = [scores(j, diagonal_first and n == 0) for n, j in enumerate(js)]
                for j, sc in zip(js, scs):
                    c = grads(j, sc, *c)
                return c

            n_first = i % ATT_GROUP
            c = lax.switch(n_first, [functools.partial(group, [i - u for u in range(n + 1)], True, first)
                                     for n in range(ATT_GROUP)])
            top = i - 1 - n_first

            def whole(p, c):
                j0 = top - ATT_GROUP * p
                return group([j0 - u for u in range(ATT_GROUP)], False, c)

            _, _, dqn = lax.fori_loop(0, (i - n_first) // ATT_GROUP, whole, c)
            qv = q_ref[rows(i), :].astype(F32)
            r = lax.rsqrt(jnp.mean(qv * qv, axis=-1, keepdims=True) + EPS)
            xh = qv * r
            gq_s[...] += _colsum(dqn * xh)
            dxh = dqn * qw
            dq_ref[rows(i), :] = (r * (dxh - xh * jnp.mean(dxh * xh, axis=-1, keepdims=True))).astype(BF16)
            return 0

        lax.fori_loop(0, n_q, q_step, 0)
        gq_ref[0] = gq_s[...]
        kh, rk = _qk_norm(k_ref, kw_ref)
        dkn = dk_s[...]
        gk_ref[0] = _colsum(dkn * kh)
        dxh = dkn * kw
        dk_ref[...] = (rk * (dxh - kh * jnp.mean(dxh * kh, axis=-1, keepdims=True))).astype(BF16)
        dv_ref[...] = dv_s[...].astype(BF16)

    def col(off):
        return pl.BlockSpec((S, HEAD_DIM), lambda h, off=off: (0, off + h))

    wspec = pl.BlockSpec((1, HEAD_DIM), lambda h: (0, 0))
    gspec = pl.BlockSpec((1, 1, HEAD_DIM), lambda h: (h, 0, 0))
    act = jax.ShapeDtypeStruct((S, H * HEAD_DIM), BF16)
    gsh = jax.ShapeDtypeStruct((H, 1, HEAD_DIM), F32)
    return _ride(
        "attn_bwd", body, riders, [proj, proj, proj, datt, attf, q_norm_w, k_norm_w], grid=(H,),
        in_specs=[col(q_off), col(q_off + H), col(q_off + 2 * H), col(0), col(0), wspec, wspec],
        out_specs=[col(0), col(0), col(0), gspec, gspec],
        out_shape=[act, act, act, gsh, gsh],
        scratch_shapes=[pltpu.VMEM((S, HEAD_DIM), BF16)] * 4 + [pltpu.VMEM((S, HEAD_DIM), F32)] * 2
        + [pltpu.VMEM((1, HEAD_DIM), F32)],
        sem=("parallel",))


def _place():
    x, y, c = lax.axis_index("x"), lax.axis_index("y"), lax.axis_index("c")
    chips = [(1 - x, y), (x, 1 - y), (1 - x, 1 - y)]
    return x, y, c, chips


def _dev_allgather(name, v):
    m_per, n = v.shape

    def body(x_ref, out_ref, send_sems, recv_sems, local_sem):
        x, y, c, chips = _place()
        me, sibling = (x, y, c), (x, y, 1 - c)

        def rows(px, py, pc):
            return out_ref.at[pl.ds((4 * px + 2 * py + pc) * m_per, m_per), :]

        def copy(k, block, to, src=None):
            return pltpu.make_async_remote_copy(
                src_ref=rows(*block) if src is None else src, dst_ref=rows(*block),
                send_sem=send_sems.at[k], recv_sem=recv_sems.at[k], device_id=to, device_id_type=MESH)

        mine = pltpu.make_async_copy(x_ref, rows(*me), local_sem)
        mine.start()
        first = [copy(0, me, sibling, src=x_ref)]
        first += [copy(1 + j, me, (*chip, c), src=x_ref) for j, chip in enumerate(chips)]
        for cp in first:
            cp.start()
        passed = [copy(4 + j, (*chip, c), sibling) for j, chip in enumerate(chips)]
        for j, chip in enumerate(chips):
            copy(1 + j, (*chip, c), me).wait_recv()
            passed[j].start()
        copy(0, sibling, me).wait_recv()
        for j, chip in enumerate(chips):
            copy(4 + j, (*chip, 1 - c), me).wait_recv()
        for cp in first + passed:
            cp.wait_send()
        mine.wait()

    return _pcall(
        body, name=name, out_shape=jax.ShapeDtypeStruct((N_DEV * m_per, n), v.dtype),
        in_specs=[pl.BlockSpec(memory_space=pltpu.VMEM)], out_specs=pl.BlockSpec(memory_space=pltpu.VMEM),
        scratch_shapes=[pltpu.SemaphoreType.DMA((7,)), pltpu.SemaphoreType.DMA((7,)), pltpu.SemaphoreType.DMA],
        compiler_params=pltpu.CompilerParams(vmem_limit_bytes=VMEM_LIMIT_V7X),
    )(v)


class _W:
    def __init__(self, name, kind, R, C):
        self.name, self.kind, self.R, self.C = name, kind, R, C

    @property
    def shard_shape(self):
        return (self.R, self.C // N_CHIPS) if self.kind == "col" else (self.R // N_CHIPS, self.C)

    @property
    def half_rows(self):
        return self.shard_shape[0] // 2

    def shard_half(self, ref, half):
        return ref.at[pl.ds(half * self.half_rows, self.half_rows), :]

    def region(self, full_ref, chip, half):
        hr = self.half_rows
        if self.kind == "col":
            cw = self.C // N_CHIPS
            return full_ref.at[pl.ds(half * hr, hr), pl.ds(chip * cw, cw)]
        return full_ref.at[pl.ds(chip * (2 * hr) + half * hr, hr), :]


def _ag_rider(ws, fulls, n_ch=4, chunks=None):
    n_w = len(ws)
    lo, hi = chunks or (0, n_ch)
    per = 6

    def parts(full, sems):
        send_sems, recv_sems = sems
        x, y, c, _ = _place()
        xn, yn, dg = (1 - x, y), (x, 1 - y), (1 - x, 1 - y)
        via = (x + (1 - c) * (1 - 2 * x), y + c * (1 - 2 * y))
        to = (x + c * (1 - 2 * x), y + (1 - c) * (1 - 2 * y))

        def reg(i, chip, half, t):
            nr = ws[i].half_rows // n_ch
            return ws[i].region(full[i], 2 * chip[0] + chip[1], half).at[pl.ds(t * nr, nr), :]

        def copy(r, i, t, k, dev):
            s = (i * (hi - lo) + t - lo) * per + k
            return pltpu.make_async_remote_copy(src_ref=r, dst_ref=r, send_sem=send_sems.at[s],
                                                recv_sem=recv_sems.at[s], device_id=dev, device_id_type=MESH)

        def direct(i, t, k):
            return copy(reg(i, (x, y), c, t), i, t, k, (*(via, to)[k], c))

        def direct_in(i, t, k):
            return copy(reg(i, (via, to)[k], c, t), i, t, k, (*(via, to)[k], c))

        def relay(i, t):
            return copy(reg(i, via, c, t), i, t, 2, (*to, c))

        def relay_in(i, t):
            return copy(reg(i, dg, c, t), i, t, 2, (*to, c))

        def hand(i, t, k, half):
            return copy(reg(i, (xn, yn, dg)[k], half, t), i, t, 3 + k, (x, y, 1 - c))

        return c, direct, direct_in, relay, relay_in, hand

    def start(_, full, sems):
        _, direct, _, _, _, _ = parts(full, sems)
        for t in range(lo, hi):
            for i in range(n_w):
                direct(i, t, 0).start()
                direct(i, t, 1).start()

    def arrived(t):
        def step(_, full, sems):
            c, _, direct_in, relay, relay_in, hand = parts(full, sems)
            for i in range(n_w):
                direct_in(i, t, 0).wait_recv()
                direct_in(i, t, 1).wait_recv()
                relay(i, t).start()
                hand(i, t, 0, c).start()
                hand(i, t, 1, c).start()
        return step

    def finish(_, full, sems):
        c, direct, _, relay, relay_in, hand = parts(full, sems)
        for t in range(lo, hi):
            for i in range(n_w):
                relay_in(i, t).wait_recv()
                hand(i, t, 2, c).start()
        for i in range(n_w):
            for t in range(lo, hi):
                for k in range(3):
                    hand(i, t, k, 1 - c).wait_recv()
        for i in range(n_w):
            for t in range(lo, hi):
                direct(i, t, 0).wait_send()
                direct(i, t, 1).wait_send()
                relay(i, t).wait_send()
                for k in range(3):
                    hand(i, t, k, c).wait_send()

    n_sem = per * (hi - lo) * n_w
    return _Rider(fulls, [jax.ShapeDtypeStruct((w.R, w.C), BF16) for w in ws],
                  [pltpu.SemaphoreType.DMA((n_sem,)), pltpu.SemaphoreType.DMA((n_sem,))], start, finish,
                  steps=[arrived(t) for t in range(lo, hi)], aliases={i: i for i in range(n_w)})


def _cast_into_full(ws, shards, chip_arr, riders=()):
    sr, sc = ws[0].shard_shape
    assert all(w.shard_shape == (sr, sc) for w in ws)
    tr, tc = _tile(sr, 512), _tile(sc, 2048)
    n_r, n_c = sr // tr, sc // tc

    def place(w):
        if w.kind == "col":
            return pl.BlockSpec((tr, tc), lambda i, j, chip: (i, chip[0] * n_c + j))
        return pl.BlockSpec((tr, tc), lambda i, j, chip: (chip[0] * n_r + i, j))

    def body(*refs):
        for a_ref, o_ref in zip(refs[:len(ws)], refs[len(ws):]):
            o_ref[...] = a_ref[...].astype(BF16)

    return _ride("cast_" + "_".join(w.name for w in ws), body, riders, list(shards), grid=(n_r, n_c),
                 in_specs=[pl.BlockSpec((tr, tc), lambda i, j, chip: (i, j))] * len(ws),
                 out_specs=[place(w) for w in ws], out_shape=[jax.ShapeDtypeStruct((w.R, w.C), BF16) for w in ws],
                 scratch_shapes=[], sem=("parallel", "parallel"), scalars=chip_arr)


def _half_view(w, g):
    return g if w.kind == "col" else g.reshape(N_CHIPS, w.R // N_CHIPS, w.C)


def _px_rider(ws, grads):
    n_w = len(ws)

    def copies(g, got, sems):
        send_sems, recv_sems = sems
        x, y, c, _ = _place()

        def half_all(w, ref, half):
            hr = w.half_rows
            if w.kind == "col":
                return ref.at[pl.ds(half * hr, hr), :]
            return ref.at[:, pl.ds(half * hr, hr), :]

        return [pltpu.make_async_remote_copy(
            src_ref=half_all(w, g[i], 1 - c), dst_ref=got[i], send_sem=send_sems.at[i], recv_sem=recv_sems.at[i],
            device_id=(x, y, 1 - c), device_id_type=MESH) for i, w in enumerate(ws)]

    def start(g, got, sems):
        for cp in copies(g, got, sems):
            cp.start()

    def finish(g, got, sems):
        for cp in copies(g, got, sems):
            cp.wait_recv()
            cp.wait_send()

    def got_shape(w):
        hr = w.half_rows
        return (hr, w.C) if w.kind == "col" else (N_CHIPS, hr, w.C)

    return _Rider([_half_view(w, g) for w, g in zip(ws, grads)],
                  [jax.ShapeDtypeStruct(got_shape(w), BF16) for w in ws],
                  [pltpu.SemaphoreType.DMA((n_w,)), pltpu.SemaphoreType.DMA((n_w,))], start, finish)


def _pair_sum(w, g, got, c_arr):
    hr = w.half_rows
    if w.kind == "col":
        tr, tc = _tile(hr, 512), _tile(w.C, 2048)
        n_r = hr // tr
        grid = (n_r, w.C // tc)
        g_spec = pl.BlockSpec((tr, tc), lambda i, j, c: (c[0] * n_r + i, j))
        o_spec = pl.BlockSpec((tr, tc), lambda i, j, c: (i, j))
    else:
        tr = _tile(hr, 512)
        n_r = hr // tr
        grid = (N_CHIPS, n_r)
        g_spec = pl.BlockSpec((1, tr, w.C), lambda s, i, c: (s, c[0] * n_r + i, 0))
        o_spec = pl.BlockSpec((1, tr, w.C), lambda s, i, c: (s, i, 0))

    def body(c_ref, g_ref, got_ref, out_ref):
        out_ref[...] = (g_ref[...].astype(F32) + got_ref[...].astype(F32)).astype(BF16)

    return _pcall(
        body, name="grad_pair_sum_" + w.name, out_shape=jax.ShapeDtypeStruct(got.shape, BF16),
        grid_spec=pltpu.PrefetchScalarGridSpec(num_scalar_prefetch=1, grid=grid, in_specs=[g_spec, o_spec],
                                               out_specs=o_spec),
        compiler_params=_params(("parallel", "parallel")),
    )(c_arr, _half_view(w, g), got)


def _chip_sum(w, p, q, cc_arr):
    hr, cols = w.half_rows, w.shard_shape[1]
    tr, tc = _tile(hr, 512), _tile(cols, 2048)
    n_r, n_c = hr // tr, cols // tc

    def body(cc_ref, own, q1, q2, q3, out_ref):
        own_v = own[...] if w.kind == "col" else own[0]
        out_ref[...] = ((own_v.astype(F32) + q1[0].astype(F32)) + q2[0].astype(F32)) + q3[0].astype(F32)

    if w.kind == "col":
        own_spec = pl.BlockSpec((tr, tc), lambda i, j, cc: (i, cc[1] * n_c + j))
    else:
        own_spec = pl.BlockSpec((1, tr, tc), lambda i, j, cc: (cc[1], i, j))
    q_specs = [pl.BlockSpec((1, tr, tc), lambda i, j, cc, s=s: ((cc[1] + s) % N_CHIPS, i, j)) for s in (1, 2, 3)]
    return _pcall(
        body, name="grad_chip_sum_" + w.name, out_shape=jax.ShapeDtypeStruct(w.shard_shape, F32),
        grid_spec=pltpu.PrefetchScalarGridSpec(
            num_scalar_prefetch=1, grid=(n_r, n_c), in_specs=[own_spec] + q_specs,
            out_specs=pl.BlockSpec((tr, tc), lambda i, j, cc: (cc[0] * n_r + i, j))),
        compiler_params=_params(("parallel", "parallel")),
    )(cc_arr, p, q, q, q)


_SEM = pl.BlockSpec(memory_space=pltpu.SEMAPHORE)
_HBM = pl.BlockSpec(memory_space=pltpu.HBM)


def _split_copies(kind, ws, p, land, send_sems, recv_sems):
    x, y, c, chips = _place()
    my_chip = 2 * x + y
    pairs = []
    for i, w in enumerate(ws):
        if kind == "pair":
            hr = w.half_rows
            src = p[i].at[pl.ds((1 - c) * hr, hr), :] if w.kind == "col" else p[i].at[:, pl.ds((1 - c) * hr, hr), :]
            cp = pltpu.make_async_remote_copy(src_ref=src, dst_ref=land[i], send_sem=send_sems.at[i],
                                              recv_sem=recv_sems.at[i], device_id=(x, y, 1 - c), device_id_type=MESH)
            pairs.append((cp, cp))
            continue
        for k, chip in enumerate(chips):
            to_chip = 2 * chip[0] + chip[1]
            src = p[i].at[:, pl.ds(to_chip * (w.C // N_CHIPS), w.C // N_CHIPS)] if w.kind == "col" else p[i].at[to_chip]
            kw = dict(send_sem=send_sems.at[3 * i + k], recv_sem=recv_sems.at[3 * i + k], device_id=(*chip, c),
                      device_id_type=MESH)
            pairs.append((pltpu.make_async_remote_copy(src_ref=src, dst_ref=land[i].at[my_chip], **kw),
                          pltpu.make_async_remote_copy(src_ref=src, dst_ref=land[i].at[to_chip], **kw)))
    return pairs


def _split_start(name, kind, ws, arrays):
    n_w = len(ws)
    if kind == "pair":
        arrays = [_half_view(w, g) for w, g in zip(ws, arrays)]
        lands = [lax.empty((w.half_rows, w.C) if w.kind == "col" else (N_CHIPS, w.half_rows, w.C), BF16) for w in ws]
    else:
        lands = [lax.empty((N_CHIPS, w.half_rows, w.shard_shape[1]), BF16) for w in ws]
    n_sem = n_w if kind == "pair" else 3 * n_w

    def body(*refs):
        p, land = refs[:n_w], refs[n_w:2 * n_w]
        for out, _ in _split_copies(kind, ws, p, land, refs[2 * n_w], refs[2 * n_w + 1]):
            out.start()
        refs[-1][...] = jnp.zeros_like(refs[-1])

    arrays = [pltpu.with_memory_space_constraint(a, pltpu.HBM) for a in list(arrays) + lands]
    res = _pcall(
        body, name=name,
        out_shape=(pltpu.SemaphoreType.DMA((n_sem,)), pltpu.SemaphoreType.DMA((n_sem,)),
                   *[pltpu.HBM(a.shape, a.dtype) for a in arrays], jax.ShapeDtypeStruct((SUBLANES, LANES), F32)),
        in_specs=[_HBM] * (2 * n_w),
        out_specs=(_SEM, _SEM, *[_HBM] * (2 * n_w), pl.BlockSpec(memory_space=pltpu.VMEM)),
        input_output_aliases={i: 2 + i for i in range(2 * n_w)},
        compiler_params=pltpu.CompilerParams(has_side_effects=pltpu.SideEffectType.DATAFLOW_SIDE_EFFECTING),
    )(*arrays)
    return (kind, ws, res[0], res[1], list(res[2:2 + n_w]), list(res[2 + n_w:2 + 2 * n_w])), res[-1]


def _split_wait(name, flight, after):
    kind, ws, send_sems, recv_sems, arrays, lands = flight
    n_w = len(ws)

    def body(*refs):
        p, land = refs[:n_w], refs[n_w:2 * n_w]
        for _, cp in _split_copies(kind, ws, p, land, refs[2 * n_w], refs[2 * n_w + 1]):
            cp.wait_send()
            cp.wait_recv()

    res = _pcall(
        body, name=name,
        out_shape=[pltpu.HBM(a.shape, a.dtype) for a in list(arrays) + list(lands)],
        in_specs=[_HBM] * (2 * n_w) + [_SEM, _SEM] + [ANY] * len(after), out_specs=[_HBM] * (2 * n_w),
        input_output_aliases={i: i for i in range(2 * n_w)},
        compiler_params=pltpu.CompilerParams(has_side_effects=pltpu.SideEffectType.DATAFLOW_SIDE_EFFECTING),
    )(*arrays, *lands, send_sems, recv_sems, *after)
    return list(res[:n_w]), list(res[n_w:])


def _sf_rider(ws, grads):
    n_w = len(ws)

    def copy(g, sems, i, half):
        send_sems, recv_sems = sems
        x, y, c, _ = _place()
        h = c if half == "mine" else 1 - c
        reg = ws[i].shard_half(g[i], h)
        return pltpu.make_async_remote_copy(src_ref=reg, dst_ref=reg, send_sem=send_sems.at[i], recv_sem=recv_sems.at[i],
                                            device_id=(x, y, 1 - c), device_id_type=MESH)

    def start(_, g, sems):
        for i in range(n_w):
            copy(g, sems, i, "mine").start()

    def finish(_, g, sems):
        for i in range(n_w):
            copy(g, sems, i, "other").wait_recv()
            copy(g, sems, i, "mine").wait_send()

    return _Rider(grads, [jax.ShapeDtypeStruct(w.shard_shape, F32) for w in ws],
                  [pltpu.SemaphoreType.DMA((n_w,)), pltpu.SemaphoreType.DMA((n_w,))], start, finish,
                  aliases={i: i for i in range(n_w)})


def _adamw_math(w, g, m, v):
    m = ADAM_B1 * m + (1.0 - ADAM_B1) * g
    v = ADAM_B2 * v + (1.0 - ADAM_B2) * (g * g)
    m_hat = m / (1.0 - ADAM_B1 ** ADAM_STEP)
    v_hat = v / (1.0 - ADAM_B2 ** ADAM_STEP)
    delta = -ADAM_LR * (m_hat / (jnp.sqrt(v_hat) + ADAM_EPS) + ADAM_WD * w)
    return delta, m, v


def _adamw(name, w, g, m, v, after=None):
    R, C = w.shape
    tr, tc = _tile(R, 256), _tile(C, 2048)
    behind = [] if after is None else [after]

    def body(w_ref, g_ref, m_ref, v_ref, *rest):
        g_out, d_out, m_out, v_out = rest[len(behind):]
        g = g_ref[...]
        g_out[...] = g
        d_out[...], m_out[...], v_out[...] = _adamw_math(w_ref[...], g, m_ref[...], v_ref[...])

    spec = pl.BlockSpec((tr, tc), lambda i, j: (i, j))
    sh = jax.ShapeDtypeStruct((R, C), F32)
    return _pcall(body, name=name, grid=(R // tr, C // tc), in_specs=[spec] * 4 + [ANY] * len(behind),
                  out_specs=[spec] * 4, out_shape=[sh] * 4, compiler_params=_params(("parallel", "parallel")))(
                      w, g, m, v, *behind)


def _ada_update(sct, dmod_sh, w, m, v, riders=()):
    R, C = w.shape
    tr, tc = _tile(R, 256), _tile(C, 1024)

    def body(s_ref, d_ref, w_ref, m_ref, v_ref, g_out, d_out, m_out, v_out):
        s, d = s_ref[...], d_ref[...]
        g = s[:, 0:1] * d[0:1, :]
        for b in range(1, N_DEV):
            g += s[:, b:b + 1] * d[b:b + 1, :]
        g_out[...] = g
        d_out[...], m_out[...], v_out[...] = _adamw_math(w_ref[...], g, m_ref[...], v_ref[...])

    spec = pl.BlockSpec((tr, tc), lambda i, j: (i, j))
    sh = jax.ShapeDtypeStruct((R, C), F32)
    return _ride(
        "ada_update", body, riders, [sct, dmod_sh, w, m, v], grid=(R // tr, C // tc),
        in_specs=[pl.BlockSpec((tr, N_DEV), lambda i, j: (i, 0)), pl.BlockSpec((N_DEV, tc), lambda i, j: (0, j)),
                  spec, spec, spec],
        out_specs=[spec] * 4, out_shape=[sh] * 4, scratch_shapes=[], sem=("parallel", "parallel"))


def _silu_rows(c_row):
    D = c_row.shape[1]

    def body(c_ref, o_ref):
        cv = c_ref[...]
        o_ref[...] = cv * jax.nn.sigmoid(cv)

    return _pcall(body, name="silu_c", out_shape=jax.ShapeDtypeStruct((1, D), F32))(c_row)


def _pack_partials(parts, widths, total):
    n = len(widths)

    def body(*refs):
        loss_p, out_ref = refs[n], refs[n + 1]
        off = 0
        for ref, wd in zip(refs[:n], widths):
            out_ref[:, off:off + wd] = jnp.sum(ref[...], axis=0)
            off += wd
        loss = jnp.sum(jnp.sum(loss_p[...], axis=0), axis=1, keepdims=True)
        out_ref[:, off:off + LANES] = jnp.broadcast_to(loss, (1, LANES))
        if off + LANES < total:
            out_ref[:, off + LANES:total] = jnp.zeros((1, total - off - LANES), F32)

    return _pcall(body, name="pack_partials", out_shape=jax.ShapeDtypeStruct((1, total), F32))(*parts)


def _small_update(gathered, offsets, params, loss_off):
    n_p = len(params)

    def over_devices(g_ref, off, wd):
        blk = g_ref[:, off:off + wd]
        g = blk[0:1, :]
        for b in range(1, N_DEV):
            g = g + blk[b:b + 1, :]
        return g

    def body(*refs):
        g_ref = refs[0]
        prm = refs[1:1 + 3 * n_p]
        outs = refs[1 + 3 * n_p:]
        outs[4 * n_p][...] = over_devices(g_ref, loss_off, LANES)
        for i, (off, wd) in enumerate(offsets):
            g = over_devices(g_ref, off, wd)
            w, m, v = prm[3 * i][...], prm[3 * i + 1][...], prm[3 * i + 2][...]
            outs[4 * i][...] = g
            outs[4 * i + 1][...], outs[4 * i + 2][...], outs[4 * i + 3][...] = _adamw_math(w, g, m, v)

    flat = [a for t in params for a in t]
    out_shape = [jax.ShapeDtypeStruct(t[0].shape, F32) for t in params for _ in range(4)]
    out_shape.append(jax.ShapeDtypeStruct((1, LANES), F32))
    return _pcall(body, name="small_update", out_shape=out_shape)(gathered, *flat)


def kernel(x, c, w_ada, b_ada, norm1_w, w_in, q_norm_w, k_norm_w, w_pool, pool_scale, w_a_up, w_b_up, w_o, norm2_w, w_ff1, w_ff2, loss_target, m_w_ada, m_b_ada, m_norm1_w, m_w_in, m_q_norm_w, m_k_norm_w, m_w_pool, m_pool_scale, m_w_a_up, m_w_b_up, m_w_o, m_norm2_w, m_w_ff1, m_w_ff2, v_w_ada, v_b_ada, v_norm1_w, v_w_in, v_q_norm_w, v_k_norm_w, v_w_pool, v_pool_scale, v_w_a_up, v_w_b_up, v_w_o, v_norm2_w, v_w_ff1, v_w_ff2):
    _, S, D = x.shape
    PW = D // 2
    H = PW // HEAD_DIM
    cg = PW // N_GROUPS
    IN = w_in.shape[2] * N_CHIPS
    FF = w_ff1.shape[2] * N_CHIPS
    A_COLS = w_ada.shape[2]
    xi, yi, ci = lax.axis_index("x"), lax.axis_index("y"), lax.axis_index("c")
    chip = 2 * xi + yi
    dev = 2 * chip + ci
    c_arr = jnp.reshape(ci, (1,)).astype(jnp.int32)
    x2, tgt = x[0], loss_target[0]

    ws = [_W("w_in", "col", D, IN), _W("w_pool", "row", PW, cg), _W("w_a_up", "col", PW, D),
          _W("w_b_up", "col", PW, D), _W("w_o", "row", D, D), _W("w_ff1", "col", D, FF), _W("w_ff2", "row", FF, D)]
    w32 = [w_in[0], w_pool[0].reshape(cg, cg), w_a_up[0], w_b_up[0], w_o[0], w_ff1[0], w_ff2[0]]
    m32 = [m_w_in[0], m_w_pool[0].reshape(cg, cg), m_w_a_up[0], m_w_b_up[0], m_w_o[0], m_w_ff1[0], m_w_ff2[0]]
    v32 = [v_w_in[0], v_w_pool[0].reshape(cg, cg), v_w_a_up[0], v_w_b_up[0], v_w_o[0], v_w_ff1[0], v_w_ff2[0]]

    W_IN, W_POOL, W_A, W_B, W_O, W_FF1, W_FF2 = ws
    chip_arr = jnp.reshape(chip, (1,)).astype(jnp.int32)
    cc_arr = jnp.stack([ci, chip]).astype(jnp.int32)
    s_in, s_pool, s_a, s_b, s_o = [_cast_into_full([w], [a], chip_arr)[0] for w, a in zip(ws[:5], w32[:5])]
    (s_ff1, s_ff2), ((win_f,),) = _cast_into_full([W_FF1, W_FF2], w32[5:], chip_arr, riders=[_ag_rider([W_IN], [s_in])])

    sc_row = _silu_rows(c)
    sc_all = _dev_allgather("gather_silu_c", sc_row.reshape(SUBLANES, D // SUBLANES)).reshape(N_DEV, D)
    sc16 = jnp.concatenate([sc_all, jnp.zeros_like(sc_all)], axis=0)
    b_cols = lax.dynamic_slice(b_ada, (0, chip * A_COLS), (1, A_COLS))
    (mod_cols,) = _mm("mod_cols", [(sc16, w_ada[0])], M=2 * N_DEV, N=A_COLS, K=D, tm=16, tn=1024, tk=1024,
                      a_pro=lambda a: a.astype(BF16), b_pro=lambda b: b.astype(BF16),
                      extras=[(b_cols, "row", 0)], outs=[_tile_out(F32)], epi=lambda accs, ex: [accs[0] + ex[0]])
    mod_all = _dev_allgather("gather_mod", mod_cols[:N_DEV]).reshape(N_CHIPS, 2, N_DEV, A_COLS)
    mod_row = lax.dynamic_index_in_dim(mod_all[:, 0], dev, axis=1, keepdims=False).reshape(1, N_CHIPS * A_COLS)
    shift1, scale1, gate1, shift2, scale2, gate2 = [mod_row[:, i * D:(i + 1) * D] for i in range(6)]

    WIDE = dict(tm=2048, tn=512, tk=2048)
    DEEP = dict(tm=1024, tn=1024, tk=2048)
    DEEPER = dict(tm=1024, tn=1024, tk=4096)
    h = _norm_mod("norm1_mod", x2, norm1_w, scale1, shift1)
    (proj,), ((wpool_f, wa_f, wb_f, wo_f),) = _mm(
        "in_proj", [(h, win_f)], M=S, N=IN, K=D, outs=[_tile_out(BF16)], epi=lambda accs, ex: [accs[0]], **WIDE,
        riders=[_ag_rider([W_POOL, W_A, W_B, W_O], [s_pool, s_a, s_b, s_o], n_ch=2)])
    pooled, pa = _pool_fwd(proj, wpool_f, pool_scale, S, PW)
    (att, attf), ((wff1_f,),) = _attn_fwd(proj, q_norm_w, k_norm_w, S, H, PW // HEAD_DIM,
                                          riders=[_ag_rider([W_FF1], [s_ff1])])

    def merge_epi(accs, ex):
        sa, sb = jax.nn.sigmoid(ex[0].astype(F32)), jax.nn.sigmoid(ex[1].astype(F32))
        return [sa * accs[0] + sb * accs[1], accs[0], accs[1]]

    (merged, ya, yb), (ff2_a,) = _mm("branch_up_merge", [(pa, wa_f), (att, wb_f)], M=S, N=D, K=PW,
                                     extras=[(proj, "tile", 4 * PW), (proj, "tile", 4 * PW + D)],
                                     outs=[_tile_out(BF16)] * 3, epi=merge_epi,
                                     riders=[_ag_rider([W_FF2], [s_ff2], chunks=(0, 1))])
    (x1, o), (ff2_b,) = _mm("out_proj", [(merged, wo_f)], M=S, N=D, K=D, extras=[(x2, "tile", 0), (gate1, "row", 0)],
                            outs=[_tile_out(F32), _tile_out(BF16)], epi=lambda accs, ex: [ex[0] + ex[1] * accs[0], accs[0]],
                            riders=[_ag_rider([W_FF2], ff2_a, chunks=(1, 2))], **WIDE)
    h2 = _norm_mod("norm2_mod", x1, norm2_w, scale2, shift2)
    (rl,), ((wff2_f,),) = _mm("ff1", [(h2, wff1_f)], M=S, N=FF, K=D, outs=[_tile_out(BF16)], **WIDE,
                              epi=lambda accs, ex: [jnp.maximum(accs[0], 0.0)],
                              riders=[_ag_rider([W_FF2], ff2_b, chunks=(2, 4))])

    def square(a):
        af = a.astype(F32)
        return (af * af).astype(BF16)

    def loss_epi(accs, ex):
        x1_t, tgt_t, g2 = ex
        f = accs[0]
        diff = (x1_t + g2 * f) - tgt_t
        dy = diff * (1.0 / D)
        return [dy, dy * g2, _colsum(dy * f), _colsum(diff * diff)]

    dy, df, dgate2_p, loss_p = _mm("ff2_loss", [(rl, wff2_f)], M=S, N=D, K=FF, a_pro=square, **DEEP,
                                   extras=[(x1, "tile", 0), (tgt, "tile", 0), (gate2, "row", 0)],
                                   outs=[_tile_out(F32), _tile_out(BF16), _COLSUM, _COLSUM], epi=loss_epi)

    tied = []

    def behind(token, a):
        a, token = lax.optimization_barrier((a, token))
        tied.append(token)
        return a

    def pair_sums(group, partials, got):
        return [_pair_sum(w, g, r, c_arr) for w, g, r in zip(group, partials, got)]

    def chip_sums(group, sums, from_chips):
        return [_chip_sum(w, p, q, cc_arr) for w, p, q in zip(group, sums, from_chips)]

    first = lambda accs, ex: [accs[0]]
    gmm = dict(ta=True, outs=[_tile_out(BF16)], epi=first, **WIDE)
    (g_ff2,) = _mm("grad_w_ff2", [(rl, df)], M=FF, N=D, K=S, a_pro=square, ta=True, tm=512, tn=2048, tk=2048,
                   outs=[_tile_out(BF16)], epi=first)
    flight, token = _split_start("pair_w_ff2_start", "pair", [W_FF2], [g_ff2])
    (dz1,) = _mm("d_ff_hidden", [(behind(token, df), wff2_f)], M=S, N=FF, K=D, tb=True, extras=[(rl, "tile", 0)],
                 outs=[_tile_out(BF16)], epi=lambda accs, ex: [accs[0] * (2.0 * ex[0].astype(F32))], **WIDE)
    sum_ff2 = pair_sums([W_FF2], *_split_wait("pair_w_ff2_wait", flight, after=[dz1] + tied))
    chip_ff2, token = _split_start("chip_w_ff2_start", "chip", [W_FF2], sum_ff2)
    (g_ff1,) = _mm("grad_w_ff1", [(behind(token, h2), dz1)], M=D, N=FF, K=S, **gmm)
    flight, token = _split_start("pair_w_ff1_start", "pair", [W_FF1], [g_ff1])
    (dh2,) = _mm("d_h2", [(behind(token, dz1), wff1_f)], M=S, N=D, K=FF, tb=True, outs=[_tile_out(F32)], epi=first,
                 **DEEPER)
    sum_ff1 = pair_sums([W_FF1], *_split_wait("pair_w_ff1_wait", flight, after=[dh2] + tied))
    chip_ff1, token = _split_start("chip_w_ff1_start", "chip", [W_FF1], sum_ff1)
    dx1, dshift2_p, dscale2_p, gn2_p, do, dgate1_p = _norm_mod_bwd("norm2_bwd", behind(token, dh2), x1, dy, norm2_w, scale2,
                                                                   gate_o=(o, gate1))
    (g_wo,) = _mm("grad_w_o", [(merged, do)], M=D, N=D, K=S, **gmm)

    def gate_epi(accs, ex):
        dm = accs[0]
        sa, sb = jax.nn.sigmoid(ex[0].astype(F32)), jax.nn.sigmoid(ex[1].astype(F32))
        ya_t, yb_t = ex[2].astype(F32), ex[3].astype(F32)
        return [dm * sa, dm * sb, dm * ya_t * (sa * (1.0 - sa)), dm * yb_t * (sb * (1.0 - sb))]

    dya, dyb, dga, dgb = _mm("d_merged", [(do, wo_f)], M=S, N=D, K=D, tb=True, tm=1024, tn=512, tk=2048,
                             extras=[(proj, "tile", 4 * PW), (proj, "tile", 4 * PW + D), (ya, "tile", 0), (yb, "tile", 0)],
                             outs=[_tile_out(BF16)] * 4, epi=gate_epi)
    both = lambda accs, ex: [accs[0], accs[1]]
    g_wa, g_wb = _mm("grad_w_up", [(pa, dya), (att, dyb)], M=PW, N=D, K=S, ta=True, outs=[_tile_out(BF16)] * 2, epi=both,
                     **WIDE)
    mid = [W_A, W_B, W_O]
    flight, token = _split_start("pair_mid_start", "pair", mid, [g_wa, g_wb, g_wo])
    dpa, datt = _mm("d_branches", [(dya, wa_f), (behind(token, dyb), wb_f)], M=S, N=PW, K=D, tb=True,
                    outs=[_tile_out(F32), _tile_out(BF16)], epi=both, tm=1024, tn=512, tk=2048)
    sum_mid = pair_sums(mid, *_split_wait("pair_mid_wait", flight, after=[datt] + tied))
    chip_mid, token = _split_start("chip_mid_start", "chip", mid, sum_mid)
    du, g_wpool4, gscale_p = _pool_bwd(dpa, pooled, wpool_f, pool_scale, S, PW)
    dq, dk, dv, gq_p, gk_p = _attn_bwd(proj, behind(token, datt), attf, q_norm_w, k_norm_w, S, H, PW // HEAD_DIM)
    dproj = jnp.concatenate([du, dq, dk, dv, dga, dgb], axis=1)
    early = [W_FF1, W_FF2]
    sum_ff1, q_ff1 = _split_wait("chip_w_ff1_wait", chip_ff1, after=[dq] + tied)
    sum_ff2, q_ff2 = _split_wait("chip_w_ff2_wait", chip_ff2, after=[dq] + tied)
    halves_early = chip_sums(early, sum_ff1 + sum_ff2, q_ff1 + q_ff2)
    (g_win,), (grads_early,) = _mm("grad_w_in", [(h, dproj)], M=D, N=IN, K=S, riders=[_sf_rider(early, halves_early)],
                                   **gmm)
    last = [W_IN, W_POOL]
    g_last = [g_win, g_wpool4.reshape(PW, cg)]
    sum_mid, q_mid = _split_wait("chip_mid_wait", chip_mid, after=[g_win] + tied)
    halves_mid = chip_sums(mid, sum_mid, q_mid)
    (dh,), (got_last, grads_mid) = _mm("d_h", [(dproj, win_f)], M=S, N=D, K=IN, tb=True, outs=[_tile_out(F32)], epi=first,
                                       riders=[_px_rider(last, g_last), _sf_rider(mid, halves_mid)], **DEEPER)
    sum_last = pair_sums(last, g_last, got_last)
    grad_x, dshift1_p, dscale1_p, gn1_p = _norm_mod_bwd("norm1_bwd", dh, x2, dx1, norm1_w, scale1)

    parts = [dshift1_p, dscale1_p, dgate1_p, dshift2_p, dscale2_p, dgate2_p, gn1_p, gn2_p,
             gscale_p.reshape(1, 1, PW), gq_p, gk_p]
    widths = [D] * 8 + [PW, HEAD_DIM, HEAD_DIM]
    used = sum(widths)
    P = -(-(used + LANES) // (SUBLANES * LANES)) * (SUBLANES * LANES)
    packed = _pack_partials(parts + [loss_p], widths, P)
    gathered = _dev_allgather("gather_vector_grads", packed.reshape(SUBLANES, P // SUBLANES)).reshape(N_DEV, P)
    sum_last, gathered = lax.optimization_barrier((sum_last, gathered))
    chip_last, token = _split_start("chip_last_start", "chip", last, sum_last)
    small = [(b_ada, m_b_ada, v_b_ada), (norm1_w, m_norm1_w, v_norm1_w), (norm2_w, m_norm2_w, v_norm2_w),
             (pool_scale, m_pool_scale, v_pool_scale), (q_norm_w, m_q_norm_w, v_q_norm_w),
             (k_norm_w, m_k_norm_w, v_k_norm_w)]
    offsets = [(0, 6 * D), (6 * D, D), (7 * D, D), (8 * D, PW), (8 * D + PW, HEAD_DIM), (8 * D + PW + HEAD_DIM, HEAD_DIM)]
    su = _small_update(gathered, offsets, small, used)
    (g_b, d_b, nm_b, nv_b, g_n1, d_n1, nm_n1, nv_n1, g_n2, d_n2, nm_n2, nv_n2, g_ps, d_ps, nm_ps, nv_ps,
     g_qn, d_qn, nm_qn, nv_qn, g_kn, d_kn, nm_kn, nv_kn, loss_sum) = su
    dmod_sh = lax.dynamic_slice(gathered, (0, chip * A_COLS), (N_DEV, A_COLS))
    dmod_sh, token = lax.optimization_barrier((dmod_sh, token))
    g_ada, d_ada, nm_ada, nv_ada = _ada_update(sc_all.T, dmod_sh, w_ada[0], m_w_ada[0], v_w_ada[0])

    upd_done = [_adamw("adamw_" + w.name, a, g, m, v, after=token)
                for w, a, g, m, v in zip(ws[2:], w32[2:], list(grads_mid) + list(grads_early), m32[2:], v32[2:])]

    sum_last, q_last = _split_wait("chip_last_wait", chip_last, after=[nv_ada] + [u[3] for u in upd_done])
    halves_last = chip_sums(last, sum_last, q_last)
    filled = _run_rider("grad_sibling_fill", _sf_rider(last, halves_last))
    upd = [_adamw("adamw_" + w.name, a, g, m, v) for w, a, g, m, v in zip(ws[:2], w32[:2], filled, m32[:2], v32[:2])]
    upd += upd_done

    loss = (0.5 / D) * loss_sum[0, 0]

    def up(a):
        return a[None]

    def pool4(a):
        return a.reshape(1, N_GROUPS, cg // N_CHIPS, cg)

    (gr_win, d_win, nm_win, nv_win), (gr_wp, d_wp, nm_wp, nv_wp), (gr_wa, d_wa, nm_wa, nv_wa), \
        (gr_wb, d_wb, nm_wb, nv_wb), (gr_wo, d_wo, nm_wo, nv_wo), (gr_f1, d_f1, nm_f1, nv_f1), \
        (gr_f2, d_f2, nm_f2, nv_f2) = upd
    return (
        loss, grad_x[None],
        up(g_ada), g_b, g_n1, up(gr_win), g_qn, g_kn, pool4(gr_wp), g_ps, up(gr_wa), up(gr_wb), up(gr_wo), g_n2,
        up(gr_f1), up(gr_f2),
        up(d_ada), d_b, d_n1, up(d_win), d_qn, d_kn, pool4(d_wp), d_ps, up(d_wa), up(d_wb), up(d_wo), d_n2,
        up(d_f1), up(d_f2),
        up(nm_ada), nm_b, nm_n1, up(nm_win), nm_qn, nm_kn, pool4(nm_wp), nm_ps, up(nm_wa), up(nm_wb), up(nm_wo), nm_n2,
        up(nm_f1), up(nm_f2),
        up(nv_ada), nv_b, nv_n1, up(nv_win), nv_qn, nv_kn, pool4(nv_wp), nv_ps, up(nv_wa), up(nv_wb), up(nv_wo), nv_n2,
        up(nv_f1), up(nv_f2),
    )
```

```python
import functools
import math

import jax
import jax.numpy as jnp
from jax import lax
from jax.experimental import pallas as pl
from jax.experimental.pallas import tpu as pltpu

F32 = jnp.float32
BF16 = jnp.bfloat16
MESH = pl.DeviceIdType.MESH
ANY = pl.BlockSpec(memory_space=pl.ANY)

EPS = 1e-6
HEAD_DIM = 128
LANES, SUBLANES = 128, 8
POOL_WINDOWS = (2, 4, 8, 16)
N_GROUPS = len(POOL_WINDOWS)
assert POOL_WINDOWS == tuple(2 << g for g in range(N_GROUPS))
N_CHIPS = 4
N_DEV = 8
ADAM_LR, ADAM_B1, ADAM_B2, ADAM_EPS, ADAM_WD, ADAM_STEP = 0.001, 0.9, 0.999, 1e-08, 0.01, 10
VMEM_LIMIT_V7X = 56 * 1024 * 1024
ATT_T = 256
ATT_GROUP = 8
POOL_T = 256


def _pcall(body, **kw):
    return pl.pallas_call(body, **kw)


def _params(sem=None):
    return pltpu.CompilerParams(dimension_semantics=sem, vmem_limit_bytes=VMEM_LIMIT_V7X)


def _tile(n, pref):
    if n <= pref:
        return n
    t = pref
    while n % t:
        t //= 2
    return t


class _Rider:
    def __init__(self, arrays, out_shape, sems, start, finish, aliases=None, steps=()):
        self.arrays, self.out_shape, self.sems = list(arrays), list(out_shape), list(sems)
        self.start, self.finish, self.aliases, self.steps = start, finish, aliases or {}, list(steps)


def _ride(name, body, riders, arrays, *, grid, in_specs, out_specs, out_shape, scratch_shapes, sem, scalars=None):
    n_in, n_out, n_scr = len(arrays), len(out_shape), len(scratch_shapes)
    r_arrays = [a for r in riders for a in r.arrays]
    r_outs = [o for r in riders for o in r.out_shape]
    r_sems = [s for r in riders for s in r.sems]
    n_hooks = max([len(r.steps) for r in riders], default=0)
    total = math.prod(grid)
    aliases, off_i, off_o = {}, n_in + (scalars is not None), n_out
    for r in riders:
        for a, o in r.aliases.items():
            aliases[off_i + a] = off_o + o
        off_i += len(r.arrays)
        off_o += len(r.out_shape)

    def full(*refs):
        p = 0
        groups = []
        for n in (n_in, len(r_arrays), n_out, len(r_outs), n_scr, len(r_sems)):
            groups.append(refs[p:p + n])
            p += n
        ins, rin, outs, rout, scr, rsem = groups

        def each(what):
            a = o = s = 0
            for r in riders:
                fn = what(r)
                if fn is not None:
                    fn(rin[a:a + len(r.arrays)], rout[o:o + len(r.out_shape)], rsem[s:s + len(r.sems)])
                a, o, s = a + len(r.arrays), o + len(r.out_shape), s + len(r.sems)

        if riders:
            lin = 0
            for d, g in enumerate(grid):
                lin = lin * g + pl.program_id(d)
            pl.when(lin == 0)(lambda: each(lambda r: r.start))
            for t in range(n_hooks):
                pl.when(lin == min(total - 1, ((t + 1) * total) // n_hooks))(
                    lambda t=t: each(lambda r: r.steps[t] if t < len(r.steps) else None))
        body(*ins, *outs, *scr)
        if riders:
            pl.when(lin == total - 1)(lambda: each(lambda r: r.finish))

    specs = dict(grid=grid, in_specs=list(in_specs) + [ANY] * len(r_arrays),
                 out_specs=list(out_specs) + [ANY] * len(r_outs), scratch_shapes=list(scratch_shapes) + r_sems)
    common = dict(name=name, out_shape=list(out_shape) + r_outs, input_output_aliases=aliases,
                  compiler_params=_params(("arbitrary",) * len(grid) if riders else sem))
    if scalars is None:
        res = _pcall(full, **specs, **common)(*arrays, *r_arrays)
    else:
        res = _pcall(lambda _, *refs: full(*refs), **common,
                     grid_spec=pltpu.PrefetchScalarGridSpec(num_scalar_prefetch=1, **specs))(scalars, *arrays, *r_arrays)
    if not riders:
        return res
    main, rest, per = res[:n_out], res[n_out:], []
    for r in riders:
        per.append(rest[:len(r.out_shape)])
        rest = rest[len(r.out_shape):]
    return main, per


def _run_rider(name, rider):
    def body(*refs):
        n_a, n_o = len(rider.arrays), len(rider.out_shape)
        ins, outs, sems = refs[:n_a], refs[n_a:n_a + n_o], refs[n_a + n_o:]
        for fn in [rider.start] + rider.steps + [rider.finish]:
            fn(ins, outs, sems)

    return _pcall(body, name=name, out_shape=rider.out_shape, in_specs=[ANY] * len(rider.arrays),
                  out_specs=[ANY] * len(rider.out_shape), scratch_shapes=rider.sems,
                  input_output_aliases=rider.aliases)(*rider.arrays)


def _mm(name, pairs, *, M, N, K, ta=False, tb=False, tm=512, tn=1024, tk=1024,
        a_pro=None, b_pro=None, extras=(), outs, epi, riders=()):
    tm, tn, tk = _tile(M, tm), _tile(N, tn), _tile(K, tk)
    n_i, n_j, n_k = M // tm, N // tn, K // tk
    n_p, n_e = len(pairs), len(extras)
    arrays, in_specs = [], []
    for a, _ in pairs:
        arrays.append(a)
        in_specs.append(pl.BlockSpec((tk, tm), lambda i, j, k: (k, i)) if ta
                        else pl.BlockSpec((tm, tk), lambda i, j, k: (i, k)))
    for _, b in pairs:
        arrays.append(b)
        in_specs.append(pl.BlockSpec((tn, tk), lambda i, j, k: (j, k)) if tb
                        else pl.BlockSpec((tk, tn), lambda i, j, k: (k, j)))
    for arr, kind, off in extras:
        ob = off // tn
        assert off % tn == 0
        arrays.append(arr)
        if kind == "tile":
            in_specs.append(pl.BlockSpec((tm, tn), lambda i, j, k, ob=ob: (i, j + ob)))
        else:
            in_specs.append(pl.BlockSpec((1, tn), lambda i, j, k, ob=ob: (0, j + ob)))
    out_shape, out_specs = [], []
    for o in outs:
        if o["kind"] == "tile":
            out_shape.append(jax.ShapeDtypeStruct((M, N), o["dtype"]))
            out_specs.append(pl.BlockSpec((tm, tn), lambda i, j, k: (i, j)))
        else:
            out_shape.append(jax.ShapeDtypeStruct((n_i, 1, N), F32))
            out_specs.append(pl.BlockSpec((1, 1, tn), lambda i, j, k: (i, 0, j)))
    dims = (((0 if ta else 1,), (1 if tb else 0,)), ((), ()))

    def body(*refs):
        a_refs, b_refs = refs[:n_p], refs[n_p:2 * n_p]
        e_refs = refs[2 * n_p:2 * n_p + n_e]
        o_refs = refs[2 * n_p + n_e:2 * n_p + n_e + len(outs)]
        acc_refs = refs[2 * n_p + n_e + len(outs):]

        def product(p):
            a, b = a_refs[p][...], b_refs[p][...]
            if a_pro is not None:
                a = a_pro(a)
            if b_pro is not None:
                b = b_pro(b)
            return lax.dot_general(a, b, dims, preferred_element_type=F32)

        def write(accs):
            vals = epi(accs, [e[...] for e in e_refs])
            for o, o_ref, val in zip(outs, o_refs, vals):
                if o["kind"] == "tile":
                    o_ref[...] = val.astype(o_ref.dtype)
                else:
                    o_ref[0] = val

        if n_k == 1:
            write([product(p) for p in range(n_p)])
            return
        k = pl.program_id(2)

        @pl.when(k == 0)
        def _():
            for acc in acc_refs:
                acc[...] = jnp.zeros_like(acc)

        for p in range(n_p):
            acc_refs[p][...] += product(p)

        pl.when(k == n_k - 1)(lambda: write([acc[...] for acc in acc_refs]))

    return _ride(name, body, riders, arrays, grid=(n_i, n_j, n_k), in_specs=in_specs, out_specs=out_specs,
                 out_shape=out_shape, scratch_shapes=[pltpu.VMEM((tm, tn), F32) for _ in pairs] if n_k > 1 else [],
                 sem=("parallel", "parallel", "arbitrary"))


def _tile_out(dtype):
    return {"kind": "tile", "dtype": dtype}


_COLSUM = {"kind": "colsum"}


def _colsum(v):
    return jnp.sum(v, axis=0, keepdims=True)


def _norm_mod(name, x, norm_w, scale, shift):
    S, D = x.shape
    tr = _tile(S, 256)

    def body(x_ref, nw_ref, sc_ref, sh_ref, h_ref):
        xv = x_ref[...]
        r = lax.rsqrt(jnp.mean(xv * xv, axis=-1, keepdims=True) + EPS)
        h_ref[...] = ((xv * r * nw_ref[...]) * (1.0 + sc_ref[...]) + sh_ref[...]).astype(BF16)

    row = pl.BlockSpec((1, D), lambda i: (0, 0))
    til = pl.BlockSpec((tr, D), lambda i: (i, 0))
    return _pcall(body, name=name, grid=(S // tr,), in_specs=[til, row, row, row], out_specs=til,
                  out_shape=jax.ShapeDtypeStruct((S, D), BF16), compiler_params=_params(("parallel",)))(
                      x, norm_w, scale, shift)


def _norm_mod_bwd(name, dh, x, dres, norm_w, scale, gate_o=None):
    S, D = x.shape
    tr = _tile(S, 256)
    n_r = S // tr
    with_gate = gate_o is not None

    def body(*refs):
        if with_gate:
            dh_ref, x_ref, dres_ref, nw_ref, sc_ref, o_ref, g_ref, dx_ref, p1, p2, p3, do_ref, p4 = refs
        else:
            dh_ref, x_ref, dres_ref, nw_ref, sc_ref, dx_ref, p1, p2, p3 = refs
        dhv, xv, nw = dh_ref[...], x_ref[...], nw_ref[...]
        r = lax.rsqrt(jnp.mean(xv * xv, axis=-1, keepdims=True) + EPS)
        xh = xv * r
        p1[0] = _colsum(dhv)
        p2[0] = _colsum(dhv * (xh * nw))
        dn = dhv * (1.0 + sc_ref[...])
        p3[0] = _colsum(dn * xh)
        dxh = dn * nw
        dx = dres_ref[...] + r * (dxh - xh * jnp.mean(dxh * xh, axis=-1, keepdims=True))
        dx_ref[...] = dx
        if with_gate:
            do_ref[...] = (dx * g_ref[...]).astype(BF16)
            p4[0] = _colsum(dx * o_ref[...].astype(F32))

    row = pl.BlockSpec((1, D), lambda i: (0, 0))
    til = pl.BlockSpec((tr, D), lambda i: (i, 0))
    part = pl.BlockSpec((1, 1, D), lambda i: (i, 0, 0))
    part_shape = jax.ShapeDtypeStruct((n_r, 1, D), F32)
    in_specs = [til, til, til, row, row]
    arrays = [dh, x, dres, norm_w, scale]
    out_specs = [til, part, part, part]
    out_shape = [jax.ShapeDtypeStruct((S, D), F32), part_shape, part_shape, part_shape]
    if with_gate:
        in_specs += [til, row]
        arrays += list(gate_o)
        out_specs += [til, part]
        out_shape += [jax.ShapeDtypeStruct((S, D), BF16), part_shape]
    return _pcall(body, name=name, grid=(n_r,), in_specs=in_specs, out_specs=out_specs, out_shape=out_shape,
                  compiler_params=_params(("parallel",)))(*arrays)


def _pool_w_specs(rows, cg):
    return [pl.BlockSpec((rows, cg), lambda g, j=j: (N_GROUPS * j + g, 0)) for j in range(N_CHIPS)]


def _pool_fwd(proj, wp_full, pool_scale, S, PW):
    cg = PW // N_GROUPS
    rows = cg // N_CHIPS
    T = _tile(S, POOL_T)
    n_t = S // T

    def body(u_ref, w0, w1, w2, w3, ps_ref, pooled_ref, pa_ref):
        g = pl.program_id(0)
        win = jnp.left_shift(2, g)
        w = jnp.concatenate([w0[...], w1[...], w2[...], w3[...]], axis=0)
        t_i = lax.broadcasted_iota(jnp.int32, (T, T), 0)
        j_i = lax.broadcasted_iota(jnp.int32, (T, T), 1)
        b_cur = ((j_i <= t_i) & (j_i > t_i - win)).astype(BF16)
        b_prev = (j_i - T > t_i - win).astype(BF16)
        row = lax.broadcasted_iota(jnp.int32, (T, 1), 0)
        for r in range(n_t):
            cur = u_ref[r * T:(r + 1) * T, :]
            ws = jnp.dot(b_cur, cur, preferred_element_type=F32)
            if r > 0:
                ws += jnp.dot(b_prev, u_ref[(r - 1) * T:r * T, :], preferred_element_type=F32)
            count = jnp.minimum(row + (r * T + 1), win).astype(F32)
            pooled = (ws / count - cur.astype(F32)).astype(BF16)
            pooled_ref[r * T:(r + 1) * T, :] = pooled
            mixed = jnp.dot(pooled, w, preferred_element_type=F32)
            pa_ref[r * T:(r + 1) * T, :] = (mixed * ps_ref[...]).astype(BF16)

    col = pl.BlockSpec((S, cg), lambda g: (0, g))
    return _pcall(
        body, name="pool_fwd", grid=(N_GROUPS,),
        in_specs=[col] + _pool_w_specs(rows, cg) + [pl.BlockSpec((1, cg), lambda g: (0, g))],
        out_specs=[col, col],
        out_shape=[jax.ShapeDtypeStruct((S, PW), BF16), jax.ShapeDtypeStruct((S, PW), BF16)],
        compiler_params=_params(("parallel",)),
    )(proj, wp_full, wp_full, wp_full, wp_full, pool_scale)


def _pool_bwd(dpa, pooled, wp_full, pool_scale, S, PW):
    cg = PW // N_GROUPS
    rows = cg // N_CHIPS
    T = _tile(S, POOL_T)
    n_t = S // T

    def body(dpa_ref, pooled_ref, w0, w1, w2, w3, ps_ref, du_ref, gw_ref, gs_ref, dp_s, dpc_s, dmx_s):
        g = pl.program_id(0)
        win = jnp.left_shift(2, g)
        w = jnp.concatenate([w0[...], w1[...], w2[...], w3[...]], axis=0)
        row = lax.broadcasted_iota(jnp.int32, (T, 1), 0)
        gs = jnp.zeros((1, cg), F32)
        for r in range(n_t):
            sl = slice(r * T, (r + 1) * T)
            mixed = jnp.dot(pooled_ref[sl, :], w, preferred_element_type=F32)
            dpa_t = dpa_ref[sl, :]
            gs += _colsum(dpa_t * mixed)
            dmx = (dpa_t * ps_ref[...]).astype(BF16)
            dmx_s[sl, :] = dmx
            dpo = lax.dot_general(dmx, w, (((1,), (1,)), ((), ())), preferred_element_type=F32)
            dp_s[sl, :] = dpo
            count = jnp.minimum(row + (r * T + 1), win).astype(F32)
            dpc_s[sl, :] = (dpo / count).astype(BF16)
        gs_ref[...] = gs
        gw = lax.dot_general(pooled_ref[...], dmx_s[...], (((0,), (0,)), ((), ())), preferred_element_type=F32)
        for j in range(N_CHIPS):
            gw_ref[j, 0] = gw[j * rows:(j + 1) * rows, :].astype(BF16)
        j_i = lax.broadcasted_iota(jnp.int32, (T, T), 0)
        t_i = lax.broadcasted_iota(jnp.int32, (T, T), 1)
        b_cur = ((t_i >= j_i) & (t_i < j_i + win)).astype(BF16)
        b_next = (t_i + T < j_i + win).astype(BF16)
        for r in range(n_t):
            sl = slice(r * T, (r + 1) * T)
            acc = jnp.dot(b_cur, dpc_s[sl, :], preferred_element_type=F32)
            if r + 1 < n_t:
                acc += jnp.dot(b_next, dpc_s[(r + 1) * T:(r + 2) * T, :], preferred_element_type=F32)
            du_ref[sl, :] = (acc - dp_s[sl, :]).astype(BF16)

    col = pl.BlockSpec((S, cg), lambda g: (0, g))
    return _pcall(
        body, name="pool_bwd", grid=(N_GROUPS,),
        in_specs=[col, col] + _pool_w_specs(rows, cg) + [pl.BlockSpec((1, cg), lambda g: (0, g))],
        out_specs=[col, pl.BlockSpec((N_CHIPS, 1, rows, cg), lambda g: (0, g, 0, 0)),
                   pl.BlockSpec((1, cg), lambda g: (0, g))],
        out_shape=[jax.ShapeDtypeStruct((S, PW), BF16),
                   jax.ShapeDtypeStruct((N_CHIPS, N_GROUPS, rows, cg), BF16),
                   jax.ShapeDtypeStruct((1, PW), F32)],
        scratch_shapes=[pltpu.VMEM((S, cg), F32), pltpu.VMEM((S, cg), BF16), pltpu.VMEM((S, cg), BF16)],
        compiler_params=_params(("parallel",)),
    )(dpa, pooled, wp_full, wp_full, wp_full, wp_full, pool_scale)


_NT = (((1,), (1,)), ((), ()))
_TN = (((0,), (0,)), ((), ()))


def _split_dot(v, tri):
    hi = v.astype(BF16)
    lo = (v - hi.astype(F32)).astype(BF16)
    return jnp.dot(hi, tri, preferred_element_type=F32) + jnp.dot(lo, tri, preferred_element_type=F32)


LOG2E = 1.4426950408889634
QK_SCALE = 1.0 / math.sqrt(HEAD_DIM)


def _sb_scores(q2_i, k_j, tri_l, masked):
    tq, tk = q2_i.shape[0], k_j.shape[0]
    s = lax.dot_general(q2_i, k_j, _NT, preferred_element_type=F32)
    lp = jnp.log(1.0 + jnp.exp2(-jnp.abs(s))) * LOG2E
    lb = jnp.minimum(s, 0.0) - lp
    l = lb - s
    mask = None
    if masked:
        mask = lax.broadcasted_iota(jnp.int32, (tq, tk), 0) > lax.broadcasted_iota(jnp.int32, (tq, tk), 1)
        l = jnp.where(mask, l, 0.0)
    return l, lb, lb + _split_dot(l, tri_l), mask


def _sb_weights(t, carry_l, mask):
    a = jnp.exp2(t + carry_l)
    return a if mask is None else jnp.where(mask, a, 0.0)


def _rowsum(v):
    return jnp.sum(v, axis=1, keepdims=True)


def _qk_norm(x_ref, w_ref):
    xv = x_ref[...].astype(F32)
    r = lax.rsqrt(jnp.mean(xv * xv, axis=-1, keepdims=True) + EPS)
    return xv * r, r


def _attn_fwd(proj, q_norm_w, k_norm_w, S, H, q_off, riders=()):
    t = _tile(S, ATT_T)
    n_q = S // t

    def body(q_ref, k_ref, v_ref, qw_ref, kw_ref, att_ref, attf_ref, qn_s, kn_s):
        qh, _ = _qk_norm(q_ref, qw_ref)
        qn_s[...] = (qh * qw_ref[...] * (QK_SCALE * LOG2E)).astype(BF16)
        kh, _ = _qk_norm(k_ref, kw_ref)
        kn_s[...] = (kh * kw_ref[...]).astype(BF16)
        tri_l = (lax.broadcasted_iota(jnp.int32, (t, t), 0) > lax.broadcasted_iota(jnp.int32, (t, t), 1)).astype(BF16)

        def rows(j):
            return pl.ds(pl.multiple_of(j * t, t), t)

        def q_step(i, _):
            q_i = qn_s[rows(i), :]

            def av(a, j):
                return jnp.dot(a.astype(BF16), v_ref[rows(j), :], preferred_element_type=F32)

            l, _, tt, mask = _sb_scores(q_i, kn_s[rows(i), :], tri_l, True)
            acc = av(_sb_weights(tt, 0.0, mask), i)
            carry = _rowsum(l)

            def single(_, c):
                carry, acc = c
                l, _, tt, _ = _sb_scores(q_i, kn_s[rows(i - 1), :], tri_l, False)
                return carry + _rowsum(l), acc + av(_sb_weights(tt, carry, None), i - 1)

            carry, acc = lax.fori_loop(0, i % 2, single, (carry, acc))
            top = i - 1 - i % 2

            def pair(p, c):
                carry, acc = c
                j0 = top - 2 * p
                l0, _, t0, _ = _sb_scores(q_i, kn_s[rows(j0), :], tri_l, False)
                l1, _, t1, _ = _sb_scores(q_i, kn_s[rows(j0 - 1), :], tri_l, False)
                mid = carry + _rowsum(l0)
                acc = acc + av(_sb_weights(t0, carry, None), j0) + av(_sb_weights(t1, mid, None), j0 - 1)
                return mid + _rowsum(l1), acc

            _, acc = lax.fori_loop(0, i // 2, pair, (carry, acc))
            att_ref[rows(i), :] = acc.astype(BF16)
            attf_ref[rows(i), :] = acc
            return 0

        lax.fori_loop(0, n_q, q_step, 0)

    def col(off):
        return pl.BlockSpec((S, HEAD_DIM), lambda h, off=off: (0, off + h))

    wspec = pl.BlockSpec((1, HEAD_DIM), lambda h: (0, 0))
    return _ride(
        "attn_fwd", body, riders, [proj, proj, proj, q_norm_w, k_norm_w], grid=(H,),
        in_specs=[col(q_off), col(q_off + H), col(q_off + 2 * H), wspec, wspec],
        out_specs=[col(0), col(0)],
        out_shape=[jax.ShapeDtypeStruct((S, H * HEAD_DIM), BF16), jax.ShapeDtypeStruct((S, H * HEAD_DIM), F32)],
        scratch_shapes=[pltpu.VMEM((S, HEAD_DIM), BF16), pltpu.VMEM((S, HEAD_DIM), BF16)],
        sem=("parallel",))


def _attn_bwd(proj, datt, attf, q_norm_w, k_norm_w, S, H, q_off, riders=()):
    t = _tile(S, ATT_T)
    n_q = S // t

    def body(q_ref, k_ref, v_ref, do_ref, o_ref, qw_ref, kw_ref, dq_ref, dk_ref, dv_ref, gq_ref, gk_ref,
             qn_s, kn_s, qz_s, kz_s, dk_s, dv_s, gq_s):
        qw, kw = qw_ref[...], kw_ref[...]
        qh, _ = _qk_norm(q_ref, qw_ref)
        qn_s[...] = (qh * qw * (QK_SCALE * LOG2E)).astype(BF16)
        qz_s[...] = (qh * qw * QK_SCALE).astype(BF16)
        kh, _ = _qk_norm(k_ref, kw_ref)
        kn_s[...] = (kh * kw).astype(BF16)
        kz_s[...] = (kh * kw * QK_SCALE).astype(BF16)
        dk_s[...] = jnp.zeros_like(dk_s)
        dv_s[...] = jnp.zeros_like(dv_s)
        gq_s[...] = jnp.zeros_like(gq_s)
        r_i = lax.broadcasted_iota(jnp.int32, (t, t), 0)
        c_i = lax.broadcasted_iota(jnp.int32, (t, t), 1)
        tri_l = (r_i > c_i).astype(BF16)
        tri_e = (r_i >= c_i).astype(BF16)

        def rows(j):
            return pl.ds(pl.multiple_of(j * t, t), t)

        def q_step(i, _):
            q_i = qn_s[rows(i), :]
            do_i = do_ref[rows(i), :]
            d_i = _rowsum(do_i.astype(F32) * o_ref[rows(i), :])

            def scores(j, masked):
                l, lb, tt, mask = _sb_scores(q_i, kn_s[rows(j), :], tri_l, masked)
                da = lax.dot_general(do_i, v_ref[rows(j), :], _NT, preferred_element_type=F32)
                return l, lb, tt, mask, da

            def grads(j, sc, carry_l, carry_e, dq_acc):
                l, lb, tt, mask, da = sc
                a_bf = _sb_weights(tt, carry_l, mask).astype(BF16)
                e = da * a_bf.astype(F32)
                p = (d_i - carry_e) - _split_dot(e, tri_e)
                dz = e - jnp.exp2(lb) * (e + p)
                if mask is not None:
                    dz = jnp.where(mask, dz, 0.0)
                dz = dz.astype(BF16)
                dk_s[rows(j), :] += lax.dot_general(dz, qz_s[rows(i), :], _TN, preferred_element_type=F32)
                dv_s[rows(j), :] += lax.dot_general(a_bf, do_i, _TN, preferred_element_type=F32)
                return (carry_l + _rowsum(l), carry_e + _rowsum(e),
                        dq_acc + jnp.dot(dz, kz_s[rows(j), :], preferred_element_type=F32))

            zero = jnp.zeros((t, 1), F32)
            first = (zero, zero, jnp.zeros((t, HEAD_DIM), F32))

            def group(js, diagonal_first, c):
                scs = [scores(j, diagonal_first and n == 0) for n, j in enumerate(js)]
                for j, sc in zip(js, scs):
                    c = grads(j, sc, *c)
                return c

            n_first = i % ATT_GROUP
            c = lax.switch(n_first, [functools.partial(group, [i - u for u in range(n + 1)], True, first)
                                     for n in range(ATT_GROUP)])
            top = i - 1 - n_first

            def whole(p, c):
                j0 = top - ATT_GROUP * p
                return group([j0 - u for u in range(ATT_GROUP)], False, c)

            _, _, dqn = lax.fori_loop(0, (i - n_first) // ATT_GROUP, whole, c)
            qv = q_ref[rows(i), :].astype(F32)
            r = lax.rsqrt(jnp.mean(qv * qv, axis=-1, keepdims=True) + EPS)
            xh = qv * r
            gq_s[...] += _colsum(dqn * xh)
            dxh = dqn * qw
            dq_ref[rows(i), :] = (r * (dxh - xh * jnp.mean(dxh * xh, axis=-1, keepdims=True))).astype(BF16)
            return 0

        lax.fori_loop(0, n_q, q_step, 0)
        gq_ref[0] = gq_s[...]
        kh, rk = _qk_norm(k_ref, kw_ref)
        dkn = dk_s[...]
        gk_ref[0] = _colsum(dkn * kh)
        dxh = dkn * kw
        dk_ref[...] = (rk * (dxh - kh * jnp.mean(dxh * kh, axis=-1, keepdims=True))).astype(BF16)
        dv_ref[...] = dv_s[...].astype(BF16)

    def col(off):
        return pl.BlockSpec((S, HEAD_DIM), lambda h, off=off: (0, off + h))

    wspec = pl.BlockSpec((1, HEAD_DIM), lambda h: (0, 0))
    gspec = pl.BlockSpec((1, 1, HEAD_DIM), lambda h: (h, 0, 0))
    act = jax.ShapeDtypeStruct((S, H * HEAD_DIM), BF16)
    gsh = jax.ShapeDtypeStruct((H, 1, HEAD_DIM), F32)
    return _ride(
        "attn_bwd", body, riders, [proj, proj, proj, datt, attf, q_norm_w, k_norm_w], grid=(H,),
        in_specs=[col(q_off), col(q_off + H), col(q_off + 2 * H), col(0), col(0), wspec, wspec],
        out_specs=[col(0), col(0), col(0), gspec, gspec],
        out_shape=[act, act, act, gsh, gsh],
        scratch_shapes=[pltpu.VMEM((S, HEAD_DIM), BF16)] * 4 + [pltpu.VMEM((S, HEAD_DIM), F32)] * 2
        + [pltpu.VMEM((1, HEAD_DIM), F32)],
        sem=("parallel",))


def _place():
    x, y, c = lax.axis_index("x"), lax.axis_index("y"), lax.axis_index("c")
    chips = [(1 - x, y), (x, 1 - y), (1 - x, 1 - y)]
    return x, y, c, chips


def _dev_allgather(name, v):
    m_per, n = v.shape

    def body(x_ref, out_ref, send_sems, recv_sems, local_sem):
        x, y, c, _ = _place()
        me = (x, y, c)

        def rows(px, py, pc):
            return out_ref.at[pl.ds((4 * px + 2 * py + pc) * m_per, m_per), :]

        def peer(r):
            return tuple(1 - b if (r >> s) & 1 else b for b, s in zip(me, (2, 1, 0)))

        def copy(r, block, to, src=None):
            return pltpu.make_async_remote_copy(
                src_ref=rows(*block) if src is None else src, dst_ref=rows(*block),
                send_sem=send_sems.at[r - 1], recv_sem=recv_sems.at[r - 1], device_id=to, device_id_type=MESH)

        mine = pltpu.make_async_copy(x_ref, rows(*me), local_sem)
        mine.start()
        sends = [copy(r, me, peer(r), src=x_ref) for r in range(1, N_DEV)]
        for cp in sends:
            cp.start()
        for r in range(1, N_DEV):
            copy(r, peer(r), me).wait_recv()
        for cp in sends:
            cp.wait_send()
        mine.wait()

    return _pcall(
        body, name=name, out_shape=jax.ShapeDtypeStruct((N_DEV * m_per, n), v.dtype),
        in_specs=[pl.BlockSpec(memory_space=pltpu.VMEM)], out_specs=pl.BlockSpec(memory_space=pltpu.VMEM),
        scratch_shapes=[pltpu.SemaphoreType.DMA((7,)), pltpu.SemaphoreType.DMA((7,)), pltpu.SemaphoreType.DMA],
        compiler_params=pltpu.CompilerParams(vmem_limit_bytes=VMEM_LIMIT_V7X),
    )(v)


class _W:
    def __init__(self, name, kind, R, C):
        self.name, self.kind, self.R, self.C = name, kind, R, C

    @property
    def shard_shape(self):
        return (self.R, self.C // N_CHIPS) if self.kind == "col" else (self.R // N_CHIPS, self.C)

    @property
    def half_rows(self):
        return self.shard_shape[0] // 2

    def shard_half(self, ref, half):
        return ref.at[pl.ds(half * self.half_rows, self.half_rows), :]

    def region(self, full_ref, chip, half):
        hr = self.half_rows
        if self.kind == "col":
            cw = self.C // N_CHIPS
            return full_ref.at[pl.ds(half * hr, hr), pl.ds(chip * cw, cw)]
        return full_ref.at[pl.ds(chip * (2 * hr) + half * hr, hr), :]


def _ag_rider(ws, fulls, n_ch=4, chunks=None):
    n_w = len(ws)
    lo, hi = chunks or (0, n_ch)
    per = 6

    def parts(full, sems):
        send_sems, recv_sems = sems
        x, y, c, _ = _place()
        xn, yn, dg = (1 - x, y), (x, 1 - y), (1 - x, 1 - y)
        via = (x + (1 - c) * (1 - 2 * x), y + c * (1 - 2 * y))
        to = (x + c * (1 - 2 * x), y + (1 - c) * (1 - 2 * y))

        def reg(i, chip, half, t):
            nr = ws[i].half_rows // n_ch
            return ws[i].region(full[i], 2 * chip[0] + chip[1], half).at[pl.ds(t * nr, nr), :]

        def copy(r, i, t, k, dev):
            s = (i * (hi - lo) + t - lo) * per + k
            return pltpu.make_async_remote_copy(src_ref=r, dst_ref=r, send_sem=send_sems.at[s],
                                                recv_sem=recv_sems.at[s], device_id=dev, device_id_type=MESH)

        def direct(i, t, k):
            return copy(reg(i, (x, y), c, t), i, t, k, (*(via, to)[k], c))

        def direct_in(i, t, k):
            return copy(reg(i, (via, to)[k], c, t), i, t, k, (*(via, to)[k], c))

        def relay(i, t):
            return copy(reg(i, via, c, t), i, t, 2, (*to, c))

        def relay_in(i, t):
            return copy(reg(i, dg, c, t), i, t, 2, (*to, c))

        def hand(i, t, k, half):
            return copy(reg(i, (xn, yn, dg)[k], half, t), i, t, 3 + k, (x, y, 1 - c))

        return c, direct, direct_in, relay, relay_in, hand

    def start(_, full, sems):
        _, direct, _, _, _, _ = parts(full, sems)
        for t in range(lo, hi):
            for i in range(n_w):
                direct(i, t, 0).start()
                direct(i, t, 1).start()

    def arrived(t):
        def step(_, full, sems):
            c, _, direct_in, relay, relay_in, hand = parts(full, sems)
            for i in range(n_w):
                direct_in(i, t, 0).wait_recv()
                direct_in(i, t, 1).wait_recv()
                relay(i, t).start()
                hand(i, t, 0, c).start()
                hand(i, t, 1, c).start()
        return step

    def finish(_, full, sems):
        c, direct, _, relay, relay_in, hand = parts(full, sems)
        for t in range(lo, hi):
            for i in range(n_w):
                relay_in(i, t).wait_recv()
                hand(i, t, 2, c).start()
        for i in range(n_w):
            for t in range(lo, hi):
                for k in range(3):
                    hand(i, t, k, 1 - c).wait_recv()
        for i in range(n_w):
            for t in range(lo, hi):
                direct(i, t, 0).wait_send()
                direct(i, t, 1).wait_send()
                relay(i, t).wait_send()
                for k in range(3):
                    hand(i, t, k, c).wait_send()

    n_sem = per * (hi - lo) * n_w
    return _Rider(fulls, [jax.ShapeDtypeStruct((w.R, w.C), BF16) for w in ws],
                  [pltpu.SemaphoreType.DMA((n_sem,)), pltpu.SemaphoreType.DMA((n_sem,))], start, finish,
                  steps=[arrived(t) for t in range(lo, hi)], aliases={i: i for i in range(n_w)})


def _cast_into_full(ws, shards, chip_arr, riders=()):
    sr, sc = ws[0].shard_shape
    assert all(w.shard_shape == (sr, sc) for w in ws)
    tr, tc = _tile(sr, 512), _tile(sc, 2048)
    n_r, n_c = sr // tr, sc // tc

    def place(w):
        if w.kind == "col":
            return pl.BlockSpec((tr, tc), lambda i, j, chip: (i, chip[0] * n_c + j))
        return pl.BlockSpec((tr, tc), lambda i, j, chip: (chip[0] * n_r + i, j))

    def body(*refs):
        for a_ref, o_ref in zip(refs[:len(ws)], refs[len(ws):]):
            o_ref[...] = a_ref[...].astype(BF16)

    return _ride("cast_" + "_".join(w.name for w in ws), body, riders, list(shards), grid=(n_r, n_c),
                 in_specs=[pl.BlockSpec((tr, tc), lambda i, j, chip: (i, j))] * len(ws),
                 out_specs=[place(w) for w in ws], out_shape=[jax.ShapeDtypeStruct((w.R, w.C), BF16) for w in ws],
                 scratch_shapes=[], sem=("parallel", "parallel"), scalars=chip_arr)


def _half_view(w, g):
    return g if w.kind == "col" else g.reshape(N_CHIPS, w.R // N_CHIPS, w.C)


def _px_rider(ws, grads):
    n_w = len(ws)

    def copies(g, got, sems):
        send_sems, recv_sems = sems
        x, y, c, _ = _place()

        def half_all(w, ref, half):
            hr = w.half_rows
            if w.kind == "col":
                return ref.at[pl.ds(half * hr, hr), :]
            return ref.at[:, pl.ds(half * hr, hr), :]

        return [pltpu.make_async_remote_copy(
            src_ref=half_all(w, g[i], 1 - c), dst_ref=got[i], send_sem=send_sems.at[i], recv_sem=recv_sems.at[i],
            device_id=(x, y, 1 - c), device_id_type=MESH) for i, w in enumerate(ws)]

    def start(g, got, sems):
        for cp in copies(g, got, sems):
            cp.start()

    def finish(g, got, sems):
        for cp in copies(g, got, sems):
            cp.wait_recv()
            cp.wait_send()

    def got_shape(w):
        hr = w.half_rows
        return (hr, w.C) if w.kind == "col" else (N_CHIPS, hr, w.C)

    return _Rider([_half_view(w, g) for w, g in zip(ws, grads)],
                  [jax.ShapeDtypeStruct(got_shape(w), BF16) for w in ws],
                  [pltpu.SemaphoreType.DMA((n_w,)), pltpu.SemaphoreType.DMA((n_w,))], start, finish)


def _pair_sum(w, g, got, c_arr):
    hr = w.half_rows
    if w.kind == "col":
        tr, tc = _tile(hr, 512), _tile(w.C, 2048)
        n_r = hr // tr
        grid = (n_r, w.C // tc)
        g_spec = pl.BlockSpec((tr, tc), lambda i, j, c: (c[0] * n_r + i, j))
        o_spec = pl.BlockSpec((tr, tc), lambda i, j, c: (i, j))
    else:
        tr = _tile(hr, 512)
        n_r = hr // tr
        grid = (N_CHIPS, n_r)
        g_spec = pl.BlockSpec((1, tr, w.C), lambda s, i, c: (s, c[0] * n_r + i, 0))
        o_spec = pl.BlockSpec((1, tr, w.C), lambda s, i, c: (s, i, 0))

    def body(c_ref, g_ref, got_ref, out_ref):
        out_ref[...] = (g_ref[...].astype(F32) + got_ref[...].astype(F32)).astype(BF16)

    return _pcall(
        body, name="grad_pair_sum_" + w.name, out_shape=jax.ShapeDtypeStruct(got.shape, BF16),
        grid_spec=pltpu.PrefetchScalarGridSpec(num_scalar_prefetch=1, grid=grid, in_specs=[g_spec, o_spec],
                                               out_specs=o_spec),
        compiler_params=_params(("parallel", "parallel")),
    )(c_arr, _half_view(w, g), got)


def _chip_sum(w, p, q, cc_arr):
    hr, cols = w.half_rows, w.shard_shape[1]
    tr, tc = _tile(hr, 512), _tile(cols, 2048)
    n_r, n_c = hr // tr, cols // tc

    def body(cc_ref, own, q1, q2, q3, out_ref):
        own_v = own[...] if w.kind == "col" else own[0]
        out_ref[...] = ((own_v.astype(F32) + q1[0].astype(F32)) + q2[0].astype(F32)) + q3[0].astype(F32)

    if w.kind == "col":
        own_spec = pl.BlockSpec((tr, tc), lambda i, j, cc: (i, cc[1] * n_c + j))
    else:
        own_spec = pl.BlockSpec((1, tr, tc), lambda i, j, cc: (cc[1], i, j))
    q_specs = [pl.BlockSpec((1, tr, tc), lambda i, j, cc, s=s: ((cc[1] + s) % N_CHIPS, i, j)) for s in (1, 2, 3)]
    return _pcall(
        body, name="grad_chip_sum_" + w.name, out_shape=jax.ShapeDtypeStruct(w.shard_shape, F32),
        grid_spec=pltpu.PrefetchScalarGridSpec(
            num_scalar_prefetch=1, grid=(n_r, n_c), in_specs=[own_spec] + q_specs,
            out_specs=pl.BlockSpec((tr, tc), lambda i, j, cc: (cc[0] * n_r + i, j))),
        compiler_params=_params(("parallel", "parallel")),
    )(cc_arr, p, q, q, q)


_SEM = pl.BlockSpec(memory_space=pltpu.SEMAPHORE)
_HBM = pl.BlockSpec(memory_space=pltpu.HBM)


def _split_copies(kind, ws, p, land, send_sems, recv_sems):
    x, y, c, chips = _place()
    my_chip = 2 * x + y
    pairs = []
    for i, w in enumerate(ws):
        if kind == "pair":
            hr = w.half_rows
            src = p[i].at[pl.ds((1 - c) * hr, hr), :] if w.kind == "col" else p[i].at[:, pl.ds((1 - c) * hr, hr), :]
            cp = pltpu.make_async_remote_copy(src_ref=src, dst_ref=land[i], send_sem=send_sems.at[i],
                                              recv_sem=recv_sems.at[i], device_id=(x, y, 1 - c), device_id_type=MESH)
            pairs.append((cp, cp))
            continue
        for k, chip in enumerate(chips):
            to_chip = 2 * chip[0] + chip[1]
            src = p[i].at[:, pl.ds(to_chip * (w.C // N_CHIPS), w.C // N_CHIPS)] if w.kind == "col" else p[i].at[to_chip]
            kw = dict(send_sem=send_sems.at[3 * i + k], recv_sem=recv_sems.at[3 * i + k], device_id=(*chip, c),
                      device_id_type=MESH)
            pairs.append((pltpu.make_async_remote_copy(src_ref=src, dst_ref=land[i].at[my_chip], **kw),
                          pltpu.make_async_remote_copy(src_ref=src, dst_ref=land[i].at[to_chip], **kw)))
    return pairs


def _split_start(name, kind, ws, arrays):
    n_w = len(ws)
    if kind == "pair":
        arrays = [_half_view(w, g) for w, g in zip(ws, arrays)]
        lands = [lax.empty((w.half_rows, w.C) if w.kind == "col" else (N_CHIPS, w.half_rows, w.C), BF16) for w in ws]
    else:
        lands = [lax.empty((N_CHIPS, w.half_rows, w.shard_shape[1]), BF16) for w in ws]
    n_sem = n_w if kind == "pair" else 3 * n_w

    def body(*refs):
        p, land = refs[:n_w], refs[n_w:2 * n_w]
        for out, _ in _split_copies(kind, ws, p, land, refs[2 * n_w], refs[2 * n_w + 1]):
            out.start()
        refs[-1][...] = jnp.zeros_like(refs[-1])

    arrays = [pltpu.with_memory_space_constraint(a, pltpu.HBM) for a in list(arrays) + lands]
    res = _pcall(
        body, name=name,
        out_shape=(pltpu.SemaphoreType.DMA((n_sem,)), pltpu.SemaphoreType.DMA((n_sem,)),
                   *[pltpu.HBM(a.shape, a.dtype) for a in arrays], jax.ShapeDtypeStruct((SUBLANES, LANES), F32)),
        in_specs=[_HBM] * (2 * n_w),
        out_specs=(_SEM, _SEM, *[_HBM] * (2 * n_w), pl.BlockSpec(memory_space=pltpu.VMEM)),
        input_output_aliases={i: 2 + i for i in range(2 * n_w)},
        compiler_params=pltpu.CompilerParams(has_side_effects=pltpu.SideEffectType.DATAFLOW_SIDE_EFFECTING),
    )(*arrays)
    return (kind, ws, res[0], res[1], list(res[2:2 + n_w]), list(res[2 + n_w:2 + 2 * n_w])), res[-1]


def _split_wait(name, flight, after):
    kind, ws, send_sems, recv_sems, arrays, lands = flight
    n_w = len(ws)

    def body(*refs):
        p, land = refs[:n_w], refs[n_w:2 * n_w]
        for _, cp in _split_copies(kind, ws, p, land, refs[2 * n_w], refs[2 * n_w + 1]):
            cp.wait_send()
            cp.wait_recv()

    res = _pcall(
        body, name=name,
        out_shape=[pltpu.HBM(a.shape, a.dtype) for a in list(arrays) + list(lands)],
        in_specs=[_HBM] * (2 * n_w) + [_SEM, _SEM] + [ANY] * len(after), out_specs=[_HBM] * (2 * n_w),
        input_output_aliases={i: i for i in range(2 * n_w)},
        compiler_params=pltpu.CompilerParams(has_side_effects=pltpu.SideEffectType.DATAFLOW_SIDE_EFFECTING),
    )(*arrays, *lands, send_sems, recv_sems, *after)
    return list(res[:n_w]), list(res[n_w:])


def _sf_rider(ws, grads):
    n_w = len(ws)

    def copy(g, sems, i, half):
        send_sems, recv_sems = sems
        x, y, c, _ = _place()
        h = c if half == "mine" else 1 - c
        reg = ws[i].shard_half(g[i], h)
        return pltpu.make_async_remote_copy(src_ref=reg, dst_ref=reg, send_sem=send_sems.at[i], recv_sem=recv_sems.at[i],
                                            device_id=(x, y, 1 - c), device_id_type=MESH)

    def start(_, g, sems):
        for i in range(n_w):
            copy(g, sems, i, "mine").start()

    def finish(_, g, sems):
        for i in range(n_w):
            copy(g, sems, i, "other").wait_recv()
            copy(g, sems, i, "mine").wait_send()

    return _Rider(grads, [jax.ShapeDtypeStruct(w.shard_shape, F32) for w in ws],
                  [pltpu.SemaphoreType.DMA((n_w,)), pltpu.SemaphoreType.DMA((n_w,))], start, finish,
                  aliases={i: i for i in range(n_w)})


def _adamw_math(w, g, m, v):
    m = ADAM_B1 * m + (1.0 - ADAM_B1) * g
    v = ADAM_B2 * v + (1.0 - ADAM_B2) * (g * g)
    m_hat = m / (1.0 - ADAM_B1 ** ADAM_STEP)
    v_hat = v / (1.0 - ADAM_B2 ** ADAM_STEP)
    delta = -ADAM_LR * (m_hat / (jnp.sqrt(v_hat) + ADAM_EPS) + ADAM_WD * w)
    return delta, m, v


def _adamw(name, w, g, m, v, after=None):
    R, C = w.shape
    tr, tc = _tile(R, 256), _tile(C, 2048)
    behind = [] if after is None else [after]

    def body(w_ref, g_ref, m_ref, v_ref, *rest):
        g_out, d_out, m_out, v_out = rest[len(behind):]
        g = g_ref[...]
        g_out[...] = g
        d_out[...], m_out[...], v_out[...] = _adamw_math(w_ref[...], g, m_ref[...], v_ref[...])

    spec = pl.BlockSpec((tr, tc), lambda i, j: (i, j))
    sh = jax.ShapeDtypeStruct((R, C), F32)
    return _pcall(body, name=name, grid=(R // tr, C // tc), in_specs=[spec] * 4 + [ANY] * len(behind),
                  out_specs=[spec] * 4, out_shape=[sh] * 4, compiler_params=_params(("parallel", "parallel")))(
                      w, g, m, v, *behind)


def _ada_update(sct, dmod_sh, w, m, v, riders=()):
    R, C = w.shape
    tr, tc = _tile(R, 512), _tile(C, 1024)

    def body(s_ref, d_ref, w_ref, m_ref, v_ref, g_out, d_out, m_out, v_out):
        s, d = s_ref[...], d_ref[...]
        g = s[:, 0:1] * d[0:1, :]
        for b in range(1, N_DEV):
            g += s[:, b:b + 1] * d[b:b + 1, :]
        g_out[...] = g
        d_out[...], m_out[...], v_out[...] = _adamw_math(w_ref[...], g, m_ref[...], v_ref[...])

    spec = pl.BlockSpec((tr, tc), lambda i, j: (i, j))
    sh = jax.ShapeDtypeStruct((R, C), F32)
    return _ride(
        "ada_update", body, riders, [sct, dmod_sh, w, m, v], grid=(R // tr, C // tc),
        in_specs=[pl.BlockSpec((tr, N_DEV), lambda i, j: (i, 0)), pl.BlockSpec((N_DEV, tc), lambda i, j: (0, j)),
                  spec, spec, spec],
        out_specs=[spec] * 4, out_shape=[sh] * 4, scratch_shapes=[], sem=("parallel", "parallel"))


def _silu_rows(c_row):
    D = c_row.shape[1]

    def body(c_ref, o_ref):
        cv = c_ref[...]
        o_ref[...] = cv * jax.nn.sigmoid(cv)

    return _pcall(body, name="silu_c", out_shape=jax.ShapeDtypeStruct((1, D), F32))(c_row)


def _pack_partials(parts, widths, total):
    n = len(widths)

    def body(*refs):
        loss_p, out_ref = refs[n], refs[n + 1]
        off = 0
        for ref, wd in zip(refs[:n], widths):
            out_ref[:, off:off + wd] = jnp.sum(ref[...], axis=0)
            off += wd
        loss = jnp.sum(jnp.sum(loss_p[...], axis=0), axis=1, keepdims=True)
        out_ref[:, off:off + LANES] = jnp.broadcast_to(loss, (1, LANES))
        if off + LANES < total:
            out_ref[:, off + LANES:total] = jnp.zeros((1, total - off - LANES), F32)

    return _pcall(body, name="pack_partials", out_shape=jax.ShapeDtypeStruct((1, total), F32))(*parts)


def _small_update(gathered, offsets, params, loss_off):
    n_p = len(params)

    def over_devices(g_ref, off, wd):
        blk = g_ref[:, off:off + wd]
        g = blk[0:1, :]
        for b in range(1, N_DEV):
            g = g + blk[b:b + 1, :]
        return g

    def body(*refs):
        g_ref = refs[0]
        prm = refs[1:1 + 3 * n_p]
        outs = refs[1 + 3 * n_p:]
        outs[4 * n_p][...] = over_devices(g_ref, loss_off, LANES)
        for i, (off, wd) in enumerate(offsets):
            g = over_devices(g_ref, off, wd)
            w, m, v = prm[3 * i][...], prm[3 * i + 1][...], prm[3 * i + 2][...]
            outs[4 * i][...] = g
            outs[4 * i + 1][...], outs[4 * i + 2][...], outs[4 * i + 3][...] = _adamw_math(w, g, m, v)

    flat = [a for t in params for a in t]
    out_shape = [jax.ShapeDtypeStruct(t[0].shape, F32) for t in params for _ in range(4)]
    out_shape.append(jax.ShapeDtypeStruct((1, LANES), F32))
    return _pcall(body, name="small_update", out_shape=out_shape)(gathered, *flat)


def kernel(x, c, w_ada, b_ada, norm1_w, w_in, q_norm_w, k_norm_w, w_pool, pool_scale, w_a_up, w_b_up, w_o, norm2_w, w_ff1, w_ff2, loss_target, m_w_ada, m_b_ada, m_norm1_w, m_w_in, m_q_norm_w, m_k_norm_w, m_w_pool, m_pool_scale, m_w_a_up, m_w_b_up, m_w_o, m_norm2_w, m_w_ff1, m_w_ff2, v_w_ada, v_b_ada, v_norm1_w, v_w_in, v_q_norm_w, v_k_norm_w, v_w_pool, v_pool_scale, v_w_a_up, v_w_b_up, v_w_o, v_norm2_w, v_w_ff1, v_w_ff2):
    _, S, D = x.shape
    PW = D // 2
    H = PW // HEAD_DIM
    cg = PW // N_GROUPS
    IN = w_in.shape[2] * N_CHIPS
    FF = w_ff1.shape[2] * N_CHIPS
    A_COLS = w_ada.shape[2]
    xi, yi, ci = lax.axis_index("x"), lax.axis_index("y"), lax.axis_index("c")
    chip = 2 * xi + yi
    dev = 2 * chip + ci
    c_arr = jnp.reshape(ci, (1,)).astype(jnp.int32)
    x2, tgt = x[0], loss_target[0]

    ws = [_W("w_in", "col", D, IN), _W("w_pool", "row", PW, cg), _W("w_a_up", "col", PW, D),
          _W("w_b_up", "col", PW, D), _W("w_o", "row", D, D), _W("w_ff1", "col", D, FF), _W("w_ff2", "row", FF, D)]
    w32 = [w_in[0], w_pool[0].reshape(cg, cg), w_a_up[0], w_b_up[0], w_o[0], w_ff1[0], w_ff2[0]]
    m32 = [m_w_in[0], m_w_pool[0].reshape(cg, cg), m_w_a_up[0], m_w_b_up[0], m_w_o[0], m_w_ff1[0], m_w_ff2[0]]
    v32 = [v_w_in[0], v_w_pool[0].reshape(cg, cg), v_w_a_up[0], v_w_b_up[0], v_w_o[0], v_w_ff1[0], v_w_ff2[0]]

    W_IN, W_POOL, W_A, W_B, W_O, W_FF1, W_FF2 = ws
    chip_arr = jnp.reshape(chip, (1,)).astype(jnp.int32)
    cc_arr = jnp.stack([ci, chip]).astype(jnp.int32)
    s_in, s_pool, s_a, s_b, s_o = [_cast_into_full([w], [a], chip_arr)[0] for w, a in zip(ws[:5], w32[:5])]
    (s_ff1, s_ff2), ((win_f,),) = _cast_into_full([W_FF1, W_FF2], w32[5:], chip_arr, riders=[_ag_rider([W_IN], [s_in])])

    sc_row = _silu_rows(c)
    sc_all = _dev_allgather("gather_silu_c", sc_row.reshape(SUBLANES, D // SUBLANES)).reshape(N_DEV, D)
    sc16 = jnp.concatenate([sc_all, jnp.zeros_like(sc_all)], axis=0)
    b_cols = lax.dynamic_slice(b_ada, (0, chip * A_COLS), (1, A_COLS))
    (mod_cols,) = _mm("mod_cols", [(sc16, w_ada[0])], M=2 * N_DEV, N=A_COLS, K=D, tm=16, tn=1024, tk=1024,
                      a_pro=lambda a: a.astype(BF16), b_pro=lambda b: b.astype(BF16),
                      extras=[(b_cols, "row", 0)], outs=[_tile_out(F32)], epi=lambda accs, ex: [accs[0] + ex[0]])
    mod_all = _dev_allgather("gather_mod", mod_cols[:N_DEV]).reshape(N_CHIPS, 2, N_DEV, A_COLS)
    mod_row = lax.dynamic_index_in_dim(mod_all[:, 0], dev, axis=1, keepdims=False).reshape(1, N_CHIPS * A_COLS)
    shift1, scale1, gate1, shift2, scale2, gate2 = [mod_row[:, i * D:(i + 1) * D] for i in range(6)]

    WIDE = dict(tm=2048, tn=512, tk=2048)
    DEEP = dict(tm=1024, tn=1024, tk=2048)
    DEEPER = dict(tm=1024, tn=1024, tk=4096)
    h = _norm_mod("norm1_mod", x2, norm1_w, scale1, shift1)
    (proj,), ((wpool_f, wa_f, wb_f, wo_f),) = _mm(
        "in_proj", [(h, win_f)], M=S, N=IN, K=D, outs=[_tile_out(BF16)], epi=lambda accs, ex: [accs[0]], **WIDE,
        riders=[_ag_rider([W_POOL, W_A, W_B, W_O], [s_pool, s_a, s_b, s_o], n_ch=2)])
    pooled, pa = _pool_fwd(proj, wpool_f, pool_scale, S, PW)
    (att, attf), ((wff1_f,),) = _attn_fwd(proj, q_norm_w, k_norm_w, S, H, PW // HEAD_DIM,
                                          riders=[_ag_rider([W_FF1], [s_ff1])])

    def merge_epi(accs, ex):
        sa, sb = jax.nn.sigmoid(ex[0].astype(F32)), jax.nn.sigmoid(ex[1].astype(F32))
        return [sa * accs[0] + sb * accs[1], accs[0], accs[1]]

    (merged, ya, yb), (ff2_a,) = _mm("branch_up_merge", [(pa, wa_f), (att, wb_f)], M=S, N=D, K=PW,
                                     extras=[(proj, "tile", 4 * PW), (proj, "tile", 4 * PW + D)],
                                     outs=[_tile_out(BF16)] * 3, epi=merge_epi,
                                     riders=[_ag_rider([W_FF2], [s_ff2], chunks=(0, 1))])
    (x1, o), (ff2_b,) = _mm("out_proj", [(merged, wo_f)], M=S, N=D, K=D, extras=[(x2, "tile", 0), (gate1, "row", 0)],
                            outs=[_tile_out(F32), _tile_out(BF16)], epi=lambda accs, ex: [ex[0] + ex[1] * accs[0], accs[0]],
                            riders=[_ag_rider([W_FF2], ff2_a, chunks=(1, 2))], **WIDE)
    h2 = _norm_mod("norm2_mod", x1, norm2_w, scale2, shift2)
    (rl,), ((wff2_f,),) = _mm("ff1", [(h2, wff1_f)], M=S, N=FF, K=D, outs=[_tile_out(BF16)], **WIDE,
                              epi=lambda accs, ex: [jnp.maximum(accs[0], 0.0)],
                              riders=[_ag_rider([W_FF2], ff2_b, chunks=(2, 4))])

    def square(a):
        af = a.astype(F32)
        return (af * af).astype(BF16)

    def loss_epi(accs, ex):
        x1_t, tgt_t, g2 = ex
        f = accs[0]
        diff = (x1_t + g2 * f) - tgt_t
        dy = diff * (1.0 / D)
        return [dy, dy * g2, _colsum(dy * f), _colsum(diff * diff)]

    dy, df, dgate2_p, loss_p = _mm("ff2_loss", [(rl, wff2_f)], M=S, N=D, K=FF, a_pro=square, **DEEP,
                                   extras=[(x1, "tile", 0), (tgt, "tile", 0), (gate2, "row", 0)],
                                   outs=[_tile_out(F32), _tile_out(BF16), _COLSUM, _COLSUM], epi=loss_epi)

    tied = []

    def behind(token, a):
        a, token = lax.optimization_barrier((a, token))
        tied.append(token)
        return a

    def pair_sums(group, partials, got):
        return [_pair_sum(w, g, r, c_arr) for w, g, r in zip(group, partials, got)]

    def chip_sums(group, sums, from_chips):
        return [_chip_sum(w, p, q, cc_arr) for w, p, q in zip(group, sums, from_chips)]

    first = lambda accs, ex: [accs[0]]
    gmm = dict(ta=True, outs=[_tile_out(BF16)], epi=first, **WIDE)
    (g_ff2,) = _mm("grad_w_ff2", [(rl, df)], M=FF, N=D, K=S, a_pro=square, ta=True, tm=512, tn=2048, tk=2048,
                   outs=[_tile_out(BF16)], epi=first)
    flight, token = _split_start("pair_w_ff2_start", "pair", [W_FF2], [g_ff2])
    (dz1,) = _mm("d_ff_hidden", [(behind(token, df), wff2_f)], M=S, N=FF, K=D, tb=True, extras=[(rl, "tile", 0)],
                 outs=[_tile_out(BF16)], epi=lambda accs, ex: [accs[0] * (2.0 * ex[0].astype(F32))], **WIDE)
    sum_ff2 = pair_sums([W_FF2], *_split_wait("pair_w_ff2_wait", flight, after=[dz1] + tied))
    chip_ff2, token = _split_start("chip_w_ff2_start", "chip", [W_FF2], sum_ff2)
    (g_ff1,) = _mm("grad_w_ff1", [(behind(token, h2), dz1)], M=D, N=FF, K=S, **gmm)
    flight, token = _split_start("pair_w_ff1_start", "pair", [W_FF1], [g_ff1])
    (dh2,) = _mm("d_h2", [(behind(token, dz1), wff1_f)], M=S, N=D, K=FF, tb=True, outs=[_tile_out(F32)], epi=first,
                 **DEEPER)
    sum_ff1 = pair_sums([W_FF1], *_split_wait("pair_w_ff1_wait", flight, after=[dh2] + tied))
    chip_ff1, token = _split_start("chip_w_ff1_start", "chip", [W_FF1], sum_ff1)
    dx1, dshift2_p, dscale2_p, gn2_p, do, dgate1_p = _norm_mod_bwd("norm2_bwd", behind(token, dh2), x1, dy, norm2_w, scale2,
                                                                   gate_o=(o, gate1))
    (g_wo,) = _mm("grad_w_o", [(merged, do)], M=D, N=D, K=S, **gmm)

    def gate_epi(accs, ex):
        dm = accs[0]
        sa, sb = jax.nn.sigmoid(ex[0].astype(F32)), jax.nn.sigmoid(ex[1].astype(F32))
        ya_t, yb_t = ex[2].astype(F32), ex[3].astype(F32)
        return [dm * sa, dm * sb, dm * ya_t * (sa * (1.0 - sa)), dm * yb_t * (sb * (1.0 - sb))]

    dya, dyb, dga, dgb = _mm("d_merged", [(do, wo_f)], M=S, N=D, K=D, tb=True, tm=1024, tn=512, tk=2048,
                             extras=[(proj, "tile", 4 * PW), (proj, "tile", 4 * PW + D), (ya, "tile", 0), (yb, "tile", 0)],
                             outs=[_tile_out(BF16)] * 4, epi=gate_epi)
    both = lambda accs, ex: [accs[0], accs[1]]
    g_wa, g_wb = _mm("grad_w_up", [(pa, dya), (att, dyb)], M=PW, N=D, K=S, ta=True, outs=[_tile_out(BF16)] * 2, epi=both,
                     **WIDE)
    mid = [W_A, W_B, W_O]
    flight, token = _split_start("pair_mid_start", "pair", mid, [g_wa, g_wb, g_wo])
    dpa, datt = _mm("d_branches", [(dya, wa_f), (behind(token, dyb), wb_f)], M=S, N=PW, K=D, tb=True,
                    outs=[_tile_out(F32), _tile_out(BF16)], epi=both, tm=1024, tn=512, tk=2048)
    sum_mid = pair_sums(mid, *_split_wait("pair_mid_wait", flight, after=[datt] + tied))
    chip_mid, token = _split_start("chip_mid_start", "chip", mid, sum_mid)
    du, g_wpool4, gscale_p = _pool_bwd(dpa, pooled, wpool_f, pool_scale, S, PW)
    dq, dk, dv, gq_p, gk_p = _attn_bwd(proj, behind(token, datt), attf, q_norm_w, k_norm_w, S, H, PW // HEAD_DIM)
    dproj = jnp.concatenate([du, dq, dk, dv, dga, dgb], axis=1)
    early = [W_FF1, W_FF2]
    sum_ff1, q_ff1 = _split_wait("chip_w_ff1_wait", chip_ff1, after=[dq] + tied)
    sum_ff2, q_ff2 = _split_wait("chip_w_ff2_wait", chip_ff2, after=[dq] + tied)
    halves_early = chip_sums(early, sum_ff1 + sum_ff2, q_ff1 + q_ff2)
    (g_win,), (grads_early,) = _mm("grad_w_in", [(h, dproj)], M=D, N=IN, K=S, riders=[_sf_rider(early, halves_early)],
                                   **gmm)
    last = [W_IN, W_POOL]
    g_last = [g_win, g_wpool4.reshape(PW, cg)]
    sum_mid, q_mid = _split_wait("chip_mid_wait", chip_mid, after=[g_win] + tied)
    halves_mid = chip_sums(mid, sum_mid, q_mid)
    (dh,), (got_last, grads_mid) = _mm("d_h", [(dproj, win_f)], M=S, N=D, K=IN, tb=True, outs=[_tile_out(F32)], epi=first,
                                       riders=[_px_rider(last, g_last), _sf_rider(mid, halves_mid)], **DEEPER)
    sum_last = pair_sums(last, g_last, got_last)
    grad_x, dshift1_p, dscale1_p, gn1_p = _norm_mod_bwd("norm1_bwd", dh, x2, dx1, norm1_w, scale1)

    parts = [dshift1_p, dscale1_p, dgate1_p, dshift2_p, dscale2_p, dgate2_p, gn1_p, gn2_p,
             gscale_p.reshape(1, 1, PW), gq_p, gk_p]
    widths = [D] * 8 + [PW, HEAD_DIM, HEAD_DIM]
    used = sum(widths)
    P = -(-(used + LANES) // (SUBLANES * LANES)) * (SUBLANES * LANES)
    packed = _pack_partials(parts + [loss_p], widths, P)
    gathered = _dev_allgather("gather_vector_grads", packed.reshape(SUBLANES, P // SUBLANES)).reshape(N_DEV, P)
    sum_last, gathered = lax.optimization_barrier((sum_last, gathered))
    chip_last, token = _split_start("chip_last_start", "chip", last, sum_last)
    small = [(b_ada, m_b_ada, v_b_ada), (norm1_w, m_norm1_w, v_norm1_w), (norm2_w, m_norm2_w, v_norm2_w),
             (pool_scale, m_pool_scale, v_pool_scale), (q_norm_w, m_q_norm_w, v_q_norm_w),
             (k_norm_w, m_k_norm_w, v_k_norm_w)]
    offsets = [(0, 6 * D), (6 * D, D), (7 * D, D), (8 * D, PW), (8 * D + PW, HEAD_DIM), (8 * D + PW + HEAD_DIM, HEAD_DIM)]
    su = _small_update(gathered, offsets, small, used)
    (g_b, d_b, nm_b, nv_b, g_n1, d_n1, nm_n1, nv_n1, g_n2, d_n2, nm_n2, nv_n2, g_ps, d_ps, nm_ps, nv_ps,
     g_qn, d_qn, nm_qn, nv_qn, g_kn, d_kn, nm_kn, nv_kn, loss_sum) = su
    dmod_sh = lax.dynamic_slice(gathered, (0, chip * A_COLS), (N_DEV, A_COLS))
    dmod_sh, token = lax.optimization_barrier((dmod_sh, token))
    g_ada, d_ada, nm_ada, nv_ada = _ada_update(sc_all.T, dmod_sh, w_ada[0], m_w_ada[0], v_w_ada[0])

    upd_done = [_adamw("adamw_" + w.name, a, g, m, v, after=token)
                for w, a, g, m, v in zip(ws[2:], w32[2:], list(grads_mid) + list(grads_early), m32[2:], v32[2:])]

    sum_last, q_last = _split_wait("chip_last_wait", chip_last, after=[nv_ada] + [u[3] for u in upd_done])
    halves_last = chip_sums(last, sum_last, q_last)
    filled = _run_rider("grad_sibling_fill", _sf_rider(last, halves_last))
    upd = [_adamw("adamw_" + w.name, a, g, m, v) for w, a, g, m, v in zip(ws[:2], w32[:2], filled, m32[:2], v32[:2])]
    upd += upd_done

    loss = (0.5 / D) * loss_sum[0, 0]

    def up(a):
        return a[None]

    def pool4(a):
        return a.reshape(1, N_GROUPS, cg // N_CHIPS, cg)

    (gr_win, d_win, nm_win, nv_win), (gr_wp, d_wp, nm_wp, nv_wp), (gr_wa, d_wa, nm_wa, nv_wa), \
        (gr_wb, d_wb, nm_wb, nv_wb), (gr_wo, d_wo, nm_wo, nv_wo), (gr_f1, d_f1, nm_f1, nv_f1), \
        (gr_f2, d_f2, nm_f2, nv_f2) = upd
    return (
        loss, grad_x[None],
        up(g_ada), g_b, g_n1, up(gr_win), g_qn, g_kn, pool4(gr_wp), g_ps, up(gr_wa), up(gr_wb), up(gr_wo), g_n2,
        up(gr_f1), up(gr_f2),
        up(d_ada), d_b, d_n1, up(d_win), d_qn, d_kn, pool4(d_wp), d_ps, up(d_wa), up(d_wb), up(d_wo), d_n2,
        up(d_f1), up(d_f2),
        up(nm_ada), nm_b, nm_n1, up(nm_win), nm_qn, nm_kn, pool4(nm_wp), nm_ps, up(nm_wa), up(nm_wb), up(nm_wo), nm_n2,
        up(nm_f1), up(nm_f2),
        up(nv_ada), nv_b, nv_n1, up(nv_win), nv_qn, nv_kn, pool4(nv_wp), nv_ps, up(nv_wa), up(nv_wb), up(nv_wo), nv_n2,
        up(nv_f1), up(nv_f2),
    )
```

```python
import functools
import math

import jax
import jax.numpy as jnp
from jax import lax
from jax.experimental import pallas as pl
from jax.experimental.pallas import tpu as pltpu

F32 = jnp.float32
BF16 = jnp.bfloat16
MESH = pl.DeviceIdType.MESH
ANY = pl.BlockSpec(memory_space=pl.ANY)

EPS = 1e-6
HEAD_DIM = 128
LANES, SUBLANES = 128, 8
POOL_WINDOWS = (2, 4, 8, 16)
N_GROUPS = len(POOL_WINDOWS)
assert POOL_WINDOWS == tuple(2 << g for g in range(N_GROUPS))
N_CHIPS = 4
N_DEV = 8
ADAM_LR, ADAM_B1, ADAM_B2, ADAM_EPS, ADAM_WD, ADAM_STEP = 0.001, 0.9, 0.999, 1e-08, 0.01, 10
VMEM_LIMIT_V7X = 56 * 1024 * 1024
ATT_T = 256
ATT_GROUP = 8
POOL_T = 256


def _pcall(body, **kw):
    return pl.pallas_call(body, **kw)


def _params(sem=None):
    return pltpu.CompilerParams(dimension_semantics=sem, vmem_limit_bytes=VMEM_LIMIT_V7X)


def _tile(n, pref):
    if n <= pref:
        return n
    t = pref
    while n % t:
        t //= 2
    return t


class _Rider:
    def __init__(self, arrays, out_shape, sems, start, finish, aliases=None, steps=()):
        self.arrays, self.out_shape, self.sems = list(arrays), list(out_shape), list(sems)
        self.start, self.finish, self.aliases, self.steps = start, finish, aliases or {}, list(steps)


def _ride(name, body, riders, arrays, *, grid, in_specs, out_specs, out_shape, scratch_shapes, sem, scalars=None):
    n_in, n_out, n_scr = len(arrays), len(out_shape), len(scratch_shapes)
    r_arrays = [a for r in riders for a in r.arrays]
    r_outs = [o for r in riders for o in r.out_shape]
    r_sems = [s for r in riders for s in r.sems]
    n_hooks = max([len(r.steps) for r in riders], default=0)
    total = math.prod(grid)
    aliases, off_i, off_o = {}, n_in + (scalars is not None), n_out
    for r in riders:
        for a, o in r.aliases.items():
            aliases[off_i + a] = off_o + o
        off_i += len(r.arrays)
        off_o += len(r.out_shape)

    def full(*refs):
        p = 0
        groups = []
        for n in (n_in, len(r_arrays), n_out, len(r_outs), n_scr, len(r_sems)):
            groups.append(refs[p:p + n])
            p += n
        ins, rin, outs, rout, scr, rsem = groups

        def each(what):
            a = o = s = 0
            for r in riders:
                fn = what(r)
                if fn is not None:
                    fn(rin[a:a + len(r.arrays)], rout[o:o + len(r.out_shape)], rsem[s:s + len(r.sems)])
                a, o, s = a + len(r.arrays), o + len(r.out_shape), s + len(r.sems)

        if riders:
            lin = 0
            for d, g in enumerate(grid):
                lin = lin * g + pl.program_id(d)
            pl.when(lin == 0)(lambda: each(lambda r: r.start))
            for t in range(n_hooks):
                pl.when(lin == min(total - 1, ((4 * t + 3) * total) // (4 * n_hooks)))(
                    lambda t=t: each(lambda r: r.steps[t] if t < len(r.steps) else None))
        body(*ins, *outs, *scr)
        if riders:
            pl.when(lin == total - 1)(lambda: each(lambda r: r.finish))

    specs = dict(grid=grid, in_specs=list(in_specs) + [ANY] * len(r_arrays),
                 out_specs=list(out_specs) + [ANY] * len(r_outs), scratch_shapes=list(scratch_shapes) + r_sems)
    common = dict(name=name, out_shape=list(out_shape) + r_outs, input_output_aliases=aliases,
                  compiler_params=_params(("arbitrary",) * len(grid) if riders else sem))
    if scalars is None:
        res = _pcall(full, **specs, **common)(*arrays, *r_arrays)
    else:
        res = _pcall(lambda _, *refs: full(*refs), **common,
                     grid_spec=pltpu.PrefetchScalarGridSpec(num_scalar_prefetch=1, **specs))(scalars, *arrays, *r_arrays)
    if not riders:
        return res
    main, rest, per = res[:n_out], res[n_out:], []
    for r in riders:
        per.append(rest[:len(r.out_shape)])
        rest = rest[len(r.out_shape):]
    return main, per


def _run_rider(name, rider):
    def body(*refs):
        n_a, n_o = len(rider.arrays), len(rider.out_shape)
        ins, outs, sems = refs[:n_a], refs[n_a:n_a + n_o], refs[n_a + n_o:]
        for fn in [rider.start] + rider.steps + [rider.finish]:
            fn(ins, outs, sems)

    return _pcall(body, name=name, out_shape=rider.out_shape, in_specs=[ANY] * len(rider.arrays),
                  out_specs=[ANY] * len(rider.out_shape), scratch_shapes=rider.sems,
                  input_output_aliases=rider.aliases)(*rider.arrays)


def _mm(name, pairs, *, M, N, K, ta=False, tb=False, tm=512, tn=1024, tk=1024,
        a_pro=None, b_pro=None, extras=(), outs, epi, riders=()):
    tm, tn, tk = _tile(M, tm), _tile(N, tn), _tile(K, tk)
    n_i, n_j, n_k = M // tm, N // tn, K // tk
    n_p, n_e = len(pairs), len(extras)
    arrays, in_specs = [], []
    for a, _ in pairs:
        arrays.append(a)
        in_specs.append(pl.BlockSpec((tk, tm), lambda i, j, k: (k, i)) if ta
                        else pl.BlockSpec((tm, tk), lambda i, j, k: (i, k)))
    for _, b in pairs:
        arrays.append(b)
        in_specs.append(pl.BlockSpec((tn, tk), lambda i, j, k: (j, k)) if tb
                        else pl.BlockSpec((tk, tn), lambda i, j, k: (k, j)))
    for arr, kind, off in extras:
        ob = off // tn
        assert off % tn == 0
        arrays.append(arr)
        if kind == "tile":
            in_specs.append(pl.BlockSpec((tm, tn), lambda i, j, k, ob=ob: (i, j + ob)))
        else:
            in_specs.append(pl.BlockSpec((1, tn), lambda i, j, k, ob=ob: (0, j + ob)))
    out_shape, out_specs = [], []
    for o in outs:
        if o["kind"] == "tile":
            out_shape.append(jax.ShapeDtypeStruct((M, N), o["dtype"]))
            out_specs.append(pl.BlockSpec((tm, tn), lambda i, j, k: (i, j)))
        else:
            out_shape.append(jax.ShapeDtypeStruct((n_i, 1, N), F32))
            out_specs.append(pl.BlockSpec((1, 1, tn), lambda i, j, k: (i, 0, j)))
    dims = (((0 if ta else 1,), (1 if tb else 0,)), ((), ()))

    def body(*refs):
        a_refs, b_refs = refs[:n_p], refs[n_p:2 * n_p]
        e_refs = refs[2 * n_p:2 * n_p + n_e]
        o_refs = refs[2 * n_p + n_e:2 * n_p + n_e + len(outs)]
        acc_refs = refs[2 * n_p + n_e + len(outs):]

        def product(p):
            a, b = a_refs[p][...], b_refs[p][...]
            if a_pro is not None:
                a = a_pro(a)
            if b_pro is not None:
                b = b_pro(b)
            return lax.dot_general(a, b, dims, preferred_element_type=F32)

        def write(accs):
            vals = epi(accs, [e[...] for e in e_refs])
            for o, o_ref, val in zip(outs, o_refs, vals):
                if o["kind"] == "tile":
                    o_ref[...] = val.astype(o_ref.dtype)
                else:
                    o_ref[0] = val

        if n_k == 1:
            write([product(p) for p in range(n_p)])
            return
        k = pl.program_id(2)

        @pl.when(k == 0)
        def _():
            for acc in acc_refs:
                acc[...] = jnp.zeros_like(acc)

        for p in range(n_p):
            acc_refs[p][...] += product(p)

        pl.when(k == n_k - 1)(lambda: write([acc[...] for acc in acc_refs]))

    return _ride(name, body, riders, arrays, grid=(n_i, n_j, n_k), in_specs=in_specs, out_specs=out_specs,
                 out_shape=out_shape, scratch_shapes=[pltpu.VMEM((tm, tn), F32) for _ in pairs] if n_k > 1 else [],
                 sem=("parallel", "parallel", "arbitrary"))


def _tile_out(dtype):
    return {"kind": "tile", "dtype": dtype}


_COLSUM = {"kind": "colsum"}


def _colsum(v):
    return jnp.sum(v, axis=0, keepdims=True)


def _norm_mod(name, x, norm_w, scale, shift):
    S, D = x.shape
    tr = _tile(S, 256)

    def body(x_ref, nw_ref, sc_ref, sh_ref, h_ref):
        xv = x_ref[...]
        r = lax.rsqrt(jnp.mean(xv * xv, axis=-1, keepdims=True) + EPS)
        h_ref[...] = ((xv * r * nw_ref[...]) * (1.0 + sc_ref[...]) + sh_ref[...]).astype(BF16)

    row = pl.BlockSpec((1, D), lambda i: (0, 0))
    til = pl.BlockSpec((tr, D), lambda i: (i, 0))
    return _pcall(body, name=name, grid=(S // tr,), in_specs=[til, row, row, row], out_specs=til,
                  out_shape=jax.ShapeDtypeStruct((S, D), BF16), compiler_params=_params(("parallel",)))(
                      x, norm_w, scale, shift)


def _norm_mod_bwd(name, dh, x, dres, norm_w, scale, gate_o=None):
    S, D = x.shape
    tr = _tile(S, 256)
    n_r = S // tr
    with_gate = gate_o is not None

    def body(*refs):
        if with_gate:
            dh_ref, x_ref, dres_ref, nw_ref, sc_ref, o_ref, g_ref, dx_ref, p1, p2, p3, do_ref, p4 = refs
        else:
            dh_ref, x_ref, dres_ref, nw_ref, sc_ref, dx_ref, p1, p2, p3 = refs
        dhv, xv, nw = dh_ref[...], x_ref[...], nw_ref[...]
        r = lax.rsqrt(jnp.mean(xv * xv, axis=-1, keepdims=True) + EPS)
        xh = xv * r
        p1[0] = _colsum(dhv)
        p2[0] = _colsum(dhv * (xh * nw))
        dn = dhv * (1.0 + sc_ref[...])
        p3[0] = _colsum(dn * xh)
        dxh = dn * nw
        dx = dres_ref[...] + r * (dxh - xh * jnp.mean(dxh * xh, axis=-1, keepdims=True))
        dx_ref[...] = dx
        if with_gate:
            do_ref[...] = (dx * g_ref[...]).astype(BF16)
            p4[0] = _colsum(dx * o_ref[...].astype(F32))

    row = pl.BlockSpec((1, D), lambda i: (0, 0))
    til = pl.BlockSpec((tr, D), lambda i: (i, 0))
    part = pl.BlockSpec((1, 1, D), lambda i: (i, 0, 0))
    part_shape = jax.ShapeDtypeStruct((n_r, 1, D), F32)
    in_specs = [til, til, til, row, row]
    arrays = [dh, x, dres, norm_w, scale]
    out_specs = [til, part, part, part]
    out_shape = [jax.ShapeDtypeStruct((S, D), F32), part_shape, part_shape, part_shape]
    if with_gate:
        in_specs += [til, row]
        arrays += list(gate_o)
        out_specs += [til, part]
        out_shape += [jax.ShapeDtypeStruct((S, D), BF16), part_shape]
    return _pcall(body, name=name, grid=(n_r,), in_specs=in_specs, out_specs=out_specs, out_shape=out_shape,
                  compiler_params=_params(("parallel",)))(*arrays)


def _pool_w_specs(rows, cg):
    return [pl.BlockSpec((rows, cg), lambda g, j=j: (N_GROUPS * j + g, 0)) for j in range(N_CHIPS)]


def _pool_fwd(proj, wp_full, pool_scale, S, PW):
    cg = PW // N_GROUPS
    rows = cg // N_CHIPS
    T = _tile(S, POOL_T)
    n_t = S // T

    def body(u_ref, w0, w1, w2, w3, ps_ref, pooled_ref, pa_ref):
        g = pl.program_id(0)
        win = jnp.left_shift(2, g)
        w = jnp.concatenate([w0[...], w1[...], w2[...], w3[...]], axis=0)
        t_i = lax.broadcasted_iota(jnp.int32, (T, T), 0)
        j_i = lax.broadcasted_iota(jnp.int32, (T, T), 1)
        b_cur = ((j_i <= t_i) & (j_i > t_i - win)).astype(BF16)
        b_prev = (j_i - T > t_i - win).astype(BF16)
        row = lax.broadcasted_iota(jnp.int32, (T, 1), 0)
        for r in range(n_t):
            cur = u_ref[r * T:(r + 1) * T, :]
            ws = jnp.dot(b_cur, cur, preferred_element_type=F32)
            if r > 0:
                ws += jnp.dot(b_prev, u_ref[(r - 1) * T:r * T, :], preferred_element_type=F32)
            count = jnp.minimum(row + (r * T + 1), win).astype(F32)
            pooled = (ws / count - cur.astype(F32)).astype(BF16)
            pooled_ref[r * T:(r + 1) * T, :] = pooled
            mixed = jnp.dot(pooled, w, preferred_element_type=F32)
            pa_ref[r * T:(r + 1) * T, :] = (mixed * ps_ref[...]).astype(BF16)

    col = pl.BlockSpec((S, cg), lambda g: (0, g))
    return _pcall(
        body, name="pool_fwd", grid=(N_GROUPS,),
        in_specs=[col] + _pool_w_specs(rows, cg) + [pl.BlockSpec((1, cg), lambda g: (0, g))],
        out_specs=[col, col],
        out_shape=[jax.ShapeDtypeStruct((S, PW), BF16), jax.ShapeDtypeStruct((S, PW), BF16)],
        compiler_params=_params(("parallel",)),
    )(proj, wp_full, wp_full, wp_full, wp_full, pool_scale)


def _pool_bwd(dpa, pooled, wp_full, pool_scale, S, PW):
    cg = PW // N_GROUPS
    rows = cg // N_CHIPS
    T = _tile(S, POOL_T)
    n_t = S // T

    def body(dpa_ref, pooled_ref, w0, w1, w2, w3, ps_ref, du_ref, gw_ref, gs_ref, dp_s, dpc_s, dmx_s):
        g = pl.program_id(0)
        win = jnp.left_shift(2, g)
        w = jnp.concatenate([w0[...], w1[...], w2[...], w3[...]], axis=0)
        row = lax.broadcasted_iota(jnp.int32, (T, 1), 0)
        gs = jnp.zeros((1, cg), F32)
        for r in range(n_t):
            sl = slice(r * T, (r + 1) * T)
            mixed = jnp.dot(pooled_ref[sl, :], w, preferred_element_type=F32)
            dpa_t = dpa_ref[sl, :]
            gs += _colsum(dpa_t * mixed)
            dmx = (dpa_t * ps_ref[...]).astype(BF16)
            dmx_s[sl, :] = dmx
            dpo = lax.dot_general(dmx, w, (((1,), (1,)), ((), ())), preferred_element_type=F32)
            dp_s[sl, :] = dpo
            count = jnp.minimum(row + (r * T + 1), win).astype(F32)
            dpc_s[sl, :] = (dpo / count).astype(BF16)
        gs_ref[...] = gs
        gw = lax.dot_general(pooled_ref[...], dmx_s[...], (((0,), (0,)), ((), ())), preferred_element_type=F32)
        for j in range(N_CHIPS):
            gw_ref[j, 0] = gw[j * rows:(j + 1) * rows, :].astype(BF16)
        j_i = lax.broadcasted_iota(jnp.int32, (T, T), 0)
        t_i = lax.broadcasted_iota(jnp.int32, (T, T), 1)
        b_cur = ((t_i >= j_i) & (t_i < j_i + win)).astype(BF16)
        b_next = (t_i + T < j_i + win).astype(BF16)
        for r in range(n_t):
            sl = slice(r * T, (r + 1) * T)
            acc = jnp.dot(b_cur, dpc_s[sl, :], preferred_element_type=F32)
            if r + 1 < n_t:
                acc += jnp.dot(b_next, dpc_s[(r + 1) * T:(r + 2) * T, :], preferred_element_type=F32)
            du_ref[sl, :] = (acc - dp_s[sl, :]).astype(BF16)

    col = pl.BlockSpec((S, cg), lambda g: (0, g))
    return _pcall(
        body, name="pool_bwd", grid=(N_GROUPS,),
        in_specs=[col, col] + _pool_w_specs(rows, cg) + [pl.BlockSpec((1, cg), lambda g: (0, g))],
        out_specs=[col, pl.BlockSpec((N_CHIPS, 1, rows, cg), lambda g: (0, g, 0, 0)),
                   pl.BlockSpec((1, cg), lambda g: (0, g))],
        out_shape=[jax.ShapeDtypeStruct((S, PW), BF16),
                   jax.ShapeDtypeStruct((N_CHIPS, N_GROUPS, rows, cg), BF16),
                   jax.ShapeDtypeStruct((1, PW), F32)],
        scratch_shapes=[pltpu.VMEM((S, cg), F32), pltpu.VMEM((S, cg), BF16), pltpu.VMEM((S, cg), BF16)],
        compiler_params=_params(("parallel",)),
    )(dpa, pooled, wp_full, wp_full, wp_full, wp_full, pool_scale)


_NT = (((1,), (1,)), ((), ()))
_TN = (((0,), (0,)), ((), ()))


def _split_dot(v, tri):
    hi = v.astype(BF16)
    lo = (v - hi.astype(F32)).astype(BF16)
    return jnp.dot(hi, tri, preferred_element_type=F32) + jnp.dot(lo, tri, preferred_element_type=F32)


LOG2E = 1.4426950408889634
QK_SCALE = 1.0 / math.sqrt(HEAD_DIM)


def _sb_scores(q2_i, k_j, tri_l, masked):
    tq, tk = q2_i.shape[0], k_j.shape[0]
    s = lax.dot_general(q2_i, k_j, _NT, preferred_element_type=F32)
    lp = jnp.log(1.0 + jnp.exp2(-jnp.abs(s))) * LOG2E
    lb = jnp.minimum(s, 0.0) - lp
    l = lb - s
    mask = None
    if masked:
        mask = lax.broadcasted_iota(jnp.int32, (tq, tk), 0) > lax.broadcasted_iota(jnp.int32, (tq, tk), 1)
        l = jnp.where(mask, l, 0.0)
    return l, lb, lb + _split_dot(l, tri_l), mask


def _sb_weights(t, carry_l, mask):
    a = jnp.exp2(t + carry_l)
    return a if mask is None else jnp.where(mask, a, 0.0)


def _rowsum(v):
    return jnp.sum(v, axis=1, keepdims=True)


def _qk_norm(x_ref, w_ref):
    xv = x_ref[...].astype(F32)
    r = lax.rsqrt(jnp.mean(xv * xv, axis=-1, keepdims=True) + EPS)
    return xv * r, r


def _attn_fwd(proj, q_norm_w, k_norm_w, S, H, q_off, riders=()):
    t = _tile(S, ATT_T)
    n_q = S // t

    def body(q_ref, k_ref, v_ref, qw_ref, kw_ref, att_ref, attf_ref, qn_s, kn_s):
        qh, _ = _qk_norm(q_ref, qw_ref)
        qn_s[...] = (qh * qw_ref[...] * (QK_SCALE * LOG2E)).astype(BF16)
        kh, _ = _qk_norm(k_ref, kw_ref)
        kn_s[...] = (kh * kw_ref[...]).astype(BF16)
        tri_l = (lax.broadcasted_iota(jnp.int32, (t, t), 0) > lax.broadcasted_iota(jnp.int32, (t, t), 1)).astype(BF16)

        def rows(j):
            return pl.ds(pl.multiple_of(j * t, t), t)

        def q_step(i, _):
            q_i = qn_s[rows(i), :]

            def av(a, j):
                return jnp.dot(a.astype(BF16), v_ref[rows(j), :], preferred_element_type=F32)

            l, _, tt, mask = _sb_scores(q_i, kn_s[rows(i), :], tri_l, True)
            acc = av(_sb_weights(tt, 0.0, mask), i)
            carry = _rowsum(l)

            def single(_, c):
                carry, acc = c
                l, _, tt, _ = _sb_scores(q_i, kn_s[rows(i - 1), :], tri_l, False)
                return carry + _rowsum(l), acc + av(_sb_weights(tt, carry, None), i - 1)

            carry, acc = lax.fori_loop(0, i % 2, single, (carry, acc))
            top = i - 1 - i % 2

            def pair(p, c):
                carry, acc = c
                j0 = top - 2 * p
                l0, _, t0, _ = _sb_scores(q_i, kn_s[rows(j0), :], tri_l, False)
                l1, _, t1, _ = _sb_scores(q_i, kn_s[rows(j0 - 1), :], tri_l, False)
                mid = carry + _rowsum(l0)
                acc = acc + av(_sb_weights(t0, carry, None), j0) + av(_sb_weights(t1, mid, None), j0 - 1)
                return mid + _rowsum(l1), acc

            _, acc = lax.fori_loop(0, i // 2, pair, (carry, acc))
            att_ref[rows(i), :] = acc.astype(BF16)
            attf_ref[rows(i), :] = acc
            return 0

        lax.fori_loop(0, n_q, q_step, 0)

    def col(off):
        return pl.BlockSpec((S, HEAD_DIM), lambda h, off=off: (0, off + h))

    wspec = pl.BlockSpec((1, HEAD_DIM), lambda h: (0, 0))
    return _ride(
        "attn_fwd", body, riders, [proj, proj, proj, q_norm_w, k_norm_w], grid=(H,),
        in_specs=[col(q_off), col(q_off + H), col(q_off + 2 * H), wspec, wspec],
        out_specs=[col(0), col(0)],
        out_shape=[jax.ShapeDtypeStruct((S, H * HEAD_DIM), BF16), jax.ShapeDtypeStruct((S, H * HEAD_DIM), F32)],
        scratch_shapes=[pltpu.VMEM((S, HEAD_DIM), BF16), pltpu.VMEM((S, HEAD_DIM), BF16)],
        sem=("parallel",))


def _attn_bwd(proj, datt, attf, q_norm_w, k_norm_w, S, H, q_off, riders=()):
    t = _tile(S, ATT_T)
    n_q = S // t

    def body(q_ref, k_ref, v_ref, do_ref, o_ref, qw_ref, kw_ref, dq_ref, dk_ref, dv_ref, gq_ref, gk_ref,
             qn_s, kn_s, qz_s, kz_s, dk_s, dv_s, gq_s):
        qw, kw = qw_ref[...], kw_ref[...]
        qh, _ = _qk_norm(q_ref, qw_ref)
        qn_s[...] = (qh * qw * (QK_SCALE * LOG2E)).astype(BF16)
        qz_s[...] = (qh * qw * QK_SCALE).astype(BF16)
        kh, _ = _qk_norm(k_ref, kw_ref)
        kn_s[...] = (kh * kw).astype(BF16)
        kz_s[...] = (kh * kw * QK_SCALE).astype(BF16)
        dk_s[...] = jnp.zeros_like(dk_s)
        dv_s[...] = jnp.zeros_like(dv_s)
        gq_s[...] = jnp.zeros_like(gq_s)
        r_i = lax.broadcasted_iota(jnp.int32, (t, t), 0)
        c_i = lax.broadcasted_iota(jnp.int32, (t, t), 1)
        tri_l = (r_i > c_i).astype(BF16)
        tri_e = (r_i >= c_i).astype(BF16)

        def rows(j):
            return pl.ds(pl.multiple_of(j * t, t), t)

        def q_step(i, _):
            q_i = qn_s[rows(i), :]
            do_i = do_ref[rows(i), :]
            d_i = _rowsum(do_i.astype(F32) * o_ref[rows(i), :])

            def scores(j, masked):
                l, lb, tt, mask = _sb_scores(q_i, kn_s[rows(j), :], tri_l, masked)
                da = lax.dot_general(do_i, v_ref[rows(j), :], _NT, preferred_element_type=F32)
                return l, lb, tt, mask, da

            def grads(j, sc, carry_l, carry_e, dq_acc):
                l, lb, tt, mask, da = sc
                a_bf = _sb_weights(tt, carry_l, mask).astype(BF16)
                e = da * a_bf.astype(F32)
                p = (d_i - carry_e) - _split_dot(e, tri_e)
                dz = e - jnp.exp2(lb) * (e + p)
                if mask is not None:
                    dz = jnp.where(mask, dz, 0.0)
                dz = dz.astype(BF16)
                dk_s[rows(j), :] += lax.dot_general(dz, qz_s[rows(i), :], _TN, preferred_element_type=F32)
                dv_s[rows(j), :] += lax.dot_general(a_bf, do_i, _TN, preferred_element_type=F32)
                return (carry_l + _rowsum(l), carry_e + _rowsum(e),
                        dq_acc + jnp.dot(dz, kz_s[rows(j), :], preferred_element_type=F32))

            zero = jnp.zeros((t, 1), F32)
            first = (zero, zero, jnp.zeros((t, HEAD_DIM), F32))

            def group(js, diagonal_first, c):
                scs = [scores(j, diagonal_first and n == 0) for n, j in enumerate(js)]
                for j, sc in zip(js, scs):
                    c = grads(j, sc, *c)
                return c

            n_first = i % ATT_GROUP
            c = lax.switch(n_first, [functools.partial(group, [i - u for u in range(n + 1)], True, first)
                                     for n in range(ATT_GROUP)])
            top = i - 1 - n_first

            def whole(p, c):
                j0 = top - ATT_GROUP * p
                return group([j0 - u for u in range(ATT_GROUP)], False, c)

            _, _, dqn = lax.fori_loop(0, (i - n_first) // ATT_GROUP, whole, c)
            qv = q_ref[rows(i), :].astype(F32)
            r = lax.rsqrt(jnp.mean(qv * qv, axis=-1, keepdims=True) + EPS)
            xh = qv * r
            gq_s[...] += _colsum(dqn * xh)
            dxh = dqn * qw
            dq_ref[rows(i), :] = (r * (dxh - xh * jnp.mean(dxh * xh, axis=-1, keepdims=True))).astype(BF16)
            return 0

        lax.fori_loop(0, n_q, q_step, 0)
        gq_ref[0] = gq_s[...]
        kh, rk = _qk_norm(k_ref, kw_ref)
        dkn = dk_s[...]
        gk_ref[0] = _colsum(dkn * kh)
        dxh = dkn * kw
        dk_ref[...] = (rk * (dxh - kh * jnp.mean(dxh * kh, axis=-1, keepdims=True))).astype(BF16)
        dv_ref[...] = dv_s[...].astype(BF16)

    def col(off):
        return pl.BlockSpec((S, HEAD_DIM), lambda h, off=off: (0, off + h))

    wspec = pl.BlockSpec((1, HEAD_DIM), lambda h: (0, 0))
    gspec = pl.BlockSpec((1, 1, HEAD_DIM), lambda h: (h, 0, 0))
    act = jax.ShapeDtypeStruct((S, H * HEAD_DIM), BF16)
    gsh = jax.ShapeDtypeStruct((H, 1, HEAD_DIM), F32)
    return _ride(
        "attn_bwd", body, riders, [proj, proj, proj, datt, attf, q_norm_w, k_norm_w], grid=(H,),
        in_specs=[col(q_off), col(q_off + H), col(q_off + 2 * H), col(0), col(0), wspec, wspec],
        out_specs=[col(0), col(0), col(0), gspec, gspec],
        out_shape=[act, act, act, gsh, gsh],
        scratch_shapes=[pltpu.VMEM((S, HEAD_DIM), BF16)] * 4 + [pltpu.VMEM((S, HEAD_DIM), F32)] * 2
        + [pltpu.VMEM((1, HEAD_DIM), F32)],
        sem=("parallel",))


def _place():
    x, y, c = lax.axis_index("x"), lax.axis_index("y"), lax.axis_index("c")
    chips = [(1 - x, y), (x, 1 - y), (1 - x, 1 - y)]
    return x, y, c, chips


def _dev_allgather(name, v):
    m_per, n = v.shape

    def body(x_ref, out_ref, send_sems, recv_sems, local_sem):
        x, y, c, _ = _place()
        me = (x, y, c)

        def rows(px, py, pc):
            return out_ref.at[pl.ds((4 * px + 2 * py + pc) * m_per, m_per), :]

        def peer(r):
            return tuple(1 - b if (r >> s) & 1 else b for b, s in zip(me, (2, 1, 0)))

        def copy(r, block, to, src=None):
            return pltpu.make_async_remote_copy(
                src_ref=rows(*block) if src is None else src, dst_ref=rows(*block),
                send_sem=send_sems.at[r - 1], recv_sem=recv_sems.at[r - 1], device_id=to, device_id_type=MESH)

        mine = pltpu.make_async_copy(x_ref, rows(*me), local_sem)
        mine.start()
        sends = [copy(r, me, peer(r), src=x_ref) for r in range(1, N_DEV)]
        for cp in sends:
            cp.start()
        for r in range(1, N_DEV):
            copy(r, peer(r), me).wait_recv()
        for cp in sends:
            cp.wait_send()
        mine.wait()

    return _pcall(
        body, name=name, out_shape=jax.ShapeDtypeStruct((N_DEV * m_per, n), v.dtype),
        in_specs=[pl.BlockSpec(memory_space=pltpu.VMEM)], out_specs=pl.BlockSpec(memory_space=pltpu.VMEM),
        scratch_shapes=[pltpu.SemaphoreType.DMA((7,)), pltpu.SemaphoreType.DMA((7,)), pltpu.SemaphoreType.DMA],
        compiler_params=pltpu.CompilerParams(vmem_limit_bytes=VMEM_LIMIT_V7X),
    )(v)


class _W:
    def __init__(self, name, kind, R, C):
        self.name, self.kind, self.R, self.C = name, kind, R, C

    @property
    def shard_shape(self):
        return (self.R, self.C // N_CHIPS) if self.kind == "col" else (self.R // N_CHIPS, self.C)

    @property
    def half_rows(self):
        return self.shard_shape[0] // 2

    def shard_half(self, ref, half):
        return ref.at[pl.ds(half * self.half_rows, self.half_rows), :]

    def region(self, full_ref, chip, half):
        hr = self.half_rows
        if self.kind == "col":
            cw = self.C // N_CHIPS
            return full_ref.at[pl.ds(half * hr, hr), pl.ds(chip * cw, cw)]
        return full_ref.at[pl.ds(chip * (2 * hr) + half * hr, hr), :]


def _ag_rider(ws, fulls, n_ch=4, chunks=None):
    n_w = len(ws)
    lo, hi = chunks or (0, n_ch)
    per = 6

    def parts(full, sems):
        send_sems, recv_sems = sems
        x, y, c, _ = _place()
        xn, yn, dg = (1 - x, y), (x, 1 - y), (1 - x, 1 - y)
        via = (x + (1 - c) * (1 - 2 * x), y + c * (1 - 2 * y))
        to = (x + c * (1 - 2 * x), y + (1 - c) * (1 - 2 * y))

        def reg(i, chip, half, t):
            nr = ws[i].half_rows // n_ch
            return ws[i].region(full[i], 2 * chip[0] + chip[1], half).at[pl.ds(t * nr, nr), :]

        def copy(r, i, t, k, dev):
            s = (i * (hi - lo) + t - lo) * per + k
            return pltpu.make_async_remote_copy(src_ref=r, dst_ref=r, send_sem=send_sems.at[s],
                                                recv_sem=recv_sems.at[s], device_id=dev, device_id_type=MESH)

        def direct(i, t, k):
            return copy(reg(i, (x, y), c, t), i, t, k, (*(via, to)[k], c))

        def direct_in(i, t, k):
            return copy(reg(i, (via, to)[k], c, t), i, t, k, (*(via, to)[k], c))

        def relay(i, t):
            return copy(reg(i, via, c, t), i, t, 2, (*to, c))

        def relay_in(i, t):
            return copy(reg(i, dg, c, t), i, t, 2, (*to, c))

        def hand(i, t, k, half):
            return copy(reg(i, (xn, yn, dg)[k], half, t), i, t, 3 + k, (x, y, 1 - c))

        return c, direct, direct_in, relay, relay_in, hand

    def start(_, full, sems):
        _, direct, _, _, _, _ = parts(full, sems)
        for t in range(lo, hi):
            for i in range(n_w):
                direct(i, t, 0).start()
                direct(i, t, 1).start()

    def arrived(t):
        def step(_, full, sems):
            c, _, direct_in, relay, relay_in, hand = parts(full, sems)
            for i in range(n_w):
                direct_in(i, t, 0).wait_recv()
                direct_in(i, t, 1).wait_recv()
                relay(i, t).start()
                hand(i, t, 0, c).start()
                hand(i, t, 1, c).start()
        return step

    def finish(_, full, sems):
        c, direct, _, relay, relay_in, hand = parts(full, sems)
        for t in range(lo, hi):
            for i in range(n_w):
                relay_in(i, t).wait_recv()
                hand(i, t, 2, c).start()
        for i in range(n_w):
            for t in range(lo, hi):
                for k in range(3):
                    hand(i, t, k, 1 - c).wait_recv()
        for i in range(n_w):
            for t in range(lo, hi):
                direct(i, t, 0).wait_send()
                direct(i, t, 1).wait_send()
                relay(i, t).wait_send()
                for k in range(3):
                    hand(i, t, k, c).wait_send()

    n_sem = per * (hi - lo) * n_w
    return _Rider(fulls, [jax.ShapeDtypeStruct((w.R, w.C), BF16) for w in ws],
                  [pltpu.SemaphoreType.DMA((n_sem,)), pltpu.SemaphoreType.DMA((n_sem,))], start, finish,
                  steps=[arrived(t) for t in range(lo, hi)], aliases={i: i for i in range(n_w)})


def _cast_into_full(ws, shards, chip_arr, riders=()):
    sr, sc = ws[0].shard_shape
    assert all(w.shard_shape == (sr, sc) for w in ws)
    tr, tc = _tile(sr, 512), _tile(sc, 2048)
    n_r, n_c = sr // tr, sc // tc

    def place(w):
        if w.kind == "col":
            return pl.BlockSpec((tr, tc), lambda i, j, chip: (i, chip[0] * n_c + j))
        return pl.BlockSpec((tr, tc), lambda i, j, chip: (chip[0] * n_r + i, j))

    def body(*refs):
        for a_ref, o_ref in zip(refs[:len(ws)], refs[len(ws):]):
            o_ref[...] = a_ref[...].astype(BF16)

    return _ride("cast_" + "_".join(w.name for w in ws), body, riders, list(shards), grid=(n_r, n_c),
                 in_specs=[pl.BlockSpec((tr, tc), lambda i, j, chip: (i, j))] * len(ws),
                 out_specs=[place(w) for w in ws], out_shape=[jax.ShapeDtypeStruct((w.R, w.C), BF16) for w in ws],
                 scratch_shapes=[], sem=("parallel", "parallel"), scalars=chip_arr)


def _half_view(w, g):
    return g if w.kind == "col" else g.reshape(N_CHIPS, w.R // N_CHIPS, w.C)


def _px_rider(ws, grads):
    n_w = len(ws)

    def copies(g, got, sems):
        send_sems, recv_sems = sems
        x, y, c, _ = _place()

        def half_all(w, ref, half):
            hr = w.half_rows
            if w.kind == "col":
                return ref.at[pl.ds(half * hr, hr), :]
            return ref.at[:, pl.ds(half * hr, hr), :]

        return [pltpu.make_async_remote_copy(
            src_ref=half_all(w, g[i], 1 - c), dst_ref=got[i], send_sem=send_sems.at[i], recv_sem=recv_sems.at[i],
            device_id=(x, y, 1 - c), device_id_type=MESH) for i, w in enumerate(ws)]

    def start(g, got, sems):
        for cp in copies(g, got, sems):
            cp.start()

    def finish(g, got, sems):
        for cp in copies(g, got, sems):
            cp.wait_recv()
            cp.wait_send()

    def got_shape(w):
        hr = w.half_rows
        return (hr, w.C) if w.kind == "col" else (N_CHIPS, hr, w.C)

    return _Rider([_half_view(w, g) for w, g in zip(ws, grads)],
                  [jax.ShapeDtypeStruct(got_shape(w), BF16) for w in ws],
                  [pltpu.SemaphoreType.DMA((n_w,)), pltpu.SemaphoreType.DMA((n_w,))], start, finish)


def _pair_sum(w, g, got, c_arr):
    hr = w.half_rows
    if w.kind == "col":
        tr, tc = _tile(hr, 512), _tile(w.C, 2048)
        n_r = hr // tr
        grid = (n_r, w.C // tc)
        g_spec = pl.BlockSpec((tr, tc), lambda i, j, c: (c[0] * n_r + i, j))
        o_spec = pl.BlockSpec((tr, tc), lambda i, j, c: (i, j))
    else:
        tr = _tile(hr, 512)
        n_r = hr // tr
        grid = (N_CHIPS, n_r)
        g_spec = pl.BlockSpec((1, tr, w.C), lambda s, i, c: (s, c[0] * n_r + i, 0))
        o_spec = pl.BlockSpec((1, tr, w.C), lambda s, i, c: (s, i, 0))

    def body(c_ref, g_ref, got_ref, out_ref):
        out_ref[...] = (g_ref[...].astype(F32) + got_ref[...].astype(F32)).astype(BF16)

    return _pcall(
        body, name="grad_pair_sum_" + w.name, out_shape=jax.ShapeDtypeStruct(got.shape, BF16),
        grid_spec=pltpu.PrefetchScalarGridSpec(num_scalar_prefetch=1, grid=grid, in_specs=[g_spec, o_spec],
                                               out_specs=o_spec),
        compiler_params=_params(("parallel", "parallel")),
    )(c_arr, _half_view(w, g), got)


def _chip_sum(w, p, q, cc_arr):
    hr, cols = w.half_rows, w.shard_shape[1]
    tr, tc = _tile(hr, 512), _tile(cols, 2048)
    n_r, n_c = hr // tr, cols // tc

    def body(cc_ref, own, q1, q2, q3, out_ref):
        own_v = own[...] if w.kind == "col" else own[0]
        out_ref[...] = ((own_v.astype(F32) + q1[0].astype(F32)) + q2[0].astype(F32)) + q3[0].astype(F32)

    if w.kind == "col":
        own_spec = pl.BlockSpec((tr, tc), lambda i, j, cc: (i, cc[1] * n_c + j))
    else:
        own_spec = pl.BlockSpec((1, tr, tc), lambda i, j, cc: (cc[1], i, j))
    q_specs = [pl.BlockSpec((1, tr, tc), lambda i, j, cc, s=s: ((cc[1] + s) % N_CHIPS, i, j)) for s in (1, 2, 3)]
    return _pcall(
        body, name="grad_chip_sum_" + w.name, out_shape=jax.ShapeDtypeStruct(w.shard_shape, F32),
        grid_spec=pltpu.PrefetchScalarGridSpec(
            num_scalar_prefetch=1, grid=(n_r, n_c), in_specs=[own_spec] + q_specs,
            out_specs=pl.BlockSpec((tr, tc), lambda i, j, cc: (cc[0] * n_r + i, j))),
        compiler_params=_params(("parallel", "parallel")),
    )(cc_arr, p, q, q, q)


_SEM = pl.BlockSpec(memory_space=pltpu.SEMAPHORE)
_HBM = pl.BlockSpec(memory_space=pltpu.HBM)


def _split_copies(kind, ws, p, land, send_sems, recv_sems):
    x, y, c, chips = _place()
    my_chip = 2 * x + y
    pairs = []
    for i, w in enumerate(ws):
        if kind == "pair":
            hr = w.half_rows
            src = p[i].at[pl.ds((1 - c) * hr, hr), :] if w.kind == "col" else p[i].at[:, pl.ds((1 - c) * hr, hr), :]
            cp = pltpu.make_async_remote_copy(src_ref=src, dst_ref=land[i], send_sem=send_sems.at[i],
                                              recv_sem=recv_sems.at[i], device_id=(x, y, 1 - c), device_id_type=MESH)
            pairs.append((cp, cp))
            continue
        for k, chip in enumerate(chips):
            to_chip = 2 * chip[0] + chip[1]
            src = p[i].at[:, pl.ds(to_chip * (w.C // N_CHIPS), w.C // N_CHIPS)] if w.kind == "col" else p[i].at[to_chip]
            kw = dict(send_sem=send_sems.at[3 * i + k], recv_sem=recv_sems.at[3 * i + k], device_id=(*chip, c),
                      device_id_type=MESH)
            pairs.append((pltpu.make_async_remote_copy(src_ref=src, dst_ref=land[i].at[my_chip], **kw),
                          pltpu.make_async_remote_copy(src_ref=src, dst_ref=land[i].at[to_chip], **kw)))
    return pairs


def _split_start(name, kind, ws, arrays):
    n_w = len(ws)
    if kind == "pair":
        arrays = [_half_view(w, g) for w, g in zip(ws, arrays)]
        lands = [lax.empty((w.half_rows, w.C) if w.kind == "col" else (N_CHIPS, w.half_rows, w.C), BF16) for w in ws]
    else:
        lands = [lax.empty((N_CHIPS, w.half_rows, w.shard_shape[1]), BF16) for w in ws]
    n_sem = n_w if kind == "pair" else 3 * n_w

    def body(*refs):
        p, land = refs[:n_w], refs[n_w:2 * n_w]
        for out, _ in _split_copies(kind, ws, p, land, refs[2 * n_w], refs[2 * n_w + 1]):
            out.start()
        refs[-1][...] = jnp.zeros_like(refs[-1])

    arrays = [pltpu.with_memory_space_constraint(a, pltpu.HBM) for a in list(arrays) + lands]
    res = _pcall(
        body, name=name,
        out_shape=(pltpu.SemaphoreType.DMA((n_sem,)), pltpu.SemaphoreType.DMA((n_sem,)),
                   *[pltpu.HBM(a.shape, a.dtype) for a in arrays], jax.ShapeDtypeStruct((SUBLANES, LANES), F32)),
        in_specs=[_HBM] * (2 * n_w),
        out_specs=(_SEM, _SEM, *[_HBM] * (2 * n_w), pl.BlockSpec(memory_space=pltpu.VMEM)),
        input_output_aliases={i: 2 + i for i in range(2 * n_w)},
        compiler_params=pltpu.CompilerParams(has_side_effects=pltpu.SideEffectType.DATAFLOW_SIDE_EFFECTING),
    )(*arrays)
    return (kind, ws, res[0], res[1], list(res[2:2 + n_w]), list(res[2 + n_w:2 + 2 * n_w])), res[-1]


def _split_wait(name, flight, after):
    kind, ws, send_sems, recv_sems, arrays, lands = flight
    n_w = len(ws)

    def body(*refs):
        p, land = refs[:n_w], refs[n_w:2 * n_w]
        for _, cp in _split_copies(kind, ws, p, land, refs[2 * n_w], refs[2 * n_w + 1]):
            cp.wait_send()
            cp.wait_recv()

    res = _pcall(
        body, name=name,
        out_shape=[pltpu.HBM(a.shape, a.dtype) for a in list(arrays) + list(lands)],
        in_specs=[_HBM] * (2 * n_w) + [_SEM, _SEM] + [ANY] * len(after), out_specs=[_HBM] * (2 * n_w),
        input_output_aliases={i: i for i in range(2 * n_w)},
        compiler_params=pltpu.CompilerParams(has_side_effects=pltpu.SideEffectType.DATAFLOW_SIDE_EFFECTING),
    )(*arrays, *lands, send_sems, recv_sems, *after)
    return list(res[:n_w]), list(res[n_w:])


def _sf_rider(ws, grads):
    n_w = len(ws)

    def copy(g, sems, i, half):
        send_sems, recv_sems = sems
        x, y, c, _ = _place()
        h = c if half == "mine" else 1 - c
        reg = ws[i].shard_half(g[i], h)
        return pltpu.make_async_remote_copy(src_ref=reg, dst_ref=reg, send_sem=send_sems.at[i], recv_sem=recv_sems.at[i],
                                            device_id=(x, y, 1 - c), device_id_type=MESH)

    def start(_, g, sems):
        for i in range(n_w):
            copy(g, sems, i, "mine").start()

    def finish(_, g, sems):
        for i in range(n_w):
            copy(g, sems, i, "other").wait_recv()
            copy(g, sems, i, "mine").wait_send()

    return _Rider(grads, [jax.ShapeDtypeStruct(w.shard_shape, F32) for w in ws],
                  [pltpu.SemaphoreType.DMA((n_w,)), pltpu.SemaphoreType.DMA((n_w,))], start, finish,
                  aliases={i: i for i in range(n_w)})


def _adamw_math(w, g, m, v):
    m = ADAM_B1 * m + (1.0 - ADAM_B1) * g
    v = ADAM_B2 * v + (1.0 - ADAM_B2) * (g * g)
    m_hat = m / (1.0 - ADAM_B1 ** ADAM_STEP)
    v_hat = v / (1.0 - ADAM_B2 ** ADAM_STEP)
    delta = -ADAM_LR * (m_hat / (jnp.sqrt(v_hat) + ADAM_EPS) + ADAM_WD * w)
    return delta, m, v


def _adamw(name, w, g, m, v, after=None):
    R, C = w.shape
    tr, tc = _tile(R, 256), _tile(C, 2048)
    behind = [] if after is None else [after]

    def body(w_ref, g_ref, m_ref, v_ref, *rest):
        g_out, d_out, m_out, v_out = rest[len(behind):]
        g = g_ref[...]
        g_out[...] = g
        d_out[...], m_out[...], v_out[...] = _adamw_math(w_ref[...], g, m_ref[...], v_ref[...])

    spec = pl.BlockSpec((tr, tc), lambda i, j: (i, j))
    sh = jax.ShapeDtypeStruct((R, C), F32)
    return _pcall(body, name=name, grid=(R // tr, C // tc), in_specs=[spec] * 4 + [ANY] * len(behind),
                  out_specs=[spec] * 4, out_shape=[sh] * 4, compiler_params=_params(("parallel", "parallel")))(
                      w, g, m, v, *behind)


def _ada_update(sct, dmod_sh, w, m, v, riders=()):
    R, C = w.shape
    tr, tc = _tile(R, 512), _tile(C, 1024)

    def body(s_ref, d_ref, w_ref, m_ref, v_ref, g_out, d_out, m_out, v_out):
        s, d = s_ref[...], d_ref[...]
        g = s[:, 0:1] * d[0:1, :]
        for b in range(1, N_DEV):
            g += s[:, b:b + 1] * d[b:b + 1, :]
        g_out[...] = g
        d_out[...], m_out[...], v_out[...] = _adamw_math(w_ref[...], g, m_ref[...], v_ref[...])

    spec = pl.BlockSpec((tr, tc), lambda i, j: (i, j))
    sh = jax.ShapeDtypeStruct((R, C), F32)
    return _ride(
        "ada_update", body, riders, [sct, dmod_sh, w, m, v], grid=(R // tr, C // tc),
        in_specs=[pl.BlockSpec((tr, N_DEV), lambda i, j: (i, 0)), pl.BlockSpec((N_DEV, tc), lambda i, j: (0, j)),
                  spec, spec, spec],
        out_specs=[spec] * 4, out_shape=[sh] * 4, scratch_shapes=[], sem=("parallel", "parallel"))


def _silu_rows(c_row):
    D = c_row.shape[1]

    def body(c_ref, o_ref):
        cv = c_ref[...]
        o_ref[...] = cv * jax.nn.sigmoid(cv)

    return _pcall(body, name="silu_c", out_shape=jax.ShapeDtypeStruct((1, D), F32))(c_row)


def _pack_partials(parts, widths, total):
    n = len(widths)

    def body(*refs):
        loss_p, out_ref = refs[n], refs[n + 1]
        off = 0
        for ref, wd in zip(refs[:n], widths):
            out_ref[:, off:off + wd] = jnp.sum(ref[...], axis=0)
            off += wd
        loss = jnp.sum(jnp.sum(loss_p[...], axis=0), axis=1, keepdims=True)
        out_ref[:, off:off + LANES] = jnp.broadcast_to(loss, (1, LANES))
        if off + LANES < total:
            out_ref[:, off + LANES:total] = jnp.zeros((1, total - off - LANES), F32)

    return _pcall(body, name="pack_partials", out_shape=jax.ShapeDtypeStruct((1, total), F32))(*parts)


def _small_update(gathered, offsets, params, loss_off):
    n_p = len(params)

    def over_devices(g_ref, off, wd):
        blk = g_ref[:, off:off + wd]
        g = blk[0:1, :]
        for b in range(1, N_DEV):
            g = g + blk[b:b + 1, :]
        return g

    def body(*refs):
        g_ref = refs[0]
        prm = refs[1:1 + 3 * n_p]
        outs = refs[1 + 3 * n_p:]
        outs[4 * n_p][...] = over_devices(g_ref, loss_off, LANES)
        for i, (off, wd) in enumerate(offsets):
            g = over_devices(g_ref, off, wd)
            w, m, v = prm[3 * i][...], prm[3 * i + 1][...], prm[3 * i + 2][...]
            outs[4 * i][...] = g
            outs[4 * i + 1][...], outs[4 * i + 2][...], outs[4 * i + 3][...] = _adamw_math(w, g, m, v)

    flat = [a for t in params for a in t]
    out_shape = [jax.ShapeDtypeStruct(t[0].shape, F32) for t in params for _ in range(4)]
    out_shape.append(jax.ShapeDtypeStruct((1, LANES), F32))
    return _pcall(body, name="small_update", out_shape=out_shape)(gathered, *flat)


def kernel(x, c, w_ada, b_ada, norm1_w, w_in, q_norm_w, k_norm_w, w_pool, pool_scale, w_a_up, w_b_up, w_o, norm2_w, w_ff1, w_ff2, loss_target, m_w_ada, m_b_ada, m_norm1_w, m_w_in, m_q_norm_w, m_k_norm_w, m_w_pool, m_pool_scale, m_w_a_up, m_w_b_up, m_w_o, m_norm2_w, m_w_ff1, m_w_ff2, v_w_ada, v_b_ada, v_norm1_w, v_w_in, v_q_norm_w, v_k_norm_w, v_w_pool, v_pool_scale, v_w_a_up, v_w_b_up, v_w_o, v_norm2_w, v_w_ff1, v_w_ff2):
    _, S, D = x.shape
    PW = D // 2
    H = PW // HEAD_DIM
    cg = PW // N_GROUPS
    IN = w_in.shape[2] * N_CHIPS
    FF = w_ff1.shape[2] * N_CHIPS
    A_COLS = w_ada.shape[2]
    xi, yi, ci = lax.axis_index("x"), lax.axis_index("y"), lax.axis_index("c")
    chip = 2 * xi + yi
    dev = 2 * chip + ci
    c_arr = jnp.reshape(ci, (1,)).astype(jnp.int32)
    x2, tgt = x[0], loss_target[0]

    ws = [_W("w_in", "col", D, IN), _W("w_pool", "row", PW, cg), _W("w_a_up", "col", PW, D),
          _W("w_b_up", "col", PW, D), _W("w_o", "row", D, D), _W("w_ff1", "col", D, FF), _W("w_ff2", "row", FF, D)]
    w32 = [w_in[0], w_pool[0].reshape(cg, cg), w_a_up[0], w_b_up[0], w_o[0], w_ff1[0], w_ff2[0]]
    m32 = [m_w_in[0], m_w_pool[0].reshape(cg, cg), m_w_a_up[0], m_w_b_up[0], m_w_o[0], m_w_ff1[0], m_w_ff2[0]]
    v32 = [v_w_in[0], v_w_pool[0].reshape(cg, cg), v_w_a_up[0], v_w_b_up[0], v_w_o[0], v_w_ff1[0], v_w_ff2[0]]

    W_IN, W_POOL, W_A, W_B, W_O, W_FF1, W_FF2 = ws
    chip_arr = jnp.reshape(chip, (1,)).astype(jnp.int32)
    cc_arr = jnp.stack([ci, chip]).astype(jnp.int32)
    s_in, s_pool, s_a, s_b, s_o = [_cast_into_full([w], [a], chip_arr)[0] for w, a in zip(ws[:5], w32[:5])]
    (s_ff1, s_ff2), ((win_f,),) = _cast_into_full([W_FF1, W_FF2], w32[5:], chip_arr, riders=[_ag_rider([W_IN], [s_in])])

    sc_row = _silu_rows(c)
    sc_all = _dev_allgather("gather_silu_c", sc_row.reshape(SUBLANES, D // SUBLANES)).reshape(N_DEV, D)
    sc16 = jnp.concatenate([sc_all, jnp.zeros_like(sc_all)], axis=0)
    b_cols = lax.dynamic_slice(b_ada, (0, chip * A_COLS), (1, A_COLS))
    (mod_cols,) = _mm("mod_cols", [(sc16, w_ada[0])], M=2 * N_DEV, N=A_COLS, K=D, tm=16, tn=1024, tk=1024,
                      a_pro=lambda a: a.astype(BF16), b_pro=lambda b: b.astype(BF16),
                      extras=[(b_cols, "row", 0)], outs=[_tile_out(F32)], epi=lambda accs, ex: [accs[0] + ex[0]])
    mod_all = _dev_allgather("gather_mod", mod_cols[:N_DEV]).reshape(N_CHIPS, 2, N_DEV, A_COLS)
    mod_row = lax.dynamic_index_in_dim(mod_all[:, 0], dev, axis=1, keepdims=False).reshape(1, N_CHIPS * A_COLS)
    shift1, scale1, gate1, shift2, scale2, gate2 = [mod_row[:, i * D:(i + 1) * D] for i in range(6)]

    WIDE = dict(tm=2048, tn=512, tk=2048)
    DEEP = dict(tm=1024, tn=1024, tk=2048)
    DEEPER = dict(tm=1024, tn=1024, tk=4096)
    h = _norm_mod("norm1_mod", x2, norm1_w, scale1, shift1)
    (proj,), ((wpool_f, wa_f, wb_f, wo_f),) = _mm(
        "in_proj", [(h, win_f)], M=S, N=IN, K=D, outs=[_tile_out(BF16)], epi=lambda accs, ex: [accs[0]], **WIDE,
        riders=[_ag_rider([W_POOL, W_A, W_B, W_O], [s_pool, s_a, s_b, s_o], n_ch=2)])
    pooled, pa = _pool_fwd(proj, wpool_f, pool_scale, S, PW)
    (att, attf), ((wff1_f,),) = _attn_fwd(proj, q_norm_w, k_norm_w, S, H, PW // HEAD_DIM,
                                          riders=[_ag_rider([W_FF1], [s_ff1])])

    def merge_epi(accs, ex):
        sa, sb = jax.nn.sigmoid(ex[0].astype(F32)), jax.nn.sigmoid(ex[1].astype(F32))
        return [sa * accs[0] + sb * accs[1], accs[0], accs[1]]

    (merged, ya, yb), (ff2_a,) = _mm("branch_up_merge", [(pa, wa_f), (att, wb_f)], M=S, N=D, K=PW,
                                     extras=[(proj, "tile", 4 * PW), (proj, "tile", 4 * PW + D)],
                                     outs=[_tile_out(BF16)] * 3, epi=merge_epi,
                                     riders=[_ag_rider([W_FF2], [s_ff2], chunks=(0, 1))])
    (x1, o), (ff2_b,) = _mm("out_proj", [(merged, wo_f)], M=S, N=D, K=D, extras=[(x2, "tile", 0), (gate1, "row", 0)],
                            outs=[_tile_out(F32), _tile_out(BF16)], epi=lambda accs, ex: [ex[0] + ex[1] * accs[0], accs[0]],
                            riders=[_ag_rider([W_FF2], ff2_a, chunks=(1, 2))], **WIDE)
    h2 = _norm_mod("norm2_mod", x1, norm2_w, scale2, shift2)
    (rl,), ((wff2_f,),) = _mm("ff1", [(h2, wff1_f)], M=S, N=FF, K=D, outs=[_tile_out(BF16)], **WIDE,
                              epi=lambda accs, ex: [jnp.maximum(accs[0], 0.0)],
                              riders=[_ag_rider([W_FF2], ff2_b, chunks=(2, 4))])

    def square(a):
        af = a.astype(F32)
        return (af * af).astype(BF16)

    def loss_epi(accs, ex):
        x1_t, tgt_t, g2 = ex
        f = accs[0]
        diff = (x1_t + g2 * f) - tgt_t
        dy = diff * (1.0 / D)
        return [dy, dy * g2, _colsum(dy * f), _colsum(diff * diff)]

    dy, df, dgate2_p, loss_p = _mm("ff2_loss", [(rl, wff2_f)], M=S, N=D, K=FF, a_pro=square, **DEEP,
                                   extras=[(x1, "tile", 0), (tgt, "tile", 0), (gate2, "row", 0)],
                                   outs=[_tile_out(F32), _tile_out(BF16), _COLSUM, _COLSUM], epi=loss_epi)

    tied = []

    def behind(token, a):
        a, token = lax.optimization_barrier((a, token))
        tied.append(token)
        return a

    def pair_sums(group, partials, got):
        return [_pair_sum(w, g, r, c_arr) for w, g, r in zip(group, partials, got)]

    def chip_sums(group, sums, from_chips):
        return [_chip_sum(w, p, q, cc_arr) for w, p, q in zip(group, sums, from_chips)]

    first = lambda accs, ex: [accs[0]]
    gmm = dict(ta=True, outs=[_tile_out(BF16)], epi=first, **WIDE)
    (g_ff2,) = _mm("grad_w_ff2", [(rl, df)], M=FF, N=D, K=S, a_pro=square, ta=True, tm=512, tn=2048, tk=2048,
                   outs=[_tile_out(BF16)], epi=first)
    flight, token = _split_start("pair_w_ff2_start", "pair", [W_FF2], [g_ff2])
    (dz1,) = _mm("d_ff_hidden", [(behind(token, df), wff2_f)], M=S, N=FF, K=D, tb=True, extras=[(rl, "tile", 0)],
                 outs=[_tile_out(BF16)], epi=lambda accs, ex: [accs[0] * (2.0 * ex[0].astype(F32))], **WIDE)
    sum_ff2 = pair_sums([W_FF2], *_split_wait("pair_w_ff2_wait", flight, after=[dz1] + tied))
    chip_ff2, token = _split_start("chip_w_ff2_start", "chip", [W_FF2], sum_ff2)
    (g_ff1,) = _mm("grad_w_ff1", [(behind(token, h2), dz1)], M=D, N=FF, K=S, **gmm)
    flight, token = _split_start("pair_w_ff1_start", "pair", [W_FF1], [g_ff1])
    (dh2,) = _mm("d_h2", [(behind(token, dz1), wff1_f)], M=S, N=D, K=FF, tb=True, outs=[_tile_out(F32)], epi=first,
                 **DEEPER)
    sum_ff1 = pair_sums([W_FF1], *_split_wait("pair_w_ff1_wait", flight, after=[dh2] + tied))
    chip_ff1, token = _split_start("chip_w_ff1_start", "chip", [W_FF1], sum_ff1)
    dx1, dshift2_p, dscale2_p, gn2_p, do, dgate1_p = _norm_mod_bwd("norm2_bwd", behind(token, dh2), x1, dy, norm2_w, scale2,
                                                                   gate_o=(o, gate1))
    (g_wo,) = _mm("grad_w_o", [(merged, do)], M=D, N=D, K=S, **gmm)

    def gate_epi(accs, ex):
        dm = accs[0]
        sa, sb = jax.nn.sigmoid(ex[0].astype(F32)), jax.nn.sigmoid(ex[1].astype(F32))
        ya_t, yb_t = ex[2].astype(F32), ex[3].astype(F32)
        return [dm * sa, dm * sb, dm * ya_t * (sa * (1.0 - sa)), dm * yb_t * (sb * (1.0 - sb))]

    dya, dyb, dga, dgb = _mm("d_merged", [(do, wo_f)], M=S, N=D, K=D, tb=True, tm=1024, tn=512, tk=2048,
                             extras=[(proj, "tile", 4 * PW), (proj, "tile", 4 * PW + D), (ya, "tile", 0), (yb, "tile", 0)],
                             outs=[_tile_out(BF16)] * 4, epi=gate_epi)
    both = lambda accs, ex: [accs[0], accs[1]]
    g_wa, g_wb = _mm("grad_w_up", [(pa, dya), (att, dyb)], M=PW, N=D, K=S, ta=True, outs=[_tile_out(BF16)] * 2, epi=both,
                     **WIDE)
    mid = [W_A, W_B, W_O]
    flight, token = _split_start("pair_mid_start", "pair", mid, [g_wa, g_wb, g_wo])
    dpa, datt = _mm("d_branches", [(dya, wa_f), (behind(token, dyb), wb_f)], M=S, N=PW, K=D, tb=True,
                    outs=[_tile_out(F32), _tile_out(BF16)], epi=both, tm=1024, tn=512, tk=2048)
    sum_mid = pair_sums(mid, *_split_wait("pair_mid_wait", flight, after=[datt] + tied))
    chip_mid, token = _split_start("chip_mid_start", "chip", mid, sum_mid)
    du, g_wpool4, gscale_p = _pool_bwd(dpa, pooled, wpool_f, pool_scale, S, PW)
    dq, dk, dv, gq_p, gk_p = _attn_bwd(proj, behind(token, datt), attf, q_norm_w, k_norm_w, S, H, PW // HEAD_DIM)
    dproj = jnp.concatenate([du, dq, dk, dv, dga, dgb], axis=1)
    early = [W_FF1, W_FF2]
    sum_ff1, q_ff1 = _split_wait("chip_w_ff1_wait", chip_ff1, after=[dq] + tied)
    sum_ff2, q_ff2 = _split_wait("chip_w_ff2_wait", chip_ff2, after=[dq] + tied)
    halves_early = chip_sums(early, sum_ff1 + sum_ff2, q_ff1 + q_ff2)
    (g_win,), (grads_early,) = _mm("grad_w_in", [(h, dproj)], M=D, N=IN, K=S, riders=[_sf_rider(early, halves_early)],
                                   **gmm)
    last = [W_IN, W_POOL]
    g_last = [g_win, g_wpool4.reshape(PW, cg)]
    sum_mid, q_mid = _split_wait("chip_mid_wait", chip_mid, after=[g_win] + tied)
    halves_mid = chip_sums(mid, sum_mid, q_mid)
    (dh,), (got_last, grads_mid) = _mm("d_h", [(dproj, win_f)], M=S, N=D, K=IN, tb=True, outs=[_tile_out(F32)], epi=first,
                                       riders=[_px_rider(last, g_last), _sf_rider(mid, halves_mid)], **DEEPER)
    sum_last = pair_sums(last, g_last, got_last)
    grad_x, dshift1_p, dscale1_p, gn1_p = _norm_mod_bwd("norm1_bwd", dh, x2, dx1, norm1_w, scale1)

    parts = [dshift1_p, dscale1_p, dgate1_p, dshift2_p, dscale2_p, dgate2_p, gn1_p, gn2_p,
             gscale_p.reshape(1, 1, PW), gq_p, gk_p]
    widths = [D] * 8 + [PW, HEAD_DIM, HEAD_DIM]
    used = sum(widths)
    P = -(-(used + LANES) // (SUBLANES * LANES)) * (SUBLANES * LANES)
    packed = _pack_partials(parts + [loss_p], widths, P)
    gathered = _dev_allgather("gather_vector_grads", packed.reshape(SUBLANES, P // SUBLANES)).reshape(N_DEV, P)
    sum_last, gathered = lax.optimization_barrier((sum_last, gathered))
    chip_last, token = _split_start("chip_last_start", "chip", last, sum_last)
    small = [(b_ada, m_b_ada, v_b_ada), (norm1_w, m_norm1_w, v_norm1_w), (norm2_w, m_norm2_w, v_norm2_w),
             (pool_scale, m_pool_scale, v_pool_scale), (q_norm_w, m_q_norm_w, v_q_norm_w),
             (k_norm_w, m_k_norm_w, v_k_norm_w)]
    offsets = [(0, 6 * D), (6 * D, D), (7 * D, D), (8 * D, PW), (8 * D + PW, HEAD_DIM), (8 * D + PW + HEAD_DIM, HEAD_DIM)]
    su = _small_update(gathered, offsets, small, used)
    (g_b, d_b, nm_b, nv_b, g_n1, d_n1, nm_n1, nv_n1, g_n2, d_n2, nm_n2, nv_n2, g_ps, d_ps, nm_ps, nv_ps,
     g_qn, d_qn, nm_qn, nv_qn, g_kn, d_kn, nm_kn, nv_kn, loss_sum) = su
    dmod_sh = lax.dynamic_slice(gathered, (0, chip * A_COLS), (N_DEV, A_COLS))
    dmod_sh, token = lax.optimization_barrier((dmod_sh, token))
    g_ada, d_ada, nm_ada, nv_ada = _ada_update(sc_all.T, dmod_sh, w_ada[0], m_w_ada[0], v_w_ada[0])

    upd_done = [_adamw("adamw_" + w.name, a, g, m, v, after=token)
                for w, a, g, m, v in zip(ws[2:], w32[2:], list(grads_mid) + list(grads_early), m32[2:], v32[2:])]

    sum_last, q_last = _split_wait("chip_last_wait", chip_last, after=[nv_ada] + [u[3] for u in upd_done])
    halves_last = chip_sums(last, sum_last, q_last)
    filled = _run_rider("grad_sibling_fill", _sf_rider(last, halves_last))
    upd = [_adamw("adamw_" + w.name, a, g, m, v) for w, a, g, m, v in zip(ws[:2], w32[:2], filled, m32[:2], v32[:2])]
    upd += upd_done

    loss = (0.5 / D) * loss_sum[0, 0]

    def up(a):
        return a[None]

    def pool4(a):
        return a.reshape(1, N_GROUPS, cg // N_CHIPS, cg)

    (gr_win, d_win, nm_win, nv_win), (gr_wp, d_wp, nm_wp, nv_wp), (gr_wa, d_wa, nm_wa, nv_wa), \
        (gr_wb, d_wb, nm_wb, nv_wb), (gr_wo, d_wo, nm_wo, nv_wo), (gr_f1, d_f1, nm_f1, nv_f1), \
        (gr_f2, d_f2, nm_f2, nv_f2) = upd
    return (
        loss, grad_x[None],
        up(g_ada), g_b, g_n1, up(gr_win), g_qn, g_kn, pool4(gr_wp), g_ps, up(gr_wa), up(gr_wb), up(gr_wo), g_n2,
        up(gr_f1), up(gr_f2),
        up(d_ada), d_b, d_n1, up(d_win), d_qn, d_kn, pool4(d_wp), d_ps, up(d_wa), up(d_wb), up(d_wo), d_n2,
        up(d_f1), up(d_f2),
        up(nm_ada), nm_b, nm_n1, up(nm_win), nm_qn, nm_kn, pool4(nm_wp), nm_ps, up(nm_wa), up(nm_wb), up(nm_wo), nm_n2,
        up(nm_f1), up(nm_f2),
        up(nv_ada), nv_b, nv_n1, up(nv_win), nv_qn, nv_kn, pool4(nv_wp), nv_ps, up(nv_wa), up(nv_wb), up(nv_wo), nv_n2,
        up(nv_f1), up(nv_f2),
    )
```

```python
import functools
import math

import jax
import jax.numpy as jnp
from jax import lax
from jax.experimental import pallas as pl
from jax.experimental.pallas import tpu as pltpu

F32 = jnp.float32
BF16 = jnp.bfloat16
MESH = pl.DeviceIdType.MESH
ANY = pl.BlockSpec(memory_space=pl.ANY)

EPS = 1e-6
HEAD_DIM = 128
LANES, SUBLANES = 128, 8
POOL_WINDOWS = (2, 4, 8, 16)
N_GROUPS = len(POOL_WINDOWS)
assert POOL_WINDOWS == tuple(2 << g for g in range(N_GROUPS))
N_CHIPS = 4
N_DEV = 8
ADAM_LR, ADAM_B1, ADAM_B2, ADAM_EPS, ADAM_WD, ADAM_STEP = 0.001, 0.9, 0.999, 1e-08, 0.01, 10
VMEM_LIMIT_V7X = 56 * 1024 * 1024
ATT_T = 256
ATT_GROUP = 8
POOL_T = 256


def _pcall(body, **kw):
    return pl.pallas_call(body, **kw)


def _params(sem=None):
    return pltpu.CompilerParams(dimension_semantics=sem, vmem_limit_bytes=VMEM_LIMIT_V7X)


def _tile(n, pref):
    if n <= pref:
        return n
    t = pref
    while n % t:
        t //= 2
    return t


class _Rider:
    def __init__(self, arrays, out_shape, sems, start, finish, aliases=None, steps=(), lead=4):
        self.arrays, self.out_shape, self.sems = list(arrays), list(out_shape), list(sems)
        self.start, self.finish, self.aliases, self.steps, self.lead = start, finish, aliases or {}, list(steps), lead


def _ride(name, body, riders, arrays, *, grid, in_specs, out_specs, out_shape, scratch_shapes, sem, scalars=None):
    n_in, n_out, n_scr = len(arrays), len(out_shape), len(scratch_shapes)
    r_arrays = [a for r in riders for a in r.arrays]
    r_outs = [o for r in riders for o in r.out_shape]
    r_sems = [s for r in riders for s in r.sems]
    n_hooks = max([len(r.steps) for r in riders], default=0)
    lead = min([r.lead for r in riders if r.steps], default=4)
    total = math.prod(grid)
    aliases, off_i, off_o = {}, n_in + (scalars is not None), n_out
    for r in riders:
        for a, o in r.aliases.items():
            aliases[off_i + a] = off_o + o
        off_i += len(r.arrays)
        off_o += len(r.out_shape)

    def full(*refs):
        p = 0
        groups = []
        for n in (n_in, len(r_arrays), n_out, len(r_outs), n_scr, len(r_sems)):
            groups.append(refs[p:p + n])
            p += n
        ins, rin, outs, rout, scr, rsem = groups

        def each(what):
            a = o = s = 0
            for r in riders:
                fn = what(r)
                if fn is not None:
                    fn(rin[a:a + len(r.arrays)], rout[o:o + len(r.out_shape)], rsem[s:s + len(r.sems)])
                a, o, s = a + len(r.arrays), o + len(r.out_shape), s + len(r.sems)

        if riders:
            lin = 0
            for d, g in enumerate(grid):
                lin = lin * g + pl.program_id(d)
            pl.when(lin == 0)(lambda: each(lambda r: r.start))
            for t in range(n_hooks):
                pl.when(lin == min(total - 1, ((4 * t + lead) * total) // (4 * n_hooks)))(
                    lambda t=t: each(lambda r: r.steps[t] if t < len(r.steps) else None))
        body(*ins, *outs, *scr)
        if riders:
            pl.when(lin == total - 1)(lambda: each(lambda r: r.finish))

    specs = dict(grid=grid, in_specs=list(in_specs) + [ANY] * len(r_arrays),
                 out_specs=list(out_specs) + [ANY] * len(r_outs), scratch_shapes=list(scratch_shapes) + r_sems)
    common = dict(name=name, out_shape=list(out_shape) + r_outs, input_output_aliases=aliases,
                  compiler_params=_params(("arbitrary",) * len(grid) if riders else sem))
    if scalars is None:
        res = _pcall(full, **specs, **common)(*arrays, *r_arrays)
    else:
        res = _pcall(lambda _, *refs: full(*refs), **common,
                     grid_spec=pltpu.PrefetchScalarGridSpec(num_scalar_prefetch=1, **specs))(scalars, *arrays, *r_arrays)
    if not riders:
        return res
    main, rest, per = res[:n_out], res[n_out:], []
    for r in riders:
        per.append(rest[:len(r.out_shape)])
        rest = rest[len(r.out_shape):]
    return main, per


def _run_rider(name, rider):
    def body(*refs):
        n_a, n_o = len(rider.arrays), len(rider.out_shape)
        ins, outs, sems = refs[:n_a], refs[n_a:n_a + n_o], refs[n_a + n_o:]
        for fn in [rider.start] + rider.steps + [rider.finish]:
            fn(ins, outs, sems)

    return _pcall(body, name=name, out_shape=rider.out_shape, in_specs=[ANY] * len(rider.arrays),
                  out_specs=[ANY] * len(rider.out_shape), scratch_shapes=rider.sems,
                  input_output_aliases=rider.aliases)(*rider.arrays)


def _mm(name, pairs, *, M, N, K, ta=False, tb=False, tm=512, tn=1024, tk=1024,
        a_pro=None, b_pro=None, extras=(), outs, epi, riders=()):
    tm, tn, tk = _tile(M, tm), _tile(N, tn), _tile(K, tk)
    n_i, n_j, n_k = M // tm, N // tn, K // tk
    n_p, n_e = len(pairs), len(extras)
    arrays, in_specs = [], []
    for a, _ in pairs:
        arrays.append(a)
        in_specs.append(pl.BlockSpec((tk, tm), lambda i, j, k: (k, i)) if ta
                        else pl.BlockSpec((tm, tk), lambda i, j, k: (i, k)))
    for _, b in pairs:
        arrays.append(b)
        in_specs.append(pl.BlockSpec((tn, tk), lambda i, j, k: (j, k)) if tb
                        else pl.BlockSpec((tk, tn), lambda i, j, k: (k, j)))
    for arr, kind, off in extras:
        ob = off // tn
        assert off % tn == 0
        arrays.append(arr)
        if kind == "tile":
            in_specs.append(pl.BlockSpec((tm, tn), lambda i, j, k, ob=ob: (i, j + ob)))
        else:
            in_specs.append(pl.BlockSpec((1, tn), lambda i, j, k, ob=ob: (0, j + ob)))
    out_shape, out_specs = [], []
    for o in outs:
        if o["kind"] == "tile":
            out_shape.append(jax.ShapeDtypeStruct((M, N), o["dtype"]))
            out_specs.append(pl.BlockSpec((tm, tn), lambda i, j, k: (i, j)))
        else:
            out_shape.append(jax.ShapeDtypeStruct((n_i, 1, N), F32))
            out_specs.append(pl.BlockSpec((1, 1, tn), lambda i, j, k: (i, 0, j)))
    dims = (((0 if ta else 1,), (1 if tb else 0,)), ((), ()))

    def body(*refs):
        a_refs, b_refs = refs[:n_p], refs[n_p:2 * n_p]
        e_refs = refs[2 * n_p:2 * n_p + n_e]
        o_refs = refs[2 * n_p + n_e:2 * n_p + n_e + len(outs)]
        acc_refs = refs[2 * n_p + n_e + len(outs):]

        def product(p):
            a, b = a_refs[p][...], b_refs[p][...]
            if a_pro is not None:
                a = a_pro(a)
            if b_pro is not None:
                b = b_pro(b)
            return lax.dot_general(a, b, dims, preferred_element_type=F32)

        def write(accs):
            vals = epi(accs, [e[...] for e in e_refs])
            for o, o_ref, val in zip(outs, o_refs, vals):
                if o["kind"] == "tile":
                    o_ref[...] = val.astype(o_ref.dtype)
                else:
                    o_ref[0] = val

        if n_k == 1:
            write([product(p) for p in range(n_p)])
            return
        k = pl.program_id(2)

        @pl.when(k == 0)
        def _():
            for acc in acc_refs:
                acc[...] = jnp.zeros_like(acc)

        for p in range(n_p):
            acc_refs[p][...] += product(p)

        pl.when(k == n_k - 1)(lambda: write([acc[...] for acc in acc_refs]))

    return _ride(name, body, riders, arrays, grid=(n_i, n_j, n_k), in_specs=in_specs, out_specs=out_specs,
                 out_shape=out_shape, scratch_shapes=[pltpu.VMEM((tm, tn), F32) for _ in pairs] if n_k > 1 else [],
                 sem=("parallel", "parallel", "arbitrary"))


def _tile_out(dtype):
    return {"kind": "tile", "dtype": dtype}


_COLSUM = {"kind": "colsum"}


def _colsum(v):
    return jnp.sum(v, axis=0, keepdims=True)


def _norm_mod(name, x, norm_w, scale, shift):
    S, D = x.shape
    tr = _tile(S, 256)

    def body(x_ref, nw_ref, sc_ref, sh_ref, h_ref):
        xv = x_ref[...]
        r = lax.rsqrt(jnp.mean(xv * xv, axis=-1, keepdims=True) + EPS)
        h_ref[...] = ((xv * r * nw_ref[...]) * (1.0 + sc_ref[...]) + sh_ref[...]).astype(BF16)

    row = pl.BlockSpec((1, D), lambda i: (0, 0))
    til = pl.BlockSpec((tr, D), lambda i: (i, 0))
    return _pcall(body, name=name, grid=(S // tr,), in_specs=[til, row, row, row], out_specs=til,
                  out_shape=jax.ShapeDtypeStruct((S, D), BF16), compiler_params=_params(("parallel",)))(
                      x, norm_w, scale, shift)


def _norm_mod_bwd(name, dh, x, dres, norm_w, scale, gate_o=None):
    S, D = x.shape
    tr = _tile(S, 256)
    n_r = S // tr
    with_gate = gate_o is not None

    def body(*refs):
        if with_gate:
            dh_ref, x_ref, dres_ref, nw_ref, sc_ref, o_ref, g_ref, dx_ref, p1, p2, p3, do_ref, p4 = refs
        else:
            dh_ref, x_ref, dres_ref, nw_ref, sc_ref, dx_ref, p1, p2, p3 = refs
        dhv, xv, nw = dh_ref[...], x_ref[...], nw_ref[...]
        r = lax.rsqrt(jnp.mean(xv * xv, axis=-1, keepdims=True) + EPS)
        xh = xv * r
        p1[0] = _colsum(dhv)
        p2[0] = _colsum(dhv * (xh * nw))
        dn = dhv * (1.0 + sc_ref[...])
        p3[0] = _colsum(dn * xh)
        dxh = dn * nw
        dx = dres_ref[...] + r * (dxh - xh * jnp.mean(dxh * xh, axis=-1, keepdims=True))
        dx_ref[...] = dx
        if with_gate:
            do_ref[...] = (dx * g_ref[...]).astype(BF16)
            p4[0] = _colsum(dx * o_ref[...].astype(F32))

    row = pl.BlockSpec((1, D), lambda i: (0, 0))
    til = pl.BlockSpec((tr, D), lambda i: (i, 0))
    part = pl.BlockSpec((1, 1, D), lambda i: (i, 0, 0))
    part_shape = jax.ShapeDtypeStruct((n_r, 1, D), F32)
    in_specs = [til, til, til, row, row]
    arrays = [dh, x, dres, norm_w, scale]
    out_specs = [til, part, part, part]
    out_shape = [jax.ShapeDtypeStruct((S, D), F32), part_shape, part_shape, part_shape]
    if with_gate:
        in_specs += [til, row]
        arrays += list(gate_o)
        out_specs += [til, part]
        out_shape += [jax.ShapeDtypeStruct((S, D), BF16), part_shape]
    return _pcall(body, name=name, grid=(n_r,), in_specs=in_specs, out_specs=out_specs, out_shape=out_shape,
                  compiler_params=_params(("parallel",)))(*arrays)


def _pool_w_specs(rows, cg):
    return [pl.BlockSpec((rows, cg), lambda g, j=j: (N_GROUPS * j + g, 0)) for j in range(N_CHIPS)]


def _pool_fwd(proj, wp_full, pool_scale, S, PW):
    cg = PW // N_GROUPS
    rows = cg // N_CHIPS
    T = _tile(S, POOL_T)
    n_t = S // T

    def body(u_ref, w0, w1, w2, w3, ps_ref, pooled_ref, pa_ref):
        g = pl.program_id(0)
        win = jnp.left_shift(2, g)
        w = jnp.concatenate([w0[...], w1[...], w2[...], w3[...]], axis=0)
        t_i = lax.broadcasted_iota(jnp.int32, (T, T), 0)
        j_i = lax.broadcasted_iota(jnp.int32, (T, T), 1)
        b_cur = ((j_i <= t_i) & (j_i > t_i - win)).astype(BF16)
        b_prev = (j_i - T > t_i - win).astype(BF16)
        row = lax.broadcasted_iota(jnp.int32, (T, 1), 0)
        for r in range(n_t):
            cur = u_ref[r * T:(r + 1) * T, :]
            ws = jnp.dot(b_cur, cur, preferred_element_type=F32)
            if r > 0:
                ws += jnp.dot(b_prev, u_ref[(r - 1) * T:r * T, :], preferred_element_type=F32)
            count = jnp.minimum(row + (r * T + 1), win).astype(F32)
            pooled = (ws / count - cur.astype(F32)).astype(BF16)
            pooled_ref[r * T:(r + 1) * T, :] = pooled
            mixed = jnp.dot(pooled, w, preferred_element_type=F32)
            pa_ref[r * T:(r + 1) * T, :] = (mixed * ps_ref[...]).astype(BF16)

    col = pl.BlockSpec((S, cg), lambda g: (0, g))
    return _pcall(
        body, name="pool_fwd", grid=(N_GROUPS,),
        in_specs=[col] + _pool_w_specs(rows, cg) + [pl.BlockSpec((1, cg), lambda g: (0, g))],
        out_specs=[col, col],
        out_shape=[jax.ShapeDtypeStruct((S, PW), BF16), jax.ShapeDtypeStruct((S, PW), BF16)],
        compiler_params=_params(("parallel",)),
    )(proj, wp_full, wp_full, wp_full, wp_full, pool_scale)


def _pool_bwd(dpa, pooled, wp_full, pool_scale, S, PW):
    cg = PW // N_GROUPS
    rows = cg // N_CHIPS
    T = _tile(S, POOL_T)
    n_t = S // T

    def body(dpa_ref, pooled_ref, w0, w1, w2, w3, ps_ref, du_ref, gw_ref, gs_ref, dp_s, dpc_s, dmx_s):
        g = pl.program_id(0)
        win = jnp.left_shift(2, g)
        w = jnp.concatenate([w0[...], w1[...], w2[...], w3[...]], axis=0)
        row = lax.broadcasted_iota(jnp.int32, (T, 1), 0)
        gs = jnp.zeros((1, cg), F32)
        for r in range(n_t):
            sl = slice(r * T, (r + 1) * T)
            mixed = jnp.dot(pooled_ref[sl, :], w, preferred_element_type=F32)
            dpa_t = dpa_ref[sl, :]
            gs += _colsum(dpa_t * mixed)
            dmx = (dpa_t * ps_ref[...]).astype(BF16)
            dmx_s[sl, :] = dmx
            dpo = lax.dot_general(dmx, w, (((1,), (1,)), ((), ())), preferred_element_type=F32)
            dp_s[sl, :] = dpo
            count = jnp.minimum(row + (r * T + 1), win).astype(F32)
            dpc_s[sl, :] = (dpo / count).astype(BF16)
        gs_ref[...] = gs
        gw = lax.dot_general(pooled_ref[...], dmx_s[...], (((0,), (0,)), ((), ())), preferred_element_type=F32)
        for j in range(N_CHIPS):
            gw_ref[j, 0] = gw[j * rows:(j + 1) * rows, :].astype(BF16)
        j_i = lax.broadcasted_iota(jnp.int32, (T, T), 0)
        t_i = lax.broadcasted_iota(jnp.int32, (T, T), 1)
        b_cur = ((t_i >= j_i) & (t_i < j_i + win)).astype(BF16)
        b_next = (t_i + T < j_i + win).astype(BF16)
        for r in range(n_t):
            sl = slice(r * T, (r + 1) * T)
            acc = jnp.dot(b_cur, dpc_s[sl, :], preferred_element_type=F32)
            if r + 1 < n_t:
                acc += jnp.dot(b_next, dpc_s[(r + 1) * T:(r + 2) * T, :], preferred_element_type=F32)
            du_ref[sl, :] = (acc - dp_s[sl, :]).astype(BF16)

    col = pl.BlockSpec((S, cg), lambda g: (0, g))
    return _pcall(
        body, name="pool_bwd", grid=(N_GROUPS,),
        in_specs=[col, col] + _pool_w_specs(rows, cg) + [pl.BlockSpec((1, cg), lambda g: (0, g))],
        out_specs=[col, pl.BlockSpec((N_CHIPS, 1, rows, cg), lambda g: (0, g, 0, 0)),
                   pl.BlockSpec((1, cg), lambda g: (0, g))],
        out_shape=[jax.ShapeDtypeStruct((S, PW), BF16),
                   jax.ShapeDtypeStruct((N_CHIPS, N_GROUPS, rows, cg), BF16),
                   jax.ShapeDtypeStruct((1, PW), F32)],
        scratch_shapes=[pltpu.VMEM((S, cg), F32), pltpu.VMEM((S, cg), BF16), pltpu.VMEM((S, cg), BF16)],
        compiler_params=_params(("parallel",)),
    )(dpa, pooled, wp_full, wp_full, wp_full, wp_full, pool_scale)


_NT = (((1,), (1,)), ((), ()))
_TN = (((0,), (0,)), ((), ()))


def _split_dot(v, tri):
    hi = v.astype(BF16)
    lo = (v - hi.astype(F32)).astype(BF16)
    return jnp.dot(hi, tri, preferred_element_type=F32) + jnp.dot(lo, tri, preferred_element_type=F32)


LOG2E = 1.4426950408889634
QK_SCALE = 1.0 / math.sqrt(HEAD_DIM)


def _sb_scores(q2_i, k_j, tri_l, masked):
    tq, tk = q2_i.shape[0], k_j.shape[0]
    s = lax.dot_general(q2_i, k_j, _NT, preferred_element_type=F32)
    lp = jnp.log(1.0 + jnp.exp2(-jnp.abs(s))) * LOG2E
    lb = jnp.minimum(s, 0.0) - lp
    l = lb - s
    mask = None
    if masked:
        mask = lax.broadcasted_iota(jnp.int32, (tq, tk), 0) > lax.broadcasted_iota(jnp.int32, (tq, tk), 1)
        l = jnp.where(mask, l, 0.0)
    return l, lb, lb + _split_dot(l, tri_l), mask


def _sb_weights(t, carry_l, mask):
    a = jnp.exp2(t + carry_l)
    return a if mask is None else jnp.where(mask, a, 0.0)


def _rowsum(v):
    return jnp.sum(v, axis=1, keepdims=True)


def _qk_norm(x_ref, w_ref):
    xv = x_ref[...].astype(F32)
    r = lax.rsqrt(jnp.mean(xv * xv, axis=-1, keepdims=True) + EPS)
    return xv * r, r


def _attn_fwd(proj, q_norm_w, k_norm_w, S, H, q_off, riders=()):
    t = _tile(S, ATT_T)
    n_q = S // t

    def body(q_ref, k_ref, v_ref, qw_ref, kw_ref, att_ref, attf_ref, qn_s, kn_s):
        qh, _ = _qk_norm(q_ref, qw_ref)
        qn_s[...] = (qh * qw_ref[...] * (QK_SCALE * LOG2E)).astype(BF16)
        kh, _ = _qk_norm(k_ref, kw_ref)
        kn_s[...] = (kh * kw_ref[...]).astype(BF16)
        tri_l = (lax.broadcasted_iota(jnp.int32, (t, t), 0) > lax.broadcasted_iota(jnp.int32, (t, t), 1)).astype(BF16)

        def rows(j):
            return pl.ds(pl.multiple_of(j * t, t), t)

        def q_step(i, _):
            q_i = qn_s[rows(i), :]

            def av(a, j):
                return jnp.dot(a.astype(BF16), v_ref[rows(j), :], preferred_element_type=F32)

            l, _, tt, mask = _sb_scores(q_i, kn_s[rows(i), :], tri_l, True)
            acc = av(_sb_weights(tt, 0.0, mask), i)
            carry = _rowsum(l)

            def single(_, c):
                carry, acc = c
                l, _, tt, _ = _sb_scores(q_i, kn_s[rows(i - 1), :], tri_l, False)
                return carry + _rowsum(l), acc + av(_sb_weights(tt, carry, None), i - 1)

            carry, acc = lax.fori_loop(0, i % 2, single, (carry, acc))
            top = i - 1 - i % 2

            def pair(p, c):
                carry, acc = c
                j0 = top - 2 * p
                l0, _, t0, _ = _sb_scores(q_i, kn_s[rows(j0), :], tri_l, False)
                l1, _, t1, _ = _sb_scores(q_i, kn_s[rows(j0 - 1), :], tri_l, False)
                mid = carry + _rowsum(l0)
                acc = acc + av(_sb_weights(t0, carry, None), j0) + av(_sb_weights(t1, mid, None), j0 - 1)
                return mid + _rowsum(l1), acc

            _, acc = lax.fori_loop(0, i // 2, pair, (carry, acc))
            att_ref[rows(i), :] = acc.astype(BF16)
            attf_ref[rows(i), :] = acc
            return 0

        lax.fori_loop(0, n_q, q_step, 0)

    def col(off):
        return pl.BlockSpec((S, HEAD_DIM), lambda h, off=off: (0, off + h))

    wspec = pl.BlockSpec((1, HEAD_DIM), lambda h: (0, 0))
    return _ride(
        "attn_fwd", body, riders, [proj, proj, proj, q_norm_w, k_norm_w], grid=(H,),
        in_specs=[col(q_off), col(q_off + H), col(q_off + 2 * H), wspec, wspec],
        out_specs=[col(0), col(0)],
        out_shape=[jax.ShapeDtypeStruct((S, H * HEAD_DIM), BF16), jax.ShapeDtypeStruct((S, H * HEAD_DIM), F32)],
        scratch_shapes=[pltpu.VMEM((S, HEAD_DIM), BF16), pltpu.VMEM((S, HEAD_DIM), BF16)],
        sem=("parallel",))


def _attn_bwd(proj, datt, attf, q_norm_w, k_norm_w, S, H, q_off, riders=()):
    t = _tile(S, ATT_T)
    n_q = S // t

    def body(q_ref, k_ref, v_ref, do_ref, o_ref, qw_ref, kw_ref, dq_ref, dk_ref, dv_ref, gq_ref, gk_ref,
             qn_s, kn_s, qz_s, kz_s, dk_s, dv_s, gq_s):
        qw, kw = qw_ref[...], kw_ref[...]
        qh, _ = _qk_norm(q_ref, qw_ref)
        qn_s[...] = (qh * qw * (QK_SCALE * LOG2E)).astype(BF16)
        qz_s[...] = (qh * qw * QK_SCALE).astype(BF16)
        kh, _ = _qk_norm(k_ref, kw_ref)
        kn_s[...] = (kh * kw).astype(BF16)
        kz_s[...] = (kh * kw * QK_SCALE).astype(BF16)
        dk_s[...] = jnp.zeros_like(dk_s)
        dv_s[...] = jnp.zeros_like(dv_s)
        gq_s[...] = jnp.zeros_like(gq_s)
        r_i = lax.broadcasted_iota(jnp.int32, (t, t), 0)
        c_i = lax.broadcasted_iota(jnp.int32, (t, t), 1)
        tri_l = (r_i > c_i).astype(BF16)
        tri_e = (r_i >= c_i).astype(BF16)

        def rows(j):
            return pl.ds(pl.multiple_of(j * t, t), t)

        def q_step(i, _):
            q_i = qn_s[rows(i), :]
            do_i = do_ref[rows(i), :]
            d_i = _rowsum(do_i.astype(F32) * o_ref[rows(i), :])

            def scores(j, masked):
                l, lb, tt, mask = _sb_scores(q_i, kn_s[rows(j), :], tri_l, masked)
                da = lax.dot_general(do_i, v_ref[rows(j), :], _NT, preferred_element_type=F32)
                return l, lb, tt, mask, da

            def grads(j, sc, carry_l, carry_e, dq_acc):
                l, lb, tt, mask, da = sc
                a_bf = _sb_weights(tt, carry_l, mask).astype(BF16)
                e = da * a_bf.astype(F32)
                p = (d_i - carry_e) - _split_dot(e, tri_e)
                dz = e - jnp.exp2(lb) * (e + p)
                if mask is not None:
                    dz = jnp.where(mask, dz, 0.0)
                dz = dz.astype(BF16)
                dk_s[rows(j), :] += lax.dot_general(dz, qz_s[rows(i), :], _TN, preferred_element_type=F32)
                dv_s[rows(j), :] += lax.dot_general(a_bf, do_i, _TN, preferred_element_type=F32)
                return (carry_l + _rowsum(l), carry_e + _rowsum(e),
                        dq_acc + jnp.dot(dz, kz_s[rows(j), :], preferred_element_type=F32))

            zero = jnp.zeros((t, 1), F32)
            first = (zero, zero, jnp.zeros((t, HEAD_DIM), F32))

            def group(js, diagonal_first, c):
                scs = [scores(j, diagonal_first and n == 0) for n, j in enumerate(js)]
                for j, sc in zip(js, scs):
                    c = grads(j, sc, *c)
                return c

            n_first = i % ATT_GROUP
            c = lax.switch(n_first, [functools.partial(group, [i - u for u in range(n + 1)], True, first)
                                     for n in range(ATT_GROUP)])
            top = i - 1 - n_first

            def whole(p, c):
                j0 = top - ATT_GROUP * p
                return group([j0 - u for u in range(ATT_GROUP)], False, c)

            _, _, dqn = lax.fori_loop(0, (i - n_first) // ATT_GROUP, whole, c)
            qv = q_ref[rows(i), :].astype(F32)
            r = lax.rsqrt(jnp.mean(qv * qv, axis=-1, keepdims=True) + EPS)
            xh = qv * r
            gq_s[...] += _colsum(dqn * xh)
            dxh = dqn * qw
            dq_ref[rows(i), :] = (r * (dxh - xh * jnp.mean(dxh * xh, axis=-1, keepdims=True))).astype(BF16)
            return 0

        lax.fori_loop(0, n_q, q_step, 0)
        gq_ref[0] = gq_s[...]
        kh, rk = _qk_norm(k_ref, kw_ref)
        dkn = dk_s[...]
        gk_ref[0] = _colsum(dkn * kh)
        dxh = dkn * kw
        dk_ref[...] = (rk * (dxh - kh * jnp.mean(dxh * kh, axis=-1, keepdims=True))).astype(BF16)
        dv_ref[...] = dv_s[...].astype(BF16)

    def col(off):
        return pl.BlockSpec((S, HEAD_DIM), lambda h, off=off: (0, off + h))

    wspec = pl.BlockSpec((1, HEAD_DIM), lambda h: (0, 0))
    gspec = pl.BlockSpec((1, 1, HEAD_DIM), lambda h: (h, 0, 0))
    act = jax.ShapeDtypeStruct((S, H * HEAD_DIM), BF16)
    gsh = jax.ShapeDtypeStruct((H, 1, HEAD_DIM), F32)
    return _ride(
        "attn_bwd", body, riders, [proj, proj, proj, datt, attf, q_norm_w, k_norm_w], grid=(H,),
        in_specs=[col(q_off), col(q_off + H), col(q_off + 2 * H), col(0), col(0), wspec, wspec],
        out_specs=[col(0), col(0), col(0), gspec, gspec],
        out_shape=[act, act, act, gsh, gsh],
        scratch_shapes=[pltpu.VMEM((S, HEAD_DIM), BF16)] * 4 + [pltpu.VMEM((S, HEAD_DIM), F32)] * 2
        + [pltpu.VMEM((1, HEAD_DIM), F32)],
        sem=("parallel",))


def _place():
    x, y, c = lax.axis_index("x"), lax.axis_index("y"), lax.axis_index("c")
    chips = [(1 - x, y), (x, 1 - y), (1 - x, 1 - y)]
    return x, y, c, chips


def _dev_allgather(name, v):
    m_per, n = v.shape

    def body(x_ref, out_ref, send_sems, recv_sems, local_sem):
        x, y, c, _ = _place()
        me = (x, y, c)

        def rows(px, py, pc):
            return out_ref.at[pl.ds((4 * px + 2 * py + pc) * m_per, m_per), :]

        def peer(r):
            return tuple(1 - b if (r >> s) & 1 else b for b, s in zip(me, (2, 1, 0)))

        def copy(r, block, to, src=None):
            return pltpu.make_async_remote_copy(
                src_ref=rows(*block) if src is None else src, dst_ref=rows(*block),
                send_sem=send_sems.at[r - 1], recv_sem=recv_sems.at[r - 1], device_id=to, device_id_type=MESH)

        mine = pltpu.make_async_copy(x_ref, rows(*me), local_sem)
        mine.start()
        sends = [copy(r, me, peer(r), src=x_ref) for r in range(1, N_DEV)]
        for cp in sends:
            cp.start()
        for r in range(1, N_DEV):
            copy(r, peer(r), me).wait_recv()
        for cp in sends:
            cp.wait_send()
        mine.wait()

    return _pcall(
        body, name=name, out_shape=jax.ShapeDtypeStruct((N_DEV * m_per, n), v.dtype),
        in_specs=[pl.BlockSpec(memory_space=pltpu.VMEM)], out_specs=pl.BlockSpec(memory_space=pltpu.VMEM),
        scratch_shapes=[pltpu.SemaphoreType.DMA((7,)), pltpu.SemaphoreType.DMA((7,)), pltpu.SemaphoreType.DMA],
        compiler_params=pltpu.CompilerParams(vmem_limit_bytes=VMEM_LIMIT_V7X),
    )(v)


class _W:
    def __init__(self, name, kind, R, C):
        self.name, self.kind, self.R, self.C = name, kind, R, C

    @property
    def shard_shape(self):
        return (self.R, self.C // N_CHIPS) if self.kind == "col" else (self.R // N_CHIPS, self.C)

    @property
    def half_rows(self):
        return self.shard_shape[0] // 2

    def shard_half(self, ref, half):
        return ref.at[pl.ds(half * self.half_rows, self.half_rows), :]

    def region(self, full_ref, chip, half):
        hr = self.half_rows
        if self.kind == "col":
            cw = self.C // N_CHIPS
            return full_ref.at[pl.ds(half * hr, hr), pl.ds(chip * cw, cw)]
        return full_ref.at[pl.ds(chip * (2 * hr) + half * hr, hr), :]


def _ag_rider(ws, fulls, n_ch=4, chunks=None, lead=3):
    n_w = len(ws)
    lo, hi = chunks or (0, n_ch)
    per = 6

    def parts(full, sems):
        send_sems, recv_sems = sems
        x, y, c, _ = _place()
        xn, yn, dg = (1 - x, y), (x, 1 - y), (1 - x, 1 - y)
        via = (x + (1 - c) * (1 - 2 * x), y + c * (1 - 2 * y))
        to = (x + c * (1 - 2 * x), y + (1 - c) * (1 - 2 * y))

        def reg(i, chip, half, t):
            nr = ws[i].half_rows // n_ch
            return ws[i].region(full[i], 2 * chip[0] + chip[1], half).at[pl.ds(t * nr, nr), :]

        def copy(r, i, t, k, dev):
            s = (i * (hi - lo) + t - lo) * per + k
            return pltpu.make_async_remote_copy(src_ref=r, dst_ref=r, send_sem=send_sems.at[s],
                                                recv_sem=recv_sems.at[s], device_id=dev, device_id_type=MESH)

        def direct(i, t, k):
            return copy(reg(i, (x, y), c, t), i, t, k, (*(via, to)[k], c))

        def direct_in(i, t, k):
            return copy(reg(i, (via, to)[k], c, t), i, t, k, (*(via, to)[k], c))

        def relay(i, t):
            return copy(reg(i, via, c, t), i, t, 2, (*to, c))

        def relay_in(i, t):
            return copy(reg(i, dg, c, t), i, t, 2, (*to, c))

        def hand(i, t, k, half):
            return copy(reg(i, (xn, yn, dg)[k], half, t), i, t, 3 + k, (x, y, 1 - c))

        return c, direct, direct_in, relay, relay_in, hand

    def start(_, full, sems):
        _, direct, _, _, _, _ = parts(full, sems)
        for t in range(lo, hi):
            for i in range(n_w):
                direct(i, t, 0).start()
                direct(i, t, 1).start()

    def arrived(t):
        def step(_, full, sems):
            c, _, direct_in, relay, relay_in, hand = parts(full, sems)
            for i in range(n_w):
                direct_in(i, t, 0).wait_recv()
                direct_in(i, t, 1).wait_recv()
                relay(i, t).start()
                hand(i, t, 0, c).start()
                hand(i, t, 1, c).start()
        return step

    def finish(_, full, sems):
        c, direct, _, relay, relay_in, hand = parts(full, sems)
        for t in range(lo, hi):
            for i in range(n_w):
                relay_in(i, t).wait_recv()
                hand(i, t, 2, c).start()
        for i in range(n_w):
            for t in range(lo, hi):
                for k in range(3):
                    hand(i, t, k, 1 - c).wait_recv()
        for i in range(n_w):
            for t in range(lo, hi):
                direct(i, t, 0).wait_send()
                direct(i, t, 1).wait_send()
                relay(i, t).wait_send()
                for k in range(3):
                    hand(i, t, k, c).wait_send()

    n_sem = per * (hi - lo) * n_w
    return _Rider(fulls, [jax.ShapeDtypeStruct((w.R, w.C), BF16) for w in ws],
                  [pltpu.SemaphoreType.DMA((n_sem,)), pltpu.SemaphoreType.DMA((n_sem,))], start, finish,
                  steps=[arrived(t) for t in range(lo, hi)], aliases={i: i for i in range(n_w)}, lead=lead)


def _cast_into_full(ws, shards, chip_arr, riders=()):
    sr, sc = ws[0].shard_shape
    assert all(w.shard_shape == (sr, sc) for w in ws)
    tr, tc = _tile(sr, 512), _tile(sc, 2048)
    n_r, n_c = sr // tr, sc // tc

    def place(w):
        if w.kind == "col":
            return pl.BlockSpec((tr, tc), lambda i, j, chip: (i, chip[0] * n_c + j))
        return pl.BlockSpec((tr, tc), lambda i, j, chip: (chip[0] * n_r + i, j))

    def body(*refs):
        for a_ref, o_ref in zip(refs[:len(ws)], refs[len(ws):]):
            o_ref[...] = a_ref[...].astype(BF16)

    return _ride("cast_" + "_".join(w.name for w in ws), body, riders, list(shards), grid=(n_r, n_c),
                 in_specs=[pl.BlockSpec((tr, tc), lambda i, j, chip: (i, j))] * len(ws),
                 out_specs=[place(w) for w in ws], out_shape=[jax.ShapeDtypeStruct((w.R, w.C), BF16) for w in ws],
                 scratch_shapes=[], sem=("parallel", "parallel"), scalars=chip_arr)


def _half_view(w, g):
    return g if w.kind == "col" else g.reshape(N_CHIPS, w.R // N_CHIPS, w.C)


def _px_rider(ws, grads):
    n_w = len(ws)

    def copies(g, got, sems):
        send_sems, recv_sems = sems
        x, y, c, _ = _place()

        def half_all(w, ref, half):
            hr = w.half_rows
            if w.kind == "col":
                return ref.at[pl.ds(half * hr, hr), :]
            return ref.at[:, pl.ds(half * hr, hr), :]

        return [pltpu.make_async_remote_copy(
            src_ref=half_all(w, g[i], 1 - c), dst_ref=got[i], send_sem=send_sems.at[i], recv_sem=recv_sems.at[i],
            device_id=(x, y, 1 - c), device_id_type=MESH) for i, w in enumerate(ws)]

    def start(g, got, sems):
        for cp in copies(g, got, sems):
            cp.start()

    def finish(g, got, sems):
        for cp in copies(g, got, sems):
            cp.wait_recv()
            cp.wait_send()

    def got_shape(w):
        hr = w.half_rows
        return (hr, w.C) if w.kind == "col" else (N_CHIPS, hr, w.C)

    return _Rider([_half_view(w, g) for w, g in zip(ws, grads)],
                  [jax.ShapeDtypeStruct(got_shape(w), BF16) for w in ws],
                  [pltpu.SemaphoreType.DMA((n_w,)), pltpu.SemaphoreType.DMA((n_w,))], start, finish)


def _pair_sum(w, g, got, c_arr):
    hr = w.half_rows
    if w.kind == "col":
        tr, tc = _tile(hr, 512), _tile(w.C, 2048)
        n_r = hr // tr
        grid = (n_r, w.C // tc)
        g_spec = pl.BlockSpec((tr, tc), lambda i, j, c: (c[0] * n_r + i, j))
        o_spec = pl.BlockSpec((tr, tc), lambda i, j, c: (i, j))
    else:
        tr = _tile(hr, 512)
        n_r = hr // tr
        grid = (N_CHIPS, n_r)
        g_spec = pl.BlockSpec((1, tr, w.C), lambda s, i, c: (s, c[0] * n_r + i, 0))
        o_spec = pl.BlockSpec((1, tr, w.C), lambda s, i, c: (s, i, 0))

    def body(c_ref, g_ref, got_ref, out_ref):
        out_ref[...] = (g_ref[...].astype(F32) + got_ref[...].astype(F32)).astype(BF16)

    return _pcall(
        body, name="grad_pair_sum_" + w.name, out_shape=jax.ShapeDtypeStruct(got.shape, BF16),
        grid_spec=pltpu.PrefetchScalarGridSpec(num_scalar_prefetch=1, grid=grid, in_specs=[g_spec, o_spec],
                                               out_specs=o_spec),
        compiler_params=_params(("parallel", "parallel")),
    )(c_arr, _half_view(w, g), got)


def _chip_sum(w, p, q, cc_arr):
    hr, cols = w.half_rows, w.shard_shape[1]
    tr, tc = _tile(hr, 512), _tile(cols, 2048)
    n_r, n_c = hr // tr, cols // tc

    def body(cc_ref, own, q1, q2, q3, out_ref):
        own_v = own[...] if w.kind == "col" else own[0]
        out_ref[...] = ((own_v.astype(F32) + q1[0].astype(F32)) + q2[0].astype(F32)) + q3[0].astype(F32)

    if w.kind == "col":
        own_spec = pl.BlockSpec((tr, tc), lambda i, j, cc: (i, cc[1] * n_c + j))
    else:
        own_spec = pl.BlockSpec((1, tr, tc), lambda i, j, cc: (cc[1], i, j))
    q_specs = [pl.BlockSpec((1, tr, tc), lambda i, j, cc, s=s: ((cc[1] + s) % N_CHIPS, i, j)) for s in (1, 2, 3)]
    return _pcall(
        body, name="grad_chip_sum_" + w.name, out_shape=jax.ShapeDtypeStruct(w.shard_shape, F32),
        grid_spec=pltpu.PrefetchScalarGridSpec(
            num_scalar_prefetch=1, grid=(n_r, n_c), in_specs=[own_spec] + q_specs,
            out_specs=pl.BlockSpec((tr, tc), lambda i, j, cc: (cc[0] * n_r + i, j))),
        compiler_params=_params(("parallel", "parallel")),
    )(cc_arr, p, q, q, q)


_SEM = pl.BlockSpec(memory_space=pltpu.SEMAPHORE)
_HBM = pl.BlockSpec(memory_space=pltpu.HBM)


def _split_copies(kind, ws, p, land, send_sems, recv_sems):
    x, y, c, chips = _place()
    my_chip = 2 * x + y
    pairs = []
    for i, w in enumerate(ws):
        if kind == "pair":
            hr = w.half_rows
            src = p[i].at[pl.ds((1 - c) * hr, hr), :] if w.kind == "col" else p[i].at[:, pl.ds((1 - c) * hr, hr), :]
            cp = pltpu.make_async_remote_copy(src_ref=src, dst_ref=land[i], send_sem=send_sems.at[i],
                                              recv_sem=recv_sems.at[i], device_id=(x, y, 1 - c), device_id_type=MESH)
            pairs.append((cp, cp))
            continue
        for k, chip in enumerate(chips):
            to_chip = 2 * chip[0] + chip[1]
            src = p[i].at[:, pl.ds(to_chip * (w.C // N_CHIPS), w.C // N_CHIPS)] if w.kind == "col" else p[i].at[to_chip]
            kw = dict(send_sem=send_sems.at[3 * i + k], recv_sem=recv_sems.at[3 * i + k], device_id=(*chip, c),
                      device_id_type=MESH)
            pairs.append((pltpu.make_async_remote_copy(src_ref=src, dst_ref=land[i].at[my_chip], **kw),
                          pltpu.make_async_remote_copy(src_ref=src, dst_ref=land[i].at[to_chip], **kw)))
    return pairs


def _split_start(name, kind, ws, arrays):
    n_w = len(ws)
    if kind == "pair":
        arrays = [_half_view(w, g) for w, g in zip(ws, arrays)]
        lands = [lax.empty((w.half_rows, w.C) if w.kind == "col" else (N_CHIPS, w.half_rows, w.C), BF16) for w in ws]
    else:
        lands = [lax.empty((N_CHIPS, w.half_rows, w.shard_shape[1]), BF16) for w in ws]
    n_sem = n_w if kind == "pair" else 3 * n_w

    def body(*refs):
        p, land = refs[:n_w], refs[n_w:2 * n_w]
        for out, _ in _split_copies(kind, ws, p, land, refs[2 * n_w], refs[2 * n_w + 1]):
            out.start()
        refs[-1][...] = jnp.zeros_like(refs[-1])

    arrays = [pltpu.with_memory_space_constraint(a, pltpu.HBM) for a in list(arrays) + lands]
    res = _pcall(
        body, name=name,
        out_shape=(pltpu.SemaphoreType.DMA((n_sem,)), pltpu.SemaphoreType.DMA((n_sem,)),
                   *[pltpu.HBM(a.shape, a.dtype) for a in arrays], jax.ShapeDtypeStruct((SUBLANES, LANES), F32)),
        in_specs=[_HBM] * (2 * n_w),
        out_specs=(_SEM, _SEM, *[_HBM] * (2 * n_w), pl.BlockSpec(memory_space=pltpu.VMEM)),
        input_output_aliases={i: 2 + i for i in range(2 * n_w)},
        compiler_params=pltpu.CompilerParams(has_side_effects=pltpu.SideEffectType.DATAFLOW_SIDE_EFFECTING),
    )(*arrays)
    return (kind, ws, res[0], res[1], list(res[2:2 + n_w]), list(res[2 + n_w:2 + 2 * n_w])), res[-1]


def _split_wait(name, flight, after):
    kind, ws, send_sems, recv_sems, arrays, lands = flight
    n_w = len(ws)

    def body(*refs):
        p, land = refs[:n_w], refs[n_w:2 * n_w]
        for _, cp in _split_copies(kind, ws, p, land, refs[2 * n_w], refs[2 * n_w + 1]):
            cp.wait_send()
            cp.wait_recv()

    res = _pcall(
        body, name=name,
        out_shape=[pltpu.HBM(a.shape, a.dtype) for a in list(arrays) + list(lands)],
        in_specs=[_HBM] * (2 * n_w) + [_SEM, _SEM] + [ANY] * len(after), out_specs=[_HBM] * (2 * n_w),
        input_output_aliases={i: i for i in range(2 * n_w)},
        compiler_params=pltpu.CompilerParams(has_side_effects=pltpu.SideEffectType.DATAFLOW_SIDE_EFFECTING),
    )(*arrays, *lands, send_sems, recv_sems, *after)
    return list(res[:n_w]), list(res[n_w:])


def _sf_rider(ws, grads):
    n_w = len(ws)

    def copy(g, sems, i, half):
        send_sems, recv_sems = sems
        x, y, c, _ = _place()
        h = c if half == "mine" else 1 - c
        reg = ws[i].shard_half(g[i], h)
        return pltpu.make_async_remote_copy(src_ref=reg, dst_ref=reg, send_sem=send_sems.at[i], recv_sem=recv_sems.at[i],
                                            device_id=(x, y, 1 - c), device_id_type=MESH)

    def start(_, g, sems):
        for i in range(n_w):
            copy(g, sems, i, "mine").start()

    def finish(_, g, sems):
        for i in range(n_w):
            copy(g, sems, i, "other").wait_recv()
            copy(g, sems, i, "mine").wait_send()

    return _Rider(grads, [jax.ShapeDtypeStruct(w.shard_shape, F32) for w in ws],
                  [pltpu.SemaphoreType.DMA((n_w,)), pltpu.SemaphoreType.DMA((n_w,))], start, finish,
                  aliases={i: i for i in range(n_w)})


def _adamw_math(w, g, m, v):
    m = ADAM_B1 * m + (1.0 - ADAM_B1) * g
    v = ADAM_B2 * v + (1.0 - ADAM_B2) * (g * g)
    m_hat = m / (1.0 - ADAM_B1 ** ADAM_STEP)
    v_hat = v / (1.0 - ADAM_B2 ** ADAM_STEP)
    delta = -ADAM_LR * (m_hat / (jnp.sqrt(v_hat) + ADAM_EPS) + ADAM_WD * w)
    return delta, m, v


def _adamw(name, w, g, m, v, after=None):
    R, C = w.shape
    tr, tc = _tile(R, 256), _tile(C, 2048)
    behind = [] if after is None else [after]

    def body(w_ref, g_ref, m_ref, v_ref, *rest):
        g_out, d_out, m_out, v_out = rest[len(behind):]
        g = g_ref[...]
        g_out[...] = g
        d_out[...], m_out[...], v_out[...] = _adamw_math(w_ref[...], g, m_ref[...], v_ref[...])

    spec = pl.BlockSpec((tr, tc), lambda i, j: (i, j))
    sh = jax.ShapeDtypeStruct((R, C), F32)
    return _pcall(body, name=name, grid=(R // tr, C // tc), in_specs=[spec] * 4 + [ANY] * len(behind),
                  out_specs=[spec] * 4, out_shape=[sh] * 4, compiler_params=_params(("parallel", "parallel")))(
                      w, g, m, v, *behind)


def _ada_update(sct, dmod_sh, w, m, v, riders=()):
    R, C = w.shape
    tr, tc = _tile(R, 512), _tile(C, 1024)

    def body(s_ref, d_ref, w_ref, m_ref, v_ref, g_out, d_out, m_out, v_out):
        s, d = s_ref[...], d_ref[...]
        g = s[:, 0:1] * d[0:1, :]
        for b in range(1, N_DEV):
            g += s[:, b:b + 1] * d[b:b + 1, :]
        g_out[...] = g
        d_out[...], m_out[...], v_out[...] = _adamw_math(w_ref[...], g, m_ref[...], v_ref[...])

    spec = pl.BlockSpec((tr, tc), lambda i, j: (i, j))
    sh = jax.ShapeDtypeStruct((R, C), F32)
    return _ride(
        "ada_update", body, riders, [sct, dmod_sh, w, m, v], grid=(R // tr, C // tc),
        in_specs=[pl.BlockSpec((tr, N_DEV), lambda i, j: (i, 0)), pl.BlockSpec((N_DEV, tc), lambda i, j: (0, j)),
                  spec, spec, spec],
        out_specs=[spec] * 4, out_shape=[sh] * 4, scratch_shapes=[], sem=("parallel", "parallel"))


def _silu_rows(c_row):
    D = c_row.shape[1]

    def body(c_ref, o_ref):
        cv = c_ref[...]
        o_ref[...] = cv * jax.nn.sigmoid(cv)

    return _pcall(body, name="silu_c", out_shape=jax.ShapeDtypeStruct((1, D), F32))(c_row)


def _pack_partials(parts, widths, total):
    n = len(widths)

    def body(*refs):
        loss_p, out_ref = refs[n], refs[n + 1]
        off = 0
        for ref, wd in zip(refs[:n], widths):
            out_ref[:, off:off + wd] = jnp.sum(ref[...], axis=0)
            off += wd
        loss = jnp.sum(jnp.sum(loss_p[...], axis=0), axis=1, keepdims=True)
        out_ref[:, off:off + LANES] = jnp.broadcast_to(loss, (1, LANES))
        if off + LANES < total:
            out_ref[:, off + LANES:total] = jnp.zeros((1, total - off - LANES), F32)

    return _pcall(body, name="pack_partials", out_shape=jax.ShapeDtypeStruct((1, total), F32))(*parts)


def _small_update(gathered, offsets, params, loss_off):
    n_p = len(params)

    def over_devices(g_ref, off, wd):
        blk = g_ref[:, off:off + wd]
        g = blk[0:1, :]
        for b in range(1, N_DEV):
            g = g + blk[b:b + 1, :]
        return g

    def body(*refs):
        g_ref = refs[0]
        prm = refs[1:1 + 3 * n_p]
        outs = refs[1 + 3 * n_p:]
        outs[4 * n_p][...] = over_devices(g_ref, loss_off, LANES)
        for i, (off, wd) in enumerate(offsets):
            g = over_devices(g_ref, off, wd)
            w, m, v = prm[3 * i][...], prm[3 * i + 1][...], prm[3 * i + 2][...]
            outs[4 * i][...] = g
            outs[4 * i + 1][...], outs[4 * i + 2][...], outs[4 * i + 3][...] = _adamw_math(w, g, m, v)

    flat = [a for t in params for a in t]
    out_shape = [jax.ShapeDtypeStruct(t[0].shape, F32) for t in params for _ in range(4)]
    out_shape.append(jax.ShapeDtypeStruct((1, LANES), F32))
    return _pcall(body, name="small_update", out_shape=out_shape)(gathered, *flat)


def kernel(x, c, w_ada, b_ada, norm1_w, w_in, q_norm_w, k_norm_w, w_pool, pool_scale, w_a_up, w_b_up, w_o, norm2_w, w_ff1, w_ff2, loss_target, m_w_ada, m_b_ada, m_norm1_w, m_w_in, m_q_norm_w, m_k_norm_w, m_w_pool, m_pool_scale, m_w_a_up, m_w_b_up, m_w_o, m_norm2_w, m_w_ff1, m_w_ff2, v_w_ada, v_b_ada, v_norm1_w, v_w_in, v_q_norm_w, v_k_norm_w, v_w_pool, v_pool_scale, v_w_a_up, v_w_b_up, v_w_o, v_norm2_w, v_w_ff1, v_w_ff2):
    _, S, D = x.shape
    PW = D // 2
    H = PW // HEAD_DIM
    cg = PW // N_GROUPS
    IN = w_in.shape[2] * N_CHIPS
    FF = w_ff1.shape[2] * N_CHIPS
    A_COLS = w_ada.shape[2]
    xi, yi, ci = lax.axis_index("x"), lax.axis_index("y"), lax.axis_index("c")
    chip = 2 * xi + yi
    dev = 2 * chip + ci
    c_arr = jnp.reshape(ci, (1,)).astype(jnp.int32)
    x2, tgt = x[0], loss_target[0]

    ws = [_W("w_in", "col", D, IN), _W("w_pool", "row", PW, cg), _W("w_a_up", "col", PW, D),
          _W("w_b_up", "col", PW, D), _W("w_o", "row", D, D), _W("w_ff1", "col", D, FF), _W("w_ff2", "row", FF, D)]
    w32 = [w_in[0], w_pool[0].reshape(cg, cg), w_a_up[0], w_b_up[0], w_o[0], w_ff1[0], w_ff2[0]]
    m32 = [m_w_in[0], m_w_pool[0].reshape(cg, cg), m_w_a_up[0], m_w_b_up[0], m_w_o[0], m_w_ff1[0], m_w_ff2[0]]
    v32 = [v_w_in[0], v_w_pool[0].reshape(cg, cg), v_w_a_up[0], v_w_b_up[0], v_w_o[0], v_w_ff1[0], v_w_ff2[0]]

    W_IN, W_POOL, W_A, W_B, W_O, W_FF1, W_FF2 = ws
    chip_arr = jnp.reshape(chip, (1,)).astype(jnp.int32)
    cc_arr = jnp.stack([ci, chip]).astype(jnp.int32)
    s_in, s_pool, s_a, s_b, s_o = [_cast_into_full([w], [a], chip_arr)[0] for w, a in zip(ws[:5], w32[:5])]
    (s_ff1, s_ff2), ((win_f,),) = _cast_into_full([W_FF1, W_FF2], w32[5:], chip_arr, riders=[_ag_rider([W_IN], [s_in])])

    sc_row = _silu_rows(c)
    sc_all = _dev_allgather("gather_silu_c", sc_row.reshape(SUBLANES, D // SUBLANES)).reshape(N_DEV, D)
    sc16 = jnp.concatenate([sc_all, jnp.zeros_like(sc_all)], axis=0)
    b_cols = lax.dynamic_slice(b_ada, (0, chip * A_COLS), (1, A_COLS))
    (mod_cols,) = _mm("mod_cols", [(sc16, w_ada[0])], M=2 * N_DEV, N=A_COLS, K=D, tm=16, tn=1024, tk=1024,
                      a_pro=lambda a: a.astype(BF16), b_pro=lambda b: b.astype(BF16),
                      extras=[(b_cols, "row", 0)], outs=[_tile_out(F32)], epi=lambda accs, ex: [accs[0] + ex[0]])
    mod_all = _dev_allgather("gather_mod", mod_cols[:N_DEV]).reshape(N_CHIPS, 2, N_DEV, A_COLS)
    mod_row = lax.dynamic_index_in_dim(mod_all[:, 0], dev, axis=1, keepdims=False).reshape(1, N_CHIPS * A_COLS)
    shift1, scale1, gate1, shift2, scale2, gate2 = [mod_row[:, i * D:(i + 1) * D] for i in range(6)]

    WIDE = dict(tm=2048, tn=512, tk=2048)
    DEEP = dict(tm=1024, tn=1024, tk=2048)
    DEEPER = dict(tm=1024, tn=1024, tk=4096)
    h = _norm_mod("norm1_mod", x2, norm1_w, scale1, shift1)
    (proj,), ((wpool_f, wa_f, wb_f, wo_f),) = _mm(
        "in_proj", [(h, win_f)], M=S, N=IN, K=D, outs=[_tile_out(BF16)], epi=lambda accs, ex: [accs[0]], **WIDE,
        riders=[_ag_rider([W_POOL, W_A, W_B, W_O], [s_pool, s_a, s_b, s_o], n_ch=2)])
    pooled, pa = _pool_fwd(proj, wpool_f, pool_scale, S, PW)
    (att, attf), ((wff1_f,),) = _attn_fwd(proj, q_norm_w, k_norm_w, S, H, PW // HEAD_DIM,
                                          riders=[_ag_rider([W_FF1], [s_ff1], lead=4)])

    def merge_epi(accs, ex):
        sa, sb = jax.nn.sigmoid(ex[0].astype(F32)), jax.nn.sigmoid(ex[1].astype(F32))
        return [sa * accs[0] + sb * accs[1], accs[0], accs[1]]

    (merged, ya, yb), (ff2_a,) = _mm("branch_up_merge", [(pa, wa_f), (att, wb_f)], M=S, N=D, K=PW,
                                     extras=[(proj, "tile", 4 * PW), (proj, "tile", 4 * PW + D)],
                                     outs=[_tile_out(BF16)] * 3, epi=merge_epi,
                                     riders=[_ag_rider([W_FF2], [s_ff2], chunks=(0, 1))])
    (x1, o), (ff2_b,) = _mm("out_proj", [(merged, wo_f)], M=S, N=D, K=D, extras=[(x2, "tile", 0), (gate1, "row", 0)],
                            outs=[_tile_out(F32), _tile_out(BF16)], epi=lambda accs, ex: [ex[0] + ex[1] * accs[0], accs[0]],
                            riders=[_ag_rider([W_FF2], ff2_a, chunks=(1, 2))], **WIDE)
    h2 = _norm_mod("norm2_mod", x1, norm2_w, scale2, shift2)
    (rl,), ((wff2_f,),) = _mm("ff1", [(h2, wff1_f)], M=S, N=FF, K=D, outs=[_tile_out(BF16)], **WIDE,
                              epi=lambda accs, ex: [jnp.maximum(accs[0], 0.0)],
                              riders=[_ag_rider([W_FF2], ff2_b, chunks=(2, 4))])

    def square(a):
        af = a.astype(F32)
        return (af * af).astype(BF16)

    def loss_epi(accs, ex):
        x1_t, tgt_t, g2 = ex
        f = accs[0]
        diff = (x1_t + g2 * f) - tgt_t
        dy = diff * (1.0 / D)
        return [dy, dy * g2, _colsum(dy * f), _colsum(diff * diff)]

    dy, df, dgate2_p, loss_p = _mm("ff2_loss", [(rl, wff2_f)], M=S, N=D, K=FF, a_pro=square, **DEEP,
                                   extras=[(x1, "tile", 0), (tgt, "tile", 0), (gate2, "row", 0)],
                                   outs=[_tile_out(F32), _tile_out(BF16), _COLSUM, _COLSUM], epi=loss_epi)

    tied = []

    def behind(token, a):
        a, token = lax.optimization_barrier((a, token))
        tied.append(token)
        return a

    def pair_sums(group, partials, got):
        return [_pair_sum(w, g, r, c_arr) for w, g, r in zip(group, partials, got)]

    def chip_sums(group, sums, from_chips):
        return [_chip_sum(w, p, q, cc_arr) for w, p, q in zip(group, sums, from_chips)]

    first = lambda accs, ex: [accs[0]]
    gmm = dict(ta=True, outs=[_tile_out(BF16)], epi=first, **WIDE)
    (g_ff2,) = _mm("grad_w_ff2", [(rl, df)], M=FF, N=D, K=S, a_pro=square, ta=True, tm=512, tn=2048, tk=2048,
                   outs=[_tile_out(BF16)], epi=first)
    flight, token = _split_start("pair_w_ff2_start", "pair", [W_FF2], [g_ff2])
    (dz1,) = _mm("d_ff_hidden", [(behind(token, df), wff2_f)], M=S, N=FF, K=D, tb=True, extras=[(rl, "tile", 0)],
                 outs=[_tile_out(BF16)], epi=lambda accs, ex: [accs[0] * (2.0 * ex[0].astype(F32))], **WIDE)
    sum_ff2 = pair_sums([W_FF2], *_split_wait("pair_w_ff2_wait", flight, after=[dz1] + tied))
    chip_ff2, token = _split_start("chip_w_ff2_start", "chip", [W_FF2], sum_ff2)
    (g_ff1,) = _mm("grad_w_ff1", [(behind(token, h2), dz1)], M=D, N=FF, K=S, **gmm)
    flight, token = _split_start("pair_w_ff1_start", "pair", [W_FF1], [g_ff1])
    (dh2,) = _mm("d_h2", [(behind(token, dz1), wff1_f)], M=S, N=D, K=FF, tb=True, outs=[_tile_out(F32)], epi=first,
                 **DEEPER)
    sum_ff1 = pair_sums([W_FF1], *_split_wait("pair_w_ff1_wait", flight, after=[dh2] + tied))
    chip_ff1, token = _split_start("chip_w_ff1_start", "chip", [W_FF1], sum_ff1)
    dx1, dshift2_p, dscale2_p, gn2_p, do, dgate1_p = _norm_mod_bwd("norm2_bwd", behind(token, dh2), x1, dy, norm2_w, scale2,
                                                                   gate_o=(o, gate1))
    (g_wo,) = _mm("grad_w_o", [(merged, do)], M=D, N=D, K=S, **gmm)

    def gate_epi(accs, ex):
        dm = accs[0]
        sa, sb = jax.nn.sigmoid(ex[0].astype(F32)), jax.nn.sigmoid(ex[1].astype(F32))
        ya_t, yb_t = ex[2].astype(F32), ex[3].astype(F32)
        return [dm * sa, dm * sb, dm * ya_t * (sa * (1.0 - sa)), dm * yb_t * (sb * (1.0 - sb))]

    dya, dyb, dga, dgb = _mm("d_merged", [(do, wo_f)], M=S, N=D, K=D, tb=True, tm=1024, tn=512, tk=2048,
                             extras=[(proj, "tile", 4 * PW), (proj, "tile", 4 * PW + D), (ya, "tile", 0), (yb, "tile", 0)],
                             outs=[_tile_out(BF16)] * 4, epi=gate_epi)
    both = lambda accs, ex: [accs[0], accs[1]]
    g_wa, g_wb = _mm("grad_w_up", [(pa, dya), (att, dyb)], M=PW, N=D, K=S, ta=True, outs=[_tile_out(BF16)] * 2, epi=both,
                     **WIDE)
    mid = [W_A, W_B, W_O]
    flight, token = _split_start("pair_mid_start", "pair", mid, [g_wa, g_wb, g_wo])
    dpa, datt = _mm("d_branches", [(dya, wa_f), (behind(token, dyb), wb_f)], M=S, N=PW, K=D, tb=True,
                    outs=[_tile_out(F32), _tile_out(BF16)], epi=both, tm=1024, tn=512, tk=2048)
    sum_mid = pair_sums(mid, *_split_wait("pair_mid_wait", flight, after=[datt] + tied))
    chip_mid, token = _split_start("chip_mid_start", "chip", mid, sum_mid)
    du, g_wpool4, gscale_p = _pool_bwd(dpa, pooled, wpool_f, pool_scale, S, PW)
    dq, dk, dv, gq_p, gk_p = _attn_bwd(proj, behind(token, datt), attf, q_norm_w, k_norm_w, S, H, PW // HEAD_DIM)
    dproj = jnp.concatenate([du, dq, dk, dv, dga, dgb], axis=1)
    early = [W_FF1, W_FF2]
    sum_ff1, q_ff1 = _split_wait("chip_w_ff1_wait", chip_ff1, after=[dq] + tied)
    sum_ff2, q_ff2 = _split_wait("chip_w_ff2_wait", chip_ff2, after=[dq] + tied)
    halves_early = chip_sums(early, sum_ff1 + sum_ff2, q_ff1 + q_ff2)
    (g_win,), (grads_early,) = _mm("grad_w_in", [(h, dproj)], M=D, N=IN, K=S, riders=[_sf_rider(early, halves_early)],
                                   **gmm)
    last = [W_IN, W_POOL]
    g_last = [g_win, g_wpool4.reshape(PW, cg)]
    sum_mid, q_mid = _split_wait("chip_mid_wait", chip_mid, after=[g_win] + tied)
    halves_mid = chip_sums(mid, sum_mid, q_mid)
    (dh,), (got_last, grads_mid) = _mm("d_h", [(dproj, win_f)], M=S, N=D, K=IN, tb=True, outs=[_tile_out(F32)], epi=first,
                                       riders=[_px_rider(last, g_last), _sf_rider(mid, halves_mid)], **DEEPER)
    sum_last = pair_sums(last, g_last, got_last)
    grad_x, dshift1_p, dscale1_p, gn1_p = _norm_mod_bwd("norm1_bwd", dh, x2, dx1, norm1_w, scale1)

    parts = [dshift1_p, dscale1_p, dgate1_p, dshift2_p, dscale2_p, dgate2_p, gn1_p, gn2_p,
             gscale_p.reshape(1, 1, PW), gq_p, gk_p]
    widths = [D] * 8 + [PW, HEAD_DIM, HEAD_DIM]
    used = sum(widths)
    P = -(-(used + LANES) // (SUBLANES * LANES)) * (SUBLANES * LANES)
    packed = _pack_partials(parts + [loss_p], widths, P)
    gathered = _dev_allgather("gather_vector_grads", packed.reshape(SUBLANES, P // SUBLANES)).reshape(N_DEV, P)
    sum_last, gathered = lax.optimization_barrier((sum_last, gathered))
    chip_last, token = _split_start("chip_last_start", "chip", last, sum_last)
    small = [(b_ada, m_b_ada, v_b_ada), (norm1_w, m_norm1_w, v_norm1_w), (norm2_w, m_norm2_w, v_norm2_w),
             (pool_scale, m_pool_scale, v_pool_scale), (q_norm_w, m_q_norm_w, v_q_norm_w),
             (k_norm_w, m_k_norm_w, v_k_norm_w)]
    offsets = [(0, 6 * D), (6 * D, D), (7 * D, D), (8 * D, PW), (8 * D + PW, HEAD_DIM), (8 * D + PW + HEAD_DIM, HEAD_DIM)]
    su = _small_update(gathered, offsets, small, used)
    (g_b, d_b, nm_b, nv_b, g_n1, d_n1, nm_n1, nv_n1, g_n2, d_n2, nm_n2, nv_n2, g_ps, d_ps, nm_ps, nv_ps,
     g_qn, d_qn, nm_qn, nv_qn, g_kn, d_kn, nm_kn, nv_kn, loss_sum) = su
    dmod_sh = lax.dynamic_slice(gathered, (0, chip * A_COLS), (N_DEV, A_COLS))
    dmod_sh, token = lax.optimization_barrier((dmod_sh, token))
    g_ada, d_ada, nm_ada, nv_ada = _ada_update(sc_all.T, dmod_sh, w_ada[0], m_w_ada[0], v_w_ada[0])

    upd_done = [_adamw("adamw_" + w.name, a, g, m, v, after=token)
                for w, a, g, m, v in zip(ws[2:], w32[2:], list(grads_mid) + list(grads_early), m32[2:], v32[2:])]

    sum_last, q_last = _split_wait("chip_last_wait", chip_last, after=[nv_ada] + [u[3] for u in upd_done])
    halves_last = chip_sums(last, sum_last, q_last)
    filled = _run_rider("grad_sibling_fill", _sf_rider(last, halves_last))
    upd = [_adamw("adamw_" + w.name, a, g, m, v) for w, a, g, m, v in zip(ws[:2], w32[:2], filled, m32[:2], v32[:2])]
    upd += upd_done

    loss = (0.5 / D) * loss_sum[0, 0]

    def up(a):
        return a[None]

    def pool4(a):
        return a.reshape(1, N_GROUPS, cg // N_CHIPS, cg)

    (gr_win, d_win, nm_win, nv_win), (gr_wp, d_wp, nm_wp, nv_wp), (gr_wa, d_wa, nm_wa, nv_wa), \
        (gr_wb, d_wb, nm_wb, nv_wb), (gr_wo, d_wo, nm_wo, nv_wo), (gr_f1, d_f1, nm_f1, nv_f1), \
        (gr_f2, d_f2, nm_f2, nv_f2) = upd
    return (
        loss, grad_x[None],
        up(g_ada), g_b, g_n1, up(gr_win), g_qn, g_kn, pool4(gr_wp), g_ps, up(gr_wa), up(gr_wb), up(gr_wo), g_n2,
        up(gr_f1), up(gr_f2),
        up(d_ada), d_b, d_n1, up(d_win), d_qn, d_kn, pool4(d_wp), d_ps, up(d_wa), up(d_wb), up(d_wo), d_n2,
        up(d_f1), up(d_f2),
        up(nm_ada), nm_b, nm_n1, up(nm_win), nm_qn, nm_kn, pool4(nm_wp), nm_ps, up(nm_wa), up(nm_wb), up(nm_wo), nm_n2,
        up(nm_f1), up(nm_f2),
        up(nv_ada), nv_b, nv_n1, up(nv_win), nv_qn, nv_kn, pool4(nv_wp), nv_ps, up(nv_wa), up(nv_wb), up(nv_wo), nv_n2,
        up(nv_f1), up(nv_f2),
    )
```

```python
import functools
import math

import jax
import jax.numpy as jnp
from jax import lax
from jax.experimental import pallas as pl
from jax.experimental.pallas import tpu as pltpu

F32 = jnp.float32
BF16 = jnp.bfloat16
MESH = pl.DeviceIdType.MESH
ANY = pl.BlockSpec(memory_space=pl.ANY)

EPS = 1e-6
HEAD_DIM = 128
LANES, SUBLANES = 128, 8
POOL_WINDOWS = (2, 4, 8, 16)
N_GROUPS = len(POOL_WINDOWS)
assert POOL_WINDOWS == tuple(2 << g for g in range(N_GROUPS))
N_CHIPS = 4
N_DEV = 8
ADAM_LR, ADAM_B1, ADAM_B2, ADAM_EPS, ADAM_WD, ADAM_STEP = 0.001, 0.9, 0.999, 1e-08, 0.01, 10
VMEM_LIMIT_V7X = 56 * 1024 * 1024
ATT_T = 256
ATT_GROUP = 8
POOL_T = 256


def _pcall(body, **kw):
    return pl.pallas_call(body, **kw)


def _params(sem=None):
    return pltpu.CompilerParams(dimension_semantics=sem, vmem_limit_bytes=VMEM_LIMIT_V7X)


def _tile(n, pref):
    if n <= pref:
        return n
    t = pref
    while n % t:
        t //= 2
    return t


class _Rider:
    def __init__(self, arrays, out_shape, sems, start, finish, aliases=None, steps=()):
        self.arrays, self.out_shape, self.sems = list(arrays), list(out_shape), list(sems)
        self.start, self.finish, self.aliases, self.steps = start, finish, aliases or {}, list(steps)


def _ride(name, body, riders, arrays, *, grid, in_specs, out_specs, out_shape, scratch_shapes, sem, scalars=None):
    n_in, n_out, n_scr = len(arrays), len(out_shape), len(scratch_shapes)
    r_arrays = [a for r in riders for a in r.arrays]
    r_outs = [o for r in riders for o in r.out_shape]
    r_sems = [s for r in riders for s in r.sems]
    n_hooks = max([len(r.steps) for r in riders], default=0)
    total = math.prod(grid)
    aliases, off_i, off_o = {}, n_in + (scalars is not None), n_out
    for r in riders:
        for a, o in r.aliases.items():
            aliases[off_i + a] = off_o + o
        off_i += len(r.arrays)
        off_o += len(r.out_shape)

    def full(*refs):
        p = 0
        groups = []
        for n in (n_in, len(r_arrays), n_out, len(r_outs), n_scr, len(r_sems)):
            groups.append(refs[p:p + n])
            p += n
        ins, rin, outs, rout, scr, rsem = groups

        def each(what):
            a = o = s = 0
            for r in riders:
                fn = what(r)
                if fn is not None:
                    fn(rin[a:a + len(r.arrays)], rout[o:o + len(r.out_shape)], rsem[s:s + len(r.sems)])
                a, o, s = a + len(r.arrays), o + len(r.out_shape), s + len(r.sems)

        if riders:
            lin = 0
            for d, g in enumerate(grid):
                lin = lin * g + pl.program_id(d)
            pl.when(lin == 0)(lambda: each(lambda r: r.start))
            for t in range(n_hooks):
                pl.when(lin == min(total - 1, ((t + 1) * total) // n_hooks))(
                    lambda t=t: each(lambda r: r.steps[t] if t < len(r.steps) else None))
        body(*ins, *outs, *scr)
        if riders:
            pl.when(lin == total - 1)(lambda: each(lambda r: r.finish))

    specs = dict(grid=grid, in_specs=list(in_specs) + [ANY] * len(r_arrays),
                 out_specs=list(out_specs) + [ANY] * len(r_outs), scratch_shapes=list(scratch_shapes) + r_sems)
    common = dict(name=name, out_shape=list(out_shape) + r_outs, input_output_aliases=aliases,
                  compiler_params=_params(("arbitrary",) * len(grid) if riders else sem))
    if scalars is None:
        res = _pcall(full, **specs, **common)(*arrays, *r_arrays)
    else:
        res = _pcall(lambda _, *refs: full(*refs), **common,
                     grid_spec=pltpu.PrefetchScalarGridSpec(num_scalar_prefetch=1, **specs))(scalars, *arrays, *r_arrays)
    if not riders:
        return res
    main, rest, per = res[:n_out], res[n_out:], []
    for r in riders:
        per.append(rest[:len(r.out_shape)])
        rest = rest[len(r.out_shape):]
    return main, per


def _run_rider(name, rider):
    def body(*refs):
        n_a, n_o = len(rider.arrays), len(rider.out_shape)
        ins, outs, sems = refs[:n_a], refs[n_a:n_a + n_o], refs[n_a + n_o:]
        for fn in [rider.start] + rider.steps + [rider.finish]:
            fn(ins, outs, sems)

    return _pcall(body, name=name, out_shape=rider.out_shape, in_specs=[ANY] * len(rider.arrays),
                  out_specs=[ANY] * len(rider.out_shape), scratch_shapes=rider.sems,
                  input_output_aliases=rider.aliases)(*rider.arrays)


def _mm(name, pairs, *, M, N, K, ta=False, tb=False, tm=512, tn=1024, tk=1024,
        a_pro=None, b_pro=None, extras=(), outs, epi, riders=()):
    tm, tn, tk = _tile(M, tm), _tile(N, tn), _tile(K, tk)
    n_i, n_j, n_k = M // tm, N // tn, K // tk
    n_p, n_e = len(pairs), len(extras)
    arrays, in_specs = [], []
    for a, _ in pairs:
        arrays.append(a)
        in_specs.append(pl.BlockSpec((tk, tm), lambda i, j, k: (k, i)) if ta
                        else pl.BlockSpec((tm, tk), lambda i, j, k: (i, k)))
    for _, b in pairs:
        arrays.append(b)
        in_specs.append(pl.BlockSpec((tn, tk), lambda i, j, k: (j, k)) if tb
                        else pl.BlockSpec((tk, tn), lambda i, j, k: (k, j)))
    for arr, kind, off in extras:
        ob = off // tn
        assert off % tn == 0
        arrays.append(arr)
        if kind == "tile":
            in_specs.append(pl.BlockSpec((tm, tn), lambda i, j, k, ob=ob: (i, j + ob)))
        else:
            in_specs.append(pl.BlockSpec((1, tn), lambda i, j, k, ob=ob: (0, j + ob)))
    out_shape, out_specs = [], []
    for o in outs:
        if o["kind"] == "tile":
            out_shape.append(jax.ShapeDtypeStruct((M, N), o["dtype"]))
            out_specs.append(pl.BlockSpec((tm, tn), lambda i, j, k: (i, j)))
        else:
            out_shape.append(jax.ShapeDtypeStruct((n_i, 1, N), F32))
            out_specs.append(pl.BlockSpec((1, 1, tn), lambda i, j, k: (i, 0, j)))
    dims = (((0 if ta else 1,), (1 if tb else 0,)), ((), ()))

    def body(*refs):
        a_refs, b_refs = refs[:n_p], refs[n_p:2 * n_p]
        e_refs = refs[2 * n_p:2 * n_p + n_e]
        o_refs = refs[2 * n_p + n_e:2 * n_p + n_e + len(outs)]
        acc_refs = refs[2 * n_p + n_e + len(outs):]

        def product(p):
            a, b = a_refs[p][...], b_refs[p][...]
            if a_pro is not None:
                a = a_pro(a)
            if b_pro is not None:
                b = b_pro(b)
            return lax.dot_general(a, b, dims, preferred_element_type=F32)

        def write(accs):
            vals = epi(accs, [e[...] for e in e_refs])
            for o, o_ref, val in zip(outs, o_refs, vals):
                if o["kind"] == "tile":
                    o_ref[...] = val.astype(o_ref.dtype)
                else:
                    o_ref[0] = val

        if n_k == 1:
            write([product(p) for p in range(n_p)])
            return
        k = pl.program_id(2)

        @pl.when(k == 0)
        def _():
            for acc in acc_refs:
                acc[...] = jnp.zeros_like(acc)

        for p in range(n_p):
            acc_refs[p][...] += product(p)

        pl.when(k == n_k - 1)(lambda: write([acc[...] for acc in acc_refs]))

    return _ride(name, body, riders, arrays, grid=(n_i, n_j, n_k), in_specs=in_specs, out_specs=out_specs,
                 out_shape=out_shape, scratch_shapes=[pltpu.VMEM((tm, tn), F32) for _ in pairs] if n_k > 1 else [],
                 sem=("parallel", "parallel", "arbitrary"))


def _tile_out(dtype):
    return {"kind": "tile", "dtype": dtype}


_COLSUM = {"kind": "colsum"}


def _colsum(v):
    return jnp.sum(v, axis=0, keepdims=True)


def _norm_mod(name, x, norm_w, scale, shift):
    S, D = x.shape
    tr = _tile(S, 256)

    def body(x_ref, nw_ref, sc_ref, sh_ref, h_ref):
        xv = x_ref[...]
        r = lax.rsqrt(jnp.mean(xv * xv, axis=-1, keepdims=True) + EPS)
        h_ref[...] = ((xv * r * nw_ref[...]) * (1.0 + sc_ref[...]) + sh_ref[...]).astype(BF16)

    row = pl.BlockSpec((1, D), lambda i: (0, 0))
    til = pl.BlockSpec((tr, D), lambda i: (i, 0))
    return _pcall(body, name=name, grid=(S // tr,), in_specs=[til, row, row, row], out_specs=til,
                  out_shape=jax.ShapeDtypeStruct((S, D), BF16), compiler_params=_params(("parallel",)))(
                      x, norm_w, scale, shift)


def _norm_mod_bwd(name, dh, x, dres, norm_w, scale, gate_o=None):
    S, D = x.shape
    tr = _tile(S, 256)
    n_r = S // tr
    with_gate = gate_o is not None

    def body(*refs):
        if with_gate:
            dh_ref, x_ref, dres_ref, nw_ref, sc_ref, o_ref, g_ref, dx_ref, p1, p2, p3, do_ref, p4 = refs
        else:
            dh_ref, x_ref, dres_ref, nw_ref, sc_ref, dx_ref, p1, p2, p3 = refs
        dhv, xv, nw = dh_ref[...], x_ref[...], nw_ref[...]
        r = lax.rsqrt(jnp.mean(xv * xv, axis=-1, keepdims=True) + EPS)
        xh = xv * r
        p1[0] = _colsum(dhv)
        p2[0] = _colsum(dhv * (xh * nw))
        dn = dhv * (1.0 + sc_ref[...])
        p3[0] = _colsum(dn * xh)
        dxh = dn * nw
        dx = dres_ref[...] + r * (dxh - xh * jnp.mean(dxh * xh, axis=-1, keepdims=True))
        dx_ref[...] = dx
        if with_gate:
            do_ref[...] = (dx * g_ref[...]).astype(BF16)
            p4[0] = _colsum(dx * o_ref[...].astype(F32))

    row = pl.BlockSpec((1, D), lambda i: (0, 0))
    til = pl.BlockSpec((tr, D), lambda i: (i, 0))
    part = pl.BlockSpec((1, 1, D), lambda i: (i, 0, 0))
    part_shape = jax.ShapeDtypeStruct((n_r, 1, D), F32)
    in_specs = [til, til, til, row, row]
    arrays = [dh, x, dres, norm_w, scale]
    out_specs = [til, part, part, part]
    out_shape = [jax.ShapeDtypeStruct((S, D), F32), part_shape, part_shape, part_shape]
    if with_gate:
        in_specs += [til, row]
        arrays += list(gate_o)
        out_specs += [til, part]
        out_shape += [jax.ShapeDtypeStruct((S, D), BF16), part_shape]
    return _pcall(body, name=name, grid=(n_r,), in_specs=in_specs, out_specs=out_specs, out_shape=out_shape,
                  compiler_params=_params(("parallel",)))(*arrays)


def _pool_w_specs(rows, cg):
    return [pl.BlockSpec((rows, cg), lambda g, j=j: (N_GROUPS * j + g, 0)) for j in range(N_CHIPS)]


def _pool_fwd(proj, wp_full, pool_scale, S, PW):
    cg = PW // N_GROUPS
    rows = cg // N_CHIPS
    T = _tile(S, POOL_T)
    n_t = S // T

    def body(u_ref, w0, w1, w2, w3, ps_ref, pooled_ref, pa_ref):
        g = pl.program_id(0)
        win = jnp.left_shift(2, g)
        w = jnp.concatenate([w0[...], w1[...], w2[...], w3[...]], axis=0)
        t_i = lax.broadcasted_iota(jnp.int32, (T, T), 0)
        j_i = lax.broadcasted_iota(jnp.int32, (T, T), 1)
        b_cur = ((j_i <= t_i) & (j_i > t_i - win)).astype(BF16)
        b_prev = (j_i - T > t_i - win).astype(BF16)
        row = lax.broadcasted_iota(jnp.int32, (T, 1), 0)
        for r in range(n_t):
            cur = u_ref[r * T:(r + 1) * T, :]
            ws = jnp.dot(b_cur, cur, preferred_element_type=F32)
            if r > 0:
                ws += jnp.dot(b_prev, u_ref[(r - 1) * T:r * T, :], preferred_element_type=F32)
            count = jnp.minimum(row + (r * T + 1), win).astype(F32)
            pooled = (ws / count - cur.astype(F32)).astype(BF16)
            pooled_ref[r * T:(r + 1) * T, :] = pooled
            mixed = jnp.dot(pooled, w, preferred_element_type=F32)
            pa_ref[r * T:(r + 1) * T, :] = (mixed * ps_ref[...]).astype(BF16)

    col = pl.BlockSpec((S, cg), lambda g: (0, g))
    return _pcall(
        body, name="pool_fwd", grid=(N_GROUPS,),
        in_specs=[col] + _pool_w_specs(rows, cg) + [pl.BlockSpec((1, cg), lambda g: (0, g))],
        out_specs=[col, col],
        out_shape=[jax.ShapeDtypeStruct((S, PW), BF16), jax.ShapeDtypeStruct((S, PW), BF16)],
        compiler_params=_params(("parallel",)),
    )(proj, wp_full, wp_full, wp_full, wp_full, pool_scale)


def _pool_bwd(dpa, pooled, wp_full, pool_scale, S, PW):
    cg = PW // N_GROUPS
    rows = cg // N_CHIPS
    T = _tile(S, POOL_T)
    n_t = S // T

    def body(dpa_ref, pooled_ref, w0, w1, w2, w3, ps_ref, du_ref, gw_ref, gs_ref, dp_s, dpc_s, dmx_s):
        g = pl.program_id(0)
        win = jnp.left_shift(2, g)
        w = jnp.concatenate([w0[...], w1[...], w2[...], w3[...]], axis=0)
        row = lax.broadcasted_iota(jnp.int32, (T, 1), 0)
        gs = jnp.zeros((1, cg), F32)
        for r in range(n_t):
            sl = slice(r * T, (r + 1) * T)
            mixed = jnp.dot(pooled_ref[sl, :], w, preferred_element_type=F32)
            dpa_t = dpa_ref[sl, :]
            gs += _colsum(dpa_t * mixed)
            dmx = (dpa_t * ps_ref[...]).astype(BF16)
            dmx_s[sl, :] = dmx
            dpo = lax.dot_general(dmx, w, (((1,), (1,)), ((), ())), preferred_element_type=F32)
            dp_s[sl, :] = dpo
            count = jnp.minimum(row + (r * T + 1), win).astype(F32)
            dpc_s[sl, :] = (dpo / count).astype(BF16)
        gs_ref[...] = gs
        gw = lax.dot_general(pooled_ref[...], dmx_s[...], (((0,), (0,)), ((), ())), preferred_element_type=F32)
        for j in range(N_CHIPS):
            gw_ref[j, 0] = gw[j * rows:(j + 1) * rows, :].astype(BF16)
        j_i = lax.broadcasted_iota(jnp.int32, (T, T), 0)
        t_i = lax.broadcasted_iota(jnp.int32, (T, T), 1)
        b_cur = ((t_i >= j_i) & (t_i < j_i + win)).astype(BF16)
        b_next = (t_i + T < j_i + win).astype(BF16)
        for r in range(n_t):
            sl = slice(r * T, (r + 1) * T)
            acc = jnp.dot(b_cur, dpc_s[sl, :], preferred_element_type=F32)
            if r + 1 < n_t:
                acc += jnp.dot(b_next, dpc_s[(r + 1) * T:(r + 2) * T, :], preferred_element_type=F32)
            du_ref[sl, :] = (acc - dp_s[sl, :]).astype(BF16)

    col = pl.BlockSpec((S, cg), lambda g: (0, g))
    return _pcall(
        body, name="pool_bwd", grid=(N_GROUPS,),
        in_specs=[col, col] + _pool_w_specs(rows, cg) + [pl.BlockSpec((1, cg), lambda g: (0, g))],
        out_specs=[col, pl.BlockSpec((N_CHIPS, 1, rows, cg), lambda g: (0, g, 0, 0)),
                   pl.BlockSpec((1, cg), lambda g: (0, g))],
        out_shape=[jax.ShapeDtypeStruct((S, PW), BF16),
                   jax.ShapeDtypeStruct((N_CHIPS, N_GROUPS, rows, cg), BF16),
                   jax.ShapeDtypeStruct((1, PW), F32)],
        scratch_shapes=[pltpu.VMEM((S, cg), F32), pltpu.VMEM((S, cg), BF16), pltpu.VMEM((S, cg), BF16)],
        compiler_params=_params(("parallel",)),
    )(dpa, pooled, wp_full, wp_full, wp_full, wp_full, pool_scale)


_NT = (((1,), (1,)), ((), ()))
_TN = (((0,), (0,)), ((), ()))


def _split_dot(v, tri):
    hi = v.astype(BF16)
    lo = (v - hi.astype(F32)).astype(BF16)
    return jnp.dot(hi, tri, preferred_element_type=F32) + jnp.dot(lo, tri, preferred_element_type=F32)


LOG2E = 1.4426950408889634
QK_SCALE = 1.0 / math.sqrt(HEAD_DIM)


def _sb_scores(q2_i, k_j, tri_l, masked):
    tq, tk = q2_i.shape[0], k_j.shape[0]
    s = lax.dot_general(q2_i, k_j, _NT, preferred_element_type=F32)
    lp = jnp.log(1.0 + jnp.exp2(-jnp.abs(s))) * LOG2E
    lb = jnp.minimum(s, 0.0) - lp
    l = lb - s
    mask = None
    if masked:
        mask = lax.broadcasted_iota(jnp.int32, (tq, tk), 0) > lax.broadcasted_iota(jnp.int32, (tq, tk), 1)
        l = jnp.where(mask, l, 0.0)
    return l, lb, lb + _split_dot(l, tri_l), mask


def _sb_weights(t, carry_l, mask):
    a = jnp.exp2(t + carry_l)
    return a if mask is None else jnp.where(mask, a, 0.0)


def _rowsum(v):
    return jnp.sum(v, axis=1, keepdims=True)


def _qk_norm(x_ref, w_ref):
    xv = x_ref[...].astype(F32)
    r = lax.rsqrt(jnp.mean(xv * xv, axis=-1, keepdims=True) + EPS)
    return xv * r, r


def _attn_fwd(proj, q_norm_w, k_norm_w, S, H, q_off, riders=()):
    t = _tile(S, ATT_T)
    n_q = S // t

    def body(q_ref, k_ref, v_ref, qw_ref, kw_ref, att_ref, attf_ref, qn_s, kn_s):
        qh, _ = _qk_norm(q_ref, qw_ref)
        qn_s[...] = (qh * qw_ref[...] * (QK_SCALE * LOG2E)).astype(BF16)
        kh, _ = _qk_norm(k_ref, kw_ref)
        kn_s[...] = (kh * kw_ref[...]).astype(BF16)
        tri_l = (lax.broadcasted_iota(jnp.int32, (t, t), 0) > lax.broadcasted_iota(jnp.int32, (t, t), 1)).astype(BF16)

        def rows(j):
            return pl.ds(pl.multiple_of(j * t, t), t)

        def q_step(i, _):
            q_i = qn_s[rows(i), :]

            def av(a, j):
                return jnp.dot(a.astype(BF16), v_ref[rows(j), :], preferred_element_type=F32)

            l, _, tt, mask = _sb_scores(q_i, kn_s[rows(i), :], tri_l, True)
            acc = av(_sb_weights(tt, 0.0, mask), i)
            carry = _rowsum(l)

            def single(_, c):
                carry, acc = c
                l, _, tt, _ = _sb_scores(q_i, kn_s[rows(i - 1), :], tri_l, False)
                return carry + _rowsum(l), acc + av(_sb_weights(tt, carry, None), i - 1)

            carry, acc = lax.fori_loop(0, i % 2, single, (carry, acc))
            top = i - 1 - i % 2

            def pair(p, c):
                carry, acc = c
                j0 = top - 2 * p
                l0, _, t0, _ = _sb_scores(q_i, kn_s[rows(j0), :], tri_l, False)
                l1, _, t1, _ = _sb_scores(q_i, kn_s[rows(j0 - 1), :], tri_l, False)
                mid = carry + _rowsum(l0)
                acc = acc + av(_sb_weights(t0, carry, None), j0) + av(_sb_weights(t1, mid, None), j0 - 1)
                return mid + _rowsum(l1), acc

            _, acc = lax.fori_loop(0, i // 2, pair, (carry, acc))
            att_ref[rows(i), :] = acc.astype(BF16)
            attf_ref[rows(i), :] = acc
            return 0

        lax.fori_loop(0, n_q, q_step, 0)

    def col(off):
        return pl.BlockSpec((S, HEAD_DIM), lambda h, off=off: (0, off + h))

    wspec = pl.BlockSpec((1, HEAD_DIM), lambda h: (0, 0))
    return _ride(
        "attn_fwd", body, riders, [proj, proj, proj, q_norm_w, k_norm_w], grid=(H,),
        in_specs=[col(q_off), col(q_off + H), col(q_off + 2 * H), wspec, wspec],
        out_specs=[col(0), col(0)],
        out_shape=[jax.ShapeDtypeStruct((S, H * HEAD_DIM), BF16), jax.ShapeDtypeStruct((S, H * HEAD_DIM), F32)],
        scratch_shapes=[pltpu.VMEM((S, HEAD_DIM), BF16), pltpu.VMEM((S, HEAD_DIM), BF16)],
        sem=("parallel",))


def _attn_bwd(proj, datt, attf, q_norm_w, k_norm_w, S, H, q_off, riders=()):
    t = _tile(S, ATT_T)
    n_q = S // t

    def body(q_ref, k_ref, v_ref, do_ref, o_ref, qw_ref, kw_ref, dq_ref, dk_ref, dv_ref, gq_ref, gk_ref,
             qn_s, kn_s, qz_s, kz_s, dk_s, dv_s, gq_s):
        qw, kw = qw_ref[...], kw_ref[...]
        qh, _ = _qk_norm(q_ref, qw_ref)
        qn_s[...] = (qh * qw * (QK_SCALE * LOG2E)).astype(BF16)
        qz_s[...] = (qh * qw * QK_SCALE).astype(BF16)
        kh, _ = _qk_norm(k_ref, kw_ref)
        kn_s[...] = (kh * kw).astype(BF16)
        kz_s[...] = (kh * kw * QK_SCALE).astype(BF16)
        dk_s[...] = jnp.zeros_like(dk_s)
        dv_s[...] = jnp.zeros_like(dv_s)
        gq_s[...] = jnp.zeros_like(gq_s)
        r_i = lax.broadcasted_iota(jnp.int32, (t, t), 0)
        c_i = lax.broadcasted_iota(jnp.int32, (t, t), 1)
        tri_l = (r_i > c_i).astype(BF16)
        tri_e = (r_i >= c_i).astype(BF16)

        def rows(j):
            return pl.ds(pl.multiple_of(j * t, t), t)

        def q_step(i, _):
            q_i = qn_s[rows(i), :]
            do_i = do_ref[rows(i), :]
            d_i = _rowsum(do_i.astype(F32) * o_ref[rows(i), :])

            def scores(j, masked):
                l, lb, tt, mask = _sb_scores(q_i, kn_s[rows(j), :], tri_l, masked)
                da = lax.dot_general(do_i, v_ref[rows(j), :], _NT, preferred_element_type=F32)
                return l, lb, tt, mask, da

            def grads(j, sc, carry_l, carry_e, dq_acc):
                l, lb, tt, mask, da = sc
                a_bf = _sb_weights(tt, carry_l, mask).astype(BF16)
                e = da * a_bf.astype(F32)
                p = (d_i - carry_e) - _split_dot(e, tri_e)
                dz = e - jnp.exp2(lb) * (e + p)
                if mask is not None:
                    dz = jnp.where(mask, dz, 0.0)
                dz = dz.astype(BF16)
                dk_s[rows(j), :] += lax.dot_general(dz, qz_s[rows(i), :], _TN, preferred_element_type=F32)
                dv_s[rows(j), :] += lax.dot_general(a_bf, do_i, _TN, preferred_element_type=F32)
                return (carry_l + _rowsum(l), carry_e + _rowsum(e),
                        dq_acc + jnp.dot(dz, kz_s[rows(j), :], preferred_element_type=F32))

            zero = jnp.zeros((t, 1), F32)
            first = (zero, zero, jnp.zeros((t, HEAD_DIM), F32))

            def group(js, diagonal_first, c):
                scs = [scores(j, diagonal_first and n == 0) for n, j in enumerate(js)]
                for j, sc in zip(js, scs):
                    c = grads(j, sc, *c)
                return c

            n_first = i % ATT_GROUP
            c = lax.switch(n_first, [functools.partial(group, [i - u for u in range(n + 1)], True, first)
                                     for n in range(ATT_GROUP)])
            top = i - 1 - n_first

            def whole(p, c):
                j0 = top - ATT_GROUP * p
                return group([j0 - u for u in range(ATT_GROUP)], False, c)

            _, _, dqn = lax.fori_loop(0, (i - n_first) // ATT_GROUP, whole, c)
            qv = q_ref[rows(i), :].astype(F32)
            r = lax.rsqrt(jnp.mean(qv * qv, axis=-1, keepdims=True) + EPS)
            xh = qv * r
            gq_s[...] += _colsum(dqn * xh)
            dxh = dqn * qw
            dq_ref[rows(i), :] = (r * (dxh - xh * jnp.mean(dxh * xh, axis=-1, keepdims=True))).astype(BF16)
            return 0

        lax.fori_loop(0, n_q, q_step, 0)
        gq_ref[0] = gq_s[...]
        kh, rk = _qk_norm(k_ref, kw_ref)
        dkn = dk_s[...]
        gk_ref[0] = _colsum(dkn * kh)
        dxh = dkn * kw
        dk_ref[...] = (rk * (dxh - kh * jnp.mean(dxh * kh, axis=-1, keepdims=True))).astype(BF16)
        dv_ref[...] = dv_s[...].astype(BF16)

    def col(off):
        return pl.BlockSpec((S, HEAD_DIM), lambda h, off=off: (0, off + h))

    wspec = pl.BlockSpec((1, HEAD_DIM), lambda h: (0, 0))
    gspec = pl.BlockSpec((1, 1, HEAD_DIM), lambda h: (h, 0, 0))
    act = jax.ShapeDtypeStruct((S, H * HEAD_DIM), BF16)
    gsh = jax.ShapeDtypeStruct((H, 1, HEAD_DIM), F32)
    return _ride(
        "attn_bwd", body, riders, [proj, proj, proj, datt, attf, q_norm_w, k_norm_w], grid=(H,),
        in_specs=[col(q_off), col(q_off + H), col(q_off + 2 * H), col(0), col(0), wspec, wspec],
        out_specs=[col(0), col(0), col(0), gspec, gspec],
        out_shape=[act, act, act, gsh, gsh],
        scratch_shapes=[pltpu.VMEM((S, HEAD_DIM), BF16)] * 4 + [pltpu.VMEM((S, HEAD_DIM), F32)] * 2
        + [pltpu.VMEM((1, HEAD_DIM), F32)],
        sem=("parallel",))


def _place():
    x, y, c = lax.axis_index("x"), lax.axis_index("y"), lax.axis_index("c")
    chips = [(1 - x, y), (x, 1 - y), (1 - x, 1 - y)]
    return x, y, c, chips


def _dev_allgather(name, v):
    m_per, n = v.shape

    def body(x_ref, out_ref, send_sems, recv_sems, local_sem):
        x, y, c, _ = _place()
        me = (x, y, c)

        def rows(px, py, pc):
            return out_ref.at[pl.ds((4 * px + 2 * py + pc) * m_per, m_per), :]

        def peer(r):
            return tuple(1 - b if (r >> s) & 1 else b for b, s in zip(me, (2, 1, 0)))

        def copy(r, block, to, src=None):
            return pltpu.make_async_remote_copy(
                src_ref=rows(*block) if src is None else src, dst_ref=rows(*block),
                send_sem=send_sems.at[r - 1], recv_sem=recv_sems.at[r - 1], device_id=to, device_id_type=MESH)

        mine = pltpu.make_async_copy(x_ref, rows(*me), local_sem)
        mine.start()
        sends = [copy(r, me, peer(r), src=x_ref) for r in range(1, N_DEV)]
        for cp in sends:
            cp.start()
        for r in range(1, N_DEV):
            copy(r, peer(r), me).wait_recv()
        for cp in sends:
            cp.wait_send()
        mine.wait()

    return _pcall(
        body, name=name, out_shape=jax.ShapeDtypeStruct((N_DEV * m_per, n), v.dtype),
        in_specs=[pl.BlockSpec(memory_space=pltpu.VMEM)], out_specs=pl.BlockSpec(memory_space=pltpu.VMEM),
        scratch_shapes=[pltpu.SemaphoreType.DMA((7,)), pltpu.SemaphoreType.DMA((7,)), pltpu.SemaphoreType.DMA],
        compiler_params=pltpu.CompilerParams(vmem_limit_bytes=VMEM_LIMIT_V7X),
    )(v)


class _W:
    def __init__(self, name, kind, R, C):
        self.name, self.kind, self.R, self.C = name, kind, R, C

    @property
    def shard_shape(self):
        return (self.R, self.C // N_CHIPS) if self.kind == "col" else (self.R // N_CHIPS, self.C)

    @property
    def half_rows(self):
        return self.shard_shape[0] // 2

    def shard_half(self, ref, half):
        return ref.at[pl.ds(half * self.half_rows, self.half_rows), :]

    def region(self, full_ref, chip, half):
        hr = self.half_rows
        if self.kind == "col":
            cw = self.C // N_CHIPS
            return full_ref.at[pl.ds(half * hr, hr), pl.ds(chip * cw, cw)]
        return full_ref.at[pl.ds(chip * (2 * hr) + half * hr, hr), :]


def _ag_rider(ws, fulls, n_ch=4, chunks=None):
    n_w = len(ws)
    lo, hi = chunks or (0, n_ch)
    per = 6

    def parts(full, sems):
        send_sems, recv_sems = sems
        x, y, c, _ = _place()
        xn, yn, dg = (1 - x, y), (x, 1 - y), (1 - x, 1 - y)
        via = (x + (1 - c) * (1 - 2 * x), y + c * (1 - 2 * y))
        to = (x + c * (1 - 2 * x), y + (1 - c) * (1 - 2 * y))

        def reg(i, chip, half, t):
            nr = ws[i].half_rows // n_ch
            return ws[i].region(full[i], 2 * chip[0] + chip[1], half).at[pl.ds(t * nr, nr), :]

        def copy(r, i, t, k, dev):
            s = (i * (hi - lo) + t - lo) * per + k
            return pltpu.make_async_remote_copy(src_ref=r, dst_ref=r, send_sem=send_sems.at[s],
                                                recv_sem=recv_sems.at[s], device_id=dev, device_id_type=MESH)

        def direct(i, t, k):
            return copy(reg(i, (x, y), c, t), i, t, k, (*(via, to)[k], c))

        def direct_in(i, t, k):
            return copy(reg(i, (via, to)[k], c, t), i, t, k, (*(via, to)[k], c))

        def relay(i, t):
            return copy(reg(i, via, c, t), i, t, 2, (*to, c))

        def relay_in(i, t):
            return copy(reg(i, dg, c, t), i, t, 2, (*to, c))

        def hand(i, t, k, half):
            return copy(reg(i, (xn, yn, dg)[k], half, t), i, t, 3 + k, (x, y, 1 - c))

        return c, direct, direct_in, relay, relay_in, hand

    def start(_, full, sems):
        _, direct, _, _, _, _ = parts(full, sems)
        for t in range(lo, hi):
            for i in range(n_w):
                direct(i, t, 0).start()
                direct(i, t, 1).start()

    def arrived(t):
        def step(_, full, sems):
            c, _, direct_in, relay, relay_in, hand = parts(full, sems)
            for i in range(n_w):
                direct_in(i, t, 0).wait_recv()
                direct_in(i, t, 1).wait_recv()
                relay(i, t).start()
                hand(i, t, 0, c).start()
                hand(i, t, 1, c).start()
        return step

    def finish(_, full, sems):
        c, direct, _, relay, relay_in, hand = parts(full, sems)
        for t in range(lo, hi):
            for i in range(n_w):
                relay_in(i, t).wait_recv()
                hand(i, t, 2, c).start()
        for i in range(n_w):
            for t in range(lo, hi):
                for k in range(3):
                    hand(i, t, k, 1 - c).wait_recv()
        for i in range(n_w):
            for t in range(lo, hi):
                direct(i, t, 0).wait_send()
                direct(i, t, 1).wait_send()
                relay(i, t).wait_send()
                for k in range(3):
                    hand(i, t, k, c).wait_send()

    n_sem = per * (hi - lo) * n_w
    return _Rider(fulls, [jax.ShapeDtypeStruct((w.R, w.C), BF16) for w in ws],
                  [pltpu.SemaphoreType.DMA((n_sem,)), pltpu.SemaphoreType.DMA((n_sem,))], start, finish,
                  steps=[arrived(t) for t in range(lo, hi)], aliases={i: i for i in range(n_w)})


def _cast_into_full(ws, shards, chip_arr, riders=()):
    sr, sc = ws[0].shard_shape
    assert all(w.shard_shape == (sr, sc) for w in ws)
    tr, tc = _tile(sr, 512), _tile(sc, 2048)
    n_r, n_c = sr // tr, sc // tc

    def place(w):
        if w.kind == "col":
            return pl.BlockSpec((tr, tc), lambda i, j, chip: (i, chip[0] * n_c + j))
        return pl.BlockSpec((tr, tc), lambda i, j, chip: (chip[0] * n_r + i, j))

    def body(*refs):
        for a_ref, o_ref in zip(refs[:len(ws)], refs[len(ws):]):
            o_ref[...] = a_ref[...].astype(BF16)

    return _ride("cast_" + "_".join(w.name for w in ws), body, riders, list(shards), grid=(n_r, n_c),
                 in_specs=[pl.BlockSpec((tr, tc), lambda i, j, chip: (i, j))] * len(ws),
                 out_specs=[place(w) for w in ws], out_shape=[jax.ShapeDtypeStruct((w.R, w.C), BF16) for w in ws],
                 scratch_shapes=[], sem=("parallel", "parallel"), scalars=chip_arr)


def _half_view(w, g):
    return g if w.kind == "col" else g.reshape(N_CHIPS, w.R // N_CHIPS, w.C)


def _px_rider(ws, grads):
    n_w = len(ws)

    def copies(g, got, sems):
        send_sems, recv_sems = sems
        x, y, c, _ = _place()

        def half_all(w, ref, half):
            hr = w.half_rows
            if w.kind == "col":
                return ref.at[pl.ds(half * hr, hr), :]
            return ref.at[:, pl.ds(half * hr, hr), :]

        return [pltpu.make_async_remote_copy(
            src_ref=half_all(w, g[i], 1 - c), dst_ref=got[i], send_sem=send_sems.at[i], recv_sem=recv_sems.at[i],
            device_id=(x, y, 1 - c), device_id_type=MESH) for i, w in enumerate(ws)]

    def start(g, got, sems):
        for cp in copies(g, got, sems):
            cp.start()

    def finish(g, got, sems):
        for cp in copies(g, got, sems):
            cp.wait_recv()
            cp.wait_send()

    def got_shape(w):
        hr = w.half_rows
        return (hr, w.C) if w.kind == "col" else (N_CHIPS, hr, w.C)

    return _Rider([_half_view(w, g) for w, g in zip(ws, grads)],
                  [jax.ShapeDtypeStruct(got_shape(w), BF16) for w in ws],
                  [pltpu.SemaphoreType.DMA((n_w,)), pltpu.SemaphoreType.DMA((n_w,))], start, finish)


def _pair_sum(w, g, got, c_arr):
    hr = w.half_rows
    if w.kind == "col":
        tr, tc = _tile(hr, 512), _tile(w.C, 2048)
        n_r = hr // tr
        grid = (n_r, w.C // tc)
        g_spec = pl.BlockSpec((tr, tc), lambda i, j, c: (c[0] * n_r + i, j))
        o_spec = pl.BlockSpec((tr, tc), lambda i, j, c: (i, j))
    else:
        tr = _tile(hr, 512)
        n_r = hr // tr
        grid = (N_CHIPS, n_r)
        g_spec = pl.BlockSpec((1, tr, w.C), lambda s, i, c: (s, c[0] * n_r + i, 0))
        o_spec = pl.BlockSpec((1, tr, w.C), lambda s, i, c: (s, i, 0))

    def body(c_ref, g_ref, got_ref, out_ref):
        out_ref[...] = (g_ref[...].astype(F32) + got_ref[...].astype(F32)).astype(BF16)

    return _pcall(
        body, name="grad_pair_sum_" + w.name, out_shape=jax.ShapeDtypeStruct(got.shape, BF16),
        grid_spec=pltpu.PrefetchScalarGridSpec(num_scalar_prefetch=1, grid=grid, in_specs=[g_spec, o_spec],
                                               out_specs=o_spec),
        compiler_params=_params(("parallel", "parallel")),
    )(c_arr, _half_view(w, g), got)


def _chip_sum(w, p, q, cc_arr):
    hr, cols = w.half_rows, w.shard_shape[1]
    tr, tc = _tile(hr, 512), _tile(cols, 2048)
    n_r, n_c = hr // tr, cols // tc

    def body(cc_ref, own, q1, q2, q3, out_ref):
        own_v = own[...] if w.kind == "col" else own[0]
        out_ref[...] = ((own_v.astype(F32) + q1[0].astype(F32)) + q2[0].astype(F32)) + q3[0].astype(F32)

    if w.kind == "col":
        own_spec = pl.BlockSpec((tr, tc), lambda i, j, cc: (i, cc[1] * n_c + j))
    else:
        own_spec = pl.BlockSpec((1, tr, tc), lambda i, j, cc: (cc[1], i, j))
    q_specs = [pl.BlockSpec((1, tr, tc), lambda i, j, cc, s=s: ((cc[1] + s) % N_CHIPS, i, j)) for s in (1, 2, 3)]
    return _pcall(
        body, name="grad_chip_sum_" + w.name, out_shape=jax.ShapeDtypeStruct(w.shard_shape, F32),
        grid_spec=pltpu.PrefetchScalarGridSpec(
            num_scalar_prefetch=1, grid=(n_r, n_c), in_specs=[own_spec] + q_specs,
            out_specs=pl.BlockSpec((tr, tc), lambda i, j, cc: (cc[0] * n_r + i, j))),
        compiler_params=_params(("parallel", "parallel")),
    )(cc_arr, p, q, q, q)


_SEM = pl.BlockSpec(memory_space=pltpu.SEMAPHORE)
_HBM = pl.BlockSpec(memory_space=pltpu.HBM)


def _split_copies(kind, ws, p, land, send_sems, recv_sems):
    x, y, c, chips = _place()
    my_chip = 2 * x + y
    pairs = []
    for i, w in enumerate(ws):
        if kind == "pair":
            hr = w.half_rows
            src = p[i].at[pl.ds((1 - c) * hr, hr), :] if w.kind == "col" else p[i].at[:, pl.ds((1 - c) * hr, hr), :]
            cp = pltpu.make_async_remote_copy(src_ref=src, dst_ref=land[i], send_sem=send_sems.at[i],
                                              recv_sem=recv_sems.at[i], device_id=(x, y, 1 - c), device_id_type=MESH)
            pairs.append((cp, cp))
            continue
        for k, chip in enumerate(chips):
            to_chip = 2 * chip[0] + chip[1]
            src = p[i].at[:, pl.ds(to_chip * (w.C // N_CHIPS), w.C // N_CHIPS)] if w.kind == "col" else p[i].at[to_chip]
            kw = dict(send_sem=send_sems.at[3 * i + k], recv_sem=recv_sems.at[3 * i + k], device_id=(*chip, c),
                      device_id_type=MESH)
            pairs.append((pltpu.make_async_remote_copy(src_ref=src, dst_ref=land[i].at[my_chip], **kw),
                          pltpu.make_async_remote_copy(src_ref=src, dst_ref=land[i].at[to_chip], **kw)))
    return pairs


def _split_start(name, kind, ws, arrays):
    n_w = len(ws)
    if kind == "pair":
        arrays = [_half_view(w, g) for w, g in zip(ws, arrays)]
        lands = [lax.empty((w.half_rows, w.C) if w.kind == "col" else (N_CHIPS, w.half_rows, w.C), BF16) for w in ws]
    else:
        lands = [lax.empty((N_CHIPS, w.half_rows, w.shard_shape[1]), BF16) for w in ws]
    n_sem = n_w if kind == "pair" else 3 * n_w

    def body(*refs):
        p, land = refs[:n_w], refs[n_w:2 * n_w]
        for out, _ in _split_copies(kind, ws, p, land, refs[2 * n_w], refs[2 * n_w + 1]):
            out.start()
        refs[-1][...] = jnp.zeros_like(refs[-1])

    arrays = [pltpu.with_memory_space_constraint(a, pltpu.HBM) for a in list(arrays) + lands]
    res = _pcall(
        body, name=name,
        out_shape=(pltpu.SemaphoreType.DMA((n_sem,)), pltpu.SemaphoreType.DMA((n_sem,)),
                   *[pltpu.HBM(a.shape, a.dtype) for a in arrays], jax.ShapeDtypeStruct((SUBLANES, LANES), F32)),
        in_specs=[_HBM] * (2 * n_w),
        out_specs=(_SEM, _SEM, *[_HBM] * (2 * n_w), pl.BlockSpec(memory_space=pltpu.VMEM)),
        input_output_aliases={i: 2 + i for i in range(2 * n_w)},
        compiler_params=pltpu.CompilerParams(has_side_effects=pltpu.SideEffectType.DATAFLOW_SIDE_EFFECTING),
    )(*arrays)
    return (kind, ws, res[0], res[1], list(res[2:2 + n_w]), list(res[2 + n_w:2 + 2 * n_w])), res[-1]


def _split_wait(name, flight, after):
    kind, ws, send_sems, recv_sems, arrays, lands = flight
    n_w = len(ws)

    def body(*refs):
        p, land = refs[:n_w], refs[n_w:2 * n_w]
        for _, cp in _split_copies(kind, ws, p, land, refs[2 * n_w], refs[2 * n_w + 1]):
            cp.wait_send()
            cp.wait_recv()

    res = _pcall(
        body, name=name,
        out_shape=[pltpu.HBM(a.shape, a.dtype) for a in list(arrays) + list(lands)],
        in_specs=[_HBM] * (2 * n_w) + [_SEM, _SEM] + [ANY] * len(after), out_specs=[_HBM] * (2 * n_w),
        input_output_aliases={i: i for i in range(2 * n_w)},
        compiler_params=pltpu.CompilerParams(has_side_effects=pltpu.SideEffectType.DATAFLOW_SIDE_EFFECTING),
    )(*arrays, *lands, send_sems, recv_sems, *after)
    return list(res[:n_w]), list(res[n_w:])


def _sf_rider(ws, grads):
    n_w = len(ws)

    def copy(g, sems, i, half):
        send_sems, recv_sems = sems
        x, y, c, _ = _place()
        h = c if half == "mine" else 1 - c
        reg = ws[i].shard_half(g[i], h)
        return pltpu.make_async_remote_copy(src_ref=reg, dst_ref=reg, send_sem=send_sems.at[i], recv_sem=recv_sems.at[i],
                                            device_id=(x, y, 1 - c), device_id_type=MESH)

    def start(_, g, sems):
        for i in range(n_w):
            copy(g, sems, i, "mine").start()

    def finish(_, g, sems):
        for i in range(n_w):
            copy(g, sems, i, "other").wait_recv()
            copy(g, sems, i, "mine").wait_send()

    return _Rider(grads, [jax.ShapeDtypeStruct(w.shard_shape, F32) for w in ws],
                  [pltpu.SemaphoreType.DMA((n_w,)), pltpu.SemaphoreType.DMA((n_w,))], start, finish,
                  aliases={i: i for i in range(n_w)})


def _adamw_math(w, g, m, v):
    m = ADAM_B1 * m + (1.0 - ADAM_B1) * g
    v = ADAM_B2 * v + (1.0 - ADAM_B2) * (g * g)
    m_hat = m / (1.0 - ADAM_B1 ** ADAM_STEP)
    v_hat = v / (1.0 - ADAM_B2 ** ADAM_STEP)
    delta = -ADAM_LR * (m_hat / (jnp.sqrt(v_hat) + ADAM_EPS) + ADAM_WD * w)
    return delta, m, v


def _adamw(name, w, g, m, v, after=None):
    R, C = w.shape
    tr, tc = _tile(R, 256), _tile(C, 2048)
    behind = [] if after is None else [after]

    def body(w_ref, g_ref, m_ref, v_ref, *rest):
        g_out, d_out, m_out, v_out = rest[len(behind):]
        g = g_ref[...]
        g_out[...] = g
        d_out[...], m_out[...], v_out[...] = _adamw_math(w_ref[...], g, m_ref[...], v_ref[...])

    spec = pl.BlockSpec((tr, tc), lambda i, j: (i, j))
    sh = jax.ShapeDtypeStruct((R, C), F32)
    return _pcall(body, name=name, grid=(R // tr, C // tc), in_specs=[spec] * 4 + [ANY] * len(behind),
                  out_specs=[spec] * 4, out_shape=[sh] * 4, compiler_params=_params(("parallel", "parallel")))(
                      w, g, m, v, *behind)


def _ada_update(sct, dmod_sh, w, m, v, riders=()):
    R, C = w.shape
    tr, tc = _tile(R, 512), _tile(C, 1024)

    def body(s_ref, d_ref, w_ref, m_ref, v_ref, g_out, d_out, m_out, v_out):
        s, d = s_ref[...], d_ref[...]
        g = s[:, 0:1] * d[0:1, :]
        for b in range(1, N_DEV):
            g += s[:, b:b + 1] * d[b:b + 1, :]
        g_out[...] = g
        d_out[...], m_out[...], v_out[...] = _adamw_math(w_ref[...], g, m_ref[...], v_ref[...])

    spec = pl.BlockSpec((tr, tc), lambda i, j: (i, j))
    sh = jax.ShapeDtypeStruct((R, C), F32)
    return _ride(
        "ada_update", body, riders, [sct, dmod_sh, w, m, v], grid=(R // tr, C // tc),
        in_specs=[pl.BlockSpec((tr, N_DEV), lambda i, j: (i, 0)), pl.BlockSpec((N_DEV, tc), lambda i, j: (0, j)),
                  spec, spec, spec],
        out_specs=[spec] * 4, out_shape=[sh] * 4, scratch_shapes=[], sem=("parallel", "parallel"))


def _silu_rows(c_row):
    D = c_row.shape[1]

    def body(c_ref, o_ref):
        cv = c_ref[...]
        o_ref[...] = cv * jax.nn.sigmoid(cv)

    return _pcall(body, name="silu_c", out_shape=jax.ShapeDtypeStruct((1, D), F32))(c_row)


def _pack_partials(parts, widths, total):
    n = len(widths)

    def body(*refs):
        loss_p, out_ref = refs[n], refs[n + 1]
        off = 0
        for ref, wd in zip(refs[:n], widths):
            out_ref[:, off:off + wd] = jnp.sum(ref[...], axis=0)
            off += wd
        loss = jnp.sum(jnp.sum(loss_p[...], axis=0), axis=1, keepdims=True)
        out_ref[:, off:off + LANES] = jnp.broadcast_to(loss, (1, LANES))
        if off + LANES < total:
            out_ref[:, off + LANES:total] = jnp.zeros((1, total - off - LANES), F32)

    return _pcall(body, name="pack_partials", out_shape=jax.ShapeDtypeStruct((1, total), F32))(*parts)


def _small_update(gathered, offsets, params, loss_off):
    n_p = len(params)

    def over_devices(g_ref, off, wd):
        blk = g_ref[:, off:off + wd]
        g = blk[0:1, :]
        for b in range(1, N_DEV):
            g = g + blk[b:b + 1, :]
        return g

    def body(*refs):
        g_ref = refs[0]
        prm = refs[1:1 + 3 * n_p]
        outs = refs[1 + 3 * n_p:]
        outs[4 * n_p][...] = over_devices(g_ref, loss_off, LANES)
        for i, (off, wd) in enumerate(offsets):
            g = over_devices(g_ref, off, wd)
            w, m, v = prm[3 * i][...], prm[3 * i + 1][...], prm[3 * i + 2][...]
            outs[4 * i][...] = g
            outs[4 * i + 1][...], outs[4 * i + 2][...], outs[4 * i + 3][...] = _adamw_math(w, g, m, v)

    flat = [a for t in params for a in t]
    out_shape = [jax.ShapeDtypeStruct(t[0].shape, F32) for t in params for _ in range(4)]
    out_shape.append(jax.ShapeDtypeStruct((1, LANES), F32))
    return _pcall(body, name="small_update", out_shape=out_shape)(gathered, *flat)


def kernel(x, c, w_ada, b_ada, norm1_w, w_in, q_norm_w, k_norm_w, w_pool, pool_scale, w_a_up, w_b_up, w_o, norm2_w, w_ff1, w_ff2, loss_target, m_w_ada, m_b_ada, m_norm1_w, m_w_in, m_q_norm_w, m_k_norm_w, m_w_pool, m_pool_scale, m_w_a_up, m_w_b_up, m_w_o, m_norm2_w, m_w_ff1, m_w_ff2, v_w_ada, v_b_ada, v_norm1_w, v_w_in, v_q_norm_w, v_k_norm_w, v_w_pool, v_pool_scale, v_w_a_up, v_w_b_up, v_w_o, v_norm2_w, v_w_ff1, v_w_ff2):
    _, S, D = x.shape
    PW = D // 2
    H = PW // HEAD_DIM
    cg = PW // N_GROUPS
    IN = w_in.shape[2] * N_CHIPS
    FF = w_ff1.shape[2] * N_CHIPS
    A_COLS = w_ada.shape[2]
    xi, yi, ci = lax.axis_index("x"), lax.axis_index("y"), lax.axis_index("c")
    chip = 2 * xi + yi
    dev = 2 * chip + ci
    c_arr = jnp.reshape(ci, (1,)).astype(jnp.int32)
    x2, tgt = x[0], loss_target[0]

    ws = [_W("w_in", "col", D, IN), _W("w_pool", "row", PW, cg), _W("w_a_up", "col", PW, D),
          _W("w_b_up", "col", PW, D), _W("w_o", "row", D, D), _W("w_ff1", "col", D, FF), _W("w_ff2", "row", FF, D)]
    w32 = [w_in[0], w_pool[0].reshape(cg, cg), w_a_up[0], w_b_up[0], w_o[0], w_ff1[0], w_ff2[0]]
    m32 = [m_w_in[0], m_w_pool[0].reshape(cg, cg), m_w_a_up[0], m_w_b_up[0], m_w_o[0], m_w_ff1[0], m_w_ff2[0]]
    v32 = [v_w_in[0], v_w_pool[0].reshape(cg, cg), v_w_a_up[0], v_w_b_up[0], v_w_o[0], v_w_ff1[0], v_w_ff2[0]]

    W_IN, W_POOL, W_A, W_B, W_O, W_FF1, W_FF2 = ws
    chip_arr = jnp.reshape(chip, (1,)).astype(jnp.int32)
    cc_arr = jnp.stack([ci, chip]).astype(jnp.int32)
    s_in, s_pool, s_a, s_b, s_o = [_cast_into_full([w], [a], chip_arr)[0] for w, a in zip(ws[:5], w32[:5])]
    (s_ff1, s_ff2), ((win_f,),) = _cast_into_full([W_FF1, W_FF2], w32[5:], chip_arr, riders=[_ag_rider([W_IN], [s_in])])

    sc_row = _silu_rows(c)
    sc_all = _dev_allgather("gather_silu_c", sc_row.reshape(SUBLANES, D // SUBLANES)).reshape(N_DEV, D)
    sc16 = jnp.concatenate([sc_all, jnp.zeros_like(sc_all)], axis=0)
    b_cols = lax.dynamic_slice(b_ada, (0, chip * A_COLS), (1, A_COLS))
    (mod_cols,) = _mm("mod_cols", [(sc16, w_ada[0])], M=2 * N_DEV, N=A_COLS, K=D, tm=16, tn=1024, tk=1024,
                      a_pro=lambda a: a.astype(BF16), b_pro=lambda b: b.astype(BF16),
                      extras=[(b_cols, "row", 0)], outs=[_tile_out(F32)], epi=lambda accs, ex: [accs[0] + ex[0]])
    mod_all = _dev_allgather("gather_mod", mod_cols[:N_DEV]).reshape(N_CHIPS, 2, N_DEV, A_COLS)
    mod_row = lax.dynamic_index_in_dim(mod_all[:, 0], dev, axis=1, keepdims=False).reshape(1, N_CHIPS * A_COLS)
    shift1, scale1, gate1, shift2, scale2, gate2 = [mod_row[:, i * D:(i + 1) * D] for i in range(6)]

    WIDE = dict(tm=2048, tn=1024, tk=2048)
    DEEP = dict(tm=1024, tn=1024, tk=2048)
    DEEPER = dict(tm=1024, tn=1024, tk=4096)
    h = _norm_mod("norm1_mod", x2, norm1_w, scale1, shift1)
    (proj,), ((wpool_f, wa_f, wb_f, wo_f),) = _mm(
        "in_proj", [(h, win_f)], M=S, N=IN, K=D, outs=[_tile_out(BF16)], epi=lambda accs, ex: [accs[0]], **WIDE,
        riders=[_ag_rider([W_POOL, W_A, W_B, W_O], [s_pool, s_a, s_b, s_o], n_ch=2)])
    pooled, pa = _pool_fwd(proj, wpool_f, pool_scale, S, PW)
    (att, attf), ((wff1_f,),) = _attn_fwd(proj, q_norm_w, k_norm_w, S, H, PW // HEAD_DIM,
                                          riders=[_ag_rider([W_FF1], [s_ff1])])

    def merge_epi(accs, ex):
        sa, sb = jax.nn.sigmoid(ex[0].astype(F32)), jax.nn.sigmoid(ex[1].astype(F32))
        return [sa * accs[0] + sb * accs[1], accs[0], accs[1]]

    (merged, ya, yb), (ff2_a,) = _mm("branch_up_merge", [(pa, wa_f), (att, wb_f)], M=S, N=D, K=PW,
                                     extras=[(proj, "tile", 4 * PW), (proj, "tile", 4 * PW + D)],
                                     outs=[_tile_out(BF16)] * 3, epi=merge_epi,
                                     riders=[_ag_rider([W_FF2], [s_ff2], chunks=(0, 1))])
    (x1, o), (ff2_b,) = _mm("out_proj", [(merged, wo_f)], M=S, N=D, K=D, extras=[(x2, "tile", 0), (gate1, "row", 0)],
                            outs=[_tile_out(F32), _tile_out(BF16)], epi=lambda accs, ex: [ex[0] + ex[1] * accs[0], accs[0]],
                            riders=[_ag_rider([W_FF2], ff2_a, chunks=(1, 2))], tm=2048, tn=512, tk=2048)
    h2 = _norm_mod("norm2_mod", x1, norm2_w, scale2, shift2)
    (rl,), ((wff2_f,),) = _mm("ff1", [(h2, wff1_f)], M=S, N=FF, K=D, outs=[_tile_out(BF16)], **WIDE,
                              epi=lambda accs, ex: [jnp.maximum(accs[0], 0.0)],
                              riders=[_ag_rider([W_FF2], ff2_b, chunks=(2, 4))])

    def square(a):
        af = a.astype(F32)
        return (af * af).astype(BF16)

    def loss_epi(accs, ex):
        x1_t, tgt_t, g2 = ex
        f = accs[0]
        diff = (x1_t + g2 * f) - tgt_t
        dy = diff * (1.0 / D)
        return [dy, dy * g2, _colsum(dy * f), _colsum(diff * diff)]

    dy, df, dgate2_p, loss_p = _mm("ff2_loss", [(rl, wff2_f)], M=S, N=D, K=FF, a_pro=square, **DEEP,
                                   extras=[(x1, "tile", 0), (tgt, "tile", 0), (gate2, "row", 0)],
                                   outs=[_tile_out(F32), _tile_out(BF16), _COLSUM, _COLSUM], epi=loss_epi)

    tied = []

    def behind(token, a):
        a, token = lax.optimization_barrier((a, token))
        tied.append(token)
        return a

    def pair_sums(group, partials, got):
        return [_pair_sum(w, g, r, c_arr) for w, g, r in zip(group, partials, got)]

    def chip_sums(group, sums, from_chips):
        return [_chip_sum(w, p, q, cc_arr) for w, p, q in zip(group, sums, from_chips)]

    first = lambda accs, ex: [accs[0]]
    gmm = dict(ta=True, outs=[_tile_out(BF16)], epi=first, **WIDE)
    (g_ff2,) = _mm("grad_w_ff2", [(rl, df)], M=FF, N=D, K=S, a_pro=square, ta=True, tm=512, tn=2048, tk=2048,
                   outs=[_tile_out(BF16)], epi=first)
    flight, token = _split_start("pair_w_ff2_start", "pair", [W_FF2], [g_ff2])
    (dz1,) = _mm("d_ff_hidden", [(behind(token, df), wff2_f)], M=S, N=FF, K=D, tb=True, extras=[(rl, "tile", 0)],
                 outs=[_tile_out(BF16)], epi=lambda accs, ex: [accs[0] * (2.0 * ex[0].astype(F32))], **WIDE)
    sum_ff2 = pair_sums([W_FF2], *_split_wait("pair_w_ff2_wait", flight, after=[dz1] + tied))
    chip_ff2, token = _split_start("chip_w_ff2_start", "chip", [W_FF2], sum_ff2)
    (g_ff1,) = _mm("grad_w_ff1", [(behind(token, h2), dz1)], M=D, N=FF, K=S, **gmm)
    flight, token = _split_start("pair_w_ff1_start", "pair", [W_FF1], [g_ff1])
    (dh2,) = _mm("d_h2", [(behind(token, dz1), wff1_f)], M=S, N=D, K=FF, tb=True, outs=[_tile_out(F32)], epi=first,
                 **DEEPER)
    sum_ff1 = pair_sums([W_FF1], *_split_wait("pair_w_ff1_wait", flight, after=[dh2] + tied))
    chip_ff1, token = _split_start("chip_w_ff1_start", "chip", [W_FF1], sum_ff1)
    dx1, dshift2_p, dscale2_p, gn2_p, do, dgate1_p = _norm_mod_bwd("norm2_bwd", behind(token, dh2), x1, dy, norm2_w, scale2,
                                                                   gate_o=(o, gate1))
    (g_wo,) = _mm("grad_w_o", [(merged, do)], M=D, N=D, K=S, **gmm)

    def gate_epi(accs, ex):
        dm = accs[0]
        sa, sb = jax.nn.sigmoid(ex[0].astype(F32)), jax.nn.sigmoid(ex[1].astype(F32))
        ya_t, yb_t = ex[2].astype(F32), ex[3].astype(F32)
        return [dm * sa, dm * sb, dm * ya_t * (sa * (1.0 - sa)), dm * yb_t * (sb * (1.0 - sb))]

    dya, dyb, dga, dgb = _mm("d_merged", [(do, wo_f)], M=S, N=D, K=D, tb=True, tm=1024, tn=512, tk=2048,
                             extras=[(proj, "tile", 4 * PW), (proj, "tile", 4 * PW + D), (ya, "tile", 0), (yb, "tile", 0)],
                             outs=[_tile_out(BF16)] * 4, epi=gate_epi)
    both = lambda accs, ex: [accs[0], accs[1]]
    g_wa, g_wb = _mm("grad_w_up", [(pa, dya), (att, dyb)], M=PW, N=D, K=S, ta=True, outs=[_tile_out(BF16)] * 2, epi=both,
                     **WIDE)
    mid = [W_A, W_B, W_O]
    flight, token = _split_start("pair_mid_start", "pair", mid, [g_wa, g_wb, g_wo])
    dpa, datt = _mm("d_branches", [(dya, wa_f), (behind(token, dyb), wb_f)], M=S, N=PW, K=D, tb=True,
                    outs=[_tile_out(F32), _tile_out(BF16)], epi=both, tm=1024, tn=512, tk=2048)
    sum_mid = pair_sums(mid, *_split_wait("pair_mid_wait", flight, after=[datt] + tied))
    chip_mid, token = _split_start("chip_mid_start", "chip", mid, sum_mid)
    du, g_wpool4, gscale_p = _pool_bwd(dpa, pooled, wpool_f, pool_scale, S, PW)
    dq, dk, dv, gq_p, gk_p = _attn_bwd(proj, behind(token, datt), attf, q_norm_w, k_norm_w, S, H, PW // HEAD_DIM)
    dproj = jnp.concatenate([du, dq, dk, dv, dga, dgb], axis=1)
    early = [W_FF1, W_FF2]
    sum_ff1, q_ff1 = _split_wait("chip_w_ff1_wait", chip_ff1, after=[dq] + tied)
    sum_ff2, q_ff2 = _split_wait("chip_w_ff2_wait", chip_ff2, after=[dq] + tied)
    halves_early = chip_sums(early, sum_ff1 + sum_ff2, q_ff1 + q_ff2)
    (g_win,), (grads_early,) = _mm("grad_w_in", [(h, dproj)], M=D, N=IN, K=S, riders=[_sf_rider(early, halves_early)],
                                   **gmm)
    last = [W_IN, W_POOL]
    g_last = [g_win, g_wpool4.reshape(PW, cg)]
    sum_mid, q_mid = _split_wait("chip_mid_wait", chip_mid, after=[g_win] + tied)
    halves_mid = chip_sums(mid, sum_mid, q_mid)
    (dh,), (got_last, grads_mid) = _mm("d_h", [(dproj, win_f)], M=S, N=D, K=IN, tb=True, outs=[_tile_out(F32)], epi=first,
                                       riders=[_px_rider(last, g_last), _sf_rider(mid, halves_mid)], **DEEPER)
    sum_last = pair_sums(last, g_last, got_last)
    grad_x, dshift1_p, dscale1_p, gn1_p = _norm_mod_bwd("norm1_bwd", dh, x2, dx1, norm1_w, scale1)

    parts = [dshift1_p, dscale1_p, dgate1_p, dshift2_p, dscale2_p, dgate2_p, gn1_p, gn2_p,
             gscale_p.reshape(1, 1, PW), gq_p, gk_p]
    widths = [D] * 8 + [PW, HEAD_DIM, HEAD_DIM]
    used = sum(widths)
    P = -(-(used + LANES) // (SUBLANES * LANES)) * (SUBLANES * LANES)
    packed = _pack_partials(parts + [loss_p], widths, P)
    gathered = _dev_allgather("gather_vector_grads", packed.reshape(SUBLANES, P // SUBLANES)).reshape(N_DEV, P)
    sum_last, gathered = lax.optimization_barrier((sum_last, gathered))
    chip_last, token = _split_start("chip_last_start", "chip", last, sum_last)
    small = [(b_ada, m_b_ada, v_b_ada), (norm1_w, m_norm1_w, v_norm1_w), (norm2_w, m_norm2_w, v_norm2_w),
             (pool_scale, m_pool_scale, v_pool_scale), (q_norm_w, m_q_norm_w, v_q_norm_w),
             (k_norm_w, m_k_norm_w, v_k_norm_w)]
    offsets = [(0, 6 * D), (6 * D, D), (7 * D, D), (8 * D, PW), (8 * D + PW, HEAD_DIM), (8 * D + PW + HEAD_DIM, HEAD_DIM)]
    su = _small_update(gathered, offsets, small, used)
    (g_b, d_b, nm_b, nv_b, g_n1, d_n1, nm_n1, nv_n1, g_n2, d_n2, nm_n2, nv_n2, g_ps, d_ps, nm_ps, nv_ps,
     g_qn, d_qn, nm_qn, nv_qn, g_kn, d_kn, nm_kn, nv_kn, loss_sum) = su
    dmod_sh = lax.dynamic_slice(gathered, (0, chip * A_COLS), (N_DEV, A_COLS))
    dmod_sh, token = lax.optimization_barrier((dmod_sh, token))
    g_ada, d_ada, nm_ada, nv_ada = _ada_update(sc_all.T, dmod_sh, w_ada[0], m_w_ada[0], v_w_ada[0])

    upd_done = [_adamw("adamw_" + w.name, a, g, m, v, after=token)
                for w, a, g, m, v in zip(ws[2:], w32[2:], list(grads_mid) + list(grads_early), m32[2:], v32[2:])]

    sum_last, q_last = _split_wait("chip_last_wait", chip_last, after=[nv_ada] + [u[3] for u in upd_done])
    halves_last = chip_sums(last, sum_last, q_last)
    filled = _run_rider("grad_sibling_fill", _sf_rider(last, halves_last))
    upd = [_adamw("adamw_" + w.name, a, g, m, v) for w, a, g, m, v in zip(ws[:2], w32[:2], filled, m32[:2], v32[:2])]
    upd += upd_done

    loss = (0.5 / D) * loss_sum[0, 0]

    def up(a):
        return a[None]

    def pool4(a):
        return a.reshape(1, N_GROUPS, cg // N_CHIPS, cg)

    (gr_win, d_win, nm_win, nv_win), (gr_wp, d_wp, nm_wp, nv_wp), (gr_wa, d_wa, nm_wa, nv_wa), \
        (gr_wb, d_wb, nm_wb, nv_wb), (gr_wo, d_wo, nm_wo, nv_wo), (gr_f1, d_f1, nm_f1, nv_f1), \
        (gr_f2, d_f2, nm_f2, nv_f2) = upd
    return (
        loss, grad_x[None],
        up(g_ada), g_b, g_n1, up(gr_win), g_qn, g_kn, pool4(gr_wp), g_ps, up(gr_wa), up(gr_wb), up(gr_wo), g_n2,
        up(gr_f1), up(gr_f2),
        up(d_ada), d_b, d_n1, up(d_win), d_qn, d_kn, pool4(d_wp), d_ps, up(d_wa), up(d_wb), up(d_wo), d_n2,
        up(d_f1), up(d_f2),
        up(nm_ada), nm_b, nm_n1, up(nm_win), nm_qn, nm_kn, pool4(nm_wp), nm_ps, up(nm_wa), up(nm_wb), up(nm_wo), nm_n2,
        up(nm_f1), up(nm_f2),
        up(nv_ada), nv_b, nv_n1, up(nv_win), nv_qn, nv_kn, pool4(nv_wp), nv_ps, up(nv_wa), up(nv_wb), up(nv_wo), nv_n2,
        up(nv_f1), up(nv_f2),
    )
```

```python
import functools
import math

import jax
import jax.numpy as jnp
from jax import lax
from jax.experimental import pallas as pl
from jax.experimental.pallas import tpu as pltpu

F32 = jnp.float32
BF16 = jnp.bfloat16
MESH = pl.DeviceIdType.MESH
ANY = pl.BlockSpec(memory_space=pl.ANY)

EPS = 1e-6
HEAD_DIM = 128
LANES, SUBLANES = 128, 8
POOL_WINDOWS = (2, 4, 8, 16)
N_GROUPS = len(POOL_WINDOWS)
assert POOL_WINDOWS == tuple(2 << g for g in range(N_GROUPS))
N_CHIPS = 4
N_DEV = 8
ADAM_LR, ADAM_B1, ADAM_B2, ADAM_EPS, ADAM_WD, ADAM_STEP = 0.001, 0.9, 0.999, 1e-08, 0.01, 10
VMEM_LIMIT_V7X = 56 * 1024 * 1024
ATT_T = 256
ATT_GROUP = 8
POOL_T = 256


def _pcall(body, **kw):
    return pl.pallas_call(body, **kw)


def _params(sem=None):
    return pltpu.CompilerParams(dimension_semantics=sem, vmem_limit_bytes=VMEM_LIMIT_V7X)


def _tile(n, pref):
    if n <= pref:
        return n
    t = pref
    while n % t:
        t //= 2
    return t


class _Rider:
    def __init__(self, arrays, out_shape, sems, start, finish, aliases=None, steps=()):
        self.arrays, self.out_shape, self.sems = list(arrays), list(out_shape), list(sems)
        self.start, self.finish, self.aliases, self.steps = start, finish, aliases or {}, list(steps)


def _ride(name, body, riders, arrays, *, grid, in_specs, out_specs, out_shape, scratch_shapes, sem, scalars=None):
    n_in, n_out, n_scr = len(arrays), len(out_shape), len(scratch_shapes)
    r_arrays = [a for r in riders for a in r.arrays]
    r_outs = [o for r in riders for o in r.out_shape]
    r_sems = [s for r in riders for s in r.sems]
    n_hooks = max([len(r.steps) for r in riders], default=0)
    total = math.prod(grid)
    aliases, off_i, off_o = {}, n_in + (scalars is not None), n_out
    for r in riders:
        for a, o in r.aliases.items():
            aliases[off_i + a] = off_o + o
        off_i += len(r.arrays)
        off_o += len(r.out_shape)

    def full(*refs):
        p = 0
        groups = []
        for n in (n_in, len(r_arrays), n_out, len(r_outs), n_scr, len(r_sems)):
            groups.append(refs[p:p + n])
            p += n
        ins, rin, outs, rout, scr, rsem = groups

        def each(what):
            a = o = s = 0
            for r in riders:
                fn = what(r)
                if fn is not None:
                    fn(rin[a:a + len(r.arrays)], rout[o:o + len(r.out_shape)], rsem[s:s + len(r.sems)])
                a, o, s = a + len(r.arrays), o + len(r.out_shape), s + len(r.sems)

        if riders:
            lin = 0
            for d, g in enumerate(grid):
                lin = lin * g + pl.program_id(d)
            pl.when(lin == 0)(lambda: each(lambda r: r.start))
            for t in range(n_hooks):
                pl.when(lin == min(total - 1, ((t + 1) * total) // n_hooks))(
                    lambda t=t: each(lambda r: r.steps[t] if t < len(r.steps) else None))
        body(*ins, *outs, *scr)
        if riders:
            pl.when(lin == total - 1)(lambda: each(lambda r: r.finish))

    specs = dict(grid=grid, in_specs=list(in_specs) + [ANY] * len(r_arrays),
                 out_specs=list(out_specs) + [ANY] * len(r_outs), scratch_shapes=list(scratch_shapes) + r_sems)
    common = dict(name=name, out_shape=list(out_shape) + r_outs, input_output_aliases=aliases,
                  compiler_params=_params(("arbitrary",) * len(grid) if riders else sem))
    if scalars is None:
        res = _pcall(full, **specs, **common)(*arrays, *r_arrays)
    else:
        res = _pcall(lambda _, *refs: full(*refs), **common,
                     grid_spec=pltpu.PrefetchScalarGridSpec(num_scalar_prefetch=1, **specs))(scalars, *arrays, *r_arrays)
    if not riders:
        return res
    main, rest, per = res[:n_out], res[n_out:], []
    for r in riders:
        per.append(rest[:len(r.out_shape)])
        rest = rest[len(r.out_shape):]
    return main, per


def _run_rider(name, rider):
    def body(*refs):
        n_a, n_o = len(rider.arrays), len(rider.out_shape)
        ins, outs, sems = refs[:n_a], refs[n_a:n_a + n_o], refs[n_a + n_o:]
        for fn in [rider.start] + rider.steps + [rider.finish]:
            fn(ins, outs, sems)

    return _pcall(body, name=name, out_shape=rider.out_shape, in_specs=[ANY] * len(rider.arrays),
                  out_specs=[ANY] * len(rider.out_shape), scratch_shapes=rider.sems,
                  input_output_aliases=rider.aliases)(*rider.arrays)


def _mm(name, pairs, *, M, N, K, ta=False, tb=False, tm=512, tn=1024, tk=1024,
        a_pro=None, b_pro=None, extras=(), outs, epi, riders=()):
    tm, tn, tk = _tile(M, tm), _tile(N, tn), _tile(K, tk)
    n_i, n_j, n_k = M // tm, N // tn, K // tk
    n_p, n_e = len(pairs), len(extras)
    arrays, in_specs = [], []
    for a, _ in pairs:
        arrays.append(a)
        in_specs.append(pl.BlockSpec((tk, tm), lambda i, j, k: (k, i)) if ta
                        else pl.BlockSpec((tm, tk), lambda i, j, k: (i, k)))
    for _, b in pairs:
        arrays.append(b)
        in_specs.append(pl.BlockSpec((tn, tk), lambda i, j, k: (j, k)) if tb
                        else pl.BlockSpec((tk, tn), lambda i, j, k: (k, j)))
    for arr, kind, off in extras:
        ob = off // tn
        assert off % tn == 0
        arrays.append(arr)
        if kind == "tile":
            in_specs.append(pl.BlockSpec((tm, tn), lambda i, j, k, ob=ob: (i, j + ob)))
        else:
            in_specs.append(pl.BlockSpec((1, tn), lambda i, j, k, ob=ob: (0, j + ob)))
    out_shape, out_specs = [], []
    for o in outs:
        if o["kind"] == "tile":
            out_shape.append(jax.ShapeDtypeStruct((M, N), o["dtype"]))
            out_specs.append(pl.BlockSpec((tm, tn), lambda i, j, k: (i, j)))
        else:
            out_shape.append(jax.ShapeDtypeStruct((n_i, 1, N), F32))
            out_specs.append(pl.BlockSpec((1, 1, tn), lambda i, j, k: (i, 0, j)))
    dims = (((0 if ta else 1,), (1 if tb else 0,)), ((), ()))

    def body(*refs):
        a_refs, b_refs = refs[:n_p], refs[n_p:2 * n_p]
        e_refs = refs[2 * n_p:2 * n_p + n_e]
        o_refs = refs[2 * n_p + n_e:2 * n_p + n_e + len(outs)]
        acc_refs = refs[2 * n_p + n_e + len(outs):]

        def product(p):
            a, b = a_refs[p][...], b_refs[p][...]
            if a_pro is not None:
                a = a_pro(a)
            if b_pro is not None:
                b = b_pro(b)
            return lax.dot_general(a, b, dims, preferred_element_type=F32)

        def write(accs):
            vals = epi(accs, [e[...] for e in e_refs])
            for o, o_ref, val in zip(outs, o_refs, vals):
                if o["kind"] == "tile":
                    o_ref[...] = val.astype(o_ref.dtype)
                else:
                    o_ref[0] = val

        if n_k == 1:
            write([product(p) for p in range(n_p)])
            return
        k = pl.program_id(2)

        @pl.when(k == 0)
        def _():
            for acc in acc_refs:
                acc[...] = jnp.zeros_like(acc)

        for p in range(n_p):
            acc_refs[p][...] += product(p)

        pl.when(k == n_k - 1)(lambda: write([acc[...] for acc in acc_refs]))

    return _ride(name, body, riders, arrays, grid=(n_i, n_j, n_k), in_specs=in_specs, out_specs=out_specs,
                 out_shape=out_shape, scratch_shapes=[pltpu.VMEM((tm, tn), F32) for _ in pairs] if n_k > 1 else [],
                 sem=("parallel", "parallel", "arbitrary"))


def _tile_out(dtype):
    return {"kind": "tile", "dtype": dtype}


_COLSUM = {"kind": "colsum"}


def _colsum(v):
    return jnp.sum(v, axis=0, keepdims=True)


def _norm_mod(name, x, norm_w, scale, shift):
    S, D = x.shape
    tr = _tile(S, 256)

    def body(x_ref, nw_ref, sc_ref, sh_ref, h_ref):
        xv = x_ref[...]
        r = lax.rsqrt(jnp.mean(xv * xv, axis=-1, keepdims=True) + EPS)
        h_ref[...] = ((xv * r * nw_ref[...]) * (1.0 + sc_ref[...]) + sh_ref[...]).astype(BF16)

    row = pl.BlockSpec((1, D), lambda i: (0, 0))
    til = pl.BlockSpec((tr, D), lambda i: (i, 0))
    return _pcall(body, name=name, grid=(S // tr,), in_specs=[til, row, row, row], out_specs=til,
                  out_shape=jax.ShapeDtypeStruct((S, D), BF16), compiler_params=_params(("parallel",)))(
                      x, norm_w, scale, shift)


def _norm_mod_bwd(name, dh, x, dres, norm_w, scale, gate_o=None):
    S, D = x.shape
    tr = _tile(S, 256)
    n_r = S // tr
    with_gate = gate_o is not None

    def body(*refs):
        if with_gate:
            dh_ref, x_ref, dres_ref, nw_ref, sc_ref, o_ref, g_ref, dx_ref, p1, p2, p3, do_ref, p4 = refs
        else:
            dh_ref, x_ref, dres_ref, nw_ref, sc_ref, dx_ref, p1, p2, p3 = refs
        dhv, xv, nw = dh_ref[...], x_ref[...], nw_ref[...]
        r = lax.rsqrt(jnp.mean(xv * xv, axis=-1, keepdims=True) + EPS)
        xh = xv * r
        p1[0] = _colsum(dhv)
        p2[0] = _colsum(dhv * (xh * nw))
        dn = dhv * (1.0 + sc_ref[...])
        p3[0] = _colsum(dn * xh)
        dxh = dn * nw
        dx = dres_ref[...] + r * (dxh - xh * jnp.mean(dxh * xh, axis=-1, keepdims=True))
        dx_ref[...] = dx
        if with_gate:
            do_ref[...] = (dx * g_ref[...]).astype(BF16)
            p4[0] = _colsum(dx * o_ref[...].astype(F32))

    row = pl.BlockSpec((1, D), lambda i: (0, 0))
    til = pl.BlockSpec((tr, D), lambda i: (i, 0))
    part = pl.BlockSpec((1, 1, D), lambda i: (i, 0, 0))
    part_shape = jax.ShapeDtypeStruct((n_r, 1, D), F32)
    in_specs = [til, til, til, row, row]
    arrays = [dh, x, dres, norm_w, scale]
    out_specs = [til, part, part, part]
    out_shape = [jax.ShapeDtypeStruct((S, D), F32), part_shape, part_shape, part_shape]
    if with_gate:
        in_specs += [til, row]
        arrays += list(gate_o)
        out_specs += [til, part]
        out_shape += [jax.ShapeDtypeStruct((S, D), BF16), part_shape]
    return _pcall(body, name=name, grid=(n_r,), in_specs=in_specs, out_specs=out_specs, out_shape=out_shape,
                  compiler_params=_params(("parallel",)))(*arrays)


def _pool_w_specs(rows, cg):
    return [pl.BlockSpec((rows, cg), lambda g, j=j: (N_GROUPS * j + g, 0)) for j in range(N_CHIPS)]


def _pool_fwd(proj, wp_full, pool_scale, S, PW):
    cg = PW // N_GROUPS
    rows = cg // N_CHIPS
    T = _tile(S, POOL_T)
    n_t = S // T

    def body(u_ref, w0, w1, w2, w3, ps_ref, pooled_ref, pa_ref):
        g = pl.program_id(0)
        win = jnp.left_shift(2, g)
        w = jnp.concatenate([w0[...], w1[...], w2[...], w3[...]], axis=0)
        t_i = lax.broadcasted_iota(jnp.int32, (T, T), 0)
        j_i = lax.broadcasted_iota(jnp.int32, (T, T), 1)
        b_cur = ((j_i <= t_i) & (j_i > t_i - win)).astype(BF16)
        b_prev = (j_i - T > t_i - win).astype(BF16)
        row = lax.broadcasted_iota(jnp.int32, (T, 1), 0)
        for r in range(n_t):
            cur = u_ref[r * T:(r + 1) * T, :]
            ws = jnp.dot(b_cur, cur, preferred_element_type=F32)
            if r > 0:
                ws += jnp.dot(b_prev, u_ref[(r - 1) * T:r * T, :], preferred_element_type=F32)
            count = jnp.minimum(row + (r * T + 1), win).astype(F32)
            pooled = (ws / count - cur.astype(F32)).astype(BF16)
            pooled_ref[r * T:(r + 1) * T, :] = pooled
            mixed = jnp.dot(pooled, w, preferred_element_type=F32)
            pa_ref[r * T:(r + 1) * T, :] = (mixed * ps_ref[...]).astype(BF16)

    col = pl.BlockSpec((S, cg), lambda g: (0, g))
    return _pcall(
        body, name="pool_fwd", grid=(N_GROUPS,),
        in_specs=[col] + _pool_w_specs(rows, cg) + [pl.BlockSpec((1, cg), lambda g: (0, g))],
        out_specs=[col, col],
        out_shape=[jax.ShapeDtypeStruct((S, PW), BF16), jax.ShapeDtypeStruct((S, PW), BF16)],
        compiler_params=_params(("parallel",)),
    )(proj, wp_full, wp_full, wp_full, wp_full, pool_scale)


def _pool_bwd(dpa, pooled, wp_full, pool_scale, S, PW):
    cg = PW // N_GROUPS
    rows = cg // N_CHIPS
    T = _tile(S, POOL_T)
    n_t = S // T

    def body(dpa_ref, pooled_ref, w0, w1, w2, w3, ps_ref, du_ref, gw_ref, gs_ref, dp_s, dpc_s, dmx_s):
        g = pl.program_id(0)
        win = jnp.left_shift(2, g)
        w = jnp.concatenate([w0[...], w1[...], w2[...], w3[...]], axis=0)
        row = lax.broadcasted_iota(jnp.int32, (T, 1), 0)
        gs = jnp.zeros((1, cg), F32)
        for r in range(n_t):
            sl = slice(r * T, (r + 1) * T)
            mixed = jnp.dot(pooled_ref[sl, :], w, preferred_element_type=F32)
            dpa_t = dpa_ref[sl, :]
            gs += _colsum(dpa_t * mixed)
            dmx = (dpa_t * ps_ref[...]).astype(BF16)
            dmx_s[sl, :] = dmx
            dpo = lax.dot_general(dmx, w, (((1,), (1,)), ((), ())), preferred_element_type=F32)
            dp_s[sl, :] = dpo
            count = jnp.minimum(row + (r * T + 1), win).astype(F32)
            dpc_s[sl, :] = (dpo / count).astype(BF16)
        gs_ref[...] = gs
        gw = lax.dot_general(pooled_ref[...], dmx_s[...], (((0,), (0,)), ((), ())), preferred_element_type=F32)
        for j in range(N_CHIPS):
            gw_ref[j, 0] = gw[j * rows:(j + 1) * rows, :].astype(BF16)
        j_i = lax.broadcasted_iota(jnp.int32, (T, T), 0)
        t_i = lax.broadcasted_iota(jnp.int32, (T, T), 1)
        b_cur = ((t_i >= j_i) & (t_i < j_i + win)).astype(BF16)
        b_next = (t_i + T < j_i + win).astype(BF16)
        for r in range(n_t):
            sl = slice(r * T, (r + 1) * T)
            acc = jnp.dot(b_cur, dpc_s[sl, :], preferred_element_type=F32)
            if r + 1 < n_t:
                acc += jnp.dot(b_next, dpc_s[(r + 1) * T:(r + 2) * T, :], preferred_element_type=F32)
            du_ref[sl, :] = (acc - dp_s[sl, :]).astype(BF16)

    col = pl.BlockSpec((S, cg), lambda g: (0, g))
    return _pcall(
        body, name="pool_bwd", grid=(N_GROUPS,),
        in_specs=[col, col] + _pool_w_specs(rows, cg) + [pl.BlockSpec((1, cg), lambda g: (0, g))],
        out_specs=[col, pl.BlockSpec((N_CHIPS, 1, rows, cg), lambda g: (0, g, 0, 0)),
                   pl.BlockSpec((1, cg), lambda g: (0, g))],
        out_shape=[jax.ShapeDtypeStruct((S, PW), BF16),
                   jax.ShapeDtypeStruct((N_CHIPS, N_GROUPS, rows, cg), BF16),
                   jax.ShapeDtypeStruct((1, PW), F32)],
        scratch_shapes=[pltpu.VMEM((S, cg), F32), pltpu.VMEM((S, cg), BF16), pltpu.VMEM((S, cg), BF16)],
        compiler_params=_params(("parallel",)),
    )(dpa, pooled, wp_full, wp_full, wp_full, wp_full, pool_scale)


_NT = (((1,), (1,)), ((), ()))
_TN = (((0,), (0,)), ((), ()))


def _split_dot(v, tri):
    hi = v.astype(BF16)
    lo = (v - hi.astype(F32)).astype(BF16)
    return jnp.dot(hi, tri, preferred_element_type=F32) + jnp.dot(lo, tri, preferred_element_type=F32)


LOG2E = 1.4426950408889634
QK_SCALE = 1.0 / math.sqrt(HEAD_DIM)


def _sb_scores(q2_i, k_j, tri_l, masked):
    tq, tk = q2_i.shape[0], k_j.shape[0]
    s = lax.dot_general(q2_i, k_j, _NT, preferred_element_type=F32)
    lp = jnp.log(1.0 + jnp.exp2(-jnp.abs(s))) * LOG2E
    lb = jnp.minimum(s, 0.0) - lp
    l = lb - s
    mask = None
    if masked:
        mask = lax.broadcasted_iota(jnp.int32, (tq, tk), 0) > lax.broadcasted_iota(jnp.int32, (tq, tk), 1)
        l = jnp.where(mask, l, 0.0)
    return l, lb, lb + _split_dot(l, tri_l), mask


def _sb_weights(t, carry_l, mask):
    a = jnp.exp2(t + carry_l)
    return a if mask is None else jnp.where(mask, a, 0.0)


def _rowsum(v):
    return jnp.sum(v, axis=1, keepdims=True)


def _qk_norm(x_ref, w_ref):
    xv = x_ref[...].astype(F32)
    r = lax.rsqrt(jnp.mean(xv * xv, axis=-1, keepdims=True) + EPS)
    return xv * r, r


def _attn_fwd(proj, q_norm_w, k_norm_w, S, H, q_off, riders=()):
    t = _tile(S, ATT_T)
    n_q = S // t

    def body(q_ref, k_ref, v_ref, qw_ref, kw_ref, att_ref, attf_ref, qn_s, kn_s):
        qh, _ = _qk_norm(q_ref, qw_ref)
        qn_s[...] = (qh * qw_ref[...] * (QK_SCALE * LOG2E)).astype(BF16)
        kh, _ = _qk_norm(k_ref, kw_ref)
        kn_s[...] = (kh * kw_ref[...]).astype(BF16)
        tri_l = (lax.broadcasted_iota(jnp.int32, (t, t), 0) > lax.broadcasted_iota(jnp.int32, (t, t), 1)).astype(BF16)

        def rows(j):
            return pl.ds(pl.multiple_of(j * t, t), t)

        def q_step(i, _):
            q_i = qn_s[rows(i), :]

            def av(a, j):
                return jnp.dot(a.astype(BF16), v_ref[rows(j), :], preferred_element_type=F32)

            l, _, tt, mask = _sb_scores(q_i, kn_s[rows(i), :], tri_l, True)
            acc = av(_sb_weights(tt, 0.0, mask), i)
            carry = _rowsum(l)

            def single(_, c):
                carry, acc = c
                l, _, tt, _ = _sb_scores(q_i, kn_s[rows(i - 1), :], tri_l, False)
                return carry + _rowsum(l), acc + av(_sb_weights(tt, carry, None), i - 1)

            carry, acc = lax.fori_loop(0, i % 2, single, (carry, acc))
            top = i - 1 - i % 2

            def pair(p, c):
                carry, acc = c
                j0 = top - 2 * p
                l0, _, t0, _ = _sb_scores(q_i, kn_s[rows(j0), :], tri_l, False)
                l1, _, t1, _ = _sb_scores(q_i, kn_s[rows(j0 - 1), :], tri_l, False)
                mid = carry + _rowsum(l0)
                acc = acc + av(_sb_weights(t0, carry, None), j0) + av(_sb_weights(t1, mid, None), j0 - 1)
                return mid + _rowsum(l1), acc

            _, acc = lax.fori_loop(0, i // 2, pair, (carry, acc))
            att_ref[rows(i), :] = acc.astype(BF16)
            attf_ref[rows(i), :] = acc
            return 0

        lax.fori_loop(0, n_q, q_step, 0)

    def col(off):
        return pl.BlockSpec((S, HEAD_DIM), lambda h, off=off: (0, off + h))

    wspec = pl.BlockSpec((1, HEAD_DIM), lambda h: (0, 0))
    return _ride(
        "attn_fwd", body, riders, [proj, proj, proj, q_norm_w, k_norm_w], grid=(H,),
        in_specs=[col(q_off), col(q_off + H), col(q_off + 2 * H), wspec, wspec],
        out_specs=[col(0), col(0)],
        out_shape=[jax.ShapeDtypeStruct((S, H * HEAD_DIM), BF16), jax.ShapeDtypeStruct((S, H * HEAD_DIM), F32)],
        scratch_shapes=[pltpu.VMEM((S, HEAD_DIM), BF16), pltpu.VMEM((S, HEAD_DIM), BF16)],
        sem=("parallel",))


def _attn_bwd(proj, datt, attf, q_norm_w, k_norm_w, S, H, q_off, riders=()):
    t = _tile(S, ATT_T)
    n_q = S // t

    def body(q_ref, k_ref, v_ref, do_ref, o_ref, qw_ref, kw_ref, dq_ref, dk_ref, dv_ref, gq_ref, gk_ref,
             qn_s, kn_s, qz_s, kz_s, dk_s, dv_s, gq_s):
        qw, kw = qw_ref[...], kw_ref[...]
        qh, _ = _qk_norm(q_ref, qw_ref)
        qn_s[...] = (qh * qw * (QK_SCALE * LOG2E)).astype(BF16)
        qz_s[...] = (qh * qw * QK_SCALE).astype(BF16)
        kh, _ = _qk_norm(k_ref, kw_ref)
        kn_s[...] = (kh * kw).astype(BF16)
        kz_s[...] = (kh * kw * QK_SCALE).astype(BF16)
        dk_s[...] = jnp.zeros_like(dk_s)
        dv_s[...] = jnp.zeros_like(dv_s)
        gq_s[...] = jnp.zeros_like(gq_s)
        r_i = lax.broadcasted_iota(jnp.int32, (t, t), 0)
        c_i = lax.broadcasted_iota(jnp.int32, (t, t), 1)
        tri_l = (r_i > c_i).astype(BF16)
        tri_e = (r_i >= c_i).astype(BF16)

        def rows(j):
            return pl.ds(pl.multiple_of(j * t, t), t)

        def q_step(i, _):
            q_i = qn_s[rows(i), :]
            do_i = do_ref[rows(i), :]
            d_i = _rowsum(do_i.astype(F32) * o_ref[rows(i), :])

            def scores(j, masked):
                l, lb, tt, mask = _sb_scores(q_i, kn_s[rows(j), :], tri_l, masked)
                da = lax.dot_general(do_i, v_ref[rows(j), :], _NT, preferred_element_type=F32)
                return l, lb, tt, mask, da

            def grads(j, sc, carry_l, carry_e, dq_acc):
                l, lb, tt, mask, da = sc
                a_bf = _sb_weights(tt, carry_l, mask).astype(BF16)
                e = da * a_bf.astype(F32)
                p = (d_i - carry_e) - _split_dot(e, tri_e)
                dz = e - jnp.exp2(lb) * (e + p)
                if mask is not None:
                    dz = jnp.where(mask, dz, 0.0)
                dz = dz.astype(BF16)
                dk_s[rows(j), :] += lax.dot_general(dz, qz_s[rows(i), :], _TN, preferred_element_type=F32)
                dv_s[rows(j), :] += lax.dot_general(a_bf, do_i, _TN, preferred_element_type=F32)
                return (carry_l + _rowsum(l), carry_e + _rowsum(e),
                        dq_acc + jnp.dot(dz, kz_s[rows(j), :], preferred_element_type=F32))

            zero = jnp.zeros((t, 1), F32)
            first = (zero, zero, jnp.zeros((t, HEAD_DIM), F32))

            def group(js, diagonal_first, c):
                scs = [scores(j, diagonal_first and n == 0) for n, j in enumerate(js)]
                for j, sc in zip(js, scs):
                    c = grads(j, sc, *c)
                return c

            n_first = i % ATT_GROUP
            c = lax.switch(n_first, [functools.partial(group, [i - u for u in range(n + 1)], True, first)
                                     for n in range(ATT_GROUP)])
            top = i - 1 - n_first

            def whole(p, c):
                j0 = top - ATT_GROUP * p
                return group([j0 - u for u in range(ATT_GROUP)], False, c)

            _, _, dqn = lax.fori_loop(0, (i - n_first) // ATT_GROUP, whole, c)
            qv = q_ref[rows(i), :].astype(F32)
            r = lax.rsqrt(jnp.mean(qv * qv, axis=-1, keepdims=True) + EPS)
            xh = qv * r
            gq_s[...] += _colsum(dqn * xh)
            dxh = dqn * qw
            dq_ref[rows(i), :] = (r * (dxh - xh * jnp.mean(dxh * xh, axis=-1, keepdims=True))).astype(BF16)
            return 0

        lax.fori_loop(0, n_q, q_step, 0)
        gq_ref[0] = gq_s[...]
        kh, rk = _qk_norm(k_ref, kw_ref)
        dkn = dk_s[...]
        gk_ref[0] = _colsum(dkn * kh)
        dxh = dkn * kw
        dk_ref[...] = (rk * (dxh - kh * jnp.mean(dxh * kh, axis=-1, keepdims=True))).astype(BF16)
        dv_ref[...] = dv_s[...].astype(BF16)

    def col(off):
        return pl.BlockSpec((S, HEAD_DIM), lambda h, off=off: (0, off + h))

    wspec = pl.BlockSpec((1, HEAD_DIM), lambda h: (0, 0))
    gspec = pl.BlockSpec((1, 1, HEAD_DIM), lambda h: (h, 0, 0))
    act = jax.ShapeDtypeStruct((S, H * HEAD_DIM), BF16)
    gsh = jax.ShapeDtypeStruct((H, 1, HEAD_DIM), F32)
    return _ride(
        "attn_bwd", body, riders, [proj, proj, proj, datt, attf, q_norm_w, k_norm_w], grid=(H,),
        in_specs=[col(q_off), col(q_off + H), col(q_off + 2 * H), col(0), col(0), wspec, wspec],
        out_specs=[col(0), col(0), col(0), gspec, gspec],
        out_shape=[act, act, act, gsh, gsh],
        scratch_shapes=[pltpu.VMEM((S, HEAD_DIM), BF16)] * 4 + [pltpu.VMEM((S, HEAD_DIM), F32)] * 2
        + [pltpu.VMEM((1, HEAD_DIM), F32)],
        sem=("parallel",))


def _place():
    x, y, c = lax.axis_index("x"), lax.axis_index("y"), lax.axis_index("c")
    chips = [(1 - x, y), (x, 1 - y), (1 - x, 1 - y)]
    return x, y, c, chips


def _dev_allgather(name, v):
    m_per, n = v.shape

    def body(x_ref, out_ref, send_sems, recv_sems, local_sem):
        x, y, c, _ = _place()
        me = (x, y, c)

        def rows(px, py, pc):
            return out_ref.at[pl.ds((4 * px + 2 * py + pc) * m_per, m_per), :]

        def peer(r):
            return tuple(1 - b if (r >> s) & 1 else b for b, s in zip(me, (2, 1, 0)))

        def copy(r, block, to, src=None):
            return pltpu.make_async_remote_copy(
                src_ref=rows(*block) if src is None else src, dst_ref=rows(*block),
                send_sem=send_sems.at[r - 1], recv_sem=recv_sems.at[r - 1], device_id=to, device_id_type=MESH)

        mine = pltpu.make_async_copy(x_ref, rows(*me), local_sem)
        mine.start()
        sends = [copy(r, me, peer(r), src=x_ref) for r in range(1, N_DEV)]
        for cp in sends:
            cp.start()
        for r in range(1, N_DEV):
            copy(r, peer(r), me).wait_recv()
        for cp in sends:
            cp.wait_send()
        mine.wait()

    return _pcall(
        body, name=name, out_shape=jax.ShapeDtypeStruct((N_DEV * m_per, n), v.dtype),
        in_specs=[pl.BlockSpec(memory_space=pltpu.VMEM)], out_specs=pl.BlockSpec(memory_space=pltpu.VMEM),
        scratch_shapes=[pltpu.SemaphoreType.DMA((7,)), pltpu.SemaphoreType.DMA((7,)), pltpu.SemaphoreType.DMA],
        compiler_params=pltpu.CompilerParams(vmem_limit_bytes=VMEM_LIMIT_V7X),
    )(v)


class _W:
    def __init__(self, name, kind, R, C):
        self.name, self.kind, self.R, self.C = name, kind, R, C

    @property
    def shard_shape(self):
        return (self.R, self.C // N_CHIPS) if self.kind == "col" else (self.R // N_CHIPS, self.C)

    @property
    def half_rows(self):
        return self.shard_shape[0] // 2

    def shard_half(self, ref, half):
        return ref.at[pl.ds(half * self.half_rows, self.half_rows), :]

    def region(self, full_ref, chip, half):
        hr = self.half_rows
        if self.kind == "col":
            cw = self.C // N_CHIPS
            return full_ref.at[pl.ds(half * hr, hr), pl.ds(chip * cw, cw)]
        return full_ref.at[pl.ds(chip * (2 * hr) + half * hr, hr), :]


def _ag_rider(ws, fulls, n_ch=4, chunks=None):
    n_w = len(ws)
    lo, hi = chunks or (0, n_ch)
    per = 6

    def parts(full, sems):
        send_sems, recv_sems = sems
        x, y, c, _ = _place()
        xn, yn, dg = (1 - x, y), (x, 1 - y), (1 - x, 1 - y)
        via = (x + (1 - c) * (1 - 2 * x), y + c * (1 - 2 * y))
        to = (x + c * (1 - 2 * x), y + (1 - c) * (1 - 2 * y))

        def reg(i, chip, half, t):
            nr = ws[i].half_rows // n_ch
            return ws[i].region(full[i], 2 * chip[0] + chip[1], half).at[pl.ds(t * nr, nr), :]

        def copy(r, i, t, k, dev):
            s = (i * (hi - lo) + t - lo) * per + k
            return pltpu.make_async_remote_copy(src_ref=r, dst_ref=r, send_sem=send_sems.at[s],
                                                recv_sem=recv_sems.at[s], device_id=dev, device_id_type=MESH)

        def direct(i, t, k):
            return copy(reg(i, (x, y), c, t), i, t, k, (*(via, to)[k], c))

        def direct_in(i, t, k):
            return copy(reg(i, (via, to)[k], c, t), i, t, k, (*(via, to)[k], c))

        def relay(i, t):
            return copy(reg(i, via, c, t), i, t, 2, (*to, c))

        def relay_in(i, t):
            return copy(reg(i, dg, c, t), i, t, 2, (*to, c))

        def hand(i, t, k, half):
            return copy(reg(i, (xn, yn, dg)[k], half, t), i, t, 3 + k, (x, y, 1 - c))

        return c, direct, direct_in, relay, relay_in, hand

    def start(_, full, sems):
        _, direct, _, _, _, _ = parts(full, sems)
        for t in range(lo, hi):
            for i in range(n_w):
                direct(i, t, 0).start()
                direct(i, t, 1).start()

    def arrived(t):
        def step(_, full, sems):
            c, _, direct_in, relay, relay_in, hand = parts(full, sems)
            for i in range(n_w):
                direct_in(i, t, 0).wait_recv()
                direct_in(i, t, 1).wait_recv()
                relay(i, t).start()
                hand(i, t, 0, c).start()
                hand(i, t, 1, c).start()
        return step

    def finish(_, full, sems):
        c, direct, _, relay, relay_in, hand = parts(full, sems)
        for t in range(lo, hi):
            for i in range(n_w):
                relay_in(i, t).wait_recv()
                hand(i, t, 2, c).start()
        for i in range(n_w):
            for t in range(lo, hi):
                for k in range(3):
                    hand(i, t, k, 1 - c).wait_recv()
        for i in range(n_w):
            for t in range(lo, hi):
                direct(i, t, 0).wait_send()
                direct(i, t, 1).wait_send()
                relay(i, t).wait_send()
                for k in range(3):
                    hand(i, t, k, c).wait_send()

    n_sem = per * (hi - lo) * n_w
    return _Rider(fulls, [jax.ShapeDtypeStruct((w.R, w.C), BF16) for w in ws],
                  [pltpu.SemaphoreType.DMA((n_sem,)), pltpu.SemaphoreType.DMA((n_sem,))], start, finish,
                  steps=[arrived(t) for t in range(lo, hi)], aliases={i: i for i in range(n_w)})


def _cast_into_full(ws, shards, chip_arr, riders=()):
    sr, sc = ws[0].shard_shape
    assert all(w.shard_shape == (sr, sc) for w in ws)
    tr, tc = _tile(sr, 512), _tile(sc, 2048)
    n_r, n_c = sr // tr, sc // tc

    def place(w):
        if w.kind == "col":
            return pl.BlockSpec((tr, tc), lambda i, j, chip: (i, chip[0] * n_c + j))
        return pl.BlockSpec((tr, tc), lambda i, j, chip: (chip[0] * n_r + i, j))

    def body(*refs):
        for a_ref, o_ref in zip(refs[:len(ws)], refs[len(ws):]):
            o_ref[...] = a_ref[...].astype(BF16)

    return _ride("cast_" + "_".join(w.name for w in ws), body, riders, list(shards), grid=(n_r, n_c),
                 in_specs=[pl.BlockSpec((tr, tc), lambda i, j, chip: (i, j))] * len(ws),
                 out_specs=[place(w) for w in ws], out_shape=[jax.ShapeDtypeStruct((w.R, w.C), BF16) for w in ws],
                 scratch_shapes=[], sem=("parallel", "parallel"), scalars=chip_arr)


def _half_view(w, g):
    return g if w.kind == "col" else g.reshape(N_CHIPS, w.R // N_CHIPS, w.C)


def _px_rider(ws, grads):
    n_w = len(ws)

    def copies(g, got, sems):
        send_sems, recv_sems = sems
        x, y, c, _ = _place()

        def half_all(w, ref, half):
            hr = w.half_rows
            if w.kind == "col":
                return ref.at[pl.ds(half * hr, hr), :]
            return ref.at[:, pl.ds(half * hr, hr), :]

        return [pltpu.make_async_remote_copy(
            src_ref=half_all(w, g[i], 1 - c), dst_ref=got[i], send_sem=send_sems.at[i], recv_sem=recv_sems.at[i],
            device_id=(x, y, 1 - c), device_id_type=MESH) for i, w in enumerate(ws)]

    def start(g, got, sems):
        for cp in copies(g, got, sems):
            cp.start()

    def finish(g, got, sems):
        for cp in copies(g, got, sems):
            cp.wait_recv()
            cp.wait_send()

    def got_shape(w):
        hr = w.half_rows
        return (hr, w.C) if w.kind == "col" else (N_CHIPS, hr, w.C)

    return _Rider([_half_view(w, g) for w, g in zip(ws, grads)],
                  [jax.ShapeDtypeStruct(got_shape(w), BF16) for w in ws],
                  [pltpu.SemaphoreType.DMA((n_w,)), pltpu.SemaphoreType.DMA((n_w,))], start, finish)


def _pair_sum(w, g, got, c_arr):
    hr = w.half_rows
    if w.kind == "col":
        tr, tc = _tile(hr, 512), _tile(w.C, 2048)
        n_r = hr // tr
        grid = (n_r, w.C // tc)
        g_spec = pl.BlockSpec((tr, tc), lambda i, j, c: (c[0] * n_r + i, j))
        o_spec = pl.BlockSpec((tr, tc), lambda i, j, c: (i, j))
    else:
        tr = _tile(hr, 512)
        n_r = hr // tr
        grid = (N_CHIPS, n_r)
        g_spec = pl.BlockSpec((1, tr, w.C), lambda s, i, c: (s, c[0] * n_r + i, 0))
        o_spec = pl.BlockSpec((1, tr, w.C), lambda s, i, c: (s, i, 0))

    def body(c_ref, g_ref, got_ref, out_ref):
        out_ref[...] = (g_ref[...].astype(F32) + got_ref[...].astype(F32)).astype(BF16)

    return _pcall(
        body, name="grad_pair_sum_" + w.name, out_shape=jax.ShapeDtypeStruct(got.shape, BF16),
        grid_spec=pltpu.PrefetchScalarGridSpec(num_scalar_prefetch=1, grid=grid, in_specs=[g_spec, o_spec],
                                               out_specs=o_spec),
        compiler_params=_params(("parallel", "parallel")),
    )(c_arr, _half_view(w, g), got)


def _chip_sum(w, p, q, cc_arr):
    hr, cols = w.half_rows, w.shard_shape[1]
    tr, tc = _tile(hr, 512), _tile(cols, 2048)
    n_r, n_c = hr // tr, cols // tc

    def body(cc_ref, own, q1, q2, q3, out_ref):
        own_v = own[...] if w.kind == "col" else own[0]
        out_ref[...] = ((own_v.astype(F32) + q1[0].astype(F32)) + q2[0].astype(F32)) + q3[0].astype(F32)

    if w.kind == "col":
        own_spec = pl.BlockSpec((tr, tc), lambda i, j, cc: (i, cc[1] * n_c + j))
    else:
        own_spec = pl.BlockSpec((1, tr, tc), lambda i, j, cc: (cc[1], i, j))
    q_specs = [pl.BlockSpec((1, tr, tc), lambda i, j, cc, s=s: ((cc[1] + s) % N_CHIPS, i, j)) for s in (1, 2, 3)]
    return _pcall(
        body, name="grad_chip_sum_" + w.name, out_shape=jax.ShapeDtypeStruct(w.shard_shape, F32),
        grid_spec=pltpu.PrefetchScalarGridSpec(
            num_scalar_prefetch=1, grid=(n_r, n_c), in_specs=[own_spec] + q_specs,
            out_specs=pl.BlockSpec((tr, tc), lambda i, j, cc: (cc[0] * n_r + i, j))),
        compiler_params=_params(("parallel", "parallel")),
    )(cc_arr, p, q, q, q)


_SEM = pl.BlockSpec(memory_space=pltpu.SEMAPHORE)
_HBM = pl.BlockSpec(memory_space=pltpu.HBM)


def _split_copies(kind, ws, p, land, send_sems, recv_sems):
    x, y, c, chips = _place()
    my_chip = 2 * x + y
    pairs = []
    for i, w in enumerate(ws):
        if kind == "pair":
            hr = w.half_rows
            src = p[i].at[pl.ds((1 - c) * hr, hr), :] if w.kind == "col" else p[i].at[:, pl.ds((1 - c) * hr, hr), :]
            cp = pltpu.make_async_remote_copy(src_ref=src, dst_ref=land[i], send_sem=send_sems.at[i],
                                              recv_sem=recv_sems.at[i], device_id=(x, y, 1 - c), device_id_type=MESH)
            pairs.append((cp, cp))
            continue
        for k, chip in enumerate(chips):
            to_chip = 2 * chip[0] + chip[1]
            src = p[i].at[:, pl.ds(to_chip * (w.C // N_CHIPS), w.C // N_CHIPS)] if w.kind == "col" else p[i].at[to_chip]
            kw = dict(send_sem=send_sems.at[3 * i + k], recv_sem=recv_sems.at[3 * i + k], device_id=(*chip, c),
                      device_id_type=MESH)
            pairs.append((pltpu.make_async_remote_copy(src_ref=src, dst_ref=land[i].at[my_chip], **kw),
                          pltpu.make_async_remote_copy(src_ref=src, dst_ref=land[i].at[to_chip], **kw)))
    return pairs


def _split_start(name, kind, ws, arrays):
    n_w = len(ws)
    if kind == "pair":
        arrays = [_half_view(w, g) for w, g in zip(ws, arrays)]
        lands = [lax.empty((w.half_rows, w.C) if w.kind == "col" else (N_CHIPS, w.half_rows, w.C), BF16) for w in ws]
    else:
        lands = [lax.empty((N_CHIPS, w.half_rows, w.shard_shape[1]), BF16) for w in ws]
    n_sem = n_w if kind == "pair" else 3 * n_w

    def body(*refs):
        p, land = refs[:n_w], refs[n_w:2 * n_w]
        for out, _ in _split_copies(kind, ws, p, land, refs[2 * n_w], refs[2 * n_w + 1]):
            out.start()
        refs[-1][...] = jnp.zeros_like(refs[-1])

    arrays = [pltpu.with_memory_space_constraint(a, pltpu.HBM) for a in list(arrays) + lands]
    res = _pcall(
        body, name=name,
        out_shape=(pltpu.SemaphoreType.DMA((n_sem,)), pltpu.SemaphoreType.DMA((n_sem,)),
                   *[pltpu.HBM(a.shape, a.dtype) for a in arrays], jax.ShapeDtypeStruct((SUBLANES, LANES), F32)),
        in_specs=[_HBM] * (2 * n_w),
        out_specs=(_SEM, _SEM, *[_HBM] * (2 * n_w), pl.BlockSpec(memory_space=pltpu.VMEM)),
        input_output_aliases={i: 2 + i for i in range(2 * n_w)},
        compiler_params=pltpu.CompilerParams(has_side_effects=pltpu.SideEffectType.DATAFLOW_SIDE_EFFECTING),
    )(*arrays)
    return (kind, ws, res[0], res[1], list(res[2:2 + n_w]), list(res[2 + n_w:2 + 2 * n_w])), res[-1]


def _split_wait(name, flight, after):
    kind, ws, send_sems, recv_sems, arrays, lands = flight
    n_w = len(ws)

    def body(*refs):
        p, land = refs[:n_w], refs[n_w:2 * n_w]
        for _, cp in _split_copies(kind, ws, p, land, refs[2 * n_w], refs[2 * n_w + 1]):
            cp.wait_send()
            cp.wait_recv()

    res = _pcall(
        body, name=name,
        out_shape=[pltpu.HBM(a.shape, a.dtype) for a in list(arrays) + list(lands)],
        in_specs=[_HBM] * (2 * n_w) + [_SEM, _SEM] + [ANY] * len(after), out_specs=[_HBM] * (2 * n_w),
        input_output_aliases={i: i for i in range(2 * n_w)},
        compiler_params=pltpu.CompilerParams(has_side_effects=pltpu.SideEffectType.DATAFLOW_SIDE_EFFECTING),
    )(*arrays, *lands, send_sems, recv_sems, *after)
    return list(res[:n_w]), list(res[n_w:])


def _sf_rider(ws, grads):
    n_w = len(ws)

    def copy(g, sems, i, half):
        send_sems, recv_sems = sems
        x, y, c, _ = _place()
        h = c if half == "mine" else 1 - c
        reg = ws[i].shard_half(g[i], h)
        return pltpu.make_async_remote_copy(src_ref=reg, dst_ref=reg, send_sem=send_sems.at[i], recv_sem=recv_sems.at[i],
                                            device_id=(x, y, 1 - c), device_id_type=MESH)

    def start(_, g, sems):
        for i in range(n_w):
            copy(g, sems, i, "mine").start()

    def finish(_, g, sems):
        for i in range(n_w):
            copy(g, sems, i, "other").wait_recv()
            copy(g, sems, i, "mine").wait_send()

    return _Rider(grads, [jax.ShapeDtypeStruct(w.shard_shape, F32) for w in ws],
                  [pltpu.SemaphoreType.DMA((n_w,)), pltpu.SemaphoreType.DMA((n_w,))], start, finish,
                  aliases={i: i for i in range(n_w)})


def _adamw_math(w, g, m, v):
    m = ADAM_B1 * m + (1.0 - ADAM_B1) * g
    v = ADAM_B2 * v + (1.0 - ADAM_B2) * (g * g)
    m_hat = m / (1.0 - ADAM_B1 ** ADAM_STEP)
    v_hat = v / (1.0 - ADAM_B2 ** ADAM_STEP)
    delta = -ADAM_LR * (m_hat / (jnp.sqrt(v_hat) + ADAM_EPS) + ADAM_WD * w)
    return delta, m, v


def _adamw(name, w, g, m, v, after=None):
    R, C = w.shape
    tr, tc = _tile(R, 256), _tile(C, 2048)
    behind = [] if after is None else [after]

    def body(w_ref, g_ref, m_ref, v_ref, *rest):
        g_out, d_out, m_out, v_out = rest[len(behind):]
        g = g_ref[...]
        g_out[...] = g
        d_out[...], m_out[...], v_out[...] = _adamw_math(w_ref[...], g, m_ref[...], v_ref[...])

    spec = pl.BlockSpec((tr, tc), lambda i, j: (i, j))
    sh = jax.ShapeDtypeStruct((R, C), F32)
    return _pcall(body, name=name, grid=(R // tr, C // tc), in_specs=[spec] * 4 + [ANY] * len(behind),
                  out_specs=[spec] * 4, out_shape=[sh] * 4, compiler_params=_params(("parallel", "parallel")))(
                      w, g, m, v, *behind)


def _ada_update(sct, dmod_sh, w, m, v, riders=()):
    R, C = w.shape
    tr, tc = _tile(R, 512), _tile(C, 1024)

    def body(s_ref, d_ref, w_ref, m_ref, v_ref, g_out, d_out, m_out, v_out):
        s, d = s_ref[...], d_ref[...]
        g = s[:, 0:1] * d[0:1, :]
        for b in range(1, N_DEV):
            g += s[:, b:b + 1] * d[b:b + 1, :]
        g_out[...] = g
        d_out[...], m_out[...], v_out[...] = _adamw_math(w_ref[...], g, m_ref[...], v_ref[...])

    spec = pl.BlockSpec((tr, tc), lambda i, j: (i, j))
    sh = jax.ShapeDtypeStruct((R, C), F32)
    return _ride(
        "ada_update", body, riders, [sct, dmod_sh, w, m, v], grid=(R // tr, C // tc),
        in_specs=[pl.BlockSpec((tr, N_DEV), lambda i, j: (i, 0)), pl.BlockSpec((N_DEV, tc), lambda i, j: (0, j)),
                  spec, spec, spec],
        out_specs=[spec] * 4, out_shape=[sh] * 4, scratch_shapes=[], sem=("parallel", "parallel"))


def _silu_rows(c_row):
    D = c_row.shape[1]

    def body(c_ref, o_ref):
        cv = c_ref[...]
        o_ref[...] = cv * jax.nn.sigmoid(cv)

    return _pcall(body, name="silu_c", out_shape=jax.ShapeDtypeStruct((1, D), F32))(c_row)


def _pack_partials(parts, widths, total):
    n = len(widths)

    def body(*refs):
        loss_p, out_ref = refs[n], refs[n + 1]
        off = 0
        for ref, wd in zip(refs[:n], widths):
            out_ref[:, off:off + wd] = jnp.sum(ref[...], axis=0)
            off += wd
        loss = jnp.sum(jnp.sum(loss_p[...], axis=0), axis=1, keepdims=True)
        out_ref[:, off:off + LANES] = jnp.broadcast_to(loss, (1, LANES))
        if off + LANES < total:
            out_ref[:, off + LANES:total] = jnp.zeros((1, total - off - LANES), F32)

    return _pcall(body, name="pack_partials", out_shape=jax.ShapeDtypeStruct((1, total), F32))(*parts)


def _small_update(gathered, offsets, params, loss_off):
    n_p = len(params)

    def over_devices(g_ref, off, wd):
        blk = g_ref[:, off:off + wd]
        g = blk[0:1, :]
        for b in range(1, N_DEV):
            g = g + blk[b:b + 1, :]
        return g

    def body(*refs):
        g_ref = refs[0]
        prm = refs[1:1 + 3 * n_p]
        outs = refs[1 + 3 * n_p:]
        outs[4 * n_p][...] = over_devices(g_ref, loss_off, LANES)
        for i, (off, wd) in enumerate(offsets):
            g = over_devices(g_ref, off, wd)
            w, m, v = prm[3 * i][...], prm[3 * i + 1][...], prm[3 * i + 2][...]
            outs[4 * i][...] = g
            outs[4 * i + 1][...], outs[4 * i + 2][...], outs[4 * i + 3][...] = _adamw_math(w, g, m, v)

    flat = [a for t in params for a in t]
    out_shape = [jax.ShapeDtypeStruct(t[0].shape, F32) for t in params for _ in range(4)]
    out_shape.append(jax.ShapeDtypeStruct((1, LANES), F32))
    return _pcall(body, name="small_update", out_shape=out_shape)(gathered, *flat)


def kernel(x, c, w_ada, b_ada, norm1_w, w_in, q_norm_w, k_norm_w, w_pool, pool_scale, w_a_up, w_b_up, w_o, norm2_w, w_ff1, w_ff2, loss_target, m_w_ada, m_b_ada, m_norm1_w, m_w_in, m_q_norm_w, m_k_norm_w, m_w_pool, m_pool_scale, m_w_a_up, m_w_b_up, m_w_o, m_norm2_w, m_w_ff1, m_w_ff2, v_w_ada, v_b_ada, v_norm1_w, v_w_in, v_q_norm_w, v_k_norm_w, v_w_pool, v_pool_scale, v_w_a_up, v_w_b_up, v_w_o, v_norm2_w, v_w_ff1, v_w_ff2):
    _, S, D = x.shape
    PW = D // 2
    H = PW // HEAD_DIM
    cg = PW // N_GROUPS
    IN = w_in.shape[2] * N_CHIPS
    FF = w_ff1.shape[2] * N_CHIPS
    A_COLS = w_ada.shape[2]
    xi, yi, ci = lax.axis_index("x"), lax.axis_index("y"), lax.axis_index("c")
    chip = 2 * xi + yi
    dev = 2 * chip + ci
    c_arr = jnp.reshape(ci, (1,)).astype(jnp.int32)
    x2, tgt = x[0], loss_target[0]

    ws = [_W("w_in", "col", D, IN), _W("w_pool", "row", PW, cg), _W("w_a_up", "col", PW, D),
          _W("w_b_up", "col", PW, D), _W("w_o", "row", D, D), _W("w_ff1", "col", D, FF), _W("w_ff2", "row", FF, D)]
    w32 = [w_in[0], w_pool[0].reshape(cg, cg), w_a_up[0], w_b_up[0], w_o[0], w_ff1[0], w_ff2[0]]
    m32 = [m_w_in[0], m_w_pool[0].reshape(cg, cg), m_w_a_up[0], m_w_b_up[0], m_w_o[0], m_w_ff1[0], m_w_ff2[0]]
    v32 = [v_w_in[0], v_w_pool[0].reshape(cg, cg), v_w_a_up[0], v_w_b_up[0], v_w_o[0], v_w_ff1[0], v_w_ff2[0]]

    W_IN, W_POOL, W_A, W_B, W_O, W_FF1, W_FF2 = ws
    chip_arr = jnp.reshape(chip, (1,)).astype(jnp.int32)
    cc_arr = jnp.stack([ci, chip]).astype(jnp.int32)
    s_in, s_pool, s_a, s_b, s_o = [_cast_into_full([w], [a], chip_arr)[0] for w, a in zip(ws[:5], w32[:5])]
    (s_ff1, s_ff2), ((win_f,),) = _cast_into_full([W_FF1, W_FF2], w32[5:], chip_arr, riders=[_ag_rider([W_IN], [s_in])])

    sc_row = _silu_rows(c)
    sc_all = _dev_allgather("gather_silu_c", sc_row.reshape(SUBLANES, D // SUBLANES)).reshape(N_DEV, D)
    sc16 = jnp.concatenate([sc_all, jnp.zeros_like(sc_all)], axis=0)
    b_cols = lax.dynamic_slice(b_ada, (0, chip * A_COLS), (1, A_COLS))
    (mod_cols,) = _mm("mod_cols", [(sc16, w_ada[0])], M=2 * N_DEV, N=A_COLS, K=D, tm=16, tn=1024, tk=1024,
                      a_pro=lambda a: a.astype(BF16), b_pro=lambda b: b.astype(BF16),
                      extras=[(b_cols, "row", 0)], outs=[_tile_out(F32)], epi=lambda accs, ex: [accs[0] + ex[0]])
    mod_all = _dev_allgather("gather_mod", mod_cols[:N_DEV]).reshape(N_CHIPS, 2, N_DEV, A_COLS)
    mod_row = lax.dynamic_index_in_dim(mod_all[:, 0], dev, axis=1, keepdims=False).reshape(1, N_CHIPS * A_COLS)
    shift1, scale1, gate1, shift2, scale2, gate2 = [mod_row[:, i * D:(i + 1) * D] for i in range(6)]

    WIDE = dict(tm=2048, tn=1024, tk=2048)
    DEEP = dict(tm=1024, tn=1024, tk=2048)
    DEEPER = dict(tm=1024, tn=1024, tk=4096)
    h = _norm_mod("norm1_mod", x2, norm1_w, scale1, shift1)
    (proj,), ((wpool_f, wa_f, wb_f, wo_f),) = _mm(
        "in_proj", [(h, win_f)], M=S, N=IN, K=D, outs=[_tile_out(BF16)], epi=lambda accs, ex: [accs[0]], **WIDE,
        riders=[_ag_rider([W_POOL, W_A, W_B, W_O], [s_pool, s_a, s_b, s_o], n_ch=2)])
    pooled, pa = _pool_fwd(proj, wpool_f, pool_scale, S, PW)
    (att, attf), ((wff1_f,),) = _attn_fwd(proj, q_norm_w, k_norm_w, S, H, PW // HEAD_DIM,
                                          riders=[_ag_rider([W_FF1], [s_ff1])])

    def merge_epi(accs, ex):
        sa, sb = jax.nn.sigmoid(ex[0].astype(F32)), jax.nn.sigmoid(ex[1].astype(F32))
        return [sa * accs[0] + sb * accs[1], accs[0], accs[1]]

    (merged, ya, yb), (ff2_a,) = _mm("branch_up_merge", [(pa, wa_f), (att, wb_f)], M=S, N=D, K=PW,
                                     extras=[(proj, "tile", 4 * PW), (proj, "tile", 4 * PW + D)],
                                     outs=[_tile_out(BF16)] * 3, epi=merge_epi,
                                     riders=[_ag_rider([W_FF2], [s_ff2], chunks=(0, 1))])
    (x1, o), (ff2_b,) = _mm("out_proj", [(merged, wo_f)], M=S, N=D, K=D, extras=[(x2, "tile", 0), (gate1, "row", 0)],
                            outs=[_tile_out(F32), _tile_out(BF16)], epi=lambda accs, ex: [ex[0] + ex[1] * accs[0], accs[0]],
                            riders=[_ag_rider([W_FF2], ff2_a, chunks=(1, 2))], tm=2048, tn=512, tk=2048)
    h2 = _norm_mod("norm2_mod", x1, norm2_w, scale2, shift2)
    (rl,), ((wff2_f,),) = _mm("ff1", [(h2, wff1_f)], M=S, N=FF, K=D, outs=[_tile_out(BF16)], **WIDE,
                              epi=lambda accs, ex: [jnp.maximum(accs[0], 0.0)],
                              riders=[_ag_rider([W_FF2], ff2_b, chunks=(2, 4))])

    def square(a):
        af = a.astype(F32)
        return (af * af).astype(BF16)

    def loss_epi(accs, ex):
        x1_t, tgt_t, g2 = ex
        f = accs[0]
        diff = (x1_t + g2 * f) - tgt_t
        dy = diff * (1.0 / D)
        return [dy, dy * g2, _colsum(dy * f), _colsum(diff * diff)]

    dy, df, dgate2_p, loss_p = _mm("ff2_loss", [(rl, wff2_f)], M=S, N=D, K=FF, a_pro=square, **DEEP,
                                   extras=[(x1, "tile", 0), (tgt, "tile", 0), (gate2, "row", 0)],
                                   outs=[_tile_out(F32), _tile_out(BF16), _COLSUM, _COLSUM], epi=loss_epi)

    tied = []

    def behind(token, a):
        a, token = lax.optimization_barrier((a, token))
        tied.append(token)
        return a

    def pair_sums(group, partials, got):
        return [_pair_sum(w, g, r, c_arr) for w, g, r in zip(group, partials, got)]

    def chip_sums(group, sums, from_chips):
        return [_chip_sum(w, p, q, cc_arr) for w, p, q in zip(group, sums, from_chips)]

    first = lambda accs, ex: [accs[0]]
    gmm = dict(ta=True, outs=[_tile_out(BF16)], epi=first, **WIDE)
    (g_ff2,) = _mm("grad_w_ff2", [(rl, df)], M=FF, N=D, K=S, a_pro=square, ta=True, tm=1024, tn=2048, tk=2048,
                   outs=[_tile_out(BF16)], epi=first)
    flight, token = _split_start("pair_w_ff2_start", "pair", [W_FF2], [g_ff2])
    (dz1,) = _mm("d_ff_hidden", [(behind(token, df), wff2_f)], M=S, N=FF, K=D, tb=True, extras=[(rl, "tile", 0)],
                 outs=[_tile_out(BF16)], epi=lambda accs, ex: [accs[0] * (2.0 * ex[0].astype(F32))], **WIDE)
    sum_ff2 = pair_sums([W_FF2], *_split_wait("pair_w_ff2_wait", flight, after=[dz1] + tied))
    chip_ff2, token = _split_start("chip_w_ff2_start", "chip", [W_FF2], sum_ff2)
    (g_ff1,) = _mm("grad_w_ff1", [(behind(token, h2), dz1)], M=D, N=FF, K=S, **gmm)
    flight, token = _split_start("pair_w_ff1_start", "pair", [W_FF1], [g_ff1])
    (dh2,) = _mm("d_h2", [(behind(token, dz1), wff1_f)], M=S, N=D, K=FF, tb=True, outs=[_tile_out(F32)], epi=first,
                 **DEEPER)
    sum_ff1 = pair_sums([W_FF1], *_split_wait("pair_w_ff1_wait", flight, after=[dh2] + tied))
    chip_ff1, token = _split_start("chip_w_ff1_start", "chip", [W_FF1], sum_ff1)
    dx1, dshift2_p, dscale2_p, gn2_p, do, dgate1_p = _norm_mod_bwd("norm2_bwd", behind(token, dh2), x1, dy, norm2_w, scale2,
                                                                   gate_o=(o, gate1))
    (g_wo,) = _mm("grad_w_o", [(merged, do)], M=D, N=D, K=S, **gmm)

    def gate_epi(accs, ex):
        dm = accs[0]
        sa, sb = jax.nn.sigmoid(ex[0].astype(F32)), jax.nn.sigmoid(ex[1].astype(F32))
        ya_t, yb_t = ex[2].astype(F32), ex[3].astype(F32)
        return [dm * sa, dm * sb, dm * ya_t * (sa * (1.0 - sa)), dm * yb_t * (sb * (1.0 - sb))]

    dya, dyb, dga, dgb = _mm("d_merged", [(do, wo_f)], M=S, N=D, K=D, tb=True, tm=1024, tn=512, tk=2048,
                             extras=[(proj, "tile", 4 * PW), (proj, "tile", 4 * PW + D), (ya, "tile", 0), (yb, "tile", 0)],
                             outs=[_tile_out(BF16)] * 4, epi=gate_epi)
    both = lambda accs, ex: [accs[0], accs[1]]
    g_wa, g_wb = _mm("grad_w_up", [(pa, dya), (att, dyb)], M=PW, N=D, K=S, ta=True, outs=[_tile_out(BF16)] * 2, epi=both,
                     **WIDE)
    mid = [W_A, W_B, W_O]
    flight, token = _split_start("pair_mid_start", "pair", mid, [g_wa, g_wb, g_wo])
    dpa, datt = _mm("d_branches", [(dya, wa_f), (behind(token, dyb), wb_f)], M=S, N=PW, K=D, tb=True,
                    outs=[_tile_out(F32), _tile_out(BF16)], epi=both, tm=1024, tn=512, tk=2048)
    sum_mid = pair_sums(mid, *_split_wait("pair_mid_wait", flight, after=[datt] + tied))
    chip_mid, token = _split_start("chip_mid_start", "chip", mid, sum_mid)
    du, g_wpool4, gscale_p = _pool_bwd(dpa, pooled, wpool_f, pool_scale, S, PW)
    dq, dk, dv, gq_p, gk_p = _attn_bwd(proj, behind(token, datt), attf, q_norm_w, k_norm_w, S, H, PW // HEAD_DIM)
    dproj = jnp.concatenate([du, dq, dk, dv, dga, dgb], axis=1)
    early = [W_FF1, W_FF2]
    sum_ff1, q_ff1 = _split_wait("chip_w_ff1_wait", chip_ff1, after=[dq] + tied)
    sum_ff2, q_ff2 = _split_wait("chip_w_ff2_wait", chip_ff2, after=[dq] + tied)
    halves_early = chip_sums(early, sum_ff1 + sum_ff2, q_ff1 + q_ff2)
    (g_win,), (grads_early,) = _mm("grad_w_in", [(h, dproj)], M=D, N=IN, K=S, riders=[_sf_rider(early, halves_early)],
                                   **gmm)
    last = [W_IN, W_POOL]
    g_last = [g_win, g_wpool4.reshape(PW, cg)]
    sum_mid, q_mid = _split_wait("chip_mid_wait", chip_mid, after=[g_win] + tied)
    halves_mid = chip_sums(mid, sum_mid, q_mid)
    (dh,), (got_last, grads_mid) = _mm("d_h", [(dproj, win_f)], M=S, N=D, K=IN, tb=True, outs=[_tile_out(F32)], epi=first,
                                       riders=[_px_rider(last, g_last), _sf_rider(mid, halves_mid)], **DEEPER)
    sum_last = pair_sums(last, g_last, got_last)
    grad_x, dshift1_p, dscale1_p, gn1_p = _norm_mod_bwd("norm1_bwd", dh, x2, dx1, norm1_w, scale1)

    parts = [dshift1_p, dscale1_p, dgate1_p, dshift2_p, dscale2_p, dgate2_p, gn1_p, gn2_p,
             gscale_p.reshape(1, 1, PW), gq_p, gk_p]
    widths = [D] * 8 + [PW, HEAD_DIM, HEAD_DIM]
    used = sum(widths)
    P = -(-(used + LANES) // (SUBLANES * LANES)) * (SUBLANES * LANES)
    packed = _pack_partials(parts + [loss_p], widths, P)
    gathered = _dev_allgather("gather_vector_grads", packed.reshape(SUBLANES, P // SUBLANES)).reshape(N_DEV, P)
    sum_last, gathered = lax.optimization_barrier((sum_last, gathered))
    chip_last, token = _split_start("chip_last_start", "chip", last, sum_last)
    small = [(b_ada, m_b_ada, v_b_ada), (norm1_w, m_norm1_w, v_norm1_w), (norm2_w, m_norm2_w, v_norm2_w),
             (pool_scale, m_pool_scale, v_pool_scale), (q_norm_w, m_q_norm_w, v_q_norm_w),
             (k_norm_w, m_k_norm_w, v_k_norm_w)]
    offsets = [(0, 6 * D), (6 * D, D), (7 * D, D), (8 * D, PW), (8 * D + PW, HEAD_DIM), (8 * D + PW + HEAD_DIM, HEAD_DIM)]
    su = _small_update(gathered, offsets, small, used)
    (g_b, d_b, nm_b, nv_b, g_n1, d_n1, nm_n1, nv_n1, g_n2, d_n2, nm_n2, nv_n2, g_ps, d_ps, nm_ps, nv_ps,
     g_qn, d_qn, nm_qn, nv_qn, g_kn, d_kn, nm_kn, nv_kn, loss_sum) = su
    dmod_sh = lax.dynamic_slice(gathered, (0, chip * A_COLS), (N_DEV, A_COLS))
    dmod_sh, token = lax.optimization_barrier((dmod_sh, token))
    g_ada, d_ada, nm_ada, nv_ada = _ada_update(sc_all.T, dmod_sh, w_ada[0], m_w_ada[0], v_w_ada[0])

    upd_done = [_adamw("adamw_" + w.name, a, g, m, v, after=token)
                for w, a, g, m, v in zip(ws[2:], w32[2:], list(grads_mid) + list(grads_early), m32[2:], v32[2:])]

    sum_last, q_last = _split_wait("chip_last_wait", chip_last, after=[nv_ada] + [u[3] for u in upd_done])
    halves_last = chip_sums(last, sum_last, q_last)
    filled = _run_rider("grad_sibling_fill", _sf_rider(last, halves_last))
    upd = [_adamw("adamw_" + w.name, a, g, m, v) for w, a, g, m, v in zip(ws[:2], w32[:2], filled, m32[:2], v32[:2])]
    upd += upd_done

    loss = (0.5 / D) * loss_sum[0, 0]

    def up(a):
        return a[None]

    def pool4(a):
        return a.reshape(1, N_GROUPS, cg // N_CHIPS, cg)

    (gr_win, d_win, nm_win, nv_win), (gr_wp, d_wp, nm_wp, nv_wp), (gr_wa, d_wa, nm_wa, nv_wa), \
        (gr_wb, d_wb, nm_wb, nv_wb), (gr_wo, d_wo, nm_wo, nv_wo), (gr_f1, d_f1, nm_f1, nv_f1), \
        (gr_f2, d_f2, nm_f2, nv_f2) = upd
    return (
        loss, grad_x[None],
        up(g_ada), g_b, g_n1, up(gr_win), g_qn, g_kn, pool4(gr_wp), g_ps, up(gr_wa), up(gr_wb), up(gr_wo), g_n2,
        up(gr_f1), up(gr_f2),
        up(d_ada), d_b, d_n1, up(d_win), d_qn, d_kn, pool4(d_wp), d_ps, up(d_wa), up(d_wb), up(d_wo), d_n2,
        up(d_f1), up(d_f2),
        up(nm_ada), nm_b, nm_n1, up(nm_win), nm_qn, nm_kn, pool4(nm_wp), nm_ps, up(nm_wa), up(nm_wb), up(nm_wo), nm_n2,
        up(nm_f1), up(nm_f2),
        up(nv_ada), nv_b, nv_n1, up(nv_win), nv_qn, nv_kn, pool4(nv_wp), nv_ps, up(nv_wa), up(nv_wb), up(nv_wo), nv_n2,
        up(nv_f1), up(nv_f2),
    )
```

```python
import functools
import math

import jax
import jax.numpy as jnp
from jax import lax
from jax.experimental import pallas as pl
from jax.experimental.pallas import tpu as pltpu

F32 = jnp.float32
BF16 = jnp.bfloat16
MESH = pl.DeviceIdType.MESH
ANY = pl.BlockSpec(memory_space=pl.ANY)

EPS = 1e-6
HEAD_DIM = 128
LANES, SUBLANES = 128, 8
POOL_WINDOWS = (2, 4, 8, 16)
N_GROUPS = len(POOL_WINDOWS)
assert POOL_WINDOWS == tuple(2 << g for g in range(N_GROUPS))
N_CHIPS = 4
N_DEV = 8
ADAM_LR, ADAM_B1, ADAM_B2, ADAM_EPS, ADAM_WD, ADAM_STEP = 0.001, 0.9, 0.999, 1e-08, 0.01, 10
VMEM_LIMIT_V7X = 56 * 1024 * 1024
ATT_T = 256
ATT_GROUP = 8
POOL_T = 256


def _pcall(body, **kw):
    return pl.pallas_call(body, **kw)


def _params(sem=None):
    return pltpu.CompilerParams(dimension_semantics=sem, vmem_limit_bytes=VMEM_LIMIT_V7X)


def _tile(n, pref):
    if n <= pref:
        return n
    t = pref
    while n % t:
        t //= 2
    return t


class _Rider:
    def __init__(self, arrays, out_shape, sems, start, finish, aliases=None, steps=()):
        self.arrays, self.out_shape, self.sems = list(arrays), list(out_shape), list(sems)
        self.start, self.finish, self.aliases, self.steps = start, finish, aliases or {}, list(steps)


def _ride(name, body, riders, arrays, *, grid, in_specs, out_specs, out_shape, scratch_shapes, sem, scalars=None):
    n_in, n_out, n_scr = len(arrays), len(out_shape), len(scratch_shapes)
    r_arrays = [a for r in riders for a in r.arrays]
    r_outs = [o for r in riders for o in r.out_shape]
    r_sems = [s for r in riders for s in r.sems]
    n_hooks = max([len(r.steps) for r in riders], default=0)
    total = math.prod(grid)
    aliases, off_i, off_o = {}, n_in + (scalars is not None), n_out
    for r in riders:
        for a, o in r.aliases.items():
            aliases[off_i + a] = off_o + o
        off_i += len(r.arrays)
        off_o += len(r.out_shape)

    def full(*refs):
        p = 0
        groups = []
        for n in (n_in, len(r_arrays), n_out, len(r_outs), n_scr, len(r_sems)):
            groups.append(refs[p:p + n])
            p += n
        ins, rin, outs, rout, scr, rsem = groups

        def each(what):
            a = o = s = 0
            for r in riders:
                fn = what(r)
                if fn is not None:
                    fn(rin[a:a + len(r.arrays)], rout[o:o + len(r.out_shape)], rsem[s:s + len(r.sems)])
                a, o, s = a + len(r.arrays), o + len(r.out_shape), s + len(r.sems)

        if riders:
            lin = 0
            for d, g in enumerate(grid):
                lin = lin * g + pl.program_id(d)
            pl.when(lin == 0)(lambda: each(lambda r: r.start))
            for t in range(n_hooks):
                pl.when(lin == min(total - 1, ((t + 1) * total) // n_hooks))(
                    lambda t=t: each(lambda r: r.steps[t] if t < len(r.steps) else None))
        body(*ins, *outs, *scr)
        if riders:
            pl.when(lin == total - 1)(lambda: each(lambda r: r.finish))

    specs = dict(grid=grid, in_specs=list(in_specs) + [ANY] * len(r_arrays),
                 out_specs=list(out_specs) + [ANY] * len(r_outs), scratch_shapes=list(scratch_shapes) + r_sems)
    common = dict(name=name, out_shape=list(out_shape) + r_outs, input_output_aliases=aliases,
                  compiler_params=_params(("arbitrary",) * len(grid) if riders else sem))
    if scalars is None:
        res = _pcall(full, **specs, **common)(*arrays, *r_arrays)
    else:
        res = _pcall(lambda _, *refs: full(*refs), **common,
                     grid_spec=pltpu.PrefetchScalarGridSpec(num_scalar_prefetch=1, **specs))(scalars, *arrays, *r_arrays)
    if not riders:
        return res
    main, rest, per = res[:n_out], res[n_out:], []
    for r in riders:
        per.append(rest[:len(r.out_shape)])
        rest = rest[len(r.out_shape):]
    return main, per


def _run_rider(name, rider):
    def body(*refs):
        n_a, n_o = len(rider.arrays), len(rider.out_shape)
        ins, outs, sems = refs[:n_a], refs[n_a:n_a + n_o], refs[n_a + n_o:]
        for fn in [rider.start] + rider.steps + [rider.finish]:
            fn(ins, outs, sems)

    return _pcall(body, name=name, out_shape=rider.out_shape, in_specs=[ANY] * len(rider.arrays),
                  out_specs=[ANY] * len(rider.out_shape), scratch_shapes=rider.sems,
                  input_output_aliases=rider.aliases)(*rider.arrays)


def _mm(name, pairs, *, M, N, K, ta=False, tb=False, tm=512, tn=1024, tk=1024,
        a_pro=None, b_pro=None, extras=(), outs, epi, riders=()):
    tm, tn, tk = _tile(M, tm), _tile(N, tn), _tile(K, tk)
    n_i, n_j, n_k = M // tm, N // tn, K // tk
    n_p, n_e = len(pairs), len(extras)
    arrays, in_specs = [], []
    for a, _ in pairs:
        arrays.append(a)
        in_specs.append(pl.BlockSpec((tk, tm), lambda i, j, k: (k, i)) if ta
                        else pl.BlockSpec((tm, tk), lambda i, j, k: (i, k)))
    for _, b in pairs:
        arrays.append(b)
        in_specs.append(pl.BlockSpec((tn, tk), lambda i, j, k: (j, k)) if tb
                        else pl.BlockSpec((tk, tn), lambda i, j, k: (k, j)))
    for arr, kind, off in extras:
        ob = off // tn
        assert off % tn == 0
        arrays.append(arr)
        if kind == "tile":
            in_specs.append(pl.BlockSpec((tm, tn), lambda i, j, k, ob=ob: (i, j + ob)))
        else:
            in_specs.append(pl.BlockSpec((1, tn), lambda i, j, k, ob=ob: (0, j + ob)))
    out_shape, out_specs = [], []
    for o in outs:
        if o["kind"] == "tile":
            out_shape.append(jax.ShapeDtypeStruct((M, N), o["dtype"]))
            out_specs.append(pl.BlockSpec((tm, tn), lambda i, j, k: (i, j)))
        else:
            out_shape.append(jax.ShapeDtypeStruct((n_i, 1, N), F32))
            out_specs.append(pl.BlockSpec((1, 1, tn), lambda i, j, k: (i, 0, j)))
    dims = (((0 if ta else 1,), (1 if tb else 0,)), ((), ()))

    def body(*refs):
        a_refs, b_refs = refs[:n_p], refs[n_p:2 * n_p]
        e_refs = refs[2 * n_p:2 * n_p + n_e]
        o_refs = refs[2 * n_p + n_e:2 * n_p + n_e + len(outs)]
        acc_refs = refs[2 * n_p + n_e + len(outs):]

        def product(p):
            a, b = a_refs[p][...], b_refs[p][...]
            if a_pro is not None:
                a = a_pro(a)
            if b_pro is not None:
                b = b_pro(b)
            return lax.dot_general(a, b, dims, preferred_element_type=F32)

        def write(accs):
            vals = epi(accs, [e[...] for e in e_refs])
            for o, o_ref, val in zip(outs, o_refs, vals):
                if o["kind"] == "tile":
                    o_ref[...] = val.astype(o_ref.dtype)
                else:
                    o_ref[0] = val

        if n_k == 1:
            write([product(p) for p in range(n_p)])
            return
        k = pl.program_id(2)

        @pl.when(k == 0)
        def _():
            for acc in acc_refs:
                acc[...] = jnp.zeros_like(acc)

        for p in range(n_p):
            acc_refs[p][...] += product(p)

        pl.when(k == n_k - 1)(lambda: write([acc[...] for acc in acc_refs]))

    return _ride(name, body, riders, arrays, grid=(n_i, n_j, n_k), in_specs=in_specs, out_specs=out_specs,
                 out_shape=out_shape, scratch_shapes=[pltpu.VMEM((tm, tn), F32) for _ in pairs] if n_k > 1 else [],
                 sem=("parallel", "parallel", "arbitrary"))


def _tile_out(dtype):
    return {"kind": "tile", "dtype": dtype}


_COLSUM = {"kind": "colsum"}


def _colsum(v):
    return jnp.sum(v, axis=0, keepdims=True)


def _norm_mod(name, x, norm_w, scale, shift):
    S, D = x.shape
    tr = _tile(S, 256)

    def body(x_ref, nw_ref, sc_ref, sh_ref, h_ref):
        xv = x_ref[...]
        r = lax.rsqrt(jnp.mean(xv * xv, axis=-1, keepdims=True) + EPS)
        h_ref[...] = ((xv * r * nw_ref[...]) * (1.0 + sc_ref[...]) + sh_ref[...]).astype(BF16)

    row = pl.BlockSpec((1, D), lambda i: (0, 0))
    til = pl.BlockSpec((tr, D), lambda i: (i, 0))
    return _pcall(body, name=name, grid=(S // tr,), in_specs=[til, row, row, row], out_specs=til,
                  out_shape=jax.ShapeDtypeStruct((S, D), BF16), compiler_params=_params(("parallel",)))(
                      x, norm_w, scale, shift)


def _norm_mod_bwd(name, dh, x, dres, norm_w, scale, gate_o=None):
    S, D = x.shape
    tr = _tile(S, 256)
    n_r = S // tr
    with_gate = gate_o is not None

    def body(*refs):
        if with_gate:
            dh_ref, x_ref, dres_ref, nw_ref, sc_ref, o_ref, g_ref, dx_ref, p1, p2, p3, do_ref, p4 = refs
        else:
            dh_ref, x_ref, dres_ref, nw_ref, sc_ref, dx_ref, p1, p2, p3 = refs
        dhv, xv, nw = dh_ref[...], x_ref[...], nw_ref[...]
        r = lax.rsqrt(jnp.mean(xv * xv, axis=-1, keepdims=True) + EPS)
        xh = xv * r
        p1[0] = _colsum(dhv)
        p2[0] = _colsum(dhv * (xh * nw))
        dn = dhv * (1.0 + sc_ref[...])
        p3[0] = _colsum(dn * xh)
        dxh = dn * nw
        dx = dres_ref[...] + r * (dxh - xh * jnp.mean(dxh * xh, axis=-1, keepdims=True))
        dx_ref[...] = dx
        if with_gate:
            do_ref[...] = (dx * g_ref[...]).astype(BF16)
            p4[0] = _colsum(dx * o_ref[...].astype(F32))

    row = pl.BlockSpec((1, D), lambda i: (0, 0))
    til = pl.BlockSpec((tr, D), lambda i: (i, 0))
    part = pl.BlockSpec((1, 1, D), lambda i: (i, 0, 0))
    part_shape = jax.ShapeDtypeStruct((n_r, 1, D), F32)
    in_specs = [til, til, til, row, row]
    arrays = [dh, x, dres, norm_w, scale]
    out_specs = [til, part, part, part]
    out_shape = [jax.ShapeDtypeStruct((S, D), F32), part_shape, part_shape, part_shape]
    if with_gate:
        in_specs += [til, row]
        arrays += list(gate_o)
        out_specs += [til, part]
        out_shape += [jax.ShapeDtypeStruct((S, D), BF16), part_shape]
    return _pcall(body, name=name, grid=(n_r,), in_specs=in_specs, out_specs=out_specs, out_shape=out_shape,
                  compiler_params=_params(("parallel",)))(*arrays)


def _pool_w_specs(rows, cg):
    return [pl.BlockSpec((rows, cg), lambda g, j=j: (N_GROUPS * j + g, 0)) for j in range(N_CHIPS)]


def _pool_fwd(proj, wp_full, pool_scale, S, PW):
    cg = PW // N_GROUPS
    rows = cg // N_CHIPS
    T = _tile(S, POOL_T)
    n_t = S // T

    def body(u_ref, w0, w1, w2, w3, ps_ref, pooled_ref, pa_ref):
        g = pl.program_id(0)
        win = jnp.left_shift(2, g)
        w = jnp.concatenate([w0[...], w1[...], w2[...], w3[...]], axis=0)
        t_i = lax.broadcasted_iota(jnp.int32, (T, T), 0)
        j_i = lax.broadcasted_iota(jnp.int32, (T, T), 1)
        b_cur = ((j_i <= t_i) & (j_i > t_i - win)).astype(BF16)
        b_prev = (j_i - T > t_i - win).astype(BF16)
        row = lax.broadcasted_iota(jnp.int32, (T, 1), 0)
        for r in range(n_t):
            cur = u_ref[r * T:(r + 1) * T, :]
            ws = jnp.dot(b_cur, cur, preferred_element_type=F32)
            if r > 0:
                ws += jnp.dot(b_prev, u_ref[(r - 1) * T:r * T, :], preferred_element_type=F32)
            count = jnp.minimum(row + (r * T + 1), win).astype(F32)
            pooled = (ws / count - cur.astype(F32)).astype(BF16)
            pooled_ref[r * T:(r + 1) * T, :] = pooled
            mixed = jnp.dot(pooled, w, preferred_element_type=F32)
            pa_ref[r * T:(r + 1) * T, :] = (mixed * ps_ref[...]).astype(BF16)

    col = pl.BlockSpec((S, cg), lambda g: (0, g))
    return _pcall(
        body, name="pool_fwd", grid=(N_GROUPS,),
        in_specs=[col] + _pool_w_specs(rows, cg) + [pl.BlockSpec((1, cg), lambda g: (0, g))],
        out_specs=[col, col],
        out_shape=[jax.ShapeDtypeStruct((S, PW), BF16), jax.ShapeDtypeStruct((S, PW), BF16)],
        compiler_params=_params(("parallel",)),
    )(proj, wp_full, wp_full, wp_full, wp_full, pool_scale)


def _pool_bwd(dpa, pooled, wp_full, pool_scale, S, PW):
    cg = PW // N_GROUPS
    rows = cg // N_CHIPS
    T = _tile(S, POOL_T)
    n_t = S // T

    def body(dpa_ref, pooled_ref, w0, w1, w2, w3, ps_ref, du_ref, gw_ref, gs_ref, dp_s, dpc_s, dmx_s):
        g = pl.program_id(0)
        win = jnp.left_shift(2, g)
        w = jnp.concatenate([w0[...], w1[...], w2[...], w3[...]], axis=0)
        row = lax.broadcasted_iota(jnp.int32, (T, 1), 0)
        gs = jnp.zeros((1, cg), F32)
        for r in range(n_t):
            sl = slice(r * T, (r + 1) * T)
            mixed = jnp.dot(pooled_ref[sl, :], w, preferred_element_type=F32)
            dpa_t = dpa_ref[sl, :]
            gs += _colsum(dpa_t * mixed)
            dmx = (dpa_t * ps_ref[...]).astype(BF16)
            dmx_s[sl, :] = dmx
            dpo = lax.dot_general(dmx, w, (((1,), (1,)), ((), ())), preferred_element_type=F32)
            dp_s[sl, :] = dpo
            count = jnp.minimum(row + (r * T + 1), win).astype(F32)
            dpc_s[sl, :] = (dpo / count).astype(BF16)
        gs_ref[...] = gs
        gw = lax.dot_general(pooled_ref[...], dmx_s[...], (((0,), (0,)), ((), ())), preferred_element_type=F32)
        for j in range(N_CHIPS):
            gw_ref[j, 0] = gw[j * rows:(j + 1) * rows, :].astype(BF16)
        j_i = lax.broadcasted_iota(jnp.int32, (T, T), 0)
        t_i = lax.broadcasted_iota(jnp.int32, (T, T), 1)
        b_cur = ((t_i >= j_i) & (t_i < j_i + win)).astype(BF16)
        b_next = (t_i + T < j_i + win).astype(BF16)
        for r in range(n_t):
            sl = slice(r * T, (r + 1) * T)
            acc = jnp.dot(b_cur, dpc_s[sl, :], preferred_element_type=F32)
            if r + 1 < n_t:
                acc += jnp.dot(b_next, dpc_s[(r + 1) * T:(r + 2) * T, :], preferred_element_type=F32)
            du_ref[sl, :] = (acc - dp_s[sl, :]).astype(BF16)

    col = pl.BlockSpec((S, cg), lambda g: (0, g))
    return _pcall(
        body, name="pool_bwd", grid=(N_GROUPS,),
        in_specs=[col, col] + _pool_w_specs(rows, cg) + [pl.BlockSpec((1, cg), lambda g: (0, g))],
        out_specs=[col, pl.BlockSpec((N_CHIPS, 1, rows, cg), lambda g: (0, g, 0, 0)),
                   pl.BlockSpec((1, cg), lambda g: (0, g))],
        out_shape=[jax.ShapeDtypeStruct((S, PW), BF16),
                   jax.ShapeDtypeStruct((N_CHIPS, N_GROUPS, rows, cg), BF16),
                   jax.ShapeDtypeStruct((1, PW), F32)],
        scratch_shapes=[pltpu.VMEM((S, cg), F32), pltpu.VMEM((S, cg), BF16), pltpu.VMEM((S, cg), BF16)],
        compiler_params=_params(("parallel",)),
    )(dpa, pooled, wp_full, wp_full, wp_full, wp_full, pool_scale)


_NT = (((1,), (1,)), ((), ()))
_TN = (((0,), (0,)), ((), ()))


def _split_dot(v, tri):
    hi = v.astype(BF16)
    lo = (v - hi.astype(F32)).astype(BF16)
    return jnp.dot(hi, tri, preferred_element_type=F32) + jnp.dot(lo, tri, preferred_element_type=F32)


LOG2E = 1.4426950408889634
QK_SCALE = 1.0 / math.sqrt(HEAD_DIM)


def _sb_scores(q2_i, k_j, tri_l, masked):
    tq, tk = q2_i.shape[0], k_j.shape[0]
    s = lax.dot_general(q2_i, k_j, _NT, preferred_element_type=F32)
    lp = jnp.log(1.0 + jnp.exp2(-jnp.abs(s))) * LOG2E
    lb = jnp.minimum(s, 0.0) - lp
    l = lb - s
    mask = None
    if masked:
        mask = lax.broadcasted_iota(jnp.int32, (tq, tk), 0) > lax.broadcasted_iota(jnp.int32, (tq, tk), 1)
        l = jnp.where(mask, l, 0.0)
    return l, lb, lb + _split_dot(l, tri_l), mask


def _sb_weights(t, carry_l, mask):
    a = jnp.exp2(t + carry_l)
    return a if mask is None else jnp.where(mask, a, 0.0)


def _rowsum(v):
    return jnp.sum(v, axis=1, keepdims=True)


def _qk_norm(x_ref, w_ref):
    xv = x_ref[...].astype(F32)
    r = lax.rsqrt(jnp.mean(xv * xv, axis=-1, keepdims=True) + EPS)
    return xv * r, r


def _attn_fwd(proj, q_norm_w, k_norm_w, S, H, q_off, riders=()):
    t = _tile(S, ATT_T)
    n_q = S // t

    def body(q_ref, k_ref, v_ref, qw_ref, kw_ref, att_ref, attf_ref, qn_s, kn_s):
        qh, _ = _qk_norm(q_ref, qw_ref)
        qn_s[...] = (qh * qw_ref[...] * (QK_SCALE * LOG2E)).astype(BF16)
        kh, _ = _qk_norm(k_ref, kw_ref)
        kn_s[...] = (kh * kw_ref[...]).astype(BF16)
        tri_l = (lax.broadcasted_iota(jnp.int32, (t, t), 0) > lax.broadcasted_iota(jnp.int32, (t, t), 1)).astype(BF16)

        def rows(j):
            return pl.ds(pl.multiple_of(j * t, t), t)

        def q_step(i, _):
            q_i = qn_s[rows(i), :]

            def av(a, j):
                return jnp.dot(a.astype(BF16), v_ref[rows(j), :], preferred_element_type=F32)

            l, _, tt, mask = _sb_scores(q_i, kn_s[rows(i), :], tri_l, True)
            acc = av(_sb_weights(tt, 0.0, mask), i)
            carry = _rowsum(l)

            def single(_, c):
                carry, acc = c
                l, _, tt, _ = _sb_scores(q_i, kn_s[rows(i - 1), :], tri_l, False)
                return carry + _rowsum(l), acc + av(_sb_weights(tt, carry, None), i - 1)

            carry, acc = lax.fori_loop(0, i % 2, single, (carry, acc))
            top = i - 1 - i % 2

            def pair(p, c):
                carry, acc = c
                j0 = top - 2 * p
                l0, _, t0, _ = _sb_scores(q_i, kn_s[rows(j0), :], tri_l, False)
                l1, _, t1, _ = _sb_scores(q_i, kn_s[rows(j0 - 1), :], tri_l, False)
                mid = carry + _rowsum(l0)
                acc = acc + av(_sb_weights(t0, carry, None), j0) + av(_sb_weights(t1, mid, None), j0 - 1)
                return mid + _rowsum(l1), acc

            _, acc = lax.fori_loop(0, i // 2, pair, (carry, acc))
            att_ref[rows(i), :] = acc.astype(BF16)
            attf_ref[rows(i), :] = acc
            return 0

        lax.fori_loop(0, n_q, q_step, 0)

    def col(off):
        return pl.BlockSpec((S, HEAD_DIM), lambda h, off=off: (0, off + h))

    wspec = pl.BlockSpec((1, HEAD_DIM), lambda h: (0, 0))
    return _ride(
        "attn_fwd", body, riders, [proj, proj, proj, q_norm_w, k_norm_w], grid=(H,),
        in_specs=[col(q_off), col(q_off + H), col(q_off + 2 * H), wspec, wspec],
        out_specs=[col(0), col(0)],
        out_shape=[jax.ShapeDtypeStruct((S, H * HEAD_DIM), BF16), jax.ShapeDtypeStruct((S, H * HEAD_DIM), F32)],
        scratch_shapes=[pltpu.VMEM((S, HEAD_DIM), BF16), pltpu.VMEM((S, HEAD_DIM), BF16)],
        sem=("parallel",))


def _attn_bwd(proj, datt, attf, q_norm_w, k_norm_w, S, H, q_off, riders=()):
    t = _tile(S, ATT_T)
    n_q = S // t

    def body(q_ref, k_ref, v_ref, do_ref, o_ref, qw_ref, kw_ref, dq_ref, dk_ref, dv_ref, gq_ref, gk_ref,
             qn_s, kn_s, qz_s, kz_s, dk_s, dv_s, gq_s):
        qw, kw = qw_ref[...], kw_ref[...]
        qh, _ = _qk_norm(q_ref, qw_ref)
        qn_s[...] = (qh * qw * (QK_SCALE * LOG2E)).astype(BF16)
        qz_s[...] = (qh * qw * QK_SCALE).astype(BF16)
        kh, _ = _qk_norm(k_ref, kw_ref)
        kn_s[...] = (kh * kw).astype(BF16)
        kz_s[...] = (kh * kw * QK_SCALE).astype(BF16)
        dk_s[...] = jnp.zeros_like(dk_s)
        dv_s[...] = jnp.zeros_like(dv_s)
        gq_s[...] = jnp.zeros_like(gq_s)
        r_i = lax.broadcasted_iota(jnp.int32, (t, t), 0)
        c_i = lax.broadcasted_iota(jnp.int32, (t, t), 1)
        tri_l = (r_i > c_i).astype(BF16)
        tri_e = (r_i >= c_i).astype(BF16)

        def rows(j):
            return pl.ds(pl.multiple_of(j * t, t), t)

        def q_step(i, _):
            q_i = qn_s[rows(i), :]
            do_i = do_ref[rows(i), :]
            d_i = _rowsum(do_i.astype(F32) * o_ref[rows(i), :])

            def scores(j, masked):
                l, lb, tt, mask = _sb_scores(q_i, kn_s[rows(j), :], tri_l, masked)
                da = lax.dot_general(do_i, v_ref[rows(j), :], _NT, preferred_element_type=F32)
                return l, lb, tt, mask, da

            def grads(j, sc, carry_l, carry_e, dq_acc):
                l, lb, tt, mask, da = sc
                a_bf = _sb_weights(tt, carry_l, mask).astype(BF16)
                e = da * a_bf.astype(F32)
                p = (d_i - carry_e) - _split_dot(e, tri_e)
                dz = e - jnp.exp2(lb) * (e + p)
                if mask is not None:
                    dz = jnp.where(mask, dz, 0.0)
                dz = dz.astype(BF16)
                dk_s[rows(j), :] += lax.dot_general(dz, qz_s[rows(i), :], _TN, preferred_element_type=F32)
                dv_s[rows(j), :] += lax.dot_general(a_bf, do_i, _TN, preferred_element_type=F32)
                return (carry_l + _rowsum(l), carry_e + _rowsum(e),
                        dq_acc + jnp.dot(dz, kz_s[rows(j), :], preferred_element_type=F32))

            zero = jnp.zeros((t, 1), F32)
            first = (zero, zero, jnp.zeros((t, HEAD_DIM), F32))

            def group(js, diagonal_first, c):
                scs = [scores(j, diagonal_first and n == 0) for n, j in enumerate(js)]
                for j, sc in zip(js, scs):
                    c = grads(j, sc, *c)
                return c

            n_first = i % ATT_GROUP
            c = lax.switch(n_first, [functools.partial(group, [i - u for u in range(n + 1)], True, first)
                                     for n in range(ATT_GROUP)])
            top = i - 1 - n_first

            def whole(p, c):
                j0 = top - ATT_GROUP * p
                return group([j0 - u for u in range(ATT_GROUP)], False, c)

            _, _, dqn = lax.fori_loop(0, (i - n_first) // ATT_GROUP, whole, c)
            qv = q_ref[rows(i), :].astype(F32)
            r = lax.rsqrt(jnp.mean(qv * qv, axis=-1, keepdims=True) + EPS)
            xh = qv * r
            gq_s[...] += _colsum(dqn * xh)
            dxh = dqn * qw
            dq_ref[rows(i), :] = (r * (dxh - xh * jnp.mean(dxh * xh, axis=-1, keepdims=True))).astype(BF16)
            return 0

        lax.fori_loop(0, n_q, q_step, 0)
        gq_ref[0] = gq_s[...]
        kh, rk = _qk_norm(k_ref, kw_ref)
        dkn = dk_s[...]
        gk_ref[0] = _colsum(dkn * kh)
        dxh = dkn * kw
        dk_ref[...] = (rk * (dxh - kh * jnp.mean(dxh * kh, axis=-1, keepdims=True))).astype(BF16)
        dv_ref[...] = dv_s[...].astype(BF16)

    def col(off):
        return pl.BlockSpec((S, HEAD_DIM), lambda h, off=off: (0, off + h))

    wspec = pl.BlockSpec((1, HEAD_DIM), lambda h: (0, 0))
    gspec = pl.BlockSpec((1, 1, HEAD_DIM), lambda h: (h, 0, 0))
    act = jax.ShapeDtypeStruct((S, H * HEAD_DIM), BF16)
    gsh = jax.ShapeDtypeStruct((H, 1, HEAD_DIM), F32)
    return _ride(
        "attn_bwd", body, riders, [proj, proj, proj, datt, attf, q_norm_w, k_norm_w], grid=(H,),
        in_specs=[col(q_off), col(q_off + H), col(q_off + 2 * H), col(0), col(0), wspec, wspec],
        out_specs=[col(0), col(0), col(0), gspec, gspec],
        out_shape=[act, act, act, gsh, gsh],
        scratch_shapes=[pltpu.VMEM((S, HEAD_DIM), BF16)] * 4 + [pltpu.VMEM((S, HEAD_DIM), F32)] * 2
        + [pltpu.VMEM((1, HEAD_DIM), F32)],
        sem=("parallel",))


def _place():
    x, y, c = lax.axis_index("x"), lax.axis_index("y"), lax.axis_index("c")
    chips = [(1 - x, y), (x, 1 - y), (1 - x, 1 - y)]
    return x, y, c, chips


def _dev_allgather(name, v):
    m_per, n = v.shape

    def body(x_ref, out_ref, send_sems, recv_sems, local_sem):
        x, y, c, _ = _place()
        me = (x, y, c)

        def rows(px, py, pc):
            return out_ref.at[pl.ds((4 * px + 2 * py + pc) * m_per, m_per), :]

        def peer(r):
            return tuple(1 - b if (r >> s) & 1 else b for b, s in zip(me, (2, 1, 0)))

        def copy(r, block, to, src=None):
            return pltpu.make_async_remote_copy(
                src_ref=rows(*block) if src is None else src, dst_ref=rows(*block),
                send_sem=send_sems.at[r - 1], recv_sem=recv_sems.at[r - 1], device_id=to, device_id_type=MESH)

        mine = pltpu.make_async_copy(x_ref, rows(*me), local_sem)
        mine.start()
        sends = [copy(r, me, peer(r), src=x_ref) for r in range(1, N_DEV)]
        for cp in sends:
            cp.start()
        for r in range(1, N_DEV):
            copy(r, peer(r), me).wait_recv()
        for cp in sends:
            cp.wait_send()
        mine.wait()

    return _pcall(
        body, name=name, out_shape=jax.ShapeDtypeStruct((N_DEV * m_per, n), v.dtype),
        in_specs=[pl.BlockSpec(memory_space=pltpu.VMEM)], out_specs=pl.BlockSpec(memory_space=pltpu.VMEM),
        scratch_shapes=[pltpu.SemaphoreType.DMA((7,)), pltpu.SemaphoreType.DMA((7,)), pltpu.SemaphoreType.DMA],
        compiler_params=pltpu.CompilerParams(vmem_limit_bytes=VMEM_LIMIT_V7X),
    )(v)


class _W:
    def __init__(self, name, kind, R, C):
        self.name, self.kind, self.R, self.C = name, kind, R, C

    @property
    def shard_shape(self):
        return (self.R, self.C // N_CHIPS) if self.kind == "col" else (self.R // N_CHIPS, self.C)

    @property
    def half_rows(self):
        return self.shard_shape[0] // 2

    def shard_half(self, ref, half):
        return ref.at[pl.ds(half * self.half_rows, self.half_rows), :]

    def region(self, full_ref, chip, half):
        hr = self.half_rows
        if self.kind == "col":
            cw = self.C // N_CHIPS
            return full_ref.at[pl.ds(half * hr, hr), pl.ds(chip * cw, cw)]
        return full_ref.at[pl.ds(chip * (2 * hr) + half * hr, hr), :]


def _ag_rider(ws, fulls, n_ch=4, chunks=None):
    n_w = len(ws)
    lo, hi = chunks or (0, n_ch)
    per = 6

    def parts(full, sems):
        send_sems, recv_sems = sems
        x, y, c, _ = _place()
        xn, yn, dg = (1 - x, y), (x, 1 - y), (1 - x, 1 - y)
        via = (x + (1 - c) * (1 - 2 * x), y + c * (1 - 2 * y))
        to = (x + c * (1 - 2 * x), y + (1 - c) * (1 - 2 * y))

        def reg(i, chip, half, t):
            nr = ws[i].half_rows // n_ch
            return ws[i].region(full[i], 2 * chip[0] + chip[1], half).at[pl.ds(t * nr, nr), :]

        def copy(r, i, t, k, dev):
            s = (i * (hi - lo) + t - lo) * per + k
            return pltpu.make_async_remote_copy(src_ref=r, dst_ref=r, send_sem=send_sems.at[s],
                                                recv_sem=recv_sems.at[s], device_id=dev, device_id_type=MESH)

        def direct(i, t, k):
            return copy(reg(i, (x, y), c, t), i, t, k, (*(via, to)[k], c))

        def direct_in(i, t, k):
            return copy(reg(i, (via, to)[k], c, t), i, t, k, (*(via, to)[k], c))

        def relay(i, t):
            return copy(reg(i, via, c, t), i, t, 2, (*to, c))

        def relay_in(i, t):
            return copy(reg(i, dg, c, t), i, t, 2, (*to, c))

        def hand(i, t, k, half):
            return copy(reg(i, (xn, yn, dg)[k], half, t), i, t, 3 + k, (x, y, 1 - c))

        return c, direct, direct_in, relay, relay_in, hand

    def start(_, full, sems):
        _, direct, _, _, _, _ = parts(full, sems)
        for t in range(lo, hi):
            for i in range(n_w):
                direct(i, t, 0).start()
                direct(i, t, 1).start()

    def arrived(t):
        def step(_, full, sems):
            c, _, direct_in, relay, relay_in, hand = parts(full, sems)
            for i in range(n_w):
                direct_in(i, t, 0).wait_recv()
                direct_in(i, t, 1).wait_recv()
                relay(i, t).start()
                hand(i, t, 0, c).start()
                hand(i, t, 1, c).start()
        return step

    def finish(_, full, sems):
        c, direct, _, relay, relay_in, hand = parts(full, sems)
        for t in range(lo, hi):
            for i in range(n_w):
                relay_in(i, t).wait_recv()
                hand(i, t, 2, c).start()
        for i in range(n_w):
            for t in range(lo, hi):
                for k in range(3):
                    hand(i, t, k, 1 - c).wait_recv()
        for i in range(n_w):
            for t in range(lo, hi):
                direct(i, t, 0).wait_send()
                direct(i, t, 1).wait_send()
                relay(i, t).wait_send()
                for k in range(3):
                    hand(i, t, k, c).wait_send()

    n_sem = per * (hi - lo) * n_w
    return _Rider(fulls, [jax.ShapeDtypeStruct((w.R, w.C), BF16) for w in ws],
                  [pltpu.SemaphoreType.DMA((n_sem,)), pltpu.SemaphoreType.DMA((n_sem,))], start, finish,
                  steps=[arrived(t) for t in range(lo, hi)], aliases={i: i for i in range(n_w)})


def _cast_into_full(ws, shards, chip_arr, riders=()):
    sr, sc = ws[0].shard_shape
    assert all(w.shard_shape == (sr, sc) for w in ws)
    tr, tc = _tile(sr, 512), _tile(sc, 2048)
    n_r, n_c = sr // tr, sc // tc

    def place(w):
        if w.kind == "col":
            return pl.BlockSpec((tr, tc), lambda i, j, chip: (i, chip[0] * n_c + j))
        return pl.BlockSpec((tr, tc), lambda i, j, chip: (chip[0] * n_r + i, j))

    def body(*refs):
        for a_ref, o_ref in zip(refs[:len(ws)], refs[len(ws):]):
            o_ref[...] = a_ref[...].astype(BF16)

    return _ride("cast_" + "_".join(w.name for w in ws), body, riders, list(shards), grid=(n_r, n_c),
                 in_specs=[pl.BlockSpec((tr, tc), lambda i, j, chip: (i, j))] * len(ws),
                 out_specs=[place(w) for w in ws], out_shape=[jax.ShapeDtypeStruct((w.R, w.C), BF16) for w in ws],
                 scratch_shapes=[], sem=("parallel", "parallel"), scalars=chip_arr)


def _half_view(w, g):
    return g if w.kind == "col" else g.reshape(N_CHIPS, w.R // N_CHIPS, w.C)


def _px_rider(ws, grads):
    n_w = len(ws)

    def copies(g, got, sems):
        send_sems, recv_sems = sems
        x, y, c, _ = _place()

        def half_all(w, ref, half):
            hr = w.half_rows
            if w.kind == "col":
                return ref.at[pl.ds(half * hr, hr), :]
            return ref.at[:, pl.ds(half * hr, hr), :]

        return [pltpu.make_async_remote_copy(
            src_ref=half_all(w, g[i], 1 - c), dst_ref=got[i], send_sem=send_sems.at[i], recv_sem=recv_sems.at[i],
            device_id=(x, y, 1 - c), device_id_type=MESH) for i, w in enumerate(ws)]

    def start(g, got, sems):
        for cp in copies(g, got, sems):
            cp.start()

    def finish(g, got, sems):
        for cp in copies(g, got, sems):
            cp.wait_recv()
            cp.wait_send()

    def got_shape(w):
        hr = w.half_rows
        return (hr, w.C) if w.kind == "col" else (N_CHIPS, hr, w.C)

    return _Rider([_half_view(w, g) for w, g in zip(ws, grads)],
                  [jax.ShapeDtypeStruct(got_shape(w), BF16) for w in ws],
                  [pltpu.SemaphoreType.DMA((n_w,)), pltpu.SemaphoreType.DMA((n_w,))], start, finish)


def _pair_sum(w, g, got, c_arr):
    hr = w.half_rows
    if w.kind == "col":
        tr, tc = _tile(hr, 512), _tile(w.C, 2048)
        n_r = hr // tr
        grid = (n_r, w.C // tc)
        g_spec = pl.BlockSpec((tr, tc), lambda i, j, c: (c[0] * n_r + i, j))
        o_spec = pl.BlockSpec((tr, tc), lambda i, j, c: (i, j))
    else:
        tr = _tile(hr, 512)
        n_r = hr // tr
        grid = (N_CHIPS, n_r)
        g_spec = pl.BlockSpec((1, tr, w.C), lambda s, i, c: (s, c[0] * n_r + i, 0))
        o_spec = pl.BlockSpec((1, tr, w.C), lambda s, i, c: (s, i, 0))

    def body(c_ref, g_ref, got_ref, out_ref):
        out_ref[...] = (g_ref[...].astype(F32) + got_ref[...].astype(F32)).astype(BF16)

    return _pcall(
        body, name="grad_pair_sum_" + w.name, out_shape=jax.ShapeDtypeStruct(got.shape, BF16),
        grid_spec=pltpu.PrefetchScalarGridSpec(num_scalar_prefetch=1, grid=grid, in_specs=[g_spec, o_spec],
                                               out_specs=o_spec),
        compiler_params=_params(("parallel", "parallel")),
    )(c_arr, _half_view(w, g), got)


def _chip_sum(w, p, q, cc_arr):
    hr, cols = w.half_rows, w.shard_shape[1]
    tr, tc = _tile(hr, 512), _tile(cols, 2048)
    n_r, n_c = hr // tr, cols // tc

    def body(cc_ref, own, q1, q2, q3, out_ref):
        own_v = own[...] if w.kind == "col" else own[0]
        out_ref[...] = ((own_v.astype(F32) + q1[0].astype(F32)) + q2[0].astype(F32)) + q3[0].astype(F32)

    if w.kind == "col":
        own_spec = pl.BlockSpec((tr, tc), lambda i, j, cc: (i, cc[1] * n_c + j))
    else:
        own_spec = pl.BlockSpec((1, tr, tc), lambda i, j, cc: (cc[1], i, j))
    q_specs = [pl.BlockSpec((1, tr, tc), lambda i, j, cc, s=s: ((cc[1] + s) % N_CHIPS, i, j)) for s in (1, 2, 3)]
    return _pcall(
        body, name="grad_chip_sum_" + w.name, out_shape=jax.ShapeDtypeStruct(w.shard_shape, F32),
        grid_spec=pltpu.PrefetchScalarGridSpec(
            num_scalar_prefetch=1, grid=(n_r, n_c), in_specs=[own_spec] + q_specs,
            out_specs=pl.BlockSpec((tr, tc), lambda i, j, cc: (cc[0] * n_r + i, j))),
        compiler_params=_params(("parallel", "parallel")),
    )(cc_arr, p, q, q, q)


_SEM = pl.BlockSpec(memory_space=pltpu.SEMAPHORE)
_HBM = pl.BlockSpec(memory_space=pltpu.HBM)


def _split_copies(kind, ws, p, land, send_sems, recv_sems):
    x, y, c, chips = _place()
    my_chip = 2 * x + y
    pairs = []
    for i, w in enumerate(ws):
        if kind == "pair":
            hr = w.half_rows
            src = p[i].at[pl.ds((1 - c) * hr, hr), :] if w.kind == "col" else p[i].at[:, pl.ds((1 - c) * hr, hr), :]
            cp = pltpu.make_async_remote_copy(src_ref=src, dst_ref=land[i], send_sem=send_sems.at[i],
                                              recv_sem=recv_sems.at[i], device_id=(x, y, 1 - c), device_id_type=MESH)
            pairs.append((cp, cp))
            continue
        for k, chip in enumerate(chips):
            to_chip = 2 * chip[0] + chip[1]
            src = p[i].at[:, pl.ds(to_chip * (w.C // N_CHIPS), w.C // N_CHIPS)] if w.kind == "col" else p[i].at[to_chip]
            kw = dict(send_sem=send_sems.at[3 * i + k], recv_sem=recv_sems.at[3 * i + k], device_id=(*chip, c),
                      device_id_type=MESH)
            pairs.append((pltpu.make_async_remote_copy(src_ref=src, dst_ref=land[i].at[my_chip], **kw),
                          pltpu.make_async_remote_copy(src_ref=src, dst_ref=land[i].at[to_chip], **kw)))
    return pairs


def _split_start(name, kind, ws, arrays):
    n_w = len(ws)
    if kind == "pair":
        arrays = [_half_view(w, g) for w, g in zip(ws, arrays)]
        lands = [lax.empty((w.half_rows, w.C) if w.kind == "col" else (N_CHIPS, w.half_rows, w.C), BF16) for w in ws]
    else:
        lands = [lax.empty((N_CHIPS, w.half_rows, w.shard_shape[1]), BF16) for w in ws]
    n_sem = n_w if kind == "pair" else 3 * n_w

    def body(*refs):
        p, land = refs[:n_w], refs[n_w:2 * n_w]
        for out, _ in _split_copies(kind, ws, p, land, refs[2 * n_w], refs[2 * n_w + 1]):
            out.start()
        refs[-1][...] = jnp.zeros_like(refs[-1])

    arrays = [pltpu.with_memory_space_constraint(a, pltpu.HBM) for a in list(arrays) + lands]
    res = _pcall(
        body, name=name,
        out_shape=(pltpu.SemaphoreType.DMA((n_sem,)), pltpu.SemaphoreType.DMA((n_sem,)),
                   *[pltpu.HBM(a.shape, a.dtype) for a in arrays], jax.ShapeDtypeStruct((SUBLANES, LANES), F32)),
        in_specs=[_HBM] * (2 * n_w),
        out_specs=(_SEM, _SEM, *[_HBM] * (2 * n_w), pl.BlockSpec(memory_space=pltpu.VMEM)),
        input_output_aliases={i: 2 + i for i in range(2 * n_w)},
        compiler_params=pltpu.CompilerParams(has_side_effects=pltpu.SideEffectType.DATAFLOW_SIDE_EFFECTING),
    )(*arrays)
    return (kind, ws, res[0], res[1], list(res[2:2 + n_w]), list(res[2 + n_w:2 + 2 * n_w])), res[-1]


def _split_wait(name, flight, after):
    kind, ws, send_sems, recv_sems, arrays, lands = flight
    n_w = len(ws)

    def body(*refs):
        p, land = refs[:n_w], refs[n_w:2 * n_w]
        for _, cp in _split_copies(kind, ws, p, land, refs[2 * n_w], refs[2 * n_w + 1]):
            cp.wait_send()
            cp.wait_recv()

    res = _pcall(
        body, name=name,
        out_shape=[pltpu.HBM(a.shape, a.dtype) for a in list(arrays) + list(lands)],
        in_specs=[_HBM] * (2 * n_w) + [_SEM, _SEM] + [ANY] * len(after), out_specs=[_HBM] * (2 * n_w),
        input_output_aliases={i: i for i in range(2 * n_w)},
        compiler_params=pltpu.CompilerParams(has_side_effects=pltpu.SideEffectType.DATAFLOW_SIDE_EFFECTING),
    )(*arrays, *lands, send_sems, recv_sems, *after)
    return list(res[:n_w]), list(res[n_w:])


def _sf_rider(ws, grads):
    n_w = len(ws)

    def copy(g, sems, i, half):
        send_sems, recv_sems = sems
        x, y, c, _ = _place()
        h = c if half == "mine" else 1 - c
        reg = ws[i].shard_half(g[i], h)
        return pltpu.make_async_remote_copy(src_ref=reg, dst_ref=reg, send_sem=send_sems.at[i], recv_sem=recv_sems.at[i],
                                            device_id=(x, y, 1 - c), device_id_type=MESH)

    def start(_, g, sems):
        for i in range(n_w):
            copy(g, sems, i, "mine").start()

    def finish(_, g, sems):
        for i in range(n_w):
            copy(g, sems, i, "other").wait_recv()
            copy(g, sems, i, "mine").wait_send()

    return _Rider(grads, [jax.ShapeDtypeStruct(w.shard_shape, F32) for w in ws],
                  [pltpu.SemaphoreType.DMA((n_w,)), pltpu.SemaphoreType.DMA((n_w,))], start, finish,
                  aliases={i: i for i in range(n_w)})


def _adamw_math(w, g, m, v):
    m = ADAM_B1 * m + (1.0 - ADAM_B1) * g
    v = ADAM_B2 * v + (1.0 - ADAM_B2) * (g * g)
    m_hat = m / (1.0 - ADAM_B1 ** ADAM_STEP)
    v_hat = v / (1.0 - ADAM_B2 ** ADAM_STEP)
    delta = -ADAM_LR * (m_hat / (jnp.sqrt(v_hat) + ADAM_EPS) + ADAM_WD * w)
    return delta, m, v


def _adamw(name, w, g, m, v, after=None):
    R, C = w.shape
    tr, tc = _tile(R, 256), _tile(C, 2048)
    behind = [] if after is None else [after]

    def body(w_ref, g_ref, m_ref, v_ref, *rest):
        g_out, d_out, m_out, v_out = rest[len(behind):]
        g = g_ref[...]
        g_out[...] = g
        d_out[...], m_out[...], v_out[...] = _adamw_math(w_ref[...], g, m_ref[...], v_ref[...])

    spec = pl.BlockSpec((tr, tc), lambda i, j: (i, j))
    sh = jax.ShapeDtypeStruct((R, C), F32)
    return _pcall(body, name=name, grid=(R // tr, C // tc), in_specs=[spec] * 4 + [ANY] * len(behind),
                  out_specs=[spec] * 4, out_shape=[sh] * 4, compiler_params=_params(("parallel", "parallel")))(
                      w, g, m, v, *behind)


def _ada_update(sct, dmod_sh, w, m, v, riders=()):
    R, C = w.shape
    tr, tc = _tile(R, 512), _tile(C, 1024)

    def body(s_ref, d_ref, w_ref, m_ref, v_ref, g_out, d_out, m_out, v_out):
        s, d = s_ref[...], d_ref[...]
        g = s[:, 0:1] * d[0:1, :]
        for b in range(1, N_DEV):
            g += s[:, b:b + 1] * d[b:b + 1, :]
        g_out[...] = g
        d_out[...], m_out[...], v_out[...] = _adamw_math(w_ref[...], g, m_ref[...], v_ref[...])

    spec = pl.BlockSpec((tr, tc), lambda i, j: (i, j))
    sh = jax.ShapeDtypeStruct((R, C), F32)
    return _ride(
        "ada_update", body, riders, [sct, dmod_sh, w, m, v], grid=(R // tr, C // tc),
        in_specs=[pl.BlockSpec((tr, N_DEV), lambda i, j: (i, 0)), pl.BlockSpec((N_DEV, tc), lambda i, j: (0, j)),
                  spec, spec, spec],
        out_specs=[spec] * 4, out_shape=[sh] * 4, scratch_shapes=[], sem=("parallel", "parallel"))


def _silu_rows(c_row):
    D = c_row.shape[1]

    def body(c_ref, o_ref):
        cv = c_ref[...]
        o_ref[...] = cv * jax.nn.sigmoid(cv)

    return _pcall(body, name="silu_c", out_shape=jax.ShapeDtypeStruct((1, D), F32))(c_row)


def _pack_partials(parts, widths, total):
    n = len(widths)

    def body(*refs):
        loss_p, out_ref = refs[n], refs[n + 1]
        off = 0
        for ref, wd in zip(refs[:n], widths):
            out_ref[:, off:off + wd] = jnp.sum(ref[...], axis=0)
            off += wd
        loss = jnp.sum(jnp.sum(loss_p[...], axis=0), axis=1, keepdims=True)
        out_ref[:, off:off + LANES] = jnp.broadcast_to(loss, (1, LANES))
        if off + LANES < total:
            out_ref[:, off + LANES:total] = jnp.zeros((1, total - off - LANES), F32)

    return _pcall(body, name="pack_partials", out_shape=jax.ShapeDtypeStruct((1, total), F32))(*parts)


def _small_update(gathered, offsets, params, loss_off):
    n_p = len(params)

    def over_devices(g_ref, off, wd):
        blk = g_ref[:, off:off + wd]
        g = blk[0:1, :]
        for b in range(1, N_DEV):
            g = g + blk[b:b + 1, :]
        return g

    def body(*refs):
        g_ref = refs[0]
        prm = refs[1:1 + 3 * n_p]
        outs = refs[1 + 3 * n_p:]
        outs[4 * n_p][...] = over_devices(g_ref, loss_off, LANES)
        for i, (off, wd) in enumerate(offsets):
            g = over_devices(g_ref, off, wd)
            w, m, v = prm[3 * i][...], prm[3 * i + 1][...], prm[3 * i + 2][...]
            outs[4 * i][...] = g
            outs[4 * i + 1][...], outs[4 * i + 2][...], outs[4 * i + 3][...] = _adamw_math(w, g, m, v)

    flat = [a for t in params for a in t]
    out_shape = [jax.ShapeDtypeStruct(t[0].shape, F32) for t in params for _ in range(4)]
    out_shape.append(jax.ShapeDtypeStruct((1, LANES), F32))
    return _pcall(body, name="small_update", out_shape=out_shape)(gathered, *flat)


def kernel(x, c, w_ada, b_ada, norm1_w, w_in, q_norm_w, k_norm_w, w_pool, pool_scale, w_a_up, w_b_up, w_o, norm2_w, w_ff1, w_ff2, loss_target, m_w_ada, m_b_ada, m_norm1_w, m_w_in, m_q_norm_w, m_k_norm_w, m_w_pool, m_pool_scale, m_w_a_up, m_w_b_up, m_w_o, m_norm2_w, m_w_ff1, m_w_ff2, v_w_ada, v_b_ada, v_norm1_w, v_w_in, v_q_norm_w, v_k_norm_w, v_w_pool, v_pool_scale, v_w_a_up, v_w_b_up, v_w_o, v_norm2_w, v_w_ff1, v_w_ff2):
    _, S, D = x.shape
    PW = D // 2
    H = PW // HEAD_DIM
    cg = PW // N_GROUPS
    IN = w_in.shape[2] * N_CHIPS
    FF = w_ff1.shape[2] * N_CHIPS
    A_COLS = w_ada.shape[2]
    xi, yi, ci = lax.axis_index("x"), lax.axis_index("y"), lax.axis_index("c")
    chip = 2 * xi + yi
    dev = 2 * chip + ci
    c_arr = jnp.reshape(ci, (1,)).astype(jnp.int32)
    x2, tgt = x[0], loss_target[0]

    ws = [_W("w_in", "col", D, IN), _W("w_pool", "row", PW, cg), _W("w_a_up", "col", PW, D),
          _W("w_b_up", "col", PW, D), _W("w_o", "row", D, D), _W("w_ff1", "col", D, FF), _W("w_ff2", "row", FF, D)]
    w32 = [w_in[0], w_pool[0].reshape(cg, cg), w_a_up[0], w_b_up[0], w_o[0], w_ff1[0], w_ff2[0]]
    m32 = [m_w_in[0], m_w_pool[0].reshape(cg, cg), m_w_a_up[0], m_w_b_up[0], m_w_o[0], m_w_ff1[0], m_w_ff2[0]]
    v32 = [v_w_in[0], v_w_pool[0].reshape(cg, cg), v_w_a_up[0], v_w_b_up[0], v_w_o[0], v_w_ff1[0], v_w_ff2[0]]

    W_IN, W_POOL, W_A, W_B, W_O, W_FF1, W_FF2 = ws
    chip_arr = jnp.reshape(chip, (1,)).astype(jnp.int32)
    cc_arr = jnp.stack([ci, chip]).astype(jnp.int32)
    s_in, s_pool, s_a, s_b, s_o = [_cast_into_full([w], [a], chip_arr)[0] for w, a in zip(ws[:5], w32[:5])]
    (s_ff1, s_ff2), ((win_f,),) = _cast_into_full([W_FF1, W_FF2], w32[5:], chip_arr, riders=[_ag_rider([W_IN], [s_in])])

    sc_row = _silu_rows(c)
    sc_all = _dev_allgather("gather_silu_c", sc_row.reshape(SUBLANES, D // SUBLANES)).reshape(N_DEV, D)
    sc16 = jnp.concatenate([sc_all, jnp.zeros_like(sc_all)], axis=0)
    b_cols = lax.dynamic_slice(b_ada, (0, chip * A_COLS), (1, A_COLS))
    (mod_cols,) = _mm("mod_cols", [(sc16, w_ada[0])], M=2 * N_DEV, N=A_COLS, K=D, tm=16, tn=1024, tk=1024,
                      a_pro=lambda a: a.astype(BF16), b_pro=lambda b: b.astype(BF16),
                      extras=[(b_cols, "row", 0)], outs=[_tile_out(F32)], epi=lambda accs, ex: [accs[0] + ex[0]])
    mod_all = _dev_allgather("gather_mod", mod_cols[:N_DEV]).reshape(N_CHIPS, 2, N_DEV, A_COLS)
    mod_row = lax.dynamic_index_in_dim(mod_all[:, 0], dev, axis=1, keepdims=False).reshape(1, N_CHIPS * A_COLS)
    shift1, scale1, gate1, shift2, scale2, gate2 = [mod_row[:, i * D:(i + 1) * D] for i in range(6)]

    WIDE = dict(tm=2048, tn=1024, tk=2048)
    DEEP = dict(tm=1024, tn=1024, tk=2048)
    DEEPER = dict(tm=1024, tn=1024, tk=4096)
    h = _norm_mod("norm1_mod", x2, norm1_w, scale1, shift1)
    (proj,), ((wpool_f, wa_f, wb_f, wo_f),) = _mm(
        "in_proj", [(h, win_f)], M=S, N=IN, K=D, outs=[_tile_out(BF16)], epi=lambda accs, ex: [accs[0]], **WIDE,
        riders=[_ag_rider([W_POOL, W_A, W_B, W_O], [s_pool, s_a, s_b, s_o], n_ch=2)])
    pooled, pa = _pool_fwd(proj, wpool_f, pool_scale, S, PW)
    (att, attf), ((wff1_f,),) = _attn_fwd(proj, q_norm_w, k_norm_w, S, H, PW // HEAD_DIM,
                                          riders=[_ag_rider([W_FF1], [s_ff1])])

    def merge_epi(accs, ex):
        sa, sb = jax.nn.sigmoid(ex[0].astype(F32)), jax.nn.sigmoid(ex[1].astype(F32))
        return [sa * accs[0] + sb * accs[1], accs[0], accs[1]]

    (merged, ya, yb), (ff2_a,) = _mm("branch_up_merge", [(pa, wa_f), (att, wb_f)], M=S, N=D, K=PW,
                                     extras=[(proj, "tile", 4 * PW), (proj, "tile", 4 * PW + D)],
                                     outs=[_tile_out(BF16)] * 3, epi=merge_epi,
                                     riders=[_ag_rider([W_FF2], [s_ff2], n_ch=8, chunks=(0, 2))])
    (x1, o), (ff2_b,) = _mm("out_proj", [(merged, wo_f)], M=S, N=D, K=D, extras=[(x2, "tile", 0), (gate1, "row", 0)],
                            outs=[_tile_out(F32), _tile_out(BF16)], epi=lambda accs, ex: [ex[0] + ex[1] * accs[0], accs[0]],
                            riders=[_ag_rider([W_FF2], ff2_a, n_ch=8, chunks=(2, 4))], tm=2048, tn=512, tk=2048)
    h2 = _norm_mod("norm2_mod", x1, norm2_w, scale2, shift2)
    (rl,), ((wff2_f,),) = _mm("ff1", [(h2, wff1_f)], M=S, N=FF, K=D, outs=[_tile_out(BF16)], **WIDE,
                              epi=lambda accs, ex: [jnp.maximum(accs[0], 0.0)],
                              riders=[_ag_rider([W_FF2], ff2_b, n_ch=8, chunks=(4, 8))])

    def square(a):
        af = a.astype(F32)
        return (af * af).astype(BF16)

    def loss_epi(accs, ex):
        x1_t, tgt_t, g2 = ex
        f = accs[0]
        diff = (x1_t + g2 * f) - tgt_t
        dy = diff * (1.0 / D)
        return [dy, dy * g2, _colsum(dy * f), _colsum(diff * diff)]

    dy, df, dgate2_p, loss_p = _mm("ff2_loss", [(rl, wff2_f)], M=S, N=D, K=FF, a_pro=square, **DEEP,
                                   extras=[(x1, "tile", 0), (tgt, "tile", 0), (gate2, "row", 0)],
                                   outs=[_tile_out(F32), _tile_out(BF16), _COLSUM, _COLSUM], epi=loss_epi)

    tied = []

    def behind(token, a):
        a, token = lax.optimization_barrier((a, token))
        tied.append(token)
        return a

    def pair_sums(group, partials, got):
        return [_pair_sum(w, g, r, c_arr) for w, g, r in zip(group, partials, got)]

    def chip_sums(group, sums, from_chips):
        return [_chip_sum(w, p, q, cc_arr) for w, p, q in zip(group, sums, from_chips)]

    first = lambda accs, ex: [accs[0]]
    gmm = dict(ta=True, outs=[_tile_out(BF16)], epi=first, **WIDE)
    (g_ff2,) = _mm("grad_w_ff2", [(rl, df)], M=FF, N=D, K=S, a_pro=square, ta=True, tm=512, tn=2048, tk=2048,
                   outs=[_tile_out(BF16)], epi=first)
    flight, token = _split_start("pair_w_ff2_start", "pair", [W_FF2], [g_ff2])
    (dz1,) = _mm("d_ff_hidden", [(behind(token, df), wff2_f)], M=S, N=FF, K=D, tb=True, extras=[(rl, "tile", 0)],
                 outs=[_tile_out(BF16)], epi=lambda accs, ex: [accs[0] * (2.0 * ex[0].astype(F32))], **WIDE)
    sum_ff2 = pair_sums([W_FF2], *_split_wait("pair_w_ff2_wait", flight, after=[dz1] + tied))
    chip_ff2, token = _split_start("chip_w_ff2_start", "chip", [W_FF2], sum_ff2)
    (g_ff1,) = _mm("grad_w_ff1", [(behind(token, h2), dz1)], M=D, N=FF, K=S, **gmm)
    flight, token = _split_start("pair_w_ff1_start", "pair", [W_FF1], [g_ff1])
    (dh2,) = _mm("d_h2", [(behind(token, dz1), wff1_f)], M=S, N=D, K=FF, tb=True, outs=[_tile_out(F32)], epi=first,
                 **DEEPER)
    sum_ff1 = pair_sums([W_FF1], *_split_wait("pair_w_ff1_wait", flight, after=[dh2] + tied))
    chip_ff1, token = _split_start("chip_w_ff1_start", "chip", [W_FF1], sum_ff1)
    dx1, dshift2_p, dscale2_p, gn2_p, do, dgate1_p = _norm_mod_bwd("norm2_bwd", behind(token, dh2), x1, dy, norm2_w, scale2,
                                                                   gate_o=(o, gate1))
    (g_wo,) = _mm("grad_w_o", [(merged, do)], M=D, N=D, K=S, **gmm)

    def gate_epi(accs, ex):
        dm = accs[0]
        sa, sb = jax.nn.sigmoid(ex[0].astype(F32)), jax.nn.sigmoid(ex[1].astype(F32))
        ya_t, yb_t = ex[2].astype(F32), ex[3].astype(F32)
        return [dm * sa, dm * sb, dm * ya_t * (sa * (1.0 - sa)), dm * yb_t * (sb * (1.0 - sb))]

    dya, dyb, dga, dgb = _mm("d_merged", [(do, wo_f)], M=S, N=D, K=D, tb=True, tm=1024, tn=512, tk=2048,
                             extras=[(proj, "tile", 4 * PW), (proj, "tile", 4 * PW + D), (ya, "tile", 0), (yb, "tile", 0)],
                             outs=[_tile_out(BF16)] * 4, epi=gate_epi)
    both = lambda accs, ex: [accs[0], accs[1]]
    g_wa, g_wb = _mm("grad_w_up", [(pa, dya), (att, dyb)], M=PW, N=D, K=S, ta=True, outs=[_tile_out(BF16)] * 2, epi=both,
                     **WIDE)
    mid = [W_A, W_B, W_O]
    flight, token = _split_start("pair_mid_start", "pair", mid, [g_wa, g_wb, g_wo])
    dpa, datt = _mm("d_branches", [(dya, wa_f), (behind(token, dyb), wb_f)], M=S, N=PW, K=D, tb=True,
                    outs=[_tile_out(F32), _tile_out(BF16)], epi=both, tm=1024, tn=512, tk=2048)
    sum_mid = pair_sums(mid, *_split_wait("pair_mid_wait", flight, after=[datt] + tied))
    chip_mid, token = _split_start("chip_mid_start", "chip", mid, sum_mid)
    du, g_wpool4, gscale_p = _pool_bwd(dpa, pooled, wpool_f, pool_scale, S, PW)
    dq, dk, dv, gq_p, gk_p = _attn_bwd(proj, behind(token, datt), attf, q_norm_w, k_norm_w, S, H, PW // HEAD_DIM)
    dproj = jnp.concatenate([du, dq, dk, dv, dga, dgb], axis=1)
    early = [W_FF1, W_FF2]
    sum_ff1, q_ff1 = _split_wait("chip_w_ff1_wait", chip_ff1, after=[dq] + tied)
    sum_ff2, q_ff2 = _split_wait("chip_w_ff2_wait", chip_ff2, after=[dq] + tied)
    halves_early = chip_sums(early, sum_ff1 + sum_ff2, q_ff1 + q_ff2)
    (g_win,), (grads_early,) = _mm("grad_w_in", [(h, dproj)], M=D, N=IN, K=S, riders=[_sf_rider(early, halves_early)],
                                   **gmm)
    last = [W_IN, W_POOL]
    g_last = [g_win, g_wpool4.reshape(PW, cg)]
    sum_mid, q_mid = _split_wait("chip_mid_wait", chip_mid, after=[g_win] + tied)
    halves_mid = chip_sums(mid, sum_mid, q_mid)
    (dh,), (got_last, grads_mid) = _mm("d_h", [(dproj, win_f)], M=S, N=D, K=IN, tb=True, outs=[_tile_out(F32)], epi=first,
                                       riders=[_px_rider(last, g_last), _sf_rider(mid, halves_mid)], **DEEPER)
    sum_last = pair_sums(last, g_last, got_last)
    grad_x, dshift1_p, dscale1_p, gn1_p = _norm_mod_bwd("norm1_bwd", dh, x2, dx1, norm1_w, scale1)

    parts = [dshift1_p, dscale1_p, dgate1_p, dshift2_p, dscale2_p, dgate2_p, gn1_p, gn2_p,
             gscale_p.reshape(1, 1, PW), gq_p, gk_p]
    widths = [D] * 8 + [PW, HEAD_DIM, HEAD_DIM]
    used = sum(widths)
    P = -(-(used + LANES) // (SUBLANES * LANES)) * (SUBLANES * LANES)
    packed = _pack_partials(parts + [loss_p], widths, P)
    gathered = _dev_allgather("gather_vector_grads", packed.reshape(SUBLANES, P // SUBLANES)).reshape(N_DEV, P)
    sum_last, gathered = lax.optimization_barrier((sum_last, gathered))
    chip_last, token = _split_start("chip_last_start", "chip", last, sum_last)
    small = [(b_ada, m_b_ada, v_b_ada), (norm1_w, m_norm1_w, v_norm1_w), (norm2_w, m_norm2_w, v_norm2_w),
             (pool_scale, m_pool_scale, v_pool_scale), (q_norm_w, m_q_norm_w, v_q_norm_w),
             (k_norm_w, m_k_norm_w, v_k_norm_w)]
    offsets = [(0, 6 * D), (6 * D, D), (7 * D, D), (8 * D, PW), (8 * D + PW, HEAD_DIM), (8 * D + PW + HEAD_DIM, HEAD_DIM)]
    su = _small_update(gathered, offsets, small, used)
    (g_b, d_b, nm_b, nv_b, g_n1, d_n1, nm_n1, nv_n1, g_n2, d_n2, nm_n2, nv_n2, g_ps, d_ps, nm_ps, nv_ps,
     g_qn, d_qn, nm_qn, nv_qn, g_kn, d_kn, nm_kn, nv_kn, loss_sum) = su
    dmod_sh = lax.dynamic_slice(gathered, (0, chip * A_COLS), (N_DEV, A_COLS))
    dmod_sh, token = lax.optimization_barrier((dmod_sh, token))
    g_ada, d_ada, nm_ada, nv_ada = _ada_update(sc_all.T, dmod_sh, w_ada[0], m_w_ada[0], v_w_ada[0])

    upd_done = [_adamw("adamw_" + w.name, a, g, m, v, after=token)
                for w, a, g, m, v in zip(ws[2:], w32[2:], list(grads_mid) + list(grads_early), m32[2:], v32[2:])]

    sum_last, q_last = _split_wait("chip_last_wait", chip_last, after=[nv_ada] + [u[3] for u in upd_done])
    halves_last = chip_sums(last, sum_last, q_last)
    filled = _run_rider("grad_sibling_fill", _sf_rider(last, halves_last))
    upd = [_adamw("adamw_" + w.name, a, g, m, v) for w, a, g, m, v in zip(ws[:2], w32[:2], filled, m32[:2], v32[:2])]
    upd += upd_done

    loss = (0.5 / D) * loss_sum[0, 0]

    def up(a):
        return a[None]

    def pool4(a):
        return a.reshape(1, N_GROUPS, cg // N_CHIPS, cg)

    (gr_win, d_win, nm_win, nv_win), (gr_wp, d_wp, nm_wp, nv_wp), (gr_wa, d_wa, nm_wa, nv_wa), \
        (gr_wb, d_wb, nm_wb, nv_wb), (gr_wo, d_wo, nm_wo, nv_wo), (gr_f1, d_f1, nm_f1, nv_f1), \
        (gr_f2, d_f2, nm_f2, nv_f2) = upd
    return (
        loss, grad_x[None],
        up(g_ada), g_b, g_n1, up(gr_win), g_qn, g_kn, pool4(gr_wp), g_ps, up(gr_wa), up(gr_wb), up(gr_wo), g_n2,
        up(gr_f1), up(gr_f2),
        up(d_ada), d_b, d_n1, up(d_win), d_qn, d_kn, pool4(d_wp), d_ps, up(d_wa), up(d_wb), up(d_wo), d_n2,
        up(d_f1), up(d_f2),
        up(nm_ada), nm_b, nm_n1, up(nm_win), nm_qn, nm_kn, pool4(nm_wp), nm_ps, up(nm_wa), up(nm_wb), up(nm_wo), nm_n2,
        up(nm_f1), up(nm_f2),
        up(nv_ada), nv_b, nv_n1, up(nv_win), nv_qn, nv_kn, pool4(nv_wp), nv_ps, up(nv_wa), up(nv_wb), up(nv_wo), nv_n2,
        up(nv_f1), up(nv_f2),
    )
```

```python
import functools
import math

import jax
import jax.numpy as jnp
from jax import lax
from jax.experimental import pallas as pl
from jax.experimental.pallas import tpu as pltpu

F32 = jnp.float32
BF16 = jnp.bfloat16
MESH = pl.DeviceIdType.MESH
ANY = pl.BlockSpec(memory_space=pl.ANY)

EPS = 1e-6
HEAD_DIM = 128
LANES, SUBLANES = 128, 8
POOL_WINDOWS = (2, 4, 8, 16)
N_GROUPS = len(POOL_WINDOWS)
assert POOL_WINDOWS == tuple(2 << g for g in range(N_GROUPS))
N_CHIPS = 4
N_DEV = 8
ADAM_LR, ADAM_B1, ADAM_B2, ADAM_EPS, ADAM_WD, ADAM_STEP = 0.001, 0.9, 0.999, 1e-08, 0.01, 10
VMEM_LIMIT_V7X = 56 * 1024 * 1024
ATT_T = 256
ATT_GROUP = 8
POOL_T = 256


def _pcall(body, **kw):
    return pl.pallas_call(body, **kw)


def _params(sem=None):
    return pltpu.CompilerParams(dimension_semantics=sem, vmem_limit_bytes=VMEM_LIMIT_V7X)


def _tile(n, pref):
    if n <= pref:
        return n
    t = pref
    while n % t:
        t //= 2
    return t


class _Rider:
    def __init__(self, arrays, out_shape, sems, start, finish, aliases=None, steps=()):
        self.arrays, self.out_shape, self.sems = list(arrays), list(out_shape), list(sems)
        self.start, self.finish, self.aliases, self.steps = start, finish, aliases or {}, list(steps)


def _ride(name, body, riders, arrays, *, grid, in_specs, out_specs, out_shape, scratch_shapes, sem, scalars=None):
    n_in, n_out, n_scr = len(arrays), len(out_shape), len(scratch_shapes)
    r_arrays = [a for r in riders for a in r.arrays]
    r_outs = [o for r in riders for o in r.out_shape]
    r_sems = [s for r in riders for s in r.sems]
    n_hooks = max([len(r.steps) for r in riders], default=0)
    total = math.prod(grid)
    aliases, off_i, off_o = {}, n_in + (scalars is not None), n_out
    for r in riders:
        for a, o in r.aliases.items():
            aliases[off_i + a] = off_o + o
        off_i += len(r.arrays)
        off_o += len(r.out_shape)

    def full(*refs):
        p = 0
        groups = []
        for n in (n_in, len(r_arrays), n_out, len(r_outs), n_scr, len(r_sems)):
            groups.append(refs[p:p + n])
            p += n
        ins, rin, outs, rout, scr, rsem = groups

        def each(what):
            a = o = s = 0
            for r in riders:
                fn = what(r)
                if fn is not None:
                    fn(rin[a:a + len(r.arrays)], rout[o:o + len(r.out_shape)], rsem[s:s + len(r.sems)])
                a, o, s = a + len(r.arrays), o + len(r.out_shape), s + len(r.sems)

        if riders:
            lin = 0
            for d, g in enumerate(grid):
                lin = lin * g + pl.program_id(d)
            pl.when(lin == 0)(lambda: each(lambda r: r.start))
            for t in range(n_hooks):
                pl.when(lin == min(total - 1, ((t + 1) * total) // n_hooks))(
                    lambda t=t: each(lambda r: r.steps[t] if t < len(r.steps) else None))
        body(*ins, *outs, *scr)
        if riders:
            pl.when(lin == total - 1)(lambda: each(lambda r: r.finish))

    specs = dict(grid=grid, in_specs=list(in_specs) + [ANY] * len(r_arrays),
                 out_specs=list(out_specs) + [ANY] * len(r_outs), scratch_shapes=list(scratch_shapes) + r_sems)
    common = dict(name=name, out_shape=list(out_shape) + r_outs, input_output_aliases=aliases,
                  compiler_params=_params(("arbitrary",) * len(grid) if riders else sem))
    if scalars is None:
        res = _pcall(full, **specs, **common)(*arrays, *r_arrays)
    else:
        res = _pcall(lambda _, *refs: full(*refs), **common,
                     grid_spec=pltpu.PrefetchScalarGridSpec(num_scalar_prefetch=1, **specs))(scalars, *arrays, *r_arrays)
    if not riders:
        return res
    main, rest, per = res[:n_out], res[n_out:], []
    for r in riders:
        per.append(rest[:len(r.out_shape)])
        rest = rest[len(r.out_shape):]
    return main, per


def _run_rider(name, rider):
    def body(*refs):
        n_a, n_o = len(rider.arrays), len(rider.out_shape)
        ins, outs, sems = refs[:n_a], refs[n_a:n_a + n_o], refs[n_a + n_o:]
        for fn in [rider.start] + rider.steps + [rider.finish]:
            fn(ins, outs, sems)

    return _pcall(body, name=name, out_shape=rider.out_shape, in_specs=[ANY] * len(rider.arrays),
                  out_specs=[ANY] * len(rider.out_shape), scratch_shapes=rider.sems,
                  input_output_aliases=rider.aliases)(*rider.arrays)


def _mm(name, pairs, *, M, N, K, ta=False, tb=False, tm=512, tn=1024, tk=1024,
        a_pro=None, b_pro=None, extras=(), outs, epi, riders=()):
    tm, tn, tk = _tile(M, tm), _tile(N, tn), _tile(K, tk)
    n_i, n_j, n_k = M // tm, N // tn, K // tk
    n_p, n_e = len(pairs), len(extras)
    arrays, in_specs = [], []
    for a, _ in pairs:
        arrays.append(a)
        in_specs.append(pl.BlockSpec((tk, tm), lambda i, j, k: (k, i)) if ta
                        else pl.BlockSpec((tm, tk), lambda i, j, k: (i, k)))
    for _, b in pairs:
        arrays.append(b)
        in_specs.append(pl.BlockSpec((tn, tk), lambda i, j, k: (j, k)) if tb
                        else pl.BlockSpec((tk, tn), lambda i, j, k: (k, j)))
    for arr, kind, off in extras:
        ob = off // tn
        assert off % tn == 0
        arrays.append(arr)
        if kind == "tile":
            in_specs.append(pl.BlockSpec((tm, tn), lambda i, j, k, ob=ob: (i, j + ob)))
        else:
            in_specs.append(pl.BlockSpec((1, tn), lambda i, j, k, ob=ob: (0, j + ob)))
    out_shape, out_specs = [], []
    for o in outs:
        if o["kind"] == "tile":
            out_shape.append(jax.ShapeDtypeStruct((M, N), o["dtype"]))
            out_specs.append(pl.BlockSpec((tm, tn), lambda i, j, k: (i, j)))
        else:
            out_shape.append(jax.ShapeDtypeStruct((n_i, 1, N), F32))
            out_specs.append(pl.BlockSpec((1, 1, tn), lambda i, j, k: (i, 0, j)))
    dims = (((0 if ta else 1,), (1 if tb else 0,)), ((), ()))

    def body(*refs):
        a_refs, b_refs = refs[:n_p], refs[n_p:2 * n_p]
        e_refs = refs[2 * n_p:2 * n_p + n_e]
        o_refs = refs[2 * n_p + n_e:2 * n_p + n_e + len(outs)]
        acc_refs = refs[2 * n_p + n_e + len(outs):]

        def product(p):
            a, b = a_refs[p][...], b_refs[p][...]
            if a_pro is not None:
                a = a_pro(a)
            if b_pro is not None:
                b = b_pro(b)
            return lax.dot_general(a, b, dims, preferred_element_type=F32)

        def write(accs):
            vals = epi(accs, [e[...] for e in e_refs])
            for o, o_ref, val in zip(outs, o_refs, vals):
                if o["kind"] == "tile":
                    o_ref[...] = val.astype(o_ref.dtype)
                else:
                    o_ref[0] = val

        if n_k == 1:
            write([product(p) for p in range(n_p)])
            return
        k = pl.program_id(2)

        @pl.when(k == 0)
        def _():
            for acc in acc_refs:
                acc[...] = jnp.zeros_like(acc)

        for p in range(n_p):
            acc_refs[p][...] += product(p)

        pl.when(k == n_k - 1)(lambda: write([acc[...] for acc in acc_refs]))

    return _ride(name, body, riders, arrays, grid=(n_i, n_j, n_k), in_specs=in_specs, out_specs=out_specs,
                 out_shape=out_shape, scratch_shapes=[pltpu.VMEM((tm, tn), F32) for _ in pairs] if n_k > 1 else [],
                 sem=("parallel", "parallel", "arbitrary"))


def _tile_out(dtype):
    return {"kind": "tile", "dtype": dtype}


_COLSUM = {"kind": "colsum"}


def _colsum(v):
    return jnp.sum(v, axis=0, keepdims=True)


def _norm_mod(name, x, norm_w, scale, shift):
    S, D = x.shape
    tr = _tile(S, 256)

    def body(x_ref, nw_ref, sc_ref, sh_ref, h_ref):
        xv = x_ref[...]
        r = lax.rsqrt(jnp.mean(xv * xv, axis=-1, keepdims=True) + EPS)
        h_ref[...] = ((xv * r * nw_ref[...]) * (1.0 + sc_ref[...]) + sh_ref[...]).astype(BF16)

    row = pl.BlockSpec((1, D), lambda i: (0, 0))
    til = pl.BlockSpec((tr, D), lambda i: (i, 0))
    return _pcall(body, name=name, grid=(S // tr,), in_specs=[til, row, row, row], out_specs=til,
                  out_shape=jax.ShapeDtypeStruct((S, D), BF16), compiler_params=_params(("parallel",)))(
                      x, norm_w, scale, shift)


def _norm_mod_bwd(name, dh, x, dres, norm_w, scale, gate_o=None):
    S, D = x.shape
    tr = _tile(S, 256)
    n_r = S // tr
    with_gate = gate_o is not None

    def body(*refs):
        if with_gate:
            dh_ref, x_ref, dres_ref, nw_ref, sc_ref, o_ref, g_ref, dx_ref, p1, p2, p3, do_ref, p4 = refs
        else:
            dh_ref, x_ref, dres_ref, nw_ref, sc_ref, dx_ref, p1, p2, p3 = refs
        dhv, xv, nw = dh_ref[...], x_ref[...], nw_ref[...]
        r = lax.rsqrt(jnp.mean(xv * xv, axis=-1, keepdims=True) + EPS)
        xh = xv * r
        p1[0] = _colsum(dhv)
        p2[0] = _colsum(dhv * (xh * nw))
        dn = dhv * (1.0 + sc_ref[...])
        p3[0] = _colsum(dn * xh)
        dxh = dn * nw
        dx = dres_ref[...] + r * (dxh - xh * jnp.mean(dxh * xh, axis=-1, keepdims=True))
        dx_ref[...] = dx
        if with_gate:
            do_ref[...] = (dx * g_ref[...]).astype(BF16)
            p4[0] = _colsum(dx * o_ref[...].astype(F32))

    row = pl.BlockSpec((1, D), lambda i: (0, 0))
    til = pl.BlockSpec((tr, D), lambda i: (i, 0))
    part = pl.BlockSpec((1, 1, D), lambda i: (i, 0, 0))
    part_shape = jax.ShapeDtypeStruct((n_r, 1, D), F32)
    in_specs = [til, til, til, row, row]
    arrays = [dh, x, dres, norm_w, scale]
    out_specs = [til, part, part, part]
    out_shape = [jax.ShapeDtypeStruct((S, D), F32), part_shape, part_shape, part_shape]
    if with_gate:
        in_specs += [til, row]
        arrays += list(gate_o)
        out_specs += [til, part]
        out_shape += [jax.ShapeDtypeStruct((S, D), BF16), part_shape]
    return _pcall(body, name=name, grid=(n_r,), in_specs=in_specs, out_specs=out_specs, out_shape=out_shape,
                  compiler_params=_params(("parallel",)))(*arrays)


def _pool_w_specs(rows, cg):
    return [pl.BlockSpec((rows, cg), lambda g, j=j: (N_GROUPS * j + g, 0)) for j in range(N_CHIPS)]


def _pool_fwd(proj, wp_full, pool_scale, S, PW):
    cg = PW // N_GROUPS
    rows = cg // N_CHIPS
    T = _tile(S, POOL_T)
    n_t = S // T

    def body(u_ref, w0, w1, w2, w3, ps_ref, pooled_ref, pa_ref):
        g = pl.program_id(0)
        win = jnp.left_shift(2, g)
        w = jnp.concatenate([w0[...], w1[...], w2[...], w3[...]], axis=0)
        t_i = lax.broadcasted_iota(jnp.int32, (T, T), 0)
        j_i = lax.broadcasted_iota(jnp.int32, (T, T), 1)
        b_cur = ((j_i <= t_i) & (j_i > t_i - win)).astype(BF16)
        b_prev = (j_i - T > t_i - win).astype(BF16)
        row = lax.broadcasted_iota(jnp.int32, (T, 1), 0)
        for r in range(n_t):
            cur = u_ref[r * T:(r + 1) * T, :]
            ws = jnp.dot(b_cur, cur, preferred_element_type=F32)
            if r > 0:
                ws += jnp.dot(b_prev, u_ref[(r - 1) * T:r * T, :], preferred_element_type=F32)
            count = jnp.minimum(row + (r * T + 1), win).astype(F32)
            pooled = (ws / count - cur.astype(F32)).astype(BF16)
            pooled_ref[r * T:(r + 1) * T, :] = pooled
            mixed = jnp.dot(pooled, w, preferred_element_type=F32)
            pa_ref[r * T:(r + 1) * T, :] = (mixed * ps_ref[...]).astype(BF16)

    col = pl.BlockSpec((S, cg), lambda g: (0, g))
    return _pcall(
        body, name="pool_fwd", grid=(N_GROUPS,),
        in_specs=[col] + _pool_w_specs(rows, cg) + [pl.BlockSpec((1, cg), lambda g: (0, g))],
        out_specs=[col, col],
        out_shape=[jax.ShapeDtypeStruct((S, PW), BF16), jax.ShapeDtypeStruct((S, PW), BF16)],
        compiler_params=_params(("parallel",)),
    )(proj, wp_full, wp_full, wp_full, wp_full, pool_scale)


def _pool_bwd(dpa, pooled, wp_full, pool_scale, S, PW):
    cg = PW // N_GROUPS
    rows = cg // N_CHIPS
    T = _tile(S, POOL_T)
    n_t = S // T

    def body(dpa_ref, pooled_ref, w0, w1, w2, w3, ps_ref, du_ref, gw_ref, gs_ref, dp_s, dpc_s, dmx_s):
        g = pl.program_id(0)
        win = jnp.left_shift(2, g)
        w = jnp.concatenate([w0[...], w1[...], w2[...], w3[...]], axis=0)
        row = lax.broadcasted_iota(jnp.int32, (T, 1), 0)
        gs = jnp.zeros((1, cg), F32)
        for r in range(n_t):
            sl = slice(r * T, (r + 1) * T)
            mixed = jnp.dot(pooled_ref[sl, :], w, preferred_element_type=F32)
            dpa_t = dpa_ref[sl, :]
            gs += _colsum(dpa_t * mixed)
            dmx = (dpa_t * ps_ref[...]).astype(BF16)
            dmx_s[sl, :] = dmx
            dpo = lax.dot_general(dmx, w, (((1,), (1,)), ((), ())), preferred_element_type=F32)
            dp_s[sl, :] = dpo
            count = jnp.minimum(row + (r * T + 1), win).astype(F32)
            dpc_s[sl, :] = (dpo / count).astype(BF16)
        gs_ref[...] = gs
        gw = lax.dot_general(pooled_ref[...], dmx_s[...], (((0,), (0,)), ((), ())), preferred_element_type=F32)
        for j in range(N_CHIPS):
            gw_ref[j, 0] = gw[j * rows:(j + 1) * rows, :].astype(BF16)
        j_i = lax.broadcasted_iota(jnp.int32, (T, T), 0)
        t_i = lax.broadcasted_iota(jnp.int32, (T, T), 1)
        b_cur = ((t_i >= j_i) & (t_i < j_i + win)).astype(BF16)
        b_next = (t_i + T < j_i + win).astype(BF16)
        for r in range(n_t):
            sl = slice(r * T, (r + 1) * T)
            acc = jnp.dot(b_cur, dpc_s[sl, :], preferred_element_type=F32)
            if r + 1 < n_t:
                acc += jnp.dot(b_next, dpc_s[(r + 1) * T:(r + 2) * T, :], preferred_element_type=F32)
            du_ref[sl, :] = (acc - dp_s[sl, :]).astype(BF16)

    col = pl.BlockSpec((S, cg), lambda g: (0, g))
    return _pcall(
        body, name="pool_bwd", grid=(N_GROUPS,),
        in_specs=[col, col] + _pool_w_specs(rows, cg) + [pl.BlockSpec((1, cg), lambda g: (0, g))],
        out_specs=[col, pl.BlockSpec((N_CHIPS, 1, rows, cg), lambda g: (0, g, 0, 0)),
                   pl.BlockSpec((1, cg), lambda g: (0, g))],
        out_shape=[jax.ShapeDtypeStruct((S, PW), BF16),
                   jax.ShapeDtypeStruct((N_CHIPS, N_GROUPS, rows, cg), BF16),
                   jax.ShapeDtypeStruct((1, PW), F32)],
        scratch_shapes=[pltpu.VMEM((S, cg), F32), pltpu.VMEM((S, cg), BF16), pltpu.VMEM((S, cg), BF16)],
        compiler_params=_params(("parallel",)),
    )(dpa, pooled, wp_full, wp_full, wp_full, wp_full, pool_scale)


_NT = (((1,), (1,)), ((), ()))
_TN = (((0,), (0,)), ((), ()))


def _split_dot(v, tri):
    hi = v.astype(BF16)
    lo = (v - hi.astype(F32)).astype(BF16)
    return jnp.dot(hi, tri, preferred_element_type=F32) + jnp.dot(lo, tri, preferred_element_type=F32)


LOG2E = 1.4426950408889634
QK_SCALE = 1.0 / math.sqrt(HEAD_DIM)


def _sb_scores(q2_i, k_j, tri_l, masked):
    tq, tk = q2_i.shape[0], k_j.shape[0]
    s = lax.dot_general(q2_i, k_j, _NT, preferred_element_type=F32)
    lp = jnp.log(1.0 + jnp.exp2(-jnp.abs(s))) * LOG2E
    lb = jnp.minimum(s, 0.0) - lp
    l = lb - s
    mask = None
    if masked:
        mask = lax.broadcasted_iota(jnp.int32, (tq, tk), 0) > lax.broadcasted_iota(jnp.int32, (tq, tk), 1)
        l = jnp.where(mask, l, 0.0)
    return l, lb, lb + _split_dot(l, tri_l), mask


def _sb_weights(t, carry_l, mask):
    a = jnp.exp2(t + carry_l)
    return a if mask is None else jnp.where(mask, a, 0.0)


def _rowsum(v):
    return jnp.sum(v, axis=1, keepdims=True)


def _qk_norm(x_ref, w_ref):
    xv = x_ref[...].astype(F32)
    r = lax.rsqrt(jnp.mean(xv * xv, axis=-1, keepdims=True) + EPS)
    return xv * r, r


def _attn_fwd(proj, q_norm_w, k_norm_w, S, H, q_off, riders=()):
    t = _tile(S, ATT_T)
    n_q = S // t

    def body(q_ref, k_ref, v_ref, qw_ref, kw_ref, att_ref, attf_ref, qn_s, kn_s):
        qh, _ = _qk_norm(q_ref, qw_ref)
        qn_s[...] = (qh * qw_ref[...] * (QK_SCALE * LOG2E)).astype(BF16)
        kh, _ = _qk_norm(k_ref, kw_ref)
        kn_s[...] = (kh * kw_ref[...]).astype(BF16)
        tri_l = (lax.broadcasted_iota(jnp.int32, (t, t), 0) > lax.broadcasted_iota(jnp.int32, (t, t), 1)).astype(BF16)

        def rows(j):
            return pl.ds(pl.multiple_of(j * t, t), t)

        def q_step(i, _):
            q_i = qn_s[rows(i), :]

            def av(a, j):
                return jnp.dot(a.astype(BF16), v_ref[rows(j), :], preferred_element_type=F32)

            l, _, tt, mask = _sb_scores(q_i, kn_s[rows(i), :], tri_l, True)
            acc = av(_sb_weights(tt, 0.0, mask), i)
            carry = _rowsum(l)

            def single(_, c):
                carry, acc = c
                l, _, tt, _ = _sb_scores(q_i, kn_s[rows(i - 1), :], tri_l, False)
                return carry + _rowsum(l), acc + av(_sb_weights(tt, carry, None), i - 1)

            carry, acc = lax.fori_loop(0, i % 2, single, (carry, acc))
            top = i - 1 - i % 2

            def pair(p, c):
                carry, acc = c
                j0 = top - 2 * p
                l0, _, t0, _ = _sb_scores(q_i, kn_s[rows(j0), :], tri_l, False)
                l1, _, t1, _ = _sb_scores(q_i, kn_s[rows(j0 - 1), :], tri_l, False)
                mid = carry + _rowsum(l0)
                acc = acc + av(_sb_weights(t0, carry, None), j0) + av(_sb_weights(t1, mid, None), j0 - 1)
                return mid + _rowsum(l1), acc

            _, acc = lax.fori_loop(0, i // 2, pair, (carry, acc))
            att_ref[rows(i), :] = acc.astype(BF16)
            attf_ref[rows(i), :] = acc
            return 0

        lax.fori_loop(0, n_q, q_step, 0)

    def col(off):
        return pl.BlockSpec((S, HEAD_DIM), lambda h, off=off: (0, off + h))

    wspec = pl.BlockSpec((1, HEAD_DIM), lambda h: (0, 0))
    return _ride(
        "attn_fwd", body, riders, [proj, proj, proj, q_norm_w, k_norm_w], grid=(H,),
        in_specs=[col(q_off), col(q_off + H), col(q_off + 2 * H), wspec, wspec],
        out_specs=[col(0), col(0)],
        out_shape=[jax.ShapeDtypeStruct((S, H * HEAD_DIM), BF16), jax.ShapeDtypeStruct((S, H * HEAD_DIM), F32)],
        scratch_shapes=[pltpu.VMEM((S, HEAD_DIM), BF16), pltpu.VMEM((S, HEAD_DIM), BF16)],
        sem=("parallel",))


def _attn_bwd(proj, datt, attf, q_norm_w, k_norm_w, S, H, q_off, riders=()):
    t = _tile(S, ATT_T)
    n_q = S // t

    def body(q_ref, k_ref, v_ref, do_ref, o_ref, qw_ref, kw_ref, dq_ref, dk_ref, dv_ref, gq_ref, gk_ref,
             qn_s, kn_s, qz_s, kz_s, dk_s, dv_s, gq_s):
        qw, kw = qw_ref[...], kw_ref[...]
        qh, _ = _qk_norm(q_ref, qw_ref)
        qn_s[...] = (qh * qw * (QK_SCALE * LOG2E)).astype(BF16)
        qz_s[...] = (qh * qw * QK_SCALE).astype(BF16)
        kh, _ = _qk_norm(k_ref, kw_ref)
        kn_s[...] = (kh * kw).astype(BF16)
        kz_s[...] = (kh * kw * QK_SCALE).astype(BF16)
        dk_s[...] = jnp.zeros_like(dk_s)
        dv_s[...] = jnp.zeros_like(dv_s)
        gq_s[...] = jnp.zeros_like(gq_s)
        r_i = lax.broadcasted_iota(jnp.int32, (t, t), 0)
        c_i = lax.broadcasted_iota(jnp.int32, (t, t), 1)
        tri_l = (r_i > c_i).astype(BF16)
        tri_e = (r_i >= c_i).astype(BF16)

        def rows(j):
            return pl.ds(pl.multiple_of(j * t, t), t)

        def q_step(i, _):
            q_i = qn_s[rows(i), :]
            do_i = do_ref[rows(i), :]
            d_i = _rowsum(do_i.astype(F32) * o_ref[rows(i), :])

            def scores(j, masked):
                l, lb, tt, mask = _sb_scores(q_i, kn_s[rows(j), :], tri_l, masked)
                da = lax.dot_general(do_i, v_ref[rows(j), :], _NT, preferred_element_type=F32)
                return l, lb, tt, mask, da

            def grads(j, sc, carry_l, carry_e, dq_acc):
                l, lb, tt, mask, da = sc
                a_bf = _sb_weights(tt, carry_l, mask).astype(BF16)
                e = da * a_bf.astype(F32)
                p = (d_i - carry_e) - _split_dot(e, tri_e)
                dz = e - jnp.exp2(lb) * (e + p)
                if mask is not None:
                    dz = jnp.where(mask, dz, 0.0)
                dz = dz.astype(BF16)
                dk_s[rows(j), :] += lax.dot_general(dz, qz_s[rows(i), :], _TN, preferred_element_type=F32)
                dv_s[rows(j), :] += lax.dot_general(a_bf, do_i, _TN, preferred_element_type=F32)
                return (carry_l + _rowsum(l), carry_e + _rowsum(e),
                        dq_acc + jnp.dot(dz, kz_s[rows(j), :], preferred_element_type=F32))

            zero = jnp.zeros((t, 1), F32)
            first = (zero, zero, jnp.zeros((t, HEAD_DIM), F32))

            def group(js, diagonal_first, c):
                scs = [scores(j, diagonal_first and n == 0) for n, j in enumerate(js)]
                for j, sc in zip(js, scs):
                    c = grads(j, sc, *c)
                return c

            n_first = i % ATT_GROUP
            c = lax.switch(n_first, [functools.partial(group, [i - u for u in range(n + 1)], True, first)
                                     for n in range(ATT_GROUP)])
            top = i - 1 - n_first

            def whole(p, c):
                j0 = top - ATT_GROUP * p
                return group([j0 - u for u in range(ATT_GROUP)], False, c)

            _, _, dqn = lax.fori_loop(0, (i - n_first) // ATT_GROUP, whole, c)
            qv = q_ref[rows(i), :].astype(F32)
            r = lax.rsqrt(jnp.mean(qv * qv, axis=-1, keepdims=True) + EPS)
            xh = qv * r
            gq_s[...] += _colsum(dqn * xh)
            dxh = dqn * qw
            dq_ref[rows(i), :] = (r * (dxh - xh * jnp.mean(dxh * xh, axis=-1, keepdims=True))).astype(BF16)
            return 0

        lax.fori_loop(0, n_q, q_step, 0)
        gq_ref[0] = gq_s[...]
        kh, rk = _qk_norm(k_ref, kw_ref)
        dkn = dk_s[...]
        gk_ref[0] = _colsum(dkn * kh)
        dxh = dkn * kw
        dk_ref[...] = (rk * (dxh - kh * jnp.mean(dxh * kh, axis=-1, keepdims=True))).astype(BF16)
        dv_ref[...] = dv_s[...].astype(BF16)

    def col(off):
        return pl.BlockSpec((S, HEAD_DIM), lambda h, off=off: (0, off + h))

    wspec = pl.BlockSpec((1, HEAD_DIM), lambda h: (0, 0))
    gspec = pl.BlockSpec((1, 1, HEAD_DIM), lambda h: (h, 0, 0))
    act = jax.ShapeDtypeStruct((S, H * HEAD_DIM), BF16)
    gsh = jax.ShapeDtypeStruct((H, 1, HEAD_DIM), F32)
    return _ride(
        "attn_bwd", body, riders, [proj, proj, proj, datt, attf, q_norm_w, k_norm_w], grid=(H,),
        in_specs=[col(q_off), col(q_off + H), col(q_off + 2 * H), col(0), col(0), wspec, wspec],
        out_specs=[col(0), col(0), col(0), gspec, gspec],
        out_shape=[act, act, act, gsh, gsh],
        scratch_shapes=[pltpu.VMEM((S, HEAD_DIM), BF16)] * 4 + [pltpu.VMEM((S, HEAD_DIM), F32)] * 2
        + [pltpu.VMEM((1, HEAD_DIM), F32)],
        sem=("parallel",))


def _place():
    x, y, c = lax.axis_index("x"), lax.axis_index("y"), lax.axis_index("c")
    chips = [(1 - x, y), (x, 1 - y), (1 - x, 1 - y)]
    return x, y, c, chips


def _dev_allgather(name, v):
    m_per, n = v.shape

    def body(x_ref, out_ref, send_sems, recv_sems, local_sem):
        x, y, c, _ = _place()
        me = (x, y, c)

        def rows(px, py, pc):
            return out_ref.at[pl.ds((4 * px + 2 * py + pc) * m_per, m_per), :]

        def peer(r):
            return tuple(1 - b if (r >> s) & 1 else b for b, s in zip(me, (2, 1, 0)))

        def copy(r, block, to, src=None):
            return pltpu.make_async_remote_copy(
                src_ref=rows(*block) if src is None else src, dst_ref=rows(*block),
                send_sem=send_sems.at[r - 1], recv_sem=recv_sems.at[r - 1], device_id=to, device_id_type=MESH)

        mine = pltpu.make_async_copy(x_ref, rows(*me), local_sem)
        mine.start()
        sends = [copy(r, me, peer(r), src=x_ref) for r in range(1, N_DEV)]
        for cp in sends:
            cp.start()
        for r in range(1, N_DEV):
            copy(r, peer(r), me).wait_recv()
        for cp in sends:
            cp.wait_send()
        mine.wait()

    return _pcall(
        body, name=name, out_shape=jax.ShapeDtypeStruct((N_DEV * m_per, n), v.dtype),
        in_specs=[pl.BlockSpec(memory_space=pltpu.VMEM)], out_specs=pl.BlockSpec(memory_space=pltpu.VMEM),
        scratch_shapes=[pltpu.SemaphoreType.DMA((7,)), pltpu.SemaphoreType.DMA((7,)), pltpu.SemaphoreType.DMA],
        compiler_params=pltpu.CompilerParams(vmem_limit_bytes=VMEM_LIMIT_V7X),
    )(v)


class _W:
    def __init__(self, name, kind, R, C):
        self.name, self.kind, self.R, self.C = name, kind, R, C

    @property
    def shard_shape(self):
        return (self.R, self.C // N_CHIPS) if self.kind == "col" else (self.R // N_CHIPS, self.C)

    @property
    def half_rows(self):
        return self.shard_shape[0] // 2

    def shard_half(self, ref, half):
        return ref.at[pl.ds(half * self.half_rows, self.half_rows), :]

    def region(self, full_ref, chip, half):
        hr = self.half_rows
        if self.kind == "col":
            cw = self.C // N_CHIPS
            return full_ref.at[pl.ds(half * hr, hr), pl.ds(chip * cw, cw)]
        return full_ref.at[pl.ds(chip * (2 * hr) + half * hr, hr), :]


def _ag_rider(ws, fulls, n_ch=4, chunks=None):
    n_w = len(ws)
    lo, hi = chunks or (0, n_ch)
    per = 6

    def parts(full, sems):
        send_sems, recv_sems = sems
        x, y, c, _ = _place()
        xn, yn, dg = (1 - x, y), (x, 1 - y), (1 - x, 1 - y)
        via = (x + (1 - c) * (1 - 2 * x), y + c * (1 - 2 * y))
        to = (x + c * (1 - 2 * x), y + (1 - c) * (1 - 2 * y))

        def reg(i, chip, half, t):
            nr = ws[i].half_rows // n_ch
            return ws[i].region(full[i], 2 * chip[0] + chip[1], half).at[pl.ds(t * nr, nr), :]

        def copy(r, i, t, k, dev):
            s = (i * (hi - lo) + t - lo) * per + k
            return pltpu.make_async_remote_copy(src_ref=r, dst_ref=r, send_sem=send_sems.at[s],
                                                recv_sem=recv_sems.at[s], device_id=dev, device_id_type=MESH)

        def direct(i, t, k):
            return copy(reg(i, (x, y), c, t), i, t, k, (*(via, to)[k], c))

        def direct_in(i, t, k):
            return copy(reg(i, (via, to)[k], c, t), i, t, k, (*(via, to)[k], c))

        def relay(i, t):
            return copy(reg(i, via, c, t), i, t, 2, (*to, c))

        def relay_in(i, t):
            return copy(reg(i, dg, c, t), i, t, 2, (*to, c))

        def hand(i, t, k, half):
            return copy(reg(i, (xn, yn, dg)[k], half, t), i, t, 3 + k, (x, y, 1 - c))

        return c, direct, direct_in, relay, relay_in, hand

    def start(_, full, sems):
        _, direct, _, _, _, _ = parts(full, sems)
        for t in range(lo, hi):
            for i in range(n_w):
                direct(i, t, 0).start()
                direct(i, t, 1).start()

    def arrived(t):
        def step(_, full, sems):
            c, _, direct_in, relay, relay_in, hand = parts(full, sems)
            for i in range(n_w):
                direct_in(i, t, 0).wait_recv()
                direct_in(i, t, 1).wait_recv()
                relay(i, t).start()
                hand(i, t, 0, c).start()
                hand(i, t, 1, c).start()
        return step

    def finish(_, full, sems):
        c, direct, _, relay, relay_in, hand = parts(full, sems)
        for t in range(lo, hi):
            for i in range(n_w):
                relay_in(i, t).wait_recv()
                hand(i, t, 2, c).start()
        for i in range(n_w):
            for t in range(lo, hi):
                for k in range(3):
                    hand(i, t, k, 1 - c).wait_recv()
        for i in range(n_w):
            for t in range(lo, hi):
                direct(i, t, 0).wait_send()
                direct(i, t, 1).wait_send()
                relay(i, t).wait_send()
                for k in range(3):
                    hand(i, t, k, c).wait_send()

    n_sem = per * (hi - lo) * n_w
    return _Rider(fulls, [jax.ShapeDtypeStruct((w.R, w.C), BF16) for w in ws],
                  [pltpu.SemaphoreType.DMA((n_sem,)), pltpu.SemaphoreType.DMA((n_sem,))], start, finish,
                  steps=[arrived(t) for t in range(lo, hi)], aliases={i: i for i in range(n_w)})


def _cast_into_full(ws, shards, chip_arr, riders=()):
    sr, sc = ws[0].shard_shape
    assert all(w.shard_shape == (sr, sc) for w in ws)
    tr, tc = _tile(sr, 512), _tile(sc, 2048)
    n_r, n_c = sr // tr, sc // tc

    def place(w):
        if w.kind == "col":
            return pl.BlockSpec((tr, tc), lambda i, j, chip: (i, chip[0] * n_c + j))
        return pl.BlockSpec((tr, tc), lambda i, j, chip: (chip[0] * n_r + i, j))

    def body(*refs):
        for a_ref, o_ref in zip(refs[:len(ws)], refs[len(ws):]):
            o_ref[...] = a_ref[...].astype(BF16)

    return _ride("cast_" + "_".join(w.name for w in ws), body, riders, list(shards), grid=(n_r, n_c),
                 in_specs=[pl.BlockSpec((tr, tc), lambda i, j, chip: (i, j))] * len(ws),
                 out_specs=[place(w) for w in ws], out_shape=[jax.ShapeDtypeStruct((w.R, w.C), BF16) for w in ws],
                 scratch_shapes=[], sem=("parallel", "parallel"), scalars=chip_arr)


def _half_view(w, g):
    return g if w.kind == "col" else g.reshape(N_CHIPS, w.R // N_CHIPS, w.C)


def _px_rider(ws, grads):
    n_w = len(ws)

    def copies(g, got, sems):
        send_sems, recv_sems = sems
        x, y, c, _ = _place()

        def half_all(w, ref, half):
            hr = w.half_rows
            if w.kind == "col":
                return ref.at[pl.ds(half * hr, hr), :]
            return ref.at[:, pl.ds(half * hr, hr), :]

        return [pltpu.make_async_remote_copy(
            src_ref=half_all(w, g[i], 1 - c), dst_ref=got[i], send_sem=send_sems.at[i], recv_sem=recv_sems.at[i],
            device_id=(x, y, 1 - c), device_id_type=MESH) for i, w in enumerate(ws)]

    def start(g, got, sems):
        for cp in copies(g, got, sems):
            cp.start()

    def finish(g, got, sems):
        for cp in copies(g, got, sems):
            cp.wait_recv()
            cp.wait_send()

    def got_shape(w):
        hr = w.half_rows
        return (hr, w.C) if w.kind == "col" else (N_CHIPS, hr, w.C)

    return _Rider([_half_view(w, g) for w, g in zip(ws, grads)],
                  [jax.ShapeDtypeStruct(got_shape(w), BF16) for w in ws],
                  [pltpu.SemaphoreType.DMA((n_w,)), pltpu.SemaphoreType.DMA((n_w,))], start, finish)


def _pair_sum(w, g, got, c_arr):
    hr = w.half_rows
    if w.kind == "col":
        tr, tc = _tile(hr, 512), _tile(w.C, 2048)
        n_r = hr // tr
        grid = (n_r, w.C // tc)
        g_spec = pl.BlockSpec((tr, tc), lambda i, j, c: (c[0] * n_r + i, j))
        o_spec = pl.BlockSpec((tr, tc), lambda i, j, c: (i, j))
    else:
        tr = _tile(hr, 512)
        n_r = hr // tr
        grid = (N_CHIPS, n_r)
        g_spec = pl.BlockSpec((1, tr, w.C), lambda s, i, c: (s, c[0] * n_r + i, 0))
        o_spec = pl.BlockSpec((1, tr, w.C), lambda s, i, c: (s, i, 0))

    def body(c_ref, g_ref, got_ref, out_ref):
        out_ref[...] = (g_ref[...].astype(F32) + got_ref[...].astype(F32)).astype(BF16)

    return _pcall(
        body, name="grad_pair_sum_" + w.name, out_shape=jax.ShapeDtypeStruct(got.shape, BF16),
        grid_spec=pltpu.PrefetchScalarGridSpec(num_scalar_prefetch=1, grid=grid, in_specs=[g_spec, o_spec],
                                               out_specs=o_spec),
        compiler_params=_params(("parallel", "parallel")),
    )(c_arr, _half_view(w, g), got)


def _chip_sum(w, p, q, cc_arr):
    hr, cols = w.half_rows, w.shard_shape[1]
    tr, tc = _tile(hr, 512), _tile(cols, 2048)
    n_r, n_c = hr // tr, cols // tc

    def body(cc_ref, own, q1, q2, q3, out_ref):
        own_v = own[...] if w.kind == "col" else own[0]
        out_ref[...] = ((own_v.astype(F32) + q1[0].astype(F32)) + q2[0].astype(F32)) + q3[0].astype(F32)

    if w.kind == "col":
        own_spec = pl.BlockSpec((tr, tc), lambda i, j, cc: (i, cc[1] * n_c + j))
    else:
        own_spec = pl.BlockSpec((1, tr, tc), lambda i, j, cc: (cc[1], i, j))
    q_specs = [pl.BlockSpec((1, tr, tc), lambda i, j, cc, s=s: ((cc[1] + s) % N_CHIPS, i, j)) for s in (1, 2, 3)]
    return _pcall(
        body, name="grad_chip_sum_" + w.name, out_shape=jax.ShapeDtypeStruct(w.shard_shape, F32),
        grid_spec=pltpu.PrefetchScalarGridSpec(
            num_scalar_prefetch=1, grid=(n_r, n_c), in_specs=[own_spec] + q_specs,
            out_specs=pl.BlockSpec((tr, tc), lambda i, j, cc: (cc[0] * n_r + i, j))),
        compiler_params=_params(("parallel", "parallel")),
    )(cc_arr, p, q, q, q)


_SEM = pl.BlockSpec(memory_space=pltpu.SEMAPHORE)
_HBM = pl.BlockSpec(memory_space=pltpu.HBM)


def _split_copies(kind, ws, p, land, send_sems, recv_sems):
    x, y, c, chips = _place()
    my_chip = 2 * x + y
    pairs = []
    for i, w in enumerate(ws):
        if kind == "pair":
            hr = w.half_rows
            src = p[i].at[pl.ds((1 - c) * hr, hr), :] if w.kind == "col" else p[i].at[:, pl.ds((1 - c) * hr, hr), :]
            cp = pltpu.make_async_remote_copy(src_ref=src, dst_ref=land[i], send_sem=send_sems.at[i],
                                              recv_sem=recv_sems.at[i], device_id=(x, y, 1 - c), device_id_type=MESH)
            pairs.append((cp, cp))
            continue
        for k, chip in enumerate(chips):
            to_chip = 2 * chip[0] + chip[1]
            src = p[i].at[:, pl.ds(to_chip * (w.C // N_CHIPS), w.C // N_CHIPS)] if w.kind == "col" else p[i].at[to_chip]
            kw = dict(send_sem=send_sems.at[3 * i + k], recv_sem=recv_sems.at[3 * i + k], device_id=(*chip, c),
                      device_id_type=MESH)
            pairs.append((pltpu.make_async_remote_copy(src_ref=src, dst_ref=land[i].at[my_chip], **kw),
                          pltpu.make_async_remote_copy(src_ref=src, dst_ref=land[i].at[to_chip], **kw)))
    return pairs


def _split_start(name, kind, ws, arrays):
    n_w = len(ws)
    if kind == "pair":
        arrays = [_half_view(w, g) for w, g in zip(ws, arrays)]
        lands = [lax.empty((w.half_rows, w.C) if w.kind == "col" else (N_CHIPS, w.half_rows, w.C), BF16) for w in ws]
    else:
        lands = [lax.empty((N_CHIPS, w.half_rows, w.shard_shape[1]), BF16) for w in ws]
    n_sem = n_w if kind == "pair" else 3 * n_w

    def body(*refs):
        p, land = refs[:n_w], refs[n_w:2 * n_w]
        for out, _ in _split_copies(kind, ws, p, land, refs[2 * n_w], refs[2 * n_w + 1]):
            out.start()
        refs[-1][...] = jnp.zeros_like(refs[-1])

    arrays = [pltpu.with_memory_space_constraint(a, pltpu.HBM) for a in list(arrays) + lands]
    res = _pcall(
        body, name=name,
        out_shape=(pltpu.SemaphoreType.DMA((n_sem,)), pltpu.SemaphoreType.DMA((n_sem,)),
                   *[pltpu.HBM(a.shape, a.dtype) for a in arrays], jax.ShapeDtypeStruct((SUBLANES, LANES), F32)),
        in_specs=[_HBM] * (2 * n_w),
        out_specs=(_SEM, _SEM, *[_HBM] * (2 * n_w), pl.BlockSpec(memory_space=pltpu.VMEM)),
        input_output_aliases={i: 2 + i for i in range(2 * n_w)},
        compiler_params=pltpu.CompilerParams(has_side_effects=pltpu.SideEffectType.DATAFLOW_SIDE_EFFECTING),
    )(*arrays)
    return (kind, ws, res[0], res[1], list(res[2:2 + n_w]), list(res[2 + n_w:2 + 2 * n_w])), res[-1]


def _split_wait(name, flight, after):
    kind, ws, send_sems, recv_sems, arrays, lands = flight
    n_w = len(ws)

    def body(*refs):
        p, land = refs[:n_w], refs[n_w:2 * n_w]
        for _, cp in _split_copies(kind, ws, p, land, refs[2 * n_w], refs[2 * n_w + 1]):
            cp.wait_send()
            cp.wait_recv()

    res = _pcall(
        body, name=name,
        out_shape=[pltpu.HBM(a.shape, a.dtype) for a in list(arrays) + list(lands)],
        in_specs=[_HBM] * (2 * n_w) + [_SEM, _SEM] + [ANY] * len(after), out_specs=[_HBM] * (2 * n_w),
        input_output_aliases={i: i for i in range(2 * n_w)},
        compiler_params=pltpu.CompilerParams(has_side_effects=pltpu.SideEffectType.DATAFLOW_SIDE_EFFECTING),
    )(*arrays, *lands, send_sems, recv_sems, *after)
    return list(res[:n_w]), list(res[n_w:])


def _sf_rider(ws, grads):
    n_w = len(ws)

    def copy(g, sems, i, half):
        send_sems, recv_sems = sems
        x, y, c, _ = _place()
        h = c if half == "mine" else 1 - c
        reg = ws[i].shard_half(g[i], h)
        return pltpu.make_async_remote_copy(src_ref=reg, dst_ref=reg, send_sem=send_sems.at[i], recv_sem=recv_sems.at[i],
                                            device_id=(x, y, 1 - c), device_id_type=MESH)

    def start(_, g, sems):
        for i in range(n_w):
            copy(g, sems, i, "mine").start()

    def finish(_, g, sems):
        for i in range(n_w):
            copy(g, sems, i, "other").wait_recv()
            copy(g, sems, i, "mine").wait_send()

    return _Rider(grads, [jax.ShapeDtypeStruct(w.shard_shape, F32) for w in ws],
                  [pltpu.SemaphoreType.DMA((n_w,)), pltpu.SemaphoreType.DMA((n_w,))], start, finish,
                  aliases={i: i for i in range(n_w)})


def _adamw_math(w, g, m, v):
    m = ADAM_B1 * m + (1.0 - ADAM_B1) * g
    v = ADAM_B2 * v + (1.0 - ADAM_B2) * (g * g)
    m_hat = m / (1.0 - ADAM_B1 ** ADAM_STEP)
    v_hat = v / (1.0 - ADAM_B2 ** ADAM_STEP)
    delta = -ADAM_LR * (m_hat / (jnp.sqrt(v_hat) + ADAM_EPS) + ADAM_WD * w)
    return delta, m, v


def _adamw(name, w, g, m, v, after=None):
    R, C = w.shape
    tr, tc = _tile(R, 256), _tile(C, 2048)
    behind = [] if after is None else [after]

    def body(w_ref, g_ref, m_ref, v_ref, *rest):
        g_out, d_out, m_out, v_out = rest[len(behind):]
        g = g_ref[...]
        g_out[...] = g
        d_out[...], m_out[...], v_out[...] = _adamw_math(w_ref[...], g, m_ref[...], v_ref[...])

    spec = pl.BlockSpec((tr, tc), lambda i, j: (i, j))
    sh = jax.ShapeDtypeStruct((R, C), F32)
    return _pcall(body, name=name, grid=(R // tr, C // tc), in_specs=[spec] * 4 + [ANY] * len(behind),
                  out_specs=[spec] * 4, out_shape=[sh] * 4, compiler_params=_params(("parallel", "parallel")))(
                      w, g, m, v, *behind)


def _ada_update(sct, dmod_sh, w, m, v, riders=()):
    R, C = w.shape
    tr, tc = _tile(R, 512), _tile(C, 1024)

    def body(s_ref, d_ref, w_ref, m_ref, v_ref, g_out, d_out, m_out, v_out):
        s, d = s_ref[...], d_ref[...]
        g = s[:, 0:1] * d[0:1, :]
        for b in range(1, N_DEV):
            g += s[:, b:b + 1] * d[b:b + 1, :]
        g_out[...] = g
        d_out[...], m_out[...], v_out[...] = _adamw_math(w_ref[...], g, m_ref[...], v_ref[...])

    spec = pl.BlockSpec((tr, tc), lambda i, j: (i, j))
    sh = jax.ShapeDtypeStruct((R, C), F32)
    return _ride(
        "ada_update", body, riders, [sct, dmod_sh, w, m, v], grid=(R // tr, C // tc),
        in_specs=[pl.BlockSpec((tr, N_DEV), lambda i, j: (i, 0)), pl.BlockSpec((N_DEV, tc), lambda i, j: (0, j)),
                  spec, spec, spec],
        out_specs=[spec] * 4, out_shape=[sh] * 4, scratch_shapes=[], sem=("parallel", "parallel"))


def _silu_rows(c_row):
    D = c_row.shape[1]

    def body(c_ref, o_ref):
        cv = c_ref[...]
        o_ref[...] = cv * jax.nn.sigmoid(cv)

    return _pcall(body, name="silu_c", out_shape=jax.ShapeDtypeStruct((1, D), F32))(c_row)


def _pack_partials(parts, widths, total):
    n = len(widths)

    def body(*refs):
        loss_p, out_ref = refs[n], refs[n + 1]
        off = 0
        for ref, wd in zip(refs[:n], widths):
            out_ref[:, off:off + wd] = jnp.sum(ref[...], axis=0)
            off += wd
        loss = jnp.sum(jnp.sum(loss_p[...], axis=0), axis=1, keepdims=True)
        out_ref[:, off:off + LANES] = jnp.broadcast_to(loss, (1, LANES))
        if off + LANES < total:
            out_ref[:, off + LANES:total] = jnp.zeros((1, total - off - LANES), F32)

    return _pcall(body, name="pack_partials", out_shape=jax.ShapeDtypeStruct((1, total), F32))(*parts)


def _small_update(gathered, offsets, params, loss_off):
    n_p = len(params)

    def over_devices(g_ref, off, wd):
        blk = g_ref[:, off:off + wd]
        g = blk[0:1, :]
        for b in range(1, N_DEV):
            g = g + blk[b:b + 1, :]
        return g

    def body(*refs):
        g_ref = refs[0]
        prm = refs[1:1 + 3 * n_p]
        outs = refs[1 + 3 * n_p:]
        outs[4 * n_p][...] = over_devices(g_ref, loss_off, LANES)
        for i, (off, wd) in enumerate(offsets):
            g = over_devices(g_ref, off, wd)
            w, m, v = prm[3 * i][...], prm[3 * i + 1][...], prm[3 * i + 2][...]
            outs[4 * i][...] = g
            outs[4 * i + 1][...], outs[4 * i + 2][...], outs[4 * i + 3][...] = _adamw_math(w, g, m, v)

    flat = [a for t in params for a in t]
    out_shape = [jax.ShapeDtypeStruct(t[0].shape, F32) for t in params for _ in range(4)]
    out_shape.append(jax.ShapeDtypeStruct((1, LANES), F32))
    return _pcall(body, name="small_update", out_shape=out_shape)(gathered, *flat)


def kernel(x, c, w_ada, b_ada, norm1_w, w_in, q_norm_w, k_norm_w, w_pool, pool_scale, w_a_up, w_b_up, w_o, norm2_w, w_ff1, w_ff2, loss_target, m_w_ada, m_b_ada, m_norm1_w, m_w_in, m_q_norm_w, m_k_norm_w, m_w_pool, m_pool_scale, m_w_a_up, m_w_b_up, m_w_o, m_norm2_w, m_w_ff1, m_w_ff2, v_w_ada, v_b_ada, v_norm1_w, v_w_in, v_q_norm_w, v_k_norm_w, v_w_pool, v_pool_scale, v_w_a_up, v_w_b_up, v_w_o, v_norm2_w, v_w_ff1, v_w_ff2):
    _, S, D = x.shape
    PW = D // 2
    H = PW // HEAD_DIM
    cg = PW // N_GROUPS
    IN = w_in.shape[2] * N_CHIPS
    FF = w_ff1.shape[2] * N_CHIPS
    A_COLS = w_ada.shape[2]
    xi, yi, ci = lax.axis_index("x"), lax.axis_index("y"), lax.axis_index("c")
    chip = 2 * xi + yi
    dev = 2 * chip + ci
    c_arr = jnp.reshape(ci, (1,)).astype(jnp.int32)
    x2, tgt = x[0], loss_target[0]

    ws = [_W("w_in", "col", D, IN), _W("w_pool", "row", PW, cg), _W("w_a_up", "col", PW, D),
          _W("w_b_up", "col", PW, D), _W("w_o", "row", D, D), _W("w_ff1", "col", D, FF), _W("w_ff2", "row", FF, D)]
    w32 = [w_in[0], w_pool[0].reshape(cg, cg), w_a_up[0], w_b_up[0], w_o[0], w_ff1[0], w_ff2[0]]
    m32 = [m_w_in[0], m_w_pool[0].reshape(cg, cg), m_w_a_up[0], m_w_b_up[0], m_w_o[0], m_w_ff1[0], m_w_ff2[0]]
    v32 = [v_w_in[0], v_w_pool[0].reshape(cg, cg), v_w_a_up[0], v_w_b_up[0], v_w_o[0], v_w_ff1[0], v_w_ff2[0]]

    W_IN, W_POOL, W_A, W_B, W_O, W_FF1, W_FF2 = ws
    chip_arr = jnp.reshape(chip, (1,)).astype(jnp.int32)
    cc_arr = jnp.stack([ci, chip]).astype(jnp.int32)
    s_in, s_pool, s_a, s_b, s_o = [_cast_into_full([w], [a], chip_arr)[0] for w, a in zip(ws[:5], w32[:5])]
    (s_ff1, s_ff2), ((win_f,),) = _cast_into_full([W_FF1, W_FF2], w32[5:], chip_arr, riders=[_ag_rider([W_IN], [s_in])])

    sc_row = _silu_rows(c)
    sc_all = _dev_allgather("gather_silu_c", sc_row.reshape(SUBLANES, D // SUBLANES)).reshape(N_DEV, D)
    sc16 = jnp.concatenate([sc_all, jnp.zeros_like(sc_all)], axis=0)
    b_cols = lax.dynamic_slice(b_ada, (0, chip * A_COLS), (1, A_COLS))
    (mod_cols,) = _mm("mod_cols", [(sc16, w_ada[0])], M=2 * N_DEV, N=A_COLS, K=D, tm=16, tn=1024, tk=1024,
                      a_pro=lambda a: a.astype(BF16), b_pro=lambda b: b.astype(BF16),
                      extras=[(b_cols, "row", 0)], outs=[_tile_out(F32)], epi=lambda accs, ex: [accs[0] + ex[0]])
    mod_all = _dev_allgather("gather_mod", mod_cols[:N_DEV]).reshape(N_CHIPS, 2, N_DEV, A_COLS)
    mod_row = lax.dynamic_index_in_dim(mod_all[:, 0], dev, axis=1, keepdims=False).reshape(1, N_CHIPS * A_COLS)
    shift1, scale1, gate1, shift2, scale2, gate2 = [mod_row[:, i * D:(i + 1) * D] for i in range(6)]

    WIDE = dict(tm=2048, tn=1024, tk=2048)
    DEEP = dict(tm=1024, tn=1024, tk=2048)
    DEEPER = dict(tm=1024, tn=1024, tk=4096)
    h = _norm_mod("norm1_mod", x2, norm1_w, scale1, shift1)
    (proj,), ((wpool_f, wa_f, wb_f, wo_f),) = _mm(
        "in_proj", [(h, win_f)], M=S, N=IN, K=D, outs=[_tile_out(BF16)], epi=lambda accs, ex: [accs[0]], **WIDE,
        riders=[_ag_rider([W_POOL, W_A, W_B, W_O], [s_pool, s_a, s_b, s_o], n_ch=4)])
    pooled, pa = _pool_fwd(proj, wpool_f, pool_scale, S, PW)
    (att, attf), ((wff1_f,),) = _attn_fwd(proj, q_norm_w, k_norm_w, S, H, PW // HEAD_DIM,
                                          riders=[_ag_rider([W_FF1], [s_ff1], n_ch=8)])

    def merge_epi(accs, ex):
        sa, sb = jax.nn.sigmoid(ex[0].astype(F32)), jax.nn.sigmoid(ex[1].astype(F32))
        return [sa * accs[0] + sb * accs[1], accs[0], accs[1]]

    (merged, ya, yb), (ff2_a,) = _mm("branch_up_merge", [(pa, wa_f), (att, wb_f)], M=S, N=D, K=PW,
                                     extras=[(proj, "tile", 4 * PW), (proj, "tile", 4 * PW + D)],
                                     outs=[_tile_out(BF16)] * 3, epi=merge_epi,
                                     riders=[_ag_rider([W_FF2], [s_ff2], n_ch=8, chunks=(0, 2))])
    (x1, o), (ff2_b,) = _mm("out_proj", [(merged, wo_f)], M=S, N=D, K=D, extras=[(x2, "tile", 0), (gate1, "row", 0)],
                            outs=[_tile_out(F32), _tile_out(BF16)], epi=lambda accs, ex: [ex[0] + ex[1] * accs[0], accs[0]],
                            riders=[_ag_rider([W_FF2], ff2_a, n_ch=8, chunks=(2, 4))], tm=2048, tn=512, tk=2048)
    h2 = _norm_mod("norm2_mod", x1, norm2_w, scale2, shift2)
    (rl,), ((wff2_f,),) = _mm("ff1", [(h2, wff1_f)], M=S, N=FF, K=D, outs=[_tile_out(BF16)], **WIDE,
                              epi=lambda accs, ex: [jnp.maximum(accs[0], 0.0)],
                              riders=[_ag_rider([W_FF2], ff2_b, n_ch=8, chunks=(4, 8))])

    def square(a):
        af = a.astype(F32)
        return (af * af).astype(BF16)

    def loss_epi(accs, ex):
        x1_t, tgt_t, g2 = ex
        f = accs[0]
        diff = (x1_t + g2 * f) - tgt_t
        dy = diff * (1.0 / D)
        return [dy, dy * g2, _colsum(dy * f), _colsum(diff * diff)]

    dy, df, dgate2_p, loss_p = _mm("ff2_loss", [(rl, wff2_f)], M=S, N=D, K=FF, a_pro=square, **DEEP,
                                   extras=[(x1, "tile", 0), (tgt, "tile", 0), (gate2, "row", 0)],
                                   outs=[_tile_out(F32), _tile_out(BF16), _COLSUM, _COLSUM], epi=loss_epi)

    tied = []

    def behind(token, a):
        a, token = lax.optimization_barrier((a, token))
        tied.append(token)
        return a

    def pair_sums(group, partials, got):
        return [_pair_sum(w, g, r, c_arr) for w, g, r in zip(group, partials, got)]

    def chip_sums(group, sums, from_chips):
        return [_chip_sum(w, p, q, cc_arr) for w, p, q in zip(group, sums, from_chips)]

    first = lambda accs, ex: [accs[0]]
    gmm = dict(ta=True, outs=[_tile_out(BF16)], epi=first, **WIDE)
    (g_ff2,) = _mm("grad_w_ff2", [(rl, df)], M=FF, N=D, K=S, a_pro=square, ta=True, tm=512, tn=2048, tk=2048,
                   outs=[_tile_out(BF16)], epi=first)
    flight, token = _split_start("pair_w_ff2_start", "pair", [W_FF2], [g_ff2])
    (dz1,) = _mm("d_ff_hidden", [(behind(token, df), wff2_f)], M=S, N=FF, K=D, tb=True, extras=[(rl, "tile", 0)],
                 outs=[_tile_out(BF16)], epi=lambda accs, ex: [accs[0] * (2.0 * ex[0].astype(F32))], **WIDE)
    sum_ff2 = pair_sums([W_FF2], *_split_wait("pair_w_ff2_wait", flight, after=[dz1] + tied))
    chip_ff2, token = _split_start("chip_w_ff2_start", "chip", [W_FF2], sum_ff2)
    (g_ff1,) = _mm("grad_w_ff1", [(behind(token, h2), dz1)], M=D, N=FF, K=S, **gmm)
    flight, token = _split_start("pair_w_ff1_start", "pair", [W_FF1], [g_ff1])
    (dh2,) = _mm("d_h2", [(behind(token, dz1), wff1_f)], M=S, N=D, K=FF, tb=True, outs=[_tile_out(F32)], epi=first,
                 **DEEPER)
    sum_ff1 = pair_sums([W_FF1], *_split_wait("pair_w_ff1_wait", flight, after=[dh2] + tied))
    chip_ff1, token = _split_start("chip_w_ff1_start", "chip", [W_FF1], sum_ff1)
    dx1, dshift2_p, dscale2_p, gn2_p, do, dgate1_p = _norm_mod_bwd("norm2_bwd", behind(token, dh2), x1, dy, norm2_w, scale2,
                                                                   gate_o=(o, gate1))
    (g_wo,) = _mm("grad_w_o", [(merged, do)], M=D, N=D, K=S, **gmm)

    def gate_epi(accs, ex):
        dm = accs[0]
        sa, sb = jax.nn.sigmoid(ex[0].astype(F32)), jax.nn.sigmoid(ex[1].astype(F32))
        ya_t, yb_t = ex[2].astype(F32), ex[3].astype(F32)
        return [dm * sa, dm * sb, dm * ya_t * (sa * (1.0 - sa)), dm * yb_t * (sb * (1.0 - sb))]

    dya, dyb, dga, dgb = _mm("d_merged", [(do, wo_f)], M=S, N=D, K=D, tb=True, tm=1024, tn=512, tk=2048,
                             extras=[(proj, "tile", 4 * PW), (proj, "tile", 4 * PW + D), (ya, "tile", 0), (yb, "tile", 0)],
                             outs=[_tile_out(BF16)] * 4, epi=gate_epi)
    both = lambda accs, ex: [accs[0], accs[1]]
    g_wa, g_wb = _mm("grad_w_up", [(pa, dya), (att, dyb)], M=PW, N=D, K=S, ta=True, outs=[_tile_out(BF16)] * 2, epi=both,
                     **WIDE)
    mid = [W_A, W_B, W_O]
    flight, token = _split_start("pair_mid_start", "pair", mid, [g_wa, g_wb, g_wo])
    dpa, datt = _mm("d_branches", [(dya, wa_f), (behind(token, dyb), wb_f)], M=S, N=PW, K=D, tb=True,
                    outs=[_tile_out(F32), _tile_out(BF16)], epi=both, tm=1024, tn=512, tk=2048)
    sum_mid = pair_sums(mid, *_split_wait("pair_mid_wait", flight, after=[datt] + tied))
    chip_mid, token = _split_start("chip_mid_start", "chip", mid, sum_mid)
    du, g_wpool4, gscale_p = _pool_bwd(dpa, pooled, wpool_f, pool_scale, S, PW)
    dq, dk, dv, gq_p, gk_p = _attn_bwd(proj, behind(token, datt), attf, q_norm_w, k_norm_w, S, H, PW // HEAD_DIM)
    dproj = jnp.concatenate([du, dq, dk, dv, dga, dgb], axis=1)
    early = [W_FF1, W_FF2]
    sum_ff1, q_ff1 = _split_wait("chip_w_ff1_wait", chip_ff1, after=[dq] + tied)
    sum_ff2, q_ff2 = _split_wait("chip_w_ff2_wait", chip_ff2, after=[dq] + tied)
    halves_early = chip_sums(early, sum_ff1 + sum_ff2, q_ff1 + q_ff2)
    (g_win,), (grads_early,) = _mm("grad_w_in", [(h, dproj)], M=D, N=IN, K=S, riders=[_sf_rider(early, halves_early)],
                                   **gmm)
    last = [W_IN, W_POOL]
    g_last = [g_win, g_wpool4.reshape(PW, cg)]
    sum_mid, q_mid = _split_wait("chip_mid_wait", chip_mid, after=[g_win] + tied)
    halves_mid = chip_sums(mid, sum_mid, q_mid)
    (dh,), (got_last, grads_mid) = _mm("d_h", [(dproj, win_f)], M=S, N=D, K=IN, tb=True, outs=[_tile_out(F32)], epi=first,
                                       riders=[_px_rider(last, g_last), _sf_rider(mid, halves_mid)], **DEEPER)
    sum_last = pair_sums(last, g_last, got_last)
    grad_x, dshift1_p, dscale1_p, gn1_p = _norm_mod_bwd("norm1_bwd", dh, x2, dx1, norm1_w, scale1)

    parts = [dshift1_p, dscale1_p, dgate1_p, dshift2_p, dscale2_p, dgate2_p, gn1_p, gn2_p,
             gscale_p.reshape(1, 1, PW), gq_p, gk_p]
    widths = [D] * 8 + [PW, HEAD_DIM, HEAD_DIM]
    used = sum(widths)
    P = -(-(used + LANES) // (SUBLANES * LANES)) * (SUBLANES * LANES)
    packed = _pack_partials(parts + [loss_p], widths, P)
    gathered = _dev_allgather("gather_vector_grads", packed.reshape(SUBLANES, P // SUBLANES)).reshape(N_DEV, P)
    sum_last, gathered = lax.optimization_barrier((sum_last, gathered))
    chip_last, token = _split_start("chip_last_start", "chip", last, sum_last)
    small = [(b_ada, m_b_ada, v_b_ada), (norm1_w, m_norm1_w, v_norm1_w), (norm2_w, m_norm2_w, v_norm2_w),
             (pool_scale, m_pool_scale, v_pool_scale), (q_norm_w, m_q_norm_w, v_q_norm_w),
             (k_norm_w, m_k_norm_w, v_k_norm_w)]
    offsets = [(0, 6 * D), (6 * D, D), (7 * D, D), (8 * D, PW), (8 * D + PW, HEAD_DIM), (8 * D + PW + HEAD_DIM, HEAD_DIM)]
    su = _small_update(gathered, offsets, small, used)
    (g_b, d_b, nm_b, nv_b, g_n1, d_n1, nm_n1, nv_n1, g_n2, d_n2, nm_n2, nv_n2, g_ps, d_ps, nm_ps, nv_ps,
     g_qn, d_qn, nm_qn, nv_qn, g_kn, d_kn, nm_kn, nv_kn, loss_sum) = su
    dmod_sh = lax.dynamic_slice(gathered, (0, chip * A_COLS), (N_DEV, A_COLS))
    dmod_sh, token = lax.optimization_barrier((dmod_sh, token))
    g_ada, d_ada, nm_ada, nv_ada = _ada_update(sc_all.T, dmod_sh, w_ada[0], m_w_ada[0], v_w_ada[0])

    upd_done = [_adamw("adamw_" + w.name, a, g, m, v, after=token)
                for w, a, g, m, v in zip(ws[2:], w32[2:], list(grads_mid) + list(grads_early), m32[2:], v32[2:])]

    sum_last, q_last = _split_wait("chip_last_wait", chip_last, after=[nv_ada] + [u[3] for u in upd_done])
    halves_last = chip_sums(last, sum_last, q_last)
    filled = _run_rider("grad_sibling_fill", _sf_rider(last, halves_last))
    upd = [_adamw("adamw_" + w.name, a, g, m, v) for w, a, g, m, v in zip(ws[:2], w32[:2], filled, m32[:2], v32[:2])]
    upd += upd_done

    loss = (0.5 / D) * loss_sum[0, 0]

    def up(a):
        return a[None]

    def pool4(a):
        return a.reshape(1, N_GROUPS, cg // N_CHIPS, cg)

    (gr_win, d_win, nm_win, nv_win), (gr_wp, d_wp, nm_wp, nv_wp), (gr_wa, d_wa, nm_wa, nv_wa), \
        (gr_wb, d_wb, nm_wb, nv_wb), (gr_wo, d_wo, nm_wo, nv_wo), (gr_f1, d_f1, nm_f1, nv_f1), \
        (gr_f2, d_f2, nm_f2, nv_f2) = upd
    return (
        loss, grad_x[None],
        up(g_ada), g_b, g_n1, up(gr_win), g_qn, g_kn, pool4(gr_wp), g_ps, up(gr_wa), up(gr_wb), up(gr_wo), g_n2,
        up(gr_f1), up(gr_f2),
        up(d_ada), d_b, d_n1, up(d_win), d_qn, d_kn, pool4(d_wp), d_ps, up(d_wa), up(d_wb), up(d_wo), d_n2,
        up(d_f1), up(d_f2),
        up(nm_ada), nm_b, nm_n1, up(nm_win), nm_qn, nm_kn, pool4(nm_wp), nm_ps, up(nm_wa), up(nm_wb), up(nm_wo), nm_n2,
        up(nm_f1), up(nm_f2),
        up(nv_ada), nv_b, nv_n1, up(nv_win), nv_qn, nv_kn, pool4(nv_wp), nv_ps, up(nv_wa), up(nv_wb), up(nv_wo), nv_n2,
        up(nv_f1), up(nv_f2),
    )
```
